```python
import jax, jax.numpy as jnp
from jax import lax
import numpy as np

D_MODEL = 1024
BATCH = 8
SEQ = 4096
DEPTH = 1

CHUNK = 64
RMS_EPS = 1e-6
ROPE_BASE = 10000.0
RET_HEADS = 4
RET_V = D_MODEL // 2
RET_QK = RET_V // 2
GLA_HEADS = 4
GLA_V = D_MODEL - RET_V
GLA_QK = GLA_V // 2
GLA_GATE_RANK = 16
GLA_GATE_NORM = 16.0
MIX_WIDTH = RET_V + GLA_V
IN_SIZES = (RET_QK, RET_QK, RET_V, RET_V, GLA_QK, GLA_QK, GLA_V, GLA_V, GLA_GATE_RANK)
IN_WIDTH = sum(IN_SIZES)
D_FF = ((8 * D_MODEL // 3 + 127) // 128) * 128

kernel_name = "macaron_retention_gla_hybrid"


def rms_norm(x, g):
    xf = x.astype(jnp.float32)
    y = xf * lax.rsqrt(jnp.mean(xf * xf, axis=-1, keepdims=True) + RMS_EPS)
    return (y * g.astype(jnp.float32)).astype(x.dtype)


def swiglu(h, w_gate, w_up, w_down):
    return (jax.nn.silu(h @ w_gate) * (h @ w_up)) @ w_down


def split_heads(t, n_heads):
    b, t_len, w = t.shape
    return t.reshape(b, t_len, n_heads, w // n_heads).transpose(0, 2, 1, 3)


def merge_heads(t):
    b, h, t_len, d = t.shape
    return t.transpose(0, 2, 1, 3).reshape(b, t_len, h * d)


def rotary(t, pos):
    dk = t.shape[-1]
    half = dk // 2
    inv = ROPE_BASE ** (-jnp.arange(half, dtype=jnp.float32) * 2.0 / dk)
    ang = pos[:, None] * inv[None, :]
    cos, sin = jnp.cos(ang), jnp.sin(ang)
    t1, t2 = t[..., :half], t[..., half:]
    return jnp.concatenate([t1 * cos - t2 * sin, t1 * sin + t2 * cos], axis=-1)


def head_rms(o):
    return o * lax.rsqrt(jnp.mean(o * o, axis=-1, keepdims=True) + RMS_EPS)


def chunk_decay_attention(q, k, v, log_a):
    bsz, n_h, t_len, dk = q.shape
    n_chunks = t_len // CHUNK
    per_key = log_a.shape[-1] != 1
    qf, kf, vf = (z.astype(jnp.float32) for z in (q, k, v))
    la = log_a.astype(jnp.float32)

    def to_chunks(z):
        return z.reshape(z.shape[0], z.shape[1], n_chunks, CHUNK, z.shape[-1]).transpose(2, 0, 1, 3, 4)

    b_cum = jnp.cumsum(to_chunks(la), axis=3)
    xs = (to_chunks(qf), to_chunks(kf), to_chunks(vf), b_cum)

    def step(state, inp):
        q_c, k_c, v_c, b_c = inp
        decay = jnp.exp(-jnp.abs(b_c[..., :, None, :] - b_c[..., None, :, :]))
        if per_key:
            scores = jnp.einsum('bhik,bhjk,bhijk->bhij', q_c, k_c, decay)
        else:
            scores = jnp.einsum('bhik,bhjk->bhij', q_c, k_c) * decay[..., 0]
        out = (jnp.einsum('bhij,bhjv->bhiv', scores, v_c)
               + jnp.einsum('bhik,bhkv->bhiv', q_c * jnp.exp(b_c), state))
        b_last = b_c[..., -1:, :]
        state = (jnp.exp(b_last[..., 0, :])[..., None] * state
                 + jnp.einsum('bhjk,bhjv->bhkv', k_c * jnp.exp(b_last - b_c), v_c))
        return state, out

    s0 = jnp.zeros((bsz, n_h, dk, v.shape[-1]), jnp.float32)
    _, outs = lax.scan(step, s0, xs)
    return outs.transpose(1, 2, 0, 3, 4).reshape(bsz, n_h, t_len, v.shape[-1])


def hybrid_mixer(h, w_in, ret_norm_g, gla_w_a2, gla_b_a, gla_norm_g, w_out):
    t_len = h.shape[1]
    pos = jnp.arange(t_len, dtype=jnp.float32)
    proj = h @ w_in
    offsets = [int(o) for o in np.cumsum(IN_SIZES)[:-1]]
    (r_q, r_k, r_v, r_g, g_q, g_k, g_v, g_g, g_low) = jnp.split(proj, offsets, axis=-1)

    rdk = RET_QK // RET_HEADS
    rq = rotary(split_heads(r_q, RET_HEADS), pos)
    rk = rotary(split_heads(r_k, RET_HEADS), pos) * (rdk ** -0.5)
    rv = split_heads(r_v, RET_HEADS)
    log_gamma = jnp.log(1.0 - 2.0 ** (-5.0 - jnp.arange(RET_HEADS, dtype=jnp.float32)))
    ret_log_a = jnp.broadcast_to(log_gamma[None, :, None, None], (1, RET_HEADS, t_len, 1))
    o_ret = head_rms(chunk_decay_attention(rq, rk, rv, ret_log_a))
    o_ret = (merge_heads(o_ret) * ret_norm_g.astype(jnp.float32)).astype(h.dtype) * jax.nn.silu(r_g)

    gdk = GLA_QK // GLA_HEADS
    gq = split_heads(g_q, GLA_HEADS) * (gdk ** -0.5)
    gk = split_heads(g_k, GLA_HEADS)
    gv = split_heads(g_v, GLA_HEADS)
    gate_logit = (g_low @ gla_w_a2 + gla_b_a).astype(jnp.float32)
    gla_log_a = split_heads(jax.nn.log_sigmoid(gate_logit) / GLA_GATE_NORM, GLA_HEADS)
    o_gla = head_rms(chunk_decay_attention(gq, gk, gv, gla_log_a))
    o_gla = (merge_heads(o_gla) * gla_norm_g.astype(jnp.float32)).astype(h.dtype) * jax.nn.silu(g_g)

    return jnp.concatenate([o_ret, o_gla], axis=-1) @ w_out


def _fwd_setup_inputs(seed: int = 0) -> dict:
    key = jax.random.key(seed)
    ks = jax.random.split(key, 20)
    f32 = jnp.float32

    def nrm(k, shape, fan_in):
        return jax.random.normal(k, shape, f32) * (fan_in ** -0.5)

    def gain(k, shape):
        return 1.0 + 0.02 * jax.random.normal(k, shape, f32)

    return {
        "x": jax.random.normal(ks[0], (BATCH, SEQ, D_MODEL), f32),
        "ffn1_norm_g": gain(ks[1], (DEPTH, D_MODEL)),
        "ffn1_w_gate": nrm(ks[2], (DEPTH, D_MODEL, D_FF), D_MODEL),
        "ffn1_w_up": nrm(ks[3], (DEPTH, D_MODEL, D_FF), D_MODEL),
        "ffn1_w_down": nrm(ks[4], (DEPTH, D_FF, D_MODEL), D_FF),
        "mix_norm_g": gain(ks[5], (DEPTH, D_MODEL)),
        "w_in": nrm(ks[6], (DEPTH, D_MODEL, IN_WIDTH), D_MODEL),
        "ret_norm_g": gain(ks[7], (DEPTH, RET_V)),
        "gla_w_a2": nrm(ks[8], (DEPTH, GLA_GATE_RANK, GLA_QK), GLA_GATE_RANK),
        "gla_b_a": 0.1 * jax.random.normal(ks[9], (DEPTH, GLA_QK), f32),
        "gla_norm_g": gain(ks[10], (DEPTH, GLA_V)),
        "w_out": nrm(ks[11], (DEPTH, MIX_WIDTH, D_MODEL), MIX_WIDTH),
        "ffn2_norm_g": gain(ks[12], (DEPTH, D_MODEL)),
        "ffn2_w_gate": nrm(ks[13], (DEPTH, D_MODEL, D_FF), D_MODEL),
        "ffn2_w_up": nrm(ks[14], (DEPTH, D_MODEL, D_FF), D_MODEL),
        "ffn2_w_down": nrm(ks[15], (DEPTH, D_FF, D_MODEL), D_FF),
        "final_norm_g": gain(ks[16], (D_MODEL,)),
    }


def _fwd_reference(x, ffn1_norm_g, ffn1_w_gate, ffn1_w_up, ffn1_w_down, mix_norm_g, w_in,
              ret_norm_g, gla_w_a2, gla_b_a, gla_norm_g, w_out, ffn2_norm_g,
              ffn2_w_gate, ffn2_w_up, ffn2_w_down, final_norm_g):
    for l in range(DEPTH):
        x = x + 0.5 * swiglu(rms_norm(x, ffn1_norm_g[l]), ffn1_w_gate[l], ffn1_w_up[l], ffn1_w_down[l])
        x = x + hybrid_mixer(rms_norm(x, mix_norm_g[l]), w_in[l], ret_norm_g[l], gla_w_a2[l],
                             gla_b_a[l], gla_norm_g[l], w_out[l])
        x = x + 0.5 * swiglu(rms_norm(x, ffn2_norm_g[l]), ffn2_w_gate[l], ffn2_w_up[l], ffn2_w_down[l])
    return rms_norm(x, final_norm_g)


import jax as _jax
import jax.numpy as _jnp

TWIN_FORMAT = 'train_step'
FWD_PARAMS = ['x', 'ffn1_norm_g', 'ffn1_w_gate', 'ffn1_w_up', 'ffn1_w_down', 'mix_norm_g', 'w_in', 'ret_norm_g', 'gla_w_a2', 'gla_b_a', 'gla_norm_g', 'w_out', 'ffn2_norm_g', 'ffn2_w_gate', 'ffn2_w_up', 'ffn2_w_down', 'final_norm_g']
TWIN_WEIGHTS = ['ffn1_norm_g', 'ffn1_w_gate', 'ffn1_w_up', 'ffn1_w_down', 'mix_norm_g', 'w_in', 'ret_norm_g', 'gla_w_a2', 'gla_b_a', 'gla_norm_g', 'w_out', 'ffn2_norm_g', 'ffn2_w_gate', 'ffn2_w_up', 'ffn2_w_down', 'final_norm_g']
TWIN_DIFF_INPUT = 'x'
TWIN_INPUTS = ['x', 'ffn1_norm_g', 'ffn1_w_gate', 'ffn1_w_up', 'ffn1_w_down', 'mix_norm_g', 'w_in', 'ret_norm_g', 'gla_w_a2', 'gla_b_a', 'gla_norm_g', 'w_out', 'ffn2_norm_g', 'ffn2_w_gate', 'ffn2_w_up', 'ffn2_w_down', 'final_norm_g', 'loss_target', 'm_ffn1_norm_g', 'm_ffn1_w_gate', 'm_ffn1_w_up', 'm_ffn1_w_down', 'm_mix_norm_g', 'm_w_in', 'm_ret_norm_g', 'm_gla_w_a2', 'm_gla_b_a', 'm_gla_norm_g', 'm_w_out', 'm_ffn2_norm_g', 'm_ffn2_w_gate', 'm_ffn2_w_up', 'm_ffn2_w_down', 'm_final_norm_g', 'v_ffn1_norm_g', 'v_ffn1_w_gate', 'v_ffn1_w_up', 'v_ffn1_w_down', 'v_mix_norm_g', 'v_w_in', 'v_ret_norm_g', 'v_gla_w_a2', 'v_gla_b_a', 'v_gla_norm_g', 'v_w_out', 'v_ffn2_norm_g', 'v_ffn2_w_gate', 'v_ffn2_w_up', 'v_ffn2_w_down', 'v_final_norm_g']
TWIN_OUTPUTS = ['loss', 'grad_x', 'grad_ffn1_norm_g', 'grad_ffn1_w_gate', 'grad_ffn1_w_up', 'grad_ffn1_w_down', 'grad_mix_norm_g', 'grad_w_in', 'grad_ret_norm_g', 'grad_gla_w_a2', 'grad_gla_b_a', 'grad_gla_norm_g', 'grad_w_out', 'grad_ffn2_norm_g', 'grad_ffn2_w_gate', 'grad_ffn2_w_up', 'grad_ffn2_w_down', 'grad_final_norm_g', 'delta_ffn1_norm_g', 'delta_ffn1_w_gate', 'delta_ffn1_w_up', 'delta_ffn1_w_down', 'delta_mix_norm_g', 'delta_w_in', 'delta_ret_norm_g', 'delta_gla_w_a2', 'delta_gla_b_a', 'delta_gla_norm_g', 'delta_w_out', 'delta_ffn2_norm_g', 'delta_ffn2_w_gate', 'delta_ffn2_w_up', 'delta_ffn2_w_down', 'delta_final_norm_g', 'new_m_ffn1_norm_g', 'new_m_ffn1_w_gate', 'new_m_ffn1_w_up', 'new_m_ffn1_w_down', 'new_m_mix_norm_g', 'new_m_w_in', 'new_m_ret_norm_g', 'new_m_gla_w_a2', 'new_m_gla_b_a', 'new_m_gla_norm_g', 'new_m_w_out', 'new_m_ffn2_norm_g', 'new_m_ffn2_w_gate', 'new_m_ffn2_w_up', 'new_m_ffn2_w_down', 'new_m_final_norm_g', 'new_v_ffn1_norm_g', 'new_v_ffn1_w_gate', 'new_v_ffn1_w_up', 'new_v_ffn1_w_down', 'new_v_mix_norm_g', 'new_v_w_in', 'new_v_ret_norm_g', 'new_v_gla_w_a2', 'new_v_gla_b_a', 'new_v_gla_norm_g', 'new_v_w_out', 'new_v_ffn2_norm_g', 'new_v_ffn2_w_gate', 'new_v_ffn2_w_up', 'new_v_ffn2_w_down', 'new_v_final_norm_g']
TWIN_LEAF_KINDS = {'loss': 'loss', 'grad_x': 'grad_x', 'grad_ffn1_norm_g': 'grad_w', 'grad_ffn1_w_gate': 'grad_w', 'grad_ffn1_w_up': 'grad_w', 'grad_ffn1_w_down': 'grad_w', 'grad_mix_norm_g': 'grad_w', 'grad_w_in': 'grad_w', 'grad_ret_norm_g': 'grad_w', 'grad_gla_w_a2': 'grad_w', 'grad_gla_b_a': 'grad_w', 'grad_gla_norm_g': 'grad_w', 'grad_w_out': 'grad_w', 'grad_ffn2_norm_g': 'grad_w', 'grad_ffn2_w_gate': 'grad_w', 'grad_ffn2_w_up': 'grad_w', 'grad_ffn2_w_down': 'grad_w', 'grad_final_norm_g': 'grad_w', 'delta_ffn1_norm_g': 'delta_w', 'delta_ffn1_w_gate': 'delta_w', 'delta_ffn1_w_up': 'delta_w', 'delta_ffn1_w_down': 'delta_w', 'delta_mix_norm_g': 'delta_w', 'delta_w_in': 'delta_w', 'delta_ret_norm_g': 'delta_w', 'delta_gla_w_a2': 'delta_w', 'delta_gla_b_a': 'delta_w', 'delta_gla_norm_g': 'delta_w', 'delta_w_out': 'delta_w', 'delta_ffn2_norm_g': 'delta_w', 'delta_ffn2_w_gate': 'delta_w', 'delta_ffn2_w_up': 'delta_w', 'delta_ffn2_w_down': 'delta_w', 'delta_final_norm_g': 'delta_w', 'new_m_ffn1_norm_g': 'new_m', 'new_m_ffn1_w_gate': 'new_m', 'new_m_ffn1_w_up': 'new_m', 'new_m_ffn1_w_down': 'new_m', 'new_m_mix_norm_g': 'new_m', 'new_m_w_in': 'new_m', 'new_m_ret_norm_g': 'new_m', 'new_m_gla_w_a2': 'new_m', 'new_m_gla_b_a': 'new_m', 'new_m_gla_norm_g': 'new_m', 'new_m_w_out': 'new_m', 'new_m_ffn2_norm_g': 'new_m', 'new_m_ffn2_w_gate': 'new_m', 'new_m_ffn2_w_up': 'new_m', 'new_m_ffn2_w_down': 'new_m', 'new_m_final_norm_g': 'new_m', 'new_v_ffn1_norm_g': 'new_v', 'new_v_ffn1_w_gate': 'new_v', 'new_v_ffn1_w_up': 'new_v', 'new_v_ffn1_w_down': 'new_v', 'new_v_mix_norm_g': 'new_v', 'new_v_w_in': 'new_v', 'new_v_ret_norm_g': 'new_v', 'new_v_gla_w_a2': 'new_v', 'new_v_gla_b_a': 'new_v', 'new_v_gla_norm_g': 'new_v', 'new_v_w_out': 'new_v', 'new_v_ffn2_norm_g': 'new_v', 'new_v_ffn2_w_gate': 'new_v', 'new_v_ffn2_w_up': 'new_v', 'new_v_ffn2_w_down': 'new_v', 'new_v_final_norm_g': 'new_v'}


def _forward(args):
    return _fwd_reference(*[args[k] for k in FWD_PARAMS])


def _output_shape():
    def fwd():
        inp = _fwd_setup_inputs(0)
        return _fwd_reference(*[inp[k] for k in FWD_PARAMS])
    out = _jax.eval_shape(fwd)
    return out.shape, out.dtype

N_MICROBATCH = 1
ADAM_LR = 0.001
ADAM_B1 = 0.9
ADAM_B2 = 0.999
ADAM_EPS = 1e-08
ADAM_WD = 0.01
ADAM_STEP = 10
PER_EXAMPLE_BATCH_AXIS = {'x': 0, 'loss_target': 0}
SHARED_INPUTS = []
_WEIGHT_DTYPES = {'ffn1_norm_g': _jnp.float32, 'ffn1_w_gate': _jnp.float32, 'ffn1_w_up': _jnp.float32, 'ffn1_w_down': _jnp.float32, 'mix_norm_g': _jnp.float32, 'w_in': _jnp.float32, 'ret_norm_g': _jnp.float32, 'gla_w_a2': _jnp.float32, 'gla_b_a': _jnp.float32, 'gla_norm_g': _jnp.float32, 'w_out': _jnp.float32, 'ffn2_norm_g': _jnp.float32, 'ffn2_w_gate': _jnp.float32, 'ffn2_w_up': _jnp.float32, 'ffn2_w_down': _jnp.float32, 'final_norm_g': _jnp.float32}
MOMENT_SCALE = {'ffn1_norm_g': 1.097227e-01, 'ffn1_w_gate': 4.299770e-02, 'ffn1_w_up': 4.155135e-02, 'ffn1_w_down': 6.901641e-02, 'mix_norm_g': 1.895874e-01, 'w_in': 1.062720e-01, 'ret_norm_g': 9.617952e-02, 'gla_w_a2': 1.407094e-02, 'gla_b_a': 5.249266e-02, 'gla_norm_g': 9.767522e-02, 'w_out': 9.047820e-02, 'ffn2_norm_g': 6.273964e-02, 'ffn2_w_gate': 2.667275e-02, 'ffn2_w_up': 2.586614e-02, 'ffn2_w_down': 4.282935e-02, 'final_norm_g': 3.201487e+01}


def _to_microbatches(a, axis):
    t = _jnp.moveaxis(a, axis, 0)
    t = t.reshape((N_MICROBATCH, t.shape[0] // N_MICROBATCH) + t.shape[1:])
    return _jnp.moveaxis(t, 1, axis + 1)


def setup_inputs(seed: int = 0) -> dict:
    inp = _fwd_setup_inputs(seed)
    key = _jax.random.fold_in(_jax.random.key(seed), 7919)
    shape, _ = _output_shape()
    out = dict(inp)
    out["loss_target"] = _jax.random.normal(_jax.random.fold_in(key, 0), shape, _jnp.float32)
    for i, name in enumerate(TWIN_WEIGHTS):
        w = inp[name].astype(_jnp.float32)
        if MOMENT_SCALE is None:
            s = _jnp.sqrt(_jnp.mean(_jnp.square(w)) + 1e-30)
        else:
            s = MOMENT_SCALE[name]
        km, kv = _jax.random.split(_jax.random.fold_in(key, i + 1))
        out[name] = w
        out["m_" + name] = s * _jax.random.normal(km, w.shape, _jnp.float32)
        out["v_" + name] = (s * s) * _jax.random.uniform(kv, w.shape, _jnp.float32, 0.5, 1.5)
    if N_MICROBATCH > 1:
        for name, axis in PER_EXAMPLE_BATCH_AXIS.items():
            out[name] = _to_microbatches(out[name], axis)
    return {'x': out['x'], 'ffn1_norm_g': out['ffn1_norm_g'], 'ffn1_w_gate': out['ffn1_w_gate'], 'ffn1_w_up': out['ffn1_w_up'], 'ffn1_w_down': out['ffn1_w_down'], 'mix_norm_g': out['mix_norm_g'], 'w_in': out['w_in'], 'ret_norm_g': out['ret_norm_g'], 'gla_w_a2': out['gla_w_a2'], 'gla_b_a': out['gla_b_a'], 'gla_norm_g': out['gla_norm_g'], 'w_out': out['w_out'], 'ffn2_norm_g': out['ffn2_norm_g'], 'ffn2_w_gate': out['ffn2_w_gate'], 'ffn2_w_up': out['ffn2_w_up'], 'ffn2_w_down': out['ffn2_w_down'], 'final_norm_g': out['final_norm_g'], 'loss_target': out['loss_target'], 'm_ffn1_norm_g': out['m_ffn1_norm_g'], 'm_ffn1_w_gate': out['m_ffn1_w_gate'], 'm_ffn1_w_up': out['m_ffn1_w_up'], 'm_ffn1_w_down': out['m_ffn1_w_down'], 'm_mix_norm_g': out['m_mix_norm_g'], 'm_w_in': out['m_w_in'], 'm_ret_norm_g': out['m_ret_norm_g'], 'm_gla_w_a2': out['m_gla_w_a2'], 'm_gla_b_a': out['m_gla_b_a'], 'm_gla_norm_g': out['m_gla_norm_g'], 'm_w_out': out['m_w_out'], 'm_ffn2_norm_g': out['m_ffn2_norm_g'], 'm_ffn2_w_gate': out['m_ffn2_w_gate'], 'm_ffn2_w_up': out['m_ffn2_w_up'], 'm_ffn2_w_down': out['m_ffn2_w_down'], 'm_final_norm_g': out['m_final_norm_g'], 'v_ffn1_norm_g': out['v_ffn1_norm_g'], 'v_ffn1_w_gate': out['v_ffn1_w_gate'], 'v_ffn1_w_up': out['v_ffn1_w_up'], 'v_ffn1_w_down': out['v_ffn1_w_down'], 'v_mix_norm_g': out['v_mix_norm_g'], 'v_w_in': out['v_w_in'], 'v_ret_norm_g': out['v_ret_norm_g'], 'v_gla_w_a2': out['v_gla_w_a2'], 'v_gla_b_a': out['v_gla_b_a'], 'v_gla_norm_g': out['v_gla_norm_g'], 'v_w_out': out['v_w_out'], 'v_ffn2_norm_g': out['v_ffn2_norm_g'], 'v_ffn2_w_gate': out['v_ffn2_w_gate'], 'v_ffn2_w_up': out['v_ffn2_w_up'], 'v_ffn2_w_down': out['v_ffn2_w_down'], 'v_final_norm_g': out['v_final_norm_g']}


def _loss(weights, diff, rest, loss_target):
    with _jax.named_scope("forward"):
        args = {**rest, TWIN_DIFF_INPUT: diff, **{k: w.astype(_WEIGHT_DTYPES[k]) for k, w in weights.items()}}
        y = _forward(args)
    with _jax.named_scope("loss_head"):
        err = _jnp.square(y.astype(_jnp.float32) - loss_target)
        return 0.5 * _jnp.sum(_jnp.mean(err, axis=-1)) if err.ndim else 0.5 * err


def _adamw(w, g, m, v):
    m = ADAM_B1 * m + (1.0 - ADAM_B1) * g
    v = ADAM_B2 * v + (1.0 - ADAM_B2) * _jnp.square(g)
    m_hat = m / (1.0 - ADAM_B1 ** ADAM_STEP)
    v_hat = v / (1.0 - ADAM_B2 ** ADAM_STEP)
    delta = -ADAM_LR * (m_hat / (_jnp.sqrt(v_hat) + ADAM_EPS) + ADAM_WD * w)
    return delta, m, v


def reference(x, ffn1_norm_g, ffn1_w_gate, ffn1_w_up, ffn1_w_down, mix_norm_g, w_in, ret_norm_g, gla_w_a2, gla_b_a, gla_norm_g, w_out, ffn2_norm_g, ffn2_w_gate, ffn2_w_up, ffn2_w_down, final_norm_g, loss_target, m_ffn1_norm_g, m_ffn1_w_gate, m_ffn1_w_up, m_ffn1_w_down, m_mix_norm_g, m_w_in, m_ret_norm_g, m_gla_w_a2, m_gla_b_a, m_gla_norm_g, m_w_out, m_ffn2_norm_g, m_ffn2_w_gate, m_ffn2_w_up, m_ffn2_w_down, m_final_norm_g, v_ffn1_norm_g, v_ffn1_w_gate, v_ffn1_w_up, v_ffn1_w_down, v_mix_norm_g, v_w_in, v_ret_norm_g, v_gla_w_a2, v_gla_b_a, v_gla_norm_g, v_w_out, v_ffn2_norm_g, v_ffn2_w_gate, v_ffn2_w_up, v_ffn2_w_down, v_final_norm_g):
    given = dict(x=x, ffn1_norm_g=ffn1_norm_g, ffn1_w_gate=ffn1_w_gate, ffn1_w_up=ffn1_w_up, ffn1_w_down=ffn1_w_down, mix_norm_g=mix_norm_g, w_in=w_in, ret_norm_g=ret_norm_g, gla_w_a2=gla_w_a2, gla_b_a=gla_b_a, gla_norm_g=gla_norm_g, w_out=w_out, ffn2_norm_g=ffn2_norm_g, ffn2_w_gate=ffn2_w_gate, ffn2_w_up=ffn2_w_up, ffn2_w_down=ffn2_w_down, final_norm_g=final_norm_g, loss_target=loss_target, m_ffn1_norm_g=m_ffn1_norm_g, m_ffn1_w_gate=m_ffn1_w_gate, m_ffn1_w_up=m_ffn1_w_up, m_ffn1_w_down=m_ffn1_w_down, m_mix_norm_g=m_mix_norm_g, m_w_in=m_w_in, m_ret_norm_g=m_ret_norm_g, m_gla_w_a2=m_gla_w_a2, m_gla_b_a=m_gla_b_a, m_gla_norm_g=m_gla_norm_g, m_w_out=m_w_out, m_ffn2_norm_g=m_ffn2_norm_g, m_ffn2_w_gate=m_ffn2_w_gate, m_ffn2_w_up=m_ffn2_w_up, m_ffn2_w_down=m_ffn2_w_down, m_final_norm_g=m_final_norm_g, v_ffn1_norm_g=v_ffn1_norm_g, v_ffn1_w_gate=v_ffn1_w_gate, v_ffn1_w_up=v_ffn1_w_up, v_ffn1_w_down=v_ffn1_w_down, v_mix_norm_g=v_mix_norm_g, v_w_in=v_w_in, v_ret_norm_g=v_ret_norm_g, v_gla_w_a2=v_gla_w_a2, v_gla_b_a=v_gla_b_a, v_gla_norm_g=v_gla_norm_g, v_w_out=v_w_out, v_ffn2_norm_g=v_ffn2_norm_g, v_ffn2_w_gate=v_ffn2_w_gate, v_ffn2_w_up=v_ffn2_w_up, v_ffn2_w_down=v_ffn2_w_down, v_final_norm_g=v_final_norm_g)
    weights = {n: given[n] for n in TWIN_WEIGHTS}
    shared = {n: given[n] for n in SHARED_INPUTS}
    per_example = {n: given[n] for n in ['x']}
    grad_fn = _jax.value_and_grad(_loss, argnums=(0, 1))

    def one_microbatch(ex, loss_target):
        ex = dict(ex)
        diff = ex.pop(TWIN_DIFF_INPUT)
        return grad_fn(weights, diff, {**shared, **ex}, loss_target)

    if N_MICROBATCH == 1:
        loss, (grad_w, grad_x) = one_microbatch(per_example, given["loss_target"])
    else:
        def body(carry, xs):
            loss_sum, grad_sum = carry
            l_k, (gw_k, gx_k) = one_microbatch(xs[0], xs[1])
            with _jax.named_scope("update"):
                return (loss_sum + l_k, _jax.tree.map(_jnp.add, grad_sum, gw_k)), gx_k

        init = (_jnp.zeros((), _jnp.float32), _jax.tree.map(_jnp.zeros_like, weights))
        (loss, grad_w), grad_x = _jax.lax.scan(body, init, (per_example, given["loss_target"]))
    with _jax.named_scope("update"):
        delta_w, new_m, new_v = {}, {}, {}
        for n in TWIN_WEIGHTS:
            delta_w[n], new_m[n], new_v[n] = _adamw(weights[n], grad_w[n], given["m_" + n], given["v_" + n])
    return (loss, grad_x, *[grad_w[n] for n in TWIN_WEIGHTS], *[delta_w[n] for n in TWIN_WEIGHTS],
            *[new_m[n] for n in TWIN_WEIGHTS], *[new_v[n] for n in TWIN_WEIGHTS])
```

```python
import functools

import jax
import jax.numpy as jnp
from jax import lax
from jax.experimental import pallas as pl
from jax.experimental.pallas import tpu as pltpu

F32 = jnp.float32
BF16 = jnp.bfloat16
MESH = pl.DeviceIdType.MESH
ANY = pl.BlockSpec(memory_space=pltpu.HBM)

N_DEV = 8
RMS_EPS = 1e-6
ROPE_BASE = 10000.0
HEADS = 4
DK = 64
DV = 128
QK_W = HEADS * DK
V_W = HEADS * DV
GATE_RANK = 16
GATE_NORM = 16.0
CHUNK = 64
SUPER = 256
PROJ_W = 3200
C_RQ, C_RK, C_RV, C_RG, C_GQ, C_GK, C_GV, C_GG, C_GL = 0, 256, 512, 1024, 1536, 1792, 2048, 2560, 3072
GL_W = PROJ_W - C_GL
ADAM_LR, ADAM_B1, ADAM_B2, ADAM_EPS, ADAM_WD, ADAM_STEP = 0.001, 0.9, 0.999, 1e-08, 0.01, 10
VMEM_LIMIT_V7X = 52 * 1024 * 1024


def _cparams(**kw):
    return pltpu.CompilerParams(vmem_limit_bytes=VMEM_LIMIT_V7X, **kw)


def _dot(a, b, form, precision=None):
    dims = {"nn": (((1,), (0,)), ((), ())), "nt": (((1,), (1,)), ((), ())), "tn": (((0,), (0,)), ((), ()))}[form]
    return lax.dot_general(a, b, dims, preferred_element_type=F32, precision=precision)


def _sigmoid(x):
    return 1.0 / (1.0 + jnp.exp(-x))


def _rms_fwd(name, x, g):
    T, D = x.shape
    tm = 512

    def body(x_ref, g_ref, o_ref):
        xv = x_ref[...]
        r = lax.rsqrt(jnp.mean(xv * xv, axis=-1, keepdims=True) + RMS_EPS)
        o_ref[...] = (xv * r * g_ref[...]).astype(o_ref.dtype)

    return pl.pallas_call(
        body, name=name, grid=(T // tm,),
        in_specs=[pl.BlockSpec((tm, D), lambda i: (i, 0)), pl.BlockSpec((1, D), lambda i: (0, 0))],
        out_specs=pl.BlockSpec((tm, D), lambda i: (i, 0)),
        out_shape=jax.ShapeDtypeStruct((T, D), BF16),
        compiler_params=_cparams(),
    )(x, g)


def _rms_bwd(name, x, g, dh, dres, out_scale):
    T, D = x.shape
    tm = 512

    def body(x_ref, g_ref, dh_ref, dres_ref, dx_ref, dxb_ref, dg_ref):
        i = pl.program_id(0)
        xv = x_ref[...]
        r = lax.rsqrt(jnp.mean(xv * xv, axis=-1, keepdims=True) + RMS_EPS)
        xhat = xv * r
        dhv = dh_ref[...]

        @pl.when(i == 0)
        def _():
            dg_ref[...] = jnp.zeros_like(dg_ref)

        dg_ref[...] += jnp.sum(dhv * xhat, axis=0, keepdims=True)
        dxhat = dhv * g_ref[...]
        dx = r * (dxhat - xhat * jnp.mean(dxhat * xhat, axis=-1, keepdims=True)) + dres_ref[...]
        dx_ref[...] = dx
        dxb_ref[...] = (out_scale * dx).astype(dxb_ref.dtype)

    tile = pl.BlockSpec((tm, D), lambda i: (i, 0))
    vec = pl.BlockSpec((1, D), lambda i: (0, 0))
    return pl.pallas_call(
        body, name=name, grid=(T // tm,),
        in_specs=[tile, vec, tile, tile],
        out_specs=[tile, tile, vec],
        out_shape=[jax.ShapeDtypeStruct((T, D), F32), jax.ShapeDtypeStruct((T, D), BF16),
                   jax.ShapeDtypeStruct((1, D), F32)],
        compiler_params=_cparams(),
    )(x, g, dh, dres)


def _final_loss_bwd(name, x, g, target, out_scale):
    T, D = x.shape
    tm = 512

    def body(x_ref, g_ref, t_ref, dx_ref, dxb_ref, dg_ref, loss_ref):
        i = pl.program_id(0)
        xv = x_ref[...]
        r = lax.rsqrt(jnp.mean(xv * xv, axis=-1, keepdims=True) + RMS_EPS)
        xhat = xv * r
        err = xhat * g_ref[...] - t_ref[...]

        @pl.when(i == 0)
        def _():
            dg_ref[...] = jnp.zeros_like(dg_ref)
            loss_ref[...] = jnp.zeros_like(loss_ref)

        loss_ref[...] += jnp.broadcast_to(jnp.sum(err * err) * (0.5 / D), loss_ref.shape)
        dy = err * (1.0 / D)
        dg_ref[...] += jnp.sum(dy * xhat, axis=0, keepdims=True)
        dxhat = dy * g_ref[...]
        dx = r * (dxhat - xhat * jnp.mean(dxhat * xhat, axis=-1, keepdims=True))
        dx_ref[...] = dx
        dxb_ref[...] = (out_scale * dx).astype(dxb_ref.dtype)

    tile = pl.BlockSpec((tm, D), lambda i: (i, 0))
    vec = pl.BlockSpec((1, D), lambda i: (0, 0))
    return pl.pallas_call(
        body, name=name, grid=(T // tm,),
        in_specs=[tile, vec, tile],
        out_specs=[tile, tile, vec, pl.BlockSpec((1, 128), lambda i: (0, 0))],
        out_shape=[jax.ShapeDtypeStruct((T, D), F32), jax.ShapeDtypeStruct((T, D), BF16),
                   jax.ShapeDtypeStruct((1, D), F32), jax.ShapeDtypeStruct((1, 128), F32)],
        compiler_params=_cparams(),
    )(x, g, target)


def _mm_nstream(name, a, ws, w_sel, w_form, comps, out_dtypes, epilogue, cn, rows=1024):
    T, K = a.shape
    N = ws[0].shape[1]
    rows = min(rows, T)
    assert N % cn == 0 and T % rows == 0
    n_w, n_c, n_o = len(ws), len(comps), len(out_dtypes)

    def body(*refs):
        a_ref = refs[0]
        w_refs = refs[1:1 + n_w]
        c_refs = refs[1 + n_w:1 + n_w + n_c]
        o_refs = refs[1 + n_w + n_c:]

        def step(r, carry):
            sl = pl.ds(pl.multiple_of(r * rows, rows), rows)
            a_blk = a_ref[sl, :]
            dots = [_dot(a_blk, w_ref[...], w_form) for w_ref in w_refs]
            outs = epilogue(dots, [c_ref[sl, :] for c_ref in c_refs])
            for o_ref, o in zip(o_refs, outs):
                o_ref[sl, :] = o.astype(o_ref.dtype)
            return carry

        lax.fori_loop(0, T // rows, step, 0)

    if w_form == "nt":
        w_specs = [pl.BlockSpec((None, cn, K), functools.partial(lambda j, s: (s, j, 0), s=s)) for s in w_sel]
    else:
        w_specs = [pl.BlockSpec((K, cn), lambda j: (0, j)) for _ in ws]
    chunk = pl.BlockSpec((T, cn), lambda j: (0, j))
    return pl.pallas_call(
        body, name=name, grid=(N // cn,),
        in_specs=[pl.BlockSpec((T, K), lambda j: (0, 0))] + w_specs + [chunk] * n_c,
        out_specs=[chunk] * n_o,
        out_shape=[jax.ShapeDtypeStruct((T, N), dt) for dt in out_dtypes],
        compiler_params=_cparams(),
    )(a, *ws, *comps)


def _mm_mstream(name, as_, ws, w_sel, w_form, res, scale, out_dtype, tm=512):
    T = as_[0].shape[0]
    n_a = len(as_)
    w_shapes = [w.shape[-2:] for w in ws]
    N = w_shapes[0][1] if w_form == "nn" else w_shapes[0][0]
    has_res = res is not None

    def body(*refs):
        a_refs = refs[:n_a]
        w_refs = refs[n_a:2 * n_a]
        res_ref = refs[2 * n_a] if has_res else None
        o_ref = refs[-1]
        acc = None
        for a_ref, w_ref in zip(a_refs, w_refs):
            d = _dot(a_ref[...], w_ref[...], w_form)
            acc = d if acc is None else acc + d
        if scale != 1.0:
            acc = acc * scale
        if has_res:
            acc = acc + res_ref[...]
        o_ref[...] = acc.astype(o_ref.dtype)

    a_specs = [pl.BlockSpec((tm, a.shape[1]), lambda i: (i, 0)) for a in as_]
    w_specs = []
    for w, s in zip(ws, w_sel):
        if w.ndim == 3:
            w_specs.append(pl.BlockSpec((None,) + tuple(w.shape[1:]), functools.partial(lambda i, s: (s, 0, 0), s=s)))
        else:
            w_specs.append(pl.BlockSpec(tuple(w.shape), lambda i: (0, 0)))
    r_specs = [pl.BlockSpec((tm, N), lambda i: (i, 0))] if has_res else []
    args = list(as_) + list(ws) + ([res] if has_res else [])
    return pl.pallas_call(
        body, name=name, grid=(T // tm,),
        in_specs=a_specs + w_specs + r_specs,
        out_specs=pl.BlockSpec((tm, N), lambda i: (i, 0)),
        out_shape=jax.ShapeDtypeStruct((T, N), out_dtype),
        compiler_params=_cparams(),
    )(*args)


def _mm_tn(name, a, b, tmo, tno, tk, out_dtype):
    T, Ma = a.shape
    Nb = b.shape[1]
    nk = T // tk

    def body(a_ref, b_ref, o_ref, acc_ref):
        k = pl.program_id(2)

        @pl.when(k == 0)
        def _():
            acc_ref[...] = jnp.zeros_like(acc_ref)

        acc_ref[...] += _dot(a_ref[...], b_ref[...], "tn")

        @pl.when(k == nk - 1)
        def _():
            o_ref[...] = acc_ref[...].astype(o_ref.dtype)

    return pl.pallas_call(
        body, name=name, grid=(Ma // tmo, Nb // tno, nk),
        in_specs=[pl.BlockSpec((tk, tmo), lambda i, j, k: (k, i)), pl.BlockSpec((tk, tno), lambda i, j, k: (k, j))],
        out_specs=pl.BlockSpec((tmo, tno), lambda i, j, k: (i, j)),
        out_shape=jax.ShapeDtypeStruct((Ma, Nb), out_dtype),
        scratch_shapes=[pltpu.VMEM((tmo, tno), F32)],
        compiler_params=_cparams(),
    )(a, b)


def _silu_mul_epilogue(dots, comps):
    g, u = dots
    act = g * _sigmoid(g) * u
    return [g, u, act]


def _dact_epilogue(dots, comps):
    (dact,) = dots
    g = comps[0].astype(F32)
    u = comps[1].astype(F32)
    s = _sigmoid(g)
    silu = g * s
    dsilu = s * (1.0 + g * (1.0 - s))
    return [dact * u * dsilu, dact * silu]


def _identity_epilogue(dots, comps):
    return list(dots)


def _swap_halves(x):
    lane = lax.broadcasted_iota(jnp.int32, x.shape, 1)
    first = (lane % DK) < (DK // 2)
    return jnp.where(first, pltpu.roll(x, 128 - DK // 2, 1), pltpu.roll(x, DK // 2, 1))


def _rotary(t, cos, sin_signed):
    halves = []
    for p in range(QK_W // 128):
        th = t[:, 128 * p:128 * (p + 1)]
        halves.append(th * cos + _swap_halves(th) * sin_signed)
    return jnp.concatenate(halves, axis=1)


def _rotary_transposed(d, cos, sin_signed):
    halves = []
    for p in range(QK_W // 128):
        dh = d[:, 128 * p:128 * (p + 1)]
        halves.append(dh * cos + _swap_halves(dh * sin_signed))
    return jnp.concatenate(halves, axis=1)


def _log_sigmoid(x):
    return jnp.minimum(x, 0.0) - jnp.log(1.0 + jnp.exp(-jnp.abs(x)))


def _attn_masks():
    row = lax.broadcasted_iota(jnp.int32, (SUPER, SUPER), 0)
    col = lax.broadcasted_iota(jnp.int32, (SUPER, SUPER), 1)
    same = (row // CHUNK) == (col // CHUNK)
    return row, col, same


def _group_inputs(grp, pr, cos, sin_signed, lg, wa2, ba):
    if grp == 0:
        q = _rotary(pr[:, C_RQ:C_RQ + QK_W], cos, sin_signed)
        k = _rotary(pr[:, C_RK:C_RK + QK_W], cos, sin_signed) * (DK ** -0.5)
        v = pr[:, C_RV:C_RV + V_W]
        gate = pr[:, C_RG:C_RG + V_W]
        pos = lax.broadcasted_iota(jnp.int32, (SUPER, QK_W), 0).astype(F32) + 1.0
        return q, k, v, gate, pos * lg, None, None
    q = pr[:, C_GQ:C_GQ + QK_W] * (DK ** -0.5)
    k = pr[:, C_GK:C_GK + QK_W]
    v = pr[:, C_GV:C_GV + V_W]
    gate = pr[:, C_GG:C_GG + V_W]
    glow = pr[:, C_GL:C_GL + GL_W]
    logit = _dot(glow.astype(BF16), wa2.astype(BF16), "nn") + ba
    la = _log_sigmoid(logit) * (1.0 / GATE_NORM)
    row, col, _ = _attn_masks()
    lower = (col <= row).astype(F32)
    b_cum = _dot(lower, la, "nn", precision=lax.Precision.HIGHEST)
    return q, k, v, gate, b_cum, glow, logit


def _decay_factors(q, k, b_cum):
    c = b_cum[SUPER // 2 - 1:SUPER // 2, :]
    bl = b_cum[SUPER - 1:SUPER, :]
    e1 = jnp.exp(b_cum - c)
    e2 = jnp.exp(c - b_cum)
    e_b = jnp.exp(b_cum)
    e_l = jnp.exp(bl - b_cum)
    return dict(e1=e1, e2=e2, eb=e_b, el=e_l, ebl=jnp.exp(bl),
                qp=q * e1, qm=q * e2, kp=k * e1, km=k * e2, qs=q * e_b, kl=k * e_l)


def _state_block_mask():
    r = lax.broadcasted_iota(jnp.int32, (V_W, QK_W), 0)
    c = lax.broadcasted_iota(jnp.int32, (V_W, QK_W), 1)
    return (r // DV) == (c // DK)


def _attn_fwd(proj, cos, sin_signed, lg, wa2p, ba, gn_ret, gn_gla):
    T = proj.shape[0]
    n_s = T // SUPER

    def body(pr_ref, cos_ref, sin_ref, lg_ref, wa2_ref, ba_ref, gr_ref, gg_ref, o_ref, y_ref, st_ref, s_ref):
        i = pl.program_id(0)

        @pl.when(i == 0)
        def _():
            s_ref[...] = jnp.zeros_like(s_ref)

        pr = pr_ref
        row, col, same = _attn_masks()
        m1 = col <= row
        m2 = jnp.logical_and(col > row, same)
        lane = lax.broadcasted_iota(jnp.int32, (1, QK_W), 1)
        blockmask = _state_block_mask()
        for grp in range(2):
            q, k, v, gate, b_cum, _, _ = _group_inputs(grp, pr, cos_ref[...], sin_ref[...], lg_ref[...],
                                                      wa2_ref[...], ba_ref[...])
            f = _decay_factors(q, k, b_cum)
            gn = gr_ref[...] if grp == 0 else gg_ref[...]
            s_prev = s_ref[grp]
            st_ref[0, grp] = s_prev
            o_inter = _dot(f["qs"].astype(BF16), s_prev.astype(BF16), "nt")
            kmb = f["km"].astype(BF16)
            kpb = f["kp"].astype(BF16)
            vb = v.astype(BF16)
            for h in range(HEADS):
                hm = (lane // DK) == h
                a1 = _dot(jnp.where(hm, f["qp"], 0.0).astype(BF16), kmb, "nt")
                a2 = _dot(jnp.where(hm, f["qm"], 0.0).astype(BF16), kpb, "nt")
                a = jnp.where(m1, a1, jnp.where(m2, a2, 0.0))
                lo = grp * V_W + h * DV
                o_h = _dot(a.astype(BF16), vb[:, h * DV:(h + 1) * DV], "nn") + o_inter[:, h * DV:(h + 1) * DV]
                o_ref[:, lo:lo + DV] = o_h
                r = lax.rsqrt(jnp.mean(o_h * o_h, axis=-1, keepdims=True) + RMS_EPS)
                gte = gate[:, h * DV:(h + 1) * DV]
                y = o_h * r * gn[:, h * DV:(h + 1) * DV] * (gte * _sigmoid(gte))
                y_ref[:, lo:lo + DV] = y.astype(y_ref.dtype)
            upd = _dot(vb, f["kl"].astype(BF16), "tn")
            s_ref[grp] = s_prev * f["ebl"] + jnp.where(blockmask, upd, 0.0)

    const = lambda shape: pl.BlockSpec(shape, lambda i: tuple(0 for _ in shape))
    return pl.pallas_call(
        body, name="attn_fwd", grid=(n_s,),
        in_specs=[pl.BlockSpec((SUPER, PROJ_W), lambda i: (i, 0)),
                  pl.BlockSpec((SUPER, 128), lambda i: (i, 0)), pl.BlockSpec((SUPER, 128), lambda i: (i, 0)),
                  const((1, QK_W)), const((GL_W, QK_W)), const((1, QK_W)), const((1, V_W)), const((1, V_W))],
        out_specs=[pl.BlockSpec((SUPER, 2 * V_W), lambda i: (i, 0)), pl.BlockSpec((SUPER, 2 * V_W), lambda i: (i, 0)),
                   pl.BlockSpec((1, 2, V_W, QK_W), lambda i: (i, 0, 0, 0))],
        out_shape=[jax.ShapeDtypeStruct((T, 2 * V_W), F32), jax.ShapeDtypeStruct((T, 2 * V_W), BF16),
                   jax.ShapeDtypeStruct((n_s, 2, V_W, QK_W), F32)],
        scratch_shapes=[pltpu.VMEM((2, V_W, QK_W), F32)],
        compiler_params=_cparams(),
    )(proj, cos, sin_signed, lg, wa2p, ba, gn_ret, gn_gla)


def _attn_bwd(proj, cos, sin_signed, lg, wa2p, ba, gn_ret, gn_gla, o, dy, states):
    T = proj.shape[0]
    n_s = T // SUPER

    def body(pr_ref, cos_ref, sin_ref, lg_ref, wa2_ref, ba_ref, gr_ref, gg_ref, o_ref, dy_ref, st_ref,
             dp_ref, dgr_ref, dgg_ref, dba_ref, dwa_ref, ds_ref):
        i = pl.program_id(0)

        @pl.when(i == 0)
        def _():
            ds_ref[...] = jnp.zeros_like(ds_ref)
            dgr_ref[...] = jnp.zeros_like(dgr_ref)
            dgg_ref[...] = jnp.zeros_like(dgg_ref)
            dba_ref[...] = jnp.zeros_like(dba_ref)
            dwa_ref[...] = jnp.zeros_like(dwa_ref)

        pr = pr_ref
        cos = cos_ref[...]
        sin_signed = sin_ref[...]
        row, col, same = _attn_masks()
        m1 = col <= row
        m2 = jnp.logical_and(col > row, same)
        m1t = row <= col
        m2t = jnp.logical_and(row > col, same)
        lane = lax.broadcasted_iota(jnp.int32, (1, QK_W), 1)
        blockmask = _state_block_mask()
        for grp in range(2):
            q, k, v, gate, b_cum, glow, logit = _group_inputs(grp, pr, cos, sin_signed, lg_ref[...],
                                                              wa2_ref[...], ba_ref[...])
            f = _decay_factors(q, k, b_cum)
            gn = gr_ref[...] if grp == 0 else gg_ref[...]
            dgn_ref = dgr_ref if grp == 0 else dgg_ref
            do_parts, dgate_parts, dgn_parts = [], [], []
            for h in range(HEADS):
                lo = grp * V_W + h * DV
                o_h = o_ref[:, lo:lo + DV]
                r = lax.rsqrt(jnp.mean(o_h * o_h, axis=-1, keepdims=True) + RMS_EPS)
                n = o_h * r
                gte = gate[:, h * DV:(h + 1) * DV]
                sg = _sigmoid(gte)
                dy_h = dy_ref[:, lo:lo + DV]
                gn_h = gn[:, h * DV:(h + 1) * DV]
                dgate_parts.append(dy_h * n * gn_h * (sg * (1.0 + gte * (1.0 - sg))))
                dz = dy_h * (gte * sg)
                dgn_parts.append(jnp.sum(dz * n, axis=0, keepdims=True))
                dn = dz * gn_h
                do_parts.append(r * (dn - n * jnp.mean(dn * n, axis=-1, keepdims=True)))
            dgn_ref[...] += jnp.concatenate(dgn_parts, axis=1)
            dgate = jnp.concatenate(dgate_parts, axis=1)
            do = jnp.concatenate(do_parts, axis=1)
            dob = do.astype(BF16)
            vb = v.astype(BF16)
            s_prev = st_ref[0, grp]
            ds_new = ds_ref[grp]
            dsb = ds_new.astype(BF16)
            qpb, qmb = f["qp"].astype(BF16), f["qm"].astype(BF16)
            kpb, kmb = f["kp"].astype(BF16), f["km"].astype(BF16)
            dqp = jnp.zeros((SUPER, QK_W), F32)
            dqm = jnp.zeros((SUPER, QK_W), F32)
            dkp = jnp.zeros((SUPER, QK_W), F32)
            dkm = jnp.zeros((SUPER, QK_W), F32)
            dv_parts = []
            for h in range(HEADS):
                hm = (lane // DK) == h
                qp_h = jnp.where(hm, f["qp"], 0.0).astype(BF16)
                qm_h = jnp.where(hm, f["qm"], 0.0).astype(BF16)
                kp_h = jnp.where(hm, f["kp"], 0.0).astype(BF16)
                km_h = jnp.where(hm, f["km"], 0.0).astype(BF16)
                at = jnp.where(m1t, _dot(km_h, qpb, "nt"), jnp.where(m2t, _dot(kp_h, qmb, "nt"), 0.0))
                do_h = dob[:, h * DV:(h + 1) * DV]
                v_h = vb[:, h * DV:(h + 1) * DV]
                dv_parts.append(_dot(at.astype(BF16), do_h, "nn"))
                da = _dot(do_h, v_h, "nt")
                dat = _dot(v_h, do_h, "nt")
                da1 = jnp.where(m1, da, 0.0).astype(BF16)
                da2 = jnp.where(m2, da, 0.0).astype(BF16)
                da1t = jnp.where(m1t, dat, 0.0).astype(BF16)
                da2t = jnp.where(m2t, dat, 0.0).astype(BF16)
                dqp = dqp + _dot(da1, km_h, "nn")
                dqm = dqm + _dot(da2, kp_h, "nn")
                dkm = dkm + _dot(da1t, qp_h, "nn")
                dkp = dkp + _dot(da2t, qm_h, "nn")
            klb = f["kl"].astype(BF16)
            qsb = f["qs"].astype(BF16)
            dqs = _dot(dob, s_prev.astype(BF16), "nn")
            dkl = _dot(vb, dsb, "nn")
            dv = jnp.concatenate(dv_parts, axis=1) + _dot(klb, dsb, "nt")
            ds_ref[grp] = ds_new * f["ebl"] + jnp.where(blockmask, _dot(dob, qsb, "tn"), 0.0)
            dq = dqp * f["e1"] + dqm * f["e2"] + dqs * f["eb"]
            dk = dkm * f["e2"] + dkp * f["e1"] + dkl * f["el"]
            if grp == 0:
                dq = _rotary_transposed(dq, cos, sin_signed)
                dk = _rotary_transposed(dk * (DK ** -0.5), cos, sin_signed)
                dp_ref[:, C_RQ:C_RQ + QK_W] = dq.astype(dp_ref.dtype)
                dp_ref[:, C_RK:C_RK + QK_W] = dk.astype(dp_ref.dtype)
                dp_ref[:, C_RV:C_RV + V_W] = dv.astype(dp_ref.dtype)
                dp_ref[:, C_RG:C_RG + V_W] = dgate.astype(dp_ref.dtype)
            else:
                dkl_kl = dkl * klb.astype(F32)
                db = (dqp * qpb.astype(F32) - dkm * kmb.astype(F32) - dqm * qmb.astype(F32)
                      + dkp * kpb.astype(F32) + dqs * qsb.astype(F32) - dkl_kl)
                last = (jnp.sum(dkl_kl, axis=0, keepdims=True)
                        + f["ebl"] * jnp.sum(s_prev * ds_new, axis=0, keepdims=True))
                rowq = lax.broadcasted_iota(jnp.int32, (SUPER, QK_W), 0)
                db = db + jnp.where(rowq == SUPER - 1, last, 0.0)
                upper = (col >= row).astype(F32)
                dla = _dot(upper, db, "nn", precision=lax.Precision.HIGHEST)
                dlogit = dla * (1.0 / GATE_NORM) * (1.0 - _sigmoid(logit))
                dlb = dlogit.astype(BF16)
                dglow = _dot(dlb, wa2_ref[...].astype(BF16), "nt")
                dwa_ref[...] += _dot(glow.astype(BF16), dlb, "tn")
                dba_ref[...] += jnp.sum(dlogit, axis=0, keepdims=True)
                dp_ref[:, C_GQ:C_GQ + QK_W] = (dq * (DK ** -0.5)).astype(dp_ref.dtype)
                dp_ref[:, C_GK:C_GK + QK_W] = dk.astype(dp_ref.dtype)
                dp_ref[:, C_GV:C_GV + V_W] = dv.astype(dp_ref.dtype)
                dp_ref[:, C_GG:C_GG + V_W] = dgate.astype(dp_ref.dtype)
                dp_ref[:, C_GL:C_GL + GL_W] = dglow.astype(dp_ref.dtype)

    rev = lambda i: n_s - 1 - i
    const = lambda shape: pl.BlockSpec(shape, lambda i: tuple(0 for _ in shape))
    return pl.pallas_call(
        body, name="attn_bwd", grid=(n_s,),
        in_specs=[pl.BlockSpec((SUPER, PROJ_W), lambda i: (rev(i), 0)),
                  pl.BlockSpec((SUPER, 128), lambda i: (rev(i), 0)), pl.BlockSpec((SUPER, 128), lambda i: (rev(i), 0)),
                  const((1, QK_W)), const((GL_W, QK_W)), const((1, QK_W)), const((1, V_W)), const((1, V_W)),
                  pl.BlockSpec((SUPER, 2 * V_W), lambda i: (rev(i), 0)),
                  pl.BlockSpec((SUPER, 2 * V_W), lambda i: (rev(i), 0)),
                  pl.BlockSpec((1, 2, V_W, QK_W), lambda i: (rev(i), 0, 0, 0))],
        out_specs=[pl.BlockSpec((SUPER, PROJ_W), lambda i: (rev(i), 0)),
                   const((1, V_W)), const((1, V_W)), const((1, QK_W)), const((GL_W, QK_W))],
        out_shape=[jax.ShapeDtypeStruct((T, PROJ_W), BF16),
                   jax.ShapeDtypeStruct((1, V_W), F32), jax.ShapeDtypeStruct((1, V_W), F32),
                   jax.ShapeDtypeStruct((1, QK_W), F32), jax.ShapeDtypeStruct((GL_W, QK_W), F32)],
        scratch_shapes=[pltpu.VMEM((2, V_W, QK_W), F32)],
        compiler_params=_cparams(),
    )(proj, cos, sin_signed, lg, wa2p, ba, gn_ret, gn_gla, o, dy, states)


def _rotary_tables(T):
    half = DK // 2
    inv = ROPE_BASE ** (-jnp.arange(half, dtype=F32) * 2.0 / DK)
    ang = jnp.arange(T, dtype=F32)[:, None] * inv[None, :]
    cos, sin = jnp.cos(ang), jnp.sin(ang)
    cos_head = jnp.concatenate([cos, cos], axis=1)
    sin_head = jnp.concatenate([-sin, sin], axis=1)
    return jnp.tile(cos_head, (1, 128 // DK)), jnp.tile(sin_head, (1, 128 // DK))


def _ffn_fwd(tag, x, g, fw, sel):
    h = _rms_fwd(tag + "_rms", x, g)
    gate, up, act = _mm_nstream(tag + "_up", h, [fw, fw], [sel, sel + 1], "nt", [], [BF16, BF16, BF16],
                                _silu_mul_epilogue, cn=256)
    x_out = _mm_mstream(tag + "_down", [act], [fw], [sel + 2], "nn", x, 0.5, F32)
    return x_out, (h, gate, up, act)


def _ffn_bwd(tag, x, g, fw, sel, saved, dx_out, dyb):
    h, gate, up, act = saved
    T = x.shape[0]
    dgate, dup = _mm_nstream(tag + "_dact", dyb, [fw], [sel + 2], "nt", [gate, up], [BF16, BF16],
                             _dact_epilogue, cn=256)
    f_half = fw.shape[1] // 2
    d_wd = _mm_tn(tag + "_dwd", act, dyb, f_half, dyb.shape[1], min(T, 1024), BF16)
    d_wg = _mm_tn(tag + "_dwg", dgate, h, f_half, h.shape[1], min(T, 1024), BF16)
    d_wu = _mm_tn(tag + "_dwu", dup, h, f_half, h.shape[1], min(T, 1024), BF16)
    dh = _mm_mstream(tag + "_dh", [dgate, dup], [fw, fw], [sel, sel + 1], "nn", None, 1.0, F32)
    return dh, (d_wg, d_wu, d_wd)


def _local_step(x, target, fw, w_out, w_in, wa2p, norms):
    T, D = x.shape
    cos, sin_signed = _rotary_tables(T)
    lg = jnp.repeat(jnp.log(1.0 - 2.0 ** (-5.0 - jnp.arange(HEADS, dtype=F32))), DK)[None, :]

    x1, saved1 = _ffn_fwd("ffn1", x, norms["ffn1"], fw, 0)
    h2 = _rms_fwd("mix_rms", x1, norms["mix"])
    (proj,) = _mm_nstream("mix_proj", h2, [w_in], [0], "nn", [], [F32], _identity_epilogue, cn=640)
    o, ymix, states = _attn_fwd(proj, cos, sin_signed, lg, wa2p, norms["b_a"], norms["ret"], norms["gla"])
    x2 = _mm_mstream("mix_out", [ymix], [w_out], [0], "nn", x1, 1.0, F32)
    x3, saved2 = _ffn_fwd("ffn2", x2, norms["ffn2"], fw, 3)

    dx3, dy3b, d_final, loss = _final_loss_bwd("final", x3, norms["final"], target, 0.5)

    dh3, d_ffn2 = _ffn_bwd("ffn2b", x2, norms["ffn2"], fw, 3, saved2, dx3, dy3b)
    dx2, dx2b, d_g2 = _rms_bwd("ffn2b_rms", x2, norms["ffn2"], dh3, dx3, 1.0)

    dymix = _mm_mstream("mixb_dy", [dx2b], [w_out], [0], "nt", None, 1.0, F32)
    d_wout = _mm_tn("mixb_dwout", ymix, dx2b, D, D, min(T, 1024), BF16)
    dproj, d_ret, d_gla, d_ba, d_wa2p = _attn_bwd(proj, cos, sin_signed, lg, wa2p, norms["b_a"], norms["ret"],
                                                  norms["gla"], o, dymix, states)
    d_win = _mm_tn("mixb_dwin", h2, dproj, D, 640, min(T, 1024), BF16)
    dh2 = _mm_mstream("mixb_dh", [dproj], [w_in], [0], "nt", None, 1.0, F32)
    dx1, dy1b, d_gmix = _rms_bwd("mixb_rms", x1, norms["mix"], dh2, dx2, 0.5)

    dh1, d_ffn1 = _ffn_bwd("ffn1b", x, norms["ffn1"], fw, 0, saved1, dx1, dy1b)
    dx0, _, d_g1 = _rms_bwd("ffn1b_rms", x, norms["ffn1"], dh1, dx1, 1.0)

    big = dict(ffn1=d_ffn1, ffn2=d_ffn2, w_out=d_wout, w_in=d_win)
    small = dict(ffn1=d_g1, mix=d_gmix, ffn2=d_g2, final=d_final, ret=d_ret, gla=d_gla, b_a=d_ba, wa2p=d_wa2p)
    return loss, dx0, big, small


def _coords():
    return lax.axis_index("x"), lax.axis_index("y"), lax.axis_index("c")


def _all_gather_weights(f_loc, o_loc, i_loc, a_loc):
    views = [lambda ref, k: ref.at[:, k], lambda ref, k: ref.at[k], lambda ref, k: ref.at[k], lambda ref, k: ref.at[k]]
    n_arr = 4

    def body(f_ref, o_ref, i_ref, a_ref, fo_ref, oo_ref, io_ref, ao_ref, send_sems, recv_sems, local_sems):
        x, y, c = _coords()
        me, sibling = (x, y, c), (x, y, 1 - c)
        chips = [(1 - x, y), (x, 1 - y), (1 - x, 1 - y)]
        srcs = [f_ref, o_ref, i_ref, a_ref]
        outs = [fo_ref, oo_ref, io_ref, ao_ref]

        def blk(m, dev):
            return views[m](outs[m], 4 * dev[0] + 2 * dev[1] + dev[2])

        def copy(m, s, block, to, src=None):
            return pltpu.make_async_remote_copy(
                src_ref=blk(m, block) if src is None else src, dst_ref=blk(m, block),
                send_sem=send_sems.at[m, s], recv_sem=recv_sems.at[m, s],
                device_id=to, device_id_type=MESH)

        mine = [pltpu.make_async_copy(srcs[m], blk(m, me), local_sems.at[m]) for m in range(n_arr)]
        for cp in mine:
            cp.start()
        first = []
        for m in range(n_arr):
            first.append(copy(m, 0, me, sibling, src=srcs[m]))
            first += [copy(m, 1 + j, me, (*chip, c), src=srcs[m]) for j, chip in enumerate(chips)]
        for cp in first:
            cp.start()
        passed = []
        for j, chip in enumerate(chips):
            for m in range(n_arr):
                copy(m, 1 + j, (*chip, c), me).wait_recv()
                fwd = copy(m, 4 + j, (*chip, c), sibling)
                fwd.start()
                passed.append(fwd)
        for m in range(n_arr):
            copy(m, 0, sibling, me).wait_recv()
            for j, chip in enumerate(chips):
                copy(m, 4 + j, (*chip, 1 - c), me).wait_recv()
        for cp in first + passed:
            cp.wait_send()
        for cp in mine:
            cp.wait()

    out_shape = [
        jax.ShapeDtypeStruct((f_loc.shape[0], N_DEV) + f_loc.shape[1:], f_loc.dtype),
        jax.ShapeDtypeStruct((N_DEV,) + o_loc.shape, o_loc.dtype),
        jax.ShapeDtypeStruct((N_DEV,) + i_loc.shape, i_loc.dtype),
        jax.ShapeDtypeStruct((N_DEV,) + a_loc.shape, a_loc.dtype),
    ]
    return pl.pallas_call(
        body, name="ag_weights", out_shape=out_shape,
        in_specs=[ANY] * 4, out_specs=[ANY] * 4,
        scratch_shapes=[pltpu.SemaphoreType.DMA((n_arr, 7)), pltpu.SemaphoreType.DMA((n_arr, 7)),
                        pltpu.SemaphoreType.DMA((n_arr,))],
    )(f_loc, o_loc, i_loc, a_loc)


def _all_gather_small(v):
    m_per, n = v.shape

    def body(x_ref, out_ref, send_sems, recv_sems, local_sem):
        x, y, c = _coords()
        me, sibling = (x, y, c), (x, y, 1 - c)
        chips = [(1 - x, y), (x, 1 - y), (1 - x, 1 - y)]

        def rows(px, py, pc):
            return out_ref.at[pl.ds((4 * px + 2 * py + pc) * m_per, m_per), :]

        def copy(k, block, to, src=None):
            return pltpu.make_async_remote_copy(
                src_ref=rows(*block) if src is None else src, dst_ref=rows(*block),
                send_sem=send_sems.at[k], recv_sem=recv_sems.at[k], device_id=to, device_id_type=MESH)

        mine = pltpu.make_async_copy(x_ref, rows(*me), local_sem)
        mine.start()
        first = [copy(0, me, sibling, src=x_ref)]
        first += [copy(1 + j, me, (*chip, c), src=x_ref) for j, chip in enumerate(chips)]
        for cp in first:
            cp.start()
        passed = [copy(4 + j, (*chip, c), sibling) for j, chip in enumerate(chips)]
        for j, chip in enumerate(chips):
            copy(1 + j, (*chip, c), me).wait_recv()
            passed[j].start()
        copy(0, sibling, me).wait_recv()
        for j, chip in enumerate(chips):
            copy(4 + j, (*chip, 1 - c), me).wait_recv()
        for cp in first + passed:
            cp.wait_send()
        mine.wait()

    return pl.pallas_call(
        body, name="ag_small", out_shape=jax.ShapeDtypeStruct((N_DEV * m_per, n), v.dtype),
        in_specs=[pl.BlockSpec(memory_space=pltpu.VMEM)], out_specs=pl.BlockSpec(memory_space=pltpu.VMEM),
        scratch_shapes=[pltpu.SemaphoreType.DMA((7,)), pltpu.SemaphoreType.DMA((7,)), pltpu.SemaphoreType.DMA],
    )(v)


def _exchange_sibling(grads):
    n = len(grads)

    def body(*refs):
        g_refs, out_refs = refs[:n], refs[n:2 * n]
        send_sems, recv_sems = refs[2 * n], refs[2 * n + 1]
        x, y, c = _coords()
        copies = [pltpu.make_async_remote_copy(
            src_ref=g_refs[m].at[:, 1 - c], dst_ref=out_refs[m], send_sem=send_sems.at[m],
            recv_sem=recv_sems.at[m], device_id=(x, y, 1 - c), device_id_type=MESH) for m in range(n)]
        for cp in copies:
            cp.start()
        for cp in copies:
            cp.wait()

    return pl.pallas_call(
        body, name="rs_sibling",
        out_shape=[jax.ShapeDtypeStruct((g.shape[0],) + g.shape[2:], g.dtype) for g in grads],
        in_specs=[ANY] * n, out_specs=[ANY] * n,
        scratch_shapes=[pltpu.SemaphoreType.DMA((n,)), pltpu.SemaphoreType.DMA((n,))],
    )(*grads)


def _exchange_chips(partials):
    n = len(partials)

    def body(*refs):
        p_refs, out_refs = refs[:n], refs[n:2 * n]
        send_sems, recv_sems = refs[2 * n], refs[2 * n + 1]
        x, y, c = _coords()
        chips = [(1 - x, y), (x, 1 - y), (1 - x, 1 - y)]
        copies = []
        for m in range(n):
            for j, (px, py) in enumerate(chips):
                copies.append(pltpu.make_async_remote_copy(
                    src_ref=p_refs[m].at[2 * px + py], dst_ref=out_refs[m].at[j], send_sem=send_sems.at[m, j],
                    recv_sem=recv_sems.at[m, j], device_id=(px, py, c), device_id_type=MESH))
        for cp in copies:
            cp.start()
        for cp in copies:
            cp.wait()

    return pl.pallas_call(
        body, name="rs_chips",
        out_shape=[jax.ShapeDtypeStruct((3,) + p.shape[1:], p.dtype) for p in partials],
        in_specs=[ANY] * n, out_specs=[ANY] * n,
        scratch_shapes=[pltpu.SemaphoreType.DMA((n, 3)), pltpu.SemaphoreType.DMA((n, 3))],
    )(*partials)


def _pair_sum(name, g, recv, core):
    _, _, R, C = g.shape

    def body(c_ref, g_ref, r_ref, o_ref):
        o_ref[...] = (g_ref[...].astype(F32) + r_ref[...].astype(F32)).astype(o_ref.dtype)

    return pl.pallas_call(
        body, name=name, out_shape=jax.ShapeDtypeStruct((4, R, C), BF16),
        grid_spec=pltpu.PrefetchScalarGridSpec(
            num_scalar_prefetch=1, grid=(4,),
            in_specs=[pl.BlockSpec((None, None, R, C), lambda j, s: (j, s[0], 0, 0)),
                      pl.BlockSpec((None, R, C), lambda j, s: (j, 0, 0))],
            out_specs=pl.BlockSpec((None, R, C), lambda j, s: (j, 0, 0))),
        compiler_params=_cparams(),
    )(core, g, recv)


def _chip_sum(name, partial, recv, chip):
    _, R, C = partial.shape

    def body(c_ref, p_ref, r_ref, o_ref):
        acc = p_ref[...].astype(F32)
        for j in range(3):
            acc = acc + r_ref[j].astype(F32)
        o_ref[...] = acc

    return pl.pallas_call(
        body, name=name, out_shape=jax.ShapeDtypeStruct((R, C), F32),
        grid_spec=pltpu.PrefetchScalarGridSpec(
            num_scalar_prefetch=1, grid=(1,),
            in_specs=[pl.BlockSpec((None, R, C), lambda i, s: (s[0], 0, 0)),
                      pl.BlockSpec((3, R, C), lambda i, s: (0, 0, 0))],
            out_specs=pl.BlockSpec((R, C), lambda i, s: (0, 0))),
        compiler_params=_cparams(),
    )(chip, partial, recv)


def _sum_devices(name, gathered, m_per):
    def body(g_ref, o_ref):
        acc = g_ref[0:m_per, :]
        for k in range(1, N_DEV):
            acc = acc + g_ref[k * m_per:(k + 1) * m_per, :]
        o_ref[...] = acc

    return pl.pallas_call(body, name=name, out_shape=jax.ShapeDtypeStruct((m_per, 128), F32))(gathered)


def _adamw(name, g, w, m, v):
    R, C = w.shape
    tr = R
    for cand in (256, 176):
        if R % cand == 0 and R > cand:
            tr = cand
            break

    def body(g_ref, w_ref, m_ref, v_ref, d_ref, nm_ref, nv_ref):
        gv = g_ref[...]
        nm = ADAM_B1 * m_ref[...] + (1.0 - ADAM_B1) * gv
        nv = ADAM_B2 * v_ref[...] + (1.0 - ADAM_B2) * (gv * gv)
        m_hat = nm / (1.0 - ADAM_B1 ** ADAM_STEP)
        v_hat = nv / (1.0 - ADAM_B2 ** ADAM_STEP)
        d_ref[...] = -ADAM_LR * (m_hat / (jnp.sqrt(v_hat) + ADAM_EPS) + ADAM_WD * w_ref[...])
        nm_ref[...] = nm
        nv_ref[...] = nv

    spec = pl.BlockSpec((tr, C), lambda i: (i, 0))
    shp = jax.ShapeDtypeStruct((R, C), F32)
    return pl.pallas_call(
        body, name=name, grid=(R // tr,), in_specs=[spec] * 4, out_specs=[spec] * 3, out_shape=[shp] * 3,
        compiler_params=_cparams(),
    )(g, w, m, v)


SMALL_ORDER = ("ffn1", "mix", "ffn2", "final", "ret", "gla", "b_a")


def kernel(x, ffn1_norm_g, ffn1_w_gate, ffn1_w_up, ffn1_w_down, mix_norm_g, w_in, ret_norm_g, gla_w_a2, gla_b_a, gla_norm_g, w_out, ffn2_norm_g, ffn2_w_gate, ffn2_w_up, ffn2_w_down, final_norm_g, loss_target, m_ffn1_norm_g, m_ffn1_w_gate, m_ffn1_w_up, m_ffn1_w_down, m_mix_norm_g, m_w_in, m_ret_norm_g, m_gla_w_a2, m_gla_b_a, m_gla_norm_g, m_w_out, m_ffn2_norm_g, m_ffn2_w_gate, m_ffn2_w_up, m_ffn2_w_down, m_final_norm_g, v_ffn1_norm_g, v_ffn1_w_gate, v_ffn1_w_up, v_ffn1_w_down, v_mix_norm_g, v_w_in, v_ret_norm_g, v_gla_w_a2, v_gla_b_a, v_gla_norm_g, v_w_out, v_ffn2_norm_g, v_ffn2_w_gate, v_ffn2_w_up, v_ffn2_w_down, v_final_norm_g):
    xi, yi, ci = _coords()
    dev = 4 * xi + 2 * yi + ci
    core = jnp.reshape(ci, (1,)).astype(jnp.int32)
    chip = jnp.reshape(2 * xi + yi, (1,)).astype(jnp.int32)

    T, D = x.shape[1], x.shape[2]
    fb = ffn1_w_gate.shape[2]
    ib = w_in.shape[2]
    ab = gla_w_a2.shape[2]

    f_loc = jnp.stack([ffn1_w_gate[0].T, ffn1_w_up[0].T, ffn1_w_down[0],
                       ffn2_w_gate[0].T, ffn2_w_up[0].T, ffn2_w_down[0]]).astype(BF16)
    f_all, o_all, i_all, a_all = _all_gather_weights(f_loc, w_out[0].astype(BF16), w_in[0].astype(BF16), gla_w_a2[0])
    fw = f_all.reshape(6, N_DEV * fb, D)
    w_out_full = o_all.reshape(D, D)
    w_in_full = jnp.pad(jnp.transpose(i_all, (1, 0, 2)).reshape(D, N_DEV * ib), ((0, 0), (0, PROJ_W - N_DEV * ib)))
    wa2 = jnp.transpose(a_all, (1, 0, 2)).reshape(GATE_RANK, N_DEV * ab)
    wa2p = jnp.pad(wa2, ((0, GL_W - GATE_RANK), (0, 0)))

    norms = dict(ffn1=ffn1_norm_g, mix=mix_norm_g, ffn2=ffn2_norm_g, final=final_norm_g.reshape(1, D),
                 ret=ret_norm_g, gla=gla_norm_g, b_a=gla_b_a)
    loss, grad_x, big, small = _local_step(x[0], loss_target[0], fw, w_out_full, w_in_full, wa2p, norms)

    d_win = big["w_in"][:, :N_DEV * ib].reshape(D, 4, 2, ib).transpose(1, 2, 0, 3)
    g_list = [g.reshape(4, 2, fb, D) for g in big["ffn1"] + big["ffn2"]]
    g_list += [big["w_out"].reshape(4, 2, D // N_DEV, D), d_win]
    names = ["ffn1_wg", "ffn1_wu", "ffn1_wd", "ffn2_wg", "ffn2_wu", "ffn2_wd", "w_out", "w_in"]
    recv_a = _exchange_sibling(g_list)
    partials = [_pair_sum("pair_" + nm, g, r, core) for nm, g, r in zip(names, g_list, recv_a)]
    recv_b = _exchange_chips(partials)
    reduced = {nm: _chip_sum("sum_" + nm, p, r, chip) for nm, p, r in zip(names, partials, recv_b)}

    flat = jnp.concatenate([small[k].reshape(-1) for k in SMALL_ORDER]
                           + [small["wa2p"][:GATE_RANK].reshape(-1), loss[0]])
    rows = -(-flat.shape[0] // 128)
    rows = -(-rows // 8) * 8
    packed = jnp.pad(flat, (0, rows * 128 - flat.shape[0])).reshape(rows, 128)
    total = _sum_devices("sum_small", _all_gather_small(packed), rows).reshape(-1)
    sizes = [small[k].size for k in SMALL_ORDER] + [GATE_RANK * QK_W, 128]
    offs = [0]
    for s in sizes:
        offs.append(offs[-1] + s)
    pieces = [total[offs[i]:offs[i + 1]] for i in range(len(sizes))]
    g_small = {k: pieces[i].reshape(small[k].shape) for i, k in enumerate(SMALL_ORDER)}
    g_wa2_full = pieces[len(SMALL_ORDER)].reshape(GATE_RANK, QK_W)
    g_wa2 = lax.dynamic_slice(g_wa2_full, (0, dev * ab), (GATE_RANK, ab))
    loss_total = pieces[len(SMALL_ORDER) + 1][0]

    grads = {
        "ffn1_norm_g": g_small["ffn1"], "ffn1_w_gate": reduced["ffn1_wg"].T[None], "ffn1_w_up": reduced["ffn1_wu"].T[None],
        "ffn1_w_down": reduced["ffn1_wd"][None], "mix_norm_g": g_small["mix"], "w_in": reduced["w_in"][None],
        "ret_norm_g": g_small["ret"], "gla_w_a2": g_wa2[None], "gla_b_a": g_small["b_a"], "gla_norm_g": g_small["gla"],
        "w_out": reduced["w_out"][None], "ffn2_norm_g": g_small["ffn2"], "ffn2_w_gate": reduced["ffn2_wg"].T[None],
        "ffn2_w_up": reduced["ffn2_wu"].T[None], "ffn2_w_down": reduced["ffn2_wd"][None],
        "final_norm_g": g_small["final"].reshape(D),
    }
    weights = dict(ffn1_norm_g=ffn1_norm_g, ffn1_w_gate=ffn1_w_gate, ffn1_w_up=ffn1_w_up, ffn1_w_down=ffn1_w_down,
                   mix_norm_g=mix_norm_g, w_in=w_in, ret_norm_g=ret_norm_g, gla_w_a2=gla_w_a2, gla_b_a=gla_b_a,
                   gla_norm_g=gla_norm_g, w_out=w_out, ffn2_norm_g=ffn2_norm_g, ffn2_w_gate=ffn2_w_gate,
                   ffn2_w_up=ffn2_w_up, ffn2_w_down=ffn2_w_down, final_norm_g=final_norm_g)
    moments_m = dict(ffn1_norm_g=m_ffn1_norm_g, ffn1_w_gate=m_ffn1_w_gate, ffn1_w_up=m_ffn1_w_up, ffn1_w_down=m_ffn1_w_down,
                     mix_norm_g=m_mix_norm_g, w_in=m_w_in, ret_norm_g=m_ret_norm_g, gla_w_a2=m_gla_w_a2, gla_b_a=m_gla_b_a,
                     gla_norm_g=m_gla_norm_g, w_out=m_w_out, ffn2_norm_g=m_ffn2_norm_g, ffn2_w_gate=m_ffn2_w_gate,
                     ffn2_w_up=m_ffn2_w_up, ffn2_w_down=m_ffn2_w_down, final_norm_g=m_final_norm_g)
    moments_v = dict(ffn1_norm_g=v_ffn1_norm_g, ffn1_w_gate=v_ffn1_w_gate, ffn1_w_up=v_ffn1_w_up, ffn1_w_down=v_ffn1_w_down,
                     mix_norm_g=v_mix_norm_g, w_in=v_w_in, ret_norm_g=v_ret_norm_g, gla_w_a2=v_gla_w_a2, gla_b_a=v_gla_b_a,
                     gla_norm_g=v_gla_norm_g, w_out=v_w_out, ffn2_norm_g=v_ffn2_norm_g, ffn2_w_gate=v_ffn2_w_gate,
                     ffn2_w_up=v_ffn2_w_up, ffn2_w_down=v_ffn2_w_down, final_norm_g=v_final_norm_g)

    order = list(weights.keys())
    deltas, new_ms, new_vs = [], [], []
    for nm in order:
        shape = weights[nm].shape
        two_d = (1, shape[0]) if len(shape) == 1 else shape[-2:]
        d, nmom, nvar = _adamw("adamw_" + nm, grads[nm].reshape(two_d), weights[nm].reshape(two_d),
                               moments_m[nm].reshape(two_d), moments_v[nm].reshape(two_d))
        deltas.append(d.reshape(shape))
        new_ms.append(nmom.reshape(shape))
        new_vs.append(nvar.reshape(shape))

    return (loss_total, grad_x[None], *[grads[nm].reshape(weights[nm].shape) for nm in order], *deltas, *new_ms, *new_vs)
```

```python
import functools
import math

import jax
import jax.numpy as jnp
from jax import lax
from jax.experimental import pallas as pl
from jax.experimental.pallas import tpu as pltpu

F32 = jnp.float32
BF16 = jnp.bfloat16
MESH = pl.DeviceIdType.MESH
HBM = pl.BlockSpec(memory_space=pltpu.HBM)

N_DEV = 8
RMS_EPS = 1e-6
ROPE_BASE = 10000.0
HEADS = 4
DK = 64
DV = 128
QK_W = HEADS * DK
V_W = HEADS * DV
GATE_RANK = 16
GATE_NORM = 16.0
CHUNK = 64
SUPER = 256
PROJ_W = 3200
C_RQ, C_RK, C_RV, C_RG, C_GQ, C_GK, C_GV, C_GG, C_GL = 0, 256, 512, 1024, 1536, 1792, 2048, 2560, 3072
GL_W = PROJ_W - C_GL
ADAM_LR, ADAM_B1, ADAM_B2, ADAM_EPS, ADAM_WD, ADAM_STEP = 0.001, 0.9, 0.999, 1e-08, 0.01, 10
VMEM_LIMIT_V7X = 52 * 1024 * 1024


def _cparams(**kw):
    return pltpu.CompilerParams(vmem_limit_bytes=VMEM_LIMIT_V7X, **kw)


def _dot(a, b, form, precision=None):
    dims = {"nn": (((1,), (0,)), ((), ())), "nt": (((1,), (1,)), ((), ())), "tn": (((0,), (0,)), ((), ()))}[form]
    return lax.dot_general(a, b, dims, preferred_element_type=F32, precision=precision)


def _sigmoid(x):
    return 1.0 / (1.0 + jnp.exp(-x))


def _coords():
    return lax.axis_index("x"), lax.axis_index("y"), lax.axis_index("c")


class _NoComm:
    inputs, out_shapes, scratch = (), (), ()


class _AllGather:
    def __init__(self, arrays, kinds):
        self.inputs = tuple(arrays)
        self.kinds = tuple(kinds)
        n = len(arrays)
        self.out_shapes = tuple(
            jax.ShapeDtypeStruct((a.shape[0], N_DEV) + a.shape[1:] if k == "stack" else (N_DEV,) + a.shape, a.dtype)
            for a, k in zip(arrays, kinds))
        self.scratch = (pltpu.SemaphoreType.DMA((n, 7)), pltpu.SemaphoreType.DMA((n, 7)),
                        pltpu.SemaphoreType.DMA((n,)))

    def _ctx(self, srcs, outs, sems):
        send_sems, recv_sems, local_sems = sems
        x, y, c = _coords()
        me, sibling = (x, y, c), (x, y, 1 - c)
        chips = [(1 - x, y), (x, 1 - y), (1 - x, 1 - y)]

        def blk(m, dev):
            k = 4 * dev[0] + 2 * dev[1] + dev[2]
            return outs[m].at[:, k] if self.kinds[m] == "stack" else outs[m].at[k]

        def copy(m, s, block, to, src=None):
            return pltpu.make_async_remote_copy(
                src_ref=blk(m, block) if src is None else src, dst_ref=blk(m, block),
                send_sem=send_sems.at[m, s], recv_sem=recv_sems.at[m, s], device_id=to, device_id_type=MESH)

        def mine(m):
            return pltpu.make_async_copy(srcs[m], blk(m, me), local_sems.at[m])

        def first(m):
            return [copy(m, 0, me, sibling, src=srcs[m])] + [
                copy(m, 1 + j, me, (*chip, c), src=srcs[m]) for j, chip in enumerate(chips)]

        return me, sibling, chips, c, copy, mine, first

    def start(self, srcs, outs, sems):
        me, sibling, chips, c, copy, mine, first = self._ctx(srcs, outs, sems)
        for m in range(len(srcs)):
            mine(m).start()
            for cp in first(m):
                cp.start()

    def mid(self, srcs, outs, sems):
        me, sibling, chips, c, copy, mine, first = self._ctx(srcs, outs, sems)
        for j, chip in enumerate(chips):
            for m in range(len(srcs)):
                copy(m, 1 + j, (*chip, c), me).wait_recv()
                copy(m, 4 + j, (*chip, c), sibling).start()

    def finish(self, srcs, outs, sems):
        me, sibling, chips, c, copy, mine, first = self._ctx(srcs, outs, sems)
        for m in range(len(srcs)):
            copy(m, 0, sibling, me).wait_recv()
            for j, chip in enumerate(chips):
                copy(m, 4 + j, (*chip, 1 - c), me).wait_recv()
            for cp in first(m):
                cp.wait_send()
            for j, chip in enumerate(chips):
                copy(m, 4 + j, (*chip, c), sibling).wait_send()
            mine(m).wait()


RELATIONS = ((0, 0, 1), (1, 0, 0), (0, 1, 0), (1, 1, 0), (1, 0, 1), (0, 1, 1), (1, 1, 1))
NEAR = (0, 1, 2, 4, 5)
FAR = (3, 6)
ALL = NEAR + FAR


class _ReduceScatter:
    def __init__(self, parts):
        self.inputs = tuple(g for g, _ in parts)
        self.slots = tuple(s for _, s in parts)
        self.out_shapes = tuple(jax.ShapeDtypeStruct((len(s),) + g.shape[2:], g.dtype) for g, s in parts)
        n_max = max(len(s) for s in self.slots)
        n = len(parts)
        self.scratch = (pltpu.SemaphoreType.DMA((n, n_max)), pltpu.SemaphoreType.DMA((n, n_max)))

    def _copies(self, srcs, outs, sems):
        send_sems, recv_sems = sems
        x, y, c = _coords()
        copies = []
        for m, slots in enumerate(self.slots):
            for i, s in enumerate(slots):
                fx, fy, fc = RELATIONS[s]
                px = 1 - x if fx else x
                py = 1 - y if fy else y
                pc = 1 - c if fc else c
                copies.append(pltpu.make_async_remote_copy(
                    src_ref=srcs[m].at[2 * px + py, pc], dst_ref=outs[m].at[i], send_sem=send_sems.at[m, i],
                    recv_sem=recv_sems.at[m, i], device_id=(px, py, pc), device_id_type=MESH))
        return copies

    def start(self, srcs, outs, sems):
        for cp in self._copies(srcs, outs, sems):
            cp.start()

    def mid(self, srcs, outs, sems):
        pass

    def finish(self, srcs, outs, sems):
        for cp in self._copies(srcs, outs, sems):
            cp.wait()


def _call(name, main, grid, in_specs, out_specs, out_shape, args, scratch=(), comm=None):
    comm = comm or _NoComm()
    counts = [len(in_specs), len(comm.inputs), len(out_shape), len(comm.out_shapes), len(scratch), len(comm.scratch)]
    n_steps = math.prod(grid)
    hosted = bool(comm.inputs)

    def body(*refs):
        parts, at = [], 0
        for n in counts:
            parts.append(refs[at:at + n])
            at += n
        ins, c_in, outs, c_out, scr, c_scr = parts
        step = pl.program_id(0)
        for d in range(1, len(grid)):
            step = step * grid[d] + pl.program_id(d)
        if hosted:
            @pl.when(step == 0)
            def _():
                comm.start(c_in, c_out, c_scr)
        main(ins, outs, scr)
        if hosted:
            @pl.when(step == n_steps // 2)
            def _():
                comm.mid(c_in, c_out, c_scr)

            @pl.when(step == n_steps - 1)
            def _():
                comm.finish(c_in, c_out, c_scr)

    res = pl.pallas_call(
        body, name=name, grid=grid,
        in_specs=list(in_specs) + [HBM] * counts[1],
        out_specs=list(out_specs) + [HBM] * counts[3],
        out_shape=list(out_shape) + list(comm.out_shapes),
        scratch_shapes=list(scratch) + list(comm.scratch),
        compiler_params=_cparams(),
    )(*args, *comm.inputs)
    return res[:counts[2]], res[counts[2]:]


def _rms_fwd(name, x, g, comm=None):
    T, D = x.shape
    tm = min(T, 512)

    def main(ins, outs, scr):
        x_ref, g_ref = ins
        xv = x_ref[...]
        r = lax.rsqrt(jnp.mean(xv * xv, axis=-1, keepdims=True) + RMS_EPS)
        outs[0][...] = (xv * r * g_ref[...]).astype(outs[0].dtype)

    tile = pl.BlockSpec((tm, D), lambda i: (i, 0))
    (h,), extra = _call(name, main, (T // tm,), [tile, pl.BlockSpec((1, D), lambda i: (0, 0))], [tile],
                        [jax.ShapeDtypeStruct((T, D), BF16)], (x, g), comm=comm)
    return h, extra


def _rms_bwd(name, x, g, dh, dres, out_scale, comm=None):
    T, D = x.shape
    tm = min(T, 512)

    def main(ins, outs, scr):
        x_ref, g_ref, dh_ref, dres_ref = ins
        dx_ref, dxb_ref, dg_ref = outs
        i = pl.program_id(0)
        xv = x_ref[...]
        r = lax.rsqrt(jnp.mean(xv * xv, axis=-1, keepdims=True) + RMS_EPS)
        xhat = xv * r
        dhv = dh_ref[...]

        @pl.when(i == 0)
        def _():
            dg_ref[...] = jnp.zeros_like(dg_ref)

        dg_ref[...] += jnp.sum(dhv * xhat, axis=0, keepdims=True)
        dxhat = dhv * g_ref[...]
        dx = r * (dxhat - xhat * jnp.mean(dxhat * xhat, axis=-1, keepdims=True)) + dres_ref[...]
        dx_ref[...] = dx
        dxb_ref[...] = (out_scale * dx).astype(dxb_ref.dtype)

    tile = pl.BlockSpec((tm, D), lambda i: (i, 0))
    vec = pl.BlockSpec((1, D), lambda i: (0, 0))
    return _call(name, main, (T // tm,), [tile, vec, tile, tile], [tile, tile, vec],
                 [jax.ShapeDtypeStruct((T, D), F32), jax.ShapeDtypeStruct((T, D), BF16),
                  jax.ShapeDtypeStruct((1, D), F32)], (x, g, dh, dres), comm=comm)


def _final_loss_bwd(name, x, g, target, out_scale):
    T, D = x.shape
    tm = min(T, 512)

    def main(ins, outs, scr):
        x_ref, g_ref, t_ref = ins
        dx_ref, dxb_ref, dg_ref, loss_ref = outs
        i = pl.program_id(0)
        xv = x_ref[...]
        r = lax.rsqrt(jnp.mean(xv * xv, axis=-1, keepdims=True) + RMS_EPS)
        xhat = xv * r
        err = xhat * g_ref[...] - t_ref[...]

        @pl.when(i == 0)
        def _():
            dg_ref[...] = jnp.zeros_like(dg_ref)
            loss_ref[...] = jnp.zeros_like(loss_ref)

        loss_ref[...] += jnp.broadcast_to(jnp.sum(err * err) * (0.5 / D), loss_ref.shape)
        dy = err * (1.0 / D)
        dg_ref[...] += jnp.sum(dy * xhat, axis=0, keepdims=True)
        dxhat = dy * g_ref[...]
        dx = r * (dxhat - xhat * jnp.mean(dxhat * xhat, axis=-1, keepdims=True))
        dx_ref[...] = dx
        dxb_ref[...] = (out_scale * dx).astype(dxb_ref.dtype)

    tile = pl.BlockSpec((tm, D), lambda i: (i, 0))
    vec = pl.BlockSpec((1, D), lambda i: (0, 0))
    outs, _ = _call(name, main, (T // tm,), [tile, vec, tile],
                    [tile, tile, vec, pl.BlockSpec((1, 128), lambda i: (0, 0))],
                    [jax.ShapeDtypeStruct((T, D), F32), jax.ShapeDtypeStruct((T, D), BF16),
                     jax.ShapeDtypeStruct((1, D), F32), jax.ShapeDtypeStruct((1, 128), F32)], (x, g, target))
    return outs


def _mm_nstream(name, a, ws, w_sel, w_form, comps, out_dtypes, epilogue, cn, rows=1024, comm=None):
    T, K = a.shape
    N = ws[0].shape[1]
    rows = min(rows, T)
    assert N % cn == 0 and T % rows == 0
    n_w, n_c = len(ws), len(comps)

    def main(ins, outs, scr):
        a_ref = ins[0]
        w_refs = ins[1:1 + n_w]
        c_refs = ins[1 + n_w:]

        def step(r, carry):
            sl = pl.ds(pl.multiple_of(r * rows, rows), rows)
            a_blk = a_ref[sl, :]
            dots = [_dot(a_blk, w_ref[...], w_form) for w_ref in w_refs]
            res = epilogue(dots, [c_ref[sl, :] for c_ref in c_refs])
            for o_ref, o in zip(outs, res):
                o_ref[sl, :] = o.astype(o_ref.dtype)
            return carry

        lax.fori_loop(0, T // rows, step, 0)

    if w_form == "nt":
        w_specs = [pl.BlockSpec((None, cn, K), functools.partial(lambda j, s: (s, j, 0), s=s)) for s in w_sel]
    else:
        w_specs = [pl.BlockSpec((K, cn), lambda j: (0, j)) for _ in ws]
    chunk = pl.BlockSpec((T, cn), lambda j: (0, j))
    return _call(name, main, (N // cn,), [pl.BlockSpec((T, K), lambda j: (0, 0))] + w_specs + [chunk] * n_c,
                 [chunk] * len(out_dtypes), [jax.ShapeDtypeStruct((T, N), dt) for dt in out_dtypes],
                 (a, *ws, *comps), comm=comm)


def _mm_mstream(name, as_, ws, w_sel, w_form, res, scale, out_dtype, tm=512, comm=None):
    T = as_[0].shape[0]
    tm = min(tm, T)
    n_a = len(as_)
    w_shapes = [w.shape[-2:] for w in ws]
    N = w_shapes[0][1] if w_form == "nn" else w_shapes[0][0]
    has_res = res is not None

    def main(ins, outs, scr):
        a_refs = ins[:n_a]
        w_refs = ins[n_a:2 * n_a]
        acc = None
        for a_ref, w_ref in zip(a_refs, w_refs):
            d = _dot(a_ref[...], w_ref[...], w_form)
            acc = d if acc is None else acc + d
        if scale != 1.0:
            acc = acc * scale
        if has_res:
            acc = acc + ins[2 * n_a][...]
        outs[0][...] = acc.astype(outs[0].dtype)

    a_specs = [pl.BlockSpec((tm, a.shape[1]), lambda i: (i, 0)) for a in as_]
    w_specs = []
    for w, s in zip(ws, w_sel):
        if w.ndim == 3:
            w_specs.append(pl.BlockSpec((None,) + tuple(w.shape[1:]), functools.partial(lambda i, s: (s, 0, 0), s=s)))
        else:
            w_specs.append(pl.BlockSpec(tuple(w.shape), lambda i: (0, 0)))
    r_specs = [pl.BlockSpec((tm, N), lambda i: (i, 0))] if has_res else []
    args = list(as_) + list(ws) + ([res] if has_res else [])
    (out,), extra = _call(name, main, (T // tm,), a_specs + w_specs + r_specs,
                          [pl.BlockSpec((tm, N), lambda i: (i, 0))], [jax.ShapeDtypeStruct((T, N), out_dtype)],
                          args, comm=comm)
    return out, extra


def _mm_tn(name, a, b, tmo, tno, out_dtype, tk=1024, comm=None):
    T, Ma = a.shape
    Nb = b.shape[1]
    tk = min(tk, T)
    nk = T // tk

    def main(ins, outs, scr):
        a_ref, b_ref = ins
        (acc_ref,) = scr
        k = pl.program_id(2)

        @pl.when(k == 0)
        def _():
            acc_ref[...] = jnp.zeros_like(acc_ref)

        acc_ref[...] += _dot(a_ref[...], b_ref[...], "tn")

        @pl.when(k == nk - 1)
        def _():
            outs[0][...] = acc_ref[...].astype(outs[0].dtype)

    (out,), extra = _call(
        name, main, (Ma // tmo, Nb // tno, nk),
        [pl.BlockSpec((tk, tmo), lambda i, j, k: (k, i)), pl.BlockSpec((tk, tno), lambda i, j, k: (k, j))],
        [pl.BlockSpec((tmo, tno), lambda i, j, k: (i, j))], [jax.ShapeDtypeStruct((Ma, Nb), out_dtype)],
        (a, b), scratch=[pltpu.VMEM((tmo, tno), F32)], comm=comm)
    return out, extra


def _silu_mul_epilogue(dots, comps):
    g, u = dots
    act = g * _sigmoid(g) * u
    return [g, u, act]


def _dact_epilogue(dots, comps):
    (dact,) = dots
    g = comps[0].astype(F32)
    u = comps[1].astype(F32)
    s = _sigmoid(g)
    silu = g * s
    dsilu = s * (1.0 + g * (1.0 - s))
    return [dact * u * dsilu, dact * silu]


def _identity_epilogue(dots, comps):
    return list(dots)


def _swap_halves(x):
    lane = lax.broadcasted_iota(jnp.int32, x.shape, 1)
    first = (lane % DK) < (DK // 2)
    return jnp.where(first, pltpu.roll(x, 128 - DK // 2, 1), pltpu.roll(x, DK // 2, 1))


def _rotary(t, cos, sin_signed):
    halves = []
    for p in range(QK_W // 128):
        th = t[:, 128 * p:128 * (p + 1)]
        halves.append(th * cos + _swap_halves(th) * sin_signed)
    return jnp.concatenate(halves, axis=1)


def _rotary_transposed(d, cos, sin_signed):
    halves = []
    for p in range(QK_W // 128):
        dh = d[:, 128 * p:128 * (p + 1)]
        halves.append(dh * cos + _swap_halves(dh * sin_signed))
    return jnp.concatenate(halves, axis=1)


def _log_sigmoid(x):
    return jnp.minimum(x, 0.0) - jnp.log(1.0 + jnp.exp(-jnp.abs(x)))


def _attn_masks():
    row = lax.broadcasted_iota(jnp.int32, (SUPER, SUPER), 0)
    col = lax.broadcasted_iota(jnp.int32, (SUPER, SUPER), 1)
    same = (row // CHUNK) == (col // CHUNK)
    return row, col, same


def _group_inputs(grp, pr, cos, sin_signed, lg, wa2, ba):
    if grp == 0:
        q = _rotary(pr[:, C_RQ:C_RQ + QK_W], cos, sin_signed)
        k = _rotary(pr[:, C_RK:C_RK + QK_W], cos, sin_signed) * (DK ** -0.5)
        v = pr[:, C_RV:C_RV + V_W]
        gate = pr[:, C_RG:C_RG + V_W]
        pos = lax.broadcasted_iota(jnp.int32, (SUPER, QK_W), 0).astype(F32) + 1.0
        return q, k, v, gate, pos * lg, None, None
    q = pr[:, C_GQ:C_GQ + QK_W] * (DK ** -0.5)
    k = pr[:, C_GK:C_GK + QK_W]
    v = pr[:, C_GV:C_GV + V_W]
    gate = pr[:, C_GG:C_GG + V_W]
    glow = pr[:, C_GL:C_GL + GL_W]
    logit = _dot(glow.astype(BF16), wa2.astype(BF16), "nn") + ba
    la = _log_sigmoid(logit) * (1.0 / GATE_NORM)
    row, col, _ = _attn_masks()
    lower = (col <= row).astype(F32)
    b_cum = _dot(lower, la, "nn", precision=lax.Precision.HIGHEST)
    return q, k, v, gate, b_cum, glow, logit


def _decay_factors(q, k, b_cum):
    c = b_cum[SUPER // 2 - 1:SUPER // 2, :]
    bl = b_cum[SUPER - 1:SUPER, :]
    e1 = jnp.exp(b_cum - c)
    e2 = jnp.exp(c - b_cum)
    e_b = jnp.exp(b_cum)
    e_l = jnp.exp(bl - b_cum)
    return dict(e1=e1, e2=e2, eb=e_b, el=e_l, ebl=jnp.exp(bl),
                qp=q * e1, qm=q * e2, kp=k * e1, km=k * e2, qs=q * e_b, kl=k * e_l)


def _state_block_mask():
    r = lax.broadcasted_iota(jnp.int32, (V_W, QK_W), 0)
    c = lax.broadcasted_iota(jnp.int32, (V_W, QK_W), 1)
    return (r // DV) == (c // DK)


def _attn_fwd(proj, cos, sin_signed, lg, wa2p, ba, gn_ret, gn_gla, comm=None):
    T = proj.shape[0]
    n_s = T // SUPER

    def main(ins, outs, scr):
        pr_ref, cos_ref, sin_ref, lg_ref, wa2_ref, ba_ref, gr_ref, gg_ref = ins
        o_ref, y_ref, st_ref = outs
        (s_ref,) = scr
        i = pl.program_id(0)

        @pl.when(i == 0)
        def _():
            s_ref[...] = jnp.zeros_like(s_ref)

        pr = pr_ref
        row, col, same = _attn_masks()
        m1 = col <= row
        m2 = jnp.logical_and(col > row, same)
        lane = lax.broadcasted_iota(jnp.int32, (1, QK_W), 1)
        blockmask = _state_block_mask()
        for grp in range(2):
            q, k, v, gate, b_cum, _, _ = _group_inputs(grp, pr, cos_ref[...], sin_ref[...], lg_ref[...],
                                                      wa2_ref[...], ba_ref[...])
            f = _decay_factors(q, k, b_cum)
            gn = gr_ref[...] if grp == 0 else gg_ref[...]
            s_prev = s_ref[grp]
            st_ref[0, grp] = s_prev
            o_inter = _dot(f["qs"].astype(BF16), s_prev.astype(BF16), "nt")
            kmb = f["km"].astype(BF16)
            kpb = f["kp"].astype(BF16)
            vb = v.astype(BF16)
            for h in range(HEADS):
                hm = (lane // DK) == h
                a1 = _dot(jnp.where(hm, f["qp"], 0.0).astype(BF16), kmb, "nt")
                a2 = _dot(jnp.where(hm, f["qm"], 0.0).astype(BF16), kpb, "nt")
                a = jnp.where(m1, a1, jnp.where(m2, a2, 0.0))
                lo = grp * V_W + h * DV
                o_h = _dot(a.astype(BF16), vb[:, h * DV:(h + 1) * DV], "nn") + o_inter[:, h * DV:(h + 1) * DV]
                o_ref[:, lo:lo + DV] = o_h
                r = lax.rsqrt(jnp.mean(o_h * o_h, axis=-1, keepdims=True) + RMS_EPS)
                gte = gate[:, h * DV:(h + 1) * DV]
                y = o_h * r * gn[:, h * DV:(h + 1) * DV] * (gte * _sigmoid(gte))
                y_ref[:, lo:lo + DV] = y.astype(y_ref.dtype)
            upd = _dot(vb, f["kl"].astype(BF16), "tn")
            s_ref[grp] = s_prev * f["ebl"] + jnp.where(blockmask, upd, 0.0)

    const = lambda shape: pl.BlockSpec(shape, lambda i: tuple(0 for _ in shape))
    return _call(
        "attn_fwd", main, (n_s,),
        [pl.BlockSpec((SUPER, PROJ_W), lambda i: (i, 0)),
         pl.BlockSpec((SUPER, 128), lambda i: (i, 0)), pl.BlockSpec((SUPER, 128), lambda i: (i, 0)),
         const((1, QK_W)), const((GL_W, QK_W)), const((1, QK_W)), const((1, V_W)), const((1, V_W))],
        [pl.BlockSpec((SUPER, 2 * V_W), lambda i: (i, 0)), pl.BlockSpec((SUPER, 2 * V_W), lambda i: (i, 0)),
         pl.BlockSpec((1, 2, V_W, QK_W), lambda i: (i, 0, 0, 0))],
        [jax.ShapeDtypeStruct((T, 2 * V_W), F32), jax.ShapeDtypeStruct((T, 2 * V_W), BF16),
         jax.ShapeDtypeStruct((n_s, 2, V_W, QK_W), F32)],
        (proj, cos, sin_signed, lg, wa2p, ba, gn_ret, gn_gla),
        scratch=[pltpu.VMEM((2, V_W, QK_W), F32)], comm=comm)


def _attn_bwd(proj, cos, sin_signed, lg, wa2p, ba, gn_ret, gn_gla, o, dy, states, comm=None):
    T = proj.shape[0]
    n_s = T // SUPER

    def main(ins, outs, scr):
        pr_ref, cos_ref, sin_ref, lg_ref, wa2_ref, ba_ref, gr_ref, gg_ref, o_ref, dy_ref, st_ref = ins
        dp_ref, dgr_ref, dgg_ref, dba_ref, dwa_ref = outs
        (ds_ref,) = scr
        i = pl.program_id(0)

        @pl.when(i == 0)
        def _():
            ds_ref[...] = jnp.zeros_like(ds_ref)
            dgr_ref[...] = jnp.zeros_like(dgr_ref)
            dgg_ref[...] = jnp.zeros_like(dgg_ref)
            dba_ref[...] = jnp.zeros_like(dba_ref)
            dwa_ref[...] = jnp.zeros_like(dwa_ref)

        pr = pr_ref
        cos = cos_ref[...]
        sin_signed = sin_ref[...]
        row, col, same = _attn_masks()
        m1 = col <= row
        m2 = jnp.logical_and(col > row, same)
        m1t = row <= col
        m2t = jnp.logical_and(row > col, same)
        lane = lax.broadcasted_iota(jnp.int32, (1, QK_W), 1)
        blockmask = _state_block_mask()
        for grp in range(2):
            q, k, v, gate, b_cum, glow, logit = _group_inputs(grp, pr, cos, sin_signed, lg_ref[...],
                                                              wa2_ref[...], ba_ref[...])
            f = _decay_factors(q, k, b_cum)
            gn = gr_ref[...] if grp == 0 else gg_ref[...]
            dgn_ref = dgr_ref if grp == 0 else dgg_ref
            do_parts, dgate_parts, dgn_parts = [], [], []
            for h in range(HEADS):
                lo = grp * V_W + h * DV
                o_h = o_ref[:, lo:lo + DV]
                r = lax.rsqrt(jnp.mean(o_h * o_h, axis=-1, keepdims=True) + RMS_EPS)
                n = o_h * r
                gte = gate[:, h * DV:(h + 1) * DV]
                sg = _sigmoid(gte)
                dy_h = dy_ref[:, lo:lo + DV]
                gn_h = gn[:, h * DV:(h + 1) * DV]
                dgate_parts.append(dy_h * n * gn_h * (sg * (1.0 + gte * (1.0 - sg))))
                dz = dy_h * (gte * sg)
                dgn_parts.append(jnp.sum(dz * n, axis=0, keepdims=True))
                dn = dz * gn_h
                do_parts.append(r * (dn - n * jnp.mean(dn * n, axis=-1, keepdims=True)))
            dgn_ref[...] += jnp.concatenate(dgn_parts, axis=1)
            dgate = jnp.concatenate(dgate_parts, axis=1)
            do = jnp.concatenate(do_parts, axis=1)
            dob = do.astype(BF16)
            vb = v.astype(BF16)
            s_prev = st_ref[0, grp]
            ds_new = ds_ref[grp]
            dsb = ds_new.astype(BF16)
            qpb, qmb = f["qp"].astype(BF16), f["qm"].astype(BF16)
            kpb, kmb = f["kp"].astype(BF16), f["km"].astype(BF16)
            dqp = jnp.zeros((SUPER, QK_W), F32)
            dqm = jnp.zeros((SUPER, QK_W), F32)
            dkp = jnp.zeros((SUPER, QK_W), F32)
            dkm = jnp.zeros((SUPER, QK_W), F32)
            dv_parts = []
            for h in range(HEADS):
                hm = (lane // DK) == h
                qp_h = jnp.where(hm, f["qp"], 0.0).astype(BF16)
                qm_h = jnp.where(hm, f["qm"], 0.0).astype(BF16)
                kp_h = jnp.where(hm, f["kp"], 0.0).astype(BF16)
                km_h = jnp.where(hm, f["km"], 0.0).astype(BF16)
                at = jnp.where(m1t, _dot(km_h, qpb, "nt"), jnp.where(m2t, _dot(kp_h, qmb, "nt"), 0.0))
                do_h = dob[:, h * DV:(h + 1) * DV]
                v_h = vb[:, h * DV:(h + 1) * DV]
                dv_parts.append(_dot(at.astype(BF16), do_h, "nn"))
                da = _dot(do_h, v_h, "nt")
                dat = _dot(v_h, do_h, "nt")
                da1 = jnp.where(m1, da, 0.0).astype(BF16)
                da2 = jnp.where(m2, da, 0.0).astype(BF16)
                da1t = jnp.where(m1t, dat, 0.0).astype(BF16)
                da2t = jnp.where(m2t, dat, 0.0).astype(BF16)
                dqp = dqp + _dot(da1, km_h, "nn")
                dqm = dqm + _dot(da2, kp_h, "nn")
                dkm = dkm + _dot(da1t, qp_h, "nn")
                dkp = dkp + _dot(da2t, qm_h, "nn")
            klb = f["kl"].astype(BF16)
            qsb = f["qs"].astype(BF16)
            dqs = _dot(dob, s_prev.astype(BF16), "nn")
            dkl = _dot(vb, dsb, "nn")
            dv = jnp.concatenate(dv_parts, axis=1) + _dot(klb, dsb, "nt")
            ds_ref[grp] = ds_new * f["ebl"] + jnp.where(blockmask, _dot(dob, qsb, "tn"), 0.0)
            dq = dqp * f["e1"] + dqm * f["e2"] + dqs * f["eb"]
            dk = dkm * f["e2"] + dkp * f["e1"] + dkl * f["el"]
            if grp == 0:
                dq = _rotary_transposed(dq, cos, sin_signed)
                dk = _rotary_transposed(dk * (DK ** -0.5), cos, sin_signed)
                dp_ref[:, C_RQ:C_RQ + QK_W] = dq.astype(dp_ref.dtype)
                dp_ref[:, C_RK:C_RK + QK_W] = dk.astype(dp_ref.dtype)
                dp_ref[:, C_RV:C_RV + V_W] = dv.astype(dp_ref.dtype)
                dp_ref[:, C_RG:C_RG + V_W] = dgate.astype(dp_ref.dtype)
            else:
                dkl_kl = dkl * klb.astype(F32)
                db = (dqp * qpb.astype(F32) - dkm * kmb.astype(F32) - dqm * qmb.astype(F32)
                      + dkp * kpb.astype(F32) + dqs * qsb.astype(F32) - dkl_kl)
                last = (jnp.sum(dkl_kl, axis=0, keepdims=True)
                        + f["ebl"] * jnp.sum(s_prev * ds_new, axis=0, keepdims=True))
                rowq = lax.broadcasted_iota(jnp.int32, (SUPER, QK_W), 0)
                db = db + jnp.where(rowq == SUPER - 1, last, 0.0)
                upper = (col >= row).astype(F32)
                dla = _dot(upper, db, "nn", precision=lax.Precision.HIGHEST)
                dlogit = dla * (1.0 / GATE_NORM) * (1.0 - _sigmoid(logit))
                dlb = dlogit.astype(BF16)
                dglow = _dot(dlb, wa2_ref[...].astype(BF16), "nt")
                dwa_ref[...] += _dot(glow.astype(BF16), dlb, "tn")
                dba_ref[...] += jnp.sum(dlogit, axis=0, keepdims=True)
                dp_ref[:, C_GQ:C_GQ + QK_W] = (dq * (DK ** -0.5)).astype(dp_ref.dtype)
                dp_ref[:, C_GK:C_GK + QK_W] = dk.astype(dp_ref.dtype)
                dp_ref[:, C_GV:C_GV + V_W] = dv.astype(dp_ref.dtype)
                dp_ref[:, C_GG:C_GG + V_W] = dgate.astype(dp_ref.dtype)
                dp_ref[:, C_GL:C_GL + GL_W] = dglow.astype(dp_ref.dtype)

    rev = lambda i: n_s - 1 - i
    const = lambda shape: pl.BlockSpec(shape, lambda i: tuple(0 for _ in shape))
    return _call(
        "attn_bwd", main, (n_s,),
        [pl.BlockSpec((SUPER, PROJ_W), lambda i: (rev(i), 0)),
         pl.BlockSpec((SUPER, 128), lambda i: (rev(i), 0)), pl.BlockSpec((SUPER, 128), lambda i: (rev(i), 0)),
         const((1, QK_W)), const((GL_W, QK_W)), const((1, QK_W)), const((1, V_W)), const((1, V_W)),
         pl.BlockSpec((SUPER, 2 * V_W), lambda i: (rev(i), 0)),
         pl.BlockSpec((SUPER, 2 * V_W), lambda i: (rev(i), 0)),
         pl.BlockSpec((1, 2, V_W, QK_W), lambda i: (rev(i), 0, 0, 0))],
        [pl.BlockSpec((SUPER, PROJ_W), lambda i: (rev(i), 0)),
         const((1, V_W)), const((1, V_W)), const((1, QK_W)), const((GL_W, QK_W))],
        [jax.ShapeDtypeStruct((T, PROJ_W), BF16),
         jax.ShapeDtypeStruct((1, V_W), F32), jax.ShapeDtypeStruct((1, V_W), F32),
         jax.ShapeDtypeStruct((1, QK_W), F32), jax.ShapeDtypeStruct((GL_W, QK_W), F32)],
        (proj, cos, sin_signed, lg, wa2p, ba, gn_ret, gn_gla, o, dy, states),
        scratch=[pltpu.VMEM((2, V_W, QK_W), F32)], comm=comm)


def _rotary_tables(T):
    half = DK // 2
    inv = ROPE_BASE ** (-jnp.arange(half, dtype=F32) * 2.0 / DK)
    ang = jnp.arange(T, dtype=F32)[:, None] * inv[None, :]
    cos, sin = jnp.cos(ang), jnp.sin(ang)
    cos_head = jnp.concatenate([cos, cos], axis=1)
    sin_head = jnp.concatenate([-sin, sin], axis=1)
    return jnp.tile(cos_head, (1, 128 // DK)), jnp.tile(sin_head, (1, 128 // DK))


def _all_gather_small(v):
    m_per, n = v.shape

    def body(x_ref, out_ref, send_sems, recv_sems, local_sem):
        x, y, c = _coords()
        me, sibling = (x, y, c), (x, y, 1 - c)
        chips = [(1 - x, y), (x, 1 - y), (1 - x, 1 - y)]

        def rows(px, py, pc):
            return out_ref.at[pl.ds((4 * px + 2 * py + pc) * m_per, m_per), :]

        def copy(k, block, to, src=None):
            return pltpu.make_async_remote_copy(
                src_ref=rows(*block) if src is None else src, dst_ref=rows(*block),
                send_sem=send_sems.at[k], recv_sem=recv_sems.at[k], device_id=to, device_id_type=MESH)

        mine = pltpu.make_async_copy(x_ref, rows(*me), local_sem)
        mine.start()
        first = [copy(0, me, sibling, src=x_ref)]
        first += [copy(1 + j, me, (*chip, c), src=x_ref) for j, chip in enumerate(chips)]
        for cp in first:
            cp.start()
        passed = [copy(4 + j, (*chip, c), sibling) for j, chip in enumerate(chips)]
        for j, chip in enumerate(chips):
            copy(1 + j, (*chip, c), me).wait_recv()
            passed[j].start()
        copy(0, sibling, me).wait_recv()
        for j, chip in enumerate(chips):
            copy(4 + j, (*chip, 1 - c), me).wait_recv()
        for cp in first + passed:
            cp.wait_send()
        mine.wait()

    return pl.pallas_call(
        body, name="ag_small", out_shape=jax.ShapeDtypeStruct((N_DEV * m_per, n), v.dtype),
        in_specs=[pl.BlockSpec(memory_space=pltpu.VMEM)], out_specs=pl.BlockSpec(memory_space=pltpu.VMEM),
        scratch_shapes=[pltpu.SemaphoreType.DMA((7,)), pltpu.SemaphoreType.DMA((7,)), pltpu.SemaphoreType.DMA],
    )(v)


def _sum_devices(name, gathered, m_per):
    def body(g_ref, o_ref):
        acc = g_ref[0:m_per, :]
        for k in range(1, N_DEV):
            acc = acc + g_ref[k * m_per:(k + 1) * m_per, :]
        o_ref[...] = acc

    return pl.pallas_call(body, name=name, out_shape=jax.ShapeDtypeStruct((m_per, 128), F32))(gathered)


def _owner_sum(name, grad, landed, owner):
    _, _, R, C = grad.shape
    n_l = len(landed)

    def body(s_ref, g_ref, *refs):
        o_ref = refs[n_l]
        acc = g_ref[...].astype(F32)
        for l_ref in refs[:n_l]:
            for j in range(l_ref.shape[0]):
                acc = acc + l_ref[j].astype(F32)
        o_ref[...] = acc

    return pl.pallas_call(
        body, name=name, out_shape=jax.ShapeDtypeStruct((R, C), F32),
        grid_spec=pltpu.PrefetchScalarGridSpec(
            num_scalar_prefetch=1, grid=(1,),
            in_specs=[pl.BlockSpec((None, None, R, C), lambda i, s: (s[0], s[1], 0, 0))]
            + [pl.BlockSpec(tuple(l.shape), lambda i, s: (0, 0, 0)) for l in landed],
            out_specs=pl.BlockSpec((R, C), lambda i, s: (0, 0))),
        compiler_params=_cparams(),
    )(owner, grad, *landed)


def _adamw(name, g, w, m, v):
    R, C = w.shape
    tr = R
    for cand in (256, 176):
        if R % cand == 0 and R > cand:
            tr = cand
            break

    def body(g_ref, w_ref, m_ref, v_ref, d_ref, nm_ref, nv_ref):
        gv = g_ref[...]
        nm = ADAM_B1 * m_ref[...] + (1.0 - ADAM_B1) * gv
        nv = ADAM_B2 * v_ref[...] + (1.0 - ADAM_B2) * (gv * gv)
        m_hat = nm / (1.0 - ADAM_B1 ** ADAM_STEP)
        v_hat = nv / (1.0 - ADAM_B2 ** ADAM_STEP)
        d_ref[...] = -ADAM_LR * (m_hat / (jnp.sqrt(v_hat) + ADAM_EPS) + ADAM_WD * w_ref[...])
        nm_ref[...] = nm
        nv_ref[...] = nv

    spec = pl.BlockSpec((tr, C), lambda i: (i, 0))
    shp = jax.ShapeDtypeStruct((R, C), F32)
    return pl.pallas_call(
        body, name=name, grid=(R // tr,), in_specs=[spec] * 4, out_specs=[spec] * 3, out_shape=[shp] * 3,
        compiler_params=_cparams(),
    )(g, w, m, v)


SMALL_ORDER = ("ffn1", "mix", "ffn2", "final", "ret", "gla", "b_a")


def kernel(x, ffn1_norm_g, ffn1_w_gate, ffn1_w_up, ffn1_w_down, mix_norm_g, w_in, ret_norm_g, gla_w_a2, gla_b_a, gla_norm_g, w_out, ffn2_norm_g, ffn2_w_gate, ffn2_w_up, ffn2_w_down, final_norm_g, loss_target, m_ffn1_norm_g, m_ffn1_w_gate, m_ffn1_w_up, m_ffn1_w_down, m_mix_norm_g, m_w_in, m_ret_norm_g, m_gla_w_a2, m_gla_b_a, m_gla_norm_g, m_w_out, m_ffn2_norm_g, m_ffn2_w_gate, m_ffn2_w_up, m_ffn2_w_down, m_final_norm_g, v_ffn1_norm_g, v_ffn1_w_gate, v_ffn1_w_up, v_ffn1_w_down, v_mix_norm_g, v_w_in, v_ret_norm_g, v_gla_w_a2, v_gla_b_a, v_gla_norm_g, v_w_out, v_ffn2_norm_g, v_ffn2_w_gate, v_ffn2_w_up, v_ffn2_w_down, v_final_norm_g):
    xi, yi, ci = _coords()
    dev = 4 * xi + 2 * yi + ci
    owner = jnp.stack([2 * xi + yi, ci]).astype(jnp.int32)

    x0, target = x[0], loss_target[0]
    T, D = x0.shape
    fb = ffn1_w_gate.shape[2]
    ib = w_in.shape[2]
    ab = gla_w_a2.shape[2]
    F = N_DEV * fb
    cos, sin_signed = _rotary_tables(T)
    lg = jnp.repeat(jnp.log(1.0 - 2.0 ** (-5.0 - jnp.arange(HEADS, dtype=F32))), DK)[None, :]
    g_final = final_norm_g.reshape(1, D)

    gu1_loc = jnp.stack([ffn1_w_gate[0].T, ffn1_w_up[0].T]).astype(BF16)
    d1_loc = ffn1_w_down.astype(BF16)
    g2_loc = ffn2_w_gate[0].T[None].astype(BF16)
    u2_loc = ffn2_w_up[0].T[None].astype(BF16)
    d2_loc = ffn2_w_down.astype(BF16)
    in_loc = w_in[0].T.astype(BF16)
    out_loc = w_out[0].astype(BF16)

    h1, (gu1,) = _rms_fwd("ffn1_rms", x0, ffn1_norm_g, comm=_AllGather([gu1_loc], ["stack"]))
    gu1 = gu1.reshape(2, F, D)
    (gate1, up1, act1), (d1, in_all) = _mm_nstream(
        "ffn1_up", h1, [gu1, gu1], [0, 1], "nt", [], [BF16, BF16, BF16], _silu_mul_epilogue, cn=256,
        comm=_AllGather([d1_loc, in_loc], ["stack", "plain"]))
    d1 = d1.reshape(1, F, D)
    w_in_t = jnp.pad(in_all.reshape(N_DEV * ib, D), ((0, PROJ_W - N_DEV * ib), (0, 0)))
    x1, (out_all, a_all) = _mm_mstream("ffn1_down", [act1], [d1], [0], "nn", x0, 0.5, F32,
                                       comm=_AllGather([out_loc, gla_w_a2[0]], ["plain", "plain"]))
    w_out_full = out_all.reshape(D, D)
    wa2 = jnp.transpose(a_all, (1, 0, 2)).reshape(GATE_RANK, N_DEV * ab)
    wa2p = jnp.pad(wa2, ((0, GL_W - GATE_RANK), (0, 0)))

    h2, _ = _rms_fwd("mix_rms", x1, mix_norm_g)
    (proj,), (g2,) = _mm_nstream("mix_proj", h2, [w_in_t[None]], [0], "nt", [], [F32], _identity_epilogue, cn=640,
                                 comm=_AllGather([g2_loc], ["stack"]))
    (o, ymix, states), (u2,) = _attn_fwd(proj, cos, sin_signed, lg, wa2p, gla_b_a, ret_norm_g, gla_norm_g,
                                         comm=_AllGather([u2_loc], ["stack"]))
    x2, _ = _mm_mstream("mix_out", [ymix], [w_out_full], [0], "nn", x1, 1.0, F32)
    g2, u2 = g2.reshape(1, F, D), u2.reshape(1, F, D)

    h3, _ = _rms_fwd("ffn2_rms", x2, ffn2_norm_g)
    (gate2, up2, act2), (d2,) = _mm_nstream(
        "ffn2_up", h3, [g2, u2], [0, 0], "nt", [], [BF16, BF16, BF16], _silu_mul_epilogue, cn=256,
        comm=_AllGather([d2_loc], ["stack"]))
    d2 = d2.reshape(1, F, D)
    x3, _ = _mm_mstream("ffn2_down", [act2], [d2], [0], "nn", x2, 0.5, F32)

    dx3, dy3b, d_final, loss = _final_loss_bwd("final", x3, g_final, target, 0.5)

    (dgate2, dup2), _ = _mm_nstream("ffn2b_dact", dy3b, [d2], [0], "nt", [gate2, up2], [BF16, BF16],
                                    _dact_epilogue, cn=256)
    dwg2, _ = _mm_tn("ffn2b_dwg", dgate2, h3, F // 2, D, BF16)
    dwg2 = dwg2.reshape(4, 2, fb, D)
    dwu2, (l_wg2_near,) = _mm_tn("ffn2b_dwu", dup2, h3, F // 2, D, BF16, comm=_ReduceScatter([(dwg2, NEAR)]))
    dwu2 = dwu2.reshape(4, 2, fb, D)
    dwd2, (l_wg2_far,) = _mm_tn("ffn2b_dwd", act2, dy3b, F // 2, D, BF16, comm=_ReduceScatter([(dwg2, FAR)]))
    dwd2 = dwd2.reshape(4, 2, fb, D)
    dh3, (l_wu2,) = _mm_mstream("ffn2b_dh", [dgate2, dup2], [g2, u2], [0, 0], "nn", None, 1.0, F32,
                                comm=_ReduceScatter([(dwu2, ALL)]))
    (dx2, dx2b, d_g2), (l_wd2_near,) = _rms_bwd("ffn2b_rms", x2, ffn2_norm_g, dh3, dx3, 1.0,
                                               comm=_ReduceScatter([(dwd2, NEAR)]))

    dymix, _ = _mm_mstream("mixb_dy", [dx2b], [w_out_full], [0], "nt", None, 1.0, F32)
    dwout, _ = _mm_tn("mixb_dwout", ymix, dx2b, D, D, BF16)
    dwout = dwout.reshape(4, 2, D // N_DEV, D)
    (dproj, d_ret, d_gla, d_ba, d_wa2p), (l_wd2_far, l_wout) = _attn_bwd(
        proj, cos, sin_signed, lg, wa2p, gla_b_a, ret_norm_g, gla_norm_g, o, dymix, states,
        comm=_ReduceScatter([(dwd2, FAR), (dwout, ALL)]))
    dwin_t, _ = _mm_tn("mixb_dwin", dproj, h2, 640, D, BF16)
    dwin = dwin_t[:N_DEV * ib].reshape(4, 2, ib, D)
    dh2, (l_win_near,) = _mm_mstream("mixb_dh", [dproj], [w_in_t], [0], "nn", None, 1.0, F32,
                                     comm=_ReduceScatter([(dwin, NEAR)]))
    (dx1, dy1b, d_gmix), _ = _rms_bwd("mixb_rms", x1, mix_norm_g, dh2, dx2, 0.5)

    (dgate1, dup1), (l_win_far,) = _mm_nstream("ffn1b_dact", dy1b, [d1], [0], "nt", [gate1, up1], [BF16, BF16],
                                               _dact_epilogue, cn=256, comm=_ReduceScatter([(dwin, FAR)]))
    dwg1, _ = _mm_tn("ffn1b_dwg", dgate1, h1, F // 2, D, BF16)
    dwg1 = dwg1.reshape(4, 2, fb, D)
    dwu1, (l_wg1_near,) = _mm_tn("ffn1b_dwu", dup1, h1, F // 2, D, BF16, comm=_ReduceScatter([(dwg1, NEAR)]))
    dwu1 = dwu1.reshape(4, 2, fb, D)
    dwd1, (l_wg1_far,) = _mm_tn("ffn1b_dwd", act1, dy1b, F // 2, D, BF16, comm=_ReduceScatter([(dwg1, FAR)]))
    dwd1 = dwd1.reshape(4, 2, fb, D)
    dh1, (l_wu1,) = _mm_mstream("ffn1b_dh", [dgate1, dup1], [gu1, gu1], [0, 1], "nn", None, 1.0, F32,
                                comm=_ReduceScatter([(dwu1, ALL)]))
    (dx0, _, d_g1), (l_wd1,) = _rms_bwd("ffn1b_rms", x0, ffn1_norm_g, dh1, dx1, 1.0,
                                        comm=_ReduceScatter([(dwd1, ALL)]))

    reduced = {
        "ffn1_wg": _owner_sum("sum_ffn1_wg", dwg1, [l_wg1_near, l_wg1_far], owner),
        "ffn1_wu": _owner_sum("sum_ffn1_wu", dwu1, [l_wu1], owner),
        "ffn1_wd": _owner_sum("sum_ffn1_wd", dwd1, [l_wd1], owner),
        "ffn2_wg": _owner_sum("sum_ffn2_wg", dwg2, [l_wg2_near, l_wg2_far], owner),
        "ffn2_wu": _owner_sum("sum_ffn2_wu", dwu2, [l_wu2], owner),
        "ffn2_wd": _owner_sum("sum_ffn2_wd", dwd2, [l_wd2_near, l_wd2_far], owner),
        "w_out": _owner_sum("sum_w_out", dwout, [l_wout], owner),
        "w_in": _owner_sum("sum_w_in", dwin, [l_win_near, l_win_far], owner),
    }

    small = dict(ffn1=d_g1, mix=d_gmix, ffn2=d_g2, final=d_final, ret=d_ret, gla=d_gla, b_a=d_ba)
    flat = jnp.concatenate([small[k].reshape(-1) for k in SMALL_ORDER]
                           + [d_wa2p[:GATE_RANK].reshape(-1), loss[0]])
    rows = -(-flat.shape[0] // 128)
    rows = -(-rows // 8) * 8
    packed = jnp.pad(flat, (0, rows * 128 - flat.shape[0])).reshape(rows, 128)
    total = _sum_devices("sum_small", _all_gather_small(packed), rows).reshape(-1)
    sizes = [small[k].size for k in SMALL_ORDER] + [GATE_RANK * QK_W, 128]
    offs = [0]
    for s in sizes:
        offs.append(offs[-1] + s)
    pieces = [total[offs[i]:offs[i + 1]] for i in range(len(sizes))]
    g_small = {k: pieces[i].reshape(small[k].shape) for i, k in enumerate(SMALL_ORDER)}
    g_wa2_full = pieces[len(SMALL_ORDER)].reshape(GATE_RANK, QK_W)
    g_wa2 = lax.dynamic_slice(g_wa2_full, (0, dev * ab), (GATE_RANK, ab))
    loss_total = pieces[len(SMALL_ORDER) + 1][0]

    grads = {
        "ffn1_norm_g": g_small["ffn1"], "ffn1_w_gate": reduced["ffn1_wg"].T[None], "ffn1_w_up": reduced["ffn1_wu"].T[None],
        "ffn1_w_down": reduced["ffn1_wd"][None], "mix_norm_g": g_small["mix"], "w_in": reduced["w_in"].T[None],
        "ret_norm_g": g_small["ret"], "gla_w_a2": g_wa2[None], "gla_b_a": g_small["b_a"], "gla_norm_g": g_small["gla"],
        "w_out": reduced["w_out"][None], "ffn2_norm_g": g_small["ffn2"], "ffn2_w_gate": reduced["ffn2_wg"].T[None],
        "ffn2_w_up": reduced["ffn2_wu"].T[None], "ffn2_w_down": reduced["ffn2_wd"][None],
        "final_norm_g": g_small["final"].reshape(D),
    }
    weights = dict(ffn1_norm_g=ffn1_norm_g, ffn1_w_gate=ffn1_w_gate, ffn1_w_up=ffn1_w_up, ffn1_w_down=ffn1_w_down,
                   mix_norm_g=mix_norm_g, w_in=w_in, ret_norm_g=ret_norm_g, gla_w_a2=gla_w_a2, gla_b_a=gla_b_a,
                   gla_norm_g=gla_norm_g, w_out=w_out, ffn2_norm_g=ffn2_norm_g, ffn2_w_gate=ffn2_w_gate,
                   ffn2_w_up=ffn2_w_up, ffn2_w_down=ffn2_w_down, final_norm_g=final_norm_g)
    moments_m = dict(ffn1_norm_g=m_ffn1_norm_g, ffn1_w_gate=m_ffn1_w_gate, ffn1_w_up=m_ffn1_w_up, ffn1_w_down=m_ffn1_w_down,
                     mix_norm_g=m_mix_norm_g, w_in=m_w_in, ret_norm_g=m_ret_norm_g, gla_w_a2=m_gla_w_a2, gla_b_a=m_gla_b_a,
                     gla_norm_g=m_gla_norm_g, w_out=m_w_out, ffn2_norm_g=m_ffn2_norm_g, ffn2_w_gate=m_ffn2_w_gate,
                     ffn2_w_up=m_ffn2_w_up, ffn2_w_down=m_ffn2_w_down, final_norm_g=m_final_norm_g)
    moments_v = dict(ffn1_norm_g=v_ffn1_norm_g, ffn1_w_gate=v_ffn1_w_gate, ffn1_w_up=v_ffn1_w_up, ffn1_w_down=v_ffn1_w_down,
                     mix_norm_g=v_mix_norm_g, w_in=v_w_in, ret_norm_g=v_ret_norm_g, gla_w_a2=v_gla_w_a2, gla_b_a=v_gla_b_a,
                     gla_norm_g=v_gla_norm_g, w_out=v_w_out, ffn2_norm_g=v_ffn2_norm_g, ffn2_w_gate=v_ffn2_w_gate,
                     ffn2_w_up=v_ffn2_w_up, ffn2_w_down=v_ffn2_w_down, final_norm_g=v_final_norm_g)

    order = list(weights.keys())
    deltas, new_ms, new_vs = [], [], []
    for nm in order:
        shape = weights[nm].shape
        two_d = (1, shape[0]) if len(shape) == 1 else shape[-2:]
        d, nmom, nvar = _adamw("adamw_" + nm, grads[nm].reshape(two_d), weights[nm].reshape(two_d),
                               moments_m[nm].reshape(two_d), moments_v[nm].reshape(two_d))
        deltas.append(d.reshape(shape))
        new_ms.append(nmom.reshape(shape))
        new_vs.append(nvar.reshape(shape))

    return (loss_total, dx0[None], *[grads[nm].reshape(weights[nm].shape) for nm in order], *deltas, *new_ms, *new_vs)
```

```python
import functools
import math

import jax
import jax.numpy as jnp
from jax import lax
from jax.experimental import pallas as pl
from jax.experimental.pallas import tpu as pltpu

F32 = jnp.float32
BF16 = jnp.bfloat16
MESH = pl.DeviceIdType.MESH
HBM = pl.BlockSpec(memory_space=pltpu.HBM)

N_DEV = 8
RMS_EPS = 1e-6
ROPE_BASE = 10000.0
HEADS = 4
DK = 64
DV = 128
QK_W = HEADS * DK
V_W = HEADS * DV
GATE_RANK = 16
GATE_NORM = 16.0
CHUNK = 64
SUPER = 256
PROJ_W = 3200
C_RQ, C_RK, C_RV, C_RG, C_GQ, C_GK, C_GV, C_GG, C_GL = 0, 256, 512, 1024, 1536, 1792, 2048, 2560, 3072
GL_W = PROJ_W - C_GL
ADAM_LR, ADAM_B1, ADAM_B2, ADAM_EPS, ADAM_WD, ADAM_STEP = 0.001, 0.9, 0.999, 1e-08, 0.01, 10
VMEM_LIMIT_V7X = 52 * 1024 * 1024


def _cparams(**kw):
    return pltpu.CompilerParams(vmem_limit_bytes=VMEM_LIMIT_V7X, **kw)


def _dot(a, b, form, precision=None):
    dims = {"nn": (((1,), (0,)), ((), ())), "nt": (((1,), (1,)), ((), ())), "tn": (((0,), (0,)), ((), ()))}[form]
    return lax.dot_general(a, b, dims, preferred_element_type=F32, precision=precision)


def _sigmoid(x):
    return 1.0 / (1.0 + jnp.exp(-x))


def _coords():
    return lax.axis_index("x"), lax.axis_index("y"), lax.axis_index("c")


class _NoComm:
    inputs, out_shapes, scratch = (), (), ()


class _AllGather:
    def __init__(self, arrays, kinds):
        self.inputs = tuple(arrays)
        self.kinds = tuple(kinds)
        n = len(arrays)
        self.out_shapes = tuple(
            jax.ShapeDtypeStruct((a.shape[0], N_DEV) + a.shape[1:] if k == "stack" else (N_DEV,) + a.shape, a.dtype)
            for a, k in zip(arrays, kinds))
        self.scratch = (pltpu.SemaphoreType.DMA((n, 7)), pltpu.SemaphoreType.DMA((n, 7)),
                        pltpu.SemaphoreType.DMA((n,)))

    def _ctx(self, srcs, outs, sems):
        send_sems, recv_sems, local_sems = sems
        x, y, c = _coords()
        me, sibling = (x, y, c), (x, y, 1 - c)
        chips = [(1 - x, y), (x, 1 - y), (1 - x, 1 - y)]

        def blk(m, dev):
            k = 4 * dev[0] + 2 * dev[1] + dev[2]
            return outs[m].at[:, k] if self.kinds[m] == "stack" else outs[m].at[k]

        def copy(m, s, block, to, src=None):
            return pltpu.make_async_remote_copy(
                src_ref=blk(m, block) if src is None else src, dst_ref=blk(m, block),
                send_sem=send_sems.at[m, s], recv_sem=recv_sems.at[m, s], device_id=to, device_id_type=MESH)

        def mine(m):
            return pltpu.make_async_copy(srcs[m], blk(m, me), local_sems.at[m])

        def first(m):
            return [copy(m, 0, me, sibling, src=srcs[m])] + [
                copy(m, 1 + j, me, (*chip, c), src=srcs[m]) for j, chip in enumerate(chips)]

        return me, sibling, chips, c, copy, mine, first

    def start(self, srcs, outs, sems):
        me, sibling, chips, c, copy, mine, first = self._ctx(srcs, outs, sems)
        for m in range(len(srcs)):
            mine(m).start()
            for cp in first(m):
                cp.start()

    def mid(self, srcs, outs, sems):
        me, sibling, chips, c, copy, mine, first = self._ctx(srcs, outs, sems)
        for j, chip in enumerate(chips):
            for m in range(len(srcs)):
                copy(m, 1 + j, (*chip, c), me).wait_recv()
                copy(m, 4 + j, (*chip, c), sibling).start()

    def finish(self, srcs, outs, sems):
        me, sibling, chips, c, copy, mine, first = self._ctx(srcs, outs, sems)
        for m in range(len(srcs)):
            copy(m, 0, sibling, me).wait_recv()
            for j, chip in enumerate(chips):
                copy(m, 4 + j, (*chip, 1 - c), me).wait_recv()
            for cp in first(m):
                cp.wait_send()
            for j, chip in enumerate(chips):
                copy(m, 4 + j, (*chip, c), sibling).wait_send()
            mine(m).wait()


RELATIONS = ((0, 0, 1), (1, 0, 0), (0, 1, 0), (1, 1, 0), (1, 0, 1), (0, 1, 1), (1, 1, 1))
NEAR = (0, 1, 2, 4, 5)
FAR = (3, 6)
ALL = NEAR + FAR


class _ReduceScatter:
    def __init__(self, parts):
        self.inputs = tuple(g for g, _ in parts)
        self.slots = tuple(s for _, s in parts)
        self.out_shapes = tuple(jax.ShapeDtypeStruct((len(s),) + g.shape[2:], g.dtype) for g, s in parts)
        n_max = max(len(s) for s in self.slots)
        n = len(parts)
        self.scratch = (pltpu.SemaphoreType.DMA((n, n_max)), pltpu.SemaphoreType.DMA((n, n_max)))

    def _copies(self, srcs, outs, sems):
        send_sems, recv_sems = sems
        x, y, c = _coords()
        copies = []
        for m, slots in enumerate(self.slots):
            for i, s in enumerate(slots):
                fx, fy, fc = RELATIONS[s]
                px = 1 - x if fx else x
                py = 1 - y if fy else y
                pc = 1 - c if fc else c
                copies.append(pltpu.make_async_remote_copy(
                    src_ref=srcs[m].at[2 * px + py, pc], dst_ref=outs[m].at[i], send_sem=send_sems.at[m, i],
                    recv_sem=recv_sems.at[m, i], device_id=(px, py, pc), device_id_type=MESH))
        return copies

    def start(self, srcs, outs, sems):
        for cp in self._copies(srcs, outs, sems):
            cp.start()

    def mid(self, srcs, outs, sems):
        pass

    def finish(self, srcs, outs, sems):
        for cp in self._copies(srcs, outs, sems):
            cp.wait()


def _call(name, main, grid, in_specs, out_specs, out_shape, args, scratch=(), comm=None):
    comm = comm or _NoComm()
    counts = [len(in_specs), len(comm.inputs), len(out_shape), len(comm.out_shapes), len(scratch), len(comm.scratch)]
    n_steps = math.prod(grid)
    hosted = bool(comm.inputs)

    def body(*refs):
        parts, at = [], 0
        for n in counts:
            parts.append(refs[at:at + n])
            at += n
        ins, c_in, outs, c_out, scr, c_scr = parts
        step = pl.program_id(0)
        for d in range(1, len(grid)):
            step = step * grid[d] + pl.program_id(d)
        if hosted:
            @pl.when(step == 0)
            def _():
                comm.start(c_in, c_out, c_scr)
        main(ins, outs, scr)
        if hosted:
            @pl.when(step == max(n_steps - 2, 0))
            def _():
                comm.mid(c_in, c_out, c_scr)

            @pl.when(step == n_steps - 1)
            def _():
                comm.finish(c_in, c_out, c_scr)

    res = pl.pallas_call(
        body, name=name, grid=grid,
        in_specs=list(in_specs) + [HBM] * counts[1],
        out_specs=list(out_specs) + [HBM] * counts[3],
        out_shape=list(out_shape) + list(comm.out_shapes),
        scratch_shapes=list(scratch) + list(comm.scratch),
        compiler_params=_cparams(),
    )(*args, *comm.inputs)
    return res[:counts[2]], res[counts[2]:]


def _rms_fwd(name, x, g, comm=None):
    T, D = x.shape
    tm = min(T, 512)

    def main(ins, outs, scr):
        x_ref, g_ref = ins
        xv = x_ref[...]
        r = lax.rsqrt(jnp.mean(xv * xv, axis=-1, keepdims=True) + RMS_EPS)
        outs[0][...] = (xv * r * g_ref[...]).astype(outs[0].dtype)

    tile = pl.BlockSpec((tm, D), lambda i: (i, 0))
    (h,), extra = _call(name, main, (T // tm,), [tile, pl.BlockSpec((1, D), lambda i: (0, 0))], [tile],
                        [jax.ShapeDtypeStruct((T, D), BF16)], (x, g), comm=comm)
    return h, extra


def _final_loss_bwd(name, x, g, target, out_scale):
    T, D = x.shape
    tm = min(T, 512)

    def main(ins, outs, scr):
        x_ref, g_ref, t_ref = ins
        dx_ref, dxb_ref, dg_ref, loss_ref = outs
        i = pl.program_id(0)
        xv = x_ref[...]
        r = lax.rsqrt(jnp.mean(xv * xv, axis=-1, keepdims=True) + RMS_EPS)
        xhat = xv * r
        err = xhat * g_ref[...] - t_ref[...]

        @pl.when(i == 0)
        def _():
            dg_ref[...] = jnp.zeros_like(dg_ref)
            loss_ref[...] = jnp.zeros_like(loss_ref)

        loss_ref[...] += jnp.broadcast_to(jnp.sum(err * err) * (0.5 / D), loss_ref.shape)
        dy = err * (1.0 / D)
        dg_ref[...] += jnp.sum(dy * xhat, axis=0, keepdims=True)
        dxhat = dy * g_ref[...]
        dx = r * (dxhat - xhat * jnp.mean(dxhat * xhat, axis=-1, keepdims=True))
        dx_ref[...] = dx
        dxb_ref[...] = (out_scale * dx).astype(dxb_ref.dtype)

    tile = pl.BlockSpec((tm, D), lambda i: (i, 0))
    vec = pl.BlockSpec((1, D), lambda i: (0, 0))
    outs, _ = _call(name, main, (T // tm,), [tile, vec, tile],
                    [tile, tile, vec, pl.BlockSpec((1, 128), lambda i: (0, 0))],
                    [jax.ShapeDtypeStruct((T, D), F32), jax.ShapeDtypeStruct((T, D), BF16),
                     jax.ShapeDtypeStruct((1, D), F32), jax.ShapeDtypeStruct((1, 128), F32)], (x, g, target))
    return outs


def _mm_nstream(name, a, ws, w_sel, w_form, comps, out_dtypes, epilogue, cn, rows=1024, comm=None):
    T, K = a.shape
    N = ws[0].shape[1]
    rows = min(rows, T)
    assert N % cn == 0 and T % rows == 0
    n_w, n_c = len(ws), len(comps)

    def main(ins, outs, scr):
        a_ref = ins[0]
        w_refs = ins[1:1 + n_w]
        c_refs = ins[1 + n_w:]

        def step(r, carry):
            sl = pl.ds(pl.multiple_of(r * rows, rows), rows)
            a_blk = a_ref[sl, :]
            dots = [_dot(a_blk, w_ref[...], w_form) for w_ref in w_refs]
            res = epilogue(dots, [c_ref[sl, :] for c_ref in c_refs])
            for o_ref, o in zip(outs, res):
                o_ref[sl, :] = o.astype(o_ref.dtype)
            return carry

        lax.fori_loop(0, T // rows, step, 0)

    if w_form == "nt":
        w_specs = [pl.BlockSpec((None, cn, K), functools.partial(lambda j, s: (s, j, 0), s=s)) for s in w_sel]
    else:
        w_specs = [pl.BlockSpec((K, cn), lambda j: (0, j)) for _ in ws]
    chunk = pl.BlockSpec((T, cn), lambda j: (0, j))
    return _call(name, main, (N // cn,), [pl.BlockSpec((T, K), lambda j: (0, 0))] + w_specs + [chunk] * n_c,
                 [chunk] * len(out_dtypes), [jax.ShapeDtypeStruct((T, N), dt) for dt in out_dtypes],
                 (a, *ws, *comps), comm=comm)


def _mm_mstream(name, as_, ws, w_sel, w_form, extras, outs_desc, epilogue, tm=512, comm=None):
    T = as_[0].shape[0]
    tm = min(tm, T)
    n_a = len(as_)
    w_shapes = [w.shape[-2:] for w in ws]
    N = w_shapes[0][1] if w_form == "nn" else w_shapes[0][0]

    def main(ins, outs, scr):
        a_refs = ins[:n_a]
        w_refs = ins[n_a:2 * n_a]
        acc = None
        for a_ref, w_ref in zip(a_refs, w_refs):
            d = _dot(a_ref[...], w_ref[...], w_form)
            acc = d if acc is None else acc + d
        epilogue(acc, ins[2 * n_a:], outs)

    kind_spec = {"tile": pl.BlockSpec((tm, N), lambda i: (i, 0)), "vec": pl.BlockSpec((1, N), lambda i: (0, 0))}
    kind_shape = {"tile": (T, N), "vec": (1, N)}
    a_specs = [pl.BlockSpec((tm, a.shape[1]), lambda i: (i, 0)) for a in as_]
    w_specs = []
    for w, s in zip(ws, w_sel):
        if w.ndim == 3:
            w_specs.append(pl.BlockSpec((None,) + tuple(w.shape[1:]), functools.partial(lambda i, s: (s, 0, 0), s=s),
                                        pipeline_mode=pl.Buffered(1)))
        else:
            w_specs.append(pl.BlockSpec(tuple(w.shape), lambda i: (0, 0), pipeline_mode=pl.Buffered(1)))
    args = list(as_) + list(ws) + [e for e, _ in extras]
    return _call(name, main, (T // tm,), a_specs + w_specs + [kind_spec[k] for _, k in extras],
                 [kind_spec[k] for _, k in outs_desc],
                 [jax.ShapeDtypeStruct(kind_shape[k], dt) for dt, k in outs_desc], args, comm=comm)


def _plain_epilogue(acc, ex, outs):
    outs[0][...] = acc.astype(outs[0].dtype)


def _residual_epilogue(scale):
    def ep(acc, ex, outs):
        outs[0][...] = ex[0][...] + scale * acc
    return ep


def _residual_rms_epilogue(scale):
    def ep(acc, ex, outs):
        xv = ex[0][...] + scale * acc
        outs[0][...] = xv
        r = lax.rsqrt(jnp.mean(xv * xv, axis=-1, keepdims=True) + RMS_EPS)
        outs[1][...] = (xv * r * ex[1][...]).astype(outs[1].dtype)
    return ep


def _rms_bwd_epilogue(out_scale):
    def ep(acc, ex, outs):
        x_ref, g_ref, dres_ref = ex
        dx_ref, dxb_ref, dg_ref = outs
        xv = x_ref[...]
        r = lax.rsqrt(jnp.mean(xv * xv, axis=-1, keepdims=True) + RMS_EPS)
        xhat = xv * r

        @pl.when(pl.program_id(0) == 0)
        def _():
            dg_ref[...] = jnp.zeros_like(dg_ref)

        dg_ref[...] += jnp.sum(acc * xhat, axis=0, keepdims=True)
        dxhat = acc * g_ref[...]
        dx = r * (dxhat - xhat * jnp.mean(dxhat * xhat, axis=-1, keepdims=True)) + dres_ref[...]
        dx_ref[...] = dx
        dxb_ref[...] = (out_scale * dx).astype(dxb_ref.dtype)
    return ep


def _mm_tn(name, a, b, tmo, tno, out_dtype, tk=1024, comm=None):
    T, Ma = a.shape
    Nb = b.shape[1]
    tk = min(tk, T)
    nk = T // tk

    def main(ins, outs, scr):
        a_ref, b_ref = ins
        (acc_ref,) = scr
        k = pl.program_id(2)

        @pl.when(k == 0)
        def _():
            acc_ref[...] = jnp.zeros_like(acc_ref)

        acc_ref[...] += _dot(a_ref[...], b_ref[...], "tn")

        @pl.when(k == nk - 1)
        def _():
            outs[0][...] = acc_ref[...].astype(outs[0].dtype)

    (out,), extra = _call(
        name, main, (Ma // tmo, Nb // tno, nk),
        [pl.BlockSpec((tk, tmo), lambda i, j, k: (k, i)), pl.BlockSpec((tk, tno), lambda i, j, k: (k, j))],
        [pl.BlockSpec((tmo, tno), lambda i, j, k: (i, j))], [jax.ShapeDtypeStruct((Ma, Nb), out_dtype)],
        (a, b), scratch=[pltpu.VMEM((tmo, tno), F32)], comm=comm)
    return out, extra


def _silu_mul_epilogue(dots, comps):
    g, u = dots
    act = g * _sigmoid(g) * u
    return [g, u, act]


def _dact_epilogue(dots, comps):
    (dact,) = dots
    g = comps[0].astype(F32)
    u = comps[1].astype(F32)
    s = _sigmoid(g)
    silu = g * s
    dsilu = s * (1.0 + g * (1.0 - s))
    return [dact * u * dsilu, dact * silu]


def _identity_epilogue(dots, comps):
    return list(dots)


def _swap_halves(x):
    lane = lax.broadcasted_iota(jnp.int32, x.shape, 1)
    first = (lane % DK) < (DK // 2)
    return jnp.where(first, pltpu.roll(x, 128 - DK // 2, 1), pltpu.roll(x, DK // 2, 1))


def _rotary(t, cos, sin_signed):
    halves = []
    for p in range(QK_W // 128):
        th = t[:, 128 * p:128 * (p + 1)]
        halves.append(th * cos + _swap_halves(th) * sin_signed)
    return jnp.concatenate(halves, axis=1)


def _rotary_transposed(d, cos, sin_signed):
    halves = []
    for p in range(QK_W // 128):
        dh = d[:, 128 * p:128 * (p + 1)]
        halves.append(dh * cos + _swap_halves(dh * sin_signed))
    return jnp.concatenate(halves, axis=1)


def _log_sigmoid(x):
    return jnp.minimum(x, 0.0) - jnp.log(1.0 + jnp.exp(-jnp.abs(x)))


def _attn_masks():
    row = lax.broadcasted_iota(jnp.int32, (SUPER, SUPER), 0)
    col = lax.broadcasted_iota(jnp.int32, (SUPER, SUPER), 1)
    same = (row // CHUNK) == (col // CHUNK)
    return row, col, same


def _group_inputs(grp, pr, cos, sin_signed, lg, wa2, ba):
    if grp == 0:
        q = _rotary(pr[:, C_RQ:C_RQ + QK_W], cos, sin_signed)
        k = _rotary(pr[:, C_RK:C_RK + QK_W], cos, sin_signed) * (DK ** -0.5)
        v = pr[:, C_RV:C_RV + V_W]
        gate = pr[:, C_RG:C_RG + V_W]
        pos = lax.broadcasted_iota(jnp.int32, (SUPER, QK_W), 0).astype(F32) + 1.0
        return q, k, v, gate, pos * lg, None, None
    q = pr[:, C_GQ:C_GQ + QK_W] * (DK ** -0.5)
    k = pr[:, C_GK:C_GK + QK_W]
    v = pr[:, C_GV:C_GV + V_W]
    gate = pr[:, C_GG:C_GG + V_W]
    glow = pr[:, C_GL:C_GL + GL_W]
    logit = _dot(glow.astype(BF16), wa2.astype(BF16), "nn") + ba
    la = _log_sigmoid(logit) * (1.0 / GATE_NORM)
    row, col, _ = _attn_masks()
    lower = (col <= row).astype(F32)
    b_cum = _dot(lower, la, "nn", precision=lax.Precision.HIGHEST)
    return q, k, v, gate, b_cum, glow, logit


def _decay_factors(q, k, b_cum):
    c = b_cum[SUPER // 2 - 1:SUPER // 2, :]
    bl = b_cum[SUPER - 1:SUPER, :]
    e1 = jnp.exp(b_cum - c)
    e2 = jnp.exp(c - b_cum)
    e_b = jnp.exp(b_cum)
    e_l = jnp.exp(bl - b_cum)
    return dict(e1=e1, e2=e2, eb=e_b, el=e_l, ebl=jnp.exp(bl),
                qp=q * e1, qm=q * e2, kp=k * e1, km=k * e2, qs=q * e_b, kl=k * e_l)


def _state_block_mask():
    r = lax.broadcasted_iota(jnp.int32, (V_W, QK_W), 0)
    c = lax.broadcasted_iota(jnp.int32, (V_W, QK_W), 1)
    return (r // DV) == (c // DK)


def _attn_fwd(proj, cos, sin_signed, lg, wa2p, ba, gn_ret, gn_gla, comm=None):
    T = proj.shape[0]
    n_s = T // SUPER

    def main(ins, outs, scr):
        pr_ref, cos_ref, sin_ref, lg_ref, wa2_ref, ba_ref, gr_ref, gg_ref = ins
        o_ref, y_ref, st_ref = outs
        (s_ref,) = scr
        i = pl.program_id(0)

        @pl.when(i == 0)
        def _():
            s_ref[...] = jnp.zeros_like(s_ref)

        pr = pr_ref
        row, col, same = _attn_masks()
        m1 = col <= row
        m2 = jnp.logical_and(col > row, same)
        lane = lax.broadcasted_iota(jnp.int32, (1, QK_W), 1)
        blockmask = _state_block_mask()
        for grp in range(2):
            q, k, v, gate, b_cum, _, _ = _group_inputs(grp, pr, cos_ref[...], sin_ref[...], lg_ref[...],
                                                      wa2_ref[...], ba_ref[...])
            f = _decay_factors(q, k, b_cum)
            gn = gr_ref[...] if grp == 0 else gg_ref[...]
            s_prev = s_ref[grp]
            st_ref[0, grp] = s_prev
            o_inter = _dot(f["qs"].astype(BF16), s_prev.astype(BF16), "nt")
            kmb = f["km"].astype(BF16)
            kpb = f["kp"].astype(BF16)
            vb = v.astype(BF16)
            for h in range(HEADS):
                hm = (lane // DK) == h
                a1 = _dot(jnp.where(hm, f["qp"], 0.0).astype(BF16), kmb, "nt")
                a2 = _dot(jnp.where(hm, f["qm"], 0.0).astype(BF16), kpb, "nt")
                a = jnp.where(m1, a1, jnp.where(m2, a2, 0.0))
                lo = grp * V_W + h * DV
                o_h = _dot(a.astype(BF16), vb[:, h * DV:(h + 1) * DV], "nn") + o_inter[:, h * DV:(h + 1) * DV]
                o_ref[:, lo:lo + DV] = o_h
                r = lax.rsqrt(jnp.mean(o_h * o_h, axis=-1, keepdims=True) + RMS_EPS)
                gte = gate[:, h * DV:(h + 1) * DV]
                y = o_h * r * gn[:, h * DV:(h + 1) * DV] * (gte * _sigmoid(gte))
                y_ref[:, lo:lo + DV] = y.astype(y_ref.dtype)
            upd = _dot(vb, f["kl"].astype(BF16), "tn")
            s_ref[grp] = s_prev * f["ebl"] + jnp.where(blockmask, upd, 0.0)

    const = lambda shape: pl.BlockSpec(shape, lambda i: tuple(0 for _ in shape))
    return _call(
        "attn_fwd", main, (n_s,),
        [pl.BlockSpec((SUPER, PROJ_W), lambda i: (i, 0)),
         pl.BlockSpec((SUPER, 128), lambda i: (i, 0)), pl.BlockSpec((SUPER, 128), lambda i: (i, 0)),
         const((1, QK_W)), const((GL_W, QK_W)), const((1, QK_W)), const((1, V_W)), const((1, V_W))],
        [pl.BlockSpec((SUPER, 2 * V_W), lambda i: (i, 0)), pl.BlockSpec((SUPER, 2 * V_W), lambda i: (i, 0)),
         pl.BlockSpec((1, 2, V_W, QK_W), lambda i: (i, 0, 0, 0))],
        [jax.ShapeDtypeStruct((T, 2 * V_W), F32), jax.ShapeDtypeStruct((T, 2 * V_W), BF16),
         jax.ShapeDtypeStruct((n_s, 2, V_W, QK_W), F32)],
        (proj, cos, sin_signed, lg, wa2p, ba, gn_ret, gn_gla),
        scratch=[pltpu.VMEM((2, V_W, QK_W), F32)], comm=comm)


def _attn_bwd(proj, cos, sin_signed, lg, wa2p, ba, gn_ret, gn_gla, o, dy, states, comm=None):
    T = proj.shape[0]
    n_s = T // SUPER

    def main(ins, outs, scr):
        pr_ref, cos_ref, sin_ref, lg_ref, wa2_ref, ba_ref, gr_ref, gg_ref, o_ref, dy_ref, st_ref = ins
        dp_ref, dgr_ref, dgg_ref, dba_ref, dwa_ref = outs
        (ds_ref,) = scr
        i = pl.program_id(0)

        @pl.when(i == 0)
        def _():
            ds_ref[...] = jnp.zeros_like(ds_ref)
            dgr_ref[...] = jnp.zeros_like(dgr_ref)
            dgg_ref[...] = jnp.zeros_like(dgg_ref)
            dba_ref[...] = jnp.zeros_like(dba_ref)
            dwa_ref[...] = jnp.zeros_like(dwa_ref)

        pr = pr_ref
        cos = cos_ref[...]
        sin_signed = sin_ref[...]
        row, col, same = _attn_masks()
        m1 = col <= row
        m2 = jnp.logical_and(col > row, same)
        m1t = row <= col
        m2t = jnp.logical_and(row > col, same)
        lane = lax.broadcasted_iota(jnp.int32, (1, QK_W), 1)
        blockmask = _state_block_mask()
        for grp in range(2):
            q, k, v, gate, b_cum, glow, logit = _group_inputs(grp, pr, cos, sin_signed, lg_ref[...],
                                                              wa2_ref[...], ba_ref[...])
            f = _decay_factors(q, k, b_cum)
            gn = gr_ref[...] if grp == 0 else gg_ref[...]
            dgn_ref = dgr_ref if grp == 0 else dgg_ref
            do_parts, dgate_parts, dgn_parts = [], [], []
            for h in range(HEADS):
                lo = grp * V_W + h * DV
                o_h = o_ref[:, lo:lo + DV]
                r = lax.rsqrt(jnp.mean(o_h * o_h, axis=-1, keepdims=True) + RMS_EPS)
                n = o_h * r
                gte = gate[:, h * DV:(h + 1) * DV]
                sg = _sigmoid(gte)
                dy_h = dy_ref[:, lo:lo + DV]
                gn_h = gn[:, h * DV:(h + 1) * DV]
                dgate_parts.append(dy_h * n * gn_h * (sg * (1.0 + gte * (1.0 - sg))))
                dz = dy_h * (gte * sg)
                dgn_parts.append(jnp.sum(dz * n, axis=0, keepdims=True))
                dn = dz * gn_h
                do_parts.append(r * (dn - n * jnp.mean(dn * n, axis=-1, keepdims=True)))
            dgn_ref[...] += jnp.concatenate(dgn_parts, axis=1)
            dgate = jnp.concatenate(dgate_parts, axis=1)
            do = jnp.concatenate(do_parts, axis=1)
            dob = do.astype(BF16)
            vb = v.astype(BF16)
            s_prev = st_ref[0, grp]
            ds_new = ds_ref[grp]
            dsb = ds_new.astype(BF16)
            qpb, qmb = f["qp"].astype(BF16), f["qm"].astype(BF16)
            kpb, kmb = f["kp"].astype(BF16), f["km"].astype(BF16)
            dqp = jnp.zeros((SUPER, QK_W), F32)
            dqm = jnp.zeros((SUPER, QK_W), F32)
            dkp = jnp.zeros((SUPER, QK_W), F32)
            dkm = jnp.zeros((SUPER, QK_W), F32)
            dv_parts = []
            for h in range(HEADS):
                hm = (lane // DK) == h
                qp_h = jnp.where(hm, f["qp"], 0.0).astype(BF16)
                qm_h = jnp.where(hm, f["qm"], 0.0).astype(BF16)
                kp_h = jnp.where(hm, f["kp"], 0.0).astype(BF16)
                km_h = jnp.where(hm, f["km"], 0.0).astype(BF16)
                at = jnp.where(m1t, _dot(km_h, qpb, "nt"), jnp.where(m2t, _dot(kp_h, qmb, "nt"), 0.0))
                do_h = dob[:, h * DV:(h + 1) * DV]
                v_h = vb[:, h * DV:(h + 1) * DV]
                dv_parts.append(_dot(at.astype(BF16), do_h, "nn"))
                da = _dot(do_h, v_h, "nt")
                dat = _dot(v_h, do_h, "nt")
                da1 = jnp.where(m1, da, 0.0).astype(BF16)
                da2 = jnp.where(m2, da, 0.0).astype(BF16)
                da1t = jnp.where(m1t, dat, 0.0).astype(BF16)
                da2t = jnp.where(m2t, dat, 0.0).astype(BF16)
                dqp = dqp + _dot(da1, km_h, "nn")
                dqm = dqm + _dot(da2, kp_h, "nn")
                dkm = dkm + _dot(da1t, qp_h, "nn")
                dkp = dkp + _dot(da2t, qm_h, "nn")
            klb = f["kl"].astype(BF16)
            qsb = f["qs"].astype(BF16)
            dqs = _dot(dob, s_prev.astype(BF16), "nn")
            dkl = _dot(vb, dsb, "nn")
            dv = jnp.concatenate(dv_parts, axis=1) + _dot(klb, dsb, "nt")
            ds_ref[grp] = ds_new * f["ebl"] + jnp.where(blockmask, _dot(dob, qsb, "tn"), 0.0)
            dq = dqp * f["e1"] + dqm * f["e2"] + dqs * f["eb"]
            dk = dkm * f["e2"] + dkp * f["e1"] + dkl * f["el"]
            if grp == 0:
                dq = _rotary_transposed(dq, cos, sin_signed)
                dk = _rotary_transposed(dk * (DK ** -0.5), cos, sin_signed)
                dp_ref[:, C_RQ:C_RQ + QK_W] = dq.astype(dp_ref.dtype)
                dp_ref[:, C_RK:C_RK + QK_W] = dk.astype(dp_ref.dtype)
                dp_ref[:, C_RV:C_RV + V_W] = dv.astype(dp_ref.dtype)
                dp_ref[:, C_RG:C_RG + V_W] = dgate.astype(dp_ref.dtype)
            else:
                dkl_kl = dkl * klb.astype(F32)
                db = (dqp * qpb.astype(F32) - dkm * kmb.astype(F32) - dqm * qmb.astype(F32)
                      + dkp * kpb.astype(F32) + dqs * qsb.astype(F32) - dkl_kl)
                last = (jnp.sum(dkl_kl, axis=0, keepdims=True)
                        + f["ebl"] * jnp.sum(s_prev * ds_new, axis=0, keepdims=True))
                rowq = lax.broadcasted_iota(jnp.int32, (SUPER, QK_W), 0)
                db = db + jnp.where(rowq == SUPER - 1, last, 0.0)
                upper = (col >= row).astype(F32)
                dla = _dot(upper, db, "nn", precision=lax.Precision.HIGHEST)
                dlogit = dla * (1.0 / GATE_NORM) * (1.0 - _sigmoid(logit))
                dlb = dlogit.astype(BF16)
                dglow = _dot(dlb, wa2_ref[...].astype(BF16), "nt")
                dwa_ref[...] += _dot(glow.astype(BF16), dlb, "tn")
                dba_ref[...] += jnp.sum(dlogit, axis=0, keepdims=True)
                dp_ref[:, C_GQ:C_GQ + QK_W] = (dq * (DK ** -0.5)).astype(dp_ref.dtype)
                dp_ref[:, C_GK:C_GK + QK_W] = dk.astype(dp_ref.dtype)
                dp_ref[:, C_GV:C_GV + V_W] = dv.astype(dp_ref.dtype)
                dp_ref[:, C_GG:C_GG + V_W] = dgate.astype(dp_ref.dtype)
                dp_ref[:, C_GL:C_GL + GL_W] = dglow.astype(dp_ref.dtype)

    rev = lambda i: n_s - 1 - i
    const = lambda shape: pl.BlockSpec(shape, lambda i: tuple(0 for _ in shape))
    return _call(
        "attn_bwd", main, (n_s,),
        [pl.BlockSpec((SUPER, PROJ_W), lambda i: (rev(i), 0)),
         pl.BlockSpec((SUPER, 128), lambda i: (rev(i), 0)), pl.BlockSpec((SUPER, 128), lambda i: (rev(i), 0)),
         const((1, QK_W)), const((GL_W, QK_W)), const((1, QK_W)), const((1, V_W)), const((1, V_W)),
         pl.BlockSpec((SUPER, 2 * V_W), lambda i: (rev(i), 0)),
         pl.BlockSpec((SUPER, 2 * V_W), lambda i: (rev(i), 0)),
         pl.BlockSpec((1, 2, V_W, QK_W), lambda i: (rev(i), 0, 0, 0))],
        [pl.BlockSpec((SUPER, PROJ_W), lambda i: (rev(i), 0)),
         const((1, V_W)), const((1, V_W)), const((1, QK_W)), const((GL_W, QK_W))],
        [jax.ShapeDtypeStruct((T, PROJ_W), BF16),
         jax.ShapeDtypeStruct((1, V_W), F32), jax.ShapeDtypeStruct((1, V_W), F32),
         jax.ShapeDtypeStruct((1, QK_W), F32), jax.ShapeDtypeStruct((GL_W, QK_W), F32)],
        (proj, cos, sin_signed, lg, wa2p, ba, gn_ret, gn_gla, o, dy, states),
        scratch=[pltpu.VMEM((2, V_W, QK_W), F32)], comm=comm)


def _rotary_tables(T):
    half = DK // 2
    inv = ROPE_BASE ** (-jnp.arange(half, dtype=F32) * 2.0 / DK)
    ang = jnp.arange(T, dtype=F32)[:, None] * inv[None, :]
    cos, sin = jnp.cos(ang), jnp.sin(ang)
    cos_head = jnp.concatenate([cos, cos], axis=1)
    sin_head = jnp.concatenate([-sin, sin], axis=1)
    return jnp.tile(cos_head, (1, 128 // DK)), jnp.tile(sin_head, (1, 128 // DK))


def _all_gather_small(v):
    m_per, n = v.shape

    def body(x_ref, out_ref, send_sems, recv_sems, local_sem):
        x, y, c = _coords()
        me, sibling = (x, y, c), (x, y, 1 - c)
        chips = [(1 - x, y), (x, 1 - y), (1 - x, 1 - y)]

        def rows(px, py, pc):
            return out_ref.at[pl.ds((4 * px + 2 * py + pc) * m_per, m_per), :]

        def copy(k, block, to, src=None):
            return pltpu.make_async_remote_copy(
                src_ref=rows(*block) if src is None else src, dst_ref=rows(*block),
                send_sem=send_sems.at[k], recv_sem=recv_sems.at[k], device_id=to, device_id_type=MESH)

        mine = pltpu.make_async_copy(x_ref, rows(*me), local_sem)
        mine.start()
        first = [copy(0, me, sibling, src=x_ref)]
        first += [copy(1 + j, me, (*chip, c), src=x_ref) for j, chip in enumerate(chips)]
        for cp in first:
            cp.start()
        passed = [copy(4 + j, (*chip, c), sibling) for j, chip in enumerate(chips)]
        for j, chip in enumerate(chips):
            copy(1 + j, (*chip, c), me).wait_recv()
            passed[j].start()
        copy(0, sibling, me).wait_recv()
        for j, chip in enumerate(chips):
            copy(4 + j, (*chip, 1 - c), me).wait_recv()
        for cp in first + passed:
            cp.wait_send()
        mine.wait()

    return pl.pallas_call(
        body, name="ag_small", out_shape=jax.ShapeDtypeStruct((N_DEV * m_per, n), v.dtype),
        in_specs=[pl.BlockSpec(memory_space=pltpu.VMEM)], out_specs=pl.BlockSpec(memory_space=pltpu.VMEM),
        scratch_shapes=[pltpu.SemaphoreType.DMA((7,)), pltpu.SemaphoreType.DMA((7,)), pltpu.SemaphoreType.DMA],
    )(v)


def _sum_devices(name, gathered, m_per):
    def body(g_ref, o_ref):
        acc = g_ref[0:m_per, :]
        for k in range(1, N_DEV):
            acc = acc + g_ref[k * m_per:(k + 1) * m_per, :]
        o_ref[...] = acc

    return pl.pallas_call(body, name=name, out_shape=jax.ShapeDtypeStruct((m_per, 128), F32))(gathered)


def _owner_sum(name, grad, landed, owner):
    _, _, R, C = grad.shape
    n_l = len(landed)

    def body(s_ref, g_ref, *refs):
        o_ref = refs[n_l]
        acc = g_ref[...].astype(F32)
        for l_ref in refs[:n_l]:
            for j in range(l_ref.shape[0]):
                acc = acc + l_ref[j].astype(F32)
        o_ref[...] = acc

    return pl.pallas_call(
        body, name=name, out_shape=jax.ShapeDtypeStruct((R, C), F32),
        grid_spec=pltpu.PrefetchScalarGridSpec(
            num_scalar_prefetch=1, grid=(1,),
            in_specs=[pl.BlockSpec((None, None, R, C), lambda i, s: (s[0], s[1], 0, 0))]
            + [pl.BlockSpec(tuple(l.shape), lambda i, s: (0, 0, 0)) for l in landed],
            out_specs=pl.BlockSpec((R, C), lambda i, s: (0, 0))),
        compiler_params=_cparams(),
    )(owner, grad, *landed)


def _adamw_group(name, items, n_blocks, comm=None):
    n = len(items)

    def main(ins, outs, scr):
        for p in range(n):
            g_ref, w_ref, m_ref, v_ref = ins[4 * p:4 * p + 4]
            d_ref, nm_ref, nv_ref = outs[3 * p:3 * p + 3]
            gv = g_ref[...]
            nm = ADAM_B1 * m_ref[...] + (1.0 - ADAM_B1) * gv
            nv = ADAM_B2 * v_ref[...] + (1.0 - ADAM_B2) * (gv * gv)
            m_hat = nm / (1.0 - ADAM_B1 ** ADAM_STEP)
            v_hat = nv / (1.0 - ADAM_B2 ** ADAM_STEP)
            d_ref[...] = -ADAM_LR * (m_hat / (jnp.sqrt(v_hat) + ADAM_EPS) + ADAM_WD * w_ref[...])
            nm_ref[...] = nm
            nv_ref[...] = nv

    in_specs, out_specs, out_shape, args = [], [], [], []
    for item in items:
        R, C = item[1].shape
        assert R % n_blocks == 0
        spec = pl.BlockSpec((R // n_blocks, C), lambda i: (i, 0))
        in_specs += [spec] * 4
        out_specs += [spec] * 3
        out_shape += [jax.ShapeDtypeStruct((R, C), F32)] * 3
        args += list(item)
    outs, extra = _call(name, main, (n_blocks,), in_specs, out_specs, out_shape, args, comm=comm)
    return [tuple(outs[3 * p:3 * p + 3]) for p in range(n)], extra


SMALL_ORDER = ("ffn1", "mix", "ffn2", "final", "ret", "gla", "b_a")


def kernel(x, ffn1_norm_g, ffn1_w_gate, ffn1_w_up, ffn1_w_down, mix_norm_g, w_in, ret_norm_g, gla_w_a2, gla_b_a, gla_norm_g, w_out, ffn2_norm_g, ffn2_w_gate, ffn2_w_up, ffn2_w_down, final_norm_g, loss_target, m_ffn1_norm_g, m_ffn1_w_gate, m_ffn1_w_up, m_ffn1_w_down, m_mix_norm_g, m_w_in, m_ret_norm_g, m_gla_w_a2, m_gla_b_a, m_gla_norm_g, m_w_out, m_ffn2_norm_g, m_ffn2_w_gate, m_ffn2_w_up, m_ffn2_w_down, m_final_norm_g, v_ffn1_norm_g, v_ffn1_w_gate, v_ffn1_w_up, v_ffn1_w_down, v_mix_norm_g, v_w_in, v_ret_norm_g, v_gla_w_a2, v_gla_b_a, v_gla_norm_g, v_w_out, v_ffn2_norm_g, v_ffn2_w_gate, v_ffn2_w_up, v_ffn2_w_down, v_final_norm_g):
    xi, yi, ci = _coords()
    dev = 4 * xi + 2 * yi + ci
    owner = jnp.stack([2 * xi + yi, ci]).astype(jnp.int32)

    x0, target = x[0], loss_target[0]
    T, D = x0.shape
    fb = ffn1_w_gate.shape[2]
    ib = w_in.shape[2]
    ab = gla_w_a2.shape[2]
    F = N_DEV * fb
    cos, sin_signed = _rotary_tables(T)
    lg = jnp.repeat(jnp.log(1.0 - 2.0 ** (-5.0 - jnp.arange(HEADS, dtype=F32))), DK)[None, :]
    g_final = final_norm_g.reshape(1, D)

    gu1_loc = jnp.stack([ffn1_w_gate[0].T, ffn1_w_up[0].T]).astype(BF16)
    d1_loc = ffn1_w_down.astype(BF16)
    g2_loc = ffn2_w_gate[0].T[None].astype(BF16)
    u2_loc = ffn2_w_up[0].T[None].astype(BF16)
    d2_loc = ffn2_w_down.astype(BF16)
    in_loc = w_in[0].T.astype(BF16)
    out_loc = w_out[0].astype(BF16)

    h1, (gu1,) = _rms_fwd("ffn1_rms", x0, ffn1_norm_g, comm=_AllGather([gu1_loc], ["stack"]))
    gu1 = gu1.reshape(2, F, D)
    (gate1, up1, act1), (d1, in_all) = _mm_nstream(
        "ffn1_up", h1, [gu1, gu1], [0, 1], "nt", [], [BF16, BF16, BF16], _silu_mul_epilogue, cn=256,
        comm=_AllGather([d1_loc, in_loc], ["stack", "plain"]))
    d1 = d1.reshape(1, F, D)
    w_in_t = jnp.pad(in_all.reshape(N_DEV * ib, D), ((0, PROJ_W - N_DEV * ib), (0, 0)))
    f32_tile, bf16_tile, f32_vec = (F32, "tile"), (BF16, "tile"), (F32, "vec")
    (x1, h2), (out_all, a_all) = _mm_mstream(
        "ffn1_down", [act1], [d1], [0], "nn", [(x0, "tile"), (mix_norm_g, "vec")], [f32_tile, bf16_tile],
        _residual_rms_epilogue(0.5), comm=_AllGather([out_loc, gla_w_a2[0]], ["plain", "plain"]))
    w_out_full = out_all.reshape(D, D)
    wa2 = jnp.transpose(a_all, (1, 0, 2)).reshape(GATE_RANK, N_DEV * ab)
    wa2p = jnp.pad(wa2, ((0, GL_W - GATE_RANK), (0, 0)))

    (proj,), (g2,) = _mm_nstream("mix_proj", h2, [w_in_t[None]], [0], "nt", [], [F32], _identity_epilogue, cn=640,
                                 comm=_AllGather([g2_loc], ["stack"]))
    (o, ymix, states), (u2,) = _attn_fwd(proj, cos, sin_signed, lg, wa2p, gla_b_a, ret_norm_g, gla_norm_g,
                                         comm=_AllGather([u2_loc], ["stack"]))
    (x2, h3), _ = _mm_mstream("mix_out", [ymix], [w_out_full], [0], "nn", [(x1, "tile"), (ffn2_norm_g, "vec")],
                              [f32_tile, bf16_tile], _residual_rms_epilogue(1.0))
    g2, u2 = g2.reshape(1, F, D), u2.reshape(1, F, D)

    (gate2, up2, act2), (d2,) = _mm_nstream(
        "ffn2_up", h3, [g2, u2], [0, 0], "nt", [], [BF16, BF16, BF16], _silu_mul_epilogue, cn=256,
        comm=_AllGather([d2_loc], ["stack"]))
    d2 = d2.reshape(1, F, D)
    (x3,), _ = _mm_mstream("ffn2_down", [act2], [d2], [0], "nn", [(x2, "tile")], [f32_tile], _residual_epilogue(0.5))

    dx3, dy3b, d_final, loss = _final_loss_bwd("final", x3, g_final, target, 0.5)

    (dgate2, dup2), _ = _mm_nstream("ffn2b_dact", dy3b, [d2], [0], "nt", [gate2, up2], [BF16, BF16],
                                    _dact_epilogue, cn=256)
    dwg2, _ = _mm_tn("ffn2b_dwg", dgate2, h3, F // 2, D, BF16)
    dwg2 = dwg2.reshape(4, 2, fb, D)
    dwu2, (l_wg2_near,) = _mm_tn("ffn2b_dwu", dup2, h3, F // 2, D, BF16, comm=_ReduceScatter([(dwg2, NEAR)]))
    dwu2 = dwu2.reshape(4, 2, fb, D)
    dwd2, (l_wg2_far,) = _mm_tn("ffn2b_dwd", act2, dy3b, F // 2, D, BF16, comm=_ReduceScatter([(dwg2, FAR)]))
    dwd2 = dwd2.reshape(4, 2, fb, D)
    rms_outs = [f32_tile, bf16_tile, f32_vec]
    (dx2, dx2b, d_g2), (l_wu2,) = _mm_mstream(
        "ffn2b_dh", [dgate2, dup2], [g2, u2], [0, 0], "nn", [(x2, "tile"), (ffn2_norm_g, "vec"), (dx3, "tile")],
        rms_outs, _rms_bwd_epilogue(1.0), comm=_ReduceScatter([(dwu2, ALL)]))

    (dymix,), _ = _mm_mstream("mixb_dy", [dx2b], [w_out_full], [0], "nt", [], [f32_tile], _plain_epilogue)
    dwout, _ = _mm_tn("mixb_dwout", ymix, dx2b, D, D, BF16)
    dwout = dwout.reshape(4, 2, D // N_DEV, D)
    (dproj, d_ret, d_gla, d_ba, d_wa2p), (l_wd2, l_wout) = _attn_bwd(
        proj, cos, sin_signed, lg, wa2p, gla_b_a, ret_norm_g, gla_norm_g, o, dymix, states,
        comm=_ReduceScatter([(dwd2, ALL), (dwout, ALL)]))
    dwin_t, _ = _mm_tn("mixb_dwin", dproj, h2, 640, D, BF16)
    dwin = dwin_t[:N_DEV * ib].reshape(4, 2, ib, D)
    (dx1, dy1b, d_gmix), (l_win_near,) = _mm_mstream(
        "mixb_dh", [dproj], [w_in_t], [0], "nn", [(x1, "tile"), (mix_norm_g, "vec"), (dx2, "tile")],
        rms_outs, _rms_bwd_epilogue(0.5), comm=_ReduceScatter([(dwin, NEAR)]))

    (dgate1, dup1), (l_win_far,) = _mm_nstream("ffn1b_dact", dy1b, [d1], [0], "nt", [gate1, up1], [BF16, BF16],
                                               _dact_epilogue, cn=256, comm=_ReduceScatter([(dwin, FAR)]))
    dwg1, _ = _mm_tn("ffn1b_dwg", dgate1, h1, F // 2, D, BF16)
    dwg1 = dwg1.reshape(4, 2, fb, D)
    dwu1, (l_wg1_near,) = _mm_tn("ffn1b_dwu", dup1, h1, F // 2, D, BF16, comm=_ReduceScatter([(dwg1, NEAR)]))
    dwu1 = dwu1.reshape(4, 2, fb, D)
    (dx0, _, d_g1), (l_wg1_far, l_wu1_near) = _mm_mstream(
        "ffn1b_dh", [dgate1, dup1], [gu1, gu1], [0, 1], "nn", [(x0, "tile"), (ffn1_norm_g, "vec"), (dx1, "tile")],
        rms_outs, _rms_bwd_epilogue(1.0), comm=_ReduceScatter([(dwg1, FAR), (dwu1, NEAR)]))
    dwd1, (l_wu1_far,) = _mm_tn("ffn1b_dwd", act1, dy1b, F // 2, D, BF16, comm=_ReduceScatter([(dwu1, FAR)]))
    dwd1 = dwd1.reshape(4, 2, fb, D)

    def adam_item(nm, grad, weight, mom, var):
        shape = weight.shape
        two_d = (1, shape[0]) if len(shape) == 1 else shape[-2:]
        return tuple(a.reshape(two_d) for a in (grad, weight, mom, var))

    sum_a = {
        "ffn2_w_gate": _owner_sum("sum_ffn2_wg", dwg2, [l_wg2_near, l_wg2_far], owner).T,
        "ffn2_w_up": _owner_sum("sum_ffn2_wu", dwu2, [l_wu2], owner).T,
        "ffn2_w_down": _owner_sum("sum_ffn2_wd", dwd2, [l_wd2], owner),
        "w_in": _owner_sum("sum_w_in", dwin, [l_win_near, l_win_far], owner).T,
        "w_out": _owner_sum("sum_w_out", dwout, [l_wout], owner),
    }
    params = dict(
        ffn2_w_gate=(ffn2_w_gate, m_ffn2_w_gate, v_ffn2_w_gate), ffn2_w_up=(ffn2_w_up, m_ffn2_w_up, v_ffn2_w_up),
        ffn2_w_down=(ffn2_w_down, m_ffn2_w_down, v_ffn2_w_down), w_in=(w_in, m_w_in, v_w_in),
        w_out=(w_out, m_w_out, v_w_out), ffn1_w_gate=(ffn1_w_gate, m_ffn1_w_gate, v_ffn1_w_gate),
        ffn1_w_up=(ffn1_w_up, m_ffn1_w_up, v_ffn1_w_up), ffn1_w_down=(ffn1_w_down, m_ffn1_w_down, v_ffn1_w_down),
        ffn1_norm_g=(ffn1_norm_g, m_ffn1_norm_g, v_ffn1_norm_g), mix_norm_g=(mix_norm_g, m_mix_norm_g, v_mix_norm_g),
        ret_norm_g=(ret_norm_g, m_ret_norm_g, v_ret_norm_g), gla_w_a2=(gla_w_a2, m_gla_w_a2, v_gla_w_a2),
        gla_b_a=(gla_b_a, m_gla_b_a, v_gla_b_a), gla_norm_g=(gla_norm_g, m_gla_norm_g, v_gla_norm_g),
        ffn2_norm_g=(ffn2_norm_g, m_ffn2_norm_g, v_ffn2_norm_g), final_norm_g=(final_norm_g, m_final_norm_g, v_final_norm_g))
    grads, updates = {}, {}

    def run_adam(name, names, grad_of, n_blocks, comm=None):
        for nm in names:
            grads[nm] = grad_of[nm].reshape(params[nm][0].shape)
        res, extra = _adamw_group(name, [adam_item(nm, grads[nm], *params[nm]) for nm in names], n_blocks, comm=comm)
        for nm, r in zip(names, res):
            updates[nm] = tuple(a.reshape(params[nm][0].shape) for a in r)
        return extra

    (l_wd1_near,) = run_adam("adamw_a", list(sum_a), sum_a, 4, comm=_ReduceScatter([(dwd1, NEAR)]))
    sum_b = {
        "ffn1_w_gate": _owner_sum("sum_ffn1_wg", dwg1, [l_wg1_near, l_wg1_far], owner).T,
        "ffn1_w_up": _owner_sum("sum_ffn1_wu", dwu1, [l_wu1_near, l_wu1_far], owner).T,
    }
    (l_wd1_far,) = run_adam("adamw_b", list(sum_b), sum_b, 4, comm=_ReduceScatter([(dwd1, FAR)]))
    sum_c = {"ffn1_w_down": _owner_sum("sum_ffn1_wd", dwd1, [l_wd1_near, l_wd1_far], owner)}
    run_adam("adamw_c", list(sum_c), sum_c, 4)

    small = dict(ffn1=d_g1, mix=d_gmix, ffn2=d_g2, final=d_final, ret=d_ret, gla=d_gla, b_a=d_ba)
    flat = jnp.concatenate([small[k].reshape(-1) for k in SMALL_ORDER]
                           + [d_wa2p[:GATE_RANK].reshape(-1), loss[0]])
    rows = -(-flat.shape[0] // 128)
    rows = -(-rows // 8) * 8
    packed = jnp.pad(flat, (0, rows * 128 - flat.shape[0])).reshape(rows, 128)
    total = _sum_devices("sum_small", _all_gather_small(packed), rows).reshape(-1)
    sizes = [small[k].size for k in SMALL_ORDER] + [GATE_RANK * QK_W, 128]
    offs = [0]
    for s in sizes:
        offs.append(offs[-1] + s)
    pieces = [total[offs[i]:offs[i + 1]] for i in range(len(sizes))]
    g_small = {k: pieces[i].reshape(small[k].shape) for i, k in enumerate(SMALL_ORDER)}
    g_wa2_full = pieces[len(SMALL_ORDER)].reshape(GATE_RANK, QK_W)
    g_wa2 = lax.dynamic_slice(g_wa2_full, (0, dev * ab), (GATE_RANK, ab))
    loss_total = pieces[len(SMALL_ORDER) + 1][0]

    small_grads = {"ffn1_norm_g": g_small["ffn1"], "mix_norm_g": g_small["mix"], "ret_norm_g": g_small["ret"],
                   "gla_w_a2": g_wa2, "gla_b_a": g_small["b_a"], "gla_norm_g": g_small["gla"],
                   "ffn2_norm_g": g_small["ffn2"], "final_norm_g": g_small["final"]}
    run_adam("adamw_small", list(small_grads), small_grads, 1)

    order = ("ffn1_norm_g", "ffn1_w_gate", "ffn1_w_up", "ffn1_w_down", "mix_norm_g", "w_in", "ret_norm_g", "gla_w_a2",
             "gla_b_a", "gla_norm_g", "w_out", "ffn2_norm_g", "ffn2_w_gate", "ffn2_w_up", "ffn2_w_down", "final_norm_g")
    return (loss_total, dx0[None], *[grads[nm] for nm in order], *[updates[nm][0] for nm in order],
            *[updates[nm][1] for nm in order], *[updates[nm][2] for nm in order])
```

```python
import functools
import math

import jax
import jax.numpy as jnp
from jax import lax
from jax.experimental import pallas as pl
from jax.experimental.pallas import tpu as pltpu

F32 = jnp.float32
BF16 = jnp.bfloat16
MESH = pl.DeviceIdType.MESH
HBM = pl.BlockSpec(memory_space=pltpu.HBM)

N_DEV = 8
RMS_EPS = 1e-6
ROPE_BASE = 10000.0
HEADS = 4
DK = 64
DV = 128
QK_W = HEADS * DK
V_W = HEADS * DV
GATE_RANK = 16
GATE_NORM = 16.0
CHUNK = 64
SUPER = 256
PROJ_W = 3200
C_RQ, C_RK, C_RV, C_RG, C_GQ, C_GK, C_GV, C_GG, C_GL = 0, 256, 512, 1024, 1536, 1792, 2048, 2560, 3072
GL_W = PROJ_W - C_GL
ADAM_LR, ADAM_B1, ADAM_B2, ADAM_EPS, ADAM_WD, ADAM_STEP = 0.001, 0.9, 0.999, 1e-08, 0.01, 10
VMEM_LIMIT_V7X = 52 * 1024 * 1024


def _cparams(**kw):
    return pltpu.CompilerParams(vmem_limit_bytes=VMEM_LIMIT_V7X, **kw)


def _dot(a, b, form, precision=None):
    dims = {"nn": (((1,), (0,)), ((), ())), "nt": (((1,), (1,)), ((), ())), "tn": (((0,), (0,)), ((), ()))}[form]
    return lax.dot_general(a, b, dims, preferred_element_type=F32, precision=precision)


def _sigmoid(x):
    return 1.0 / (1.0 + jnp.exp(-x))


def _coords():
    return lax.axis_index("x"), lax.axis_index("y"), lax.axis_index("c")


class _NoComm:
    inputs, out_shapes, scratch = (), (), ()


class _AllGather:
    def __init__(self, arrays, kinds):
        self.inputs = tuple(arrays)
        self.kinds = tuple(kinds)
        n = len(arrays)
        self.out_shapes = tuple(
            jax.ShapeDtypeStruct((a.shape[0], N_DEV) + a.shape[1:] if k == "stack" else (N_DEV,) + a.shape, a.dtype)
            for a, k in zip(arrays, kinds))
        self.scratch = (pltpu.SemaphoreType.DMA((n, 7)), pltpu.SemaphoreType.DMA((n, 7)),
                        pltpu.SemaphoreType.DMA((n,)))

    def _ctx(self, srcs, outs, sems):
        send_sems, recv_sems, local_sems = sems
        x, y, c = _coords()
        me, sibling = (x, y, c), (x, y, 1 - c)
        chips = [(1 - x, y), (x, 1 - y), (1 - x, 1 - y)]

        def blk(m, dev):
            k = 4 * dev[0] + 2 * dev[1] + dev[2]
            return outs[m].at[:, k] if self.kinds[m] == "stack" else outs[m].at[k]

        def copy(m, s, block, to, src=None):
            return pltpu.make_async_remote_copy(
                src_ref=blk(m, block) if src is None else src, dst_ref=blk(m, block),
                send_sem=send_sems.at[m, s], recv_sem=recv_sems.at[m, s], device_id=to, device_id_type=MESH)

        def mine(m):
            return pltpu.make_async_copy(srcs[m], blk(m, me), local_sems.at[m])

        def first(m):
            return [copy(m, 0, me, sibling, src=srcs[m])] + [
                copy(m, 1 + j, me, (*chip, c), src=srcs[m]) for j, chip in enumerate(chips)]

        return me, sibling, chips, c, copy, mine, first

    def start(self, srcs, outs, sems):
        me, sibling, chips, c, copy, mine, first = self._ctx(srcs, outs, sems)
        for m in range(len(srcs)):
            mine(m).start()
            for cp in first(m):
                cp.start()

    def mid(self, srcs, outs, sems):
        me, sibling, chips, c, copy, mine, first = self._ctx(srcs, outs, sems)
        for j, chip in enumerate(chips):
            for m in range(len(srcs)):
                copy(m, 1 + j, (*chip, c), me).wait_recv()
                copy(m, 4 + j, (*chip, c), sibling).start()

    def finish(self, srcs, outs, sems):
        me, sibling, chips, c, copy, mine, first = self._ctx(srcs, outs, sems)
        for m in range(len(srcs)):
            copy(m, 0, sibling, me).wait_recv()
            for j, chip in enumerate(chips):
                copy(m, 4 + j, (*chip, 1 - c), me).wait_recv()
            for cp in first(m):
                cp.wait_send()
            for j, chip in enumerate(chips):
                copy(m, 4 + j, (*chip, c), sibling).wait_send()
            mine(m).wait()


RELATIONS = ((0, 0, 1), (1, 0, 0), (0, 1, 0), (1, 1, 0), (1, 0, 1), (0, 1, 1), (1, 1, 1))
NEAR = (0, 1, 2, 4, 5)
FAR = (3, 6)
ALL = NEAR + FAR


class _ReduceScatter:
    def __init__(self, parts):
        self.inputs = tuple(g for g, _ in parts)
        self.slots = tuple(s for _, s in parts)
        self.out_shapes = tuple(jax.ShapeDtypeStruct((len(s),) + g.shape[2:], g.dtype) for g, s in parts)
        n_max = max(len(s) for s in self.slots)
        n = len(parts)
        self.scratch = (pltpu.SemaphoreType.DMA((n, n_max)), pltpu.SemaphoreType.DMA((n, n_max)))

    def _copies(self, srcs, outs, sems):
        send_sems, recv_sems = sems
        x, y, c = _coords()
        copies = []
        for m, slots in enumerate(self.slots):
            for i, s in enumerate(slots):
                fx, fy, fc = RELATIONS[s]
                px = 1 - x if fx else x
                py = 1 - y if fy else y
                pc = 1 - c if fc else c
                copies.append(pltpu.make_async_remote_copy(
                    src_ref=srcs[m].at[2 * px + py, pc], dst_ref=outs[m].at[i], send_sem=send_sems.at[m, i],
                    recv_sem=recv_sems.at[m, i], device_id=(px, py, pc), device_id_type=MESH))
        return copies

    def start(self, srcs, outs, sems):
        for cp in self._copies(srcs, outs, sems):
            cp.start()

    def mid(self, srcs, outs, sems):
        pass

    def finish(self, srcs, outs, sems):
        for cp in self._copies(srcs, outs, sems):
            cp.wait()


class _Multi:
    def __init__(self, comms):
        self.comms = comms
        self.inputs = tuple(a for c in comms for a in c.inputs)
        self.out_shapes = tuple(s for c in comms for s in c.out_shapes)
        self.scratch = tuple(s for c in comms for s in c.scratch)

    def _each(self, phase, srcs, outs, sems):
        i = o = k = 0
        for c in self.comms:
            ni, no, nk = len(c.inputs), len(c.out_shapes), len(c.scratch)
            getattr(c, phase)(srcs[i:i + ni], outs[o:o + no], sems[k:k + nk])
            i, o, k = i + ni, o + no, k + nk

    def start(self, srcs, outs, sems):
        self._each("start", srcs, outs, sems)

    def mid(self, srcs, outs, sems):
        self._each("mid", srcs, outs, sems)

    def finish(self, srcs, outs, sems):
        self._each("finish", srcs, outs, sems)


def _call(name, main, grid, in_specs, out_specs, out_shape, args, scratch=(), comm=None, prefetch=None):
    comm = comm or _NoComm()
    counts = [len(in_specs), len(comm.inputs), len(out_shape), len(comm.out_shapes), len(scratch), len(comm.scratch)]
    n_steps = math.prod(grid)
    hosted = bool(comm.inputs)

    def body(*refs):
        if prefetch is not None:
            refs = refs[1:]
        parts, at = [], 0
        for n in counts:
            parts.append(refs[at:at + n])
            at += n
        ins, c_in, outs, c_out, scr, c_scr = parts
        step = pl.program_id(0)
        for d in range(1, len(grid)):
            step = step * grid[d] + pl.program_id(d)
        if hosted:
            @pl.when(step == 0)
            def _():
                comm.start(c_in, c_out, c_scr)
        main(ins, outs, scr)
        if hosted:
            @pl.when(step == max(n_steps - 2, 0))
            def _():
                comm.mid(c_in, c_out, c_scr)

            @pl.when(step == n_steps - 1)
            def _():
                comm.finish(c_in, c_out, c_scr)

    all_in = list(in_specs) + [HBM] * counts[1]
    all_out = list(out_specs) + [HBM] * counts[3]
    all_scratch = list(scratch) + list(comm.scratch)
    shapes = list(out_shape) + list(comm.out_shapes)
    if prefetch is None:
        res = pl.pallas_call(body, name=name, grid=grid, in_specs=all_in, out_specs=all_out, out_shape=shapes,
                             scratch_shapes=all_scratch, compiler_params=_cparams())(*args, *comm.inputs)
    else:
        res = pl.pallas_call(
            body, name=name, out_shape=shapes,
            grid_spec=pltpu.PrefetchScalarGridSpec(num_scalar_prefetch=1, grid=grid, in_specs=all_in,
                                                   out_specs=all_out, scratch_shapes=all_scratch),
            compiler_params=_cparams())(prefetch, *args, *comm.inputs)
    return res[:counts[2]], res[counts[2]:]


def _rms_fwd(name, x, g, comm=None):
    T, D = x.shape
    tm = min(T, 512)

    def main(ins, outs, scr):
        x_ref, g_ref = ins
        xv = x_ref[...]
        r = lax.rsqrt(jnp.mean(xv * xv, axis=-1, keepdims=True) + RMS_EPS)
        outs[0][...] = (xv * r * g_ref[...]).astype(outs[0].dtype)

    tile = pl.BlockSpec((tm, D), lambda i: (i, 0))
    (h,), extra = _call(name, main, (T // tm,), [tile, pl.BlockSpec((1, D), lambda i: (0, 0))], [tile],
                        [jax.ShapeDtypeStruct((T, D), BF16)], (x, g), comm=comm)
    return h, extra


def _final_loss_bwd(name, x, g, target, out_scale):
    T, D = x.shape
    tm = min(T, 512)

    def main(ins, outs, scr):
        x_ref, g_ref, t_ref = ins
        dx_ref, dxb_ref, dg_ref, loss_ref = outs
        i = pl.program_id(0)
        xv = x_ref[...]
        r = lax.rsqrt(jnp.mean(xv * xv, axis=-1, keepdims=True) + RMS_EPS)
        xhat = xv * r
        err = xhat * g_ref[...] - t_ref[...]

        @pl.when(i == 0)
        def _():
            dg_ref[...] = jnp.zeros_like(dg_ref)
            loss_ref[...] = jnp.zeros_like(loss_ref)

        loss_ref[...] += jnp.broadcast_to(jnp.sum(err * err) * (0.5 / D), loss_ref.shape)
        dy = err * (1.0 / D)
        dg_ref[...] += jnp.sum(dy * xhat, axis=0, keepdims=True)
        dxhat = dy * g_ref[...]
        dx = r * (dxhat - xhat * jnp.mean(dxhat * xhat, axis=-1, keepdims=True))
        dx_ref[...] = dx
        dxb_ref[...] = (out_scale * dx).astype(dxb_ref.dtype)

    tile = pl.BlockSpec((tm, D), lambda i: (i, 0))
    vec = pl.BlockSpec((1, D), lambda i: (0, 0))
    outs, _ = _call(name, main, (T // tm,), [tile, vec, tile],
                    [tile, tile, vec, pl.BlockSpec((1, 128), lambda i: (0, 0))],
                    [jax.ShapeDtypeStruct((T, D), F32), jax.ShapeDtypeStruct((T, D), BF16),
                     jax.ShapeDtypeStruct((1, D), F32), jax.ShapeDtypeStruct((1, 128), F32)], (x, g, target))
    return outs


def _mm_nstream(name, a, ws, w_sel, w_form, comps, out_dtypes, epilogue, cn, rows=1024, comm=None):
    T, K = a.shape
    N = ws[0].shape[1]
    rows = min(rows, T)
    assert N % cn == 0 and T % rows == 0
    n_w, n_c = len(ws), len(comps)

    def main(ins, outs, scr):
        a_ref = ins[0]
        w_refs = ins[1:1 + n_w]
        c_refs = ins[1 + n_w:]

        for r in range(T // rows):
            sl = slice(r * rows, (r + 1) * rows)
            a_blk = a_ref[sl, :]
            dots = [_dot(a_blk, w_ref[...], w_form) for w_ref in w_refs]
            res = epilogue(dots, [c_ref[sl, :] for c_ref in c_refs])
            for o_ref, o in zip(outs, res):
                o_ref[sl, :] = o.astype(o_ref.dtype)

    if w_form == "nt":
        w_specs = [pl.BlockSpec((None, cn, K), functools.partial(lambda j, s: (s, j, 0), s=s)) for s in w_sel]
    else:
        w_specs = [pl.BlockSpec((K, cn), lambda j: (0, j)) for _ in ws]
    chunk = pl.BlockSpec((T, cn), lambda j: (0, j))
    return _call(name, main, (N // cn,), [pl.BlockSpec((T, K), lambda j: (0, 0))] + w_specs + [chunk] * n_c,
                 [chunk] * len(out_dtypes), [jax.ShapeDtypeStruct((T, N), dt) for dt in out_dtypes],
                 (a, *ws, *comps), comm=comm)


def _mm_mstream(name, as_, ws, w_sel, w_form, extras, outs_desc, epilogue, tm=512, comm=None):
    T = as_[0].shape[0]
    tm = min(tm, T)
    n_a = len(as_)
    w_shapes = [w.shape[-2:] for w in ws]
    N = w_shapes[0][1] if w_form == "nn" else w_shapes[0][0]

    def main(ins, outs, scr):
        a_refs = ins[:n_a]
        w_refs = ins[n_a:2 * n_a]
        acc = None
        for a_ref, w_ref in zip(a_refs, w_refs):
            d = _dot(a_ref[...], w_ref[...], w_form)
            acc = d if acc is None else acc + d
        epilogue(acc, ins[2 * n_a:], outs)

    kind_spec = {"tile": pl.BlockSpec((tm, N), lambda i: (i, 0)), "vec": pl.BlockSpec((1, N), lambda i: (0, 0))}
    kind_shape = {"tile": (T, N), "vec": (1, N)}
    a_specs = [pl.BlockSpec((tm, a.shape[1]), lambda i: (i, 0)) for a in as_]
    w_specs = []
    for w, s in zip(ws, w_sel):
        if w.ndim == 3:
            w_specs.append(pl.BlockSpec((None,) + tuple(w.shape[1:]), functools.partial(lambda i, s: (s, 0, 0), s=s),
                                        pipeline_mode=pl.Buffered(1)))
        else:
            w_specs.append(pl.BlockSpec(tuple(w.shape), lambda i: (0, 0), pipeline_mode=pl.Buffered(1)))
    args = list(as_) + list(ws) + [e for e, _ in extras]
    return _call(name, main, (T // tm,), a_specs + w_specs + [kind_spec[k] for _, k in extras],
                 [kind_spec[k] for _, k in outs_desc],
                 [jax.ShapeDtypeStruct(kind_shape[k], dt) for dt, k in outs_desc], args, comm=comm)


def _plain_epilogue(acc, ex, outs):
    outs[0][...] = acc.astype(outs[0].dtype)


def _residual_epilogue(scale):
    def ep(acc, ex, outs):
        outs[0][...] = ex[0][...] + scale * acc
    return ep


def _residual_rms_epilogue(scale):
    def ep(acc, ex, outs):
        xv = ex[0][...] + scale * acc
        outs[0][...] = xv
        r = lax.rsqrt(jnp.mean(xv * xv, axis=-1, keepdims=True) + RMS_EPS)
        outs[1][...] = (xv * r * ex[1][...]).astype(outs[1].dtype)
    return ep


def _rms_bwd_epilogue(out_scale):
    def ep(acc, ex, outs):
        x_ref, g_ref, dres_ref = ex
        dx_ref, dxb_ref, dg_ref = outs
        xv = x_ref[...]
        r = lax.rsqrt(jnp.mean(xv * xv, axis=-1, keepdims=True) + RMS_EPS)
        xhat = xv * r

        @pl.when(pl.program_id(0) == 0)
        def _():
            dg_ref[...] = jnp.zeros_like(dg_ref)

        dg_ref[...] += jnp.sum(acc * xhat, axis=0, keepdims=True)
        dxhat = acc * g_ref[...]
        dx = r * (dxhat - xhat * jnp.mean(dxhat * xhat, axis=-1, keepdims=True)) + dres_ref[...]
        dx_ref[...] = dx
        dxb_ref[...] = (out_scale * dx).astype(dxb_ref.dtype)
    return ep


def _mm_tn(name, a, b, tmo, tno, out_dtype, tk=1024, comm=None):
    T, Ma = a.shape
    Nb = b.shape[1]
    tk = min(tk, T)
    nk = T // tk

    def main(ins, outs, scr):
        a_ref, b_ref = ins
        (acc_ref,) = scr
        k = pl.program_id(2)

        @pl.when(k == 0)
        def _():
            acc_ref[...] = jnp.zeros_like(acc_ref)

        acc_ref[...] += _dot(a_ref[...], b_ref[...], "tn")

        @pl.when(k == nk - 1)
        def _():
            outs[0][...] = acc_ref[...].astype(outs[0].dtype)

    (out,), extra = _call(
        name, main, (Ma // tmo, Nb // tno, nk),
        [pl.BlockSpec((tk, tmo), lambda i, j, k: (k, i)), pl.BlockSpec((tk, tno), lambda i, j, k: (k, j))],
        [pl.BlockSpec((tmo, tno), lambda i, j, k: (i, j))], [jax.ShapeDtypeStruct((Ma, Nb), out_dtype)],
        (a, b), scratch=[pltpu.VMEM((tmo, tno), F32)], comm=comm)
    return out, extra


def _silu_mul_epilogue(dots, comps):
    g, u = dots
    act = g * _sigmoid(g) * u
    return [g, u, act]


def _dact_epilogue(dots, comps):
    (dact,) = dots
    g = comps[0].astype(F32)
    u = comps[1].astype(F32)
    s = _sigmoid(g)
    silu = g * s
    dsilu = s + silu * (1.0 - s)
    return [dact * u * dsilu, dact * silu]


def _identity_epilogue(dots, comps):
    return list(dots)


def _swap_halves(x):
    lane = lax.broadcasted_iota(jnp.int32, x.shape, 1)
    first = (lane % DK) < (DK // 2)
    return jnp.where(first, pltpu.roll(x, 128 - DK // 2, 1), pltpu.roll(x, DK // 2, 1))


def _rotary(t, cos, sin_signed):
    halves = []
    for p in range(QK_W // 128):
        th = t[:, 128 * p:128 * (p + 1)]
        halves.append(th * cos + _swap_halves(th) * sin_signed)
    return jnp.concatenate(halves, axis=1)


def _rotary_transposed(d, cos, sin_signed):
    halves = []
    for p in range(QK_W // 128):
        dh = d[:, 128 * p:128 * (p + 1)]
        halves.append(dh * cos + _swap_halves(dh * sin_signed))
    return jnp.concatenate(halves, axis=1)


def _log_sigmoid(x):
    return jnp.minimum(x, 0.0) - jnp.log(1.0 + jnp.exp(-jnp.abs(x)))


def _attn_masks():
    row = lax.broadcasted_iota(jnp.int32, (SUPER, SUPER), 0)
    col = lax.broadcasted_iota(jnp.int32, (SUPER, SUPER), 1)
    same = (row // CHUNK) == (col // CHUNK)
    return row, col, same


def _group_inputs(grp, pr, cos, sin_signed, lg, wa2, ba):
    if grp == 0:
        q = _rotary(pr[:, C_RQ:C_RQ + QK_W], cos, sin_signed)
        k = _rotary(pr[:, C_RK:C_RK + QK_W], cos, sin_signed) * (DK ** -0.5)
        v = pr[:, C_RV:C_RV + V_W]
        gate = pr[:, C_RG:C_RG + V_W]
        pos = lax.broadcasted_iota(jnp.int32, (SUPER, QK_W), 0).astype(F32) + 1.0
        return q, k, v, gate, pos * lg, None, None
    q = pr[:, C_GQ:C_GQ + QK_W] * (DK ** -0.5)
    k = pr[:, C_GK:C_GK + QK_W]
    v = pr[:, C_GV:C_GV + V_W]
    gate = pr[:, C_GG:C_GG + V_W]
    glow = pr[:, C_GL:C_GL + GL_W]
    logit = _dot(glow.astype(BF16), wa2.astype(BF16), "nn") + ba
    la = _log_sigmoid(logit) * (1.0 / GATE_NORM)
    row, col, _ = _attn_masks()
    lower = (col <= row).astype(F32)
    b_cum = _dot(lower, la, "nn", precision=lax.Precision.HIGHEST)
    return q, k, v, gate, b_cum, glow, logit


def _decay_factors(q, k, b_cum):
    c = b_cum[SUPER // 2 - 1:SUPER // 2, :]
    bl = b_cum[SUPER - 1:SUPER, :]
    e1 = jnp.exp(b_cum - c)
    e2 = jnp.exp(c - b_cum)
    e_b = jnp.exp(b_cum)
    e_l = jnp.exp(bl - b_cum)
    return dict(e1=e1, e2=e2, eb=e_b, el=e_l, ebl=jnp.exp(bl),
                qp=q * e1, qm=q * e2, kp=k * e1, km=k * e2, qs=q * e_b, kl=k * e_l)


def _state_block_mask():
    r = lax.broadcasted_iota(jnp.int32, (V_W, QK_W), 0)
    c = lax.broadcasted_iota(jnp.int32, (V_W, QK_W), 1)
    return (r // DV) == (c // DK)


def _attn_fwd(proj, cos, sin_signed, lg, wa2p, ba, gn_ret, gn_gla, comm=None):
    T = proj.shape[0]
    n_s = T // SUPER

    def main(ins, outs, scr):
        pr_ref, cos_ref, sin_ref, lg_ref, wa2_ref, ba_ref, gr_ref, gg_ref = ins
        o_ref, y_ref, st_ref = outs
        (s_ref,) = scr
        i = pl.program_id(0)

        @pl.when(i == 0)
        def _():
            s_ref[...] = jnp.zeros_like(s_ref)

        pr = pr_ref
        row, col, same = _attn_masks()
        m1 = col <= row
        m2 = jnp.logical_and(col > row, same)
        lane = lax.broadcasted_iota(jnp.int32, (1, QK_W), 1)
        blockmask = _state_block_mask()
        for grp in range(2):
            q, k, v, gate, b_cum, _, _ = _group_inputs(grp, pr, cos_ref[...], sin_ref[...], lg_ref[...],
                                                      wa2_ref[...], ba_ref[...])
            f = _decay_factors(q, k, b_cum)
            gn = gr_ref[...] if grp == 0 else gg_ref[...]
            s_prev = s_ref[grp]
            st_ref[0, grp] = s_prev
            o_inter = _dot(f["qs"].astype(BF16), s_prev.astype(BF16), "nt")
            kmb = f["km"].astype(BF16)
            kpb = f["kp"].astype(BF16)
            vb = v.astype(BF16)
            for h in range(HEADS):
                hm = (lane // DK) == h
                a1 = _dot(jnp.where(hm, f["qp"], 0.0).astype(BF16), kmb, "nt")
                a2 = _dot(jnp.where(hm, f["qm"], 0.0).astype(BF16), kpb, "nt")
                a = jnp.where(m1, a1, jnp.where(m2, a2, 0.0))
                lo = grp * V_W + h * DV
                o_h = _dot(a.astype(BF16), vb[:, h * DV:(h + 1) * DV], "nn") + o_inter[:, h * DV:(h + 1) * DV]
                o_ref[:, lo:lo + DV] = o_h
                r = lax.rsqrt(jnp.mean(o_h * o_h, axis=-1, keepdims=True) + RMS_EPS)
                gte = gate[:, h * DV:(h + 1) * DV]
                y = o_h * r * gn[:, h * DV:(h + 1) * DV] * (gte * _sigmoid(gte))
                y_ref[:, lo:lo + DV] = y.astype(y_ref.dtype)
            upd = _dot(vb, f["kl"].astype(BF16), "tn")
            s_ref[grp] = s_prev * f["ebl"] + jnp.where(blockmask, upd, 0.0)

    const = lambda shape: pl.BlockSpec(shape, lambda i: tuple(0 for _ in shape))
    return _call(
        "attn_fwd", main, (n_s,),
        [pl.BlockSpec((SUPER, PROJ_W), lambda i: (i, 0)),
         pl.BlockSpec((SUPER, 128), lambda i: (i, 0)), pl.BlockSpec((SUPER, 128), lambda i: (i, 0)),
         const((1, QK_W)), const((GL_W, QK_W)), const((1, QK_W)), const((1, V_W)), const((1, V_W))],
        [pl.BlockSpec((SUPER, 2 * V_W), lambda i: (i, 0)), pl.BlockSpec((SUPER, 2 * V_W), lambda i: (i, 0)),
         pl.BlockSpec((1, 2, V_W, QK_W), lambda i: (i, 0, 0, 0))],
        [jax.ShapeDtypeStruct((T, 2 * V_W), F32), jax.ShapeDtypeStruct((T, 2 * V_W), BF16),
         jax.ShapeDtypeStruct((n_s, 2, V_W, QK_W), F32)],
        (proj, cos, sin_signed, lg, wa2p, ba, gn_ret, gn_gla),
        scratch=[pltpu.VMEM((2, V_W, QK_W), F32)], comm=comm)


def _attn_bwd(proj, cos, sin_signed, lg, wa2p, ba, gn_ret, gn_gla, o, dy, states, comm=None):
    T = proj.shape[0]
    n_s = T // SUPER

    def main(ins, outs, scr):
        pr_ref, cos_ref, sin_ref, lg_ref, wa2_ref, ba_ref, gr_ref, gg_ref, o_ref, dy_ref, st_ref = ins
        dp_ref, dgr_ref, dgg_ref, dba_ref, dwa_ref = outs
        (ds_ref,) = scr
        i = pl.program_id(0)

        @pl.when(i == 0)
        def _():
            ds_ref[...] = jnp.zeros_like(ds_ref)
            dgr_ref[...] = jnp.zeros_like(dgr_ref)
            dgg_ref[...] = jnp.zeros_like(dgg_ref)
            dba_ref[...] = jnp.zeros_like(dba_ref)
            dwa_ref[...] = jnp.zeros_like(dwa_ref)

        pr = pr_ref
        cos = cos_ref[...]
        sin_signed = sin_ref[...]
        row, col, same = _attn_masks()
        m1 = col <= row
        m2 = jnp.logical_and(col > row, same)
        m1t = row <= col
        m2t = jnp.logical_and(row > col, same)
        lane = lax.broadcasted_iota(jnp.int32, (1, QK_W), 1)
        blockmask = _state_block_mask()
        for grp in range(2):
            q, k, v, gate, b_cum, glow, logit = _group_inputs(grp, pr, cos, sin_signed, lg_ref[...],
                                                              wa2_ref[...], ba_ref[...])
            f = _decay_factors(q, k, b_cum)
            gn = gr_ref[...] if grp == 0 else gg_ref[...]
            dgn_ref = dgr_ref if grp == 0 else dgg_ref
            do_parts, dgate_parts, dgn_parts = [], [], []
            for h in range(HEADS):
                lo = grp * V_W + h * DV
                o_h = o_ref[:, lo:lo + DV]
                r = lax.rsqrt(jnp.mean(o_h * o_h, axis=-1, keepdims=True) + RMS_EPS)
                n = o_h * r
                gte = gate[:, h * DV:(h + 1) * DV]
                sg = _sigmoid(gte)
                dy_h = dy_ref[:, lo:lo + DV]
                gn_h = gn[:, h * DV:(h + 1) * DV]
                dgate_parts.append(dy_h * n * gn_h * (sg * (1.0 + gte * (1.0 - sg))))
                dz = dy_h * (gte * sg)
                dgn_parts.append(jnp.sum(dz * n, axis=0, keepdims=True))
                dn = dz * gn_h
                do_parts.append(r * (dn - n * jnp.mean(dn * n, axis=-1, keepdims=True)))
            dgn_ref[...] += jnp.concatenate(dgn_parts, axis=1)
            dgate = jnp.concatenate(dgate_parts, axis=1)
            do = jnp.concatenate(do_parts, axis=1)
            dob = do.astype(BF16)
            vb = v.astype(BF16)
            s_prev = st_ref[0, grp]
            ds_new = ds_ref[grp]
            dsb = ds_new.astype(BF16)
            qpb, qmb = f["qp"].astype(BF16), f["qm"].astype(BF16)
            kpb, kmb = f["kp"].astype(BF16), f["km"].astype(BF16)
            dqp = jnp.zeros((SUPER, QK_W), F32)
            dqm = jnp.zeros((SUPER, QK_W), F32)
            dkp = jnp.zeros((SUPER, QK_W), F32)
            dkm = jnp.zeros((SUPER, QK_W), F32)
            dv_parts = []
            for h in range(HEADS):
                hm = (lane // DK) == h
                qp_h = jnp.where(hm, f["qp"], 0.0).astype(BF16)
                qm_h = jnp.where(hm, f["qm"], 0.0).astype(BF16)
                kp_h = jnp.where(hm, f["kp"], 0.0).astype(BF16)
                km_h = jnp.where(hm, f["km"], 0.0).astype(BF16)
                at = jnp.where(m1t, _dot(km_h, qpb, "nt"), jnp.where(m2t, _dot(kp_h, qmb, "nt"), 0.0))
                do_h = dob[:, h * DV:(h + 1) * DV]
                v_h = vb[:, h * DV:(h + 1) * DV]
                dv_parts.append(_dot(at.astype(BF16), do_h, "nn"))
                da = _dot(do_h, v_h, "nt")
                dat = _dot(v_h, do_h, "nt")
                da1 = jnp.where(m1, da, 0.0).astype(BF16)
                da2 = jnp.where(m2, da, 0.0).astype(BF16)
                da1t = jnp.where(m1t, dat, 0.0).astype(BF16)
                da2t = jnp.where(m2t, dat, 0.0).astype(BF16)
                dqp = dqp + _dot(da1, km_h, "nn")
                dqm = dqm + _dot(da2, kp_h, "nn")
                dkm = dkm + _dot(da1t, qp_h, "nn")
                dkp = dkp + _dot(da2t, qm_h, "nn")
            klb = f["kl"].astype(BF16)
            qsb = f["qs"].astype(BF16)
            dqs = _dot(dob, s_prev.astype(BF16), "nn")
            dkl = _dot(vb, dsb, "nn")
            dv = jnp.concatenate(dv_parts, axis=1) + _dot(klb, dsb, "nt")
            ds_ref[grp] = ds_new * f["ebl"] + jnp.where(blockmask, _dot(dob, qsb, "tn"), 0.0)
            dq = dqp * f["e1"] + dqm * f["e2"] + dqs * f["eb"]
            dk = dkm * f["e2"] + dkp * f["e1"] + dkl * f["el"]
            if grp == 0:
                dq = _rotary_transposed(dq, cos, sin_signed)
                dk = _rotary_transposed(dk * (DK ** -0.5), cos, sin_signed)
                dp_ref[:, C_RQ:C_RQ + QK_W] = dq.astype(dp_ref.dtype)
                dp_ref[:, C_RK:C_RK + QK_W] = dk.astype(dp_ref.dtype)
                dp_ref[:, C_RV:C_RV + V_W] = dv.astype(dp_ref.dtype)
                dp_ref[:, C_RG:C_RG + V_W] = dgate.astype(dp_ref.dtype)
            else:
                dkl_kl = dkl * klb.astype(F32)
                db = (dqp * qpb.astype(F32) - dkm * kmb.astype(F32) - dqm * qmb.astype(F32)
                      + dkp * kpb.astype(F32) + dqs * qsb.astype(F32) - dkl_kl)
                last = (jnp.sum(dkl_kl, axis=0, keepdims=True)
                        + f["ebl"] * jnp.sum(s_prev * ds_new, axis=0, keepdims=True))
                rowq = lax.broadcasted_iota(jnp.int32, (SUPER, QK_W), 0)
                db = db + jnp.where(rowq == SUPER - 1, last, 0.0)
                upper = (col >= row).astype(F32)
                dla = _dot(upper, db, "nn", precision=lax.Precision.HIGHEST)
                dlogit = dla * (1.0 / GATE_NORM) * (1.0 - _sigmoid(logit))
                dlb = dlogit.astype(BF16)
                dglow = _dot(dlb, wa2_ref[...].astype(BF16), "nt")
                dwa_ref[...] += _dot(glow.astype(BF16), dlb, "tn")
                dba_ref[...] += jnp.sum(dlogit, axis=0, keepdims=True)
                dp_ref[:, C_GQ:C_GQ + QK_W] = (dq * (DK ** -0.5)).astype(dp_ref.dtype)
                dp_ref[:, C_GK:C_GK + QK_W] = dk.astype(dp_ref.dtype)
                dp_ref[:, C_GV:C_GV + V_W] = dv.astype(dp_ref.dtype)
                dp_ref[:, C_GG:C_GG + V_W] = dgate.astype(dp_ref.dtype)
                dp_ref[:, C_GL:C_GL + GL_W] = dglow.astype(dp_ref.dtype)

    rev = lambda i: n_s - 1 - i
    const = lambda shape: pl.BlockSpec(shape, lambda i: tuple(0 for _ in shape))
    return _call(
        "attn_bwd", main, (n_s,),
        [pl.BlockSpec((SUPER, PROJ_W), lambda i: (rev(i), 0)),
         pl.BlockSpec((SUPER, 128), lambda i: (rev(i), 0)), pl.BlockSpec((SUPER, 128), lambda i: (rev(i), 0)),
         const((1, QK_W)), const((GL_W, QK_W)), const((1, QK_W)), const((1, V_W)), const((1, V_W)),
         pl.BlockSpec((SUPER, 2 * V_W), lambda i: (rev(i), 0)),
         pl.BlockSpec((SUPER, 2 * V_W), lambda i: (rev(i), 0)),
         pl.BlockSpec((1, 2, V_W, QK_W), lambda i: (rev(i), 0, 0, 0))],
        [pl.BlockSpec((SUPER, PROJ_W), lambda i: (rev(i), 0)),
         const((1, V_W)), const((1, V_W)), const((1, QK_W)), const((GL_W, QK_W))],
        [jax.ShapeDtypeStruct((T, PROJ_W), BF16),
         jax.ShapeDtypeStruct((1, V_W), F32), jax.ShapeDtypeStruct((1, V_W), F32),
         jax.ShapeDtypeStruct((1, QK_W), F32), jax.ShapeDtypeStruct((GL_W, QK_W), F32)],
        (proj, cos, sin_signed, lg, wa2p, ba, gn_ret, gn_gla, o, dy, states),
        scratch=[pltpu.VMEM((2, V_W, QK_W), F32)], comm=comm)


def _rotary_tables(T):
    half = DK // 2
    inv = ROPE_BASE ** (-jnp.arange(half, dtype=F32) * 2.0 / DK)
    ang = jnp.arange(T, dtype=F32)[:, None] * inv[None, :]
    cos, sin = jnp.cos(ang), jnp.sin(ang)
    cos_head = jnp.concatenate([cos, cos], axis=1)
    sin_head = jnp.concatenate([-sin, sin], axis=1)
    return jnp.tile(cos_head, (1, 128 // DK)), jnp.tile(sin_head, (1, 128 // DK))


def _sum_devices(name, gathered, m_per):
    def body(g_ref, o_ref):
        acc = g_ref[0:m_per, :]
        for k in range(1, N_DEV):
            acc = acc + g_ref[k * m_per:(k + 1) * m_per, :]
        o_ref[...] = acc

    return pl.pallas_call(body, name=name, out_shape=jax.ShapeDtypeStruct((m_per, 128), F32))(gathered)


def _owner_sums(name, items, owner, comm=None):
    counts = [1 + len(landed) for _, landed in items]

    def main(ins, outs, scr):
        at = 0
        for o_ref, n in zip(outs, counts):
            acc = ins[at][...].astype(F32)
            for l_ref in ins[at + 1:at + n]:
                for j in range(l_ref.shape[0]):
                    acc = acc + l_ref[j].astype(F32)
            o_ref[...] = acc
            at += n

    once = pl.Buffered(1)
    in_specs, out_specs, out_shape, args = [], [], [], []
    for grad, landed in items:
        R, C = grad.shape[2:]
        in_specs.append(pl.BlockSpec((None, None, R, C), lambda i, s: (s[0], s[1], 0, 0), pipeline_mode=once))
        in_specs += [pl.BlockSpec(tuple(l.shape), lambda i, s: (0, 0, 0), pipeline_mode=once) for l in landed]
        out_specs.append(pl.BlockSpec((R, C), lambda i, s: (0, 0)))
        out_shape.append(jax.ShapeDtypeStruct((R, C), F32))
        args += [grad, *landed]
    return _call(name, main, (1,), in_specs, out_specs, out_shape, args, comm=comm, prefetch=owner)


def _rms_bwd(name, x, g, dh, dres, out_scale):
    T, D = x.shape
    tm = min(T, 512)
    ep = _rms_bwd_epilogue(out_scale)

    def main(ins, outs, scr):
        x_ref, g_ref, dh_ref, dres_ref = ins
        ep(dh_ref[...], (x_ref, g_ref, dres_ref), outs)

    tile = pl.BlockSpec((tm, D), lambda i: (i, 0))
    vec = pl.BlockSpec((1, D), lambda i: (0, 0))
    outs, _ = _call(name, main, (T // tm,), [tile, vec, tile, tile], [tile, tile, vec],
                    [jax.ShapeDtypeStruct((T, D), F32), jax.ShapeDtypeStruct((T, D), BF16),
                     jax.ShapeDtypeStruct((1, D), F32)], (x, g, dh, dres))
    return outs


def _adamw_group(name, items, n_blocks, comm=None):
    n = len(items)

    def main(ins, outs, scr):
        for p in range(n):
            g_ref, w_ref, m_ref, v_ref = ins[4 * p:4 * p + 4]
            d_ref, nm_ref, nv_ref = outs[3 * p:3 * p + 3]
            gv = g_ref[...]
            nm = ADAM_B1 * m_ref[...] + (1.0 - ADAM_B1) * gv
            nv = ADAM_B2 * v_ref[...] + (1.0 - ADAM_B2) * (gv * gv)
            m_hat = nm / (1.0 - ADAM_B1 ** ADAM_STEP)
            v_hat = nv / (1.0 - ADAM_B2 ** ADAM_STEP)
            d_ref[...] = -ADAM_LR * (m_hat / (jnp.sqrt(v_hat) + ADAM_EPS) + ADAM_WD * w_ref[...])
            nm_ref[...] = nm
            nv_ref[...] = nv

    in_specs, out_specs, out_shape, args = [], [], [], []
    for item in items:
        R, C = item[1].shape
        assert R % n_blocks == 0
        spec = pl.BlockSpec((R // n_blocks, C), lambda i: (i, 0))
        in_specs += [spec] * 4
        out_specs += [spec] * 3
        out_shape += [jax.ShapeDtypeStruct((R, C), F32)] * 3
        args += list(item)
    outs, extra = _call(name, main, (n_blocks,), in_specs, out_specs, out_shape, args, comm=comm)
    return [tuple(outs[3 * p:3 * p + 3]) for p in range(n)], extra


SMALL_ORDER = ("ffn1", "mix", "ffn2", "final", "ret", "gla", "b_a")


def kernel(x, ffn1_norm_g, ffn1_w_gate, ffn1_w_up, ffn1_w_down, mix_norm_g, w_in, ret_norm_g, gla_w_a2, gla_b_a, gla_norm_g, w_out, ffn2_norm_g, ffn2_w_gate, ffn2_w_up, ffn2_w_down, final_norm_g, loss_target, m_ffn1_norm_g, m_ffn1_w_gate, m_ffn1_w_up, m_ffn1_w_down, m_mix_norm_g, m_w_in, m_ret_norm_g, m_gla_w_a2, m_gla_b_a, m_gla_norm_g, m_w_out, m_ffn2_norm_g, m_ffn2_w_gate, m_ffn2_w_up, m_ffn2_w_down, m_final_norm_g, v_ffn1_norm_g, v_ffn1_w_gate, v_ffn1_w_up, v_ffn1_w_down, v_mix_norm_g, v_w_in, v_ret_norm_g, v_gla_w_a2, v_gla_b_a, v_gla_norm_g, v_w_out, v_ffn2_norm_g, v_ffn2_w_gate, v_ffn2_w_up, v_ffn2_w_down, v_final_norm_g):
    xi, yi, ci = _coords()
    dev = 4 * xi + 2 * yi + ci
    owner = jnp.stack([2 * xi + yi, ci]).astype(jnp.int32)

    x0, target = x[0], loss_target[0]
    T, D = x0.shape
    fb = ffn1_w_gate.shape[2]
    ib = w_in.shape[2]
    ab = gla_w_a2.shape[2]
    F = N_DEV * fb
    cos, sin_signed = _rotary_tables(T)
    lg = jnp.repeat(jnp.log(1.0 - 2.0 ** (-5.0 - jnp.arange(HEADS, dtype=F32))), DK)[None, :]
    g_final = final_norm_g.reshape(1, D)

    gu1_loc = jnp.stack([ffn1_w_gate[0].T, ffn1_w_up[0].T]).astype(BF16)
    d1_loc = ffn1_w_down.astype(BF16)
    g2_loc = ffn2_w_gate[0].T[None].astype(BF16)
    u2_loc = ffn2_w_up[0].T[None].astype(BF16)
    d2_loc = ffn2_w_down.astype(BF16)
    in_loc = w_in[0].T.astype(BF16)
    out_loc = w_out[0].astype(BF16)

    h1, (gu1,) = _rms_fwd("ffn1_rms", x0, ffn1_norm_g, comm=_AllGather([gu1_loc], ["stack"]))
    gu1 = gu1.reshape(2, F, D)
    (gate1, up1, act1), (d1, in_all) = _mm_nstream(
        "ffn1_up", h1, [gu1, gu1], [0, 1], "nt", [], [BF16, BF16, BF16], _silu_mul_epilogue, cn=256,
        comm=_AllGather([d1_loc, in_loc], ["stack", "plain"]))
    d1 = d1.reshape(1, F, D)
    w_in_t = jnp.pad(in_all.reshape(1, N_DEV * ib, D), ((0, 0), (0, PROJ_W - N_DEV * ib), (0, 0)))
    f32_tile, bf16_tile, f32_vec = (F32, "tile"), (BF16, "tile"), (F32, "vec")
    (x1, h2), (out_all, a_all) = _mm_mstream(
        "ffn1_down", [act1], [d1], [0], "nn", [(x0, "tile"), (mix_norm_g, "vec")], [f32_tile, bf16_tile],
        _residual_rms_epilogue(0.5), comm=_AllGather([out_loc, gla_w_a2[0]], ["plain", "plain"]))
    w_out_full = out_all.reshape(D, D)
    wa2 = jnp.transpose(a_all, (1, 0, 2)).reshape(GATE_RANK, N_DEV * ab)
    wa2p = jnp.pad(wa2, ((0, GL_W - GATE_RANK), (0, 0)))

    (proj,), (g2,) = _mm_nstream("mix_proj", h2, [w_in_t], [0], "nt", [], [F32], _identity_epilogue, cn=640,
                                 comm=_AllGather([g2_loc], ["stack"]))
    (o, ymix, states), (u2,) = _attn_fwd(proj, cos, sin_signed, lg, wa2p, gla_b_a, ret_norm_g, gla_norm_g,
                                         comm=_AllGather([u2_loc], ["stack"]))
    (x2, h3), _ = _mm_mstream("mix_out", [ymix], [w_out_full], [0], "nn", [(x1, "tile"), (ffn2_norm_g, "vec")],
                              [f32_tile, bf16_tile], _residual_rms_epilogue(1.0))
    g2, u2 = g2.reshape(1, F, D), u2.reshape(1, F, D)

    (gate2, up2, act2), (d2,) = _mm_nstream(
        "ffn2_up", h3, [g2, u2], [0, 0], "nt", [], [BF16, BF16, BF16], _silu_mul_epilogue, cn=256,
        comm=_AllGather([d2_loc], ["stack"]))
    d2 = d2.reshape(1, F, D)
    (x3,), _ = _mm_mstream("ffn2_down", [act2], [d2], [0], "nn", [(x2, "tile")], [f32_tile], _residual_epilogue(0.5))

    dx3, dy3b, d_final, loss = _final_loss_bwd("final", x3, g_final, target, 0.5)

    (dgate2, dup2), _ = _mm_nstream("ffn2b_dact", dy3b, [d2], [0], "nt", [gate2, up2], [BF16, BF16],
                                    _dact_epilogue, cn=256)
    dwg2, _ = _mm_tn("ffn2b_dwg", dgate2, h3, F // 2, D, BF16)
    dwg2 = dwg2.reshape(4, 2, fb, D)
    dwu2, (l_wg2_near,) = _mm_tn("ffn2b_dwu", dup2, h3, F // 2, D, BF16, comm=_ReduceScatter([(dwg2, NEAR)]))
    dwu2 = dwu2.reshape(4, 2, fb, D)
    dwd2, (l_wg2_far,) = _mm_tn("ffn2b_dwd", act2, dy3b, F // 2, D, BF16, comm=_ReduceScatter([(dwg2, FAR)]))
    dwd2 = dwd2.reshape(4, 2, fb, D)
    rms_outs = [f32_tile, bf16_tile, f32_vec]
    (dx2, dx2b, d_g2), (l_wu2,) = _mm_mstream(
        "ffn2b_dh", [dgate2, dup2], [g2, u2], [0, 0], "nn", [(x2, "tile"), (ffn2_norm_g, "vec"), (dx3, "tile")],
        rms_outs, _rms_bwd_epilogue(1.0), comm=_ReduceScatter([(dwu2, ALL)]))

    (dymix,), _ = _mm_mstream("mixb_dy", [dx2b], [w_out_full], [0], "nt", [], [f32_tile], _plain_epilogue)
    dwout, _ = _mm_tn("mixb_dwout", ymix, dx2b, D, D, BF16)
    dwout = dwout.reshape(4, 2, D // N_DEV, D)
    (dproj, d_ret, d_gla, d_ba, d_wa2p), (l_wd2, l_wout) = _attn_bwd(
        proj, cos, sin_signed, lg, wa2p, gla_b_a, ret_norm_g, gla_norm_g, o, dymix, states,
        comm=_ReduceScatter([(dwd2, ALL), (dwout, ALL)]))
    dwin_t, _ = _mm_tn("mixb_dwin", dproj, h2, 640, D, BF16)
    dwin = dwin_t[:N_DEV * ib].reshape(4, 2, ib, D)
    (dx1, dy1b, d_gmix), (l_win_near,) = _mm_mstream(
        "mixb_dh", [dproj], [w_in_t], [0], "nn", [(x1, "tile"), (mix_norm_g, "vec"), (dx2, "tile")],
        rms_outs, _rms_bwd_epilogue(0.5), comm=_ReduceScatter([(dwin, NEAR)]))

    (dgate1, dup1), (l_win_far,) = _mm_nstream("ffn1b_dact", dy1b, [d1], [0], "nt", [gate1, up1], [BF16, BF16],
                                               _dact_epilogue, cn=256, comm=_ReduceScatter([(dwin, FAR)]))
    dwg1, _ = _mm_tn("ffn1b_dwg", dgate1, h1, F // 2, D, BF16)
    dwg1 = dwg1.reshape(4, 2, fb, D)
    dwu1, (l_wg1_near,) = _mm_tn("ffn1b_dwu", dup1, h1, F // 2, D, BF16, comm=_ReduceScatter([(dwg1, NEAR)]))
    dwu1 = dwu1.reshape(4, 2, fb, D)
    dwd1, (l_wg1_far, l_wu1_near) = _mm_tn("ffn1b_dwd", act1, dy1b, F // 2, D, BF16,
                                           comm=_ReduceScatter([(dwg1, FAR), (dwu1, NEAR)]))
    dwd1 = dwd1.reshape(4, 2, fb, D)
    (dh1,), (l_wu1_far, l_wd1_near) = _mm_mstream(
        "ffn1b_dh", [dgate1, dup1], [gu1, gu1], [0, 1], "nn", [], [f32_tile], _plain_epilogue,
        comm=_ReduceScatter([(dwu1, FAR), (dwd1, NEAR)]))
    dx0, _, d_g1 = _rms_bwd("ffn1b_rms", x0, ffn1_norm_g, dh1, dx1, 1.0)

    small = dict(ffn1=d_g1, mix=d_gmix, ffn2=d_g2, final=d_final, ret=d_ret, gla=d_gla, b_a=d_ba)
    flat = jnp.concatenate([small[k].reshape(-1) for k in SMALL_ORDER]
                           + [d_wa2p[:GATE_RANK].reshape(-1), loss[0]])
    rows = -(-flat.shape[0] // 128)
    rows = -(-rows // 8) * 8
    packed = jnp.pad(flat, (0, rows * 128 - flat.shape[0])).reshape(rows, 128)

    def adam_item(nm, grad, weight, mom, var):
        shape = weight.shape
        two_d = (1, shape[0]) if len(shape) == 1 else shape[-2:]
        return tuple(a.reshape(two_d) for a in (grad, weight, mom, var))

    sums_a, (l_wd1_far, gathered) = _owner_sums(
        "sum_a", [(dwg2, [l_wg2_near, l_wg2_far]), (dwu2, [l_wu2]), (dwd2, [l_wd2]),
                  (dwin, [l_win_near, l_win_far]), (dwout, [l_wout])], owner,
        comm=_Multi([_ReduceScatter([(dwd1, FAR)]), _AllGather([packed], ["plain"])]))
    sums_b, _ = _owner_sums(
        "sum_b", [(dwg1, [l_wg1_near, l_wg1_far]), (dwu1, [l_wu1_near, l_wu1_far]), (dwd1, [l_wd1_near, l_wd1_far])],
        owner)
    big_grads = {"ffn2_w_gate": sums_a[0].T, "ffn2_w_up": sums_a[1].T, "ffn2_w_down": sums_a[2], "w_in": sums_a[3].T,
                 "w_out": sums_a[4], "ffn1_w_gate": sums_b[0].T, "ffn1_w_up": sums_b[1].T, "ffn1_w_down": sums_b[2]}
    params = dict(
        ffn2_w_gate=(ffn2_w_gate, m_ffn2_w_gate, v_ffn2_w_gate), ffn2_w_up=(ffn2_w_up, m_ffn2_w_up, v_ffn2_w_up),
        ffn2_w_down=(ffn2_w_down, m_ffn2_w_down, v_ffn2_w_down), w_in=(w_in, m_w_in, v_w_in),
        w_out=(w_out, m_w_out, v_w_out), ffn1_w_gate=(ffn1_w_gate, m_ffn1_w_gate, v_ffn1_w_gate),
        ffn1_w_up=(ffn1_w_up, m_ffn1_w_up, v_ffn1_w_up), ffn1_w_down=(ffn1_w_down, m_ffn1_w_down, v_ffn1_w_down),
        ffn1_norm_g=(ffn1_norm_g, m_ffn1_norm_g, v_ffn1_norm_g), mix_norm_g=(mix_norm_g, m_mix_norm_g, v_mix_norm_g),
        ret_norm_g=(ret_norm_g, m_ret_norm_g, v_ret_norm_g), gla_w_a2=(gla_w_a2, m_gla_w_a2, v_gla_w_a2),
        gla_b_a=(gla_b_a, m_gla_b_a, v_gla_b_a), gla_norm_g=(gla_norm_g, m_gla_norm_g, v_gla_norm_g),
        ffn2_norm_g=(ffn2_norm_g, m_ffn2_norm_g, v_ffn2_norm_g), final_norm_g=(final_norm_g, m_final_norm_g, v_final_norm_g))
    grads, updates = {}, {}

    def run_adam(name, names, grad_of, n_blocks):
        for nm in names:
            grads[nm] = grad_of[nm].reshape(params[nm][0].shape)
        res, _ = _adamw_group(name, [adam_item(nm, grads[nm], *params[nm]) for nm in names], n_blocks)
        for nm, r in zip(names, res):
            updates[nm] = tuple(a.reshape(params[nm][0].shape) for a in r)

    run_adam("adamw_big", list(big_grads), big_grads, 4)

    total = _sum_devices("sum_small", gathered.reshape(N_DEV * rows, 128), rows).reshape(-1)
    sizes = [small[k].size for k in SMALL_ORDER] + [GATE_RANK * QK_W, 128]
    offs = [0]
    for s in sizes:
        offs.append(offs[-1] + s)
    pieces = [total[offs[i]:offs[i + 1]] for i in range(len(sizes))]
    g_small = {k: pieces[i].reshape(small[k].shape) for i, k in enumerate(SMALL_ORDER)}
    g_wa2_full = pieces[len(SMALL_ORDER)].reshape(GATE_RANK, QK_W)
    g_wa2 = lax.dynamic_slice(g_wa2_full, (0, dev * ab), (GATE_RANK, ab))
    loss_total = pieces[len(SMALL_ORDER) + 1][0]

    small_grads = {"ffn1_norm_g": g_small["ffn1"], "mix_norm_g": g_small["mix"], "ret_norm_g": g_small["ret"],
                   "gla_w_a2": g_wa2, "gla_b_a": g_small["b_a"], "gla_norm_g": g_small["gla"],
                   "ffn2_norm_g": g_small["ffn2"], "final_norm_g": g_small["final"]}
    run_adam("adamw_small", list(small_grads), small_grads, 1)

    order = ("ffn1_norm_g", "ffn1_w_gate", "ffn1_w_up", "ffn1_w_down", "mix_norm_g", "w_in", "ret_norm_g", "gla_w_a2",
             "gla_b_a", "gla_norm_g", "w_out", "ffn2_norm_g", "ffn2_w_gate", "ffn2_w_up", "ffn2_w_down", "final_norm_g")
    return (loss_total, dx0[None], *[grads[nm] for nm in order], *[updates[nm][0] for nm in order],
            *[updates[nm][1] for nm in order], *[updates[nm][2] for nm in order])
```

```python
import functools
import math

import jax
import jax.numpy as jnp
from jax import lax
from jax.experimental import pallas as pl
from jax.experimental.pallas import tpu as pltpu

F32 = jnp.float32
BF16 = jnp.bfloat16
MESH = pl.DeviceIdType.MESH
HBM = pl.BlockSpec(memory_space=pltpu.HBM)

N_DEV = 8
RMS_EPS = 1e-6
ROPE_BASE = 10000.0
HEADS = 4
DK = 64
DV = 128
QK_W = HEADS * DK
V_W = HEADS * DV
GATE_RANK = 16
GATE_NORM = 16.0
CHUNK = 64
SUPER = 256
PROJ_W = 3200
C_RQ, C_RK, C_RV, C_RG, C_GQ, C_GK, C_GV, C_GG, C_GL = 0, 256, 512, 1024, 1536, 1792, 2048, 2560, 3072
GL_W = PROJ_W - C_GL
ADAM_LR, ADAM_B1, ADAM_B2, ADAM_EPS, ADAM_WD, ADAM_STEP = 0.001, 0.9, 0.999, 1e-08, 0.01, 10
VMEM_LIMIT_V7X = 52 * 1024 * 1024


def _cparams(**kw):
    return pltpu.CompilerParams(vmem_limit_bytes=VMEM_LIMIT_V7X, **kw)


def _dot(a, b, form, precision=None):
    dims = {"nn": (((1,), (0,)), ((), ())), "nt": (((1,), (1,)), ((), ())), "tn": (((0,), (0,)), ((), ()))}[form]
    return lax.dot_general(a, b, dims, preferred_element_type=F32, precision=precision)


def _sigmoid(x):
    return 1.0 / (1.0 + jnp.exp(-x))


def _coords():
    return lax.axis_index("x"), lax.axis_index("y"), lax.axis_index("c")


class _NoComm:
    inputs, out_shapes, scratch = (), (), ()


class _AllGather:
    def __init__(self, arrays, kinds):
        self.inputs = tuple(arrays)
        self.kinds = tuple(kinds)
        n = len(arrays)
        self.out_shapes = tuple(
            jax.ShapeDtypeStruct((a.shape[0], N_DEV) + a.shape[1:] if k == "stack" else (N_DEV,) + a.shape, a.dtype)
            for a, k in zip(arrays, kinds))
        self.scratch = (pltpu.SemaphoreType.DMA((n, 7)), pltpu.SemaphoreType.DMA((n, 7)),
                        pltpu.SemaphoreType.DMA((n,)))

    def _ctx(self, srcs, outs, sems):
        send_sems, recv_sems, local_sems = sems
        x, y, c = _coords()
        me, sibling = (x, y, c), (x, y, 1 - c)
        chips = [(1 - x, y), (x, 1 - y), (1 - x, 1 - y)]

        def blk(m, dev):
            k = 4 * dev[0] + 2 * dev[1] + dev[2]
            return outs[m].at[:, k] if self.kinds[m] == "stack" else outs[m].at[k]

        def copy(m, s, block, to, src=None):
            return pltpu.make_async_remote_copy(
                src_ref=blk(m, block) if src is None else src, dst_ref=blk(m, block),
                send_sem=send_sems.at[m, s], recv_sem=recv_sems.at[m, s], device_id=to, device_id_type=MESH)

        def mine(m):
            return pltpu.make_async_copy(srcs[m], blk(m, me), local_sems.at[m])

        def first(m):
            return [copy(m, 0, me, sibling, src=srcs[m])] + [
                copy(m, 1 + j, me, (*chip, c), src=srcs[m]) for j, chip in enumerate(chips)]

        return me, sibling, chips, c, copy, mine, first

    def start(self, srcs, outs, sems):
        me, sibling, chips, c, copy, mine, first = self._ctx(srcs, outs, sems)
        for m in range(len(srcs)):
            mine(m).start()
            for cp in first(m):
                cp.start()

    def mid(self, srcs, outs, sems):
        me, sibling, chips, c, copy, mine, first = self._ctx(srcs, outs, sems)
        for j, chip in enumerate(chips):
            for m in range(len(srcs)):
                copy(m, 1 + j, (*chip, c), me).wait_recv()
                copy(m, 4 + j, (*chip, c), sibling).start()

    def finish(self, srcs, outs, sems):
        me, sibling, chips, c, copy, mine, first = self._ctx(srcs, outs, sems)
        for m in range(len(srcs)):
            copy(m, 0, sibling, me).wait_recv()
            for j, chip in enumerate(chips):
                copy(m, 4 + j, (*chip, 1 - c), me).wait_recv()
            for cp in first(m):
                cp.wait_send()
            for j, chip in enumerate(chips):
                copy(m, 4 + j, (*chip, c), sibling).wait_send()
            mine(m).wait()


RELATIONS = ((0, 0, 1), (1, 0, 0), (0, 1, 0), (1, 1, 0), (1, 0, 1), (0, 1, 1), (1, 1, 1))
NEAR = (0, 1, 2, 4, 5)
FAR = (3, 6)
ALL = NEAR + FAR


class _ReduceScatter:
    def __init__(self, parts):
        self.inputs = tuple(g for g, _ in parts)
        self.slots = tuple(s for _, s in parts)
        self.out_shapes = tuple(jax.ShapeDtypeStruct((len(s),) + g.shape[2:], g.dtype) for g, s in parts)
        n_max = max(len(s) for s in self.slots)
        n = len(parts)
        self.scratch = (pltpu.SemaphoreType.DMA((n, n_max)), pltpu.SemaphoreType.DMA((n, n_max)))

    def _copies(self, srcs, outs, sems):
        send_sems, recv_sems = sems
        x, y, c = _coords()
        copies = []
        for m, slots in enumerate(self.slots):
            for i, s in enumerate(slots):
                fx, fy, fc = RELATIONS[s]
                px = 1 - x if fx else x
                py = 1 - y if fy else y
                pc = 1 - c if fc else c
                copies.append(pltpu.make_async_remote_copy(
                    src_ref=srcs[m].at[2 * px + py, pc], dst_ref=outs[m].at[i], send_sem=send_sems.at[m, i],
                    recv_sem=recv_sems.at[m, i], device_id=(px, py, pc), device_id_type=MESH))
        return copies

    def start(self, srcs, outs, sems):
        for cp in self._copies(srcs, outs, sems):
            cp.start()

    def mid(self, srcs, outs, sems):
        pass

    def finish(self, srcs, outs, sems):
        for cp in self._copies(srcs, outs, sems):
            cp.wait()


class _SiblingExchange:
    def __init__(self, grads):
        self.inputs = tuple(grads)
        self.out_shapes = tuple(jax.ShapeDtypeStruct((4,) + g.shape[2:], g.dtype) for g in grads)
        self.scratch = (pltpu.SemaphoreType.DMA((len(grads),)), pltpu.SemaphoreType.DMA((len(grads),)))

    def _copies(self, srcs, outs, sems):
        send_sems, recv_sems = sems
        x, y, c = _coords()
        return [pltpu.make_async_remote_copy(
            src_ref=srcs[m].at[:, 1 - c], dst_ref=outs[m], send_sem=send_sems.at[m], recv_sem=recv_sems.at[m],
            device_id=(x, y, 1 - c), device_id_type=MESH) for m in range(len(srcs))]

    def start(self, srcs, outs, sems):
        for cp in self._copies(srcs, outs, sems):
            cp.start()

    def mid(self, srcs, outs, sems):
        pass

    def finish(self, srcs, outs, sems):
        for cp in self._copies(srcs, outs, sems):
            cp.wait()


class _ChipExchange:
    def __init__(self, partials):
        self.inputs = tuple(partials)
        self.out_shapes = tuple(jax.ShapeDtypeStruct((3,) + p.shape[1:], p.dtype) for p in partials)
        n = len(partials)
        self.scratch = (pltpu.SemaphoreType.DMA((n, 3)), pltpu.SemaphoreType.DMA((n, 3)))

    def _copies(self, srcs, outs, sems):
        send_sems, recv_sems = sems
        x, y, c = _coords()
        copies = []
        for m in range(len(srcs)):
            for j, (px, py) in enumerate([(1 - x, y), (x, 1 - y), (1 - x, 1 - y)]):
                copies.append(pltpu.make_async_remote_copy(
                    src_ref=srcs[m].at[2 * px + py], dst_ref=outs[m].at[j], send_sem=send_sems.at[m, j],
                    recv_sem=recv_sems.at[m, j], device_id=(px, py, c), device_id_type=MESH))
        return copies

    def start(self, srcs, outs, sems):
        for cp in self._copies(srcs, outs, sems):
            cp.start()

    def mid(self, srcs, outs, sems):
        pass

    def finish(self, srcs, outs, sems):
        for cp in self._copies(srcs, outs, sems):
            cp.wait()


class _Multi:
    def __init__(self, comms):
        self.comms = comms
        self.inputs = tuple(a for c in comms for a in c.inputs)
        self.out_shapes = tuple(s for c in comms for s in c.out_shapes)
        self.scratch = tuple(s for c in comms for s in c.scratch)

    def _each(self, phase, srcs, outs, sems):
        i = o = k = 0
        for c in self.comms:
            ni, no, nk = len(c.inputs), len(c.out_shapes), len(c.scratch)
            getattr(c, phase)(srcs[i:i + ni], outs[o:o + no], sems[k:k + nk])
            i, o, k = i + ni, o + no, k + nk

    def start(self, srcs, outs, sems):
        self._each("start", srcs, outs, sems)

    def mid(self, srcs, outs, sems):
        self._each("mid", srcs, outs, sems)

    def finish(self, srcs, outs, sems):
        self._each("finish", srcs, outs, sems)


def _call(name, main, grid, in_specs, out_specs, out_shape, args, scratch=(), comm=None, prefetch=None):
    comm = comm or _NoComm()
    counts = [len(in_specs), len(comm.inputs), len(out_shape), len(comm.out_shapes), len(scratch), len(comm.scratch)]
    n_steps = math.prod(grid)
    hosted = bool(comm.inputs)

    def body(*refs):
        if prefetch is not None:
            refs = refs[1:]
        parts, at = [], 0
        for n in counts:
            parts.append(refs[at:at + n])
            at += n
        ins, c_in, outs, c_out, scr, c_scr = parts
        step = pl.program_id(0)
        for d in range(1, len(grid)):
            step = step * grid[d] + pl.program_id(d)
        if hosted:
            @pl.when(step == 0)
            def _():
                comm.start(c_in, c_out, c_scr)
        main(ins, outs, scr)
        if hosted:
            @pl.when(step == max(n_steps - 2, 0))
            def _():
                comm.mid(c_in, c_out, c_scr)

            @pl.when(step == n_steps - 1)
            def _():
                comm.finish(c_in, c_out, c_scr)

    all_in = list(in_specs) + [HBM] * counts[1]
    all_out = list(out_specs) + [HBM] * counts[3]
    all_scratch = list(scratch) + list(comm.scratch)
    shapes = list(out_shape) + list(comm.out_shapes)
    if prefetch is None:
        res = pl.pallas_call(body, name=name, grid=grid, in_specs=all_in, out_specs=all_out, out_shape=shapes,
                             scratch_shapes=all_scratch, compiler_params=_cparams())(*args, *comm.inputs)
    else:
        res = pl.pallas_call(
            body, name=name, out_shape=shapes,
            grid_spec=pltpu.PrefetchScalarGridSpec(num_scalar_prefetch=1, grid=grid, in_specs=all_in,
                                                   out_specs=all_out, scratch_shapes=all_scratch),
            compiler_params=_cparams())(prefetch, *args, *comm.inputs)
    return res[:counts[2]], res[counts[2]:]


def _rms_fwd(name, x, g, comm=None):
    T, D = x.shape
    tm = min(T, 512)

    def main(ins, outs, scr):
        x_ref, g_ref = ins
        xv = x_ref[...]
        r = lax.rsqrt(jnp.mean(xv * xv, axis=-1, keepdims=True) + RMS_EPS)
        outs[0][...] = (xv * r * g_ref[...]).astype(outs[0].dtype)

    tile = pl.BlockSpec((tm, D), lambda i: (i, 0))
    (h,), extra = _call(name, main, (T // tm,), [tile, pl.BlockSpec((1, D), lambda i: (0, 0))], [tile],
                        [jax.ShapeDtypeStruct((T, D), BF16)], (x, g), comm=comm)
    return h, extra


def _final_loss_bwd(name, x, g, target, out_scale):
    T, D = x.shape
    tm = min(T, 512)

    def main(ins, outs, scr):
        x_ref, g_ref, t_ref = ins
        dx_ref, dxb_ref, dg_ref, loss_ref = outs
        i = pl.program_id(0)
        xv = x_ref[...]
        r = lax.rsqrt(jnp.mean(xv * xv, axis=-1, keepdims=True) + RMS_EPS)
        xhat = xv * r
        err = xhat * g_ref[...] - t_ref[...]

        @pl.when(i == 0)
        def _():
            dg_ref[...] = jnp.zeros_like(dg_ref)
            loss_ref[...] = jnp.zeros_like(loss_ref)

        loss_ref[...] += jnp.broadcast_to(jnp.sum(err * err) * (0.5 / D), loss_ref.shape)
        dy = err * (1.0 / D)
        dg_ref[...] += jnp.sum(dy * xhat, axis=0, keepdims=True)
        dxhat = dy * g_ref[...]
        dx = r * (dxhat - xhat * jnp.mean(dxhat * xhat, axis=-1, keepdims=True))
        dx_ref[...] = dx
        dxb_ref[...] = (out_scale * dx).astype(dxb_ref.dtype)

    tile = pl.BlockSpec((tm, D), lambda i: (i, 0))
    vec = pl.BlockSpec((1, D), lambda i: (0, 0))
    outs, _ = _call(name, main, (T // tm,), [tile, vec, tile],
                    [tile, tile, vec, pl.BlockSpec((1, 128), lambda i: (0, 0))],
                    [jax.ShapeDtypeStruct((T, D), F32), jax.ShapeDtypeStruct((T, D), BF16),
                     jax.ShapeDtypeStruct((1, D), F32), jax.ShapeDtypeStruct((1, 128), F32)], (x, g, target))
    return outs


def _mm_nstream(name, a, ws, w_sel, w_form, comps, out_dtypes, epilogue, cn, rows=1024, comm=None):
    T, K = a.shape
    N = ws[0].shape[1]
    rows = min(rows, T)
    assert N % cn == 0 and T % rows == 0
    n_w, n_c = len(ws), len(comps)

    def main(ins, outs, scr):
        a_ref = ins[0]
        w_refs = ins[1:1 + n_w]
        c_refs = ins[1 + n_w:]

        for r in range(T // rows):
            sl = slice(r * rows, (r + 1) * rows)
            a_blk = a_ref[sl, :]
            dots = [_dot(a_blk, w_ref[...], w_form) for w_ref in w_refs]
            res = epilogue(dots, [c_ref[sl, :] for c_ref in c_refs])
            for o_ref, o in zip(outs, res):
                o_ref[sl, :] = o.astype(o_ref.dtype)

    if w_form == "nt":
        w_specs = [pl.BlockSpec((None, cn, K), functools.partial(lambda j, s: (s, j, 0), s=s)) for s in w_sel]
    else:
        w_specs = [pl.BlockSpec((K, cn), lambda j: (0, j)) for _ in ws]
    chunk = pl.BlockSpec((T, cn), lambda j: (0, j))
    return _call(name, main, (N // cn,), [pl.BlockSpec((T, K), lambda j: (0, 0))] + w_specs + [chunk] * n_c,
                 [chunk] * len(out_dtypes), [jax.ShapeDtypeStruct((T, N), dt) for dt in out_dtypes],
                 (a, *ws, *comps), comm=comm)


def _mm_mstream(name, as_, ws, w_sel, w_form, extras, outs_desc, epilogue, tm=512, comm=None):
    T = as_[0].shape[0]
    tm = min(tm, T)
    n_a = len(as_)
    w_shapes = [w.shape[-2:] for w in ws]
    N = w_shapes[0][1] if w_form == "nn" else w_shapes[0][0]

    def main(ins, outs, scr):
        a_refs = ins[:n_a]
        w_refs = ins[n_a:2 * n_a]
        acc = None
        for a_ref, w_ref in zip(a_refs, w_refs):
            d = _dot(a_ref[...], w_ref[...], w_form)
            acc = d if acc is None else acc + d
        epilogue(acc, ins[2 * n_a:], outs)

    kind_spec = {"tile": pl.BlockSpec((tm, N), lambda i: (i, 0)), "vec": pl.BlockSpec((1, N), lambda i: (0, 0))}
    kind_shape = {"tile": (T, N), "vec": (1, N)}
    a_specs = [pl.BlockSpec((tm, a.shape[1]), lambda i: (i, 0)) for a in as_]
    w_specs = []
    for w, s in zip(ws, w_sel):
        if w.ndim == 3:
            w_specs.append(pl.BlockSpec((None,) + tuple(w.shape[1:]), functools.partial(lambda i, s: (s, 0, 0), s=s),
                                        pipeline_mode=pl.Buffered(1)))
        else:
            w_specs.append(pl.BlockSpec(tuple(w.shape), lambda i: (0, 0), pipeline_mode=pl.Buffered(1)))
    args = list(as_) + list(ws) + [e for e, _ in extras]
    return _call(name, main, (T // tm,), a_specs + w_specs + [kind_spec[k] for _, k in extras],
                 [kind_spec[k] for _, k in outs_desc],
                 [jax.ShapeDtypeStruct(kind_shape[k], dt) for dt, k in outs_desc], args, comm=comm)


def _plain_epilogue(acc, ex, outs):
    outs[0][...] = acc.astype(outs[0].dtype)


def _residual_epilogue(scale):
    def ep(acc, ex, outs):
        outs[0][...] = ex[0][...] + scale * acc
    return ep


def _residual_rms_epilogue(scale):
    def ep(acc, ex, outs):
        xv = ex[0][...] + scale * acc
        outs[0][...] = xv
        r = lax.rsqrt(jnp.mean(xv * xv, axis=-1, keepdims=True) + RMS_EPS)
        outs[1][...] = (xv * r * ex[1][...]).astype(outs[1].dtype)
    return ep


def _rms_bwd_epilogue(out_scale):
    def ep(acc, ex, outs):
        x_ref, g_ref, dres_ref = ex
        dx_ref, dxb_ref, dg_ref = outs
        xv = x_ref[...]
        r = lax.rsqrt(jnp.mean(xv * xv, axis=-1, keepdims=True) + RMS_EPS)
        xhat = xv * r

        @pl.when(pl.program_id(0) == 0)
        def _():
            dg_ref[...] = jnp.zeros_like(dg_ref)

        dg_ref[...] += jnp.sum(acc * xhat, axis=0, keepdims=True)
        dxhat = acc * g_ref[...]
        dx = r * (dxhat - xhat * jnp.mean(dxhat * xhat, axis=-1, keepdims=True)) + dres_ref[...]
        dx_ref[...] = dx
        dxb_ref[...] = (out_scale * dx).astype(dxb_ref.dtype)
    return ep


def _mm_tn(name, a, b, tmo, tno, out_dtype, tk=1024, comm=None):
    T, Ma = a.shape
    Nb = b.shape[1]
    tk = min(tk, T)
    nk = T // tk

    def main(ins, outs, scr):
        a_ref, b_ref = ins
        (acc_ref,) = scr
        k = pl.program_id(2)

        @pl.when(k == 0)
        def _():
            acc_ref[...] = jnp.zeros_like(acc_ref)

        acc_ref[...] += _dot(a_ref[...], b_ref[...], "tn")

        @pl.when(k == nk - 1)
        def _():
            outs[0][...] = acc_ref[...].astype(outs[0].dtype)

    (out,), extra = _call(
        name, main, (Ma // tmo, Nb // tno, nk),
        [pl.BlockSpec((tk, tmo), lambda i, j, k: (k, i)), pl.BlockSpec((tk, tno), lambda i, j, k: (k, j))],
        [pl.BlockSpec((tmo, tno), lambda i, j, k: (i, j))], [jax.ShapeDtypeStruct((Ma, Nb), out_dtype)],
        (a, b), scratch=[pltpu.VMEM((tmo, tno), F32)], comm=comm)
    return out, extra


def _silu_mul_epilogue(dots, comps):
    g, u = dots
    act = g * _sigmoid(g) * u
    return [g, u, act]


def _up_act_epilogue(dots, comps):
    (u,) = dots
    g = comps[0].astype(F32)
    return [u, g * _sigmoid(g) * u]


def _dact_epilogue(dots, comps):
    (dact,) = dots
    g = comps[0].astype(F32)
    u = comps[1].astype(F32)
    s = _sigmoid(g)
    silu = g * s
    dsilu = s + silu * (1.0 - s)
    return [dact * u * dsilu, dact * silu]


def _identity_epilogue(dots, comps):
    return list(dots)


def _swap_halves(x):
    lane = lax.broadcasted_iota(jnp.int32, x.shape, 1)
    first = (lane % DK) < (DK // 2)
    return jnp.where(first, pltpu.roll(x, 128 - DK // 2, 1), pltpu.roll(x, DK // 2, 1))


def _rotary(t, cos, sin_signed):
    halves = []
    for p in range(QK_W // 128):
        th = t[:, 128 * p:128 * (p + 1)]
        halves.append(th * cos + _swap_halves(th) * sin_signed)
    return jnp.concatenate(halves, axis=1)


def _rotary_transposed(d, cos, sin_signed):
    halves = []
    for p in range(QK_W // 128):
        dh = d[:, 128 * p:128 * (p + 1)]
        halves.append(dh * cos + _swap_halves(dh * sin_signed))
    return jnp.concatenate(halves, axis=1)


def _log_sigmoid(x):
    return jnp.minimum(x, 0.0) - jnp.log(1.0 + jnp.exp(-jnp.abs(x)))


def _attn_masks():
    row = lax.broadcasted_iota(jnp.int32, (SUPER, SUPER), 0)
    col = lax.broadcasted_iota(jnp.int32, (SUPER, SUPER), 1)
    same = (row // CHUNK) == (col // CHUNK)
    return row, col, same


def _group_inputs(grp, pr, cos, sin_signed, lg, wa2, ba):
    if grp == 0:
        q = _rotary(pr[:, C_RQ:C_RQ + QK_W], cos, sin_signed)
        k = _rotary(pr[:, C_RK:C_RK + QK_W], cos, sin_signed) * (DK ** -0.5)
        v = pr[:, C_RV:C_RV + V_W]
        gate = pr[:, C_RG:C_RG + V_W]
        pos = lax.broadcasted_iota(jnp.int32, (SUPER, QK_W), 0).astype(F32) + 1.0
        return q, k, v, gate, pos * lg, None, None
    q = pr[:, C_GQ:C_GQ + QK_W] * (DK ** -0.5)
    k = pr[:, C_GK:C_GK + QK_W]
    v = pr[:, C_GV:C_GV + V_W]
    gate = pr[:, C_GG:C_GG + V_W]
    glow = pr[:, C_GL:C_GL + GL_W]
    logit = _dot(glow.astype(BF16), wa2.astype(BF16), "nn") + ba
    la = _log_sigmoid(logit) * (1.0 / GATE_NORM)
    row, col, _ = _attn_masks()
    lower = (col <= row).astype(F32)
    b_cum = _dot(lower, la, "nn", precision=lax.Precision.HIGHEST)
    return q, k, v, gate, b_cum, glow, logit


def _decay_factors(q, k, b_cum):
    c = b_cum[SUPER // 2 - 1:SUPER // 2, :]
    bl = b_cum[SUPER - 1:SUPER, :]
    e1 = jnp.exp(b_cum - c)
    e2 = jnp.exp(c - b_cum)
    e_b = jnp.exp(b_cum)
    e_l = jnp.exp(bl - b_cum)
    return dict(e1=e1, e2=e2, eb=e_b, el=e_l, ebl=jnp.exp(bl),
                qp=q * e1, qm=q * e2, kp=k * e1, km=k * e2, qs=q * e_b, kl=k * e_l)


def _state_block_mask():
    r = lax.broadcasted_iota(jnp.int32, (V_W, QK_W), 0)
    c = lax.broadcasted_iota(jnp.int32, (V_W, QK_W), 1)
    return (r // DV) == (c // DK)


def _attn_fwd(proj, cos, sin_signed, lg, wa2p, ba, gn_ret, gn_gla, comm=None):
    T = proj.shape[0]
    n_s = T // SUPER

    def main(ins, outs, scr):
        pr_ref, cos_ref, sin_ref, lg_ref, wa2_ref, ba_ref, gr_ref, gg_ref = ins
        o_ref, y_ref, st_ref = outs
        (s_ref,) = scr
        i = pl.program_id(0)

        @pl.when(i == 0)
        def _():
            s_ref[...] = jnp.zeros_like(s_ref)

        pr = pr_ref
        row, col, same = _attn_masks()
        m1 = col <= row
        m2 = jnp.logical_and(col > row, same)
        lane = lax.broadcasted_iota(jnp.int32, (1, QK_W), 1)
        blockmask = _state_block_mask()
        for grp in range(2):
            q, k, v, gate, b_cum, _, _ = _group_inputs(grp, pr, cos_ref[...], sin_ref[...], lg_ref[...],
                                                      wa2_ref[...], ba_ref[...])
            f = _decay_factors(q, k, b_cum)
            gn = gr_ref[...] if grp == 0 else gg_ref[...]
            s_prev = s_ref[grp]
            st_ref[0, grp] = s_prev
            o_inter = _dot(f["qs"].astype(BF16), s_prev.astype(BF16), "nt")
            kmb = f["km"].astype(BF16)
            kpb = f["kp"].astype(BF16)
            vb = v.astype(BF16)
            for h in range(HEADS):
                hm = (lane // DK) == h
                a1 = _dot(jnp.where(hm, f["qp"], 0.0).astype(BF16), kmb, "nt")
                a2 = _dot(jnp.where(hm, f["qm"], 0.0).astype(BF16), kpb, "nt")
                a = jnp.where(m1, a1, jnp.where(m2, a2, 0.0))
                lo = grp * V_W + h * DV
                o_h = _dot(a.astype(BF16), vb[:, h * DV:(h + 1) * DV], "nn") + o_inter[:, h * DV:(h + 1) * DV]
                o_ref[:, lo:lo + DV] = o_h
                r = lax.rsqrt(jnp.mean(o_h * o_h, axis=-1, keepdims=True) + RMS_EPS)
                gte = gate[:, h * DV:(h + 1) * DV]
                y = o_h * r * gn[:, h * DV:(h + 1) * DV] * (gte * _sigmoid(gte))
                y_ref[:, lo:lo + DV] = y.astype(y_ref.dtype)
            upd = _dot(vb, f["kl"].astype(BF16), "tn")
            s_ref[grp] = s_prev * f["ebl"] + jnp.where(blockmask, upd, 0.0)

    const = lambda shape: pl.BlockSpec(shape, lambda i: tuple(0 for _ in shape))
    return _call(
        "attn_fwd", main, (n_s,),
        [pl.BlockSpec((SUPER, PROJ_W), lambda i: (i, 0)),
         pl.BlockSpec((SUPER, 128), lambda i: (i, 0)), pl.BlockSpec((SUPER, 128), lambda i: (i, 0)),
         const((1, QK_W)), const((GL_W, QK_W)), const((1, QK_W)), const((1, V_W)), const((1, V_W))],
        [pl.BlockSpec((SUPER, 2 * V_W), lambda i: (i, 0)), pl.BlockSpec((SUPER, 2 * V_W), lambda i: (i, 0)),
         pl.BlockSpec((1, 2, V_W, QK_W), lambda i: (i, 0, 0, 0))],
        [jax.ShapeDtypeStruct((T, 2 * V_W), F32), jax.ShapeDtypeStruct((T, 2 * V_W), BF16),
         jax.ShapeDtypeStruct((n_s, 2, V_W, QK_W), F32)],
        (proj, cos, sin_signed, lg, wa2p, ba, gn_ret, gn_gla),
        scratch=[pltpu.VMEM((2, V_W, QK_W), F32)], comm=comm)


def _attn_bwd(proj, cos, sin_signed, lg, wa2p, ba, gn_ret, gn_gla, o, dy, states, comm=None):
    T = proj.shape[0]
    n_s = T // SUPER

    def main(ins, outs, scr):
        pr_ref, cos_ref, sin_ref, lg_ref, wa2_ref, ba_ref, gr_ref, gg_ref, o_ref, dy_ref, st_ref = ins
        dp_ref, dgr_ref, dgg_ref, dba_ref, dwa_ref = outs
        (ds_ref,) = scr
        i = pl.program_id(0)

        @pl.when(i == 0)
        def _():
            ds_ref[...] = jnp.zeros_like(ds_ref)
            dgr_ref[...] = jnp.zeros_like(dgr_ref)
            dgg_ref[...] = jnp.zeros_like(dgg_ref)
            dba_ref[...] = jnp.zeros_like(dba_ref)
            dwa_ref[...] = jnp.zeros_like(dwa_ref)

        pr = pr_ref
        cos = cos_ref[...]
        sin_signed = sin_ref[...]
        row, col, same = _attn_masks()
        m1 = col <= row
        m2 = jnp.logical_and(col > row, same)
        m1t = row <= col
        m2t = jnp.logical_and(row > col, same)
        lane = lax.broadcasted_iota(jnp.int32, (1, QK_W), 1)
        blockmask = _state_block_mask()
        for grp in range(2):
            q, k, v, gate, b_cum, glow, logit = _group_inputs(grp, pr, cos, sin_signed, lg_ref[...],
                                                              wa2_ref[...], ba_ref[...])
            f = _decay_factors(q, k, b_cum)
            gn = gr_ref[...] if grp == 0 else gg_ref[...]
            dgn_ref = dgr_ref if grp == 0 else dgg_ref
            do_parts, dgate_parts, dgn_parts = [], [], []
            for h in range(HEADS):
                lo = grp * V_W + h * DV
                o_h = o_ref[:, lo:lo + DV]
                r = lax.rsqrt(jnp.mean(o_h * o_h, axis=-1, keepdims=True) + RMS_EPS)
                n = o_h * r
                gte = gate[:, h * DV:(h + 1) * DV]
                sg = _sigmoid(gte)
                dy_h = dy_ref[:, lo:lo + DV]
                gn_h = gn[:, h * DV:(h + 1) * DV]
                dgate_parts.append(dy_h * n * gn_h * (sg * (1.0 + gte * (1.0 - sg))))
                dz = dy_h * (gte * sg)
                dgn_parts.append(jnp.sum(dz * n, axis=0, keepdims=True))
                dn = dz * gn_h
                do_parts.append(r * (dn - n * jnp.mean(dn * n, axis=-1, keepdims=True)))
            dgn_ref[...] += jnp.concatenate(dgn_parts, axis=1)
            dgate = jnp.concatenate(dgate_parts, axis=1)
            do = jnp.concatenate(do_parts, axis=1)
            dob = do.astype(BF16)
            vb = v.astype(BF16)
            s_prev = st_ref[0, grp]
            ds_new = ds_ref[grp]
            dsb = ds_new.astype(BF16)
            qpb, qmb = f["qp"].astype(BF16), f["qm"].astype(BF16)
            kpb, kmb = f["kp"].astype(BF16), f["km"].astype(BF16)
            dqp = jnp.zeros((SUPER, QK_W), F32)
            dqm = jnp.zeros((SUPER, QK_W), F32)
            dkp = jnp.zeros((SUPER, QK_W), F32)
            dkm = jnp.zeros((SUPER, QK_W), F32)
            dv_parts = []
            for h in range(HEADS):
                hm = (lane // DK) == h
                qp_h = jnp.where(hm, f["qp"], 0.0).astype(BF16)
                qm_h = jnp.where(hm, f["qm"], 0.0).astype(BF16)
                kp_h = jnp.where(hm, f["kp"], 0.0).astype(BF16)
                km_h = jnp.where(hm, f["km"], 0.0).astype(BF16)
                at = jnp.where(m1t, _dot(km_h, qpb, "nt"), jnp.where(m2t, _dot(kp_h, qmb, "nt"), 0.0))
                do_h = dob[:, h * DV:(h + 1) * DV]
                v_h = vb[:, h * DV:(h + 1) * DV]
                dv_parts.append(_dot(at.astype(BF16), do_h, "nn"))
                da = _dot(do_h, v_h, "nt")
                dat = _dot(v_h, do_h, "nt")
                da1 = jnp.where(m1, da, 0.0).astype(BF16)
                da2 = jnp.where(m2, da, 0.0).astype(BF16)
                da1t = jnp.where(m1t, dat, 0.0).astype(BF16)
                da2t = jnp.where(m2t, dat, 0.0).astype(BF16)
                dqp = dqp + _dot(da1, km_h, "nn")
                dqm = dqm + _dot(da2, kp_h, "nn")
                dkm = dkm + _dot(da1t, qp_h, "nn")
                dkp = dkp + _dot(da2t, qm_h, "nn")
            klb = f["kl"].astype(BF16)
            qsb = f["qs"].astype(BF16)
            dqs = _dot(dob, s_prev.astype(BF16), "nn")
            dkl = _dot(vb, dsb, "nn")
            dv = jnp.concatenate(dv_parts, axis=1) + _dot(klb, dsb, "nt")
            ds_ref[grp] = ds_new * f["ebl"] + jnp.where(blockmask, _dot(dob, qsb, "tn"), 0.0)
            dq = dqp * f["e1"] + dqm * f["e2"] + dqs * f["eb"]
            dk = dkm * f["e2"] + dkp * f["e1"] + dkl * f["el"]
            if grp == 0:
                dq = _rotary_transposed(dq, cos, sin_signed)
                dk = _rotary_transposed(dk * (DK ** -0.5), cos, sin_signed)
                dp_ref[:, C_RQ:C_RQ + QK_W] = dq.astype(dp_ref.dtype)
                dp_ref[:, C_RK:C_RK + QK_W] = dk.astype(dp_ref.dtype)
                dp_ref[:, C_RV:C_RV + V_W] = dv.astype(dp_ref.dtype)
                dp_ref[:, C_RG:C_RG + V_W] = dgate.astype(dp_ref.dtype)
            else:
                dkl_kl = dkl * klb.astype(F32)
                db = (dqp * qpb.astype(F32) - dkm * kmb.astype(F32) - dqm * qmb.astype(F32)
                      + dkp * kpb.astype(F32) + dqs * qsb.astype(F32) - dkl_kl)
                last = (jnp.sum(dkl_kl, axis=0, keepdims=True)
                        + f["ebl"] * jnp.sum(s_prev * ds_new, axis=0, keepdims=True))
                rowq = lax.broadcasted_iota(jnp.int32, (SUPER, QK_W), 0)
                db = db + jnp.where(rowq == SUPER - 1, last, 0.0)
                upper = (col >= row).astype(F32)
                dla = _dot(upper, db, "nn", precision=lax.Precision.HIGHEST)
                dlogit = dla * (1.0 / GATE_NORM) * (1.0 - _sigmoid(logit))
                dlb = dlogit.astype(BF16)
                dglow = _dot(dlb, wa2_ref[...].astype(BF16), "nt")
                dwa_ref[...] += _dot(glow.astype(BF16), dlb, "tn")
                dba_ref[...] += jnp.sum(dlogit, axis=0, keepdims=True)
                dp_ref[:, C_GQ:C_GQ + QK_W] = (dq * (DK ** -0.5)).astype(dp_ref.dtype)
                dp_ref[:, C_GK:C_GK + QK_W] = dk.astype(dp_ref.dtype)
                dp_ref[:, C_GV:C_GV + V_W] = dv.astype(dp_ref.dtype)
                dp_ref[:, C_GG:C_GG + V_W] = dgate.astype(dp_ref.dtype)
                dp_ref[:, C_GL:C_GL + GL_W] = dglow.astype(dp_ref.dtype)

    rev = lambda i: n_s - 1 - i
    const = lambda shape: pl.BlockSpec(shape, lambda i: tuple(0 for _ in shape))
    return _call(
        "attn_bwd", main, (n_s,),
        [pl.BlockSpec((SUPER, PROJ_W), lambda i: (rev(i), 0)),
         pl.BlockSpec((SUPER, 128), lambda i: (rev(i), 0)), pl.BlockSpec((SUPER, 128), lambda i: (rev(i), 0)),
         const((1, QK_W)), const((GL_W, QK_W)), const((1, QK_W)), const((1, V_W)), const((1, V_W)),
         pl.BlockSpec((SUPER, 2 * V_W), lambda i: (rev(i), 0)),
         pl.BlockSpec((SUPER, 2 * V_W), lambda i: (rev(i), 0)),
         pl.BlockSpec((1, 2, V_W, QK_W), lambda i: (rev(i), 0, 0, 0))],
        [pl.BlockSpec((SUPER, PROJ_W), lambda i: (rev(i), 0)),
         const((1, V_W)), const((1, V_W)), const((1, QK_W)), const((GL_W, QK_W))],
        [jax.ShapeDtypeStruct((T, PROJ_W), BF16),
         jax.ShapeDtypeStruct((1, V_W), F32), jax.ShapeDtypeStruct((1, V_W), F32),
         jax.ShapeDtypeStruct((1, QK_W), F32), jax.ShapeDtypeStruct((GL_W, QK_W), F32)],
        (proj, cos, sin_signed, lg, wa2p, ba, gn_ret, gn_gla, o, dy, states),
        scratch=[pltpu.VMEM((2, V_W, QK_W), F32)], comm=comm)


def _rotary_tables(T):
    half = DK // 2
    inv = ROPE_BASE ** (-jnp.arange(half, dtype=F32) * 2.0 / DK)
    ang = jnp.arange(T, dtype=F32)[:, None] * inv[None, :]
    cos, sin = jnp.cos(ang), jnp.sin(ang)
    cos_head = jnp.concatenate([cos, cos], axis=1)
    sin_head = jnp.concatenate([-sin, sin], axis=1)
    return jnp.tile(cos_head, (1, 128 // DK)), jnp.tile(sin_head, (1, 128 // DK))


def _sum_devices(name, gathered, m_per):
    def body(g_ref, o_ref):
        acc = g_ref[0:m_per, :]
        for k in range(1, N_DEV):
            acc = acc + g_ref[k * m_per:(k + 1) * m_per, :]
        o_ref[...] = acc

    return pl.pallas_call(body, name=name, out_shape=jax.ShapeDtypeStruct((m_per, 128), F32))(gathered)


def _pair_sum(name, grad, landed, core):
    R, C = grad.shape[2:]

    def main(ins, outs, scr):
        outs[0][...] = (ins[0][...].astype(F32) + ins[1][...].astype(F32)).astype(outs[0].dtype)

    blk = pl.BlockSpec((None, R, C), lambda j, s: (j, 0, 0))
    (out,), _ = _call(name, main, (4,), [pl.BlockSpec((None, None, R, C), lambda j, s: (j, s[0], 0, 0)), blk], [blk],
                      [jax.ShapeDtypeStruct((4, R, C), BF16)], (grad, landed), prefetch=core)
    return out


def _owner_sums(name, items, owner, comm=None):
    counts = [1 + len(landed) for _, landed in items]

    def main(ins, outs, scr):
        at = 0
        for o_ref, n in zip(outs, counts):
            acc = ins[at][...].astype(F32)
            for l_ref in ins[at + 1:at + n]:
                for j in range(l_ref.shape[0]):
                    acc = acc + l_ref[j].astype(F32)
            o_ref[...] = acc
            at += n

    once = pl.Buffered(1)
    in_specs, out_specs, out_shape, args = [], [], [], []
    for grad, landed in items:
        R, C = grad.shape[-2:]
        if grad.ndim == 4:
            in_specs.append(pl.BlockSpec((None, None, R, C), lambda i, s: (s[0], s[1], 0, 0), pipeline_mode=once))
        else:
            in_specs.append(pl.BlockSpec((None, R, C), lambda i, s: (s[0], 0, 0), pipeline_mode=once))
        in_specs += [pl.BlockSpec(tuple(l.shape), lambda i, s: (0, 0, 0), pipeline_mode=once) for l in landed]
        out_specs.append(pl.BlockSpec((R, C), lambda i, s: (0, 0)))
        out_shape.append(jax.ShapeDtypeStruct((R, C), F32))
        args += [grad, *landed]
    return _call(name, main, (1,), in_specs, out_specs, out_shape, args, comm=comm, prefetch=owner)


def _rms_bwd(name, x, g, dh, dres, out_scale):
    T, D = x.shape
    tm = min(T, 512)
    ep = _rms_bwd_epilogue(out_scale)

    def main(ins, outs, scr):
        x_ref, g_ref, dh_ref, dres_ref = ins
        ep(dh_ref[...], (x_ref, g_ref, dres_ref), outs)

    tile = pl.BlockSpec((tm, D), lambda i: (i, 0))
    vec = pl.BlockSpec((1, D), lambda i: (0, 0))
    outs, _ = _call(name, main, (T // tm,), [tile, vec, tile, tile], [tile, tile, vec],
                    [jax.ShapeDtypeStruct((T, D), F32), jax.ShapeDtypeStruct((T, D), BF16),
                     jax.ShapeDtypeStruct((1, D), F32)], (x, g, dh, dres))
    return outs


def _adamw_group(name, items, n_blocks, comm=None):
    n = len(items)

    def main(ins, outs, scr):
        for p in range(n):
            g_ref, w_ref, m_ref, v_ref = ins[4 * p:4 * p + 4]
            d_ref, nm_ref, nv_ref = outs[3 * p:3 * p + 3]
            gv = g_ref[...]
            nm = ADAM_B1 * m_ref[...] + (1.0 - ADAM_B1) * gv
            nv = ADAM_B2 * v_ref[...] + (1.0 - ADAM_B2) * (gv * gv)
            m_hat = nm / (1.0 - ADAM_B1 ** ADAM_STEP)
            v_hat = nv / (1.0 - ADAM_B2 ** ADAM_STEP)
            d_ref[...] = -ADAM_LR * (m_hat / (jnp.sqrt(v_hat) + ADAM_EPS) + ADAM_WD * w_ref[...])
            nm_ref[...] = nm
            nv_ref[...] = nv

    in_specs, out_specs, out_shape, args = [], [], [], []
    for item in items:
        R, C = item[1].shape
        assert R % n_blocks == 0
        spec = pl.BlockSpec((R // n_blocks, C), lambda i: (i, 0))
        in_specs += [spec] * 4
        out_specs += [spec] * 3
        out_shape += [jax.ShapeDtypeStruct((R, C), F32)] * 3
        args += list(item)
    outs, extra = _call(name, main, (n_blocks,), in_specs, out_specs, out_shape, args, comm=comm)
    return [tuple(outs[3 * p:3 * p + 3]) for p in range(n)], extra


SMALL_ORDER = ("ffn1", "mix", "ffn2", "final", "ret", "gla", "b_a")


def kernel(x, ffn1_norm_g, ffn1_w_gate, ffn1_w_up, ffn1_w_down, mix_norm_g, w_in, ret_norm_g, gla_w_a2, gla_b_a, gla_norm_g, w_out, ffn2_norm_g, ffn2_w_gate, ffn2_w_up, ffn2_w_down, final_norm_g, loss_target, m_ffn1_norm_g, m_ffn1_w_gate, m_ffn1_w_up, m_ffn1_w_down, m_mix_norm_g, m_w_in, m_ret_norm_g, m_gla_w_a2, m_gla_b_a, m_gla_norm_g, m_w_out, m_ffn2_norm_g, m_ffn2_w_gate, m_ffn2_w_up, m_ffn2_w_down, m_final_norm_g, v_ffn1_norm_g, v_ffn1_w_gate, v_ffn1_w_up, v_ffn1_w_down, v_mix_norm_g, v_w_in, v_ret_norm_g, v_gla_w_a2, v_gla_b_a, v_gla_norm_g, v_w_out, v_ffn2_norm_g, v_ffn2_w_gate, v_ffn2_w_up, v_ffn2_w_down, v_final_norm_g):
    xi, yi, ci = _coords()
    dev = 4 * xi + 2 * yi + ci
    owner = jnp.stack([2 * xi + yi, ci]).astype(jnp.int32)

    x0, target = x[0], loss_target[0]
    T, D = x0.shape
    fb = ffn1_w_gate.shape[2]
    ib = w_in.shape[2]
    ab = gla_w_a2.shape[2]
    F = N_DEV * fb
    cos, sin_signed = _rotary_tables(T)
    lg = jnp.repeat(jnp.log(1.0 - 2.0 ** (-5.0 - jnp.arange(HEADS, dtype=F32))), DK)[None, :]
    g_final = final_norm_g.reshape(1, D)

    g1_loc = ffn1_w_gate[0].T[None].astype(BF16)
    u1_loc = ffn1_w_up[0].T[None].astype(BF16)
    d1_loc = ffn1_w_down.astype(BF16)
    g2_loc = ffn2_w_gate[0].T[None].astype(BF16)
    u2_loc = ffn2_w_up[0].T[None].astype(BF16)
    d2_loc = ffn2_w_down.astype(BF16)
    in_loc = w_in[0].T.astype(BF16)
    out_loc = w_out[0].astype(BF16)

    h1, (g1,) = _rms_fwd("ffn1_rms", x0, ffn1_norm_g, comm=_AllGather([g1_loc], ["stack"]))
    g1 = g1.reshape(1, F, D)
    (gate1,), (u1,) = _mm_nstream("ffn1_gate", h1, [g1], [0], "nt", [], [BF16], _identity_epilogue, cn=256,
                                  comm=_AllGather([u1_loc], ["stack"]))
    u1 = u1.reshape(1, F, D)
    (up1, act1), (d1,) = _mm_nstream("ffn1_up", h1, [u1], [0], "nt", [gate1], [BF16, BF16], _up_act_epilogue, cn=256,
                                     comm=_AllGather([d1_loc], ["stack"]))
    d1 = d1.reshape(1, F, D)
    f32_tile, bf16_tile, f32_vec = (F32, "tile"), (BF16, "tile"), (F32, "vec")
    (x1, h2), (in_all, a_all) = _mm_mstream(
        "ffn1_down", [act1], [d1], [0], "nn", [(x0, "tile"), (mix_norm_g, "vec")], [f32_tile, bf16_tile],
        _residual_rms_epilogue(0.5), comm=_AllGather([in_loc, gla_w_a2[0]], ["plain", "plain"]))
    w_in_t = jnp.pad(in_all.reshape(1, N_DEV * ib, D), ((0, 0), (0, PROJ_W - N_DEV * ib), (0, 0)))
    wa2 = jnp.transpose(a_all, (1, 0, 2)).reshape(GATE_RANK, N_DEV * ab)
    wa2p = jnp.pad(wa2, ((0, GL_W - GATE_RANK), (0, 0)))

    (proj,), (g2,) = _mm_nstream("mix_proj", h2, [w_in_t], [0], "nt", [], [F32], _identity_epilogue, cn=640,
                                 comm=_AllGather([g2_loc], ["stack"]))
    (o, ymix, states), (u2, out_all) = _attn_fwd(proj, cos, sin_signed, lg, wa2p, gla_b_a, ret_norm_g, gla_norm_g,
                                                 comm=_AllGather([u2_loc, out_loc], ["stack", "plain"]))
    w_out_full = out_all.reshape(D, D)
    (x2, h3), _ = _mm_mstream("mix_out", [ymix], [w_out_full], [0], "nn", [(x1, "tile"), (ffn2_norm_g, "vec")],
                              [f32_tile, bf16_tile], _residual_rms_epilogue(1.0))
    g2, u2 = g2.reshape(1, F, D), u2.reshape(1, F, D)

    (gate2, up2, act2), (d2,) = _mm_nstream(
        "ffn2_up", h3, [g2, u2], [0, 0], "nt", [], [BF16, BF16, BF16], _silu_mul_epilogue, cn=256,
        comm=_AllGather([d2_loc], ["stack"]))
    d2 = d2.reshape(1, F, D)
    (x3,), _ = _mm_mstream("ffn2_down", [act2], [d2], [0], "nn", [(x2, "tile")], [f32_tile], _residual_epilogue(0.5))

    dx3, dy3b, d_final, loss = _final_loss_bwd("final", x3, g_final, target, 0.5)

    (dgate2, dup2), _ = _mm_nstream("ffn2b_dact", dy3b, [d2], [0], "nt", [gate2, up2], [BF16, BF16],
                                    _dact_epilogue, cn=256)
    dwg2, _ = _mm_tn("ffn2b_dwg", dgate2, h3, F // 2, D, BF16)
    dwg2 = dwg2.reshape(4, 2, fb, D)
    dwu2, (l_wg2_near,) = _mm_tn("ffn2b_dwu", dup2, h3, F // 2, D, BF16, comm=_ReduceScatter([(dwg2, NEAR)]))
    dwu2 = dwu2.reshape(4, 2, fb, D)
    dwd2, (l_wg2_far,) = _mm_tn("ffn2b_dwd", act2, dy3b, F // 2, D, BF16, comm=_ReduceScatter([(dwg2, FAR)]))
    dwd2 = dwd2.reshape(4, 2, fb, D)
    rms_outs = [f32_tile, bf16_tile, f32_vec]
    (dx2, dx2b, d_g2), (l_wu2,) = _mm_mstream(
        "ffn2b_dh", [dgate2, dup2], [g2, u2], [0, 0], "nn", [(x2, "tile"), (ffn2_norm_g, "vec"), (dx3, "tile")],
        rms_outs, _rms_bwd_epilogue(1.0), comm=_ReduceScatter([(dwu2, ALL)]))

    (dymix,), _ = _mm_mstream("mixb_dy", [dx2b], [w_out_full], [0], "nt", [], [f32_tile], _plain_epilogue)
    dwout, _ = _mm_tn("mixb_dwout", ymix, dx2b, D, D, BF16)
    dwout = dwout.reshape(4, 2, D // N_DEV, D)
    (dproj, d_ret, d_gla, d_ba, d_wa2p), (l_wd2, l_wout) = _attn_bwd(
        proj, cos, sin_signed, lg, wa2p, gla_b_a, ret_norm_g, gla_norm_g, o, dymix, states,
        comm=_ReduceScatter([(dwd2, ALL), (dwout, ALL)]))
    dwin_t, _ = _mm_tn("mixb_dwin", dproj, h2, 640, D, BF16)
    dwin = dwin_t[:N_DEV * ib].reshape(4, 2, ib, D)
    (dx1, dy1b, d_gmix), (l_win_near,) = _mm_mstream(
        "mixb_dh", [dproj], [w_in_t], [0], "nn", [(x1, "tile"), (mix_norm_g, "vec"), (dx2, "tile")],
        rms_outs, _rms_bwd_epilogue(0.5), comm=_ReduceScatter([(dwin, NEAR)]))

    (dgate1, dup1), (l_win_far,) = _mm_nstream("ffn1b_dact", dy1b, [d1], [0], "nt", [gate1, up1], [BF16, BF16],
                                               _dact_epilogue, cn=256, comm=_ReduceScatter([(dwin, FAR)]))
    dwg1, _ = _mm_tn("ffn1b_dwg", dgate1, h1, F // 2, D, BF16)
    dwg1 = dwg1.reshape(4, 2, fb, D)
    core = owner[1:2]
    dwu1, (s_wg1,) = _mm_tn("ffn1b_dwu", dup1, h1, F // 2, D, BF16, comm=_SiblingExchange([dwg1]))
    dwu1 = dwu1.reshape(4, 2, fb, D)
    p_wg1 = _pair_sum("pair_wg1", dwg1, s_wg1, core)
    dwd1, (c_wg1, s_wu1) = _mm_tn("ffn1b_dwd", act1, dy1b, F // 2, D, BF16,
                                  comm=_Multi([_ChipExchange([p_wg1]), _SiblingExchange([dwu1])]))
    dwd1 = dwd1.reshape(4, 2, fb, D)
    p_wu1 = _pair_sum("pair_wu1", dwu1, s_wu1, core)
    (dh1,), (c_wu1, s_wd1) = _mm_mstream(
        "ffn1b_dh", [dgate1, dup1], [g1, u1], [0, 0], "nn", [], [f32_tile], _plain_epilogue,
        comm=_Multi([_ChipExchange([p_wu1]), _SiblingExchange([dwd1])]))
    p_wd1 = _pair_sum("pair_wd1", dwd1, s_wd1, core)
    dx0, _, d_g1 = _rms_bwd("ffn1b_rms", x0, ffn1_norm_g, dh1, dx1, 1.0)

    small = dict(ffn1=d_g1, mix=d_gmix, ffn2=d_g2, final=d_final, ret=d_ret, gla=d_gla, b_a=d_ba)
    flat = jnp.concatenate([small[k].reshape(-1) for k in SMALL_ORDER]
                           + [d_wa2p[:GATE_RANK].reshape(-1), loss[0]])
    rows = -(-flat.shape[0] // 128)
    rows = -(-rows // 8) * 8
    packed = jnp.pad(flat, (0, rows * 128 - flat.shape[0])).reshape(rows, 128)

    def adam_item(nm, grad, weight, mom, var):
        shape = weight.shape
        two_d = (1, shape[0]) if len(shape) == 1 else shape[-2:]
        return tuple(a.reshape(two_d) for a in (grad, weight, mom, var))

    sums_a, (c_wd1, gathered) = _owner_sums(
        "sum_a", [(dwg2, [l_wg2_near, l_wg2_far]), (dwu2, [l_wu2]), (dwd2, [l_wd2]),
                  (dwin, [l_win_near, l_win_far]), (dwout, [l_wout])], owner,
        comm=_Multi([_ChipExchange([p_wd1]), _AllGather([packed], ["plain"])]))
    sums_b, _ = _owner_sums("sum_b", [(p_wg1, [c_wg1]), (p_wu1, [c_wu1]), (p_wd1, [c_wd1])], owner)
    big_grads = {"ffn2_w_gate": sums_a[0].T, "ffn2_w_up": sums_a[1].T, "ffn2_w_down": sums_a[2], "w_in": sums_a[3].T,
                 "w_out": sums_a[4], "ffn1_w_gate": sums_b[0].T, "ffn1_w_up": sums_b[1].T, "ffn1_w_down": sums_b[2]}
    params = dict(
        ffn2_w_gate=(ffn2_w_gate, m_ffn2_w_gate, v_ffn2_w_gate), ffn2_w_up=(ffn2_w_up, m_ffn2_w_up, v_ffn2_w_up),
        ffn2_w_down=(ffn2_w_down, m_ffn2_w_down, v_ffn2_w_down), w_in=(w_in, m_w_in, v_w_in),
        w_out=(w_out, m_w_out, v_w_out), ffn1_w_gate=(ffn1_w_gate, m_ffn1_w_gate, v_ffn1_w_gate),
        ffn1_w_up=(ffn1_w_up, m_ffn1_w_up, v_ffn1_w_up), ffn1_w_down=(ffn1_w_down, m_ffn1_w_down, v_ffn1_w_down),
        ffn1_norm_g=(ffn1_norm_g, m_ffn1_norm_g, v_ffn1_norm_g), mix_norm_g=(mix_norm_g, m_mix_norm_g, v_mix_norm_g),
        ret_norm_g=(ret_norm_g, m_ret_norm_g, v_ret_norm_g), gla_w_a2=(gla_w_a2, m_gla_w_a2, v_gla_w_a2),
        gla_b_a=(gla_b_a, m_gla_b_a, v_gla_b_a), gla_norm_g=(gla_norm_g, m_gla_norm_g, v_gla_norm_g),
        ffn2_norm_g=(ffn2_norm_g, m_ffn2_norm_g, v_ffn2_norm_g), final_norm_g=(final_norm_g, m_final_norm_g, v_final_norm_g))
    grads, updates = {}, {}

    def run_adam(name, names, grad_of, n_blocks):
        for nm in names:
            grads[nm] = grad_of[nm].reshape(params[nm][0].shape)
        res, _ = _adamw_group(name, [adam_item(nm, grads[nm], *params[nm]) for nm in names], n_blocks)
        for nm, r in zip(names, res):
            updates[nm] = tuple(a.reshape(params[nm][0].shape) for a in r)

    run_adam("adamw_big", list(big_grads), big_grads, 4)

    total = _sum_devices("sum_small", gathered.reshape(N_DEV * rows, 128), rows).reshape(-1)
    sizes = [small[k].size for k in SMALL_ORDER] + [GATE_RANK * QK_W, 128]
    offs = [0]
    for s in sizes:
        offs.append(offs[-1] + s)
    pieces = [total[offs[i]:offs[i + 1]] for i in range(len(sizes))]
    g_small = {k: pieces[i].reshape(small[k].shape) for i, k in enumerate(SMALL_ORDER)}
    g_wa2_full = pieces[len(SMALL_ORDER)].reshape(GATE_RANK, QK_W)
    g_wa2 = lax.dynamic_slice(g_wa2_full, (0, dev * ab), (GATE_RANK, ab))
    loss_total = pieces[len(SMALL_ORDER) + 1][0]

    small_grads = {"ffn1_norm_g": g_small["ffn1"], "mix_norm_g": g_small["mix"], "ret_norm_g": g_small["ret"],
                   "gla_w_a2": g_wa2, "gla_b_a": g_small["b_a"], "gla_norm_g": g_small["gla"],
                   "ffn2_norm_g": g_small["ffn2"], "final_norm_g": g_small["final"]}
    run_adam("adamw_small", list(small_grads), small_grads, 1)

    order = ("ffn1_norm_g", "ffn1_w_gate", "ffn1_w_up", "ffn1_w_down", "mix_norm_g", "w_in", "ret_norm_g", "gla_w_a2",
             "gla_b_a", "gla_norm_g", "w_out", "ffn2_norm_g", "ffn2_w_gate", "ffn2_w_up", "ffn2_w_down", "final_norm_g")
    return (loss_total, dx0[None], *[grads[nm] for nm in order], *[updates[nm][0] for nm in order],
            *[updates[nm][1] for nm in order], *[updates[nm][2] for nm in order])
```

```python
import functools
import math

import jax
import jax.numpy as jnp
from jax import lax
from jax.experimental import pallas as pl
from jax.experimental.pallas import tpu as pltpu

F32 = jnp.float32
BF16 = jnp.bfloat16
MESH = pl.DeviceIdType.MESH
HBM = pl.BlockSpec(memory_space=pltpu.HBM)

N_DEV = 8
RMS_EPS = 1e-6
ROPE_BASE = 10000.0
HEADS = 4
DK = 64
DV = 128
QK_W = HEADS * DK
V_W = HEADS * DV
GATE_RANK = 16
GATE_NORM = 16.0
CHUNK = 64
SUPER = 256
PROJ_W = 3200
C_RQ, C_RK, C_RV, C_RG, C_GQ, C_GK, C_GV, C_GG, C_GL = 0, 256, 512, 1024, 1536, 1792, 2048, 2560, 3072
GL_W = PROJ_W - C_GL
ADAM_LR, ADAM_B1, ADAM_B2, ADAM_EPS, ADAM_WD, ADAM_STEP = 0.001, 0.9, 0.999, 1e-08, 0.01, 10
VMEM_LIMIT_V7X = 52 * 1024 * 1024


def _cparams(**kw):
    return pltpu.CompilerParams(vmem_limit_bytes=VMEM_LIMIT_V7X, **kw)


def _dot(a, b, form, precision=None):
    dims = {"nn": (((1,), (0,)), ((), ())), "nt": (((1,), (1,)), ((), ())), "tn": (((0,), (0,)), ((), ()))}[form]
    return lax.dot_general(a, b, dims, preferred_element_type=F32, precision=precision)


def _sigmoid(x):
    return 1.0 / (1.0 + jnp.exp(-x))


def _coords():
    return lax.axis_index("x"), lax.axis_index("y"), lax.axis_index("c")


class _NoComm:
    inputs, out_shapes, scratch = (), (), ()


class _AllGather:
    def __init__(self, arrays, kinds):
        self.inputs = tuple(arrays)
        self.kinds = tuple(kinds)
        n = len(arrays)
        self.out_shapes = tuple(
            jax.ShapeDtypeStruct((a.shape[0], N_DEV) + a.shape[1:] if k == "stack" else (N_DEV,) + a.shape, a.dtype)
            for a, k in zip(arrays, kinds))
        self.scratch = (pltpu.SemaphoreType.DMA((n, 7)), pltpu.SemaphoreType.DMA((n, 7)),
                        pltpu.SemaphoreType.DMA((n,)))

    def _ctx(self, srcs, outs, sems):
        send_sems, recv_sems, local_sems = sems
        x, y, c = _coords()
        me, sibling = (x, y, c), (x, y, 1 - c)
        chips = [(1 - x, y), (x, 1 - y), (1 - x, 1 - y)]

        def blk(m, dev):
            k = 4 * dev[0] + 2 * dev[1] + dev[2]
            return outs[m].at[:, k] if self.kinds[m] == "stack" else outs[m].at[k]

        def copy(m, s, block, to, src=None):
            return pltpu.make_async_remote_copy(
                src_ref=blk(m, block) if src is None else src, dst_ref=blk(m, block),
                send_sem=send_sems.at[m, s], recv_sem=recv_sems.at[m, s], device_id=to, device_id_type=MESH)

        def mine(m):
            return pltpu.make_async_copy(srcs[m], blk(m, me), local_sems.at[m])

        def first(m):
            return [copy(m, 0, me, sibling, src=srcs[m])] + [
                copy(m, 1 + j, me, (*chip, c), src=srcs[m]) for j, chip in enumerate(chips)]

        return me, sibling, chips, c, copy, mine, first

    def start(self, srcs, outs, sems):
        me, sibling, chips, c, copy, mine, first = self._ctx(srcs, outs, sems)
        for m in range(len(srcs)):
            mine(m).start()
            for cp in first(m):
                cp.start()

    def mid(self, srcs, outs, sems):
        me, sibling, chips, c, copy, mine, first = self._ctx(srcs, outs, sems)
        for j, chip in enumerate(chips):
            for m in range(len(srcs)):
                copy(m, 1 + j, (*chip, c), me).wait_recv()
                copy(m, 4 + j, (*chip, c), sibling).start()

    def finish(self, srcs, outs, sems):
        me, sibling, chips, c, copy, mine, first = self._ctx(srcs, outs, sems)
        for m in range(len(srcs)):
            copy(m, 0, sibling, me).wait_recv()
            for j, chip in enumerate(chips):
                copy(m, 4 + j, (*chip, 1 - c), me).wait_recv()
            for cp in first(m):
                cp.wait_send()
            for j, chip in enumerate(chips):
                copy(m, 4 + j, (*chip, c), sibling).wait_send()
            mine(m).wait()


RELATIONS = ((0, 0, 1), (1, 0, 0), (0, 1, 0), (1, 1, 0), (1, 0, 1), (0, 1, 1), (1, 1, 1))
NEAR = (0, 1, 2, 4, 5)
FAR = (3, 6)
ALL = NEAR + FAR


class _ReduceScatter:
    def __init__(self, parts):
        self.inputs = tuple(g for g, _ in parts)
        self.slots = tuple(s for _, s in parts)
        self.out_shapes = tuple(jax.ShapeDtypeStruct((len(s),) + g.shape[2:], g.dtype) for g, s in parts)
        n_max = max(len(s) for s in self.slots)
        n = len(parts)
        self.scratch = (pltpu.SemaphoreType.DMA((n, n_max)), pltpu.SemaphoreType.DMA((n, n_max)))

    def _copies(self, srcs, outs, sems):
        send_sems, recv_sems = sems
        x, y, c = _coords()
        copies = []
        for m, slots in enumerate(self.slots):
            for i, s in enumerate(slots):
                fx, fy, fc = RELATIONS[s]
                px = 1 - x if fx else x
                py = 1 - y if fy else y
                pc = 1 - c if fc else c
                copies.append(pltpu.make_async_remote_copy(
                    src_ref=srcs[m].at[2 * px + py, pc], dst_ref=outs[m].at[i], send_sem=send_sems.at[m, i],
                    recv_sem=recv_sems.at[m, i], device_id=(px, py, pc), device_id_type=MESH))
        return copies

    def start(self, srcs, outs, sems):
        for cp in self._copies(srcs, outs, sems):
            cp.start()

    def mid(self, srcs, outs, sems):
        pass

    def finish(self, srcs, outs, sems):
        for cp in self._copies(srcs, outs, sems):
            cp.wait()


class _SiblingExchange:
    def __init__(self, grads):
        self.inputs = tuple(grads)
        self.out_shapes = tuple(jax.ShapeDtypeStruct((4,) + g.shape[2:], g.dtype) for g in grads)
        self.scratch = (pltpu.SemaphoreType.DMA((len(grads),)), pltpu.SemaphoreType.DMA((len(grads),)))

    def _copies(self, srcs, outs, sems):
        send_sems, recv_sems = sems
        x, y, c = _coords()
        return [pltpu.make_async_remote_copy(
            src_ref=srcs[m].at[:, 1 - c], dst_ref=outs[m], send_sem=send_sems.at[m], recv_sem=recv_sems.at[m],
            device_id=(x, y, 1 - c), device_id_type=MESH) for m in range(len(srcs))]

    def start(self, srcs, outs, sems):
        for cp in self._copies(srcs, outs, sems):
            cp.start()

    def mid(self, srcs, outs, sems):
        pass

    def finish(self, srcs, outs, sems):
        for cp in self._copies(srcs, outs, sems):
            cp.wait()


class _ChipExchange:
    def __init__(self, partials):
        self.inputs = tuple(partials)
        self.out_shapes = tuple(jax.ShapeDtypeStruct((3,) + p.shape[1:], p.dtype) for p in partials)
        n = len(partials)
        self.scratch = (pltpu.SemaphoreType.DMA((n, 3)), pltpu.SemaphoreType.DMA((n, 3)))

    def _copies(self, srcs, outs, sems):
        send_sems, recv_sems = sems
        x, y, c = _coords()
        copies = []
        for m in range(len(srcs)):
            for j, (px, py) in enumerate([(1 - x, y), (x, 1 - y), (1 - x, 1 - y)]):
                copies.append(pltpu.make_async_remote_copy(
                    src_ref=srcs[m].at[2 * px + py], dst_ref=outs[m].at[j], send_sem=send_sems.at[m, j],
                    recv_sem=recv_sems.at[m, j], device_id=(px, py, c), device_id_type=MESH))
        return copies

    def start(self, srcs, outs, sems):
        for cp in self._copies(srcs, outs, sems):
            cp.start()

    def mid(self, srcs, outs, sems):
        pass

    def finish(self, srcs, outs, sems):
        for cp in self._copies(srcs, outs, sems):
            cp.wait()


class _Multi:
    def __init__(self, comms):
        self.comms = comms
        self.inputs = tuple(a for c in comms for a in c.inputs)
        self.out_shapes = tuple(s for c in comms for s in c.out_shapes)
        self.scratch = tuple(s for c in comms for s in c.scratch)

    def _each(self, phase, srcs, outs, sems):
        i = o = k = 0
        for c in self.comms:
            ni, no, nk = len(c.inputs), len(c.out_shapes), len(c.scratch)
            getattr(c, phase)(srcs[i:i + ni], outs[o:o + no], sems[k:k + nk])
            i, o, k = i + ni, o + no, k + nk

    def start(self, srcs, outs, sems):
        self._each("start", srcs, outs, sems)

    def mid(self, srcs, outs, sems):
        self._each("mid", srcs, outs, sems)

    def finish(self, srcs, outs, sems):
        self._each("finish", srcs, outs, sems)


def _call(name, main, grid, in_specs, out_specs, out_shape, args, scratch=(), comm=None, prefetch=None):
    comm = comm or _NoComm()
    counts = [len(in_specs), len(comm.inputs), len(out_shape), len(comm.out_shapes), len(scratch), len(comm.scratch)]
    n_steps = math.prod(grid)
    hosted = bool(comm.inputs)

    def body(*refs):
        if prefetch is not None:
            refs = refs[1:]
        parts, at = [], 0
        for n in counts:
            parts.append(refs[at:at + n])
            at += n
        ins, c_in, outs, c_out, scr, c_scr = parts
        step = pl.program_id(0)
        for d in range(1, len(grid)):
            step = step * grid[d] + pl.program_id(d)
        if hosted:
            @pl.when(step == 0)
            def _():
                comm.start(c_in, c_out, c_scr)
        main(ins, outs, scr)
        if hosted:
            @pl.when(step == max(n_steps - 2, 0))
            def _():
                comm.mid(c_in, c_out, c_scr)

            @pl.when(step == n_steps - 1)
            def _():
                comm.finish(c_in, c_out, c_scr)

    all_in = list(in_specs) + [HBM] * counts[1]
    all_out = list(out_specs) + [HBM] * counts[3]
    all_scratch = list(scratch) + list(comm.scratch)
    shapes = list(out_shape) + list(comm.out_shapes)
    if prefetch is None:
        res = pl.pallas_call(body, name=name, grid=grid, in_specs=all_in, out_specs=all_out, out_shape=shapes,
                             scratch_shapes=all_scratch, compiler_params=_cparams())(*args, *comm.inputs)
    else:
        res = pl.pallas_call(
            body, name=name, out_shape=shapes,
            grid_spec=pltpu.PrefetchScalarGridSpec(num_scalar_prefetch=1, grid=grid, in_specs=all_in,
                                                   out_specs=all_out, scratch_shapes=all_scratch),
            compiler_params=_cparams())(prefetch, *args, *comm.inputs)
    return res[:counts[2]], res[counts[2]:]


def _rms_fwd(name, x, g, comm=None):
    T, D = x.shape
    tm = min(T, 512)

    def main(ins, outs, scr):
        x_ref, g_ref = ins
        xv = x_ref[...]
        r = lax.rsqrt(jnp.mean(xv * xv, axis=-1, keepdims=True) + RMS_EPS)
        outs[0][...] = (xv * r * g_ref[...]).astype(outs[0].dtype)

    tile = pl.BlockSpec((tm, D), lambda i: (i, 0))
    (h,), extra = _call(name, main, (T // tm,), [tile, pl.BlockSpec((1, D), lambda i: (0, 0))], [tile],
                        [jax.ShapeDtypeStruct((T, D), BF16)], (x, g), comm=comm)
    return h, extra


def _final_loss_bwd(name, x, g, target, out_scale):
    T, D = x.shape
    tm = min(T, 512)

    def main(ins, outs, scr):
        x_ref, g_ref, t_ref = ins
        dx_ref, dxb_ref, dg_ref, loss_ref = outs
        i = pl.program_id(0)
        xv = x_ref[...]
        r = lax.rsqrt(jnp.mean(xv * xv, axis=-1, keepdims=True) + RMS_EPS)
        xhat = xv * r
        err = xhat * g_ref[...] - t_ref[...]

        @pl.when(i == 0)
        def _():
            dg_ref[...] = jnp.zeros_like(dg_ref)
            loss_ref[...] = jnp.zeros_like(loss_ref)

        loss_ref[...] += jnp.broadcast_to(jnp.sum(err * err) * (0.5 / D), loss_ref.shape)
        dy = err * (1.0 / D)
        dg_ref[...] += jnp.sum(dy * xhat, axis=0, keepdims=True)
        dxhat = dy * g_ref[...]
        dx = r * (dxhat - xhat * jnp.mean(dxhat * xhat, axis=-1, keepdims=True))
        dx_ref[...] = dx
        dxb_ref[...] = (out_scale * dx).astype(dxb_ref.dtype)

    tile = pl.BlockSpec((tm, D), lambda i: (i, 0))
    vec = pl.BlockSpec((1, D), lambda i: (0, 0))
    outs, _ = _call(name, main, (T // tm,), [tile, vec, tile],
                    [tile, tile, vec, pl.BlockSpec((1, 128), lambda i: (0, 0))],
                    [jax.ShapeDtypeStruct((T, D), F32), jax.ShapeDtypeStruct((T, D), BF16),
                     jax.ShapeDtypeStruct((1, D), F32), jax.ShapeDtypeStruct((1, 128), F32)], (x, g, target))
    return outs


def _mm_nstream(name, a, ws, w_sel, w_form, comps, out_dtypes, epilogue, cn, rows=1024, comm=None):
    T, K = a.shape
    N = ws[0].shape[1]
    rows = min(rows, T)
    assert N % cn == 0 and T % rows == 0
    n_w, n_c = len(ws), len(comps)

    def main(ins, outs, scr):
        a_ref = ins[0]
        w_refs = ins[1:1 + n_w]
        c_refs = ins[1 + n_w:]

        for r in range(T // rows):
            sl = slice(r * rows, (r + 1) * rows)
            a_blk = a_ref[sl, :]
            dots = [_dot(a_blk, w_ref[...], w_form) for w_ref in w_refs]
            res = epilogue(dots, [c_ref[sl, :] for c_ref in c_refs])
            for o_ref, o in zip(outs, res):
                o_ref[sl, :] = o.astype(o_ref.dtype)

    if w_form == "nt":
        w_specs = [pl.BlockSpec((None, cn, K), functools.partial(lambda j, s: (s, j, 0), s=s)) for s in w_sel]
    else:
        w_specs = [pl.BlockSpec((K, cn), lambda j: (0, j)) for _ in ws]
    chunk = pl.BlockSpec((T, cn), lambda j: (0, j))
    return _call(name, main, (N // cn,), [pl.BlockSpec((T, K), lambda j: (0, 0))] + w_specs + [chunk] * n_c,
                 [chunk] * len(out_dtypes), [jax.ShapeDtypeStruct((T, N), dt) for dt in out_dtypes],
                 (a, *ws, *comps), comm=comm)


def _mm_mstream(name, as_, ws, w_sel, w_form, extras, outs_desc, epilogue, tm=512, comm=None):
    T = as_[0].shape[0]
    tm = min(tm, T)
    n_a = len(as_)
    w_shapes = [w.shape[-2:] for w in ws]
    N = w_shapes[0][1] if w_form == "nn" else w_shapes[0][0]

    def main(ins, outs, scr):
        a_refs = ins[:n_a]
        w_refs = ins[n_a:2 * n_a]
        acc = None
        for a_ref, w_ref in zip(a_refs, w_refs):
            d = _dot(a_ref[...], w_ref[...], w_form)
            acc = d if acc is None else acc + d
        epilogue(acc, ins[2 * n_a:], outs)

    kind_spec = {"tile": pl.BlockSpec((tm, N), lambda i: (i, 0)), "vec": pl.BlockSpec((1, N), lambda i: (0, 0))}
    kind_shape = {"tile": (T, N), "vec": (1, N)}
    a_specs = [pl.BlockSpec((tm, a.shape[1]), lambda i: (i, 0)) for a in as_]
    w_specs = []
    for w, s in zip(ws, w_sel):
        if w.ndim == 3:
            w_specs.append(pl.BlockSpec((None,) + tuple(w.shape[1:]), functools.partial(lambda i, s: (s, 0, 0), s=s),
                                        pipeline_mode=pl.Buffered(1)))
        else:
            w_specs.append(pl.BlockSpec(tuple(w.shape), lambda i: (0, 0), pipeline_mode=pl.Buffered(1)))
    args = list(as_) + list(ws) + [e for e, _ in extras]
    return _call(name, main, (T // tm,), a_specs + w_specs + [kind_spec[k] for _, k in extras],
                 [kind_spec[k] for _, k in outs_desc],
                 [jax.ShapeDtypeStruct(kind_shape[k], dt) for dt, k in outs_desc], args, comm=comm)


def _plain_epilogue(acc, ex, outs):
    outs[0][...] = acc.astype(outs[0].dtype)


def _residual_epilogue(scale):
    def ep(acc, ex, outs):
        outs[0][...] = ex[0][...] + scale * acc
    return ep


def _residual_rms_epilogue(scale):
    def ep(acc, ex, outs):
        xv = ex[0][...] + scale * acc
        outs[0][...] = xv
        r = lax.rsqrt(jnp.mean(xv * xv, axis=-1, keepdims=True) + RMS_EPS)
        outs[1][...] = (xv * r * ex[1][...]).astype(outs[1].dtype)
    return ep


def _rms_bwd_epilogue(out_scale):
    def ep(acc, ex, outs):
        x_ref, g_ref, dres_ref = ex
        dx_ref, dxb_ref, dg_ref = outs
        xv = x_ref[...]
        r = lax.rsqrt(jnp.mean(xv * xv, axis=-1, keepdims=True) + RMS_EPS)
        xhat = xv * r

        @pl.when(pl.program_id(0) == 0)
        def _():
            dg_ref[...] = jnp.zeros_like(dg_ref)

        dg_ref[...] += jnp.sum(acc * xhat, axis=0, keepdims=True)
        dxhat = acc * g_ref[...]
        dx = r * (dxhat - xhat * jnp.mean(dxhat * xhat, axis=-1, keepdims=True)) + dres_ref[...]
        dx_ref[...] = dx
        dxb_ref[...] = (out_scale * dx).astype(dxb_ref.dtype)
    return ep


def _mm_tn(name, a, b, tmo, tno, out_dtype, tk=1024, comm=None):
    T, Ma = a.shape
    Nb = b.shape[1]
    tk = min(tk, T)
    nk = T // tk

    def main(ins, outs, scr):
        a_ref, b_ref = ins
        (acc_ref,) = scr
        k = pl.program_id(2)

        @pl.when(k == 0)
        def _():
            acc_ref[...] = jnp.zeros_like(acc_ref)

        acc_ref[...] += _dot(a_ref[...], b_ref[...], "tn")

        @pl.when(k == nk - 1)
        def _():
            outs[0][...] = acc_ref[...].astype(outs[0].dtype)

    (out,), extra = _call(
        name, main, (Ma // tmo, Nb // tno, nk),
        [pl.BlockSpec((tk, tmo), lambda i, j, k: (k, i)), pl.BlockSpec((tk, tno), lambda i, j, k: (k, j))],
        [pl.BlockSpec((tmo, tno), lambda i, j, k: (i, j))], [jax.ShapeDtypeStruct((Ma, Nb), out_dtype)],
        (a, b), scratch=[pltpu.VMEM((tmo, tno), F32)], comm=comm)
    return out, extra


def _silu_mul_epilogue(dots, comps):
    g, u = dots
    act = g * _sigmoid(g) * u
    return [g, u, act]


def _up_act_epilogue(dots, comps):
    (u,) = dots
    g = comps[0].astype(F32)
    return [u, g * _sigmoid(g) * u]


def _dact_epilogue(dots, comps):
    (dact,) = dots
    g = comps[0].astype(F32)
    u = comps[1].astype(F32)
    s = _sigmoid(g)
    silu = g * s
    dsilu = s + silu * (1.0 - s)
    return [dact * u * dsilu, dact * silu]


def _identity_epilogue(dots, comps):
    return list(dots)


def _swap_halves(x):
    lane = lax.broadcasted_iota(jnp.int32, x.shape, 1)
    first = (lane % DK) < (DK // 2)
    return jnp.where(first, pltpu.roll(x, 128 - DK // 2, 1), pltpu.roll(x, DK // 2, 1))


def _rotary(t, cos, sin_signed):
    halves = []
    for p in range(QK_W // 128):
        th = t[:, 128 * p:128 * (p + 1)]
        halves.append(th * cos + _swap_halves(th) * sin_signed)
    return jnp.concatenate(halves, axis=1)


def _rotary_transposed(d, cos, sin_signed):
    halves = []
    for p in range(QK_W // 128):
        dh = d[:, 128 * p:128 * (p + 1)]
        halves.append(dh * cos + _swap_halves(dh * sin_signed))
    return jnp.concatenate(halves, axis=1)


def _log_sigmoid(x):
    return jnp.minimum(x, 0.0) - jnp.log(1.0 + jnp.exp(-jnp.abs(x)))


def _attn_masks():
    row = lax.broadcasted_iota(jnp.int32, (SUPER, SUPER), 0)
    col = lax.broadcasted_iota(jnp.int32, (SUPER, SUPER), 1)
    same = (row // CHUNK) == (col // CHUNK)
    return row, col, same


def _group_inputs(grp, pr, cos, sin_signed, lg, wa2, ba):
    if grp == 0:
        q = _rotary(pr[:, C_RQ:C_RQ + QK_W], cos, sin_signed)
        k = _rotary(pr[:, C_RK:C_RK + QK_W], cos, sin_signed) * (DK ** -0.5)
        v = pr[:, C_RV:C_RV + V_W]
        gate = pr[:, C_RG:C_RG + V_W]
        pos = lax.broadcasted_iota(jnp.int32, (SUPER, QK_W), 0).astype(F32) + 1.0
        return q, k, v, gate, pos * lg, None, None
    q = pr[:, C_GQ:C_GQ + QK_W] * (DK ** -0.5)
    k = pr[:, C_GK:C_GK + QK_W]
    v = pr[:, C_GV:C_GV + V_W]
    gate = pr[:, C_GG:C_GG + V_W]
    glow = pr[:, C_GL:C_GL + GL_W]
    logit = _dot(glow.astype(BF16), wa2.astype(BF16), "nn") + ba
    la = _log_sigmoid(logit) * (1.0 / GATE_NORM)
    row, col, _ = _attn_masks()
    lower = (col <= row).astype(F32)
    b_cum = _dot(lower, la, "nn", precision=lax.Precision.HIGHEST)
    return q, k, v, gate, b_cum, glow, logit


def _decay_factors(q, k, b_cum):
    c = b_cum[SUPER // 2 - 1:SUPER // 2, :]
    bl = b_cum[SUPER - 1:SUPER, :]
    e1 = jnp.exp(b_cum - c)
    e2 = jnp.exp(c - b_cum)
    e_b = jnp.exp(b_cum)
    e_l = jnp.exp(bl - b_cum)
    return dict(e1=e1, e2=e2, eb=e_b, el=e_l, ebl=jnp.exp(bl),
                qp=q * e1, qm=q * e2, kp=k * e1, km=k * e2, qs=q * e_b, kl=k * e_l)


def _state_block_mask():
    r = lax.broadcasted_iota(jnp.int32, (V_W, QK_W), 0)
    c = lax.broadcasted_iota(jnp.int32, (V_W, QK_W), 1)
    return (r // DV) == (c // DK)


def _attn_fwd(proj, cos, sin_signed, lg, wa2p, ba, gn_ret, gn_gla, comm=None):
    T = proj.shape[0]
    n_s = T // SUPER

    def main(ins, outs, scr):
        pr_ref, cos_ref, sin_ref, lg_ref, wa2_ref, ba_ref, gr_ref, gg_ref = ins
        o_ref, y_ref, st_ref = outs
        (s_ref,) = scr
        i = pl.program_id(0)

        @pl.when(i == 0)
        def _():
            s_ref[...] = jnp.zeros_like(s_ref)

        pr = pr_ref
        row, col, same = _attn_masks()
        m1 = col <= row
        m2 = jnp.logical_and(col > row, same)
        lane = lax.broadcasted_iota(jnp.int32, (1, QK_W), 1)
        blockmask = _state_block_mask()
        for grp in range(2):
            q, k, v, gate, b_cum, _, _ = _group_inputs(grp, pr, cos_ref[...], sin_ref[...], lg_ref[...],
                                                      wa2_ref[...], ba_ref[...])
            f = _decay_factors(q, k, b_cum)
            gn = gr_ref[...] if grp == 0 else gg_ref[...]
            s_prev = s_ref[grp]
            st_ref[0, grp] = s_prev
            o_inter = _dot(f["qs"].astype(BF16), s_prev.astype(BF16), "nt")
            kmb = f["km"].astype(BF16)
            kpb = f["kp"].astype(BF16)
            vb = v.astype(BF16)
            for h in range(HEADS):
                hm = (lane // DK) == h
                a1 = _dot(jnp.where(hm, f["qp"], 0.0).astype(BF16), kmb, "nt")
                a2 = _dot(jnp.where(hm, f["qm"], 0.0).astype(BF16), kpb, "nt")
                a = jnp.where(m1, a1, jnp.where(m2, a2, 0.0))
                lo = grp * V_W + h * DV
                o_h = _dot(a.astype(BF16), vb[:, h * DV:(h + 1) * DV], "nn") + o_inter[:, h * DV:(h + 1) * DV]
                o_ref[:, lo:lo + DV] = o_h
                r = lax.rsqrt(jnp.mean(o_h * o_h, axis=-1, keepdims=True) + RMS_EPS)
                gte = gate[:, h * DV:(h + 1) * DV]
                y = o_h * r * gn[:, h * DV:(h + 1) * DV] * (gte * _sigmoid(gte))
                y_ref[:, lo:lo + DV] = y.astype(y_ref.dtype)
            upd = _dot(vb, f["kl"].astype(BF16), "tn")
            s_ref[grp] = s_prev * f["ebl"] + jnp.where(blockmask, upd, 0.0)

    const = lambda shape: pl.BlockSpec(shape, lambda i: tuple(0 for _ in shape))
    return _call(
        "attn_fwd", main, (n_s,),
        [pl.BlockSpec((SUPER, PROJ_W), lambda i: (i, 0)),
         pl.BlockSpec((SUPER, 128), lambda i: (i, 0)), pl.BlockSpec((SUPER, 128), lambda i: (i, 0)),
         const((1, QK_W)), const((GL_W, QK_W)), const((1, QK_W)), const((1, V_W)), const((1, V_W))],
        [pl.BlockSpec((SUPER, 2 * V_W), lambda i: (i, 0)), pl.BlockSpec((SUPER, 2 * V_W), lambda i: (i, 0)),
         pl.BlockSpec((1, 2, V_W, QK_W), lambda i: (i, 0, 0, 0))],
        [jax.ShapeDtypeStruct((T, 2 * V_W), F32), jax.ShapeDtypeStruct((T, 2 * V_W), BF16),
         jax.ShapeDtypeStruct((n_s, 2, V_W, QK_W), F32)],
        (proj, cos, sin_signed, lg, wa2p, ba, gn_ret, gn_gla),
        scratch=[pltpu.VMEM((2, V_W, QK_W), F32)], comm=comm)


def _attn_bwd(proj, cos, sin_signed, lg, wa2p, ba, gn_ret, gn_gla, o, dy, states, comm=None):
    T = proj.shape[0]
    n_s = T // SUPER

    def main(ins, outs, scr):
        pr_ref, cos_ref, sin_ref, lg_ref, wa2_ref, ba_ref, gr_ref, gg_ref, o_ref, dy_ref, st_ref = ins
        dp_ref, dgr_ref, dgg_ref, dba_ref, dwa_ref = outs
        (ds_ref,) = scr
        i = pl.program_id(0)

        @pl.when(i == 0)
        def _():
            ds_ref[...] = jnp.zeros_like(ds_ref)
            dgr_ref[...] = jnp.zeros_like(dgr_ref)
            dgg_ref[...] = jnp.zeros_like(dgg_ref)
            dba_ref[...] = jnp.zeros_like(dba_ref)
            dwa_ref[...] = jnp.zeros_like(dwa_ref)

        pr = pr_ref
        cos = cos_ref[...]
        sin_signed = sin_ref[...]
        row, col, same = _attn_masks()
        m1 = col <= row
        m2 = jnp.logical_and(col > row, same)
        m1t = row <= col
        m2t = jnp.logical_and(row > col, same)
        lane = lax.broadcasted_iota(jnp.int32, (1, QK_W), 1)
        blockmask = _state_block_mask()
        for grp in range(2):
            q, k, v, gate, b_cum, glow, logit = _group_inputs(grp, pr, cos, sin_signed, lg_ref[...],
                                                              wa2_ref[...], ba_ref[...])
            f = _decay_factors(q, k, b_cum)
            gn = gr_ref[...] if grp == 0 else gg_ref[...]
            dgn_ref = dgr_ref if grp == 0 else dgg_ref
            do_parts, dgate_parts, dgn_parts = [], [], []
            for h in range(HEADS):
                lo = grp * V_W + h * DV
                o_h = o_ref[:, lo:lo + DV]
                r = lax.rsqrt(jnp.mean(o_h * o_h, axis=-1, keepdims=True) + RMS_EPS)
                n = o_h * r
                gte = gate[:, h * DV:(h + 1) * DV]
                sg = _sigmoid(gte)
                dy_h = dy_ref[:, lo:lo + DV]
                gn_h = gn[:, h * DV:(h + 1) * DV]
                dgate_parts.append(dy_h * n * gn_h * (sg * (1.0 + gte * (1.0 - sg))))
                dz = dy_h * (gte * sg)
                dgn_parts.append(jnp.sum(dz * n, axis=0, keepdims=True))
                dn = dz * gn_h
                do_parts.append(r * (dn - n * jnp.mean(dn * n, axis=-1, keepdims=True)))
            dgn_ref[...] += jnp.concatenate(dgn_parts, axis=1)
            dgate = jnp.concatenate(dgate_parts, axis=1)
            do = jnp.concatenate(do_parts, axis=1)
            dob = do.astype(BF16)
            vb = v.astype(BF16)
            s_prev = st_ref[0, grp]
            ds_new = ds_ref[grp]
            dsb = ds_new.astype(BF16)
            qpb, qmb = f["qp"].astype(BF16), f["qm"].astype(BF16)
            kpb, kmb = f["kp"].astype(BF16), f["km"].astype(BF16)
            dqp = jnp.zeros((SUPER, QK_W), F32)
            dqm = jnp.zeros((SUPER, QK_W), F32)
            dkp = jnp.zeros((SUPER, QK_W), F32)
            dkm = jnp.zeros((SUPER, QK_W), F32)
            dv_parts = []
            for h in range(HEADS):
                hm = (lane // DK) == h
                qp_h = jnp.where(hm, f["qp"], 0.0).astype(BF16)
                qm_h = jnp.where(hm, f["qm"], 0.0).astype(BF16)
                kp_h = jnp.where(hm, f["kp"], 0.0).astype(BF16)
                km_h = jnp.where(hm, f["km"], 0.0).astype(BF16)
                at = jnp.where(m1t, _dot(km_h, qpb, "nt"), jnp.where(m2t, _dot(kp_h, qmb, "nt"), 0.0))
                do_h = dob[:, h * DV:(h + 1) * DV]
                v_h = vb[:, h * DV:(h + 1) * DV]
                dv_parts.append(_dot(at.astype(BF16), do_h, "nn"))
                da = _dot(do_h, v_h, "nt")
                dat = _dot(v_h, do_h, "nt")
                da1 = jnp.where(m1, da, 0.0).astype(BF16)
                da2 = jnp.where(m2, da, 0.0).astype(BF16)
                da1t = jnp.where(m1t, dat, 0.0).astype(BF16)
                da2t = jnp.where(m2t, dat, 0.0).astype(BF16)
                dqp = dqp + _dot(da1, km_h, "nn")
                dqm = dqm + _dot(da2, kp_h, "nn")
                dkm = dkm + _dot(da1t, qp_h, "nn")
                dkp = dkp + _dot(da2t, qm_h, "nn")
            klb = f["kl"].astype(BF16)
            qsb = f["qs"].astype(BF16)
            dqs = _dot(dob, s_prev.astype(BF16), "nn")
            dkl = _dot(vb, dsb, "nn")
            dv = jnp.concatenate(dv_parts, axis=1) + _dot(klb, dsb, "nt")
            ds_ref[grp] = ds_new * f["ebl"] + jnp.where(blockmask, _dot(dob, qsb, "tn"), 0.0)
            dq = dqp * f["e1"] + dqm * f["e2"] + dqs * f["eb"]
            dk = dkm * f["e2"] + dkp * f["e1"] + dkl * f["el"]
            if grp == 0:
                dq = _rotary_transposed(dq, cos, sin_signed)
                dk = _rotary_transposed(dk * (DK ** -0.5), cos, sin_signed)
                dp_ref[:, C_RQ:C_RQ + QK_W] = dq.astype(dp_ref.dtype)
                dp_ref[:, C_RK:C_RK + QK_W] = dk.astype(dp_ref.dtype)
                dp_ref[:, C_RV:C_RV + V_W] = dv.astype(dp_ref.dtype)
                dp_ref[:, C_RG:C_RG + V_W] = dgate.astype(dp_ref.dtype)
            else:
                dkl_kl = dkl * klb.astype(F32)
                db = (dqp * qpb.astype(F32) - dkm * kmb.astype(F32) - dqm * qmb.astype(F32)
                      + dkp * kpb.astype(F32) + dqs * qsb.astype(F32) - dkl_kl)
                last = (jnp.sum(dkl_kl, axis=0, keepdims=True)
                        + f["ebl"] * jnp.sum(s_prev * ds_new, axis=0, keepdims=True))
                rowq = lax.broadcasted_iota(jnp.int32, (SUPER, QK_W), 0)
                db = db + jnp.where(rowq == SUPER - 1, last, 0.0)
                upper = (col >= row).astype(F32)
                dla = _dot(upper, db, "nn", precision=lax.Precision.HIGHEST)
                dlogit = dla * (1.0 / GATE_NORM) * (1.0 - _sigmoid(logit))
                dlb = dlogit.astype(BF16)
                dglow = _dot(dlb, wa2_ref[...].astype(BF16), "nt")
                dwa_ref[...] += _dot(glow.astype(BF16), dlb, "tn")
                dba_ref[...] += jnp.sum(dlogit, axis=0, keepdims=True)
                dp_ref[:, C_GQ:C_GQ + QK_W] = (dq * (DK ** -0.5)).astype(dp_ref.dtype)
                dp_ref[:, C_GK:C_GK + QK_W] = dk.astype(dp_ref.dtype)
                dp_ref[:, C_GV:C_GV + V_W] = dv.astype(dp_ref.dtype)
                dp_ref[:, C_GG:C_GG + V_W] = dgate.astype(dp_ref.dtype)
                dp_ref[:, C_GL:C_GL + GL_W] = dglow.astype(dp_ref.dtype)

    rev = lambda i: n_s - 1 - i
    const = lambda shape: pl.BlockSpec(shape, lambda i: tuple(0 for _ in shape))
    return _call(
        "attn_bwd", main, (n_s,),
        [pl.BlockSpec((SUPER, PROJ_W), lambda i: (rev(i), 0)),
         pl.BlockSpec((SUPER, 128), lambda i: (rev(i), 0)), pl.BlockSpec((SUPER, 128), lambda i: (rev(i), 0)),
         const((1, QK_W)), const((GL_W, QK_W)), const((1, QK_W)), const((1, V_W)), const((1, V_W)),
         pl.BlockSpec((SUPER, 2 * V_W), lambda i: (rev(i), 0)),
         pl.BlockSpec((SUPER, 2 * V_W), lambda i: (rev(i), 0)),
         pl.BlockSpec((1, 2, V_W, QK_W), lambda i: (rev(i), 0, 0, 0))],
        [pl.BlockSpec((SUPER, PROJ_W), lambda i: (rev(i), 0)),
         const((1, V_W)), const((1, V_W)), const((1, QK_W)), const((GL_W, QK_W))],
        [jax.ShapeDtypeStruct((T, PROJ_W), BF16),
         jax.ShapeDtypeStruct((1, V_W), F32), jax.ShapeDtypeStruct((1, V_W), F32),
         jax.ShapeDtypeStruct((1, QK_W), F32), jax.ShapeDtypeStruct((GL_W, QK_W), F32)],
        (proj, cos, sin_signed, lg, wa2p, ba, gn_ret, gn_gla, o, dy, states),
        scratch=[pltpu.VMEM((2, V_W, QK_W), F32)], comm=comm)


def _rotary_tables(T):
    half = DK // 2
    inv = ROPE_BASE ** (-jnp.arange(half, dtype=F32) * 2.0 / DK)
    ang = jnp.arange(T, dtype=F32)[:, None] * inv[None, :]
    cos, sin = jnp.cos(ang), jnp.sin(ang)
    cos_head = jnp.concatenate([cos, cos], axis=1)
    sin_head = jnp.concatenate([-sin, sin], axis=1)
    return jnp.tile(cos_head, (1, 128 // DK)), jnp.tile(sin_head, (1, 128 // DK))


def _sum_devices(name, gathered, m_per):
    def body(g_ref, o_ref):
        acc = g_ref[0:m_per, :]
        for k in range(1, N_DEV):
            acc = acc + g_ref[k * m_per:(k + 1) * m_per, :]
        o_ref[...] = acc

    return pl.pallas_call(body, name=name, out_shape=jax.ShapeDtypeStruct((m_per, 128), F32))(gathered)


def _pair_sum(name, grad, landed, core):
    R, C = grad.shape[2:]

    def main(ins, outs, scr):
        outs[0][...] = (ins[0][...].astype(F32) + ins[1][...].astype(F32)).astype(outs[0].dtype)

    blk = pl.BlockSpec((None, R, C), lambda j, s: (j, 0, 0))
    (out,), _ = _call(name, main, (4,), [pl.BlockSpec((None, None, R, C), lambda j, s: (j, s[0], 0, 0)), blk], [blk],
                      [jax.ShapeDtypeStruct((4, R, C), BF16)], (grad, landed), prefetch=core)
    return out


def _owner_sums(name, items, owner, comm=None):
    counts = [1 + len(landed) for _, landed in items]

    def main(ins, outs, scr):
        at = 0
        for o_ref, n in zip(outs, counts):
            acc = ins[at][...].astype(F32)
            for l_ref in ins[at + 1:at + n]:
                for j in range(l_ref.shape[0]):
                    acc = acc + l_ref[j].astype(F32)
            o_ref[...] = acc
            at += n

    once = pl.Buffered(1)
    in_specs, out_specs, out_shape, args = [], [], [], []
    for grad, landed in items:
        R, C = grad.shape[-2:]
        if grad.ndim == 4:
            in_specs.append(pl.BlockSpec((None, None, R, C), lambda i, s: (s[0], s[1], 0, 0), pipeline_mode=once))
        else:
            in_specs.append(pl.BlockSpec((None, R, C), lambda i, s: (s[0], 0, 0), pipeline_mode=once))
        in_specs += [pl.BlockSpec(tuple(l.shape), lambda i, s: (0, 0, 0), pipeline_mode=once) for l in landed]
        out_specs.append(pl.BlockSpec((R, C), lambda i, s: (0, 0)))
        out_shape.append(jax.ShapeDtypeStruct((R, C), F32))
        args += [grad, *landed]
    return _call(name, main, (1,), in_specs, out_specs, out_shape, args, comm=comm, prefetch=owner)


def _rms_bwd(name, x, g, dh, dres, out_scale):
    T, D = x.shape
    tm = min(T, 512)
    ep = _rms_bwd_epilogue(out_scale)

    def main(ins, outs, scr):
        x_ref, g_ref, dh_ref, dres_ref = ins
        ep(dh_ref[...], (x_ref, g_ref, dres_ref), outs)

    tile = pl.BlockSpec((tm, D), lambda i: (i, 0))
    vec = pl.BlockSpec((1, D), lambda i: (0, 0))
    outs, _ = _call(name, main, (T // tm,), [tile, vec, tile, tile], [tile, tile, vec],
                    [jax.ShapeDtypeStruct((T, D), F32), jax.ShapeDtypeStruct((T, D), BF16),
                     jax.ShapeDtypeStruct((1, D), F32)], (x, g, dh, dres))
    return outs


def _adamw_group(name, items, n_blocks, comm=None):
    n = len(items)

    def main(ins, outs, scr):
        for p in range(n):
            g_ref, w_ref, m_ref, v_ref = ins[4 * p:4 * p + 4]
            d_ref, nm_ref, nv_ref = outs[3 * p:3 * p + 3]
            gv = g_ref[...]
            nm = ADAM_B1 * m_ref[...] + (1.0 - ADAM_B1) * gv
            nv = ADAM_B2 * v_ref[...] + (1.0 - ADAM_B2) * (gv * gv)
            m_hat = nm / (1.0 - ADAM_B1 ** ADAM_STEP)
            v_hat = nv / (1.0 - ADAM_B2 ** ADAM_STEP)
            d_ref[...] = -ADAM_LR * (m_hat / (jnp.sqrt(v_hat) + ADAM_EPS) + ADAM_WD * w_ref[...])
            nm_ref[...] = nm
            nv_ref[...] = nv

    in_specs, out_specs, out_shape, args = [], [], [], []
    for item in items:
        R, C = item[1].shape
        assert R % n_blocks == 0
        spec = pl.BlockSpec((R // n_blocks, C), lambda i: (i, 0))
        in_specs += [spec] * 4
        out_specs += [spec] * 3
        out_shape += [jax.ShapeDtypeStruct((R, C), F32)] * 3
        args += list(item)
    outs, extra = _call(name, main, (n_blocks,), in_specs, out_specs, out_shape, args, comm=comm)
    return [tuple(outs[3 * p:3 * p + 3]) for p in range(n)], extra


SMALL_ORDER = ("ffn1", "mix", "ffn2", "final", "ret", "gla", "b_a")


def kernel(x, ffn1_norm_g, ffn1_w_gate, ffn1_w_up, ffn1_w_down, mix_norm_g, w_in, ret_norm_g, gla_w_a2, gla_b_a, gla_norm_g, w_out, ffn2_norm_g, ffn2_w_gate, ffn2_w_up, ffn2_w_down, final_norm_g, loss_target, m_ffn1_norm_g, m_ffn1_w_gate, m_ffn1_w_up, m_ffn1_w_down, m_mix_norm_g, m_w_in, m_ret_norm_g, m_gla_w_a2, m_gla_b_a, m_gla_norm_g, m_w_out, m_ffn2_norm_g, m_ffn2_w_gate, m_ffn2_w_up, m_ffn2_w_down, m_final_norm_g, v_ffn1_norm_g, v_ffn1_w_gate, v_ffn1_w_up, v_ffn1_w_down, v_mix_norm_g, v_w_in, v_ret_norm_g, v_gla_w_a2, v_gla_b_a, v_gla_norm_g, v_w_out, v_ffn2_norm_g, v_ffn2_w_gate, v_ffn2_w_up, v_ffn2_w_down, v_final_norm_g):
    xi, yi, ci = _coords()
    dev = 4 * xi + 2 * yi + ci
    owner = jnp.stack([2 * xi + yi, ci]).astype(jnp.int32)

    x0, target = x[0], loss_target[0]
    T, D = x0.shape
    fb = ffn1_w_gate.shape[2]
    ib = w_in.shape[2]
    ab = gla_w_a2.shape[2]
    F = N_DEV * fb
    cos, sin_signed = _rotary_tables(T)
    lg = jnp.repeat(jnp.log(1.0 - 2.0 ** (-5.0 - jnp.arange(HEADS, dtype=F32))), DK)[None, :]
    g_final = final_norm_g.reshape(1, D)

    g1_loc = ffn1_w_gate[0].T[None].astype(BF16)
    u1_loc = ffn1_w_up[0].T[None].astype(BF16)
    d1_loc = ffn1_w_down.astype(BF16)
    g2_loc = ffn2_w_gate[0].T[None].astype(BF16)
    u2_loc = ffn2_w_up[0].T[None].astype(BF16)
    d2_loc = ffn2_w_down.astype(BF16)
    in_loc = w_in[0].T.astype(BF16)
    out_loc = w_out[0].astype(BF16)

    h1, (g1,) = _rms_fwd("ffn1_rms", x0, ffn1_norm_g, comm=_AllGather([g1_loc], ["stack"]))
    g1 = g1.reshape(1, F, D)
    (gate1,), (u1,) = _mm_nstream("ffn1_gate", h1, [g1], [0], "nt", [], [BF16], _identity_epilogue, cn=256,
                                  comm=_AllGather([u1_loc], ["stack"]))
    u1 = u1.reshape(1, F, D)
    (up1, act1), (d1,) = _mm_nstream("ffn1_up", h1, [u1], [0], "nt", [gate1], [BF16, BF16], _up_act_epilogue, cn=256,
                                     comm=_AllGather([d1_loc], ["stack"]))
    d1 = d1.reshape(1, F, D)
    f32_tile, bf16_tile, f32_vec = (F32, "tile"), (BF16, "tile"), (F32, "vec")
    (x1, h2), (in_all, a_all) = _mm_mstream(
        "ffn1_down", [act1], [d1], [0], "nn", [(x0, "tile"), (mix_norm_g, "vec")], [f32_tile, bf16_tile],
        _residual_rms_epilogue(0.5), comm=_AllGather([in_loc, gla_w_a2[0]], ["plain", "plain"]))
    w_in_t = jnp.pad(in_all.reshape(1, N_DEV * ib, D), ((0, 0), (0, PROJ_W - N_DEV * ib), (0, 0)))
    wa2 = jnp.transpose(a_all, (1, 0, 2)).reshape(GATE_RANK, N_DEV * ab)
    wa2p = jnp.pad(wa2, ((0, GL_W - GATE_RANK), (0, 0)))

    (proj,), (g2,) = _mm_nstream("mix_proj", h2, [w_in_t], [0], "nt", [], [F32], _identity_epilogue, cn=640,
                                 comm=_AllGather([g2_loc], ["stack"]))
    (o, ymix, states), (u2, out_all) = _attn_fwd(proj, cos, sin_signed, lg, wa2p, gla_b_a, ret_norm_g, gla_norm_g,
                                                 comm=_AllGather([u2_loc, out_loc], ["stack", "plain"]))
    w_out_full = out_all.reshape(D, D)
    (x2, h3), _ = _mm_mstream("mix_out", [ymix], [w_out_full], [0], "nn", [(x1, "tile"), (ffn2_norm_g, "vec")],
                              [f32_tile, bf16_tile], _residual_rms_epilogue(1.0))
    g2, u2 = g2.reshape(1, F, D), u2.reshape(1, F, D)

    (gate2, up2, act2), (d2,) = _mm_nstream(
        "ffn2_up", h3, [g2, u2], [0, 0], "nt", [], [BF16, BF16, BF16], _silu_mul_epilogue, cn=256,
        comm=_AllGather([d2_loc], ["stack"]))
    d2 = d2.reshape(1, F, D)
    (x3,), _ = _mm_mstream("ffn2_down", [act2], [d2], [0], "nn", [(x2, "tile")], [f32_tile], _residual_epilogue(0.5))

    dx3, dy3b, d_final, loss = _final_loss_bwd("final", x3, g_final, target, 0.5)

    dwd2, _ = _mm_tn("ffn2b_dwd", act2, dy3b, F // 2, D, BF16)
    dwd2 = dwd2.reshape(4, 2, fb, D)
    (dgate2, dup2), (l_wd2_near,) = _mm_nstream("ffn2b_dact", dy3b, [d2], [0], "nt", [gate2, up2], [BF16, BF16],
                                                _dact_epilogue, cn=256, comm=_ReduceScatter([(dwd2, NEAR)]))
    dwg2, (l_wd2_far,) = _mm_tn("ffn2b_dwg", dgate2, h3, F // 2, D, BF16, comm=_ReduceScatter([(dwd2, FAR)]))
    dwg2 = dwg2.reshape(4, 2, fb, D)
    dwu2, (l_wg2_near,) = _mm_tn("ffn2b_dwu", dup2, h3, F // 2, D, BF16, comm=_ReduceScatter([(dwg2, NEAR)]))
    dwu2 = dwu2.reshape(4, 2, fb, D)
    rms_outs = [f32_tile, bf16_tile, f32_vec]
    (dx2, dx2b, d_g2), (l_wg2_far, l_wu2_near) = _mm_mstream(
        "ffn2b_dh", [dgate2, dup2], [g2, u2], [0, 0], "nn", [(x2, "tile"), (ffn2_norm_g, "vec"), (dx3, "tile")],
        rms_outs, _rms_bwd_epilogue(1.0), comm=_ReduceScatter([(dwg2, FAR), (dwu2, NEAR)]))

    (dymix,), _ = _mm_mstream("mixb_dy", [dx2b], [w_out_full], [0], "nt", [], [f32_tile], _plain_epilogue)
    dwout, _ = _mm_tn("mixb_dwout", ymix, dx2b, D, D, BF16)
    dwout = dwout.reshape(4, 2, D // N_DEV, D)
    (dproj, d_ret, d_gla, d_ba, d_wa2p), (l_wu2_far, l_wout) = _attn_bwd(
        proj, cos, sin_signed, lg, wa2p, gla_b_a, ret_norm_g, gla_norm_g, o, dymix, states,
        comm=_ReduceScatter([(dwu2, FAR), (dwout, ALL)]))
    dwin_t, _ = _mm_tn("mixb_dwin", dproj, h2, 640, D, BF16)
    dwin = dwin_t[:N_DEV * ib].reshape(4, 2, ib, D)
    (dx1, dy1b, d_gmix), (l_win_near,) = _mm_mstream(
        "mixb_dh", [dproj], [w_in_t], [0], "nn", [(x1, "tile"), (mix_norm_g, "vec"), (dx2, "tile")],
        rms_outs, _rms_bwd_epilogue(0.5), comm=_ReduceScatter([(dwin, NEAR)]))

    dwd1, (l_win_far,) = _mm_tn("ffn1b_dwd", act1, dy1b, F // 2, D, BF16, comm=_ReduceScatter([(dwin, FAR)]))
    dwd1 = dwd1.reshape(4, 2, fb, D)
    (dgate1, dup1), (l_wd1_near,) = _mm_nstream("ffn1b_dact", dy1b, [d1], [0], "nt", [gate1, up1], [BF16, BF16],
                                                _dact_epilogue, cn=256, comm=_ReduceScatter([(dwd1, NEAR)]))
    dwg1, (l_wd1_far,) = _mm_tn("ffn1b_dwg", dgate1, h1, F // 2, D, BF16, comm=_ReduceScatter([(dwd1, FAR)]))
    dwg1 = dwg1.reshape(4, 2, fb, D)
    core = owner[1:2]
    dwu1, (s_wg1,) = _mm_tn("ffn1b_dwu", dup1, h1, F // 2, D, BF16, comm=_SiblingExchange([dwg1]))
    dwu1 = dwu1.reshape(4, 2, fb, D)
    p_wg1 = _pair_sum("pair_wg1", dwg1, s_wg1, core)
    (dh1,), (c_wg1, s_wu1) = _mm_mstream(
        "ffn1b_dh", [dgate1, dup1], [g1, u1], [0, 0], "nn", [], [f32_tile], _plain_epilogue,
        comm=_Multi([_ChipExchange([p_wg1]), _SiblingExchange([dwu1])]))
    p_wu1 = _pair_sum("pair_wu1", dwu1, s_wu1, core)
    dx0, _, d_g1 = _rms_bwd("ffn1b_rms", x0, ffn1_norm_g, dh1, dx1, 1.0)

    small = dict(ffn1=d_g1, mix=d_gmix, ffn2=d_g2, final=d_final, ret=d_ret, gla=d_gla, b_a=d_ba)
    flat = jnp.concatenate([small[k].reshape(-1) for k in SMALL_ORDER]
                           + [d_wa2p[:GATE_RANK].reshape(-1), loss[0]])
    rows = -(-flat.shape[0] // 128)
    rows = -(-rows // 8) * 8
    packed = jnp.pad(flat, (0, rows * 128 - flat.shape[0])).reshape(rows, 128)

    def adam_item(nm, grad, weight, mom, var):
        shape = weight.shape
        two_d = (1, shape[0]) if len(shape) == 1 else shape[-2:]
        return tuple(a.reshape(two_d) for a in (grad, weight, mom, var))

    sums_a, (c_wu1, gathered) = _owner_sums(
        "sum_a", [(dwg2, [l_wg2_near, l_wg2_far]), (dwu2, [l_wu2_near, l_wu2_far]), (dwd2, [l_wd2_near, l_wd2_far]),
                  (dwin, [l_win_near, l_win_far]), (dwout, [l_wout])], owner,
        comm=_Multi([_ChipExchange([p_wu1]), _AllGather([packed], ["plain"])]))
    sums_b, _ = _owner_sums("sum_b", [(p_wg1, [c_wg1]), (p_wu1, [c_wu1]), (dwd1, [l_wd1_near, l_wd1_far])], owner)
    big_grads = {"ffn2_w_gate": sums_a[0].T, "ffn2_w_up": sums_a[1].T, "ffn2_w_down": sums_a[2], "w_in": sums_a[3].T,
                 "w_out": sums_a[4], "ffn1_w_gate": sums_b[0].T, "ffn1_w_up": sums_b[1].T, "ffn1_w_down": sums_b[2]}
    params = dict(
        ffn2_w_gate=(ffn2_w_gate, m_ffn2_w_gate, v_ffn2_w_gate), ffn2_w_up=(ffn2_w_up, m_ffn2_w_up, v_ffn2_w_up),
        ffn2_w_down=(ffn2_w_down, m_ffn2_w_down, v_ffn2_w_down), w_in=(w_in, m_w_in, v_w_in),
        w_out=(w_out, m_w_out, v_w_out), ffn1_w_gate=(ffn1_w_gate, m_ffn1_w_gate, v_ffn1_w_gate),
        ffn1_w_up=(ffn1_w_up, m_ffn1_w_up, v_ffn1_w_up), ffn1_w_down=(ffn1_w_down, m_ffn1_w_down, v_ffn1_w_down),
        ffn1_norm_g=(ffn1_norm_g, m_ffn1_norm_g, v_ffn1_norm_g), mix_norm_g=(mix_norm_g, m_mix_norm_g, v_mix_norm_g),
        ret_norm_g=(ret_norm_g, m_ret_norm_g, v_ret_norm_g), gla_w_a2=(gla_w_a2, m_gla_w_a2, v_gla_w_a2),
        gla_b_a=(gla_b_a, m_gla_b_a, v_gla_b_a), gla_norm_g=(gla_norm_g, m_gla_norm_g, v_gla_norm_g),
        ffn2_norm_g=(ffn2_norm_g, m_ffn2_norm_g, v_ffn2_norm_g), final_norm_g=(final_norm_g, m_final_norm_g, v_final_norm_g))
    grads, updates = {}, {}

    def run_adam(name, names, grad_of, n_blocks):
        for nm in names:
            grads[nm] = grad_of[nm].reshape(params[nm][0].shape)
        res, _ = _adamw_group(name, [adam_item(nm, grads[nm], *params[nm]) for nm in names], n_blocks)
        for nm, r in zip(names, res):
            updates[nm] = tuple(a.reshape(params[nm][0].shape) for a in r)

    run_adam("adamw_big", list(big_grads), big_grads, 4)

    total = _sum_devices("sum_small", gathered.reshape(N_DEV * rows, 128), rows).reshape(-1)
    sizes = [small[k].size for k in SMALL_ORDER] + [GATE_RANK * QK_W, 128]
    offs = [0]
    for s in sizes:
        offs.append(offs[-1] + s)
    pieces = [total[offs[i]:offs[i + 1]] for i in range(len(sizes))]
    g_small = {k: pieces[i].reshape(small[k].shape) for i, k in enumerate(SMALL_ORDER)}
    g_wa2_full = pieces[len(SMALL_ORDER)].reshape(GATE_RANK, QK_W)
    g_wa2 = lax.dynamic_slice(g_wa2_full, (0, dev * ab), (GATE_RANK, ab))
    loss_total = pieces[len(SMALL_ORDER) + 1][0]

    small_grads = {"ffn1_norm_g": g_small["ffn1"], "mix_norm_g": g_small["mix"], "ret_norm_g": g_small["ret"],
                   "gla_w_a2": g_wa2, "gla_b_a": g_small["b_a"], "gla_norm_g": g_small["gla"],
                   "ffn2_norm_g": g_small["ffn2"], "final_norm_g": g_small["final"]}
    run_adam("adamw_small", list(small_grads), small_grads, 1)

    order = ("ffn1_norm_g", "ffn1_w_gate", "ffn1_w_up", "ffn1_w_down", "mix_norm_g", "w_in", "ret_norm_g", "gla_w_a2",
             "gla_b_a", "gla_norm_g", "w_out", "ffn2_norm_g", "ffn2_w_gate", "ffn2_w_up", "ffn2_w_down", "final_norm_g")
    return (loss_total, dx0[None], *[grads[nm] for nm in order], *[updates[nm][0] for nm in order],
            *[updates[nm][1] for nm in order], *[updates[nm][2] for nm in order])
```

```python
import functools
import math

import jax
import jax.numpy as jnp
from jax import lax
from jax.experimental import pallas as pl
from jax.experimental.pallas import tpu as pltpu

F32 = jnp.float32
BF16 = jnp.bfloat16
MESH = pl.DeviceIdType.MESH
HBM = pl.BlockSpec(memory_space=pltpu.HBM)

N_DEV = 8
RMS_EPS = 1e-6
ROPE_BASE = 10000.0
HEADS = 4
DK = 64
DV = 128
QK_W = HEADS * DK
V_W = HEADS * DV
GATE_RANK = 16
GATE_NORM = 16.0
CHUNK = 64
SUPER = 256
PROJ_W = 3200
C_RQ, C_RK, C_RV, C_RG, C_GQ, C_GK, C_GV, C_GG, C_GL = 0, 256, 512, 1024, 1536, 1792, 2048, 2560, 3072
GL_W = PROJ_W - C_GL
ADAM_LR, ADAM_B1, ADAM_B2, ADAM_EPS, ADAM_WD, ADAM_STEP = 0.001, 0.9, 0.999, 1e-08, 0.01, 10
VMEM_LIMIT_V7X = 52 * 1024 * 1024


def _cparams(**kw):
    return pltpu.CompilerParams(vmem_limit_bytes=VMEM_LIMIT_V7X, **kw)


def _dot(a, b, form, precision=None):
    dims = {"nn": (((1,), (0,)), ((), ())), "nt": (((1,), (1,)), ((), ())), "tn": (((0,), (0,)), ((), ()))}[form]
    return lax.dot_general(a, b, dims, preferred_element_type=F32, precision=precision)


def _sigmoid(x):
    return 1.0 / (1.0 + jnp.exp(-x))


def _coords():
    return lax.axis_index("x"), lax.axis_index("y"), lax.axis_index("c")


class _NoComm:
    inputs, out_shapes, scratch = (), (), ()


class _AllGather:
    def __init__(self, arrays, kinds):
        self.inputs = tuple(arrays)
        self.kinds = tuple(kinds)
        n = len(arrays)
        self.out_shapes = tuple(
            jax.ShapeDtypeStruct((a.shape[0], N_DEV) + a.shape[1:] if k == "stack" else (N_DEV,) + a.shape, a.dtype)
            for a, k in zip(arrays, kinds))
        self.scratch = (pltpu.SemaphoreType.DMA((n, 7)), pltpu.SemaphoreType.DMA((n, 7)),
                        pltpu.SemaphoreType.DMA((n,)))

    def _ctx(self, srcs, outs, sems):
        send_sems, recv_sems, local_sems = sems
        x, y, c = _coords()
        me, sibling = (x, y, c), (x, y, 1 - c)
        chips = [(1 - x, y), (x, 1 - y), (1 - x, 1 - y)]

        def blk(m, dev):
            k = 4 * dev[0] + 2 * dev[1] + dev[2]
            return outs[m].at[:, k] if self.kinds[m] == "stack" else outs[m].at[k]

        def copy(m, s, block, to, src=None):
            return pltpu.make_async_remote_copy(
                src_ref=blk(m, block) if src is None else src, dst_ref=blk(m, block),
                send_sem=send_sems.at[m, s], recv_sem=recv_sems.at[m, s], device_id=to, device_id_type=MESH)

        def mine(m):
            return pltpu.make_async_copy(srcs[m], blk(m, me), local_sems.at[m])

        def first(m):
            return [copy(m, 0, me, sibling, src=srcs[m])] + [
                copy(m, 1 + j, me, (*chip, c), src=srcs[m]) for j, chip in enumerate(chips)]

        return me, sibling, chips, c, copy, mine, first

    def start(self, srcs, outs, sems):
        me, sibling, chips, c, copy, mine, first = self._ctx(srcs, outs, sems)
        for m in range(len(srcs)):
            mine(m).start()
            for cp in first(m):
                cp.start()

    def mid(self, srcs, outs, sems):
        me, sibling, chips, c, copy, mine, first = self._ctx(srcs, outs, sems)
        for j, chip in enumerate(chips):
            for m in range(len(srcs)):
                copy(m, 1 + j, (*chip, c), me).wait_recv()
                copy(m, 4 + j, (*chip, c), sibling).start()

    def finish(self, srcs, outs, sems):
        me, sibling, chips, c, copy, mine, first = self._ctx(srcs, outs, sems)
        for m in range(len(srcs)):
            copy(m, 0, sibling, me).wait_recv()
            for j, chip in enumerate(chips):
                copy(m, 4 + j, (*chip, 1 - c), me).wait_recv()
            for cp in first(m):
                cp.wait_send()
            for j, chip in enumerate(chips):
                copy(m, 4 + j, (*chip, c), sibling).wait_send()
            mine(m).wait()


RELATIONS = ((0, 0, 1), (1, 0, 0), (0, 1, 0), (1, 1, 0), (1, 0, 1), (0, 1, 1), (1, 1, 1))
NEAR = (0, 1, 2, 4, 5)
FAR = (3, 6)
ALL = NEAR + FAR


class _ReduceScatter:
    def __init__(self, parts):
        self.inputs = tuple(g for g, _ in parts)
        self.slots = tuple(s for _, s in parts)
        self.out_shapes = tuple(jax.ShapeDtypeStruct((len(s),) + g.shape[2:], g.dtype) for g, s in parts)
        n_max = max(len(s) for s in self.slots)
        n = len(parts)
        self.scratch = (pltpu.SemaphoreType.DMA((n, n_max)), pltpu.SemaphoreType.DMA((n, n_max)))

    def _copies(self, srcs, outs, sems):
        send_sems, recv_sems = sems
        x, y, c = _coords()
        copies = []
        for m, slots in enumerate(self.slots):
            for i, s in enumerate(slots):
                fx, fy, fc = RELATIONS[s]
                px = 1 - x if fx else x
                py = 1 - y if fy else y
                pc = 1 - c if fc else c
                copies.append(pltpu.make_async_remote_copy(
                    src_ref=srcs[m].at[2 * px + py, pc], dst_ref=outs[m].at[i], send_sem=send_sems.at[m, i],
                    recv_sem=recv_sems.at[m, i], device_id=(px, py, pc), device_id_type=MESH))
        return copies

    def start(self, srcs, outs, sems):
        for cp in self._copies(srcs, outs, sems):
            cp.start()

    def mid(self, srcs, outs, sems):
        pass

    def finish(self, srcs, outs, sems):
        for cp in self._copies(srcs, outs, sems):
            cp.wait()


class _SiblingExchange:
    def __init__(self, grads):
        self.inputs = tuple(grads)
        self.out_shapes = tuple(jax.ShapeDtypeStruct((4,) + g.shape[2:], g.dtype) for g in grads)
        self.scratch = (pltpu.SemaphoreType.DMA((len(grads),)), pltpu.SemaphoreType.DMA((len(grads),)))

    def _copies(self, srcs, outs, sems):
        send_sems, recv_sems = sems
        x, y, c = _coords()
        return [pltpu.make_async_remote_copy(
            src_ref=srcs[m].at[:, 1 - c], dst_ref=outs[m], send_sem=send_sems.at[m], recv_sem=recv_sems.at[m],
            device_id=(x, y, 1 - c), device_id_type=MESH) for m in range(len(srcs))]

    def start(self, srcs, outs, sems):
        for cp in self._copies(srcs, outs, sems):
            cp.start()

    def mid(self, srcs, outs, sems):
        pass

    def finish(self, srcs, outs, sems):
        for cp in self._copies(srcs, outs, sems):
            cp.wait()


class _ChipExchange:
    def __init__(self, partials):
        self.inputs = tuple(partials)
        self.out_shapes = tuple(jax.ShapeDtypeStruct((3,) + p.shape[1:], p.dtype) for p in partials)
        n = len(partials)
        self.scratch = (pltpu.SemaphoreType.DMA((n, 3)), pltpu.SemaphoreType.DMA((n, 3)))

    def _copies(self, srcs, outs, sems):
        send_sems, recv_sems = sems
        x, y, c = _coords()
        copies = []
        for m in range(len(srcs)):
            for j, (px, py) in enumerate([(1 - x, y), (x, 1 - y), (1 - x, 1 - y)]):
                copies.append(pltpu.make_async_remote_copy(
                    src_ref=srcs[m].at[2 * px + py], dst_ref=outs[m].at[j], send_sem=send_sems.at[m, j],
                    recv_sem=recv_sems.at[m, j], device_id=(px, py, c), device_id_type=MESH))
        return copies

    def start(self, srcs, outs, sems):
        for cp in self._copies(srcs, outs, sems):
            cp.start()

    def mid(self, srcs, outs, sems):
        pass

    def finish(self, srcs, outs, sems):
        for cp in self._copies(srcs, outs, sems):
            cp.wait()


class _Multi:
    def __init__(self, comms):
        self.comms = comms
        self.inputs = tuple(a for c in comms for a in c.inputs)
        self.out_shapes = tuple(s for c in comms for s in c.out_shapes)
        self.scratch = tuple(s for c in comms for s in c.scratch)

    def _each(self, phase, srcs, outs, sems):
        i = o = k = 0
        for c in self.comms:
            ni, no, nk = len(c.inputs), len(c.out_shapes), len(c.scratch)
            getattr(c, phase)(srcs[i:i + ni], outs[o:o + no], sems[k:k + nk])
            i, o, k = i + ni, o + no, k + nk

    def start(self, srcs, outs, sems):
        self._each("start", srcs, outs, sems)

    def mid(self, srcs, outs, sems):
        self._each("mid", srcs, outs, sems)

    def finish(self, srcs, outs, sems):
        self._each("finish", srcs, outs, sems)


def _call(name, main, grid, in_specs, out_specs, out_shape, args, scratch=(), comm=None, prefetch=None):
    comm = comm or _NoComm()
    counts = [len(in_specs), len(comm.inputs), len(out_shape), len(comm.out_shapes), len(scratch), len(comm.scratch)]
    n_steps = math.prod(grid)
    hosted = bool(comm.inputs)

    def body(*refs):
        if prefetch is not None:
            refs = refs[1:]
        parts, at = [], 0
        for n in counts:
            parts.append(refs[at:at + n])
            at += n
        ins, c_in, outs, c_out, scr, c_scr = parts
        step = pl.program_id(0)
        for d in range(1, len(grid)):
            step = step * grid[d] + pl.program_id(d)
        if hosted:
            @pl.when(step == 0)
            def _():
                comm.start(c_in, c_out, c_scr)
        main(ins, outs, scr)
        if hosted:
            @pl.when(step == max(n_steps - 2, 0))
            def _():
                comm.mid(c_in, c_out, c_scr)

            @pl.when(step == n_steps - 1)
            def _():
                comm.finish(c_in, c_out, c_scr)

    all_in = list(in_specs) + [HBM] * counts[1]
    all_out = list(out_specs) + [HBM] * counts[3]
    all_scratch = list(scratch) + list(comm.scratch)
    shapes = list(out_shape) + list(comm.out_shapes)
    if prefetch is None:
        res = pl.pallas_call(body, name=name, grid=grid, in_specs=all_in, out_specs=all_out, out_shape=shapes,
                             scratch_shapes=all_scratch, compiler_params=_cparams())(*args, *comm.inputs)
    else:
        res = pl.pallas_call(
            body, name=name, out_shape=shapes,
            grid_spec=pltpu.PrefetchScalarGridSpec(num_scalar_prefetch=1, grid=grid, in_specs=all_in,
                                                   out_specs=all_out, scratch_shapes=all_scratch),
            compiler_params=_cparams())(prefetch, *args, *comm.inputs)
    return res[:counts[2]], res[counts[2]:]


def _rms_fwd(name, x, g, comm=None):
    T, D = x.shape
    tm = min(T, 512)

    def main(ins, outs, scr):
        x_ref, g_ref = ins
        xv = x_ref[...]
        r = lax.rsqrt(jnp.mean(xv * xv, axis=-1, keepdims=True) + RMS_EPS)
        outs[0][...] = (xv * r * g_ref[...]).astype(outs[0].dtype)

    tile = pl.BlockSpec((tm, D), lambda i: (i, 0))
    (h,), extra = _call(name, main, (T // tm,), [tile, pl.BlockSpec((1, D), lambda i: (0, 0))], [tile],
                        [jax.ShapeDtypeStruct((T, D), BF16)], (x, g), comm=comm)
    return h, extra


def _final_loss_bwd(name, x, g, target, out_scale):
    T, D = x.shape
    tm = min(T, 512)

    def main(ins, outs, scr):
        x_ref, g_ref, t_ref = ins
        dx_ref, dxb_ref, dg_ref, loss_ref = outs
        i = pl.program_id(0)
        xv = x_ref[...]
        r = lax.rsqrt(jnp.mean(xv * xv, axis=-1, keepdims=True) + RMS_EPS)
        xhat = xv * r
        err = xhat * g_ref[...] - t_ref[...]

        @pl.when(i == 0)
        def _():
            dg_ref[...] = jnp.zeros_like(dg_ref)
            loss_ref[...] = jnp.zeros_like(loss_ref)

        loss_ref[...] += jnp.broadcast_to(jnp.sum(err * err) * (0.5 / D), loss_ref.shape)
        dy = err * (1.0 / D)
        dg_ref[...] += jnp.sum(dy * xhat, axis=0, keepdims=True)
        dxhat = dy * g_ref[...]
        dx = r * (dxhat - xhat * jnp.mean(dxhat * xhat, axis=-1, keepdims=True))
        dx_ref[...] = dx
        dxb_ref[...] = (out_scale * dx).astype(dxb_ref.dtype)

    tile = pl.BlockSpec((tm, D), lambda i: (i, 0))
    vec = pl.BlockSpec((1, D), lambda i: (0, 0))
    outs, _ = _call(name, main, (T // tm,), [tile, vec, tile],
                    [tile, tile, vec, pl.BlockSpec((1, 128), lambda i: (0, 0))],
                    [jax.ShapeDtypeStruct((T, D), F32), jax.ShapeDtypeStruct((T, D), BF16),
                     jax.ShapeDtypeStruct((1, D), F32), jax.ShapeDtypeStruct((1, 128), F32)], (x, g, target))
    return outs


def _mm_nstream(name, a, ws, w_sel, w_form, comps, out_dtypes, epilogue, cn, rows=1024, comm=None):
    T, K = a.shape
    N = ws[0].shape[1]
    rows = min(rows, T)
    assert N % cn == 0 and T % rows == 0
    n_w, n_c = len(ws), len(comps)

    def main(ins, outs, scr):
        a_ref = ins[0]
        w_refs = ins[1:1 + n_w]
        c_refs = ins[1 + n_w:]

        for r in range(T // rows):
            sl = slice(r * rows, (r + 1) * rows)
            a_blk = a_ref[sl, :]
            dots = [_dot(a_blk, w_ref[...], w_form) for w_ref in w_refs]
            res = epilogue(dots, [c_ref[sl, :] for c_ref in c_refs])
            for o_ref, o in zip(outs, res):
                o_ref[sl, :] = o.astype(o_ref.dtype)

    if w_form == "nt":
        w_specs = [pl.BlockSpec((None, cn, K), functools.partial(lambda j, s: (s, j, 0), s=s)) for s in w_sel]
    else:
        w_specs = [pl.BlockSpec((K, cn), lambda j: (0, j)) for _ in ws]
    chunk = pl.BlockSpec((T, cn), lambda j: (0, j))
    return _call(name, main, (N // cn,), [pl.BlockSpec((T, K), lambda j: (0, 0))] + w_specs + [chunk] * n_c,
                 [chunk] * len(out_dtypes), [jax.ShapeDtypeStruct((T, N), dt) for dt in out_dtypes],
                 (a, *ws, *comps), comm=comm)


def _mm_mstream(name, as_, ws, w_sel, w_form, extras, outs_desc, epilogue, tm=512, comm=None):
    T = as_[0].shape[0]
    tm = min(tm, T)
    n_a = len(as_)
    w_shapes = [w.shape[-2:] for w in ws]
    N = w_shapes[0][1] if w_form == "nn" else w_shapes[0][0]

    def main(ins, outs, scr):
        a_refs = ins[:n_a]
        w_refs = ins[n_a:2 * n_a]
        acc = None
        for a_ref, w_ref in zip(a_refs, w_refs):
            d = _dot(a_ref[...], w_ref[...], w_form)
            acc = d if acc is None else acc + d
        epilogue(acc, ins[2 * n_a:], outs)

    kind_spec = {"tile": pl.BlockSpec((tm, N), lambda i: (i, 0)), "vec": pl.BlockSpec((1, N), lambda i: (0, 0))}
    kind_shape = {"tile": (T, N), "vec": (1, N)}
    a_specs = [pl.BlockSpec((tm, a.shape[1]), lambda i: (i, 0)) for a in as_]
    w_specs = []
    for w, s in zip(ws, w_sel):
        if w.ndim == 3:
            w_specs.append(pl.BlockSpec((None,) + tuple(w.shape[1:]), functools.partial(lambda i, s: (s, 0, 0), s=s),
                                        pipeline_mode=pl.Buffered(1)))
        else:
            w_specs.append(pl.BlockSpec(tuple(w.shape), lambda i: (0, 0), pipeline_mode=pl.Buffered(1)))
    args = list(as_) + list(ws) + [e for e, _ in extras]
    return _call(name, main, (T // tm,), a_specs + w_specs + [kind_spec[k] for _, k in extras],
                 [kind_spec[k] for _, k in outs_desc],
                 [jax.ShapeDtypeStruct(kind_shape[k], dt) for dt, k in outs_desc], args, comm=comm)


def _plain_epilogue(acc, ex, outs):
    outs[0][...] = acc.astype(outs[0].dtype)


def _residual_epilogue(scale):
    def ep(acc, ex, outs):
        outs[0][...] = ex[0][...] + scale * acc
    return ep


def _residual_rms_epilogue(scale):
    def ep(acc, ex, outs):
        xv = ex[0][...] + scale * acc
        outs[0][...] = xv
        r = lax.rsqrt(jnp.mean(xv * xv, axis=-1, keepdims=True) + RMS_EPS)
        outs[1][...] = (xv * r * ex[1][...]).astype(outs[1].dtype)
    return ep


def _rms_bwd_epilogue(out_scale):
    def ep(acc, ex, outs):
        x_ref, g_ref, dres_ref = ex
        dx_ref, dxb_ref, dg_ref = outs
        xv = x_ref[...]
        r = lax.rsqrt(jnp.mean(xv * xv, axis=-1, keepdims=True) + RMS_EPS)
        xhat = xv * r

        @pl.when(pl.program_id(0) == 0)
        def _():
            dg_ref[...] = jnp.zeros_like(dg_ref)

        dg_ref[...] += jnp.sum(acc * xhat, axis=0, keepdims=True)
        dxhat = acc * g_ref[...]
        dx = r * (dxhat - xhat * jnp.mean(dxhat * xhat, axis=-1, keepdims=True)) + dres_ref[...]
        dx_ref[...] = dx
        dxb_ref[...] = (out_scale * dx).astype(dxb_ref.dtype)
    return ep


def _mm_tn(name, a, b, tmo, tno, out_dtype, tk=1024, comm=None):
    T, Ma = a.shape
    Nb = b.shape[1]
    tk = min(tk, T)
    nk = T // tk

    def main(ins, outs, scr):
        a_ref, b_ref = ins
        (acc_ref,) = scr
        k = pl.program_id(2)

        @pl.when(k == 0)
        def _():
            acc_ref[...] = jnp.zeros_like(acc_ref)

        acc_ref[...] += _dot(a_ref[...], b_ref[...], "tn")

        @pl.when(k == nk - 1)
        def _():
            outs[0][...] = acc_ref[...].astype(outs[0].dtype)

    (out,), extra = _call(
        name, main, (Ma // tmo, Nb // tno, nk),
        [pl.BlockSpec((tk, tmo), lambda i, j, k: (k, i)), pl.BlockSpec((tk, tno), lambda i, j, k: (k, j))],
        [pl.BlockSpec((tmo, tno), lambda i, j, k: (i, j))], [jax.ShapeDtypeStruct((Ma, Nb), out_dtype)],
        (a, b), scratch=[pltpu.VMEM((tmo, tno), F32)], comm=comm)
    return out, extra


def _silu_mul_epilogue(dots, comps):
    g, u = dots
    act = g * _sigmoid(g) * u
    return [g, u, act]


def _up_act_epilogue(dots, comps):
    (u,) = dots
    g = comps[0].astype(F32)
    return [u, g * _sigmoid(g) * u]


def _dact_epilogue(dots, comps):
    (dact,) = dots
    g = comps[0].astype(F32)
    u = comps[1].astype(F32)
    s = _sigmoid(g)
    silu = g * s
    dsilu = s + silu * (1.0 - s)
    return [dact * u * dsilu, dact * silu]


def _identity_epilogue(dots, comps):
    return list(dots)


def _swap_halves(x):
    lane = lax.broadcasted_iota(jnp.int32, x.shape, 1)
    first = (lane % DK) < (DK // 2)
    return jnp.where(first, pltpu.roll(x, 128 - DK // 2, 1), pltpu.roll(x, DK // 2, 1))


def _rotary(t, cos, sin_signed):
    halves = []
    for p in range(QK_W // 128):
        th = t[:, 128 * p:128 * (p + 1)]
        halves.append(th * cos + _swap_halves(th) * sin_signed)
    return jnp.concatenate(halves, axis=1)


def _rotary_transposed(d, cos, sin_signed):
    halves = []
    for p in range(QK_W // 128):
        dh = d[:, 128 * p:128 * (p + 1)]
        halves.append(dh * cos + _swap_halves(dh * sin_signed))
    return jnp.concatenate(halves, axis=1)


def _log_sigmoid(x):
    return jnp.minimum(x, 0.0) - jnp.log(1.0 + jnp.exp(-jnp.abs(x)))


def _attn_masks():
    row = lax.broadcasted_iota(jnp.int32, (SUPER, SUPER), 0)
    col = lax.broadcasted_iota(jnp.int32, (SUPER, SUPER), 1)
    same = (row // CHUNK) == (col // CHUNK)
    return row, col, same


def _group_inputs(grp, pr, cos, sin_signed, lg, wa2, ba):
    if grp == 0:
        q = _rotary(pr[:, C_RQ:C_RQ + QK_W], cos, sin_signed)
        k = _rotary(pr[:, C_RK:C_RK + QK_W], cos, sin_signed) * (DK ** -0.5)
        v = pr[:, C_RV:C_RV + V_W]
        gate = pr[:, C_RG:C_RG + V_W]
        pos = lax.broadcasted_iota(jnp.int32, (SUPER, QK_W), 0).astype(F32) + 1.0
        return q, k, v, gate, pos * lg, None, None
    q = pr[:, C_GQ:C_GQ + QK_W] * (DK ** -0.5)
    k = pr[:, C_GK:C_GK + QK_W]
    v = pr[:, C_GV:C_GV + V_W]
    gate = pr[:, C_GG:C_GG + V_W]
    glow = pr[:, C_GL:C_GL + GL_W]
    logit = _dot(glow.astype(BF16), wa2.astype(BF16), "nn") + ba
    la = _log_sigmoid(logit) * (1.0 / GATE_NORM)
    row, col, _ = _attn_masks()
    lower = (col <= row).astype(F32)
    b_cum = _dot(lower, la, "nn", precision=lax.Precision.HIGHEST)
    return q, k, v, gate, b_cum, glow, logit


def _decay_factors(q, k, b_cum):
    c = b_cum[SUPER // 2 - 1:SUPER // 2, :]
    bl = b_cum[SUPER - 1:SUPER, :]
    e1 = jnp.exp(b_cum - c)
    e2 = jnp.exp(c - b_cum)
    e_b = jnp.exp(b_cum)
    e_l = jnp.exp(bl - b_cum)
    return dict(e1=e1, e2=e2, eb=e_b, el=e_l, ebl=jnp.exp(bl),
                qp=q * e1, qm=q * e2, kp=k * e1, km=k * e2, qs=q * e_b, kl=k * e_l)


def _state_block_mask():
    r = lax.broadcasted_iota(jnp.int32, (V_W, QK_W), 0)
    c = lax.broadcasted_iota(jnp.int32, (V_W, QK_W), 1)
    return (r // DV) == (c // DK)


def _attn_fwd(proj, cos, sin_signed, lg, wa2p, ba, gn_ret, gn_gla, comm=None):
    T = proj.shape[0]
    n_s = T // SUPER

    def main(ins, outs, scr):
        pr_ref, cos_ref, sin_ref, lg_ref, wa2_ref, ba_ref, gr_ref, gg_ref = ins
        o_ref, y_ref, st_ref = outs
        (s_ref,) = scr
        i = pl.program_id(0)

        @pl.when(i == 0)
        def _():
            s_ref[...] = jnp.zeros_like(s_ref)

        pr = pr_ref
        row, col, same = _attn_masks()
        m1 = col <= row
        m2 = jnp.logical_and(col > row, same)
        lane = lax.broadcasted_iota(jnp.int32, (1, QK_W), 1)
        blockmask = _state_block_mask()
        for grp in range(2):
            q, k, v, gate, b_cum, _, _ = _group_inputs(grp, pr, cos_ref[...], sin_ref[...], lg_ref[...],
                                                      wa2_ref[...], ba_ref[...])
            f = _decay_factors(q, k, b_cum)
            gn = gr_ref[...] if grp == 0 else gg_ref[...]
            s_prev = s_ref[grp]
            st_ref[0, grp] = s_prev
            o_inter = _dot(f["qs"].astype(BF16), s_prev.astype(BF16), "nt")
            kmb = f["km"].astype(BF16)
            kpb = f["kp"].astype(BF16)
            vb = v.astype(BF16)
            for h in range(HEADS):
                hm = (lane // DK) == h
                a1 = _dot(jnp.where(hm, f["qp"], 0.0).astype(BF16), kmb, "nt")
                a2 = _dot(jnp.where(hm, f["qm"], 0.0).astype(BF16), kpb, "nt")
                a = jnp.where(m1, a1, jnp.where(m2, a2, 0.0))
                lo = grp * V_W + h * DV
                o_h = _dot(a.astype(BF16), vb[:, h * DV:(h + 1) * DV], "nn") + o_inter[:, h * DV:(h + 1) * DV]
                o_ref[:, lo:lo + DV] = o_h
                r = lax.rsqrt(jnp.mean(o_h * o_h, axis=-1, keepdims=True) + RMS_EPS)
                gte = gate[:, h * DV:(h + 1) * DV]
                y = o_h * r * gn[:, h * DV:(h + 1) * DV] * (gte * _sigmoid(gte))
                y_ref[:, lo:lo + DV] = y.astype(y_ref.dtype)
            upd = _dot(vb, f["kl"].astype(BF16), "tn")
            s_ref[grp] = s_prev * f["ebl"] + jnp.where(blockmask, upd, 0.0)

    const = lambda shape: pl.BlockSpec(shape, lambda i: tuple(0 for _ in shape))
    return _call(
        "attn_fwd", main, (n_s,),
        [pl.BlockSpec((SUPER, PROJ_W), lambda i: (i, 0)),
         pl.BlockSpec((SUPER, 128), lambda i: (i, 0)), pl.BlockSpec((SUPER, 128), lambda i: (i, 0)),
         const((1, QK_W)), const((GL_W, QK_W)), const((1, QK_W)), const((1, V_W)), const((1, V_W))],
        [pl.BlockSpec((SUPER, 2 * V_W), lambda i: (i, 0)), pl.BlockSpec((SUPER, 2 * V_W), lambda i: (i, 0)),
         pl.BlockSpec((1, 2, V_W, QK_W), lambda i: (i, 0, 0, 0))],
        [jax.ShapeDtypeStruct((T, 2 * V_W), F32), jax.ShapeDtypeStruct((T, 2 * V_W), BF16),
         jax.ShapeDtypeStruct((n_s, 2, V_W, QK_W), F32)],
        (proj, cos, sin_signed, lg, wa2p, ba, gn_ret, gn_gla),
        scratch=[pltpu.VMEM((2, V_W, QK_W), F32)], comm=comm)


def _attn_bwd(proj, cos, sin_signed, lg, wa2p, ba, gn_ret, gn_gla, o, dy, states, comm=None):
    T = proj.shape[0]
    n_s = T // SUPER

    def main(ins, outs, scr):
        pr_ref, cos_ref, sin_ref, lg_ref, wa2_ref, ba_ref, gr_ref, gg_ref, o_ref, dy_ref, st_ref = ins
        dp_ref, dgr_ref, dgg_ref, dba_ref, dwa_ref = outs
        (ds_ref,) = scr
        i = pl.program_id(0)

        @pl.when(i == 0)
        def _():
            ds_ref[...] = jnp.zeros_like(ds_ref)
            dgr_ref[...] = jnp.zeros_like(dgr_ref)
            dgg_ref[...] = jnp.zeros_like(dgg_ref)
            dba_ref[...] = jnp.zeros_like(dba_ref)
            dwa_ref[...] = jnp.zeros_like(dwa_ref)

        pr = pr_ref
        cos = cos_ref[...]
        sin_signed = sin_ref[...]
        row, col, same = _attn_masks()
        m1 = col <= row
        m2 = jnp.logical_and(col > row, same)
        m1t = row <= col
        m2t = jnp.logical_and(row > col, same)
        lane = lax.broadcasted_iota(jnp.int32, (1, QK_W), 1)
        blockmask = _state_block_mask()
        for grp in range(2):
            q, k, v, gate, b_cum, glow, logit = _group_inputs(grp, pr, cos, sin_signed, lg_ref[...],
                                                              wa2_ref[...], ba_ref[...])
            f = _decay_factors(q, k, b_cum)
            gn = gr_ref[...] if grp == 0 else gg_ref[...]
            dgn_ref = dgr_ref if grp == 0 else dgg_ref
            do_parts, dgate_parts, dgn_parts = [], [], []
            for h in range(HEADS):
                lo = grp * V_W + h * DV
                o_h = o_ref[:, lo:lo + DV]
                r = lax.rsqrt(jnp.mean(o_h * o_h, axis=-1, keepdims=True) + RMS_EPS)
                n = o_h * r
                gte = gate[:, h * DV:(h + 1) * DV]
                sg = _sigmoid(gte)
                dy_h = dy_ref[:, lo:lo + DV]
                gn_h = gn[:, h * DV:(h + 1) * DV]
                dgate_parts.append(dy_h * n * gn_h * (sg * (1.0 + gte * (1.0 - sg))))
                dz = dy_h * (gte * sg)
                dgn_parts.append(jnp.sum(dz * n, axis=0, keepdims=True))
                dn = dz * gn_h
                do_parts.append(r * (dn - n * jnp.mean(dn * n, axis=-1, keepdims=True)))
            dgn_ref[...] += jnp.concatenate(dgn_parts, axis=1)
            dgate = jnp.concatenate(dgate_parts, axis=1)
            do = jnp.concatenate(do_parts, axis=1)
            dob = do.astype(BF16)
            vb = v.astype(BF16)
            s_prev = st_ref[0, grp]
            ds_new = ds_ref[grp]
            dsb = ds_new.astype(BF16)
            qpb, qmb = f["qp"].astype(BF16), f["qm"].astype(BF16)
            kpb, kmb = f["kp"].astype(BF16), f["km"].astype(BF16)
            dqp = jnp.zeros((SUPER, QK_W), F32)
            dqm = jnp.zeros((SUPER, QK_W), F32)
            dkp = jnp.zeros((SUPER, QK_W), F32)
            dkm = jnp.zeros((SUPER, QK_W), F32)
            dv_parts = []
            for h in range(HEADS):
                hm = (lane // DK) == h
                qp_h = jnp.where(hm, f["qp"], 0.0).astype(BF16)
                qm_h = jnp.where(hm, f["qm"], 0.0).astype(BF16)
                kp_h = jnp.where(hm, f["kp"], 0.0).astype(BF16)
                km_h = jnp.where(hm, f["km"], 0.0).astype(BF16)
                at = jnp.where(m1t, _dot(km_h, qpb, "nt"), jnp.where(m2t, _dot(kp_h, qmb, "nt"), 0.0))
                do_h = dob[:, h * DV:(h + 1) * DV]
                v_h = vb[:, h * DV:(h + 1) * DV]
                dv_parts.append(_dot(at.astype(BF16), do_h, "nn"))
                da = _dot(do_h, v_h, "nt")
                dat = _dot(v_h, do_h, "nt")
                da1 = jnp.where(m1, da, 0.0).astype(BF16)
                da2 = jnp.where(m2, da, 0.0).astype(BF16)
                da1t = jnp.where(m1t, dat, 0.0).astype(BF16)
                da2t = jnp.where(m2t, dat, 0.0).astype(BF16)
                dqp = dqp + _dot(da1, km_h, "nn")
                dqm = dqm + _dot(da2, kp_h, "nn")
                dkm = dkm + _dot(da1t, qp_h, "nn")
                dkp = dkp + _dot(da2t, qm_h, "nn")
            klb = f["kl"].astype(BF16)
            qsb = f["qs"].astype(BF16)
            dqs = _dot(dob, s_prev.astype(BF16), "nn")
            dkl = _dot(vb, dsb, "nn")
            dv = jnp.concatenate(dv_parts, axis=1) + _dot(klb, dsb, "nt")
            ds_ref[grp] = ds_new * f["ebl"] + jnp.where(blockmask, _dot(dob, qsb, "tn"), 0.0)
            dq = dqp * f["e1"] + dqm * f["e2"] + dqs * f["eb"]
            dk = dkm * f["e2"] + dkp * f["e1"] + dkl * f["el"]
            if grp == 0:
                dq = _rotary_transposed(dq, cos, sin_signed)
                dk = _rotary_transposed(dk * (DK ** -0.5), cos, sin_signed)
                dp_ref[:, C_RQ:C_RQ + QK_W] = dq.astype(dp_ref.dtype)
                dp_ref[:, C_RK:C_RK + QK_W] = dk.astype(dp_ref.dtype)
                dp_ref[:, C_RV:C_RV + V_W] = dv.astype(dp_ref.dtype)
                dp_ref[:, C_RG:C_RG + V_W] = dgate.astype(dp_ref.dtype)
            else:
                dkl_kl = dkl * klb.astype(F32)
                db = (dqp * qpb.astype(F32) - dkm * kmb.astype(F32) - dqm * qmb.astype(F32)
                      + dkp * kpb.astype(F32) + dqs * qsb.astype(F32) - dkl_kl)
                last = (jnp.sum(dkl_kl, axis=0, keepdims=True)
                        + f["ebl"] * jnp.sum(s_prev * ds_new, axis=0, keepdims=True))
                rowq = lax.broadcasted_iota(jnp.int32, (SUPER, QK_W), 0)
                db = db + jnp.where(rowq == SUPER - 1, last, 0.0)
                upper = (col >= row).astype(F32)
                dla = _dot(upper, db, "nn", precision=lax.Precision.HIGHEST)
                dlogit = dla * (1.0 / GATE_NORM) * (1.0 - _sigmoid(logit))
                dlb = dlogit.astype(BF16)
                dglow = _dot(dlb, wa2_ref[...].astype(BF16), "nt")
                dwa_ref[...] += _dot(glow.astype(BF16), dlb, "tn")
                dba_ref[...] += jnp.sum(dlogit, axis=0, keepdims=True)
                dp_ref[:, C_GQ:C_GQ + QK_W] = (dq * (DK ** -0.5)).astype(dp_ref.dtype)
                dp_ref[:, C_GK:C_GK + QK_W] = dk.astype(dp_ref.dtype)
                dp_ref[:, C_GV:C_GV + V_W] = dv.astype(dp_ref.dtype)
                dp_ref[:, C_GG:C_GG + V_W] = dgate.astype(dp_ref.dtype)
                dp_ref[:, C_GL:C_GL + GL_W] = dglow.astype(dp_ref.dtype)

    rev = lambda i: n_s - 1 - i
    const = lambda shape: pl.BlockSpec(shape, lambda i: tuple(0 for _ in shape))
    return _call(
        "attn_bwd", main, (n_s,),
        [pl.BlockSpec((SUPER, PROJ_W), lambda i: (rev(i), 0)),
         pl.BlockSpec((SUPER, 128), lambda i: (rev(i), 0)), pl.BlockSpec((SUPER, 128), lambda i: (rev(i), 0)),
         const((1, QK_W)), const((GL_W, QK_W)), const((1, QK_W)), const((1, V_W)), const((1, V_W)),
         pl.BlockSpec((SUPER, 2 * V_W), lambda i: (rev(i), 0)),
         pl.BlockSpec((SUPER, 2 * V_W), lambda i: (rev(i), 0)),
         pl.BlockSpec((1, 2, V_W, QK_W), lambda i: (rev(i), 0, 0, 0))],
        [pl.BlockSpec((SUPER, PROJ_W), lambda i: (rev(i), 0)),
         const((1, V_W)), const((1, V_W)), const((1, QK_W)), const((GL_W, QK_W))],
        [jax.ShapeDtypeStruct((T, PROJ_W), BF16),
         jax.ShapeDtypeStruct((1, V_W), F32), jax.ShapeDtypeStruct((1, V_W), F32),
         jax.ShapeDtypeStruct((1, QK_W), F32), jax.ShapeDtypeStruct((GL_W, QK_W), F32)],
        (proj, cos, sin_signed, lg, wa2p, ba, gn_ret, gn_gla, o, dy, states),
        scratch=[pltpu.VMEM((2, V_W, QK_W), F32)], comm=comm)


def _rotary_tables(T):
    half = DK // 2
    inv = ROPE_BASE ** (-jnp.arange(half, dtype=F32) * 2.0 / DK)
    ang = jnp.arange(T, dtype=F32)[:, None] * inv[None, :]
    cos, sin = jnp.cos(ang), jnp.sin(ang)
    cos_head = jnp.concatenate([cos, cos], axis=1)
    sin_head = jnp.concatenate([-sin, sin], axis=1)
    return jnp.tile(cos_head, (1, 128 // DK)), jnp.tile(sin_head, (1, 128 // DK))


def _sum_devices(name, gathered, m_per):
    def body(g_ref, o_ref):
        acc = g_ref[0:m_per, :]
        for k in range(1, N_DEV):
            acc = acc + g_ref[k * m_per:(k + 1) * m_per, :]
        o_ref[...] = acc

    return pl.pallas_call(body, name=name, out_shape=jax.ShapeDtypeStruct((m_per, 128), F32))(gathered)


def _pair_sum(name, grad, landed, core, comm=None):
    R, C = grad.shape[2:]

    def main(ins, outs, scr):
        outs[0][...] = (ins[0][...].astype(F32) + ins[1][...].astype(F32)).astype(outs[0].dtype)

    blk = pl.BlockSpec((None, R, C), lambda j, s: (j, 0, 0))
    (out,), extra = _call(name, main, (4,), [pl.BlockSpec((None, None, R, C), lambda j, s: (j, s[0], 0, 0)), blk],
                          [blk], [jax.ShapeDtypeStruct((4, R, C), BF16)], (grad, landed), prefetch=core, comm=comm)
    return out, extra


def _owner_sums(name, items, owner, comm=None):
    counts = [1 + len(landed) for _, landed in items]

    def main(ins, outs, scr):
        at = 0
        for o_ref, n in zip(outs, counts):
            acc = ins[at][...].astype(F32)
            for l_ref in ins[at + 1:at + n]:
                for j in range(l_ref.shape[0]):
                    acc = acc + l_ref[j].astype(F32)
            o_ref[...] = acc
            at += n

    once = pl.Buffered(1)
    in_specs, out_specs, out_shape, args = [], [], [], []
    for grad, landed in items:
        R, C = grad.shape[-2:]
        if grad.ndim == 4:
            in_specs.append(pl.BlockSpec((None, None, R, C), lambda i, s: (s[0], s[1], 0, 0), pipeline_mode=once))
        else:
            in_specs.append(pl.BlockSpec((None, R, C), lambda i, s: (s[0], 0, 0), pipeline_mode=once))
        in_specs += [pl.BlockSpec(tuple(l.shape), lambda i, s: (0, 0, 0), pipeline_mode=once) for l in landed]
        out_specs.append(pl.BlockSpec((R, C), lambda i, s: (0, 0)))
        out_shape.append(jax.ShapeDtypeStruct((R, C), F32))
        args += [grad, *landed]
    return _call(name, main, (1,), in_specs, out_specs, out_shape, args, comm=comm, prefetch=owner)


def _rms_bwd(name, x, g, dh, dres, out_scale):
    T, D = x.shape
    tm = min(T, 512)
    ep = _rms_bwd_epilogue(out_scale)

    def main(ins, outs, scr):
        x_ref, g_ref, dh_ref, dres_ref = ins
        ep(dh_ref[...], (x_ref, g_ref, dres_ref), outs)

    tile = pl.BlockSpec((tm, D), lambda i: (i, 0))
    vec = pl.BlockSpec((1, D), lambda i: (0, 0))
    outs, _ = _call(name, main, (T // tm,), [tile, vec, tile, tile], [tile, tile, vec],
                    [jax.ShapeDtypeStruct((T, D), F32), jax.ShapeDtypeStruct((T, D), BF16),
                     jax.ShapeDtypeStruct((1, D), F32)], (x, g, dh, dres))
    return outs


def _adamw_group(name, items, n_blocks, comm=None):
    n = len(items)

    def main(ins, outs, scr):
        for p in range(n):
            g_ref, w_ref, m_ref, v_ref = ins[4 * p:4 * p + 4]
            d_ref, nm_ref, nv_ref = outs[3 * p:3 * p + 3]
            gv = g_ref[...]
            nm = ADAM_B1 * m_ref[...] + (1.0 - ADAM_B1) * gv
            nv = ADAM_B2 * v_ref[...] + (1.0 - ADAM_B2) * (gv * gv)
            m_hat = nm / (1.0 - ADAM_B1 ** ADAM_STEP)
            v_hat = nv / (1.0 - ADAM_B2 ** ADAM_STEP)
            d_ref[...] = -ADAM_LR * (m_hat / (jnp.sqrt(v_hat) + ADAM_EPS) + ADAM_WD * w_ref[...])
            nm_ref[...] = nm
            nv_ref[...] = nv

    in_specs, out_specs, out_shape, args = [], [], [], []
    for item in items:
        R, C = item[1].shape
        assert R % n_blocks == 0
        spec = pl.BlockSpec((R // n_blocks, C), lambda i: (i, 0))
        in_specs += [spec] * 4
        out_specs += [spec] * 3
        out_shape += [jax.ShapeDtypeStruct((R, C), F32)] * 3
        args += list(item)
    outs, extra = _call(name, main, (n_blocks,), in_specs, out_specs, out_shape, args, comm=comm)
    return [tuple(outs[3 * p:3 * p + 3]) for p in range(n)], extra


SMALL_ORDER = ("ffn1", "mix", "ffn2", "final", "ret", "gla", "b_a")


def kernel(x, ffn1_norm_g, ffn1_w_gate, ffn1_w_up, ffn1_w_down, mix_norm_g, w_in, ret_norm_g, gla_w_a2, gla_b_a, gla_norm_g, w_out, ffn2_norm_g, ffn2_w_gate, ffn2_w_up, ffn2_w_down, final_norm_g, loss_target, m_ffn1_norm_g, m_ffn1_w_gate, m_ffn1_w_up, m_ffn1_w_down, m_mix_norm_g, m_w_in, m_ret_norm_g, m_gla_w_a2, m_gla_b_a, m_gla_norm_g, m_w_out, m_ffn2_norm_g, m_ffn2_w_gate, m_ffn2_w_up, m_ffn2_w_down, m_final_norm_g, v_ffn1_norm_g, v_ffn1_w_gate, v_ffn1_w_up, v_ffn1_w_down, v_mix_norm_g, v_w_in, v_ret_norm_g, v_gla_w_a2, v_gla_b_a, v_gla_norm_g, v_w_out, v_ffn2_norm_g, v_ffn2_w_gate, v_ffn2_w_up, v_ffn2_w_down, v_final_norm_g):
    xi, yi, ci = _coords()
    dev = 4 * xi + 2 * yi + ci
    owner = jnp.stack([2 * xi + yi, ci]).astype(jnp.int32)

    x0, target = x[0], loss_target[0]
    T, D = x0.shape
    fb = ffn1_w_gate.shape[2]
    ib = w_in.shape[2]
    ab = gla_w_a2.shape[2]
    F = N_DEV * fb
    cos, sin_signed = _rotary_tables(T)
    lg = jnp.repeat(jnp.log(1.0 - 2.0 ** (-5.0 - jnp.arange(HEADS, dtype=F32))), DK)[None, :]
    g_final = final_norm_g.reshape(1, D)

    g1_loc = ffn1_w_gate[0].T[None].astype(BF16)
    u1_loc = ffn1_w_up[0].T[None].astype(BF16)
    d1_loc = ffn1_w_down.astype(BF16)
    g2_loc = ffn2_w_gate[0].T[None].astype(BF16)
    u2_loc = ffn2_w_up[0].T[None].astype(BF16)
    d2_loc = ffn2_w_down.astype(BF16)
    in_loc = w_in[0].T.astype(BF16)
    out_loc = w_out[0].astype(BF16)

    h1, (g1,) = _rms_fwd("ffn1_rms", x0, ffn1_norm_g, comm=_AllGather([g1_loc], ["stack"]))
    g1 = g1.reshape(1, F, D)
    (gate1,), (u1,) = _mm_nstream("ffn1_gate", h1, [g1], [0], "nt", [], [BF16], _identity_epilogue, cn=256,
                                  comm=_AllGather([u1_loc], ["stack"]))
    u1 = u1.reshape(1, F, D)
    (up1, act1), (d1,) = _mm_nstream("ffn1_up", h1, [u1], [0], "nt", [gate1], [BF16, BF16], _up_act_epilogue, cn=256,
                                     comm=_AllGather([d1_loc], ["stack"]))
    d1 = d1.reshape(1, F, D)
    f32_tile, bf16_tile, f32_vec = (F32, "tile"), (BF16, "tile"), (F32, "vec")
    (x1, h2), (in_all, a_all) = _mm_mstream(
        "ffn1_down", [act1], [d1], [0], "nn", [(x0, "tile"), (mix_norm_g, "vec")], [f32_tile, bf16_tile],
        _residual_rms_epilogue(0.5), comm=_AllGather([in_loc, gla_w_a2[0]], ["plain", "plain"]))
    w_in_t = jnp.pad(in_all.reshape(1, N_DEV * ib, D), ((0, 0), (0, PROJ_W - N_DEV * ib), (0, 0)))
    wa2 = jnp.transpose(a_all, (1, 0, 2)).reshape(GATE_RANK, N_DEV * ab)
    wa2p = jnp.pad(wa2, ((0, GL_W - GATE_RANK), (0, 0)))

    (proj,), (g2,) = _mm_nstream("mix_proj", h2, [w_in_t], [0], "nt", [], [F32], _identity_epilogue, cn=640,
                                 comm=_AllGather([g2_loc], ["stack"]))
    (o, ymix, states), (u2, out_all) = _attn_fwd(proj, cos, sin_signed, lg, wa2p, gla_b_a, ret_norm_g, gla_norm_g,
                                                 comm=_AllGather([u2_loc, out_loc], ["stack", "plain"]))
    w_out_full = out_all.reshape(D, D)
    (x2, h3), _ = _mm_mstream("mix_out", [ymix], [w_out_full], [0], "nn", [(x1, "tile"), (ffn2_norm_g, "vec")],
                              [f32_tile, bf16_tile], _residual_rms_epilogue(1.0))
    g2, u2 = g2.reshape(1, F, D), u2.reshape(1, F, D)

    (gate2, up2, act2), (d2,) = _mm_nstream(
        "ffn2_up", h3, [g2, u2], [0, 0], "nt", [], [BF16, BF16, BF16], _silu_mul_epilogue, cn=256,
        comm=_AllGather([d2_loc], ["stack"]))
    d2 = d2.reshape(1, F, D)
    (x3,), _ = _mm_mstream("ffn2_down", [act2], [d2], [0], "nn", [(x2, "tile")], [f32_tile], _residual_epilogue(0.5))

    dx3, dy3b, d_final, loss = _final_loss_bwd("final", x3, g_final, target, 0.5)

    dwd2, _ = _mm_tn("ffn2b_dwd", act2, dy3b, F // 2, D, BF16)
    dwd2 = dwd2.reshape(4, 2, fb, D)
    (dgate2, dup2), (l_wd2_near,) = _mm_nstream("ffn2b_dact", dy3b, [d2], [0], "nt", [gate2, up2], [BF16, BF16],
                                                _dact_epilogue, cn=256, comm=_ReduceScatter([(dwd2, NEAR)]))
    dwg2, (l_wd2_far,) = _mm_tn("ffn2b_dwg", dgate2, h3, F // 2, D, BF16, comm=_ReduceScatter([(dwd2, FAR)]))
    dwg2 = dwg2.reshape(4, 2, fb, D)
    dwu2, (l_wg2_near,) = _mm_tn("ffn2b_dwu", dup2, h3, F // 2, D, BF16, comm=_ReduceScatter([(dwg2, NEAR)]))
    dwu2 = dwu2.reshape(4, 2, fb, D)
    rms_outs = [f32_tile, bf16_tile, f32_vec]
    (dx2, dx2b, d_g2), (l_wg2_far, l_wu2_near) = _mm_mstream(
        "ffn2b_dh", [dgate2, dup2], [g2, u2], [0, 0], "nn", [(x2, "tile"), (ffn2_norm_g, "vec"), (dx3, "tile")],
        rms_outs, _rms_bwd_epilogue(1.0), comm=_ReduceScatter([(dwg2, FAR), (dwu2, NEAR)]))

    (dymix,), _ = _mm_mstream("mixb_dy", [dx2b], [w_out_full], [0], "nt", [], [f32_tile], _plain_epilogue)
    dwout, _ = _mm_tn("mixb_dwout", ymix, dx2b, D, D, BF16)
    dwout = dwout.reshape(4, 2, D // N_DEV, D)
    (dproj, d_ret, d_gla, d_ba, d_wa2p), (l_wu2_far, l_wout) = _attn_bwd(
        proj, cos, sin_signed, lg, wa2p, gla_b_a, ret_norm_g, gla_norm_g, o, dymix, states,
        comm=_ReduceScatter([(dwu2, FAR), (dwout, ALL)]))
    dwin_t, _ = _mm_tn("mixb_dwin", dproj, h2, 640, D, BF16)
    dwin = dwin_t[:N_DEV * ib].reshape(4, 2, ib, D)
    (dx1, dy1b, d_gmix), (l_win_near,) = _mm_mstream(
        "mixb_dh", [dproj], [w_in_t], [0], "nn", [(x1, "tile"), (mix_norm_g, "vec"), (dx2, "tile")],
        rms_outs, _rms_bwd_epilogue(0.5), comm=_ReduceScatter([(dwin, NEAR)]))

    dwd1, (l_win_far,) = _mm_tn("ffn1b_dwd", act1, dy1b, F // 2, D, BF16, comm=_ReduceScatter([(dwin, FAR)]))
    dwd1 = dwd1.reshape(4, 2, fb, D)
    (dgate1, dup1), (l_wd1_near,) = _mm_nstream("ffn1b_dact", dy1b, [d1], [0], "nt", [gate1, up1], [BF16, BF16],
                                                _dact_epilogue, cn=256, comm=_ReduceScatter([(dwd1, NEAR)]))
    dwg1, (l_wd1_far,) = _mm_tn("ffn1b_dwg", dgate1, h1, F // 2, D, BF16, comm=_ReduceScatter([(dwd1, FAR)]))
    dwg1 = dwg1.reshape(4, 2, fb, D)
    core = owner[1:2]
    dwu1, (s_wg1,) = _mm_tn("ffn1b_dwu", dup1, h1, F // 2, D, BF16, comm=_SiblingExchange([dwg1]))
    dwu1 = dwu1.reshape(4, 2, fb, D)
    p_wg1, (s_wu1,) = _pair_sum("pair_wg1", dwg1, s_wg1, core, comm=_SiblingExchange([dwu1]))
    p_wu1, _ = _pair_sum("pair_wu1", dwu1, s_wu1, core)
    (dh1,), (c_wg1, c_wu1) = _mm_mstream(
        "ffn1b_dh", [dgate1, dup1], [g1, u1], [0, 0], "nn", [], [f32_tile], _plain_epilogue,
        comm=_ChipExchange([p_wg1, p_wu1]))
    dx0, _, d_g1 = _rms_bwd("ffn1b_rms", x0, ffn1_norm_g, dh1, dx1, 1.0)

    small = dict(ffn1=d_g1, mix=d_gmix, ffn2=d_g2, final=d_final, ret=d_ret, gla=d_gla, b_a=d_ba)
    flat = jnp.concatenate([small[k].reshape(-1) for k in SMALL_ORDER]
                           + [d_wa2p[:GATE_RANK].reshape(-1), loss[0]])
    rows = -(-flat.shape[0] // 128)
    rows = -(-rows // 8) * 8
    packed = jnp.pad(flat, (0, rows * 128 - flat.shape[0])).reshape(rows, 128)

    def adam_item(nm, grad, weight, mom, var):
        shape = weight.shape
        two_d = (1, shape[0]) if len(shape) == 1 else shape[-2:]
        return tuple(a.reshape(two_d) for a in (grad, weight, mom, var))

    sums_a, (gathered,) = _owner_sums(
        "sum_a", [(dwg2, [l_wg2_near, l_wg2_far]), (dwu2, [l_wu2_near, l_wu2_far]), (dwd2, [l_wd2_near, l_wd2_far]),
                  (dwin, [l_win_near, l_win_far]), (dwout, [l_wout])], owner,
        comm=_AllGather([packed], ["plain"]))
    sums_b, _ = _owner_sums("sum_b", [(p_wg1, [c_wg1]), (p_wu1, [c_wu1]), (dwd1, [l_wd1_near, l_wd1_far])], owner)
    big_grads = {"ffn2_w_gate": sums_a[0].T, "ffn2_w_up": sums_a[1].T, "ffn2_w_down": sums_a[2], "w_in": sums_a[3].T,
                 "w_out": sums_a[4], "ffn1_w_gate": sums_b[0].T, "ffn1_w_up": sums_b[1].T, "ffn1_w_down": sums_b[2]}
    params = dict(
        ffn2_w_gate=(ffn2_w_gate, m_ffn2_w_gate, v_ffn2_w_gate), ffn2_w_up=(ffn2_w_up, m_ffn2_w_up, v_ffn2_w_up),
        ffn2_w_down=(ffn2_w_down, m_ffn2_w_down, v_ffn2_w_down), w_in=(w_in, m_w_in, v_w_in),
        w_out=(w_out, m_w_out, v_w_out), ffn1_w_gate=(ffn1_w_gate, m_ffn1_w_gate, v_ffn1_w_gate),
        ffn1_w_up=(ffn1_w_up, m_ffn1_w_up, v_ffn1_w_up), ffn1_w_down=(ffn1_w_down, m_ffn1_w_down, v_ffn1_w_down),
        ffn1_norm_g=(ffn1_norm_g, m_ffn1_norm_g, v_ffn1_norm_g), mix_norm_g=(mix_norm_g, m_mix_norm_g, v_mix_norm_g),
        ret_norm_g=(ret_norm_g, m_ret_norm_g, v_ret_norm_g), gla_w_a2=(gla_w_a2, m_gla_w_a2, v_gla_w_a2),
        gla_b_a=(gla_b_a, m_gla_b_a, v_gla_b_a), gla_norm_g=(gla_norm_g, m_gla_norm_g, v_gla_norm_g),
        ffn2_norm_g=(ffn2_norm_g, m_ffn2_norm_g, v_ffn2_norm_g), final_norm_g=(final_norm_g, m_final_norm_g, v_final_norm_g))
    grads, updates = {}, {}

    def run_adam(name, names, grad_of, n_blocks):
        for nm in names:
            grads[nm] = grad_of[nm].reshape(params[nm][0].shape)
        res, _ = _adamw_group(name, [adam_item(nm, grads[nm], *params[nm]) for nm in names], n_blocks)
        for nm, r in zip(names, res):
            updates[nm] = tuple(a.reshape(params[nm][0].shape) for a in r)

    run_adam("adamw_big", list(big_grads), big_grads, 4)

    total = _sum_devices("sum_small", gathered.reshape(N_DEV * rows, 128), rows).reshape(-1)
    sizes = [small[k].size for k in SMALL_ORDER] + [GATE_RANK * QK_W, 128]
    offs = [0]
    for s in sizes:
        offs.append(offs[-1] + s)
    pieces = [total[offs[i]:offs[i + 1]] for i in range(len(sizes))]
    g_small = {k: pieces[i].reshape(small[k].shape) for i, k in enumerate(SMALL_ORDER)}
    g_wa2_full = pieces[len(SMALL_ORDER)].reshape(GATE_RANK, QK_W)
    g_wa2 = lax.dynamic_slice(g_wa2_full, (0, dev * ab), (GATE_RANK, ab))
    loss_total = pieces[len(SMALL_ORDER) + 1][0]

    small_grads = {"ffn1_norm_g": g_small["ffn1"], "mix_norm_g": g_small["mix"], "ret_norm_g": g_small["ret"],
                   "gla_w_a2": g_wa2, "gla_b_a": g_small["b_a"], "gla_norm_g": g_small["gla"],
                   "ffn2_norm_g": g_small["ffn2"], "final_norm_g": g_small["final"]}
    run_adam("adamw_small", list(small_grads), small_grads, 1)

    order = ("ffn1_norm_g", "ffn1_w_gate", "ffn1_w_up", "ffn1_w_down", "mix_norm_g", "w_in", "ret_norm_g", "gla_w_a2",
             "gla_b_a", "gla_norm_g", "w_out", "ffn2_norm_g", "ffn2_w_gate", "ffn2_w_up", "ffn2_w_down", "final_norm_g")
    return (loss_total, dx0[None], *[grads[nm] for nm in order], *[updates[nm][0] for nm in order],
            *[updates[nm][1] for nm in order], *[updates[nm][2] for nm in order])
```

```python
import functools
import math

import jax
import jax.numpy as jnp
from jax import lax
from jax.experimental import pallas as pl
from jax.experimental.pallas import tpu as pltpu

F32 = jnp.float32
BF16 = jnp.bfloat16
MESH = pl.DeviceIdType.MESH
HBM = pl.BlockSpec(memory_space=pltpu.HBM)

N_DEV = 8
RMS_EPS = 1e-6
ROPE_BASE = 10000.0
HEADS = 4
DK = 64
DV = 128
QK_W = HEADS * DK
V_W = HEADS * DV
GATE_RANK = 16
GATE_NORM = 16.0
CHUNK = 64
SUPER = 256
PROJ_W = 3200
C_RQ, C_RK, C_RV, C_RG, C_GQ, C_GK, C_GV, C_GG, C_GL = 0, 256, 512, 1024, 1536, 1792, 2048, 2560, 3072
GL_W = PROJ_W - C_GL
ADAM_LR, ADAM_B1, ADAM_B2, ADAM_EPS, ADAM_WD, ADAM_STEP = 0.001, 0.9, 0.999, 1e-08, 0.01, 10
VMEM_LIMIT_V7X = 52 * 1024 * 1024


def _cparams(**kw):
    return pltpu.CompilerParams(vmem_limit_bytes=VMEM_LIMIT_V7X, **kw)


def _dot(a, b, form, precision=None):
    dims = {"nn": (((1,), (0,)), ((), ())), "nt": (((1,), (1,)), ((), ())), "tn": (((0,), (0,)), ((), ()))}[form]
    return lax.dot_general(a, b, dims, preferred_element_type=F32, precision=precision)


def _sigmoid(x):
    return 1.0 / (1.0 + jnp.exp(-x))


def _coords():
    return lax.axis_index("x"), lax.axis_index("y"), lax.axis_index("c")


class _NoComm:
    inputs, out_shapes, scratch = (), (), ()


class _AllGather:
    def __init__(self, arrays, kinds):
        self.inputs = tuple(arrays)
        self.kinds = tuple(kinds)
        n = len(arrays)
        self.out_shapes = tuple(
            jax.ShapeDtypeStruct((a.shape[0], N_DEV) + a.shape[1:] if k == "stack" else (N_DEV,) + a.shape, a.dtype)
            for a, k in zip(arrays, kinds))
        self.scratch = (pltpu.SemaphoreType.DMA((n, 7)), pltpu.SemaphoreType.DMA((n, 7)),
                        pltpu.SemaphoreType.DMA((n,)))

    def _ctx(self, srcs, outs, sems):
        send_sems, recv_sems, local_sems = sems
        x, y, c = _coords()
        me, sibling = (x, y, c), (x, y, 1 - c)
        chips = [(1 - x, y), (x, 1 - y), (1 - x, 1 - y)]

        def blk(m, dev):
            k = 4 * dev[0] + 2 * dev[1] + dev[2]
            return outs[m].at[:, k] if self.kinds[m] == "stack" else outs[m].at[k]

        def copy(m, s, block, to, src=None):
            return pltpu.make_async_remote_copy(
                src_ref=blk(m, block) if src is None else src, dst_ref=blk(m, block),
                send_sem=send_sems.at[m, s], recv_sem=recv_sems.at[m, s], device_id=to, device_id_type=MESH)

        def mine(m):
            return pltpu.make_async_copy(srcs[m], blk(m, me), local_sems.at[m])

        def first(m):
            return [copy(m, 0, me, sibling, src=srcs[m])] + [
                copy(m, 1 + j, me, (*chip, c), src=srcs[m]) for j, chip in enumerate(chips)]

        return me, sibling, chips, c, copy, mine, first

    def start(self, srcs, outs, sems):
        me, sibling, chips, c, copy, mine, first = self._ctx(srcs, outs, sems)
        for m in range(len(srcs)):
            mine(m).start()
            for cp in first(m):
                cp.start()

    def mid(self, srcs, outs, sems):
        me, sibling, chips, c, copy, mine, first = self._ctx(srcs, outs, sems)
        for j, chip in enumerate(chips):
            for m in range(len(srcs)):
                copy(m, 1 + j, (*chip, c), me).wait_recv()
                copy(m, 4 + j, (*chip, c), sibling).start()

    def finish(self, srcs, outs, sems):
        me, sibling, chips, c, copy, mine, first = self._ctx(srcs, outs, sems)
        for m in range(len(srcs)):
            copy(m, 0, sibling, me).wait_recv()
            for j, chip in enumerate(chips):
                copy(m, 4 + j, (*chip, 1 - c), me).wait_recv()
            for cp in first(m):
                cp.wait_send()
            for j, chip in enumerate(chips):
                copy(m, 4 + j, (*chip, c), sibling).wait_send()
            mine(m).wait()


RELATIONS = ((0, 0, 1), (1, 0, 0), (0, 1, 0), (1, 1, 0), (1, 0, 1), (0, 1, 1), (1, 1, 1))
NEAR = (0, 1, 2, 4, 5)
FAR = (3, 6)
ALL = NEAR + FAR


class _ReduceScatter:
    def __init__(self, parts):
        self.inputs = tuple(g for g, _ in parts)
        self.slots = tuple(s for _, s in parts)
        self.out_shapes = tuple(jax.ShapeDtypeStruct((len(s),) + g.shape[2:], g.dtype) for g, s in parts)
        n_max = max(len(s) for s in self.slots)
        n = len(parts)
        self.scratch = (pltpu.SemaphoreType.DMA((n, n_max)), pltpu.SemaphoreType.DMA((n, n_max)))

    def _copies(self, srcs, outs, sems):
        send_sems, recv_sems = sems
        x, y, c = _coords()
        copies = []
        for m, slots in enumerate(self.slots):
            for i, s in enumerate(slots):
                fx, fy, fc = RELATIONS[s]
                px = 1 - x if fx else x
                py = 1 - y if fy else y
                pc = 1 - c if fc else c
                copies.append(pltpu.make_async_remote_copy(
                    src_ref=srcs[m].at[2 * px + py, pc], dst_ref=outs[m].at[i], send_sem=send_sems.at[m, i],
                    recv_sem=recv_sems.at[m, i], device_id=(px, py, pc), device_id_type=MESH))
        return copies

    def start(self, srcs, outs, sems):
        for cp in self._copies(srcs, outs, sems):
            cp.start()

    def mid(self, srcs, outs, sems):
        pass

    def finish(self, srcs, outs, sems):
        for cp in self._copies(srcs, outs, sems):
            cp.wait()


class _SiblingExchange:
    def __init__(self, grads):
        self.inputs = tuple(grads)
        self.out_shapes = tuple(jax.ShapeDtypeStruct((4,) + g.shape[2:], g.dtype) for g in grads)
        self.scratch = (pltpu.SemaphoreType.DMA((len(grads),)), pltpu.SemaphoreType.DMA((len(grads),)))

    def _copies(self, srcs, outs, sems):
        send_sems, recv_sems = sems
        x, y, c = _coords()
        return [pltpu.make_async_remote_copy(
            src_ref=srcs[m].at[:, 1 - c], dst_ref=outs[m], send_sem=send_sems.at[m], recv_sem=recv_sems.at[m],
            device_id=(x, y, 1 - c), device_id_type=MESH) for m in range(len(srcs))]

    def start(self, srcs, outs, sems):
        for cp in self._copies(srcs, outs, sems):
            cp.start()

    def mid(self, srcs, outs, sems):
        pass

    def finish(self, srcs, outs, sems):
        for cp in self._copies(srcs, outs, sems):
            cp.wait()


class _ChipExchange:
    def __init__(self, partials):
        self.inputs = tuple(partials)
        self.out_shapes = tuple(jax.ShapeDtypeStruct((3,) + p.shape[1:], p.dtype) for p in partials)
        n = len(partials)
        self.scratch = (pltpu.SemaphoreType.DMA((n, 3)), pltpu.SemaphoreType.DMA((n, 3)))

    def _copies(self, srcs, outs, sems):
        send_sems, recv_sems = sems
        x, y, c = _coords()
        copies = []
        for m in range(len(srcs)):
            for j, (px, py) in enumerate([(1 - x, y), (x, 1 - y), (1 - x, 1 - y)]):
                copies.append(pltpu.make_async_remote_copy(
                    src_ref=srcs[m].at[2 * px + py], dst_ref=outs[m].at[j], send_sem=send_sems.at[m, j],
                    recv_sem=recv_sems.at[m, j], device_id=(px, py, c), device_id_type=MESH))
        return copies

    def start(self, srcs, outs, sems):
        for cp in self._copies(srcs, outs, sems):
            cp.start()

    def mid(self, srcs, outs, sems):
        pass

    def finish(self, srcs, outs, sems):
        for cp in self._copies(srcs, outs, sems):
            cp.wait()


class _Multi:
    def __init__(self, comms):
        self.comms = comms
        self.inputs = tuple(a for c in comms for a in c.inputs)
        self.out_shapes = tuple(s for c in comms for s in c.out_shapes)
        self.scratch = tuple(s for c in comms for s in c.scratch)

    def _each(self, phase, srcs, outs, sems):
        i = o = k = 0
        for c in self.comms:
            ni, no, nk = len(c.inputs), len(c.out_shapes), len(c.scratch)
            getattr(c, phase)(srcs[i:i + ni], outs[o:o + no], sems[k:k + nk])
            i, o, k = i + ni, o + no, k + nk

    def start(self, srcs, outs, sems):
        self._each("start", srcs, outs, sems)

    def mid(self, srcs, outs, sems):
        self._each("mid", srcs, outs, sems)

    def finish(self, srcs, outs, sems):
        self._each("finish", srcs, outs, sems)


def _call(name, main, grid, in_specs, out_specs, out_shape, args, scratch=(), comm=None, prefetch=None):
    comm = comm or _NoComm()
    counts = [len(in_specs), len(comm.inputs), len(out_shape), len(comm.out_shapes), len(scratch), len(comm.scratch)]
    n_steps = math.prod(grid)
    hosted = bool(comm.inputs)

    def body(*refs):
        if prefetch is not None:
            refs = refs[1:]
        parts, at = [], 0
        for n in counts:
            parts.append(refs[at:at + n])
            at += n
        ins, c_in, outs, c_out, scr, c_scr = parts
        step = pl.program_id(0)
        for d in range(1, len(grid)):
            step = step * grid[d] + pl.program_id(d)
        if hosted:
            @pl.when(step == 0)
            def _():
                comm.start(c_in, c_out, c_scr)
        main(ins, outs, scr)
        if hosted:
            @pl.when(step == max(n_steps - 2, 0))
            def _():
                comm.mid(c_in, c_out, c_scr)

            @pl.when(step == n_steps - 1)
            def _():
                comm.finish(c_in, c_out, c_scr)

    all_in = list(in_specs) + [HBM] * counts[1]
    all_out = list(out_specs) + [HBM] * counts[3]
    all_scratch = list(scratch) + list(comm.scratch)
    shapes = list(out_shape) + list(comm.out_shapes)
    if prefetch is None:
        res = pl.pallas_call(body, name=name, grid=grid, in_specs=all_in, out_specs=all_out, out_shape=shapes,
                             scratch_shapes=all_scratch, compiler_params=_cparams())(*args, *comm.inputs)
    else:
        res = pl.pallas_call(
            body, name=name, out_shape=shapes,
            grid_spec=pltpu.PrefetchScalarGridSpec(num_scalar_prefetch=1, grid=grid, in_specs=all_in,
                                                   out_specs=all_out, scratch_shapes=all_scratch),
            compiler_params=_cparams())(prefetch, *args, *comm.inputs)
    return res[:counts[2]], res[counts[2]:]


def _rms_fwd(name, x, g, comm=None):
    T, D = x.shape
    tm = min(T, 512)

    def main(ins, outs, scr):
        x_ref, g_ref = ins
        xv = x_ref[...]
        r = lax.rsqrt(jnp.mean(xv * xv, axis=-1, keepdims=True) + RMS_EPS)
        outs[0][...] = (xv * r * g_ref[...]).astype(outs[0].dtype)

    tile = pl.BlockSpec((tm, D), lambda i: (i, 0))
    (h,), extra = _call(name, main, (T // tm,), [tile, pl.BlockSpec((1, D), lambda i: (0, 0))], [tile],
                        [jax.ShapeDtypeStruct((T, D), BF16)], (x, g), comm=comm)
    return h, extra


def _final_loss_bwd(name, x, g, target, out_scale):
    T, D = x.shape
    tm = min(T, 512)

    def main(ins, outs, scr):
        x_ref, g_ref, t_ref = ins
        dx_ref, dxb_ref, dg_ref, loss_ref = outs
        i = pl.program_id(0)
        xv = x_ref[...]
        r = lax.rsqrt(jnp.mean(xv * xv, axis=-1, keepdims=True) + RMS_EPS)
        xhat = xv * r
        err = xhat * g_ref[...] - t_ref[...]

        @pl.when(i == 0)
        def _():
            dg_ref[...] = jnp.zeros_like(dg_ref)
            loss_ref[...] = jnp.zeros_like(loss_ref)

        loss_ref[...] += jnp.broadcast_to(jnp.sum(err * err) * (0.5 / D), loss_ref.shape)
        dy = err * (1.0 / D)
        dg_ref[...] += jnp.sum(dy * xhat, axis=0, keepdims=True)
        dxhat = dy * g_ref[...]
        dx = r * (dxhat - xhat * jnp.mean(dxhat * xhat, axis=-1, keepdims=True))
        dx_ref[...] = dx
        dxb_ref[...] = (out_scale * dx).astype(dxb_ref.dtype)

    tile = pl.BlockSpec((tm, D), lambda i: (i, 0))
    vec = pl.BlockSpec((1, D), lambda i: (0, 0))
    outs, _ = _call(name, main, (T // tm,), [tile, vec, tile],
                    [tile, tile, vec, pl.BlockSpec((1, 128), lambda i: (0, 0))],
                    [jax.ShapeDtypeStruct((T, D), F32), jax.ShapeDtypeStruct((T, D), BF16),
                     jax.ShapeDtypeStruct((1, D), F32), jax.ShapeDtypeStruct((1, 128), F32)], (x, g, target))
    return outs


def _mm_nstream(name, a, ws, w_sel, w_form, comps, out_dtypes, epilogue, cn, rows=1024, comm=None):
    T, K = a.shape
    N = ws[0].shape[1]
    rows = min(rows, T)
    assert N % cn == 0 and T % rows == 0
    n_w, n_c = len(ws), len(comps)

    def main(ins, outs, scr):
        a_ref = ins[0]
        w_refs = ins[1:1 + n_w]
        c_refs = ins[1 + n_w:]

        for r in range(T // rows):
            sl = slice(r * rows, (r + 1) * rows)
            a_blk = a_ref[sl, :]
            dots = [_dot(a_blk, w_ref[...], w_form) for w_ref in w_refs]
            res = epilogue(dots, [c_ref[sl, :] for c_ref in c_refs])
            for o_ref, o in zip(outs, res):
                o_ref[sl, :] = o.astype(o_ref.dtype)

    if w_form == "nt":
        w_specs = [pl.BlockSpec((None, cn, K), functools.partial(lambda j, s: (s, j, 0), s=s)) for s in w_sel]
    else:
        w_specs = [pl.BlockSpec((K, cn), lambda j: (0, j)) for _ in ws]
    chunk = pl.BlockSpec((T, cn), lambda j: (0, j))
    return _call(name, main, (N // cn,), [pl.BlockSpec((T, K), lambda j: (0, 0))] + w_specs + [chunk] * n_c,
                 [chunk] * len(out_dtypes), [jax.ShapeDtypeStruct((T, N), dt) for dt in out_dtypes],
                 (a, *ws, *comps), comm=comm)


def _mm_mstream(name, as_, ws, w_sel, w_form, extras, outs_desc, epilogue, tm=512, comm=None):
    T = as_[0].shape[0]
    tm = min(tm, T)
    n_a = len(as_)
    w_shapes = [w.shape[-2:] for w in ws]
    N = w_shapes[0][1] if w_form == "nn" else w_shapes[0][0]

    def main(ins, outs, scr):
        a_refs = ins[:n_a]
        w_refs = ins[n_a:2 * n_a]
        acc = None
        for a_ref, w_ref in zip(a_refs, w_refs):
            d = _dot(a_ref[...], w_ref[...], w_form)
            acc = d if acc is None else acc + d
        epilogue(acc, ins[2 * n_a:], outs)

    kind_spec = {"tile": pl.BlockSpec((tm, N), lambda i: (i, 0)), "vec": pl.BlockSpec((1, N), lambda i: (0, 0))}
    kind_shape = {"tile": (T, N), "vec": (1, N)}
    a_specs = [pl.BlockSpec((tm, a.shape[1]), lambda i: (i, 0)) for a in as_]
    w_specs = []
    for w, s in zip(ws, w_sel):
        if w.ndim == 3:
            w_specs.append(pl.BlockSpec((None,) + tuple(w.shape[1:]), functools.partial(lambda i, s: (s, 0, 0), s=s),
                                        pipeline_mode=pl.Buffered(1)))
        else:
            w_specs.append(pl.BlockSpec(tuple(w.shape), lambda i: (0, 0), pipeline_mode=pl.Buffered(1)))
    args = list(as_) + list(ws) + [e for e, _ in extras]
    return _call(name, main, (T // tm,), a_specs + w_specs + [kind_spec[k] for _, k in extras],
                 [kind_spec[k] for _, k in outs_desc],
                 [jax.ShapeDtypeStruct(kind_shape[k], dt) for dt, k in outs_desc], args, comm=comm)


def _plain_epilogue(acc, ex, outs):
    outs[0][...] = acc.astype(outs[0].dtype)


def _residual_epilogue(scale):
    def ep(acc, ex, outs):
        outs[0][...] = ex[0][...] + scale * acc
    return ep


def _residual_rms_epilogue(scale):
    def ep(acc, ex, outs):
        xv = ex[0][...] + scale * acc
        outs[0][...] = xv
        r = lax.rsqrt(jnp.mean(xv * xv, axis=-1, keepdims=True) + RMS_EPS)
        outs[1][...] = (xv * r * ex[1][...]).astype(outs[1].dtype)
    return ep


def _rms_bwd_epilogue(out_scale):
    def ep(acc, ex, outs):
        x_ref, g_ref, dres_ref = ex
        dx_ref, dxb_ref, dg_ref = outs
        xv = x_ref[...]
        r = lax.rsqrt(jnp.mean(xv * xv, axis=-1, keepdims=True) + RMS_EPS)
        xhat = xv * r

        @pl.when(pl.program_id(0) == 0)
        def _():
            dg_ref[...] = jnp.zeros_like(dg_ref)

        dg_ref[...] += jnp.sum(acc * xhat, axis=0, keepdims=True)
        dxhat = acc * g_ref[...]
        dx = r * (dxhat - xhat * jnp.mean(dxhat * xhat, axis=-1, keepdims=True)) + dres_ref[...]
        dx_ref[...] = dx
        dxb_ref[...] = (out_scale * dx).astype(dxb_ref.dtype)
    return ep


def _mm_tn(name, a, b, tmo, tno, out_dtype, tk=1024, comm=None):
    T, Ma = a.shape
    Nb = b.shape[1]
    tk = min(tk, T)
    nk = T // tk

    def main(ins, outs, scr):
        a_ref, b_ref = ins
        (acc_ref,) = scr
        k = pl.program_id(2)

        @pl.when(k == 0)
        def _():
            acc_ref[...] = jnp.zeros_like(acc_ref)

        acc_ref[...] += _dot(a_ref[...], b_ref[...], "tn")

        @pl.when(k == nk - 1)
        def _():
            outs[0][...] = acc_ref[...].astype(outs[0].dtype)

    (out,), extra = _call(
        name, main, (Ma // tmo, Nb // tno, nk),
        [pl.BlockSpec((tk, tmo), lambda i, j, k: (k, i)), pl.BlockSpec((tk, tno), lambda i, j, k: (k, j))],
        [pl.BlockSpec((tmo, tno), lambda i, j, k: (i, j))], [jax.ShapeDtypeStruct((Ma, Nb), out_dtype)],
        (a, b), scratch=[pltpu.VMEM((tmo, tno), F32)], comm=comm)
    return out, extra


def _silu_mul_epilogue(dots, comps):
    g, u = dots
    act = g * _sigmoid(g) * u
    return [g, u, act]


def _up_act_epilogue(dots, comps):
    (u,) = dots
    g = comps[0].astype(F32)
    return [u, g * _sigmoid(g) * u]


def _dact_epilogue(dots, comps):
    (dact,) = dots
    g = comps[0].astype(F32)
    u = comps[1].astype(F32)
    s = _sigmoid(g)
    silu = g * s
    dsilu = s + silu * (1.0 - s)
    return [dact * u * dsilu, dact * silu]


def _identity_epilogue(dots, comps):
    return list(dots)


def _swap_halves(x):
    lane = lax.broadcasted_iota(jnp.int32, x.shape, 1)
    first = (lane % DK) < (DK // 2)
    return jnp.where(first, pltpu.roll(x, 128 - DK // 2, 1), pltpu.roll(x, DK // 2, 1))


def _rotary(t, cos, sin_signed):
    halves = []
    for p in range(QK_W // 128):
        th = t[:, 128 * p:128 * (p + 1)]
        halves.append(th * cos + _swap_halves(th) * sin_signed)
    return jnp.concatenate(halves, axis=1)


def _rotary_transposed(d, cos, sin_signed):
    halves = []
    for p in range(QK_W // 128):
        dh = d[:, 128 * p:128 * (p + 1)]
        halves.append(dh * cos + _swap_halves(dh * sin_signed))
    return jnp.concatenate(halves, axis=1)


def _log_sigmoid(x):
    return jnp.minimum(x, 0.0) - jnp.log(1.0 + jnp.exp(-jnp.abs(x)))


def _attn_masks():
    row = lax.broadcasted_iota(jnp.int32, (SUPER, SUPER), 0)
    col = lax.broadcasted_iota(jnp.int32, (SUPER, SUPER), 1)
    same = (row // CHUNK) == (col // CHUNK)
    return row, col, same


def _group_inputs(grp, pr, cos, sin_signed, lg, wa2, ba):
    if grp == 0:
        q = _rotary(pr[:, C_RQ:C_RQ + QK_W], cos, sin_signed)
        k = _rotary(pr[:, C_RK:C_RK + QK_W], cos, sin_signed) * (DK ** -0.5)
        v = pr[:, C_RV:C_RV + V_W]
        gate = pr[:, C_RG:C_RG + V_W]
        pos = lax.broadcasted_iota(jnp.int32, (SUPER, QK_W), 0).astype(F32) + 1.0
        return q, k, v, gate, pos * lg, None, None
    q = pr[:, C_GQ:C_GQ + QK_W] * (DK ** -0.5)
    k = pr[:, C_GK:C_GK + QK_W]
    v = pr[:, C_GV:C_GV + V_W]
    gate = pr[:, C_GG:C_GG + V_W]
    glow = pr[:, C_GL:C_GL + GL_W]
    logit = _dot(glow.astype(BF16), wa2.astype(BF16), "nn") + ba
    la = _log_sigmoid(logit) * (1.0 / GATE_NORM)
    row, col, _ = _attn_masks()
    lower = (col <= row).astype(F32)
    b_cum = _dot(lower, la, "nn", precision=lax.Precision.HIGHEST)
    return q, k, v, gate, b_cum, glow, logit


def _decay_factors(q, k, b_cum):
    c = b_cum[SUPER // 2 - 1:SUPER // 2, :]
    bl = b_cum[SUPER - 1:SUPER, :]
    e1 = jnp.exp(b_cum - c)
    e2 = jnp.exp(c - b_cum)
    e_b = jnp.exp(b_cum)
    e_l = jnp.exp(bl - b_cum)
    return dict(e1=e1, e2=e2, eb=e_b, el=e_l, ebl=jnp.exp(bl),
                qp=q * e1, qm=q * e2, kp=k * e1, km=k * e2, qs=q * e_b, kl=k * e_l)


def _state_block_mask():
    r = lax.broadcasted_iota(jnp.int32, (V_W, QK_W), 0)
    c = lax.broadcasted_iota(jnp.int32, (V_W, QK_W), 1)
    return (r // DV) == (c // DK)


def _attn_fwd(proj, cos, sin_signed, lg, wa2p, ba, gn_ret, gn_gla, comm=None):
    T = proj.shape[0]
    n_s = T // SUPER

    def main(ins, outs, scr):
        pr_ref, cos_ref, sin_ref, lg_ref, wa2_ref, ba_ref, gr_ref, gg_ref = ins
        o_ref, y_ref, st_ref = outs
        (s_ref,) = scr
        i = pl.program_id(0)

        @pl.when(i == 0)
        def _():
            s_ref[...] = jnp.zeros_like(s_ref)

        pr = pr_ref
        row, col, same = _attn_masks()
        m1 = col <= row
        m2 = jnp.logical_and(col > row, same)
        lane = lax.broadcasted_iota(jnp.int32, (1, QK_W), 1)
        blockmask = _state_block_mask()
        for grp in range(2):
            q, k, v, gate, b_cum, _, _ = _group_inputs(grp, pr, cos_ref[...], sin_ref[...], lg_ref[...],
                                                      wa2_ref[...], ba_ref[...])
            f = _decay_factors(q, k, b_cum)
            gn = gr_ref[...] if grp == 0 else gg_ref[...]
            s_prev = s_ref[grp]
            st_ref[0, grp] = s_prev
            o_inter = _dot(f["qs"].astype(BF16), s_prev.astype(BF16), "nt")
            kmb = f["km"].astype(BF16)
            kpb = f["kp"].astype(BF16)
            vb = v.astype(BF16)
            for h in range(HEADS):
                hm = (lane // DK) == h
                a1 = _dot(jnp.where(hm, f["qp"], 0.0).astype(BF16), kmb, "nt")
                a2 = _dot(jnp.where(hm, f["qm"], 0.0).astype(BF16), kpb, "nt")
                a = jnp.where(m1, a1, jnp.where(m2, a2, 0.0))
                lo = grp * V_W + h * DV
                o_h = _dot(a.astype(BF16), vb[:, h * DV:(h + 1) * DV], "nn") + o_inter[:, h * DV:(h + 1) * DV]
                o_ref[:, lo:lo + DV] = o_h
                r = lax.rsqrt(jnp.mean(o_h * o_h, axis=-1, keepdims=True) + RMS_EPS)
                gte = gate[:, h * DV:(h + 1) * DV]
                y = o_h * r * gn[:, h * DV:(h + 1) * DV] * (gte * _sigmoid(gte))
                y_ref[:, lo:lo + DV] = y.astype(y_ref.dtype)
            upd = _dot(vb, f["kl"].astype(BF16), "tn")
            s_ref[grp] = s_prev * f["ebl"] + jnp.where(blockmask, upd, 0.0)

    const = lambda shape: pl.BlockSpec(shape, lambda i: tuple(0 for _ in shape))
    return _call(
        "attn_fwd", main, (n_s,),
        [pl.BlockSpec((SUPER, PROJ_W), lambda i: (i, 0)),
         pl.BlockSpec((SUPER, 128), lambda i: (i, 0)), pl.BlockSpec((SUPER, 128), lambda i: (i, 0)),
         const((1, QK_W)), const((GL_W, QK_W)), const((1, QK_W)), const((1, V_W)), const((1, V_W))],
        [pl.BlockSpec((SUPER, 2 * V_W), lambda i: (i, 0)), pl.BlockSpec((SUPER, 2 * V_W), lambda i: (i, 0)),
         pl.BlockSpec((1, 2, V_W, QK_W), lambda i: (i, 0, 0, 0))],
        [jax.ShapeDtypeStruct((T, 2 * V_W), F32), jax.ShapeDtypeStruct((T, 2 * V_W), BF16),
         jax.ShapeDtypeStruct((n_s, 2, V_W, QK_W), F32)],
        (proj, cos, sin_signed, lg, wa2p, ba, gn_ret, gn_gla),
        scratch=[pltpu.VMEM((2, V_W, QK_W), F32)], comm=comm)


def _attn_bwd(proj, cos, sin_signed, lg, wa2p, ba, gn_ret, gn_gla, o, dy, states, comm=None):
    T = proj.shape[0]
    n_s = T // SUPER

    def main(ins, outs, scr):
        pr_ref, cos_ref, sin_ref, lg_ref, wa2_ref, ba_ref, gr_ref, gg_ref, o_ref, dy_ref, st_ref = ins
        dp_ref, dgr_ref, dgg_ref, dba_ref, dwa_ref = outs
        (ds_ref,) = scr
        i = pl.program_id(0)

        @pl.when(i == 0)
        def _():
            ds_ref[...] = jnp.zeros_like(ds_ref)
            dgr_ref[...] = jnp.zeros_like(dgr_ref)
            dgg_ref[...] = jnp.zeros_like(dgg_ref)
            dba_ref[...] = jnp.zeros_like(dba_ref)
            dwa_ref[...] = jnp.zeros_like(dwa_ref)

        pr = pr_ref
        cos = cos_ref[...]
        sin_signed = sin_ref[...]
        row, col, same = _attn_masks()
        m1 = col <= row
        m2 = jnp.logical_and(col > row, same)
        m1t = row <= col
        m2t = jnp.logical_and(row > col, same)
        lane = lax.broadcasted_iota(jnp.int32, (1, QK_W), 1)
        blockmask = _state_block_mask()
        for grp in range(2):
            q, k, v, gate, b_cum, glow, logit = _group_inputs(grp, pr, cos, sin_signed, lg_ref[...],
                                                              wa2_ref[...], ba_ref[...])
            f = _decay_factors(q, k, b_cum)
            gn = gr_ref[...] if grp == 0 else gg_ref[...]
            dgn_ref = dgr_ref if grp == 0 else dgg_ref
            do_parts, dgate_parts, dgn_parts = [], [], []
            for h in range(HEADS):
                lo = grp * V_W + h * DV
                o_h = o_ref[:, lo:lo + DV]
                r = lax.rsqrt(jnp.mean(o_h * o_h, axis=-1, keepdims=True) + RMS_EPS)
                n = o_h * r
                gte = gate[:, h * DV:(h + 1) * DV]
                sg = _sigmoid(gte)
                dy_h = dy_ref[:, lo:lo + DV]
                gn_h = gn[:, h * DV:(h + 1) * DV]
                dgate_parts.append(dy_h * n * gn_h * (sg * (1.0 + gte * (1.0 - sg))))
                dz = dy_h * (gte * sg)
                dgn_parts.append(jnp.sum(dz * n, axis=0, keepdims=True))
                dn = dz * gn_h
                do_parts.append(r * (dn - n * jnp.mean(dn * n, axis=-1, keepdims=True)))
            dgn_ref[...] += jnp.concatenate(dgn_parts, axis=1)
            dgate = jnp.concatenate(dgate_parts, axis=1)
            do = jnp.concatenate(do_parts, axis=1)
            dob = do.astype(BF16)
            vb = v.astype(BF16)
            s_prev = st_ref[0, grp]
            ds_new = ds_ref[grp]
            dsb = ds_new.astype(BF16)
            qpb, qmb = f["qp"].astype(BF16), f["qm"].astype(BF16)
            kpb, kmb = f["kp"].astype(BF16), f["km"].astype(BF16)
            dqp = jnp.zeros((SUPER, QK_W), F32)
            dqm = jnp.zeros((SUPER, QK_W), F32)
            dkp = jnp.zeros((SUPER, QK_W), F32)
            dkm = jnp.zeros((SUPER, QK_W), F32)
            dv_parts = []
            for h in range(HEADS):
                hm = (lane // DK) == h
                qp_h = jnp.where(hm, f["qp"], 0.0).astype(BF16)
                qm_h = jnp.where(hm, f["qm"], 0.0).astype(BF16)
                kp_h = jnp.where(hm, f["kp"], 0.0).astype(BF16)
                km_h = jnp.where(hm, f["km"], 0.0).astype(BF16)
                at = jnp.where(m1t, _dot(km_h, qpb, "nt"), jnp.where(m2t, _dot(kp_h, qmb, "nt"), 0.0))
                do_h = dob[:, h * DV:(h + 1) * DV]
                v_h = vb[:, h * DV:(h + 1) * DV]
                dv_parts.append(_dot(at.astype(BF16), do_h, "nn"))
                da = _dot(do_h, v_h, "nt")
                dat = _dot(v_h, do_h, "nt")
                da1 = jnp.where(m1, da, 0.0).astype(BF16)
                da2 = jnp.where(m2, da, 0.0).astype(BF16)
                da1t = jnp.where(m1t, dat, 0.0).astype(BF16)
                da2t = jnp.where(m2t, dat, 0.0).astype(BF16)
                dqp = dqp + _dot(da1, km_h, "nn")
                dqm = dqm + _dot(da2, kp_h, "nn")
                dkm = dkm + _dot(da1t, qp_h, "nn")
                dkp = dkp + _dot(da2t, qm_h, "nn")
            klb = f["kl"].astype(BF16)
            qsb = f["qs"].astype(BF16)
            dqs = _dot(dob, s_prev.astype(BF16), "nn")
            dkl = _dot(vb, dsb, "nn")
            dv = jnp.concatenate(dv_parts, axis=1) + _dot(klb, dsb, "nt")
            ds_ref[grp] = ds_new * f["ebl"] + jnp.where(blockmask, _dot(dob, qsb, "tn"), 0.0)
            dq = dqp * f["e1"] + dqm * f["e2"] + dqs * f["eb"]
            dk = dkm * f["e2"] + dkp * f["e1"] + dkl * f["el"]
            if grp == 0:
                dq = _rotary_transposed(dq, cos, sin_signed)
                dk = _rotary_transposed(dk * (DK ** -0.5), cos, sin_signed)
                dp_ref[:, C_RQ:C_RQ + QK_W] = dq.astype(dp_ref.dtype)
                dp_ref[:, C_RK:C_RK + QK_W] = dk.astype(dp_ref.dtype)
                dp_ref[:, C_RV:C_RV + V_W] = dv.astype(dp_ref.dtype)
                dp_ref[:, C_RG:C_RG + V_W] = dgate.astype(dp_ref.dtype)
            else:
                dkl_kl = dkl * klb.astype(F32)
                db = (dqp * qpb.astype(F32) - dkm * kmb.astype(F32) - dqm * qmb.astype(F32)
                      + dkp * kpb.astype(F32) + dqs * qsb.astype(F32) - dkl_kl)
                last = (jnp.sum(dkl_kl, axis=0, keepdims=True)
                        + f["ebl"] * jnp.sum(s_prev * ds_new, axis=0, keepdims=True))
                rowq = lax.broadcasted_iota(jnp.int32, (SUPER, QK_W), 0)
                db = db + jnp.where(rowq == SUPER - 1, last, 0.0)
                upper = (col >= row).astype(F32)
                dla = _dot(upper, db, "nn", precision=lax.Precision.HIGHEST)
                dlogit = dla * (1.0 / GATE_NORM) * (1.0 - _sigmoid(logit))
                dlb = dlogit.astype(BF16)
                dglow = _dot(dlb, wa2_ref[...].astype(BF16), "nt")
                dwa_ref[...] += _dot(glow.astype(BF16), dlb, "tn")
                dba_ref[...] += jnp.sum(dlogit, axis=0, keepdims=True)
                dp_ref[:, C_GQ:C_GQ + QK_W] = (dq * (DK ** -0.5)).astype(dp_ref.dtype)
                dp_ref[:, C_GK:C_GK + QK_W] = dk.astype(dp_ref.dtype)
                dp_ref[:, C_GV:C_GV + V_W] = dv.astype(dp_ref.dtype)
                dp_ref[:, C_GG:C_GG + V_W] = dgate.astype(dp_ref.dtype)
                dp_ref[:, C_GL:C_GL + GL_W] = dglow.astype(dp_ref.dtype)

    rev = lambda i: n_s - 1 - i
    const = lambda shape: pl.BlockSpec(shape, lambda i: tuple(0 for _ in shape))
    return _call(
        "attn_bwd", main, (n_s,),
        [pl.BlockSpec((SUPER, PROJ_W), lambda i: (rev(i), 0)),
         pl.BlockSpec((SUPER, 128), lambda i: (rev(i), 0)), pl.BlockSpec((SUPER, 128), lambda i: (rev(i), 0)),
         const((1, QK_W)), const((GL_W, QK_W)), const((1, QK_W)), const((1, V_W)), const((1, V_W)),
         pl.BlockSpec((SUPER, 2 * V_W), lambda i: (rev(i), 0)),
         pl.BlockSpec((SUPER, 2 * V_W), lambda i: (rev(i), 0)),
         pl.BlockSpec((1, 2, V_W, QK_W), lambda i: (rev(i), 0, 0, 0))],
        [pl.BlockSpec((SUPER, PROJ_W), lambda i: (rev(i), 0)),
         const((1, V_W)), const((1, V_W)), const((1, QK_W)), const((GL_W, QK_W))],
        [jax.ShapeDtypeStruct((T, PROJ_W), BF16),
         jax.ShapeDtypeStruct((1, V_W), F32), jax.ShapeDtypeStruct((1, V_W), F32),
         jax.ShapeDtypeStruct((1, QK_W), F32), jax.ShapeDtypeStruct((GL_W, QK_W), F32)],
        (proj, cos, sin_signed, lg, wa2p, ba, gn_ret, gn_gla, o, dy, states),
        scratch=[pltpu.VMEM((2, V_W, QK_W), F32)], comm=comm)


def _rotary_tables(T):
    half = DK // 2
    inv = ROPE_BASE ** (-jnp.arange(half, dtype=F32) * 2.0 / DK)
    ang = jnp.arange(T, dtype=F32)[:, None] * inv[None, :]
    cos, sin = jnp.cos(ang), jnp.sin(ang)
    cos_head = jnp.concatenate([cos, cos], axis=1)
    sin_head = jnp.concatenate([-sin, sin], axis=1)
    return jnp.tile(cos_head, (1, 128 // DK)), jnp.tile(sin_head, (1, 128 // DK))


def _sum_devices(name, gathered, m_per):
    def body(g_ref, o_ref):
        acc = g_ref[0:m_per, :]
        for k in range(1, N_DEV):
            acc = acc + g_ref[k * m_per:(k + 1) * m_per, :]
        o_ref[...] = acc

    return pl.pallas_call(body, name=name, out_shape=jax.ShapeDtypeStruct((m_per, 128), F32))(gathered)


def _pair_sum(name, grad, landed, core, comm=None):
    R, C = grad.shape[2:]

    def main(ins, outs, scr):
        outs[0][...] = (ins[0][...].astype(F32) + ins[1][...].astype(F32)).astype(outs[0].dtype)

    blk = pl.BlockSpec((None, R, C), lambda j, s: (j, 0, 0))
    (out,), extra = _call(name, main, (4,), [pl.BlockSpec((None, None, R, C), lambda j, s: (j, s[0], 0, 0)), blk],
                          [blk], [jax.ShapeDtypeStruct((4, R, C), BF16)], (grad, landed), prefetch=core, comm=comm)
    return out, extra


def _owner_sums(name, items, owner, comm=None):
    counts = [1 + len(landed) for _, landed in items]

    def main(ins, outs, scr):
        at = 0
        for o_ref, n in zip(outs, counts):
            acc = ins[at][...].astype(F32)
            for l_ref in ins[at + 1:at + n]:
                for j in range(l_ref.shape[0]):
                    acc = acc + l_ref[j].astype(F32)
            o_ref[...] = acc
            at += n

    once = pl.Buffered(1)
    in_specs, out_specs, out_shape, args = [], [], [], []
    for grad, landed in items:
        R, C = grad.shape[-2:]
        if grad.ndim == 4:
            in_specs.append(pl.BlockSpec((None, None, R, C), lambda i, s: (s[0], s[1], 0, 0), pipeline_mode=once))
        else:
            in_specs.append(pl.BlockSpec((None, R, C), lambda i, s: (s[0], 0, 0), pipeline_mode=once))
        in_specs += [pl.BlockSpec(tuple(l.shape), lambda i, s: (0, 0, 0), pipeline_mode=once) for l in landed]
        out_specs.append(pl.BlockSpec((R, C), lambda i, s: (0, 0)))
        out_shape.append(jax.ShapeDtypeStruct((R, C), F32))
        args += [grad, *landed]
    return _call(name, main, (1,), in_specs, out_specs, out_shape, args, comm=comm, prefetch=owner)


def _rms_bwd(name, x, g, dh, dres, out_scale):
    T, D = x.shape
    tm = min(T, 512)
    ep = _rms_bwd_epilogue(out_scale)

    def main(ins, outs, scr):
        x_ref, g_ref, dh_ref, dres_ref = ins
        ep(dh_ref[...], (x_ref, g_ref, dres_ref), outs)

    tile = pl.BlockSpec((tm, D), lambda i: (i, 0))
    vec = pl.BlockSpec((1, D), lambda i: (0, 0))
    outs, _ = _call(name, main, (T // tm,), [tile, vec, tile, tile], [tile, tile, vec],
                    [jax.ShapeDtypeStruct((T, D), F32), jax.ShapeDtypeStruct((T, D), BF16),
                     jax.ShapeDtypeStruct((1, D), F32)], (x, g, dh, dres))
    return outs


def _adamw_group(name, items, n_blocks, comm=None):
    n = len(items)

    def main(ins, outs, scr):
        for p in range(n):
            g_ref, w_ref, m_ref, v_ref = ins[4 * p:4 * p + 4]
            d_ref, nm_ref, nv_ref = outs[3 * p:3 * p + 3]
            gv = g_ref[...]
            nm = ADAM_B1 * m_ref[...] + (1.0 - ADAM_B1) * gv
            nv = ADAM_B2 * v_ref[...] + (1.0 - ADAM_B2) * (gv * gv)
            m_hat = nm / (1.0 - ADAM_B1 ** ADAM_STEP)
            v_hat = nv / (1.0 - ADAM_B2 ** ADAM_STEP)
            d_ref[...] = -ADAM_LR * (m_hat / (jnp.sqrt(v_hat) + ADAM_EPS) + ADAM_WD * w_ref[...])
            nm_ref[...] = nm
            nv_ref[...] = nv

    in_specs, out_specs, out_shape, args = [], [], [], []
    for item in items:
        R, C = item[1].shape
        assert R % n_blocks == 0
        spec = pl.BlockSpec((R // n_blocks, C), lambda i: (i, 0))
        in_specs += [spec] * 4
        out_specs += [spec] * 3
        out_shape += [jax.ShapeDtypeStruct((R, C), F32)] * 3
        args += list(item)
    outs, extra = _call(name, main, (n_blocks,), in_specs, out_specs, out_shape, args, comm=comm)
    return [tuple(outs[3 * p:3 * p + 3]) for p in range(n)], extra


SMALL_ORDER = ("ffn1", "mix", "ffn2", "final", "ret", "gla", "b_a")


def kernel(x, ffn1_norm_g, ffn1_w_gate, ffn1_w_up, ffn1_w_down, mix_norm_g, w_in, ret_norm_g, gla_w_a2, gla_b_a, gla_norm_g, w_out, ffn2_norm_g, ffn2_w_gate, ffn2_w_up, ffn2_w_down, final_norm_g, loss_target, m_ffn1_norm_g, m_ffn1_w_gate, m_ffn1_w_up, m_ffn1_w_down, m_mix_norm_g, m_w_in, m_ret_norm_g, m_gla_w_a2, m_gla_b_a, m_gla_norm_g, m_w_out, m_ffn2_norm_g, m_ffn2_w_gate, m_ffn2_w_up, m_ffn2_w_down, m_final_norm_g, v_ffn1_norm_g, v_ffn1_w_gate, v_ffn1_w_up, v_ffn1_w_down, v_mix_norm_g, v_w_in, v_ret_norm_g, v_gla_w_a2, v_gla_b_a, v_gla_norm_g, v_w_out, v_ffn2_norm_g, v_ffn2_w_gate, v_ffn2_w_up, v_ffn2_w_down, v_final_norm_g):
    xi, yi, ci = _coords()
    dev = 4 * xi + 2 * yi + ci
    owner = jnp.stack([2 * xi + yi, ci]).astype(jnp.int32)

    x0, target = x[0], loss_target[0]
    T, D = x0.shape
    fb = ffn1_w_gate.shape[2]
    ib = w_in.shape[2]
    ab = gla_w_a2.shape[2]
    F = N_DEV * fb
    cos, sin_signed = _rotary_tables(T)
    lg = jnp.repeat(jnp.log(1.0 - 2.0 ** (-5.0 - jnp.arange(HEADS, dtype=F32))), DK)[None, :]
    g_final = final_norm_g.reshape(1, D)

    g1_loc = ffn1_w_gate[0].T[None].astype(BF16)
    u1_loc = ffn1_w_up[0].T[None].astype(BF16)
    d1_loc = ffn1_w_down.astype(BF16)
    g2_loc = ffn2_w_gate[0].T[None].astype(BF16)
    u2_loc = ffn2_w_up[0].T[None].astype(BF16)
    d2_loc = ffn2_w_down.astype(BF16)
    in_loc = w_in[0].T.astype(BF16)
    out_loc = w_out[0].astype(BF16)

    h1, (g1,) = _rms_fwd("ffn1_rms", x0, ffn1_norm_g, comm=_AllGather([g1_loc], ["stack"]))
    g1 = g1.reshape(1, F, D)
    (gate1,), (u1,) = _mm_nstream("ffn1_gate", h1, [g1], [0], "nt", [], [BF16], _identity_epilogue, cn=256,
                                  comm=_AllGather([u1_loc], ["stack"]))
    u1 = u1.reshape(1, F, D)
    (up1, act1), (d1,) = _mm_nstream("ffn1_up", h1, [u1], [0], "nt", [gate1], [BF16, BF16], _up_act_epilogue, cn=256,
                                     comm=_AllGather([d1_loc], ["stack"]))
    d1 = d1.reshape(1, F, D)
    f32_tile, bf16_tile, f32_vec = (F32, "tile"), (BF16, "tile"), (F32, "vec")
    (x1, h2), (in_all, a_all) = _mm_mstream(
        "ffn1_down", [act1], [d1], [0], "nn", [(x0, "tile"), (mix_norm_g, "vec")], [f32_tile, bf16_tile],
        _residual_rms_epilogue(0.5), comm=_AllGather([in_loc, gla_w_a2[0]], ["plain", "plain"]))
    w_in_t = jnp.pad(in_all.reshape(1, N_DEV * ib, D), ((0, 0), (0, PROJ_W - N_DEV * ib), (0, 0)))
    wa2 = jnp.transpose(a_all, (1, 0, 2)).reshape(GATE_RANK, N_DEV * ab)
    wa2p = jnp.pad(wa2, ((0, GL_W - GATE_RANK), (0, 0)))

    (proj,), (g2,) = _mm_nstream("mix_proj", h2, [w_in_t], [0], "nt", [], [F32], _identity_epilogue, cn=640,
                                 comm=_AllGather([g2_loc], ["stack"]))
    (o, ymix, states), (u2, out_all) = _attn_fwd(proj, cos, sin_signed, lg, wa2p, gla_b_a, ret_norm_g, gla_norm_g,
                                                 comm=_AllGather([u2_loc, out_loc], ["stack", "plain"]))
    w_out_full = out_all.reshape(D, D)
    (x2, h3), _ = _mm_mstream("mix_out", [ymix], [w_out_full], [0], "nn", [(x1, "tile"), (ffn2_norm_g, "vec")],
                              [f32_tile, bf16_tile], _residual_rms_epilogue(1.0))
    g2, u2 = g2.reshape(1, F, D), u2.reshape(1, F, D)

    (gate2, up2, act2), (d2,) = _mm_nstream(
        "ffn2_up", h3, [g2, u2], [0, 0], "nt", [], [BF16, BF16, BF16], _silu_mul_epilogue, cn=256,
        comm=_AllGather([d2_loc], ["stack"]))
    d2 = d2.reshape(1, F, D)
    (x3,), _ = _mm_mstream("ffn2_down", [act2], [d2], [0], "nn", [(x2, "tile")], [f32_tile], _residual_epilogue(0.5))

    dx3, dy3b, d_final, loss = _final_loss_bwd("final", x3, g_final, target, 0.5)

    dwd2, _ = _mm_tn("ffn2b_dwd", act2, dy3b, F // 2, D, BF16)
    dwd2 = dwd2.reshape(4, 2, fb, D)
    (dgate2, dup2), (l_wd2_near,) = _mm_nstream("ffn2b_dact", dy3b, [d2], [0], "nt", [gate2, up2], [BF16, BF16],
                                                _dact_epilogue, cn=256, comm=_ReduceScatter([(dwd2, NEAR)]))
    dwg2, (l_wd2_far,) = _mm_tn("ffn2b_dwg", dgate2, h3, F // 2, D, BF16, comm=_ReduceScatter([(dwd2, FAR)]))
    dwg2 = dwg2.reshape(4, 2, fb, D)
    dwu2, (l_wg2_near,) = _mm_tn("ffn2b_dwu", dup2, h3, F // 2, D, BF16, comm=_ReduceScatter([(dwg2, NEAR)]))
    dwu2 = dwu2.reshape(4, 2, fb, D)
    rms_outs = [f32_tile, bf16_tile, f32_vec]
    (dx2, dx2b, d_g2), (l_wg2_far, l_wu2_near) = _mm_mstream(
        "ffn2b_dh", [dgate2, dup2], [g2, u2], [0, 0], "nn", [(x2, "tile"), (ffn2_norm_g, "vec"), (dx3, "tile")],
        rms_outs, _rms_bwd_epilogue(1.0), comm=_ReduceScatter([(dwg2, FAR), (dwu2, NEAR)]))

    (dymix,), _ = _mm_mstream("mixb_dy", [dx2b], [w_out_full], [0], "nt", [], [f32_tile], _plain_epilogue)
    dwout, _ = _mm_tn("mixb_dwout", ymix, dx2b, D, D, BF16)
    dwout = dwout.reshape(4, 2, D // N_DEV, D)
    (dproj, d_ret, d_gla, d_ba, d_wa2p), (l_wu2_far, l_wout) = _attn_bwd(
        proj, cos, sin_signed, lg, wa2p, gla_b_a, ret_norm_g, gla_norm_g, o, dymix, states,
        comm=_ReduceScatter([(dwu2, FAR), (dwout, ALL)]))
    dwin_t, _ = _mm_tn("mixb_dwin", dproj, h2, 640, D, BF16)
    dwin = dwin_t[:N_DEV * ib].reshape(4, 2, ib, D)
    (dx1, dy1b, d_gmix), (l_win_near,) = _mm_mstream(
        "mixb_dh", [dproj], [w_in_t], [0], "nn", [(x1, "tile"), (mix_norm_g, "vec"), (dx2, "tile")],
        rms_outs, _rms_bwd_epilogue(0.5), comm=_ReduceScatter([(dwin, NEAR)]))

    dwd1, (l_win_far,) = _mm_tn("ffn1b_dwd", act1, dy1b, F // 2, D, BF16, comm=_ReduceScatter([(dwin, FAR)]))
    dwd1 = dwd1.reshape(4, 2, fb, D)
    (dgate1, dup1), (l_wd1_near,) = _mm_nstream("ffn1b_dact", dy1b, [d1], [0], "nt", [gate1, up1], [BF16, BF16],
                                                _dact_epilogue, cn=256, comm=_ReduceScatter([(dwd1, NEAR)]))
    dwg1, (l_wd1_far,) = _mm_tn("ffn1b_dwg", dgate1, h1, F // 2, D, BF16, comm=_ReduceScatter([(dwd1, FAR)]))
    dwg1 = dwg1.reshape(4, 2, fb, D)
    core = owner[1:2]
    dwu1, (s_wg1,) = _mm_tn("ffn1b_dwu", dup1, h1, F // 2, D, BF16, comm=_SiblingExchange([dwg1]))
    dwu1 = dwu1.reshape(4, 2, fb, D)
    p_wg1, (s_wu1,) = _pair_sum("pair_wg1", dwg1, s_wg1, core, comm=_SiblingExchange([dwu1]))
    p_wu1, _ = _pair_sum("pair_wu1", dwu1, s_wu1, core)
    (dh1,), (c_wg1, c_wu1) = _mm_mstream(
        "ffn1b_dh", [dgate1, dup1], [g1, u1], [0, 0], "nn", [], [f32_tile], _plain_epilogue,
        comm=_ChipExchange([p_wg1, p_wu1]))
    dx0, _, d_g1 = _rms_bwd("ffn1b_rms", x0, ffn1_norm_g, dh1, dx1, 1.0)

    small = dict(ffn1=d_g1, mix=d_gmix, ffn2=d_g2, final=d_final, ret=d_ret, gla=d_gla, b_a=d_ba)
    flat = jnp.concatenate([small[k].reshape(-1) for k in SMALL_ORDER]
                           + [d_wa2p[:GATE_RANK].reshape(-1), loss[0]])
    rows = -(-flat.shape[0] // 128)
    rows = -(-rows // 8) * 8
    packed = jnp.pad(flat, (0, rows * 128 - flat.shape[0])).reshape(rows, 128)

    transposed = ("ffn1_w_gate", "ffn1_w_up", "ffn2_w_gate", "ffn2_w_up", "w_in")

    def to_2d(nm, a):
        if nm in transposed:
            return a[0].T
        return a.reshape((1, a.shape[0]) if a.ndim == 1 else a.shape[-2:])

    def from_2d(nm, a):
        return a.T[None] if nm in transposed else a.reshape(params[nm][0].shape)

    sums_a, (gathered,) = _owner_sums(
        "sum_a", [(dwg2, [l_wg2_near, l_wg2_far]), (dwu2, [l_wu2_near, l_wu2_far]), (dwd2, [l_wd2_near, l_wd2_far]),
                  (dwin, [l_win_near, l_win_far]), (dwout, [l_wout])], owner,
        comm=_AllGather([packed], ["plain"]))
    sums_b, _ = _owner_sums("sum_b", [(p_wg1, [c_wg1]), (p_wu1, [c_wu1]), (dwd1, [l_wd1_near, l_wd1_far])], owner)
    big_grads = {"ffn2_w_gate": sums_a[0], "ffn2_w_up": sums_a[1], "ffn2_w_down": sums_a[2], "w_out": sums_a[4],
                 "ffn1_w_gate": sums_b[0], "ffn1_w_up": sums_b[1], "ffn1_w_down": sums_b[2]}
    params = dict(
        ffn2_w_gate=(ffn2_w_gate, m_ffn2_w_gate, v_ffn2_w_gate), ffn2_w_up=(ffn2_w_up, m_ffn2_w_up, v_ffn2_w_up),
        ffn2_w_down=(ffn2_w_down, m_ffn2_w_down, v_ffn2_w_down), w_in=(w_in, m_w_in, v_w_in),
        w_out=(w_out, m_w_out, v_w_out), ffn1_w_gate=(ffn1_w_gate, m_ffn1_w_gate, v_ffn1_w_gate),
        ffn1_w_up=(ffn1_w_up, m_ffn1_w_up, v_ffn1_w_up), ffn1_w_down=(ffn1_w_down, m_ffn1_w_down, v_ffn1_w_down),
        ffn1_norm_g=(ffn1_norm_g, m_ffn1_norm_g, v_ffn1_norm_g), mix_norm_g=(mix_norm_g, m_mix_norm_g, v_mix_norm_g),
        ret_norm_g=(ret_norm_g, m_ret_norm_g, v_ret_norm_g), gla_w_a2=(gla_w_a2, m_gla_w_a2, v_gla_w_a2),
        gla_b_a=(gla_b_a, m_gla_b_a, v_gla_b_a), gla_norm_g=(gla_norm_g, m_gla_norm_g, v_gla_norm_g),
        ffn2_norm_g=(ffn2_norm_g, m_ffn2_norm_g, v_ffn2_norm_g), final_norm_g=(final_norm_g, m_final_norm_g, v_final_norm_g))
    grads, updates = {}, {}

    def run_adam(name, names, grad_2d, n_blocks):
        items = [(grad_2d[nm],) + tuple(to_2d(nm, a) for a in params[nm]) for nm in names]
        res, _ = _adamw_group(name, items, n_blocks)
        for nm, r in zip(names, res):
            grads[nm] = from_2d(nm, grad_2d[nm])
            updates[nm] = tuple(from_2d(nm, a) for a in r)

    run_adam("adamw_big", list(big_grads), big_grads, 4)
    run_adam("adamw_w_in", ["w_in"], {"w_in": sums_a[3]}, 1)

    total = _sum_devices("sum_small", gathered.reshape(N_DEV * rows, 128), rows).reshape(-1)
    sizes = [small[k].size for k in SMALL_ORDER] + [GATE_RANK * QK_W, 128]
    offs = [0]
    for s in sizes:
        offs.append(offs[-1] + s)
    pieces = [total[offs[i]:offs[i + 1]] for i in range(len(sizes))]
    g_small = {k: pieces[i].reshape(small[k].shape) for i, k in enumerate(SMALL_ORDER)}
    g_wa2_full = pieces[len(SMALL_ORDER)].reshape(GATE_RANK, QK_W)
    g_wa2 = lax.dynamic_slice(g_wa2_full, (0, dev * ab), (GATE_RANK, ab))
    loss_total = pieces[len(SMALL_ORDER) + 1][0]

    small_grads = {"ffn1_norm_g": g_small["ffn1"], "mix_norm_g": g_small["mix"], "ret_norm_g": g_small["ret"],
                   "gla_w_a2": g_wa2, "gla_b_a": g_small["b_a"], "gla_norm_g": g_small["gla"],
                   "ffn2_norm_g": g_small["ffn2"], "final_norm_g": g_small["final"]}
    run_adam("adamw_small", list(small_grads), small_grads, 1)

    order = ("ffn1_norm_g", "ffn1_w_gate", "ffn1_w_up", "ffn1_w_down", "mix_norm_g", "w_in", "ret_norm_g", "gla_w_a2",
             "gla_b_a", "gla_norm_g", "w_out", "ffn2_norm_g", "ffn2_w_gate", "ffn2_w_up", "ffn2_w_down", "final_norm_g")
    return (loss_total, dx0[None], *[grads[nm] for nm in order], *[updates[nm][0] for nm in order],
            *[updates[nm][1] for nm in order], *[updates[nm][2] for nm in order])
```

```python
import functools
import math

import jax
import jax.numpy as jnp
from jax import lax
from jax.experimental import pallas as pl
from jax.experimental.pallas import tpu as pltpu

F32 = jnp.float32
BF16 = jnp.bfloat16
MESH = pl.DeviceIdType.MESH
HBM = pl.BlockSpec(memory_space=pltpu.HBM)

N_DEV = 8
RMS_EPS = 1e-6
ROPE_BASE = 10000.0
HEADS = 4
DK = 64
DV = 128
QK_W = HEADS * DK
V_W = HEADS * DV
GATE_RANK = 16
GATE_NORM = 16.0
CHUNK = 64
SUPER = 256
PROJ_W = 3200
C_RQ, C_RK, C_RV, C_RG, C_GQ, C_GK, C_GV, C_GG, C_GL = 0, 256, 512, 1024, 1536, 1792, 2048, 2560, 3072
GL_W = PROJ_W - C_GL
ADAM_LR, ADAM_B1, ADAM_B2, ADAM_EPS, ADAM_WD, ADAM_STEP = 0.001, 0.9, 0.999, 1e-08, 0.01, 10
VMEM_LIMIT_V7X = 52 * 1024 * 1024


def _cparams(**kw):
    return pltpu.CompilerParams(vmem_limit_bytes=VMEM_LIMIT_V7X, **kw)


def _dot(a, b, form, precision=None):
    dims = {"nn": (((1,), (0,)), ((), ())), "nt": (((1,), (1,)), ((), ())), "tn": (((0,), (0,)), ((), ()))}[form]
    return lax.dot_general(a, b, dims, preferred_element_type=F32, precision=precision)


def _sigmoid(x):
    return 1.0 / (1.0 + jnp.exp(-x))


def _coords():
    return lax.axis_index("x"), lax.axis_index("y"), lax.axis_index("c")


class _NoComm:
    inputs, out_shapes, scratch = (), (), ()


class _AllGather:
    def __init__(self, arrays, kinds):
        self.inputs = tuple(arrays)
        self.kinds = tuple(kinds)
        n = len(arrays)
        self.out_shapes = tuple(
            jax.ShapeDtypeStruct((a.shape[0], N_DEV) + a.shape[1:] if k == "stack" else (N_DEV,) + a.shape, a.dtype)
            for a, k in zip(arrays, kinds))
        self.scratch = (pltpu.SemaphoreType.DMA((n, 7)), pltpu.SemaphoreType.DMA((n, 7)),
                        pltpu.SemaphoreType.DMA((n,)))

    def _ctx(self, srcs, outs, sems):
        send_sems, recv_sems, local_sems = sems
        x, y, c = _coords()
        me, sibling = (x, y, c), (x, y, 1 - c)
        chips = [(1 - x, y), (x, 1 - y), (1 - x, 1 - y)]

        def blk(m, dev):
            k = 4 * dev[0] + 2 * dev[1] + dev[2]
            return outs[m].at[:, k] if self.kinds[m] == "stack" else outs[m].at[k]

        def copy(m, s, block, to, src=None):
            return pltpu.make_async_remote_copy(
                src_ref=blk(m, block) if src is None else src, dst_ref=blk(m, block),
                send_sem=send_sems.at[m, s], recv_sem=recv_sems.at[m, s], device_id=to, device_id_type=MESH)

        def mine(m):
            return pltpu.make_async_copy(srcs[m], blk(m, me), local_sems.at[m])

        def first(m):
            return [copy(m, 0, me, sibling, src=srcs[m])] + [
                copy(m, 1 + j, me, (*chip, c), src=srcs[m]) for j, chip in enumerate(chips)]

        return me, sibling, chips, c, copy, mine, first

    def start(self, srcs, outs, sems):
        me, sibling, chips, c, copy, mine, first = self._ctx(srcs, outs, sems)
        for m in range(len(srcs)):
            mine(m).start()
            for cp in first(m):
                cp.start()

    def mid(self, srcs, outs, sems):
        me, sibling, chips, c, copy, mine, first = self._ctx(srcs, outs, sems)
        for j, chip in enumerate(chips):
            for m in range(len(srcs)):
                copy(m, 1 + j, (*chip, c), me).wait_recv()
                copy(m, 4 + j, (*chip, c), sibling).start()

    def finish(self, srcs, outs, sems):
        me, sibling, chips, c, copy, mine, first = self._ctx(srcs, outs, sems)
        for m in range(len(srcs)):
            copy(m, 0, sibling, me).wait_recv()
            for j, chip in enumerate(chips):
                copy(m, 4 + j, (*chip, 1 - c), me).wait_recv()
            for cp in first(m):
                cp.wait_send()
            for j, chip in enumerate(chips):
                copy(m, 4 + j, (*chip, c), sibling).wait_send()
            mine(m).wait()


RELATIONS = ((0, 0, 1), (1, 0, 0), (0, 1, 0), (1, 1, 0), (1, 0, 1), (0, 1, 1), (1, 1, 1))
NEAR = (0, 1, 2, 4, 5)
FAR = (3, 6)
ALL = NEAR + FAR


class _ReduceScatter:
    def __init__(self, parts):
        self.inputs = tuple(g for g, _ in parts)
        self.slots = tuple(s for _, s in parts)
        self.out_shapes = tuple(jax.ShapeDtypeStruct((len(s),) + g.shape[2:], g.dtype) for g, s in parts)
        n_max = max(len(s) for s in self.slots)
        n = len(parts)
        self.scratch = (pltpu.SemaphoreType.DMA((n, n_max)), pltpu.SemaphoreType.DMA((n, n_max)))

    def _copies(self, srcs, outs, sems):
        send_sems, recv_sems = sems
        x, y, c = _coords()
        copies = []
        for m, slots in enumerate(self.slots):
            for i, s in enumerate(slots):
                fx, fy, fc = RELATIONS[s]
                px = 1 - x if fx else x
                py = 1 - y if fy else y
                pc = 1 - c if fc else c
                copies.append(pltpu.make_async_remote_copy(
                    src_ref=srcs[m].at[2 * px + py, pc], dst_ref=outs[m].at[i], send_sem=send_sems.at[m, i],
                    recv_sem=recv_sems.at[m, i], device_id=(px, py, pc), device_id_type=MESH))
        return copies

    def start(self, srcs, outs, sems):
        for cp in self._copies(srcs, outs, sems):
            cp.start()

    def mid(self, srcs, outs, sems):
        pass

    def finish(self, srcs, outs, sems):
        for cp in self._copies(srcs, outs, sems):
            cp.wait()


class _SiblingExchange:
    def __init__(self, grads):
        self.inputs = tuple(grads)
        self.out_shapes = tuple(jax.ShapeDtypeStruct((4,) + g.shape[2:], g.dtype) for g in grads)
        self.scratch = (pltpu.SemaphoreType.DMA((len(grads),)), pltpu.SemaphoreType.DMA((len(grads),)))

    def _copies(self, srcs, outs, sems):
        send_sems, recv_sems = sems
        x, y, c = _coords()
        return [pltpu.make_async_remote_copy(
            src_ref=srcs[m].at[:, 1 - c], dst_ref=outs[m], send_sem=send_sems.at[m], recv_sem=recv_sems.at[m],
            device_id=(x, y, 1 - c), device_id_type=MESH) for m in range(len(srcs))]

    def start(self, srcs, outs, sems):
        for cp in self._copies(srcs, outs, sems):
            cp.start()

    def mid(self, srcs, outs, sems):
        pass

    def finish(self, srcs, outs, sems):
        for cp in self._copies(srcs, outs, sems):
            cp.wait()


class _ChipExchange:
    def __init__(self, partials):
        self.inputs = tuple(partials)
        self.out_shapes = tuple(jax.ShapeDtypeStruct((3,) + p.shape[1:], p.dtype) for p in partials)
        n = len(partials)
        self.scratch = (pltpu.SemaphoreType.DMA((n, 3)), pltpu.SemaphoreType.DMA((n, 3)))

    def _copies(self, srcs, outs, sems):
        send_sems, recv_sems = sems
        x, y, c = _coords()
        copies = []
        for m in range(len(srcs)):
            for j, (px, py) in enumerate([(1 - x, y), (x, 1 - y), (1 - x, 1 - y)]):
                copies.append(pltpu.make_async_remote_copy(
                    src_ref=srcs[m].at[2 * px + py], dst_ref=outs[m].at[j], send_sem=send_sems.at[m, j],
                    recv_sem=recv_sems.at[m, j], device_id=(px, py, c), device_id_type=MESH))
        return copies

    def start(self, srcs, outs, sems):
        for cp in self._copies(srcs, outs, sems):
            cp.start()

    def mid(self, srcs, outs, sems):
        pass

    def finish(self, srcs, outs, sems):
        for cp in self._copies(srcs, outs, sems):
            cp.wait()


class _Multi:
    def __init__(self, comms):
        self.comms = comms
        self.inputs = tuple(a for c in comms for a in c.inputs)
        self.out_shapes = tuple(s for c in comms for s in c.out_shapes)
        self.scratch = tuple(s for c in comms for s in c.scratch)

    def _each(self, phase, srcs, outs, sems):
        i = o = k = 0
        for c in self.comms:
            ni, no, nk = len(c.inputs), len(c.out_shapes), len(c.scratch)
            getattr(c, phase)(srcs[i:i + ni], outs[o:o + no], sems[k:k + nk])
            i, o, k = i + ni, o + no, k + nk

    def start(self, srcs, outs, sems):
        self._each("start", srcs, outs, sems)

    def mid(self, srcs, outs, sems):
        self._each("mid", srcs, outs, sems)

    def finish(self, srcs, outs, sems):
        self._each("finish", srcs, outs, sems)


def _call(name, main, grid, in_specs, out_specs, out_shape, args, scratch=(), comm=None, prefetch=None):
    comm = comm or _NoComm()
    counts = [len(in_specs), len(comm.inputs), len(out_shape), len(comm.out_shapes), len(scratch), len(comm.scratch)]
    n_steps = math.prod(grid)
    hosted = bool(comm.inputs)

    def body(*refs):
        if prefetch is not None:
            refs = refs[1:]
        parts, at = [], 0
        for n in counts:
            parts.append(refs[at:at + n])
            at += n
        ins, c_in, outs, c_out, scr, c_scr = parts
        step = pl.program_id(0)
        for d in range(1, len(grid)):
            step = step * grid[d] + pl.program_id(d)
        if hosted:
            @pl.when(step == 0)
            def _():
                comm.start(c_in, c_out, c_scr)
        main(ins, outs, scr)
        if hosted:
            @pl.when(step == max(n_steps - 2, 0))
            def _():
                comm.mid(c_in, c_out, c_scr)

            @pl.when(step == n_steps - 1)
            def _():
                comm.finish(c_in, c_out, c_scr)

    all_in = list(in_specs) + [HBM] * counts[1]
    all_out = list(out_specs) + [HBM] * counts[3]
    all_scratch = list(scratch) + list(comm.scratch)
    shapes = list(out_shape) + list(comm.out_shapes)
    if prefetch is None:
        res = pl.pallas_call(body, name=name, grid=grid, in_specs=all_in, out_specs=all_out, out_shape=shapes,
                             scratch_shapes=all_scratch, compiler_params=_cparams())(*args, *comm.inputs)
    else:
        res = pl.pallas_call(
            body, name=name, out_shape=shapes,
            grid_spec=pltpu.PrefetchScalarGridSpec(num_scalar_prefetch=1, grid=grid, in_specs=all_in,
                                                   out_specs=all_out, scratch_shapes=all_scratch),
            compiler_params=_cparams())(prefetch, *args, *comm.inputs)
    return res[:counts[2]], res[counts[2]:]


def _rms_fwd(name, x, g, comm=None):
    T, D = x.shape
    tm = min(T, 512)

    def main(ins, outs, scr):
        x_ref, g_ref = ins
        xv = x_ref[...]
        r = lax.rsqrt(jnp.mean(xv * xv, axis=-1, keepdims=True) + RMS_EPS)
        outs[0][...] = (xv * r * g_ref[...]).astype(outs[0].dtype)

    tile = pl.BlockSpec((tm, D), lambda i: (i, 0))
    (h,), extra = _call(name, main, (T // tm,), [tile, pl.BlockSpec((1, D), lambda i: (0, 0))], [tile],
                        [jax.ShapeDtypeStruct((T, D), BF16)], (x, g), comm=comm)
    return h, extra


def _final_loss_bwd(name, x, g, target, out_scale):
    T, D = x.shape
    tm = min(T, 512)

    def main(ins, outs, scr):
        x_ref, g_ref, t_ref = ins
        dx_ref, dxb_ref, dg_ref, loss_ref = outs
        i = pl.program_id(0)
        xv = x_ref[...]
        r = lax.rsqrt(jnp.mean(xv * xv, axis=-1, keepdims=True) + RMS_EPS)
        xhat = xv * r
        err = xhat * g_ref[...] - t_ref[...]

        @pl.when(i == 0)
        def _():
            dg_ref[...] = jnp.zeros_like(dg_ref)
            loss_ref[...] = jnp.zeros_like(loss_ref)

        loss_ref[...] += jnp.broadcast_to(jnp.sum(err * err) * (0.5 / D), loss_ref.shape)
        dy = err * (1.0 / D)
        dg_ref[...] += jnp.sum(dy * xhat, axis=0, keepdims=True)
        dxhat = dy * g_ref[...]
        dx = r * (dxhat - xhat * jnp.mean(dxhat * xhat, axis=-1, keepdims=True))
        dx_ref[...] = dx
        dxb_ref[...] = (out_scale * dx).astype(dxb_ref.dtype)

    tile = pl.BlockSpec((tm, D), lambda i: (i, 0))
    vec = pl.BlockSpec((1, D), lambda i: (0, 0))
    outs, _ = _call(name, main, (T // tm,), [tile, vec, tile],
                    [tile, tile, vec, pl.BlockSpec((1, 128), lambda i: (0, 0))],
                    [jax.ShapeDtypeStruct((T, D), F32), jax.ShapeDtypeStruct((T, D), BF16),
                     jax.ShapeDtypeStruct((1, D), F32), jax.ShapeDtypeStruct((1, 128), F32)], (x, g, target))
    return outs


def _mm_nstream(name, a, ws, w_sel, w_form, comps, out_dtypes, epilogue, cn, rows=1024, comm=None):
    T, K = a.shape
    N = ws[0].shape[1]
    rows = min(rows, T)
    assert N % cn == 0 and T % rows == 0
    n_w, n_c = len(ws), len(comps)

    def main(ins, outs, scr):
        a_ref = ins[0]
        w_refs = ins[1:1 + n_w]
        c_refs = ins[1 + n_w:]

        for r in range(T // rows):
            sl = slice(r * rows, (r + 1) * rows)
            a_blk = a_ref[sl, :]
            dots = [_dot(a_blk, w_ref[...], w_form) for w_ref in w_refs]
            res = epilogue(dots, [c_ref[sl, :] for c_ref in c_refs])
            for o_ref, o in zip(outs, res):
                o_ref[sl, :] = o.astype(o_ref.dtype)

    if w_form == "nt":
        w_specs = [pl.BlockSpec((None, cn, K), functools.partial(lambda j, s: (s, j, 0), s=s)) for s in w_sel]
    else:
        w_specs = [pl.BlockSpec((K, cn), lambda j: (0, j)) for _ in ws]
    chunk = pl.BlockSpec((T, cn), lambda j: (0, j))
    return _call(name, main, (N // cn,), [pl.BlockSpec((T, K), lambda j: (0, 0))] + w_specs + [chunk] * n_c,
                 [chunk] * len(out_dtypes), [jax.ShapeDtypeStruct((T, N), dt) for dt in out_dtypes],
                 (a, *ws, *comps), comm=comm)


def _mm_mstream(name, as_, ws, w_sel, w_form, extras, outs_desc, epilogue, tm=512, comm=None):
    T = as_[0].shape[0]
    tm = min(tm, T)
    n_a = len(as_)
    w_shapes = [w.shape[-2:] for w in ws]
    N = w_shapes[0][1] if w_form == "nn" else w_shapes[0][0]

    def main(ins, outs, scr):
        a_refs = ins[:n_a]
        w_refs = ins[n_a:2 * n_a]
        acc = None
        for a_ref, w_ref in zip(a_refs, w_refs):
            d = _dot(a_ref[...], w_ref[...], w_form)
            acc = d if acc is None else acc + d
        epilogue(acc, ins[2 * n_a:], outs)

    kind_spec = {"tile": pl.BlockSpec((tm, N), lambda i: (i, 0)), "vec": pl.BlockSpec((1, N), lambda i: (0, 0))}
    kind_shape = {"tile": (T, N), "vec": (1, N)}
    a_specs = [pl.BlockSpec((tm, a.shape[1]), lambda i: (i, 0)) for a in as_]
    w_specs = []
    for w, s in zip(ws, w_sel):
        if w.ndim == 3:
            w_specs.append(pl.BlockSpec((None,) + tuple(w.shape[1:]), functools.partial(lambda i, s: (s, 0, 0), s=s),
                                        pipeline_mode=pl.Buffered(1)))
        else:
            w_specs.append(pl.BlockSpec(tuple(w.shape), lambda i: (0, 0), pipeline_mode=pl.Buffered(1)))
    args = list(as_) + list(ws) + [e for e, _ in extras]
    return _call(name, main, (T // tm,), a_specs + w_specs + [kind_spec[k] for _, k in extras],
                 [kind_spec[k] for _, k in outs_desc],
                 [jax.ShapeDtypeStruct(kind_shape[k], dt) for dt, k in outs_desc], args, comm=comm)


def _plain_epilogue(acc, ex, outs):
    outs[0][...] = acc.astype(outs[0].dtype)


def _residual_epilogue(scale):
    def ep(acc, ex, outs):
        outs[0][...] = ex[0][...] + scale * acc
    return ep


def _residual_rms_epilogue(scale):
    def ep(acc, ex, outs):
        xv = ex[0][...] + scale * acc
        outs[0][...] = xv
        r = lax.rsqrt(jnp.mean(xv * xv, axis=-1, keepdims=True) + RMS_EPS)
        outs[1][...] = (xv * r * ex[1][...]).astype(outs[1].dtype)
    return ep


def _rms_bwd_epilogue(out_scale):
    def ep(acc, ex, outs):
        x_ref, g_ref, dres_ref = ex
        dx_ref, dxb_ref, dg_ref = outs
        xv = x_ref[...]
        r = lax.rsqrt(jnp.mean(xv * xv, axis=-1, keepdims=True) + RMS_EPS)
        xhat = xv * r

        @pl.when(pl.program_id(0) == 0)
        def _():
            dg_ref[...] = jnp.zeros_like(dg_ref)

        dg_ref[...] += jnp.sum(acc * xhat, axis=0, keepdims=True)
        dxhat = acc * g_ref[...]
        dx = r * (dxhat - xhat * jnp.mean(dxhat * xhat, axis=-1, keepdims=True)) + dres_ref[...]
        dx_ref[...] = dx
        dxb_ref[...] = (out_scale * dx).astype(dxb_ref.dtype)
    return ep


def _mm_tn(name, a, b, tmo, tno, out_dtype, tk=1024, comm=None):
    T, Ma = a.shape
    Nb = b.shape[1]
    tk = min(tk, T)
    nk = T // tk

    def main(ins, outs, scr):
        a_ref, b_ref = ins
        (acc_ref,) = scr
        k = pl.program_id(2)

        @pl.when(k == 0)
        def _():
            acc_ref[...] = jnp.zeros_like(acc_ref)

        acc_ref[...] += _dot(a_ref[...], b_ref[...], "tn")

        @pl.when(k == nk - 1)
        def _():
            outs[0][...] = acc_ref[...].astype(outs[0].dtype)

    (out,), extra = _call(
        name, main, (Ma // tmo, Nb // tno, nk),
        [pl.BlockSpec((tk, tmo), lambda i, j, k: (k, i)), pl.BlockSpec((tk, tno), lambda i, j, k: (k, j))],
        [pl.BlockSpec((tmo, tno), lambda i, j, k: (i, j))], [jax.ShapeDtypeStruct((Ma, Nb), out_dtype)],
        (a, b), scratch=[pltpu.VMEM((tmo, tno), F32)], comm=comm)
    return out, extra


def _swiglu_parts(g, u):
    s = _sigmoid(g)
    silu = g * s
    return [u * (s + silu * (1.0 - s)), silu, silu * u]


def _silu_mul_epilogue(dots, comps):
    g, u = dots
    return _swiglu_parts(g, u)


def _up_act_epilogue(dots, comps):
    (u,) = dots
    return _swiglu_parts(comps[0].astype(F32), u)


def _dact_epilogue(dots, comps):
    dact = dots[0].astype(BF16)
    return [dact * comps[0], dact * comps[1]]


def _identity_epilogue(dots, comps):
    return list(dots)


def _swap_halves(x):
    lane = lax.broadcasted_iota(jnp.int32, x.shape, 1)
    first = (lane % DK) < (DK // 2)
    return jnp.where(first, pltpu.roll(x, 128 - DK // 2, 1), pltpu.roll(x, DK // 2, 1))


def _rotary(t, cos, sin_signed):
    halves = []
    for p in range(QK_W // 128):
        th = t[:, 128 * p:128 * (p + 1)]
        halves.append(th * cos + _swap_halves(th) * sin_signed)
    return jnp.concatenate(halves, axis=1)


def _rotary_transposed(d, cos, sin_signed):
    halves = []
    for p in range(QK_W // 128):
        dh = d[:, 128 * p:128 * (p + 1)]
        halves.append(dh * cos + _swap_halves(dh * sin_signed))
    return jnp.concatenate(halves, axis=1)


def _log_sigmoid(x):
    return jnp.minimum(x, 0.0) - jnp.log(1.0 + jnp.exp(-jnp.abs(x)))


def _attn_masks():
    row = lax.broadcasted_iota(jnp.int32, (SUPER, SUPER), 0)
    col = lax.broadcasted_iota(jnp.int32, (SUPER, SUPER), 1)
    same = (row // CHUNK) == (col // CHUNK)
    return row, col, same


def _group_inputs(grp, pr, cos, sin_signed, lg, wa2, ba):
    if grp == 0:
        q = _rotary(pr[:, C_RQ:C_RQ + QK_W], cos, sin_signed)
        k = _rotary(pr[:, C_RK:C_RK + QK_W], cos, sin_signed) * (DK ** -0.5)
        v = pr[:, C_RV:C_RV + V_W]
        gate = pr[:, C_RG:C_RG + V_W]
        pos = lax.broadcasted_iota(jnp.int32, (SUPER, QK_W), 0).astype(F32) + 1.0
        return q, k, v, gate, pos * lg, None, None
    q = pr[:, C_GQ:C_GQ + QK_W] * (DK ** -0.5)
    k = pr[:, C_GK:C_GK + QK_W]
    v = pr[:, C_GV:C_GV + V_W]
    gate = pr[:, C_GG:C_GG + V_W]
    glow = pr[:, C_GL:C_GL + GL_W]
    logit = _dot(glow.astype(BF16), wa2.astype(BF16), "nn") + ba
    la = _log_sigmoid(logit) * (1.0 / GATE_NORM)
    row, col, _ = _attn_masks()
    lower = (col <= row).astype(F32)
    b_cum = _dot(lower, la, "nn", precision=lax.Precision.HIGHEST)
    return q, k, v, gate, b_cum, glow, logit


def _decay_factors(q, k, b_cum):
    c = b_cum[SUPER // 2 - 1:SUPER // 2, :]
    bl = b_cum[SUPER - 1:SUPER, :]
    e1 = jnp.exp(b_cum - c)
    e2 = jnp.exp(c - b_cum)
    e_b = jnp.exp(b_cum)
    e_l = jnp.exp(bl - b_cum)
    return dict(e1=e1, e2=e2, eb=e_b, el=e_l, ebl=jnp.exp(bl),
                qp=q * e1, qm=q * e2, kp=k * e1, km=k * e2, qs=q * e_b, kl=k * e_l)


def _state_block_mask():
    r = lax.broadcasted_iota(jnp.int32, (V_W, QK_W), 0)
    c = lax.broadcasted_iota(jnp.int32, (V_W, QK_W), 1)
    return (r // DV) == (c // DK)


def _attn_fwd(proj, cos, sin_signed, lg, wa2p, ba, gn_ret, gn_gla, comm=None):
    T = proj.shape[0]
    n_s = T // SUPER

    def main(ins, outs, scr):
        pr_ref, cos_ref, sin_ref, lg_ref, wa2_ref, ba_ref, gr_ref, gg_ref = ins
        o_ref, y_ref, st_ref = outs
        (s_ref,) = scr
        i = pl.program_id(0)

        @pl.when(i == 0)
        def _():
            s_ref[...] = jnp.zeros_like(s_ref)

        pr = pr_ref
        row, col, same = _attn_masks()
        m1 = col <= row
        m2 = jnp.logical_and(col > row, same)
        lane = lax.broadcasted_iota(jnp.int32, (1, QK_W), 1)
        blockmask = _state_block_mask()
        for grp in range(2):
            q, k, v, gate, b_cum, _, _ = _group_inputs(grp, pr, cos_ref[...], sin_ref[...], lg_ref[...],
                                                      wa2_ref[...], ba_ref[...])
            f = _decay_factors(q, k, b_cum)
            gn = gr_ref[...] if grp == 0 else gg_ref[...]
            s_prev = s_ref[grp]
            st_ref[0, grp] = s_prev
            o_inter = _dot(f["qs"].astype(BF16), s_prev.astype(BF16), "nt")
            kmb = f["km"].astype(BF16)
            kpb = f["kp"].astype(BF16)
            vb = v.astype(BF16)
            for h in range(HEADS):
                hm = (lane // DK) == h
                a1 = _dot(jnp.where(hm, f["qp"], 0.0).astype(BF16), kmb, "nt")
                a2 = _dot(jnp.where(hm, f["qm"], 0.0).astype(BF16), kpb, "nt")
                a = jnp.where(m1, a1, jnp.where(m2, a2, 0.0))
                lo = grp * V_W + h * DV
                o_h = _dot(a.astype(BF16), vb[:, h * DV:(h + 1) * DV], "nn") + o_inter[:, h * DV:(h + 1) * DV]
                o_ref[:, lo:lo + DV] = o_h
                r = lax.rsqrt(jnp.mean(o_h * o_h, axis=-1, keepdims=True) + RMS_EPS)
                gte = gate[:, h * DV:(h + 1) * DV]
                y = o_h * r * gn[:, h * DV:(h + 1) * DV] * (gte * _sigmoid(gte))
                y_ref[:, lo:lo + DV] = y.astype(y_ref.dtype)
            upd = _dot(vb, f["kl"].astype(BF16), "tn")
            s_ref[grp] = s_prev * f["ebl"] + jnp.where(blockmask, upd, 0.0)

    const = lambda shape: pl.BlockSpec(shape, lambda i: tuple(0 for _ in shape))
    return _call(
        "attn_fwd", main, (n_s,),
        [pl.BlockSpec((SUPER, PROJ_W), lambda i: (i, 0)),
         pl.BlockSpec((SUPER, 128), lambda i: (i, 0)), pl.BlockSpec((SUPER, 128), lambda i: (i, 0)),
         const((1, QK_W)), const((GL_W, QK_W)), const((1, QK_W)), const((1, V_W)), const((1, V_W))],
        [pl.BlockSpec((SUPER, 2 * V_W), lambda i: (i, 0)), pl.BlockSpec((SUPER, 2 * V_W), lambda i: (i, 0)),
         pl.BlockSpec((1, 2, V_W, QK_W), lambda i: (i, 0, 0, 0))],
        [jax.ShapeDtypeStruct((T, 2 * V_W), F32), jax.ShapeDtypeStruct((T, 2 * V_W), BF16),
         jax.ShapeDtypeStruct((n_s, 2, V_W, QK_W), F32)],
        (proj, cos, sin_signed, lg, wa2p, ba, gn_ret, gn_gla),
        scratch=[pltpu.VMEM((2, V_W, QK_W), F32)], comm=comm)


def _attn_bwd(proj, cos, sin_signed, lg, wa2p, ba, gn_ret, gn_gla, o, dy, states, comm=None):
    T = proj.shape[0]
    n_s = T // SUPER

    def main(ins, outs, scr):
        pr_ref, cos_ref, sin_ref, lg_ref, wa2_ref, ba_ref, gr_ref, gg_ref, o_ref, dy_ref, st_ref = ins
        dp_ref, dgr_ref, dgg_ref, dba_ref, dwa_ref = outs
        (ds_ref,) = scr
        i = pl.program_id(0)

        @pl.when(i == 0)
        def _():
            ds_ref[...] = jnp.zeros_like(ds_ref)
            dgr_ref[...] = jnp.zeros_like(dgr_ref)
            dgg_ref[...] = jnp.zeros_like(dgg_ref)
            dba_ref[...] = jnp.zeros_like(dba_ref)
            dwa_ref[...] = jnp.zeros_like(dwa_ref)

        pr = pr_ref
        cos = cos_ref[...]
        sin_signed = sin_ref[...]
        row, col, same = _attn_masks()
        m1 = col <= row
        m2 = jnp.logical_and(col > row, same)
        m1t = row <= col
        m2t = jnp.logical_and(row > col, same)
        lane = lax.broadcasted_iota(jnp.int32, (1, QK_W), 1)
        blockmask = _state_block_mask()
        for grp in range(2):
            q, k, v, gate, b_cum, glow, logit = _group_inputs(grp, pr, cos, sin_signed, lg_ref[...],
                                                              wa2_ref[...], ba_ref[...])
            f = _decay_factors(q, k, b_cum)
            gn = gr_ref[...] if grp == 0 else gg_ref[...]
            dgn_ref = dgr_ref if grp == 0 else dgg_ref
            do_parts, dgate_parts, dgn_parts = [], [], []
            for h in range(HEADS):
                lo = grp * V_W + h * DV
                o_h = o_ref[:, lo:lo + DV]
                r = lax.rsqrt(jnp.mean(o_h * o_h, axis=-1, keepdims=True) + RMS_EPS)
                n = o_h * r
                gte = gate[:, h * DV:(h + 1) * DV]
                sg = _sigmoid(gte)
                dy_h = dy_ref[:, lo:lo + DV]
                gn_h = gn[:, h * DV:(h + 1) * DV]
                dgate_parts.append(dy_h * n * gn_h * (sg * (1.0 + gte * (1.0 - sg))))
                dz = dy_h * (gte * sg)
                dgn_parts.append(jnp.sum(dz * n, axis=0, keepdims=True))
                dn = dz * gn_h
                do_parts.append(r * (dn - n * jnp.mean(dn * n, axis=-1, keepdims=True)))
            dgn_ref[...] += jnp.concatenate(dgn_parts, axis=1)
            dgate = jnp.concatenate(dgate_parts, axis=1)
            do = jnp.concatenate(do_parts, axis=1)
            dob = do.astype(BF16)
            vb = v.astype(BF16)
            s_prev = st_ref[0, grp]
            ds_new = ds_ref[grp]
            dsb = ds_new.astype(BF16)
            qpb, qmb = f["qp"].astype(BF16), f["qm"].astype(BF16)
            kpb, kmb = f["kp"].astype(BF16), f["km"].astype(BF16)
            dqp = jnp.zeros((SUPER, QK_W), F32)
            dqm = jnp.zeros((SUPER, QK_W), F32)
            dkp = jnp.zeros((SUPER, QK_W), F32)
            dkm = jnp.zeros((SUPER, QK_W), F32)
            dv_parts = []
            for h in range(HEADS):
                hm = (lane // DK) == h
                qp_h = jnp.where(hm, f["qp"], 0.0).astype(BF16)
                qm_h = jnp.where(hm, f["qm"], 0.0).astype(BF16)
                kp_h = jnp.where(hm, f["kp"], 0.0).astype(BF16)
                km_h = jnp.where(hm, f["km"], 0.0).astype(BF16)
                at = jnp.where(m1t, _dot(km_h, qpb, "nt"), jnp.where(m2t, _dot(kp_h, qmb, "nt"), 0.0))
                do_h = dob[:, h * DV:(h + 1) * DV]
                v_h = vb[:, h * DV:(h + 1) * DV]
                dv_parts.append(_dot(at.astype(BF16), do_h, "nn"))
                da = _dot(do_h, v_h, "nt")
                dat = _dot(v_h, do_h, "nt")
                da1 = jnp.where(m1, da, 0.0).astype(BF16)
                da2 = jnp.where(m2, da, 0.0).astype(BF16)
                da1t = jnp.where(m1t, dat, 0.0).astype(BF16)
                da2t = jnp.where(m2t, dat, 0.0).astype(BF16)
                dqp = dqp + _dot(da1, km_h, "nn")
                dqm = dqm + _dot(da2, kp_h, "nn")
                dkm = dkm + _dot(da1t, qp_h, "nn")
                dkp = dkp + _dot(da2t, qm_h, "nn")
            klb = f["kl"].astype(BF16)
            qsb = f["qs"].astype(BF16)
            dqs = _dot(dob, s_prev.astype(BF16), "nn")
            dkl = _dot(vb, dsb, "nn")
            dv = jnp.concatenate(dv_parts, axis=1) + _dot(klb, dsb, "nt")
            ds_ref[grp] = ds_new * f["ebl"] + jnp.where(blockmask, _dot(dob, qsb, "tn"), 0.0)
            dq = dqp * f["e1"] + dqm * f["e2"] + dqs * f["eb"]
            dk = dkm * f["e2"] + dkp * f["e1"] + dkl * f["el"]
            if grp == 0:
                dq = _rotary_transposed(dq, cos, sin_signed)
                dk = _rotary_transposed(dk * (DK ** -0.5), cos, sin_signed)
                dp_ref[:, C_RQ:C_RQ + QK_W] = dq.astype(dp_ref.dtype)
                dp_ref[:, C_RK:C_RK + QK_W] = dk.astype(dp_ref.dtype)
                dp_ref[:, C_RV:C_RV + V_W] = dv.astype(dp_ref.dtype)
                dp_ref[:, C_RG:C_RG + V_W] = dgate.astype(dp_ref.dtype)
            else:
                dkl_kl = dkl * klb.astype(F32)
                db = (dqp * qpb.astype(F32) - dkm * kmb.astype(F32) - dqm * qmb.astype(F32)
                      + dkp * kpb.astype(F32) + dqs * qsb.astype(F32) - dkl_kl)
                last = (jnp.sum(dkl_kl, axis=0, keepdims=True)
                        + f["ebl"] * jnp.sum(s_prev * ds_new, axis=0, keepdims=True))
                rowq = lax.broadcasted_iota(jnp.int32, (SUPER, QK_W), 0)
                db = db + jnp.where(rowq == SUPER - 1, last, 0.0)
                upper = (col >= row).astype(F32)
                dla = _dot(upper, db, "nn", precision=lax.Precision.HIGHEST)
                dlogit = dla * (1.0 / GATE_NORM) * (1.0 - _sigmoid(logit))
                dlb = dlogit.astype(BF16)
                dglow = _dot(dlb, wa2_ref[...].astype(BF16), "nt")
                dwa_ref[...] += _dot(glow.astype(BF16), dlb, "tn")
                dba_ref[...] += jnp.sum(dlogit, axis=0, keepdims=True)
                dp_ref[:, C_GQ:C_GQ + QK_W] = (dq * (DK ** -0.5)).astype(dp_ref.dtype)
                dp_ref[:, C_GK:C_GK + QK_W] = dk.astype(dp_ref.dtype)
                dp_ref[:, C_GV:C_GV + V_W] = dv.astype(dp_ref.dtype)
                dp_ref[:, C_GG:C_GG + V_W] = dgate.astype(dp_ref.dtype)
                dp_ref[:, C_GL:C_GL + GL_W] = dglow.astype(dp_ref.dtype)

    rev = lambda i: n_s - 1 - i
    const = lambda shape: pl.BlockSpec(shape, lambda i: tuple(0 for _ in shape))
    return _call(
        "attn_bwd", main, (n_s,),
        [pl.BlockSpec((SUPER, PROJ_W), lambda i: (rev(i), 0)),
         pl.BlockSpec((SUPER, 128), lambda i: (rev(i), 0)), pl.BlockSpec((SUPER, 128), lambda i: (rev(i), 0)),
         const((1, QK_W)), const((GL_W, QK_W)), const((1, QK_W)), const((1, V_W)), const((1, V_W)),
         pl.BlockSpec((SUPER, 2 * V_W), lambda i: (rev(i), 0)),
         pl.BlockSpec((SUPER, 2 * V_W), lambda i: (rev(i), 0)),
         pl.BlockSpec((1, 2, V_W, QK_W), lambda i: (rev(i), 0, 0, 0))],
        [pl.BlockSpec((SUPER, PROJ_W), lambda i: (rev(i), 0)),
         const((1, V_W)), const((1, V_W)), const((1, QK_W)), const((GL_W, QK_W))],
        [jax.ShapeDtypeStruct((T, PROJ_W), BF16),
         jax.ShapeDtypeStruct((1, V_W), F32), jax.ShapeDtypeStruct((1, V_W), F32),
         jax.ShapeDtypeStruct((1, QK_W), F32), jax.ShapeDtypeStruct((GL_W, QK_W), F32)],
        (proj, cos, sin_signed, lg, wa2p, ba, gn_ret, gn_gla, o, dy, states),
        scratch=[pltpu.VMEM((2, V_W, QK_W), F32)], comm=comm)


def _rotary_tables(T):
    half = DK // 2
    inv = ROPE_BASE ** (-jnp.arange(half, dtype=F32) * 2.0 / DK)
    ang = jnp.arange(T, dtype=F32)[:, None] * inv[None, :]
    cos, sin = jnp.cos(ang), jnp.sin(ang)
    cos_head = jnp.concatenate([cos, cos], axis=1)
    sin_head = jnp.concatenate([-sin, sin], axis=1)
    return jnp.tile(cos_head, (1, 128 // DK)), jnp.tile(sin_head, (1, 128 // DK))


def _sum_devices(name, gathered, m_per):
    def body(g_ref, o_ref):
        acc = g_ref[0:m_per, :]
        for k in range(1, N_DEV):
            acc = acc + g_ref[k * m_per:(k + 1) * m_per, :]
        o_ref[...] = acc

    return pl.pallas_call(body, name=name, out_shape=jax.ShapeDtypeStruct((m_per, 128), F32))(gathered)


def _pair_sum(name, grad, landed, core, comm=None):
    R, C = grad.shape[2:]

    def main(ins, outs, scr):
        outs[0][...] = (ins[0][...].astype(F32) + ins[1][...].astype(F32)).astype(outs[0].dtype)

    blk = pl.BlockSpec((None, R, C), lambda j, s: (j, 0, 0))
    (out,), extra = _call(name, main, (4,), [pl.BlockSpec((None, None, R, C), lambda j, s: (j, s[0], 0, 0)), blk],
                          [blk], [jax.ShapeDtypeStruct((4, R, C), BF16)], (grad, landed), prefetch=core, comm=comm)
    return out, extra


def _owner_sums(name, items, owner, comm=None):
    counts = [1 + len(landed) for _, landed in items]

    def main(ins, outs, scr):
        at = 0
        for o_ref, n in zip(outs, counts):
            acc = ins[at][...].astype(F32)
            for l_ref in ins[at + 1:at + n]:
                for j in range(l_ref.shape[0]):
                    acc = acc + l_ref[j].astype(F32)
            o_ref[...] = acc
            at += n

    once = pl.Buffered(1)
    in_specs, out_specs, out_shape, args = [], [], [], []
    for grad, landed in items:
        R, C = grad.shape[-2:]
        if grad.ndim == 4:
            in_specs.append(pl.BlockSpec((None, None, R, C), lambda i, s: (s[0], s[1], 0, 0), pipeline_mode=once))
        else:
            in_specs.append(pl.BlockSpec((None, R, C), lambda i, s: (s[0], 0, 0), pipeline_mode=once))
        in_specs += [pl.BlockSpec(tuple(l.shape), lambda i, s: (0, 0, 0), pipeline_mode=once) for l in landed]
        out_specs.append(pl.BlockSpec((R, C), lambda i, s: (0, 0)))
        out_shape.append(jax.ShapeDtypeStruct((R, C), F32))
        args += [grad, *landed]
    return _call(name, main, (1,), in_specs, out_specs, out_shape, args, comm=comm, prefetch=owner)


def _rms_bwd(name, x, g, dh, dres, out_scale):
    T, D = x.shape
    tm = min(T, 512)
    ep = _rms_bwd_epilogue(out_scale)

    def main(ins, outs, scr):
        x_ref, g_ref, dh_ref, dres_ref = ins
        ep(dh_ref[...], (x_ref, g_ref, dres_ref), outs)

    tile = pl.BlockSpec((tm, D), lambda i: (i, 0))
    vec = pl.BlockSpec((1, D), lambda i: (0, 0))
    outs, _ = _call(name, main, (T // tm,), [tile, vec, tile, tile], [tile, tile, vec],
                    [jax.ShapeDtypeStruct((T, D), F32), jax.ShapeDtypeStruct((T, D), BF16),
                     jax.ShapeDtypeStruct((1, D), F32)], (x, g, dh, dres))
    return outs


def _adamw_group(name, items, n_blocks, comm=None):
    n = len(items)

    def main(ins, outs, scr):
        for p in range(n):
            g_ref, w_ref, m_ref, v_ref = ins[4 * p:4 * p + 4]
            d_ref, nm_ref, nv_ref = outs[3 * p:3 * p + 3]
            gv = g_ref[...]
            nm = ADAM_B1 * m_ref[...] + (1.0 - ADAM_B1) * gv
            nv = ADAM_B2 * v_ref[...] + (1.0 - ADAM_B2) * (gv * gv)
            m_hat = nm / (1.0 - ADAM_B1 ** ADAM_STEP)
            v_hat = nv / (1.0 - ADAM_B2 ** ADAM_STEP)
            d_ref[...] = -ADAM_LR * (m_hat / (jnp.sqrt(v_hat) + ADAM_EPS) + ADAM_WD * w_ref[...])
            nm_ref[...] = nm
            nv_ref[...] = nv

    in_specs, out_specs, out_shape, args = [], [], [], []
    for item in items:
        R, C = item[1].shape
        assert R % n_blocks == 0
        spec = pl.BlockSpec((R // n_blocks, C), lambda i: (i, 0))
        in_specs += [spec] * 4
        out_specs += [spec] * 3
        out_shape += [jax.ShapeDtypeStruct((R, C), F32)] * 3
        args += list(item)
    outs, extra = _call(name, main, (n_blocks,), in_specs, out_specs, out_shape, args, comm=comm)
    return [tuple(outs[3 * p:3 * p + 3]) for p in range(n)], extra


SMALL_ORDER = ("ffn1", "mix", "ffn2", "final", "ret", "gla", "b_a")


def kernel(x, ffn1_norm_g, ffn1_w_gate, ffn1_w_up, ffn1_w_down, mix_norm_g, w_in, ret_norm_g, gla_w_a2, gla_b_a, gla_norm_g, w_out, ffn2_norm_g, ffn2_w_gate, ffn2_w_up, ffn2_w_down, final_norm_g, loss_target, m_ffn1_norm_g, m_ffn1_w_gate, m_ffn1_w_up, m_ffn1_w_down, m_mix_norm_g, m_w_in, m_ret_norm_g, m_gla_w_a2, m_gla_b_a, m_gla_norm_g, m_w_out, m_ffn2_norm_g, m_ffn2_w_gate, m_ffn2_w_up, m_ffn2_w_down, m_final_norm_g, v_ffn1_norm_g, v_ffn1_w_gate, v_ffn1_w_up, v_ffn1_w_down, v_mix_norm_g, v_w_in, v_ret_norm_g, v_gla_w_a2, v_gla_b_a, v_gla_norm_g, v_w_out, v_ffn2_norm_g, v_ffn2_w_gate, v_ffn2_w_up, v_ffn2_w_down, v_final_norm_g):
    xi, yi, ci = _coords()
    dev = 4 * xi + 2 * yi + ci
    owner = jnp.stack([2 * xi + yi, ci]).astype(jnp.int32)

    x0, target = x[0], loss_target[0]
    T, D = x0.shape
    fb = ffn1_w_gate.shape[2]
    ib = w_in.shape[2]
    ab = gla_w_a2.shape[2]
    F = N_DEV * fb
    cos, sin_signed = _rotary_tables(T)
    lg = jnp.repeat(jnp.log(1.0 - 2.0 ** (-5.0 - jnp.arange(HEADS, dtype=F32))), DK)[None, :]
    g_final = final_norm_g.reshape(1, D)

    g1_loc = ffn1_w_gate[0].T[None].astype(BF16)
    u1_loc = ffn1_w_up[0].T[None].astype(BF16)
    d1_loc = ffn1_w_down.astype(BF16)
    g2_loc = ffn2_w_gate[0].T[None].astype(BF16)
    u2_loc = ffn2_w_up[0].T[None].astype(BF16)
    d2_loc = ffn2_w_down.astype(BF16)
    in_loc = w_in[0].T.astype(BF16)
    out_loc = w_out[0].astype(BF16)

    h1, (g1,) = _rms_fwd("ffn1_rms", x0, ffn1_norm_g, comm=_AllGather([g1_loc], ["stack"]))
    g1 = g1.reshape(1, F, D)
    (gate1,), (u1,) = _mm_nstream("ffn1_gate", h1, [g1], [0], "nt", [], [BF16], _identity_epilogue, cn=256,
                                  comm=_AllGather([u1_loc], ["stack"]))
    u1 = u1.reshape(1, F, D)
    (dsu1, sl1, act1), (d1,) = _mm_nstream("ffn1_up", h1, [u1], [0], "nt", [gate1], [BF16, BF16, BF16],
                                           _up_act_epilogue, cn=256, comm=_AllGather([d1_loc], ["stack"]))
    d1 = d1.reshape(1, F, D)
    f32_tile, bf16_tile, f32_vec = (F32, "tile"), (BF16, "tile"), (F32, "vec")
    (x1, h2), (in_all, a_all) = _mm_mstream(
        "ffn1_down", [act1], [d1], [0], "nn", [(x0, "tile"), (mix_norm_g, "vec")], [f32_tile, bf16_tile],
        _residual_rms_epilogue(0.5), comm=_AllGather([in_loc, gla_w_a2[0]], ["plain", "plain"]))
    w_in_t = jnp.pad(in_all.reshape(1, N_DEV * ib, D), ((0, 0), (0, PROJ_W - N_DEV * ib), (0, 0)))
    wa2 = jnp.transpose(a_all, (1, 0, 2)).reshape(GATE_RANK, N_DEV * ab)
    wa2p = jnp.pad(wa2, ((0, GL_W - GATE_RANK), (0, 0)))

    (proj,), (g2,) = _mm_nstream("mix_proj", h2, [w_in_t], [0], "nt", [], [F32], _identity_epilogue, cn=640,
                                 comm=_AllGather([g2_loc], ["stack"]))
    (o, ymix, states), (u2, out_all) = _attn_fwd(proj, cos, sin_signed, lg, wa2p, gla_b_a, ret_norm_g, gla_norm_g,
                                                 comm=_AllGather([u2_loc, out_loc], ["stack", "plain"]))
    w_out_full = out_all.reshape(D, D)
    (x2, h3), _ = _mm_mstream("mix_out", [ymix], [w_out_full], [0], "nn", [(x1, "tile"), (ffn2_norm_g, "vec")],
                              [f32_tile, bf16_tile], _residual_rms_epilogue(1.0))
    g2, u2 = g2.reshape(1, F, D), u2.reshape(1, F, D)

    (dsu2, sl2, act2), (d2,) = _mm_nstream(
        "ffn2_up", h3, [g2, u2], [0, 0], "nt", [], [BF16, BF16, BF16], _silu_mul_epilogue, cn=256,
        comm=_AllGather([d2_loc], ["stack"]))
    d2 = d2.reshape(1, F, D)
    (x3,), _ = _mm_mstream("ffn2_down", [act2], [d2], [0], "nn", [(x2, "tile")], [f32_tile], _residual_epilogue(0.5))

    dx3, dy3b, d_final, loss = _final_loss_bwd("final", x3, g_final, target, 0.5)

    dwd2, _ = _mm_tn("ffn2b_dwd", act2, dy3b, F // 2, D, BF16)
    dwd2 = dwd2.reshape(4, 2, fb, D)
    (dgate2, dup2), (l_wd2_near,) = _mm_nstream("ffn2b_dact", dy3b, [d2], [0], "nt", [dsu2, sl2], [BF16, BF16],
                                                _dact_epilogue, cn=256, comm=_ReduceScatter([(dwd2, NEAR)]))
    dwg2, (l_wd2_far,) = _mm_tn("ffn2b_dwg", dgate2, h3, F // 2, D, BF16, comm=_ReduceScatter([(dwd2, FAR)]))
    dwg2 = dwg2.reshape(4, 2, fb, D)
    dwu2, (l_wg2_near,) = _mm_tn("ffn2b_dwu", dup2, h3, F // 2, D, BF16, comm=_ReduceScatter([(dwg2, NEAR)]))
    dwu2 = dwu2.reshape(4, 2, fb, D)
    rms_outs = [f32_tile, bf16_tile, f32_vec]
    (dx2, dx2b, d_g2), (l_wg2_far, l_wu2_near) = _mm_mstream(
        "ffn2b_dh", [dgate2, dup2], [g2, u2], [0, 0], "nn", [(x2, "tile"), (ffn2_norm_g, "vec"), (dx3, "tile")],
        rms_outs, _rms_bwd_epilogue(1.0), comm=_ReduceScatter([(dwg2, FAR), (dwu2, NEAR)]))

    (dymix,), _ = _mm_mstream("mixb_dy", [dx2b], [w_out_full], [0], "nt", [], [f32_tile], _plain_epilogue)
    dwout, _ = _mm_tn("mixb_dwout", ymix, dx2b, D, D, BF16)
    dwout = dwout.reshape(4, 2, D // N_DEV, D)
    (dproj, d_ret, d_gla, d_ba, d_wa2p), (l_wu2_far, l_wout) = _attn_bwd(
        proj, cos, sin_signed, lg, wa2p, gla_b_a, ret_norm_g, gla_norm_g, o, dymix, states,
        comm=_ReduceScatter([(dwu2, FAR), (dwout, ALL)]))
    dwin_t, _ = _mm_tn("mixb_dwin", dproj, h2, 640, D, BF16)
    dwin = dwin_t[:N_DEV * ib].reshape(4, 2, ib, D)
    (dx1, dy1b, d_gmix), (l_win_near,) = _mm_mstream(
        "mixb_dh", [dproj], [w_in_t], [0], "nn", [(x1, "tile"), (mix_norm_g, "vec"), (dx2, "tile")],
        rms_outs, _rms_bwd_epilogue(0.5), comm=_ReduceScatter([(dwin, NEAR)]))

    dwd1, (l_win_far,) = _mm_tn("ffn1b_dwd", act1, dy1b, F // 2, D, BF16, comm=_ReduceScatter([(dwin, FAR)]))
    dwd1 = dwd1.reshape(4, 2, fb, D)
    (dgate1, dup1), (l_wd1_near,) = _mm_nstream("ffn1b_dact", dy1b, [d1], [0], "nt", [dsu1, sl1], [BF16, BF16],
                                                _dact_epilogue, cn=256, comm=_ReduceScatter([(dwd1, NEAR)]))
    dwg1, (l_wd1_far,) = _mm_tn("ffn1b_dwg", dgate1, h1, F // 2, D, BF16, comm=_ReduceScatter([(dwd1, FAR)]))
    dwg1 = dwg1.reshape(4, 2, fb, D)
    core = owner[1:2]
    dwu1, (s_wg1,) = _mm_tn("ffn1b_dwu", dup1, h1, F // 2, D, BF16, comm=_SiblingExchange([dwg1]))
    dwu1 = dwu1.reshape(4, 2, fb, D)
    p_wg1, (s_wu1,) = _pair_sum("pair_wg1", dwg1, s_wg1, core, comm=_SiblingExchange([dwu1]))
    p_wu1, _ = _pair_sum("pair_wu1", dwu1, s_wu1, core)
    (dh1,), (c_wg1, c_wu1) = _mm_mstream(
        "ffn1b_dh", [dgate1, dup1], [g1, u1], [0, 0], "nn", [], [f32_tile], _plain_epilogue,
        comm=_ChipExchange([p_wg1, p_wu1]))
    dx0, _, d_g1 = _rms_bwd("ffn1b_rms", x0, ffn1_norm_g, dh1, dx1, 1.0)

    small = dict(ffn1=d_g1, mix=d_gmix, ffn2=d_g2, final=d_final, ret=d_ret, gla=d_gla, b_a=d_ba)
    flat = jnp.concatenate([small[k].reshape(-1) for k in SMALL_ORDER]
                           + [d_wa2p[:GATE_RANK].reshape(-1), loss[0]])
    rows = -(-flat.shape[0] // 128)
    rows = -(-rows // 8) * 8
    packed = jnp.pad(flat, (0, rows * 128 - flat.shape[0])).reshape(rows, 128)

    transposed = ("ffn1_w_gate", "ffn1_w_up", "ffn2_w_gate", "ffn2_w_up", "w_in")

    def to_2d(nm, a):
        if nm in transposed:
            return a[0].T
        return a.reshape((1, a.shape[0]) if a.ndim == 1 else a.shape[-2:])

    def from_2d(nm, a):
        return a.T[None] if nm in transposed else a.reshape(params[nm][0].shape)

    sums_a, (gathered,) = _owner_sums(
        "sum_a", [(dwg2, [l_wg2_near, l_wg2_far]), (dwu2, [l_wu2_near, l_wu2_far]), (dwd2, [l_wd2_near, l_wd2_far]),
                  (dwin, [l_win_near, l_win_far]), (dwout, [l_wout])], owner,
        comm=_AllGather([packed], ["plain"]))
    sums_b, _ = _owner_sums("sum_b", [(p_wg1, [c_wg1]), (p_wu1, [c_wu1]), (dwd1, [l_wd1_near, l_wd1_far])], owner)
    big_grads = {"ffn2_w_gate": sums_a[0], "ffn2_w_up": sums_a[1], "ffn2_w_down": sums_a[2], "w_out": sums_a[4],
                 "ffn1_w_gate": sums_b[0], "ffn1_w_up": sums_b[1], "ffn1_w_down": sums_b[2]}
    params = dict(
        ffn2_w_gate=(ffn2_w_gate, m_ffn2_w_gate, v_ffn2_w_gate), ffn2_w_up=(ffn2_w_up, m_ffn2_w_up, v_ffn2_w_up),
        ffn2_w_down=(ffn2_w_down, m_ffn2_w_down, v_ffn2_w_down), w_in=(w_in, m_w_in, v_w_in),
        w_out=(w_out, m_w_out, v_w_out), ffn1_w_gate=(ffn1_w_gate, m_ffn1_w_gate, v_ffn1_w_gate),
        ffn1_w_up=(ffn1_w_up, m_ffn1_w_up, v_ffn1_w_up), ffn1_w_down=(ffn1_w_down, m_ffn1_w_down, v_ffn1_w_down),
        ffn1_norm_g=(ffn1_norm_g, m_ffn1_norm_g, v_ffn1_norm_g), mix_norm_g=(mix_norm_g, m_mix_norm_g, v_mix_norm_g),
        ret_norm_g=(ret_norm_g, m_ret_norm_g, v_ret_norm_g), gla_w_a2=(gla_w_a2, m_gla_w_a2, v_gla_w_a2),
        gla_b_a=(gla_b_a, m_gla_b_a, v_gla_b_a), gla_norm_g=(gla_norm_g, m_gla_norm_g, v_gla_norm_g),
        ffn2_norm_g=(ffn2_norm_g, m_ffn2_norm_g, v_ffn2_norm_g), final_norm_g=(final_norm_g, m_final_norm_g, v_final_norm_g))
    grads, updates = {}, {}

    def run_adam(name, names, grad_2d, n_blocks):
        items = [(grad_2d[nm],) + tuple(to_2d(nm, a) for a in params[nm]) for nm in names]
        res, _ = _adamw_group(name, items, n_blocks)
        for nm, r in zip(names, res):
            grads[nm] = from_2d(nm, grad_2d[nm])
            updates[nm] = tuple(from_2d(nm, a) for a in r)

    run_adam("adamw_big", list(big_grads), big_grads, 4)
    run_adam("adamw_w_in", ["w_in"], {"w_in": sums_a[3]}, 1)

    total = _sum_devices("sum_small", gathered.reshape(N_DEV * rows, 128), rows).reshape(-1)
    sizes = [small[k].size for k in SMALL_ORDER] + [GATE_RANK * QK_W, 128]
    offs = [0]
    for s in sizes:
        offs.append(offs[-1] + s)
    pieces = [total[offs[i]:offs[i + 1]] for i in range(len(sizes))]
    g_small = {k: pieces[i].reshape(small[k].shape) for i, k in enumerate(SMALL_ORDER)}
    g_wa2_full = pieces[len(SMALL_ORDER)].reshape(GATE_RANK, QK_W)
    g_wa2 = lax.dynamic_slice(g_wa2_full, (0, dev * ab), (GATE_RANK, ab))
    loss_total = pieces[len(SMALL_ORDER) + 1][0]

    small_grads = {"ffn1_norm_g": g_small["ffn1"], "mix_norm_g": g_small["mix"], "ret_norm_g": g_small["ret"],
                   "gla_w_a2": g_wa2, "gla_b_a": g_small["b_a"], "gla_norm_g": g_small["gla"],
                   "ffn2_norm_g": g_small["ffn2"], "final_norm_g": g_small["final"]}
    run_adam("adamw_small", list(small_grads), small_grads, 1)

    order = ("ffn1_norm_g", "ffn1_w_gate", "ffn1_w_up", "ffn1_w_down", "mix_norm_g", "w_in", "ret_norm_g", "gla_w_a2",
             "gla_b_a", "gla_norm_g", "w_out", "ffn2_norm_g", "ffn2_w_gate", "ffn2_w_up", "ffn2_w_down", "final_norm_g")
    return (loss_total, dx0[None], *[grads[nm] for nm in order], *[updates[nm][0] for nm in order],
            *[updates[nm][1] for nm in order], *[updates[nm][2] for nm in order])
```

```python
import functools
import math

import jax
import jax.numpy as jnp
from jax import lax
from jax.experimental import pallas as pl
from jax.experimental.pallas import tpu as pltpu

F32 = jnp.float32
BF16 = jnp.bfloat16
MESH = pl.DeviceIdType.MESH
HBM = pl.BlockSpec(memory_space=pltpu.HBM)

N_DEV = 8
RMS_EPS = 1e-6
ROPE_BASE = 10000.0
HEADS = 4
DK = 64
DV = 128
QK_W = HEADS * DK
V_W = HEADS * DV
GATE_RANK = 16
GATE_NORM = 16.0
CHUNK = 64
SUPER = 256
PROJ_W = 3200
C_RQ, C_RK, C_RV, C_RG, C_GQ, C_GK, C_GV, C_GG, C_GL = 0, 256, 512, 1024, 1536, 1792, 2048, 2560, 3072
GL_W = PROJ_W - C_GL
ADAM_LR, ADAM_B1, ADAM_B2, ADAM_EPS, ADAM_WD, ADAM_STEP = 0.001, 0.9, 0.999, 1e-08, 0.01, 10
VMEM_LIMIT_V7X = 52 * 1024 * 1024


def _cparams(**kw):
    return pltpu.CompilerParams(vmem_limit_bytes=VMEM_LIMIT_V7X, **kw)


def _dot(a, b, form, precision=None):
    dims = {"nn": (((1,), (0,)), ((), ())), "nt": (((1,), (1,)), ((), ())), "tn": (((0,), (0,)), ((), ()))}[form]
    return lax.dot_general(a, b, dims, preferred_element_type=F32, precision=precision)


def _sigmoid(x):
    return 1.0 / (1.0 + jnp.exp(-x))


def _coords():
    return lax.axis_index("x"), lax.axis_index("y"), lax.axis_index("c")


class _NoComm:
    inputs, out_shapes, scratch = (), (), ()


class _AllGather:
    def __init__(self, arrays, kinds):
        self.inputs = tuple(arrays)
        self.kinds = tuple(kinds)
        n = len(arrays)
        self.out_shapes = tuple(
            jax.ShapeDtypeStruct((a.shape[0], N_DEV) + a.shape[1:] if k == "stack" else (N_DEV,) + a.shape, a.dtype)
            for a, k in zip(arrays, kinds))
        self.scratch = (pltpu.SemaphoreType.DMA((n, 7)), pltpu.SemaphoreType.DMA((n, 7)),
                        pltpu.SemaphoreType.DMA((n,)))

    def _ctx(self, srcs, outs, sems):
        send_sems, recv_sems, local_sems = sems
        x, y, c = _coords()
        me, sibling = (x, y, c), (x, y, 1 - c)
        chips = [(1 - x, y), (x, 1 - y), (1 - x, 1 - y)]

        def blk(m, dev):
            k = 4 * dev[0] + 2 * dev[1] + dev[2]
            return outs[m].at[:, k] if self.kinds[m] == "stack" else outs[m].at[k]

        def copy(m, s, block, to, src=None):
            return pltpu.make_async_remote_copy(
                src_ref=blk(m, block) if src is None else src, dst_ref=blk(m, block),
                send_sem=send_sems.at[m, s], recv_sem=recv_sems.at[m, s], device_id=to, device_id_type=MESH)

        def mine(m):
            return pltpu.make_async_copy(srcs[m], blk(m, me), local_sems.at[m])

        def first(m):
            return [copy(m, 0, me, sibling, src=srcs[m])] + [
                copy(m, 1 + j, me, (*chip, c), src=srcs[m]) for j, chip in enumerate(chips)]

        return me, sibling, chips, c, copy, mine, first

    def start(self, srcs, outs, sems):
        me, sibling, chips, c, copy, mine, first = self._ctx(srcs, outs, sems)
        for m in range(len(srcs)):
            mine(m).start()
            for cp in first(m):
                cp.start()

    def mid(self, srcs, outs, sems):
        me, sibling, chips, c, copy, mine, first = self._ctx(srcs, outs, sems)
        for j, chip in enumerate(chips):
            for m in range(len(srcs)):
                copy(m, 1 + j, (*chip, c), me).wait_recv()
                copy(m, 4 + j, (*chip, c), sibling).start()

    def finish(self, srcs, outs, sems):
        me, sibling, chips, c, copy, mine, first = self._ctx(srcs, outs, sems)
        for m in range(len(srcs)):
            copy(m, 0, sibling, me).wait_recv()
            for j, chip in enumerate(chips):
                copy(m, 4 + j, (*chip, 1 - c), me).wait_recv()
            for cp in first(m):
                cp.wait_send()
            for j, chip in enumerate(chips):
                copy(m, 4 + j, (*chip, c), sibling).wait_send()
            mine(m).wait()


RELATIONS = ((0, 0, 1), (1, 0, 0), (0, 1, 0), (1, 1, 0), (1, 0, 1), (0, 1, 1), (1, 1, 1))
NEAR = (0, 1, 2, 4, 5)
FAR = (3, 6)
ALL = NEAR + FAR


class _ReduceScatter:
    def __init__(self, parts):
        self.inputs = tuple(g for g, _ in parts)
        self.slots = tuple(s for _, s in parts)
        self.out_shapes = tuple(jax.ShapeDtypeStruct((len(s),) + g.shape[2:], g.dtype) for g, s in parts)
        n_max = max(len(s) for s in self.slots)
        n = len(parts)
        self.scratch = (pltpu.SemaphoreType.DMA((n, n_max)), pltpu.SemaphoreType.DMA((n, n_max)))

    def _copies(self, srcs, outs, sems):
        send_sems, recv_sems = sems
        x, y, c = _coords()
        copies = []
        for m, slots in enumerate(self.slots):
            for i, s in enumerate(slots):
                fx, fy, fc = RELATIONS[s]
                px = 1 - x if fx else x
                py = 1 - y if fy else y
                pc = 1 - c if fc else c
                copies.append(pltpu.make_async_remote_copy(
                    src_ref=srcs[m].at[2 * px + py, pc], dst_ref=outs[m].at[i], send_sem=send_sems.at[m, i],
                    recv_sem=recv_sems.at[m, i], device_id=(px, py, pc), device_id_type=MESH))
        return copies

    def start(self, srcs, outs, sems):
        for cp in self._copies(srcs, outs, sems):
            cp.start()

    def mid(self, srcs, outs, sems):
        pass

    def finish(self, srcs, outs, sems):
        for cp in self._copies(srcs, outs, sems):
            cp.wait()


class _SiblingExchange:
    def __init__(self, grads):
        self.inputs = tuple(grads)
        self.out_shapes = tuple(jax.ShapeDtypeStruct((4,) + g.shape[2:], g.dtype) for g in grads)
        self.scratch = (pltpu.SemaphoreType.DMA((len(grads),)), pltpu.SemaphoreType.DMA((len(grads),)))

    def _copies(self, srcs, outs, sems):
        send_sems, recv_sems = sems
        x, y, c = _coords()
        return [pltpu.make_async_remote_copy(
            src_ref=srcs[m].at[:, 1 - c], dst_ref=outs[m], send_sem=send_sems.at[m], recv_sem=recv_sems.at[m],
            device_id=(x, y, 1 - c), device_id_type=MESH) for m in range(len(srcs))]

    def start(self, srcs, outs, sems):
        for cp in self._copies(srcs, outs, sems):
            cp.start()

    def mid(self, srcs, outs, sems):
        pass

    def finish(self, srcs, outs, sems):
        for cp in self._copies(srcs, outs, sems):
            cp.wait()


class _ChipExchange:
    def __init__(self, partials):
        self.inputs = tuple(partials)
        self.out_shapes = tuple(jax.ShapeDtypeStruct((3,) + p.shape[1:], p.dtype) for p in partials)
        n = len(partials)
        self.scratch = (pltpu.SemaphoreType.DMA((n, 3)), pltpu.SemaphoreType.DMA((n, 3)))

    def _copies(self, srcs, outs, sems):
        send_sems, recv_sems = sems
        x, y, c = _coords()
        copies = []
        for m in range(len(srcs)):
            for j, (px, py) in enumerate([(1 - x, y), (x, 1 - y), (1 - x, 1 - y)]):
                copies.append(pltpu.make_async_remote_copy(
                    src_ref=srcs[m].at[2 * px + py], dst_ref=outs[m].at[j], send_sem=send_sems.at[m, j],
                    recv_sem=recv_sems.at[m, j], device_id=(px, py, c), device_id_type=MESH))
        return copies

    def start(self, srcs, outs, sems):
        for cp in self._copies(srcs, outs, sems):
            cp.start()

    def mid(self, srcs, outs, sems):
        pass

    def finish(self, srcs, outs, sems):
        for cp in self._copies(srcs, outs, sems):
            cp.wait()


class _Multi:
    def __init__(self, comms):
        self.comms = comms
        self.inputs = tuple(a for c in comms for a in c.inputs)
        self.out_shapes = tuple(s for c in comms for s in c.out_shapes)
        self.scratch = tuple(s for c in comms for s in c.scratch)

    def _each(self, phase, srcs, outs, sems):
        i = o = k = 0
        for c in self.comms:
            ni, no, nk = len(c.inputs), len(c.out_shapes), len(c.scratch)
            getattr(c, phase)(srcs[i:i + ni], outs[o:o + no], sems[k:k + nk])
            i, o, k = i + ni, o + no, k + nk

    def start(self, srcs, outs, sems):
        self._each("start", srcs, outs, sems)

    def mid(self, srcs, outs, sems):
        self._each("mid", srcs, outs, sems)

    def finish(self, srcs, outs, sems):
        self._each("finish", srcs, outs, sems)


def _call(name, main, grid, in_specs, out_specs, out_shape, args, scratch=(), comm=None, prefetch=None):
    comm = comm or _NoComm()
    counts = [len(in_specs), len(comm.inputs), len(out_shape), len(comm.out_shapes), len(scratch), len(comm.scratch)]
    n_steps = math.prod(grid)
    hosted = bool(comm.inputs)

    def body(*refs):
        if prefetch is not None:
            refs = refs[1:]
        parts, at = [], 0
        for n in counts:
            parts.append(refs[at:at + n])
            at += n
        ins, c_in, outs, c_out, scr, c_scr = parts
        step = pl.program_id(0)
        for d in range(1, len(grid)):
            step = step * grid[d] + pl.program_id(d)
        if hosted:
            @pl.when(step == 0)
            def _():
                comm.start(c_in, c_out, c_scr)
        main(ins, outs, scr)
        if hosted:
            @pl.when(step == max(n_steps - 2, 0))
            def _():
                comm.mid(c_in, c_out, c_scr)

            @pl.when(step == n_steps - 1)
            def _():
                comm.finish(c_in, c_out, c_scr)

    all_in = list(in_specs) + [HBM] * counts[1]
    all_out = list(out_specs) + [HBM] * counts[3]
    all_scratch = list(scratch) + list(comm.scratch)
    shapes = list(out_shape) + list(comm.out_shapes)
    if prefetch is None:
        res = pl.pallas_call(body, name=name, grid=grid, in_specs=all_in, out_specs=all_out, out_shape=shapes,
                             scratch_shapes=all_scratch, compiler_params=_cparams())(*args, *comm.inputs)
    else:
        res = pl.pallas_call(
            body, name=name, out_shape=shapes,
            grid_spec=pltpu.PrefetchScalarGridSpec(num_scalar_prefetch=1, grid=grid, in_specs=all_in,
                                                   out_specs=all_out, scratch_shapes=all_scratch),
            compiler_params=_cparams())(prefetch, *args, *comm.inputs)
    return res[:counts[2]], res[counts[2]:]


def _rms_fwd(name, x, g, comm=None):
    T, D = x.shape
    tm = min(T, 512)

    def main(ins, outs, scr):
        x_ref, g_ref = ins
        xv = x_ref[...]
        r = lax.rsqrt(jnp.mean(xv * xv, axis=-1, keepdims=True) + RMS_EPS)
        outs[0][...] = (xv * r * g_ref[...]).astype(outs[0].dtype)

    tile = pl.BlockSpec((tm, D), lambda i: (i, 0))
    (h,), extra = _call(name, main, (T // tm,), [tile, pl.BlockSpec((1, D), lambda i: (0, 0))], [tile],
                        [jax.ShapeDtypeStruct((T, D), BF16)], (x, g), comm=comm)
    return h, extra


def _final_loss_epilogue(scale, out_scale):
    def ep(acc, ex, outs):
        res_ref, g_ref, t_ref = ex
        dx_ref, dxb_ref, dg_ref, loss_ref = outs
        n = acc.shape[-1]
        xv = res_ref[...] + scale * acc
        r = lax.rsqrt(jnp.mean(xv * xv, axis=-1, keepdims=True) + RMS_EPS)
        xhat = xv * r
        err = xhat * g_ref[...] - t_ref[...]

        @pl.when(pl.program_id(0) == 0)
        def _():
            dg_ref[...] = jnp.zeros_like(dg_ref)
            loss_ref[...] = jnp.zeros_like(loss_ref)

        loss_ref[...] += jnp.broadcast_to(jnp.sum(err * err) * (0.5 / n), loss_ref.shape)
        dy = err * (1.0 / n)
        dg_ref[...] += jnp.sum(dy * xhat, axis=0, keepdims=True)
        dxhat = dy * g_ref[...]
        dx = r * (dxhat - xhat * jnp.mean(dxhat * xhat, axis=-1, keepdims=True))
        dx_ref[...] = dx
        dxb_ref[...] = (out_scale * dx).astype(dxb_ref.dtype)
    return ep


def _mm_nstream(name, a, ws, w_sel, w_form, comps, out_dtypes, epilogue, cn, rows=1024, comm=None):
    T, K = a.shape
    N = ws[0].shape[1]
    rows = min(rows, T)
    assert N % cn == 0 and T % rows == 0
    n_w, n_c = len(ws), len(comps)

    def main(ins, outs, scr):
        a_ref = ins[0]
        w_refs = ins[1:1 + n_w]
        c_refs = ins[1 + n_w:]

        for r in range(T // rows):
            sl = slice(r * rows, (r + 1) * rows)
            a_blk = a_ref[sl, :]
            dots = [_dot(a_blk, w_ref[...], w_form) for w_ref in w_refs]
            res = epilogue(dots, [c_ref[sl, :] for c_ref in c_refs])
            for o_ref, o in zip(outs, res):
                o_ref[sl, :] = o.astype(o_ref.dtype)

    if w_form == "nt":
        w_specs = [pl.BlockSpec((None, cn, K), functools.partial(lambda j, s: (s, j, 0), s=s)) for s in w_sel]
    else:
        w_specs = [pl.BlockSpec((K, cn), lambda j: (0, j)) for _ in ws]
    chunk = pl.BlockSpec((T, cn), lambda j: (0, j))
    return _call(name, main, (N // cn,), [pl.BlockSpec((T, K), lambda j: (0, 0))] + w_specs + [chunk] * n_c,
                 [chunk] * len(out_dtypes), [jax.ShapeDtypeStruct((T, N), dt) for dt in out_dtypes],
                 (a, *ws, *comps), comm=comm)


def _mm_mstream(name, as_, ws, w_sel, w_form, extras, outs_desc, epilogue, tm=512, comm=None):
    T = as_[0].shape[0]
    tm = min(tm, T)
    n_a = len(as_)
    w_shapes = [w.shape[-2:] for w in ws]
    N = w_shapes[0][1] if w_form == "nn" else w_shapes[0][0]

    def main(ins, outs, scr):
        a_refs = ins[:n_a]
        w_refs = ins[n_a:2 * n_a]
        acc = None
        for a_ref, w_ref in zip(a_refs, w_refs):
            d = _dot(a_ref[...], w_ref[...], w_form)
            acc = d if acc is None else acc + d
        epilogue(acc, ins[2 * n_a:], outs)

    kind_spec = {"tile": pl.BlockSpec((tm, N), lambda i: (i, 0)), "vec": pl.BlockSpec((1, N), lambda i: (0, 0))}
    kind_shape = {"tile": (T, N), "vec": (1, N)}
    a_specs = [pl.BlockSpec((tm, a.shape[1]), lambda i: (i, 0)) for a in as_]
    w_specs = []
    for w, s in zip(ws, w_sel):
        if w.ndim == 3:
            w_specs.append(pl.BlockSpec((None,) + tuple(w.shape[1:]), functools.partial(lambda i, s: (s, 0, 0), s=s),
                                        pipeline_mode=pl.Buffered(1)))
        else:
            w_specs.append(pl.BlockSpec(tuple(w.shape), lambda i: (0, 0), pipeline_mode=pl.Buffered(1)))
    args = list(as_) + list(ws) + [e for e, _ in extras]
    return _call(name, main, (T // tm,), a_specs + w_specs + [kind_spec[k] for _, k in extras],
                 [kind_spec[k] for _, k in outs_desc],
                 [jax.ShapeDtypeStruct(kind_shape[k], dt) for dt, k in outs_desc], args, comm=comm)


def _plain_epilogue(acc, ex, outs):
    outs[0][...] = acc.astype(outs[0].dtype)


def _residual_rms_epilogue(scale):
    def ep(acc, ex, outs):
        xv = ex[0][...] + scale * acc
        outs[0][...] = xv
        r = lax.rsqrt(jnp.mean(xv * xv, axis=-1, keepdims=True) + RMS_EPS)
        outs[1][...] = (xv * r * ex[1][...]).astype(outs[1].dtype)
    return ep


def _rms_bwd_epilogue(out_scale):
    def ep(acc, ex, outs):
        x_ref, g_ref, dres_ref = ex
        dx_ref, dxb_ref, dg_ref = outs
        xv = x_ref[...]
        r = lax.rsqrt(jnp.mean(xv * xv, axis=-1, keepdims=True) + RMS_EPS)
        xhat = xv * r

        @pl.when(pl.program_id(0) == 0)
        def _():
            dg_ref[...] = jnp.zeros_like(dg_ref)

        dg_ref[...] += jnp.sum(acc * xhat, axis=0, keepdims=True)
        dxhat = acc * g_ref[...]
        dx = r * (dxhat - xhat * jnp.mean(dxhat * xhat, axis=-1, keepdims=True)) + dres_ref[...]
        dx_ref[...] = dx
        dxb_ref[...] = (out_scale * dx).astype(dxb_ref.dtype)
    return ep


def _mm_tn(name, a, b, tmo, tno, out_dtype, tk=1024, comm=None):
    T, Ma = a.shape
    Nb = b.shape[1]
    tk = min(tk, T)
    nk = T // tk

    def main(ins, outs, scr):
        a_ref, b_ref = ins
        (acc_ref,) = scr
        k = pl.program_id(2)

        @pl.when(k == 0)
        def _():
            acc_ref[...] = jnp.zeros_like(acc_ref)

        acc_ref[...] += _dot(a_ref[...], b_ref[...], "tn")

        @pl.when(k == nk - 1)
        def _():
            outs[0][...] = acc_ref[...].astype(outs[0].dtype)

    (out,), extra = _call(
        name, main, (Ma // tmo, Nb // tno, nk),
        [pl.BlockSpec((tk, tmo), lambda i, j, k: (k, i)), pl.BlockSpec((tk, tno), lambda i, j, k: (k, j))],
        [pl.BlockSpec((tmo, tno), lambda i, j, k: (i, j))], [jax.ShapeDtypeStruct((Ma, Nb), out_dtype)],
        (a, b), scratch=[pltpu.VMEM((tmo, tno), F32)], comm=comm)
    return out, extra


def _swiglu_parts(g, u):
    s = _sigmoid(g)
    silu = g * s
    return [u * (s + silu * (1.0 - s)), silu, silu * u]


def _silu_mul_epilogue(dots, comps):
    g, u = dots
    return _swiglu_parts(g, u)


def _up_act_epilogue(dots, comps):
    (u,) = dots
    return _swiglu_parts(comps[0].astype(F32), u)


def _dact_epilogue(dots, comps):
    dact = dots[0].astype(BF16)
    return [dact * comps[0], dact * comps[1]]


def _identity_epilogue(dots, comps):
    return list(dots)


def _swap_halves(x):
    lane = lax.broadcasted_iota(jnp.int32, x.shape, 1)
    first = (lane % DK) < (DK // 2)
    return jnp.where(first, pltpu.roll(x, 128 - DK // 2, 1), pltpu.roll(x, DK // 2, 1))


def _rotary(t, cos, sin_signed):
    halves = []
    for p in range(QK_W // 128):
        th = t[:, 128 * p:128 * (p + 1)]
        halves.append(th * cos + _swap_halves(th) * sin_signed)
    return jnp.concatenate(halves, axis=1)


def _rotary_transposed(d, cos, sin_signed):
    halves = []
    for p in range(QK_W // 128):
        dh = d[:, 128 * p:128 * (p + 1)]
        halves.append(dh * cos + _swap_halves(dh * sin_signed))
    return jnp.concatenate(halves, axis=1)


def _log_sigmoid(x):
    return jnp.minimum(x, 0.0) - jnp.log(1.0 + jnp.exp(-jnp.abs(x)))


def _attn_masks():
    row = lax.broadcasted_iota(jnp.int32, (SUPER, SUPER), 0)
    col = lax.broadcasted_iota(jnp.int32, (SUPER, SUPER), 1)
    same = (row // CHUNK) == (col // CHUNK)
    return row, col, same


def _group_inputs(grp, pr, cos, sin_signed, lg, wa2, ba):
    if grp == 0:
        q = _rotary(pr[:, C_RQ:C_RQ + QK_W], cos, sin_signed)
        k = _rotary(pr[:, C_RK:C_RK + QK_W], cos, sin_signed) * (DK ** -0.5)
        v = pr[:, C_RV:C_RV + V_W]
        gate = pr[:, C_RG:C_RG + V_W]
        pos = lax.broadcasted_iota(jnp.int32, (SUPER, QK_W), 0).astype(F32) + 1.0
        return q, k, v, gate, pos * lg, None, None
    q = pr[:, C_GQ:C_GQ + QK_W] * (DK ** -0.5)
    k = pr[:, C_GK:C_GK + QK_W]
    v = pr[:, C_GV:C_GV + V_W]
    gate = pr[:, C_GG:C_GG + V_W]
    glow = pr[:, C_GL:C_GL + GL_W]
    logit = _dot(glow.astype(BF16), wa2.astype(BF16), "nn") + ba
    la = _log_sigmoid(logit) * (1.0 / GATE_NORM)
    row, col, _ = _attn_masks()
    lower = (col <= row).astype(F32)
    b_cum = _dot(lower, la, "nn", precision=lax.Precision.HIGHEST)
    return q, k, v, gate, b_cum, glow, logit


def _decay_factors(q, k, b_cum):
    c = b_cum[SUPER // 2 - 1:SUPER // 2, :]
    bl = b_cum[SUPER - 1:SUPER, :]
    e1 = jnp.exp(b_cum - c)
    e2 = jnp.exp(c - b_cum)
    e_b = jnp.exp(b_cum)
    e_l = jnp.exp(bl - b_cum)
    return dict(e1=e1, e2=e2, eb=e_b, el=e_l, ebl=jnp.exp(bl),
                qp=q * e1, qm=q * e2, kp=k * e1, km=k * e2, qs=q * e_b, kl=k * e_l)


def _state_block_mask():
    r = lax.broadcasted_iota(jnp.int32, (V_W, QK_W), 0)
    c = lax.broadcasted_iota(jnp.int32, (V_W, QK_W), 1)
    return (r // DV) == (c // DK)


def _attn_fwd(proj, cos, sin_signed, lg, wa2p, ba, gn_ret, gn_gla, comm=None):
    T = proj.shape[0]
    n_s = T // SUPER

    def main(ins, outs, scr):
        pr_ref, cos_ref, sin_ref, lg_ref, wa2_ref, ba_ref, gr_ref, gg_ref = ins
        o_ref, y_ref, st_ref = outs
        (s_ref,) = scr
        i = pl.program_id(0)

        @pl.when(i == 0)
        def _():
            s_ref[...] = jnp.zeros_like(s_ref)

        pr = pr_ref
        row, col, same = _attn_masks()
        m1 = col <= row
        m2 = jnp.logical_and(col > row, same)
        lane = lax.broadcasted_iota(jnp.int32, (1, QK_W), 1)
        blockmask = _state_block_mask()
        for grp in range(2):
            q, k, v, gate, b_cum, _, _ = _group_inputs(grp, pr, cos_ref[...], sin_ref[...], lg_ref[...],
                                                      wa2_ref[...], ba_ref[...])
            f = _decay_factors(q, k, b_cum)
            gn = gr_ref[...] if grp == 0 else gg_ref[...]
            s_prev = s_ref[grp]
            st_ref[0, grp] = s_prev
            o_inter = _dot(f["qs"].astype(BF16), s_prev.astype(BF16), "nt")
            kmb = f["km"].astype(BF16)
            kpb = f["kp"].astype(BF16)
            vb = v.astype(BF16)
            for h in range(HEADS):
                hm = (lane // DK) == h
                a1 = _dot(jnp.where(hm, f["qp"], 0.0).astype(BF16), kmb, "nt")
                a2 = _dot(jnp.where(hm, f["qm"], 0.0).astype(BF16), kpb, "nt")
                a = jnp.where(m1, a1, jnp.where(m2, a2, 0.0))
                lo = grp * V_W + h * DV
                o_h = _dot(a.astype(BF16), vb[:, h * DV:(h + 1) * DV], "nn") + o_inter[:, h * DV:(h + 1) * DV]
                o_ref[:, lo:lo + DV] = o_h
                r = lax.rsqrt(jnp.mean(o_h * o_h, axis=-1, keepdims=True) + RMS_EPS)
                gte = gate[:, h * DV:(h + 1) * DV]
                y = o_h * r * gn[:, h * DV:(h + 1) * DV] * (gte * _sigmoid(gte))
                y_ref[:, lo:lo + DV] = y.astype(y_ref.dtype)
            upd = _dot(vb, f["kl"].astype(BF16), "tn")
            s_ref[grp] = s_prev * f["ebl"] + jnp.where(blockmask, upd, 0.0)

    const = lambda shape: pl.BlockSpec(shape, lambda i: tuple(0 for _ in shape))
    return _call(
        "attn_fwd", main, (n_s,),
        [pl.BlockSpec((SUPER, PROJ_W), lambda i: (i, 0)),
         pl.BlockSpec((SUPER, 128), lambda i: (i, 0)), pl.BlockSpec((SUPER, 128), lambda i: (i, 0)),
         const((1, QK_W)), const((GL_W, QK_W)), const((1, QK_W)), const((1, V_W)), const((1, V_W))],
        [pl.BlockSpec((SUPER, 2 * V_W), lambda i: (i, 0)), pl.BlockSpec((SUPER, 2 * V_W), lambda i: (i, 0)),
         pl.BlockSpec((1, 2, V_W, QK_W), lambda i: (i, 0, 0, 0))],
        [jax.ShapeDtypeStruct((T, 2 * V_W), F32), jax.ShapeDtypeStruct((T, 2 * V_W), BF16),
         jax.ShapeDtypeStruct((n_s, 2, V_W, QK_W), F32)],
        (proj, cos, sin_signed, lg, wa2p, ba, gn_ret, gn_gla),
        scratch=[pltpu.VMEM((2, V_W, QK_W), F32)], comm=comm)


def _attn_bwd(proj, cos, sin_signed, lg, wa2p, ba, gn_ret, gn_gla, o, dy, states, comm=None):
    T = proj.shape[0]
    n_s = T // SUPER

    def main(ins, outs, scr):
        pr_ref, cos_ref, sin_ref, lg_ref, wa2_ref, ba_ref, gr_ref, gg_ref, o_ref, dy_ref, st_ref = ins
        dp_ref, dgr_ref, dgg_ref, dba_ref, dwa_ref = outs
        (ds_ref,) = scr
        i = pl.program_id(0)

        @pl.when(i == 0)
        def _():
            ds_ref[...] = jnp.zeros_like(ds_ref)
            dgr_ref[...] = jnp.zeros_like(dgr_ref)
            dgg_ref[...] = jnp.zeros_like(dgg_ref)
            dba_ref[...] = jnp.zeros_like(dba_ref)
            dwa_ref[...] = jnp.zeros_like(dwa_ref)

        pr = pr_ref
        cos = cos_ref[...]
        sin_signed = sin_ref[...]
        row, col, same = _attn_masks()
        m1 = col <= row
        m2 = jnp.logical_and(col > row, same)
        m1t = row <= col
        m2t = jnp.logical_and(row > col, same)
        lane = lax.broadcasted_iota(jnp.int32, (1, QK_W), 1)
        blockmask = _state_block_mask()
        for grp in range(2):
            q, k, v, gate, b_cum, glow, logit = _group_inputs(grp, pr, cos, sin_signed, lg_ref[...],
                                                              wa2_ref[...], ba_ref[...])
            f = _decay_factors(q, k, b_cum)
            gn = gr_ref[...] if grp == 0 else gg_ref[...]
            dgn_ref = dgr_ref if grp == 0 else dgg_ref
            do_parts, dgate_parts, dgn_parts = [], [], []
            for h in range(HEADS):
                lo = grp * V_W + h * DV
                o_h = o_ref[:, lo:lo + DV]
                r = lax.rsqrt(jnp.mean(o_h * o_h, axis=-1, keepdims=True) + RMS_EPS)
                n = o_h * r
                gte = gate[:, h * DV:(h + 1) * DV]
                sg = _sigmoid(gte)
                dy_h = dy_ref[:, lo:lo + DV]
                gn_h = gn[:, h * DV:(h + 1) * DV]
                dgate_parts.append(dy_h * n * gn_h * (sg * (1.0 + gte * (1.0 - sg))))
                dz = dy_h * (gte * sg)
                dgn_parts.append(jnp.sum(dz * n, axis=0, keepdims=True))
                dn = dz * gn_h
                do_parts.append(r * (dn - n * jnp.mean(dn * n, axis=-1, keepdims=True)))
            dgn_ref[...] += jnp.concatenate(dgn_parts, axis=1)
            dgate = jnp.concatenate(dgate_parts, axis=1)
            do = jnp.concatenate(do_parts, axis=1)
            dob = do.astype(BF16)
            vb = v.astype(BF16)
            s_prev = st_ref[0, grp]
            ds_new = ds_ref[grp]
            dsb = ds_new.astype(BF16)
            qpb, qmb = f["qp"].astype(BF16), f["qm"].astype(BF16)
            kpb, kmb = f["kp"].astype(BF16), f["km"].astype(BF16)
            dqp = jnp.zeros((SUPER, QK_W), F32)
            dqm = jnp.zeros((SUPER, QK_W), F32)
            dkp = jnp.zeros((SUPER, QK_W), F32)
            dkm = jnp.zeros((SUPER, QK_W), F32)
            dv_parts = []
            for h in range(HEADS):
                hm = (lane // DK) == h
                qp_h = jnp.where(hm, f["qp"], 0.0).astype(BF16)
                qm_h = jnp.where(hm, f["qm"], 0.0).astype(BF16)
                kp_h = jnp.where(hm, f["kp"], 0.0).astype(BF16)
                km_h = jnp.where(hm, f["km"], 0.0).astype(BF16)
                at = jnp.where(m1t, _dot(km_h, qpb, "nt"), jnp.where(m2t, _dot(kp_h, qmb, "nt"), 0.0))
                do_h = dob[:, h * DV:(h + 1) * DV]
                v_h = vb[:, h * DV:(h + 1) * DV]
                dv_parts.append(_dot(at.astype(BF16), do_h, "nn"))
                da = _dot(do_h, v_h, "nt")
                dat = _dot(v_h, do_h, "nt")
                da1 = jnp.where(m1, da, 0.0).astype(BF16)
                da2 = jnp.where(m2, da, 0.0).astype(BF16)
                da1t = jnp.where(m1t, dat, 0.0).astype(BF16)
                da2t = jnp.where(m2t, dat, 0.0).astype(BF16)
                dqp = dqp + _dot(da1, km_h, "nn")
                dqm = dqm + _dot(da2, kp_h, "nn")
                dkm = dkm + _dot(da1t, qp_h, "nn")
                dkp = dkp + _dot(da2t, qm_h, "nn")
            klb = f["kl"].astype(BF16)
            qsb = f["qs"].astype(BF16)
            dqs = _dot(dob, s_prev.astype(BF16), "nn")
            dkl = _dot(vb, dsb, "nn")
            dv = jnp.concatenate(dv_parts, axis=1) + _dot(klb, dsb, "nt")
            ds_ref[grp] = ds_new * f["ebl"] + jnp.where(blockmask, _dot(dob, qsb, "tn"), 0.0)
            dq = dqp * f["e1"] + dqm * f["e2"] + dqs * f["eb"]
            dk = dkm * f["e2"] + dkp * f["e1"] + dkl * f["el"]
            if grp == 0:
                dq = _rotary_transposed(dq, cos, sin_signed)
                dk = _rotary_transposed(dk * (DK ** -0.5), cos, sin_signed)
                dp_ref[:, C_RQ:C_RQ + QK_W] = dq.astype(dp_ref.dtype)
                dp_ref[:, C_RK:C_RK + QK_W] = dk.astype(dp_ref.dtype)
                dp_ref[:, C_RV:C_RV + V_W] = dv.astype(dp_ref.dtype)
                dp_ref[:, C_RG:C_RG + V_W] = dgate.astype(dp_ref.dtype)
            else:
                dkl_kl = dkl * klb.astype(F32)
                db = (dqp * qpb.astype(F32) - dkm * kmb.astype(F32) - dqm * qmb.astype(F32)
                      + dkp * kpb.astype(F32) + dqs * qsb.astype(F32) - dkl_kl)
                last = (jnp.sum(dkl_kl, axis=0, keepdims=True)
                        + f["ebl"] * jnp.sum(s_prev * ds_new, axis=0, keepdims=True))
                rowq = lax.broadcasted_iota(jnp.int32, (SUPER, QK_W), 0)
                db = db + jnp.where(rowq == SUPER - 1, last, 0.0)
                upper = (col >= row).astype(F32)
                dla = _dot(upper, db, "nn", precision=lax.Precision.HIGHEST)
                dlogit = dla * (1.0 / GATE_NORM) * (1.0 - _sigmoid(logit))
                dlb = dlogit.astype(BF16)
                dglow = _dot(dlb, wa2_ref[...].astype(BF16), "nt")
                dwa_ref[...] += _dot(glow.astype(BF16), dlb, "tn")
                dba_ref[...] += jnp.sum(dlogit, axis=0, keepdims=True)
                dp_ref[:, C_GQ:C_GQ + QK_W] = (dq * (DK ** -0.5)).astype(dp_ref.dtype)
                dp_ref[:, C_GK:C_GK + QK_W] = dk.astype(dp_ref.dtype)
                dp_ref[:, C_GV:C_GV + V_W] = dv.astype(dp_ref.dtype)
                dp_ref[:, C_GG:C_GG + V_W] = dgate.astype(dp_ref.dtype)
                dp_ref[:, C_GL:C_GL + GL_W] = dglow.astype(dp_ref.dtype)

    rev = lambda i: n_s - 1 - i
    const = lambda shape: pl.BlockSpec(shape, lambda i: tuple(0 for _ in shape))
    return _call(
        "attn_bwd", main, (n_s,),
        [pl.BlockSpec((SUPER, PROJ_W), lambda i: (rev(i), 0)),
         pl.BlockSpec((SUPER, 128), lambda i: (rev(i), 0)), pl.BlockSpec((SUPER, 128), lambda i: (rev(i), 0)),
         const((1, QK_W)), const((GL_W, QK_W)), const((1, QK_W)), const((1, V_W)), const((1, V_W)),
         pl.BlockSpec((SUPER, 2 * V_W), lambda i: (rev(i), 0)),
         pl.BlockSpec((SUPER, 2 * V_W), lambda i: (rev(i), 0)),
         pl.BlockSpec((1, 2, V_W, QK_W), lambda i: (rev(i), 0, 0, 0))],
        [pl.BlockSpec((SUPER, PROJ_W), lambda i: (rev(i), 0)),
         const((1, V_W)), const((1, V_W)), const((1, QK_W)), const((GL_W, QK_W))],
        [jax.ShapeDtypeStruct((T, PROJ_W), BF16),
         jax.ShapeDtypeStruct((1, V_W), F32), jax.ShapeDtypeStruct((1, V_W), F32),
         jax.ShapeDtypeStruct((1, QK_W), F32), jax.ShapeDtypeStruct((GL_W, QK_W), F32)],
        (proj, cos, sin_signed, lg, wa2p, ba, gn_ret, gn_gla, o, dy, states),
        scratch=[pltpu.VMEM((2, V_W, QK_W), F32)], comm=comm)


def _rotary_tables(T):
    half = DK // 2
    inv = ROPE_BASE ** (-jnp.arange(half, dtype=F32) * 2.0 / DK)
    ang = jnp.arange(T, dtype=F32)[:, None] * inv[None, :]
    cos, sin = jnp.cos(ang), jnp.sin(ang)
    cos_head = jnp.concatenate([cos, cos], axis=1)
    sin_head = jnp.concatenate([-sin, sin], axis=1)
    return jnp.tile(cos_head, (1, 128 // DK)), jnp.tile(sin_head, (1, 128 // DK))


def _sum_devices(name, gathered, m_per):
    def body(g_ref, o_ref):
        acc = g_ref[0:m_per, :]
        for k in range(1, N_DEV):
            acc = acc + g_ref[k * m_per:(k + 1) * m_per, :]
        o_ref[...] = acc

    return pl.pallas_call(body, name=name, out_shape=jax.ShapeDtypeStruct((m_per, 128), F32))(gathered)


def _pair_sum(name, grad, landed, core, comm=None):
    R, C = grad.shape[2:]

    def main(ins, outs, scr):
        outs[0][...] = (ins[0][...].astype(F32) + ins[1][...].astype(F32)).astype(outs[0].dtype)

    blk = pl.BlockSpec((None, R, C), lambda j, s: (j, 0, 0))
    (out,), extra = _call(name, main, (4,), [pl.BlockSpec((None, None, R, C), lambda j, s: (j, s[0], 0, 0)), blk],
                          [blk], [jax.ShapeDtypeStruct((4, R, C), BF16)], (grad, landed), prefetch=core, comm=comm)
    return out, extra


def _owner_sums(name, items, owner, comm=None):
    counts = [1 + len(landed) for _, landed in items]

    def main(ins, outs, scr):
        at = 0
        for o_ref, n in zip(outs, counts):
            acc = ins[at][...].astype(F32)
            for l_ref in ins[at + 1:at + n]:
                for j in range(l_ref.shape[0]):
                    acc = acc + l_ref[j].astype(F32)
            o_ref[...] = acc
            at += n

    once = pl.Buffered(1)
    in_specs, out_specs, out_shape, args = [], [], [], []
    for grad, landed in items:
        R, C = grad.shape[-2:]
        if grad.ndim == 4:
            in_specs.append(pl.BlockSpec((None, None, R, C), lambda i, s: (s[0], s[1], 0, 0), pipeline_mode=once))
        else:
            in_specs.append(pl.BlockSpec((None, R, C), lambda i, s: (s[0], 0, 0), pipeline_mode=once))
        in_specs += [pl.BlockSpec(tuple(l.shape), lambda i, s: (0, 0, 0), pipeline_mode=once) for l in landed]
        out_specs.append(pl.BlockSpec((R, C), lambda i, s: (0, 0)))
        out_shape.append(jax.ShapeDtypeStruct((R, C), F32))
        args += [grad, *landed]
    return _call(name, main, (1,), in_specs, out_specs, out_shape, args, comm=comm, prefetch=owner)


def _rms_bwd(name, x, g, dh, dres, out_scale):
    T, D = x.shape
    tm = min(T, 512)
    ep = _rms_bwd_epilogue(out_scale)

    def main(ins, outs, scr):
        x_ref, g_ref, dh_ref, dres_ref = ins
        ep(dh_ref[...], (x_ref, g_ref, dres_ref), outs)

    tile = pl.BlockSpec((tm, D), lambda i: (i, 0))
    vec = pl.BlockSpec((1, D), lambda i: (0, 0))
    outs, _ = _call(name, main, (T // tm,), [tile, vec, tile, tile], [tile, tile, vec],
                    [jax.ShapeDtypeStruct((T, D), F32), jax.ShapeDtypeStruct((T, D), BF16),
                     jax.ShapeDtypeStruct((1, D), F32)], (x, g, dh, dres))
    return outs


def _adamw_group(name, items, n_blocks, comm=None):
    n = len(items)

    def main(ins, outs, scr):
        for p in range(n):
            g_ref, w_ref, m_ref, v_ref = ins[4 * p:4 * p + 4]
            d_ref, nm_ref, nv_ref = outs[3 * p:3 * p + 3]
            gv = g_ref[...]
            nm = ADAM_B1 * m_ref[...] + (1.0 - ADAM_B1) * gv
            nv = ADAM_B2 * v_ref[...] + (1.0 - ADAM_B2) * (gv * gv)
            m_hat = nm / (1.0 - ADAM_B1 ** ADAM_STEP)
            v_hat = nv / (1.0 - ADAM_B2 ** ADAM_STEP)
            d_ref[...] = -ADAM_LR * (m_hat / (jnp.sqrt(v_hat) + ADAM_EPS) + ADAM_WD * w_ref[...])
            nm_ref[...] = nm
            nv_ref[...] = nv

    in_specs, out_specs, out_shape, args = [], [], [], []
    for item in items:
        R, C = item[1].shape
        assert R % n_blocks == 0
        spec = pl.BlockSpec((R // n_blocks, C), lambda i: (i, 0))
        in_specs += [spec] * 4
        out_specs += [spec] * 3
        out_shape += [jax.ShapeDtypeStruct((R, C), F32)] * 3
        args += list(item)
    outs, extra = _call(name, main, (n_blocks,), in_specs, out_specs, out_shape, args, comm=comm)
    return [tuple(outs[3 * p:3 * p + 3]) for p in range(n)], extra


SMALL_ORDER = ("ffn1", "mix", "ffn2", "final", "ret", "gla", "b_a")


def kernel(x, ffn1_norm_g, ffn1_w_gate, ffn1_w_up, ffn1_w_down, mix_norm_g, w_in, ret_norm_g, gla_w_a2, gla_b_a, gla_norm_g, w_out, ffn2_norm_g, ffn2_w_gate, ffn2_w_up, ffn2_w_down, final_norm_g, loss_target, m_ffn1_norm_g, m_ffn1_w_gate, m_ffn1_w_up, m_ffn1_w_down, m_mix_norm_g, m_w_in, m_ret_norm_g, m_gla_w_a2, m_gla_b_a, m_gla_norm_g, m_w_out, m_ffn2_norm_g, m_ffn2_w_gate, m_ffn2_w_up, m_ffn2_w_down, m_final_norm_g, v_ffn1_norm_g, v_ffn1_w_gate, v_ffn1_w_up, v_ffn1_w_down, v_mix_norm_g, v_w_in, v_ret_norm_g, v_gla_w_a2, v_gla_b_a, v_gla_norm_g, v_w_out, v_ffn2_norm_g, v_ffn2_w_gate, v_ffn2_w_up, v_ffn2_w_down, v_final_norm_g):
    xi, yi, ci = _coords()
    dev = 4 * xi + 2 * yi + ci
    owner = jnp.stack([2 * xi + yi, ci]).astype(jnp.int32)

    x0, target = x[0], loss_target[0]
    T, D = x0.shape
    fb = ffn1_w_gate.shape[2]
    ib = w_in.shape[2]
    ab = gla_w_a2.shape[2]
    F = N_DEV * fb
    cos, sin_signed = _rotary_tables(T)
    lg = jnp.repeat(jnp.log(1.0 - 2.0 ** (-5.0 - jnp.arange(HEADS, dtype=F32))), DK)[None, :]
    g_final = final_norm_g.reshape(1, D)

    g1_loc = ffn1_w_gate[0].T[None].astype(BF16)
    u1_loc = ffn1_w_up[0].T[None].astype(BF16)
    d1_loc = ffn1_w_down.astype(BF16)
    g2_loc = ffn2_w_gate[0].T[None].astype(BF16)
    u2_loc = ffn2_w_up[0].T[None].astype(BF16)
    d2_loc = ffn2_w_down.astype(BF16)
    in_loc = w_in[0].T.astype(BF16)
    out_loc = w_out[0].astype(BF16)

    h1, (g1,) = _rms_fwd("ffn1_rms", x0, ffn1_norm_g, comm=_AllGather([g1_loc], ["stack"]))
    g1 = g1.reshape(1, F, D)
    (gate1,), (u1,) = _mm_nstream("ffn1_gate", h1, [g1], [0], "nt", [], [BF16], _identity_epilogue, cn=256,
                                  comm=_AllGather([u1_loc], ["stack"]))
    u1 = u1.reshape(1, F, D)
    (dsu1, sl1, act1), (d1,) = _mm_nstream("ffn1_up", h1, [u1], [0], "nt", [gate1], [BF16, BF16, BF16],
                                           _up_act_epilogue, cn=256, comm=_AllGather([d1_loc], ["stack"]))
    d1 = d1.reshape(1, F, D)
    f32_tile, bf16_tile, f32_vec = (F32, "tile"), (BF16, "tile"), (F32, "vec")
    (x1, h2), (in_all, a_all) = _mm_mstream(
        "ffn1_down", [act1], [d1], [0], "nn", [(x0, "tile"), (mix_norm_g, "vec")], [f32_tile, bf16_tile],
        _residual_rms_epilogue(0.5), comm=_AllGather([in_loc, gla_w_a2[0]], ["plain", "plain"]))
    w_in_t = jnp.pad(in_all.reshape(1, N_DEV * ib, D), ((0, 0), (0, PROJ_W - N_DEV * ib), (0, 0)))
    wa2 = jnp.transpose(a_all, (1, 0, 2)).reshape(GATE_RANK, N_DEV * ab)
    wa2p = jnp.pad(wa2, ((0, GL_W - GATE_RANK), (0, 0)))

    (proj,), (g2,) = _mm_nstream("mix_proj", h2, [w_in_t], [0], "nt", [], [F32], _identity_epilogue, cn=640,
                                 comm=_AllGather([g2_loc], ["stack"]))
    (o, ymix, states), (u2, out_all) = _attn_fwd(proj, cos, sin_signed, lg, wa2p, gla_b_a, ret_norm_g, gla_norm_g,
                                                 comm=_AllGather([u2_loc, out_loc], ["stack", "plain"]))
    w_out_full = out_all.reshape(D, D)
    (x2, h3), _ = _mm_mstream("mix_out", [ymix], [w_out_full], [0], "nn", [(x1, "tile"), (ffn2_norm_g, "vec")],
                              [f32_tile, bf16_tile], _residual_rms_epilogue(1.0))
    g2, u2 = g2.reshape(1, F, D), u2.reshape(1, F, D)

    (dsu2, sl2, act2), (d2,) = _mm_nstream(
        "ffn2_up", h3, [g2, u2], [0, 0], "nt", [], [BF16, BF16, BF16], _silu_mul_epilogue, cn=256,
        comm=_AllGather([d2_loc], ["stack"]))
    d2 = d2.reshape(1, F, D)
    (dx3, dy3b, d_final, loss), _ = _mm_mstream(
        "ffn2_down", [act2], [d2], [0], "nn", [(x2, "tile"), (g_final, "vec"), (target, "tile")],
        [f32_tile, bf16_tile, f32_vec, f32_vec], _final_loss_epilogue(0.5, 0.5))

    dwd2, _ = _mm_tn("ffn2b_dwd", act2, dy3b, F // 2, D, BF16)
    dwd2 = dwd2.reshape(4, 2, fb, D)
    (dgate2, dup2), (l_wd2_near,) = _mm_nstream("ffn2b_dact", dy3b, [d2], [0], "nt", [dsu2, sl2], [BF16, BF16],
                                                _dact_epilogue, cn=256, comm=_ReduceScatter([(dwd2, NEAR)]))
    dwg2, (l_wd2_far,) = _mm_tn("ffn2b_dwg", dgate2, h3, F // 2, D, BF16, comm=_ReduceScatter([(dwd2, FAR)]))
    dwg2 = dwg2.reshape(4, 2, fb, D)
    dwu2, (l_wg2_near,) = _mm_tn("ffn2b_dwu", dup2, h3, F // 2, D, BF16, comm=_ReduceScatter([(dwg2, NEAR)]))
    dwu2 = dwu2.reshape(4, 2, fb, D)
    rms_outs = [f32_tile, bf16_tile, f32_vec]
    (dx2, dx2b, d_g2), (l_wg2_far, l_wu2_near) = _mm_mstream(
        "ffn2b_dh", [dgate2, dup2], [g2, u2], [0, 0], "nn", [(x2, "tile"), (ffn2_norm_g, "vec"), (dx3, "tile")],
        rms_outs, _rms_bwd_epilogue(1.0), comm=_ReduceScatter([(dwg2, FAR), (dwu2, NEAR)]))

    (dymix,), _ = _mm_mstream("mixb_dy", [dx2b], [w_out_full], [0], "nt", [], [f32_tile], _plain_epilogue)
    dwout, _ = _mm_tn("mixb_dwout", ymix, dx2b, D, D, BF16)
    dwout = dwout.reshape(4, 2, D // N_DEV, D)
    (dproj, d_ret, d_gla, d_ba, d_wa2p), (l_wu2_far, l_wout) = _attn_bwd(
        proj, cos, sin_signed, lg, wa2p, gla_b_a, ret_norm_g, gla_norm_g, o, dymix, states,
        comm=_ReduceScatter([(dwu2, FAR), (dwout, ALL)]))
    dwin_t, _ = _mm_tn("mixb_dwin", dproj, h2, 640, D, BF16)
    dwin = dwin_t[:N_DEV * ib].reshape(4, 2, ib, D)
    (dx1, dy1b, d_gmix), (l_win_near,) = _mm_mstream(
        "mixb_dh", [dproj], [w_in_t], [0], "nn", [(x1, "tile"), (mix_norm_g, "vec"), (dx2, "tile")],
        rms_outs, _rms_bwd_epilogue(0.5), comm=_ReduceScatter([(dwin, NEAR)]))

    dwd1, (l_win_far,) = _mm_tn("ffn1b_dwd", act1, dy1b, F // 2, D, BF16, comm=_ReduceScatter([(dwin, FAR)]))
    dwd1 = dwd1.reshape(4, 2, fb, D)
    (dgate1, dup1), (l_wd1_near,) = _mm_nstream("ffn1b_dact", dy1b, [d1], [0], "nt", [dsu1, sl1], [BF16, BF16],
                                                _dact_epilogue, cn=256, comm=_ReduceScatter([(dwd1, NEAR)]))
    dwg1, (l_wd1_far,) = _mm_tn("ffn1b_dwg", dgate1, h1, F // 2, D, BF16, comm=_ReduceScatter([(dwd1, FAR)]))
    dwg1 = dwg1.reshape(4, 2, fb, D)
    core = owner[1:2]
    dwu1, (s_wg1,) = _mm_tn("ffn1b_dwu", dup1, h1, F // 2, D, BF16, comm=_SiblingExchange([dwg1]))
    dwu1 = dwu1.reshape(4, 2, fb, D)
    p_wg1, (s_wu1,) = _pair_sum("pair_wg1", dwg1, s_wg1, core, comm=_SiblingExchange([dwu1]))
    p_wu1, _ = _pair_sum("pair_wu1", dwu1, s_wu1, core)
    (dh1,), (c_wg1, c_wu1) = _mm_mstream(
        "ffn1b_dh", [dgate1, dup1], [g1, u1], [0, 0], "nn", [], [f32_tile], _plain_epilogue,
        comm=_ChipExchange([p_wg1, p_wu1]))
    dx0, _, d_g1 = _rms_bwd("ffn1b_rms", x0, ffn1_norm_g, dh1, dx1, 1.0)

    small = dict(ffn1=d_g1, mix=d_gmix, ffn2=d_g2, final=d_final, ret=d_ret, gla=d_gla, b_a=d_ba)
    flat = jnp.concatenate([small[k].reshape(-1) for k in SMALL_ORDER]
                           + [d_wa2p[:GATE_RANK].reshape(-1), loss[0, :128]])
    rows = -(-flat.shape[0] // 128)
    rows = -(-rows // 8) * 8
    packed = jnp.pad(flat, (0, rows * 128 - flat.shape[0])).reshape(rows, 128)

    transposed = ("ffn1_w_gate", "ffn1_w_up", "ffn2_w_gate", "ffn2_w_up", "w_in")

    def to_2d(nm, a):
        if nm in transposed:
            return a[0].T
        return a.reshape((1, a.shape[0]) if a.ndim == 1 else a.shape[-2:])

    def from_2d(nm, a):
        return a.T[None] if nm in transposed else a.reshape(params[nm][0].shape)

    sums_a, (gathered,) = _owner_sums(
        "sum_a", [(dwg2, [l_wg2_near, l_wg2_far]), (dwu2, [l_wu2_near, l_wu2_far]), (dwd2, [l_wd2_near, l_wd2_far]),
                  (dwin, [l_win_near, l_win_far]), (dwout, [l_wout])], owner,
        comm=_AllGather([packed], ["plain"]))
    sums_b, _ = _owner_sums("sum_b", [(p_wg1, [c_wg1]), (p_wu1, [c_wu1]), (dwd1, [l_wd1_near, l_wd1_far])], owner)
    big_grads = {"ffn2_w_gate": sums_a[0], "ffn2_w_up": sums_a[1], "ffn2_w_down": sums_a[2], "w_out": sums_a[4],
                 "ffn1_w_gate": sums_b[0], "ffn1_w_up": sums_b[1], "ffn1_w_down": sums_b[2]}
    params = dict(
        ffn2_w_gate=(ffn2_w_gate, m_ffn2_w_gate, v_ffn2_w_gate), ffn2_w_up=(ffn2_w_up, m_ffn2_w_up, v_ffn2_w_up),
        ffn2_w_down=(ffn2_w_down, m_ffn2_w_down, v_ffn2_w_down), w_in=(w_in, m_w_in, v_w_in),
        w_out=(w_out, m_w_out, v_w_out), ffn1_w_gate=(ffn1_w_gate, m_ffn1_w_gate, v_ffn1_w_gate),
        ffn1_w_up=(ffn1_w_up, m_ffn1_w_up, v_ffn1_w_up), ffn1_w_down=(ffn1_w_down, m_ffn1_w_down, v_ffn1_w_down),
        ffn1_norm_g=(ffn1_norm_g, m_ffn1_norm_g, v_ffn1_norm_g), mix_norm_g=(mix_norm_g, m_mix_norm_g, v_mix_norm_g),
        ret_norm_g=(ret_norm_g, m_ret_norm_g, v_ret_norm_g), gla_w_a2=(gla_w_a2, m_gla_w_a2, v_gla_w_a2),
        gla_b_a=(gla_b_a, m_gla_b_a, v_gla_b_a), gla_norm_g=(gla_norm_g, m_gla_norm_g, v_gla_norm_g),
        ffn2_norm_g=(ffn2_norm_g, m_ffn2_norm_g, v_ffn2_norm_g), final_norm_g=(final_norm_g, m_final_norm_g, v_final_norm_g))
    grads, updates = {}, {}

    def run_adam(name, names, grad_2d, n_blocks):
        items = [(grad_2d[nm],) + tuple(to_2d(nm, a) for a in params[nm]) for nm in names]
        res, _ = _adamw_group(name, items, n_blocks)
        for nm, r in zip(names, res):
            grads[nm] = from_2d(nm, grad_2d[nm])
            updates[nm] = tuple(from_2d(nm, a) for a in r)

    run_adam("adamw_big", list(big_grads), big_grads, 4)
    run_adam("adamw_w_in", ["w_in"], {"w_in": sums_a[3]}, 1)

    total = _sum_devices("sum_small", gathered.reshape(N_DEV * rows, 128), rows).reshape(-1)
    sizes = [small[k].size for k in SMALL_ORDER] + [GATE_RANK * QK_W, 128]
    offs = [0]
    for s in sizes:
        offs.append(offs[-1] + s)
    pieces = [total[offs[i]:offs[i + 1]] for i in range(len(sizes))]
    g_small = {k: pieces[i].reshape(small[k].shape) for i, k in enumerate(SMALL_ORDER)}
    g_wa2_full = pieces[len(SMALL_ORDER)].reshape(GATE_RANK, QK_W)
    g_wa2 = lax.dynamic_slice(g_wa2_full, (0, dev * ab), (GATE_RANK, ab))
    loss_total = pieces[len(SMALL_ORDER) + 1][0]

    small_grads = {"ffn1_norm_g": g_small["ffn1"], "mix_norm_g": g_small["mix"], "ret_norm_g": g_small["ret"],
                   "gla_w_a2": g_wa2, "gla_b_a": g_small["b_a"], "gla_norm_g": g_small["gla"],
                   "ffn2_norm_g": g_small["ffn2"], "final_norm_g": g_small["final"]}
    run_adam("adamw_small", list(small_grads), small_grads, 1)

    order = ("ffn1_norm_g", "ffn1_w_gate", "ffn1_w_up", "ffn1_w_down", "mix_norm_g", "w_in", "ret_norm_g", "gla_w_a2",
             "gla_b_a", "gla_norm_g", "w_out", "ffn2_norm_g", "ffn2_w_gate", "ffn2_w_up", "ffn2_w_down", "final_norm_g")
    return (loss_total, dx0[None], *[grads[nm] for nm in order], *[updates[nm][0] for nm in order],
            *[updates[nm][1] for nm in order], *[updates[nm][2] for nm in order])
```

```python
import functools
import math

import jax
import jax.numpy as jnp
from jax import lax
from jax.experimental import pallas as pl
from jax.experimental.pallas import tpu as pltpu

F32 = jnp.float32
BF16 = jnp.bfloat16
MESH = pl.DeviceIdType.MESH
HBM = pl.BlockSpec(memory_space=pltpu.HBM)

N_DEV = 8
RMS_EPS = 1e-6
ROPE_BASE = 10000.0
HEADS = 4
DK = 64
DV = 128
QK_W = HEADS * DK
V_W = HEADS * DV
GATE_RANK = 16
GATE_NORM = 16.0
CHUNK = 64
SUPER = 256
PROJ_W = 3200
C_RQ, C_RK, C_RV, C_RG, C_GQ, C_GK, C_GV, C_GG, C_GL = 0, 256, 512, 1024, 1536, 1792, 2048, 2560, 3072
GL_W = PROJ_W - C_GL
ADAM_LR, ADAM_B1, ADAM_B2, ADAM_EPS, ADAM_WD, ADAM_STEP = 0.001, 0.9, 0.999, 1e-08, 0.01, 10
VMEM_LIMIT_V7X = 52 * 1024 * 1024


def _cparams(**kw):
    return pltpu.CompilerParams(vmem_limit_bytes=VMEM_LIMIT_V7X, **kw)


def _dot(a, b, form, precision=None):
    dims = {"nn": (((1,), (0,)), ((), ())), "nt": (((1,), (1,)), ((), ())), "tn": (((0,), (0,)), ((), ()))}[form]
    return lax.dot_general(a, b, dims, preferred_element_type=F32, precision=precision)


def _sigmoid(x):
    return 1.0 / (1.0 + jnp.exp(-x))


def _coords():
    return lax.axis_index("x"), lax.axis_index("y"), lax.axis_index("c")


class _NoComm:
    inputs, out_shapes, scratch = (), (), ()


class _AllGather:
    def __init__(self, arrays, kinds):
        self.inputs = tuple(arrays)
        self.kinds = tuple(kinds)
        n = len(arrays)
        self.out_shapes = tuple(
            jax.ShapeDtypeStruct((a.shape[0], N_DEV) + a.shape[1:] if k == "stack" else (N_DEV,) + a.shape, a.dtype)
            for a, k in zip(arrays, kinds))
        self.scratch = (pltpu.SemaphoreType.DMA((n, 7)), pltpu.SemaphoreType.DMA((n, 7)),
                        pltpu.SemaphoreType.DMA((n,)))

    def _ctx(self, srcs, outs, sems):
        send_sems, recv_sems, local_sems = sems
        x, y, c = _coords()
        me, sibling = (x, y, c), (x, y, 1 - c)
        chips = [(1 - x, y), (x, 1 - y), (1 - x, 1 - y)]

        def blk(m, dev):
            k = 4 * dev[0] + 2 * dev[1] + dev[2]
            return outs[m].at[:, k] if self.kinds[m] == "stack" else outs[m].at[k]

        def copy(m, s, block, to, src=None):
            return pltpu.make_async_remote_copy(
                src_ref=blk(m, block) if src is None else src, dst_ref=blk(m, block),
                send_sem=send_sems.at[m, s], recv_sem=recv_sems.at[m, s], device_id=to, device_id_type=MESH)

        def mine(m):
            return pltpu.make_async_copy(srcs[m], blk(m, me), local_sems.at[m])

        def first(m):
            return [copy(m, 0, me, sibling, src=srcs[m])] + [
                copy(m, 1 + j, me, (*chip, c), src=srcs[m]) for j, chip in enumerate(chips)]

        return me, sibling, chips, c, copy, mine, first

    def start(self, srcs, outs, sems):
        me, sibling, chips, c, copy, mine, first = self._ctx(srcs, outs, sems)
        for m in range(len(srcs)):
            mine(m).start()
            for cp in first(m):
                cp.start()

    def mid(self, srcs, outs, sems):
        me, sibling, chips, c, copy, mine, first = self._ctx(srcs, outs, sems)
        for j, chip in enumerate(chips):
            for m in range(len(srcs)):
                copy(m, 1 + j, (*chip, c), me).wait_recv()
                copy(m, 4 + j, (*chip, c), sibling).start()

    def finish(self, srcs, outs, sems):
        me, sibling, chips, c, copy, mine, first = self._ctx(srcs, outs, sems)
        for m in range(len(srcs)):
            copy(m, 0, sibling, me).wait_recv()
            for j, chip in enumerate(chips):
                copy(m, 4 + j, (*chip, 1 - c), me).wait_recv()
            for cp in first(m):
                cp.wait_send()
            for j, chip in enumerate(chips):
                copy(m, 4 + j, (*chip, c), sibling).wait_send()
            mine(m).wait()


RELATIONS = ((0, 0, 1), (1, 0, 0), (0, 1, 0), (1, 1, 0), (1, 0, 1), (0, 1, 1), (1, 1, 1))
NEAR = (0, 1, 2, 4, 5)
FAR = (3, 6)
ALL = NEAR + FAR


class _ReduceScatter:
    def __init__(self, parts):
        self.inputs = tuple(g for g, _ in parts)
        self.slots = tuple(s for _, s in parts)
        self.out_shapes = tuple(jax.ShapeDtypeStruct((len(s),) + g.shape[2:], g.dtype) for g, s in parts)
        n_max = max(len(s) for s in self.slots)
        n = len(parts)
        self.scratch = (pltpu.SemaphoreType.DMA((n, n_max)), pltpu.SemaphoreType.DMA((n, n_max)))

    def _copies(self, srcs, outs, sems):
        send_sems, recv_sems = sems
        x, y, c = _coords()
        copies = []
        for m, slots in enumerate(self.slots):
            for i, s in enumerate(slots):
                fx, fy, fc = RELATIONS[s]
                px = 1 - x if fx else x
                py = 1 - y if fy else y
                pc = 1 - c if fc else c
                copies.append(pltpu.make_async_remote_copy(
                    src_ref=srcs[m].at[2 * px + py, pc], dst_ref=outs[m].at[i], send_sem=send_sems.at[m, i],
                    recv_sem=recv_sems.at[m, i], device_id=(px, py, pc), device_id_type=MESH))
        return copies

    def start(self, srcs, outs, sems):
        for cp in self._copies(srcs, outs, sems):
            cp.start()

    def mid(self, srcs, outs, sems):
        pass

    def finish(self, srcs, outs, sems):
        for cp in self._copies(srcs, outs, sems):
            cp.wait()


class _SiblingExchange:
    def __init__(self, grads):
        self.inputs = tuple(grads)
        self.out_shapes = tuple(jax.ShapeDtypeStruct((4,) + g.shape[2:], g.dtype) for g in grads)
        self.scratch = (pltpu.SemaphoreType.DMA((len(grads),)), pltpu.SemaphoreType.DMA((len(grads),)))

    def _copies(self, srcs, outs, sems):
        send_sems, recv_sems = sems
        x, y, c = _coords()
        return [pltpu.make_async_remote_copy(
            src_ref=srcs[m].at[:, 1 - c], dst_ref=outs[m], send_sem=send_sems.at[m], recv_sem=recv_sems.at[m],
            device_id=(x, y, 1 - c), device_id_type=MESH) for m in range(len(srcs))]

    def start(self, srcs, outs, sems):
        for cp in self._copies(srcs, outs, sems):
            cp.start()

    def mid(self, srcs, outs, sems):
        pass

    def finish(self, srcs, outs, sems):
        for cp in self._copies(srcs, outs, sems):
            cp.wait()


class _ChipExchange:
    def __init__(self, partials):
        self.inputs = tuple(partials)
        self.out_shapes = tuple(jax.ShapeDtypeStruct((3,) + p.shape[1:], p.dtype) for p in partials)
        n = len(partials)
        self.scratch = (pltpu.SemaphoreType.DMA((n, 3)), pltpu.SemaphoreType.DMA((n, 3)))

    def _copies(self, srcs, outs, sems):
        send_sems, recv_sems = sems
        x, y, c = _coords()
        copies = []
        for m in range(len(srcs)):
            for j, (px, py) in enumerate([(1 - x, y), (x, 1 - y), (1 - x, 1 - y)]):
                copies.append(pltpu.make_async_remote_copy(
                    src_ref=srcs[m].at[2 * px + py], dst_ref=outs[m].at[j], send_sem=send_sems.at[m, j],
                    recv_sem=recv_sems.at[m, j], device_id=(px, py, c), device_id_type=MESH))
        return copies

    def start(self, srcs, outs, sems):
        for cp in self._copies(srcs, outs, sems):
            cp.start()

    def mid(self, srcs, outs, sems):
        pass

    def finish(self, srcs, outs, sems):
        for cp in self._copies(srcs, outs, sems):
            cp.wait()


class _Multi:
    def __init__(self, comms):
        self.comms = comms
        self.inputs = tuple(a for c in comms for a in c.inputs)
        self.out_shapes = tuple(s for c in comms for s in c.out_shapes)
        self.scratch = tuple(s for c in comms for s in c.scratch)

    def _each(self, phase, srcs, outs, sems):
        i = o = k = 0
        for c in self.comms:
            ni, no, nk = len(c.inputs), len(c.out_shapes), len(c.scratch)
            getattr(c, phase)(srcs[i:i + ni], outs[o:o + no], sems[k:k + nk])
            i, o, k = i + ni, o + no, k + nk

    def start(self, srcs, outs, sems):
        self._each("start", srcs, outs, sems)

    def mid(self, srcs, outs, sems):
        self._each("mid", srcs, outs, sems)

    def finish(self, srcs, outs, sems):
        self._each("finish", srcs, outs, sems)


def _call(name, main, grid, in_specs, out_specs, out_shape, args, scratch=(), comm=None, prefetch=None):
    comm = comm or _NoComm()
    counts = [len(in_specs), len(comm.inputs), len(out_shape), len(comm.out_shapes), len(scratch), len(comm.scratch)]
    n_steps = math.prod(grid)
    hosted = bool(comm.inputs)

    def body(*refs):
        if prefetch is not None:
            refs = refs[1:]
        parts, at = [], 0
        for n in counts:
            parts.append(refs[at:at + n])
            at += n
        ins, c_in, outs, c_out, scr, c_scr = parts
        step = pl.program_id(0)
        for d in range(1, len(grid)):
            step = step * grid[d] + pl.program_id(d)
        if hosted:
            @pl.when(step == 0)
            def _():
                comm.start(c_in, c_out, c_scr)
        main(ins, outs, scr)
        if hosted:
            @pl.when(step == max(n_steps - 2, 0))
            def _():
                comm.mid(c_in, c_out, c_scr)

            @pl.when(step == n_steps - 1)
            def _():
                comm.finish(c_in, c_out, c_scr)

    all_in = list(in_specs) + [HBM] * counts[1]
    all_out = list(out_specs) + [HBM] * counts[3]
    all_scratch = list(scratch) + list(comm.scratch)
    shapes = list(out_shape) + list(comm.out_shapes)
    if prefetch is None:
        res = pl.pallas_call(body, name=name, grid=grid, in_specs=all_in, out_specs=all_out, out_shape=shapes,
                             scratch_shapes=all_scratch, compiler_params=_cparams())(*args, *comm.inputs)
    else:
        res = pl.pallas_call(
            body, name=name, out_shape=shapes,
            grid_spec=pltpu.PrefetchScalarGridSpec(num_scalar_prefetch=1, grid=grid, in_specs=all_in,
                                                   out_specs=all_out, scratch_shapes=all_scratch),
            compiler_params=_cparams())(prefetch, *args, *comm.inputs)
    return res[:counts[2]], res[counts[2]:]


def _rms_fwd(name, x, g, comm=None):
    T, D = x.shape
    tm = min(T, 512)

    def main(ins, outs, scr):
        x_ref, g_ref = ins
        xv = x_ref[...]
        r = lax.rsqrt(jnp.mean(xv * xv, axis=-1, keepdims=True) + RMS_EPS)
        outs[0][...] = (xv * r * g_ref[...]).astype(outs[0].dtype)

    tile = pl.BlockSpec((tm, D), lambda i: (i, 0))
    (h,), extra = _call(name, main, (T // tm,), [tile, pl.BlockSpec((1, D), lambda i: (0, 0))], [tile],
                        [jax.ShapeDtypeStruct((T, D), BF16)], (x, g), comm=comm)
    return h, extra


def _final_loss_epilogue(scale, out_scale):
    def ep(acc, ex, outs):
        res_ref, g_ref, t_ref = ex
        dx_ref, dxb_ref, dg_ref, loss_ref = outs
        n = acc.shape[-1]
        xv = res_ref[...] + scale * acc
        r = lax.rsqrt(jnp.mean(xv * xv, axis=-1, keepdims=True) + RMS_EPS)
        xhat = xv * r
        err = xhat * g_ref[...] - t_ref[...]

        @pl.when(pl.program_id(0) == 0)
        def _():
            dg_ref[...] = jnp.zeros_like(dg_ref)
            loss_ref[...] = jnp.zeros_like(loss_ref)

        loss_ref[...] += jnp.broadcast_to(jnp.sum(err * err) * (0.5 / n), loss_ref.shape)
        dy = err * (1.0 / n)
        dg_ref[...] += jnp.sum(dy * xhat, axis=0, keepdims=True)
        dxhat = dy * g_ref[...]
        dx = r * (dxhat - xhat * jnp.mean(dxhat * xhat, axis=-1, keepdims=True))
        dx_ref[...] = dx
        dxb_ref[...] = (out_scale * dx).astype(dxb_ref.dtype)
    return ep


def _mm_nstream(name, a, ws, w_sel, w_form, comps, out_dtypes, epilogue, cn, rows=1024, comm=None):
    T, K = a.shape
    N = ws[0].shape[1]
    rows = min(rows, T)
    assert N % cn == 0 and T % rows == 0
    n_w, n_c = len(ws), len(comps)

    def main(ins, outs, scr):
        a_ref = ins[0]
        w_refs = ins[1:1 + n_w]
        c_refs = ins[1 + n_w:]

        for r in range(T // rows):
            sl = slice(r * rows, (r + 1) * rows)
            a_blk = a_ref[sl, :]
            dots = [_dot(a_blk, w_ref[...], w_form) for w_ref in w_refs]
            res = epilogue(dots, [c_ref[sl, :] for c_ref in c_refs])
            for o_ref, o in zip(outs, res):
                o_ref[sl, :] = o.astype(o_ref.dtype)

    if w_form == "nt":
        w_specs = [pl.BlockSpec((None, cn, K), functools.partial(lambda j, s: (s, j, 0), s=s)) for s in w_sel]
    else:
        w_specs = [pl.BlockSpec((K, cn), lambda j: (0, j)) for _ in ws]
    chunk = pl.BlockSpec((T, cn), lambda j: (0, j))
    return _call(name, main, (N // cn,), [pl.BlockSpec((T, K), lambda j: (0, 0))] + w_specs + [chunk] * n_c,
                 [chunk] * len(out_dtypes), [jax.ShapeDtypeStruct((T, N), dt) for dt in out_dtypes],
                 (a, *ws, *comps), comm=comm)


def _mm_mstream(name, as_, ws, w_sel, w_form, extras, outs_desc, epilogue, tm=512, comm=None):
    T = as_[0].shape[0]
    tm = min(tm, T)
    n_a = len(as_)
    w_shapes = [w.shape[-2:] for w in ws]
    N = w_shapes[0][1] if w_form == "nn" else w_shapes[0][0]

    def main(ins, outs, scr):
        a_refs = ins[:n_a]
        w_refs = ins[n_a:2 * n_a]
        acc = None
        for a_ref, w_ref in zip(a_refs, w_refs):
            d = _dot(a_ref[...], w_ref[...], w_form)
            acc = d if acc is None else acc + d
        epilogue(acc, ins[2 * n_a:], outs)

    kind_spec = {"tile": pl.BlockSpec((tm, N), lambda i: (i, 0)), "vec": pl.BlockSpec((1, N), lambda i: (0, 0))}
    kind_shape = {"tile": (T, N), "vec": (1, N)}
    a_specs = [pl.BlockSpec((tm, a.shape[1]), lambda i: (i, 0)) for a in as_]
    w_specs = []
    for w, s in zip(ws, w_sel):
        if w.ndim == 3:
            w_specs.append(pl.BlockSpec((None,) + tuple(w.shape[1:]), functools.partial(lambda i, s: (s, 0, 0), s=s),
                                        pipeline_mode=pl.Buffered(1)))
        else:
            w_specs.append(pl.BlockSpec(tuple(w.shape), lambda i: (0, 0), pipeline_mode=pl.Buffered(1)))
    args = list(as_) + list(ws) + [e for e, _ in extras]
    return _call(name, main, (T // tm,), a_specs + w_specs + [kind_spec[k] for _, k in extras],
                 [kind_spec[k] for _, k in outs_desc],
                 [jax.ShapeDtypeStruct(kind_shape[k], dt) for dt, k in outs_desc], args, comm=comm)


def _plain_epilogue(acc, ex, outs):
    outs[0][...] = acc.astype(outs[0].dtype)


def _residual_rms_epilogue(scale):
    def ep(acc, ex, outs):
        xv = ex[0][...] + scale * acc
        outs[0][...] = xv
        r = lax.rsqrt(jnp.mean(xv * xv, axis=-1, keepdims=True) + RMS_EPS)
        outs[1][...] = (xv * r * ex[1][...]).astype(outs[1].dtype)
    return ep


def _rms_bwd_epilogue(out_scale):
    def ep(acc, ex, outs):
        x_ref, g_ref, dres_ref = ex
        dx_ref, dxb_ref, dg_ref = outs
        xv = x_ref[...]
        r = lax.rsqrt(jnp.mean(xv * xv, axis=-1, keepdims=True) + RMS_EPS)
        xhat = xv * r

        @pl.when(pl.program_id(0) == 0)
        def _():
            dg_ref[...] = jnp.zeros_like(dg_ref)

        dg_ref[...] += jnp.sum(acc * xhat, axis=0, keepdims=True)
        dxhat = acc * g_ref[...]
        dx = r * (dxhat - xhat * jnp.mean(dxhat * xhat, axis=-1, keepdims=True)) + dres_ref[...]
        dx_ref[...] = dx
        dxb_ref[...] = (out_scale * dx).astype(dxb_ref.dtype)
    return ep


def _mm_tn(name, a, b, tmo, tno, out_dtype, tk=2048, comm=None):
    T, Ma = a.shape
    Nb = b.shape[1]
    tk = min(tk, T)
    nk = T // tk

    def main(ins, outs, scr):
        a_ref, b_ref = ins
        (acc_ref,) = scr
        k = pl.program_id(2)

        @pl.when(k == 0)
        def _():
            acc_ref[...] = jnp.zeros_like(acc_ref)

        acc_ref[...] += _dot(a_ref[...], b_ref[...], "tn")

        @pl.when(k == nk - 1)
        def _():
            outs[0][...] = acc_ref[...].astype(outs[0].dtype)

    (out,), extra = _call(
        name, main, (Ma // tmo, Nb // tno, nk),
        [pl.BlockSpec((tk, tmo), lambda i, j, k: (k, i)), pl.BlockSpec((tk, tno), lambda i, j, k: (k, j))],
        [pl.BlockSpec((tmo, tno), lambda i, j, k: (i, j))], [jax.ShapeDtypeStruct((Ma, Nb), out_dtype)],
        (a, b), scratch=[pltpu.VMEM((tmo, tno), F32)], comm=comm)
    return out, extra


def _swiglu_parts(g, u):
    s = _sigmoid(g)
    silu = g * s
    return [u * (s + silu * (1.0 - s)), silu, silu * u]


def _silu_mul_epilogue(dots, comps):
    g, u = dots
    return _swiglu_parts(g, u)


def _up_act_epilogue(dots, comps):
    (u,) = dots
    return _swiglu_parts(comps[0].astype(F32), u)


def _dact_epilogue(dots, comps):
    dact = dots[0].astype(BF16)
    return [dact * comps[0], dact * comps[1]]


def _identity_epilogue(dots, comps):
    return list(dots)


def _swap_halves(x):
    lane = lax.broadcasted_iota(jnp.int32, x.shape, 1)
    first = (lane % DK) < (DK // 2)
    return jnp.where(first, pltpu.roll(x, 128 - DK // 2, 1), pltpu.roll(x, DK // 2, 1))


def _rotary(t, cos, sin_signed):
    halves = []
    for p in range(QK_W // 128):
        th = t[:, 128 * p:128 * (p + 1)]
        halves.append(th * cos + _swap_halves(th) * sin_signed)
    return jnp.concatenate(halves, axis=1)


def _rotary_transposed(d, cos, sin_signed):
    halves = []
    for p in range(QK_W // 128):
        dh = d[:, 128 * p:128 * (p + 1)]
        halves.append(dh * cos + _swap_halves(dh * sin_signed))
    return jnp.concatenate(halves, axis=1)


def _log_sigmoid(x):
    return jnp.minimum(x, 0.0) - jnp.log(1.0 + jnp.exp(-jnp.abs(x)))


def _attn_masks():
    row = lax.broadcasted_iota(jnp.int32, (SUPER, SUPER), 0)
    col = lax.broadcasted_iota(jnp.int32, (SUPER, SUPER), 1)
    same = (row // CHUNK) == (col // CHUNK)
    return row, col, same


def _group_inputs(grp, pr, cos, sin_signed, lg, wa2, ba):
    if grp == 0:
        q = _rotary(pr[:, C_RQ:C_RQ + QK_W], cos, sin_signed)
        k = _rotary(pr[:, C_RK:C_RK + QK_W], cos, sin_signed) * (DK ** -0.5)
        v = pr[:, C_RV:C_RV + V_W]
        gate = pr[:, C_RG:C_RG + V_W]
        pos = lax.broadcasted_iota(jnp.int32, (SUPER, QK_W), 0).astype(F32) + 1.0
        return q, k, v, gate, pos * lg, None, None
    q = pr[:, C_GQ:C_GQ + QK_W] * (DK ** -0.5)
    k = pr[:, C_GK:C_GK + QK_W]
    v = pr[:, C_GV:C_GV + V_W]
    gate = pr[:, C_GG:C_GG + V_W]
    glow = pr[:, C_GL:C_GL + GL_W]
    logit = _dot(glow.astype(BF16), wa2.astype(BF16), "nn") + ba
    la = _log_sigmoid(logit) * (1.0 / GATE_NORM)
    row, col, _ = _attn_masks()
    lower = (col <= row).astype(F32)
    b_cum = _dot(lower, la, "nn", precision=lax.Precision.HIGHEST)
    return q, k, v, gate, b_cum, glow, logit


def _decay_factors(q, k, b_cum):
    c = b_cum[SUPER // 2 - 1:SUPER // 2, :]
    bl = b_cum[SUPER - 1:SUPER, :]
    e1 = jnp.exp(b_cum - c)
    e2 = jnp.exp(c - b_cum)
    e_b = jnp.exp(b_cum)
    e_l = jnp.exp(bl - b_cum)
    return dict(e1=e1, e2=e2, eb=e_b, el=e_l, ebl=jnp.exp(bl),
                qp=q * e1, qm=q * e2, kp=k * e1, km=k * e2, qs=q * e_b, kl=k * e_l)


def _state_block_mask():
    r = lax.broadcasted_iota(jnp.int32, (V_W, QK_W), 0)
    c = lax.broadcasted_iota(jnp.int32, (V_W, QK_W), 1)
    return (r // DV) == (c // DK)


def _attn_fwd(proj, cos, sin_signed, lg, wa2p, ba, gn_ret, gn_gla, comm=None):
    T = proj.shape[0]
    n_s = T // SUPER

    def main(ins, outs, scr):
        pr_ref, cos_ref, sin_ref, lg_ref, wa2_ref, ba_ref, gr_ref, gg_ref = ins
        o_ref, y_ref, st_ref = outs
        (s_ref,) = scr
        i = pl.program_id(0)

        @pl.when(i == 0)
        def _():
            s_ref[...] = jnp.zeros_like(s_ref)

        pr = pr_ref
        row, col, same = _attn_masks()
        m1 = col <= row
        m2 = jnp.logical_and(col > row, same)
        lane = lax.broadcasted_iota(jnp.int32, (1, QK_W), 1)
        blockmask = _state_block_mask()
        for grp in range(2):
            q, k, v, gate, b_cum, _, _ = _group_inputs(grp, pr, cos_ref[...], sin_ref[...], lg_ref[...],
                                                      wa2_ref[...], ba_ref[...])
            f = _decay_factors(q, k, b_cum)
            gn = gr_ref[...] if grp == 0 else gg_ref[...]
            s_prev = s_ref[grp]
            st_ref[0, grp] = s_prev
            o_inter = _dot(f["qs"].astype(BF16), s_prev.astype(BF16), "nt")
            kmb = f["km"].astype(BF16)
            kpb = f["kp"].astype(BF16)
            vb = v.astype(BF16)
            for h in range(HEADS):
                hm = (lane // DK) == h
                a1 = _dot(jnp.where(hm, f["qp"], 0.0).astype(BF16), kmb, "nt")
                a2 = _dot(jnp.where(hm, f["qm"], 0.0).astype(BF16), kpb, "nt")
                a = jnp.where(m1, a1, jnp.where(m2, a2, 0.0))
                lo = grp * V_W + h * DV
                o_h = _dot(a.astype(BF16), vb[:, h * DV:(h + 1) * DV], "nn") + o_inter[:, h * DV:(h + 1) * DV]
                o_ref[:, lo:lo + DV] = o_h
                r = lax.rsqrt(jnp.mean(o_h * o_h, axis=-1, keepdims=True) + RMS_EPS)
                gte = gate[:, h * DV:(h + 1) * DV]
                y = o_h * r * gn[:, h * DV:(h + 1) * DV] * (gte * _sigmoid(gte))
                y_ref[:, lo:lo + DV] = y.astype(y_ref.dtype)
            upd = _dot(vb, f["kl"].astype(BF16), "tn")
            s_ref[grp] = s_prev * f["ebl"] + jnp.where(blockmask, upd, 0.0)

    const = lambda shape: pl.BlockSpec(shape, lambda i: tuple(0 for _ in shape))
    return _call(
        "attn_fwd", main, (n_s,),
        [pl.BlockSpec((SUPER, PROJ_W), lambda i: (i, 0)),
         pl.BlockSpec((SUPER, 128), lambda i: (i, 0)), pl.BlockSpec((SUPER, 128), lambda i: (i, 0)),
         const((1, QK_W)), const((GL_W, QK_W)), const((1, QK_W)), const((1, V_W)), const((1, V_W))],
        [pl.BlockSpec((SUPER, 2 * V_W), lambda i: (i, 0)), pl.BlockSpec((SUPER, 2 * V_W), lambda i: (i, 0)),
         pl.BlockSpec((1, 2, V_W, QK_W), lambda i: (i, 0, 0, 0))],
        [jax.ShapeDtypeStruct((T, 2 * V_W), F32), jax.ShapeDtypeStruct((T, 2 * V_W), BF16),
         jax.ShapeDtypeStruct((n_s, 2, V_W, QK_W), F32)],
        (proj, cos, sin_signed, lg, wa2p, ba, gn_ret, gn_gla),
        scratch=[pltpu.VMEM((2, V_W, QK_W), F32)], comm=comm)


def _attn_bwd(proj, cos, sin_signed, lg, wa2p, ba, gn_ret, gn_gla, o, dy, states, comm=None):
    T = proj.shape[0]
    n_s = T // SUPER

    def main(ins, outs, scr):
        pr_ref, cos_ref, sin_ref, lg_ref, wa2_ref, ba_ref, gr_ref, gg_ref, o_ref, dy_ref, st_ref = ins
        dp_ref, dgr_ref, dgg_ref, dba_ref, dwa_ref = outs
        (ds_ref,) = scr
        i = pl.program_id(0)

        @pl.when(i == 0)
        def _():
            ds_ref[...] = jnp.zeros_like(ds_ref)
            dgr_ref[...] = jnp.zeros_like(dgr_ref)
            dgg_ref[...] = jnp.zeros_like(dgg_ref)
            dba_ref[...] = jnp.zeros_like(dba_ref)
            dwa_ref[...] = jnp.zeros_like(dwa_ref)

        pr = pr_ref
        cos = cos_ref[...]
        sin_signed = sin_ref[...]
        row, col, same = _attn_masks()
        m1 = col <= row
        m2 = jnp.logical_and(col > row, same)
        m1t = row <= col
        m2t = jnp.logical_and(row > col, same)
        lane = lax.broadcasted_iota(jnp.int32, (1, QK_W), 1)
        blockmask = _state_block_mask()
        for grp in range(2):
            q, k, v, gate, b_cum, glow, logit = _group_inputs(grp, pr, cos, sin_signed, lg_ref[...],
                                                              wa2_ref[...], ba_ref[...])
            f = _decay_factors(q, k, b_cum)
            gn = gr_ref[...] if grp == 0 else gg_ref[...]
            dgn_ref = dgr_ref if grp == 0 else dgg_ref
            do_parts, dgate_parts, dgn_parts = [], [], []
            for h in range(HEADS):
                lo = grp * V_W + h * DV
                o_h = o_ref[:, lo:lo + DV]
                r = lax.rsqrt(jnp.mean(o_h * o_h, axis=-1, keepdims=True) + RMS_EPS)
                n = o_h * r
                gte = gate[:, h * DV:(h + 1) * DV]
                sg = _sigmoid(gte)
                dy_h = dy_ref[:, lo:lo + DV]
                gn_h = gn[:, h * DV:(h + 1) * DV]
                dgate_parts.append(dy_h * n * gn_h * (sg * (1.0 + gte * (1.0 - sg))))
                dz = dy_h * (gte * sg)
                dgn_parts.append(jnp.sum(dz * n, axis=0, keepdims=True))
                dn = dz * gn_h
                do_parts.append(r * (dn - n * jnp.mean(dn * n, axis=-1, keepdims=True)))
            dgn_ref[...] += jnp.concatenate(dgn_parts, axis=1)
            dgate = jnp.concatenate(dgate_parts, axis=1)
            do = jnp.concatenate(do_parts, axis=1)
            dob = do.astype(BF16)
            vb = v.astype(BF16)
            s_prev = st_ref[0, grp]
            ds_new = ds_ref[grp]
            dsb = ds_new.astype(BF16)
            qpb, qmb = f["qp"].astype(BF16), f["qm"].astype(BF16)
            kpb, kmb = f["kp"].astype(BF16), f["km"].astype(BF16)
            dqp = jnp.zeros((SUPER, QK_W), F32)
            dqm = jnp.zeros((SUPER, QK_W), F32)
            dkp = jnp.zeros((SUPER, QK_W), F32)
            dkm = jnp.zeros((SUPER, QK_W), F32)
            dv_parts = []
            for h in range(HEADS):
                hm = (lane // DK) == h
                qp_h = jnp.where(hm, f["qp"], 0.0).astype(BF16)
                qm_h = jnp.where(hm, f["qm"], 0.0).astype(BF16)
                kp_h = jnp.where(hm, f["kp"], 0.0).astype(BF16)
                km_h = jnp.where(hm, f["km"], 0.0).astype(BF16)
                at = jnp.where(m1t, _dot(km_h, qpb, "nt"), jnp.where(m2t, _dot(kp_h, qmb, "nt"), 0.0))
                do_h = dob[:, h * DV:(h + 1) * DV]
                v_h = vb[:, h * DV:(h + 1) * DV]
                dv_parts.append(_dot(at.astype(BF16), do_h, "nn"))
                da = _dot(do_h, v_h, "nt")
                dat = _dot(v_h, do_h, "nt")
                da1 = jnp.where(m1, da, 0.0).astype(BF16)
                da2 = jnp.where(m2, da, 0.0).astype(BF16)
                da1t = jnp.where(m1t, dat, 0.0).astype(BF16)
                da2t = jnp.where(m2t, dat, 0.0).astype(BF16)
                dqp = dqp + _dot(da1, km_h, "nn")
                dqm = dqm + _dot(da2, kp_h, "nn")
                dkm = dkm + _dot(da1t, qp_h, "nn")
                dkp = dkp + _dot(da2t, qm_h, "nn")
            klb = f["kl"].astype(BF16)
            qsb = f["qs"].astype(BF16)
            dqs = _dot(dob, s_prev.astype(BF16), "nn")
            dkl = _dot(vb, dsb, "nn")
            dv = jnp.concatenate(dv_parts, axis=1) + _dot(klb, dsb, "nt")
            ds_ref[grp] = ds_new * f["ebl"] + jnp.where(blockmask, _dot(dob, qsb, "tn"), 0.0)
            dq = dqp * f["e1"] + dqm * f["e2"] + dqs * f["eb"]
            dk = dkm * f["e2"] + dkp * f["e1"] + dkl * f["el"]
            if grp == 0:
                dq = _rotary_transposed(dq, cos, sin_signed)
                dk = _rotary_transposed(dk * (DK ** -0.5), cos, sin_signed)
                dp_ref[:, C_RQ:C_RQ + QK_W] = dq.astype(dp_ref.dtype)
                dp_ref[:, C_RK:C_RK + QK_W] = dk.astype(dp_ref.dtype)
                dp_ref[:, C_RV:C_RV + V_W] = dv.astype(dp_ref.dtype)
                dp_ref[:, C_RG:C_RG + V_W] = dgate.astype(dp_ref.dtype)
            else:
                dkl_kl = dkl * klb.astype(F32)
                db = (dqp * qpb.astype(F32) - dkm * kmb.astype(F32) - dqm * qmb.astype(F32)
                      + dkp * kpb.astype(F32) + dqs * qsb.astype(F32) - dkl_kl)
                last = (jnp.sum(dkl_kl, axis=0, keepdims=True)
                        + f["ebl"] * jnp.sum(s_prev * ds_new, axis=0, keepdims=True))
                rowq = lax.broadcasted_iota(jnp.int32, (SUPER, QK_W), 0)
                db = db + jnp.where(rowq == SUPER - 1, last, 0.0)
                upper = (col >= row).astype(F32)
                dla = _dot(upper, db, "nn", precision=lax.Precision.HIGHEST)
                dlogit = dla * (1.0 / GATE_NORM) * (1.0 - _sigmoid(logit))
                dlb = dlogit.astype(BF16)
                dglow = _dot(dlb, wa2_ref[...].astype(BF16), "nt")
                dwa_ref[...] += _dot(glow.astype(BF16), dlb, "tn")
                dba_ref[...] += jnp.sum(dlogit, axis=0, keepdims=True)
                dp_ref[:, C_GQ:C_GQ + QK_W] = (dq * (DK ** -0.5)).astype(dp_ref.dtype)
                dp_ref[:, C_GK:C_GK + QK_W] = dk.astype(dp_ref.dtype)
                dp_ref[:, C_GV:C_GV + V_W] = dv.astype(dp_ref.dtype)
                dp_ref[:, C_GG:C_GG + V_W] = dgate.astype(dp_ref.dtype)
                dp_ref[:, C_GL:C_GL + GL_W] = dglow.astype(dp_ref.dtype)

    rev = lambda i: n_s - 1 - i
    const = lambda shape: pl.BlockSpec(shape, lambda i: tuple(0 for _ in shape))
    return _call(
        "attn_bwd", main, (n_s,),
        [pl.BlockSpec((SUPER, PROJ_W), lambda i: (rev(i), 0)),
         pl.BlockSpec((SUPER, 128), lambda i: (rev(i), 0)), pl.BlockSpec((SUPER, 128), lambda i: (rev(i), 0)),
         const((1, QK_W)), const((GL_W, QK_W)), const((1, QK_W)), const((1, V_W)), const((1, V_W)),
         pl.BlockSpec((SUPER, 2 * V_W), lambda i: (rev(i), 0)),
         pl.BlockSpec((SUPER, 2 * V_W), lambda i: (rev(i), 0)),
         pl.BlockSpec((1, 2, V_W, QK_W), lambda i: (rev(i), 0, 0, 0))],
        [pl.BlockSpec((SUPER, PROJ_W), lambda i: (rev(i), 0)),
         const((1, V_W)), const((1, V_W)), const((1, QK_W)), const((GL_W, QK_W))],
        [jax.ShapeDtypeStruct((T, PROJ_W), BF16),
         jax.ShapeDtypeStruct((1, V_W), F32), jax.ShapeDtypeStruct((1, V_W), F32),
         jax.ShapeDtypeStruct((1, QK_W), F32), jax.ShapeDtypeStruct((GL_W, QK_W), F32)],
        (proj, cos, sin_signed, lg, wa2p, ba, gn_ret, gn_gla, o, dy, states),
        scratch=[pltpu.VMEM((2, V_W, QK_W), F32)], comm=comm)


def _rotary_tables(T):
    half = DK // 2
    inv = ROPE_BASE ** (-jnp.arange(half, dtype=F32) * 2.0 / DK)
    ang = jnp.arange(T, dtype=F32)[:, None] * inv[None, :]
    cos, sin = jnp.cos(ang), jnp.sin(ang)
    cos_head = jnp.concatenate([cos, cos], axis=1)
    sin_head = jnp.concatenate([-sin, sin], axis=1)
    return jnp.tile(cos_head, (1, 128 // DK)), jnp.tile(sin_head, (1, 128 // DK))


def _sum_devices(name, gathered, m_per):
    def body(g_ref, o_ref):
        acc = g_ref[0:m_per, :]
        for k in range(1, N_DEV):
            acc = acc + g_ref[k * m_per:(k + 1) * m_per, :]
        o_ref[...] = acc

    return pl.pallas_call(body, name=name, out_shape=jax.ShapeDtypeStruct((m_per, 128), F32))(gathered)


def _pair_sum(name, grad, landed, core, comm=None):
    R, C = grad.shape[2:]

    def main(ins, outs, scr):
        outs[0][...] = (ins[0][...].astype(F32) + ins[1][...].astype(F32)).astype(outs[0].dtype)

    blk = pl.BlockSpec((None, R, C), lambda j, s: (j, 0, 0))
    (out,), extra = _call(name, main, (4,), [pl.BlockSpec((None, None, R, C), lambda j, s: (j, s[0], 0, 0)), blk],
                          [blk], [jax.ShapeDtypeStruct((4, R, C), BF16)], (grad, landed), prefetch=core, comm=comm)
    return out, extra


def _owner_sums(name, items, owner, comm=None):
    counts = [1 + len(landed) for _, landed in items]

    def main(ins, outs, scr):
        at = 0
        for o_ref, n in zip(outs, counts):
            acc = ins[at][...].astype(F32)
            for l_ref in ins[at + 1:at + n]:
                for j in range(l_ref.shape[0]):
                    acc = acc + l_ref[j].astype(F32)
            o_ref[...] = acc
            at += n

    once = pl.Buffered(1)
    in_specs, out_specs, out_shape, args = [], [], [], []
    for grad, landed in items:
        R, C = grad.shape[-2:]
        if grad.ndim == 4:
            in_specs.append(pl.BlockSpec((None, None, R, C), lambda i, s: (s[0], s[1], 0, 0), pipeline_mode=once))
        else:
            in_specs.append(pl.BlockSpec((None, R, C), lambda i, s: (s[0], 0, 0), pipeline_mode=once))
        in_specs += [pl.BlockSpec(tuple(l.shape), lambda i, s: (0, 0, 0), pipeline_mode=once) for l in landed]
        out_specs.append(pl.BlockSpec((R, C), lambda i, s: (0, 0)))
        out_shape.append(jax.ShapeDtypeStruct((R, C), F32))
        args += [grad, *landed]
    return _call(name, main, (1,), in_specs, out_specs, out_shape, args, comm=comm, prefetch=owner)


def _rms_bwd(name, x, g, dh, dres, out_scale):
    T, D = x.shape
    tm = min(T, 512)
    ep = _rms_bwd_epilogue(out_scale)

    def main(ins, outs, scr):
        x_ref, g_ref, dh_ref, dres_ref = ins
        ep(dh_ref[...], (x_ref, g_ref, dres_ref), outs)

    tile = pl.BlockSpec((tm, D), lambda i: (i, 0))
    vec = pl.BlockSpec((1, D), lambda i: (0, 0))
    outs, _ = _call(name, main, (T // tm,), [tile, vec, tile, tile], [tile, tile, vec],
                    [jax.ShapeDtypeStruct((T, D), F32), jax.ShapeDtypeStruct((T, D), BF16),
                     jax.ShapeDtypeStruct((1, D), F32)], (x, g, dh, dres))
    return outs


def _adamw_group(name, items, n_blocks, comm=None):
    n = len(items)

    def main(ins, outs, scr):
        for p in range(n):
            g_ref, w_ref, m_ref, v_ref = ins[4 * p:4 * p + 4]
            d_ref, nm_ref, nv_ref = outs[3 * p:3 * p + 3]
            gv = g_ref[...]
            nm = ADAM_B1 * m_ref[...] + (1.0 - ADAM_B1) * gv
            nv = ADAM_B2 * v_ref[...] + (1.0 - ADAM_B2) * (gv * gv)
            m_hat = nm / (1.0 - ADAM_B1 ** ADAM_STEP)
            v_hat = nv / (1.0 - ADAM_B2 ** ADAM_STEP)
            d_ref[...] = -ADAM_LR * (m_hat / (jnp.sqrt(v_hat) + ADAM_EPS) + ADAM_WD * w_ref[...])
            nm_ref[...] = nm
            nv_ref[...] = nv

    in_specs, out_specs, out_shape, args = [], [], [], []
    for item in items:
        R, C = item[1].shape
        assert R % n_blocks == 0
        spec = pl.BlockSpec((R // n_blocks, C), lambda i: (i, 0))
        in_specs += [spec] * 4
        out_specs += [spec] * 3
        out_shape += [jax.ShapeDtypeStruct((R, C), F32)] * 3
        args += list(item)
    outs, extra = _call(name, main, (n_blocks,), in_specs, out_specs, out_shape, args, comm=comm)
    return [tuple(outs[3 * p:3 * p + 3]) for p in range(n)], extra


SMALL_ORDER = ("ffn1", "mix", "ffn2", "final", "ret", "gla", "b_a")


def kernel(x, ffn1_norm_g, ffn1_w_gate, ffn1_w_up, ffn1_w_down, mix_norm_g, w_in, ret_norm_g, gla_w_a2, gla_b_a, gla_norm_g, w_out, ffn2_norm_g, ffn2_w_gate, ffn2_w_up, ffn2_w_down, final_norm_g, loss_target, m_ffn1_norm_g, m_ffn1_w_gate, m_ffn1_w_up, m_ffn1_w_down, m_mix_norm_g, m_w_in, m_ret_norm_g, m_gla_w_a2, m_gla_b_a, m_gla_norm_g, m_w_out, m_ffn2_norm_g, m_ffn2_w_gate, m_ffn2_w_up, m_ffn2_w_down, m_final_norm_g, v_ffn1_norm_g, v_ffn1_w_gate, v_ffn1_w_up, v_ffn1_w_down, v_mix_norm_g, v_w_in, v_ret_norm_g, v_gla_w_a2, v_gla_b_a, v_gla_norm_g, v_w_out, v_ffn2_norm_g, v_ffn2_w_gate, v_ffn2_w_up, v_ffn2_w_down, v_final_norm_g):
    xi, yi, ci = _coords()
    dev = 4 * xi + 2 * yi + ci
    owner = jnp.stack([2 * xi + yi, ci]).astype(jnp.int32)

    x0, target = x[0], loss_target[0]
    T, D = x0.shape
    fb = ffn1_w_gate.shape[2]
    ib = w_in.shape[2]
    ab = gla_w_a2.shape[2]
    F = N_DEV * fb
    cos, sin_signed = _rotary_tables(T)
    lg = jnp.repeat(jnp.log(1.0 - 2.0 ** (-5.0 - jnp.arange(HEADS, dtype=F32))), DK)[None, :]
    g_final = final_norm_g.reshape(1, D)

    g1_loc = ffn1_w_gate[0].T[None].astype(BF16)
    u1_loc = ffn1_w_up[0].T[None].astype(BF16)
    d1_loc = ffn1_w_down.astype(BF16)
    g2_loc = ffn2_w_gate[0].T[None].astype(BF16)
    u2_loc = ffn2_w_up[0].T[None].astype(BF16)
    d2_loc = ffn2_w_down.astype(BF16)
    in_loc = w_in[0].T.astype(BF16)
    out_loc = w_out[0].astype(BF16)

    h1, (g1,) = _rms_fwd("ffn1_rms", x0, ffn1_norm_g, comm=_AllGather([g1_loc], ["stack"]))
    g1 = g1.reshape(1, F, D)
    (gate1,), (u1,) = _mm_nstream("ffn1_gate", h1, [g1], [0], "nt", [], [BF16], _identity_epilogue, cn=256,
                                  comm=_AllGather([u1_loc], ["stack"]))
    u1 = u1.reshape(1, F, D)
    (dsu1, sl1, act1), (d1,) = _mm_nstream("ffn1_up", h1, [u1], [0], "nt", [gate1], [BF16, BF16, BF16],
                                           _up_act_epilogue, cn=256, comm=_AllGather([d1_loc], ["stack"]))
    d1 = d1.reshape(1, F, D)
    f32_tile, bf16_tile, f32_vec = (F32, "tile"), (BF16, "tile"), (F32, "vec")
    (x1, h2), (in_all, a_all) = _mm_mstream(
        "ffn1_down", [act1], [d1], [0], "nn", [(x0, "tile"), (mix_norm_g, "vec")], [f32_tile, bf16_tile],
        _residual_rms_epilogue(0.5), comm=_AllGather([in_loc, gla_w_a2[0]], ["plain", "plain"]))
    w_in_t = jnp.pad(in_all.reshape(1, N_DEV * ib, D), ((0, 0), (0, PROJ_W - N_DEV * ib), (0, 0)))
    wa2 = jnp.transpose(a_all, (1, 0, 2)).reshape(GATE_RANK, N_DEV * ab)
    wa2p = jnp.pad(wa2, ((0, GL_W - GATE_RANK), (0, 0)))

    (proj,), (g2,) = _mm_nstream("mix_proj", h2, [w_in_t], [0], "nt", [], [F32], _identity_epilogue, cn=640,
                                 comm=_AllGather([g2_loc], ["stack"]))
    (o, ymix, states), (u2, out_all) = _attn_fwd(proj, cos, sin_signed, lg, wa2p, gla_b_a, ret_norm_g, gla_norm_g,
                                                 comm=_AllGather([u2_loc, out_loc], ["stack", "plain"]))
    w_out_full = out_all.reshape(D, D)
    (x2, h3), _ = _mm_mstream("mix_out", [ymix], [w_out_full], [0], "nn", [(x1, "tile"), (ffn2_norm_g, "vec")],
                              [f32_tile, bf16_tile], _residual_rms_epilogue(1.0))
    g2, u2 = g2.reshape(1, F, D), u2.reshape(1, F, D)

    (dsu2, sl2, act2), (d2,) = _mm_nstream(
        "ffn2_up", h3, [g2, u2], [0, 0], "nt", [], [BF16, BF16, BF16], _silu_mul_epilogue, cn=256,
        comm=_AllGather([d2_loc], ["stack"]))
    d2 = d2.reshape(1, F, D)
    (dx3, dy3b, d_final, loss), _ = _mm_mstream(
        "ffn2_down", [act2], [d2], [0], "nn", [(x2, "tile"), (g_final, "vec"), (target, "tile")],
        [f32_tile, bf16_tile, f32_vec, f32_vec], _final_loss_epilogue(0.5, 0.5))

    dwd2, _ = _mm_tn("ffn2b_dwd", act2, dy3b, F // 2, D, BF16)
    dwd2 = dwd2.reshape(4, 2, fb, D)
    (dgate2, dup2), (l_wd2_near,) = _mm_nstream("ffn2b_dact", dy3b, [d2], [0], "nt", [dsu2, sl2], [BF16, BF16],
                                                _dact_epilogue, cn=256, comm=_ReduceScatter([(dwd2, NEAR)]))
    dwg2, (l_wd2_far,) = _mm_tn("ffn2b_dwg", dgate2, h3, F // 2, D, BF16, comm=_ReduceScatter([(dwd2, FAR)]))
    dwg2 = dwg2.reshape(4, 2, fb, D)
    dwu2, (l_wg2_near,) = _mm_tn("ffn2b_dwu", dup2, h3, F // 2, D, BF16, comm=_ReduceScatter([(dwg2, NEAR)]))
    dwu2 = dwu2.reshape(4, 2, fb, D)
    rms_outs = [f32_tile, bf16_tile, f32_vec]
    (dx2, dx2b, d_g2), (l_wg2_far, l_wu2_near) = _mm_mstream(
        "ffn2b_dh", [dgate2, dup2], [g2, u2], [0, 0], "nn", [(x2, "tile"), (ffn2_norm_g, "vec"), (dx3, "tile")],
        rms_outs, _rms_bwd_epilogue(1.0), comm=_ReduceScatter([(dwg2, FAR), (dwu2, NEAR)]))

    (dymix,), _ = _mm_mstream("mixb_dy", [dx2b], [w_out_full], [0], "nt", [], [f32_tile], _plain_epilogue)
    dwout, _ = _mm_tn("mixb_dwout", ymix, dx2b, D, D, BF16)
    dwout = dwout.reshape(4, 2, D // N_DEV, D)
    (dproj, d_ret, d_gla, d_ba, d_wa2p), (l_wu2_far, l_wout) = _attn_bwd(
        proj, cos, sin_signed, lg, wa2p, gla_b_a, ret_norm_g, gla_norm_g, o, dymix, states,
        comm=_ReduceScatter([(dwu2, FAR), (dwout, ALL)]))
    dwin_t, _ = _mm_tn("mixb_dwin", dproj, h2, 640, D, BF16)
    dwin = dwin_t[:N_DEV * ib].reshape(4, 2, ib, D)
    (dx1, dy1b, d_gmix), (l_win_near,) = _mm_mstream(
        "mixb_dh", [dproj], [w_in_t], [0], "nn", [(x1, "tile"), (mix_norm_g, "vec"), (dx2, "tile")],
        rms_outs, _rms_bwd_epilogue(0.5), comm=_ReduceScatter([(dwin, NEAR)]))

    dwd1, (l_win_far,) = _mm_tn("ffn1b_dwd", act1, dy1b, F // 2, D, BF16, comm=_ReduceScatter([(dwin, FAR)]))
    dwd1 = dwd1.reshape(4, 2, fb, D)
    (dgate1, dup1), (l_wd1_near,) = _mm_nstream("ffn1b_dact", dy1b, [d1], [0], "nt", [dsu1, sl1], [BF16, BF16],
                                                _dact_epilogue, cn=256, comm=_ReduceScatter([(dwd1, NEAR)]))
    dwg1, (l_wd1_far,) = _mm_tn("ffn1b_dwg", dgate1, h1, F // 2, D, BF16, comm=_ReduceScatter([(dwd1, FAR)]))
    dwg1 = dwg1.reshape(4, 2, fb, D)
    core = owner[1:2]
    dwu1, (s_wg1,) = _mm_tn("ffn1b_dwu", dup1, h1, F // 2, D, BF16, comm=_SiblingExchange([dwg1]))
    dwu1 = dwu1.reshape(4, 2, fb, D)
    p_wg1, (s_wu1,) = _pair_sum("pair_wg1", dwg1, s_wg1, core, comm=_SiblingExchange([dwu1]))
    p_wu1, _ = _pair_sum("pair_wu1", dwu1, s_wu1, core)
    (dh1,), (c_wg1, c_wu1) = _mm_mstream(
        "ffn1b_dh", [dgate1, dup1], [g1, u1], [0, 0], "nn", [], [f32_tile], _plain_epilogue,
        comm=_ChipExchange([p_wg1, p_wu1]))
    dx0, _, d_g1 = _rms_bwd("ffn1b_rms", x0, ffn1_norm_g, dh1, dx1, 1.0)

    small = dict(ffn1=d_g1, mix=d_gmix, ffn2=d_g2, final=d_final, ret=d_ret, gla=d_gla, b_a=d_ba)
    flat = jnp.concatenate([small[k].reshape(-1) for k in SMALL_ORDER]
                           + [d_wa2p[:GATE_RANK].reshape(-1), loss[0, :128]])
    rows = -(-flat.shape[0] // 128)
    rows = -(-rows // 8) * 8
    packed = jnp.pad(flat, (0, rows * 128 - flat.shape[0])).reshape(rows, 128)

    transposed = ("ffn1_w_gate", "ffn1_w_up", "ffn2_w_gate", "ffn2_w_up", "w_in")

    def to_2d(nm, a):
        if nm in transposed:
            return a[0].T
        return a.reshape((1, a.shape[0]) if a.ndim == 1 else a.shape[-2:])

    def from_2d(nm, a):
        return a.T[None] if nm in transposed else a.reshape(params[nm][0].shape)

    sums_a, (gathered,) = _owner_sums(
        "sum_a", [(dwg2, [l_wg2_near, l_wg2_far]), (dwu2, [l_wu2_near, l_wu2_far]), (dwd2, [l_wd2_near, l_wd2_far]),
                  (dwin, [l_win_near, l_win_far]), (dwout, [l_wout])], owner,
        comm=_AllGather([packed], ["plain"]))
    sums_b, _ = _owner_sums("sum_b", [(p_wg1, [c_wg1]), (p_wu1, [c_wu1]), (dwd1, [l_wd1_near, l_wd1_far])], owner)
    big_grads = {"ffn2_w_gate": sums_a[0], "ffn2_w_up": sums_a[1], "ffn2_w_down": sums_a[2], "w_out": sums_a[4],
                 "ffn1_w_gate": sums_b[0], "ffn1_w_up": sums_b[1], "ffn1_w_down": sums_b[2]}
    params = dict(
        ffn2_w_gate=(ffn2_w_gate, m_ffn2_w_gate, v_ffn2_w_gate), ffn2_w_up=(ffn2_w_up, m_ffn2_w_up, v_ffn2_w_up),
        ffn2_w_down=(ffn2_w_down, m_ffn2_w_down, v_ffn2_w_down), w_in=(w_in, m_w_in, v_w_in),
        w_out=(w_out, m_w_out, v_w_out), ffn1_w_gate=(ffn1_w_gate, m_ffn1_w_gate, v_ffn1_w_gate),
        ffn1_w_up=(ffn1_w_up, m_ffn1_w_up, v_ffn1_w_up), ffn1_w_down=(ffn1_w_down, m_ffn1_w_down, v_ffn1_w_down),
        ffn1_norm_g=(ffn1_norm_g, m_ffn1_norm_g, v_ffn1_norm_g), mix_norm_g=(mix_norm_g, m_mix_norm_g, v_mix_norm_g),
        ret_norm_g=(ret_norm_g, m_ret_norm_g, v_ret_norm_g), gla_w_a2=(gla_w_a2, m_gla_w_a2, v_gla_w_a2),
        gla_b_a=(gla_b_a, m_gla_b_a, v_gla_b_a), gla_norm_g=(gla_norm_g, m_gla_norm_g, v_gla_norm_g),
        ffn2_norm_g=(ffn2_norm_g, m_ffn2_norm_g, v_ffn2_norm_g), final_norm_g=(final_norm_g, m_final_norm_g, v_final_norm_g))
    grads, updates = {}, {}

    def run_adam(name, names, grad_2d, n_blocks):
        items = [(grad_2d[nm],) + tuple(to_2d(nm, a) for a in params[nm]) for nm in names]
        res, _ = _adamw_group(name, items, n_blocks)
        for nm, r in zip(names, res):
            grads[nm] = from_2d(nm, grad_2d[nm])
            updates[nm] = tuple(from_2d(nm, a) for a in r)

    run_adam("adamw_big", list(big_grads), big_grads, 4)
    run_adam("adamw_w_in", ["w_in"], {"w_in": sums_a[3]}, 1)

    total = _sum_devices("sum_small", gathered.reshape(N_DEV * rows, 128), rows).reshape(-1)
    sizes = [small[k].size for k in SMALL_ORDER] + [GATE_RANK * QK_W, 128]
    offs = [0]
    for s in sizes:
        offs.append(offs[-1] + s)
    pieces = [total[offs[i]:offs[i + 1]] for i in range(len(sizes))]
    g_small = {k: pieces[i].reshape(small[k].shape) for i, k in enumerate(SMALL_ORDER)}
    g_wa2_full = pieces[len(SMALL_ORDER)].reshape(GATE_RANK, QK_W)
    g_wa2 = lax.dynamic_slice(g_wa2_full, (0, dev * ab), (GATE_RANK, ab))
    loss_total = pieces[len(SMALL_ORDER) + 1][0]

    small_grads = {"ffn1_norm_g": g_small["ffn1"], "mix_norm_g": g_small["mix"], "ret_norm_g": g_small["ret"],
                   "gla_w_a2": g_wa2, "gla_b_a": g_small["b_a"], "gla_norm_g": g_small["gla"],
                   "ffn2_norm_g": g_small["ffn2"], "final_norm_g": g_small["final"]}
    run_adam("adamw_small", list(small_grads), small_grads, 1)

    order = ("ffn1_norm_g", "ffn1_w_gate", "ffn1_w_up", "ffn1_w_down", "mix_norm_g", "w_in", "ret_norm_g", "gla_w_a2",
             "gla_b_a", "gla_norm_g", "w_out", "ffn2_norm_g", "ffn2_w_gate", "ffn2_w_up", "ffn2_w_down", "final_norm_g")
    return (loss_total, dx0[None], *[grads[nm] for nm in order], *[updates[nm][0] for nm in order],
            *[updates[nm][1] for nm in order], *[updates[nm][2] for nm in order])
```

```python
import functools
import math

import jax
import jax.numpy as jnp
from jax import lax
from jax.experimental import pallas as pl
from jax.experimental.pallas import tpu as pltpu

F32 = jnp.float32
BF16 = jnp.bfloat16
MESH = pl.DeviceIdType.MESH
HBM = pl.BlockSpec(memory_space=pltpu.HBM)

N_DEV = 8
RMS_EPS = 1e-6
ROPE_BASE = 10000.0
HEADS = 4
DK = 64
DV = 128
QK_W = HEADS * DK
V_W = HEADS * DV
GATE_RANK = 16
GATE_NORM = 16.0
CHUNK = 64
SUPER = 256
PROJ_W = 3200
C_RQ, C_RK, C_RV, C_RG, C_GQ, C_GK, C_GV, C_GG, C_GL = 0, 256, 512, 1024, 1536, 1792, 2048, 2560, 3072
GL_W = PROJ_W - C_GL
ADAM_LR, ADAM_B1, ADAM_B2, ADAM_EPS, ADAM_WD, ADAM_STEP = 0.001, 0.9, 0.999, 1e-08, 0.01, 10
VMEM_LIMIT_V7X = 52 * 1024 * 1024


def _cparams(**kw):
    return pltpu.CompilerParams(vmem_limit_bytes=VMEM_LIMIT_V7X, **kw)


def _dot(a, b, form, precision=None):
    dims = {"nn": (((1,), (0,)), ((), ())), "nt": (((1,), (1,)), ((), ())), "tn": (((0,), (0,)), ((), ()))}[form]
    return lax.dot_general(a, b, dims, preferred_element_type=F32, precision=precision)


def _sigmoid(x):
    return 1.0 / (1.0 + jnp.exp(-x))


def _coords():
    return lax.axis_index("x"), lax.axis_index("y"), lax.axis_index("c")


class _NoComm:
    inputs, out_shapes, scratch = (), (), ()


class _AllGather:
    def __init__(self, arrays, kinds):
        self.inputs = tuple(arrays)
        self.kinds = tuple(kinds)
        n = len(arrays)
        self.out_shapes = tuple(
            jax.ShapeDtypeStruct((a.shape[0], N_DEV) + a.shape[1:] if k == "stack" else (N_DEV,) + a.shape, a.dtype)
            for a, k in zip(arrays, kinds))
        self.scratch = (pltpu.SemaphoreType.DMA((n, 7)), pltpu.SemaphoreType.DMA((n, 7)),
                        pltpu.SemaphoreType.DMA((n,)))

    def _ctx(self, srcs, outs, sems):
        send_sems, recv_sems, local_sems = sems
        x, y, c = _coords()
        me, sibling = (x, y, c), (x, y, 1 - c)
        chips = [(1 - x, y), (x, 1 - y), (1 - x, 1 - y)]

        def blk(m, dev):
            k = 4 * dev[0] + 2 * dev[1] + dev[2]
            return outs[m].at[:, k] if self.kinds[m] == "stack" else outs[m].at[k]

        def copy(m, s, block, to, src=None):
            return pltpu.make_async_remote_copy(
                src_ref=blk(m, block) if src is None else src, dst_ref=blk(m, block),
                send_sem=send_sems.at[m, s], recv_sem=recv_sems.at[m, s], device_id=to, device_id_type=MESH)

        def mine(m):
            return pltpu.make_async_copy(srcs[m], blk(m, me), local_sems.at[m])

        def first(m):
            return [copy(m, 0, me, sibling, src=srcs[m])] + [
                copy(m, 1 + j, me, (*chip, c), src=srcs[m]) for j, chip in enumerate(chips)]

        return me, sibling, chips, c, copy, mine, first

    def start(self, srcs, outs, sems):
        me, sibling, chips, c, copy, mine, first = self._ctx(srcs, outs, sems)
        for m in range(len(srcs)):
            mine(m).start()
            for cp in first(m):
                cp.start()

    def mid(self, srcs, outs, sems):
        me, sibling, chips, c, copy, mine, first = self._ctx(srcs, outs, sems)
        for j, chip in enumerate(chips):
            for m in range(len(srcs)):
                copy(m, 1 + j, (*chip, c), me).wait_recv()
                copy(m, 4 + j, (*chip, c), sibling).start()

    def finish(self, srcs, outs, sems):
        me, sibling, chips, c, copy, mine, first = self._ctx(srcs, outs, sems)
        for m in range(len(srcs)):
            copy(m, 0, sibling, me).wait_recv()
            for j, chip in enumerate(chips):
                copy(m, 4 + j, (*chip, 1 - c), me).wait_recv()
            for cp in first(m):
                cp.wait_send()
            for j, chip in enumerate(chips):
                copy(m, 4 + j, (*chip, c), sibling).wait_send()
            mine(m).wait()


RELATIONS = ((0, 0, 1), (1, 0, 0), (0, 1, 0), (1, 1, 0), (1, 0, 1), (0, 1, 1), (1, 1, 1))
NEAR = (0, 1, 2, 4, 5)
FAR = (3, 6)
ALL = NEAR + FAR


class _ReduceScatter:
    def __init__(self, parts):
        self.inputs = tuple(g for g, _ in parts)
        self.slots = tuple(s for _, s in parts)
        self.out_shapes = tuple(jax.ShapeDtypeStruct((len(s),) + g.shape[2:], g.dtype) for g, s in parts)
        n_max = max(len(s) for s in self.slots)
        n = len(parts)
        self.scratch = (pltpu.SemaphoreType.DMA((n, n_max)), pltpu.SemaphoreType.DMA((n, n_max)))

    def _copies(self, srcs, outs, sems):
        send_sems, recv_sems = sems
        x, y, c = _coords()
        copies = []
        for m, slots in enumerate(self.slots):
            for i, s in enumerate(slots):
                fx, fy, fc = RELATIONS[s]
                px = 1 - x if fx else x
                py = 1 - y if fy else y
                pc = 1 - c if fc else c
                copies.append(pltpu.make_async_remote_copy(
                    src_ref=srcs[m].at[2 * px + py, pc], dst_ref=outs[m].at[i], send_sem=send_sems.at[m, i],
                    recv_sem=recv_sems.at[m, i], device_id=(px, py, pc), device_id_type=MESH))
        return copies

    def start(self, srcs, outs, sems):
        for cp in self._copies(srcs, outs, sems):
            cp.start()

    def mid(self, srcs, outs, sems):
        pass

    def finish(self, srcs, outs, sems):
        for cp in self._copies(srcs, outs, sems):
            cp.wait()


class _SiblingExchange:
    def __init__(self, grads):
        self.inputs = tuple(grads)
        self.out_shapes = tuple(jax.ShapeDtypeStruct((4,) + g.shape[2:], g.dtype) for g in grads)
        self.scratch = (pltpu.SemaphoreType.DMA((len(grads),)), pltpu.SemaphoreType.DMA((len(grads),)))

    def _copies(self, srcs, outs, sems):
        send_sems, recv_sems = sems
        x, y, c = _coords()
        return [pltpu.make_async_remote_copy(
            src_ref=srcs[m].at[:, 1 - c], dst_ref=outs[m], send_sem=send_sems.at[m], recv_sem=recv_sems.at[m],
            device_id=(x, y, 1 - c), device_id_type=MESH) for m in range(len(srcs))]

    def start(self, srcs, outs, sems):
        for cp in self._copies(srcs, outs, sems):
            cp.start()

    def mid(self, srcs, outs, sems):
        pass

    def finish(self, srcs, outs, sems):
        for cp in self._copies(srcs, outs, sems):
            cp.wait()


class _ChipExchange:
    def __init__(self, partials):
        self.inputs = tuple(partials)
        self.out_shapes = tuple(jax.ShapeDtypeStruct((3,) + p.shape[1:], p.dtype) for p in partials)
        n = len(partials)
        self.scratch = (pltpu.SemaphoreType.DMA((n, 3)), pltpu.SemaphoreType.DMA((n, 3)))

    def _copies(self, srcs, outs, sems):
        send_sems, recv_sems = sems
        x, y, c = _coords()
        copies = []
        for m in range(len(srcs)):
            for j, (px, py) in enumerate([(1 - x, y), (x, 1 - y), (1 - x, 1 - y)]):
                copies.append(pltpu.make_async_remote_copy(
                    src_ref=srcs[m].at[2 * px + py], dst_ref=outs[m].at[j], send_sem=send_sems.at[m, j],
                    recv_sem=recv_sems.at[m, j], device_id=(px, py, c), device_id_type=MESH))
        return copies

    def start(self, srcs, outs, sems):
        for cp in self._copies(srcs, outs, sems):
            cp.start()

    def mid(self, srcs, outs, sems):
        pass

    def finish(self, srcs, outs, sems):
        for cp in self._copies(srcs, outs, sems):
            cp.wait()


class _Multi:
    def __init__(self, comms):
        self.comms = comms
        self.inputs = tuple(a for c in comms for a in c.inputs)
        self.out_shapes = tuple(s for c in comms for s in c.out_shapes)
        self.scratch = tuple(s for c in comms for s in c.scratch)

    def _each(self, phase, srcs, outs, sems):
        i = o = k = 0
        for c in self.comms:
            ni, no, nk = len(c.inputs), len(c.out_shapes), len(c.scratch)
            getattr(c, phase)(srcs[i:i + ni], outs[o:o + no], sems[k:k + nk])
            i, o, k = i + ni, o + no, k + nk

    def start(self, srcs, outs, sems):
        self._each("start", srcs, outs, sems)

    def mid(self, srcs, outs, sems):
        self._each("mid", srcs, outs, sems)

    def finish(self, srcs, outs, sems):
        self._each("finish", srcs, outs, sems)


def _call(name, main, grid, in_specs, out_specs, out_shape, args, scratch=(), comm=None, prefetch=None):
    comm = comm or _NoComm()
    counts = [len(in_specs), len(comm.inputs), len(out_shape), len(comm.out_shapes), len(scratch), len(comm.scratch)]
    n_steps = math.prod(grid)
    hosted = bool(comm.inputs)

    def body(*refs):
        if prefetch is not None:
            refs = refs[1:]
        parts, at = [], 0
        for n in counts:
            parts.append(refs[at:at + n])
            at += n
        ins, c_in, outs, c_out, scr, c_scr = parts
        step = pl.program_id(0)
        for d in range(1, len(grid)):
            step = step * grid[d] + pl.program_id(d)
        if hosted:
            @pl.when(step == 0)
            def _():
                comm.start(c_in, c_out, c_scr)
        main(ins, outs, scr)
        if hosted:
            @pl.when(step == max(n_steps - 2, 0))
            def _():
                comm.mid(c_in, c_out, c_scr)

            @pl.when(step == n_steps - 1)
            def _():
                comm.finish(c_in, c_out, c_scr)

    all_in = list(in_specs) + [HBM] * counts[1]
    all_out = list(out_specs) + [HBM] * counts[3]
    all_scratch = list(scratch) + list(comm.scratch)
    shapes = list(out_shape) + list(comm.out_shapes)
    if prefetch is None:
        res = pl.pallas_call(body, name=name, grid=grid, in_specs=all_in, out_specs=all_out, out_shape=shapes,
                             scratch_shapes=all_scratch, compiler_params=_cparams())(*args, *comm.inputs)
    else:
        res = pl.pallas_call(
            body, name=name, out_shape=shapes,
            grid_spec=pltpu.PrefetchScalarGridSpec(num_scalar_prefetch=1, grid=grid, in_specs=all_in,
                                                   out_specs=all_out, scratch_shapes=all_scratch),
            compiler_params=_cparams())(prefetch, *args, *comm.inputs)
    return res[:counts[2]], res[counts[2]:]


def _rms_fwd(name, x, g, comm=None):
    T, D = x.shape
    tm = min(T, 512)

    def main(ins, outs, scr):
        x_ref, g_ref = ins
        xv = x_ref[...]
        r = lax.rsqrt(jnp.mean(xv * xv, axis=-1, keepdims=True) + RMS_EPS)
        outs[0][...] = (xv * r * g_ref[...]).astype(outs[0].dtype)

    tile = pl.BlockSpec((tm, D), lambda i: (i, 0))
    (h,), extra = _call(name, main, (T // tm,), [tile, pl.BlockSpec((1, D), lambda i: (0, 0))], [tile],
                        [jax.ShapeDtypeStruct((T, D), BF16)], (x, g), comm=comm)
    return h, extra


def _final_loss_epilogue(scale, out_scale):
    def ep(acc, ex, outs):
        res_ref, g_ref, t_ref = ex
        dx_ref, dxb_ref, dg_ref, loss_ref = outs
        n = acc.shape[-1]
        xv = res_ref[...] + scale * acc
        r = lax.rsqrt(jnp.mean(xv * xv, axis=-1, keepdims=True) + RMS_EPS)
        xhat = xv * r
        err = xhat * g_ref[...] - t_ref[...]

        @pl.when(pl.program_id(0) == 0)
        def _():
            dg_ref[...] = jnp.zeros_like(dg_ref)
            loss_ref[...] = jnp.zeros_like(loss_ref)

        loss_ref[...] += jnp.broadcast_to(jnp.sum(err * err) * (0.5 / n), loss_ref.shape)
        dy = err * (1.0 / n)
        dg_ref[...] += jnp.sum(dy * xhat, axis=0, keepdims=True)
        dxhat = dy * g_ref[...]
        dx = r * (dxhat - xhat * jnp.mean(dxhat * xhat, axis=-1, keepdims=True))
        dx_ref[...] = dx
        dxb_ref[...] = (out_scale * dx).astype(dxb_ref.dtype)
    return ep


def _mm_nstream(name, a, ws, w_sel, w_form, comps, out_dtypes, epilogue, cn, rows=1024, comm=None):
    T, K = a.shape
    N = ws[0].shape[1]
    rows = min(rows, T)
    assert N % cn == 0 and T % rows == 0
    n_w, n_c = len(ws), len(comps)

    def main(ins, outs, scr):
        a_ref = ins[0]
        w_refs = ins[1:1 + n_w]
        c_refs = ins[1 + n_w:]

        for r in range(T // rows):
            sl = slice(r * rows, (r + 1) * rows)
            a_blk = a_ref[sl, :]
            dots = [_dot(a_blk, w_ref[...], w_form) for w_ref in w_refs]
            res = epilogue(dots, [c_ref[sl, :] for c_ref in c_refs])
            for o_ref, o in zip(outs, res):
                o_ref[sl, :] = o.astype(o_ref.dtype)

    if w_form == "nt":
        w_specs = [pl.BlockSpec((None, cn, K), functools.partial(lambda j, s: (s, j, 0), s=s)) for s in w_sel]
    else:
        w_specs = [pl.BlockSpec((K, cn), lambda j: (0, j)) for _ in ws]
    chunk = pl.BlockSpec((T, cn), lambda j: (0, j))
    return _call(name, main, (N // cn,), [pl.BlockSpec((T, K), lambda j: (0, 0))] + w_specs + [chunk] * n_c,
                 [chunk] * len(out_dtypes), [jax.ShapeDtypeStruct((T, N), dt) for dt in out_dtypes],
                 (a, *ws, *comps), comm=comm)


def _mm_mstream(name, as_, ws, w_sel, w_form, extras, outs_desc, epilogue, tm=512, comm=None):
    T = as_[0].shape[0]
    tm = min(tm, T)
    n_a = len(as_)
    w_shapes = [w.shape[-2:] for w in ws]
    N = w_shapes[0][1] if w_form == "nn" else w_shapes[0][0]

    def main(ins, outs, scr):
        a_refs = ins[:n_a]
        w_refs = ins[n_a:2 * n_a]
        acc = None
        for a_ref, w_ref in zip(a_refs, w_refs):
            d = _dot(a_ref[...], w_ref[...], w_form)
            acc = d if acc is None else acc + d
        epilogue(acc, ins[2 * n_a:], outs)

    kind_spec = {"tile": pl.BlockSpec((tm, N), lambda i: (i, 0)), "vec": pl.BlockSpec((1, N), lambda i: (0, 0))}
    kind_shape = {"tile": (T, N), "vec": (1, N)}
    a_specs = [pl.BlockSpec((tm, a.shape[1]), lambda i: (i, 0)) for a in as_]
    w_specs = []
    for w, s in zip(ws, w_sel):
        if w.ndim == 3:
            w_specs.append(pl.BlockSpec((None,) + tuple(w.shape[1:]), functools.partial(lambda i, s: (s, 0, 0), s=s),
                                        pipeline_mode=pl.Buffered(1)))
        else:
            w_specs.append(pl.BlockSpec(tuple(w.shape), lambda i: (0, 0), pipeline_mode=pl.Buffered(1)))
    args = list(as_) + list(ws) + [e for e, _ in extras]
    return _call(name, main, (T // tm,), a_specs + w_specs + [kind_spec[k] for _, k in extras],
                 [kind_spec[k] for _, k in outs_desc],
                 [jax.ShapeDtypeStruct(kind_shape[k], dt) for dt, k in outs_desc], args, comm=comm)


def _plain_epilogue(acc, ex, outs):
    outs[0][...] = acc.astype(outs[0].dtype)


def _residual_rms_epilogue(scale):
    def ep(acc, ex, outs):
        xv = ex[0][...] + scale * acc
        outs[0][...] = xv
        r = lax.rsqrt(jnp.mean(xv * xv, axis=-1, keepdims=True) + RMS_EPS)
        outs[1][...] = (xv * r * ex[1][...]).astype(outs[1].dtype)
    return ep


def _rms_bwd_epilogue(out_scale):
    def ep(acc, ex, outs):
        x_ref, g_ref, dres_ref = ex
        dx_ref, dxb_ref, dg_ref = outs
        xv = x_ref[...]
        r = lax.rsqrt(jnp.mean(xv * xv, axis=-1, keepdims=True) + RMS_EPS)
        xhat = xv * r

        @pl.when(pl.program_id(0) == 0)
        def _():
            dg_ref[...] = jnp.zeros_like(dg_ref)

        dg_ref[...] += jnp.sum(acc * xhat, axis=0, keepdims=True)
        dxhat = acc * g_ref[...]
        dx = r * (dxhat - xhat * jnp.mean(dxhat * xhat, axis=-1, keepdims=True)) + dres_ref[...]
        dx_ref[...] = dx
        dxb_ref[...] = (out_scale * dx).astype(dxb_ref.dtype)
    return ep


def _mm_tn(name, a, b, tmo, tno, out_dtype, tk=1024, comm=None):
    T, Ma = a.shape
    Nb = b.shape[1]
    tk = min(tk, T)
    nk = T // tk

    def main(ins, outs, scr):
        a_ref, b_ref = ins
        (acc_ref,) = scr
        k = pl.program_id(2)

        @pl.when(k == 0)
        def _():
            acc_ref[...] = jnp.zeros_like(acc_ref)

        acc_ref[...] += _dot(a_ref[...], b_ref[...], "tn")

        @pl.when(k == nk - 1)
        def _():
            outs[0][...] = acc_ref[...].astype(outs[0].dtype)

    (out,), extra = _call(
        name, main, (Ma // tmo, Nb // tno, nk),
        [pl.BlockSpec((tk, tmo), lambda i, j, k: (k, i)), pl.BlockSpec((tk, tno), lambda i, j, k: (k, j))],
        [pl.BlockSpec((tmo, tno), lambda i, j, k: (i, j))], [jax.ShapeDtypeStruct((Ma, Nb), out_dtype)],
        (a, b), scratch=[pltpu.VMEM((tmo, tno), F32)], comm=comm)
    return out, extra


def _swiglu_parts(g, u):
    s = _sigmoid(g)
    silu = g * s
    return [u * (s + silu * (1.0 - s)), silu, silu * u]


def _silu_mul_epilogue(dots, comps):
    g, u = dots
    return _swiglu_parts(g, u)


def _up_act_epilogue(dots, comps):
    (u,) = dots
    return _swiglu_parts(comps[0].astype(F32), u)


def _dact_epilogue(dots, comps):
    dact = dots[0].astype(BF16)
    return [dact * comps[0], dact * comps[1]]


def _identity_epilogue(dots, comps):
    return list(dots)


def _swap_halves(x):
    lane = lax.broadcasted_iota(jnp.int32, x.shape, 1)
    first = (lane % DK) < (DK // 2)
    return jnp.where(first, pltpu.roll(x, 128 - DK // 2, 1), pltpu.roll(x, DK // 2, 1))


def _rotary(t, cos, sin_signed):
    halves = []
    for p in range(QK_W // 128):
        th = t[:, 128 * p:128 * (p + 1)]
        halves.append(th * cos + _swap_halves(th) * sin_signed)
    return jnp.concatenate(halves, axis=1)


def _rotary_transposed(d, cos, sin_signed):
    halves = []
    for p in range(QK_W // 128):
        dh = d[:, 128 * p:128 * (p + 1)]
        halves.append(dh * cos + _swap_halves(dh * sin_signed))
    return jnp.concatenate(halves, axis=1)


def _log_sigmoid(x):
    return jnp.minimum(x, 0.0) - jnp.log(1.0 + jnp.exp(-jnp.abs(x)))


def _attn_masks():
    row = lax.broadcasted_iota(jnp.int32, (SUPER, SUPER), 0)
    col = lax.broadcasted_iota(jnp.int32, (SUPER, SUPER), 1)
    same = (row // CHUNK) == (col // CHUNK)
    return row, col, same


def _group_inputs(grp, pr, cos, sin_signed, lg, wa2, ba):
    if grp == 0:
        q = _rotary(pr[:, C_RQ:C_RQ + QK_W], cos, sin_signed)
        k = _rotary(pr[:, C_RK:C_RK + QK_W], cos, sin_signed) * (DK ** -0.5)
        v = pr[:, C_RV:C_RV + V_W]
        gate = pr[:, C_RG:C_RG + V_W]
        pos = lax.broadcasted_iota(jnp.int32, (SUPER, QK_W), 0).astype(F32) + 1.0
        return q, k, v, gate, pos * lg, None, None
    q = pr[:, C_GQ:C_GQ + QK_W] * (DK ** -0.5)
    k = pr[:, C_GK:C_GK + QK_W]
    v = pr[:, C_GV:C_GV + V_W]
    gate = pr[:, C_GG:C_GG + V_W]
    glow = pr[:, C_GL:C_GL + GL_W]
    logit = _dot(glow.astype(BF16), wa2.astype(BF16), "nn") + ba
    la = _log_sigmoid(logit) * (1.0 / GATE_NORM)
    row, col, _ = _attn_masks()
    lower = (col <= row).astype(F32)
    b_cum = _dot(lower, la, "nn", precision=lax.Precision.HIGHEST)
    return q, k, v, gate, b_cum, glow, logit


def _decay_factors(q, k, b_cum):
    c = b_cum[SUPER // 2 - 1:SUPER // 2, :]
    bl = b_cum[SUPER - 1:SUPER, :]
    e1 = jnp.exp(b_cum - c)
    e2 = jnp.exp(c - b_cum)
    e_b = jnp.exp(b_cum)
    e_l = jnp.exp(bl - b_cum)
    return dict(e1=e1, e2=e2, eb=e_b, el=e_l, ebl=jnp.exp(bl),
                qp=q * e1, qm=q * e2, kp=k * e1, km=k * e2, qs=q * e_b, kl=k * e_l)


def _state_block_mask():
    r = lax.broadcasted_iota(jnp.int32, (V_W, QK_W), 0)
    c = lax.broadcasted_iota(jnp.int32, (V_W, QK_W), 1)
    return (r // DV) == (c // DK)


def _attn_fwd(proj, cos, sin_signed, lg, wa2p, ba, gn_ret, gn_gla, x_res, w_out, g_next, comm=None):
    T = proj.shape[0]
    n_s = T // SUPER
    D = x_res.shape[1]

    def main(ins, outs, scr):
        pr_ref, cos_ref, sin_ref, lg_ref, wa2_ref, ba_ref, gr_ref, gg_ref, xres_ref, wout_ref, gnext_ref = ins
        o_ref, y_ref, st_ref, x_ref, h_ref = outs
        (s_ref,) = scr
        i = pl.program_id(0)

        @pl.when(i == 0)
        def _():
            s_ref[...] = jnp.zeros_like(s_ref)

        pr = pr_ref
        row, col, same = _attn_masks()
        m1 = col <= row
        m2 = jnp.logical_and(col > row, same)
        lane = lax.broadcasted_iota(jnp.int32, (1, QK_W), 1)
        blockmask = _state_block_mask()
        for grp in range(2):
            q, k, v, gate, b_cum, _, _ = _group_inputs(grp, pr, cos_ref[...], sin_ref[...], lg_ref[...],
                                                      wa2_ref[...], ba_ref[...])
            f = _decay_factors(q, k, b_cum)
            gn = gr_ref[...] if grp == 0 else gg_ref[...]
            s_prev = s_ref[grp]
            st_ref[0, grp] = s_prev
            o_inter = _dot(f["qs"].astype(BF16), s_prev.astype(BF16), "nt")
            kmb = f["km"].astype(BF16)
            kpb = f["kp"].astype(BF16)
            vb = v.astype(BF16)
            for h in range(HEADS):
                hm = (lane // DK) == h
                a1 = _dot(jnp.where(hm, f["qp"], 0.0).astype(BF16), kmb, "nt")
                a2 = _dot(jnp.where(hm, f["qm"], 0.0).astype(BF16), kpb, "nt")
                a = jnp.where(m1, a1, jnp.where(m2, a2, 0.0))
                lo = grp * V_W + h * DV
                o_h = _dot(a.astype(BF16), vb[:, h * DV:(h + 1) * DV], "nn") + o_inter[:, h * DV:(h + 1) * DV]
                o_ref[:, lo:lo + DV] = o_h
                r = lax.rsqrt(jnp.mean(o_h * o_h, axis=-1, keepdims=True) + RMS_EPS)
                gte = gate[:, h * DV:(h + 1) * DV]
                y = o_h * r * gn[:, h * DV:(h + 1) * DV] * (gte * _sigmoid(gte))
                y_ref[:, lo:lo + DV] = y.astype(y_ref.dtype)
            upd = _dot(vb, f["kl"].astype(BF16), "tn")
            s_ref[grp] = s_prev * f["ebl"] + jnp.where(blockmask, upd, 0.0)
        xv = xres_ref[...] + _dot(y_ref[...], wout_ref[...], "nn")
        x_ref[...] = xv
        r = lax.rsqrt(jnp.mean(xv * xv, axis=-1, keepdims=True) + RMS_EPS)
        h_ref[...] = (xv * r * gnext_ref[...]).astype(h_ref.dtype)

    const = lambda shape: pl.BlockSpec(shape, lambda i: tuple(0 for _ in shape))
    rows = lambda w: pl.BlockSpec((SUPER, w), lambda i: (i, 0))
    return _call(
        "attn_fwd", main, (n_s,),
        [rows(PROJ_W), rows(128), rows(128),
         const((1, QK_W)), const((GL_W, QK_W)), const((1, QK_W)), const((1, V_W)), const((1, V_W)),
         rows(D), const((2 * V_W, D)), const((1, D))],
        [rows(2 * V_W), rows(2 * V_W), pl.BlockSpec((1, 2, V_W, QK_W), lambda i: (i, 0, 0, 0)), rows(D), rows(D)],
        [jax.ShapeDtypeStruct((T, 2 * V_W), F32), jax.ShapeDtypeStruct((T, 2 * V_W), BF16),
         jax.ShapeDtypeStruct((n_s, 2, V_W, QK_W), F32), jax.ShapeDtypeStruct((T, D), F32),
         jax.ShapeDtypeStruct((T, D), BF16)],
        (proj, cos, sin_signed, lg, wa2p, ba, gn_ret, gn_gla, x_res, w_out, g_next),
        scratch=[pltpu.VMEM((2, V_W, QK_W), F32)], comm=comm)


def _attn_bwd(proj, cos, sin_signed, lg, wa2p, ba, gn_ret, gn_gla, o, dx, w_out, states, comm=None):
    T = proj.shape[0]
    n_s = T // SUPER
    D = dx.shape[1]

    def main(ins, outs, scr):
        pr_ref, cos_ref, sin_ref, lg_ref, wa2_ref, ba_ref, gr_ref, gg_ref, o_ref, dx_ref, wout_ref, st_ref = ins
        dp_ref, dgr_ref, dgg_ref, dba_ref, dwa_ref = outs
        (ds_ref, dy_ref) = scr
        i = pl.program_id(0)
        dy_ref[...] = _dot(dx_ref[...], wout_ref[...], "nt")

        @pl.when(i == 0)
        def _():
            ds_ref[...] = jnp.zeros_like(ds_ref)
            dgr_ref[...] = jnp.zeros_like(dgr_ref)
            dgg_ref[...] = jnp.zeros_like(dgg_ref)
            dba_ref[...] = jnp.zeros_like(dba_ref)
            dwa_ref[...] = jnp.zeros_like(dwa_ref)

        pr = pr_ref
        cos = cos_ref[...]
        sin_signed = sin_ref[...]
        row, col, same = _attn_masks()
        m1 = col <= row
        m2 = jnp.logical_and(col > row, same)
        m1t = row <= col
        m2t = jnp.logical_and(row > col, same)
        lane = lax.broadcasted_iota(jnp.int32, (1, QK_W), 1)
        blockmask = _state_block_mask()
        for grp in range(2):
            q, k, v, gate, b_cum, glow, logit = _group_inputs(grp, pr, cos, sin_signed, lg_ref[...],
                                                              wa2_ref[...], ba_ref[...])
            f = _decay_factors(q, k, b_cum)
            gn = gr_ref[...] if grp == 0 else gg_ref[...]
            dgn_ref = dgr_ref if grp == 0 else dgg_ref
            do_parts, dgate_parts, dgn_parts = [], [], []
            for h in range(HEADS):
                lo = grp * V_W + h * DV
                o_h = o_ref[:, lo:lo + DV]
                r = lax.rsqrt(jnp.mean(o_h * o_h, axis=-1, keepdims=True) + RMS_EPS)
                n = o_h * r
                gte = gate[:, h * DV:(h + 1) * DV]
                sg = _sigmoid(gte)
                dy_h = dy_ref[:, lo:lo + DV]
                gn_h = gn[:, h * DV:(h + 1) * DV]
                dgate_parts.append(dy_h * n * gn_h * (sg * (1.0 + gte * (1.0 - sg))))
                dz = dy_h * (gte * sg)
                dgn_parts.append(jnp.sum(dz * n, axis=0, keepdims=True))
                dn = dz * gn_h
                do_parts.append(r * (dn - n * jnp.mean(dn * n, axis=-1, keepdims=True)))
            dgn_ref[...] += jnp.concatenate(dgn_parts, axis=1)
            dgate = jnp.concatenate(dgate_parts, axis=1)
            do = jnp.concatenate(do_parts, axis=1)
            dob = do.astype(BF16)
            vb = v.astype(BF16)
            s_prev = st_ref[0, grp]
            ds_new = ds_ref[grp]
            dsb = ds_new.astype(BF16)
            qpb, qmb = f["qp"].astype(BF16), f["qm"].astype(BF16)
            kpb, kmb = f["kp"].astype(BF16), f["km"].astype(BF16)
            dqp = jnp.zeros((SUPER, QK_W), F32)
            dqm = jnp.zeros((SUPER, QK_W), F32)
            dkp = jnp.zeros((SUPER, QK_W), F32)
            dkm = jnp.zeros((SUPER, QK_W), F32)
            dv_parts = []
            for h in range(HEADS):
                hm = (lane // DK) == h
                qp_h = jnp.where(hm, f["qp"], 0.0).astype(BF16)
                qm_h = jnp.where(hm, f["qm"], 0.0).astype(BF16)
                kp_h = jnp.where(hm, f["kp"], 0.0).astype(BF16)
                km_h = jnp.where(hm, f["km"], 0.0).astype(BF16)
                at = jnp.where(m1t, _dot(km_h, qpb, "nt"), jnp.where(m2t, _dot(kp_h, qmb, "nt"), 0.0))
                do_h = dob[:, h * DV:(h + 1) * DV]
                v_h = vb[:, h * DV:(h + 1) * DV]
                dv_parts.append(_dot(at.astype(BF16), do_h, "nn"))
                da = _dot(do_h, v_h, "nt")
                dat = _dot(v_h, do_h, "nt")
                da1 = jnp.where(m1, da, 0.0).astype(BF16)
                da2 = jnp.where(m2, da, 0.0).astype(BF16)
                da1t = jnp.where(m1t, dat, 0.0).astype(BF16)
                da2t = jnp.where(m2t, dat, 0.0).astype(BF16)
                dqp = dqp + _dot(da1, km_h, "nn")
                dqm = dqm + _dot(da2, kp_h, "nn")
                dkm = dkm + _dot(da1t, qp_h, "nn")
                dkp = dkp + _dot(da2t, qm_h, "nn")
            klb = f["kl"].astype(BF16)
            qsb = f["qs"].astype(BF16)
            dqs = _dot(dob, s_prev.astype(BF16), "nn")
            dkl = _dot(vb, dsb, "nn")
            dv = jnp.concatenate(dv_parts, axis=1) + _dot(klb, dsb, "nt")
            ds_ref[grp] = ds_new * f["ebl"] + jnp.where(blockmask, _dot(dob, qsb, "tn"), 0.0)
            dq = dqp * f["e1"] + dqm * f["e2"] + dqs * f["eb"]
            dk = dkm * f["e2"] + dkp * f["e1"] + dkl * f["el"]
            if grp == 0:
                dq = _rotary_transposed(dq, cos, sin_signed)
                dk = _rotary_transposed(dk * (DK ** -0.5), cos, sin_signed)
                dp_ref[:, C_RQ:C_RQ + QK_W] = dq.astype(dp_ref.dtype)
                dp_ref[:, C_RK:C_RK + QK_W] = dk.astype(dp_ref.dtype)
                dp_ref[:, C_RV:C_RV + V_W] = dv.astype(dp_ref.dtype)
                dp_ref[:, C_RG:C_RG + V_W] = dgate.astype(dp_ref.dtype)
            else:
                dkl_kl = dkl * klb.astype(F32)
                db = (dqp * qpb.astype(F32) - dkm * kmb.astype(F32) - dqm * qmb.astype(F32)
                      + dkp * kpb.astype(F32) + dqs * qsb.astype(F32) - dkl_kl)
                last = (jnp.sum(dkl_kl, axis=0, keepdims=True)
                        + f["ebl"] * jnp.sum(s_prev * ds_new, axis=0, keepdims=True))
                rowq = lax.broadcasted_iota(jnp.int32, (SUPER, QK_W), 0)
                db = db + jnp.where(rowq == SUPER - 1, last, 0.0)
                upper = (col >= row).astype(F32)
                dla = _dot(upper, db, "nn", precision=lax.Precision.HIGHEST)
                dlogit = dla * (1.0 / GATE_NORM) * (1.0 - _sigmoid(logit))
                dlb = dlogit.astype(BF16)
                dglow = _dot(dlb, wa2_ref[...].astype(BF16), "nt")
                dwa_ref[...] += _dot(glow.astype(BF16), dlb, "tn")
                dba_ref[...] += jnp.sum(dlogit, axis=0, keepdims=True)
                dp_ref[:, C_GQ:C_GQ + QK_W] = (dq * (DK ** -0.5)).astype(dp_ref.dtype)
                dp_ref[:, C_GK:C_GK + QK_W] = dk.astype(dp_ref.dtype)
                dp_ref[:, C_GV:C_GV + V_W] = dv.astype(dp_ref.dtype)
                dp_ref[:, C_GG:C_GG + V_W] = dgate.astype(dp_ref.dtype)
                dp_ref[:, C_GL:C_GL + GL_W] = dglow.astype(dp_ref.dtype)

    rev = lambda i: n_s - 1 - i
    const = lambda shape: pl.BlockSpec(shape, lambda i: tuple(0 for _ in shape))
    return _call(
        "attn_bwd", main, (n_s,),
        [pl.BlockSpec((SUPER, PROJ_W), lambda i: (rev(i), 0)),
         pl.BlockSpec((SUPER, 128), lambda i: (rev(i), 0)), pl.BlockSpec((SUPER, 128), lambda i: (rev(i), 0)),
         const((1, QK_W)), const((GL_W, QK_W)), const((1, QK_W)), const((1, V_W)), const((1, V_W)),
         pl.BlockSpec((SUPER, 2 * V_W), lambda i: (rev(i), 0)),
         pl.BlockSpec((SUPER, D), lambda i: (rev(i), 0)), const((2 * V_W, D)),
         pl.BlockSpec((1, 2, V_W, QK_W), lambda i: (rev(i), 0, 0, 0))],
        [pl.BlockSpec((SUPER, PROJ_W), lambda i: (rev(i), 0)),
         const((1, V_W)), const((1, V_W)), const((1, QK_W)), const((GL_W, QK_W))],
        [jax.ShapeDtypeStruct((T, PROJ_W), BF16),
         jax.ShapeDtypeStruct((1, V_W), F32), jax.ShapeDtypeStruct((1, V_W), F32),
         jax.ShapeDtypeStruct((1, QK_W), F32), jax.ShapeDtypeStruct((GL_W, QK_W), F32)],
        (proj, cos, sin_signed, lg, wa2p, ba, gn_ret, gn_gla, o, dx, w_out, states),
        scratch=[pltpu.VMEM((2, V_W, QK_W), F32), pltpu.VMEM((SUPER, 2 * V_W), F32)], comm=comm)


def _rotary_tables(T):
    half = DK // 2
    inv = ROPE_BASE ** (-jnp.arange(half, dtype=F32) * 2.0 / DK)
    ang = jnp.arange(T, dtype=F32)[:, None] * inv[None, :]
    cos, sin = jnp.cos(ang), jnp.sin(ang)
    cos_head = jnp.concatenate([cos, cos], axis=1)
    sin_head = jnp.concatenate([-sin, sin], axis=1)
    return jnp.tile(cos_head, (1, 128 // DK)), jnp.tile(sin_head, (1, 128 // DK))


def _sum_devices(name, gathered, m_per):
    def body(g_ref, o_ref):
        acc = g_ref[0:m_per, :]
        for k in range(1, N_DEV):
            acc = acc + g_ref[k * m_per:(k + 1) * m_per, :]
        o_ref[...] = acc

    return pl.pallas_call(body, name=name, out_shape=jax.ShapeDtypeStruct((m_per, 128), F32))(gathered)


def _pair_sum(name, grad, landed, core, comm=None):
    R, C = grad.shape[2:]

    def main(ins, outs, scr):
        outs[0][...] = (ins[0][...].astype(F32) + ins[1][...].astype(F32)).astype(outs[0].dtype)

    blk = pl.BlockSpec((None, R, C), lambda j, s: (j, 0, 0))
    (out,), extra = _call(name, main, (4,), [pl.BlockSpec((None, None, R, C), lambda j, s: (j, s[0], 0, 0)), blk],
                          [blk], [jax.ShapeDtypeStruct((4, R, C), BF16)], (grad, landed), prefetch=core, comm=comm)
    return out, extra


def _owner_sums(name, items, owner, comm=None):
    counts = [1 + len(landed) for _, landed in items]

    def main(ins, outs, scr):
        at = 0
        for o_ref, n in zip(outs, counts):
            acc = ins[at][...].astype(F32)
            for l_ref in ins[at + 1:at + n]:
                for j in range(l_ref.shape[0]):
                    acc = acc + l_ref[j].astype(F32)
            o_ref[...] = acc
            at += n

    once = pl.Buffered(1)
    in_specs, out_specs, out_shape, args = [], [], [], []
    for grad, landed in items:
        R, C = grad.shape[-2:]
        if grad.ndim == 4:
            in_specs.append(pl.BlockSpec((None, None, R, C), lambda i, s: (s[0], s[1], 0, 0), pipeline_mode=once))
        else:
            in_specs.append(pl.BlockSpec((None, R, C), lambda i, s: (s[0], 0, 0), pipeline_mode=once))
        in_specs += [pl.BlockSpec(tuple(l.shape), lambda i, s: (0, 0, 0), pipeline_mode=once) for l in landed]
        out_specs.append(pl.BlockSpec((R, C), lambda i, s: (0, 0)))
        out_shape.append(jax.ShapeDtypeStruct((R, C), F32))
        args += [grad, *landed]
    return _call(name, main, (1,), in_specs, out_specs, out_shape, args, comm=comm, prefetch=owner)


def _rms_bwd(name, x, g, dh, dres, out_scale):
    T, D = x.shape
    tm = min(T, 512)
    ep = _rms_bwd_epilogue(out_scale)

    def main(ins, outs, scr):
        x_ref, g_ref, dh_ref, dres_ref = ins
        ep(dh_ref[...], (x_ref, g_ref, dres_ref), outs)

    tile = pl.BlockSpec((tm, D), lambda i: (i, 0))
    vec = pl.BlockSpec((1, D), lambda i: (0, 0))
    outs, _ = _call(name, main, (T // tm,), [tile, vec, tile, tile], [tile, tile, vec],
                    [jax.ShapeDtypeStruct((T, D), F32), jax.ShapeDtypeStruct((T, D), BF16),
                     jax.ShapeDtypeStruct((1, D), F32)], (x, g, dh, dres))
    return outs


def _adamw_group(name, items, n_blocks, comm=None):
    n = len(items)

    def main(ins, outs, scr):
        for p in range(n):
            g_ref, w_ref, m_ref, v_ref = ins[4 * p:4 * p + 4]
            d_ref, nm_ref, nv_ref = outs[3 * p:3 * p + 3]
            gv = g_ref[...]
            nm = ADAM_B1 * m_ref[...] + (1.0 - ADAM_B1) * gv
            nv = ADAM_B2 * v_ref[...] + (1.0 - ADAM_B2) * (gv * gv)
            m_hat = nm / (1.0 - ADAM_B1 ** ADAM_STEP)
            v_hat = nv / (1.0 - ADAM_B2 ** ADAM_STEP)
            d_ref[...] = -ADAM_LR * (m_hat / (jnp.sqrt(v_hat) + ADAM_EPS) + ADAM_WD * w_ref[...])
            nm_ref[...] = nm
            nv_ref[...] = nv

    in_specs, out_specs, out_shape, args = [], [], [], []
    for item in items:
        R, C = item[1].shape
        assert R % n_blocks == 0
        spec = pl.BlockSpec((R // n_blocks, C), lambda i: (i, 0))
        in_specs += [spec] * 4
        out_specs += [spec] * 3
        out_shape += [jax.ShapeDtypeStruct((R, C), F32)] * 3
        args += list(item)
    outs, extra = _call(name, main, (n_blocks,), in_specs, out_specs, out_shape, args, comm=comm)
    return [tuple(outs[3 * p:3 * p + 3]) for p in range(n)], extra


SMALL_ORDER = ("ffn1", "mix", "ffn2", "final", "ret", "gla", "b_a")


def kernel(x, ffn1_norm_g, ffn1_w_gate, ffn1_w_up, ffn1_w_down, mix_norm_g, w_in, ret_norm_g, gla_w_a2, gla_b_a, gla_norm_g, w_out, ffn2_norm_g, ffn2_w_gate, ffn2_w_up, ffn2_w_down, final_norm_g, loss_target, m_ffn1_norm_g, m_ffn1_w_gate, m_ffn1_w_up, m_ffn1_w_down, m_mix_norm_g, m_w_in, m_ret_norm_g, m_gla_w_a2, m_gla_b_a, m_gla_norm_g, m_w_out, m_ffn2_norm_g, m_ffn2_w_gate, m_ffn2_w_up, m_ffn2_w_down, m_final_norm_g, v_ffn1_norm_g, v_ffn1_w_gate, v_ffn1_w_up, v_ffn1_w_down, v_mix_norm_g, v_w_in, v_ret_norm_g, v_gla_w_a2, v_gla_b_a, v_gla_norm_g, v_w_out, v_ffn2_norm_g, v_ffn2_w_gate, v_ffn2_w_up, v_ffn2_w_down, v_final_norm_g):
    xi, yi, ci = _coords()
    dev = 4 * xi + 2 * yi + ci
    owner = jnp.stack([2 * xi + yi, ci]).astype(jnp.int32)

    x0, target = x[0], loss_target[0]
    T, D = x0.shape
    fb = ffn1_w_gate.shape[2]
    ib = w_in.shape[2]
    ab = gla_w_a2.shape[2]
    F = N_DEV * fb
    cos, sin_signed = _rotary_tables(T)
    lg = jnp.repeat(jnp.log(1.0 - 2.0 ** (-5.0 - jnp.arange(HEADS, dtype=F32))), DK)[None, :]
    g_final = final_norm_g.reshape(1, D)

    g1_loc = ffn1_w_gate[0].T[None].astype(BF16)
    u1_loc = ffn1_w_up[0].T[None].astype(BF16)
    d1_loc = ffn1_w_down.astype(BF16)
    g2_loc = ffn2_w_gate[0].T[None].astype(BF16)
    u2_loc = ffn2_w_up[0].T[None].astype(BF16)
    d2_loc = ffn2_w_down.astype(BF16)
    in_loc = w_in[0].T.astype(BF16)
    out_loc = w_out[0].astype(BF16)

    h1, (g1,) = _rms_fwd("ffn1_rms", x0, ffn1_norm_g, comm=_AllGather([g1_loc], ["stack"]))
    g1 = g1.reshape(1, F, D)
    (gate1,), (u1,) = _mm_nstream("ffn1_gate", h1, [g1], [0], "nt", [], [BF16], _identity_epilogue, cn=256,
                                  comm=_AllGather([u1_loc], ["stack"]))
    u1 = u1.reshape(1, F, D)
    (dsu1, sl1, act1), (d1,) = _mm_nstream("ffn1_up", h1, [u1], [0], "nt", [gate1], [BF16, BF16, BF16],
                                           _up_act_epilogue, cn=256, comm=_AllGather([d1_loc], ["stack"]))
    d1 = d1.reshape(1, F, D)
    f32_tile, bf16_tile, f32_vec = (F32, "tile"), (BF16, "tile"), (F32, "vec")
    (x1, h2), (in_all, a_all) = _mm_mstream(
        "ffn1_down", [act1], [d1], [0], "nn", [(x0, "tile"), (mix_norm_g, "vec")], [f32_tile, bf16_tile],
        _residual_rms_epilogue(0.5), comm=_AllGather([in_loc, gla_w_a2[0]], ["plain", "plain"]))
    w_in_t = jnp.pad(in_all.reshape(1, N_DEV * ib, D), ((0, 0), (0, PROJ_W - N_DEV * ib), (0, 0)))
    wa2 = jnp.transpose(a_all, (1, 0, 2)).reshape(GATE_RANK, N_DEV * ab)
    wa2p = jnp.pad(wa2, ((0, GL_W - GATE_RANK), (0, 0)))

    (proj,), (g2, out_all) = _mm_nstream("mix_proj", h2, [w_in_t], [0], "nt", [], [F32], _identity_epilogue, cn=640,
                                         comm=_AllGather([g2_loc, out_loc], ["stack", "plain"]))
    w_out_full = out_all.reshape(D, D)
    (o, ymix, states, x2, h3), (u2,) = _attn_fwd(proj, cos, sin_signed, lg, wa2p, gla_b_a, ret_norm_g, gla_norm_g,
                                                 x1, w_out_full, ffn2_norm_g, comm=_AllGather([u2_loc], ["stack"]))
    g2, u2 = g2.reshape(1, F, D), u2.reshape(1, F, D)

    (dsu2, sl2, act2), (d2,) = _mm_nstream(
        "ffn2_up", h3, [g2, u2], [0, 0], "nt", [], [BF16, BF16, BF16], _silu_mul_epilogue, cn=256,
        comm=_AllGather([d2_loc], ["stack"]))
    d2 = d2.reshape(1, F, D)
    (dx3, dy3b, d_final, loss), _ = _mm_mstream(
        "ffn2_down", [act2], [d2], [0], "nn", [(x2, "tile"), (g_final, "vec"), (target, "tile")],
        [f32_tile, bf16_tile, f32_vec, f32_vec], _final_loss_epilogue(0.5, 0.5))

    dwd2, _ = _mm_tn("ffn2b_dwd", act2, dy3b, F // 2, D, BF16)
    dwd2 = dwd2.reshape(4, 2, fb, D)
    (dgate2, dup2), (l_wd2_near,) = _mm_nstream("ffn2b_dact", dy3b, [d2], [0], "nt", [dsu2, sl2], [BF16, BF16],
                                                _dact_epilogue, cn=256, comm=_ReduceScatter([(dwd2, NEAR)]))
    dwg2, (l_wd2_far,) = _mm_tn("ffn2b_dwg", dgate2, h3, F // 2, D, BF16, comm=_ReduceScatter([(dwd2, FAR)]))
    dwg2 = dwg2.reshape(4, 2, fb, D)
    dwu2, (l_wg2_near,) = _mm_tn("ffn2b_dwu", dup2, h3, F // 2, D, BF16, comm=_ReduceScatter([(dwg2, NEAR)]))
    dwu2 = dwu2.reshape(4, 2, fb, D)
    rms_outs = [f32_tile, bf16_tile, f32_vec]
    (dx2, dx2b, d_g2), (l_wg2_far, l_wu2_near) = _mm_mstream(
        "ffn2b_dh", [dgate2, dup2], [g2, u2], [0, 0], "nn", [(x2, "tile"), (ffn2_norm_g, "vec"), (dx3, "tile")],
        rms_outs, _rms_bwd_epilogue(1.0), comm=_ReduceScatter([(dwg2, FAR), (dwu2, NEAR)]))

    dwout, _ = _mm_tn("mixb_dwout", ymix, dx2b, D, D, BF16)
    dwout = dwout.reshape(4, 2, D // N_DEV, D)
    (dproj, d_ret, d_gla, d_ba, d_wa2p), (l_wu2_far, l_wout) = _attn_bwd(
        proj, cos, sin_signed, lg, wa2p, gla_b_a, ret_norm_g, gla_norm_g, o, dx2b, w_out_full, states,
        comm=_ReduceScatter([(dwu2, FAR), (dwout, ALL)]))
    dwin_t, _ = _mm_tn("mixb_dwin", dproj, h2, 640, D, BF16, tk=2048)
    dwin = dwin_t[:N_DEV * ib].reshape(4, 2, ib, D)
    (dx1, dy1b, d_gmix), (l_win_near,) = _mm_mstream(
        "mixb_dh", [dproj], [w_in_t], [0], "nn", [(x1, "tile"), (mix_norm_g, "vec"), (dx2, "tile")],
        rms_outs, _rms_bwd_epilogue(0.5), comm=_ReduceScatter([(dwin, NEAR)]))

    dwd1, (l_win_far,) = _mm_tn("ffn1b_dwd", act1, dy1b, F // 2, D, BF16, comm=_ReduceScatter([(dwin, FAR)]))
    dwd1 = dwd1.reshape(4, 2, fb, D)
    (dgate1, dup1), (l_wd1_near,) = _mm_nstream("ffn1b_dact", dy1b, [d1], [0], "nt", [dsu1, sl1], [BF16, BF16],
                                                _dact_epilogue, cn=256, comm=_ReduceScatter([(dwd1, NEAR)]))
    dwg1, (l_wd1_far,) = _mm_tn("ffn1b_dwg", dgate1, h1, F // 2, D, BF16, comm=_ReduceScatter([(dwd1, FAR)]))
    dwg1 = dwg1.reshape(4, 2, fb, D)
    core = owner[1:2]
    dwu1, (s_wg1,) = _mm_tn("ffn1b_dwu", dup1, h1, F // 2, D, BF16, comm=_SiblingExchange([dwg1]))
    dwu1 = dwu1.reshape(4, 2, fb, D)
    p_wg1, (s_wu1,) = _pair_sum("pair_wg1", dwg1, s_wg1, core, comm=_SiblingExchange([dwu1]))
    p_wu1, _ = _pair_sum("pair_wu1", dwu1, s_wu1, core)
    (dh1,), (c_wg1, c_wu1) = _mm_mstream(
        "ffn1b_dh", [dgate1, dup1], [g1, u1], [0, 0], "nn", [], [f32_tile], _plain_epilogue,
        comm=_ChipExchange([p_wg1, p_wu1]))
    dx0, _, d_g1 = _rms_bwd("ffn1b_rms", x0, ffn1_norm_g, dh1, dx1, 1.0)

    small = dict(ffn1=d_g1, mix=d_gmix, ffn2=d_g2, final=d_final, ret=d_ret, gla=d_gla, b_a=d_ba)
    flat = jnp.concatenate([small[k].reshape(-1) for k in SMALL_ORDER]
                           + [d_wa2p[:GATE_RANK].reshape(-1), loss[0, :128]])
    rows = -(-flat.shape[0] // 128)
    rows = -(-rows // 8) * 8
    packed = jnp.pad(flat, (0, rows * 128 - flat.shape[0])).reshape(rows, 128)

    transposed = ("ffn1_w_gate", "ffn1_w_up", "ffn2_w_gate", "ffn2_w_up", "w_in")

    def to_2d(nm, a):
        if nm in transposed:
            return a[0].T
        return a.reshape((1, a.shape[0]) if a.ndim == 1 else a.shape[-2:])

    def from_2d(nm, a):
        return a.T[None] if nm in transposed else a.reshape(params[nm][0].shape)

    sums_a, (gathered,) = _owner_sums(
        "sum_a", [(dwg2, [l_wg2_near, l_wg2_far]), (dwu2, [l_wu2_near, l_wu2_far]), (dwd2, [l_wd2_near, l_wd2_far]),
                  (dwin, [l_win_near, l_win_far]), (dwout, [l_wout])], owner,
        comm=_AllGather([packed], ["plain"]))
    sums_b, _ = _owner_sums("sum_b", [(p_wg1, [c_wg1]), (p_wu1, [c_wu1]), (dwd1, [l_wd1_near, l_wd1_far])], owner)
    big_grads = {"ffn2_w_gate": sums_a[0], "ffn2_w_up": sums_a[1], "ffn2_w_down": sums_a[2], "w_out": sums_a[4],
                 "ffn1_w_gate": sums_b[0], "ffn1_w_up": sums_b[1], "ffn1_w_down": sums_b[2]}
    params = dict(
        ffn2_w_gate=(ffn2_w_gate, m_ffn2_w_gate, v_ffn2_w_gate), ffn2_w_up=(ffn2_w_up, m_ffn2_w_up, v_ffn2_w_up),
        ffn2_w_down=(ffn2_w_down, m_ffn2_w_down, v_ffn2_w_down), w_in=(w_in, m_w_in, v_w_in),
        w_out=(w_out, m_w_out, v_w_out), ffn1_w_gate=(ffn1_w_gate, m_ffn1_w_gate, v_ffn1_w_gate),
        ffn1_w_up=(ffn1_w_up, m_ffn1_w_up, v_ffn1_w_up), ffn1_w_down=(ffn1_w_down, m_ffn1_w_down, v_ffn1_w_down),
        ffn1_norm_g=(ffn1_norm_g, m_ffn1_norm_g, v_ffn1_norm_g), mix_norm_g=(mix_norm_g, m_mix_norm_g, v_mix_norm_g),
        ret_norm_g=(ret_norm_g, m_ret_norm_g, v_ret_norm_g), gla_w_a2=(gla_w_a2, m_gla_w_a2, v_gla_w_a2),
        gla_b_a=(gla_b_a, m_gla_b_a, v_gla_b_a), gla_norm_g=(gla_norm_g, m_gla_norm_g, v_gla_norm_g),
        ffn2_norm_g=(ffn2_norm_g, m_ffn2_norm_g, v_ffn2_norm_g), final_norm_g=(final_norm_g, m_final_norm_g, v_final_norm_g))
    grads, updates = {}, {}

    def run_adam(name, names, grad_2d, n_blocks):
        items = [(grad_2d[nm],) + tuple(to_2d(nm, a) for a in params[nm]) for nm in names]
        res, _ = _adamw_group(name, items, n_blocks)
        for nm, r in zip(names, res):
            grads[nm] = from_2d(nm, grad_2d[nm])
            updates[nm] = tuple(from_2d(nm, a) for a in r)

    run_adam("adamw_big", list(big_grads), big_grads, 4)
    run_adam("adamw_w_in", ["w_in"], {"w_in": sums_a[3]}, 1)

    total = _sum_devices("sum_small", gathered.reshape(N_DEV * rows, 128), rows).reshape(-1)
    sizes = [small[k].size for k in SMALL_ORDER] + [GATE_RANK * QK_W, 128]
    offs = [0]
    for s in sizes:
        offs.append(offs[-1] + s)
    pieces = [total[offs[i]:offs[i + 1]] for i in range(len(sizes))]
    g_small = {k: pieces[i].reshape(small[k].shape) for i, k in enumerate(SMALL_ORDER)}
    g_wa2_full = pieces[len(SMALL_ORDER)].reshape(GATE_RANK, QK_W)
    g_wa2 = lax.dynamic_slice(g_wa2_full, (0, dev * ab), (GATE_RANK, ab))
    loss_total = pieces[len(SMALL_ORDER) + 1][0]

    small_grads = {"ffn1_norm_g": g_small["ffn1"], "mix_norm_g": g_small["mix"], "ret_norm_g": g_small["ret"],
                   "gla_w_a2": g_wa2, "gla_b_a": g_small["b_a"], "gla_norm_g": g_small["gla"],
                   "ffn2_norm_g": g_small["ffn2"], "final_norm_g": g_small["final"]}
    run_adam("adamw_small", list(small_grads), small_grads, 1)

    order = ("ffn1_norm_g", "ffn1_w_gate", "ffn1_w_up", "ffn1_w_down", "mix_norm_g", "w_in", "ret_norm_g", "gla_w_a2",
             "gla_b_a", "gla_norm_g", "w_out", "ffn2_norm_g", "ffn2_w_gate", "ffn2_w_up", "ffn2_w_down", "final_norm_g")
    return (loss_total, dx0[None], *[grads[nm] for nm in order], *[updates[nm][0] for nm in order],
            *[updates[nm][1] for nm in order], *[updates[nm][2] for nm in order])
```

```python
import functools
import math

import jax
import jax.numpy as jnp
from jax import lax
from jax.experimental import pallas as pl
from jax.experimental.pallas import tpu as pltpu

F32 = jnp.float32
BF16 = jnp.bfloat16
MESH = pl.DeviceIdType.MESH
HBM = pl.BlockSpec(memory_space=pltpu.HBM)

N_DEV = 8
RMS_EPS = 1e-6
ROPE_BASE = 10000.0
HEADS = 4
DK = 64
DV = 128
QK_W = HEADS * DK
V_W = HEADS * DV
GATE_RANK = 16
GATE_NORM = 16.0
CHUNK = 64
SUPER = 256
PROJ_W = 3200
C_RQ, C_RK, C_RV, C_RG, C_GQ, C_GK, C_GV, C_GG, C_GL = 0, 256, 512, 1024, 1536, 1792, 2048, 2560, 3072
GL_W = PROJ_W - C_GL
ADAM_LR, ADAM_B1, ADAM_B2, ADAM_EPS, ADAM_WD, ADAM_STEP = 0.001, 0.9, 0.999, 1e-08, 0.01, 10
VMEM_LIMIT_V7X = 52 * 1024 * 1024


def _cparams(**kw):
    return pltpu.CompilerParams(vmem_limit_bytes=VMEM_LIMIT_V7X, **kw)


def _dot(a, b, form, precision=None):
    dims = {"nn": (((1,), (0,)), ((), ())), "nt": (((1,), (1,)), ((), ())), "tn": (((0,), (0,)), ((), ()))}[form]
    return lax.dot_general(a, b, dims, preferred_element_type=F32, precision=precision)


def _sigmoid(x):
    return 1.0 / (1.0 + jnp.exp(-x))


def _coords():
    return lax.axis_index("x"), lax.axis_index("y"), lax.axis_index("c")


class _NoComm:
    inputs, out_shapes, scratch = (), (), ()


class _AllGather:
    def __init__(self, arrays, kinds):
        self.inputs = tuple(arrays)
        self.kinds = tuple(kinds)
        n = len(arrays)
        self.out_shapes = tuple(
            jax.ShapeDtypeStruct((a.shape[0], N_DEV) + a.shape[1:] if k == "stack" else (N_DEV,) + a.shape, a.dtype)
            for a, k in zip(arrays, kinds))
        self.scratch = (pltpu.SemaphoreType.DMA((n, 7)), pltpu.SemaphoreType.DMA((n, 7)),
                        pltpu.SemaphoreType.DMA((n,)))

    def _ctx(self, srcs, outs, sems):
        send_sems, recv_sems, local_sems = sems
        x, y, c = _coords()
        me, sibling = (x, y, c), (x, y, 1 - c)
        chips = [(1 - x, y), (x, 1 - y), (1 - x, 1 - y)]

        def blk(m, dev):
            k = 4 * dev[0] + 2 * dev[1] + dev[2]
            return outs[m].at[:, k] if self.kinds[m] == "stack" else outs[m].at[k]

        def copy(m, s, block, to, src=None):
            return pltpu.make_async_remote_copy(
                src_ref=blk(m, block) if src is None else src, dst_ref=blk(m, block),
                send_sem=send_sems.at[m, s], recv_sem=recv_sems.at[m, s], device_id=to, device_id_type=MESH)

        def mine(m):
            return pltpu.make_async_copy(srcs[m], blk(m, me), local_sems.at[m])

        def first(m):
            return [copy(m, 0, me, sibling, src=srcs[m])] + [
                copy(m, 1 + j, me, (*chip, c), src=srcs[m]) for j, chip in enumerate(chips)]

        return me, sibling, chips, c, copy, mine, first

    def start(self, srcs, outs, sems):
        me, sibling, chips, c, copy, mine, first = self._ctx(srcs, outs, sems)
        for m in range(len(srcs)):
            mine(m).start()
            for cp in first(m):
                cp.start()

    def mid(self, srcs, outs, sems):
        me, sibling, chips, c, copy, mine, first = self._ctx(srcs, outs, sems)
        for j, chip in enumerate(chips):
            for m in range(len(srcs)):
                copy(m, 1 + j, (*chip, c), me).wait_recv()
                copy(m, 4 + j, (*chip, c), sibling).start()

    def finish(self, srcs, outs, sems):
        me, sibling, chips, c, copy, mine, first = self._ctx(srcs, outs, sems)
        for m in range(len(srcs)):
            copy(m, 0, sibling, me).wait_recv()
            for j, chip in enumerate(chips):
                copy(m, 4 + j, (*chip, 1 - c), me).wait_recv()
            for cp in first(m):
                cp.wait_send()
            for j, chip in enumerate(chips):
                copy(m, 4 + j, (*chip, c), sibling).wait_send()
            mine(m).wait()


RELATIONS = ((0, 0, 1), (1, 0, 0), (0, 1, 0), (1, 1, 0), (1, 0, 1), (0, 1, 1), (1, 1, 1))
NEAR = (0, 1, 2, 4, 5)
FAR = (3, 6)
ALL = NEAR + FAR


class _ReduceScatter:
    def __init__(self, parts):
        self.inputs = tuple(g for g, _ in parts)
        self.slots = tuple(s for _, s in parts)
        self.out_shapes = tuple(jax.ShapeDtypeStruct((len(s),) + g.shape[2:], g.dtype) for g, s in parts)
        n_max = max(len(s) for s in self.slots)
        n = len(parts)
        self.scratch = (pltpu.SemaphoreType.DMA((n, n_max)), pltpu.SemaphoreType.DMA((n, n_max)))

    def _copies(self, srcs, outs, sems):
        send_sems, recv_sems = sems
        x, y, c = _coords()
        copies = []
        for m, slots in enumerate(self.slots):
            for i, s in enumerate(slots):
                fx, fy, fc = RELATIONS[s]
                px = 1 - x if fx else x
                py = 1 - y if fy else y
                pc = 1 - c if fc else c
                copies.append(pltpu.make_async_remote_copy(
                    src_ref=srcs[m].at[2 * px + py, pc], dst_ref=outs[m].at[i], send_sem=send_sems.at[m, i],
                    recv_sem=recv_sems.at[m, i], device_id=(px, py, pc), device_id_type=MESH))
        return copies

    def start(self, srcs, outs, sems):
        for cp in self._copies(srcs, outs, sems):
            cp.start()

    def mid(self, srcs, outs, sems):
        pass

    def finish(self, srcs, outs, sems):
        for cp in self._copies(srcs, outs, sems):
            cp.wait()


class _SiblingExchange:
    def __init__(self, grads):
        self.inputs = tuple(grads)
        self.out_shapes = tuple(jax.ShapeDtypeStruct((4,) + g.shape[2:], g.dtype) for g in grads)
        self.scratch = (pltpu.SemaphoreType.DMA((len(grads),)), pltpu.SemaphoreType.DMA((len(grads),)))

    def _copies(self, srcs, outs, sems):
        send_sems, recv_sems = sems
        x, y, c = _coords()
        return [pltpu.make_async_remote_copy(
            src_ref=srcs[m].at[:, 1 - c], dst_ref=outs[m], send_sem=send_sems.at[m], recv_sem=recv_sems.at[m],
            device_id=(x, y, 1 - c), device_id_type=MESH) for m in range(len(srcs))]

    def start(self, srcs, outs, sems):
        for cp in self._copies(srcs, outs, sems):
            cp.start()

    def mid(self, srcs, outs, sems):
        pass

    def finish(self, srcs, outs, sems):
        for cp in self._copies(srcs, outs, sems):
            cp.wait()


class _ChipExchange:
    def __init__(self, partials):
        self.inputs = tuple(partials)
        self.out_shapes = tuple(jax.ShapeDtypeStruct((3,) + p.shape[1:], p.dtype) for p in partials)
        n = len(partials)
        self.scratch = (pltpu.SemaphoreType.DMA((n, 3)), pltpu.SemaphoreType.DMA((n, 3)))

    def _copies(self, srcs, outs, sems):
        send_sems, recv_sems = sems
        x, y, c = _coords()
        copies = []
        for m in range(len(srcs)):
            for j, (px, py) in enumerate([(1 - x, y), (x, 1 - y), (1 - x, 1 - y)]):
                copies.append(pltpu.make_async_remote_copy(
                    src_ref=srcs[m].at[2 * px + py], dst_ref=outs[m].at[j], send_sem=send_sems.at[m, j],
                    recv_sem=recv_sems.at[m, j], device_id=(px, py, c), device_id_type=MESH))
        return copies

    def start(self, srcs, outs, sems):
        for cp in self._copies(srcs, outs, sems):
            cp.start()

    def mid(self, srcs, outs, sems):
        pass

    def finish(self, srcs, outs, sems):
        for cp in self._copies(srcs, outs, sems):
            cp.wait()


class _Multi:
    def __init__(self, comms):
        self.comms = comms
        self.inputs = tuple(a for c in comms for a in c.inputs)
        self.out_shapes = tuple(s for c in comms for s in c.out_shapes)
        self.scratch = tuple(s for c in comms for s in c.scratch)

    def _each(self, phase, srcs, outs, sems):
        i = o = k = 0
        for c in self.comms:
            ni, no, nk = len(c.inputs), len(c.out_shapes), len(c.scratch)
            getattr(c, phase)(srcs[i:i + ni], outs[o:o + no], sems[k:k + nk])
            i, o, k = i + ni, o + no, k + nk

    def start(self, srcs, outs, sems):
        self._each("start", srcs, outs, sems)

    def mid(self, srcs, outs, sems):
        self._each("mid", srcs, outs, sems)

    def finish(self, srcs, outs, sems):
        self._each("finish", srcs, outs, sems)


def _call(name, main, grid, in_specs, out_specs, out_shape, args, scratch=(), comm=None, prefetch=None):
    comm = comm or _NoComm()
    counts = [len(in_specs), len(comm.inputs), len(out_shape), len(comm.out_shapes), len(scratch), len(comm.scratch)]
    n_steps = math.prod(grid)
    hosted = bool(comm.inputs)

    def body(*refs):
        if prefetch is not None:
            refs = refs[1:]
        parts, at = [], 0
        for n in counts:
            parts.append(refs[at:at + n])
            at += n
        ins, c_in, outs, c_out, scr, c_scr = parts
        step = pl.program_id(0)
        for d in range(1, len(grid)):
            step = step * grid[d] + pl.program_id(d)
        if hosted:
            @pl.when(step == 0)
            def _():
                comm.start(c_in, c_out, c_scr)
        main(ins, outs, scr)
        if hosted:
            @pl.when(step == max(n_steps - 2, 0))
            def _():
                comm.mid(c_in, c_out, c_scr)

            @pl.when(step == n_steps - 1)
            def _():
                comm.finish(c_in, c_out, c_scr)

    all_in = list(in_specs) + [HBM] * counts[1]
    all_out = list(out_specs) + [HBM] * counts[3]
    all_scratch = list(scratch) + list(comm.scratch)
    shapes = list(out_shape) + list(comm.out_shapes)
    if prefetch is None:
        res = pl.pallas_call(body, name=name, grid=grid, in_specs=all_in, out_specs=all_out, out_shape=shapes,
                             scratch_shapes=all_scratch, compiler_params=_cparams())(*args, *comm.inputs)
    else:
        res = pl.pallas_call(
            body, name=name, out_shape=shapes,
            grid_spec=pltpu.PrefetchScalarGridSpec(num_scalar_prefetch=1, grid=grid, in_specs=all_in,
                                                   out_specs=all_out, scratch_shapes=all_scratch),
            compiler_params=_cparams())(prefetch, *args, *comm.inputs)
    return res[:counts[2]], res[counts[2]:]


def _rms_fwd(name, x, g, comm=None):
    T, D = x.shape
    tm = min(T, 512)

    def main(ins, outs, scr):
        x_ref, g_ref = ins
        xv = x_ref[...]
        r = lax.rsqrt(jnp.mean(xv * xv, axis=-1, keepdims=True) + RMS_EPS)
        outs[0][...] = (xv * r * g_ref[...]).astype(outs[0].dtype)

    tile = pl.BlockSpec((tm, D), lambda i: (i, 0))
    (h,), extra = _call(name, main, (T // tm,), [tile, pl.BlockSpec((1, D), lambda i: (0, 0))], [tile],
                        [jax.ShapeDtypeStruct((T, D), BF16)], (x, g), comm=comm)
    return h, extra


def _final_loss_epilogue(scale, out_scale):
    def ep(acc, ex, outs):
        res_ref, g_ref, t_ref = ex
        dx_ref, dxb_ref, dg_ref, loss_ref = outs
        n = acc.shape[-1]
        xv = res_ref[...] + scale * acc
        r = lax.rsqrt(jnp.mean(xv * xv, axis=-1, keepdims=True) + RMS_EPS)
        xhat = xv * r
        err = xhat * g_ref[...] - t_ref[...]

        @pl.when(pl.program_id(0) == 0)
        def _():
            dg_ref[...] = jnp.zeros_like(dg_ref)
            loss_ref[...] = jnp.zeros_like(loss_ref)

        loss_ref[...] += jnp.broadcast_to(jnp.sum(err * err) * (0.5 / n), loss_ref.shape)
        dy = err * (1.0 / n)
        dg_ref[...] += jnp.sum(dy * xhat, axis=0, keepdims=True)
        dxhat = dy * g_ref[...]
        dx = r * (dxhat - xhat * jnp.mean(dxhat * xhat, axis=-1, keepdims=True))
        dx_ref[...] = dx
        dxb_ref[...] = (out_scale * dx).astype(dxb_ref.dtype)
    return ep


def _mm_nstream(name, a, ws, w_sel, w_form, comps, out_dtypes, epilogue, cn, rows=1024, comm=None):
    T, K = a.shape
    N = ws[0].shape[1]
    rows = min(rows, T)
    assert N % cn == 0 and T % rows == 0
    n_w, n_c = len(ws), len(comps)

    def main(ins, outs, scr):
        a_ref = ins[0]
        w_refs = ins[1:1 + n_w]
        c_refs = ins[1 + n_w:]

        for r in range(T // rows):
            sl = slice(r * rows, (r + 1) * rows)
            a_blk = a_ref[sl, :]
            dots = [_dot(a_blk, w_ref[...], w_form) for w_ref in w_refs]
            res = epilogue(dots, [c_ref[sl, :] for c_ref in c_refs])
            for o_ref, o in zip(outs, res):
                o_ref[sl, :] = o.astype(o_ref.dtype)

    if w_form == "nt":
        w_specs = [pl.BlockSpec((None, cn, K), functools.partial(lambda j, s: (s, j, 0), s=s)) for s in w_sel]
    else:
        w_specs = [pl.BlockSpec((K, cn), lambda j: (0, j)) for _ in ws]
    chunk = pl.BlockSpec((T, cn), lambda j: (0, j))
    return _call(name, main, (N // cn,), [pl.BlockSpec((T, K), lambda j: (0, 0))] + w_specs + [chunk] * n_c,
                 [chunk] * len(out_dtypes), [jax.ShapeDtypeStruct((T, N), dt) for dt in out_dtypes],
                 (a, *ws, *comps), comm=comm)


def _mm_mstream(name, as_, ws, w_sel, w_form, extras, outs_desc, epilogue, tm=512, comm=None):
    T = as_[0].shape[0]
    tm = min(tm, T)
    n_a = len(as_)
    w_shapes = [w.shape[-2:] for w in ws]
    N = w_shapes[0][1] if w_form == "nn" else w_shapes[0][0]

    def main(ins, outs, scr):
        a_refs = ins[:n_a]
        w_refs = ins[n_a:2 * n_a]
        acc = None
        for a_ref, w_ref in zip(a_refs, w_refs):
            d = _dot(a_ref[...], w_ref[...], w_form)
            acc = d if acc is None else acc + d
        epilogue(acc, ins[2 * n_a:], outs)

    kind_spec = {"tile": pl.BlockSpec((tm, N), lambda i: (i, 0)), "vec": pl.BlockSpec((1, N), lambda i: (0, 0))}
    kind_shape = {"tile": (T, N), "vec": (1, N)}
    a_specs = [pl.BlockSpec((tm, a.shape[1]), lambda i: (i, 0)) for a in as_]
    w_specs = []
    for w, s in zip(ws, w_sel):
        if w.ndim == 3:
            w_specs.append(pl.BlockSpec((None,) + tuple(w.shape[1:]), functools.partial(lambda i, s: (s, 0, 0), s=s),
                                        pipeline_mode=pl.Buffered(1)))
        else:
            w_specs.append(pl.BlockSpec(tuple(w.shape), lambda i: (0, 0), pipeline_mode=pl.Buffered(1)))
    args = list(as_) + list(ws) + [e for e, _ in extras]
    return _call(name, main, (T // tm,), a_specs + w_specs + [kind_spec[k] for _, k in extras],
                 [kind_spec[k] for _, k in outs_desc],
                 [jax.ShapeDtypeStruct(kind_shape[k], dt) for dt, k in outs_desc], args, comm=comm)


def _plain_epilogue(acc, ex, outs):
    outs[0][...] = acc.astype(outs[0].dtype)


def _residual_rms_epilogue(scale):
    def ep(acc, ex, outs):
        xv = ex[0][...] + scale * acc
        outs[0][...] = xv
        r = lax.rsqrt(jnp.mean(xv * xv, axis=-1, keepdims=True) + RMS_EPS)
        outs[1][...] = (xv * r * ex[1][...]).astype(outs[1].dtype)
    return ep


def _rms_bwd_epilogue(out_scale):
    def ep(acc, ex, outs):
        x_ref, g_ref, dres_ref = ex
        dx_ref, dxb_ref, dg_ref = outs
        xv = x_ref[...]
        r = lax.rsqrt(jnp.mean(xv * xv, axis=-1, keepdims=True) + RMS_EPS)
        xhat = xv * r

        @pl.when(pl.program_id(0) == 0)
        def _():
            dg_ref[...] = jnp.zeros_like(dg_ref)

        dg_ref[...] += jnp.sum(acc * xhat, axis=0, keepdims=True)
        dxhat = acc * g_ref[...]
        dx = r * (dxhat - xhat * jnp.mean(dxhat * xhat, axis=-1, keepdims=True)) + dres_ref[...]
        dx_ref[...] = dx
        dxb_ref[...] = (out_scale * dx).astype(dxb_ref.dtype)
    return ep


def _mm_tn(name, a, b, tmo, tno, out_dtype, tk=1024, comm=None):
    T, Ma = a.shape
    Nb = b.shape[1]
    tk = min(tk, T)
    nk = T // tk

    def main(ins, outs, scr):
        a_ref, b_ref = ins
        (acc_ref,) = scr
        k = pl.program_id(2)

        @pl.when(k == 0)
        def _():
            acc_ref[...] = jnp.zeros_like(acc_ref)

        acc_ref[...] += _dot(a_ref[...], b_ref[...], "tn")

        @pl.when(k == nk - 1)
        def _():
            outs[0][...] = acc_ref[...].astype(outs[0].dtype)

    (out,), extra = _call(
        name, main, (Ma // tmo, Nb // tno, nk),
        [pl.BlockSpec((tk, tmo), lambda i, j, k: (k, i)), pl.BlockSpec((tk, tno), lambda i, j, k: (k, j))],
        [pl.BlockSpec((tmo, tno), lambda i, j, k: (i, j))], [jax.ShapeDtypeStruct((Ma, Nb), out_dtype)],
        (a, b), scratch=[pltpu.VMEM((tmo, tno), F32)], comm=comm)
    return out, extra


def _swiglu_parts(g, u):
    s = _sigmoid(g)
    silu = g * s
    return [u * (s + silu * (1.0 - s)), silu, silu * u]


def _silu_mul_epilogue(dots, comps):
    g, u = dots
    return _swiglu_parts(g, u)


def _gate_parts_epilogue(dots, comps):
    (g,) = dots
    s = _sigmoid(g)
    silu = g * s
    return [s + silu * (1.0 - s), silu]


def _up_act_epilogue(dots, comps):
    (u,) = dots
    return [u * comps[0].astype(F32), u * comps[1].astype(F32)]


def _dact_epilogue(dots, comps):
    dact = dots[0].astype(BF16)
    return [dact * comps[0], dact * comps[1]]


def _identity_epilogue(dots, comps):
    return list(dots)


def _swap_halves(x):
    lane = lax.broadcasted_iota(jnp.int32, x.shape, 1)
    first = (lane % DK) < (DK // 2)
    return jnp.where(first, pltpu.roll(x, 128 - DK // 2, 1), pltpu.roll(x, DK // 2, 1))


def _rotary(t, cos, sin_signed):
    halves = []
    for p in range(QK_W // 128):
        th = t[:, 128 * p:128 * (p + 1)]
        halves.append(th * cos + _swap_halves(th) * sin_signed)
    return jnp.concatenate(halves, axis=1)


def _rotary_transposed(d, cos, sin_signed):
    halves = []
    for p in range(QK_W // 128):
        dh = d[:, 128 * p:128 * (p + 1)]
        halves.append(dh * cos + _swap_halves(dh * sin_signed))
    return jnp.concatenate(halves, axis=1)


def _log_sigmoid(x):
    return jnp.minimum(x, 0.0) - jnp.log(1.0 + jnp.exp(-jnp.abs(x)))


def _attn_masks():
    row = lax.broadcasted_iota(jnp.int32, (SUPER, SUPER), 0)
    col = lax.broadcasted_iota(jnp.int32, (SUPER, SUPER), 1)
    same = (row // CHUNK) == (col // CHUNK)
    return row, col, same


def _group_inputs(grp, pr, cos, sin_signed, lg, wa2, ba):
    if grp == 0:
        q = _rotary(pr[:, C_RQ:C_RQ + QK_W], cos, sin_signed)
        k = _rotary(pr[:, C_RK:C_RK + QK_W], cos, sin_signed) * (DK ** -0.5)
        v = pr[:, C_RV:C_RV + V_W]
        gate = pr[:, C_RG:C_RG + V_W]
        pos = lax.broadcasted_iota(jnp.int32, (SUPER, QK_W), 0).astype(F32) + 1.0
        return q, k, v, gate, pos * lg, None, None
    q = pr[:, C_GQ:C_GQ + QK_W] * (DK ** -0.5)
    k = pr[:, C_GK:C_GK + QK_W]
    v = pr[:, C_GV:C_GV + V_W]
    gate = pr[:, C_GG:C_GG + V_W]
    glow = pr[:, C_GL:C_GL + GL_W]
    logit = _dot(glow.astype(BF16), wa2.astype(BF16), "nn") + ba
    la = _log_sigmoid(logit) * (1.0 / GATE_NORM)
    row, col, _ = _attn_masks()
    lower = (col <= row).astype(F32)
    b_cum = _dot(lower, la, "nn", precision=lax.Precision.HIGHEST)
    return q, k, v, gate, b_cum, glow, logit


def _decay_factors(q, k, b_cum):
    c = b_cum[SUPER // 2 - 1:SUPER // 2, :]
    bl = b_cum[SUPER - 1:SUPER, :]
    e1 = jnp.exp(b_cum - c)
    e2 = jnp.exp(c - b_cum)
    e_b = jnp.exp(b_cum)
    e_l = jnp.exp(bl - b_cum)
    return dict(e1=e1, e2=e2, eb=e_b, el=e_l, ebl=jnp.exp(bl),
                qp=q * e1, qm=q * e2, kp=k * e1, km=k * e2, qs=q * e_b, kl=k * e_l)


def _state_block_mask():
    r = lax.broadcasted_iota(jnp.int32, (V_W, QK_W), 0)
    c = lax.broadcasted_iota(jnp.int32, (V_W, QK_W), 1)
    return (r // DV) == (c // DK)


def _attn_fwd(proj, cos, sin_signed, lg, wa2p, ba, gn_ret, gn_gla, x_res, w_out, g_next, comm=None):
    T = proj.shape[0]
    n_s = T // SUPER
    D = x_res.shape[1]

    def main(ins, outs, scr):
        pr_ref, cos_ref, sin_ref, lg_ref, wa2_ref, ba_ref, gr_ref, gg_ref, xres_ref, wout_ref, gnext_ref = ins
        o_ref, y_ref, st_ref, x_ref, h_ref = outs
        (s_ref,) = scr
        i = pl.program_id(0)

        @pl.when(i == 0)
        def _():
            s_ref[...] = jnp.zeros_like(s_ref)

        pr = pr_ref
        row, col, same = _attn_masks()
        m1 = col <= row
        m2 = jnp.logical_and(col > row, same)
        lane = lax.broadcasted_iota(jnp.int32, (1, QK_W), 1)
        blockmask = _state_block_mask()
        for grp in range(2):
            q, k, v, gate, b_cum, _, _ = _group_inputs(grp, pr, cos_ref[...], sin_ref[...], lg_ref[...],
                                                      wa2_ref[...], ba_ref[...])
            f = _decay_factors(q, k, b_cum)
            gn = gr_ref[...] if grp == 0 else gg_ref[...]
            s_prev = s_ref[grp]
            st_ref[0, grp] = s_prev
            o_inter = _dot(f["qs"].astype(BF16), s_prev.astype(BF16), "nt")
            kmb = f["km"].astype(BF16)
            kpb = f["kp"].astype(BF16)
            vb = v.astype(BF16)
            for h in range(HEADS):
                hm = (lane // DK) == h
                a1 = _dot(jnp.where(hm, f["qp"], 0.0).astype(BF16), kmb, "nt")
                a2 = _dot(jnp.where(hm, f["qm"], 0.0).astype(BF16), kpb, "nt")
                a = jnp.where(m1, a1, jnp.where(m2, a2, 0.0))
                lo = grp * V_W + h * DV
                o_h = _dot(a.astype(BF16), vb[:, h * DV:(h + 1) * DV], "nn") + o_inter[:, h * DV:(h + 1) * DV]
                o_ref[:, lo:lo + DV] = o_h
                r = lax.rsqrt(jnp.mean(o_h * o_h, axis=-1, keepdims=True) + RMS_EPS)
                gte = gate[:, h * DV:(h + 1) * DV]
                y = o_h * r * gn[:, h * DV:(h + 1) * DV] * (gte * _sigmoid(gte))
                y_ref[:, lo:lo + DV] = y.astype(y_ref.dtype)
            upd = _dot(vb, f["kl"].astype(BF16), "tn")
            s_ref[grp] = s_prev * f["ebl"] + jnp.where(blockmask, upd, 0.0)
        xv = xres_ref[...] + _dot(y_ref[...], wout_ref[...], "nn")
        x_ref[...] = xv
        r = lax.rsqrt(jnp.mean(xv * xv, axis=-1, keepdims=True) + RMS_EPS)
        h_ref[...] = (xv * r * gnext_ref[...]).astype(h_ref.dtype)

    const = lambda shape: pl.BlockSpec(shape, lambda i: tuple(0 for _ in shape))
    rows = lambda w: pl.BlockSpec((SUPER, w), lambda i: (i, 0))
    return _call(
        "attn_fwd", main, (n_s,),
        [rows(PROJ_W), rows(128), rows(128),
         const((1, QK_W)), const((GL_W, QK_W)), const((1, QK_W)), const((1, V_W)), const((1, V_W)),
         rows(D), const((2 * V_W, D)), const((1, D))],
        [rows(2 * V_W), rows(2 * V_W), pl.BlockSpec((1, 2, V_W, QK_W), lambda i: (i, 0, 0, 0)), rows(D), rows(D)],
        [jax.ShapeDtypeStruct((T, 2 * V_W), F32), jax.ShapeDtypeStruct((T, 2 * V_W), BF16),
         jax.ShapeDtypeStruct((n_s, 2, V_W, QK_W), F32), jax.ShapeDtypeStruct((T, D), F32),
         jax.ShapeDtypeStruct((T, D), BF16)],
        (proj, cos, sin_signed, lg, wa2p, ba, gn_ret, gn_gla, x_res, w_out, g_next),
        scratch=[pltpu.VMEM((2, V_W, QK_W), F32)], comm=comm)


def _attn_bwd(proj, cos, sin_signed, lg, wa2p, ba, gn_ret, gn_gla, o, dx, w_out, states, comm=None):
    T = proj.shape[0]
    n_s = T // SUPER
    D = dx.shape[1]

    def main(ins, outs, scr):
        pr_ref, cos_ref, sin_ref, lg_ref, wa2_ref, ba_ref, gr_ref, gg_ref, o_ref, dx_ref, wout_ref, st_ref = ins
        dp_ref, dgr_ref, dgg_ref, dba_ref, dwa_ref = outs
        (ds_ref, dy_ref) = scr
        i = pl.program_id(0)
        dy_ref[...] = _dot(dx_ref[...], wout_ref[...], "nt")

        @pl.when(i == 0)
        def _():
            ds_ref[...] = jnp.zeros_like(ds_ref)
            dgr_ref[...] = jnp.zeros_like(dgr_ref)
            dgg_ref[...] = jnp.zeros_like(dgg_ref)
            dba_ref[...] = jnp.zeros_like(dba_ref)
            dwa_ref[...] = jnp.zeros_like(dwa_ref)

        pr = pr_ref
        cos = cos_ref[...]
        sin_signed = sin_ref[...]
        row, col, same = _attn_masks()
        m1 = col <= row
        m2 = jnp.logical_and(col > row, same)
        m1t = row <= col
        m2t = jnp.logical_and(row > col, same)
        lane = lax.broadcasted_iota(jnp.int32, (1, QK_W), 1)
        blockmask = _state_block_mask()
        for grp in range(2):
            q, k, v, gate, b_cum, glow, logit = _group_inputs(grp, pr, cos, sin_signed, lg_ref[...],
                                                              wa2_ref[...], ba_ref[...])
            f = _decay_factors(q, k, b_cum)
            gn = gr_ref[...] if grp == 0 else gg_ref[...]
            dgn_ref = dgr_ref if grp == 0 else dgg_ref
            do_parts, dgate_parts, dgn_parts = [], [], []
            for h in range(HEADS):
                lo = grp * V_W + h * DV
                o_h = o_ref[:, lo:lo + DV]
                r = lax.rsqrt(jnp.mean(o_h * o_h, axis=-1, keepdims=True) + RMS_EPS)
                n = o_h * r
                gte = gate[:, h * DV:(h + 1) * DV]
                sg = _sigmoid(gte)
                dy_h = dy_ref[:, lo:lo + DV]
                gn_h = gn[:, h * DV:(h + 1) * DV]
                dgate_parts.append(dy_h * n * gn_h * (sg * (1.0 + gte * (1.0 - sg))))
                dz = dy_h * (gte * sg)
                dgn_parts.append(jnp.sum(dz * n, axis=0, keepdims=True))
                dn = dz * gn_h
                do_parts.append(r * (dn - n * jnp.mean(dn * n, axis=-1, keepdims=True)))
            dgn_ref[...] += jnp.concatenate(dgn_parts, axis=1)
            dgate = jnp.concatenate(dgate_parts, axis=1)
            do = jnp.concatenate(do_parts, axis=1)
            dob = do.astype(BF16)
            vb = v.astype(BF16)
            s_prev = st_ref[0, grp]
            ds_new = ds_ref[grp]
            dsb = ds_new.astype(BF16)
            qpb, qmb = f["qp"].astype(BF16), f["qm"].astype(BF16)
            kpb, kmb = f["kp"].astype(BF16), f["km"].astype(BF16)
            dqp = jnp.zeros((SUPER, QK_W), F32)
            dqm = jnp.zeros((SUPER, QK_W), F32)
            dkp = jnp.zeros((SUPER, QK_W), F32)
            dkm = jnp.zeros((SUPER, QK_W), F32)
            dv_parts = []
            for h in range(HEADS):
                hm = (lane // DK) == h
                qp_h = jnp.where(hm, f["qp"], 0.0).astype(BF16)
                qm_h = jnp.where(hm, f["qm"], 0.0).astype(BF16)
                kp_h = jnp.where(hm, f["kp"], 0.0).astype(BF16)
                km_h = jnp.where(hm, f["km"], 0.0).astype(BF16)
                at = jnp.where(m1t, _dot(km_h, qpb, "nt"), jnp.where(m2t, _dot(kp_h, qmb, "nt"), 0.0))
                do_h = dob[:, h * DV:(h + 1) * DV]
                v_h = vb[:, h * DV:(h + 1) * DV]
                dv_parts.append(_dot(at.astype(BF16), do_h, "nn"))
                da = _dot(do_h, v_h, "nt")
                dat = _dot(v_h, do_h, "nt")
                da1 = jnp.where(m1, da, 0.0).astype(BF16)
                da2 = jnp.where(m2, da, 0.0).astype(BF16)
                da1t = jnp.where(m1t, dat, 0.0).astype(BF16)
                da2t = jnp.where(m2t, dat, 0.0).astype(BF16)
                dqp = dqp + _dot(da1, km_h, "nn")
                dqm = dqm + _dot(da2, kp_h, "nn")
                dkm = dkm + _dot(da1t, qp_h, "nn")
                dkp = dkp + _dot(da2t, qm_h, "nn")
            klb = f["kl"].astype(BF16)
            qsb = f["qs"].astype(BF16)
            dqs = _dot(dob, s_prev.astype(BF16), "nn")
            dkl = _dot(vb, dsb, "nn")
            dv = jnp.concatenate(dv_parts, axis=1) + _dot(klb, dsb, "nt")
            ds_ref[grp] = ds_new * f["ebl"] + jnp.where(blockmask, _dot(dob, qsb, "tn"), 0.0)
            dq = dqp * f["e1"] + dqm * f["e2"] + dqs * f["eb"]
            dk = dkm * f["e2"] + dkp * f["e1"] + dkl * f["el"]
            if grp == 0:
                dq = _rotary_transposed(dq, cos, sin_signed)
                dk = _rotary_transposed(dk * (DK ** -0.5), cos, sin_signed)
                dp_ref[:, C_RQ:C_RQ + QK_W] = dq.astype(dp_ref.dtype)
                dp_ref[:, C_RK:C_RK + QK_W] = dk.astype(dp_ref.dtype)
                dp_ref[:, C_RV:C_RV + V_W] = dv.astype(dp_ref.dtype)
                dp_ref[:, C_RG:C_RG + V_W] = dgate.astype(dp_ref.dtype)
            else:
                dkl_kl = dkl * klb.astype(F32)
                db = (dqp * qpb.astype(F32) - dkm * kmb.astype(F32) - dqm * qmb.astype(F32)
                      + dkp * kpb.astype(F32) + dqs * qsb.astype(F32) - dkl_kl)
                last = (jnp.sum(dkl_kl, axis=0, keepdims=True)
                        + f["ebl"] * jnp.sum(s_prev * ds_new, axis=0, keepdims=True))
                rowq = lax.broadcasted_iota(jnp.int32, (SUPER, QK_W), 0)
                db = db + jnp.where(rowq == SUPER - 1, last, 0.0)
                upper = (col >= row).astype(F32)
                dla = _dot(upper, db, "nn", precision=lax.Precision.HIGHEST)
                dlogit = dla * (1.0 / GATE_NORM) * (1.0 - _sigmoid(logit))
                dlb = dlogit.astype(BF16)
                dglow = _dot(dlb, wa2_ref[...].astype(BF16), "nt")
                dwa_ref[...] += _dot(glow.astype(BF16), dlb, "tn")
                dba_ref[...] += jnp.sum(dlogit, axis=0, keepdims=True)
                dp_ref[:, C_GQ:C_GQ + QK_W] = (dq * (DK ** -0.5)).astype(dp_ref.dtype)
                dp_ref[:, C_GK:C_GK + QK_W] = dk.astype(dp_ref.dtype)
                dp_ref[:, C_GV:C_GV + V_W] = dv.astype(dp_ref.dtype)
                dp_ref[:, C_GG:C_GG + V_W] = dgate.astype(dp_ref.dtype)
                dp_ref[:, C_GL:C_GL + GL_W] = dglow.astype(dp_ref.dtype)

    rev = lambda i: n_s - 1 - i
    const = lambda shape: pl.BlockSpec(shape, lambda i: tuple(0 for _ in shape))
    return _call(
        "attn_bwd", main, (n_s,),
        [pl.BlockSpec((SUPER, PROJ_W), lambda i: (rev(i), 0)),
         pl.BlockSpec((SUPER, 128), lambda i: (rev(i), 0)), pl.BlockSpec((SUPER, 128), lambda i: (rev(i), 0)),
         const((1, QK_W)), const((GL_W, QK_W)), const((1, QK_W)), const((1, V_W)), const((1, V_W)),
         pl.BlockSpec((SUPER, 2 * V_W), lambda i: (rev(i), 0)),
         pl.BlockSpec((SUPER, D), lambda i: (rev(i), 0)), const((2 * V_W, D)),
         pl.BlockSpec((1, 2, V_W, QK_W), lambda i: (rev(i), 0, 0, 0))],
        [pl.BlockSpec((SUPER, PROJ_W), lambda i: (rev(i), 0)),
         const((1, V_W)), const((1, V_W)), const((1, QK_W)), const((GL_W, QK_W))],
        [jax.ShapeDtypeStruct((T, PROJ_W), BF16),
         jax.ShapeDtypeStruct((1, V_W), F32), jax.ShapeDtypeStruct((1, V_W), F32),
         jax.ShapeDtypeStruct((1, QK_W), F32), jax.ShapeDtypeStruct((GL_W, QK_W), F32)],
        (proj, cos, sin_signed, lg, wa2p, ba, gn_ret, gn_gla, o, dx, w_out, states),
        scratch=[pltpu.VMEM((2, V_W, QK_W), F32), pltpu.VMEM((SUPER, 2 * V_W), F32)], comm=comm)


def _rotary_tables(T):
    half = DK // 2
    inv = ROPE_BASE ** (-jnp.arange(half, dtype=F32) * 2.0 / DK)
    ang = jnp.arange(T, dtype=F32)[:, None] * inv[None, :]
    cos, sin = jnp.cos(ang), jnp.sin(ang)
    cos_head = jnp.concatenate([cos, cos], axis=1)
    sin_head = jnp.concatenate([-sin, sin], axis=1)
    return jnp.tile(cos_head, (1, 128 // DK)), jnp.tile(sin_head, (1, 128 // DK))


def _sum_devices(name, gathered, m_per):
    def body(g_ref, o_ref):
        acc = g_ref[0:m_per, :]
        for k in range(1, N_DEV):
            acc = acc + g_ref[k * m_per:(k + 1) * m_per, :]
        o_ref[...] = acc

    return pl.pallas_call(body, name=name, out_shape=jax.ShapeDtypeStruct((m_per, 128), F32))(gathered)


def _pair_sum(name, grad, landed, core, comm=None):
    R, C = grad.shape[2:]

    def main(ins, outs, scr):
        outs[0][...] = (ins[0][...].astype(F32) + ins[1][...].astype(F32)).astype(outs[0].dtype)

    blk = pl.BlockSpec((None, R, C), lambda j, s: (j, 0, 0))
    (out,), extra = _call(name, main, (4,), [pl.BlockSpec((None, None, R, C), lambda j, s: (j, s[0], 0, 0)), blk],
                          [blk], [jax.ShapeDtypeStruct((4, R, C), BF16)], (grad, landed), prefetch=core, comm=comm)
    return out, extra


def _owner_sums(name, items, owner, comm=None):
    counts = [1 + len(landed) for _, landed in items]

    def main(ins, outs, scr):
        at = 0
        for o_ref, n in zip(outs, counts):
            acc = ins[at][...].astype(F32)
            for l_ref in ins[at + 1:at + n]:
                for j in range(l_ref.shape[0]):
                    acc = acc + l_ref[j].astype(F32)
            o_ref[...] = acc
            at += n

    once = pl.Buffered(1)
    in_specs, out_specs, out_shape, args = [], [], [], []
    for grad, landed in items:
        R, C = grad.shape[-2:]
        if grad.ndim == 4:
            in_specs.append(pl.BlockSpec((None, None, R, C), lambda i, s: (s[0], s[1], 0, 0), pipeline_mode=once))
        else:
            in_specs.append(pl.BlockSpec((None, R, C), lambda i, s: (s[0], 0, 0), pipeline_mode=once))
        in_specs += [pl.BlockSpec(tuple(l.shape), lambda i, s: (0, 0, 0), pipeline_mode=once) for l in landed]
        out_specs.append(pl.BlockSpec((R, C), lambda i, s: (0, 0)))
        out_shape.append(jax.ShapeDtypeStruct((R, C), F32))
        args += [grad, *landed]
    return _call(name, main, (1,), in_specs, out_specs, out_shape, args, comm=comm, prefetch=owner)


def _rms_bwd(name, x, g, dh, dres, out_scale):
    T, D = x.shape
    tm = min(T, 512)
    ep = _rms_bwd_epilogue(out_scale)

    def main(ins, outs, scr):
        x_ref, g_ref, dh_ref, dres_ref = ins
        ep(dh_ref[...], (x_ref, g_ref, dres_ref), outs)

    tile = pl.BlockSpec((tm, D), lambda i: (i, 0))
    vec = pl.BlockSpec((1, D), lambda i: (0, 0))
    outs, _ = _call(name, main, (T // tm,), [tile, vec, tile, tile], [tile, tile, vec],
                    [jax.ShapeDtypeStruct((T, D), F32), jax.ShapeDtypeStruct((T, D), BF16),
                     jax.ShapeDtypeStruct((1, D), F32)], (x, g, dh, dres))
    return outs


def _adamw_group(name, items, n_blocks, comm=None):
    n = len(items)

    def main(ins, outs, scr):
        for p in range(n):
            g_ref, w_ref, m_ref, v_ref = ins[4 * p:4 * p + 4]
            d_ref, nm_ref, nv_ref = outs[3 * p:3 * p + 3]
            gv = g_ref[...]
            nm = ADAM_B1 * m_ref[...] + (1.0 - ADAM_B1) * gv
            nv = ADAM_B2 * v_ref[...] + (1.0 - ADAM_B2) * (gv * gv)
            m_hat = nm / (1.0 - ADAM_B1 ** ADAM_STEP)
            v_hat = nv / (1.0 - ADAM_B2 ** ADAM_STEP)
            d_ref[...] = -ADAM_LR * (m_hat / (jnp.sqrt(v_hat) + ADAM_EPS) + ADAM_WD * w_ref[...])
            nm_ref[...] = nm
            nv_ref[...] = nv

    in_specs, out_specs, out_shape, args = [], [], [], []
    for item in items:
        R, C = item[1].shape
        assert R % n_blocks == 0
        spec = pl.BlockSpec((R // n_blocks, C), lambda i: (i, 0))
        in_specs += [spec] * 4
        out_specs += [spec] * 3
        out_shape += [jax.ShapeDtypeStruct((R, C), F32)] * 3
        args += list(item)
    outs, extra = _call(name, main, (n_blocks,), in_specs, out_specs, out_shape, args, comm=comm)
    return [tuple(outs[3 * p:3 * p + 3]) for p in range(n)], extra


SMALL_ORDER = ("ffn1", "mix", "ffn2", "final", "ret", "gla", "b_a")


def kernel(x, ffn1_norm_g, ffn1_w_gate, ffn1_w_up, ffn1_w_down, mix_norm_g, w_in, ret_norm_g, gla_w_a2, gla_b_a, gla_norm_g, w_out, ffn2_norm_g, ffn2_w_gate, ffn2_w_up, ffn2_w_down, final_norm_g, loss_target, m_ffn1_norm_g, m_ffn1_w_gate, m_ffn1_w_up, m_ffn1_w_down, m_mix_norm_g, m_w_in, m_ret_norm_g, m_gla_w_a2, m_gla_b_a, m_gla_norm_g, m_w_out, m_ffn2_norm_g, m_ffn2_w_gate, m_ffn2_w_up, m_ffn2_w_down, m_final_norm_g, v_ffn1_norm_g, v_ffn1_w_gate, v_ffn1_w_up, v_ffn1_w_down, v_mix_norm_g, v_w_in, v_ret_norm_g, v_gla_w_a2, v_gla_b_a, v_gla_norm_g, v_w_out, v_ffn2_norm_g, v_ffn2_w_gate, v_ffn2_w_up, v_ffn2_w_down, v_final_norm_g):
    xi, yi, ci = _coords()
    dev = 4 * xi + 2 * yi + ci
    owner = jnp.stack([2 * xi + yi, ci]).astype(jnp.int32)

    x0, target = x[0], loss_target[0]
    T, D = x0.shape
    fb = ffn1_w_gate.shape[2]
    ib = w_in.shape[2]
    ab = gla_w_a2.shape[2]
    F = N_DEV * fb
    cos, sin_signed = _rotary_tables(T)
    lg = jnp.repeat(jnp.log(1.0 - 2.0 ** (-5.0 - jnp.arange(HEADS, dtype=F32))), DK)[None, :]
    g_final = final_norm_g.reshape(1, D)

    g1_loc = ffn1_w_gate[0].T[None].astype(BF16)
    u1_loc = ffn1_w_up[0].T[None].astype(BF16)
    d1_loc = ffn1_w_down.astype(BF16)
    g2_loc = ffn2_w_gate[0].T[None].astype(BF16)
    u2_loc = ffn2_w_up[0].T[None].astype(BF16)
    d2_loc = ffn2_w_down.astype(BF16)
    in_loc = w_in[0].T.astype(BF16)
    out_loc = w_out[0].astype(BF16)

    h1, (g1,) = _rms_fwd("ffn1_rms", x0, ffn1_norm_g, comm=_AllGather([g1_loc], ["stack"]))
    g1 = g1.reshape(1, F, D)
    (dsl1, sl1), (u1,) = _mm_nstream("ffn1_gate", h1, [g1], [0], "nt", [], [BF16, BF16], _gate_parts_epilogue, cn=256,
                                     comm=_AllGather([u1_loc], ["stack"]))
    u1 = u1.reshape(1, F, D)
    (dsu1, act1), (d1,) = _mm_nstream("ffn1_up", h1, [u1], [0], "nt", [dsl1, sl1], [BF16, BF16],
                                      _up_act_epilogue, cn=256, comm=_AllGather([d1_loc], ["stack"]))
    d1 = d1.reshape(1, F, D)
    f32_tile, bf16_tile, f32_vec = (F32, "tile"), (BF16, "tile"), (F32, "vec")
    (x1, h2), (in_all, a_all) = _mm_mstream(
        "ffn1_down", [act1], [d1], [0], "nn", [(x0, "tile"), (mix_norm_g, "vec")], [f32_tile, bf16_tile],
        _residual_rms_epilogue(0.5), comm=_AllGather([in_loc, gla_w_a2[0]], ["plain", "plain"]))
    w_in_t = jnp.pad(in_all.reshape(1, N_DEV * ib, D), ((0, 0), (0, PROJ_W - N_DEV * ib), (0, 0)))
    wa2 = jnp.transpose(a_all, (1, 0, 2)).reshape(GATE_RANK, N_DEV * ab)
    wa2p = jnp.pad(wa2, ((0, GL_W - GATE_RANK), (0, 0)))

    (proj,), (g2, out_all) = _mm_nstream("mix_proj", h2, [w_in_t], [0], "nt", [], [F32], _identity_epilogue, cn=640,
                                         comm=_AllGather([g2_loc, out_loc], ["stack", "plain"]))
    w_out_full = out_all.reshape(D, D)
    (o, ymix, states, x2, h3), (u2,) = _attn_fwd(proj, cos, sin_signed, lg, wa2p, gla_b_a, ret_norm_g, gla_norm_g,
                                                 x1, w_out_full, ffn2_norm_g, comm=_AllGather([u2_loc], ["stack"]))
    g2, u2 = g2.reshape(1, F, D), u2.reshape(1, F, D)

    (dsu2, sl2, act2), (d2,) = _mm_nstream(
        "ffn2_up", h3, [g2, u2], [0, 0], "nt", [], [BF16, BF16, BF16], _silu_mul_epilogue, cn=256,
        comm=_AllGather([d2_loc], ["stack"]))
    d2 = d2.reshape(1, F, D)
    (dx3, dy3b, d_final, loss), _ = _mm_mstream(
        "ffn2_down", [act2], [d2], [0], "nn", [(x2, "tile"), (g_final, "vec"), (target, "tile")],
        [f32_tile, bf16_tile, f32_vec, f32_vec], _final_loss_epilogue(0.5, 0.5))

    dwd2, _ = _mm_tn("ffn2b_dwd", act2, dy3b, F // 2, D, BF16)
    dwd2 = dwd2.reshape(4, 2, fb, D)
    (dgate2, dup2), (l_wd2_near,) = _mm_nstream("ffn2b_dact", dy3b, [d2], [0], "nt", [dsu2, sl2], [BF16, BF16],
                                                _dact_epilogue, cn=256, comm=_ReduceScatter([(dwd2, NEAR)]))
    dwg2, (l_wd2_far,) = _mm_tn("ffn2b_dwg", dgate2, h3, F // 2, D, BF16, comm=_ReduceScatter([(dwd2, FAR)]))
    dwg2 = dwg2.reshape(4, 2, fb, D)
    dwu2, (l_wg2_near,) = _mm_tn("ffn2b_dwu", dup2, h3, F // 2, D, BF16, comm=_ReduceScatter([(dwg2, NEAR)]))
    dwu2 = dwu2.reshape(4, 2, fb, D)
    rms_outs = [f32_tile, bf16_tile, f32_vec]
    (dx2, dx2b, d_g2), (l_wg2_far, l_wu2_near) = _mm_mstream(
        "ffn2b_dh", [dgate2, dup2], [g2, u2], [0, 0], "nn", [(x2, "tile"), (ffn2_norm_g, "vec"), (dx3, "tile")],
        rms_outs, _rms_bwd_epilogue(1.0), comm=_ReduceScatter([(dwg2, FAR), (dwu2, NEAR)]))

    dwout, _ = _mm_tn("mixb_dwout", ymix, dx2b, D, D, BF16)
    dwout = dwout.reshape(4, 2, D // N_DEV, D)
    (dproj, d_ret, d_gla, d_ba, d_wa2p), (l_wu2_far, l_wout) = _attn_bwd(
        proj, cos, sin_signed, lg, wa2p, gla_b_a, ret_norm_g, gla_norm_g, o, dx2b, w_out_full, states,
        comm=_ReduceScatter([(dwu2, FAR), (dwout, ALL)]))
    dwin_t, _ = _mm_tn("mixb_dwin", dproj, h2, 640, D, BF16, tk=2048)
    dwin = dwin_t[:N_DEV * ib].reshape(4, 2, ib, D)
    (dx1, dy1b, d_gmix), (l_win_near,) = _mm_mstream(
        "mixb_dh", [dproj], [w_in_t], [0], "nn", [(x1, "tile"), (mix_norm_g, "vec"), (dx2, "tile")],
        rms_outs, _rms_bwd_epilogue(0.5), comm=_ReduceScatter([(dwin, NEAR)]))

    dwd1, (l_win_far,) = _mm_tn("ffn1b_dwd", act1, dy1b, F // 2, D, BF16, comm=_ReduceScatter([(dwin, FAR)]))
    dwd1 = dwd1.reshape(4, 2, fb, D)
    (dgate1, dup1), (l_wd1_near,) = _mm_nstream("ffn1b_dact", dy1b, [d1], [0], "nt", [dsu1, sl1], [BF16, BF16],
                                                _dact_epilogue, cn=256, comm=_ReduceScatter([(dwd1, NEAR)]))
    dwg1, (l_wd1_far,) = _mm_tn("ffn1b_dwg", dgate1, h1, F // 2, D, BF16, comm=_ReduceScatter([(dwd1, FAR)]))
    dwg1 = dwg1.reshape(4, 2, fb, D)
    core = owner[1:2]
    dwu1, (s_wg1,) = _mm_tn("ffn1b_dwu", dup1, h1, F // 2, D, BF16, comm=_SiblingExchange([dwg1]))
    dwu1 = dwu1.reshape(4, 2, fb, D)
    p_wg1, (s_wu1,) = _pair_sum("pair_wg1", dwg1, s_wg1, core, comm=_SiblingExchange([dwu1]))
    p_wu1, _ = _pair_sum("pair_wu1", dwu1, s_wu1, core)
    (dh1,), (c_wg1, c_wu1) = _mm_mstream(
        "ffn1b_dh", [dgate1, dup1], [g1, u1], [0, 0], "nn", [], [f32_tile], _plain_epilogue,
        comm=_ChipExchange([p_wg1, p_wu1]))
    dx0, _, d_g1 = _rms_bwd("ffn1b_rms", x0, ffn1_norm_g, dh1, dx1, 1.0)

    small = dict(ffn1=d_g1, mix=d_gmix, ffn2=d_g2, final=d_final, ret=d_ret, gla=d_gla, b_a=d_ba)
    flat = jnp.concatenate([small[k].reshape(-1) for k in SMALL_ORDER]
                           + [d_wa2p[:GATE_RANK].reshape(-1), loss[0, :128]])
    rows = -(-flat.shape[0] // 128)
    rows = -(-rows // 8) * 8
    packed = jnp.pad(flat, (0, rows * 128 - flat.shape[0])).reshape(rows, 128)

    transposed = ("ffn1_w_gate", "ffn1_w_up", "ffn2_w_gate", "ffn2_w_up", "w_in")

    def to_2d(nm, a):
        if nm in transposed:
            return a[0].T
        return a.reshape((1, a.shape[0]) if a.ndim == 1 else a.shape[-2:])

    def from_2d(nm, a):
        return a.T[None] if nm in transposed else a.reshape(params[nm][0].shape)

    sums_a, (gathered,) = _owner_sums(
        "sum_a", [(dwg2, [l_wg2_near, l_wg2_far]), (dwu2, [l_wu2_near, l_wu2_far]), (dwd2, [l_wd2_near, l_wd2_far]),
                  (dwin, [l_win_near, l_win_far]), (dwout, [l_wout])], owner,
        comm=_AllGather([packed], ["plain"]))
    sums_b, _ = _owner_sums("sum_b", [(p_wg1, [c_wg1]), (p_wu1, [c_wu1]), (dwd1, [l_wd1_near, l_wd1_far])], owner)
    big_grads = {"ffn2_w_gate": sums_a[0], "ffn2_w_up": sums_a[1], "ffn2_w_down": sums_a[2], "w_out": sums_a[4],
                 "ffn1_w_gate": sums_b[0], "ffn1_w_up": sums_b[1], "ffn1_w_down": sums_b[2]}
    params = dict(
        ffn2_w_gate=(ffn2_w_gate, m_ffn2_w_gate, v_ffn2_w_gate), ffn2_w_up=(ffn2_w_up, m_ffn2_w_up, v_ffn2_w_up),
        ffn2_w_down=(ffn2_w_down, m_ffn2_w_down, v_ffn2_w_down), w_in=(w_in, m_w_in, v_w_in),
        w_out=(w_out, m_w_out, v_w_out), ffn1_w_gate=(ffn1_w_gate, m_ffn1_w_gate, v_ffn1_w_gate),
        ffn1_w_up=(ffn1_w_up, m_ffn1_w_up, v_ffn1_w_up), ffn1_w_down=(ffn1_w_down, m_ffn1_w_down, v_ffn1_w_down),
        ffn1_norm_g=(ffn1_norm_g, m_ffn1_norm_g, v_ffn1_norm_g), mix_norm_g=(mix_norm_g, m_mix_norm_g, v_mix_norm_g),
        ret_norm_g=(ret_norm_g, m_ret_norm_g, v_ret_norm_g), gla_w_a2=(gla_w_a2, m_gla_w_a2, v_gla_w_a2),
        gla_b_a=(gla_b_a, m_gla_b_a, v_gla_b_a), gla_norm_g=(gla_norm_g, m_gla_norm_g, v_gla_norm_g),
        ffn2_norm_g=(ffn2_norm_g, m_ffn2_norm_g, v_ffn2_norm_g), final_norm_g=(final_norm_g, m_final_norm_g, v_final_norm_g))
    grads, updates = {}, {}

    def run_adam(name, names, grad_2d, n_blocks):
        items = [(grad_2d[nm],) + tuple(to_2d(nm, a) for a in params[nm]) for nm in names]
        res, _ = _adamw_group(name, items, n_blocks)
        for nm, r in zip(names, res):
            grads[nm] = from_2d(nm, grad_2d[nm])
            updates[nm] = tuple(from_2d(nm, a) for a in r)

    run_adam("adamw_big", list(big_grads), big_grads, 4)
    run_adam("adamw_w_in", ["w_in"], {"w_in": sums_a[3]}, 1)

    total = _sum_devices("sum_small", gathered.reshape(N_DEV * rows, 128), rows).reshape(-1)
    sizes = [small[k].size for k in SMALL_ORDER] + [GATE_RANK * QK_W, 128]
    offs = [0]
    for s in sizes:
        offs.append(offs[-1] + s)
    pieces = [total[offs[i]:offs[i + 1]] for i in range(len(sizes))]
    g_small = {k: pieces[i].reshape(small[k].shape) for i, k in enumerate(SMALL_ORDER)}
    g_wa2_full = pieces[len(SMALL_ORDER)].reshape(GATE_RANK, QK_W)
    g_wa2 = lax.dynamic_slice(g_wa2_full, (0, dev * ab), (GATE_RANK, ab))
    loss_total = pieces[len(SMALL_ORDER) + 1][0]

    small_grads = {"ffn1_norm_g": g_small["ffn1"], "mix_norm_g": g_small["mix"], "ret_norm_g": g_small["ret"],
                   "gla_w_a2": g_wa2, "gla_b_a": g_small["b_a"], "gla_norm_g": g_small["gla"],
                   "ffn2_norm_g": g_small["ffn2"], "final_norm_g": g_small["final"]}
    run_adam("adamw_small", list(small_grads), small_grads, 1)

    order = ("ffn1_norm_g", "ffn1_w_gate", "ffn1_w_up", "ffn1_w_down", "mix_norm_g", "w_in", "ret_norm_g", "gla_w_a2",
             "gla_b_a", "gla_norm_g", "w_out", "ffn2_norm_g", "ffn2_w_gate", "ffn2_w_up", "ffn2_w_down", "final_norm_g")
    return (loss_total, dx0[None], *[grads[nm] for nm in order], *[updates[nm][0] for nm in order],
            *[updates[nm][1] for nm in order], *[updates[nm][2] for nm in order])
```

```python
import functools
import math

import jax
import jax.numpy as jnp
from jax import lax
from jax.experimental import pallas as pl
from jax.experimental.pallas import tpu as pltpu

F32 = jnp.float32
BF16 = jnp.bfloat16
MESH = pl.DeviceIdType.MESH
HBM = pl.BlockSpec(memory_space=pltpu.HBM)

N_DEV = 8
RMS_EPS = 1e-6
ROPE_BASE = 10000.0
HEADS = 4
DK = 64
DV = 128
QK_W = HEADS * DK
V_W = HEADS * DV
GATE_RANK = 16
GATE_NORM = 16.0
CHUNK = 64
SUPER = 256
PROJ_W = 3200
C_RQ, C_RK, C_RV, C_RG, C_GQ, C_GK, C_GV, C_GG, C_GL = 0, 256, 512, 1024, 1536, 1792, 2048, 2560, 3072
GL_W = PROJ_W - C_GL
ADAM_LR, ADAM_B1, ADAM_B2, ADAM_EPS, ADAM_WD, ADAM_STEP = 0.001, 0.9, 0.999, 1e-08, 0.01, 10
VMEM_LIMIT_V7X = 52 * 1024 * 1024


def _cparams(**kw):
    return pltpu.CompilerParams(vmem_limit_bytes=VMEM_LIMIT_V7X, **kw)


def _dot(a, b, form, precision=None):
    dims = {"nn": (((1,), (0,)), ((), ())), "nt": (((1,), (1,)), ((), ())), "tn": (((0,), (0,)), ((), ()))}[form]
    return lax.dot_general(a, b, dims, preferred_element_type=F32, precision=precision)


def _sigmoid(x):
    return 1.0 / (1.0 + jnp.exp(-x))


def _coords():
    return lax.axis_index("x"), lax.axis_index("y"), lax.axis_index("c")


class _NoComm:
    inputs, out_shapes, scratch = (), (), ()


class _AllGather:
    def __init__(self, arrays, kinds):
        self.inputs = tuple(arrays)
        self.kinds = tuple(kinds)
        n = len(arrays)
        self.out_shapes = tuple(
            jax.ShapeDtypeStruct((a.shape[0], N_DEV) + a.shape[1:] if k == "stack" else (N_DEV,) + a.shape, a.dtype)
            for a, k in zip(arrays, kinds))
        self.scratch = (pltpu.SemaphoreType.DMA((n, 7)), pltpu.SemaphoreType.DMA((n, 7)),
                        pltpu.SemaphoreType.DMA((n,)))

    def _ctx(self, srcs, outs, sems):
        send_sems, recv_sems, local_sems = sems
        x, y, c = _coords()
        me, sibling = (x, y, c), (x, y, 1 - c)
        chips = [(1 - x, y), (x, 1 - y), (1 - x, 1 - y)]

        def blk(m, dev):
            k = 4 * dev[0] + 2 * dev[1] + dev[2]
            return outs[m].at[:, k] if self.kinds[m] == "stack" else outs[m].at[k]

        def copy(m, s, block, to, src=None):
            return pltpu.make_async_remote_copy(
                src_ref=blk(m, block) if src is None else src, dst_ref=blk(m, block),
                send_sem=send_sems.at[m, s], recv_sem=recv_sems.at[m, s], device_id=to, device_id_type=MESH)

        def mine(m):
            return pltpu.make_async_copy(srcs[m], blk(m, me), local_sems.at[m])

        def first(m):
            return [copy(m, 0, me, sibling, src=srcs[m])] + [
                copy(m, 1 + j, me, (*chip, c), src=srcs[m]) for j, chip in enumerate(chips)]

        return me, sibling, chips, c, copy, mine, first

    def start(self, srcs, outs, sems):
        me, sibling, chips, c, copy, mine, first = self._ctx(srcs, outs, sems)
        for m in range(len(srcs)):
            mine(m).start()
            for cp in first(m):
                cp.start()

    def mid(self, srcs, outs, sems):
        me, sibling, chips, c, copy, mine, first = self._ctx(srcs, outs, sems)
        for j, chip in enumerate(chips):
            for m in range(len(srcs)):
                copy(m, 1 + j, (*chip, c), me).wait_recv()
                copy(m, 4 + j, (*chip, c), sibling).start()

    def finish(self, srcs, outs, sems):
        me, sibling, chips, c, copy, mine, first = self._ctx(srcs, outs, sems)
        for m in range(len(srcs)):
            copy(m, 0, sibling, me).wait_recv()
            for j, chip in enumerate(chips):
                copy(m, 4 + j, (*chip, 1 - c), me).wait_recv()
            for cp in first(m):
                cp.wait_send()
            for j, chip in enumerate(chips):
                copy(m, 4 + j, (*chip, c), sibling).wait_send()
            mine(m).wait()


RELATIONS = ((0, 0, 1), (1, 0, 0), (0, 1, 0), (1, 1, 0), (1, 0, 1), (0, 1, 1), (1, 1, 1))
NEAR = (0, 1, 2, 4, 5)
FAR = (3, 6)
ALL = NEAR + FAR


class _ReduceScatter:
    def __init__(self, parts):
        self.inputs = tuple(g for g, _ in parts)
        self.slots = tuple(s for _, s in parts)
        self.out_shapes = tuple(jax.ShapeDtypeStruct((len(s),) + g.shape[2:], g.dtype) for g, s in parts)
        n_max = max(len(s) for s in self.slots)
        n = len(parts)
        self.scratch = (pltpu.SemaphoreType.DMA((n, n_max)), pltpu.SemaphoreType.DMA((n, n_max)))

    def _copies(self, srcs, outs, sems):
        send_sems, recv_sems = sems
        x, y, c = _coords()
        copies = []
        for m, slots in enumerate(self.slots):
            for i, s in enumerate(slots):
                fx, fy, fc = RELATIONS[s]
                px = 1 - x if fx else x
                py = 1 - y if fy else y
                pc = 1 - c if fc else c
                copies.append(pltpu.make_async_remote_copy(
                    src_ref=srcs[m].at[2 * px + py, pc], dst_ref=outs[m].at[i], send_sem=send_sems.at[m, i],
                    recv_sem=recv_sems.at[m, i], device_id=(px, py, pc), device_id_type=MESH))
        return copies

    def start(self, srcs, outs, sems):
        for cp in self._copies(srcs, outs, sems):
            cp.start()

    def mid(self, srcs, outs, sems):
        pass

    def finish(self, srcs, outs, sems):
        for cp in self._copies(srcs, outs, sems):
            cp.wait()


class _SiblingExchange:
    def __init__(self, grads):
        self.inputs = tuple(grads)
        self.out_shapes = tuple(jax.ShapeDtypeStruct((4,) + g.shape[2:], g.dtype) for g in grads)
        self.scratch = (pltpu.SemaphoreType.DMA((len(grads),)), pltpu.SemaphoreType.DMA((len(grads),)))

    def _copies(self, srcs, outs, sems):
        send_sems, recv_sems = sems
        x, y, c = _coords()
        return [pltpu.make_async_remote_copy(
            src_ref=srcs[m].at[:, 1 - c], dst_ref=outs[m], send_sem=send_sems.at[m], recv_sem=recv_sems.at[m],
            device_id=(x, y, 1 - c), device_id_type=MESH) for m in range(len(srcs))]

    def start(self, srcs, outs, sems):
        for cp in self._copies(srcs, outs, sems):
            cp.start()

    def mid(self, srcs, outs, sems):
        pass

    def finish(self, srcs, outs, sems):
        for cp in self._copies(srcs, outs, sems):
            cp.wait()


class _ChipExchange:
    def __init__(self, partials):
        self.inputs = tuple(partials)
        self.out_shapes = tuple(jax.ShapeDtypeStruct((3,) + p.shape[1:], p.dtype) for p in partials)
        n = len(partials)
        self.scratch = (pltpu.SemaphoreType.DMA((n, 3)), pltpu.SemaphoreType.DMA((n, 3)))

    def _copies(self, srcs, outs, sems):
        send_sems, recv_sems = sems
        x, y, c = _coords()
        copies = []
        for m in range(len(srcs)):
            for j, (px, py) in enumerate([(1 - x, y), (x, 1 - y), (1 - x, 1 - y)]):
                copies.append(pltpu.make_async_remote_copy(
                    src_ref=srcs[m].at[2 * px + py], dst_ref=outs[m].at[j], send_sem=send_sems.at[m, j],
                    recv_sem=recv_sems.at[m, j], device_id=(px, py, c), device_id_type=MESH))
        return copies

    def start(self, srcs, outs, sems):
        for cp in self._copies(srcs, outs, sems):
            cp.start()

    def mid(self, srcs, outs, sems):
        pass

    def finish(self, srcs, outs, sems):
        for cp in self._copies(srcs, outs, sems):
            cp.wait()


class _Multi:
    def __init__(self, comms):
        self.comms = comms
        self.inputs = tuple(a for c in comms for a in c.inputs)
        self.out_shapes = tuple(s for c in comms for s in c.out_shapes)
        self.scratch = tuple(s for c in comms for s in c.scratch)

    def _each(self, phase, srcs, outs, sems):
        i = o = k = 0
        for c in self.comms:
            ni, no, nk = len(c.inputs), len(c.out_shapes), len(c.scratch)
            getattr(c, phase)(srcs[i:i + ni], outs[o:o + no], sems[k:k + nk])
            i, o, k = i + ni, o + no, k + nk

    def start(self, srcs, outs, sems):
        self._each("start", srcs, outs, sems)

    def mid(self, srcs, outs, sems):
        self._each("mid", srcs, outs, sems)

    def finish(self, srcs, outs, sems):
        self._each("finish", srcs, outs, sems)


def _call(name, main, grid, in_specs, out_specs, out_shape, args, scratch=(), comm=None, prefetch=None):
    comm = comm or _NoComm()
    counts = [len(in_specs), len(comm.inputs), len(out_shape), len(comm.out_shapes), len(scratch), len(comm.scratch)]
    n_steps = math.prod(grid)
    hosted = bool(comm.inputs)

    def body(*refs):
        if prefetch is not None:
            refs = refs[1:]
        parts, at = [], 0
        for n in counts:
            parts.append(refs[at:at + n])
            at += n
        ins, c_in, outs, c_out, scr, c_scr = parts
        step = pl.program_id(0)
        for d in range(1, len(grid)):
            step = step * grid[d] + pl.program_id(d)
        if hosted:
            @pl.when(step == 0)
            def _():
                comm.start(c_in, c_out, c_scr)
        main(ins, outs, scr)
        if hosted:
            @pl.when(step == max(n_steps - 2, 0))
            def _():
                comm.mid(c_in, c_out, c_scr)

            @pl.when(step == n_steps - 1)
            def _():
                comm.finish(c_in, c_out, c_scr)

    all_in = list(in_specs) + [HBM] * counts[1]
    all_out = list(out_specs) + [HBM] * counts[3]
    all_scratch = list(scratch) + list(comm.scratch)
    shapes = list(out_shape) + list(comm.out_shapes)
    if prefetch is None:
        res = pl.pallas_call(body, name=name, grid=grid, in_specs=all_in, out_specs=all_out, out_shape=shapes,
                             scratch_shapes=all_scratch, compiler_params=_cparams())(*args, *comm.inputs)
    else:
        res = pl.pallas_call(
            body, name=name, out_shape=shapes,
            grid_spec=pltpu.PrefetchScalarGridSpec(num_scalar_prefetch=1, grid=grid, in_specs=all_in,
                                                   out_specs=all_out, scratch_shapes=all_scratch),
            compiler_params=_cparams())(prefetch, *args, *comm.inputs)
    return res[:counts[2]], res[counts[2]:]


def _rms_fwd(name, x, g, comm=None):
    T, D = x.shape
    tm = min(T, 512)

    def main(ins, outs, scr):
        x_ref, g_ref = ins
        xv = x_ref[...]
        r = lax.rsqrt(jnp.mean(xv * xv, axis=-1, keepdims=True) + RMS_EPS)
        outs[0][...] = (xv * r * g_ref[...]).astype(outs[0].dtype)

    tile = pl.BlockSpec((tm, D), lambda i: (i, 0))
    (h,), extra = _call(name, main, (T // tm,), [tile, pl.BlockSpec((1, D), lambda i: (0, 0))], [tile],
                        [jax.ShapeDtypeStruct((T, D), BF16)], (x, g), comm=comm)
    return h, extra


def _final_loss_epilogue(scale, out_scale):
    def ep(acc, ex, outs):
        res_ref, g_ref, t_ref = ex
        dx_ref, dxb_ref, dg_ref, loss_ref = outs
        n = acc.shape[-1]
        xv = res_ref[...] + scale * acc
        r = lax.rsqrt(jnp.mean(xv * xv, axis=-1, keepdims=True) + RMS_EPS)
        xhat = xv * r
        err = xhat * g_ref[...] - t_ref[...]

        @pl.when(pl.program_id(0) == 0)
        def _():
            dg_ref[...] = jnp.zeros_like(dg_ref)
            loss_ref[...] = jnp.zeros_like(loss_ref)

        loss_ref[...] += jnp.broadcast_to(jnp.sum(err * err) * (0.5 / n), loss_ref.shape)
        dy = err * (1.0 / n)
        dg_ref[...] += jnp.sum(dy * xhat, axis=0, keepdims=True)
        dxhat = dy * g_ref[...]
        dx = r * (dxhat - xhat * jnp.mean(dxhat * xhat, axis=-1, keepdims=True))
        dx_ref[...] = dx
        dxb_ref[...] = (out_scale * dx).astype(dxb_ref.dtype)
    return ep


def _mm_nstream(name, a, ws, w_sel, w_form, comps, out_dtypes, epilogue, cn, rows=1024, comm=None):
    T, K = a.shape
    N = ws[0].shape[1]
    rows = min(rows, T)
    assert N % cn == 0 and T % rows == 0
    n_w, n_c = len(ws), len(comps)

    def main(ins, outs, scr):
        a_ref = ins[0]
        w_refs = ins[1:1 + n_w]
        c_refs = ins[1 + n_w:]

        for r in range(T // rows):
            sl = slice(r * rows, (r + 1) * rows)
            a_blk = a_ref[sl, :]
            dots = [_dot(a_blk, w_ref[...], w_form) for w_ref in w_refs]
            res = epilogue(dots, [c_ref[sl, :] for c_ref in c_refs])
            for o_ref, o in zip(outs, res):
                o_ref[sl, :] = o.astype(o_ref.dtype)

    if w_form == "nt":
        w_specs = [pl.BlockSpec((None, cn, K), functools.partial(lambda j, s: (s, j, 0), s=s)) for s in w_sel]
    else:
        w_specs = [pl.BlockSpec((K, cn), lambda j: (0, j)) for _ in ws]
    chunk = pl.BlockSpec((T, cn), lambda j: (0, j))
    return _call(name, main, (N // cn,), [pl.BlockSpec((T, K), lambda j: (0, 0))] + w_specs + [chunk] * n_c,
                 [chunk] * len(out_dtypes), [jax.ShapeDtypeStruct((T, N), dt) for dt in out_dtypes],
                 (a, *ws, *comps), comm=comm)


def _mm_mstream(name, as_, ws, w_sel, w_form, extras, outs_desc, epilogue, tm=512, comm=None):
    T = as_[0].shape[0]
    tm = min(tm, T)
    n_a = len(as_)
    w_shapes = [w.shape[-2:] for w in ws]
    N = w_shapes[0][1] if w_form == "nn" else w_shapes[0][0]

    def main(ins, outs, scr):
        a_refs = ins[:n_a]
        w_refs = ins[n_a:2 * n_a]
        acc = None
        for a_ref, w_ref in zip(a_refs, w_refs):
            d = _dot(a_ref[...], w_ref[...], w_form)
            acc = d if acc is None else acc + d
        epilogue(acc, ins[2 * n_a:], outs)

    kind_spec = {"tile": pl.BlockSpec((tm, N), lambda i: (i, 0)), "vec": pl.BlockSpec((1, N), lambda i: (0, 0))}
    kind_shape = {"tile": (T, N), "vec": (1, N)}
    a_specs = [pl.BlockSpec((tm, a.shape[1]), lambda i: (i, 0)) for a in as_]
    w_specs = []
    for w, s in zip(ws, w_sel):
        if w.ndim == 3:
            w_specs.append(pl.BlockSpec((None,) + tuple(w.shape[1:]), functools.partial(lambda i, s: (s, 0, 0), s=s),
                                        pipeline_mode=pl.Buffered(1)))
        else:
            w_specs.append(pl.BlockSpec(tuple(w.shape), lambda i: (0, 0), pipeline_mode=pl.Buffered(1)))
    args = list(as_) + list(ws) + [e for e, _ in extras]
    return _call(name, main, (T // tm,), a_specs + w_specs + [kind_spec[k] for _, k in extras],
                 [kind_spec[k] for _, k in outs_desc],
                 [jax.ShapeDtypeStruct(kind_shape[k], dt) for dt, k in outs_desc], args, comm=comm)


def _plain_epilogue(acc, ex, outs):
    outs[0][...] = acc.astype(outs[0].dtype)


def _residual_rms_epilogue(scale):
    def ep(acc, ex, outs):
        xv = ex[0][...] + scale * acc
        outs[0][...] = xv
        r = lax.rsqrt(jnp.mean(xv * xv, axis=-1, keepdims=True) + RMS_EPS)
        outs[1][...] = (xv * r * ex[1][...]).astype(outs[1].dtype)
    return ep


def _rms_bwd_epilogue(out_scale):
    def ep(acc, ex, outs):
        x_ref, g_ref, dres_ref = ex
        dx_ref, dxb_ref, dg_ref = outs
        xv = x_ref[...]
        r = lax.rsqrt(jnp.mean(xv * xv, axis=-1, keepdims=True) + RMS_EPS)
        xhat = xv * r

        @pl.when(pl.program_id(0) == 0)
        def _():
            dg_ref[...] = jnp.zeros_like(dg_ref)

        dg_ref[...] += jnp.sum(acc * xhat, axis=0, keepdims=True)
        dxhat = acc * g_ref[...]
        dx = r * (dxhat - xhat * jnp.mean(dxhat * xhat, axis=-1, keepdims=True)) + dres_ref[...]
        dx_ref[...] = dx
        dxb_ref[...] = (out_scale * dx).astype(dxb_ref.dtype)
    return ep


def _mm_tn(name, a, b, tmo, tno, out_dtype, tk=1024, comm=None):
    T, Ma = a.shape
    Nb = b.shape[1]
    tk = min(tk, T)
    nk = T // tk

    def main(ins, outs, scr):
        a_ref, b_ref = ins
        (acc_ref,) = scr
        k = pl.program_id(2)

        @pl.when(k == 0)
        def _():
            acc_ref[...] = jnp.zeros_like(acc_ref)

        acc_ref[...] += _dot(a_ref[...], b_ref[...], "tn")

        @pl.when(k == nk - 1)
        def _():
            outs[0][...] = acc_ref[...].astype(outs[0].dtype)

    (out,), extra = _call(
        name, main, (Ma // tmo, Nb // tno, nk),
        [pl.BlockSpec((tk, tmo), lambda i, j, k: (k, i)), pl.BlockSpec((tk, tno), lambda i, j, k: (k, j))],
        [pl.BlockSpec((tmo, tno), lambda i, j, k: (i, j))], [jax.ShapeDtypeStruct((Ma, Nb), out_dtype)],
        (a, b), scratch=[pltpu.VMEM((tmo, tno), F32)], comm=comm)
    return out, extra


def _swiglu_parts(g, u):
    s = _sigmoid(g)
    silu = g * s
    return [u * (s + silu * (1.0 - s)), silu, silu * u]


def _silu_mul_epilogue(dots, comps):
    g, u = dots
    return _swiglu_parts(g, u)


def _gate_parts_epilogue(dots, comps):
    (g,) = dots
    s = _sigmoid(g)
    silu = g * s
    return [s + silu * (1.0 - s), silu]


def _up_act_epilogue(dots, comps):
    (u,) = dots
    return [u * comps[0].astype(F32), u * comps[1].astype(F32)]


def _dact_epilogue(dots, comps):
    dact = dots[0].astype(BF16)
    return [dact * comps[0], dact * comps[1]]


def _identity_epilogue(dots, comps):
    return list(dots)


def _swap_halves(x):
    lane = lax.broadcasted_iota(jnp.int32, x.shape, 1)
    first = (lane % DK) < (DK // 2)
    return jnp.where(first, pltpu.roll(x, 128 - DK // 2, 1), pltpu.roll(x, DK // 2, 1))


def _rotary(t, cos, sin_signed):
    halves = []
    for p in range(QK_W // 128):
        th = t[:, 128 * p:128 * (p + 1)]
        halves.append(th * cos + _swap_halves(th) * sin_signed)
    return jnp.concatenate(halves, axis=1)


def _rotary_transposed(d, cos, sin_signed):
    halves = []
    for p in range(QK_W // 128):
        dh = d[:, 128 * p:128 * (p + 1)]
        halves.append(dh * cos + _swap_halves(dh * sin_signed))
    return jnp.concatenate(halves, axis=1)


def _log_sigmoid(x):
    return jnp.minimum(x, 0.0) - jnp.log(1.0 + jnp.exp(-jnp.abs(x)))


def _attn_masks():
    row = lax.broadcasted_iota(jnp.int32, (SUPER, SUPER), 0)
    col = lax.broadcasted_iota(jnp.int32, (SUPER, SUPER), 1)
    same = (row // CHUNK) == (col // CHUNK)
    return row, col, same


def _group_inputs(grp, pr, cos, sin_signed, lg, wa2, ba):
    seg = lambda lo, width: pr[:, lo:lo + width].astype(F32)
    if grp == 0:
        q = _rotary(seg(C_RQ, QK_W), cos, sin_signed)
        k = _rotary(seg(C_RK, QK_W), cos, sin_signed) * (DK ** -0.5)
        v = pr[:, C_RV:C_RV + V_W]
        gate = seg(C_RG, V_W)
        pos = lax.broadcasted_iota(jnp.int32, (SUPER, QK_W), 0).astype(F32) + 1.0
        return q, k, v, gate, pos * lg, None, None
    q = seg(C_GQ, QK_W) * (DK ** -0.5)
    k = seg(C_GK, QK_W)
    v = pr[:, C_GV:C_GV + V_W]
    gate = seg(C_GG, V_W)
    glow = pr[:, C_GL:C_GL + GL_W]
    logit = _dot(glow.astype(BF16), wa2.astype(BF16), "nn") + ba
    la = _log_sigmoid(logit) * (1.0 / GATE_NORM)
    row, col, _ = _attn_masks()
    lower = (col <= row).astype(F32)
    b_cum = _dot(lower, la, "nn", precision=lax.Precision.HIGHEST)
    return q, k, v, gate, b_cum, glow, logit


def _decay_factors(q, k, b_cum):
    c = b_cum[SUPER // 2 - 1:SUPER // 2, :]
    bl = b_cum[SUPER - 1:SUPER, :]
    e1 = jnp.exp(b_cum - c)
    e2 = jnp.exp(c - b_cum)
    e_b = jnp.exp(b_cum)
    e_l = jnp.exp(bl - b_cum)
    return dict(e1=e1, e2=e2, eb=e_b, el=e_l, ebl=jnp.exp(bl),
                qp=q * e1, qm=q * e2, kp=k * e1, km=k * e2, qs=q * e_b, kl=k * e_l)


def _state_block_mask():
    r = lax.broadcasted_iota(jnp.int32, (V_W, QK_W), 0)
    c = lax.broadcasted_iota(jnp.int32, (V_W, QK_W), 1)
    return (r // DV) == (c // DK)


def _attn_fwd(proj, cos, sin_signed, lg, wa2p, ba, gn_ret, gn_gla, x_res, w_out, g_next, comm=None):
    T = proj.shape[0]
    n_s = T // SUPER
    D = x_res.shape[1]

    def main(ins, outs, scr):
        pr_ref, cos_ref, sin_ref, lg_ref, wa2_ref, ba_ref, gr_ref, gg_ref, xres_ref, wout_ref, gnext_ref = ins
        o_ref, y_ref, st_ref, x_ref, h_ref = outs
        (s_ref,) = scr
        i = pl.program_id(0)

        @pl.when(i == 0)
        def _():
            s_ref[...] = jnp.zeros_like(s_ref)

        pr = pr_ref
        row, col, same = _attn_masks()
        m1 = col <= row
        m2 = jnp.logical_and(col > row, same)
        lane = lax.broadcasted_iota(jnp.int32, (1, QK_W), 1)
        blockmask = _state_block_mask()
        for grp in range(2):
            q, k, v, gate, b_cum, _, _ = _group_inputs(grp, pr, cos_ref[...], sin_ref[...], lg_ref[...],
                                                      wa2_ref[...], ba_ref[...])
            f = _decay_factors(q, k, b_cum)
            gn = gr_ref[...] if grp == 0 else gg_ref[...]
            s_prev = s_ref[grp]
            st_ref[0, grp] = s_prev
            o_inter = _dot(f["qs"].astype(BF16), s_prev.astype(BF16), "nt")
            kmb = f["km"].astype(BF16)
            kpb = f["kp"].astype(BF16)
            vb = v.astype(BF16)
            for h in range(HEADS):
                hm = (lane // DK) == h
                a1 = _dot(jnp.where(hm, f["qp"], 0.0).astype(BF16), kmb, "nt")
                a2 = _dot(jnp.where(hm, f["qm"], 0.0).astype(BF16), kpb, "nt")
                a = jnp.where(m1, a1, jnp.where(m2, a2, 0.0))
                lo = grp * V_W + h * DV
                o_h = _dot(a.astype(BF16), vb[:, h * DV:(h + 1) * DV], "nn") + o_inter[:, h * DV:(h + 1) * DV]
                o_ref[:, lo:lo + DV] = o_h
                r = lax.rsqrt(jnp.mean(o_h * o_h, axis=-1, keepdims=True) + RMS_EPS)
                gte = gate[:, h * DV:(h + 1) * DV]
                y = o_h * r * gn[:, h * DV:(h + 1) * DV] * (gte * _sigmoid(gte))
                y_ref[:, lo:lo + DV] = y.astype(y_ref.dtype)
            upd = _dot(vb, f["kl"].astype(BF16), "tn")
            s_ref[grp] = s_prev * f["ebl"] + jnp.where(blockmask, upd, 0.0)
        xv = xres_ref[...] + _dot(y_ref[...], wout_ref[...], "nn")
        x_ref[...] = xv
        r = lax.rsqrt(jnp.mean(xv * xv, axis=-1, keepdims=True) + RMS_EPS)
        h_ref[...] = (xv * r * gnext_ref[...]).astype(h_ref.dtype)

    const = lambda shape: pl.BlockSpec(shape, lambda i: tuple(0 for _ in shape))
    rows = lambda w: pl.BlockSpec((SUPER, w), lambda i: (i, 0))
    return _call(
        "attn_fwd", main, (n_s,),
        [rows(PROJ_W), rows(128), rows(128),
         const((1, QK_W)), const((GL_W, QK_W)), const((1, QK_W)), const((1, V_W)), const((1, V_W)),
         rows(D), const((2 * V_W, D)), const((1, D))],
        [rows(2 * V_W), rows(2 * V_W), pl.BlockSpec((1, 2, V_W, QK_W), lambda i: (i, 0, 0, 0)), rows(D), rows(D)],
        [jax.ShapeDtypeStruct((T, 2 * V_W), F32), jax.ShapeDtypeStruct((T, 2 * V_W), BF16),
         jax.ShapeDtypeStruct((n_s, 2, V_W, QK_W), F32), jax.ShapeDtypeStruct((T, D), F32),
         jax.ShapeDtypeStruct((T, D), BF16)],
        (proj, cos, sin_signed, lg, wa2p, ba, gn_ret, gn_gla, x_res, w_out, g_next),
        scratch=[pltpu.VMEM((2, V_W, QK_W), F32)], comm=comm)


def _attn_bwd(proj, cos, sin_signed, lg, wa2p, ba, gn_ret, gn_gla, o, dx, w_out, states, comm=None):
    T = proj.shape[0]
    n_s = T // SUPER
    D = dx.shape[1]

    def main(ins, outs, scr):
        pr_ref, cos_ref, sin_ref, lg_ref, wa2_ref, ba_ref, gr_ref, gg_ref, o_ref, dx_ref, wout_ref, st_ref = ins
        dp_ref, dgr_ref, dgg_ref, dba_ref, dwa_ref = outs
        (ds_ref, dy_ref) = scr
        i = pl.program_id(0)
        dy_ref[...] = _dot(dx_ref[...], wout_ref[...], "nt")

        @pl.when(i == 0)
        def _():
            ds_ref[...] = jnp.zeros_like(ds_ref)
            dgr_ref[...] = jnp.zeros_like(dgr_ref)
            dgg_ref[...] = jnp.zeros_like(dgg_ref)
            dba_ref[...] = jnp.zeros_like(dba_ref)
            dwa_ref[...] = jnp.zeros_like(dwa_ref)

        pr = pr_ref
        cos = cos_ref[...]
        sin_signed = sin_ref[...]
        row, col, same = _attn_masks()
        m1 = col <= row
        m2 = jnp.logical_and(col > row, same)
        m1t = row <= col
        m2t = jnp.logical_and(row > col, same)
        lane = lax.broadcasted_iota(jnp.int32, (1, QK_W), 1)
        blockmask = _state_block_mask()
        for grp in range(2):
            q, k, v, gate, b_cum, glow, logit = _group_inputs(grp, pr, cos, sin_signed, lg_ref[...],
                                                              wa2_ref[...], ba_ref[...])
            f = _decay_factors(q, k, b_cum)
            gn = gr_ref[...] if grp == 0 else gg_ref[...]
            dgn_ref = dgr_ref if grp == 0 else dgg_ref
            do_parts, dgate_parts, dgn_parts = [], [], []
            for h in range(HEADS):
                lo = grp * V_W + h * DV
                o_h = o_ref[:, lo:lo + DV]
                r = lax.rsqrt(jnp.mean(o_h * o_h, axis=-1, keepdims=True) + RMS_EPS)
                n = o_h * r
                gte = gate[:, h * DV:(h + 1) * DV]
                sg = _sigmoid(gte)
                dy_h = dy_ref[:, lo:lo + DV]
                gn_h = gn[:, h * DV:(h + 1) * DV]
                dgate_parts.append(dy_h * n * gn_h * (sg * (1.0 + gte * (1.0 - sg))))
                dz = dy_h * (gte * sg)
                dgn_parts.append(jnp.sum(dz * n, axis=0, keepdims=True))
                dn = dz * gn_h
                do_parts.append(r * (dn - n * jnp.mean(dn * n, axis=-1, keepdims=True)))
            dgn_ref[...] += jnp.concatenate(dgn_parts, axis=1)
            dgate = jnp.concatenate(dgate_parts, axis=1)
            do = jnp.concatenate(do_parts, axis=1)
            dob = do.astype(BF16)
            vb = v.astype(BF16)
            s_prev = st_ref[0, grp]
            ds_new = ds_ref[grp]
            dsb = ds_new.astype(BF16)
            qpb, qmb = f["qp"].astype(BF16), f["qm"].astype(BF16)
            kpb, kmb = f["kp"].astype(BF16), f["km"].astype(BF16)
            dqp = jnp.zeros((SUPER, QK_W), F32)
            dqm = jnp.zeros((SUPER, QK_W), F32)
            dkp = jnp.zeros((SUPER, QK_W), F32)
            dkm = jnp.zeros((SUPER, QK_W), F32)
            dv_parts = []
            for h in range(HEADS):
                hm = (lane // DK) == h
                qp_h = jnp.where(hm, f["qp"], 0.0).astype(BF16)
                qm_h = jnp.where(hm, f["qm"], 0.0).astype(BF16)
                kp_h = jnp.where(hm, f["kp"], 0.0).astype(BF16)
                km_h = jnp.where(hm, f["km"], 0.0).astype(BF16)
                at = jnp.where(m1t, _dot(km_h, qpb, "nt"), jnp.where(m2t, _dot(kp_h, qmb, "nt"), 0.0))
                do_h = dob[:, h * DV:(h + 1) * DV]
                v_h = vb[:, h * DV:(h + 1) * DV]
                dv_parts.append(_dot(at.astype(BF16), do_h, "nn"))
                da = _dot(do_h, v_h, "nt")
                dat = _dot(v_h, do_h, "nt")
                da1 = jnp.where(m1, da, 0.0).astype(BF16)
                da2 = jnp.where(m2, da, 0.0).astype(BF16)
                da1t = jnp.where(m1t, dat, 0.0).astype(BF16)
                da2t = jnp.where(m2t, dat, 0.0).astype(BF16)
                dqp = dqp + _dot(da1, km_h, "nn")
                dqm = dqm + _dot(da2, kp_h, "nn")
                dkm = dkm + _dot(da1t, qp_h, "nn")
                dkp = dkp + _dot(da2t, qm_h, "nn")
            klb = f["kl"].astype(BF16)
            qsb = f["qs"].astype(BF16)
            dqs = _dot(dob, s_prev.astype(BF16), "nn")
            dkl = _dot(vb, dsb, "nn")
            dv = jnp.concatenate(dv_parts, axis=1) + _dot(klb, dsb, "nt")
            ds_ref[grp] = ds_new * f["ebl"] + jnp.where(blockmask, _dot(dob, qsb, "tn"), 0.0)
            dq = dqp * f["e1"] + dqm * f["e2"] + dqs * f["eb"]
            dk = dkm * f["e2"] + dkp * f["e1"] + dkl * f["el"]
            if grp == 0:
                dq = _rotary_transposed(dq, cos, sin_signed)
                dk = _rotary_transposed(dk * (DK ** -0.5), cos, sin_signed)
                dp_ref[:, C_RQ:C_RQ + QK_W] = dq.astype(dp_ref.dtype)
                dp_ref[:, C_RK:C_RK + QK_W] = dk.astype(dp_ref.dtype)
                dp_ref[:, C_RV:C_RV + V_W] = dv.astype(dp_ref.dtype)
                dp_ref[:, C_RG:C_RG + V_W] = dgate.astype(dp_ref.dtype)
            else:
                dkl_kl = dkl * klb.astype(F32)
                db = (dqp * qpb.astype(F32) - dkm * kmb.astype(F32) - dqm * qmb.astype(F32)
                      + dkp * kpb.astype(F32) + dqs * qsb.astype(F32) - dkl_kl)
                last = (jnp.sum(dkl_kl, axis=0, keepdims=True)
                        + f["ebl"] * jnp.sum(s_prev * ds_new, axis=0, keepdims=True))
                rowq = lax.broadcasted_iota(jnp.int32, (SUPER, QK_W), 0)
                db = db + jnp.where(rowq == SUPER - 1, last, 0.0)
                upper = (col >= row).astype(F32)
                dla = _dot(upper, db, "nn", precision=lax.Precision.HIGHEST)
                dlogit = dla * (1.0 / GATE_NORM) * (1.0 - _sigmoid(logit))
                dlb = dlogit.astype(BF16)
                dglow = _dot(dlb, wa2_ref[...].astype(BF16), "nt")
                dwa_ref[...] += _dot(glow.astype(BF16), dlb, "tn")
                dba_ref[...] += jnp.sum(dlogit, axis=0, keepdims=True)
                dp_ref[:, C_GQ:C_GQ + QK_W] = (dq * (DK ** -0.5)).astype(dp_ref.dtype)
                dp_ref[:, C_GK:C_GK + QK_W] = dk.astype(dp_ref.dtype)
                dp_ref[:, C_GV:C_GV + V_W] = dv.astype(dp_ref.dtype)
                dp_ref[:, C_GG:C_GG + V_W] = dgate.astype(dp_ref.dtype)
                dp_ref[:, C_GL:C_GL + GL_W] = dglow.astype(dp_ref.dtype)

    rev = lambda i: n_s - 1 - i
    const = lambda shape: pl.BlockSpec(shape, lambda i: tuple(0 for _ in shape))
    return _call(
        "attn_bwd", main, (n_s,),
        [pl.BlockSpec((SUPER, PROJ_W), lambda i: (rev(i), 0)),
         pl.BlockSpec((SUPER, 128), lambda i: (rev(i), 0)), pl.BlockSpec((SUPER, 128), lambda i: (rev(i), 0)),
         const((1, QK_W)), const((GL_W, QK_W)), const((1, QK_W)), const((1, V_W)), const((1, V_W)),
         pl.BlockSpec((SUPER, 2 * V_W), lambda i: (rev(i), 0)),
         pl.BlockSpec((SUPER, D), lambda i: (rev(i), 0)), const((2 * V_W, D)),
         pl.BlockSpec((1, 2, V_W, QK_W), lambda i: (rev(i), 0, 0, 0))],
        [pl.BlockSpec((SUPER, PROJ_W), lambda i: (rev(i), 0)),
         const((1, V_W)), const((1, V_W)), const((1, QK_W)), const((GL_W, QK_W))],
        [jax.ShapeDtypeStruct((T, PROJ_W), BF16),
         jax.ShapeDtypeStruct((1, V_W), F32), jax.ShapeDtypeStruct((1, V_W), F32),
         jax.ShapeDtypeStruct((1, QK_W), F32), jax.ShapeDtypeStruct((GL_W, QK_W), F32)],
        (proj, cos, sin_signed, lg, wa2p, ba, gn_ret, gn_gla, o, dx, w_out, states),
        scratch=[pltpu.VMEM((2, V_W, QK_W), F32), pltpu.VMEM((SUPER, 2 * V_W), F32)], comm=comm)


def _rotary_tables(T):
    half = DK // 2
    inv = ROPE_BASE ** (-jnp.arange(half, dtype=F32) * 2.0 / DK)
    ang = jnp.arange(T, dtype=F32)[:, None] * inv[None, :]
    cos, sin = jnp.cos(ang), jnp.sin(ang)
    cos_head = jnp.concatenate([cos, cos], axis=1)
    sin_head = jnp.concatenate([-sin, sin], axis=1)
    return jnp.tile(cos_head, (1, 128 // DK)), jnp.tile(sin_head, (1, 128 // DK))


def _sum_devices(name, gathered, m_per):
    def body(g_ref, o_ref):
        acc = g_ref[0:m_per, :]
        for k in range(1, N_DEV):
            acc = acc + g_ref[k * m_per:(k + 1) * m_per, :]
        o_ref[...] = acc

    return pl.pallas_call(body, name=name, out_shape=jax.ShapeDtypeStruct((m_per, 128), F32))(gathered)


def _pair_sum(name, grad, landed, core, comm=None):
    R, C = grad.shape[2:]

    def main(ins, outs, scr):
        outs[0][...] = (ins[0][...].astype(F32) + ins[1][...].astype(F32)).astype(outs[0].dtype)

    blk = pl.BlockSpec((None, R, C), lambda j, s: (j, 0, 0))
    (out,), extra = _call(name, main, (4,), [pl.BlockSpec((None, None, R, C), lambda j, s: (j, s[0], 0, 0)), blk],
                          [blk], [jax.ShapeDtypeStruct((4, R, C), BF16)], (grad, landed), prefetch=core, comm=comm)
    return out, extra


def _owner_sums(name, items, owner, comm=None):
    counts = [1 + len(landed) for _, landed in items]

    def main(ins, outs, scr):
        at = 0
        for o_ref, n in zip(outs, counts):
            acc = ins[at][...].astype(F32)
            for l_ref in ins[at + 1:at + n]:
                for j in range(l_ref.shape[0]):
                    acc = acc + l_ref[j].astype(F32)
            o_ref[...] = acc
            at += n

    once = pl.Buffered(1)
    in_specs, out_specs, out_shape, args = [], [], [], []
    for grad, landed in items:
        R, C = grad.shape[-2:]
        if grad.ndim == 4:
            in_specs.append(pl.BlockSpec((None, None, R, C), lambda i, s: (s[0], s[1], 0, 0), pipeline_mode=once))
        else:
            in_specs.append(pl.BlockSpec((None, R, C), lambda i, s: (s[0], 0, 0), pipeline_mode=once))
        in_specs += [pl.BlockSpec(tuple(l.shape), lambda i, s: (0, 0, 0), pipeline_mode=once) for l in landed]
        out_specs.append(pl.BlockSpec((R, C), lambda i, s: (0, 0)))
        out_shape.append(jax.ShapeDtypeStruct((R, C), F32))
        args += [grad, *landed]
    return _call(name, main, (1,), in_specs, out_specs, out_shape, args, comm=comm, prefetch=owner)


def _rms_bwd(name, x, g, dh, dres, out_scale):
    T, D = x.shape
    tm = min(T, 512)
    ep = _rms_bwd_epilogue(out_scale)

    def main(ins, outs, scr):
        x_ref, g_ref, dh_ref, dres_ref = ins
        ep(dh_ref[...], (x_ref, g_ref, dres_ref), outs)

    tile = pl.BlockSpec((tm, D), lambda i: (i, 0))
    vec = pl.BlockSpec((1, D), lambda i: (0, 0))
    outs, _ = _call(name, main, (T // tm,), [tile, vec, tile, tile], [tile, tile, vec],
                    [jax.ShapeDtypeStruct((T, D), F32), jax.ShapeDtypeStruct((T, D), BF16),
                     jax.ShapeDtypeStruct((1, D), F32)], (x, g, dh, dres))
    return outs


def _adamw_group(name, items, n_blocks, comm=None):
    n = len(items)

    def main(ins, outs, scr):
        for p in range(n):
            g_ref, w_ref, m_ref, v_ref = ins[4 * p:4 * p + 4]
            d_ref, nm_ref, nv_ref = outs[3 * p:3 * p + 3]
            gv = g_ref[...]
            nm = ADAM_B1 * m_ref[...] + (1.0 - ADAM_B1) * gv
            nv = ADAM_B2 * v_ref[...] + (1.0 - ADAM_B2) * (gv * gv)
            m_hat = nm / (1.0 - ADAM_B1 ** ADAM_STEP)
            v_hat = nv / (1.0 - ADAM_B2 ** ADAM_STEP)
            d_ref[...] = -ADAM_LR * (m_hat / (jnp.sqrt(v_hat) + ADAM_EPS) + ADAM_WD * w_ref[...])
            nm_ref[...] = nm
            nv_ref[...] = nv

    in_specs, out_specs, out_shape, args = [], [], [], []
    for item in items:
        R, C = item[1].shape
        assert R % n_blocks == 0
        spec = pl.BlockSpec((R // n_blocks, C), lambda i: (i, 0))
        in_specs += [spec] * 4
        out_specs += [spec] * 3
        out_shape += [jax.ShapeDtypeStruct((R, C), F32)] * 3
        args += list(item)
    outs, extra = _call(name, main, (n_blocks,), in_specs, out_specs, out_shape, args, comm=comm)
    return [tuple(outs[3 * p:3 * p + 3]) for p in range(n)], extra


SMALL_ORDER = ("ffn1", "mix", "ffn2", "final", "ret", "gla", "b_a")


def kernel(x, ffn1_norm_g, ffn1_w_gate, ffn1_w_up, ffn1_w_down, mix_norm_g, w_in, ret_norm_g, gla_w_a2, gla_b_a, gla_norm_g, w_out, ffn2_norm_g, ffn2_w_gate, ffn2_w_up, ffn2_w_down, final_norm_g, loss_target, m_ffn1_norm_g, m_ffn1_w_gate, m_ffn1_w_up, m_ffn1_w_down, m_mix_norm_g, m_w_in, m_ret_norm_g, m_gla_w_a2, m_gla_b_a, m_gla_norm_g, m_w_out, m_ffn2_norm_g, m_ffn2_w_gate, m_ffn2_w_up, m_ffn2_w_down, m_final_norm_g, v_ffn1_norm_g, v_ffn1_w_gate, v_ffn1_w_up, v_ffn1_w_down, v_mix_norm_g, v_w_in, v_ret_norm_g, v_gla_w_a2, v_gla_b_a, v_gla_norm_g, v_w_out, v_ffn2_norm_g, v_ffn2_w_gate, v_ffn2_w_up, v_ffn2_w_down, v_final_norm_g):
    xi, yi, ci = _coords()
    dev = 4 * xi + 2 * yi + ci
    owner = jnp.stack([2 * xi + yi, ci]).astype(jnp.int32)

    x0, target = x[0], loss_target[0]
    T, D = x0.shape
    fb = ffn1_w_gate.shape[2]
    ib = w_in.shape[2]
    ab = gla_w_a2.shape[2]
    F = N_DEV * fb
    cos, sin_signed = _rotary_tables(T)
    lg = jnp.repeat(jnp.log(1.0 - 2.0 ** (-5.0 - jnp.arange(HEADS, dtype=F32))), DK)[None, :]
    g_final = final_norm_g.reshape(1, D)

    g1_loc = ffn1_w_gate[0].T[None].astype(BF16)
    u1_loc = ffn1_w_up[0].T[None].astype(BF16)
    d1_loc = ffn1_w_down.astype(BF16)
    g2_loc = ffn2_w_gate[0].T[None].astype(BF16)
    u2_loc = ffn2_w_up[0].T[None].astype(BF16)
    d2_loc = ffn2_w_down.astype(BF16)
    in_loc = w_in[0].T.astype(BF16)
    out_loc = w_out[0].astype(BF16)

    h1, (g1,) = _rms_fwd("ffn1_rms", x0, ffn1_norm_g, comm=_AllGather([g1_loc], ["stack"]))
    g1 = g1.reshape(1, F, D)
    (dsl1, sl1), (u1,) = _mm_nstream("ffn1_gate", h1, [g1], [0], "nt", [], [BF16, BF16], _gate_parts_epilogue, cn=256,
                                     comm=_AllGather([u1_loc], ["stack"]))
    u1 = u1.reshape(1, F, D)
    (dsu1, act1), (d1,) = _mm_nstream("ffn1_up", h1, [u1], [0], "nt", [dsl1, sl1], [BF16, BF16],
                                      _up_act_epilogue, cn=256, comm=_AllGather([d1_loc], ["stack"]))
    d1 = d1.reshape(1, F, D)
    f32_tile, bf16_tile, f32_vec = (F32, "tile"), (BF16, "tile"), (F32, "vec")
    (x1, h2), (in_all, a_all) = _mm_mstream(
        "ffn1_down", [act1], [d1], [0], "nn", [(x0, "tile"), (mix_norm_g, "vec")], [f32_tile, bf16_tile],
        _residual_rms_epilogue(0.5), comm=_AllGather([in_loc, gla_w_a2[0]], ["plain", "plain"]))
    w_in_t = jnp.pad(in_all.reshape(1, N_DEV * ib, D), ((0, 0), (0, PROJ_W - N_DEV * ib), (0, 0)))
    wa2 = jnp.transpose(a_all, (1, 0, 2)).reshape(GATE_RANK, N_DEV * ab)
    wa2p = jnp.pad(wa2, ((0, GL_W - GATE_RANK), (0, 0)))

    (proj,), (g2, out_all) = _mm_nstream("mix_proj", h2, [w_in_t], [0], "nt", [], [BF16], _identity_epilogue, cn=640,
                                         comm=_AllGather([g2_loc, out_loc], ["stack", "plain"]))
    w_out_full = out_all.reshape(D, D)
    (o, ymix, states, x2, h3), (u2,) = _attn_fwd(proj, cos, sin_signed, lg, wa2p, gla_b_a, ret_norm_g, gla_norm_g,
                                                 x1, w_out_full, ffn2_norm_g, comm=_AllGather([u2_loc], ["stack"]))
    g2, u2 = g2.reshape(1, F, D), u2.reshape(1, F, D)

    (dsu2, sl2, act2), (d2,) = _mm_nstream(
        "ffn2_up", h3, [g2, u2], [0, 0], "nt", [], [BF16, BF16, BF16], _silu_mul_epilogue, cn=256,
        comm=_AllGather([d2_loc], ["stack"]))
    d2 = d2.reshape(1, F, D)
    (dx3, dy3b, d_final, loss), _ = _mm_mstream(
        "ffn2_down", [act2], [d2], [0], "nn", [(x2, "tile"), (g_final, "vec"), (target, "tile")],
        [f32_tile, bf16_tile, f32_vec, f32_vec], _final_loss_epilogue(0.5, 0.5))

    dwd2, _ = _mm_tn("ffn2b_dwd", act2, dy3b, F // 2, D, BF16)
    dwd2 = dwd2.reshape(4, 2, fb, D)
    (dgate2, dup2), (l_wd2_near,) = _mm_nstream("ffn2b_dact", dy3b, [d2], [0], "nt", [dsu2, sl2], [BF16, BF16],
                                                _dact_epilogue, cn=256, comm=_ReduceScatter([(dwd2, NEAR)]))
    dwg2, (l_wd2_far,) = _mm_tn("ffn2b_dwg", dgate2, h3, F // 2, D, BF16, comm=_ReduceScatter([(dwd2, FAR)]))
    dwg2 = dwg2.reshape(4, 2, fb, D)
    dwu2, (l_wg2_near,) = _mm_tn("ffn2b_dwu", dup2, h3, F // 2, D, BF16, comm=_ReduceScatter([(dwg2, NEAR)]))
    dwu2 = dwu2.reshape(4, 2, fb, D)
    rms_outs = [f32_tile, bf16_tile, f32_vec]
    (dx2, dx2b, d_g2), (l_wg2_far, l_wu2_near) = _mm_mstream(
        "ffn2b_dh", [dgate2, dup2], [g2, u2], [0, 0], "nn", [(x2, "tile"), (ffn2_norm_g, "vec"), (dx3, "tile")],
        rms_outs, _rms_bwd_epilogue(1.0), comm=_ReduceScatter([(dwg2, FAR), (dwu2, NEAR)]))

    dwout, _ = _mm_tn("mixb_dwout", ymix, dx2b, D, D, BF16)
    dwout = dwout.reshape(4, 2, D // N_DEV, D)
    (dproj, d_ret, d_gla, d_ba, d_wa2p), (l_wu2_far, l_wout) = _attn_bwd(
        proj, cos, sin_signed, lg, wa2p, gla_b_a, ret_norm_g, gla_norm_g, o, dx2b, w_out_full, states,
        comm=_ReduceScatter([(dwu2, FAR), (dwout, ALL)]))
    dwin_t, _ = _mm_tn("mixb_dwin", dproj, h2, 640, D, BF16, tk=2048)
    dwin = dwin_t[:N_DEV * ib].reshape(4, 2, ib, D)
    (dx1, dy1b, d_gmix), (l_win_near,) = _mm_mstream(
        "mixb_dh", [dproj], [w_in_t], [0], "nn", [(x1, "tile"), (mix_norm_g, "vec"), (dx2, "tile")],
        rms_outs, _rms_bwd_epilogue(0.5), comm=_ReduceScatter([(dwin, NEAR)]))

    dwd1, (l_win_far,) = _mm_tn("ffn1b_dwd", act1, dy1b, F // 2, D, BF16, comm=_ReduceScatter([(dwin, FAR)]))
    dwd1 = dwd1.reshape(4, 2, fb, D)
    (dgate1, dup1), (l_wd1_near,) = _mm_nstream("ffn1b_dact", dy1b, [d1], [0], "nt", [dsu1, sl1], [BF16, BF16],
                                                _dact_epilogue, cn=256, comm=_ReduceScatter([(dwd1, NEAR)]))
    dwg1, (l_wd1_far,) = _mm_tn("ffn1b_dwg", dgate1, h1, F // 2, D, BF16, comm=_ReduceScatter([(dwd1, FAR)]))
    dwg1 = dwg1.reshape(4, 2, fb, D)
    core = owner[1:2]
    dwu1, (s_wg1,) = _mm_tn("ffn1b_dwu", dup1, h1, F // 2, D, BF16, comm=_SiblingExchange([dwg1]))
    dwu1 = dwu1.reshape(4, 2, fb, D)
    p_wg1, (s_wu1,) = _pair_sum("pair_wg1", dwg1, s_wg1, core, comm=_SiblingExchange([dwu1]))
    p_wu1, _ = _pair_sum("pair_wu1", dwu1, s_wu1, core)
    (dh1,), (c_wg1, c_wu1) = _mm_mstream(
        "ffn1b_dh", [dgate1, dup1], [g1, u1], [0, 0], "nn", [], [f32_tile], _plain_epilogue,
        comm=_ChipExchange([p_wg1, p_wu1]))
    dx0, _, d_g1 = _rms_bwd("ffn1b_rms", x0, ffn1_norm_g, dh1, dx1, 1.0)

    small = dict(ffn1=d_g1, mix=d_gmix, ffn2=d_g2, final=d_final, ret=d_ret, gla=d_gla, b_a=d_ba)
    flat = jnp.concatenate([small[k].reshape(-1) for k in SMALL_ORDER]
                           + [d_wa2p[:GATE_RANK].reshape(-1), loss[0, :128]])
    rows = -(-flat.shape[0] // 128)
    rows = -(-rows // 8) * 8
    packed = jnp.pad(flat, (0, rows * 128 - flat.shape[0])).reshape(rows, 128)

    transposed = ("ffn1_w_gate", "ffn1_w_up", "ffn2_w_gate", "ffn2_w_up", "w_in")

    def to_2d(nm, a):
        if nm in transposed:
            return a[0].T
        return a.reshape((1, a.shape[0]) if a.ndim == 1 else a.shape[-2:])

    def from_2d(nm, a):
        return a.T[None] if nm in transposed else a.reshape(params[nm][0].shape)

    sums_a, (gathered,) = _owner_sums(
        "sum_a", [(dwg2, [l_wg2_near, l_wg2_far]), (dwu2, [l_wu2_near, l_wu2_far]), (dwd2, [l_wd2_near, l_wd2_far]),
                  (dwin, [l_win_near, l_win_far]), (dwout, [l_wout])], owner,
        comm=_AllGather([packed], ["plain"]))
    sums_b, _ = _owner_sums("sum_b", [(p_wg1, [c_wg1]), (p_wu1, [c_wu1]), (dwd1, [l_wd1_near, l_wd1_far])], owner)
    big_grads = {"ffn2_w_gate": sums_a[0], "ffn2_w_up": sums_a[1], "ffn2_w_down": sums_a[2], "w_out": sums_a[4],
                 "ffn1_w_gate": sums_b[0], "ffn1_w_up": sums_b[1], "ffn1_w_down": sums_b[2]}
    params = dict(
        ffn2_w_gate=(ffn2_w_gate, m_ffn2_w_gate, v_ffn2_w_gate), ffn2_w_up=(ffn2_w_up, m_ffn2_w_up, v_ffn2_w_up),
        ffn2_w_down=(ffn2_w_down, m_ffn2_w_down, v_ffn2_w_down), w_in=(w_in, m_w_in, v_w_in),
        w_out=(w_out, m_w_out, v_w_out), ffn1_w_gate=(ffn1_w_gate, m_ffn1_w_gate, v_ffn1_w_gate),
        ffn1_w_up=(ffn1_w_up, m_ffn1_w_up, v_ffn1_w_up), ffn1_w_down=(ffn1_w_down, m_ffn1_w_down, v_ffn1_w_down),
        ffn1_norm_g=(ffn1_norm_g, m_ffn1_norm_g, v_ffn1_norm_g), mix_norm_g=(mix_norm_g, m_mix_norm_g, v_mix_norm_g),
        ret_norm_g=(ret_norm_g, m_ret_norm_g, v_ret_norm_g), gla_w_a2=(gla_w_a2, m_gla_w_a2, v_gla_w_a2),
        gla_b_a=(gla_b_a, m_gla_b_a, v_gla_b_a), gla_norm_g=(gla_norm_g, m_gla_norm_g, v_gla_norm_g),
        ffn2_norm_g=(ffn2_norm_g, m_ffn2_norm_g, v_ffn2_norm_g), final_norm_g=(final_norm_g, m_final_norm_g, v_final_norm_g))
    grads, updates = {}, {}

    def run_adam(name, names, grad_2d, n_blocks):
        items = [(grad_2d[nm],) + tuple(to_2d(nm, a) for a in params[nm]) for nm in names]
        res, _ = _adamw_group(name, items, n_blocks)
        for nm, r in zip(names, res):
            grads[nm] = from_2d(nm, grad_2d[nm])
            updates[nm] = tuple(from_2d(nm, a) for a in r)

    run_adam("adamw_big", list(big_grads), big_grads, 4)
    run_adam("adamw_w_in", ["w_in"], {"w_in": sums_a[3]}, 1)

    total = _sum_devices("sum_small", gathered.reshape(N_DEV * rows, 128), rows).reshape(-1)
    sizes = [small[k].size for k in SMALL_ORDER] + [GATE_RANK * QK_W, 128]
    offs = [0]
    for s in sizes:
        offs.append(offs[-1] + s)
    pieces = [total[offs[i]:offs[i + 1]] for i in range(len(sizes))]
    g_small = {k: pieces[i].reshape(small[k].shape) for i, k in enumerate(SMALL_ORDER)}
    g_wa2_full = pieces[len(SMALL_ORDER)].reshape(GATE_RANK, QK_W)
    g_wa2 = lax.dynamic_slice(g_wa2_full, (0, dev * ab), (GATE_RANK, ab))
    loss_total = pieces[len(SMALL_ORDER) + 1][0]

    small_grads = {"ffn1_norm_g": g_small["ffn1"], "mix_norm_g": g_small["mix"], "ret_norm_g": g_small["ret"],
                   "gla_w_a2": g_wa2, "gla_b_a": g_small["b_a"], "gla_norm_g": g_small["gla"],
                   "ffn2_norm_g": g_small["ffn2"], "final_norm_g": g_small["final"]}
    run_adam("adamw_small", list(small_grads), small_grads, 1)

    order = ("ffn1_norm_g", "ffn1_w_gate", "ffn1_w_up", "ffn1_w_down", "mix_norm_g", "w_in", "ret_norm_g", "gla_w_a2",
             "gla_b_a", "gla_norm_g", "w_out", "ffn2_norm_g", "ffn2_w_gate", "ffn2_w_up", "ffn2_w_down", "final_norm_g")
    return (loss_total, dx0[None], *[grads[nm] for nm in order], *[updates[nm][0] for nm in order],
            *[updates[nm][1] for nm in order], *[updates[nm][2] for nm in order])
```

```python
import functools
import math

import jax
import jax.numpy as jnp
from jax import lax
from jax.experimental import pallas as pl
from jax.experimental.pallas import tpu as pltpu

F32 = jnp.float32
BF16 = jnp.bfloat16
MESH = pl.DeviceIdType.MESH
HBM = pl.BlockSpec(memory_space=pltpu.HBM)

N_DEV = 8
RMS_EPS = 1e-6
ROPE_BASE = 10000.0
HEADS = 4
DK = 64
DV = 128
QK_W = HEADS * DK
V_W = HEADS * DV
GATE_RANK = 16
GATE_NORM = 16.0
CHUNK = 64
SUPER = 256
PROJ_W = 3200
C_RQ, C_RK, C_RV, C_RG, C_GQ, C_GK, C_GV, C_GG, C_GL = 0, 256, 512, 1024, 1536, 1792, 2048, 2560, 3072
GL_W = PROJ_W - C_GL
ADAM_LR, ADAM_B1, ADAM_B2, ADAM_EPS, ADAM_WD, ADAM_STEP = 0.001, 0.9, 0.999, 1e-08, 0.01, 10
VMEM_LIMIT_V7X = 52 * 1024 * 1024


def _cparams(**kw):
    return pltpu.CompilerParams(vmem_limit_bytes=VMEM_LIMIT_V7X, **kw)


def _dot(a, b, form, precision=None):
    dims = {"nn": (((1,), (0,)), ((), ())), "nt": (((1,), (1,)), ((), ())), "tn": (((0,), (0,)), ((), ()))}[form]
    return lax.dot_general(a, b, dims, preferred_element_type=F32, precision=precision)


def _sigmoid(x):
    return 1.0 / (1.0 + jnp.exp(-x))


def _coords():
    return lax.axis_index("x"), lax.axis_index("y"), lax.axis_index("c")


class _NoComm:
    inputs, out_shapes, scratch = (), (), ()


class _AllGather:
    def __init__(self, arrays, kinds):
        self.inputs = tuple(arrays)
        self.kinds = tuple(kinds)
        n = len(arrays)
        self.out_shapes = tuple(
            jax.ShapeDtypeStruct((a.shape[0], N_DEV) + a.shape[1:] if k == "stack" else (N_DEV,) + a.shape, a.dtype)
            for a, k in zip(arrays, kinds))
        self.scratch = (pltpu.SemaphoreType.DMA((n, 7)), pltpu.SemaphoreType.DMA((n, 7)),
                        pltpu.SemaphoreType.DMA((n,)))

    def _ctx(self, srcs, outs, sems):
        send_sems, recv_sems, local_sems = sems
        x, y, c = _coords()
        me, sibling = (x, y, c), (x, y, 1 - c)
        chips = [(1 - x, y), (x, 1 - y), (1 - x, 1 - y)]

        def blk(m, dev):
            k = 4 * dev[0] + 2 * dev[1] + dev[2]
            return outs[m].at[:, k] if self.kinds[m] == "stack" else outs[m].at[k]

        def copy(m, s, block, to, src=None):
            return pltpu.make_async_remote_copy(
                src_ref=blk(m, block) if src is None else src, dst_ref=blk(m, block),
                send_sem=send_sems.at[m, s], recv_sem=recv_sems.at[m, s], device_id=to, device_id_type=MESH)

        def mine(m):
            return pltpu.make_async_copy(srcs[m], blk(m, me), local_sems.at[m])

        def first(m):
            return [copy(m, 0, me, sibling, src=srcs[m])] + [
                copy(m, 1 + j, me, (*chip, c), src=srcs[m]) for j, chip in enumerate(chips)]

        return me, sibling, chips, c, copy, mine, first

    def start(self, srcs, outs, sems):
        me, sibling, chips, c, copy, mine, first = self._ctx(srcs, outs, sems)
        for m in range(len(srcs)):
            mine(m).start()
            for cp in first(m):
                cp.start()

    def mid(self, srcs, outs, sems):
        me, sibling, chips, c, copy, mine, first = self._ctx(srcs, outs, sems)
        for j, chip in enumerate(chips):
            for m in range(len(srcs)):
                copy(m, 1 + j, (*chip, c), me).wait_recv()
                copy(m, 4 + j, (*chip, c), sibling).start()

    def finish(self, srcs, outs, sems):
        me, sibling, chips, c, copy, mine, first = self._ctx(srcs, outs, sems)
        for m in range(len(srcs)):
            copy(m, 0, sibling, me).wait_recv()
            for j, chip in enumerate(chips):
                copy(m, 4 + j, (*chip, 1 - c), me).wait_recv()
            for cp in first(m):
                cp.wait_send()
            for j, chip in enumerate(chips):
                copy(m, 4 + j, (*chip, c), sibling).wait_send()
            mine(m).wait()


RELATIONS = ((0, 0, 1), (1, 0, 0), (0, 1, 0), (1, 1, 0), (1, 0, 1), (0, 1, 1), (1, 1, 1))
NEAR = (0, 1, 2, 4, 5)
FAR = (3, 6)
ALL = NEAR + FAR


class _ReduceScatter:
    def __init__(self, parts):
        self.inputs = tuple(g for g, _ in parts)
        self.slots = tuple(s for _, s in parts)
        self.out_shapes = tuple(jax.ShapeDtypeStruct((len(s),) + g.shape[2:], g.dtype) for g, s in parts)
        n_max = max(len(s) for s in self.slots)
        n = len(parts)
        self.scratch = (pltpu.SemaphoreType.DMA((n, n_max)), pltpu.SemaphoreType.DMA((n, n_max)))

    def _copies(self, srcs, outs, sems):
        send_sems, recv_sems = sems
        x, y, c = _coords()
        copies = []
        for m, slots in enumerate(self.slots):
            for i, s in enumerate(slots):
                fx, fy, fc = RELATIONS[s]
                px = 1 - x if fx else x
                py = 1 - y if fy else y
                pc = 1 - c if fc else c
                copies.append(pltpu.make_async_remote_copy(
                    src_ref=srcs[m].at[2 * px + py, pc], dst_ref=outs[m].at[i], send_sem=send_sems.at[m, i],
                    recv_sem=recv_sems.at[m, i], device_id=(px, py, pc), device_id_type=MESH))
        return copies

    def start(self, srcs, outs, sems):
        for cp in self._copies(srcs, outs, sems):
            cp.start()

    def mid(self, srcs, outs, sems):
        pass

    def finish(self, srcs, outs, sems):
        for cp in self._copies(srcs, outs, sems):
            cp.wait()


SEM = pl.BlockSpec(memory_space=pltpu.SEMAPHORE)
SPLIT_PARAMS = dict(has_side_effects=pltpu.SideEffectType.DATAFLOW_SIDE_EFFECTING)


def _owner_copies(grad_ref, land_ref, send_sems, recv_sems):
    x, y, c = _coords()
    copies = []
    for s, (fx, fy, fc) in enumerate(RELATIONS):
        px = 1 - x if fx else x
        py = 1 - y if fy else y
        pc = 1 - c if fc else c
        copies.append(pltpu.make_async_remote_copy(
            src_ref=grad_ref.at[2 * px + py, pc], dst_ref=land_ref.at[s], send_sem=send_sems.at[s],
            recv_sem=recv_sems.at[s], device_id=(px, py, pc), device_id_type=MESH))
    return copies


def _send_to_owners(name, grad):
    n = len(RELATIONS)
    land_shape = (n,) + grad.shape[2:]

    def body(g_ref, land_ref, send_sems, recv_sems, g_thru, land_thru, token):
        for cp in _owner_copies(g_ref, land_ref, send_sems, recv_sems):
            cp.start()
        token[...] = jnp.zeros_like(token)

    return pl.pallas_call(
        body, name=name,
        out_shape=(pltpu.SemaphoreType.DMA((n,)), pltpu.SemaphoreType.DMA((n,)), pltpu.HBM(grad.shape, grad.dtype),
                   pltpu.HBM(land_shape, grad.dtype), jax.ShapeDtypeStruct((8, 128), F32)),
        in_specs=(HBM, HBM), out_specs=(SEM, SEM, HBM, HBM, pl.BlockSpec(memory_space=pltpu.VMEM)),
        input_output_aliases={0: 2, 1: 3}, compiler_params=pltpu.CompilerParams(**SPLIT_PARAMS),
    )(pltpu.with_memory_space_constraint(grad, pltpu.HBM),
      pltpu.with_memory_space_constraint(lax.empty(land_shape, grad.dtype), pltpu.HBM))


def _await_owners(name, started, after):
    send_sems, recv_sems, g_thru, land_thru, _ = started

    def body(g_ref, land_ref, send_sems, recv_sems, after_ref, g_out, land_out):
        for cp in _owner_copies(g_ref, land_ref, send_sems, recv_sems):
            cp.wait_send()
            cp.wait_recv()

    return pl.pallas_call(
        body, name=name, out_shape=(pltpu.HBM(g_thru.shape, g_thru.dtype), pltpu.HBM(land_thru.shape, land_thru.dtype)),
        in_specs=(HBM, HBM, SEM, SEM, pl.BlockSpec(memory_space=pl.ANY)), out_specs=(HBM, HBM),
        input_output_aliases={0: 0, 1: 1}, compiler_params=pltpu.CompilerParams(**SPLIT_PARAMS),
    )(g_thru, land_thru, send_sems, recv_sems, after)


class _SiblingExchange:
    def __init__(self, grads):
        self.inputs = tuple(grads)
        self.out_shapes = tuple(jax.ShapeDtypeStruct((4,) + g.shape[2:], g.dtype) for g in grads)
        self.scratch = (pltpu.SemaphoreType.DMA((len(grads),)), pltpu.SemaphoreType.DMA((len(grads),)))

    def _copies(self, srcs, outs, sems):
        send_sems, recv_sems = sems
        x, y, c = _coords()
        return [pltpu.make_async_remote_copy(
            src_ref=srcs[m].at[:, 1 - c], dst_ref=outs[m], send_sem=send_sems.at[m], recv_sem=recv_sems.at[m],
            device_id=(x, y, 1 - c), device_id_type=MESH) for m in range(len(srcs))]

    def start(self, srcs, outs, sems):
        for cp in self._copies(srcs, outs, sems):
            cp.start()

    def mid(self, srcs, outs, sems):
        pass

    def finish(self, srcs, outs, sems):
        for cp in self._copies(srcs, outs, sems):
            cp.wait()


class _ChipExchange:
    def __init__(self, partials):
        self.inputs = tuple(partials)
        self.out_shapes = tuple(jax.ShapeDtypeStruct((3,) + p.shape[1:], p.dtype) for p in partials)
        n = len(partials)
        self.scratch = (pltpu.SemaphoreType.DMA((n, 3)), pltpu.SemaphoreType.DMA((n, 3)))

    def _copies(self, srcs, outs, sems):
        send_sems, recv_sems = sems
        x, y, c = _coords()
        copies = []
        for m in range(len(srcs)):
            for j, (px, py) in enumerate([(1 - x, y), (x, 1 - y), (1 - x, 1 - y)]):
                copies.append(pltpu.make_async_remote_copy(
                    src_ref=srcs[m].at[2 * px + py], dst_ref=outs[m].at[j], send_sem=send_sems.at[m, j],
                    recv_sem=recv_sems.at[m, j], device_id=(px, py, c), device_id_type=MESH))
        return copies

    def start(self, srcs, outs, sems):
        for cp in self._copies(srcs, outs, sems):
            cp.start()

    def mid(self, srcs, outs, sems):
        pass

    def finish(self, srcs, outs, sems):
        for cp in self._copies(srcs, outs, sems):
            cp.wait()


class _Multi:
    def __init__(self, comms):
        self.comms = comms
        self.inputs = tuple(a for c in comms for a in c.inputs)
        self.out_shapes = tuple(s for c in comms for s in c.out_shapes)
        self.scratch = tuple(s for c in comms for s in c.scratch)

    def _each(self, phase, srcs, outs, sems):
        i = o = k = 0
        for c in self.comms:
            ni, no, nk = len(c.inputs), len(c.out_shapes), len(c.scratch)
            getattr(c, phase)(srcs[i:i + ni], outs[o:o + no], sems[k:k + nk])
            i, o, k = i + ni, o + no, k + nk

    def start(self, srcs, outs, sems):
        self._each("start", srcs, outs, sems)

    def mid(self, srcs, outs, sems):
        self._each("mid", srcs, outs, sems)

    def finish(self, srcs, outs, sems):
        self._each("finish", srcs, outs, sems)


def _call(name, main, grid, in_specs, out_specs, out_shape, args, scratch=(), comm=None, prefetch=None):
    comm = comm or _NoComm()
    counts = [len(in_specs), len(comm.inputs), len(out_shape), len(comm.out_shapes), len(scratch), len(comm.scratch)]
    n_steps = math.prod(grid)
    hosted = bool(comm.inputs)

    def body(*refs):
        if prefetch is not None:
            refs = refs[1:]
        parts, at = [], 0
        for n in counts:
            parts.append(refs[at:at + n])
            at += n
        ins, c_in, outs, c_out, scr, c_scr = parts
        step = pl.program_id(0)
        for d in range(1, len(grid)):
            step = step * grid[d] + pl.program_id(d)
        if hosted:
            @pl.when(step == 0)
            def _():
                comm.start(c_in, c_out, c_scr)
        main(ins, outs, scr)
        if hosted:
            @pl.when(step == max(n_steps - 2, 0))
            def _():
                comm.mid(c_in, c_out, c_scr)

            @pl.when(step == n_steps - 1)
            def _():
                comm.finish(c_in, c_out, c_scr)

    all_in = list(in_specs) + [HBM] * counts[1]
    all_out = list(out_specs) + [HBM] * counts[3]
    all_scratch = list(scratch) + list(comm.scratch)
    shapes = list(out_shape) + list(comm.out_shapes)
    if prefetch is None:
        res = pl.pallas_call(body, name=name, grid=grid, in_specs=all_in, out_specs=all_out, out_shape=shapes,
                             scratch_shapes=all_scratch, compiler_params=_cparams())(*args, *comm.inputs)
    else:
        res = pl.pallas_call(
            body, name=name, out_shape=shapes,
            grid_spec=pltpu.PrefetchScalarGridSpec(num_scalar_prefetch=1, grid=grid, in_specs=all_in,
                                                   out_specs=all_out, scratch_shapes=all_scratch),
            compiler_params=_cparams())(prefetch, *args, *comm.inputs)
    return res[:counts[2]], res[counts[2]:]


def _rms_fwd(name, x, g, comm=None):
    T, D = x.shape
    tm = min(T, 512)

    def main(ins, outs, scr):
        x_ref, g_ref = ins
        xv = x_ref[...]
        r = lax.rsqrt(jnp.mean(xv * xv, axis=-1, keepdims=True) + RMS_EPS)
        outs[0][...] = (xv * r * g_ref[...]).astype(outs[0].dtype)

    tile = pl.BlockSpec((tm, D), lambda i: (i, 0))
    (h,), extra = _call(name, main, (T // tm,), [tile, pl.BlockSpec((1, D), lambda i: (0, 0))], [tile],
                        [jax.ShapeDtypeStruct((T, D), BF16)], (x, g), comm=comm)
    return h, extra


def _final_loss_epilogue(scale, out_scale):
    def ep(acc, ex, outs):
        res_ref, g_ref, t_ref = ex
        dx_ref, dxb_ref, dg_ref, loss_ref = outs
        n = acc.shape[-1]
        xv = res_ref[...] + scale * acc
        r = lax.rsqrt(jnp.mean(xv * xv, axis=-1, keepdims=True) + RMS_EPS)
        xhat = xv * r
        err = xhat * g_ref[...] - t_ref[...]

        @pl.when(pl.program_id(0) == 0)
        def _():
            dg_ref[...] = jnp.zeros_like(dg_ref)
            loss_ref[...] = jnp.zeros_like(loss_ref)

        loss_ref[...] += jnp.broadcast_to(jnp.sum(err * err) * (0.5 / n), loss_ref.shape)
        dy = err * (1.0 / n)
        dg_ref[...] += jnp.sum(dy * xhat, axis=0, keepdims=True)
        dxhat = dy * g_ref[...]
        dx = r * (dxhat - xhat * jnp.mean(dxhat * xhat, axis=-1, keepdims=True))
        dx_ref[...] = dx
        dxb_ref[...] = (out_scale * dx).astype(dxb_ref.dtype)
    return ep


def _mm_nstream(name, a, ws, w_sel, w_form, comps, out_dtypes, epilogue, cn, rows=1024, comm=None, after=None):
    T, K = a.shape
    N = ws[0].shape[1]
    rows = min(rows, T)
    assert N % cn == 0 and T % rows == 0
    n_w, n_c = len(ws), len(comps)

    def main(ins, outs, scr):
        a_ref = ins[0]
        w_refs = ins[1:1 + n_w]
        c_refs = ins[1 + n_w:1 + n_w + n_c]

        for r in range(T // rows):
            sl = slice(r * rows, (r + 1) * rows)
            a_blk = a_ref[sl, :]
            dots = [_dot(a_blk, w_ref[...], w_form) for w_ref in w_refs]
            res = epilogue(dots, [c_ref[sl, :] for c_ref in c_refs])
            for o_ref, o in zip(outs, res):
                o_ref[sl, :] = o.astype(o_ref.dtype)

    if w_form == "nt":
        w_specs = [pl.BlockSpec((None, cn, K), functools.partial(lambda j, s: (s, j, 0), s=s)) for s in w_sel]
    else:
        w_specs = [pl.BlockSpec((K, cn), lambda j: (0, j)) for _ in ws]
    chunk = pl.BlockSpec((T, cn), lambda j: (0, j))
    order = [] if after is None else [after]
    order_specs = [pl.BlockSpec(after.shape, lambda j: (0, 0))] if order else []
    return _call(name, main, (N // cn,),
                 [pl.BlockSpec((T, K), lambda j: (0, 0))] + w_specs + [chunk] * n_c + order_specs,
                 [chunk] * len(out_dtypes), [jax.ShapeDtypeStruct((T, N), dt) for dt in out_dtypes],
                 (a, *ws, *comps, *order), comm=comm)


def _mm_mstream(name, as_, ws, w_sel, w_form, extras, outs_desc, epilogue, tm=512, comm=None):
    T = as_[0].shape[0]
    tm = min(tm, T)
    n_a = len(as_)
    w_shapes = [w.shape[-2:] for w in ws]
    N = w_shapes[0][1] if w_form == "nn" else w_shapes[0][0]

    def main(ins, outs, scr):
        a_refs = ins[:n_a]
        w_refs = ins[n_a:2 * n_a]
        acc = None
        for a_ref, w_ref in zip(a_refs, w_refs):
            d = _dot(a_ref[...], w_ref[...], w_form)
            acc = d if acc is None else acc + d
        epilogue(acc, ins[2 * n_a:], outs)

    kind_spec = {"tile": pl.BlockSpec((tm, N), lambda i: (i, 0)), "vec": pl.BlockSpec((1, N), lambda i: (0, 0))}
    kind_shape = {"tile": (T, N), "vec": (1, N)}
    a_specs = [pl.BlockSpec((tm, a.shape[1]), lambda i: (i, 0)) for a in as_]
    w_specs = []
    for w, s in zip(ws, w_sel):
        if w.ndim == 3:
            w_specs.append(pl.BlockSpec((None,) + tuple(w.shape[1:]), functools.partial(lambda i, s: (s, 0, 0), s=s),
                                        pipeline_mode=pl.Buffered(1)))
        else:
            w_specs.append(pl.BlockSpec(tuple(w.shape), lambda i: (0, 0), pipeline_mode=pl.Buffered(1)))
    args = list(as_) + list(ws) + [e for e, _ in extras]
    return _call(name, main, (T // tm,), a_specs + w_specs + [kind_spec[k] for _, k in extras],
                 [kind_spec[k] for _, k in outs_desc],
                 [jax.ShapeDtypeStruct(kind_shape[k], dt) for dt, k in outs_desc], args, comm=comm)


def _plain_epilogue(acc, ex, outs):
    outs[0][...] = acc.astype(outs[0].dtype)


def _residual_rms_epilogue(scale):
    def ep(acc, ex, outs):
        xv = ex[0][...] + scale * acc
        outs[0][...] = xv
        r = lax.rsqrt(jnp.mean(xv * xv, axis=-1, keepdims=True) + RMS_EPS)
        outs[1][...] = (xv * r * ex[1][...]).astype(outs[1].dtype)
    return ep


def _rms_bwd_epilogue(out_scale):
    def ep(acc, ex, outs):
        x_ref, g_ref, dres_ref = ex
        dx_ref, dxb_ref, dg_ref = outs
        xv = x_ref[...]
        r = lax.rsqrt(jnp.mean(xv * xv, axis=-1, keepdims=True) + RMS_EPS)
        xhat = xv * r

        @pl.when(pl.program_id(0) == 0)
        def _():
            dg_ref[...] = jnp.zeros_like(dg_ref)

        dg_ref[...] += jnp.sum(acc * xhat, axis=0, keepdims=True)
        dxhat = acc * g_ref[...]
        dx = r * (dxhat - xhat * jnp.mean(dxhat * xhat, axis=-1, keepdims=True)) + dres_ref[...]
        dx_ref[...] = dx
        dxb_ref[...] = (out_scale * dx).astype(dxb_ref.dtype)
    return ep


def _mm_tn(name, a, b, tmo, tno, out_dtype, tk=1024, comm=None):
    T, Ma = a.shape
    Nb = b.shape[1]
    tk = min(tk, T)
    nk = T // tk

    def main(ins, outs, scr):
        a_ref, b_ref = ins
        (acc_ref,) = scr
        k = pl.program_id(2)

        @pl.when(k == 0)
        def _():
            acc_ref[...] = jnp.zeros_like(acc_ref)

        acc_ref[...] += _dot(a_ref[...], b_ref[...], "tn")

        @pl.when(k == nk - 1)
        def _():
            outs[0][...] = acc_ref[...].astype(outs[0].dtype)

    (out,), extra = _call(
        name, main, (Ma // tmo, Nb // tno, nk),
        [pl.BlockSpec((tk, tmo), lambda i, j, k: (k, i)), pl.BlockSpec((tk, tno), lambda i, j, k: (k, j))],
        [pl.BlockSpec((tmo, tno), lambda i, j, k: (i, j))], [jax.ShapeDtypeStruct((Ma, Nb), out_dtype)],
        (a, b), scratch=[pltpu.VMEM((tmo, tno), F32)], comm=comm)
    return out, extra


def _swiglu_parts(g, u):
    s = _sigmoid(g)
    silu = g * s
    return [u * (s + silu * (1.0 - s)), silu, silu * u]


def _silu_mul_epilogue(dots, comps):
    g, u = dots
    return _swiglu_parts(g, u)


def _gate_parts_epilogue(dots, comps):
    (g,) = dots
    s = _sigmoid(g)
    silu = g * s
    return [s + silu * (1.0 - s), silu]


def _up_act_epilogue(dots, comps):
    (u,) = dots
    return [u * comps[0].astype(F32), u * comps[1].astype(F32)]


def _dact_epilogue(dots, comps):
    dact = dots[0].astype(BF16)
    return [dact * comps[0], dact * comps[1]]


def _identity_epilogue(dots, comps):
    return list(dots)


def _swap_halves(x):
    lane = lax.broadcasted_iota(jnp.int32, x.shape, 1)
    first = (lane % DK) < (DK // 2)
    return jnp.where(first, pltpu.roll(x, 128 - DK // 2, 1), pltpu.roll(x, DK // 2, 1))


def _rotary(t, cos, sin_signed):
    halves = []
    for p in range(QK_W // 128):
        th = t[:, 128 * p:128 * (p + 1)]
        halves.append(th * cos + _swap_halves(th) * sin_signed)
    return jnp.concatenate(halves, axis=1)


def _rotary_transposed(d, cos, sin_signed):
    halves = []
    for p in range(QK_W // 128):
        dh = d[:, 128 * p:128 * (p + 1)]
        halves.append(dh * cos + _swap_halves(dh * sin_signed))
    return jnp.concatenate(halves, axis=1)


def _log_sigmoid(x):
    return jnp.minimum(x, 0.0) - jnp.log(1.0 + jnp.exp(-jnp.abs(x)))


def _attn_masks():
    row = lax.broadcasted_iota(jnp.int32, (SUPER, SUPER), 0)
    col = lax.broadcasted_iota(jnp.int32, (SUPER, SUPER), 1)
    same = (row // CHUNK) == (col // CHUNK)
    return row, col, same


def _group_inputs(grp, pr, cos, sin_signed, lg, wa2, ba):
    seg = lambda lo, width: pr[:, lo:lo + width].astype(F32)
    if grp == 0:
        q = _rotary(seg(C_RQ, QK_W), cos, sin_signed)
        k = _rotary(seg(C_RK, QK_W), cos, sin_signed) * (DK ** -0.5)
        v = pr[:, C_RV:C_RV + V_W]
        gate = seg(C_RG, V_W)
        pos = lax.broadcasted_iota(jnp.int32, (SUPER, QK_W), 0).astype(F32) + 1.0
        return q, k, v, gate, pos * lg, None, None
    q = seg(C_GQ, QK_W) * (DK ** -0.5)
    k = seg(C_GK, QK_W)
    v = pr[:, C_GV:C_GV + V_W]
    gate = seg(C_GG, V_W)
    glow = pr[:, C_GL:C_GL + GL_W]
    logit = _dot(glow.astype(BF16), wa2.astype(BF16), "nn") + ba
    la = _log_sigmoid(logit) * (1.0 / GATE_NORM)
    row, col, _ = _attn_masks()
    lower = (col <= row).astype(F32)
    b_cum = _dot(lower, la, "nn", precision=lax.Precision.HIGHEST)
    return q, k, v, gate, b_cum, glow, logit


def _decay_factors(q, k, b_cum):
    c = b_cum[SUPER // 2 - 1:SUPER // 2, :]
    bl = b_cum[SUPER - 1:SUPER, :]
    e1 = jnp.exp(b_cum - c)
    e2 = jnp.exp(c - b_cum)
    e_b = jnp.exp(b_cum)
    e_l = jnp.exp(bl - b_cum)
    return dict(e1=e1, e2=e2, eb=e_b, el=e_l, ebl=jnp.exp(bl),
                qp=q * e1, qm=q * e2, kp=k * e1, km=k * e2, qs=q * e_b, kl=k * e_l)


def _state_block_mask():
    r = lax.broadcasted_iota(jnp.int32, (V_W, QK_W), 0)
    c = lax.broadcasted_iota(jnp.int32, (V_W, QK_W), 1)
    return (r // DV) == (c // DK)


def _attn_fwd(proj, cos, sin_signed, lg, wa2p, ba, gn_ret, gn_gla, x_res, w_out, g_next, comm=None):
    T = proj.shape[0]
    n_s = T // SUPER
    D = x_res.shape[1]

    def main(ins, outs, scr):
        pr_ref, cos_ref, sin_ref, lg_ref, wa2_ref, ba_ref, gr_ref, gg_ref, xres_ref, wout_ref, gnext_ref = ins
        o_ref, y_ref, st_ref, x_ref, h_ref = outs
        (s_ref,) = scr
        i = pl.program_id(0)

        @pl.when(i == 0)
        def _():
            s_ref[...] = jnp.zeros_like(s_ref)

        pr = pr_ref
        row, col, same = _attn_masks()
        m1 = col <= row
        m2 = jnp.logical_and(col > row, same)
        lane = lax.broadcasted_iota(jnp.int32, (1, QK_W), 1)
        blockmask = _state_block_mask()
        for grp in range(2):
            q, k, v, gate, b_cum, _, _ = _group_inputs(grp, pr, cos_ref[...], sin_ref[...], lg_ref[...],
                                                      wa2_ref[...], ba_ref[...])
            f = _decay_factors(q, k, b_cum)
            gn = gr_ref[...] if grp == 0 else gg_ref[...]
            s_prev = s_ref[grp]
            st_ref[0, grp] = s_prev
            o_inter = _dot(f["qs"].astype(BF16), s_prev.astype(BF16), "nt")
            kmb = f["km"].astype(BF16)
            kpb = f["kp"].astype(BF16)
            vb = v.astype(BF16)
            for h in range(HEADS):
                hm = (lane // DK) == h
                a1 = _dot(jnp.where(hm, f["qp"], 0.0).astype(BF16), kmb, "nt")
                a2 = _dot(jnp.where(hm, f["qm"], 0.0).astype(BF16), kpb, "nt")
                a = jnp.where(m1, a1, jnp.where(m2, a2, 0.0))
                lo = grp * V_W + h * DV
                o_h = _dot(a.astype(BF16), vb[:, h * DV:(h + 1) * DV], "nn") + o_inter[:, h * DV:(h + 1) * DV]
                o_ref[:, lo:lo + DV] = o_h
                r = lax.rsqrt(jnp.mean(o_h * o_h, axis=-1, keepdims=True) + RMS_EPS)
                gte = gate[:, h * DV:(h + 1) * DV]
                y = o_h * r * gn[:, h * DV:(h + 1) * DV] * (gte * _sigmoid(gte))
                y_ref[:, lo:lo + DV] = y.astype(y_ref.dtype)
            upd = _dot(vb, f["kl"].astype(BF16), "tn")
            s_ref[grp] = s_prev * f["ebl"] + jnp.where(blockmask, upd, 0.0)
        xv = xres_ref[...] + _dot(y_ref[...], wout_ref[...], "nn")
        x_ref[...] = xv
        r = lax.rsqrt(jnp.mean(xv * xv, axis=-1, keepdims=True) + RMS_EPS)
        h_ref[...] = (xv * r * gnext_ref[...]).astype(h_ref.dtype)

    const = lambda shape: pl.BlockSpec(shape, lambda i: tuple(0 for _ in shape))
    rows = lambda w: pl.BlockSpec((SUPER, w), lambda i: (i, 0))
    return _call(
        "attn_fwd", main, (n_s,),
        [rows(PROJ_W), rows(128), rows(128),
         const((1, QK_W)), const((GL_W, QK_W)), const((1, QK_W)), const((1, V_W)), const((1, V_W)),
         rows(D), const((2 * V_W, D)), const((1, D))],
        [rows(2 * V_W), rows(2 * V_W), pl.BlockSpec((1, 2, V_W, QK_W), lambda i: (i, 0, 0, 0)), rows(D), rows(D)],
        [jax.ShapeDtypeStruct((T, 2 * V_W), F32), jax.ShapeDtypeStruct((T, 2 * V_W), BF16),
         jax.ShapeDtypeStruct((n_s, 2, V_W, QK_W), F32), jax.ShapeDtypeStruct((T, D), F32),
         jax.ShapeDtypeStruct((T, D), BF16)],
        (proj, cos, sin_signed, lg, wa2p, ba, gn_ret, gn_gla, x_res, w_out, g_next),
        scratch=[pltpu.VMEM((2, V_W, QK_W), F32)], comm=comm)


def _attn_bwd(proj, cos, sin_signed, lg, wa2p, ba, gn_ret, gn_gla, o, dx, w_out, states, comm=None):
    T = proj.shape[0]
    n_s = T // SUPER
    D = dx.shape[1]

    def main(ins, outs, scr):
        pr_ref, cos_ref, sin_ref, lg_ref, wa2_ref, ba_ref, gr_ref, gg_ref, o_ref, dx_ref, wout_ref, st_ref = ins
        dp_ref, dgr_ref, dgg_ref, dba_ref, dwa_ref = outs
        (ds_ref, dy_ref) = scr
        i = pl.program_id(0)
        dy_ref[...] = _dot(dx_ref[...], wout_ref[...], "nt")

        @pl.when(i == 0)
        def _():
            ds_ref[...] = jnp.zeros_like(ds_ref)
            dgr_ref[...] = jnp.zeros_like(dgr_ref)
            dgg_ref[...] = jnp.zeros_like(dgg_ref)
            dba_ref[...] = jnp.zeros_like(dba_ref)
            dwa_ref[...] = jnp.zeros_like(dwa_ref)

        pr = pr_ref
        cos = cos_ref[...]
        sin_signed = sin_ref[...]
        row, col, same = _attn_masks()
        m1 = col <= row
        m2 = jnp.logical_and(col > row, same)
        m1t = row <= col
        m2t = jnp.logical_and(row > col, same)
        lane = lax.broadcasted_iota(jnp.int32, (1, QK_W), 1)
        blockmask = _state_block_mask()
        for grp in range(2):
            q, k, v, gate, b_cum, glow, logit = _group_inputs(grp, pr, cos, sin_signed, lg_ref[...],
                                                              wa2_ref[...], ba_ref[...])
            f = _decay_factors(q, k, b_cum)
            gn = gr_ref[...] if grp == 0 else gg_ref[...]
            dgn_ref = dgr_ref if grp == 0 else dgg_ref
            do_parts, dgate_parts, dgn_parts = [], [], []
            for h in range(HEADS):
                lo = grp * V_W + h * DV
                o_h = o_ref[:, lo:lo + DV]
                r = lax.rsqrt(jnp.mean(o_h * o_h, axis=-1, keepdims=True) + RMS_EPS)
                n = o_h * r
                gte = gate[:, h * DV:(h + 1) * DV]
                sg = _sigmoid(gte)
                dy_h = dy_ref[:, lo:lo + DV]
                gn_h = gn[:, h * DV:(h + 1) * DV]
                dgate_parts.append(dy_h * n * gn_h * (sg * (1.0 + gte * (1.0 - sg))))
                dz = dy_h * (gte * sg)
                dgn_parts.append(jnp.sum(dz * n, axis=0, keepdims=True))
                dn = dz * gn_h
                do_parts.append(r * (dn - n * jnp.mean(dn * n, axis=-1, keepdims=True)))
            dgn_ref[...] += jnp.concatenate(dgn_parts, axis=1)
            dgate = jnp.concatenate(dgate_parts, axis=1)
            do = jnp.concatenate(do_parts, axis=1)
            dob = do.astype(BF16)
            vb = v.astype(BF16)
            s_prev = st_ref[0, grp]
            ds_new = ds_ref[grp]
            dsb = ds_new.astype(BF16)
            qpb, qmb = f["qp"].astype(BF16), f["qm"].astype(BF16)
            kpb, kmb = f["kp"].astype(BF16), f["km"].astype(BF16)
            dqp = jnp.zeros((SUPER, QK_W), F32)
            dqm = jnp.zeros((SUPER, QK_W), F32)
            dkp = jnp.zeros((SUPER, QK_W), F32)
            dkm = jnp.zeros((SUPER, QK_W), F32)
            dv_parts = []
            for h in range(HEADS):
                hm = (lane // DK) == h
                qp_h = jnp.where(hm, f["qp"], 0.0).astype(BF16)
                qm_h = jnp.where(hm, f["qm"], 0.0).astype(BF16)
                kp_h = jnp.where(hm, f["kp"], 0.0).astype(BF16)
                km_h = jnp.where(hm, f["km"], 0.0).astype(BF16)
                at = jnp.where(m1t, _dot(km_h, qpb, "nt"), jnp.where(m2t, _dot(kp_h, qmb, "nt"), 0.0))
                do_h = dob[:, h * DV:(h + 1) * DV]
                v_h = vb[:, h * DV:(h + 1) * DV]
                dv_parts.append(_dot(at.astype(BF16), do_h, "nn"))
                da = _dot(do_h, v_h, "nt")
                dat = _dot(v_h, do_h, "nt")
                da1 = jnp.where(m1, da, 0.0).astype(BF16)
                da2 = jnp.where(m2, da, 0.0).astype(BF16)
                da1t = jnp.where(m1t, dat, 0.0).astype(BF16)
                da2t = jnp.where(m2t, dat, 0.0).astype(BF16)
                dqp = dqp + _dot(da1, km_h, "nn")
                dqm = dqm + _dot(da2, kp_h, "nn")
                dkm = dkm + _dot(da1t, qp_h, "nn")
                dkp = dkp + _dot(da2t, qm_h, "nn")
            klb = f["kl"].astype(BF16)
            qsb = f["qs"].astype(BF16)
            dqs = _dot(dob, s_prev.astype(BF16), "nn")
            dkl = _dot(vb, dsb, "nn")
            dv = jnp.concatenate(dv_parts, axis=1) + _dot(klb, dsb, "nt")
            ds_ref[grp] = ds_new * f["ebl"] + jnp.where(blockmask, _dot(dob, qsb, "tn"), 0.0)
            dq = dqp * f["e1"] + dqm * f["e2"] + dqs * f["eb"]
            dk = dkm * f["e2"] + dkp * f["e1"] + dkl * f["el"]
            if grp == 0:
                dq = _rotary_transposed(dq, cos, sin_signed)
                dk = _rotary_transposed(dk * (DK ** -0.5), cos, sin_signed)
                dp_ref[:, C_RQ:C_RQ + QK_W] = dq.astype(dp_ref.dtype)
                dp_ref[:, C_RK:C_RK + QK_W] = dk.astype(dp_ref.dtype)
                dp_ref[:, C_RV:C_RV + V_W] = dv.astype(dp_ref.dtype)
                dp_ref[:, C_RG:C_RG + V_W] = dgate.astype(dp_ref.dtype)
            else:
                dkl_kl = dkl * klb.astype(F32)
                db = (dqp * qpb.astype(F32) - dkm * kmb.astype(F32) - dqm * qmb.astype(F32)
                      + dkp * kpb.astype(F32) + dqs * qsb.astype(F32) - dkl_kl)
                last = (jnp.sum(dkl_kl, axis=0, keepdims=True)
                        + f["ebl"] * jnp.sum(s_prev * ds_new, axis=0, keepdims=True))
                rowq = lax.broadcasted_iota(jnp.int32, (SUPER, QK_W), 0)
                db = db + jnp.where(rowq == SUPER - 1, last, 0.0)
                upper = (col >= row).astype(F32)
                dla = _dot(upper, db, "nn", precision=lax.Precision.HIGHEST)
                dlogit = dla * (1.0 / GATE_NORM) * (1.0 - _sigmoid(logit))
                dlb = dlogit.astype(BF16)
                dglow = _dot(dlb, wa2_ref[...].astype(BF16), "nt")
                dwa_ref[...] += _dot(glow.astype(BF16), dlb, "tn")
                dba_ref[...] += jnp.sum(dlogit, axis=0, keepdims=True)
                dp_ref[:, C_GQ:C_GQ + QK_W] = (dq * (DK ** -0.5)).astype(dp_ref.dtype)
                dp_ref[:, C_GK:C_GK + QK_W] = dk.astype(dp_ref.dtype)
                dp_ref[:, C_GV:C_GV + V_W] = dv.astype(dp_ref.dtype)
                dp_ref[:, C_GG:C_GG + V_W] = dgate.astype(dp_ref.dtype)
                dp_ref[:, C_GL:C_GL + GL_W] = dglow.astype(dp_ref.dtype)

    rev = lambda i: n_s - 1 - i
    const = lambda shape: pl.BlockSpec(shape, lambda i: tuple(0 for _ in shape))
    return _call(
        "attn_bwd", main, (n_s,),
        [pl.BlockSpec((SUPER, PROJ_W), lambda i: (rev(i), 0)),
         pl.BlockSpec((SUPER, 128), lambda i: (rev(i), 0)), pl.BlockSpec((SUPER, 128), lambda i: (rev(i), 0)),
         const((1, QK_W)), const((GL_W, QK_W)), const((1, QK_W)), const((1, V_W)), const((1, V_W)),
         pl.BlockSpec((SUPER, 2 * V_W), lambda i: (rev(i), 0)),
         pl.BlockSpec((SUPER, D), lambda i: (rev(i), 0)), const((2 * V_W, D)),
         pl.BlockSpec((1, 2, V_W, QK_W), lambda i: (rev(i), 0, 0, 0))],
        [pl.BlockSpec((SUPER, PROJ_W), lambda i: (rev(i), 0)),
         const((1, V_W)), const((1, V_W)), const((1, QK_W)), const((GL_W, QK_W))],
        [jax.ShapeDtypeStruct((T, PROJ_W), BF16),
         jax.ShapeDtypeStruct((1, V_W), F32), jax.ShapeDtypeStruct((1, V_W), F32),
         jax.ShapeDtypeStruct((1, QK_W), F32), jax.ShapeDtypeStruct((GL_W, QK_W), F32)],
        (proj, cos, sin_signed, lg, wa2p, ba, gn_ret, gn_gla, o, dx, w_out, states),
        scratch=[pltpu.VMEM((2, V_W, QK_W), F32), pltpu.VMEM((SUPER, 2 * V_W), F32)], comm=comm)


def _rotary_tables(T):
    half = DK // 2
    inv = ROPE_BASE ** (-jnp.arange(half, dtype=F32) * 2.0 / DK)
    ang = jnp.arange(T, dtype=F32)[:, None] * inv[None, :]
    cos, sin = jnp.cos(ang), jnp.sin(ang)
    cos_head = jnp.concatenate([cos, cos], axis=1)
    sin_head = jnp.concatenate([-sin, sin], axis=1)
    return jnp.tile(cos_head, (1, 128 // DK)), jnp.tile(sin_head, (1, 128 // DK))


def _sum_devices(name, gathered, m_per):
    def body(g_ref, o_ref):
        acc = g_ref[0:m_per, :]
        for k in range(1, N_DEV):
            acc = acc + g_ref[k * m_per:(k + 1) * m_per, :]
        o_ref[...] = acc

    return pl.pallas_call(body, name=name, out_shape=jax.ShapeDtypeStruct((m_per, 128), F32))(gathered)


def _pair_sum(name, grad, landed, core, comm=None):
    R, C = grad.shape[2:]

    def main(ins, outs, scr):
        outs[0][...] = (ins[0][...].astype(F32) + ins[1][...].astype(F32)).astype(outs[0].dtype)

    blk = pl.BlockSpec((None, R, C), lambda j, s: (j, 0, 0))
    (out,), extra = _call(name, main, (4,), [pl.BlockSpec((None, None, R, C), lambda j, s: (j, s[0], 0, 0)), blk],
                          [blk], [jax.ShapeDtypeStruct((4, R, C), BF16)], (grad, landed), prefetch=core, comm=comm)
    return out, extra


def _owner_sums(name, items, owner, comm=None):
    counts = [1 + len(landed) for _, landed in items]

    def main(ins, outs, scr):
        at = 0
        for o_ref, n in zip(outs, counts):
            acc = ins[at][...].astype(F32)
            for l_ref in ins[at + 1:at + n]:
                for j in range(l_ref.shape[0]):
                    acc = acc + l_ref[j].astype(F32)
            o_ref[...] = acc
            at += n

    once = pl.Buffered(1)
    in_specs, out_specs, out_shape, args = [], [], [], []
    for grad, landed in items:
        R, C = grad.shape[-2:]
        if grad.ndim == 4:
            in_specs.append(pl.BlockSpec((None, None, R, C), lambda i, s: (s[0], s[1], 0, 0), pipeline_mode=once))
        else:
            in_specs.append(pl.BlockSpec((None, R, C), lambda i, s: (s[0], 0, 0), pipeline_mode=once))
        in_specs += [pl.BlockSpec(tuple(l.shape), lambda i, s: (0, 0, 0), pipeline_mode=once) for l in landed]
        out_specs.append(pl.BlockSpec((R, C), lambda i, s: (0, 0)))
        out_shape.append(jax.ShapeDtypeStruct((R, C), F32))
        args += [grad, *landed]
    return _call(name, main, (1,), in_specs, out_specs, out_shape, args, comm=comm, prefetch=owner)


def _rms_bwd(name, x, g, dh, dres, out_scale):
    T, D = x.shape
    tm = min(T, 512)
    ep = _rms_bwd_epilogue(out_scale)

    def main(ins, outs, scr):
        x_ref, g_ref, dh_ref, dres_ref = ins
        ep(dh_ref[...], (x_ref, g_ref, dres_ref), outs)

    tile = pl.BlockSpec((tm, D), lambda i: (i, 0))
    vec = pl.BlockSpec((1, D), lambda i: (0, 0))
    outs, _ = _call(name, main, (T // tm,), [tile, vec, tile, tile], [tile, tile, vec],
                    [jax.ShapeDtypeStruct((T, D), F32), jax.ShapeDtypeStruct((T, D), BF16),
                     jax.ShapeDtypeStruct((1, D), F32)], (x, g, dh, dres))
    return outs


def _adamw_group(name, items, n_blocks, comm=None):
    n = len(items)

    def main(ins, outs, scr):
        for p in range(n):
            g_ref, w_ref, m_ref, v_ref = ins[4 * p:4 * p + 4]
            d_ref, nm_ref, nv_ref = outs[3 * p:3 * p + 3]
            gv = g_ref[...]
            nm = ADAM_B1 * m_ref[...] + (1.0 - ADAM_B1) * gv
            nv = ADAM_B2 * v_ref[...] + (1.0 - ADAM_B2) * (gv * gv)
            m_hat = nm / (1.0 - ADAM_B1 ** ADAM_STEP)
            v_hat = nv / (1.0 - ADAM_B2 ** ADAM_STEP)
            d_ref[...] = -ADAM_LR * (m_hat / (jnp.sqrt(v_hat) + ADAM_EPS) + ADAM_WD * w_ref[...])
            nm_ref[...] = nm
            nv_ref[...] = nv

    in_specs, out_specs, out_shape, args = [], [], [], []
    for item in items:
        R, C = item[1].shape
        assert R % n_blocks == 0
        spec = pl.BlockSpec((R // n_blocks, C), lambda i: (i, 0))
        in_specs += [spec] * 4
        out_specs += [spec] * 3
        out_shape += [jax.ShapeDtypeStruct((R, C), F32)] * 3
        args += list(item)
    outs, extra = _call(name, main, (n_blocks,), in_specs, out_specs, out_shape, args, comm=comm)
    return [tuple(outs[3 * p:3 * p + 3]) for p in range(n)], extra


SMALL_ORDER = ("ffn1", "mix", "ffn2", "final", "ret", "gla", "b_a")


def kernel(x, ffn1_norm_g, ffn1_w_gate, ffn1_w_up, ffn1_w_down, mix_norm_g, w_in, ret_norm_g, gla_w_a2, gla_b_a, gla_norm_g, w_out, ffn2_norm_g, ffn2_w_gate, ffn2_w_up, ffn2_w_down, final_norm_g, loss_target, m_ffn1_norm_g, m_ffn1_w_gate, m_ffn1_w_up, m_ffn1_w_down, m_mix_norm_g, m_w_in, m_ret_norm_g, m_gla_w_a2, m_gla_b_a, m_gla_norm_g, m_w_out, m_ffn2_norm_g, m_ffn2_w_gate, m_ffn2_w_up, m_ffn2_w_down, m_final_norm_g, v_ffn1_norm_g, v_ffn1_w_gate, v_ffn1_w_up, v_ffn1_w_down, v_mix_norm_g, v_w_in, v_ret_norm_g, v_gla_w_a2, v_gla_b_a, v_gla_norm_g, v_w_out, v_ffn2_norm_g, v_ffn2_w_gate, v_ffn2_w_up, v_ffn2_w_down, v_final_norm_g):
    xi, yi, ci = _coords()
    dev = 4 * xi + 2 * yi + ci
    owner = jnp.stack([2 * xi + yi, ci]).astype(jnp.int32)

    x0, target = x[0], loss_target[0]
    T, D = x0.shape
    fb = ffn1_w_gate.shape[2]
    ib = w_in.shape[2]
    ab = gla_w_a2.shape[2]
    F = N_DEV * fb
    cos, sin_signed = _rotary_tables(T)
    lg = jnp.repeat(jnp.log(1.0 - 2.0 ** (-5.0 - jnp.arange(HEADS, dtype=F32))), DK)[None, :]
    g_final = final_norm_g.reshape(1, D)

    g1_loc = ffn1_w_gate[0].T[None].astype(BF16)
    u1_loc = ffn1_w_up[0].T[None].astype(BF16)
    d1_loc = ffn1_w_down.astype(BF16)
    g2_loc = ffn2_w_gate[0].T[None].astype(BF16)
    u2_loc = ffn2_w_up[0].T[None].astype(BF16)
    d2_loc = ffn2_w_down.astype(BF16)
    in_loc = w_in[0].T.astype(BF16)
    out_loc = w_out[0].astype(BF16)

    h1, (g1,) = _rms_fwd("ffn1_rms", x0, ffn1_norm_g, comm=_AllGather([g1_loc], ["stack"]))
    g1 = g1.reshape(1, F, D)
    (dsl1, sl1), (u1,) = _mm_nstream("ffn1_gate", h1, [g1], [0], "nt", [], [BF16, BF16], _gate_parts_epilogue, cn=256,
                                     comm=_AllGather([u1_loc], ["stack"]))
    u1 = u1.reshape(1, F, D)
    (dsu1, act1), (d1,) = _mm_nstream("ffn1_up", h1, [u1], [0], "nt", [dsl1, sl1], [BF16, BF16],
                                      _up_act_epilogue, cn=256, comm=_AllGather([d1_loc], ["stack"]))
    d1 = d1.reshape(1, F, D)
    f32_tile, bf16_tile, f32_vec = (F32, "tile"), (BF16, "tile"), (F32, "vec")
    (x1, h2), (in_all, a_all) = _mm_mstream(
        "ffn1_down", [act1], [d1], [0], "nn", [(x0, "tile"), (mix_norm_g, "vec")], [f32_tile, bf16_tile],
        _residual_rms_epilogue(0.5), comm=_AllGather([in_loc, gla_w_a2[0]], ["plain", "plain"]))
    w_in_t = jnp.pad(in_all.reshape(1, N_DEV * ib, D), ((0, 0), (0, PROJ_W - N_DEV * ib), (0, 0)))
    wa2 = jnp.transpose(a_all, (1, 0, 2)).reshape(GATE_RANK, N_DEV * ab)
    wa2p = jnp.pad(wa2, ((0, GL_W - GATE_RANK), (0, 0)))

    (proj,), (g2, out_all) = _mm_nstream("mix_proj", h2, [w_in_t], [0], "nt", [], [BF16], _identity_epilogue, cn=640,
                                         comm=_AllGather([g2_loc, out_loc], ["stack", "plain"]))
    w_out_full = out_all.reshape(D, D)
    (o, ymix, states, x2, h3), (u2,) = _attn_fwd(proj, cos, sin_signed, lg, wa2p, gla_b_a, ret_norm_g, gla_norm_g,
                                                 x1, w_out_full, ffn2_norm_g, comm=_AllGather([u2_loc], ["stack"]))
    g2, u2 = g2.reshape(1, F, D), u2.reshape(1, F, D)

    (dsu2, sl2, act2), (d2,) = _mm_nstream(
        "ffn2_up", h3, [g2, u2], [0, 0], "nt", [], [BF16, BF16, BF16], _silu_mul_epilogue, cn=256,
        comm=_AllGather([d2_loc], ["stack"]))
    d2 = d2.reshape(1, F, D)
    (dx3, dy3b, d_final, loss), _ = _mm_mstream(
        "ffn2_down", [act2], [d2], [0], "nn", [(x2, "tile"), (g_final, "vec"), (target, "tile")],
        [f32_tile, bf16_tile, f32_vec, f32_vec], _final_loss_epilogue(0.5, 0.5))

    dwd2, _ = _mm_tn("ffn2b_dwd", act2, dy3b, F // 2, D, BF16)
    dwd2 = dwd2.reshape(4, 2, fb, D)
    wd2_sent = _send_to_owners("send_wd2", dwd2)
    (dgate2, dup2), _ = _mm_nstream("ffn2b_dact", dy3b, [d2], [0], "nt", [dsu2, sl2], [BF16, BF16],
                                    _dact_epilogue, cn=256, after=wd2_sent[4])
    dwg2, _ = _mm_tn("ffn2b_dwg", dgate2, h3, F // 2, D, BF16)
    dwg2 = dwg2.reshape(4, 2, fb, D)
    dwu2, (l_wg2_near,) = _mm_tn("ffn2b_dwu", dup2, h3, F // 2, D, BF16, comm=_ReduceScatter([(dwg2, NEAR)]))
    dwu2 = dwu2.reshape(4, 2, fb, D)
    rms_outs = [f32_tile, bf16_tile, f32_vec]
    (dx2, dx2b, d_g2), (l_wg2_far, l_wu2_near) = _mm_mstream(
        "ffn2b_dh", [dgate2, dup2], [g2, u2], [0, 0], "nn", [(x2, "tile"), (ffn2_norm_g, "vec"), (dx3, "tile")],
        rms_outs, _rms_bwd_epilogue(1.0), comm=_ReduceScatter([(dwg2, FAR), (dwu2, NEAR)]))

    dwout, _ = _mm_tn("mixb_dwout", ymix, dx2b, D, D, BF16)
    dwout = dwout.reshape(4, 2, D // N_DEV, D)
    (dproj, d_ret, d_gla, d_ba, d_wa2p), (l_wu2_far, l_wout) = _attn_bwd(
        proj, cos, sin_signed, lg, wa2p, gla_b_a, ret_norm_g, gla_norm_g, o, dx2b, w_out_full, states,
        comm=_ReduceScatter([(dwu2, FAR), (dwout, ALL)]))
    dwin_t, _ = _mm_tn("mixb_dwin", dproj, h2, 640, D, BF16, tk=2048)
    dwin = dwin_t[:N_DEV * ib].reshape(4, 2, ib, D)
    (dx1, dy1b, d_gmix), (l_win_near,) = _mm_mstream(
        "mixb_dh", [dproj], [w_in_t], [0], "nn", [(x1, "tile"), (mix_norm_g, "vec"), (dx2, "tile")],
        rms_outs, _rms_bwd_epilogue(0.5), comm=_ReduceScatter([(dwin, NEAR)]))

    dwd1, (l_win_far,) = _mm_tn("ffn1b_dwd", act1, dy1b, F // 2, D, BF16, comm=_ReduceScatter([(dwin, FAR)]))
    dwd1 = dwd1.reshape(4, 2, fb, D)
    (dgate1, dup1), (l_wd1_near,) = _mm_nstream("ffn1b_dact", dy1b, [d1], [0], "nt", [dsu1, sl1], [BF16, BF16],
                                                _dact_epilogue, cn=256, comm=_ReduceScatter([(dwd1, NEAR)]))
    dwg1, (l_wd1_far,) = _mm_tn("ffn1b_dwg", dgate1, h1, F // 2, D, BF16, comm=_ReduceScatter([(dwd1, FAR)]))
    dwg1 = dwg1.reshape(4, 2, fb, D)
    core = owner[1:2]
    dwu1, (s_wg1,) = _mm_tn("ffn1b_dwu", dup1, h1, F // 2, D, BF16, comm=_SiblingExchange([dwg1]))
    dwu1 = dwu1.reshape(4, 2, fb, D)
    p_wg1, (s_wu1,) = _pair_sum("pair_wg1", dwg1, s_wg1, core, comm=_SiblingExchange([dwu1]))
    p_wu1, _ = _pair_sum("pair_wu1", dwu1, s_wu1, core)
    (dh1,), (c_wg1, c_wu1) = _mm_mstream(
        "ffn1b_dh", [dgate1, dup1], [g1, u1], [0, 0], "nn", [], [f32_tile], _plain_epilogue,
        comm=_ChipExchange([p_wg1, p_wu1]))
    dx0, _, d_g1 = _rms_bwd("ffn1b_rms", x0, ffn1_norm_g, dh1, dx1, 1.0)

    small = dict(ffn1=d_g1, mix=d_gmix, ffn2=d_g2, final=d_final, ret=d_ret, gla=d_gla, b_a=d_ba)
    flat = jnp.concatenate([small[k].reshape(-1) for k in SMALL_ORDER]
                           + [d_wa2p[:GATE_RANK].reshape(-1), loss[0, :128]])
    rows = -(-flat.shape[0] // 128)
    rows = -(-rows // 8) * 8
    packed = jnp.pad(flat, (0, rows * 128 - flat.shape[0])).reshape(rows, 128)

    transposed = ("ffn1_w_gate", "ffn1_w_up", "ffn2_w_gate", "ffn2_w_up", "w_in")

    def to_2d(nm, a):
        if nm in transposed:
            return a[0].T
        return a.reshape((1, a.shape[0]) if a.ndim == 1 else a.shape[-2:])

    def from_2d(nm, a):
        return a.T[None] if nm in transposed else a.reshape(params[nm][0].shape)

    dwd2, l_wd2 = _await_owners("await_wd2", wd2_sent, dx0)
    sums_a, (gathered,) = _owner_sums(
        "sum_a", [(dwg2, [l_wg2_near, l_wg2_far]), (dwu2, [l_wu2_near, l_wu2_far]), (dwd2, [l_wd2]),
                  (dwin, [l_win_near, l_win_far]), (dwout, [l_wout])], owner,
        comm=_AllGather([packed], ["plain"]))
    sums_b, _ = _owner_sums("sum_b", [(p_wg1, [c_wg1]), (p_wu1, [c_wu1]), (dwd1, [l_wd1_near, l_wd1_far])], owner)
    big_grads = {"ffn2_w_gate": sums_a[0], "ffn2_w_up": sums_a[1], "ffn2_w_down": sums_a[2], "w_out": sums_a[4],
                 "ffn1_w_gate": sums_b[0], "ffn1_w_up": sums_b[1], "ffn1_w_down": sums_b[2]}
    params = dict(
        ffn2_w_gate=(ffn2_w_gate, m_ffn2_w_gate, v_ffn2_w_gate), ffn2_w_up=(ffn2_w_up, m_ffn2_w_up, v_ffn2_w_up),
        ffn2_w_down=(ffn2_w_down, m_ffn2_w_down, v_ffn2_w_down), w_in=(w_in, m_w_in, v_w_in),
        w_out=(w_out, m_w_out, v_w_out), ffn1_w_gate=(ffn1_w_gate, m_ffn1_w_gate, v_ffn1_w_gate),
        ffn1_w_up=(ffn1_w_up, m_ffn1_w_up, v_ffn1_w_up), ffn1_w_down=(ffn1_w_down, m_ffn1_w_down, v_ffn1_w_down),
        ffn1_norm_g=(ffn1_norm_g, m_ffn1_norm_g, v_ffn1_norm_g), mix_norm_g=(mix_norm_g, m_mix_norm_g, v_mix_norm_g),
        ret_norm_g=(ret_norm_g, m_ret_norm_g, v_ret_norm_g), gla_w_a2=(gla_w_a2, m_gla_w_a2, v_gla_w_a2),
        gla_b_a=(gla_b_a, m_gla_b_a, v_gla_b_a), gla_norm_g=(gla_norm_g, m_gla_norm_g, v_gla_norm_g),
        ffn2_norm_g=(ffn2_norm_g, m_ffn2_norm_g, v_ffn2_norm_g), final_norm_g=(final_norm_g, m_final_norm_g, v_final_norm_g))
    grads, updates = {}, {}

    def run_adam(name, names, grad_2d, n_blocks):
        items = [(grad_2d[nm],) + tuple(to_2d(nm, a) for a in params[nm]) for nm in names]
        res, _ = _adamw_group(name, items, n_blocks)
        for nm, r in zip(names, res):
            grads[nm] = from_2d(nm, grad_2d[nm])
            updates[nm] = tuple(from_2d(nm, a) for a in r)

    run_adam("adamw_big", list(big_grads), big_grads, 4)
    run_adam("adamw_w_in", ["w_in"], {"w_in": sums_a[3]}, 1)

    total = _sum_devices("sum_small", gathered.reshape(N_DEV * rows, 128), rows).reshape(-1)
    sizes = [small[k].size for k in SMALL_ORDER] + [GATE_RANK * QK_W, 128]
    offs = [0]
    for s in sizes:
        offs.append(offs[-1] + s)
    pieces = [total[offs[i]:offs[i + 1]] for i in range(len(sizes))]
    g_small = {k: pieces[i].reshape(small[k].shape) for i, k in enumerate(SMALL_ORDER)}
    g_wa2_full = pieces[len(SMALL_ORDER)].reshape(GATE_RANK, QK_W)
    g_wa2 = lax.dynamic_slice(g_wa2_full, (0, dev * ab), (GATE_RANK, ab))
    loss_total = pieces[len(SMALL_ORDER) + 1][0]

    small_grads = {"ffn1_norm_g": g_small["ffn1"], "mix_norm_g": g_small["mix"], "ret_norm_g": g_small["ret"],
                   "gla_w_a2": g_wa2, "gla_b_a": g_small["b_a"], "gla_norm_g": g_small["gla"],
                   "ffn2_norm_g": g_small["ffn2"], "final_norm_g": g_small["final"]}
    run_adam("adamw_small", list(small_grads), small_grads, 1)

    order = ("ffn1_norm_g", "ffn1_w_gate", "ffn1_w_up", "ffn1_w_down", "mix_norm_g", "w_in", "ret_norm_g", "gla_w_a2",
             "gla_b_a", "gla_norm_g", "w_out", "ffn2_norm_g", "ffn2_w_gate", "ffn2_w_up", "ffn2_w_down", "final_norm_g")
    return (loss_total, dx0[None], *[grads[nm] for nm in order], *[updates[nm][0] for nm in order],
            *[updates[nm][1] for nm in order], *[updates[nm][2] for nm in order])
```

```python
import functools
import math

import jax
import jax.numpy as jnp
from jax import lax
from jax.experimental import pallas as pl
from jax.experimental.pallas import tpu as pltpu

F32 = jnp.float32
BF16 = jnp.bfloat16
MESH = pl.DeviceIdType.MESH
HBM = pl.BlockSpec(memory_space=pltpu.HBM)

N_DEV = 8
RMS_EPS = 1e-6
ROPE_BASE = 10000.0
HEADS = 4
DK = 64
DV = 128
QK_W = HEADS * DK
V_W = HEADS * DV
GATE_RANK = 16
GATE_NORM = 16.0
CHUNK = 64
SUPER = 256
PROJ_W = 3200
C_RQ, C_RK, C_RV, C_RG, C_GQ, C_GK, C_GV, C_GG, C_GL = 0, 256, 512, 1024, 1536, 1792, 2048, 2560, 3072
GL_W = PROJ_W - C_GL
ADAM_LR, ADAM_B1, ADAM_B2, ADAM_EPS, ADAM_WD, ADAM_STEP = 0.001, 0.9, 0.999, 1e-08, 0.01, 10
VMEM_LIMIT_V7X = 52 * 1024 * 1024


def _cparams(**kw):
    return pltpu.CompilerParams(vmem_limit_bytes=VMEM_LIMIT_V7X, **kw)


def _dot(a, b, form, precision=None):
    dims = {"nn": (((1,), (0,)), ((), ())), "nt": (((1,), (1,)), ((), ())), "tn": (((0,), (0,)), ((), ()))}[form]
    return lax.dot_general(a, b, dims, preferred_element_type=F32, precision=precision)


def _sigmoid(x):
    return 1.0 / (1.0 + jnp.exp(-x))


def _coords():
    return lax.axis_index("x"), lax.axis_index("y"), lax.axis_index("c")


class _NoComm:
    inputs, out_shapes, scratch = (), (), ()


class _AllGather:
    def __init__(self, arrays, kinds):
        self.inputs = tuple(arrays)
        self.kinds = tuple(kinds)
        n = len(arrays)
        self.out_shapes = tuple(
            jax.ShapeDtypeStruct((a.shape[0], N_DEV) + a.shape[1:] if k == "stack" else (N_DEV,) + a.shape, a.dtype)
            for a, k in zip(arrays, kinds))
        self.scratch = (pltpu.SemaphoreType.DMA((n, 7)), pltpu.SemaphoreType.DMA((n, 7)),
                        pltpu.SemaphoreType.DMA((n,)))

    def _ctx(self, srcs, outs, sems):
        send_sems, recv_sems, local_sems = sems
        x, y, c = _coords()
        me, sibling = (x, y, c), (x, y, 1 - c)
        chips = [(1 - x, y), (x, 1 - y), (1 - x, 1 - y)]

        def blk(m, dev):
            k = 4 * dev[0] + 2 * dev[1] + dev[2]
            return outs[m].at[:, k] if self.kinds[m] == "stack" else outs[m].at[k]

        def copy(m, s, block, to, src=None):
            return pltpu.make_async_remote_copy(
                src_ref=blk(m, block) if src is None else src, dst_ref=blk(m, block),
                send_sem=send_sems.at[m, s], recv_sem=recv_sems.at[m, s], device_id=to, device_id_type=MESH)

        def mine(m):
            return pltpu.make_async_copy(srcs[m], blk(m, me), local_sems.at[m])

        def first(m):
            return [copy(m, 0, me, sibling, src=srcs[m])] + [
                copy(m, 1 + j, me, (*chip, c), src=srcs[m]) for j, chip in enumerate(chips)]

        return me, sibling, chips, c, copy, mine, first

    def start(self, srcs, outs, sems):
        me, sibling, chips, c, copy, mine, first = self._ctx(srcs, outs, sems)
        for m in range(len(srcs)):
            mine(m).start()
            for cp in first(m):
                cp.start()

    def mid(self, srcs, outs, sems):
        me, sibling, chips, c, copy, mine, first = self._ctx(srcs, outs, sems)
        for j, chip in enumerate(chips):
            for m in range(len(srcs)):
                copy(m, 1 + j, (*chip, c), me).wait_recv()
                copy(m, 4 + j, (*chip, c), sibling).start()

    def finish(self, srcs, outs, sems):
        me, sibling, chips, c, copy, mine, first = self._ctx(srcs, outs, sems)
        for m in range(len(srcs)):
            copy(m, 0, sibling, me).wait_recv()
            for j, chip in enumerate(chips):
                copy(m, 4 + j, (*chip, 1 - c), me).wait_recv()
            for cp in first(m):
                cp.wait_send()
            for j, chip in enumerate(chips):
                copy(m, 4 + j, (*chip, c), sibling).wait_send()
            mine(m).wait()


RELATIONS = ((0, 0, 1), (1, 0, 0), (0, 1, 0), (1, 1, 0), (1, 0, 1), (0, 1, 1), (1, 1, 1))
SEM = pl.BlockSpec(memory_space=pltpu.SEMAPHORE)
SPLIT_PARAMS = dict(has_side_effects=pltpu.SideEffectType.DATAFLOW_SIDE_EFFECTING)


def _owner_copies(grad_ref, land_ref, send_sems, recv_sems):
    x, y, c = _coords()
    copies = []
    for s, (fx, fy, fc) in enumerate(RELATIONS):
        px = 1 - x if fx else x
        py = 1 - y if fy else y
        pc = 1 - c if fc else c
        copies.append(pltpu.make_async_remote_copy(
            src_ref=grad_ref.at[2 * px + py, pc], dst_ref=land_ref.at[s], send_sem=send_sems.at[s],
            recv_sem=recv_sems.at[s], device_id=(px, py, pc), device_id_type=MESH))
    return copies


def _send_to_owners(name, grad):
    n = len(RELATIONS)
    land_shape = (n,) + grad.shape[2:]

    def body(g_ref, land_ref, send_sems, recv_sems, g_thru, land_thru, token):
        for cp in _owner_copies(g_ref, land_ref, send_sems, recv_sems):
            cp.start()
        token[...] = jnp.zeros_like(token)

    return pl.pallas_call(
        body, name=name,
        out_shape=(pltpu.SemaphoreType.DMA((n,)), pltpu.SemaphoreType.DMA((n,)), pltpu.HBM(grad.shape, grad.dtype),
                   pltpu.HBM(land_shape, grad.dtype), jax.ShapeDtypeStruct((8, 128), F32)),
        in_specs=(HBM, HBM), out_specs=(SEM, SEM, HBM, HBM, pl.BlockSpec(memory_space=pltpu.VMEM)),
        input_output_aliases={0: 2, 1: 3}, compiler_params=pltpu.CompilerParams(**SPLIT_PARAMS),
    )(pltpu.with_memory_space_constraint(grad, pltpu.HBM),
      pltpu.with_memory_space_constraint(lax.empty(land_shape, grad.dtype), pltpu.HBM))


def _await_owners(name, started, after):
    send_sems, recv_sems, g_thru, land_thru, _ = started

    def body(g_ref, land_ref, send_sems, recv_sems, after_ref, g_out, land_out):
        for cp in _owner_copies(g_ref, land_ref, send_sems, recv_sems):
            cp.wait_send()
            cp.wait_recv()

    return pl.pallas_call(
        body, name=name, out_shape=(pltpu.HBM(g_thru.shape, g_thru.dtype), pltpu.HBM(land_thru.shape, land_thru.dtype)),
        in_specs=(HBM, HBM, SEM, SEM, pl.BlockSpec(memory_space=pl.ANY)), out_specs=(HBM, HBM),
        input_output_aliases={0: 0, 1: 1}, compiler_params=pltpu.CompilerParams(**SPLIT_PARAMS),
    )(g_thru, land_thru, send_sems, recv_sems, after)


def _call(name, main, grid, in_specs, out_specs, out_shape, args, scratch=(), comm=None, prefetch=None, after=None):
    comm = comm or _NoComm()
    n_main = len(in_specs)
    if after is not None:
        in_specs = list(in_specs) + [pl.BlockSpec(after.shape, lambda *_: (0,) * after.ndim)]
        args = tuple(args) + (after,)
    counts = [len(in_specs), len(comm.inputs), len(out_shape), len(comm.out_shapes), len(scratch), len(comm.scratch)]
    n_steps = math.prod(grid)
    hosted = bool(comm.inputs)

    def body(*refs):
        if prefetch is not None:
            refs = refs[1:]
        parts, at = [], 0
        for n in counts:
            parts.append(refs[at:at + n])
            at += n
        ins, c_in, outs, c_out, scr, c_scr = parts
        ins = ins[:n_main]
        step = pl.program_id(0)
        for d in range(1, len(grid)):
            step = step * grid[d] + pl.program_id(d)
        if hosted:
            @pl.when(step == 0)
            def _():
                comm.start(c_in, c_out, c_scr)
        main(ins, outs, scr)
        if hosted:
            @pl.when(step == max(n_steps - 2, 0))
            def _():
                comm.mid(c_in, c_out, c_scr)

            @pl.when(step == n_steps - 1)
            def _():
                comm.finish(c_in, c_out, c_scr)

    all_in = list(in_specs) + [HBM] * counts[1]
    all_out = list(out_specs) + [HBM] * counts[3]
    all_scratch = list(scratch) + list(comm.scratch)
    shapes = list(out_shape) + list(comm.out_shapes)
    if prefetch is None:
        res = pl.pallas_call(body, name=name, grid=grid, in_specs=all_in, out_specs=all_out, out_shape=shapes,
                             scratch_shapes=all_scratch, compiler_params=_cparams())(*args, *comm.inputs)
    else:
        res = pl.pallas_call(
            body, name=name, out_shape=shapes,
            grid_spec=pltpu.PrefetchScalarGridSpec(num_scalar_prefetch=1, grid=grid, in_specs=all_in,
                                                   out_specs=all_out, scratch_shapes=all_scratch),
            compiler_params=_cparams())(prefetch, *args, *comm.inputs)
    return res[:counts[2]], res[counts[2]:]


def _rms_fwd(name, x, g, comm=None):
    T, D = x.shape
    tm = min(T, 512)

    def main(ins, outs, scr):
        x_ref, g_ref = ins
        xv = x_ref[...]
        r = lax.rsqrt(jnp.mean(xv * xv, axis=-1, keepdims=True) + RMS_EPS)
        outs[0][...] = (xv * r * g_ref[...]).astype(outs[0].dtype)

    tile = pl.BlockSpec((tm, D), lambda i: (i, 0))
    (h,), extra = _call(name, main, (T // tm,), [tile, pl.BlockSpec((1, D), lambda i: (0, 0))], [tile],
                        [jax.ShapeDtypeStruct((T, D), BF16)], (x, g), comm=comm)
    return h, extra


def _final_loss_epilogue(scale, out_scale):
    def ep(acc, ex, outs):
        res_ref, g_ref, t_ref = ex
        dx_ref, dxb_ref, dg_ref, loss_ref = outs
        n = acc.shape[-1]
        xv = res_ref[...] + scale * acc
        r = lax.rsqrt(jnp.mean(xv * xv, axis=-1, keepdims=True) + RMS_EPS)
        xhat = xv * r
        err = xhat * g_ref[...] - t_ref[...]

        @pl.when(pl.program_id(0) == 0)
        def _():
            dg_ref[...] = jnp.zeros_like(dg_ref)
            loss_ref[...] = jnp.zeros_like(loss_ref)

        loss_ref[...] += jnp.broadcast_to(jnp.sum(err * err) * (0.5 / n), loss_ref.shape)
        dy = err * (1.0 / n)
        dg_ref[...] += jnp.sum(dy * xhat, axis=0, keepdims=True)
        dxhat = dy * g_ref[...]
        dx = r * (dxhat - xhat * jnp.mean(dxhat * xhat, axis=-1, keepdims=True))
        dx_ref[...] = dx
        dxb_ref[...] = (out_scale * dx).astype(dxb_ref.dtype)
    return ep


def _mm_nstream(name, a, ws, w_sel, w_form, comps, out_dtypes, epilogue, cn, rows=1024, comm=None, after=None):
    T, K = a.shape
    N = ws[0].shape[1]
    rows = min(rows, T)
    assert N % cn == 0 and T % rows == 0
    n_w, n_c = len(ws), len(comps)

    def main(ins, outs, scr):
        a_ref = ins[0]
        w_refs = ins[1:1 + n_w]
        c_refs = ins[1 + n_w:1 + n_w + n_c]

        for r in range(T // rows):
            sl = slice(r * rows, (r + 1) * rows)
            a_blk = a_ref[sl, :]
            dots = [_dot(a_blk, w_ref[...], w_form) for w_ref in w_refs]
            res = epilogue(dots, [c_ref[sl, :] for c_ref in c_refs])
            for o_ref, o in zip(outs, res):
                o_ref[sl, :] = o.astype(o_ref.dtype)

    if w_form == "nt":
        w_specs = [pl.BlockSpec((None, cn, K), functools.partial(lambda j, s: (s, j, 0), s=s)) for s in w_sel]
    else:
        w_specs = [pl.BlockSpec((K, cn), lambda j: (0, j)) for _ in ws]
    chunk = pl.BlockSpec((T, cn), lambda j: (0, j))
    return _call(name, main, (N // cn,), [pl.BlockSpec((T, K), lambda j: (0, 0))] + w_specs + [chunk] * n_c,
                 [chunk] * len(out_dtypes), [jax.ShapeDtypeStruct((T, N), dt) for dt in out_dtypes],
                 (a, *ws, *comps), comm=comm, after=after)


def _mm_mstream(name, as_, ws, w_sel, w_form, extras, outs_desc, epilogue, tm=512, comm=None, after=None):
    T = as_[0].shape[0]
    tm = min(tm, T)
    n_a = len(as_)
    w_shapes = [w.shape[-2:] for w in ws]
    N = w_shapes[0][1] if w_form == "nn" else w_shapes[0][0]

    def main(ins, outs, scr):
        a_refs = ins[:n_a]
        w_refs = ins[n_a:2 * n_a]
        acc = None
        for a_ref, w_ref in zip(a_refs, w_refs):
            d = _dot(a_ref[...], w_ref[...], w_form)
            acc = d if acc is None else acc + d
        epilogue(acc, ins[2 * n_a:], outs)

    kind_spec = {"tile": pl.BlockSpec((tm, N), lambda i: (i, 0)), "vec": pl.BlockSpec((1, N), lambda i: (0, 0))}
    kind_shape = {"tile": (T, N), "vec": (1, N)}
    a_specs = [pl.BlockSpec((tm, a.shape[1]), lambda i: (i, 0)) for a in as_]
    w_specs = []
    for w, s in zip(ws, w_sel):
        if w.ndim == 3:
            w_specs.append(pl.BlockSpec((None,) + tuple(w.shape[1:]), functools.partial(lambda i, s: (s, 0, 0), s=s),
                                        pipeline_mode=pl.Buffered(1)))
        else:
            w_specs.append(pl.BlockSpec(tuple(w.shape), lambda i: (0, 0), pipeline_mode=pl.Buffered(1)))
    args = list(as_) + list(ws) + [e for e, _ in extras]
    return _call(name, main, (T // tm,), a_specs + w_specs + [kind_spec[k] for _, k in extras],
                 [kind_spec[k] for _, k in outs_desc],
                 [jax.ShapeDtypeStruct(kind_shape[k], dt) for dt, k in outs_desc], args, comm=comm, after=after)


def _residual_rms_epilogue(scale):
    def ep(acc, ex, outs):
        xv = ex[0][...] + scale * acc
        outs[0][...] = xv
        r = lax.rsqrt(jnp.mean(xv * xv, axis=-1, keepdims=True) + RMS_EPS)
        outs[1][...] = (xv * r * ex[1][...]).astype(outs[1].dtype)
    return ep


def _rms_bwd_epilogue(out_scale):
    def ep(acc, ex, outs):
        x_ref, g_ref, dres_ref = ex
        dx_ref, dxb_ref, dg_ref = outs
        xv = x_ref[...]
        r = lax.rsqrt(jnp.mean(xv * xv, axis=-1, keepdims=True) + RMS_EPS)
        xhat = xv * r

        @pl.when(pl.program_id(0) == 0)
        def _():
            dg_ref[...] = jnp.zeros_like(dg_ref)

        dg_ref[...] += jnp.sum(acc * xhat, axis=0, keepdims=True)
        dxhat = acc * g_ref[...]
        dx = r * (dxhat - xhat * jnp.mean(dxhat * xhat, axis=-1, keepdims=True)) + dres_ref[...]
        dx_ref[...] = dx
        dxb_ref[...] = (out_scale * dx).astype(dxb_ref.dtype)
    return ep


def _mm_tn(name, a, b, tmo, tno, out_dtype, tk=1024, comm=None, after=None):
    T, Ma = a.shape
    Nb = b.shape[1]
    tk = min(tk, T)
    nk = T // tk

    def main(ins, outs, scr):
        a_ref, b_ref = ins
        (acc_ref,) = scr
        k = pl.program_id(2)

        @pl.when(k == 0)
        def _():
            acc_ref[...] = jnp.zeros_like(acc_ref)

        acc_ref[...] += _dot(a_ref[...], b_ref[...], "tn")

        @pl.when(k == nk - 1)
        def _():
            outs[0][...] = acc_ref[...].astype(outs[0].dtype)

    (out,), extra = _call(
        name, main, (Ma // tmo, Nb // tno, nk),
        [pl.BlockSpec((tk, tmo), lambda i, j, k: (k, i)), pl.BlockSpec((tk, tno), lambda i, j, k: (k, j))],
        [pl.BlockSpec((tmo, tno), lambda i, j, k: (i, j))], [jax.ShapeDtypeStruct((Ma, Nb), out_dtype)],
        (a, b), scratch=[pltpu.VMEM((tmo, tno), F32)], comm=comm, after=after)
    return out, extra


def _swiglu_parts(g, u):
    s = _sigmoid(g)
    silu = g * s
    return [u * (s + silu * (1.0 - s)), silu, silu * u]


def _silu_mul_epilogue(dots, comps):
    g, u = dots
    return _swiglu_parts(g, u)


def _gate_parts_epilogue(dots, comps):
    (g,) = dots
    s = _sigmoid(g)
    silu = g * s
    return [s + silu * (1.0 - s), silu]


def _up_act_epilogue(dots, comps):
    (u,) = dots
    return [u * comps[0].astype(F32), u * comps[1].astype(F32)]


def _dact_epilogue(dots, comps):
    dact = dots[0].astype(BF16)
    return [dact * comps[0], dact * comps[1]]


def _identity_epilogue(dots, comps):
    return list(dots)


def _swap_halves(x):
    lane = lax.broadcasted_iota(jnp.int32, x.shape, 1)
    first = (lane % DK) < (DK // 2)
    return jnp.where(first, pltpu.roll(x, 128 - DK // 2, 1), pltpu.roll(x, DK // 2, 1))


def _rotary(t, cos, sin_signed):
    halves = []
    for p in range(QK_W // 128):
        th = t[:, 128 * p:128 * (p + 1)]
        halves.append(th * cos + _swap_halves(th) * sin_signed)
    return jnp.concatenate(halves, axis=1)


def _rotary_transposed(d, cos, sin_signed):
    halves = []
    for p in range(QK_W // 128):
        dh = d[:, 128 * p:128 * (p + 1)]
        halves.append(dh * cos + _swap_halves(dh * sin_signed))
    return jnp.concatenate(halves, axis=1)


def _log_sigmoid(x):
    return jnp.minimum(x, 0.0) - jnp.log(1.0 + jnp.exp(-jnp.abs(x)))


def _attn_masks():
    row = lax.broadcasted_iota(jnp.int32, (SUPER, SUPER), 0)
    col = lax.broadcasted_iota(jnp.int32, (SUPER, SUPER), 1)
    same = (row // CHUNK) == (col // CHUNK)
    return row, col, same


def _group_inputs(grp, pr, cos, sin_signed, lg, wa2, ba):
    seg = lambda lo, width: pr[:, lo:lo + width].astype(F32)
    if grp == 0:
        q = _rotary(seg(C_RQ, QK_W), cos, sin_signed)
        k = _rotary(seg(C_RK, QK_W), cos, sin_signed) * (DK ** -0.5)
        v = pr[:, C_RV:C_RV + V_W]
        gate = seg(C_RG, V_W)
        pos = lax.broadcasted_iota(jnp.int32, (SUPER, QK_W), 0).astype(F32) + 1.0
        return q, k, v, gate, pos * lg, None, None
    q = seg(C_GQ, QK_W) * (DK ** -0.5)
    k = seg(C_GK, QK_W)
    v = pr[:, C_GV:C_GV + V_W]
    gate = seg(C_GG, V_W)
    glow = pr[:, C_GL:C_GL + GL_W]
    logit = _dot(glow.astype(BF16), wa2.astype(BF16), "nn") + ba
    la = _log_sigmoid(logit) * (1.0 / GATE_NORM)
    row, col, _ = _attn_masks()
    lower = (col <= row).astype(F32)
    b_cum = _dot(lower, la, "nn", precision=lax.Precision.HIGHEST)
    return q, k, v, gate, b_cum, glow, logit


def _decay_factors(q, k, b_cum):
    c = b_cum[SUPER // 2 - 1:SUPER // 2, :]
    bl = b_cum[SUPER - 1:SUPER, :]
    e1 = jnp.exp(b_cum - c)
    e2 = jnp.exp(c - b_cum)
    e_b = jnp.exp(b_cum)
    e_l = jnp.exp(bl - b_cum)
    return dict(e1=e1, e2=e2, eb=e_b, el=e_l, ebl=jnp.exp(bl),
                qp=q * e1, qm=q * e2, kp=k * e1, km=k * e2, qs=q * e_b, kl=k * e_l)


def _state_block_mask():
    r = lax.broadcasted_iota(jnp.int32, (V_W, QK_W), 0)
    c = lax.broadcasted_iota(jnp.int32, (V_W, QK_W), 1)
    return (r // DV) == (c // DK)


def _attn_fwd(proj, cos, sin_signed, lg, wa2p, ba, gn_ret, gn_gla, x_res, w_out, g_next, comm=None):
    T = proj.shape[0]
    n_s = T // SUPER
    D = x_res.shape[1]

    def main(ins, outs, scr):
        pr_ref, cos_ref, sin_ref, lg_ref, wa2_ref, ba_ref, gr_ref, gg_ref, xres_ref, wout_ref, gnext_ref = ins
        o_ref, y_ref, st_ref, x_ref, h_ref = outs
        (s_ref,) = scr
        i = pl.program_id(0)

        @pl.when(i == 0)
        def _():
            s_ref[...] = jnp.zeros_like(s_ref)

        pr = pr_ref
        row, col, same = _attn_masks()
        m1 = col <= row
        m2 = jnp.logical_and(col > row, same)
        lane = lax.broadcasted_iota(jnp.int32, (1, QK_W), 1)
        blockmask = _state_block_mask()
        for grp in range(2):
            q, k, v, gate, b_cum, _, _ = _group_inputs(grp, pr, cos_ref[...], sin_ref[...], lg_ref[...],
                                                      wa2_ref[...], ba_ref[...])
            f = _decay_factors(q, k, b_cum)
            gn = gr_ref[...] if grp == 0 else gg_ref[...]
            s_prev = s_ref[grp]
            st_ref[0, grp] = s_prev
            o_inter = _dot(f["qs"].astype(BF16), s_prev.astype(BF16), "nt")
            kmb = f["km"].astype(BF16)
            kpb = f["kp"].astype(BF16)
            vb = v.astype(BF16)
            for h in range(HEADS):
                hm = (lane // DK) == h
                a1 = _dot(jnp.where(hm, f["qp"], 0.0).astype(BF16), kmb, "nt")
                a2 = _dot(jnp.where(hm, f["qm"], 0.0).astype(BF16), kpb, "nt")
                a = jnp.where(m1, a1, jnp.where(m2, a2, 0.0))
                lo = grp * V_W + h * DV
                o_h = _dot(a.astype(BF16), vb[:, h * DV:(h + 1) * DV], "nn") + o_inter[:, h * DV:(h + 1) * DV]
                o_ref[:, lo:lo + DV] = o_h
                r = lax.rsqrt(jnp.mean(o_h * o_h, axis=-1, keepdims=True) + RMS_EPS)
                gte = gate[:, h * DV:(h + 1) * DV]
                y = o_h * r * gn[:, h * DV:(h + 1) * DV] * (gte * _sigmoid(gte))
                y_ref[:, lo:lo + DV] = y.astype(y_ref.dtype)
            upd = _dot(vb, f["kl"].astype(BF16), "tn")
            s_ref[grp] = s_prev * f["ebl"] + jnp.where(blockmask, upd, 0.0)
        xv = xres_ref[...] + _dot(y_ref[...], wout_ref[...], "nn")
        x_ref[...] = xv
        r = lax.rsqrt(jnp.mean(xv * xv, axis=-1, keepdims=True) + RMS_EPS)
        h_ref[...] = (xv * r * gnext_ref[...]).astype(h_ref.dtype)

    const = lambda shape: pl.BlockSpec(shape, lambda i: tuple(0 for _ in shape))
    rows = lambda w: pl.BlockSpec((SUPER, w), lambda i: (i, 0))
    return _call(
        "attn_fwd", main, (n_s,),
        [rows(PROJ_W), rows(128), rows(128),
         const((1, QK_W)), const((GL_W, QK_W)), const((1, QK_W)), const((1, V_W)), const((1, V_W)),
         rows(D), const((2 * V_W, D)), const((1, D))],
        [rows(2 * V_W), rows(2 * V_W), pl.BlockSpec((1, 2, V_W, QK_W), lambda i: (i, 0, 0, 0)), rows(D), rows(D)],
        [jax.ShapeDtypeStruct((T, 2 * V_W), F32), jax.ShapeDtypeStruct((T, 2 * V_W), BF16),
         jax.ShapeDtypeStruct((n_s, 2, V_W, QK_W), F32), jax.ShapeDtypeStruct((T, D), F32),
         jax.ShapeDtypeStruct((T, D), BF16)],
        (proj, cos, sin_signed, lg, wa2p, ba, gn_ret, gn_gla, x_res, w_out, g_next),
        scratch=[pltpu.VMEM((2, V_W, QK_W), F32)], comm=comm)


def _attn_bwd(proj, cos, sin_signed, lg, wa2p, ba, gn_ret, gn_gla, o, dx, w_out, states, comm=None, after=None):
    T = proj.shape[0]
    n_s = T // SUPER
    D = dx.shape[1]

    def main(ins, outs, scr):
        pr_ref, cos_ref, sin_ref, lg_ref, wa2_ref, ba_ref, gr_ref, gg_ref, o_ref, dx_ref, wout_ref, st_ref = ins
        dp_ref, dgr_ref, dgg_ref, dba_ref, dwa_ref = outs
        (ds_ref, dy_ref) = scr
        i = pl.program_id(0)
        dy_ref[...] = _dot(dx_ref[...], wout_ref[...], "nt")

        @pl.when(i == 0)
        def _():
            ds_ref[...] = jnp.zeros_like(ds_ref)
            dgr_ref[...] = jnp.zeros_like(dgr_ref)
            dgg_ref[...] = jnp.zeros_like(dgg_ref)
            dba_ref[...] = jnp.zeros_like(dba_ref)
            dwa_ref[...] = jnp.zeros_like(dwa_ref)

        pr = pr_ref
        cos = cos_ref[...]
        sin_signed = sin_ref[...]
        row, col, same = _attn_masks()
        m1 = col <= row
        m2 = jnp.logical_and(col > row, same)
        m1t = row <= col
        m2t = jnp.logical_and(row > col, same)
        lane = lax.broadcasted_iota(jnp.int32, (1, QK_W), 1)
        blockmask = _state_block_mask()
        for grp in range(2):
            q, k, v, gate, b_cum, glow, logit = _group_inputs(grp, pr, cos, sin_signed, lg_ref[...],
                                                              wa2_ref[...], ba_ref[...])
            f = _decay_factors(q, k, b_cum)
            gn = gr_ref[...] if grp == 0 else gg_ref[...]
            dgn_ref = dgr_ref if grp == 0 else dgg_ref
            do_parts, dgate_parts, dgn_parts = [], [], []
            for h in range(HEADS):
                lo = grp * V_W + h * DV
                o_h = o_ref[:, lo:lo + DV]
                r = lax.rsqrt(jnp.mean(o_h * o_h, axis=-1, keepdims=True) + RMS_EPS)
                n = o_h * r
                gte = gate[:, h * DV:(h + 1) * DV]
                sg = _sigmoid(gte)
                dy_h = dy_ref[:, lo:lo + DV]
                gn_h = gn[:, h * DV:(h + 1) * DV]
                dgate_parts.append(dy_h * n * gn_h * (sg * (1.0 + gte * (1.0 - sg))))
                dz = dy_h * (gte * sg)
                dgn_parts.append(jnp.sum(dz * n, axis=0, keepdims=True))
                dn = dz * gn_h
                do_parts.append(r * (dn - n * jnp.mean(dn * n, axis=-1, keepdims=True)))
            dgn_ref[...] += jnp.concatenate(dgn_parts, axis=1)
            dgate = jnp.concatenate(dgate_parts, axis=1)
            do = jnp.concatenate(do_parts, axis=1)
            dob = do.astype(BF16)
            vb = v.astype(BF16)
            s_prev = st_ref[0, grp]
            ds_new = ds_ref[grp]
            dsb = ds_new.astype(BF16)
            qpb, qmb = f["qp"].astype(BF16), f["qm"].astype(BF16)
            kpb, kmb = f["kp"].astype(BF16), f["km"].astype(BF16)
            dqp = jnp.zeros((SUPER, QK_W), F32)
            dqm = jnp.zeros((SUPER, QK_W), F32)
            dkp = jnp.zeros((SUPER, QK_W), F32)
            dkm = jnp.zeros((SUPER, QK_W), F32)
            dv_parts = []
            for h in range(HEADS):
                hm = (lane // DK) == h
                qp_h = jnp.where(hm, f["qp"], 0.0).astype(BF16)
                qm_h = jnp.where(hm, f["qm"], 0.0).astype(BF16)
                kp_h = jnp.where(hm, f["kp"], 0.0).astype(BF16)
                km_h = jnp.where(hm, f["km"], 0.0).astype(BF16)
                at = jnp.where(m1t, _dot(km_h, qpb, "nt"), jnp.where(m2t, _dot(kp_h, qmb, "nt"), 0.0))
                do_h = dob[:, h * DV:(h + 1) * DV]
                v_h = vb[:, h * DV:(h + 1) * DV]
                dv_parts.append(_dot(at.astype(BF16), do_h, "nn"))
                da = _dot(do_h, v_h, "nt")
                dat = _dot(v_h, do_h, "nt")
                da1 = jnp.where(m1, da, 0.0).astype(BF16)
                da2 = jnp.where(m2, da, 0.0).astype(BF16)
                da1t = jnp.where(m1t, dat, 0.0).astype(BF16)
                da2t = jnp.where(m2t, dat, 0.0).astype(BF16)
                dqp = dqp + _dot(da1, km_h, "nn")
                dqm = dqm + _dot(da2, kp_h, "nn")
                dkm = dkm + _dot(da1t, qp_h, "nn")
                dkp = dkp + _dot(da2t, qm_h, "nn")
            klb = f["kl"].astype(BF16)
            qsb = f["qs"].astype(BF16)
            dqs = _dot(dob, s_prev.astype(BF16), "nn")
            dkl = _dot(vb, dsb, "nn")
            dv = jnp.concatenate(dv_parts, axis=1) + _dot(klb, dsb, "nt")
            ds_ref[grp] = ds_new * f["ebl"] + jnp.where(blockmask, _dot(dob, qsb, "tn"), 0.0)
            dq = dqp * f["e1"] + dqm * f["e2"] + dqs * f["eb"]
            dk = dkm * f["e2"] + dkp * f["e1"] + dkl * f["el"]
            if grp == 0:
                dq = _rotary_transposed(dq, cos, sin_signed)
                dk = _rotary_transposed(dk * (DK ** -0.5), cos, sin_signed)
                dp_ref[:, C_RQ:C_RQ + QK_W] = dq.astype(dp_ref.dtype)
                dp_ref[:, C_RK:C_RK + QK_W] = dk.astype(dp_ref.dtype)
                dp_ref[:, C_RV:C_RV + V_W] = dv.astype(dp_ref.dtype)
                dp_ref[:, C_RG:C_RG + V_W] = dgate.astype(dp_ref.dtype)
            else:
                dkl_kl = dkl * klb.astype(F32)
                db = (dqp * qpb.astype(F32) - dkm * kmb.astype(F32) - dqm * qmb.astype(F32)
                      + dkp * kpb.astype(F32) + dqs * qsb.astype(F32) - dkl_kl)
                last = (jnp.sum(dkl_kl, axis=0, keepdims=True)
                        + f["ebl"] * jnp.sum(s_prev * ds_new, axis=0, keepdims=True))
                rowq = lax.broadcasted_iota(jnp.int32, (SUPER, QK_W), 0)
                db = db + jnp.where(rowq == SUPER - 1, last, 0.0)
                upper = (col >= row).astype(F32)
                dla = _dot(upper, db, "nn", precision=lax.Precision.HIGHEST)
                dlogit = dla * (1.0 / GATE_NORM) * (1.0 - _sigmoid(logit))
                dlb = dlogit.astype(BF16)
                dglow = _dot(dlb, wa2_ref[...].astype(BF16), "nt")
                dwa_ref[...] += _dot(glow.astype(BF16), dlb, "tn")
                dba_ref[...] += jnp.sum(dlogit, axis=0, keepdims=True)
                dp_ref[:, C_GQ:C_GQ + QK_W] = (dq * (DK ** -0.5)).astype(dp_ref.dtype)
                dp_ref[:, C_GK:C_GK + QK_W] = dk.astype(dp_ref.dtype)
                dp_ref[:, C_GV:C_GV + V_W] = dv.astype(dp_ref.dtype)
                dp_ref[:, C_GG:C_GG + V_W] = dgate.astype(dp_ref.dtype)
                dp_ref[:, C_GL:C_GL + GL_W] = dglow.astype(dp_ref.dtype)

    rev = lambda i: n_s - 1 - i
    const = lambda shape: pl.BlockSpec(shape, lambda i: tuple(0 for _ in shape))
    return _call(
        "attn_bwd", main, (n_s,),
        [pl.BlockSpec((SUPER, PROJ_W), lambda i: (rev(i), 0)),
         pl.BlockSpec((SUPER, 128), lambda i: (rev(i), 0)), pl.BlockSpec((SUPER, 128), lambda i: (rev(i), 0)),
         const((1, QK_W)), const((GL_W, QK_W)), const((1, QK_W)), const((1, V_W)), const((1, V_W)),
         pl.BlockSpec((SUPER, 2 * V_W), lambda i: (rev(i), 0)),
         pl.BlockSpec((SUPER, D), lambda i: (rev(i), 0)), const((2 * V_W, D)),
         pl.BlockSpec((1, 2, V_W, QK_W), lambda i: (rev(i), 0, 0, 0))],
        [pl.BlockSpec((SUPER, PROJ_W), lambda i: (rev(i), 0)),
         const((1, V_W)), const((1, V_W)), const((1, QK_W)), const((GL_W, QK_W))],
        [jax.ShapeDtypeStruct((T, PROJ_W), BF16),
         jax.ShapeDtypeStruct((1, V_W), F32), jax.ShapeDtypeStruct((1, V_W), F32),
         jax.ShapeDtypeStruct((1, QK_W), F32), jax.ShapeDtypeStruct((GL_W, QK_W), F32)],
        (proj, cos, sin_signed, lg, wa2p, ba, gn_ret, gn_gla, o, dx, w_out, states),
        scratch=[pltpu.VMEM((2, V_W, QK_W), F32), pltpu.VMEM((SUPER, 2 * V_W), F32)], comm=comm, after=after)


def _rotary_tables(T):
    half = DK // 2
    inv = ROPE_BASE ** (-jnp.arange(half, dtype=F32) * 2.0 / DK)
    ang = jnp.arange(T, dtype=F32)[:, None] * inv[None, :]
    cos, sin = jnp.cos(ang), jnp.sin(ang)
    cos_head = jnp.concatenate([cos, cos], axis=1)
    sin_head = jnp.concatenate([-sin, sin], axis=1)
    return jnp.tile(cos_head, (1, 128 // DK)), jnp.tile(sin_head, (1, 128 // DK))


def _sum_devices(name, gathered, m_per):
    def body(g_ref, o_ref):
        acc = g_ref[0:m_per, :]
        for k in range(1, N_DEV):
            acc = acc + g_ref[k * m_per:(k + 1) * m_per, :]
        o_ref[...] = acc

    return pl.pallas_call(body, name=name, out_shape=jax.ShapeDtypeStruct((m_per, 128), F32))(gathered)


def _owner_sums(name, items, owner, comm=None):
    counts = [1 + len(landed) for _, landed in items]

    def main(ins, outs, scr):
        at = 0
        for o_ref, n in zip(outs, counts):
            acc = ins[at][...].astype(F32)
            for l_ref in ins[at + 1:at + n]:
                for j in range(l_ref.shape[0]):
                    acc = acc + l_ref[j].astype(F32)
            o_ref[...] = acc
            at += n

    once = pl.Buffered(1)
    in_specs, out_specs, out_shape, args = [], [], [], []
    for grad, landed in items:
        R, C = grad.shape[-2:]
        in_specs.append(pl.BlockSpec((None, None, R, C), lambda i, s: (s[0], s[1], 0, 0), pipeline_mode=once))
        in_specs +=[pl.BlockSpec(tuple(l.shape), lambda i, s: (0, 0, 0), pipeline_mode=once) for l in landed]
        out_specs.append(pl.BlockSpec((R, C), lambda i, s: (0, 0)))
        out_shape.append(jax.ShapeDtypeStruct((R, C), F32))
        args += [grad, *landed]
    return _call(name, main, (1,), in_specs, out_specs, out_shape, args, comm=comm, prefetch=owner)


def _adamw_group(name, items, n_blocks, comm=None):
    n = len(items)

    def main(ins, outs, scr):
        for p in range(n):
            g_ref, w_ref, m_ref, v_ref = ins[4 * p:4 * p + 4]
            d_ref, nm_ref, nv_ref = outs[3 * p:3 * p + 3]
            gv = g_ref[...]
            nm = ADAM_B1 * m_ref[...] + (1.0 - ADAM_B1) * gv
            nv = ADAM_B2 * v_ref[...] + (1.0 - ADAM_B2) * (gv * gv)
            m_hat = nm / (1.0 - ADAM_B1 ** ADAM_STEP)
            v_hat = nv / (1.0 - ADAM_B2 ** ADAM_STEP)
            d_ref[...] = -ADAM_LR * (m_hat / (jnp.sqrt(v_hat) + ADAM_EPS) + ADAM_WD * w_ref[...])
            nm_ref[...] = nm
            nv_ref[...] = nv

    in_specs, out_specs, out_shape, args = [], [], [], []
    for item in items:
        R, C = item[1].shape
        assert R % n_blocks == 0
        spec = pl.BlockSpec((R // n_blocks, C), lambda i: (i, 0))
        in_specs += [spec] * 4
        out_specs += [spec] * 3
        out_shape += [jax.ShapeDtypeStruct((R, C), F32)] * 3
        args += list(item)
    outs, extra = _call(name, main, (n_blocks,), in_specs, out_specs, out_shape, args, comm=comm)
    return [tuple(outs[3 * p:3 * p + 3]) for p in range(n)], extra


SMALL_ORDER = ("ffn1", "mix", "ffn2", "final", "ret", "gla", "b_a")


def kernel(x, ffn1_norm_g, ffn1_w_gate, ffn1_w_up, ffn1_w_down, mix_norm_g, w_in, ret_norm_g, gla_w_a2, gla_b_a, gla_norm_g, w_out, ffn2_norm_g, ffn2_w_gate, ffn2_w_up, ffn2_w_down, final_norm_g, loss_target, m_ffn1_norm_g, m_ffn1_w_gate, m_ffn1_w_up, m_ffn1_w_down, m_mix_norm_g, m_w_in, m_ret_norm_g, m_gla_w_a2, m_gla_b_a, m_gla_norm_g, m_w_out, m_ffn2_norm_g, m_ffn2_w_gate, m_ffn2_w_up, m_ffn2_w_down, m_final_norm_g, v_ffn1_norm_g, v_ffn1_w_gate, v_ffn1_w_up, v_ffn1_w_down, v_mix_norm_g, v_w_in, v_ret_norm_g, v_gla_w_a2, v_gla_b_a, v_gla_norm_g, v_w_out, v_ffn2_norm_g, v_ffn2_w_gate, v_ffn2_w_up, v_ffn2_w_down, v_final_norm_g):
    xi, yi, ci = _coords()
    dev = 4 * xi + 2 * yi + ci
    owner = jnp.stack([2 * xi + yi, ci]).astype(jnp.int32)

    x0, target = x[0], loss_target[0]
    T, D = x0.shape
    fb = ffn1_w_gate.shape[2]
    ib = w_in.shape[2]
    ab = gla_w_a2.shape[2]
    F = N_DEV * fb
    cos, sin_signed = _rotary_tables(T)
    lg = jnp.repeat(jnp.log(1.0 - 2.0 ** (-5.0 - jnp.arange(HEADS, dtype=F32))), DK)[None, :]
    g_final = final_norm_g.reshape(1, D)

    g1_loc = ffn1_w_gate[0].T[None].astype(BF16)
    u1_loc = ffn1_w_up[0].T[None].astype(BF16)
    d1_loc = ffn1_w_down.astype(BF16)
    g2_loc = ffn2_w_gate[0].T[None].astype(BF16)
    u2_loc = ffn2_w_up[0].T[None].astype(BF16)
    d2_loc = ffn2_w_down.astype(BF16)
    in_loc = w_in[0].T.astype(BF16)
    out_loc = w_out[0].astype(BF16)

    h1, (g1,) = _rms_fwd("ffn1_rms", x0, ffn1_norm_g, comm=_AllGather([g1_loc], ["stack"]))
    g1 = g1.reshape(1, F, D)
    (dsl1, sl1), (u1,) = _mm_nstream("ffn1_gate", h1, [g1], [0], "nt", [], [BF16, BF16], _gate_parts_epilogue, cn=256,
                                     comm=_AllGather([u1_loc], ["stack"]))
    u1 = u1.reshape(1, F, D)
    (dsu1, act1), (d1,) = _mm_nstream("ffn1_up", h1, [u1], [0], "nt", [dsl1, sl1], [BF16, BF16],
                                      _up_act_epilogue, cn=256, comm=_AllGather([d1_loc], ["stack"]))
    d1 = d1.reshape(1, F, D)
    f32_tile, bf16_tile, f32_vec = (F32, "tile"), (BF16, "tile"), (F32, "vec")
    (x1, h2), (in_all, a_all) = _mm_mstream(
        "ffn1_down", [act1], [d1], [0], "nn", [(x0, "tile"), (mix_norm_g, "vec")], [f32_tile, bf16_tile],
        _residual_rms_epilogue(0.5), comm=_AllGather([in_loc, gla_w_a2[0]], ["plain", "plain"]))
    w_in_t = jnp.pad(in_all.reshape(1, N_DEV * ib, D), ((0, 0), (0, PROJ_W - N_DEV * ib), (0, 0)))
    wa2 = jnp.transpose(a_all, (1, 0, 2)).reshape(GATE_RANK, N_DEV * ab)
    wa2p = jnp.pad(wa2, ((0, GL_W - GATE_RANK), (0, 0)))

    (proj,), (g2, out_all) = _mm_nstream("mix_proj", h2, [w_in_t], [0], "nt", [], [BF16], _identity_epilogue, cn=640,
                                         comm=_AllGather([g2_loc, out_loc], ["stack", "plain"]))
    w_out_full = out_all.reshape(D, D)
    (o, ymix, states, x2, h3), (u2,) = _attn_fwd(proj, cos, sin_signed, lg, wa2p, gla_b_a, ret_norm_g, gla_norm_g,
                                                 x1, w_out_full, ffn2_norm_g, comm=_AllGather([u2_loc], ["stack"]))
    g2, u2 = g2.reshape(1, F, D), u2.reshape(1, F, D)

    (dsu2, sl2, act2), (d2,) = _mm_nstream(
        "ffn2_up", h3, [g2, u2], [0, 0], "nt", [], [BF16, BF16, BF16], _silu_mul_epilogue, cn=256,
        comm=_AllGather([d2_loc], ["stack"]))
    d2 = d2.reshape(1, F, D)
    (dx3, dy3b, d_final, loss), _ = _mm_mstream(
        "ffn2_down", [act2], [d2], [0], "nn", [(x2, "tile"), (g_final, "vec"), (target, "tile")],
        [f32_tile, bf16_tile, f32_vec, f32_vec], _final_loss_epilogue(0.5, 0.5))

    sent = {}

    def send(nm, grad):
        sent[nm] = _send_to_owners("send_" + nm, grad)
        return sent[nm][4]

    dwd2, _ = _mm_tn("ffn2b_dwd", act2, dy3b, F // 2, D, BF16)
    tok = send("wd2", dwd2.reshape(4, 2, fb, D))
    (dgate2, dup2), _ = _mm_nstream("ffn2b_dact", dy3b, [d2], [0], "nt", [dsu2, sl2], [BF16, BF16],
                                    _dact_epilogue, cn=256, after=tok)
    dwg2, _ = _mm_tn("ffn2b_dwg", dgate2, h3, F // 2, D, BF16)
    tok = send("wg2", dwg2.reshape(4, 2, fb, D))
    dwu2, _ = _mm_tn("ffn2b_dwu", dup2, h3, F // 2, D, BF16, after=tok)
    tok = send("wu2", dwu2.reshape(4, 2, fb, D))
    rms_outs = [f32_tile, bf16_tile, f32_vec]
    (dx2, dx2b, d_g2), _ = _mm_mstream(
        "ffn2b_dh", [dgate2, dup2], [g2, u2], [0, 0], "nn", [(x2, "tile"), (ffn2_norm_g, "vec"), (dx3, "tile")],
        rms_outs, _rms_bwd_epilogue(1.0), after=tok)

    dwout, _ = _mm_tn("mixb_dwout", ymix, dx2b, D, D, BF16)
    tok = send("wout", dwout.reshape(4, 2, D // N_DEV, D))
    (dproj, d_ret, d_gla, d_ba, d_wa2p), _ = _attn_bwd(
        proj, cos, sin_signed, lg, wa2p, gla_b_a, ret_norm_g, gla_norm_g, o, dx2b, w_out_full, states, after=tok)
    dwin_t, _ = _mm_tn("mixb_dwin", dproj, h2, 640, D, BF16, tk=2048)
    tok = send("win", dwin_t[:N_DEV * ib].reshape(4, 2, ib, D))
    (dx1, dy1b, d_gmix), _ = _mm_mstream(
        "mixb_dh", [dproj], [w_in_t], [0], "nn", [(x1, "tile"), (mix_norm_g, "vec"), (dx2, "tile")],
        rms_outs, _rms_bwd_epilogue(0.5), after=tok)

    dwd1, _ = _mm_tn("ffn1b_dwd", act1, dy1b, F // 2, D, BF16)
    tok = send("wd1", dwd1.reshape(4, 2, fb, D))
    (dgate1, dup1), _ = _mm_nstream("ffn1b_dact", dy1b, [d1], [0], "nt", [dsu1, sl1], [BF16, BF16],
                                    _dact_epilogue, cn=256, after=tok)
    dwg1, _ = _mm_tn("ffn1b_dwg", dgate1, h1, F // 2, D, BF16)
    tok = send("wg1", dwg1.reshape(4, 2, fb, D))
    dwu1, _ = _mm_tn("ffn1b_dwu", dup1, h1, F // 2, D, BF16, after=tok)
    tok = send("wu1", dwu1.reshape(4, 2, fb, D))
    (dx0, _, d_g1), _ = _mm_mstream(
        "ffn1b_dh", [dgate1, dup1], [g1, u1], [0, 0], "nn", [(x0, "tile"), (ffn1_norm_g, "vec"), (dx1, "tile")],
        rms_outs, _rms_bwd_epilogue(1.0), after=tok)

    small = dict(ffn1=d_g1, mix=d_gmix, ffn2=d_g2, final=d_final, ret=d_ret, gla=d_gla, b_a=d_ba)
    flat = jnp.concatenate([small[k].reshape(-1) for k in SMALL_ORDER]
                           + [d_wa2p[:GATE_RANK].reshape(-1), loss[0, :128]])
    rows = -(-flat.shape[0] // 128)
    rows = -(-rows // 8) * 8
    packed = jnp.pad(flat, (0, rows * 128 - flat.shape[0])).reshape(rows, 128)

    transposed = ("ffn1_w_gate", "ffn1_w_up", "ffn2_w_gate", "ffn2_w_up", "w_in")

    def to_2d(nm, a):
        if nm in transposed:
            return a[0].T
        return a.reshape((1, a.shape[0]) if a.ndim == 1 else a.shape[-2:])

    def from_2d(nm, a):
        return a.T[None] if nm in transposed else a.reshape(params[nm][0].shape)

    def arrived(nm):
        grad, landed = _await_owners("await_" + nm, sent[nm], dx0)
        return grad, [landed]

    sums_a, (gathered,) = _owner_sums(
        "sum_a", [arrived("wg2"), arrived("wu2"), arrived("wd2"), arrived("win"), arrived("wout")], owner,
        comm=_AllGather([packed], ["plain"]))
    sums_b, _ = _owner_sums("sum_b", [arrived("wg1"), arrived("wu1"), arrived("wd1")], owner)
    big_grads = {"ffn2_w_gate": sums_a[0], "ffn2_w_up": sums_a[1], "ffn2_w_down": sums_a[2], "w_out": sums_a[4],
                 "ffn1_w_gate": sums_b[0], "ffn1_w_up": sums_b[1], "ffn1_w_down": sums_b[2]}
    params = dict(
        ffn2_w_gate=(ffn2_w_gate, m_ffn2_w_gate, v_ffn2_w_gate), ffn2_w_up=(ffn2_w_up, m_ffn2_w_up, v_ffn2_w_up),
        ffn2_w_down=(ffn2_w_down, m_ffn2_w_down, v_ffn2_w_down), w_in=(w_in, m_w_in, v_w_in),
        w_out=(w_out, m_w_out, v_w_out), ffn1_w_gate=(ffn1_w_gate, m_ffn1_w_gate, v_ffn1_w_gate),
        ffn1_w_up=(ffn1_w_up, m_ffn1_w_up, v_ffn1_w_up), ffn1_w_down=(ffn1_w_down, m_ffn1_w_down, v_ffn1_w_down),
        ffn1_norm_g=(ffn1_norm_g, m_ffn1_norm_g, v_ffn1_norm_g), mix_norm_g=(mix_norm_g, m_mix_norm_g, v_mix_norm_g),
        ret_norm_g=(ret_norm_g, m_ret_norm_g, v_ret_norm_g), gla_w_a2=(gla_w_a2, m_gla_w_a2, v_gla_w_a2),
        gla_b_a=(gla_b_a, m_gla_b_a, v_gla_b_a), gla_norm_g=(gla_norm_g, m_gla_norm_g, v_gla_norm_g),
        ffn2_norm_g=(ffn2_norm_g, m_ffn2_norm_g, v_ffn2_norm_g), final_norm_g=(final_norm_g, m_final_norm_g, v_final_norm_g))
    grads, updates = {}, {}

    def run_adam(name, names, grad_2d, n_blocks):
        items = [(grad_2d[nm],) + tuple(to_2d(nm, a) for a in params[nm]) for nm in names]
        res, _ = _adamw_group(name, items, n_blocks)
        for nm, r in zip(names, res):
            grads[nm] = from_2d(nm, grad_2d[nm])
            updates[nm] = tuple(from_2d(nm, a) for a in r)

    run_adam("adamw_big", list(big_grads), big_grads, 4)
    run_adam("adamw_w_in", ["w_in"], {"w_in": sums_a[3]}, 1)

    total = _sum_devices("sum_small", gathered.reshape(N_DEV * rows, 128), rows).reshape(-1)
    sizes = [small[k].size for k in SMALL_ORDER] + [GATE_RANK * QK_W, 128]
    offs = [0]
    for s in sizes:
        offs.append(offs[-1] + s)
    pieces = [total[offs[i]:offs[i + 1]] for i in range(len(sizes))]
    g_small = {k: pieces[i].reshape(small[k].shape) for i, k in enumerate(SMALL_ORDER)}
    g_wa2_full = pieces[len(SMALL_ORDER)].reshape(GATE_RANK, QK_W)
    g_wa2 = lax.dynamic_slice(g_wa2_full, (0, dev * ab), (GATE_RANK, ab))
    loss_total = pieces[len(SMALL_ORDER) + 1][0]

    small_grads = {"ffn1_norm_g": g_small["ffn1"], "mix_norm_g": g_small["mix"], "ret_norm_g": g_small["ret"],
                   "gla_w_a2": g_wa2, "gla_b_a": g_small["b_a"], "gla_norm_g": g_small["gla"],
                   "ffn2_norm_g": g_small["ffn2"], "final_norm_g": g_small["final"]}
    run_adam("adamw_small", list(small_grads), small_grads, 1)

    order = ("ffn1_norm_g", "ffn1_w_gate", "ffn1_w_up", "ffn1_w_down", "mix_norm_g", "w_in", "ret_norm_g", "gla_w_a2",
             "gla_b_a", "gla_norm_g", "w_out", "ffn2_norm_g", "ffn2_w_gate", "ffn2_w_up", "ffn2_w_down", "final_norm_g")
    return (loss_total, dx0[None], *[grads[nm] for nm in order], *[updates[nm][0] for nm in order],
            *[updates[nm][1] for nm in order], *[updates[nm][2] for nm in order])
```

```python
import functools
import math

import jax
import jax.numpy as jnp
from jax import lax
from jax.experimental import pallas as pl
from jax.experimental.pallas import tpu as pltpu

F32 = jnp.float32
BF16 = jnp.bfloat16
MESH = pl.DeviceIdType.MESH
HBM = pl.BlockSpec(memory_space=pltpu.HBM)

N_DEV = 8
RMS_EPS = 1e-6
ROPE_BASE = 10000.0
HEADS = 4
DK = 64
DV = 128
QK_W = HEADS * DK
V_W = HEADS * DV
GATE_RANK = 16
GATE_NORM = 16.0
CHUNK = 64
SUPER = 256
PROJ_W = 3200
C_RQ, C_RK, C_RV, C_RG, C_GQ, C_GK, C_GV, C_GG, C_GL = 0, 256, 512, 1024, 1536, 1792, 2048, 2560, 3072
GL_W = PROJ_W - C_GL
ADAM_LR, ADAM_B1, ADAM_B2, ADAM_EPS, ADAM_WD, ADAM_STEP = 0.001, 0.9, 0.999, 1e-08, 0.01, 10
VMEM_LIMIT_V7X = 52 * 1024 * 1024


def _cparams(**kw):
    return pltpu.CompilerParams(vmem_limit_bytes=VMEM_LIMIT_V7X, **kw)


def _dot(a, b, form, precision=None):
    dims = {"nn": (((1,), (0,)), ((), ())), "nt": (((1,), (1,)), ((), ())), "tn": (((0,), (0,)), ((), ()))}[form]
    return lax.dot_general(a, b, dims, preferred_element_type=F32, precision=precision)


def _sigmoid(x):
    return 1.0 / (1.0 + jnp.exp(-x))


def _coords():
    return lax.axis_index("x"), lax.axis_index("y"), lax.axis_index("c")


class _NoComm:
    inputs, out_shapes, scratch = (), (), ()


class _AllGather:
    def __init__(self, arrays, kinds):
        self.inputs = tuple(arrays)
        self.kinds = tuple(kinds)
        n = len(arrays)
        self.out_shapes = tuple(
            jax.ShapeDtypeStruct((a.shape[0], N_DEV) + a.shape[1:] if k == "stack" else (N_DEV,) + a.shape, a.dtype)
            for a, k in zip(arrays, kinds))
        self.scratch = (pltpu.SemaphoreType.DMA((n, 7)), pltpu.SemaphoreType.DMA((n, 7)),
                        pltpu.SemaphoreType.DMA((n,)))

    def _ctx(self, srcs, outs, sems):
        send_sems, recv_sems, local_sems = sems
        x, y, c = _coords()
        me, sibling = (x, y, c), (x, y, 1 - c)
        chips = [(1 - x, y), (x, 1 - y), (1 - x, 1 - y)]

        def blk(m, dev):
            k = 4 * dev[0] + 2 * dev[1] + dev[2]
            return outs[m].at[:, k] if self.kinds[m] == "stack" else outs[m].at[k]

        def copy(m, s, block, to, src=None):
            return pltpu.make_async_remote_copy(
                src_ref=blk(m, block) if src is None else src, dst_ref=blk(m, block),
                send_sem=send_sems.at[m, s], recv_sem=recv_sems.at[m, s], device_id=to, device_id_type=MESH)

        def mine(m):
            return pltpu.make_async_copy(srcs[m], blk(m, me), local_sems.at[m])

        def first(m):
            return [copy(m, 0, me, sibling, src=srcs[m])] + [
                copy(m, 1 + j, me, (*chip, c), src=srcs[m]) for j, chip in enumerate(chips)]

        return me, sibling, chips, c, copy, mine, first

    def start(self, srcs, outs, sems):
        me, sibling, chips, c, copy, mine, first = self._ctx(srcs, outs, sems)
        for m in range(len(srcs)):
            mine(m).start()
            for cp in first(m):
                cp.start()

    def mid(self, srcs, outs, sems):
        me, sibling, chips, c, copy, mine, first = self._ctx(srcs, outs, sems)
        for j, chip in enumerate(chips):
            for m in range(len(srcs)):
                copy(m, 1 + j, (*chip, c), me).wait_recv()
                copy(m, 4 + j, (*chip, c), sibling).start()

    def finish(self, srcs, outs, sems):
        me, sibling, chips, c, copy, mine, first = self._ctx(srcs, outs, sems)
        for m in range(len(srcs)):
            copy(m, 0, sibling, me).wait_recv()
            for j, chip in enumerate(chips):
                copy(m, 4 + j, (*chip, 1 - c), me).wait_recv()
            for cp in first(m):
                cp.wait_send()
            for j, chip in enumerate(chips):
                copy(m, 4 + j, (*chip, c), sibling).wait_send()
            mine(m).wait()


RELATIONS = ((0, 0, 1), (1, 0, 0), (0, 1, 0), (1, 1, 0), (1, 0, 1), (0, 1, 1), (1, 1, 1))
SEM = pl.BlockSpec(memory_space=pltpu.SEMAPHORE)
SPLIT_PARAMS = dict(has_side_effects=pltpu.SideEffectType.DATAFLOW_SIDE_EFFECTING)


def _owner_copies(grad_ref, land_ref, send_sems, recv_sems):
    x, y, c = _coords()
    copies = []
    for s, (fx, fy, fc) in enumerate(RELATIONS):
        px = 1 - x if fx else x
        py = 1 - y if fy else y
        pc = 1 - c if fc else c
        copies.append(pltpu.make_async_remote_copy(
            src_ref=grad_ref.at[2 * px + py, pc], dst_ref=land_ref.at[s], send_sem=send_sems.at[s],
            recv_sem=recv_sems.at[s], device_id=(px, py, pc), device_id_type=MESH))
    return copies


def _send_to_owners(name, grad):
    n = len(RELATIONS)
    land_shape = (n,) + grad.shape[2:]

    def body(g_ref, land_ref, send_sems, recv_sems, g_thru, land_thru, token):
        for cp in _owner_copies(g_ref, land_ref, send_sems, recv_sems):
            cp.start()
        token[...] = jnp.zeros_like(token)

    return pl.pallas_call(
        body, name=name,
        out_shape=(pltpu.SemaphoreType.DMA((n,)), pltpu.SemaphoreType.DMA((n,)), pltpu.HBM(grad.shape, grad.dtype),
                   pltpu.HBM(land_shape, grad.dtype), jax.ShapeDtypeStruct((8, 128), F32)),
        in_specs=(HBM, HBM), out_specs=(SEM, SEM, HBM, HBM, pl.BlockSpec(memory_space=pltpu.VMEM)),
        input_output_aliases={0: 2, 1: 3}, compiler_params=pltpu.CompilerParams(**SPLIT_PARAMS),
    )(pltpu.with_memory_space_constraint(grad, pltpu.HBM),
      pltpu.with_memory_space_constraint(lax.empty(land_shape, grad.dtype), pltpu.HBM))


def _await_owners(name, started, after):
    send_sems, recv_sems, g_thru, land_thru, _ = started

    def body(g_ref, land_ref, send_sems, recv_sems, after_ref, g_out, land_out):
        for cp in _owner_copies(g_ref, land_ref, send_sems, recv_sems):
            cp.wait_send()
            cp.wait_recv()

    return pl.pallas_call(
        body, name=name, out_shape=(pltpu.HBM(g_thru.shape, g_thru.dtype), pltpu.HBM(land_thru.shape, land_thru.dtype)),
        in_specs=(HBM, HBM, SEM, SEM, pl.BlockSpec(memory_space=pl.ANY)), out_specs=(HBM, HBM),
        input_output_aliases={0: 0, 1: 1}, compiler_params=pltpu.CompilerParams(**SPLIT_PARAMS),
    )(g_thru, land_thru, send_sems, recv_sems, after)


def _call(name, main, grid, in_specs, out_specs, out_shape, args, scratch=(), comm=None, prefetch=None, after=None):
    comm = comm or _NoComm()
    n_main = len(in_specs)
    if after is not None:
        in_specs = list(in_specs) + [pl.BlockSpec(after.shape, lambda *_: (0,) * after.ndim)]
        args = tuple(args) + (after,)
    counts = [len(in_specs), len(comm.inputs), len(out_shape), len(comm.out_shapes), len(scratch), len(comm.scratch)]
    n_steps = math.prod(grid)
    hosted = bool(comm.inputs)

    def body(*refs):
        if prefetch is not None:
            refs = refs[1:]
        parts, at = [], 0
        for n in counts:
            parts.append(refs[at:at + n])
            at += n
        ins, c_in, outs, c_out, scr, c_scr = parts
        ins = ins[:n_main]
        step = pl.program_id(0)
        for d in range(1, len(grid)):
            step = step * grid[d] + pl.program_id(d)
        if hosted:
            @pl.when(step == 0)
            def _():
                comm.start(c_in, c_out, c_scr)
        main(ins, outs, scr)
        if hosted:
            @pl.when(step == max(n_steps - 2, 0))
            def _():
                comm.mid(c_in, c_out, c_scr)

            @pl.when(step == n_steps - 1)
            def _():
                comm.finish(c_in, c_out, c_scr)

    all_in = list(in_specs) + [HBM] * counts[1]
    all_out = list(out_specs) + [HBM] * counts[3]
    all_scratch = list(scratch) + list(comm.scratch)
    shapes = list(out_shape) + list(comm.out_shapes)
    if prefetch is None:
        res = pl.pallas_call(body, name=name, grid=grid, in_specs=all_in, out_specs=all_out, out_shape=shapes,
                             scratch_shapes=all_scratch, compiler_params=_cparams())(*args, *comm.inputs)
    else:
        res = pl.pallas_call(
            body, name=name, out_shape=shapes,
            grid_spec=pltpu.PrefetchScalarGridSpec(num_scalar_prefetch=1, grid=grid, in_specs=all_in,
                                                   out_specs=all_out, scratch_shapes=all_scratch),
            compiler_params=_cparams())(prefetch, *args, *comm.inputs)
    return res[:counts[2]], res[counts[2]:]


def _rms_fwd(name, x, g, comm=None):
    T, D = x.shape
    tm = min(T, 512)

    def main(ins, outs, scr):
        x_ref, g_ref = ins
        xv = x_ref[...]
        r = lax.rsqrt(jnp.mean(xv * xv, axis=-1, keepdims=True) + RMS_EPS)
        outs[0][...] = (xv * r * g_ref[...]).astype(outs[0].dtype)

    tile = pl.BlockSpec((tm, D), lambda i: (i, 0))
    (h,), extra = _call(name, main, (T // tm,), [tile, pl.BlockSpec((1, D), lambda i: (0, 0))], [tile],
                        [jax.ShapeDtypeStruct((T, D), BF16)], (x, g), comm=comm)
    return h, extra


def _final_loss_epilogue(scale, out_scale):
    def ep(acc, ex, outs):
        res_ref, g_ref, t_ref = ex
        dx_ref, dxb_ref, dg_ref, loss_ref = outs
        n = acc.shape[-1]
        xv = res_ref[...] + scale * acc
        r = lax.rsqrt(jnp.mean(xv * xv, axis=-1, keepdims=True) + RMS_EPS)
        xhat = xv * r
        err = xhat * g_ref[...] - t_ref[...]

        @pl.when(pl.program_id(0) == 0)
        def _():
            dg_ref[...] = jnp.zeros_like(dg_ref)
            loss_ref[...] = jnp.zeros_like(loss_ref)

        loss_ref[...] += jnp.broadcast_to(jnp.sum(err * err) * (0.5 / n), loss_ref.shape)
        dy = err * (1.0 / n)
        dg_ref[...] += jnp.sum(dy * xhat, axis=0, keepdims=True)
        dxhat = dy * g_ref[...]
        dx = r * (dxhat - xhat * jnp.mean(dxhat * xhat, axis=-1, keepdims=True))
        dx_ref[...] = dx
        dxb_ref[...] = (out_scale * dx).astype(dxb_ref.dtype)
    return ep


def _mm_nstream(name, a, ws, w_sel, w_form, comps, out_dtypes, epilogue, cn, rows=1024, comm=None, after=None):
    T, K = a.shape
    N = ws[0].shape[1]
    rows = min(rows, T)
    assert N % cn == 0 and T % rows == 0
    n_w, n_c = len(ws), len(comps)

    def main(ins, outs, scr):
        a_ref = ins[0]
        w_refs = ins[1:1 + n_w]
        c_refs = ins[1 + n_w:1 + n_w + n_c]

        for r in range(T // rows):
            sl = slice(r * rows, (r + 1) * rows)
            a_blk = a_ref[sl, :]
            dots = [_dot(a_blk, w_ref[...], w_form) for w_ref in w_refs]
            res = epilogue(dots, [c_ref[sl, :] for c_ref in c_refs])
            for o_ref, o in zip(outs, res):
                o_ref[sl, :] = o.astype(o_ref.dtype)

    if w_form == "nt":
        w_specs = [pl.BlockSpec((None, cn, K), functools.partial(lambda j, s: (s, j, 0), s=s)) for s in w_sel]
    else:
        w_specs = [pl.BlockSpec((K, cn), lambda j: (0, j)) for _ in ws]
    chunk = pl.BlockSpec((T, cn), lambda j: (0, j))
    return _call(name, main, (N // cn,), [pl.BlockSpec((T, K), lambda j: (0, 0))] + w_specs + [chunk] * n_c,
                 [chunk] * len(out_dtypes), [jax.ShapeDtypeStruct((T, N), dt) for dt in out_dtypes],
                 (a, *ws, *comps), comm=comm, after=after)


def _mm_mstream(name, as_, ws, w_sel, w_form, extras, outs_desc, epilogue, tm=512, comm=None, after=None):
    T = as_[0].shape[0]
    tm = min(tm, T)
    n_a = len(as_)
    w_shapes = [w.shape[-2:] for w in ws]
    N = w_shapes[0][1] if w_form == "nn" else w_shapes[0][0]

    def main(ins, outs, scr):
        a_refs = ins[:n_a]
        w_refs = ins[n_a:2 * n_a]
        acc = None
        for a_ref, w_ref in zip(a_refs, w_refs):
            d = _dot(a_ref[...], w_ref[...], w_form)
            acc = d if acc is None else acc + d
        epilogue(acc, ins[2 * n_a:], outs)

    kind_spec = {"tile": pl.BlockSpec((tm, N), lambda i: (i, 0)), "vec": pl.BlockSpec((1, N), lambda i: (0, 0))}
    kind_shape = {"tile": (T, N), "vec": (1, N)}
    a_specs = [pl.BlockSpec((tm, a.shape[1]), lambda i: (i, 0)) for a in as_]
    w_specs = []
    for w, s in zip(ws, w_sel):
        if w.ndim == 3:
            w_specs.append(pl.BlockSpec((None,) + tuple(w.shape[1:]), functools.partial(lambda i, s: (s, 0, 0), s=s),
                                        pipeline_mode=pl.Buffered(1)))
        else:
            w_specs.append(pl.BlockSpec(tuple(w.shape), lambda i: (0, 0), pipeline_mode=pl.Buffered(1)))
    args = list(as_) + list(ws) + [e for e, _ in extras]
    return _call(name, main, (T // tm,), a_specs + w_specs + [kind_spec[k] for _, k in extras],
                 [kind_spec[k] for _, k in outs_desc],
                 [jax.ShapeDtypeStruct(kind_shape[k], dt) for dt, k in outs_desc], args, comm=comm, after=after)


def _residual_rms_epilogue(scale):
    def ep(acc, ex, outs):
        xv = ex[0][...] + scale * acc
        outs[0][...] = xv
        r = lax.rsqrt(jnp.mean(xv * xv, axis=-1, keepdims=True) + RMS_EPS)
        outs[1][...] = (xv * r * ex[1][...]).astype(outs[1].dtype)
    return ep


def _rms_bwd_epilogue(out_scale):
    def ep(acc, ex, outs):
        x_ref, g_ref, dres_ref = ex
        dx_ref, dxb_ref, dg_ref = outs
        xv = x_ref[...]
        r = lax.rsqrt(jnp.mean(xv * xv, axis=-1, keepdims=True) + RMS_EPS)
        xhat = xv * r

        @pl.when(pl.program_id(0) == 0)
        def _():
            dg_ref[...] = jnp.zeros_like(dg_ref)

        dg_ref[...] += jnp.sum(acc * xhat, axis=0, keepdims=True)
        dxhat = acc * g_ref[...]
        dx = r * (dxhat - xhat * jnp.mean(dxhat * xhat, axis=-1, keepdims=True)) + dres_ref[...]
        dx_ref[...] = dx
        dxb_ref[...] = (out_scale * dx).astype(dxb_ref.dtype)
    return ep


def _mm_tn(name, a, b, tmo, tno, out_dtype, tk=1024, comm=None, after=None):
    T, Ma = a.shape
    Nb = b.shape[1]
    tk = min(tk, T)
    nk = T // tk

    def main(ins, outs, scr):
        a_ref, b_ref = ins
        (acc_ref,) = scr
        k = pl.program_id(2)

        @pl.when(k == 0)
        def _():
            acc_ref[...] = jnp.zeros_like(acc_ref)

        acc_ref[...] += _dot(a_ref[...], b_ref[...], "tn")

        @pl.when(k == nk - 1)
        def _():
            outs[0][...] = acc_ref[...].astype(outs[0].dtype)

    (out,), extra = _call(
        name, main, (Ma // tmo, Nb // tno, nk),
        [pl.BlockSpec((tk, tmo), lambda i, j, k: (k, i)), pl.BlockSpec((tk, tno), lambda i, j, k: (k, j))],
        [pl.BlockSpec((tmo, tno), lambda i, j, k: (i, j))], [jax.ShapeDtypeStruct((Ma, Nb), out_dtype)],
        (a, b), scratch=[pltpu.VMEM((tmo, tno), F32)], comm=comm, after=after)
    return out, extra


def _swiglu_parts(g, u):
    s = _sigmoid(g)
    silu = g * s
    return [u * (s + silu * (1.0 - s)), silu, silu * u]


def _silu_mul_epilogue(dots, comps):
    g, u = dots
    return _swiglu_parts(g, u)


def _gate_parts_epilogue(dots, comps):
    (g,) = dots
    s = _sigmoid(g)
    silu = g * s
    return [s + silu * (1.0 - s), silu]


def _up_act_epilogue(dots, comps):
    (u,) = dots
    return [u * comps[0].astype(F32), u * comps[1].astype(F32)]


def _dact_epilogue(dots, comps):
    dact = dots[0].astype(BF16)
    return [dact * comps[0], dact * comps[1]]


def _identity_epilogue(dots, comps):
    return list(dots)


def _swap_halves(x):
    lane = lax.broadcasted_iota(jnp.int32, x.shape, 1)
    first = (lane % DK) < (DK // 2)
    return jnp.where(first, pltpu.roll(x, 128 - DK // 2, 1), pltpu.roll(x, DK // 2, 1))


def _rotary(t, cos, sin_signed):
    halves = []
    for p in range(QK_W // 128):
        th = t[:, 128 * p:128 * (p + 1)]
        halves.append(th * cos + _swap_halves(th) * sin_signed)
    return jnp.concatenate(halves, axis=1)


def _rotary_transposed(d, cos, sin_signed):
    halves = []
    for p in range(QK_W // 128):
        dh = d[:, 128 * p:128 * (p + 1)]
        halves.append(dh * cos + _swap_halves(dh * sin_signed))
    return jnp.concatenate(halves, axis=1)


def _log_sigmoid(x):
    return jnp.minimum(x, 0.0) - jnp.log(1.0 + jnp.exp(-jnp.abs(x)))


def _attn_masks():
    row = lax.broadcasted_iota(jnp.int32, (SUPER, SUPER), 0)
    col = lax.broadcasted_iota(jnp.int32, (SUPER, SUPER), 1)
    same = (row // CHUNK) == (col // CHUNK)
    return row, col, same


def _group_inputs(grp, pr, cos, sin_signed, lg, wa2, ba):
    seg = lambda lo, width: pr[:, lo:lo + width].astype(F32)
    if grp == 0:
        q = _rotary(seg(C_RQ, QK_W), cos, sin_signed)
        k = _rotary(seg(C_RK, QK_W), cos, sin_signed) * (DK ** -0.5)
        v = pr[:, C_RV:C_RV + V_W]
        gate = seg(C_RG, V_W)
        pos = lax.broadcasted_iota(jnp.int32, (SUPER, QK_W), 0).astype(F32) + 1.0
        return q, k, v, gate, pos * lg, None, None
    q = seg(C_GQ, QK_W) * (DK ** -0.5)
    k = seg(C_GK, QK_W)
    v = pr[:, C_GV:C_GV + V_W]
    gate = seg(C_GG, V_W)
    glow = pr[:, C_GL:C_GL + GL_W]
    logit = _dot(glow.astype(BF16), wa2.astype(BF16), "nn") + ba
    la = _log_sigmoid(logit) * (1.0 / GATE_NORM)
    row, col, _ = _attn_masks()
    lower = (col <= row).astype(F32)
    b_cum = _dot(lower, la, "nn", precision=lax.Precision.HIGHEST)
    return q, k, v, gate, b_cum, glow, logit


def _decay_factors(q, k, b_cum):
    c = b_cum[SUPER // 2 - 1:SUPER // 2, :]
    bl = b_cum[SUPER - 1:SUPER, :]
    e1 = jnp.exp(b_cum - c)
    e2 = jnp.exp(c - b_cum)
    e_b = jnp.exp(b_cum)
    e_l = jnp.exp(bl - b_cum)
    return dict(e1=e1, e2=e2, eb=e_b, el=e_l, ebl=jnp.exp(bl),
                qp=q * e1, qm=q * e2, kp=k * e1, km=k * e2, qs=q * e_b, kl=k * e_l)


def _state_block_mask():
    r = lax.broadcasted_iota(jnp.int32, (V_W, QK_W), 0)
    c = lax.broadcasted_iota(jnp.int32, (V_W, QK_W), 1)
    return (r // DV) == (c // DK)


def _attn_fwd(proj, cos, sin_signed, lg, wa2p, ba, gn_ret, gn_gla, x_res, w_out, g_next, comm=None):
    T = proj.shape[0]
    n_s = T // SUPER
    D = x_res.shape[1]

    def main(ins, outs, scr):
        pr_ref, cos_ref, sin_ref, lg_ref, wa2_ref, ba_ref, gr_ref, gg_ref, xres_ref, wout_ref, gnext_ref = ins
        o_ref, y_ref, st_ref, x_ref, h_ref = outs
        (s_ref,) = scr
        i = pl.program_id(0)

        @pl.when(i == 0)
        def _():
            s_ref[...] = jnp.zeros_like(s_ref)

        pr = pr_ref
        row, col, same = _attn_masks()
        m1 = col <= row
        m2 = jnp.logical_and(col > row, same)
        lane = lax.broadcasted_iota(jnp.int32, (1, QK_W), 1)
        blockmask = _state_block_mask()
        for grp in range(2):
            q, k, v, gate, b_cum, _, _ = _group_inputs(grp, pr, cos_ref[...], sin_ref[...], lg_ref[...],
                                                      wa2_ref[...], ba_ref[...])
            f = _decay_factors(q, k, b_cum)
            gn = gr_ref[...] if grp == 0 else gg_ref[...]
            s_prev = s_ref[grp]
            st_ref[0, grp] = s_prev
            o_inter = _dot(f["qs"].astype(BF16), s_prev.astype(BF16), "nt")
            kmb = f["km"].astype(BF16)
            kpb = f["kp"].astype(BF16)
            vb = v.astype(BF16)
            for h in range(HEADS):
                hm = (lane // DK) == h
                a1 = _dot(jnp.where(hm, f["qp"], 0.0).astype(BF16), kmb, "nt")
                a2 = _dot(jnp.where(hm, f["qm"], 0.0).astype(BF16), kpb, "nt")
                a = jnp.where(m1, a1, jnp.where(m2, a2, 0.0))
                lo = grp * V_W + h * DV
                o_h = _dot(a.astype(BF16), vb[:, h * DV:(h + 1) * DV], "nn") + o_inter[:, h * DV:(h + 1) * DV]
                o_ref[:, lo:lo + DV] = o_h
                r = lax.rsqrt(jnp.mean(o_h * o_h, axis=-1, keepdims=True) + RMS_EPS)
                gte = gate[:, h * DV:(h + 1) * DV]
                y = o_h * r * gn[:, h * DV:(h + 1) * DV] * (gte * _sigmoid(gte))
                y_ref[:, lo:lo + DV] = y.astype(y_ref.dtype)
            upd = _dot(vb, f["kl"].astype(BF16), "tn")
            s_ref[grp] = s_prev * f["ebl"] + jnp.where(blockmask, upd, 0.0)
        xv = xres_ref[...] + _dot(y_ref[...], wout_ref[...], "nn")
        x_ref[...] = xv
        r = lax.rsqrt(jnp.mean(xv * xv, axis=-1, keepdims=True) + RMS_EPS)
        h_ref[...] = (xv * r * gnext_ref[...]).astype(h_ref.dtype)

    const = lambda shape: pl.BlockSpec(shape, lambda i: tuple(0 for _ in shape))
    rows = lambda w: pl.BlockSpec((SUPER, w), lambda i: (i, 0))
    return _call(
        "attn_fwd", main, (n_s,),
        [rows(PROJ_W), rows(128), rows(128),
         const((1, QK_W)), const((GL_W, QK_W)), const((1, QK_W)), const((1, V_W)), const((1, V_W)),
         rows(D), const((2 * V_W, D)), const((1, D))],
        [rows(2 * V_W), rows(2 * V_W), pl.BlockSpec((1, 2, V_W, QK_W), lambda i: (i, 0, 0, 0)), rows(D), rows(D)],
        [jax.ShapeDtypeStruct((T, 2 * V_W), F32), jax.ShapeDtypeStruct((T, 2 * V_W), BF16),
         jax.ShapeDtypeStruct((n_s, 2, V_W, QK_W), F32), jax.ShapeDtypeStruct((T, D), F32),
         jax.ShapeDtypeStruct((T, D), BF16)],
        (proj, cos, sin_signed, lg, wa2p, ba, gn_ret, gn_gla, x_res, w_out, g_next),
        scratch=[pltpu.VMEM((2, V_W, QK_W), F32)], comm=comm)


def _attn_bwd(proj, cos, sin_signed, lg, wa2p, ba, gn_ret, gn_gla, o, dx, w_out, states, comm=None, after=None):
    T = proj.shape[0]
    n_s = T // SUPER
    D = dx.shape[1]

    def main(ins, outs, scr):
        pr_ref, cos_ref, sin_ref, lg_ref, wa2_ref, ba_ref, gr_ref, gg_ref, o_ref, dx_ref, wout_ref, st_ref = ins
        dp_ref, dgr_ref, dgg_ref, dba_ref, dwa_ref = outs
        (ds_ref, dy_ref) = scr
        i = pl.program_id(0)
        dy_ref[...] = _dot(dx_ref[...], wout_ref[...], "nt")

        @pl.when(i == 0)
        def _():
            ds_ref[...] = jnp.zeros_like(ds_ref)
            dgr_ref[...] = jnp.zeros_like(dgr_ref)
            dgg_ref[...] = jnp.zeros_like(dgg_ref)
            dba_ref[...] = jnp.zeros_like(dba_ref)
            dwa_ref[...] = jnp.zeros_like(dwa_ref)

        pr = pr_ref
        cos = cos_ref[...]
        sin_signed = sin_ref[...]
        row, col, same = _attn_masks()
        m1 = col <= row
        m2 = jnp.logical_and(col > row, same)
        m1t = row <= col
        m2t = jnp.logical_and(row > col, same)
        lane = lax.broadcasted_iota(jnp.int32, (1, QK_W), 1)
        blockmask = _state_block_mask()
        for grp in range(2):
            q, k, v, gate, b_cum, glow, logit = _group_inputs(grp, pr, cos, sin_signed, lg_ref[...],
                                                              wa2_ref[...], ba_ref[...])
            f = _decay_factors(q, k, b_cum)
            gn = gr_ref[...] if grp == 0 else gg_ref[...]
            dgn_ref = dgr_ref if grp == 0 else dgg_ref
            do_parts, dgate_parts, dgn_parts = [], [], []
            for h in range(HEADS):
                lo = grp * V_W + h * DV
                o_h = o_ref[:, lo:lo + DV]
                r = lax.rsqrt(jnp.mean(o_h * o_h, axis=-1, keepdims=True) + RMS_EPS)
                n = o_h * r
                gte = gate[:, h * DV:(h + 1) * DV]
                sg = _sigmoid(gte)
                dy_h = dy_ref[:, lo:lo + DV]
                gn_h = gn[:, h * DV:(h + 1) * DV]
                dgate_parts.append(dy_h * n * gn_h * (sg * (1.0 + gte * (1.0 - sg))))
                dz = dy_h * (gte * sg)
                dgn_parts.append(jnp.sum(dz * n, axis=0, keepdims=True))
                dn = dz * gn_h
                do_parts.append(r * (dn - n * jnp.mean(dn * n, axis=-1, keepdims=True)))
            dgn_ref[...] += jnp.concatenate(dgn_parts, axis=1)
            dgate = jnp.concatenate(dgate_parts, axis=1)
            do = jnp.concatenate(do_parts, axis=1)
            dob = do.astype(BF16)
            vb = v.astype(BF16)
            s_prev = st_ref[0, grp]
            ds_new = ds_ref[grp]
            dsb = ds_new.astype(BF16)
            qpb, qmb = f["qp"].astype(BF16), f["qm"].astype(BF16)
            kpb, kmb = f["kp"].astype(BF16), f["km"].astype(BF16)
            dqp = jnp.zeros((SUPER, QK_W), F32)
            dqm = jnp.zeros((SUPER, QK_W), F32)
            dkp = jnp.zeros((SUPER, QK_W), F32)
            dkm = jnp.zeros((SUPER, QK_W), F32)
            dv_parts = []
            for h in range(HEADS):
                hm = (lane // DK) == h
                qp_h = jnp.where(hm, f["qp"], 0.0).astype(BF16)
                qm_h = jnp.where(hm, f["qm"], 0.0).astype(BF16)
                kp_h = jnp.where(hm, f["kp"], 0.0).astype(BF16)
                km_h = jnp.where(hm, f["km"], 0.0).astype(BF16)
                at = jnp.where(m1t, _dot(km_h, qpb, "nt"), jnp.where(m2t, _dot(kp_h, qmb, "nt"), 0.0))
                do_h = dob[:, h * DV:(h + 1) * DV]
                v_h = vb[:, h * DV:(h + 1) * DV]
                dv_parts.append(_dot(at.astype(BF16), do_h, "nn"))
                da = _dot(do_h, v_h, "nt")
                dat = _dot(v_h, do_h, "nt")
                da1 = jnp.where(m1, da, 0.0).astype(BF16)
                da2 = jnp.where(m2, da, 0.0).astype(BF16)
                da1t = jnp.where(m1t, dat, 0.0).astype(BF16)
                da2t = jnp.where(m2t, dat, 0.0).astype(BF16)
                dqp = dqp + _dot(da1, km_h, "nn")
                dqm = dqm + _dot(da2, kp_h, "nn")
                dkm = dkm + _dot(da1t, qp_h, "nn")
                dkp = dkp + _dot(da2t, qm_h, "nn")
            klb = f["kl"].astype(BF16)
            qsb = f["qs"].astype(BF16)
            dqs = _dot(dob, s_prev.astype(BF16), "nn")
            dkl = _dot(vb, dsb, "nn")
            dv = jnp.concatenate(dv_parts, axis=1) + _dot(klb, dsb, "nt")
            ds_ref[grp] = ds_new * f["ebl"] + jnp.where(blockmask, _dot(dob, qsb, "tn"), 0.0)
            dq = dqp * f["e1"] + dqm * f["e2"] + dqs * f["eb"]
            dk = dkm * f["e2"] + dkp * f["e1"] + dkl * f["el"]
            if grp == 0:
                dq = _rotary_transposed(dq, cos, sin_signed)
                dk = _rotary_transposed(dk * (DK ** -0.5), cos, sin_signed)
                dp_ref[:, C_RQ:C_RQ + QK_W] = dq.astype(dp_ref.dtype)
                dp_ref[:, C_RK:C_RK + QK_W] = dk.astype(dp_ref.dtype)
                dp_ref[:, C_RV:C_RV + V_W] = dv.astype(dp_ref.dtype)
                dp_ref[:, C_RG:C_RG + V_W] = dgate.astype(dp_ref.dtype)
            else:
                dkl_kl = dkl * klb.astype(F32)
                db = (dqp * qpb.astype(F32) - dkm * kmb.astype(F32) - dqm * qmb.astype(F32)
                      + dkp * kpb.astype(F32) + dqs * qsb.astype(F32) - dkl_kl)
                last = (jnp.sum(dkl_kl, axis=0, keepdims=True)
                        + f["ebl"] * jnp.sum(s_prev * ds_new, axis=0, keepdims=True))
                rowq = lax.broadcasted_iota(jnp.int32, (SUPER, QK_W), 0)
                db = db + jnp.where(rowq == SUPER - 1, last, 0.0)
                upper = (col >= row).astype(F32)
                dla = _dot(upper, db, "nn", precision=lax.Precision.HIGHEST)
                dlogit = dla * (1.0 / GATE_NORM) * (1.0 - _sigmoid(logit))
                dlb = dlogit.astype(BF16)
                dglow = _dot(dlb, wa2_ref[...].astype(BF16), "nt")
                dwa_ref[...] += _dot(glow.astype(BF16), dlb, "tn")
                dba_ref[...] += jnp.sum(dlogit, axis=0, keepdims=True)
                dp_ref[:, C_GQ:C_GQ + QK_W] = (dq * (DK ** -0.5)).astype(dp_ref.dtype)
                dp_ref[:, C_GK:C_GK + QK_W] = dk.astype(dp_ref.dtype)
                dp_ref[:, C_GV:C_GV + V_W] = dv.astype(dp_ref.dtype)
                dp_ref[:, C_GG:C_GG + V_W] = dgate.astype(dp_ref.dtype)
                dp_ref[:, C_GL:C_GL + GL_W] = dglow.astype(dp_ref.dtype)

    rev = lambda i: n_s - 1 - i
    const = lambda shape: pl.BlockSpec(shape, lambda i: tuple(0 for _ in shape))
    return _call(
        "attn_bwd", main, (n_s,),
        [pl.BlockSpec((SUPER, PROJ_W), lambda i: (rev(i), 0)),
         pl.BlockSpec((SUPER, 128), lambda i: (rev(i), 0)), pl.BlockSpec((SUPER, 128), lambda i: (rev(i), 0)),
         const((1, QK_W)), const((GL_W, QK_W)), const((1, QK_W)), const((1, V_W)), const((1, V_W)),
         pl.BlockSpec((SUPER, 2 * V_W), lambda i: (rev(i), 0)),
         pl.BlockSpec((SUPER, D), lambda i: (rev(i), 0)), const((2 * V_W, D)),
         pl.BlockSpec((1, 2, V_W, QK_W), lambda i: (rev(i), 0, 0, 0))],
        [pl.BlockSpec((SUPER, PROJ_W), lambda i: (rev(i), 0)),
         const((1, V_W)), const((1, V_W)), const((1, QK_W)), const((GL_W, QK_W))],
        [jax.ShapeDtypeStruct((T, PROJ_W), BF16),
         jax.ShapeDtypeStruct((1, V_W), F32), jax.ShapeDtypeStruct((1, V_W), F32),
         jax.ShapeDtypeStruct((1, QK_W), F32), jax.ShapeDtypeStruct((GL_W, QK_W), F32)],
        (proj, cos, sin_signed, lg, wa2p, ba, gn_ret, gn_gla, o, dx, w_out, states),
        scratch=[pltpu.VMEM((2, V_W, QK_W), F32), pltpu.VMEM((SUPER, 2 * V_W), F32)], comm=comm, after=after)


def _rotary_tables(T):
    half = DK // 2
    inv = ROPE_BASE ** (-jnp.arange(half, dtype=F32) * 2.0 / DK)
    ang = jnp.arange(T, dtype=F32)[:, None] * inv[None, :]
    cos, sin = jnp.cos(ang), jnp.sin(ang)
    cos_head = jnp.concatenate([cos, cos], axis=1)
    sin_head = jnp.concatenate([-sin, sin], axis=1)
    return jnp.tile(cos_head, (1, 128 // DK)), jnp.tile(sin_head, (1, 128 // DK))


def _sum_devices(name, gathered, m_per):
    def body(g_ref, o_ref):
        acc = g_ref[0:m_per, :]
        for k in range(1, N_DEV):
            acc = acc + g_ref[k * m_per:(k + 1) * m_per, :]
        o_ref[...] = acc

    return pl.pallas_call(body, name=name, out_shape=jax.ShapeDtypeStruct((m_per, 128), F32))(gathered)


def _owner_sums(name, items, owner, comm=None):
    counts = [1 + len(landed) for _, landed in items]

    def main(ins, outs, scr):
        at = 0
        for o_ref, n in zip(outs, counts):
            acc = ins[at][...].astype(F32)
            for l_ref in ins[at + 1:at + n]:
                for j in range(l_ref.shape[0]):
                    acc = acc + l_ref[j].astype(F32)
            o_ref[...] = acc
            at += n

    once = pl.Buffered(1)
    in_specs, out_specs, out_shape, args = [], [], [], []
    for grad, landed in items:
        R, C = grad.shape[-2:]
        in_specs.append(pl.BlockSpec((None, None, R, C), lambda i, s: (s[0], s[1], 0, 0), pipeline_mode=once))
        in_specs += [pl.BlockSpec(tuple(l.shape), lambda i, s: (0, 0, 0), pipeline_mode=once) for l in landed]
        out_specs.append(pl.BlockSpec((R, C), lambda i, s: (0, 0)))
        out_shape.append(jax.ShapeDtypeStruct((R, C), F32))
        args += [grad, *landed]
    return _call(name, main, (1,), in_specs, out_specs, out_shape, args, comm=comm, prefetch=owner)


def _adamw_group(name, items, n_blocks, comm=None):
    n = len(items)

    def main(ins, outs, scr):
        for p in range(n):
            g_ref, w_ref, m_ref, v_ref = ins[4 * p:4 * p + 4]
            d_ref, nm_ref, nv_ref = outs[3 * p:3 * p + 3]
            gv = g_ref[...]
            nm = ADAM_B1 * m_ref[...] + (1.0 - ADAM_B1) * gv
            nv = ADAM_B2 * v_ref[...] + (1.0 - ADAM_B2) * (gv * gv)
            m_hat = nm / (1.0 - ADAM_B1 ** ADAM_STEP)
            v_hat = nv / (1.0 - ADAM_B2 ** ADAM_STEP)
            d_ref[...] = -ADAM_LR * (m_hat / (jnp.sqrt(v_hat) + ADAM_EPS) + ADAM_WD * w_ref[...])
            nm_ref[...] = nm
            nv_ref[...] = nv

    in_specs, out_specs, out_shape, args = [], [], [], []
    for item in items:
        R, C = item[1].shape
        assert R % n_blocks == 0
        spec = pl.BlockSpec((R // n_blocks, C), lambda i: (i, 0))
        in_specs += [spec] * 4
        out_specs += [spec] * 3
        out_shape += [jax.ShapeDtypeStruct((R, C), F32)] * 3
        args += list(item)
    outs, extra = _call(name, main, (n_blocks,), in_specs, out_specs, out_shape, args, comm=comm)
    return [tuple(outs[3 * p:3 * p + 3]) for p in range(n)], extra


SMALL_ORDER = ("ffn1", "mix", "ffn2", "final", "ret", "gla", "b_a")


def kernel(x, ffn1_norm_g, ffn1_w_gate, ffn1_w_up, ffn1_w_down, mix_norm_g, w_in, ret_norm_g, gla_w_a2, gla_b_a, gla_norm_g, w_out, ffn2_norm_g, ffn2_w_gate, ffn2_w_up, ffn2_w_down, final_norm_g, loss_target, m_ffn1_norm_g, m_ffn1_w_gate, m_ffn1_w_up, m_ffn1_w_down, m_mix_norm_g, m_w_in, m_ret_norm_g, m_gla_w_a2, m_gla_b_a, m_gla_norm_g, m_w_out, m_ffn2_norm_g, m_ffn2_w_gate, m_ffn2_w_up, m_ffn2_w_down, m_final_norm_g, v_ffn1_norm_g, v_ffn1_w_gate, v_ffn1_w_up, v_ffn1_w_down, v_mix_norm_g, v_w_in, v_ret_norm_g, v_gla_w_a2, v_gla_b_a, v_gla_norm_g, v_w_out, v_ffn2_norm_g, v_ffn2_w_gate, v_ffn2_w_up, v_ffn2_w_down, v_final_norm_g):
    xi, yi, ci = _coords()
    dev = 4 * xi + 2 * yi + ci
    owner = jnp.stack([2 * xi + yi, ci]).astype(jnp.int32)

    x0, target = x[0], loss_target[0]
    T, D = x0.shape
    fb = ffn1_w_gate.shape[2]
    ib = w_in.shape[2]
    ab = gla_w_a2.shape[2]
    F = N_DEV * fb
    cos, sin_signed = _rotary_tables(T)
    lg = jnp.repeat(jnp.log(1.0 - 2.0 ** (-5.0 - jnp.arange(HEADS, dtype=F32))), DK)[None, :]
    g_final = final_norm_g.reshape(1, D)

    g1_loc = ffn1_w_gate[0].T[None].astype(BF16)
    u1_loc = ffn1_w_up[0].T[None].astype(BF16)
    d1_loc = ffn1_w_down.astype(BF16)
    g2_loc = ffn2_w_gate[0].T[None].astype(BF16)
    u2_loc = ffn2_w_up[0].T[None].astype(BF16)
    d2_loc = ffn2_w_down.astype(BF16)
    in_loc = w_in[0].T.astype(BF16)
    out_loc = w_out[0].astype(BF16)

    h1, (g1,) = _rms_fwd("ffn1_rms", x0, ffn1_norm_g, comm=_AllGather([g1_loc], ["stack"]))
    g1 = g1.reshape(1, F, D)
    (dsl1, sl1), (u1,) = _mm_nstream("ffn1_gate", h1, [g1], [0], "nt", [], [BF16, BF16], _gate_parts_epilogue, cn=256,
                                     comm=_AllGather([u1_loc], ["stack"]))
    u1 = u1.reshape(1, F, D)
    (dsu1, act1), (d1,) = _mm_nstream("ffn1_up", h1, [u1], [0], "nt", [dsl1, sl1], [BF16, BF16],
                                      _up_act_epilogue, cn=256, comm=_AllGather([d1_loc], ["stack"]))
    d1 = d1.reshape(1, F, D)
    f32_tile, bf16_tile, f32_vec = (F32, "tile"), (BF16, "tile"), (F32, "vec")
    (x1, h2), (in_all, a_all) = _mm_mstream(
        "ffn1_down", [act1], [d1], [0], "nn", [(x0, "tile"), (mix_norm_g, "vec")], [f32_tile, bf16_tile],
        _residual_rms_epilogue(0.5), comm=_AllGather([in_loc, gla_w_a2[0]], ["plain", "plain"]))
    w_in_t = jnp.pad(in_all.reshape(1, N_DEV * ib, D), ((0, 0), (0, PROJ_W - N_DEV * ib), (0, 0)))
    wa2 = jnp.transpose(a_all, (1, 0, 2)).reshape(GATE_RANK, N_DEV * ab)
    wa2p = jnp.pad(wa2, ((0, GL_W - GATE_RANK), (0, 0)))

    (proj,), (g2, out_all) = _mm_nstream("mix_proj", h2, [w_in_t], [0], "nt", [], [BF16], _identity_epilogue, cn=640,
                                         comm=_AllGather([g2_loc, out_loc], ["stack", "plain"]))
    w_out_full = out_all.reshape(D, D)
    (o, ymix, states, x2, h3), (u2,) = _attn_fwd(proj, cos, sin_signed, lg, wa2p, gla_b_a, ret_norm_g, gla_norm_g,
                                                 x1, w_out_full, ffn2_norm_g, comm=_AllGather([u2_loc], ["stack"]))
    g2, u2 = g2.reshape(1, F, D), u2.reshape(1, F, D)

    (dsu2, sl2, act2), (d2,) = _mm_nstream(
        "ffn2_up", h3, [g2, u2], [0, 0], "nt", [], [BF16, BF16, BF16], _silu_mul_epilogue, cn=256,
        comm=_AllGather([d2_loc], ["stack"]))
    d2 = d2.reshape(1, F, D)
    (dx3, dy3b, d_final, loss), _ = _mm_mstream(
        "ffn2_down", [act2], [d2], [0], "nn", [(x2, "tile"), (g_final, "vec"), (target, "tile")],
        [f32_tile, bf16_tile, f32_vec, f32_vec], _final_loss_epilogue(0.5, 0.5))

    sent = {}

    def send(nm, grad):
        sent[nm] = _send_to_owners("send_" + nm, grad)
        return sent[nm][4]

    dwd2, _ = _mm_tn("ffn2b_dwd", act2, dy3b, F // 2, D, BF16)
    tok = send("wd2", dwd2.reshape(4, 2, fb, D))
    (dgate2, dup2), _ = _mm_nstream("ffn2b_dact", dy3b, [d2], [0], "nt", [dsu2, sl2], [BF16, BF16],
                                    _dact_epilogue, cn=256, after=tok)
    dwg2, _ = _mm_tn("ffn2b_dwg", dgate2, h3, F // 2, D, BF16)
    tok = send("wg2", dwg2.reshape(4, 2, fb, D))
    dwu2, _ = _mm_tn("ffn2b_dwu", dup2, h3, F // 2, D, BF16, after=tok)
    tok = send("wu2", dwu2.reshape(4, 2, fb, D))
    rms_outs = [f32_tile, bf16_tile, f32_vec]
    (dx2, dx2b, d_g2), _ = _mm_mstream(
        "ffn2b_dh", [dgate2, dup2], [g2, u2], [0, 0], "nn", [(x2, "tile"), (ffn2_norm_g, "vec"), (dx3, "tile")],
        rms_outs, _rms_bwd_epilogue(1.0), after=tok)

    dwout, _ = _mm_tn("mixb_dwout", ymix, dx2b, D, D, BF16)
    tok = send("wout", dwout.reshape(4, 2, D // N_DEV, D))
    (dproj, d_ret, d_gla, d_ba, d_wa2p), _ = _attn_bwd(
        proj, cos, sin_signed, lg, wa2p, gla_b_a, ret_norm_g, gla_norm_g, o, dx2b, w_out_full, states, after=tok)
    dwin_t, _ = _mm_tn("mixb_dwin", dproj, h2, 640, D, BF16, tk=2048)
    tok = send("win", dwin_t[:N_DEV * ib].reshape(4, 2, ib, D))
    (dx1, dy1b, d_gmix), _ = _mm_mstream(
        "mixb_dh", [dproj], [w_in_t], [0], "nn", [(x1, "tile"), (mix_norm_g, "vec"), (dx2, "tile")],
        rms_outs, _rms_bwd_epilogue(0.5), after=tok)

    dwd1, _ = _mm_tn("ffn1b_dwd", act1, dy1b, F // 2, D, BF16)
    tok = send("wd1", dwd1.reshape(4, 2, fb, D))
    (dgate1, dup1), _ = _mm_nstream("ffn1b_dact", dy1b, [d1], [0], "nt", [dsu1, sl1], [BF16, BF16],
                                    _dact_epilogue, cn=256, after=tok)
    dwg1, _ = _mm_tn("ffn1b_dwg", dgate1, h1, F // 2, D, BF16)
    tok = send("wg1", dwg1.reshape(4, 2, fb, D))
    dwu1, _ = _mm_tn("ffn1b_dwu", dup1, h1, F // 2, D, BF16, after=tok)
    tok = send("wu1", dwu1.reshape(4, 2, fb, D))
    (dx0, _, d_g1), _ = _mm_mstream(
        "ffn1b_dh", [dgate1, dup1], [g1, u1], [0, 0], "nn", [(x0, "tile"), (ffn1_norm_g, "vec"), (dx1, "tile")],
        rms_outs, _rms_bwd_epilogue(1.0), after=tok)

    small = dict(ffn1=d_g1, mix=d_gmix, ffn2=d_g2, final=d_final, ret=d_ret, gla=d_gla, b_a=d_ba)
    flat = jnp.concatenate([small[k].reshape(-1) for k in SMALL_ORDER]
                           + [d_wa2p[:GATE_RANK].reshape(-1), loss[0, :128]])
    rows = -(-flat.shape[0] // 128)
    rows = -(-rows // 8) * 8
    packed = jnp.pad(flat, (0, rows * 128 - flat.shape[0])).reshape(rows, 128)

    transposed = ("ffn1_w_gate", "ffn1_w_up", "ffn2_w_gate", "ffn2_w_up", "w_in")

    def to_2d(nm, a):
        if nm in transposed:
            return a[0].T
        return a.reshape((1, a.shape[0]) if a.ndim == 1 else a.shape[-2:])

    def from_2d(nm, a):
        return a.T[None] if nm in transposed else a.reshape(params[nm][0].shape)

    def arrived(nm, after):
        grad, landed = _await_owners("await_" + nm, sent[nm], after)
        return grad, [landed]

    sums_a, (gathered,) = _owner_sums(
        "sum_a", [arrived(nm, dx0) for nm in ("wg2", "wu2", "wd2", "win", "wout")], owner,
        comm=_AllGather([packed], ["plain"]))
    params = dict(
        ffn2_w_gate=(ffn2_w_gate, m_ffn2_w_gate, v_ffn2_w_gate), ffn2_w_up=(ffn2_w_up, m_ffn2_w_up, v_ffn2_w_up),
        ffn2_w_down=(ffn2_w_down, m_ffn2_w_down, v_ffn2_w_down), w_in=(w_in, m_w_in, v_w_in),
        w_out=(w_out, m_w_out, v_w_out), ffn1_w_gate=(ffn1_w_gate, m_ffn1_w_gate, v_ffn1_w_gate),
        ffn1_w_up=(ffn1_w_up, m_ffn1_w_up, v_ffn1_w_up), ffn1_w_down=(ffn1_w_down, m_ffn1_w_down, v_ffn1_w_down),
        ffn1_norm_g=(ffn1_norm_g, m_ffn1_norm_g, v_ffn1_norm_g), mix_norm_g=(mix_norm_g, m_mix_norm_g, v_mix_norm_g),
        ret_norm_g=(ret_norm_g, m_ret_norm_g, v_ret_norm_g), gla_w_a2=(gla_w_a2, m_gla_w_a2, v_gla_w_a2),
        gla_b_a=(gla_b_a, m_gla_b_a, v_gla_b_a), gla_norm_g=(gla_norm_g, m_gla_norm_g, v_gla_norm_g),
        ffn2_norm_g=(ffn2_norm_g, m_ffn2_norm_g, v_ffn2_norm_g), final_norm_g=(final_norm_g, m_final_norm_g, v_final_norm_g))
    grads, updates = {}, {}

    def run_adam(name, names, grad_2d, n_blocks):
        items = [(grad_2d[nm],) + tuple(to_2d(nm, a) for a in params[nm]) for nm in names]
        res, _ = _adamw_group(name, items, n_blocks)
        for nm, r in zip(names, res):
            grads[nm] = from_2d(nm, grad_2d[nm])
            updates[nm] = tuple(from_2d(nm, a) for a in r)
        return res

    grads_a = {"ffn2_w_gate": sums_a[0], "ffn2_w_up": sums_a[1], "ffn2_w_down": sums_a[2], "w_out": sums_a[4]}
    run_adam("adamw_w_in", ["w_in"], {"w_in": sums_a[3]}, 1)
    done_a = run_adam("adamw_a", list(grads_a), grads_a, 4)[0][0]
    sums_b, _ = _owner_sums("sum_b", [arrived(nm, done_a) for nm in ("wg1", "wu1", "wd1")], owner)
    grads_b = {"ffn1_w_gate": sums_b[0], "ffn1_w_up": sums_b[1], "ffn1_w_down": sums_b[2]}
    run_adam("adamw_b", list(grads_b), grads_b, 4)

    total = _sum_devices("sum_small", gathered.reshape(N_DEV * rows, 128), rows).reshape(-1)
    sizes = [small[k].size for k in SMALL_ORDER] + [GATE_RANK * QK_W, 128]
    offs = [0]
    for s in sizes:
        offs.append(offs[-1] + s)
    pieces = [total[offs[i]:offs[i + 1]] for i in range(len(sizes))]
    g_small = {k: pieces[i].reshape(small[k].shape) for i, k in enumerate(SMALL_ORDER)}
    g_wa2_full = pieces[len(SMALL_ORDER)].reshape(GATE_RANK, QK_W)
    g_wa2 = lax.dynamic_slice(g_wa2_full, (0, dev * ab), (GATE_RANK, ab))
    loss_total = pieces[len(SMALL_ORDER) + 1][0]

    small_grads = {"ffn1_norm_g": g_small["ffn1"], "mix_norm_g": g_small["mix"], "ret_norm_g": g_small["ret"],
                   "gla_w_a2": g_wa2, "gla_b_a": g_small["b_a"], "gla_norm_g": g_small["gla"],
                   "ffn2_norm_g": g_small["ffn2"], "final_norm_g": g_small["final"]}
    run_adam("adamw_small", list(small_grads), small_grads, 1)

    order = ("ffn1_norm_g", "ffn1_w_gate", "ffn1_w_up", "ffn1_w_down", "mix_norm_g", "w_in", "ret_norm_g", "gla_w_a2",
             "gla_b_a", "gla_norm_g", "w_out", "ffn2_norm_g", "ffn2_w_gate", "ffn2_w_up", "ffn2_w_down", "final_norm_g")
    return (loss_total, dx0[None], *[grads[nm] for nm in order], *[updates[nm][0] for nm in order],
            *[updates[nm][1] for nm in order], *[updates[nm][2] for nm in order])
```

```python
import functools
import math

import jax
import jax.numpy as jnp
from jax import lax
from jax.experimental import pallas as pl
from jax.experimental.pallas import tpu as pltpu

F32 = jnp.float32
BF16 = jnp.bfloat16
MESH = pl.DeviceIdType.MESH
HBM = pl.BlockSpec(memory_space=pltpu.HBM)

N_DEV = 8
RMS_EPS = 1e-6
ROPE_BASE = 10000.0
HEADS = 4
DK = 64
DV = 128
QK_W = HEADS * DK
V_W = HEADS * DV
GATE_RANK = 16
GATE_NORM = 16.0
CHUNK = 64
SUPER = 256
PROJ_W = 3200
C_RQ, C_RK, C_RV, C_RG, C_GQ, C_GK, C_GV, C_GG, C_GL = 0, 256, 512, 1024, 1536, 1792, 2048, 2560, 3072
GL_W = PROJ_W - C_GL
ADAM_LR, ADAM_B1, ADAM_B2, ADAM_EPS, ADAM_WD, ADAM_STEP = 0.001, 0.9, 0.999, 1e-08, 0.01, 10
VMEM_LIMIT_V7X = 52 * 1024 * 1024


def _cparams(**kw):
    return pltpu.CompilerParams(vmem_limit_bytes=VMEM_LIMIT_V7X, **kw)


def _dot(a, b, form, precision=None):
    dims = {"nn": (((1,), (0,)), ((), ())), "nt": (((1,), (1,)), ((), ())), "tn": (((0,), (0,)), ((), ()))}[form]
    return lax.dot_general(a, b, dims, preferred_element_type=F32, precision=precision)


def _sigmoid(x):
    return 1.0 / (1.0 + jnp.exp(-x))


def _coords():
    return lax.axis_index("x"), lax.axis_index("y"), lax.axis_index("c")


class _NoComm:
    inputs, out_shapes, scratch = (), (), ()


class _StartSends:
    scratch = ()

    def __init__(self, grads):
        n = len(RELATIONS)
        self.inputs, self.out_shapes, self.out_specs, self.aliases = (), (), (), {}
        for k, g in enumerate(grads):
            land_shape = (n,) + g.shape[2:]
            self.inputs += (pltpu.with_memory_space_constraint(g, pltpu.HBM),
                            pltpu.with_memory_space_constraint(lax.empty(land_shape, g.dtype), pltpu.HBM))
            self.out_shapes += (pltpu.SemaphoreType.DMA((n,)), pltpu.SemaphoreType.DMA((n,)),
                                pltpu.HBM(g.shape, g.dtype), pltpu.HBM(land_shape, g.dtype))
            self.out_specs += (SEM, SEM, HBM, HBM)
            self.aliases.update({2 * k: 4 * k + 2, 2 * k + 1: 4 * k + 3})

    def start(self, srcs, outs, sems):
        for k in range(len(srcs) // 2):
            send_sems, recv_sems = outs[4 * k], outs[4 * k + 1]
            for cp in _owner_copies(srcs[2 * k], srcs[2 * k + 1], send_sems, recv_sems):
                cp.start()

    def mid(self, srcs, outs, sems):
        pass

    def finish(self, srcs, outs, sems):
        pass


class _AllGather:
    def __init__(self, arrays, kinds):
        self.inputs = tuple(arrays)
        self.kinds = tuple(kinds)
        n = len(arrays)
        self.out_shapes = tuple(
            jax.ShapeDtypeStruct((a.shape[0], N_DEV) + a.shape[1:] if k == "stack" else (N_DEV,) + a.shape, a.dtype)
            for a, k in zip(arrays, kinds))
        self.scratch = (pltpu.SemaphoreType.DMA((n, 7)), pltpu.SemaphoreType.DMA((n, 7)),
                        pltpu.SemaphoreType.DMA((n,)))

    def _ctx(self, srcs, outs, sems):
        send_sems, recv_sems, local_sems = sems
        x, y, c = _coords()
        me, sibling = (x, y, c), (x, y, 1 - c)
        chips = [(1 - x, y), (x, 1 - y), (1 - x, 1 - y)]

        def blk(m, dev):
            k = 4 * dev[0] + 2 * dev[1] + dev[2]
            return outs[m].at[:, k] if self.kinds[m] == "stack" else outs[m].at[k]

        def copy(m, s, block, to, src=None):
            return pltpu.make_async_remote_copy(
                src_ref=blk(m, block) if src is None else src, dst_ref=blk(m, block),
                send_sem=send_sems.at[m, s], recv_sem=recv_sems.at[m, s], device_id=to, device_id_type=MESH)

        def mine(m):
            return pltpu.make_async_copy(srcs[m], blk(m, me), local_sems.at[m])

        def first(m):
            return [copy(m, 0, me, sibling, src=srcs[m])] + [
                copy(m, 1 + j, me, (*chip, c), src=srcs[m]) for j, chip in enumerate(chips)]

        return me, sibling, chips, c, copy, mine, first

    def start(self, srcs, outs, sems):
        me, sibling, chips, c, copy, mine, first = self._ctx(srcs, outs, sems)
        for m in range(len(srcs)):
            mine(m).start()
            for cp in first(m):
                cp.start()

    def mid(self, srcs, outs, sems):
        me, sibling, chips, c, copy, mine, first = self._ctx(srcs, outs, sems)
        for j, chip in enumerate(chips):
            for m in range(len(srcs)):
                copy(m, 1 + j, (*chip, c), me).wait_recv()
                copy(m, 4 + j, (*chip, c), sibling).start()

    def finish(self, srcs, outs, sems):
        me, sibling, chips, c, copy, mine, first = self._ctx(srcs, outs, sems)
        for m in range(len(srcs)):
            copy(m, 0, sibling, me).wait_recv()
            for j, chip in enumerate(chips):
                copy(m, 4 + j, (*chip, 1 - c), me).wait_recv()
            for cp in first(m):
                cp.wait_send()
            for j, chip in enumerate(chips):
                copy(m, 4 + j, (*chip, c), sibling).wait_send()
            mine(m).wait()


RELATIONS = ((0, 0, 1), (1, 0, 0), (0, 1, 0), (1, 1, 0), (1, 0, 1), (0, 1, 1), (1, 1, 1))
SEM = pl.BlockSpec(memory_space=pltpu.SEMAPHORE)
SPLIT_PARAMS = dict(has_side_effects=pltpu.SideEffectType.DATAFLOW_SIDE_EFFECTING)


def _owner_copies(grad_ref, land_ref, send_sems, recv_sems):
    x, y, c = _coords()
    copies = []
    for s, (fx, fy, fc) in enumerate(RELATIONS):
        px = 1 - x if fx else x
        py = 1 - y if fy else y
        pc = 1 - c if fc else c
        copies.append(pltpu.make_async_remote_copy(
            src_ref=grad_ref.at[2 * px + py, pc], dst_ref=land_ref.at[s], send_sem=send_sems.at[s],
            recv_sem=recv_sems.at[s], device_id=(px, py, pc), device_id_type=MESH))
    return copies


def _send_to_owners(name, grad):
    n = len(RELATIONS)
    land_shape = (n,) + grad.shape[2:]

    def body(g_ref, land_ref, send_sems, recv_sems, g_thru, land_thru, token):
        for cp in _owner_copies(g_ref, land_ref, send_sems, recv_sems):
            cp.start()
        token[...] = jnp.zeros_like(token)

    return pl.pallas_call(
        body, name=name,
        out_shape=(pltpu.SemaphoreType.DMA((n,)), pltpu.SemaphoreType.DMA((n,)), pltpu.HBM(grad.shape, grad.dtype),
                   pltpu.HBM(land_shape, grad.dtype), jax.ShapeDtypeStruct((8, 128), F32)),
        in_specs=(HBM, HBM), out_specs=(SEM, SEM, HBM, HBM, pl.BlockSpec(memory_space=pltpu.VMEM)),
        input_output_aliases={0: 2, 1: 3}, compiler_params=pltpu.CompilerParams(**SPLIT_PARAMS),
    )(pltpu.with_memory_space_constraint(grad, pltpu.HBM),
      pltpu.with_memory_space_constraint(lax.empty(land_shape, grad.dtype), pltpu.HBM))


def _await_owners(name, started, after):
    send_sems, recv_sems, g_thru, land_thru, _ = started

    def body(g_ref, land_ref, send_sems, recv_sems, after_ref, g_out, land_out):
        for cp in _owner_copies(g_ref, land_ref, send_sems, recv_sems):
            cp.wait_send()
            cp.wait_recv()

    return pl.pallas_call(
        body, name=name, out_shape=(pltpu.HBM(g_thru.shape, g_thru.dtype), pltpu.HBM(land_thru.shape, land_thru.dtype)),
        in_specs=(HBM, HBM, SEM, SEM, pl.BlockSpec(memory_space=pl.ANY)), out_specs=(HBM, HBM),
        input_output_aliases={0: 0, 1: 1}, compiler_params=pltpu.CompilerParams(**SPLIT_PARAMS),
    )(g_thru, land_thru, send_sems, recv_sems, after)


def _call(name, main, grid, in_specs, out_specs, out_shape, args, scratch=(), comm=None, prefetch=None, after=None):
    comm = comm or _NoComm()
    n_main = len(in_specs)
    if after is not None:
        in_specs = list(in_specs) + [pl.BlockSpec(after.shape, lambda *_: (0,) * after.ndim)]
        args = tuple(args) + (after,)
    counts = [len(in_specs), len(comm.inputs), len(out_shape), len(comm.out_shapes), len(scratch), len(comm.scratch)]
    n_steps = math.prod(grid)
    hosted = bool(comm.inputs)

    def body(*refs):
        if prefetch is not None:
            refs = refs[1:]
        parts, at = [], 0
        for n in counts:
            parts.append(refs[at:at + n])
            at += n
        ins, c_in, outs, c_out, scr, c_scr = parts
        ins = ins[:n_main]
        step = pl.program_id(0)
        for d in range(1, len(grid)):
            step = step * grid[d] + pl.program_id(d)
        if hosted:
            @pl.when(step == 0)
            def _():
                comm.start(c_in, c_out, c_scr)
        main(ins, outs, scr)
        if hosted:
            @pl.when(step == max(n_steps - 2, 0))
            def _():
                comm.mid(c_in, c_out, c_scr)

            @pl.when(step == n_steps - 1)
            def _():
                comm.finish(c_in, c_out, c_scr)

    all_in = list(in_specs) + [HBM] * counts[1]
    all_out = list(out_specs) + list(getattr(comm, "out_specs", [HBM] * counts[3]))
    all_scratch = list(scratch) + list(comm.scratch)
    shapes = list(out_shape) + list(comm.out_shapes)
    first_in = counts[0] + (0 if prefetch is None else 1)
    aliases = {first_in + i: counts[2] + o for i, o in getattr(comm, "aliases", {}).items()}
    params = _cparams(**SPLIT_PARAMS) if aliases else _cparams()
    if prefetch is None:
        res = pl.pallas_call(body, name=name, grid=grid, in_specs=all_in, out_specs=all_out, out_shape=shapes,
                             scratch_shapes=all_scratch, input_output_aliases=aliases,
                             compiler_params=params)(*args, *comm.inputs)
    else:
        res = pl.pallas_call(
            body, name=name, out_shape=shapes, input_output_aliases=aliases,
            grid_spec=pltpu.PrefetchScalarGridSpec(num_scalar_prefetch=1, grid=grid, in_specs=all_in,
                                                   out_specs=all_out, scratch_shapes=all_scratch),
            compiler_params=params)(prefetch, *args, *comm.inputs)
    return res[:counts[2]], res[counts[2]:]


def _rms_fwd(name, x, g, comm=None):
    T, D = x.shape
    tm = min(T, 512)

    def main(ins, outs, scr):
        x_ref, g_ref = ins
        xv = x_ref[...]
        r = lax.rsqrt(jnp.mean(xv * xv, axis=-1, keepdims=True) + RMS_EPS)
        outs[0][...] = (xv * r * g_ref[...]).astype(outs[0].dtype)

    tile = pl.BlockSpec((tm, D), lambda i: (i, 0))
    (h,), extra = _call(name, main, (T // tm,), [tile, pl.BlockSpec((1, D), lambda i: (0, 0))], [tile],
                        [jax.ShapeDtypeStruct((T, D), BF16)], (x, g), comm=comm)
    return h, extra


def _final_loss_epilogue(scale, out_scale):
    def ep(acc, ex, outs):
        res_ref, g_ref, t_ref = ex
        dx_ref, dxb_ref, dg_ref, loss_ref = outs
        n = acc.shape[-1]
        xv = res_ref[...] + scale * acc
        r = lax.rsqrt(jnp.mean(xv * xv, axis=-1, keepdims=True) + RMS_EPS)
        xhat = xv * r
        err = xhat * g_ref[...] - t_ref[...]

        @pl.when(pl.program_id(0) == 0)
        def _():
            dg_ref[...] = jnp.zeros_like(dg_ref)
            loss_ref[...] = jnp.zeros_like(loss_ref)

        loss_ref[...] += jnp.broadcast_to(jnp.sum(err * err) * (0.5 / n), loss_ref.shape)
        dy = err * (1.0 / n)
        dg_ref[...] += jnp.sum(dy * xhat, axis=0, keepdims=True)
        dxhat = dy * g_ref[...]
        dx = r * (dxhat - xhat * jnp.mean(dxhat * xhat, axis=-1, keepdims=True))
        dx_ref[...] = dx
        dxb_ref[...] = (out_scale * dx).astype(dxb_ref.dtype)
    return ep


def _mm_nstream(name, a, ws, w_sel, w_form, comps, out_dtypes, epilogue, cn, rows=1024, comm=None, after=None):
    T, K = a.shape
    N = ws[0].shape[1]
    rows = min(rows, T)
    assert N % cn == 0 and T % rows == 0
    n_w, n_c = len(ws), len(comps)

    def main(ins, outs, scr):
        a_ref = ins[0]
        w_refs = ins[1:1 + n_w]
        c_refs = ins[1 + n_w:1 + n_w + n_c]

        for r in range(T // rows):
            sl = slice(r * rows, (r + 1) * rows)
            a_blk = a_ref[sl, :]
            dots = [_dot(a_blk, w_ref[...], w_form) for w_ref in w_refs]
            res = epilogue(dots, [c_ref[sl, :] for c_ref in c_refs])
            for o_ref, o in zip(outs, res):
                o_ref[sl, :] = o.astype(o_ref.dtype)

    if w_form == "nt":
        w_specs = [pl.BlockSpec((None, cn, K), functools.partial(lambda j, s: (s, j, 0), s=s)) for s in w_sel]
    else:
        w_specs = [pl.BlockSpec((K, cn), lambda j: (0, j)) for _ in ws]
    chunk = pl.BlockSpec((T, cn), lambda j: (0, j))
    return _call(name, main, (N // cn,), [pl.BlockSpec((T, K), lambda j: (0, 0))] + w_specs + [chunk] * n_c,
                 [chunk] * len(out_dtypes), [jax.ShapeDtypeStruct((T, N), dt) for dt in out_dtypes],
                 (a, *ws, *comps), comm=comm, after=after)


def _mm_mstream(name, as_, ws, w_sel, w_form, extras, outs_desc, epilogue, tm=512, comm=None, after=None):
    T = as_[0].shape[0]
    tm = min(tm, T)
    n_a = len(as_)
    w_shapes = [w.shape[-2:] for w in ws]
    N = w_shapes[0][1] if w_form == "nn" else w_shapes[0][0]

    def main(ins, outs, scr):
        a_refs = ins[:n_a]
        w_refs = ins[n_a:2 * n_a]
        acc = None
        for a_ref, w_ref in zip(a_refs, w_refs):
            d = _dot(a_ref[...], w_ref[...], w_form)
            acc = d if acc is None else acc + d
        epilogue(acc, ins[2 * n_a:], outs)

    kind_spec = {"tile": pl.BlockSpec((tm, N), lambda i: (i, 0)), "vec": pl.BlockSpec((1, N), lambda i: (0, 0))}
    kind_shape = {"tile": (T, N), "vec": (1, N)}
    a_specs = [pl.BlockSpec((tm, a.shape[1]), lambda i: (i, 0)) for a in as_]
    w_specs = []
    for w, s in zip(ws, w_sel):
        if w.ndim == 3:
            w_specs.append(pl.BlockSpec((None,) + tuple(w.shape[1:]), functools.partial(lambda i, s: (s, 0, 0), s=s),
                                        pipeline_mode=pl.Buffered(1)))
        else:
            w_specs.append(pl.BlockSpec(tuple(w.shape), lambda i: (0, 0), pipeline_mode=pl.Buffered(1)))
    args = list(as_) + list(ws) + [e for e, _ in extras]
    return _call(name, main, (T // tm,), a_specs + w_specs + [kind_spec[k] for _, k in extras],
                 [kind_spec[k] for _, k in outs_desc],
                 [jax.ShapeDtypeStruct(kind_shape[k], dt) for dt, k in outs_desc], args, comm=comm, after=after)


def _residual_rms_epilogue(scale):
    def ep(acc, ex, outs):
        xv = ex[0][...] + scale * acc
        outs[0][...] = xv
        r = lax.rsqrt(jnp.mean(xv * xv, axis=-1, keepdims=True) + RMS_EPS)
        outs[1][...] = (xv * r * ex[1][...]).astype(outs[1].dtype)
    return ep


def _rms_bwd_epilogue(out_scale):
    def ep(acc, ex, outs):
        x_ref, g_ref, dres_ref = ex
        dx_ref, dxb_ref, dg_ref = outs
        xv = x_ref[...]
        r = lax.rsqrt(jnp.mean(xv * xv, axis=-1, keepdims=True) + RMS_EPS)
        xhat = xv * r

        @pl.when(pl.program_id(0) == 0)
        def _():
            dg_ref[...] = jnp.zeros_like(dg_ref)

        dg_ref[...] += jnp.sum(acc * xhat, axis=0, keepdims=True)
        dxhat = acc * g_ref[...]
        dx = r * (dxhat - xhat * jnp.mean(dxhat * xhat, axis=-1, keepdims=True)) + dres_ref[...]
        dx_ref[...] = dx
        dxb_ref[...] = (out_scale * dx).astype(dxb_ref.dtype)
    return ep


def _mm_tn(name, a, b, tmo, tno, out_dtype, tk=1024, comm=None, after=None):
    T, Ma = a.shape
    Nb = b.shape[1]
    tk = min(tk, T)
    nk = T // tk

    def main(ins, outs, scr):
        a_ref, b_ref = ins
        (acc_ref,) = scr
        k = pl.program_id(2)

        @pl.when(k == 0)
        def _():
            acc_ref[...] = jnp.zeros_like(acc_ref)

        acc_ref[...] += _dot(a_ref[...], b_ref[...], "tn")

        @pl.when(k == nk - 1)
        def _():
            outs[0][...] = acc_ref[...].astype(outs[0].dtype)

    (out,), extra = _call(
        name, main, (Ma // tmo, Nb // tno, nk),
        [pl.BlockSpec((tk, tmo), lambda i, j, k: (k, i)), pl.BlockSpec((tk, tno), lambda i, j, k: (k, j))],
        [pl.BlockSpec((tmo, tno), lambda i, j, k: (i, j))], [jax.ShapeDtypeStruct((Ma, Nb), out_dtype)],
        (a, b), scratch=[pltpu.VMEM((tmo, tno), F32)], comm=comm, after=after)
    return out, extra


def _swiglu_parts(g, u):
    s = _sigmoid(g)
    silu = g * s
    return [u * (s + silu * (1.0 - s)), silu, silu * u]


def _silu_mul_epilogue(dots, comps):
    g, u = dots
    return _swiglu_parts(g, u)


def _gate_parts_epilogue(dots, comps):
    (g,) = dots
    s = _sigmoid(g)
    silu = g * s
    return [s + silu * (1.0 - s), silu]


def _up_act_epilogue(dots, comps):
    (u,) = dots
    return [u * comps[0].astype(F32), u * comps[1].astype(F32)]


def _dact_epilogue(dots, comps):
    dact = dots[0].astype(BF16)
    return [dact * comps[0], dact * comps[1]]


def _identity_epilogue(dots, comps):
    return list(dots)


def _swap_halves(x):
    lane = lax.broadcasted_iota(jnp.int32, x.shape, 1)
    first = (lane % DK) < (DK // 2)
    return jnp.where(first, pltpu.roll(x, 128 - DK // 2, 1), pltpu.roll(x, DK // 2, 1))


def _rotary(t, cos, sin_signed):
    halves = []
    for p in range(QK_W // 128):
        th = t[:, 128 * p:128 * (p + 1)]
        halves.append(th * cos + _swap_halves(th) * sin_signed)
    return jnp.concatenate(halves, axis=1)


def _rotary_transposed(d, cos, sin_signed):
    halves = []
    for p in range(QK_W // 128):
        dh = d[:, 128 * p:128 * (p + 1)]
        halves.append(dh * cos + _swap_halves(dh * sin_signed))
    return jnp.concatenate(halves, axis=1)


def _log_sigmoid(x):
    return jnp.minimum(x, 0.0) - jnp.log(1.0 + jnp.exp(-jnp.abs(x)))


def _attn_masks():
    row = lax.broadcasted_iota(jnp.int32, (SUPER, SUPER), 0)
    col = lax.broadcasted_iota(jnp.int32, (SUPER, SUPER), 1)
    same = (row // CHUNK) == (col // CHUNK)
    return row, col, same


def _group_inputs(grp, pr, cos, sin_signed, lg, wa2, ba):
    seg = lambda lo, width: pr[:, lo:lo + width].astype(F32)
    if grp == 0:
        q = _rotary(seg(C_RQ, QK_W), cos, sin_signed)
        k = _rotary(seg(C_RK, QK_W), cos, sin_signed) * (DK ** -0.5)
        v = pr[:, C_RV:C_RV + V_W]
        gate = seg(C_RG, V_W)
        pos = lax.broadcasted_iota(jnp.int32, (SUPER, QK_W), 0).astype(F32) + 1.0
        return q, k, v, gate, pos * lg, None, None
    q = seg(C_GQ, QK_W) * (DK ** -0.5)
    k = seg(C_GK, QK_W)
    v = pr[:, C_GV:C_GV + V_W]
    gate = seg(C_GG, V_W)
    glow = pr[:, C_GL:C_GL + GL_W]
    logit = _dot(glow.astype(BF16), wa2.astype(BF16), "nn") + ba
    la = _log_sigmoid(logit) * (1.0 / GATE_NORM)
    row, col, _ = _attn_masks()
    lower = (col <= row).astype(F32)
    b_cum = _dot(lower, la, "nn", precision=lax.Precision.HIGHEST)
    return q, k, v, gate, b_cum, glow, logit


def _decay_factors(q, k, b_cum):
    c = b_cum[SUPER // 2 - 1:SUPER // 2, :]
    bl = b_cum[SUPER - 1:SUPER, :]
    e1 = jnp.exp(b_cum - c)
    e2 = jnp.exp(c - b_cum)
    e_b = jnp.exp(b_cum)
    e_l = jnp.exp(bl - b_cum)
    return dict(e1=e1, e2=e2, eb=e_b, el=e_l, ebl=jnp.exp(bl),
                qp=q * e1, qm=q * e2, kp=k * e1, km=k * e2, qs=q * e_b, kl=k * e_l)


def _state_block_mask():
    r = lax.broadcasted_iota(jnp.int32, (V_W, QK_W), 0)
    c = lax.broadcasted_iota(jnp.int32, (V_W, QK_W), 1)
    return (r // DV) == (c // DK)


def _attn_fwd(proj, cos, sin_signed, lg, wa2p, ba, gn_ret, gn_gla, x_res, w_out, g_next, comm=None):
    T = proj.shape[0]
    n_s = T // SUPER
    D = x_res.shape[1]

    def main(ins, outs, scr):
        pr_ref, cos_ref, sin_ref, lg_ref, wa2_ref, ba_ref, gr_ref, gg_ref, xres_ref, wout_ref, gnext_ref = ins
        o_ref, y_ref, st_ref, x_ref, h_ref = outs
        (s_ref,) = scr
        i = pl.program_id(0)

        @pl.when(i == 0)
        def _():
            s_ref[...] = jnp.zeros_like(s_ref)

        pr = pr_ref
        row, col, same = _attn_masks()
        m1 = col <= row
        m2 = jnp.logical_and(col > row, same)
        lane = lax.broadcasted_iota(jnp.int32, (1, QK_W), 1)
        blockmask = _state_block_mask()
        for grp in range(2):
            q, k, v, gate, b_cum, _, _ = _group_inputs(grp, pr, cos_ref[...], sin_ref[...], lg_ref[...],
                                                      wa2_ref[...], ba_ref[...])
            f = _decay_factors(q, k, b_cum)
            gn = gr_ref[...] if grp == 0 else gg_ref[...]
            s_prev = s_ref[grp]
            st_ref[0, grp] = s_prev
            o_inter = _dot(f["qs"].astype(BF16), s_prev.astype(BF16), "nt")
            kmb = f["km"].astype(BF16)
            kpb = f["kp"].astype(BF16)
            vb = v.astype(BF16)
            for h in range(HEADS):
                hm = (lane // DK) == h
                a1 = _dot(jnp.where(hm, f["qp"], 0.0).astype(BF16), kmb, "nt")
                a2 = _dot(jnp.where(hm, f["qm"], 0.0).astype(BF16), kpb, "nt")
                a = jnp.where(m1, a1, jnp.where(m2, a2, 0.0))
                lo = grp * V_W + h * DV
                o_h = _dot(a.astype(BF16), vb[:, h * DV:(h + 1) * DV], "nn") + o_inter[:, h * DV:(h + 1) * DV]
                o_ref[:, lo:lo + DV] = o_h
                r = lax.rsqrt(jnp.mean(o_h * o_h, axis=-1, keepdims=True) + RMS_EPS)
                gte = gate[:, h * DV:(h + 1) * DV]
                y = o_h * r * gn[:, h * DV:(h + 1) * DV] * (gte * _sigmoid(gte))
                y_ref[:, lo:lo + DV] = y.astype(y_ref.dtype)
            upd = _dot(vb, f["kl"].astype(BF16), "tn")
            s_ref[grp] = s_prev * f["ebl"] + jnp.where(blockmask, upd, 0.0)
        xv = xres_ref[...] + _dot(y_ref[...], wout_ref[...], "nn")
        x_ref[...] = xv
        r = lax.rsqrt(jnp.mean(xv * xv, axis=-1, keepdims=True) + RMS_EPS)
        h_ref[...] = (xv * r * gnext_ref[...]).astype(h_ref.dtype)

    const = lambda shape: pl.BlockSpec(shape, lambda i: tuple(0 for _ in shape))
    rows = lambda w: pl.BlockSpec((SUPER, w), lambda i: (i, 0))
    return _call(
        "attn_fwd", main, (n_s,),
        [rows(PROJ_W), rows(128), rows(128),
         const((1, QK_W)), const((GL_W, QK_W)), const((1, QK_W)), const((1, V_W)), const((1, V_W)),
         rows(D), const((2 * V_W, D)), const((1, D))],
        [rows(2 * V_W), rows(2 * V_W), pl.BlockSpec((1, 2, V_W, QK_W), lambda i: (i, 0, 0, 0)), rows(D), rows(D)],
        [jax.ShapeDtypeStruct((T, 2 * V_W), F32), jax.ShapeDtypeStruct((T, 2 * V_W), BF16),
         jax.ShapeDtypeStruct((n_s, 2, V_W, QK_W), F32), jax.ShapeDtypeStruct((T, D), F32),
         jax.ShapeDtypeStruct((T, D), BF16)],
        (proj, cos, sin_signed, lg, wa2p, ba, gn_ret, gn_gla, x_res, w_out, g_next),
        scratch=[pltpu.VMEM((2, V_W, QK_W), F32)], comm=comm)


def _attn_bwd(proj, cos, sin_signed, lg, wa2p, ba, gn_ret, gn_gla, o, dx, w_out, states, comm=None, after=None):
    T = proj.shape[0]
    n_s = T // SUPER
    D = dx.shape[1]

    def main(ins, outs, scr):
        pr_ref, cos_ref, sin_ref, lg_ref, wa2_ref, ba_ref, gr_ref, gg_ref, o_ref, dx_ref, wout_ref, st_ref = ins
        dp_ref, dgr_ref, dgg_ref, dba_ref, dwa_ref = outs
        (ds_ref, dy_ref) = scr
        i = pl.program_id(0)
        dy_ref[...] = _dot(dx_ref[...], wout_ref[...], "nt")

        @pl.when(i == 0)
        def _():
            ds_ref[...] = jnp.zeros_like(ds_ref)
            dgr_ref[...] = jnp.zeros_like(dgr_ref)
            dgg_ref[...] = jnp.zeros_like(dgg_ref)
            dba_ref[...] = jnp.zeros_like(dba_ref)
            dwa_ref[...] = jnp.zeros_like(dwa_ref)

        pr = pr_ref
        cos = cos_ref[...]
        sin_signed = sin_ref[...]
        row, col, same = _attn_masks()
        m1 = col <= row
        m2 = jnp.logical_and(col > row, same)
        m1t = row <= col
        m2t = jnp.logical_and(row > col, same)
        lane = lax.broadcasted_iota(jnp.int32, (1, QK_W), 1)
        blockmask = _state_block_mask()
        for grp in range(2):
            q, k, v, gate, b_cum, glow, logit = _group_inputs(grp, pr, cos, sin_signed, lg_ref[...],
                                                              wa2_ref[...], ba_ref[...])
            f = _decay_factors(q, k, b_cum)
            gn = gr_ref[...] if grp == 0 else gg_ref[...]
            dgn_ref = dgr_ref if grp == 0 else dgg_ref
            do_parts, dgate_parts, dgn_parts = [], [], []
            for h in range(HEADS):
                lo = grp * V_W + h * DV
                o_h = o_ref[:, lo:lo + DV]
                r = lax.rsqrt(jnp.mean(o_h * o_h, axis=-1, keepdims=True) + RMS_EPS)
                n = o_h * r
                gte = gate[:, h * DV:(h + 1) * DV]
                sg = _sigmoid(gte)
                dy_h = dy_ref[:, lo:lo + DV]
                gn_h = gn[:, h * DV:(h + 1) * DV]
                dgate_parts.append(dy_h * n * gn_h * (sg * (1.0 + gte * (1.0 - sg))))
                dz = dy_h * (gte * sg)
                dgn_parts.append(jnp.sum(dz * n, axis=0, keepdims=True))
                dn = dz * gn_h
                do_parts.append(r * (dn - n * jnp.mean(dn * n, axis=-1, keepdims=True)))
            dgn_ref[...] += jnp.concatenate(dgn_parts, axis=1)
            dgate = jnp.concatenate(dgate_parts, axis=1)
            do = jnp.concatenate(do_parts, axis=1)
            dob = do.astype(BF16)
            vb = v.astype(BF16)
            s_prev = st_ref[0, grp]
            ds_new = ds_ref[grp]
            dsb = ds_new.astype(BF16)
            qpb, qmb = f["qp"].astype(BF16), f["qm"].astype(BF16)
            kpb, kmb = f["kp"].astype(BF16), f["km"].astype(BF16)
            dqp = jnp.zeros((SUPER, QK_W), F32)
            dqm = jnp.zeros((SUPER, QK_W), F32)
            dkp = jnp.zeros((SUPER, QK_W), F32)
            dkm = jnp.zeros((SUPER, QK_W), F32)
            dv_parts = []
            for h in range(HEADS):
                hm = (lane // DK) == h
                qp_h = jnp.where(hm, f["qp"], 0.0).astype(BF16)
                qm_h = jnp.where(hm, f["qm"], 0.0).astype(BF16)
                kp_h = jnp.where(hm, f["kp"], 0.0).astype(BF16)
                km_h = jnp.where(hm, f["km"], 0.0).astype(BF16)
                at = jnp.where(m1t, _dot(km_h, qpb, "nt"), jnp.where(m2t, _dot(kp_h, qmb, "nt"), 0.0))
                do_h = dob[:, h * DV:(h + 1) * DV]
                v_h = vb[:, h * DV:(h + 1) * DV]
                dv_parts.append(_dot(at.astype(BF16), do_h, "nn"))
                da = _dot(do_h, v_h, "nt")
                dat = _dot(v_h, do_h, "nt")
                da1 = jnp.where(m1, da, 0.0).astype(BF16)
                da2 = jnp.where(m2, da, 0.0).astype(BF16)
                da1t = jnp.where(m1t, dat, 0.0).astype(BF16)
                da2t = jnp.where(m2t, dat, 0.0).astype(BF16)
                dqp = dqp + _dot(da1, km_h, "nn")
                dqm = dqm + _dot(da2, kp_h, "nn")
                dkm = dkm + _dot(da1t, qp_h, "nn")
                dkp = dkp + _dot(da2t, qm_h, "nn")
            klb = f["kl"].astype(BF16)
            qsb = f["qs"].astype(BF16)
            dqs = _dot(dob, s_prev.astype(BF16), "nn")
            dkl = _dot(vb, dsb, "nn")
            dv = jnp.concatenate(dv_parts, axis=1) + _dot(klb, dsb, "nt")
            ds_ref[grp] = ds_new * f["ebl"] + jnp.where(blockmask, _dot(dob, qsb, "tn"), 0.0)
            dq = dqp * f["e1"] + dqm * f["e2"] + dqs * f["eb"]
            dk = dkm * f["e2"] + dkp * f["e1"] + dkl * f["el"]
            if grp == 0:
                dq = _rotary_transposed(dq, cos, sin_signed)
                dk = _rotary_transposed(dk * (DK ** -0.5), cos, sin_signed)
                dp_ref[:, C_RQ:C_RQ + QK_W] = dq.astype(dp_ref.dtype)
                dp_ref[:, C_RK:C_RK + QK_W] = dk.astype(dp_ref.dtype)
                dp_ref[:, C_RV:C_RV + V_W] = dv.astype(dp_ref.dtype)
                dp_ref[:, C_RG:C_RG + V_W] = dgate.astype(dp_ref.dtype)
            else:
                dkl_kl = dkl * klb.astype(F32)
                db = (dqp * qpb.astype(F32) - dkm * kmb.astype(F32) - dqm * qmb.astype(F32)
                      + dkp * kpb.astype(F32) + dqs * qsb.astype(F32) - dkl_kl)
                last = (jnp.sum(dkl_kl, axis=0, keepdims=True)
                        + f["ebl"] * jnp.sum(s_prev * ds_new, axis=0, keepdims=True))
                rowq = lax.broadcasted_iota(jnp.int32, (SUPER, QK_W), 0)
                db = db + jnp.where(rowq == SUPER - 1, last, 0.0)
                upper = (col >= row).astype(F32)
                dla = _dot(upper, db, "nn", precision=lax.Precision.HIGHEST)
                dlogit = dla * (1.0 / GATE_NORM) * (1.0 - _sigmoid(logit))
                dlb = dlogit.astype(BF16)
                dglow = _dot(dlb, wa2_ref[...].astype(BF16), "nt")
                dwa_ref[...] += _dot(glow.astype(BF16), dlb, "tn")
                dba_ref[...] += jnp.sum(dlogit, axis=0, keepdims=True)
                dp_ref[:, C_GQ:C_GQ + QK_W] = (dq * (DK ** -0.5)).astype(dp_ref.dtype)
                dp_ref[:, C_GK:C_GK + QK_W] = dk.astype(dp_ref.dtype)
                dp_ref[:, C_GV:C_GV + V_W] = dv.astype(dp_ref.dtype)
                dp_ref[:, C_GG:C_GG + V_W] = dgate.astype(dp_ref.dtype)
                dp_ref[:, C_GL:C_GL + GL_W] = dglow.astype(dp_ref.dtype)

    rev = lambda i: n_s - 1 - i
    const = lambda shape: pl.BlockSpec(shape, lambda i: tuple(0 for _ in shape))
    return _call(
        "attn_bwd", main, (n_s,),
        [pl.BlockSpec((SUPER, PROJ_W), lambda i: (rev(i), 0)),
         pl.BlockSpec((SUPER, 128), lambda i: (rev(i), 0)), pl.BlockSpec((SUPER, 128), lambda i: (rev(i), 0)),
         const((1, QK_W)), const((GL_W, QK_W)), const((1, QK_W)), const((1, V_W)), const((1, V_W)),
         pl.BlockSpec((SUPER, 2 * V_W), lambda i: (rev(i), 0)),
         pl.BlockSpec((SUPER, D), lambda i: (rev(i), 0)), const((2 * V_W, D)),
         pl.BlockSpec((1, 2, V_W, QK_W), lambda i: (rev(i), 0, 0, 0))],
        [pl.BlockSpec((SUPER, PROJ_W), lambda i: (rev(i), 0)),
         const((1, V_W)), const((1, V_W)), const((1, QK_W)), const((GL_W, QK_W))],
        [jax.ShapeDtypeStruct((T, PROJ_W), BF16),
         jax.ShapeDtypeStruct((1, V_W), F32), jax.ShapeDtypeStruct((1, V_W), F32),
         jax.ShapeDtypeStruct((1, QK_W), F32), jax.ShapeDtypeStruct((GL_W, QK_W), F32)],
        (proj, cos, sin_signed, lg, wa2p, ba, gn_ret, gn_gla, o, dx, w_out, states),
        scratch=[pltpu.VMEM((2, V_W, QK_W), F32), pltpu.VMEM((SUPER, 2 * V_W), F32)], comm=comm, after=after)


def _rotary_tables(T):
    half = DK // 2
    inv = ROPE_BASE ** (-jnp.arange(half, dtype=F32) * 2.0 / DK)
    ang = jnp.arange(T, dtype=F32)[:, None] * inv[None, :]
    cos, sin = jnp.cos(ang), jnp.sin(ang)
    cos_head = jnp.concatenate([cos, cos], axis=1)
    sin_head = jnp.concatenate([-sin, sin], axis=1)
    return jnp.tile(cos_head, (1, 128 // DK)), jnp.tile(sin_head, (1, 128 // DK))


def _sum_devices(name, gathered, m_per):
    def body(g_ref, o_ref):
        acc = g_ref[0:m_per, :]
        for k in range(1, N_DEV):
            acc = acc + g_ref[k * m_per:(k + 1) * m_per, :]
        o_ref[...] = acc

    return pl.pallas_call(body, name=name, out_shape=jax.ShapeDtypeStruct((m_per, 128), F32))(gathered)


def _owner_sums(name, items, owner, comm=None):
    counts = [1 + len(landed) for _, landed in items]

    def main(ins, outs, scr):
        at = 0
        for o_ref, n in zip(outs, counts):
            acc = ins[at][...].astype(F32)
            for l_ref in ins[at + 1:at + n]:
                for j in range(l_ref.shape[0]):
                    acc = acc + l_ref[j].astype(F32)
            o_ref[...] = acc
            at += n

    once = pl.Buffered(1)
    in_specs, out_specs, out_shape, args = [], [], [], []
    for grad, landed in items:
        R, C = grad.shape[-2:]
        in_specs.append(pl.BlockSpec((None, None, R, C), lambda i, s: (s[0], s[1], 0, 0), pipeline_mode=once))
        in_specs += [pl.BlockSpec(tuple(l.shape), lambda i, s: (0, 0, 0), pipeline_mode=once) for l in landed]
        out_specs.append(pl.BlockSpec((R, C), lambda i, s: (0, 0)))
        out_shape.append(jax.ShapeDtypeStruct((R, C), F32))
        args += [grad, *landed]
    return _call(name, main, (1,), in_specs, out_specs, out_shape, args, comm=comm, prefetch=owner)


def _adamw_group(name, items, n_blocks, comm=None):
    n = len(items)

    def main(ins, outs, scr):
        for p in range(n):
            g_ref, w_ref, m_ref, v_ref = ins[4 * p:4 * p + 4]
            d_ref, nm_ref, nv_ref = outs[3 * p:3 * p + 3]
            gv = g_ref[...]
            nm = ADAM_B1 * m_ref[...] + (1.0 - ADAM_B1) * gv
            nv = ADAM_B2 * v_ref[...] + (1.0 - ADAM_B2) * (gv * gv)
            m_hat = nm / (1.0 - ADAM_B1 ** ADAM_STEP)
            v_hat = nv / (1.0 - ADAM_B2 ** ADAM_STEP)
            d_ref[...] = -ADAM_LR * (m_hat / (jnp.sqrt(v_hat) + ADAM_EPS) + ADAM_WD * w_ref[...])
            nm_ref[...] = nm
            nv_ref[...] = nv

    in_specs, out_specs, out_shape, args = [], [], [], []
    for item in items:
        R, C = item[1].shape
        assert R % n_blocks == 0
        spec = pl.BlockSpec((R // n_blocks, C), lambda i: (i, 0))
        in_specs += [spec] * 4
        out_specs += [spec] * 3
        out_shape += [jax.ShapeDtypeStruct((R, C), F32)] * 3
        args += list(item)
    outs, extra = _call(name, main, (n_blocks,), in_specs, out_specs, out_shape, args, comm=comm)
    return [tuple(outs[3 * p:3 * p + 3]) for p in range(n)], extra


SMALL_ORDER = ("ffn1", "mix", "ffn2", "final", "ret", "gla", "b_a")


def kernel(x, ffn1_norm_g, ffn1_w_gate, ffn1_w_up, ffn1_w_down, mix_norm_g, w_in, ret_norm_g, gla_w_a2, gla_b_a, gla_norm_g, w_out, ffn2_norm_g, ffn2_w_gate, ffn2_w_up, ffn2_w_down, final_norm_g, loss_target, m_ffn1_norm_g, m_ffn1_w_gate, m_ffn1_w_up, m_ffn1_w_down, m_mix_norm_g, m_w_in, m_ret_norm_g, m_gla_w_a2, m_gla_b_a, m_gla_norm_g, m_w_out, m_ffn2_norm_g, m_ffn2_w_gate, m_ffn2_w_up, m_ffn2_w_down, m_final_norm_g, v_ffn1_norm_g, v_ffn1_w_gate, v_ffn1_w_up, v_ffn1_w_down, v_mix_norm_g, v_w_in, v_ret_norm_g, v_gla_w_a2, v_gla_b_a, v_gla_norm_g, v_w_out, v_ffn2_norm_g, v_ffn2_w_gate, v_ffn2_w_up, v_ffn2_w_down, v_final_norm_g):
    xi, yi, ci = _coords()
    dev = 4 * xi + 2 * yi + ci
    owner = jnp.stack([2 * xi + yi, ci]).astype(jnp.int32)

    x0, target = x[0], loss_target[0]
    T, D = x0.shape
    fb = ffn1_w_gate.shape[2]
    ib = w_in.shape[2]
    ab = gla_w_a2.shape[2]
    F = N_DEV * fb
    cos, sin_signed = _rotary_tables(T)
    lg = jnp.repeat(jnp.log(1.0 - 2.0 ** (-5.0 - jnp.arange(HEADS, dtype=F32))), DK)[None, :]
    g_final = final_norm_g.reshape(1, D)

    g1_loc = ffn1_w_gate[0].T[None].astype(BF16)
    u1_loc = ffn1_w_up[0].T[None].astype(BF16)
    d1_loc = ffn1_w_down.astype(BF16)
    g2_loc = ffn2_w_gate[0].T[None].astype(BF16)
    u2_loc = ffn2_w_up[0].T[None].astype(BF16)
    d2_loc = ffn2_w_down.astype(BF16)
    in_loc = w_in[0].T.astype(BF16)
    out_loc = w_out[0].astype(BF16)

    h1, (g1,) = _rms_fwd("ffn1_rms", x0, ffn1_norm_g, comm=_AllGather([g1_loc], ["stack"]))
    g1 = g1.reshape(1, F, D)
    (dsl1, sl1), (u1,) = _mm_nstream("ffn1_gate", h1, [g1], [0], "nt", [], [BF16, BF16], _gate_parts_epilogue, cn=256,
                                     comm=_AllGather([u1_loc], ["stack"]))
    u1 = u1.reshape(1, F, D)
    (dsu1, act1), (d1,) = _mm_nstream("ffn1_up", h1, [u1], [0], "nt", [dsl1, sl1], [BF16, BF16],
                                      _up_act_epilogue, cn=256, comm=_AllGather([d1_loc], ["stack"]))
    d1 = d1.reshape(1, F, D)
    f32_tile, bf16_tile, f32_vec = (F32, "tile"), (BF16, "tile"), (F32, "vec")
    (x1, h2), (in_all, a_all) = _mm_mstream(
        "ffn1_down", [act1], [d1], [0], "nn", [(x0, "tile"), (mix_norm_g, "vec")], [f32_tile, bf16_tile],
        _residual_rms_epilogue(0.5), comm=_AllGather([in_loc, gla_w_a2[0]], ["plain", "plain"]))
    w_in_t = jnp.pad(in_all.reshape(1, N_DEV * ib, D), ((0, 0), (0, PROJ_W - N_DEV * ib), (0, 0)))
    wa2 = jnp.transpose(a_all, (1, 0, 2)).reshape(GATE_RANK, N_DEV * ab)
    wa2p = jnp.pad(wa2, ((0, GL_W - GATE_RANK), (0, 0)))

    (proj,), (g2, out_all) = _mm_nstream("mix_proj", h2, [w_in_t], [0], "nt", [], [BF16], _identity_epilogue, cn=640,
                                         comm=_AllGather([g2_loc, out_loc], ["stack", "plain"]))
    w_out_full = out_all.reshape(D, D)
    (o, ymix, states, x2, h3), (u2,) = _attn_fwd(proj, cos, sin_signed, lg, wa2p, gla_b_a, ret_norm_g, gla_norm_g,
                                                 x1, w_out_full, ffn2_norm_g, comm=_AllGather([u2_loc], ["stack"]))
    g2, u2 = g2.reshape(1, F, D), u2.reshape(1, F, D)

    (dsu2, sl2, act2), (d2,) = _mm_nstream(
        "ffn2_up", h3, [g2, u2], [0, 0], "nt", [], [BF16, BF16, BF16], _silu_mul_epilogue, cn=256,
        comm=_AllGather([d2_loc], ["stack"]))
    d2 = d2.reshape(1, F, D)
    (dx3, dy3b, d_final, loss), _ = _mm_mstream(
        "ffn2_down", [act2], [d2], [0], "nn", [(x2, "tile"), (g_final, "vec"), (target, "tile")],
        [f32_tile, bf16_tile, f32_vec, f32_vec], _final_loss_epilogue(0.5, 0.5))

    sent = {}

    def riding(nm, grad):
        return nm, _StartSends([grad])

    def hosted(host, outs):
        sent[host[0]] = tuple(outs) + (None,)

    dwd2, _ = _mm_tn("ffn2b_dwd", act2, dy3b, F // 2, D, BF16)
    host = riding("wd2", dwd2.reshape(4, 2, fb, D))
    (dgate2, dup2), st = _mm_nstream("ffn2b_dact", dy3b, [d2], [0], "nt", [dsu2, sl2], [BF16, BF16],
                                     _dact_epilogue, cn=256, comm=host[1])
    hosted(host, st)
    dwg2, _ = _mm_tn("ffn2b_dwg", dgate2, h3, F // 2, D, BF16)
    host = riding("wg2", dwg2.reshape(4, 2, fb, D))
    dwu2, st = _mm_tn("ffn2b_dwu", dup2, h3, F // 2, D, BF16, comm=host[1])
    hosted(host, st)
    host = riding("wu2", dwu2.reshape(4, 2, fb, D))
    rms_outs = [f32_tile, bf16_tile, f32_vec]
    (dx2, dx2b, d_g2), st = _mm_mstream(
        "ffn2b_dh", [dgate2, dup2], [g2, u2], [0, 0], "nn", [(x2, "tile"), (ffn2_norm_g, "vec"), (dx3, "tile")],
        rms_outs, _rms_bwd_epilogue(1.0), comm=host[1])
    hosted(host, st)

    dwout, _ = _mm_tn("mixb_dwout", ymix, dx2b, D, D, BF16)
    host = riding("wout", dwout.reshape(4, 2, D // N_DEV, D))
    (dproj, d_ret, d_gla, d_ba, d_wa2p), st = _attn_bwd(
        proj, cos, sin_signed, lg, wa2p, gla_b_a, ret_norm_g, gla_norm_g, o, dx2b, w_out_full, states, comm=host[1])
    hosted(host, st)
    dwin_t, _ = _mm_tn("mixb_dwin", dproj, h2, 640, D, BF16, tk=2048)
    host = riding("win", dwin_t[:N_DEV * ib].reshape(4, 2, ib, D))
    (dx1, dy1b, d_gmix), st = _mm_mstream(
        "mixb_dh", [dproj], [w_in_t], [0], "nn", [(x1, "tile"), (mix_norm_g, "vec"), (dx2, "tile")],
        rms_outs, _rms_bwd_epilogue(0.5), comm=host[1])
    hosted(host, st)

    dwd1, _ = _mm_tn("ffn1b_dwd", act1, dy1b, F // 2, D, BF16)
    host = riding("wd1", dwd1.reshape(4, 2, fb, D))
    (dgate1, dup1), st = _mm_nstream("ffn1b_dact", dy1b, [d1], [0], "nt", [dsu1, sl1], [BF16, BF16],
                                     _dact_epilogue, cn=256, comm=host[1])
    hosted(host, st)
    dwg1, _ = _mm_tn("ffn1b_dwg", dgate1, h1, F // 2, D, BF16)
    host = riding("wg1", dwg1.reshape(4, 2, fb, D))
    dwu1, st = _mm_tn("ffn1b_dwu", dup1, h1, F // 2, D, BF16, comm=host[1])
    hosted(host, st)
    sent["wu1"] = _send_to_owners("send_wu1", dwu1.reshape(4, 2, fb, D))
    (dx0, _, d_g1), _ = _mm_mstream(
        "ffn1b_dh", [dgate1, dup1], [g1, u1], [0, 0], "nn", [(x0, "tile"), (ffn1_norm_g, "vec"), (dx1, "tile")],
        rms_outs, _rms_bwd_epilogue(1.0), after=sent["wu1"][4])

    small = dict(ffn1=d_g1, mix=d_gmix, ffn2=d_g2, final=d_final, ret=d_ret, gla=d_gla, b_a=d_ba)
    flat = jnp.concatenate([small[k].reshape(-1) for k in SMALL_ORDER]
                           + [d_wa2p[:GATE_RANK].reshape(-1), loss[0, :128]])
    rows = -(-flat.shape[0] // 128)
    rows = -(-rows // 8) * 8
    packed = jnp.pad(flat, (0, rows * 128 - flat.shape[0])).reshape(rows, 128)

    transposed = ("ffn1_w_gate", "ffn1_w_up", "ffn2_w_gate", "ffn2_w_up", "w_in")

    def to_2d(nm, a):
        if nm in transposed:
            return a[0].T
        return a.reshape((1, a.shape[0]) if a.ndim == 1 else a.shape[-2:])

    def from_2d(nm, a):
        return a.T[None] if nm in transposed else a.reshape(params[nm][0].shape)

    def arrived(nm, after):
        grad, landed = _await_owners("await_" + nm, sent[nm], after)
        return grad, [landed]

    sums_a, (gathered,) = _owner_sums(
        "sum_a", [arrived(nm, dx0) for nm in ("wg2", "wu2", "wd2", "win", "wout")], owner,
        comm=_AllGather([packed], ["plain"]))
    params = dict(
        ffn2_w_gate=(ffn2_w_gate, m_ffn2_w_gate, v_ffn2_w_gate), ffn2_w_up=(ffn2_w_up, m_ffn2_w_up, v_ffn2_w_up),
        ffn2_w_down=(ffn2_w_down, m_ffn2_w_down, v_ffn2_w_down), w_in=(w_in, m_w_in, v_w_in),
        w_out=(w_out, m_w_out, v_w_out), ffn1_w_gate=(ffn1_w_gate, m_ffn1_w_gate, v_ffn1_w_gate),
        ffn1_w_up=(ffn1_w_up, m_ffn1_w_up, v_ffn1_w_up), ffn1_w_down=(ffn1_w_down, m_ffn1_w_down, v_ffn1_w_down),
        ffn1_norm_g=(ffn1_norm_g, m_ffn1_norm_g, v_ffn1_norm_g), mix_norm_g=(mix_norm_g, m_mix_norm_g, v_mix_norm_g),
        ret_norm_g=(ret_norm_g, m_ret_norm_g, v_ret_norm_g), gla_w_a2=(gla_w_a2, m_gla_w_a2, v_gla_w_a2),
        gla_b_a=(gla_b_a, m_gla_b_a, v_gla_b_a), gla_norm_g=(gla_norm_g, m_gla_norm_g, v_gla_norm_g),
        ffn2_norm_g=(ffn2_norm_g, m_ffn2_norm_g, v_ffn2_norm_g), final_norm_g=(final_norm_g, m_final_norm_g, v_final_norm_g))
    grads, updates = {}, {}

    def run_adam(name, names, grad_2d, n_blocks):
        items = [(grad_2d[nm],) + tuple(to_2d(nm, a) for a in params[nm]) for nm in names]
        res, _ = _adamw_group(name, items, n_blocks)
        for nm, r in zip(names, res):
            grads[nm] = from_2d(nm, grad_2d[nm])
            updates[nm] = tuple(from_2d(nm, a) for a in r)
        return res

    grads_a = {"ffn2_w_gate": sums_a[0], "ffn2_w_up": sums_a[1], "ffn2_w_down": sums_a[2], "w_out": sums_a[4]}
    run_adam("adamw_w_in", ["w_in"], {"w_in": sums_a[3]}, 1)
    done_a = run_adam("adamw_a", list(grads_a), grads_a, 4)[0][0]
    sums_b, _ = _owner_sums("sum_b", [arrived(nm, done_a) for nm in ("wg1", "wu1", "wd1")], owner)
    grads_b = {"ffn1_w_gate": sums_b[0], "ffn1_w_up": sums_b[1], "ffn1_w_down": sums_b[2]}
    run_adam("adamw_b", list(grads_b), grads_b, 4)

    total = _sum_devices("sum_small", gathered.reshape(N_DEV * rows, 128), rows).reshape(-1)
    sizes = [small[k].size for k in SMALL_ORDER] + [GATE_RANK * QK_W, 128]
    offs = [0]
    for s in sizes:
        offs.append(offs[-1] + s)
    pieces = [total[offs[i]:offs[i + 1]] for i in range(len(sizes))]
    g_small = {k: pieces[i].reshape(small[k].shape) for i, k in enumerate(SMALL_ORDER)}
    g_wa2_full = pieces[len(SMALL_ORDER)].reshape(GATE_RANK, QK_W)
    g_wa2 = lax.dynamic_slice(g_wa2_full, (0, dev * ab), (GATE_RANK, ab))
    loss_total = pieces[len(SMALL_ORDER) + 1][0]

    small_grads = {"ffn1_norm_g": g_small["ffn1"], "mix_norm_g": g_small["mix"], "ret_norm_g": g_small["ret"],
                   "gla_w_a2": g_wa2, "gla_b_a": g_small["b_a"], "gla_norm_g": g_small["gla"],
                   "ffn2_norm_g": g_small["ffn2"], "final_norm_g": g_small["final"]}
    run_adam("adamw_small", list(small_grads), small_grads, 1)

    order = ("ffn1_norm_g", "ffn1_w_gate", "ffn1_w_up", "ffn1_w_down", "mix_norm_g", "w_in", "ret_norm_g", "gla_w_a2",
             "gla_b_a", "gla_norm_g", "w_out", "ffn2_norm_g", "ffn2_w_gate", "ffn2_w_up", "ffn2_w_down", "final_norm_g")
    return (loss_total, dx0[None], *[grads[nm] for nm in order], *[updates[nm][0] for nm in order],
            *[updates[nm][1] for nm in order], *[updates[nm][2] for nm in order])
```

```python
import functools
import math

import jax
import jax.numpy as jnp
from jax import lax
from jax.experimental import pallas as pl
from jax.experimental.pallas import tpu as pltpu

F32 = jnp.float32
BF16 = jnp.bfloat16
MESH = pl.DeviceIdType.MESH
HBM = pl.BlockSpec(memory_space=pltpu.HBM)

N_DEV = 8
RMS_EPS = 1e-6
ROPE_BASE = 10000.0
HEADS = 4
DK = 64
DV = 128
QK_W = HEADS * DK
V_W = HEADS * DV
GATE_RANK = 16
GATE_NORM = 16.0
CHUNK = 64
SUPER = 256
PROJ_W = 3200
C_RQ, C_RK, C_RV, C_RG, C_GQ, C_GK, C_GV, C_GG, C_GL = 0, 256, 512, 1024, 1536, 1792, 2048, 2560, 3072
GL_W = PROJ_W - C_GL
ADAM_LR, ADAM_B1, ADAM_B2, ADAM_EPS, ADAM_WD, ADAM_STEP = 0.001, 0.9, 0.999, 1e-08, 0.01, 10
VMEM_LIMIT_V7X = 52 * 1024 * 1024


def _cparams(**kw):
    return pltpu.CompilerParams(vmem_limit_bytes=VMEM_LIMIT_V7X, **kw)


def _dot(a, b, form, precision=None):
    dims = {"nn": (((1,), (0,)), ((), ())), "nt": (((1,), (1,)), ((), ())), "tn": (((0,), (0,)), ((), ()))}[form]
    return lax.dot_general(a, b, dims, preferred_element_type=F32, precision=precision)


def _sigmoid(x):
    return 1.0 / (1.0 + jnp.exp(-x))


def _coords():
    return lax.axis_index("x"), lax.axis_index("y"), lax.axis_index("c")


class _NoComm:
    inputs, out_shapes, scratch = (), (), ()


class _AllGather:
    def __init__(self, arrays, kinds):
        self.inputs = tuple(arrays)
        self.kinds = tuple(kinds)
        n = len(arrays)
        self.out_shapes = tuple(
            jax.ShapeDtypeStruct((a.shape[0], N_DEV) + a.shape[1:] if k == "stack" else (N_DEV,) + a.shape, a.dtype)
            for a, k in zip(arrays, kinds))
        self.scratch = (pltpu.SemaphoreType.DMA((n, 7)), pltpu.SemaphoreType.DMA((n, 7)),
                        pltpu.SemaphoreType.DMA((n,)))

    def _ctx(self, srcs, outs, sems):
        send_sems, recv_sems, local_sems = sems
        x, y, c = _coords()
        me, sibling = (x, y, c), (x, y, 1 - c)
        chips = [(1 - x, y), (x, 1 - y), (1 - x, 1 - y)]

        def blk(m, dev):
            k = 4 * dev[0] + 2 * dev[1] + dev[2]
            return outs[m].at[:, k] if self.kinds[m] == "stack" else outs[m].at[k]

        def copy(m, s, block, to, src=None):
            return pltpu.make_async_remote_copy(
                src_ref=blk(m, block) if src is None else src, dst_ref=blk(m, block),
                send_sem=send_sems.at[m, s], recv_sem=recv_sems.at[m, s], device_id=to, device_id_type=MESH)

        def mine(m):
            return pltpu.make_async_copy(srcs[m], blk(m, me), local_sems.at[m])

        def first(m):
            return [copy(m, 0, me, sibling, src=srcs[m])] + [
                copy(m, 1 + j, me, (*chip, c), src=srcs[m]) for j, chip in enumerate(chips)]

        return me, sibling, chips, c, copy, mine, first

    def start(self, srcs, outs, sems):
        me, sibling, chips, c, copy, mine, first = self._ctx(srcs, outs, sems)
        for m in range(len(srcs)):
            mine(m).start()
            for cp in first(m):
                cp.start()

    def mid(self, srcs, outs, sems):
        me, sibling, chips, c, copy, mine, first = self._ctx(srcs, outs, sems)
        for j, chip in enumerate(chips):
            for m in range(len(srcs)):
                copy(m, 1 + j, (*chip, c), me).wait_recv()
                copy(m, 4 + j, (*chip, c), sibling).start()

    def finish(self, srcs, outs, sems):
        me, sibling, chips, c, copy, mine, first = self._ctx(srcs, outs, sems)
        for m in range(len(srcs)):
            copy(m, 0, sibling, me).wait_recv()
            for j, chip in enumerate(chips):
                copy(m, 4 + j, (*chip, 1 - c), me).wait_recv()
            for cp in first(m):
                cp.wait_send()
            for j, chip in enumerate(chips):
                copy(m, 4 + j, (*chip, c), sibling).wait_send()
            mine(m).wait()


RELATIONS = ((0, 0, 1), (1, 0, 0), (0, 1, 0), (1, 1, 0), (1, 0, 1), (0, 1, 1), (1, 1, 1))
SEM = pl.BlockSpec(memory_space=pltpu.SEMAPHORE)
SPLIT_PARAMS = dict(has_side_effects=pltpu.SideEffectType.DATAFLOW_SIDE_EFFECTING)


def _owner_copies(grad_ref, land_ref, send_sems, recv_sems):
    x, y, c = _coords()
    copies = []
    for s, (fx, fy, fc) in enumerate(RELATIONS):
        px = 1 - x if fx else x
        py = 1 - y if fy else y
        pc = 1 - c if fc else c
        copies.append(pltpu.make_async_remote_copy(
            src_ref=grad_ref.at[2 * px + py, pc], dst_ref=land_ref.at[s], send_sem=send_sems.at[s],
            recv_sem=recv_sems.at[s], device_id=(px, py, pc), device_id_type=MESH))
    return copies


def _send_to_owners(name, grad):
    n = len(RELATIONS)
    land_shape = (n,) + grad.shape[2:]

    def body(g_ref, land_ref, send_sems, recv_sems, g_thru, land_thru, token):
        for cp in _owner_copies(g_ref, land_ref, send_sems, recv_sems):
            cp.start()
        token[...] = jnp.zeros_like(token)

    return pl.pallas_call(
        body, name=name,
        out_shape=(pltpu.SemaphoreType.DMA((n,)), pltpu.SemaphoreType.DMA((n,)), pltpu.HBM(grad.shape, grad.dtype),
                   pltpu.HBM(land_shape, grad.dtype), jax.ShapeDtypeStruct((8, 128), F32)),
        in_specs=(HBM, HBM), out_specs=(SEM, SEM, HBM, HBM, pl.BlockSpec(memory_space=pltpu.VMEM)),
        input_output_aliases={0: 2, 1: 3}, compiler_params=pltpu.CompilerParams(**SPLIT_PARAMS),
    )(pltpu.with_memory_space_constraint(grad, pltpu.HBM),
      pltpu.with_memory_space_constraint(lax.empty(land_shape, grad.dtype), pltpu.HBM))


def _await_owners(name, started, after):
    send_sems, recv_sems, g_thru, land_thru, _ = started

    def body(g_ref, land_ref, send_sems, recv_sems, after_ref, g_out, land_out):
        for cp in _owner_copies(g_ref, land_ref, send_sems, recv_sems):
            cp.wait_send()
            cp.wait_recv()

    return pl.pallas_call(
        body, name=name, out_shape=(pltpu.HBM(g_thru.shape, g_thru.dtype), pltpu.HBM(land_thru.shape, land_thru.dtype)),
        in_specs=(HBM, HBM, SEM, SEM, pl.BlockSpec(memory_space=pl.ANY)), out_specs=(HBM, HBM),
        input_output_aliases={0: 0, 1: 1}, compiler_params=pltpu.CompilerParams(**SPLIT_PARAMS),
    )(g_thru, land_thru, send_sems, recv_sems, after)


def _call(name, main, grid, in_specs, out_specs, out_shape, args, scratch=(), comm=None, prefetch=None, after=None):
    comm = comm or _NoComm()
    n_main = len(in_specs)
    if after is not None:
        in_specs = list(in_specs) + [pl.BlockSpec(after.shape, lambda *_: (0,) * after.ndim)]
        args = tuple(args) + (after,)
    counts = [len(in_specs), len(comm.inputs), len(out_shape), len(comm.out_shapes), len(scratch), len(comm.scratch)]
    n_steps = math.prod(grid)
    hosted = bool(comm.inputs)

    def body(*refs):
        if prefetch is not None:
            refs = refs[1:]
        parts, at = [], 0
        for n in counts:
            parts.append(refs[at:at + n])
            at += n
        ins, c_in, outs, c_out, scr, c_scr = parts
        ins = ins[:n_main]
        step = pl.program_id(0)
        for d in range(1, len(grid)):
            step = step * grid[d] + pl.program_id(d)
        if hosted:
            @pl.when(step == 0)
            def _():
                comm.start(c_in, c_out, c_scr)
        main(ins, outs, scr)
        if hosted:
            @pl.when(step == max(n_steps - 2, 0))
            def _():
                comm.mid(c_in, c_out, c_scr)

            @pl.when(step == n_steps - 1)
            def _():
                comm.finish(c_in, c_out, c_scr)

    all_in = list(in_specs) + [HBM] * counts[1]
    all_out = list(out_specs) + [HBM] * counts[3]
    all_scratch = list(scratch) + list(comm.scratch)
    shapes = list(out_shape) + list(comm.out_shapes)
    if prefetch is None:
        res = pl.pallas_call(body, name=name, grid=grid, in_specs=all_in, out_specs=all_out, out_shape=shapes,
                             scratch_shapes=all_scratch, compiler_params=_cparams())(*args, *comm.inputs)
    else:
        res = pl.pallas_call(
            body, name=name, out_shape=shapes,
            grid_spec=pltpu.PrefetchScalarGridSpec(num_scalar_prefetch=1, grid=grid, in_specs=all_in,
                                                   out_specs=all_out, scratch_shapes=all_scratch),
            compiler_params=_cparams())(prefetch, *args, *comm.inputs)
    return res[:counts[2]], res[counts[2]:]


def _rms_fwd(name, x, g, comm=None):
    T, D = x.shape
    tm = min(T, 512)

    def main(ins, outs, scr):
        x_ref, g_ref = ins
        xv = x_ref[...]
        r = lax.rsqrt(jnp.mean(xv * xv, axis=-1, keepdims=True) + RMS_EPS)
        outs[0][...] = (xv * r * g_ref[...]).astype(outs[0].dtype)

    tile = pl.BlockSpec((tm, D), lambda i: (i, 0))
    (h,), extra = _call(name, main, (T // tm,), [tile, pl.BlockSpec((1, D), lambda i: (0, 0))], [tile],
                        [jax.ShapeDtypeStruct((T, D), BF16)], (x, g), comm=comm)
    return h, extra


def _final_loss_epilogue(scale, out_scale):
    def ep(acc, ex, outs):
        res_ref, g_ref, t_ref = ex
        dx_ref, dxb_ref, dg_ref, loss_ref = outs
        n = acc.shape[-1]
        xv = res_ref[...] + scale * acc
        r = lax.rsqrt(jnp.mean(xv * xv, axis=-1, keepdims=True) + RMS_EPS)
        xhat = xv * r
        err = xhat * g_ref[...] - t_ref[...]

        @pl.when(pl.program_id(0) == 0)
        def _():
            dg_ref[...] = jnp.zeros_like(dg_ref)
            loss_ref[...] = jnp.zeros_like(loss_ref)

        loss_ref[...] += jnp.broadcast_to(jnp.sum(err * err) * (0.5 / n), loss_ref.shape)
        dy = err * (1.0 / n)
        dg_ref[...] += jnp.sum(dy * xhat, axis=0, keepdims=True)
        dxhat = dy * g_ref[...]
        dx = r * (dxhat - xhat * jnp.mean(dxhat * xhat, axis=-1, keepdims=True))
        dx_ref[...] = dx
        dxb_ref[...] = (out_scale * dx).astype(dxb_ref.dtype)
    return ep


def _mm_nstream(name, a, ws, w_sel, w_form, comps, out_dtypes, epilogue, cn, rows=1024, comm=None, after=None):
    T, K = a.shape
    N = ws[0].shape[1]
    rows = min(rows, T)
    assert N % cn == 0 and T % rows == 0
    n_w, n_c = len(ws), len(comps)

    def main(ins, outs, scr):
        a_ref = ins[0]
        w_refs = ins[1:1 + n_w]
        c_refs = ins[1 + n_w:1 + n_w + n_c]

        for r in range(T // rows):
            sl = slice(r * rows, (r + 1) * rows)
            a_blk = a_ref[sl, :]
            dots = [_dot(a_blk, w_ref[...], w_form) for w_ref in w_refs]
            res = epilogue(dots, [c_ref[sl, :] for c_ref in c_refs])
            for o_ref, o in zip(outs, res):
                o_ref[sl, :] = o.astype(o_ref.dtype)

    if w_form == "nt":
        w_specs = [pl.BlockSpec((None, cn, K), functools.partial(lambda j, s: (s, j, 0), s=s)) for s in w_sel]
    else:
        w_specs = [pl.BlockSpec((K, cn), lambda j: (0, j)) for _ in ws]
    chunk = pl.BlockSpec((T, cn), lambda j: (0, j))
    return _call(name, main, (N // cn,), [pl.BlockSpec((T, K), lambda j: (0, 0))] + w_specs + [chunk] * n_c,
                 [chunk] * len(out_dtypes), [jax.ShapeDtypeStruct((T, N), dt) for dt in out_dtypes],
                 (a, *ws, *comps), comm=comm, after=after)


def _mm_mstream(name, as_, ws, w_sel, w_form, extras, outs_desc, epilogue, tm=512, comm=None, after=None):
    T = as_[0].shape[0]
    tm = min(tm, T)
    n_a = len(as_)
    w_shapes = [w.shape[-2:] for w in ws]
    N = w_shapes[0][1] if w_form == "nn" else w_shapes[0][0]

    def main(ins, outs, scr):
        a_refs = ins[:n_a]
        w_refs = ins[n_a:2 * n_a]
        acc = None
        for a_ref, w_ref in zip(a_refs, w_refs):
            d = _dot(a_ref[...], w_ref[...], w_form)
            acc = d if acc is None else acc + d
        epilogue(acc, ins[2 * n_a:], outs)

    kind_spec = {"tile": pl.BlockSpec((tm, N), lambda i: (i, 0)), "vec": pl.BlockSpec((1, N), lambda i: (0, 0))}
    kind_shape = {"tile": (T, N), "vec": (1, N)}
    a_specs = [pl.BlockSpec((tm, a.shape[1]), lambda i: (i, 0)) for a in as_]
    w_specs = []
    for w, s in zip(ws, w_sel):
        if w.ndim == 3:
            w_specs.append(pl.BlockSpec((None,) + tuple(w.shape[1:]), functools.partial(lambda i, s: (s, 0, 0), s=s),
                                        pipeline_mode=pl.Buffered(1)))
        else:
            w_specs.append(pl.BlockSpec(tuple(w.shape), lambda i: (0, 0), pipeline_mode=pl.Buffered(1)))
    args = list(as_) + list(ws) + [e for e, _ in extras]
    return _call(name, main, (T // tm,), a_specs + w_specs + [kind_spec[k] for _, k in extras],
                 [kind_spec[k] for _, k in outs_desc],
                 [jax.ShapeDtypeStruct(kind_shape[k], dt) for dt, k in outs_desc], args, comm=comm, after=after)


def _residual_rms_epilogue(scale):
    def ep(acc, ex, outs):
        xv = ex[0][...] + scale * acc
        outs[0][...] = xv
        r = lax.rsqrt(jnp.mean(xv * xv, axis=-1, keepdims=True) + RMS_EPS)
        outs[1][...] = (xv * r * ex[1][...]).astype(outs[1].dtype)
    return ep


def _rms_bwd_epilogue(out_scale):
    def ep(acc, ex, outs):
        x_ref, g_ref, dres_ref = ex
        dx_ref, dxb_ref, dg_ref = outs
        xv = x_ref[...]
        r = lax.rsqrt(jnp.mean(xv * xv, axis=-1, keepdims=True) + RMS_EPS)
        xhat = xv * r

        @pl.when(pl.program_id(0) == 0)
        def _():
            dg_ref[...] = jnp.zeros_like(dg_ref)

        dg_ref[...] += jnp.sum(acc * xhat, axis=0, keepdims=True)
        dxhat = acc * g_ref[...]
        dx = r * (dxhat - xhat * jnp.mean(dxhat * xhat, axis=-1, keepdims=True)) + dres_ref[...]
        dx_ref[...] = dx
        dxb_ref[...] = (out_scale * dx).astype(dxb_ref.dtype)
    return ep


def _mm_tn(name, a, b, tmo, tno, out_dtype, tk=1024, comm=None, after=None):
    T, Ma = a.shape
    Nb = b.shape[1]
    tk = min(tk, T)
    nk = T // tk

    def main(ins, outs, scr):
        a_ref, b_ref = ins
        (acc_ref,) = scr
        k = pl.program_id(2)

        @pl.when(k == 0)
        def _():
            acc_ref[...] = jnp.zeros_like(acc_ref)

        acc_ref[...] += _dot(a_ref[...], b_ref[...], "tn")

        @pl.when(k == nk - 1)
        def _():
            outs[0][...] = acc_ref[...].astype(outs[0].dtype)

    (out,), extra = _call(
        name, main, (Ma // tmo, Nb // tno, nk),
        [pl.BlockSpec((tk, tmo), lambda i, j, k: (k, i)), pl.BlockSpec((tk, tno), lambda i, j, k: (k, j))],
        [pl.BlockSpec((tmo, tno), lambda i, j, k: (i, j))], [jax.ShapeDtypeStruct((Ma, Nb), out_dtype)],
        (a, b), scratch=[pltpu.VMEM((tmo, tno), F32)], comm=comm, after=after)
    return out, extra


def _swiglu_parts(g, u):
    s = _sigmoid(g)
    silu = g * s
    return [u * (s + silu * (1.0 - s)), silu, silu * u]


def _silu_mul_epilogue(dots, comps):
    g, u = dots
    return _swiglu_parts(g, u)


def _gate_parts_epilogue(dots, comps):
    (g,) = dots
    s = _sigmoid(g)
    silu = g * s
    return [s + silu * (1.0 - s), silu]


def _up_act_epilogue(dots, comps):
    (u,) = dots
    return [u * comps[0].astype(F32), u * comps[1].astype(F32)]


def _dact_epilogue(dots, comps):
    dact = dots[0].astype(BF16)
    return [dact * comps[0], dact * comps[1]]


def _identity_epilogue(dots, comps):
    return list(dots)


def _swap_halves(x):
    lane = lax.broadcasted_iota(jnp.int32, x.shape, 1)
    first = (lane % DK) < (DK // 2)
    return jnp.where(first, pltpu.roll(x, 128 - DK // 2, 1), pltpu.roll(x, DK // 2, 1))


def _rotary(t, cos, sin_signed):
    halves = []
    for p in range(QK_W // 128):
        th = t[:, 128 * p:128 * (p + 1)]
        halves.append(th * cos + _swap_halves(th) * sin_signed)
    return jnp.concatenate(halves, axis=1)


def _rotary_transposed(d, cos, sin_signed):
    halves = []
    for p in range(QK_W // 128):
        dh = d[:, 128 * p:128 * (p + 1)]
        halves.append(dh * cos + _swap_halves(dh * sin_signed))
    return jnp.concatenate(halves, axis=1)


def _log_sigmoid(x):
    return jnp.minimum(x, 0.0) - jnp.log(1.0 + jnp.exp(-jnp.abs(x)))


def _attn_masks():
    row = lax.broadcasted_iota(jnp.int32, (SUPER, SUPER), 0)
    col = lax.broadcasted_iota(jnp.int32, (SUPER, SUPER), 1)
    same = (row // CHUNK) == (col // CHUNK)
    return row, col, same


def _group_inputs(grp, pr, cos, sin_signed, lg, wa2, ba):
    seg = lambda lo, width: pr[:, lo:lo + width].astype(F32)
    if grp == 0:
        q = _rotary(seg(C_RQ, QK_W), cos, sin_signed)
        k = _rotary(seg(C_RK, QK_W), cos, sin_signed) * (DK ** -0.5)
        v = pr[:, C_RV:C_RV + V_W]
        gate = seg(C_RG, V_W)
        pos = lax.broadcasted_iota(jnp.int32, (SUPER, QK_W), 0).astype(F32) + 1.0
        return q, k, v, gate, pos * lg, None, None
    q = seg(C_GQ, QK_W) * (DK ** -0.5)
    k = seg(C_GK, QK_W)
    v = pr[:, C_GV:C_GV + V_W]
    gate = seg(C_GG, V_W)
    glow = pr[:, C_GL:C_GL + GL_W]
    logit = _dot(glow.astype(BF16), wa2.astype(BF16), "nn") + ba
    la = _log_sigmoid(logit) * (1.0 / GATE_NORM)
    row, col, _ = _attn_masks()
    lower = (col <= row).astype(F32)
    b_cum = _dot(lower, la, "nn", precision=lax.Precision.HIGHEST)
    return q, k, v, gate, b_cum, glow, logit


def _decay_factors(q, k, b_cum):
    c = b_cum[SUPER // 2 - 1:SUPER // 2, :]
    bl = b_cum[SUPER - 1:SUPER, :]
    e1 = jnp.exp(b_cum - c)
    e2 = jnp.exp(c - b_cum)
    e_b = jnp.exp(b_cum)
    e_l = jnp.exp(bl - b_cum)
    return dict(e1=e1, e2=e2, eb=e_b, el=e_l, ebl=jnp.exp(bl),
                qp=q * e1, qm=q * e2, kp=k * e1, km=k * e2, qs=q * e_b, kl=k * e_l)


def _state_block_mask():
    r = lax.broadcasted_iota(jnp.int32, (V_W, QK_W), 0)
    c = lax.broadcasted_iota(jnp.int32, (V_W, QK_W), 1)
    return (r // DV) == (c // DK)


def _attn_fwd(proj, cos, sin_signed, lg, wa2p, ba, gn_ret, gn_gla, x_res, w_out, g_next, comm=None):
    T = proj.shape[0]
    n_s = T // SUPER
    D = x_res.shape[1]

    def main(ins, outs, scr):
        pr_ref, cos_ref, sin_ref, lg_ref, wa2_ref, ba_ref, gr_ref, gg_ref, xres_ref, wout_ref, gnext_ref = ins
        o_ref, y_ref, st_ref, x_ref, h_ref = outs
        (s_ref,) = scr
        i = pl.program_id(0)

        @pl.when(i == 0)
        def _():
            s_ref[...] = jnp.zeros_like(s_ref)

        pr = pr_ref
        row, col, same = _attn_masks()
        m1 = col <= row
        m2 = jnp.logical_and(col > row, same)
        lane = lax.broadcasted_iota(jnp.int32, (1, QK_W), 1)
        blockmask = _state_block_mask()
        for grp in range(2):
            q, k, v, gate, b_cum, _, _ = _group_inputs(grp, pr, cos_ref[...], sin_ref[...], lg_ref[...],
                                                      wa2_ref[...], ba_ref[...])
            f = _decay_factors(q, k, b_cum)
            gn = gr_ref[...] if grp == 0 else gg_ref[...]
            s_prev = s_ref[grp]
            st_ref[0, grp] = s_prev
            o_inter = _dot(f["qs"].astype(BF16), s_prev.astype(BF16), "nt")
            kmb = f["km"].astype(BF16)
            kpb = f["kp"].astype(BF16)
            vb = v.astype(BF16)
            for h in range(HEADS):
                hm = (lane // DK) == h
                a1 = _dot(jnp.where(hm, f["qp"], 0.0).astype(BF16), kmb, "nt")
                a2 = _dot(jnp.where(hm, f["qm"], 0.0).astype(BF16), kpb, "nt")
                a = jnp.where(m1, a1, jnp.where(m2, a2, 0.0))
                lo = grp * V_W + h * DV
                o_h = _dot(a.astype(BF16), vb[:, h * DV:(h + 1) * DV], "nn") + o_inter[:, h * DV:(h + 1) * DV]
                o_ref[:, lo:lo + DV] = o_h
                r = lax.rsqrt(jnp.mean(o_h * o_h, axis=-1, keepdims=True) + RMS_EPS)
                gte = gate[:, h * DV:(h + 1) * DV]
                y = o_h * r * gn[:, h * DV:(h + 1) * DV] * (gte * _sigmoid(gte))
                y_ref[:, lo:lo + DV] = y.astype(y_ref.dtype)
            upd = _dot(vb, f["kl"].astype(BF16), "tn")
            s_ref[grp] = s_prev * f["ebl"] + jnp.where(blockmask, upd, 0.0)
        xv = xres_ref[...] + _dot(y_ref[...], wout_ref[...], "nn")
        x_ref[...] = xv
        r = lax.rsqrt(jnp.mean(xv * xv, axis=-1, keepdims=True) + RMS_EPS)
        h_ref[...] = (xv * r * gnext_ref[...]).astype(h_ref.dtype)

    const = lambda shape: pl.BlockSpec(shape, lambda i: tuple(0 for _ in shape))
    rows = lambda w: pl.BlockSpec((SUPER, w), lambda i: (i, 0))
    return _call(
        "attn_fwd", main, (n_s,),
        [rows(PROJ_W), rows(128), rows(128),
         const((1, QK_W)), const((GL_W, QK_W)), const((1, QK_W)), const((1, V_W)), const((1, V_W)),
         rows(D), const((2 * V_W, D)), const((1, D))],
        [rows(2 * V_W), rows(2 * V_W), pl.BlockSpec((1, 2, V_W, QK_W), lambda i: (i, 0, 0, 0)), rows(D), rows(D)],
        [jax.ShapeDtypeStruct((T, 2 * V_W), F32), jax.ShapeDtypeStruct((T, 2 * V_W), BF16),
         jax.ShapeDtypeStruct((n_s, 2, V_W, QK_W), F32), jax.ShapeDtypeStruct((T, D), F32),
         jax.ShapeDtypeStruct((T, D), BF16)],
        (proj, cos, sin_signed, lg, wa2p, ba, gn_ret, gn_gla, x_res, w_out, g_next),
        scratch=[pltpu.VMEM((2, V_W, QK_W), F32)], comm=comm)


def _attn_bwd(proj, cos, sin_signed, lg, wa2p, ba, gn_ret, gn_gla, o, dx, w_out, states, comm=None, after=None):
    T = proj.shape[0]
    n_s = T // SUPER
    D = dx.shape[1]

    def main(ins, outs, scr):
        pr_ref, cos_ref, sin_ref, lg_ref, wa2_ref, ba_ref, gr_ref, gg_ref, o_ref, dx_ref, wout_ref, st_ref = ins
        dp_ref, dgr_ref, dgg_ref, dba_ref, dwa_ref = outs
        (ds_ref, dy_ref) = scr
        i = pl.program_id(0)
        dy_ref[...] = _dot(dx_ref[...], wout_ref[...], "nt")

        @pl.when(i == 0)
        def _():
            ds_ref[...] = jnp.zeros_like(ds_ref)
            dgr_ref[...] = jnp.zeros_like(dgr_ref)
            dgg_ref[...] = jnp.zeros_like(dgg_ref)
            dba_ref[...] = jnp.zeros_like(dba_ref)
            dwa_ref[...] = jnp.zeros_like(dwa_ref)

        pr = pr_ref
        cos = cos_ref[...]
        sin_signed = sin_ref[...]
        row, col, same = _attn_masks()
        m1 = col <= row
        m2 = jnp.logical_and(col > row, same)
        m1t = row <= col
        m2t = jnp.logical_and(row > col, same)
        lane = lax.broadcasted_iota(jnp.int32, (1, QK_W), 1)
        blockmask = _state_block_mask()
        for grp in range(2):
            q, k, v, gate, b_cum, glow, logit = _group_inputs(grp, pr, cos, sin_signed, lg_ref[...],
                                                              wa2_ref[...], ba_ref[...])
            f = _decay_factors(q, k, b_cum)
            gn = gr_ref[...] if grp == 0 else gg_ref[...]
            dgn_ref = dgr_ref if grp == 0 else dgg_ref
            do_parts, dgate_parts, dgn_parts = [], [], []
            for h in range(HEADS):
                lo = grp * V_W + h * DV
                o_h = o_ref[:, lo:lo + DV]
                r = lax.rsqrt(jnp.mean(o_h * o_h, axis=-1, keepdims=True) + RMS_EPS)
                n = o_h * r
                gte = gate[:, h * DV:(h + 1) * DV]
                sg = _sigmoid(gte)
                dy_h = dy_ref[:, lo:lo + DV]
                gn_h = gn[:, h * DV:(h + 1) * DV]
                dgate_parts.append(dy_h * n * gn_h * (sg * (1.0 + gte * (1.0 - sg))))
                dz = dy_h * (gte * sg)
                dgn_parts.append(jnp.sum(dz * n, axis=0, keepdims=True))
                dn = dz * gn_h
                do_parts.append(r * (dn - n * jnp.mean(dn * n, axis=-1, keepdims=True)))
            dgn_ref[...] += jnp.concatenate(dgn_parts, axis=1)
            dgate = jnp.concatenate(dgate_parts, axis=1)
            do = jnp.concatenate(do_parts, axis=1)
            dob = do.astype(BF16)
            vb = v.astype(BF16)
            s_prev = st_ref[0, grp]
            ds_new = ds_ref[grp]
            dsb = ds_new.astype(BF16)
            qpb, qmb = f["qp"].astype(BF16), f["qm"].astype(BF16)
            kpb, kmb = f["kp"].astype(BF16), f["km"].astype(BF16)
            dqp = jnp.zeros((SUPER, QK_W), F32)
            dqm = jnp.zeros((SUPER, QK_W), F32)
            dkp = jnp.zeros((SUPER, QK_W), F32)
            dkm = jnp.zeros((SUPER, QK_W), F32)
            dv_parts = []
            for h in range(HEADS):
                hm = (lane // DK) == h
                qp_h = jnp.where(hm, f["qp"], 0.0).astype(BF16)
                qm_h = jnp.where(hm, f["qm"], 0.0).astype(BF16)
                kp_h = jnp.where(hm, f["kp"], 0.0).astype(BF16)
                km_h = jnp.where(hm, f["km"], 0.0).astype(BF16)
                at = jnp.where(m1t, _dot(km_h, qpb, "nt"), jnp.where(m2t, _dot(kp_h, qmb, "nt"), 0.0))
                do_h = dob[:, h * DV:(h + 1) * DV]
                v_h = vb[:, h * DV:(h + 1) * DV]
                dv_parts.append(_dot(at.astype(BF16), do_h, "nn"))
                da = _dot(do_h, v_h, "nt")
                dat = _dot(v_h, do_h, "nt")
                da1 = jnp.where(m1, da, 0.0).astype(BF16)
                da2 = jnp.where(m2, da, 0.0).astype(BF16)
                da1t = jnp.where(m1t, dat, 0.0).astype(BF16)
                da2t = jnp.where(m2t, dat, 0.0).astype(BF16)
                dqp = dqp + _dot(da1, km_h, "nn")
                dqm = dqm + _dot(da2, kp_h, "nn")
                dkm = dkm + _dot(da1t, qp_h, "nn")
                dkp = dkp + _dot(da2t, qm_h, "nn")
            klb = f["kl"].astype(BF16)
            qsb = f["qs"].astype(BF16)
            dqs = _dot(dob, s_prev.astype(BF16), "nn")
            dkl = _dot(vb, dsb, "nn")
            dv = jnp.concatenate(dv_parts, axis=1) + _dot(klb, dsb, "nt")
            ds_ref[grp] = ds_new * f["ebl"] + jnp.where(blockmask, _dot(dob, qsb, "tn"), 0.0)
            dq = dqp * f["e1"] + dqm * f["e2"] + dqs * f["eb"]
            dk = dkm * f["e2"] + dkp * f["e1"] + dkl * f["el"]
            if grp == 0:
                dq = _rotary_transposed(dq, cos, sin_signed)
                dk = _rotary_transposed(dk * (DK ** -0.5), cos, sin_signed)
                dp_ref[:, C_RQ:C_RQ + QK_W] = dq.astype(dp_ref.dtype)
                dp_ref[:, C_RK:C_RK + QK_W] = dk.astype(dp_ref.dtype)
                dp_ref[:, C_RV:C_RV + V_W] = dv.astype(dp_ref.dtype)
                dp_ref[:, C_RG:C_RG + V_W] = dgate.astype(dp_ref.dtype)
            else:
                dkl_kl = dkl * klb.astype(F32)
                db = (dqp * qpb.astype(F32) - dkm * kmb.astype(F32) - dqm * qmb.astype(F32)
                      + dkp * kpb.astype(F32) + dqs * qsb.astype(F32) - dkl_kl)
                last = (jnp.sum(dkl_kl, axis=0, keepdims=True)
                        + f["ebl"] * jnp.sum(s_prev * ds_new, axis=0, keepdims=True))
                rowq = lax.broadcasted_iota(jnp.int32, (SUPER, QK_W), 0)
                db = db + jnp.where(rowq == SUPER - 1, last, 0.0)
                upper = (col >= row).astype(F32)
                dla = _dot(upper, db, "nn", precision=lax.Precision.HIGHEST)
                dlogit = dla * (1.0 / GATE_NORM) * (1.0 - _sigmoid(logit))
                dlb = dlogit.astype(BF16)
                dglow = _dot(dlb, wa2_ref[...].astype(BF16), "nt")
                dwa_ref[...] += _dot(glow.astype(BF16), dlb, "tn")
                dba_ref[...] += jnp.sum(dlogit, axis=0, keepdims=True)
                dp_ref[:, C_GQ:C_GQ + QK_W] = (dq * (DK ** -0.5)).astype(dp_ref.dtype)
                dp_ref[:, C_GK:C_GK + QK_W] = dk.astype(dp_ref.dtype)
                dp_ref[:, C_GV:C_GV + V_W] = dv.astype(dp_ref.dtype)
                dp_ref[:, C_GG:C_GG + V_W] = dgate.astype(dp_ref.dtype)
                dp_ref[:, C_GL:C_GL + GL_W] = dglow.astype(dp_ref.dtype)

    rev = lambda i: n_s - 1 - i
    const = lambda shape: pl.BlockSpec(shape, lambda i: tuple(0 for _ in shape))
    return _call(
        "attn_bwd", main, (n_s,),
        [pl.BlockSpec((SUPER, PROJ_W), lambda i: (rev(i), 0)),
         pl.BlockSpec((SUPER, 128), lambda i: (rev(i), 0)), pl.BlockSpec((SUPER, 128), lambda i: (rev(i), 0)),
         const((1, QK_W)), const((GL_W, QK_W)), const((1, QK_W)), const((1, V_W)), const((1, V_W)),
         pl.BlockSpec((SUPER, 2 * V_W), lambda i: (rev(i), 0)),
         pl.BlockSpec((SUPER, D), lambda i: (rev(i), 0)), const((2 * V_W, D)),
         pl.BlockSpec((1, 2, V_W, QK_W), lambda i: (rev(i), 0, 0, 0))],
        [pl.BlockSpec((SUPER, PROJ_W), lambda i: (rev(i), 0)),
         const((1, V_W)), const((1, V_W)), const((1, QK_W)), const((GL_W, QK_W))],
        [jax.ShapeDtypeStruct((T, PROJ_W), BF16),
         jax.ShapeDtypeStruct((1, V_W), F32), jax.ShapeDtypeStruct((1, V_W), F32),
         jax.ShapeDtypeStruct((1, QK_W), F32), jax.ShapeDtypeStruct((GL_W, QK_W), F32)],
        (proj, cos, sin_signed, lg, wa2p, ba, gn_ret, gn_gla, o, dx, w_out, states),
        scratch=[pltpu.VMEM((2, V_W, QK_W), F32), pltpu.VMEM((SUPER, 2 * V_W), F32)], comm=comm, after=after)


def _rotary_tables(T):
    half = DK // 2
    lane = jnp.arange(128)
    inv = ROPE_BASE ** (-((lane % half).astype(F32)) * 2.0 / DK)
    sign = jnp.where((lane % DK) < half, -1.0, 1.0).astype(F32)
    ang = jnp.arange(T, dtype=F32)[:, None] * inv[None, :]
    return jnp.cos(ang), jnp.sin(ang) * sign[None, :]


def _sum_devices(name, gathered, m_per):
    def body(g_ref, o_ref):
        acc = g_ref[0:m_per, :]
        for k in range(1, N_DEV):
            acc = acc + g_ref[k * m_per:(k + 1) * m_per, :]
        o_ref[...] = acc

    return pl.pallas_call(body, name=name, out_shape=jax.ShapeDtypeStruct((m_per, 128), F32))(gathered)


def _owner_sums(name, items, owner, comm=None):
    counts = [1 + len(landed) for _, landed in items]

    def main(ins, outs, scr):
        at = 0
        for o_ref, n in zip(outs, counts):
            acc = ins[at][...].astype(F32)
            for l_ref in ins[at + 1:at + n]:
                for j in range(l_ref.shape[0]):
                    acc = acc + l_ref[j].astype(F32)
            o_ref[...] = acc
            at += n

    once = pl.Buffered(1)
    in_specs, out_specs, out_shape, args = [], [], [], []
    for grad, landed in items:
        R, C = grad.shape[-2:]
        in_specs.append(pl.BlockSpec((None, None, R, C), lambda i, s: (s[0], s[1], 0, 0), pipeline_mode=once))
        in_specs += [pl.BlockSpec(tuple(l.shape), lambda i, s: (0, 0, 0), pipeline_mode=once) for l in landed]
        out_specs.append(pl.BlockSpec((R, C), lambda i, s: (0, 0)))
        out_shape.append(jax.ShapeDtypeStruct((R, C), F32))
        args += [grad, *landed]
    return _call(name, main, (1,), in_specs, out_specs, out_shape, args, comm=comm, prefetch=owner)


def _adamw_group(name, items, n_blocks, comm=None):
    n = len(items)

    def main(ins, outs, scr):
        for p in range(n):
            g_ref, w_ref, m_ref, v_ref = ins[4 * p:4 * p + 4]
            d_ref, nm_ref, nv_ref = outs[3 * p:3 * p + 3]
            gv = g_ref[...]
            nm = ADAM_B1 * m_ref[...] + (1.0 - ADAM_B1) * gv
            nv = ADAM_B2 * v_ref[...] + (1.0 - ADAM_B2) * (gv * gv)
            m_hat = nm / (1.0 - ADAM_B1 ** ADAM_STEP)
            v_hat = nv / (1.0 - ADAM_B2 ** ADAM_STEP)
            d_ref[...] = -ADAM_LR * (m_hat / (jnp.sqrt(v_hat) + ADAM_EPS) + ADAM_WD * w_ref[...])
            nm_ref[...] = nm
            nv_ref[...] = nv

    in_specs, out_specs, out_shape, args = [], [], [], []
    for item in items:
        R, C = item[1].shape
        assert R % n_blocks == 0
        spec = pl.BlockSpec((R // n_blocks, C), lambda i: (i, 0))
        in_specs += [spec] * 4
        out_specs += [spec] * 3
        out_shape += [jax.ShapeDtypeStruct((R, C), F32)] * 3
        args += list(item)
    outs, extra = _call(name, main, (n_blocks,), in_specs, out_specs, out_shape, args, comm=comm)
    return [tuple(outs[3 * p:3 * p + 3]) for p in range(n)], extra


SMALL_ORDER = ("ffn1", "mix", "ffn2", "final", "ret", "gla", "b_a")


def kernel(x, ffn1_norm_g, ffn1_w_gate, ffn1_w_up, ffn1_w_down, mix_norm_g, w_in, ret_norm_g, gla_w_a2, gla_b_a, gla_norm_g, w_out, ffn2_norm_g, ffn2_w_gate, ffn2_w_up, ffn2_w_down, final_norm_g, loss_target, m_ffn1_norm_g, m_ffn1_w_gate, m_ffn1_w_up, m_ffn1_w_down, m_mix_norm_g, m_w_in, m_ret_norm_g, m_gla_w_a2, m_gla_b_a, m_gla_norm_g, m_w_out, m_ffn2_norm_g, m_ffn2_w_gate, m_ffn2_w_up, m_ffn2_w_down, m_final_norm_g, v_ffn1_norm_g, v_ffn1_w_gate, v_ffn1_w_up, v_ffn1_w_down, v_mix_norm_g, v_w_in, v_ret_norm_g, v_gla_w_a2, v_gla_b_a, v_gla_norm_g, v_w_out, v_ffn2_norm_g, v_ffn2_w_gate, v_ffn2_w_up, v_ffn2_w_down, v_final_norm_g):
    xi, yi, ci = _coords()
    dev = 4 * xi + 2 * yi + ci
    owner = jnp.stack([2 * xi + yi, ci]).astype(jnp.int32)

    x0, target = x[0], loss_target[0]
    T, D = x0.shape
    fb = ffn1_w_gate.shape[2]
    ib = w_in.shape[2]
    ab = gla_w_a2.shape[2]
    F = N_DEV * fb
    cos, sin_signed = _rotary_tables(T)
    lg = jnp.repeat(jnp.log(1.0 - 2.0 ** (-5.0 - jnp.arange(HEADS, dtype=F32))), DK)[None, :]
    g_final = final_norm_g.reshape(1, D)

    g1_loc = ffn1_w_gate[0].T[None].astype(BF16)
    u1_loc = ffn1_w_up[0].T[None].astype(BF16)
    d1_loc = ffn1_w_down.astype(BF16)
    g2_loc = ffn2_w_gate[0].T[None].astype(BF16)
    u2_loc = ffn2_w_up[0].T[None].astype(BF16)
    d2_loc = ffn2_w_down.astype(BF16)
    in_loc = w_in[0].T.astype(BF16)
    out_loc = w_out[0].astype(BF16)

    h1, (g1,) = _rms_fwd("ffn1_rms", x0, ffn1_norm_g, comm=_AllGather([g1_loc], ["stack"]))
    g1 = g1.reshape(1, F, D)
    (dsl1, sl1), (u1,) = _mm_nstream("ffn1_gate", h1, [g1], [0], "nt", [], [BF16, BF16], _gate_parts_epilogue, cn=256,
                                     comm=_AllGather([u1_loc], ["stack"]))
    u1 = u1.reshape(1, F, D)
    (dsu1, act1), (d1,) = _mm_nstream("ffn1_up", h1, [u1], [0], "nt", [dsl1, sl1], [BF16, BF16],
                                      _up_act_epilogue, cn=256, comm=_AllGather([d1_loc], ["stack"]))
    d1 = d1.reshape(1, F, D)
    f32_tile, bf16_tile, f32_vec = (F32, "tile"), (BF16, "tile"), (F32, "vec")
    (x1, h2), (in_all, a_all) = _mm_mstream(
        "ffn1_down", [act1], [d1], [0], "nn", [(x0, "tile"), (mix_norm_g, "vec")], [f32_tile, bf16_tile],
        _residual_rms_epilogue(0.5), comm=_AllGather([in_loc, gla_w_a2[0]], ["plain", "plain"]))
    w_in_t = jnp.concatenate([in_all[k] for k in range(N_DEV)]
                             + [jnp.zeros((PROJ_W - N_DEV * ib, D), in_all.dtype)], axis=0)[None]
    wa2 = jnp.transpose(a_all, (1, 0, 2)).reshape(GATE_RANK, N_DEV * ab)
    wa2p = jnp.pad(wa2, ((0, GL_W - GATE_RANK), (0, 0)))

    (proj,), (g2, out_all) = _mm_nstream("mix_proj", h2, [w_in_t], [0], "nt", [], [BF16], _identity_epilogue, cn=640,
                                         comm=_AllGather([g2_loc, out_loc], ["stack", "plain"]))
    w_out_full = out_all.reshape(D, D)
    (o, ymix, states, x2, h3), (u2,) = _attn_fwd(proj, cos, sin_signed, lg, wa2p, gla_b_a, ret_norm_g, gla_norm_g,
                                                 x1, w_out_full, ffn2_norm_g, comm=_AllGather([u2_loc], ["stack"]))
    g2, u2 = g2.reshape(1, F, D), u2.reshape(1, F, D)

    (dsu2, sl2, act2), (d2,) = _mm_nstream(
        "ffn2_up", h3, [g2, u2], [0, 0], "nt", [], [BF16, BF16, BF16], _silu_mul_epilogue, cn=256,
        comm=_AllGather([d2_loc], ["stack"]))
    d2 = d2.reshape(1, F, D)
    (dx3, dy3b, d_final, loss), _ = _mm_mstream(
        "ffn2_down", [act2], [d2], [0], "nn", [(x2, "tile"), (g_final, "vec"), (target, "tile")],
        [f32_tile, bf16_tile, f32_vec, f32_vec], _final_loss_epilogue(0.5, 0.5))

    sent = {}

    def send(nm, grad):
        sent[nm] = _send_to_owners("send_" + nm, grad)
        return sent[nm][4]

    dwd2, _ = _mm_tn("ffn2b_dwd", act2, dy3b, F // 2, D, BF16)
    tok = send("wd2", dwd2.reshape(4, 2, fb, D))
    (dgate2, dup2), _ = _mm_nstream("ffn2b_dact", dy3b, [d2], [0], "nt", [dsu2, sl2], [BF16, BF16],
                                    _dact_epilogue, cn=256, after=tok)
    dwg2, _ = _mm_tn("ffn2b_dwg", dgate2, h3, F // 2, D, BF16)
    tok = send("wg2", dwg2.reshape(4, 2, fb, D))
    dwu2, _ = _mm_tn("ffn2b_dwu", dup2, h3, F // 2, D, BF16, after=tok)
    tok = send("wu2", dwu2.reshape(4, 2, fb, D))
    rms_outs = [f32_tile, bf16_tile, f32_vec]
    (dx2, dx2b, d_g2), _ = _mm_mstream(
        "ffn2b_dh", [dgate2, dup2], [g2, u2], [0, 0], "nn", [(x2, "tile"), (ffn2_norm_g, "vec"), (dx3, "tile")],
        rms_outs, _rms_bwd_epilogue(1.0), after=tok)

    dwout, _ = _mm_tn("mixb_dwout", ymix, dx2b, D, D, BF16)
    tok = send("wout", dwout.reshape(4, 2, D // N_DEV, D))
    (dproj, d_ret, d_gla, d_ba, d_wa2p), _ = _attn_bwd(
        proj, cos, sin_signed, lg, wa2p, gla_b_a, ret_norm_g, gla_norm_g, o, dx2b, w_out_full, states, after=tok)
    dwin_t, _ = _mm_tn("mixb_dwin", dproj, h2, 640, D, BF16, tk=2048)
    tok = send("win", jnp.stack([dwin_t[ib * k:ib * (k + 1)] for k in range(N_DEV)]).reshape(4, 2, ib, D))
    (dx1, dy1b, d_gmix), _ = _mm_mstream(
        "mixb_dh", [dproj], [w_in_t], [0], "nn", [(x1, "tile"), (mix_norm_g, "vec"), (dx2, "tile")],
        rms_outs, _rms_bwd_epilogue(0.5), after=tok)

    dwd1, _ = _mm_tn("ffn1b_dwd", act1, dy1b, F // 2, D, BF16)
    tok = send("wd1", dwd1.reshape(4, 2, fb, D))
    (dgate1, dup1), _ = _mm_nstream("ffn1b_dact", dy1b, [d1], [0], "nt", [dsu1, sl1], [BF16, BF16],
                                    _dact_epilogue, cn=256, after=tok)
    dwg1, _ = _mm_tn("ffn1b_dwg", dgate1, h1, F // 2, D, BF16)
    tok = send("wg1", dwg1.reshape(4, 2, fb, D))
    dwu1, _ = _mm_tn("ffn1b_dwu", dup1, h1, F // 2, D, BF16, after=tok)
    tok = send("wu1", dwu1.reshape(4, 2, fb, D))
    (dx0, _, d_g1), _ = _mm_mstream(
        "ffn1b_dh", [dgate1, dup1], [g1, u1], [0, 0], "nn", [(x0, "tile"), (ffn1_norm_g, "vec"), (dx1, "tile")],
        rms_outs, _rms_bwd_epilogue(1.0), after=tok)

    small = dict(ffn1=d_g1, mix=d_gmix, ffn2=d_g2, final=d_final, ret=d_ret, gla=d_gla, b_a=d_ba)
    flat = jnp.concatenate([small[k].reshape(-1) for k in SMALL_ORDER]
                           + [d_wa2p[:GATE_RANK].reshape(-1), loss[0, :128]])
    rows = -(-flat.shape[0] // 128)
    rows = -(-rows // 8) * 8
    packed = jnp.pad(flat, (0, rows * 128 - flat.shape[0])).reshape(rows, 128)

    transposed = ("ffn1_w_gate", "ffn1_w_up", "ffn2_w_gate", "ffn2_w_up", "w_in")

    def to_2d(nm, a):
        if nm in transposed:
            return a[0].T
        return a.reshape((1, a.shape[0]) if a.ndim == 1 else a.shape[-2:])

    def from_2d(nm, a):
        return a.T[None] if nm in transposed else a.reshape(params[nm][0].shape)

    def arrived(nm, after):
        grad, landed = _await_owners("await_" + nm, sent[nm], after)
        return grad, [landed]

    sums_a, (gathered,) = _owner_sums(
        "sum_a", [arrived(nm, dx0) for nm in ("wg2", "wu2", "wd2", "win", "wout")], owner,
        comm=_AllGather([packed], ["plain"]))
    params = dict(
        ffn2_w_gate=(ffn2_w_gate, m_ffn2_w_gate, v_ffn2_w_gate), ffn2_w_up=(ffn2_w_up, m_ffn2_w_up, v_ffn2_w_up),
        ffn2_w_down=(ffn2_w_down, m_ffn2_w_down, v_ffn2_w_down), w_in=(w_in, m_w_in, v_w_in),
        w_out=(w_out, m_w_out, v_w_out), ffn1_w_gate=(ffn1_w_gate, m_ffn1_w_gate, v_ffn1_w_gate),
        ffn1_w_up=(ffn1_w_up, m_ffn1_w_up, v_ffn1_w_up), ffn1_w_down=(ffn1_w_down, m_ffn1_w_down, v_ffn1_w_down),
        ffn1_norm_g=(ffn1_norm_g, m_ffn1_norm_g, v_ffn1_norm_g), mix_norm_g=(mix_norm_g, m_mix_norm_g, v_mix_norm_g),
        ret_norm_g=(ret_norm_g, m_ret_norm_g, v_ret_norm_g), gla_w_a2=(gla_w_a2, m_gla_w_a2, v_gla_w_a2),
        gla_b_a=(gla_b_a, m_gla_b_a, v_gla_b_a), gla_norm_g=(gla_norm_g, m_gla_norm_g, v_gla_norm_g),
        ffn2_norm_g=(ffn2_norm_g, m_ffn2_norm_g, v_ffn2_norm_g), final_norm_g=(final_norm_g, m_final_norm_g, v_final_norm_g))
    grads, updates = {}, {}

    def run_adam(name, names, grad_2d, n_blocks):
        items = [(grad_2d[nm],) + tuple(to_2d(nm, a) for a in params[nm]) for nm in names]
        res, _ = _adamw_group(name, items, n_blocks)
        for nm, r in zip(names, res):
            grads[nm] = from_2d(nm, grad_2d[nm])
            updates[nm] = tuple(from_2d(nm, a) for a in r)
        return res

    grads_a = {"ffn2_w_gate": sums_a[0], "ffn2_w_up": sums_a[1], "ffn2_w_down": sums_a[2], "w_out": sums_a[4]}
    run_adam("adamw_w_in", ["w_in"], {"w_in": sums_a[3]}, 1)
    done_a = run_adam("adamw_a", list(grads_a), grads_a, 4)[0][0]
    sums_b, _ = _owner_sums("sum_b", [arrived(nm, done_a) for nm in ("wg1", "wu1", "wd1")], owner)
    grads_b = {"ffn1_w_gate": sums_b[0], "ffn1_w_up": sums_b[1], "ffn1_w_down": sums_b[2]}
    run_adam("adamw_b", list(grads_b), grads_b, 4)

    total = _sum_devices("sum_small", gathered.reshape(N_DEV * rows, 128), rows).reshape(-1)
    sizes = [small[k].size for k in SMALL_ORDER] + [GATE_RANK * QK_W, 128]
    offs = [0]
    for s in sizes:
        offs.append(offs[-1] + s)
    pieces = [total[offs[i]:offs[i + 1]] for i in range(len(sizes))]
    g_small = {k: pieces[i].reshape(small[k].shape) for i, k in enumerate(SMALL_ORDER)}
    g_wa2_full = pieces[len(SMALL_ORDER)].reshape(GATE_RANK, QK_W)
    g_wa2 = lax.dynamic_slice(g_wa2_full, (0, dev * ab), (GATE_RANK, ab))
    loss_total = pieces[len(SMALL_ORDER) + 1][0]

    small_grads = {"ffn1_norm_g": g_small["ffn1"], "mix_norm_g": g_small["mix"], "ret_norm_g": g_small["ret"],
                   "gla_w_a2": g_wa2, "gla_b_a": g_small["b_a"], "gla_norm_g": g_small["gla"],
                   "ffn2_norm_g": g_small["ffn2"], "final_norm_g": g_small["final"]}
    run_adam("adamw_small", list(small_grads), small_grads, 1)

    order = ("ffn1_norm_g", "ffn1_w_gate", "ffn1_w_up", "ffn1_w_down", "mix_norm_g", "w_in", "ret_norm_g", "gla_w_a2",
             "gla_b_a", "gla_norm_g", "w_out", "ffn2_norm_g", "ffn2_w_gate", "ffn2_w_up", "ffn2_w_down", "final_norm_g")
    return (loss_total, dx0[None], *[grads[nm] for nm in order], *[updates[nm][0] for nm in order],
            *[updates[nm][1] for nm in order], *[updates[nm][2] for nm in order])
```

```python
import functools
import math

import jax
import jax.numpy as jnp
from jax import lax
from jax.experimental import pallas as pl
from jax.experimental.pallas import tpu as pltpu

F32 = jnp.float32
BF16 = jnp.bfloat16
MESH = pl.DeviceIdType.MESH
HBM = pl.BlockSpec(memory_space=pltpu.HBM)

N_DEV = 8
RMS_EPS = 1e-6
ROPE_BASE = 10000.0
HEADS = 4
DK = 64
DV = 128
QK_W = HEADS * DK
V_W = HEADS * DV
GATE_RANK = 16
GATE_NORM = 16.0
CHUNK = 64
SUPER = 256
PROJ_W = 3200
C_RQ, C_RK, C_RV, C_RG, C_GQ, C_GK, C_GV, C_GG, C_GL = 0, 256, 512, 1024, 1536, 1792, 2048, 2560, 3072
GL_W = PROJ_W - C_GL
ADAM_LR, ADAM_B1, ADAM_B2, ADAM_EPS, ADAM_WD, ADAM_STEP = 0.001, 0.9, 0.999, 1e-08, 0.01, 10
VMEM_LIMIT_V7X = 52 * 1024 * 1024


def _cparams(**kw):
    return pltpu.CompilerParams(vmem_limit_bytes=VMEM_LIMIT_V7X, **kw)


def _dot(a, b, form, precision=None):
    dims = {"nn": (((1,), (0,)), ((), ())), "nt": (((1,), (1,)), ((), ())), "tn": (((0,), (0,)), ((), ()))}[form]
    return lax.dot_general(a, b, dims, preferred_element_type=F32, precision=precision)


def _sigmoid(x):
    return 1.0 / (1.0 + jnp.exp(-x))


def _coords():
    return lax.axis_index("x"), lax.axis_index("y"), lax.axis_index("c")


class _NoComm:
    inputs, out_shapes, scratch = (), (), ()


class _AllGather:
    def __init__(self, arrays, kinds):
        self.inputs = tuple(arrays)
        self.kinds = tuple(kinds)
        n = len(arrays)
        self.out_shapes = tuple(
            jax.ShapeDtypeStruct((a.shape[0], N_DEV) + a.shape[1:] if k == "stack" else (N_DEV,) + a.shape, a.dtype)
            for a, k in zip(arrays, kinds))
        self.scratch = (pltpu.SemaphoreType.DMA((n, 7)), pltpu.SemaphoreType.DMA((n, 7)),
                        pltpu.SemaphoreType.DMA((n,)))

    def _ctx(self, srcs, outs, sems):
        send_sems, recv_sems, local_sems = sems
        x, y, c = _coords()
        me, sibling = (x, y, c), (x, y, 1 - c)
        chips = [(1 - x, y), (x, 1 - y), (1 - x, 1 - y)]

        def blk(m, dev):
            k = 4 * dev[0] + 2 * dev[1] + dev[2]
            return outs[m].at[:, k] if self.kinds[m] == "stack" else outs[m].at[k]

        def copy(m, s, block, to, src=None):
            return pltpu.make_async_remote_copy(
                src_ref=blk(m, block) if src is None else src, dst_ref=blk(m, block),
                send_sem=send_sems.at[m, s], recv_sem=recv_sems.at[m, s], device_id=to, device_id_type=MESH)

        def mine(m):
            return pltpu.make_async_copy(srcs[m], blk(m, me), local_sems.at[m])

        def first(m):
            return [copy(m, 0, me, sibling, src=srcs[m])] + [
                copy(m, 1 + j, me, (*chip, c), src=srcs[m]) for j, chip in enumerate(chips)]

        return me, sibling, chips, c, copy, mine, first

    def start(self, srcs, outs, sems):
        me, sibling, chips, c, copy, mine, first = self._ctx(srcs, outs, sems)
        for m in range(len(srcs)):
            mine(m).start()
            for cp in first(m):
                cp.start()

    def mid(self, srcs, outs, sems):
        me, sibling, chips, c, copy, mine, first = self._ctx(srcs, outs, sems)
        for j, chip in enumerate(chips):
            for m in range(len(srcs)):
                copy(m, 1 + j, (*chip, c), me).wait_recv()
                copy(m, 4 + j, (*chip, c), sibling).start()

    def finish(self, srcs, outs, sems):
        me, sibling, chips, c, copy, mine, first = self._ctx(srcs, outs, sems)
        for m in range(len(srcs)):
            copy(m, 0, sibling, me).wait_recv()
            for j, chip in enumerate(chips):
                copy(m, 4 + j, (*chip, 1 - c), me).wait_recv()
            for cp in first(m):
                cp.wait_send()
            for j, chip in enumerate(chips):
                copy(m, 4 + j, (*chip, c), sibling).wait_send()
            mine(m).wait()


RELATIONS = ((0, 0, 1), (1, 0, 0), (0, 1, 0), (1, 1, 0), (1, 0, 1), (0, 1, 1), (1, 1, 1))
SEM = pl.BlockSpec(memory_space=pltpu.SEMAPHORE)
SPLIT_PARAMS = dict(has_side_effects=pltpu.SideEffectType.DATAFLOW_SIDE_EFFECTING)


def _owner_copies(grad_ref, land_ref, send_sems, recv_sems):
    x, y, c = _coords()
    copies = []
    for s, (fx, fy, fc) in enumerate(RELATIONS):
        px = 1 - x if fx else x
        py = 1 - y if fy else y
        pc = 1 - c if fc else c
        copies.append(pltpu.make_async_remote_copy(
            src_ref=grad_ref.at[2 * px + py, pc], dst_ref=land_ref.at[s], send_sem=send_sems.at[s],
            recv_sem=recv_sems.at[s], device_id=(px, py, pc), device_id_type=MESH))
    return copies


def _send_to_owners(name, grad):
    n = len(RELATIONS)
    land_shape = (n,) + grad.shape[2:]

    def body(g_ref, land_ref, send_sems, recv_sems, g_thru, land_thru, token):
        for cp in _owner_copies(g_ref, land_ref, send_sems, recv_sems):
            cp.start()
        token[...] = jnp.zeros_like(token)

    return pl.pallas_call(
        body, name=name,
        out_shape=(pltpu.SemaphoreType.DMA((n,)), pltpu.SemaphoreType.DMA((n,)), pltpu.HBM(grad.shape, grad.dtype),
                   pltpu.HBM(land_shape, grad.dtype), jax.ShapeDtypeStruct((8, 128), F32)),
        in_specs=(HBM, HBM), out_specs=(SEM, SEM, HBM, HBM, pl.BlockSpec(memory_space=pltpu.VMEM)),
        input_output_aliases={0: 2, 1: 3}, compiler_params=pltpu.CompilerParams(**SPLIT_PARAMS),
    )(pltpu.with_memory_space_constraint(grad, pltpu.HBM),
      pltpu.with_memory_space_constraint(lax.empty(land_shape, grad.dtype), pltpu.HBM))


def _await_owners(name, started, after):
    send_sems, recv_sems, g_thru, land_thru, _ = started

    def body(g_ref, land_ref, send_sems, recv_sems, after_ref, g_out, land_out):
        for cp in _owner_copies(g_ref, land_ref, send_sems, recv_sems):
            cp.wait_send()
            cp.wait_recv()

    return pl.pallas_call(
        body, name=name, out_shape=(pltpu.HBM(g_thru.shape, g_thru.dtype), pltpu.HBM(land_thru.shape, land_thru.dtype)),
        in_specs=(HBM, HBM, SEM, SEM, pl.BlockSpec(memory_space=pl.ANY)), out_specs=(HBM, HBM),
        input_output_aliases={0: 0, 1: 1}, compiler_params=pltpu.CompilerParams(**SPLIT_PARAMS),
    )(g_thru, land_thru, send_sems, recv_sems, after)


def _call(name, main, grid, in_specs, out_specs, out_shape, args, scratch=(), comm=None, prefetch=None, after=None):
    comm = comm or _NoComm()
    n_main = len(in_specs)
    if after is not None:
        in_specs = list(in_specs) + [pl.BlockSpec(after.shape, lambda *_: (0,) * after.ndim)]
        args = tuple(args) + (after,)
    counts = [len(in_specs), len(comm.inputs), len(out_shape), len(comm.out_shapes), len(scratch), len(comm.scratch)]
    n_steps = math.prod(grid)
    hosted = bool(comm.inputs)

    def body(*refs):
        if prefetch is not None:
            refs = refs[1:]
        parts, at = [], 0
        for n in counts:
            parts.append(refs[at:at + n])
            at += n
        ins, c_in, outs, c_out, scr, c_scr = parts
        ins = ins[:n_main]
        step = pl.program_id(0)
        for d in range(1, len(grid)):
            step = step * grid[d] + pl.program_id(d)
        if hosted:
            @pl.when(step == 0)
            def _():
                comm.start(c_in, c_out, c_scr)
        main(ins, outs, scr)
        if hosted:
            @pl.when(step == max(n_steps - 2, 0))
            def _():
                comm.mid(c_in, c_out, c_scr)

            @pl.when(step == n_steps - 1)
            def _():
                comm.finish(c_in, c_out, c_scr)

    all_in = list(in_specs) + [HBM] * counts[1]
    all_out = list(out_specs) + [HBM] * counts[3]
    all_scratch = list(scratch) + list(comm.scratch)
    shapes = list(out_shape) + list(comm.out_shapes)
    if prefetch is None:
        res = pl.pallas_call(body, name=name, grid=grid, in_specs=all_in, out_specs=all_out, out_shape=shapes,
                             scratch_shapes=all_scratch, compiler_params=_cparams())(*args, *comm.inputs)
    else:
        res = pl.pallas_call(
            body, name=name, out_shape=shapes,
            grid_spec=pltpu.PrefetchScalarGridSpec(num_scalar_prefetch=1, grid=grid, in_specs=all_in,
                                                   out_specs=all_out, scratch_shapes=all_scratch),
            compiler_params=_cparams())(prefetch, *args, *comm.inputs)
    return res[:counts[2]], res[counts[2]:]


def _rms_fwd(name, x, g, comm=None):
    T, D = x.shape
    tm = min(T, 512)

    def main(ins, outs, scr):
        x_ref, g_ref = ins
        xv = x_ref[...]
        r = lax.rsqrt(jnp.mean(xv * xv, axis=-1, keepdims=True) + RMS_EPS)
        outs[0][...] = (xv * r * g_ref[...]).astype(outs[0].dtype)

    tile = pl.BlockSpec((tm, D), lambda i: (i, 0))
    (h,), extra = _call(name, main, (T // tm,), [tile, pl.BlockSpec((1, D), lambda i: (0, 0))], [tile],
                        [jax.ShapeDtypeStruct((T, D), BF16)], (x, g), comm=comm)
    return h, extra


def _final_loss_epilogue(scale, out_scale):
    def ep(acc, ex, outs):
        res_ref, g_ref, t_ref = ex
        dx_ref, dxb_ref, dg_ref, loss_ref = outs
        n = acc.shape[-1]
        xv = res_ref[...] + scale * acc
        r = lax.rsqrt(jnp.mean(xv * xv, axis=-1, keepdims=True) + RMS_EPS)
        xhat = xv * r
        err = xhat * g_ref[...] - t_ref[...]

        @pl.when(pl.program_id(0) == 0)
        def _():
            dg_ref[...] = jnp.zeros_like(dg_ref)
            loss_ref[...] = jnp.zeros_like(loss_ref)

        loss_ref[...] += jnp.broadcast_to(jnp.sum(err * err) * (0.5 / n), loss_ref.shape)
        dy = err * (1.0 / n)
        dg_ref[...] += jnp.sum(dy * xhat, axis=0, keepdims=True)
        dxhat = dy * g_ref[...]
        dx = r * (dxhat - xhat * jnp.mean(dxhat * xhat, axis=-1, keepdims=True))
        dx_ref[...] = dx
        dxb_ref[...] = (out_scale * dx).astype(dxb_ref.dtype)
    return ep


def _mm_nstream(name, a, ws, w_sel, w_form, comps, out_dtypes, epilogue, cn, rows=1024, comm=None, after=None):
    T, K = a.shape
    N = ws[0].shape[1]
    rows = min(rows, T)
    assert N % cn == 0 and T % rows == 0
    n_w, n_c = len(ws), len(comps)

    def main(ins, outs, scr):
        a_ref = ins[0]
        w_refs = ins[1:1 + n_w]
        c_refs = ins[1 + n_w:1 + n_w + n_c]

        for r in range(T // rows):
            sl = slice(r * rows, (r + 1) * rows)
            a_blk = a_ref[sl, :]
            dots = [_dot(a_blk, w_ref[...], w_form) for w_ref in w_refs]
            res = epilogue(dots, [c_ref[sl, :] for c_ref in c_refs])
            for o_ref, o in zip(outs, res):
                o_ref[sl, :] = o.astype(o_ref.dtype)

    if w_form == "nt":
        w_specs = [pl.BlockSpec((None, cn, K), functools.partial(lambda j, s: (s, j, 0), s=s)) for s in w_sel]
    else:
        w_specs = [pl.BlockSpec((K, cn), lambda j: (0, j)) for _ in ws]
    chunk = pl.BlockSpec((T, cn), lambda j: (0, j))
    return _call(name, main, (N // cn,), [pl.BlockSpec((T, K), lambda j: (0, 0))] + w_specs + [chunk] * n_c,
                 [chunk] * len(out_dtypes), [jax.ShapeDtypeStruct((T, N), dt) for dt in out_dtypes],
                 (a, *ws, *comps), comm=comm, after=after)


def _mm_mstream(name, as_, ws, w_sel, w_form, extras, outs_desc, epilogue, tm=512, comm=None, after=None):
    T = as_[0].shape[0]
    tm = min(tm, T)
    n_a = len(as_)
    w_shapes = [w.shape[-2:] for w in ws]
    N = w_shapes[0][1] if w_form == "nn" else w_shapes[0][0]

    def main(ins, outs, scr):
        a_refs = ins[:n_a]
        w_refs = ins[n_a:2 * n_a]
        acc = None
        for a_ref, w_ref in zip(a_refs, w_refs):
            d = _dot(a_ref[...], w_ref[...], w_form)
            acc = d if acc is None else acc + d
        epilogue(acc, ins[2 * n_a:], outs)

    kind_spec = {"tile": pl.BlockSpec((tm, N), lambda i: (i, 0)), "vec": pl.BlockSpec((1, N), lambda i: (0, 0))}
    kind_shape = {"tile": (T, N), "vec": (1, N)}
    a_specs = [pl.BlockSpec((tm, a.shape[1]), lambda i: (i, 0)) for a in as_]
    w_specs = []
    for w, s in zip(ws, w_sel):
        if w.ndim == 3:
            w_specs.append(pl.BlockSpec((None,) + tuple(w.shape[1:]), functools.partial(lambda i, s: (s, 0, 0), s=s),
                                        pipeline_mode=pl.Buffered(1)))
        else:
            w_specs.append(pl.BlockSpec(tuple(w.shape), lambda i: (0, 0), pipeline_mode=pl.Buffered(1)))
    args = list(as_) + list(ws) + [e for e, _ in extras]
    return _call(name, main, (T // tm,), a_specs + w_specs + [kind_spec[k] for _, k in extras],
                 [kind_spec[k] for _, k in outs_desc],
                 [jax.ShapeDtypeStruct(kind_shape[k], dt) for dt, k in outs_desc], args, comm=comm, after=after)


def _residual_rms_epilogue(scale):
    def ep(acc, ex, outs):
        xv = ex[0][...] + scale * acc
        outs[0][...] = xv
        r = lax.rsqrt(jnp.mean(xv * xv, axis=-1, keepdims=True) + RMS_EPS)
        outs[1][...] = (xv * r * ex[1][...]).astype(outs[1].dtype)
    return ep


def _rms_bwd_epilogue(out_scale):
    def ep(acc, ex, outs):
        x_ref, g_ref, dres_ref = ex
        dx_ref, dxb_ref, dg_ref = outs
        xv = x_ref[...]
        r = lax.rsqrt(jnp.mean(xv * xv, axis=-1, keepdims=True) + RMS_EPS)
        xhat = xv * r

        @pl.when(pl.program_id(0) == 0)
        def _():
            dg_ref[...] = jnp.zeros_like(dg_ref)

        dg_ref[...] += jnp.sum(acc * xhat, axis=0, keepdims=True)
        dxhat = acc * g_ref[...]
        dx = r * (dxhat - xhat * jnp.mean(dxhat * xhat, axis=-1, keepdims=True)) + dres_ref[...]
        dx_ref[...] = dx
        dxb_ref[...] = (out_scale * dx).astype(dxb_ref.dtype)
    return ep


def _mm_tn(name, a, b, tmo, tno, out_dtype, tk=1024, comm=None, after=None):
    T, Ma = a.shape
    Nb = b.shape[1]
    tk = min(tk, T)
    nk = T // tk

    def main(ins, outs, scr):
        a_ref, b_ref = ins
        (acc_ref,) = scr
        k = pl.program_id(2)

        @pl.when(k == 0)
        def _():
            acc_ref[...] = jnp.zeros_like(acc_ref)

        acc_ref[...] += _dot(a_ref[...], b_ref[...], "tn")

        @pl.when(k == nk - 1)
        def _():
            outs[0][...] = acc_ref[...].astype(outs[0].dtype)

    (out,), extra = _call(
        name, main, (Ma // tmo, Nb // tno, nk),
        [pl.BlockSpec((tk, tmo), lambda i, j, k: (k, i)), pl.BlockSpec((tk, tno), lambda i, j, k: (k, j))],
        [pl.BlockSpec((tmo, tno), lambda i, j, k: (i, j))], [jax.ShapeDtypeStruct((Ma, Nb), out_dtype)],
        (a, b), scratch=[pltpu.VMEM((tmo, tno), F32)], comm=comm, after=after)
    return out, extra


def _swiglu_parts(g, u):
    s = _sigmoid(g)
    silu = g * s
    return [u * (s + silu * (1.0 - s)), silu, silu * u]


def _silu_mul_epilogue(dots, comps):
    g, u = dots
    return _swiglu_parts(g, u)


def _gate_parts_epilogue(dots, comps):
    (g,) = dots
    s = _sigmoid(g)
    silu = g * s
    return [s + silu * (1.0 - s), silu]


def _up_act_epilogue(dots, comps):
    (u,) = dots
    return [u * comps[0].astype(F32), u * comps[1].astype(F32)]


def _dact_epilogue(dots, comps):
    dact = dots[0].astype(BF16)
    return [dact * comps[0], dact * comps[1]]


def _identity_epilogue(dots, comps):
    return list(dots)


def _swap_halves(x):
    lane = lax.broadcasted_iota(jnp.int32, x.shape, 1)
    first = (lane % DK) < (DK // 2)
    return jnp.where(first, pltpu.roll(x, 128 - DK // 2, 1), pltpu.roll(x, DK // 2, 1))


def _rotary(t, cos, sin_signed):
    halves = []
    for p in range(QK_W // 128):
        th = t[:, 128 * p:128 * (p + 1)]
        halves.append(th * cos + _swap_halves(th) * sin_signed)
    return jnp.concatenate(halves, axis=1)


def _rotary_transposed(d, cos, sin_signed):
    halves = []
    for p in range(QK_W // 128):
        dh = d[:, 128 * p:128 * (p + 1)]
        halves.append(dh * cos + _swap_halves(dh * sin_signed))
    return jnp.concatenate(halves, axis=1)


def _log_sigmoid(x):
    return jnp.minimum(x, 0.0) - jnp.log(1.0 + jnp.exp(-jnp.abs(x)))


def _attn_masks():
    row = lax.broadcasted_iota(jnp.int32, (SUPER, SUPER), 0)
    col = lax.broadcasted_iota(jnp.int32, (SUPER, SUPER), 1)
    same = (row // CHUNK) == (col // CHUNK)
    return row, col, same


def _group_inputs(grp, pr, cos, sin_signed, lg, wa2, ba):
    seg = lambda lo, width: pr[:, lo:lo + width].astype(F32)
    if grp == 0:
        q = _rotary(seg(C_RQ, QK_W), cos, sin_signed)
        k = _rotary(seg(C_RK, QK_W), cos, sin_signed) * (DK ** -0.5)
        v = pr[:, C_RV:C_RV + V_W]
        gate = seg(C_RG, V_W)
        pos = lax.broadcasted_iota(jnp.int32, (SUPER, QK_W), 0).astype(F32) + 1.0
        return q, k, v, gate, pos * lg, None, None
    q = seg(C_GQ, QK_W) * (DK ** -0.5)
    k = seg(C_GK, QK_W)
    v = pr[:, C_GV:C_GV + V_W]
    gate = seg(C_GG, V_W)
    glow = pr[:, C_GL:C_GL + GL_W]
    logit = _dot(glow.astype(BF16), wa2.astype(BF16), "nn") + ba
    la = _log_sigmoid(logit) * (1.0 / GATE_NORM)
    row, col, _ = _attn_masks()
    lower = (col <= row).astype(F32)
    b_cum = _dot(lower, la, "nn", precision=lax.Precision.HIGHEST)
    return q, k, v, gate, b_cum, glow, logit


def _decay_factors(q, k, b_cum):
    c = b_cum[SUPER // 2 - 1:SUPER // 2, :]
    bl = b_cum[SUPER - 1:SUPER, :]
    e1 = jnp.exp(b_cum - c)
    e2 = jnp.exp(c - b_cum)
    e_b = jnp.exp(b_cum)
    e_l = jnp.exp(bl - b_cum)
    return dict(e1=e1, e2=e2, eb=e_b, el=e_l, ebl=jnp.exp(bl),
                qp=q * e1, qm=q * e2, kp=k * e1, km=k * e2, qs=q * e_b, kl=k * e_l)


def _state_block_mask():
    r = lax.broadcasted_iota(jnp.int32, (V_W, QK_W), 0)
    c = lax.broadcasted_iota(jnp.int32, (V_W, QK_W), 1)
    return (r // DV) == (c // DK)


def _attn_fwd(proj, cos, sin_signed, lg, wa2p, ba, gn_ret, gn_gla, x_res, w_out, g_next, comm=None):
    T = proj.shape[0]
    n_s = T // SUPER
    D = x_res.shape[1]

    def main(ins, outs, scr):
        pr_ref, cos_ref, sin_ref, lg_ref, wa2_ref, ba_ref, gr_ref, gg_ref, xres_ref, wout_ref, gnext_ref = ins
        o_ref, y_ref, st_ref, x_ref, h_ref = outs
        (s_ref,) = scr
        i = pl.program_id(0)

        @pl.when(i == 0)
        def _():
            s_ref[...] = jnp.zeros_like(s_ref)

        pr = pr_ref
        row, col, same = _attn_masks()
        m1 = col <= row
        m2 = jnp.logical_and(col > row, same)
        lane = lax.broadcasted_iota(jnp.int32, (1, QK_W), 1)
        blockmask = _state_block_mask()
        for grp in range(2):
            q, k, v, gate, b_cum, _, _ = _group_inputs(grp, pr, cos_ref[...], sin_ref[...], lg_ref[...],
                                                      wa2_ref[...], ba_ref[...])
            f = _decay_factors(q, k, b_cum)
            gn = gr_ref[...] if grp == 0 else gg_ref[...]
            s_prev = s_ref[grp]
            st_ref[0, grp] = s_prev
            o_inter = _dot(f["qs"].astype(BF16), s_prev.astype(BF16), "nt")
            kmb = f["km"].astype(BF16)
            kpb = f["kp"].astype(BF16)
            qpb = f["qp"].astype(BF16)
            qmb = f["qm"].astype(BF16)
            vb = v.astype(BF16)
            zero = jnp.zeros((), BF16)
            for h in range(HEADS):
                hm = (lane // DK) == h
                a1 = _dot(jnp.where(hm, qpb, zero), kmb, "nt").astype(BF16)
                a2 = _dot(jnp.where(hm, qmb, zero), kpb, "nt").astype(BF16)
                a = jnp.where(m1, a1, jnp.where(m2, a2, zero))
                lo = grp * V_W + h * DV
                o_h = _dot(a, vb[:, h * DV:(h + 1) * DV], "nn") + o_inter[:, h * DV:(h + 1) * DV]
                o_ref[:, lo:lo + DV] = o_h
                r = lax.rsqrt(jnp.mean(o_h * o_h, axis=-1, keepdims=True) + RMS_EPS)
                gte = gate[:, h * DV:(h + 1) * DV]
                y = o_h * r * gn[:, h * DV:(h + 1) * DV] * (gte * _sigmoid(gte))
                y_ref[:, lo:lo + DV] = y.astype(y_ref.dtype)
            upd = _dot(vb, f["kl"].astype(BF16), "tn")
            s_ref[grp] = s_prev * f["ebl"] + jnp.where(blockmask, upd, 0.0)
        xv = xres_ref[...] + _dot(y_ref[...], wout_ref[...], "nn")
        x_ref[...] = xv
        r = lax.rsqrt(jnp.mean(xv * xv, axis=-1, keepdims=True) + RMS_EPS)
        h_ref[...] = (xv * r * gnext_ref[...]).astype(h_ref.dtype)

    const = lambda shape: pl.BlockSpec(shape, lambda i: tuple(0 for _ in shape))
    rows = lambda w: pl.BlockSpec((SUPER, w), lambda i: (i, 0))
    return _call(
        "attn_fwd", main, (n_s,),
        [rows(PROJ_W), rows(128), rows(128),
         const((1, QK_W)), const((GL_W, QK_W)), const((1, QK_W)), const((1, V_W)), const((1, V_W)),
         rows(D), const((2 * V_W, D)), const((1, D))],
        [rows(2 * V_W), rows(2 * V_W), pl.BlockSpec((1, 2, V_W, QK_W), lambda i: (i, 0, 0, 0)), rows(D), rows(D)],
        [jax.ShapeDtypeStruct((T, 2 * V_W), F32), jax.ShapeDtypeStruct((T, 2 * V_W), BF16),
         jax.ShapeDtypeStruct((n_s, 2, V_W, QK_W), F32), jax.ShapeDtypeStruct((T, D), F32),
         jax.ShapeDtypeStruct((T, D), BF16)],
        (proj, cos, sin_signed, lg, wa2p, ba, gn_ret, gn_gla, x_res, w_out, g_next),
        scratch=[pltpu.VMEM((2, V_W, QK_W), F32)], comm=comm)


def _attn_bwd(proj, cos, sin_signed, lg, wa2p, ba, gn_ret, gn_gla, o, dx, w_out, states, comm=None, after=None):
    T = proj.shape[0]
    n_s = T // SUPER
    D = dx.shape[1]

    def main(ins, outs, scr):
        pr_ref, cos_ref, sin_ref, lg_ref, wa2_ref, ba_ref, gr_ref, gg_ref, o_ref, dx_ref, wout_ref, st_ref = ins
        dp_ref, dgr_ref, dgg_ref, dba_ref, dwa_ref = outs
        (ds_ref, dy_ref) = scr
        i = pl.program_id(0)
        dy_ref[...] = _dot(dx_ref[...], wout_ref[...], "nt")

        @pl.when(i == 0)
        def _():
            ds_ref[...] = jnp.zeros_like(ds_ref)
            dgr_ref[...] = jnp.zeros_like(dgr_ref)
            dgg_ref[...] = jnp.zeros_like(dgg_ref)
            dba_ref[...] = jnp.zeros_like(dba_ref)
            dwa_ref[...] = jnp.zeros_like(dwa_ref)

        pr = pr_ref
        cos = cos_ref[...]
        sin_signed = sin_ref[...]
        row, col, same = _attn_masks()
        m1 = col <= row
        m2 = jnp.logical_and(col > row, same)
        m1t = row <= col
        m2t = jnp.logical_and(row > col, same)
        lane = lax.broadcasted_iota(jnp.int32, (1, QK_W), 1)
        blockmask = _state_block_mask()
        for grp in range(2):
            q, k, v, gate, b_cum, glow, logit = _group_inputs(grp, pr, cos, sin_signed, lg_ref[...],
                                                              wa2_ref[...], ba_ref[...])
            f = _decay_factors(q, k, b_cum)
            gn = gr_ref[...] if grp == 0 else gg_ref[...]
            dgn_ref = dgr_ref if grp == 0 else dgg_ref
            do_parts, dgate_parts, dgn_parts = [], [], []
            for h in range(HEADS):
                lo = grp * V_W + h * DV
                o_h = o_ref[:, lo:lo + DV]
                r = lax.rsqrt(jnp.mean(o_h * o_h, axis=-1, keepdims=True) + RMS_EPS)
                n = o_h * r
                gte = gate[:, h * DV:(h + 1) * DV]
                sg = _sigmoid(gte)
                dy_h = dy_ref[:, lo:lo + DV]
                gn_h = gn[:, h * DV:(h + 1) * DV]
                dgate_parts.append(dy_h * n * gn_h * (sg * (1.0 + gte * (1.0 - sg))))
                dz = dy_h * (gte * sg)
                dgn_parts.append(jnp.sum(dz * n, axis=0, keepdims=True))
                dn = dz * gn_h
                do_parts.append(r * (dn - n * jnp.mean(dn * n, axis=-1, keepdims=True)))
            dgn_ref[...] += jnp.concatenate(dgn_parts, axis=1)
            dgate = jnp.concatenate(dgate_parts, axis=1)
            do = jnp.concatenate(do_parts, axis=1)
            dob = do.astype(BF16)
            vb = v.astype(BF16)
            s_prev = st_ref[0, grp]
            ds_new = ds_ref[grp]
            dsb = ds_new.astype(BF16)
            qpb, qmb = f["qp"].astype(BF16), f["qm"].astype(BF16)
            kpb, kmb = f["kp"].astype(BF16), f["km"].astype(BF16)
            dqp = jnp.zeros((SUPER, QK_W), F32)
            dqm = jnp.zeros((SUPER, QK_W), F32)
            dkp = jnp.zeros((SUPER, QK_W), F32)
            dkm = jnp.zeros((SUPER, QK_W), F32)
            dv_parts = []
            zero = jnp.zeros((), BF16)
            for h in range(HEADS):
                hm = (lane // DK) == h
                qp_h = jnp.where(hm, qpb, zero)
                qm_h = jnp.where(hm, qmb, zero)
                kp_h = jnp.where(hm, kpb, zero)
                km_h = jnp.where(hm, kmb, zero)
                at = jnp.where(m1t, _dot(km_h, qpb, "nt").astype(BF16),
                               jnp.where(m2t, _dot(kp_h, qmb, "nt").astype(BF16), zero))
                do_h = dob[:, h * DV:(h + 1) * DV]
                v_h = vb[:, h * DV:(h + 1) * DV]
                dv_parts.append(_dot(at, do_h, "nn"))
                da = _dot(do_h, v_h, "nt").astype(BF16)
                dat = _dot(v_h, do_h, "nt").astype(BF16)
                da1 = jnp.where(m1, da, zero)
                da2 = jnp.where(m2, da, zero)
                da1t = jnp.where(m1t, dat, zero)
                da2t = jnp.where(m2t, dat, zero)
                dqp = dqp + _dot(da1, km_h, "nn")
                dqm = dqm + _dot(da2, kp_h, "nn")
                dkm = dkm + _dot(da1t, qp_h, "nn")
                dkp = dkp + _dot(da2t, qm_h, "nn")
            klb = f["kl"].astype(BF16)
            qsb = f["qs"].astype(BF16)
            dqs = _dot(dob, s_prev.astype(BF16), "nn")
            dkl = _dot(vb, dsb, "nn")
            dv = jnp.concatenate(dv_parts, axis=1) + _dot(klb, dsb, "nt")
            ds_ref[grp] = ds_new * f["ebl"] + jnp.where(blockmask, _dot(dob, qsb, "tn"), 0.0)
            dq = dqp * f["e1"] + dqm * f["e2"] + dqs * f["eb"]
            dk = dkm * f["e2"] + dkp * f["e1"] + dkl * f["el"]
            if grp == 0:
                dq = _rotary_transposed(dq, cos, sin_signed)
                dk = _rotary_transposed(dk * (DK ** -0.5), cos, sin_signed)
                dp_ref[:, C_RQ:C_RQ + QK_W] = dq.astype(dp_ref.dtype)
                dp_ref[:, C_RK:C_RK + QK_W] = dk.astype(dp_ref.dtype)
                dp_ref[:, C_RV:C_RV + V_W] = dv.astype(dp_ref.dtype)
                dp_ref[:, C_RG:C_RG + V_W] = dgate.astype(dp_ref.dtype)
            else:
                dkl_kl = dkl * klb.astype(F32)
                db = (dqp * qpb.astype(F32) - dkm * kmb.astype(F32) - dqm * qmb.astype(F32)
                      + dkp * kpb.astype(F32) + dqs * qsb.astype(F32) - dkl_kl)
                last = (jnp.sum(dkl_kl, axis=0, keepdims=True)
                        + f["ebl"] * jnp.sum(s_prev * ds_new, axis=0, keepdims=True))
                rowq = lax.broadcasted_iota(jnp.int32, (SUPER, QK_W), 0)
                db = db + jnp.where(rowq == SUPER - 1, last, 0.0)
                upper = (col >= row).astype(F32)
                dla = _dot(upper, db, "nn", precision=lax.Precision.HIGHEST)
                dlogit = dla * (1.0 / GATE_NORM) * (1.0 - _sigmoid(logit))
                dlb = dlogit.astype(BF16)
                dglow = _dot(dlb, wa2_ref[...].astype(BF16), "nt")
                dwa_ref[...] += _dot(glow.astype(BF16), dlb, "tn")
                dba_ref[...] += jnp.sum(dlogit, axis=0, keepdims=True)
                dp_ref[:, C_GQ:C_GQ + QK_W] = (dq * (DK ** -0.5)).astype(dp_ref.dtype)
                dp_ref[:, C_GK:C_GK + QK_W] = dk.astype(dp_ref.dtype)
                dp_ref[:, C_GV:C_GV + V_W] = dv.astype(dp_ref.dtype)
                dp_ref[:, C_GG:C_GG + V_W] = dgate.astype(dp_ref.dtype)
                dp_ref[:, C_GL:C_GL + GL_W] = dglow.astype(dp_ref.dtype)

    rev = lambda i: n_s - 1 - i
    const = lambda shape: pl.BlockSpec(shape, lambda i: tuple(0 for _ in shape))
    return _call(
        "attn_bwd", main, (n_s,),
        [pl.BlockSpec((SUPER, PROJ_W), lambda i: (rev(i), 0)),
         pl.BlockSpec((SUPER, 128), lambda i: (rev(i), 0)), pl.BlockSpec((SUPER, 128), lambda i: (rev(i), 0)),
         const((1, QK_W)), const((GL_W, QK_W)), const((1, QK_W)), const((1, V_W)), const((1, V_W)),
         pl.BlockSpec((SUPER, 2 * V_W), lambda i: (rev(i), 0)),
         pl.BlockSpec((SUPER, D), lambda i: (rev(i), 0)), const((2 * V_W, D)),
         pl.BlockSpec((1, 2, V_W, QK_W), lambda i: (rev(i), 0, 0, 0))],
        [pl.BlockSpec((SUPER, PROJ_W), lambda i: (rev(i), 0)),
         const((1, V_W)), const((1, V_W)), const((1, QK_W)), const((GL_W, QK_W))],
        [jax.ShapeDtypeStruct((T, PROJ_W), BF16),
         jax.ShapeDtypeStruct((1, V_W), F32), jax.ShapeDtypeStruct((1, V_W), F32),
         jax.ShapeDtypeStruct((1, QK_W), F32), jax.ShapeDtypeStruct((GL_W, QK_W), F32)],
        (proj, cos, sin_signed, lg, wa2p, ba, gn_ret, gn_gla, o, dx, w_out, states),
        scratch=[pltpu.VMEM((2, V_W, QK_W), F32), pltpu.VMEM((SUPER, 2 * V_W), F32)], comm=comm, after=after)


def _rotary_tables(T):
    half = DK // 2
    inv = ROPE_BASE ** (-jnp.arange(half, dtype=F32) * 2.0 / DK)
    ang = jnp.arange(T, dtype=F32)[:, None] * inv[None, :]
    cos, sin = jnp.cos(ang), jnp.sin(ang)
    cos_head = jnp.concatenate([cos, cos], axis=1)
    sin_head = jnp.concatenate([-sin, sin], axis=1)
    return jnp.tile(cos_head, (1, 128 // DK)), jnp.tile(sin_head, (1, 128 // DK))


def _sum_devices(name, gathered, m_per):
    def body(g_ref, o_ref):
        acc = g_ref[0:m_per, :]
        for k in range(1, N_DEV):
            acc = acc + g_ref[k * m_per:(k + 1) * m_per, :]
        o_ref[...] = acc

    return pl.pallas_call(body, name=name, out_shape=jax.ShapeDtypeStruct((m_per, 128), F32))(gathered)


def _owner_sums(name, items, owner, comm=None):
    counts = [1 + len(landed) for _, landed in items]

    def main(ins, outs, scr):
        at = 0
        for o_ref, n in zip(outs, counts):
            acc = ins[at][...].astype(F32)
            for l_ref in ins[at + 1:at + n]:
                for j in range(l_ref.shape[0]):
                    acc = acc + l_ref[j].astype(F32)
            o_ref[...] = acc
            at += n

    once = pl.Buffered(1)
    in_specs, out_specs, out_shape, args = [], [], [], []
    for grad, landed in items:
        R, C = grad.shape[-2:]
        in_specs.append(pl.BlockSpec((None, None, R, C), lambda i, s: (s[0], s[1], 0, 0), pipeline_mode=once))
        in_specs += [pl.BlockSpec(tuple(l.shape), lambda i, s: (0, 0, 0), pipeline_mode=once) for l in landed]
        out_specs.append(pl.BlockSpec((R, C), lambda i, s: (0, 0)))
        out_shape.append(jax.ShapeDtypeStruct((R, C), F32))
        args += [grad, *landed]
    return _call(name, main, (1,), in_specs, out_specs, out_shape, args, comm=comm, prefetch=owner)


def _adamw_group(name, items, n_blocks, comm=None):
    n = len(items)

    def main(ins, outs, scr):
        for p in range(n):
            g_ref, w_ref, m_ref, v_ref = ins[4 * p:4 * p + 4]
            d_ref, nm_ref, nv_ref = outs[3 * p:3 * p + 3]
            gv = g_ref[...]
            nm = ADAM_B1 * m_ref[...] + (1.0 - ADAM_B1) * gv
            nv = ADAM_B2 * v_ref[...] + (1.0 - ADAM_B2) * (gv * gv)
            m_hat = nm / (1.0 - ADAM_B1 ** ADAM_STEP)
            v_hat = nv / (1.0 - ADAM_B2 ** ADAM_STEP)
            d_ref[...] = -ADAM_LR * (m_hat / (jnp.sqrt(v_hat) + ADAM_EPS) + ADAM_WD * w_ref[...])
            nm_ref[...] = nm
            nv_ref[...] = nv

    in_specs, out_specs, out_shape, args = [], [], [], []
    for item in items:
        R, C = item[1].shape
        assert R % n_blocks == 0
        spec = pl.BlockSpec((R // n_blocks, C), lambda i: (i, 0))
        in_specs += [spec] * 4
        out_specs += [spec] * 3
        out_shape += [jax.ShapeDtypeStruct((R, C), F32)] * 3
        args += list(item)
    outs, extra = _call(name, main, (n_blocks,), in_specs, out_specs, out_shape, args, comm=comm)
    return [tuple(outs[3 * p:3 * p + 3]) for p in range(n)], extra


SMALL_ORDER = ("ffn1", "mix", "ffn2", "final", "ret", "gla", "b_a")


def kernel(x, ffn1_norm_g, ffn1_w_gate, ffn1_w_up, ffn1_w_down, mix_norm_g, w_in, ret_norm_g, gla_w_a2, gla_b_a, gla_norm_g, w_out, ffn2_norm_g, ffn2_w_gate, ffn2_w_up, ffn2_w_down, final_norm_g, loss_target, m_ffn1_norm_g, m_ffn1_w_gate, m_ffn1_w_up, m_ffn1_w_down, m_mix_norm_g, m_w_in, m_ret_norm_g, m_gla_w_a2, m_gla_b_a, m_gla_norm_g, m_w_out, m_ffn2_norm_g, m_ffn2_w_gate, m_ffn2_w_up, m_ffn2_w_down, m_final_norm_g, v_ffn1_norm_g, v_ffn1_w_gate, v_ffn1_w_up, v_ffn1_w_down, v_mix_norm_g, v_w_in, v_ret_norm_g, v_gla_w_a2, v_gla_b_a, v_gla_norm_g, v_w_out, v_ffn2_norm_g, v_ffn2_w_gate, v_ffn2_w_up, v_ffn2_w_down, v_final_norm_g):
    xi, yi, ci = _coords()
    dev = 4 * xi + 2 * yi + ci
    owner = jnp.stack([2 * xi + yi, ci]).astype(jnp.int32)

    x0, target = x[0], loss_target[0]
    T, D = x0.shape
    fb = ffn1_w_gate.shape[2]
    ib = w_in.shape[2]
    ab = gla_w_a2.shape[2]
    F = N_DEV * fb
    cos, sin_signed = _rotary_tables(T)
    lg = jnp.repeat(jnp.log(1.0 - 2.0 ** (-5.0 - jnp.arange(HEADS, dtype=F32))), DK)[None, :]
    g_final = final_norm_g.reshape(1, D)

    g1_loc = ffn1_w_gate[0].T[None].astype(BF16)
    u1_loc = ffn1_w_up[0].T[None].astype(BF16)
    d1_loc = ffn1_w_down.astype(BF16)
    g2_loc = ffn2_w_gate[0].T[None].astype(BF16)
    u2_loc = ffn2_w_up[0].T[None].astype(BF16)
    d2_loc = ffn2_w_down.astype(BF16)
    in_loc = w_in[0].T.astype(BF16)
    out_loc = w_out[0].astype(BF16)

    h1, (g1,) = _rms_fwd("ffn1_rms", x0, ffn1_norm_g, comm=_AllGather([g1_loc], ["stack"]))
    g1 = g1.reshape(1, F, D)
    (dsl1, sl1), (u1,) = _mm_nstream("ffn1_gate", h1, [g1], [0], "nt", [], [BF16, BF16], _gate_parts_epilogue, cn=256,
                                     comm=_AllGather([u1_loc], ["stack"]))
    u1 = u1.reshape(1, F, D)
    (dsu1, act1), (d1,) = _mm_nstream("ffn1_up", h1, [u1], [0], "nt", [dsl1, sl1], [BF16, BF16],
                                      _up_act_epilogue, cn=256, comm=_AllGather([d1_loc], ["stack"]))
    d1 = d1.reshape(1, F, D)
    f32_tile, bf16_tile, f32_vec = (F32, "tile"), (BF16, "tile"), (F32, "vec")
    (x1, h2), (in_all, a_all) = _mm_mstream(
        "ffn1_down", [act1], [d1], [0], "nn", [(x0, "tile"), (mix_norm_g, "vec")], [f32_tile, bf16_tile],
        _residual_rms_epilogue(0.5), comm=_AllGather([in_loc, gla_w_a2[0]], ["plain", "plain"]))
    w_in_t = jnp.pad(in_all.reshape(1, N_DEV * ib, D), ((0, 0), (0, PROJ_W - N_DEV * ib), (0, 0)))
    wa2 = jnp.transpose(a_all, (1, 0, 2)).reshape(GATE_RANK, N_DEV * ab)
    wa2p = jnp.pad(wa2, ((0, GL_W - GATE_RANK), (0, 0)))

    (proj,), (g2, out_all) = _mm_nstream("mix_proj", h2, [w_in_t], [0], "nt", [], [BF16], _identity_epilogue, cn=640,
                                         comm=_AllGather([g2_loc, out_loc], ["stack", "plain"]))
    w_out_full = out_all.reshape(D, D)
    (o, ymix, states, x2, h3), (u2,) = _attn_fwd(proj, cos, sin_signed, lg, wa2p, gla_b_a, ret_norm_g, gla_norm_g,
                                                 x1, w_out_full, ffn2_norm_g, comm=_AllGather([u2_loc], ["stack"]))
    g2, u2 = g2.reshape(1, F, D), u2.reshape(1, F, D)

    (dsu2, sl2, act2), (d2,) = _mm_nstream(
        "ffn2_up", h3, [g2, u2], [0, 0], "nt", [], [BF16, BF16, BF16], _silu_mul_epilogue, cn=256,
        comm=_AllGather([d2_loc], ["stack"]))
    d2 = d2.reshape(1, F, D)
    (dx3, dy3b, d_final, loss), _ = _mm_mstream(
        "ffn2_down", [act2], [d2], [0], "nn", [(x2, "tile"), (g_final, "vec"), (target, "tile")],
        [f32_tile, bf16_tile, f32_vec, f32_vec], _final_loss_epilogue(0.5, 0.5))

    sent = {}

    def send(nm, grad):
        sent[nm] = _send_to_owners("send_" + nm, grad)
        return sent[nm][4]

    dwd2, _ = _mm_tn("ffn2b_dwd", act2, dy3b, F // 2, D, BF16)
    tok = send("wd2", dwd2.reshape(4, 2, fb, D))
    (dgate2, dup2), _ = _mm_nstream("ffn2b_dact", dy3b, [d2], [0], "nt", [dsu2, sl2], [BF16, BF16],
                                    _dact_epilogue, cn=256, after=tok)
    dwg2, _ = _mm_tn("ffn2b_dwg", dgate2, h3, F // 2, D, BF16)
    tok = send("wg2", dwg2.reshape(4, 2, fb, D))
    dwu2, _ = _mm_tn("ffn2b_dwu", dup2, h3, F // 2, D, BF16, after=tok)
    tok = send("wu2", dwu2.reshape(4, 2, fb, D))
    rms_outs = [f32_tile, bf16_tile, f32_vec]
    (dx2, dx2b, d_g2), _ = _mm_mstream(
        "ffn2b_dh", [dgate2, dup2], [g2, u2], [0, 0], "nn", [(x2, "tile"), (ffn2_norm_g, "vec"), (dx3, "tile")],
        rms_outs, _rms_bwd_epilogue(1.0), after=tok)

    dwout, _ = _mm_tn("mixb_dwout", ymix, dx2b, D, D, BF16)
    tok = send("wout", dwout.reshape(4, 2, D // N_DEV, D))
    (dproj, d_ret, d_gla, d_ba, d_wa2p), _ = _attn_bwd(
        proj, cos, sin_signed, lg, wa2p, gla_b_a, ret_norm_g, gla_norm_g, o, dx2b, w_out_full, states, after=tok)
    dwin_t, _ = _mm_tn("mixb_dwin", dproj, h2, 640, D, BF16, tk=2048)
    tok = send("win", dwin_t[:N_DEV * ib].reshape(4, 2, ib, D))
    (dx1, dy1b, d_gmix), _ = _mm_mstream(
        "mixb_dh", [dproj], [w_in_t], [0], "nn", [(x1, "tile"), (mix_norm_g, "vec"), (dx2, "tile")],
        rms_outs, _rms_bwd_epilogue(0.5), after=tok)

    dwd1, _ = _mm_tn("ffn1b_dwd", act1, dy1b, F // 2, D, BF16)
    tok = send("wd1", dwd1.reshape(4, 2, fb, D))
    (dgate1, dup1), _ = _mm_nstream("ffn1b_dact", dy1b, [d1], [0], "nt", [dsu1, sl1], [BF16, BF16],
                                    _dact_epilogue, cn=256, after=tok)
    dwg1, _ = _mm_tn("ffn1b_dwg", dgate1, h1, F // 2, D, BF16)
    tok = send("wg1", dwg1.reshape(4, 2, fb, D))
    dwu1, _ = _mm_tn("ffn1b_dwu", dup1, h1, F // 2, D, BF16, after=tok)
    tok = send("wu1", dwu1.reshape(4, 2, fb, D))
    (dx0, _, d_g1), _ = _mm_mstream(
        "ffn1b_dh", [dgate1, dup1], [g1, u1], [0, 0], "nn", [(x0, "tile"), (ffn1_norm_g, "vec"), (dx1, "tile")],
        rms_outs, _rms_bwd_epilogue(1.0), after=tok)

    small = dict(ffn1=d_g1, mix=d_gmix, ffn2=d_g2, final=d_final, ret=d_ret, gla=d_gla, b_a=d_ba)
    flat = jnp.concatenate([small[k].reshape(-1) for k in SMALL_ORDER]
                           + [d_wa2p[:GATE_RANK].reshape(-1), loss[0, :128]])
    rows = -(-flat.shape[0] // 128)
    rows = -(-rows // 8) * 8
    packed = jnp.pad(flat, (0, rows * 128 - flat.shape[0])).reshape(rows, 128)

    transposed = ("ffn1_w_gate", "ffn1_w_up", "ffn2_w_gate", "ffn2_w_up", "w_in")

    def to_2d(nm, a):
        if nm in transposed:
            return a[0].T
        return a.reshape((1, a.shape[0]) if a.ndim == 1 else a.shape[-2:])

    def from_2d(nm, a):
        return a.T[None] if nm in transposed else a.reshape(params[nm][0].shape)

    def arrived(nm, after):
        grad, landed = _await_owners("await_" + nm, sent[nm], after)
        return grad, [landed]

    sums_a, (gathered,) = _owner_sums(
        "sum_a", [arrived(nm, dx0) for nm in ("wg2", "wu2", "wd2", "win", "wout")], owner,
        comm=_AllGather([packed], ["plain"]))
    params = dict(
        ffn2_w_gate=(ffn2_w_gate, m_ffn2_w_gate, v_ffn2_w_gate), ffn2_w_up=(ffn2_w_up, m_ffn2_w_up, v_ffn2_w_up),
        ffn2_w_down=(ffn2_w_down, m_ffn2_w_down, v_ffn2_w_down), w_in=(w_in, m_w_in, v_w_in),
        w_out=(w_out, m_w_out, v_w_out), ffn1_w_gate=(ffn1_w_gate, m_ffn1_w_gate, v_ffn1_w_gate),
        ffn1_w_up=(ffn1_w_up, m_ffn1_w_up, v_ffn1_w_up), ffn1_w_down=(ffn1_w_down, m_ffn1_w_down, v_ffn1_w_down),
        ffn1_norm_g=(ffn1_norm_g, m_ffn1_norm_g, v_ffn1_norm_g), mix_norm_g=(mix_norm_g, m_mix_norm_g, v_mix_norm_g),
        ret_norm_g=(ret_norm_g, m_ret_norm_g, v_ret_norm_g), gla_w_a2=(gla_w_a2, m_gla_w_a2, v_gla_w_a2),
        gla_b_a=(gla_b_a, m_gla_b_a, v_gla_b_a), gla_norm_g=(gla_norm_g, m_gla_norm_g, v_gla_norm_g),
        ffn2_norm_g=(ffn2_norm_g, m_ffn2_norm_g, v_ffn2_norm_g), final_norm_g=(final_norm_g, m_final_norm_g, v_final_norm_g))
    grads, updates = {}, {}

    def run_adam(name, names, grad_2d, n_blocks):
        items = [(grad_2d[nm],) + tuple(to_2d(nm, a) for a in params[nm]) for nm in names]
        res, _ = _adamw_group(name, items, n_blocks)
        for nm, r in zip(names, res):
            grads[nm] = from_2d(nm, grad_2d[nm])
            updates[nm] = tuple(from_2d(nm, a) for a in r)
        return res

    grads_a = {"ffn2_w_gate": sums_a[0], "ffn2_w_up": sums_a[1], "ffn2_w_down": sums_a[2], "w_out": sums_a[4]}
    run_adam("adamw_w_in", ["w_in"], {"w_in": sums_a[3]}, 1)
    done_a = run_adam("adamw_a", list(grads_a), grads_a, 4)[0][0]
    sums_b, _ = _owner_sums("sum_b", [arrived(nm, done_a) for nm in ("wg1", "wu1", "wd1")], owner)
    grads_b = {"ffn1_w_gate": sums_b[0], "ffn1_w_up": sums_b[1], "ffn1_w_down": sums_b[2]}
    run_adam("adamw_b", list(grads_b), grads_b, 4)

    total = _sum_devices("sum_small", gathered.reshape(N_DEV * rows, 128), rows).reshape(-1)
    sizes = [small[k].size for k in SMALL_ORDER] + [GATE_RANK * QK_W, 128]
    offs = [0]
    for s in sizes:
        offs.append(offs[-1] + s)
    pieces = [total[offs[i]:offs[i + 1]] for i in range(len(sizes))]
    g_small = {k: pieces[i].reshape(small[k].shape) for i, k in enumerate(SMALL_ORDER)}
    g_wa2_full = pieces[len(SMALL_ORDER)].reshape(GATE_RANK, QK_W)
    g_wa2 = lax.dynamic_slice(g_wa2_full, (0, dev * ab), (GATE_RANK, ab))
    loss_total = pieces[len(SMALL_ORDER) + 1][0]

    small_grads = {"ffn1_norm_g": g_small["ffn1"], "mix_norm_g": g_small["mix"], "ret_norm_g": g_small["ret"],
                   "gla_w_a2": g_wa2, "gla_b_a": g_small["b_a"], "gla_norm_g": g_small["gla"],
                   "ffn2_norm_g": g_small["ffn2"], "final_norm_g": g_small["final"]}
    run_adam("adamw_small", list(small_grads), small_grads, 1)

    order = ("ffn1_norm_g", "ffn1_w_gate", "ffn1_w_up", "ffn1_w_down", "mix_norm_g", "w_in", "ret_norm_g", "gla_w_a2",
             "gla_b_a", "gla_norm_g", "w_out", "ffn2_norm_g", "ffn2_w_gate", "ffn2_w_up", "ffn2_w_down", "final_norm_g")
    return (loss_total, dx0[None], *[grads[nm] for nm in order], *[updates[nm][0] for nm in order],
            *[updates[nm][1] for nm in order], *[updates[nm][2] for nm in order])
```

```python
import functools
import math

import jax
import jax.numpy as jnp
from jax import lax
from jax.experimental import pallas as pl
from jax.experimental.pallas import tpu as pltpu

F32 = jnp.float32
BF16 = jnp.bfloat16
MESH = pl.DeviceIdType.MESH
HBM = pl.BlockSpec(memory_space=pltpu.HBM)

N_DEV = 8
RMS_EPS = 1e-6
ROPE_BASE = 10000.0
HEADS = 4
DK = 64
DV = 128
QK_W = HEADS * DK
V_W = HEADS * DV
GATE_RANK = 16
GATE_NORM = 16.0
CHUNK = 64
SUPER = 256
PROJ_W = 3200
C_RQ, C_RK, C_RV, C_RG, C_GQ, C_GK, C_GV, C_GG, C_GL = 0, 256, 512, 1024, 1536, 1792, 2048, 2560, 3072
GL_W = PROJ_W - C_GL
ADAM_LR, ADAM_B1, ADAM_B2, ADAM_EPS, ADAM_WD, ADAM_STEP = 0.001, 0.9, 0.999, 1e-08, 0.01, 10
VMEM_LIMIT_V7X = 52 * 1024 * 1024


def _cparams(**kw):
    return pltpu.CompilerParams(vmem_limit_bytes=VMEM_LIMIT_V7X, **kw)


def _dot(a, b, form, precision=None):
    dims = {"nn": (((1,), (0,)), ((), ())), "nt": (((1,), (1,)), ((), ())), "tn": (((0,), (0,)), ((), ()))}[form]
    return lax.dot_general(a, b, dims, preferred_element_type=F32, precision=precision)


def _sigmoid(x):
    return 1.0 / (1.0 + jnp.exp(-x))


def _coords():
    return lax.axis_index("x"), lax.axis_index("y"), lax.axis_index("c")


class _NoComm:
    inputs, out_shapes, scratch = (), (), ()


class _AllGather:
    def __init__(self, arrays, kinds):
        self.inputs = tuple(arrays)
        self.kinds = tuple(kinds)
        n = len(arrays)
        self.out_shapes = tuple(
            jax.ShapeDtypeStruct((a.shape[0], N_DEV) + a.shape[1:] if k == "stack" else (N_DEV,) + a.shape, a.dtype)
            for a, k in zip(arrays, kinds))
        self.scratch = (pltpu.SemaphoreType.DMA((n, 7)), pltpu.SemaphoreType.DMA((n, 7)),
                        pltpu.SemaphoreType.DMA((n,)))

    def _ctx(self, srcs, outs, sems):
        send_sems, recv_sems, local_sems = sems
        x, y, c = _coords()
        me, sibling = (x, y, c), (x, y, 1 - c)
        chips = [(1 - x, y), (x, 1 - y), (1 - x, 1 - y)]

        def blk(m, dev):
            k = 4 * dev[0] + 2 * dev[1] + dev[2]
            return outs[m].at[:, k] if self.kinds[m] == "stack" else outs[m].at[k]

        def copy(m, s, block, to, src=None):
            return pltpu.make_async_remote_copy(
                src_ref=blk(m, block) if src is None else src, dst_ref=blk(m, block),
                send_sem=send_sems.at[m, s], recv_sem=recv_sems.at[m, s], device_id=to, device_id_type=MESH)

        def mine(m):
            return pltpu.make_async_copy(srcs[m], blk(m, me), local_sems.at[m])

        def first(m):
            return [copy(m, 0, me, sibling, src=srcs[m])] + [
                copy(m, 1 + j, me, (*chip, c), src=srcs[m]) for j, chip in enumerate(chips)]

        return me, sibling, chips, c, copy, mine, first

    def start(self, srcs, outs, sems):
        me, sibling, chips, c, copy, mine, first = self._ctx(srcs, outs, sems)
        for m in range(len(srcs)):
            mine(m).start()
            for cp in first(m):
                cp.start()

    def mid(self, srcs, outs, sems):
        me, sibling, chips, c, copy, mine, first = self._ctx(srcs, outs, sems)
        for j, chip in enumerate(chips):
            for m in range(len(srcs)):
                copy(m, 1 + j, (*chip, c), me).wait_recv()
                copy(m, 4 + j, (*chip, c), sibling).start()

    def finish(self, srcs, outs, sems):
        me, sibling, chips, c, copy, mine, first = self._ctx(srcs, outs, sems)
        for m in range(len(srcs)):
            copy(m, 0, sibling, me).wait_recv()
            for j, chip in enumerate(chips):
                copy(m, 4 + j, (*chip, 1 - c), me).wait_recv()
            for cp in first(m):
                cp.wait_send()
            for j, chip in enumerate(chips):
                copy(m, 4 + j, (*chip, c), sibling).wait_send()
            mine(m).wait()


RELATIONS = ((0, 0, 1), (1, 0, 0), (0, 1, 0), (1, 1, 0), (1, 0, 1), (0, 1, 1), (1, 1, 1))
SEM = pl.BlockSpec(memory_space=pltpu.SEMAPHORE)
SPLIT_PARAMS = dict(has_side_effects=pltpu.SideEffectType.DATAFLOW_SIDE_EFFECTING)


def _owner_copies(grad_ref, land_ref, send_sems, recv_sems):
    x, y, c = _coords()
    copies = []
    for s, (fx, fy, fc) in enumerate(RELATIONS):
        px = 1 - x if fx else x
        py = 1 - y if fy else y
        pc = 1 - c if fc else c
        copies.append(pltpu.make_async_remote_copy(
            src_ref=grad_ref.at[2 * px + py, pc], dst_ref=land_ref.at[s], send_sem=send_sems.at[s],
            recv_sem=recv_sems.at[s], device_id=(px, py, pc), device_id_type=MESH))
    return copies


def _send_to_owners(name, grad):
    n = len(RELATIONS)
    land_shape = (n,) + grad.shape[2:]

    def body(g_ref, land_ref, send_sems, recv_sems, g_thru, land_thru, token):
        for cp in _owner_copies(g_ref, land_ref, send_sems, recv_sems):
            cp.start()
        token[...] = jnp.zeros_like(token)

    return pl.pallas_call(
        body, name=name,
        out_shape=(pltpu.SemaphoreType.DMA((n,)), pltpu.SemaphoreType.DMA((n,)), pltpu.HBM(grad.shape, grad.dtype),
                   pltpu.HBM(land_shape, grad.dtype), jax.ShapeDtypeStruct((8, 128), F32)),
        in_specs=(HBM, HBM), out_specs=(SEM, SEM, HBM, HBM, pl.BlockSpec(memory_space=pltpu.VMEM)),
        input_output_aliases={0: 2, 1: 3}, compiler_params=pltpu.CompilerParams(**SPLIT_PARAMS),
    )(pltpu.with_memory_space_constraint(grad, pltpu.HBM),
      pltpu.with_memory_space_constraint(lax.empty(land_shape, grad.dtype), pltpu.HBM))


def _await_owners(name, started, after):
    send_sems, recv_sems, g_thru, land_thru, _ = started

    def body(g_ref, land_ref, send_sems, recv_sems, after_ref, g_out, land_out):
        for cp in _owner_copies(g_ref, land_ref, send_sems, recv_sems):
            cp.wait_send()
            cp.wait_recv()

    return pl.pallas_call(
        body, name=name, out_shape=(pltpu.HBM(g_thru.shape, g_thru.dtype), pltpu.HBM(land_thru.shape, land_thru.dtype)),
        in_specs=(HBM, HBM, SEM, SEM, pl.BlockSpec(memory_space=pl.ANY)), out_specs=(HBM, HBM),
        input_output_aliases={0: 0, 1: 1}, compiler_params=pltpu.CompilerParams(**SPLIT_PARAMS),
    )(g_thru, land_thru, send_sems, recv_sems, after)


def _call(name, main, grid, in_specs, out_specs, out_shape, args, scratch=(), comm=None, prefetch=None, after=None):
    comm = comm or _NoComm()
    n_main = len(in_specs)
    if after is not None:
        in_specs = list(in_specs) + [pl.BlockSpec(after.shape, lambda *_: (0,) * after.ndim)]
        args = tuple(args) + (after,)
    counts = [len(in_specs), len(comm.inputs), len(out_shape), len(comm.out_shapes), len(scratch), len(comm.scratch)]
    n_steps = math.prod(grid)
    hosted = bool(comm.inputs)

    def body(*refs):
        if prefetch is not None:
            refs = refs[1:]
        parts, at = [], 0
        for n in counts:
            parts.append(refs[at:at + n])
            at += n
        ins, c_in, outs, c_out, scr, c_scr = parts
        ins = ins[:n_main]
        step = pl.program_id(0)
        for d in range(1, len(grid)):
            step = step * grid[d] + pl.program_id(d)
        if hosted:
            @pl.when(step == 0)
            def _():
                comm.start(c_in, c_out, c_scr)
        main(ins, outs, scr)
        if hosted:
            @pl.when(step == max(n_steps - 2, 0))
            def _():
                comm.mid(c_in, c_out, c_scr)

            @pl.when(step == n_steps - 1)
            def _():
                comm.finish(c_in, c_out, c_scr)

    all_in = list(in_specs) + [HBM] * counts[1]
    all_out = list(out_specs) + [HBM] * counts[3]
    all_scratch = list(scratch) + list(comm.scratch)
    shapes = list(out_shape) + list(comm.out_shapes)
    if prefetch is None:
        res = pl.pallas_call(body, name=name, grid=grid, in_specs=all_in, out_specs=all_out, out_shape=shapes,
                             scratch_shapes=all_scratch, compiler_params=_cparams())(*args, *comm.inputs)
    else:
        res = pl.pallas_call(
            body, name=name, out_shape=shapes,
            grid_spec=pltpu.PrefetchScalarGridSpec(num_scalar_prefetch=1, grid=grid, in_specs=all_in,
                                                   out_specs=all_out, scratch_shapes=all_scratch),
            compiler_params=_cparams())(prefetch, *args, *comm.inputs)
    return res[:counts[2]], res[counts[2]:]


def _rms_fwd(name, x, g, comm=None):
    T, D = x.shape
    tm = min(T, 512)

    def main(ins, outs, scr):
        x_ref, g_ref = ins
        xv = x_ref[...]
        r = lax.rsqrt(jnp.mean(xv * xv, axis=-1, keepdims=True) + RMS_EPS)
        outs[0][...] = (xv * r * g_ref[...]).astype(outs[0].dtype)

    tile = pl.BlockSpec((tm, D), lambda i: (i, 0))
    (h,), extra = _call(name, main, (T // tm,), [tile, pl.BlockSpec((1, D), lambda i: (0, 0))], [tile],
                        [jax.ShapeDtypeStruct((T, D), BF16)], (x, g), comm=comm)
    return h, extra


def _final_loss_epilogue(scale, out_scale):
    def ep(acc, ex, outs):
        res_ref, g_ref, t_ref = ex
        dx_ref, dxb_ref, dg_ref, loss_ref = outs
        n = acc.shape[-1]
        xv = res_ref[...] + scale * acc
        r = lax.rsqrt(jnp.mean(xv * xv, axis=-1, keepdims=True) + RMS_EPS)
        xhat = xv * r
        err = xhat * g_ref[...] - t_ref[...]

        @pl.when(pl.program_id(0) == 0)
        def _():
            dg_ref[...] = jnp.zeros_like(dg_ref)
            loss_ref[...] = jnp.zeros_like(loss_ref)

        loss_ref[...] += jnp.broadcast_to(jnp.sum(err * err) * (0.5 / n), loss_ref.shape)
        dy = err * (1.0 / n)
        dg_ref[...] += jnp.sum(dy * xhat, axis=0, keepdims=True)
        dxhat = dy * g_ref[...]
        dx = r * (dxhat - xhat * jnp.mean(dxhat * xhat, axis=-1, keepdims=True))
        dx_ref[...] = dx
        dxb_ref[...] = (out_scale * dx).astype(dxb_ref.dtype)
    return ep


def _mm_nstream(name, a, ws, w_sel, w_form, comps, out_dtypes, epilogue, cn, rows=1024, comm=None, after=None):
    T, K = a.shape
    N = ws[0].shape[1]
    rows = min(rows, T)
    assert N % cn == 0 and T % rows == 0
    n_w, n_c = len(ws), len(comps)

    def main(ins, outs, scr):
        a_ref = ins[0]
        w_refs = ins[1:1 + n_w]
        c_refs = ins[1 + n_w:1 + n_w + n_c]

        for r in range(T // rows):
            sl = slice(r * rows, (r + 1) * rows)
            a_blk = a_ref[sl, :]
            dots = [_dot(a_blk, w_ref[...], w_form) for w_ref in w_refs]
            res = epilogue(dots, [c_ref[sl, :] for c_ref in c_refs])
            for o_ref, o in zip(outs, res):
                o_ref[sl, :] = o.astype(o_ref.dtype)

    if w_form == "nt":
        w_specs = [pl.BlockSpec((None, cn, K), functools.partial(lambda j, s: (s, j, 0), s=s)) for s in w_sel]
    else:
        w_specs = [pl.BlockSpec((K, cn), lambda j: (0, j)) for _ in ws]
    chunk = pl.BlockSpec((T, cn), lambda j: (0, j))
    return _call(name, main, (N // cn,), [pl.BlockSpec((T, K), lambda j: (0, 0))] + w_specs + [chunk] * n_c,
                 [chunk] * len(out_dtypes), [jax.ShapeDtypeStruct((T, N), dt) for dt in out_dtypes],
                 (a, *ws, *comps), comm=comm, after=after)


def _mm_mstream(name, as_, ws, w_sel, w_form, extras, outs_desc, epilogue, tm=512, comm=None, after=None):
    T = as_[0].shape[0]
    tm = min(tm, T)
    n_a = len(as_)
    w_shapes = [w.shape[-2:] for w in ws]
    N = w_shapes[0][1] if w_form == "nn" else w_shapes[0][0]

    def main(ins, outs, scr):
        a_refs = ins[:n_a]
        w_refs = ins[n_a:2 * n_a]
        acc = None
        for a_ref, w_ref in zip(a_refs, w_refs):
            d = _dot(a_ref[...], w_ref[...], w_form)
            acc = d if acc is None else acc + d
        epilogue(acc, ins[2 * n_a:], outs)

    kind_spec = {"tile": pl.BlockSpec((tm, N), lambda i: (i, 0)), "vec": pl.BlockSpec((1, N), lambda i: (0, 0))}
    kind_shape = {"tile": (T, N), "vec": (1, N)}
    a_specs = [pl.BlockSpec((tm, a.shape[1]), lambda i: (i, 0)) for a in as_]
    w_specs = []
    for w, s in zip(ws, w_sel):
        if w.ndim == 3:
            w_specs.append(pl.BlockSpec((None,) + tuple(w.shape[1:]), functools.partial(lambda i, s: (s, 0, 0), s=s),
                                        pipeline_mode=pl.Buffered(1)))
        else:
            w_specs.append(pl.BlockSpec(tuple(w.shape), lambda i: (0, 0), pipeline_mode=pl.Buffered(1)))
    args = list(as_) + list(ws) + [e for e, _ in extras]
    return _call(name, main, (T // tm,), a_specs + w_specs + [kind_spec[k] for _, k in extras],
                 [kind_spec[k] for _, k in outs_desc],
                 [jax.ShapeDtypeStruct(kind_shape[k], dt) for dt, k in outs_desc], args, comm=comm, after=after)


def _residual_rms_epilogue(scale):
    def ep(acc, ex, outs):
        xv = ex[0][...] + scale * acc
        outs[0][...] = xv
        r = lax.rsqrt(jnp.mean(xv * xv, axis=-1, keepdims=True) + RMS_EPS)
        outs[1][...] = (xv * r * ex[1][...]).astype(outs[1].dtype)
    return ep


def _rms_bwd_epilogue(out_scale):
    def ep(acc, ex, outs):
        x_ref, g_ref, dres_ref = ex
        dx_ref, dxb_ref, dg_ref = outs
        xv = x_ref[...]
        r = lax.rsqrt(jnp.mean(xv * xv, axis=-1, keepdims=True) + RMS_EPS)
        xhat = xv * r

        @pl.when(pl.program_id(0) == 0)
        def _():
            dg_ref[...] = jnp.zeros_like(dg_ref)

        dg_ref[...] += jnp.sum(acc * xhat, axis=0, keepdims=True)
        dxhat = acc * g_ref[...]
        dx = r * (dxhat - xhat * jnp.mean(dxhat * xhat, axis=-1, keepdims=True)) + dres_ref[...]
        dx_ref[...] = dx
        dxb_ref[...] = (out_scale * dx).astype(dxb_ref.dtype)
    return ep


def _mm_tn(name, a, b, tmo, tno, out_dtype, tk=1024, comm=None, after=None):
    T, Ma = a.shape
    Nb = b.shape[1]
    tk = min(tk, T)
    nk = T // tk

    def main(ins, outs, scr):
        a_ref, b_ref = ins
        (acc_ref,) = scr
        k = pl.program_id(2)

        @pl.when(k == 0)
        def _():
            acc_ref[...] = jnp.zeros_like(acc_ref)

        acc_ref[...] += _dot(a_ref[...], b_ref[...], "tn")

        @pl.when(k == nk - 1)
        def _():
            outs[0][...] = acc_ref[...].astype(outs[0].dtype)

    (out,), extra = _call(
        name, main, (Ma // tmo, Nb // tno, nk),
        [pl.BlockSpec((tk, tmo), lambda i, j, k: (k, i)), pl.BlockSpec((tk, tno), lambda i, j, k: (k, j))],
        [pl.BlockSpec((tmo, tno), lambda i, j, k: (i, j))], [jax.ShapeDtypeStruct((Ma, Nb), out_dtype)],
        (a, b), scratch=[pltpu.VMEM((tmo, tno), F32)], comm=comm, after=after)
    return out, extra


def _swiglu_parts(g, u):
    s = _sigmoid(g)
    silu = g * s
    return [u * (s + silu * (1.0 - s)), silu, silu * u]


def _silu_mul_epilogue(dots, comps):
    g, u = dots
    return _swiglu_parts(g, u)


def _gate_parts_epilogue(dots, comps):
    (g,) = dots
    s = _sigmoid(g)
    silu = g * s
    return [s + silu * (1.0 - s), silu]


def _up_act_epilogue(dots, comps):
    (u,) = dots
    return [u * comps[0].astype(F32), u * comps[1].astype(F32)]


def _dact_epilogue(dots, comps):
    dact = dots[0].astype(BF16)
    return [dact * comps[0], dact * comps[1]]


def _identity_epilogue(dots, comps):
    return list(dots)


def _swap_halves(x):
    lane = lax.broadcasted_iota(jnp.int32, x.shape, 1)
    first = (lane % DK) < (DK // 2)
    return jnp.where(first, pltpu.roll(x, 128 - DK // 2, 1), pltpu.roll(x, DK // 2, 1))


def _rotary(t, cos, sin_signed):
    halves = []
    for p in range(QK_W // 128):
        th = t[:, 128 * p:128 * (p + 1)]
        halves.append(th * cos + _swap_halves(th) * sin_signed)
    return jnp.concatenate(halves, axis=1)


def _rotary_transposed(d, cos, sin_signed):
    halves = []
    for p in range(QK_W // 128):
        dh = d[:, 128 * p:128 * (p + 1)]
        halves.append(dh * cos + _swap_halves(dh * sin_signed))
    return jnp.concatenate(halves, axis=1)


def _log_sigmoid(x):
    return jnp.minimum(x, 0.0) - jnp.log(1.0 + jnp.exp(-jnp.abs(x)))


def _attn_masks():
    row = lax.broadcasted_iota(jnp.int32, (SUPER, SUPER), 0)
    col = lax.broadcasted_iota(jnp.int32, (SUPER, SUPER), 1)
    same = (row // CHUNK) == (col // CHUNK)
    return row, col, same


def _group_inputs(grp, pr, cos, sin_signed, lg, wa2, ba):
    seg = lambda lo, width: pr[:, lo:lo + width].astype(F32)
    if grp == 0:
        q = _rotary(seg(C_RQ, QK_W), cos, sin_signed)
        k = _rotary(seg(C_RK, QK_W), cos, sin_signed) * (DK ** -0.5)
        v = pr[:, C_RV:C_RV + V_W]
        gate = seg(C_RG, V_W)
        pos = lax.broadcasted_iota(jnp.int32, (SUPER, QK_W), 0).astype(F32) + 1.0
        return q, k, v, gate, pos * lg, None, None
    q = seg(C_GQ, QK_W) * (DK ** -0.5)
    k = seg(C_GK, QK_W)
    v = pr[:, C_GV:C_GV + V_W]
    gate = seg(C_GG, V_W)
    glow = pr[:, C_GL:C_GL + GL_W]
    logit = _dot(glow.astype(BF16), wa2.astype(BF16), "nn") + ba
    la = _log_sigmoid(logit) * (1.0 / GATE_NORM)
    row, col, _ = _attn_masks()
    lower = (col <= row).astype(F32)
    b_cum = _dot(lower, la, "nn", precision=lax.Precision.HIGHEST)
    return q, k, v, gate, b_cum, glow, logit


def _decay_factors(q, k, b_cum):
    c = b_cum[SUPER // 2 - 1:SUPER // 2, :]
    bl = b_cum[SUPER - 1:SUPER, :]
    e1 = jnp.exp(b_cum - c)
    e2 = jnp.exp(c - b_cum)
    e_b = jnp.exp(b_cum)
    e_l = jnp.exp(bl - b_cum)
    return dict(e1=e1, e2=e2, eb=e_b, el=e_l, ebl=jnp.exp(bl),
                qp=q * e1, qm=q * e2, kp=k * e1, km=k * e2, qs=q * e_b, kl=k * e_l)


def _state_block_mask():
    r = lax.broadcasted_iota(jnp.int32, (V_W, QK_W), 0)
    c = lax.broadcasted_iota(jnp.int32, (V_W, QK_W), 1)
    return (r // DV) == (c // DK)


def _attn_fwd(proj, cos, sin_signed, lg, wa2p, ba, gn_ret, gn_gla, x_res, w_out, g_next, comm=None):
    T = proj.shape[0]
    n_s = T // SUPER
    D = x_res.shape[1]

    def main(ins, outs, scr):
        pr_ref, cos_ref, sin_ref, lg_ref, wa2_ref, ba_ref, gr_ref, gg_ref, xres_ref, wout_ref, gnext_ref = ins
        o_ref, y_ref, st_ref, x_ref, h_ref = outs
        (s_ref,) = scr
        i = pl.program_id(0)

        @pl.when(i == 0)
        def _():
            s_ref[...] = jnp.zeros_like(s_ref)

        pr = pr_ref
        row, col, same = _attn_masks()
        m1 = col <= row
        m2 = jnp.logical_and(col > row, same)
        lane = lax.broadcasted_iota(jnp.int32, (1, QK_W), 1)
        blockmask = _state_block_mask()
        for grp in range(2):
            q, k, v, gate, b_cum, _, _ = _group_inputs(grp, pr, cos_ref[...], sin_ref[...], lg_ref[...],
                                                      wa2_ref[...], ba_ref[...])
            f = _decay_factors(q, k, b_cum)
            gn = gr_ref[...] if grp == 0 else gg_ref[...]
            s_prev = s_ref[grp]
            st_ref[0, grp] = s_prev
            o_inter = _dot(f["qs"].astype(BF16), s_prev.astype(BF16), "nt")
            kmb = f["km"].astype(BF16)
            kpb = f["kp"].astype(BF16)
            vb = v.astype(BF16)
            for h in range(HEADS):
                hm = (lane // DK) == h
                a1 = _dot(jnp.where(hm, f["qp"], 0.0).astype(BF16), kmb, "nt")
                a2 = _dot(jnp.where(hm, f["qm"], 0.0).astype(BF16), kpb, "nt")
                a = jnp.where(m1, a1, jnp.where(m2, a2, 0.0))
                lo = grp * V_W + h * DV
                o_h = _dot(a.astype(BF16), vb[:, h * DV:(h + 1) * DV], "nn") + o_inter[:, h * DV:(h + 1) * DV]
                o_ref[:, lo:lo + DV] = o_h
                r = lax.rsqrt(jnp.mean(o_h * o_h, axis=-1, keepdims=True) + RMS_EPS)
                gte = gate[:, h * DV:(h + 1) * DV]
                y = o_h * r * gn[:, h * DV:(h + 1) * DV] * (gte * _sigmoid(gte))
                y_ref[:, lo:lo + DV] = y.astype(y_ref.dtype)
            upd = _dot(vb, f["kl"].astype(BF16), "tn")
            s_ref[grp] = s_prev * f["ebl"] + jnp.where(blockmask, upd, 0.0)
        xv = xres_ref[...] + _dot(y_ref[...], wout_ref[...], "nn")
        x_ref[...] = xv
        r = lax.rsqrt(jnp.mean(xv * xv, axis=-1, keepdims=True) + RMS_EPS)
        h_ref[...] = (xv * r * gnext_ref[...]).astype(h_ref.dtype)

    const = lambda shape: pl.BlockSpec(shape, lambda i: tuple(0 for _ in shape))
    rows = lambda w: pl.BlockSpec((SUPER, w), lambda i: (i, 0))
    return _call(
        "attn_fwd", main, (n_s,),
        [rows(PROJ_W), rows(128), rows(128),
         const((1, QK_W)), const((GL_W, QK_W)), const((1, QK_W)), const((1, V_W)), const((1, V_W)),
         rows(D), const((2 * V_W, D)), const((1, D))],
        [rows(2 * V_W), rows(2 * V_W), pl.BlockSpec((1, 2, V_W, QK_W), lambda i: (i, 0, 0, 0)), rows(D), rows(D)],
        [jax.ShapeDtypeStruct((T, 2 * V_W), F32), jax.ShapeDtypeStruct((T, 2 * V_W), BF16),
         jax.ShapeDtypeStruct((n_s, 2, V_W, QK_W), F32), jax.ShapeDtypeStruct((T, D), F32),
         jax.ShapeDtypeStruct((T, D), BF16)],
        (proj, cos, sin_signed, lg, wa2p, ba, gn_ret, gn_gla, x_res, w_out, g_next),
        scratch=[pltpu.VMEM((2, V_W, QK_W), F32)], comm=comm)


def _attn_bwd(proj, cos, sin_signed, lg, wa2p, ba, gn_ret, gn_gla, o, dx, w_out, states, comm=None, after=None):
    T = proj.shape[0]
    n_s = T // SUPER
    D = dx.shape[1]

    def main(ins, outs, scr):
        pr_ref, cos_ref, sin_ref, lg_ref, wa2_ref, ba_ref, gr_ref, gg_ref, o_ref, dx_ref, wout_ref, st_ref = ins
        dp_ref, dgr_ref, dgg_ref, dba_ref, dwa_ref = outs
        (ds_ref, dy_ref) = scr
        i = pl.program_id(0)
        dy_ref[...] = _dot(dx_ref[...], wout_ref[...], "nt")

        @pl.when(i == 0)
        def _():
            ds_ref[...] = jnp.zeros_like(ds_ref)
            dgr_ref[...] = jnp.zeros_like(dgr_ref)
            dgg_ref[...] = jnp.zeros_like(dgg_ref)
            dba_ref[...] = jnp.zeros_like(dba_ref)
            dwa_ref[...] = jnp.zeros_like(dwa_ref)

        pr = pr_ref
        cos = cos_ref[...]
        sin_signed = sin_ref[...]
        row, col, same = _attn_masks()
        m1 = col <= row
        m2 = jnp.logical_and(col > row, same)
        m1t = row <= col
        m2t = jnp.logical_and(row > col, same)
        lane = lax.broadcasted_iota(jnp.int32, (1, QK_W), 1)
        blockmask = _state_block_mask()
        for grp in range(2):
            q, k, v, gate, b_cum, glow, logit = _group_inputs(grp, pr, cos, sin_signed, lg_ref[...],
                                                              wa2_ref[...], ba_ref[...])
            f = _decay_factors(q, k, b_cum)
            gn = gr_ref[...] if grp == 0 else gg_ref[...]
            dgn_ref = dgr_ref if grp == 0 else dgg_ref
            do_parts, dgate_parts, dgn_parts = [], [], []
            for h in range(HEADS):
                lo = grp * V_W + h * DV
                o_h = o_ref[:, lo:lo + DV]
                r = lax.rsqrt(jnp.mean(o_h * o_h, axis=-1, keepdims=True) + RMS_EPS)
                n = o_h * r
                gte = gate[:, h * DV:(h + 1) * DV]
                sg = _sigmoid(gte)
                dy_h = dy_ref[:, lo:lo + DV]
                gn_h = gn[:, h * DV:(h + 1) * DV]
                dgate_parts.append(dy_h * n * gn_h * (sg * (1.0 + gte * (1.0 - sg))))
                dz = dy_h * (gte * sg)
                dgn_parts.append(jnp.sum(dz * n, axis=0, keepdims=True))
                dn = dz * gn_h
                do_parts.append(r * (dn - n * jnp.mean(dn * n, axis=-1, keepdims=True)))
            dgn_ref[...] += jnp.concatenate(dgn_parts, axis=1)
            dgate = jnp.concatenate(dgate_parts, axis=1)
            do = jnp.concatenate(do_parts, axis=1)
            dob = do.astype(BF16)
            vb = v.astype(BF16)
            s_prev = st_ref[0, grp]
            ds_new = ds_ref[grp]
            dsb = ds_new.astype(BF16)
            qpb, qmb = f["qp"].astype(BF16), f["qm"].astype(BF16)
            kpb, kmb = f["kp"].astype(BF16), f["km"].astype(BF16)
            dqp = jnp.zeros((SUPER, QK_W), F32)
            dqm = jnp.zeros((SUPER, QK_W), F32)
            dkp = jnp.zeros((SUPER, QK_W), F32)
            dkm = jnp.zeros((SUPER, QK_W), F32)
            dv_parts = []
            for h in range(HEADS):
                hm = (lane // DK) == h
                qp_h = jnp.where(hm, f["qp"], 0.0).astype(BF16)
                qm_h = jnp.where(hm, f["qm"], 0.0).astype(BF16)
                kp_h = jnp.where(hm, f["kp"], 0.0).astype(BF16)
                km_h = jnp.where(hm, f["km"], 0.0).astype(BF16)
                at = jnp.where(m1t, _dot(km_h, qpb, "nt"), jnp.where(m2t, _dot(kp_h, qmb, "nt"), 0.0))
                do_h = dob[:, h * DV:(h + 1) * DV]
                v_h = vb[:, h * DV:(h + 1) * DV]
                dv_parts.append(_dot(at.astype(BF16), do_h, "nn"))
                da = _dot(do_h, v_h, "nt")
                dat = _dot(v_h, do_h, "nt")
                da1 = jnp.where(m1, da, 0.0).astype(BF16)
                da2 = jnp.where(m2, da, 0.0).astype(BF16)
                da1t = jnp.where(m1t, dat, 0.0).astype(BF16)
                da2t = jnp.where(m2t, dat, 0.0).astype(BF16)
                dqp = dqp + _dot(da1, km_h, "nn")
                dqm = dqm + _dot(da2, kp_h, "nn")
                dkm = dkm + _dot(da1t, qp_h, "nn")
                dkp = dkp + _dot(da2t, qm_h, "nn")
            klb = f["kl"].astype(BF16)
            qsb = f["qs"].astype(BF16)
            dqs = _dot(dob, s_prev.astype(BF16), "nn")
            dkl = _dot(vb, dsb, "nn")
            dv = jnp.concatenate(dv_parts, axis=1) + _dot(klb, dsb, "nt")
            ds_ref[grp] = ds_new * f["ebl"] + jnp.where(blockmask, _dot(dob, qsb, "tn"), 0.0)
            dq = dqp * f["e1"] + dqm * f["e2"] + dqs * f["eb"]
            dk = dkm * f["e2"] + dkp * f["e1"] + dkl * f["el"]
            if grp == 0:
                dq = _rotary_transposed(dq, cos, sin_signed)
                dk = _rotary_transposed(dk * (DK ** -0.5), cos, sin_signed)
                dp_ref[:, C_RQ:C_RQ + QK_W] = dq.astype(dp_ref.dtype)
                dp_ref[:, C_RK:C_RK + QK_W] = dk.astype(dp_ref.dtype)
                dp_ref[:, C_RV:C_RV + V_W] = dv.astype(dp_ref.dtype)
                dp_ref[:, C_RG:C_RG + V_W] = dgate.astype(dp_ref.dtype)
            else:
                dkl_kl = dkl * klb.astype(F32)
                db = (dqp * qpb.astype(F32) - dkm * kmb.astype(F32) - dqm * qmb.astype(F32)
                      + dkp * kpb.astype(F32) + dqs * qsb.astype(F32) - dkl_kl)
                last = (jnp.sum(dkl_kl, axis=0, keepdims=True)
                        + f["ebl"] * jnp.sum(s_prev * ds_new, axis=0, keepdims=True))
                rowq = lax.broadcasted_iota(jnp.int32, (SUPER, QK_W), 0)
                db = db + jnp.where(rowq == SUPER - 1, last, 0.0)
                upper = (col >= row).astype(F32)
                dla = _dot(upper, db, "nn", precision=lax.Precision.HIGHEST)
                dlogit = dla * (1.0 / GATE_NORM) * (1.0 - _sigmoid(logit))
                dlb = dlogit.astype(BF16)
                dglow = _dot(dlb, wa2_ref[...].astype(BF16), "nt")
                dwa_ref[...] += _dot(glow.astype(BF16), dlb, "tn")
                dba_ref[...] += jnp.sum(dlogit, axis=0, keepdims=True)
                dp_ref[:, C_GQ:C_GQ + QK_W] = (dq * (DK ** -0.5)).astype(dp_ref.dtype)
                dp_ref[:, C_GK:C_GK + QK_W] = dk.astype(dp_ref.dtype)
                dp_ref[:, C_GV:C_GV + V_W] = dv.astype(dp_ref.dtype)
                dp_ref[:, C_GG:C_GG + V_W] = dgate.astype(dp_ref.dtype)
                dp_ref[:, C_GL:C_GL + GL_W] = dglow.astype(dp_ref.dtype)

    rev = lambda i: n_s - 1 - i
    const = lambda shape: pl.BlockSpec(shape, lambda i: tuple(0 for _ in shape))
    return _call(
        "attn_bwd", main, (n_s,),
        [pl.BlockSpec((SUPER, PROJ_W), lambda i: (rev(i), 0)),
         pl.BlockSpec((SUPER, 128), lambda i: (rev(i), 0)), pl.BlockSpec((SUPER, 128), lambda i: (rev(i), 0)),
         const((1, QK_W)), const((GL_W, QK_W)), const((1, QK_W)), const((1, V_W)), const((1, V_W)),
         pl.BlockSpec((SUPER, 2 * V_W), lambda i: (rev(i), 0)),
         pl.BlockSpec((SUPER, D), lambda i: (rev(i), 0)), const((2 * V_W, D)),
         pl.BlockSpec((1, 2, V_W, QK_W), lambda i: (rev(i), 0, 0, 0))],
        [pl.BlockSpec((SUPER, PROJ_W), lambda i: (rev(i), 0)),
         const((1, V_W)), const((1, V_W)), const((1, QK_W)), const((GL_W, QK_W))],
        [jax.ShapeDtypeStruct((T, PROJ_W), BF16),
         jax.ShapeDtypeStruct((1, V_W), F32), jax.ShapeDtypeStruct((1, V_W), F32),
         jax.ShapeDtypeStruct((1, QK_W), F32), jax.ShapeDtypeStruct((GL_W, QK_W), F32)],
        (proj, cos, sin_signed, lg, wa2p, ba, gn_ret, gn_gla, o, dx, w_out, states),
        scratch=[pltpu.VMEM((2, V_W, QK_W), F32), pltpu.VMEM((SUPER, 2 * V_W), F32)], comm=comm, after=after)


def _rotary_tables(T):
    half = DK // 2
    inv = ROPE_BASE ** (-jnp.arange(half, dtype=F32) * 2.0 / DK)
    ang = jnp.arange(T, dtype=F32)[:, None] * inv[None, :]
    cos, sin = jnp.cos(ang), jnp.sin(ang)
    cos_head = jnp.concatenate([cos, cos], axis=1)
    sin_head = jnp.concatenate([-sin, sin], axis=1)
    return jnp.tile(cos_head, (1, 128 // DK)), jnp.tile(sin_head, (1, 128 // DK))


def _sum_devices(name, gathered, m_per):
    def body(g_ref, o_ref):
        acc = g_ref[0:m_per, :]
        for k in range(1, N_DEV):
            acc = acc + g_ref[k * m_per:(k + 1) * m_per, :]
        o_ref[...] = acc

    return pl.pallas_call(body, name=name, out_shape=jax.ShapeDtypeStruct((m_per, 128), F32))(gathered)


def _owner_sums(name, items, owner, comm=None):
    counts = [1 + len(landed) for _, landed in items]

    def main(ins, outs, scr):
        at = 0
        for o_ref, n in zip(outs, counts):
            acc = ins[at][...].astype(F32)
            for l_ref in ins[at + 1:at + n]:
                for j in range(l_ref.shape[0]):
                    acc = acc + l_ref[j].astype(F32)
            o_ref[...] = acc
            at += n

    n_steps = 2
    in_specs, out_specs, out_shape, args = [], [], [], []
    for grad, landed in items:
        R, C = grad.shape[-2:]
        if (R // n_steps) % 16 == 0:
            rows, step = R // n_steps, (lambda i: i)
        else:
            rows, step = R, (lambda i: 0)
        in_specs.append(pl.BlockSpec((None, None, rows, C),
                                     functools.partial(lambda i, s, step: (s[0], s[1], step(i), 0), step=step)))
        in_specs += [pl.BlockSpec((l.shape[0], rows, C), functools.partial(lambda i, s, step: (0, step(i), 0), step=step))
                     for l in landed]
        out_specs.append(pl.BlockSpec((rows, C), functools.partial(lambda i, s, step: (step(i), 0), step=step)))
        out_shape.append(jax.ShapeDtypeStruct((R, C), F32))
        args += [grad, *landed]
    return _call(name, main, (n_steps,), in_specs, out_specs, out_shape, args, comm=comm, prefetch=owner)


def _adamw_group(name, items, n_blocks, comm=None):
    n = len(items)

    def main(ins, outs, scr):
        for p in range(n):
            g_ref, w_ref, m_ref, v_ref = ins[4 * p:4 * p + 4]
            d_ref, nm_ref, nv_ref = outs[3 * p:3 * p + 3]
            gv = g_ref[...]
            nm = ADAM_B1 * m_ref[...] + (1.0 - ADAM_B1) * gv
            nv = ADAM_B2 * v_ref[...] + (1.0 - ADAM_B2) * (gv * gv)
            m_hat = nm / (1.0 - ADAM_B1 ** ADAM_STEP)
            v_hat = nv / (1.0 - ADAM_B2 ** ADAM_STEP)
            d_ref[...] = -ADAM_LR * (m_hat / (jnp.sqrt(v_hat) + ADAM_EPS) + ADAM_WD * w_ref[...])
            nm_ref[...] = nm
            nv_ref[...] = nv

    in_specs, out_specs, out_shape, args = [], [], [], []
    for item in items:
        R, C = item[1].shape
        assert R % n_blocks == 0
        spec = pl.BlockSpec((R // n_blocks, C), lambda i: (i, 0))
        in_specs += [spec] * 4
        out_specs += [spec] * 3
        out_shape += [jax.ShapeDtypeStruct((R, C), F32)] * 3
        args += list(item)
    outs, extra = _call(name, main, (n_blocks,), in_specs, out_specs, out_shape, args, comm=comm)
    return [tuple(outs[3 * p:3 * p + 3]) for p in range(n)], extra


SMALL_ORDER = ("ffn1", "mix", "ffn2", "final", "ret", "gla", "b_a")


def kernel(x, ffn1_norm_g, ffn1_w_gate, ffn1_w_up, ffn1_w_down, mix_norm_g, w_in, ret_norm_g, gla_w_a2, gla_b_a, gla_norm_g, w_out, ffn2_norm_g, ffn2_w_gate, ffn2_w_up, ffn2_w_down, final_norm_g, loss_target, m_ffn1_norm_g, m_ffn1_w_gate, m_ffn1_w_up, m_ffn1_w_down, m_mix_norm_g, m_w_in, m_ret_norm_g, m_gla_w_a2, m_gla_b_a, m_gla_norm_g, m_w_out, m_ffn2_norm_g, m_ffn2_w_gate, m_ffn2_w_up, m_ffn2_w_down, m_final_norm_g, v_ffn1_norm_g, v_ffn1_w_gate, v_ffn1_w_up, v_ffn1_w_down, v_mix_norm_g, v_w_in, v_ret_norm_g, v_gla_w_a2, v_gla_b_a, v_gla_norm_g, v_w_out, v_ffn2_norm_g, v_ffn2_w_gate, v_ffn2_w_up, v_ffn2_w_down, v_final_norm_g):
    xi, yi, ci = _coords()
    dev = 4 * xi + 2 * yi + ci
    owner = jnp.stack([2 * xi + yi, ci]).astype(jnp.int32)

    x0, target = x[0], loss_target[0]
    T, D = x0.shape
    fb = ffn1_w_gate.shape[2]
    ib = w_in.shape[2]
    ab = gla_w_a2.shape[2]
    F = N_DEV * fb
    cos, sin_signed = _rotary_tables(T)
    lg = jnp.repeat(jnp.log(1.0 - 2.0 ** (-5.0 - jnp.arange(HEADS, dtype=F32))), DK)[None, :]
    g_final = final_norm_g.reshape(1, D)

    g1_loc = ffn1_w_gate[0].T[None].astype(BF16)
    u1_loc = ffn1_w_up[0].T[None].astype(BF16)
    d1_loc = ffn1_w_down.astype(BF16)
    g2_loc = ffn2_w_gate[0].T[None].astype(BF16)
    u2_loc = ffn2_w_up[0].T[None].astype(BF16)
    d2_loc = ffn2_w_down.astype(BF16)
    in_loc = w_in[0].T.astype(BF16)
    out_loc = w_out[0].astype(BF16)

    h1, (g1,) = _rms_fwd("ffn1_rms", x0, ffn1_norm_g, comm=_AllGather([g1_loc], ["stack"]))
    g1 = g1.reshape(1, F, D)
    (dsl1, sl1), (u1,) = _mm_nstream("ffn1_gate", h1, [g1], [0], "nt", [], [BF16, BF16], _gate_parts_epilogue, cn=256,
                                     comm=_AllGather([u1_loc], ["stack"]))
    u1 = u1.reshape(1, F, D)
    (dsu1, act1), (d1,) = _mm_nstream("ffn1_up", h1, [u1], [0], "nt", [dsl1, sl1], [BF16, BF16],
                                      _up_act_epilogue, cn=256, comm=_AllGather([d1_loc], ["stack"]))
    d1 = d1.reshape(1, F, D)
    f32_tile, bf16_tile, f32_vec = (F32, "tile"), (BF16, "tile"), (F32, "vec")
    (x1, h2), (in_all, a_all) = _mm_mstream(
        "ffn1_down", [act1], [d1], [0], "nn", [(x0, "tile"), (mix_norm_g, "vec")], [f32_tile, bf16_tile],
        _residual_rms_epilogue(0.5), comm=_AllGather([in_loc, gla_w_a2[0]], ["plain", "plain"]))
    w_in_t = jnp.pad(in_all.reshape(1, N_DEV * ib, D), ((0, 0), (0, PROJ_W - N_DEV * ib), (0, 0)))
    wa2 = jnp.transpose(a_all, (1, 0, 2)).reshape(GATE_RANK, N_DEV * ab)
    wa2p = jnp.pad(wa2, ((0, GL_W - GATE_RANK), (0, 0)))

    (proj,), (g2, out_all) = _mm_nstream("mix_proj", h2, [w_in_t], [0], "nt", [], [BF16], _identity_epilogue, cn=640,
                                         comm=_AllGather([g2_loc, out_loc], ["stack", "plain"]))
    w_out_full = out_all.reshape(D, D)
    (o, ymix, states, x2, h3), (u2,) = _attn_fwd(proj, cos, sin_signed, lg, wa2p, gla_b_a, ret_norm_g, gla_norm_g,
                                                 x1, w_out_full, ffn2_norm_g, comm=_AllGather([u2_loc], ["stack"]))
    g2, u2 = g2.reshape(1, F, D), u2.reshape(1, F, D)

    (dsu2, sl2, act2), (d2,) = _mm_nstream(
        "ffn2_up", h3, [g2, u2], [0, 0], "nt", [], [BF16, BF16, BF16], _silu_mul_epilogue, cn=256,
        comm=_AllGather([d2_loc], ["stack"]))
    d2 = d2.reshape(1, F, D)
    (dx3, dy3b, d_final, loss), _ = _mm_mstream(
        "ffn2_down", [act2], [d2], [0], "nn", [(x2, "tile"), (g_final, "vec"), (target, "tile")],
        [f32_tile, bf16_tile, f32_vec, f32_vec], _final_loss_epilogue(0.5, 0.5))

    sent = {}

    def send(nm, grad):
        sent[nm] = _send_to_owners("send_" + nm, grad)
        return sent[nm][4]

    dwd2, _ = _mm_tn("ffn2b_dwd", act2, dy3b, F // 2, D, BF16)
    tok = send("wd2", dwd2.reshape(4, 2, fb, D))
    (dgate2, dup2), _ = _mm_nstream("ffn2b_dact", dy3b, [d2], [0], "nt", [dsu2, sl2], [BF16, BF16],
                                    _dact_epilogue, cn=256, after=tok)
    dwg2, _ = _mm_tn("ffn2b_dwg", dgate2, h3, F // 2, D, BF16)
    tok = send("wg2", dwg2.reshape(4, 2, fb, D))
    dwu2, _ = _mm_tn("ffn2b_dwu", dup2, h3, F // 2, D, BF16, after=tok)
    tok = send("wu2", dwu2.reshape(4, 2, fb, D))
    rms_outs = [f32_tile, bf16_tile, f32_vec]
    (dx2, dx2b, d_g2), _ = _mm_mstream(
        "ffn2b_dh", [dgate2, dup2], [g2, u2], [0, 0], "nn", [(x2, "tile"), (ffn2_norm_g, "vec"), (dx3, "tile")],
        rms_outs, _rms_bwd_epilogue(1.0), after=tok)

    dwout, _ = _mm_tn("mixb_dwout", ymix, dx2b, D, D, BF16)
    tok = send("wout", dwout.reshape(4, 2, D // N_DEV, D))
    (dproj, d_ret, d_gla, d_ba, d_wa2p), _ = _attn_bwd(
        proj, cos, sin_signed, lg, wa2p, gla_b_a, ret_norm_g, gla_norm_g, o, dx2b, w_out_full, states, after=tok)
    dwin_t, _ = _mm_tn("mixb_dwin", dproj, h2, 640, D, BF16, tk=2048)
    tok = send("win", dwin_t[:N_DEV * ib].reshape(4, 2, ib, D))
    (dx1, dy1b, d_gmix), _ = _mm_mstream(
        "mixb_dh", [dproj], [w_in_t], [0], "nn", [(x1, "tile"), (mix_norm_g, "vec"), (dx2, "tile")],
        rms_outs, _rms_bwd_epilogue(0.5), after=tok)

    dwd1, _ = _mm_tn("ffn1b_dwd", act1, dy1b, F // 2, D, BF16)
    tok = send("wd1", dwd1.reshape(4, 2, fb, D))
    (dgate1, dup1), _ = _mm_nstream("ffn1b_dact", dy1b, [d1], [0], "nt", [dsu1, sl1], [BF16, BF16],
                                    _dact_epilogue, cn=256, after=tok)
    dwg1, _ = _mm_tn("ffn1b_dwg", dgate1, h1, F // 2, D, BF16)
    tok = send("wg1", dwg1.reshape(4, 2, fb, D))
    dwu1, _ = _mm_tn("ffn1b_dwu", dup1, h1, F // 2, D, BF16, after=tok)
    tok = send("wu1", dwu1.reshape(4, 2, fb, D))
    (dx0, _, d_g1), _ = _mm_mstream(
        "ffn1b_dh", [dgate1, dup1], [g1, u1], [0, 0], "nn", [(x0, "tile"), (ffn1_norm_g, "vec"), (dx1, "tile")],
        rms_outs, _rms_bwd_epilogue(1.0), after=tok)

    small = dict(ffn1=d_g1, mix=d_gmix, ffn2=d_g2, final=d_final, ret=d_ret, gla=d_gla, b_a=d_ba)
    flat = jnp.concatenate([small[k].reshape(-1) for k in SMALL_ORDER]
                           + [d_wa2p[:GATE_RANK].reshape(-1), loss[0, :128]])
    rows = -(-flat.shape[0] // 128)
    rows = -(-rows // 8) * 8
    packed = jnp.pad(flat, (0, rows * 128 - flat.shape[0])).reshape(rows, 128)

    transposed = ("ffn1_w_gate", "ffn1_w_up", "ffn2_w_gate", "ffn2_w_up", "w_in")

    def to_2d(nm, a):
        if nm in transposed:
            return a[0].T
        return a.reshape((1, a.shape[0]) if a.ndim == 1 else a.shape[-2:])

    def from_2d(nm, a):
        return a.T[None] if nm in transposed else a.reshape(params[nm][0].shape)

    def arrived(nm, after):
        grad, landed = _await_owners("await_" + nm, sent[nm], after)
        return grad, [landed]

    sums_a, (gathered,) = _owner_sums(
        "sum_a", [arrived(nm, dx0) for nm in ("wg2", "wu2", "wd2", "win", "wout")], owner,
        comm=_AllGather([packed], ["plain"]))
    params = dict(
        ffn2_w_gate=(ffn2_w_gate, m_ffn2_w_gate, v_ffn2_w_gate), ffn2_w_up=(ffn2_w_up, m_ffn2_w_up, v_ffn2_w_up),
        ffn2_w_down=(ffn2_w_down, m_ffn2_w_down, v_ffn2_w_down), w_in=(w_in, m_w_in, v_w_in),
        w_out=(w_out, m_w_out, v_w_out), ffn1_w_gate=(ffn1_w_gate, m_ffn1_w_gate, v_ffn1_w_gate),
        ffn1_w_up=(ffn1_w_up, m_ffn1_w_up, v_ffn1_w_up), ffn1_w_down=(ffn1_w_down, m_ffn1_w_down, v_ffn1_w_down),
        ffn1_norm_g=(ffn1_norm_g, m_ffn1_norm_g, v_ffn1_norm_g), mix_norm_g=(mix_norm_g, m_mix_norm_g, v_mix_norm_g),
        ret_norm_g=(ret_norm_g, m_ret_norm_g, v_ret_norm_g), gla_w_a2=(gla_w_a2, m_gla_w_a2, v_gla_w_a2),
        gla_b_a=(gla_b_a, m_gla_b_a, v_gla_b_a), gla_norm_g=(gla_norm_g, m_gla_norm_g, v_gla_norm_g),
        ffn2_norm_g=(ffn2_norm_g, m_ffn2_norm_g, v_ffn2_norm_g), final_norm_g=(final_norm_g, m_final_norm_g, v_final_norm_g))
    grads, updates = {}, {}

    def run_adam(name, names, grad_2d, n_blocks):
        items = [(grad_2d[nm],) + tuple(to_2d(nm, a) for a in params[nm]) for nm in names]
        res, _ = _adamw_group(name, items, n_blocks)
        for nm, r in zip(names, res):
            grads[nm] = from_2d(nm, grad_2d[nm])
            updates[nm] = tuple(from_2d(nm, a) for a in r)
        return res

    grads_a = {"ffn2_w_gate": sums_a[0], "ffn2_w_up": sums_a[1], "ffn2_w_down": sums_a[2], "w_out": sums_a[4]}
    run_adam("adamw_w_in", ["w_in"], {"w_in": sums_a[3]}, 1)
    done_a = run_adam("adamw_a", list(grads_a), grads_a, 4)[0][0]
    sums_b, _ = _owner_sums("sum_b", [arrived(nm, done_a) for nm in ("wg1", "wu1", "wd1")], owner)
    grads_b = {"ffn1_w_gate": sums_b[0], "ffn1_w_up": sums_b[1], "ffn1_w_down": sums_b[2]}
    run_adam("adamw_b", list(grads_b), grads_b, 4)

    total = _sum_devices("sum_small", gathered.reshape(N_DEV * rows, 128), rows).reshape(-1)
    sizes = [small[k].size for k in SMALL_ORDER] + [GATE_RANK * QK_W, 128]
    offs = [0]
    for s in sizes:
        offs.append(offs[-1] + s)
    pieces = [total[offs[i]:offs[i + 1]] for i in range(len(sizes))]
    g_small = {k: pieces[i].reshape(small[k].shape) for i, k in enumerate(SMALL_ORDER)}
    g_wa2_full = pieces[len(SMALL_ORDER)].reshape(GATE_RANK, QK_W)
    g_wa2 = lax.dynamic_slice(g_wa2_full, (0, dev * ab), (GATE_RANK, ab))
    loss_total = pieces[len(SMALL_ORDER) + 1][0]

    small_grads = {"ffn1_norm_g": g_small["ffn1"], "mix_norm_g": g_small["mix"], "ret_norm_g": g_small["ret"],
                   "gla_w_a2": g_wa2, "gla_b_a": g_small["b_a"], "gla_norm_g": g_small["gla"],
                   "ffn2_norm_g": g_small["ffn2"], "final_norm_g": g_small["final"]}
    run_adam("adamw_small", list(small_grads), small_grads, 1)

    order = ("ffn1_norm_g", "ffn1_w_gate", "ffn1_w_up", "ffn1_w_down", "mix_norm_g", "w_in", "ret_norm_g", "gla_w_a2",
             "gla_b_a", "gla_norm_g", "w_out", "ffn2_norm_g", "ffn2_w_gate", "ffn2_w_up", "ffn2_w_down", "final_norm_g")
    return (loss_total, dx0[None], *[grads[nm] for nm in order], *[updates[nm][0] for nm in order],
            *[updates[nm][1] for nm in order], *[updates[nm][2] for nm in order])
```

```python
import functools
import math

import jax
import jax.numpy as jnp
from jax import lax
from jax.experimental import pallas as pl
from jax.experimental.pallas import tpu as pltpu

F32 = jnp.float32
BF16 = jnp.bfloat16
MESH = pl.DeviceIdType.MESH
HBM = pl.BlockSpec(memory_space=pltpu.HBM)

N_DEV = 8
RMS_EPS = 1e-6
ROPE_BASE = 10000.0
HEADS = 4
DK = 64
DV = 128
QK_W = HEADS * DK
V_W = HEADS * DV
GATE_RANK = 16
GATE_NORM = 16.0
CHUNK = 64
SUPER = 256
PROJ_W = 3200
C_RQ, C_RK, C_RV, C_RG, C_GQ, C_GK, C_GV, C_GG, C_GL = 0, 256, 512, 1024, 1536, 1792, 2048, 2560, 3072
GL_W = PROJ_W - C_GL
ADAM_LR, ADAM_B1, ADAM_B2, ADAM_EPS, ADAM_WD, ADAM_STEP = 0.001, 0.9, 0.999, 1e-08, 0.01, 10
VMEM_LIMIT_V7X = 52 * 1024 * 1024


def _cparams(**kw):
    return pltpu.CompilerParams(vmem_limit_bytes=VMEM_LIMIT_V7X, **kw)


def _dot(a, b, form, precision=None):
    dims = {"nn": (((1,), (0,)), ((), ())), "nt": (((1,), (1,)), ((), ())), "tn": (((0,), (0,)), ((), ()))}[form]
    return lax.dot_general(a, b, dims, preferred_element_type=F32, precision=precision)


def _sigmoid(x):
    return 1.0 / (1.0 + jnp.exp(-x))


def _coords():
    return lax.axis_index("x"), lax.axis_index("y"), lax.axis_index("c")


class _NoComm:
    inputs, out_shapes, scratch = (), (), ()


class _AllGather:
    def __init__(self, arrays, kinds):
        self.inputs = tuple(arrays)
        self.kinds = tuple(kinds)
        n = len(arrays)
        self.out_shapes = tuple(
            jax.ShapeDtypeStruct((a.shape[0], N_DEV) + a.shape[1:] if k == "stack" else (N_DEV,) + a.shape, a.dtype)
            for a, k in zip(arrays, kinds))
        self.scratch = (pltpu.SemaphoreType.DMA((n, 7)), pltpu.SemaphoreType.DMA((n, 7)),
                        pltpu.SemaphoreType.DMA((n,)))

    def _ctx(self, srcs, outs, sems):
        send_sems, recv_sems, local_sems = sems
        x, y, c = _coords()
        me, sibling = (x, y, c), (x, y, 1 - c)
        chips = [(1 - x, y), (x, 1 - y), (1 - x, 1 - y)]

        def blk(m, dev):
            k = 4 * dev[0] + 2 * dev[1] + dev[2]
            return outs[m].at[:, k] if self.kinds[m] == "stack" else outs[m].at[k]

        def copy(m, s, block, to, src=None):
            return pltpu.make_async_remote_copy(
                src_ref=blk(m, block) if src is None else src, dst_ref=blk(m, block),
                send_sem=send_sems.at[m, s], recv_sem=recv_sems.at[m, s], device_id=to, device_id_type=MESH)

        def mine(m):
            return pltpu.make_async_copy(srcs[m], blk(m, me), local_sems.at[m])

        def first(m):
            return [copy(m, 0, me, sibling, src=srcs[m])] + [
                copy(m, 1 + j, me, (*chip, c), src=srcs[m]) for j, chip in enumerate(chips)]

        return me, sibling, chips, c, copy, mine, first

    def start(self, srcs, outs, sems):
        me, sibling, chips, c, copy, mine, first = self._ctx(srcs, outs, sems)
        for m in range(len(srcs)):
            mine(m).start()
            for cp in first(m):
                cp.start()

    def mid(self, srcs, outs, sems):
        me, sibling, chips, c, copy, mine, first = self._ctx(srcs, outs, sems)
        for j, chip in enumerate(chips):
            for m in range(len(srcs)):
                copy(m, 1 + j, (*chip, c), me).wait_recv()
                copy(m, 4 + j, (*chip, c), sibling).start()

    def finish(self, srcs, outs, sems):
        me, sibling, chips, c, copy, mine, first = self._ctx(srcs, outs, sems)
        for m in range(len(srcs)):
            copy(m, 0, sibling, me).wait_recv()
            for j, chip in enumerate(chips):
                copy(m, 4 + j, (*chip, 1 - c), me).wait_recv()
            for cp in first(m):
                cp.wait_send()
            for j, chip in enumerate(chips):
                copy(m, 4 + j, (*chip, c), sibling).wait_send()
            mine(m).wait()


RELATIONS = ((0, 0, 1), (1, 0, 0), (0, 1, 0), (1, 1, 0), (1, 0, 1), (0, 1, 1), (1, 1, 1))
SEM = pl.BlockSpec(memory_space=pltpu.SEMAPHORE)
SPLIT_PARAMS = dict(has_side_effects=pltpu.SideEffectType.DATAFLOW_SIDE_EFFECTING)


def _owner_copies(grad_ref, land_ref, send_sems, recv_sems):
    x, y, c = _coords()
    copies = []
    for s, (fx, fy, fc) in enumerate(RELATIONS):
        px = 1 - x if fx else x
        py = 1 - y if fy else y
        pc = 1 - c if fc else c
        copies.append(pltpu.make_async_remote_copy(
            src_ref=grad_ref.at[2 * px + py, pc], dst_ref=land_ref.at[s], send_sem=send_sems.at[s],
            recv_sem=recv_sems.at[s], device_id=(px, py, pc), device_id_type=MESH))
    return copies


def _send_to_owners(name, grad):
    n = len(RELATIONS)
    land_shape = (n,) + grad.shape[2:]

    def body(g_ref, land_ref, send_sems, recv_sems, g_thru, land_thru, token):
        for cp in _owner_copies(g_ref, land_ref, send_sems, recv_sems):
            cp.start()
        token[...] = jnp.zeros_like(token)

    return pl.pallas_call(
        body, name=name,
        out_shape=(pltpu.SemaphoreType.DMA((n,)), pltpu.SemaphoreType.DMA((n,)), pltpu.HBM(grad.shape, grad.dtype),
                   pltpu.HBM(land_shape, grad.dtype), jax.ShapeDtypeStruct((8, 128), F32)),
        in_specs=(HBM, HBM), out_specs=(SEM, SEM, HBM, HBM, pl.BlockSpec(memory_space=pltpu.VMEM)),
        input_output_aliases={0: 2, 1: 3}, compiler_params=pltpu.CompilerParams(**SPLIT_PARAMS),
    )(pltpu.with_memory_space_constraint(grad, pltpu.HBM),
      pltpu.with_memory_space_constraint(lax.empty(land_shape, grad.dtype), pltpu.HBM))


def _await_owners(name, started, after):
    send_sems, recv_sems, g_thru, land_thru, _ = started

    def body(g_ref, land_ref, send_sems, recv_sems, after_ref, g_out, land_out):
        for cp in _owner_copies(g_ref, land_ref, send_sems, recv_sems):
            cp.wait_send()
            cp.wait_recv()

    return pl.pallas_call(
        body, name=name, out_shape=(pltpu.HBM(g_thru.shape, g_thru.dtype), pltpu.HBM(land_thru.shape, land_thru.dtype)),
        in_specs=(HBM, HBM, SEM, SEM, pl.BlockSpec(memory_space=pl.ANY)), out_specs=(HBM, HBM),
        input_output_aliases={0: 0, 1: 1}, compiler_params=pltpu.CompilerParams(**SPLIT_PARAMS),
    )(g_thru, land_thru, send_sems, recv_sems, after)


def _call(name, main, grid, in_specs, out_specs, out_shape, args, scratch=(), comm=None, prefetch=None, after=None):
    comm = comm or _NoComm()
    n_main = len(in_specs)
    if after is not None:
        in_specs = list(in_specs) + [pl.BlockSpec(after.shape, lambda *_: (0,) * after.ndim)]
        args = tuple(args) + (after,)
    counts = [len(in_specs), len(comm.inputs), len(out_shape), len(comm.out_shapes), len(scratch), len(comm.scratch)]
    n_steps = math.prod(grid)
    hosted = bool(comm.inputs)

    def body(*refs):
        if prefetch is not None:
            refs = refs[1:]
        parts, at = [], 0
        for n in counts:
            parts.append(refs[at:at + n])
            at += n
        ins, c_in, outs, c_out, scr, c_scr = parts
        ins = ins[:n_main]
        step = pl.program_id(0)
        for d in range(1, len(grid)):
            step = step * grid[d] + pl.program_id(d)
        if hosted:
            @pl.when(step == 0)
            def _():
                comm.start(c_in, c_out, c_scr)
        main(ins, outs, scr)
        if hosted:
            @pl.when(step == max(n_steps - 2, 0))
            def _():
                comm.mid(c_in, c_out, c_scr)

            @pl.when(step == n_steps - 1)
            def _():
                comm.finish(c_in, c_out, c_scr)

    all_in = list(in_specs) + [HBM] * counts[1]
    all_out = list(out_specs) + [HBM] * counts[3]
    all_scratch = list(scratch) + list(comm.scratch)
    shapes = list(out_shape) + list(comm.out_shapes)
    if prefetch is None:
        res = pl.pallas_call(body, name=name, grid=grid, in_specs=all_in, out_specs=all_out, out_shape=shapes,
                             scratch_shapes=all_scratch, compiler_params=_cparams())(*args, *comm.inputs)
    else:
        res = pl.pallas_call(
            body, name=name, out_shape=shapes,
            grid_spec=pltpu.PrefetchScalarGridSpec(num_scalar_prefetch=1, grid=grid, in_specs=all_in,
                                                   out_specs=all_out, scratch_shapes=all_scratch),
            compiler_params=_cparams())(prefetch, *args, *comm.inputs)
    return res[:counts[2]], res[counts[2]:]


def _rms_fwd(name, x, g, comm=None):
    T, D = x.shape
    tm = min(T, 512)

    def main(ins, outs, scr):
        x_ref, g_ref = ins
        xv = x_ref[...]
        r = lax.rsqrt(jnp.mean(xv * xv, axis=-1, keepdims=True) + RMS_EPS)
        outs[0][...] = (xv * r * g_ref[...]).astype(outs[0].dtype)

    tile = pl.BlockSpec((tm, D), lambda i: (i, 0))
    (h,), extra = _call(name, main, (T // tm,), [tile, pl.BlockSpec((1, D), lambda i: (0, 0))], [tile],
                        [jax.ShapeDtypeStruct((T, D), BF16)], (x, g), comm=comm)
    return h, extra


def _final_loss_epilogue(scale, out_scale):
    def ep(acc, ex, outs):
        res_ref, g_ref, t_ref = ex
        dx_ref, dxb_ref, dg_ref, loss_ref = outs
        n = acc.shape[-1]
        xv = res_ref[...] + scale * acc
        r = lax.rsqrt(jnp.mean(xv * xv, axis=-1, keepdims=True) + RMS_EPS)
        xhat = xv * r
        err = xhat * g_ref[...] - t_ref[...]

        @pl.when(pl.program_id(0) == 0)
        def _():
            dg_ref[...] = jnp.zeros_like(dg_ref)
            loss_ref[...] = jnp.zeros_like(loss_ref)

        loss_ref[...] += jnp.broadcast_to(jnp.sum(err * err) * (0.5 / n), loss_ref.shape)
        dy = err * (1.0 / n)
        dg_ref[...] += jnp.sum(dy * xhat, axis=0, keepdims=True)
        dxhat = dy * g_ref[...]
        dx = r * (dxhat - xhat * jnp.mean(dxhat * xhat, axis=-1, keepdims=True))
        dx_ref[...] = dx
        dxb_ref[...] = (out_scale * dx).astype(dxb_ref.dtype)
    return ep


def _mm_nstream(name, a, ws, w_sel, w_form, comps, out_dtypes, epilogue, cn, rows=1024, comm=None, after=None):
    T, K = a.shape
    N = ws[0].shape[1]
    rows = min(rows, T)
    assert N % cn == 0 and T % rows == 0
    n_w, n_c = len(ws), len(comps)

    def main(ins, outs, scr):
        a_ref = ins[0]
        w_refs = ins[1:1 + n_w]
        c_refs = ins[1 + n_w:1 + n_w + n_c]

        for r in range(T // rows):
            sl = slice(r * rows, (r + 1) * rows)
            a_blk = a_ref[sl, :]
            dots = [_dot(a_blk, w_ref[...], w_form) for w_ref in w_refs]
            res = epilogue(dots, [c_ref[sl, :] for c_ref in c_refs])
            for o_ref, o in zip(outs, res):
                o_ref[sl, :] = o.astype(o_ref.dtype)

    if w_form == "nt":
        w_specs = [pl.BlockSpec((None, cn, K), functools.partial(lambda j, s: (s, j, 0), s=s)) for s in w_sel]
    else:
        w_specs = [pl.BlockSpec((K, cn), lambda j: (0, j)) for _ in ws]
    chunk = pl.BlockSpec((T, cn), lambda j: (0, j))
    return _call(name, main, (N // cn,), [pl.BlockSpec((T, K), lambda j: (0, 0))] + w_specs + [chunk] * n_c,
                 [chunk] * len(out_dtypes), [jax.ShapeDtypeStruct((T, N), dt) for dt in out_dtypes],
                 (a, *ws, *comps), comm=comm, after=after)


def _mm_mstream(name, as_, ws, w_sel, w_form, extras, outs_desc, epilogue, tm=512, comm=None, after=None):
    T = as_[0].shape[0]
    tm = min(tm, T)
    n_a = len(as_)
    w_shapes = [w.shape[-2:] for w in ws]
    N = w_shapes[0][1] if w_form == "nn" else w_shapes[0][0]

    def main(ins, outs, scr):
        a_refs = ins[:n_a]
        w_refs = ins[n_a:2 * n_a]
        acc = None
        for a_ref, w_ref in zip(a_refs, w_refs):
            d = _dot(a_ref[...], w_ref[...], w_form)
            acc = d if acc is None else acc + d
        epilogue(acc, ins[2 * n_a:], outs)

    kind_spec = {"tile": pl.BlockSpec((tm, N), lambda i: (i, 0)), "vec": pl.BlockSpec((1, N), lambda i: (0, 0))}
    kind_shape = {"tile": (T, N), "vec": (1, N)}
    a_specs = [pl.BlockSpec((tm, a.shape[1]), lambda i: (i, 0)) for a in as_]
    w_specs = []
    for w, s in zip(ws, w_sel):
        if w.ndim == 3:
            w_specs.append(pl.BlockSpec((None,) + tuple(w.shape[1:]), functools.partial(lambda i, s: (s, 0, 0), s=s),
                                        pipeline_mode=pl.Buffered(1)))
        else:
            w_specs.append(pl.BlockSpec(tuple(w.shape), lambda i: (0, 0), pipeline_mode=pl.Buffered(1)))
    args = list(as_) + list(ws) + [e for e, _ in extras]
    return _call(name, main, (T // tm,), a_specs + w_specs + [kind_spec[k] for _, k in extras],
                 [kind_spec[k] for _, k in outs_desc],
                 [jax.ShapeDtypeStruct(kind_shape[k], dt) for dt, k in outs_desc], args, comm=comm, after=after)


def _residual_rms_epilogue(scale):
    def ep(acc, ex, outs):
        xv = ex[0][...] + scale * acc
        outs[0][...] = xv
        r = lax.rsqrt(jnp.mean(xv * xv, axis=-1, keepdims=True) + RMS_EPS)
        outs[1][...] = (xv * r * ex[1][...]).astype(outs[1].dtype)
    return ep


def _rms_bwd_epilogue(out_scale):
    def ep(acc, ex, outs):
        x_ref, g_ref, dres_ref = ex
        dx_ref, dxb_ref, dg_ref = outs
        xv = x_ref[...]
        r = lax.rsqrt(jnp.mean(xv * xv, axis=-1, keepdims=True) + RMS_EPS)
        xhat = xv * r

        @pl.when(pl.program_id(0) == 0)
        def _():
            dg_ref[...] = jnp.zeros_like(dg_ref)

        dg_ref[...] += jnp.sum(acc * xhat, axis=0, keepdims=True)
        dxhat = acc * g_ref[...]
        dx = r * (dxhat - xhat * jnp.mean(dxhat * xhat, axis=-1, keepdims=True)) + dres_ref[...]
        dx_ref[...] = dx
        dxb_ref[...] = (out_scale * dx).astype(dxb_ref.dtype)
    return ep


def _mm_tn(name, a, b, tmo, tno, out_dtype, tk=1024, comm=None, after=None):
    T, Ma = a.shape
    Nb = b.shape[1]
    tk = min(tk, T)
    nk = T // tk

    def main(ins, outs, scr):
        a_ref, b_ref = ins
        (acc_ref,) = scr
        k = pl.program_id(2)

        @pl.when(k == 0)
        def _():
            acc_ref[...] = jnp.zeros_like(acc_ref)

        acc_ref[...] += _dot(a_ref[...], b_ref[...], "tn")

        @pl.when(k == nk - 1)
        def _():
            outs[0][...] = acc_ref[...].astype(outs[0].dtype)

    (out,), extra = _call(
        name, main, (Ma // tmo, Nb // tno, nk),
        [pl.BlockSpec((tk, tmo), lambda i, j, k: (k, i)), pl.BlockSpec((tk, tno), lambda i, j, k: (k, j))],
        [pl.BlockSpec((tmo, tno), lambda i, j, k: (i, j))], [jax.ShapeDtypeStruct((Ma, Nb), out_dtype)],
        (a, b), scratch=[pltpu.VMEM((tmo, tno), F32)], comm=comm, after=after)
    return out, extra


def _swiglu_parts(g, u):
    s = _sigmoid(g)
    silu = g * s
    return [u * (s + silu * (1.0 - s)), silu, silu * u]


def _silu_mul_epilogue(dots, comps):
    g, u = dots
    return _swiglu_parts(g, u)


def _gate_parts_epilogue(dots, comps):
    (g,) = dots
    s = _sigmoid(g)
    silu = g * s
    return [s + silu * (1.0 - s), silu]


def _up_act_epilogue(dots, comps):
    (u,) = dots
    return [u * comps[0].astype(F32), u * comps[1].astype(F32)]


def _dact_epilogue(dots, comps):
    dact = dots[0].astype(BF16)
    return [dact * comps[0], dact * comps[1]]


def _identity_epilogue(dots, comps):
    return list(dots)


def _swap_halves(x):
    lane = lax.broadcasted_iota(jnp.int32, x.shape, 1)
    first = (lane % DK) < (DK // 2)
    return jnp.where(first, pltpu.roll(x, 128 - DK // 2, 1), pltpu.roll(x, DK // 2, 1))


def _rotary(t, cos, sin_signed):
    halves = []
    for p in range(QK_W // 128):
        th = t[:, 128 * p:128 * (p + 1)]
        halves.append(th * cos + _swap_halves(th) * sin_signed)
    return jnp.concatenate(halves, axis=1)


def _rotary_transposed(d, cos, sin_signed):
    halves = []
    for p in range(QK_W // 128):
        dh = d[:, 128 * p:128 * (p + 1)]
        halves.append(dh * cos + _swap_halves(dh * sin_signed))
    return jnp.concatenate(halves, axis=1)


def _log_sigmoid(x):
    return jnp.minimum(x, 0.0) - jnp.log(1.0 + jnp.exp(-jnp.abs(x)))


def _attn_masks():
    row = lax.broadcasted_iota(jnp.int32, (SUPER, SUPER), 0)
    col = lax.broadcasted_iota(jnp.int32, (SUPER, SUPER), 1)
    same = (row // CHUNK) == (col // CHUNK)
    return row, col, same


def _group_inputs(grp, pr, cos, sin_signed, lg, wa2, ba):
    seg = lambda lo, width: pr[:, lo:lo + width].astype(F32)
    if grp == 0:
        q = _rotary(seg(C_RQ, QK_W), cos, sin_signed)
        k = _rotary(seg(C_RK, QK_W), cos, sin_signed) * (DK ** -0.5)
        v = pr[:, C_RV:C_RV + V_W]
        gate = seg(C_RG, V_W)
        pos = lax.broadcasted_iota(jnp.int32, (SUPER, QK_W), 0).astype(F32) + 1.0
        return q, k, v, gate, pos * lg, None, None
    q = seg(C_GQ, QK_W) * (DK ** -0.5)
    k = seg(C_GK, QK_W)
    v = pr[:, C_GV:C_GV + V_W]
    gate = seg(C_GG, V_W)
    glow = pr[:, C_GL:C_GL + GL_W]
    logit = _dot(glow.astype(BF16), wa2.astype(BF16), "nn") + ba
    la = _log_sigmoid(logit) * (1.0 / GATE_NORM)
    row, col, _ = _attn_masks()
    lower = (col <= row).astype(F32)
    b_cum = _dot(lower, la, "nn", precision=lax.Precision.HIGHEST)
    return q, k, v, gate, b_cum, glow, logit


def _decay_factors(q, k, b_cum):
    c = b_cum[SUPER // 2 - 1:SUPER // 2, :]
    bl = b_cum[SUPER - 1:SUPER, :]
    e1 = jnp.exp(b_cum - c)
    e2 = jnp.exp(c - b_cum)
    e_b = jnp.exp(b_cum)
    e_l = jnp.exp(bl - b_cum)
    return dict(e1=e1, e2=e2, eb=e_b, el=e_l, ebl=jnp.exp(bl),
                qp=q * e1, qm=q * e2, kp=k * e1, km=k * e2, qs=q * e_b, kl=k * e_l)


def _state_block_mask():
    r = lax.broadcasted_iota(jnp.int32, (V_W, QK_W), 0)
    c = lax.broadcasted_iota(jnp.int32, (V_W, QK_W), 1)
    return (r // DV) == (c // DK)


def _attn_fwd(proj, cos, sin_signed, lg, wa2p, ba, gn_ret, gn_gla, x_res, w_out, g_next, comm=None):
    T = proj.shape[0]
    n_s = T // SUPER
    D = x_res.shape[1]

    def main(ins, outs, scr):
        pr_ref, cos_ref, sin_ref, lg_ref, wa2_ref, ba_ref, gr_ref, gg_ref, xres_ref, wout_ref, gnext_ref = ins
        o_ref, y_ref, st_ref, x_ref, h_ref = outs
        (s_ref,) = scr
        i = pl.program_id(0)

        @pl.when(i == 0)
        def _():
            s_ref[...] = jnp.zeros_like(s_ref)

        pr = pr_ref
        row, col, same = _attn_masks()
        m1 = col <= row
        m2 = jnp.logical_and(col > row, same)
        lane = lax.broadcasted_iota(jnp.int32, (1, QK_W), 1)
        blockmask = _state_block_mask()
        for grp in range(2):
            q, k, v, gate, b_cum, _, _ = _group_inputs(grp, pr, cos_ref[...], sin_ref[...], lg_ref[...],
                                                      wa2_ref[...], ba_ref[...])
            f = _decay_factors(q, k, b_cum)
            gn = gr_ref[...] if grp == 0 else gg_ref[...]
            s_prev = s_ref[grp]
            st_ref[0, grp] = s_prev
            o_inter = _dot(f["qs"].astype(BF16), s_prev.astype(BF16), "nt")
            kmb = f["km"].astype(BF16)
            kpb = f["kp"].astype(BF16)
            vb = v.astype(BF16)
            for h in range(HEADS):
                hm = (lane // DK) == h
                a1 = _dot(jnp.where(hm, f["qp"], 0.0).astype(BF16), kmb, "nt")
                a2 = _dot(jnp.where(hm, f["qm"], 0.0).astype(BF16), kpb, "nt")
                a = jnp.where(m1, a1, jnp.where(m2, a2, 0.0))
                lo = grp * V_W + h * DV
                o_h = _dot(a.astype(BF16), vb[:, h * DV:(h + 1) * DV], "nn") + o_inter[:, h * DV:(h + 1) * DV]
                o_ref[:, lo:lo + DV] = o_h
                r = lax.rsqrt(jnp.mean(o_h * o_h, axis=-1, keepdims=True) + RMS_EPS)
                gte = gate[:, h * DV:(h + 1) * DV]
                y = o_h * r * gn[:, h * DV:(h + 1) * DV] * (gte * _sigmoid(gte))
                y_ref[:, lo:lo + DV] = y.astype(y_ref.dtype)
            upd = _dot(vb, f["kl"].astype(BF16), "tn")
            s_ref[grp] = s_prev * f["ebl"] + jnp.where(blockmask, upd, 0.0)
        xv = xres_ref[...] + _dot(y_ref[...], wout_ref[...], "nn")
        x_ref[...] = xv
        r = lax.rsqrt(jnp.mean(xv * xv, axis=-1, keepdims=True) + RMS_EPS)
        h_ref[...] = (xv * r * gnext_ref[...]).astype(h_ref.dtype)

    const = lambda shape: pl.BlockSpec(shape, lambda i: tuple(0 for _ in shape))
    rows = lambda w: pl.BlockSpec((SUPER, w), lambda i: (i, 0))
    return _call(
        "attn_fwd", main, (n_s,),
        [rows(PROJ_W), rows(128), rows(128),
         const((1, QK_W)), const((GL_W, QK_W)), const((1, QK_W)), const((1, V_W)), const((1, V_W)),
         rows(D), const((2 * V_W, D)), const((1, D))],
        [rows(2 * V_W), rows(2 * V_W), pl.BlockSpec((1, 2, V_W, QK_W), lambda i: (i, 0, 0, 0)), rows(D), rows(D)],
        [jax.ShapeDtypeStruct((T, 2 * V_W), F32), jax.ShapeDtypeStruct((T, 2 * V_W), BF16),
         jax.ShapeDtypeStruct((n_s, 2, V_W, QK_W), F32), jax.ShapeDtypeStruct((T, D), F32),
         jax.ShapeDtypeStruct((T, D), BF16)],
        (proj, cos, sin_signed, lg, wa2p, ba, gn_ret, gn_gla, x_res, w_out, g_next),
        scratch=[pltpu.VMEM((2, V_W, QK_W), F32)], comm=comm)


def _attn_bwd(proj, cos, sin_signed, lg, wa2p, ba, gn_ret, gn_gla, o, dx, w_out, states, comm=None, after=None):
    T = proj.shape[0]
    n_s = T // SUPER
    D = dx.shape[1]

    def main(ins, outs, scr):
        pr_ref, cos_ref, sin_ref, lg_ref, wa2_ref, ba_ref, gr_ref, gg_ref, o_ref, dx_ref, wout_ref, st_ref = ins
        dp_ref, dgr_ref, dgg_ref, dba_ref, dwa_ref = outs
        (ds_ref, dy_ref) = scr
        i = pl.program_id(0)
        dy_ref[...] = _dot(dx_ref[...], wout_ref[...], "nt")

        @pl.when(i == 0)
        def _():
            ds_ref[...] = jnp.zeros_like(ds_ref)
            dgr_ref[...] = jnp.zeros_like(dgr_ref)
            dgg_ref[...] = jnp.zeros_like(dgg_ref)
            dba_ref[...] = jnp.zeros_like(dba_ref)
            dwa_ref[...] = jnp.zeros_like(dwa_ref)

        pr = pr_ref
        cos = cos_ref[...]
        sin_signed = sin_ref[...]
        row, col, same = _attn_masks()
        m1 = col <= row
        m2 = jnp.logical_and(col > row, same)
        m1t = row <= col
        m2t = jnp.logical_and(row > col, same)
        lane = lax.broadcasted_iota(jnp.int32, (1, QK_W), 1)
        blockmask = _state_block_mask()
        for grp in range(2):
            q, k, v, gate, b_cum, glow, logit = _group_inputs(grp, pr, cos, sin_signed, lg_ref[...],
                                                              wa2_ref[...], ba_ref[...])
            f = _decay_factors(q, k, b_cum)
            gn = gr_ref[...] if grp == 0 else gg_ref[...]
            dgn_ref = dgr_ref if grp == 0 else dgg_ref
            do_parts, dgate_parts, dgn_parts = [], [], []
            for h in range(HEADS):
                lo = grp * V_W + h * DV
                o_h = o_ref[:, lo:lo + DV]
                r = lax.rsqrt(jnp.mean(o_h * o_h, axis=-1, keepdims=True) + RMS_EPS)
                n = o_h * r
                gte = gate[:, h * DV:(h + 1) * DV]
                sg = _sigmoid(gte)
                dy_h = dy_ref[:, lo:lo + DV]
                gn_h = gn[:, h * DV:(h + 1) * DV]
                dgate_parts.append(dy_h * n * gn_h * (sg * (1.0 + gte * (1.0 - sg))))
                dz = dy_h * (gte * sg)
                dgn_parts.append(jnp.sum(dz * n, axis=0, keepdims=True))
                dn = dz * gn_h
                do_parts.append(r * (dn - n * jnp.mean(dn * n, axis=-1, keepdims=True)))
            dgn_ref[...] += jnp.concatenate(dgn_parts, axis=1)
            dgate = jnp.concatenate(dgate_parts, axis=1)
            do = jnp.concatenate(do_parts, axis=1)
            dob = do.astype(BF16)
            vb = v.astype(BF16)
            s_prev = st_ref[0, grp]
            ds_new = ds_ref[grp]
            dsb = ds_new.astype(BF16)
            qpb, qmb = f["qp"].astype(BF16), f["qm"].astype(BF16)
            kpb, kmb = f["kp"].astype(BF16), f["km"].astype(BF16)
            dqp = jnp.zeros((SUPER, QK_W), F32)
            dqm = jnp.zeros((SUPER, QK_W), F32)
            dkp = jnp.zeros((SUPER, QK_W), F32)
            dkm = jnp.zeros((SUPER, QK_W), F32)
            dv_parts = []
            for h in range(HEADS):
                hm = (lane // DK) == h
                qp_h = jnp.where(hm, f["qp"], 0.0).astype(BF16)
                qm_h = jnp.where(hm, f["qm"], 0.0).astype(BF16)
                kp_h = jnp.where(hm, f["kp"], 0.0).astype(BF16)
                km_h = jnp.where(hm, f["km"], 0.0).astype(BF16)
                at = jnp.where(m1t, _dot(km_h, qpb, "nt"), jnp.where(m2t, _dot(kp_h, qmb, "nt"), 0.0))
                do_h = dob[:, h * DV:(h + 1) * DV]
                v_h = vb[:, h * DV:(h + 1) * DV]
                dv_parts.append(_dot(at.astype(BF16), do_h, "nn"))
                da = _dot(do_h, v_h, "nt")
                dat = _dot(v_h, do_h, "nt")
                da1 = jnp.where(m1, da, 0.0).astype(BF16)
                da2 = jnp.where(m2, da, 0.0).astype(BF16)
                da1t = jnp.where(m1t, dat, 0.0).astype(BF16)
                da2t = jnp.where(m2t, dat, 0.0).astype(BF16)
                dqp = dqp + _dot(da1, km_h, "nn")
                dqm = dqm + _dot(da2, kp_h, "nn")
                dkm = dkm + _dot(da1t, qp_h, "nn")
                dkp = dkp + _dot(da2t, qm_h, "nn")
            klb = f["kl"].astype(BF16)
            qsb = f["qs"].astype(BF16)
            dqs = _dot(dob, s_prev.astype(BF16), "nn")
            dkl = _dot(vb, dsb, "nn")
            dv = jnp.concatenate(dv_parts, axis=1) + _dot(klb, dsb, "nt")
            ds_ref[grp] = ds_new * f["ebl"] + jnp.where(blockmask, _dot(dob, qsb, "tn"), 0.0)
            dq = dqp * f["e1"] + dqm * f["e2"] + dqs * f["eb"]
            dk = dkm * f["e2"] + dkp * f["e1"] + dkl * f["el"]
            if grp == 0:
                dq = _rotary_transposed(dq, cos, sin_signed)
                dk = _rotary_transposed(dk * (DK ** -0.5), cos, sin_signed)
                dp_ref[:, C_RQ:C_RQ + QK_W] = dq.astype(dp_ref.dtype)
                dp_ref[:, C_RK:C_RK + QK_W] = dk.astype(dp_ref.dtype)
                dp_ref[:, C_RV:C_RV + V_W] = dv.astype(dp_ref.dtype)
                dp_ref[:, C_RG:C_RG + V_W] = dgate.astype(dp_ref.dtype)
            else:
                dkl_kl = dkl * klb.astype(F32)
                db = (dqp * qpb.astype(F32) - dkm * kmb.astype(F32) - dqm * qmb.astype(F32)
                      + dkp * kpb.astype(F32) + dqs * qsb.astype(F32) - dkl_kl)
                last = (jnp.sum(dkl_kl, axis=0, keepdims=True)
                        + f["ebl"] * jnp.sum(s_prev * ds_new, axis=0, keepdims=True))
                rowq = lax.broadcasted_iota(jnp.int32, (SUPER, QK_W), 0)
                db = db + jnp.where(rowq == SUPER - 1, last, 0.0)
                upper = (col >= row).astype(F32)
                dla = _dot(upper, db, "nn", precision=lax.Precision.HIGHEST)
                dlogit = dla * (1.0 / GATE_NORM) * (1.0 - _sigmoid(logit))
                dlb = dlogit.astype(BF16)
                dglow = _dot(dlb, wa2_ref[...].astype(BF16), "nt")
                dwa_ref[...] += _dot(glow.astype(BF16), dlb, "tn")
                dba_ref[...] += jnp.sum(dlogit, axis=0, keepdims=True)
                dp_ref[:, C_GQ:C_GQ + QK_W] = (dq * (DK ** -0.5)).astype(dp_ref.dtype)
                dp_ref[:, C_GK:C_GK + QK_W] = dk.astype(dp_ref.dtype)
                dp_ref[:, C_GV:C_GV + V_W] = dv.astype(dp_ref.dtype)
                dp_ref[:, C_GG:C_GG + V_W] = dgate.astype(dp_ref.dtype)
                dp_ref[:, C_GL:C_GL + GL_W] = dglow.astype(dp_ref.dtype)

    rev = lambda i: n_s - 1 - i
    const = lambda shape: pl.BlockSpec(shape, lambda i: tuple(0 for _ in shape))
    return _call(
        "attn_bwd", main, (n_s,),
        [pl.BlockSpec((SUPER, PROJ_W), lambda i: (rev(i), 0)),
         pl.BlockSpec((SUPER, 128), lambda i: (rev(i), 0)), pl.BlockSpec((SUPER, 128), lambda i: (rev(i), 0)),
         const((1, QK_W)), const((GL_W, QK_W)), const((1, QK_W)), const((1, V_W)), const((1, V_W)),
         pl.BlockSpec((SUPER, 2 * V_W), lambda i: (rev(i), 0)),
         pl.BlockSpec((SUPER, D), lambda i: (rev(i), 0)), const((2 * V_W, D)),
         pl.BlockSpec((1, 2, V_W, QK_W), lambda i: (rev(i), 0, 0, 0))],
        [pl.BlockSpec((SUPER, PROJ_W), lambda i: (rev(i), 0)),
         const((1, V_W)), const((1, V_W)), const((1, QK_W)), const((GL_W, QK_W))],
        [jax.ShapeDtypeStruct((T, PROJ_W), BF16),
         jax.ShapeDtypeStruct((1, V_W), F32), jax.ShapeDtypeStruct((1, V_W), F32),
         jax.ShapeDtypeStruct((1, QK_W), F32), jax.ShapeDtypeStruct((GL_W, QK_W), F32)],
        (proj, cos, sin_signed, lg, wa2p, ba, gn_ret, gn_gla, o, dx, w_out, states),
        scratch=[pltpu.VMEM((2, V_W, QK_W), F32), pltpu.VMEM((SUPER, 2 * V_W), F32)], comm=comm, after=after)


def _rotary_tables(T):
    half = DK // 2
    inv = ROPE_BASE ** (-jnp.arange(half, dtype=F32) * 2.0 / DK)
    ang = jnp.arange(T, dtype=F32)[:, None] * inv[None, :]
    cos, sin = jnp.cos(ang), jnp.sin(ang)
    cos_head = jnp.concatenate([cos, cos], axis=1)
    sin_head = jnp.concatenate([-sin, sin], axis=1)
    return jnp.tile(cos_head, (1, 128 // DK)), jnp.tile(sin_head, (1, 128 // DK))


def _sum_devices(name, gathered, m_per):
    def body(g_ref, o_ref):
        acc = g_ref[0:m_per, :]
        for k in range(1, N_DEV):
            acc = acc + g_ref[k * m_per:(k + 1) * m_per, :]
        o_ref[...] = acc

    return pl.pallas_call(body, name=name, out_shape=jax.ShapeDtypeStruct((m_per, 128), F32))(gathered)


def _owner_sums(name, items, owner, comm=None):
    counts = [1 + len(landed) for _, landed in items]

    def main(ins, outs, scr):
        at = 0
        for o_ref, n in zip(outs, counts):
            acc = ins[at][...].astype(F32)
            for l_ref in ins[at + 1:at + n]:
                for j in range(l_ref.shape[0]):
                    acc = acc + l_ref[j].astype(F32)
            o_ref[...] = acc
            at += n

    once = pl.Buffered(1)
    in_specs, out_specs, out_shape, args = [], [], [], []
    for grad, landed in items:
        R, C = grad.shape[-2:]
        in_specs.append(pl.BlockSpec((None, None, R, C), lambda i, s: (s[0], s[1], 0, 0), pipeline_mode=once))
        in_specs += [pl.BlockSpec(tuple(l.shape), lambda i, s: (0, 0, 0), pipeline_mode=once) for l in landed]
        out_specs.append(pl.BlockSpec((R, C), lambda i, s: (0, 0)))
        out_shape.append(jax.ShapeDtypeStruct((R, C), F32))
        args += [grad, *landed]
    return _call(name, main, (1,), in_specs, out_specs, out_shape, args, comm=comm, prefetch=owner)


def _adamw_group(name, items, n_blocks, comm=None):
    n = len(items)

    def main(ins, outs, scr):
        for p in range(n):
            g_ref, w_ref, m_ref, v_ref = ins[4 * p:4 * p + 4]
            d_ref, nm_ref, nv_ref = outs[3 * p:3 * p + 3]
            gv = g_ref[...]
            nm = ADAM_B1 * m_ref[...] + (1.0 - ADAM_B1) * gv
            nv = ADAM_B2 * v_ref[...] + (1.0 - ADAM_B2) * (gv * gv)
            m_hat = nm / (1.0 - ADAM_B1 ** ADAM_STEP)
            v_hat = nv / (1.0 - ADAM_B2 ** ADAM_STEP)
            d_ref[...] = -ADAM_LR * (m_hat / (jnp.sqrt(v_hat) + ADAM_EPS) + ADAM_WD * w_ref[...])
            nm_ref[...] = nm
            nv_ref[...] = nv

    in_specs, out_specs, out_shape, args = [], [], [], []
    for item in items:
        R, C = item[1].shape
        assert R % n_blocks == 0
        spec = pl.BlockSpec((R // n_blocks, C), lambda i: (i, 0))
        in_specs += [spec] * 4
        out_specs += [spec] * 3
        out_shape += [jax.ShapeDtypeStruct((R, C), F32)] * 3
        args += list(item)
    outs, extra = _call(name, main, (n_blocks,), in_specs, out_specs, out_shape, args, comm=comm)
    return [tuple(outs[3 * p:3 * p + 3]) for p in range(n)], extra


SMALL_ORDER = ("ffn1", "mix", "ffn2", "final", "ret", "gla", "b_a")


def kernel(x, ffn1_norm_g, ffn1_w_gate, ffn1_w_up, ffn1_w_down, mix_norm_g, w_in, ret_norm_g, gla_w_a2, gla_b_a, gla_norm_g, w_out, ffn2_norm_g, ffn2_w_gate, ffn2_w_up, ffn2_w_down, final_norm_g, loss_target, m_ffn1_norm_g, m_ffn1_w_gate, m_ffn1_w_up, m_ffn1_w_down, m_mix_norm_g, m_w_in, m_ret_norm_g, m_gla_w_a2, m_gla_b_a, m_gla_norm_g, m_w_out, m_ffn2_norm_g, m_ffn2_w_gate, m_ffn2_w_up, m_ffn2_w_down, m_final_norm_g, v_ffn1_norm_g, v_ffn1_w_gate, v_ffn1_w_up, v_ffn1_w_down, v_mix_norm_g, v_w_in, v_ret_norm_g, v_gla_w_a2, v_gla_b_a, v_gla_norm_g, v_w_out, v_ffn2_norm_g, v_ffn2_w_gate, v_ffn2_w_up, v_ffn2_w_down, v_final_norm_g):
    xi, yi, ci = _coords()
    dev = 4 * xi + 2 * yi + ci
    owner = jnp.stack([2 * xi + yi, ci]).astype(jnp.int32)

    x0, target = x[0], loss_target[0]
    T, D = x0.shape
    fb = ffn1_w_gate.shape[2]
    ib = w_in.shape[2]
    ab = gla_w_a2.shape[2]
    F = N_DEV * fb
    cos, sin_signed = _rotary_tables(T)
    lg = jnp.repeat(jnp.log(1.0 - 2.0 ** (-5.0 - jnp.arange(HEADS, dtype=F32))), DK)[None, :]
    g_final = final_norm_g.reshape(1, D)

    g1_loc = ffn1_w_gate[0].T[None].astype(BF16)
    u1_loc = ffn1_w_up[0].T[None].astype(BF16)
    d1_loc = ffn1_w_down.astype(BF16)
    g2_loc = ffn2_w_gate[0].T[None].astype(BF16)
    u2_loc = ffn2_w_up[0].T[None].astype(BF16)
    d2_loc = ffn2_w_down.astype(BF16)
    in_loc = w_in[0].T.astype(BF16)
    out_loc = w_out[0].astype(BF16)

    h1, (g1,) = _rms_fwd("ffn1_rms", x0, ffn1_norm_g, comm=_AllGather([g1_loc], ["stack"]))
    g1 = g1.reshape(1, F, D)
    (dsl1, sl1), (u1,) = _mm_nstream("ffn1_gate", h1, [g1], [0], "nt", [], [BF16, BF16], _gate_parts_epilogue, cn=256,
                                     comm=_AllGather([u1_loc], ["stack"]))
    u1 = u1.reshape(1, F, D)
    (dsu1, act1), (d1, out_all) = _mm_nstream("ffn1_up", h1, [u1], [0], "nt", [dsl1, sl1], [BF16, BF16], _up_act_epilogue,
                                              cn=256, comm=_AllGather([d1_loc, out_loc], ["stack", "plain"]))
    d1 = d1.reshape(1, F, D)
    w_out_full = out_all.reshape(D, D)
    f32_tile, bf16_tile, f32_vec = (F32, "tile"), (BF16, "tile"), (F32, "vec")
    (x1, h2), (in_all, a_all) = _mm_mstream(
        "ffn1_down", [act1], [d1], [0], "nn", [(x0, "tile"), (mix_norm_g, "vec")], [f32_tile, bf16_tile],
        _residual_rms_epilogue(0.5), comm=_AllGather([in_loc, gla_w_a2[0]], ["plain", "plain"]))
    w_in_t = jnp.pad(in_all.reshape(1, N_DEV * ib, D), ((0, 0), (0, PROJ_W - N_DEV * ib), (0, 0)))
    wa2 = jnp.transpose(a_all, (1, 0, 2)).reshape(GATE_RANK, N_DEV * ab)
    wa2p = jnp.pad(wa2, ((0, GL_W - GATE_RANK), (0, 0)))

    (proj,), (g2,) = _mm_nstream("mix_proj", h2, [w_in_t], [0], "nt", [], [BF16], _identity_epilogue, cn=640,
                                 comm=_AllGather([g2_loc], ["stack"]))
    (o, ymix, states, x2, h3), (u2,) = _attn_fwd(proj, cos, sin_signed, lg, wa2p, gla_b_a, ret_norm_g, gla_norm_g,
                                                 x1, w_out_full, ffn2_norm_g, comm=_AllGather([u2_loc], ["stack"]))
    g2, u2 = g2.reshape(1, F, D), u2.reshape(1, F, D)

    (dsu2, sl2, act2), (d2,) = _mm_nstream(
        "ffn2_up", h3, [g2, u2], [0, 0], "nt", [], [BF16, BF16, BF16], _silu_mul_epilogue, cn=256,
        comm=_AllGather([d2_loc], ["stack"]))
    d2 = d2.reshape(1, F, D)
    (dx3, dy3b, d_final, loss), _ = _mm_mstream(
        "ffn2_down", [act2], [d2], [0], "nn", [(x2, "tile"), (g_final, "vec"), (target, "tile")],
        [f32_tile, bf16_tile, f32_vec, f32_vec], _final_loss_epilogue(0.5, 0.5))

    sent = {}

    def send(nm, grad):
        sent[nm] = _send_to_owners("send_" + nm, grad)
        return sent[nm][4]

    dwd2, _ = _mm_tn("ffn2b_dwd", act2, dy3b, F // 2, D, BF16)
    tok = send("wd2", dwd2.reshape(4, 2, fb, D))
    (dgate2, dup2), _ = _mm_nstream("ffn2b_dact", dy3b, [d2], [0], "nt", [dsu2, sl2], [BF16, BF16],
                                    _dact_epilogue, cn=256, after=tok)
    dwg2, _ = _mm_tn("ffn2b_dwg", dgate2, h3, F // 2, D, BF16)
    tok = send("wg2", dwg2.reshape(4, 2, fb, D))
    dwu2, _ = _mm_tn("ffn2b_dwu", dup2, h3, F // 2, D, BF16, after=tok)
    tok = send("wu2", dwu2.reshape(4, 2, fb, D))
    rms_outs = [f32_tile, bf16_tile, f32_vec]
    (dx2, dx2b, d_g2), _ = _mm_mstream(
        "ffn2b_dh", [dgate2, dup2], [g2, u2], [0, 0], "nn", [(x2, "tile"), (ffn2_norm_g, "vec"), (dx3, "tile")],
        rms_outs, _rms_bwd_epilogue(1.0), after=tok)

    dwout, _ = _mm_tn("mixb_dwout", ymix, dx2b, D, D, BF16)
    tok = send("wout", dwout.reshape(4, 2, D // N_DEV, D))
    (dproj, d_ret, d_gla, d_ba, d_wa2p), _ = _attn_bwd(
        proj, cos, sin_signed, lg, wa2p, gla_b_a, ret_norm_g, gla_norm_g, o, dx2b, w_out_full, states, after=tok)
    dwin_t, _ = _mm_tn("mixb_dwin", dproj, h2, 640, D, BF16, tk=2048)
    tok = send("win", dwin_t[:N_DEV * ib].reshape(4, 2, ib, D))
    (dx1, dy1b, d_gmix), _ = _mm_mstream(
        "mixb_dh", [dproj], [w_in_t], [0], "nn", [(x1, "tile"), (mix_norm_g, "vec"), (dx2, "tile")],
        rms_outs, _rms_bwd_epilogue(0.5), after=tok)

    dwd1, _ = _mm_tn("ffn1b_dwd", act1, dy1b, F // 2, D, BF16)
    tok = send("wd1", dwd1.reshape(4, 2, fb, D))
    (dgate1, dup1), _ = _mm_nstream("ffn1b_dact", dy1b, [d1], [0], "nt", [dsu1, sl1], [BF16, BF16],
                                    _dact_epilogue, cn=256, after=tok)
    dwg1, _ = _mm_tn("ffn1b_dwg", dgate1, h1, F // 2, D, BF16)
    tok = send("wg1", dwg1.reshape(4, 2, fb, D))
    dwu1, _ = _mm_tn("ffn1b_dwu", dup1, h1, F // 2, D, BF16, after=tok)
    tok = send("wu1", dwu1.reshape(4, 2, fb, D))
    (dx0, _, d_g1), _ = _mm_mstream(
        "ffn1b_dh", [dgate1, dup1], [g1, u1], [0, 0], "nn", [(x0, "tile"), (ffn1_norm_g, "vec"), (dx1, "tile")],
        rms_outs, _rms_bwd_epilogue(1.0), after=tok)

    small = dict(ffn1=d_g1, mix=d_gmix, ffn2=d_g2, final=d_final, ret=d_ret, gla=d_gla, b_a=d_ba)
    flat = jnp.concatenate([small[k].reshape(-1) for k in SMALL_ORDER]
                           + [d_wa2p[:GATE_RANK].reshape(-1), loss[0, :128]])
    rows = -(-flat.shape[0] // 128)
    rows = -(-rows // 8) * 8
    packed = jnp.pad(flat, (0, rows * 128 - flat.shape[0])).reshape(rows, 128)

    transposed = ("ffn1_w_gate", "ffn1_w_up", "ffn2_w_gate", "ffn2_w_up", "w_in")

    def to_2d(nm, a):
        if nm in transposed:
            return a[0].T
        return a.reshape((1, a.shape[0]) if a.ndim == 1 else a.shape[-2:])

    def from_2d(nm, a):
        return a.T[None] if nm in transposed else a.reshape(params[nm][0].shape)

    def arrived(nm, after):
        grad, landed = _await_owners("await_" + nm, sent[nm], after)
        return grad, [landed]

    sums_a, (gathered,) = _owner_sums(
        "sum_a", [arrived(nm, dx0) for nm in ("wg2", "wu2", "wd2", "win", "wout")], owner,
        comm=_AllGather([packed], ["plain"]))
    params = dict(
        ffn2_w_gate=(ffn2_w_gate, m_ffn2_w_gate, v_ffn2_w_gate), ffn2_w_up=(ffn2_w_up, m_ffn2_w_up, v_ffn2_w_up),
        ffn2_w_down=(ffn2_w_down, m_ffn2_w_down, v_ffn2_w_down), w_in=(w_in, m_w_in, v_w_in),
        w_out=(w_out, m_w_out, v_w_out), ffn1_w_gate=(ffn1_w_gate, m_ffn1_w_gate, v_ffn1_w_gate),
        ffn1_w_up=(ffn1_w_up, m_ffn1_w_up, v_ffn1_w_up), ffn1_w_down=(ffn1_w_down, m_ffn1_w_down, v_ffn1_w_down),
        ffn1_norm_g=(ffn1_norm_g, m_ffn1_norm_g, v_ffn1_norm_g), mix_norm_g=(mix_norm_g, m_mix_norm_g, v_mix_norm_g),
        ret_norm_g=(ret_norm_g, m_ret_norm_g, v_ret_norm_g), gla_w_a2=(gla_w_a2, m_gla_w_a2, v_gla_w_a2),
        gla_b_a=(gla_b_a, m_gla_b_a, v_gla_b_a), gla_norm_g=(gla_norm_g, m_gla_norm_g, v_gla_norm_g),
        ffn2_norm_g=(ffn2_norm_g, m_ffn2_norm_g, v_ffn2_norm_g), final_norm_g=(final_norm_g, m_final_norm_g, v_final_norm_g))
    grads, updates = {}, {}

    def run_adam(name, names, grad_2d, n_blocks):
        items = [(grad_2d[nm],) + tuple(to_2d(nm, a) for a in params[nm]) for nm in names]
        res, _ = _adamw_group(name, items, n_blocks)
        for nm, r in zip(names, res):
            grads[nm] = from_2d(nm, grad_2d[nm])
            updates[nm] = tuple(from_2d(nm, a) for a in r)
        return res

    grads_a = {"ffn2_w_gate": sums_a[0], "ffn2_w_up": sums_a[1], "ffn2_w_down": sums_a[2], "w_out": sums_a[4]}
    run_adam("adamw_w_in", ["w_in"], {"w_in": sums_a[3]}, 1)
    done_a = run_adam("adamw_a", list(grads_a), grads_a, 4)[0][0]
    sums_b, _ = _owner_sums("sum_b", [arrived(nm, done_a) for nm in ("wg1", "wu1", "wd1")], owner)
    grads_b = {"ffn1_w_gate": sums_b[0], "ffn1_w_up": sums_b[1], "ffn1_w_down": sums_b[2]}
    run_adam("adamw_b", list(grads_b), grads_b, 4)

    total = _sum_devices("sum_small", gathered.reshape(N_DEV * rows, 128), rows).reshape(-1)
    sizes = [small[k].size for k in SMALL_ORDER] + [GATE_RANK * QK_W, 128]
    offs = [0]
    for s in sizes:
        offs.append(offs[-1] + s)
    pieces = [total[offs[i]:offs[i + 1]] for i in range(len(sizes))]
    g_small = {k: pieces[i].reshape(small[k].shape) for i, k in enumerate(SMALL_ORDER)}
    g_wa2_full = pieces[len(SMALL_ORDER)].reshape(GATE_RANK, QK_W)
    g_wa2 = lax.dynamic_slice(g_wa2_full, (0, dev * ab), (GATE_RANK, ab))
    loss_total = pieces[len(SMALL_ORDER) + 1][0]

    small_grads = {"ffn1_norm_g": g_small["ffn1"], "mix_norm_g": g_small["mix"], "ret_norm_g": g_small["ret"],
                   "gla_w_a2": g_wa2, "gla_b_a": g_small["b_a"], "gla_norm_g": g_small["gla"],
                   "ffn2_norm_g": g_small["ffn2"], "final_norm_g": g_small["final"]}
    run_adam("adamw_small", list(small_grads), small_grads, 1)

    order = ("ffn1_norm_g", "ffn1_w_gate", "ffn1_w_up", "ffn1_w_down", "mix_norm_g", "w_in", "ret_norm_g", "gla_w_a2",
             "gla_b_a", "gla_norm_g", "w_out", "ffn2_norm_g", "ffn2_w_gate", "ffn2_w_up", "ffn2_w_down", "final_norm_g")
    return (loss_total, dx0[None], *[grads[nm] for nm in order], *[updates[nm][0] for nm in order],
            *[updates[nm][1] for nm in order], *[updates[nm][2] for nm in order])
```

```python
import functools
import math

import jax
import jax.numpy as jnp
from jax import lax
from jax.experimental import pallas as pl
from jax.experimental.pallas import tpu as pltpu

F32 = jnp.float32
BF16 = jnp.bfloat16
MESH = pl.DeviceIdType.MESH
HBM = pl.BlockSpec(memory_space=pltpu.HBM)

N_DEV = 8
RMS_EPS = 1e-6
ROPE_BASE = 10000.0
HEADS = 4
DK = 64
DV = 128
QK_W = HEADS * DK
V_W = HEADS * DV
GATE_RANK = 16
GATE_NORM = 16.0
CHUNK = 64
SUPER = 256
PROJ_W = 3200
C_RQ, C_RK, C_RV, C_RG, C_GQ, C_GK, C_GV, C_GG, C_GL = 0, 256, 512, 1024, 1536, 1792, 2048, 2560, 3072
GL_W = PROJ_W - C_GL
ADAM_LR, ADAM_B1, ADAM_B2, ADAM_EPS, ADAM_WD, ADAM_STEP = 0.001, 0.9, 0.999, 1e-08, 0.01, 10
VMEM_LIMIT_V7X = 52 * 1024 * 1024


def _cparams(**kw):
    return pltpu.CompilerParams(vmem_limit_bytes=VMEM_LIMIT_V7X, **kw)


def _dot(a, b, form, precision=None):
    dims = {"nn": (((1,), (0,)), ((), ())), "nt": (((1,), (1,)), ((), ())), "tn": (((0,), (0,)), ((), ()))}[form]
    return lax.dot_general(a, b, dims, preferred_element_type=F32, precision=precision)


def _sigmoid(x):
    return 1.0 / (1.0 + jnp.exp(-x))


def _coords():
    return lax.axis_index("x"), lax.axis_index("y"), lax.axis_index("c")


class _NoComm:
    inputs, out_shapes, scratch = (), (), ()


class _AllGather:
    def __init__(self, arrays, kinds):
        self.inputs = tuple(arrays)
        self.kinds = tuple(kinds)
        n = len(arrays)
        self.out_shapes = tuple(
            jax.ShapeDtypeStruct((a.shape[0], N_DEV) + a.shape[1:] if k == "stack" else (N_DEV,) + a.shape, a.dtype)
            for a, k in zip(arrays, kinds))
        self.scratch = (pltpu.SemaphoreType.DMA((n, 7)), pltpu.SemaphoreType.DMA((n, 7)),
                        pltpu.SemaphoreType.DMA((n,)))

    def _ctx(self, srcs, outs, sems):
        send_sems, recv_sems, local_sems = sems
        x, y, c = _coords()
        me, sibling = (x, y, c), (x, y, 1 - c)
        chips = [(1 - x, y), (x, 1 - y), (1 - x, 1 - y)]

        def blk(m, dev):
            k = 4 * dev[0] + 2 * dev[1] + dev[2]
            return outs[m].at[:, k] if self.kinds[m] == "stack" else outs[m].at[k]

        def copy(m, s, block, to, src=None):
            return pltpu.make_async_remote_copy(
                src_ref=blk(m, block) if src is None else src, dst_ref=blk(m, block),
                send_sem=send_sems.at[m, s], recv_sem=recv_sems.at[m, s], device_id=to, device_id_type=MESH)

        def mine(m):
            return pltpu.make_async_copy(srcs[m], blk(m, me), local_sems.at[m])

        def first(m):
            return [copy(m, 0, me, sibling, src=srcs[m])] + [
                copy(m, 1 + j, me, (*chip, c), src=srcs[m]) for j, chip in enumerate(chips)]

        return me, sibling, chips, c, copy, mine, first

    def start(self, srcs, outs, sems):
        me, sibling, chips, c, copy, mine, first = self._ctx(srcs, outs, sems)
        for m in range(len(srcs)):
            mine(m).start()
            for cp in first(m):
                cp.start()

    def mid(self, srcs, outs, sems):
        me, sibling, chips, c, copy, mine, first = self._ctx(srcs, outs, sems)
        for j, chip in enumerate(chips):
            for m in range(len(srcs)):
                copy(m, 1 + j, (*chip, c), me).wait_recv()
                copy(m, 4 + j, (*chip, c), sibling).start()

    def finish(self, srcs, outs, sems):
        me, sibling, chips, c, copy, mine, first = self._ctx(srcs, outs, sems)
        for m in range(len(srcs)):
            copy(m, 0, sibling, me).wait_recv()
            for j, chip in enumerate(chips):
                copy(m, 4 + j, (*chip, 1 - c), me).wait_recv()
            for cp in first(m):
                cp.wait_send()
            for j, chip in enumerate(chips):
                copy(m, 4 + j, (*chip, c), sibling).wait_send()
            mine(m).wait()


RELATIONS = ((0, 0, 1), (1, 0, 0), (0, 1, 0), (1, 1, 0), (1, 0, 1), (0, 1, 1), (1, 1, 1))
SEM = pl.BlockSpec(memory_space=pltpu.SEMAPHORE)
SPLIT_PARAMS = dict(has_side_effects=pltpu.SideEffectType.DATAFLOW_SIDE_EFFECTING)


def _owner_copies(grad_ref, land_ref, send_sems, recv_sems):
    x, y, c = _coords()
    copies = []
    for s, (fx, fy, fc) in enumerate(RELATIONS):
        px = 1 - x if fx else x
        py = 1 - y if fy else y
        pc = 1 - c if fc else c
        copies.append(pltpu.make_async_remote_copy(
            src_ref=grad_ref.at[2 * px + py, pc], dst_ref=land_ref.at[s], send_sem=send_sems.at[s],
            recv_sem=recv_sems.at[s], device_id=(px, py, pc), device_id_type=MESH))
    return copies


def _send_to_owners(name, grad):
    n = len(RELATIONS)
    land_shape = (n,) + grad.shape[2:]

    def body(g_ref, land_ref, send_sems, recv_sems, g_thru, land_thru, token):
        for cp in _owner_copies(g_ref, land_ref, send_sems, recv_sems):
            cp.start()
        token[...] = jnp.zeros_like(token)

    return pl.pallas_call(
        body, name=name,
        out_shape=(pltpu.SemaphoreType.DMA((n,)), pltpu.SemaphoreType.DMA((n,)), pltpu.HBM(grad.shape, grad.dtype),
                   pltpu.HBM(land_shape, grad.dtype), jax.ShapeDtypeStruct((8, 128), F32)),
        in_specs=(HBM, HBM), out_specs=(SEM, SEM, HBM, HBM, pl.BlockSpec(memory_space=pltpu.VMEM)),
        input_output_aliases={0: 2, 1: 3}, compiler_params=pltpu.CompilerParams(**SPLIT_PARAMS),
    )(pltpu.with_memory_space_constraint(grad, pltpu.HBM),
      pltpu.with_memory_space_constraint(lax.empty(land_shape, grad.dtype), pltpu.HBM))


def _await_owners(name, started, after):
    send_sems, recv_sems, g_thru, land_thru, _ = started

    def body(g_ref, land_ref, send_sems, recv_sems, after_ref, g_out, land_out):
        for cp in _owner_copies(g_ref, land_ref, send_sems, recv_sems):
            cp.wait_send()
            cp.wait_recv()

    return pl.pallas_call(
        body, name=name, out_shape=(pltpu.HBM(g_thru.shape, g_thru.dtype), pltpu.HBM(land_thru.shape, land_thru.dtype)),
        in_specs=(HBM, HBM, SEM, SEM, pl.BlockSpec(memory_space=pl.ANY)), out_specs=(HBM, HBM),
        input_output_aliases={0: 0, 1: 1}, compiler_params=pltpu.CompilerParams(**SPLIT_PARAMS),
    )(g_thru, land_thru, send_sems, recv_sems, after)


def _call(name, main, grid, in_specs, out_specs, out_shape, args, scratch=(), comm=None, prefetch=None, after=None):
    comm = comm or _NoComm()
    n_main = len(in_specs)
    if after is not None:
        in_specs = list(in_specs) + [pl.BlockSpec(after.shape, lambda *_: (0,) * after.ndim)]
        args = tuple(args) + (after,)
    counts = [len(in_specs), len(comm.inputs), len(out_shape), len(comm.out_shapes), len(scratch), len(comm.scratch)]
    n_steps = math.prod(grid)
    hosted = bool(comm.inputs)

    def body(*refs):
        if prefetch is not None:
            refs = refs[1:]
        parts, at = [], 0
        for n in counts:
            parts.append(refs[at:at + n])
            at += n
        ins, c_in, outs, c_out, scr, c_scr = parts
        ins = ins[:n_main]
        step = pl.program_id(0)
        for d in range(1, len(grid)):
            step = step * grid[d] + pl.program_id(d)
        if hosted:
            @pl.when(step == 0)
            def _():
                comm.start(c_in, c_out, c_scr)
        main(ins, outs, scr)
        if hosted:
            @pl.when(step == max(n_steps - 2, 0))
            def _():
                comm.mid(c_in, c_out, c_scr)

            @pl.when(step == n_steps - 1)
            def _():
                comm.finish(c_in, c_out, c_scr)

    all_in = list(in_specs) + [HBM] * counts[1]
    all_out = list(out_specs) + [HBM] * counts[3]
    all_scratch = list(scratch) + list(comm.scratch)
    shapes = list(out_shape) + list(comm.out_shapes)
    if prefetch is None:
        res = pl.pallas_call(body, name=name, grid=grid, in_specs=all_in, out_specs=all_out, out_shape=shapes,
                             scratch_shapes=all_scratch, compiler_params=_cparams())(*args, *comm.inputs)
    else:
        res = pl.pallas_call(
            body, name=name, out_shape=shapes,
            grid_spec=pltpu.PrefetchScalarGridSpec(num_scalar_prefetch=1, grid=grid, in_specs=all_in,
                                                   out_specs=all_out, scratch_shapes=all_scratch),
            compiler_params=_cparams())(prefetch, *args, *comm.inputs)
    return res[:counts[2]], res[counts[2]:]


def _rms_fwd(name, x, g, comm=None):
    T, D = x.shape
    tm = min(T, 512)

    def main(ins, outs, scr):
        x_ref, g_ref = ins
        xv = x_ref[...]
        r = lax.rsqrt(jnp.mean(xv * xv, axis=-1, keepdims=True) + RMS_EPS)
        outs[0][...] = (xv * r * g_ref[...]).astype(outs[0].dtype)

    tile = pl.BlockSpec((tm, D), lambda i: (i, 0))
    (h,), extra = _call(name, main, (T // tm,), [tile, pl.BlockSpec((1, D), lambda i: (0, 0))], [tile],
                        [jax.ShapeDtypeStruct((T, D), BF16)], (x, g), comm=comm)
    return h, extra


def _final_loss_epilogue(scale, out_scale):
    def ep(acc, ex, outs):
        res_ref, g_ref, t_ref = ex
        dx_ref, dxb_ref, dg_ref, loss_ref = outs
        n = acc.shape[-1]
        xv = res_ref[...] + scale * acc
        r = lax.rsqrt(jnp.mean(xv * xv, axis=-1, keepdims=True) + RMS_EPS)
        xhat = xv * r
        err = xhat * g_ref[...] - t_ref[...]

        @pl.when(pl.program_id(0) == 0)
        def _():
            dg_ref[...] = jnp.zeros_like(dg_ref)
            loss_ref[...] = jnp.zeros_like(loss_ref)

        loss_ref[...] += jnp.broadcast_to(jnp.sum(err * err) * (0.5 / n), loss_ref.shape)
        dy = err * (1.0 / n)
        dg_ref[...] += jnp.sum(dy * xhat, axis=0, keepdims=True)
        dxhat = dy * g_ref[...]
        dx = r * (dxhat - xhat * jnp.mean(dxhat * xhat, axis=-1, keepdims=True))
        dx_ref[...] = dx
        dxb_ref[...] = (out_scale * dx).astype(dxb_ref.dtype)
    return ep


def _mm_nstream(name, a, ws, w_sel, w_form, comps, out_dtypes, epilogue, cn, rows=1024, comm=None, after=None):
    T, K = a.shape
    N = ws[0].shape[1]
    rows = min(rows, T)
    assert N % cn == 0 and T % rows == 0
    n_w, n_c = len(ws), len(comps)

    def main(ins, outs, scr):
        a_ref = ins[0]
        w_refs = ins[1:1 + n_w]
        c_refs = ins[1 + n_w:1 + n_w + n_c]

        for r in range(T // rows):
            sl = slice(r * rows, (r + 1) * rows)
            a_blk = a_ref[sl, :]
            dots = [_dot(a_blk, w_ref[...], w_form) for w_ref in w_refs]
            res = epilogue(dots, [c_ref[sl, :] for c_ref in c_refs])
            for o_ref, o in zip(outs, res):
                o_ref[sl, :] = o.astype(o_ref.dtype)

    if w_form == "nt":
        w_specs = [pl.BlockSpec((None, cn, K), functools.partial(lambda j, s: (s, j, 0), s=s)) for s in w_sel]
    else:
        w_specs = [pl.BlockSpec((K, cn), lambda j: (0, j)) for _ in ws]
    chunk = pl.BlockSpec((T, cn), lambda j: (0, j))
    return _call(name, main, (N // cn,), [pl.BlockSpec((T, K), lambda j: (0, 0))] + w_specs + [chunk] * n_c,
                 [chunk] * len(out_dtypes), [jax.ShapeDtypeStruct((T, N), dt) for dt in out_dtypes],
                 (a, *ws, *comps), comm=comm, after=after)


def _mm_mstream(name, as_, ws, w_sel, w_form, extras, outs_desc, epilogue, tm=512, comm=None, after=None):
    T = as_[0].shape[0]
    tm = min(tm, T)
    n_a = len(as_)
    w_shapes = [w.shape[-2:] for w in ws]
    N = w_shapes[0][1] if w_form == "nn" else w_shapes[0][0]

    def main(ins, outs, scr):
        a_refs = ins[:n_a]
        w_refs = ins[n_a:2 * n_a]
        acc = None
        for a_ref, w_ref in zip(a_refs, w_refs):
            d = _dot(a_ref[...], w_ref[...], w_form)
            acc = d if acc is None else acc + d
        epilogue(acc, ins[2 * n_a:], outs)

    kind_spec = {"tile": pl.BlockSpec((tm, N), lambda i: (i, 0)), "vec": pl.BlockSpec((1, N), lambda i: (0, 0))}
    kind_shape = {"tile": (T, N), "vec": (1, N)}
    a_specs = [pl.BlockSpec((tm, a.shape[1]), lambda i: (i, 0)) for a in as_]
    w_specs = []
    for w, s in zip(ws, w_sel):
        if w.ndim == 3:
            w_specs.append(pl.BlockSpec((None,) + tuple(w.shape[1:]), functools.partial(lambda i, s: (s, 0, 0), s=s),
                                        pipeline_mode=pl.Buffered(1)))
        else:
            w_specs.append(pl.BlockSpec(tuple(w.shape), lambda i: (0, 0), pipeline_mode=pl.Buffered(1)))
    args = list(as_) + list(ws) + [e for e, _ in extras]
    return _call(name, main, (T // tm,), a_specs + w_specs + [kind_spec[k] for _, k in extras],
                 [kind_spec[k] for _, k in outs_desc],
                 [jax.ShapeDtypeStruct(kind_shape[k], dt) for dt, k in outs_desc], args, comm=comm, after=after)


def _residual_rms_epilogue(scale):
    def ep(acc, ex, outs):
        xv = ex[0][...] + scale * acc
        outs[0][...] = xv
        r = lax.rsqrt(jnp.mean(xv * xv, axis=-1, keepdims=True) + RMS_EPS)
        outs[1][...] = (xv * r * ex[1][...]).astype(outs[1].dtype)
    return ep


def _rms_bwd_epilogue(out_scale):
    def ep(acc, ex, outs):
        x_ref, g_ref, dres_ref = ex
        dx_ref, dxb_ref, dg_ref = outs
        xv = x_ref[...]
        r = lax.rsqrt(jnp.mean(xv * xv, axis=-1, keepdims=True) + RMS_EPS)
        xhat = xv * r

        @pl.when(pl.program_id(0) == 0)
        def _():
            dg_ref[...] = jnp.zeros_like(dg_ref)

        dg_ref[...] += jnp.sum(acc * xhat, axis=0, keepdims=True)
        dxhat = acc * g_ref[...]
        dx = r * (dxhat - xhat * jnp.mean(dxhat * xhat, axis=-1, keepdims=True)) + dres_ref[...]
        dx_ref[...] = dx
        dxb_ref[...] = (out_scale * dx).astype(dxb_ref.dtype)
    return ep


def _mm_tn(name, a, b, tmo, tno, out_dtype, tk=1024, comm=None, after=None):
    T, Ma = a.shape
    Nb = b.shape[1]
    tk = min(tk, T)
    nk = T // tk

    def main(ins, outs, scr):
        a_ref, b_ref = ins
        (acc_ref,) = scr
        k = pl.program_id(2)

        @pl.when(k == 0)
        def _():
            acc_ref[...] = jnp.zeros_like(acc_ref)

        acc_ref[...] += _dot(a_ref[...], b_ref[...], "tn")

        @pl.when(k == nk - 1)
        def _():
            outs[0][...] = acc_ref[...].astype(outs[0].dtype)

    (out,), extra = _call(
        name, main, (Ma // tmo, Nb // tno, nk),
        [pl.BlockSpec((tk, tmo), lambda i, j, k: (k, i)), pl.BlockSpec((tk, tno), lambda i, j, k: (k, j))],
        [pl.BlockSpec((tmo, tno), lambda i, j, k: (i, j))], [jax.ShapeDtypeStruct((Ma, Nb), out_dtype)],
        (a, b), scratch=[pltpu.VMEM((tmo, tno), F32)], comm=comm, after=after)
    return out, extra


def _swiglu_parts(g, u):
    s = _sigmoid(g)
    silu = g * s
    return [u * (s + silu * (1.0 - s)), silu, silu * u]


def _silu_mul_epilogue(dots, comps):
    g, u = dots
    return _swiglu_parts(g, u)


def _gate_parts_epilogue(dots, comps):
    (g,) = dots
    s = _sigmoid(g)
    silu = g * s
    return [s + silu * (1.0 - s), silu]


def _up_act_epilogue(dots, comps):
    (u,) = dots
    return [u * comps[0].astype(F32), u * comps[1].astype(F32)]


def _dact_epilogue(dots, comps):
    dact = dots[0].astype(BF16)
    return [dact * comps[0], dact * comps[1]]


def _identity_epilogue(dots, comps):
    return list(dots)


def _swap_halves(x):
    lane = lax.broadcasted_iota(jnp.int32, x.shape, 1)
    first = (lane % DK) < (DK // 2)
    return jnp.where(first, pltpu.roll(x, 128 - DK // 2, 1), pltpu.roll(x, DK // 2, 1))


def _rotary(t, cos, sin_signed):
    halves = []
    for p in range(QK_W // 128):
        th = t[:, 128 * p:128 * (p + 1)]
        halves.append(th * cos + _swap_halves(th) * sin_signed)
    return jnp.concatenate(halves, axis=1)


def _rotary_transposed(d, cos, sin_signed):
    halves = []
    for p in range(QK_W // 128):
        dh = d[:, 128 * p:128 * (p + 1)]
        halves.append(dh * cos + _swap_halves(dh * sin_signed))
    return jnp.concatenate(halves, axis=1)


def _log_sigmoid(x):
    return jnp.minimum(x, 0.0) - jnp.log(1.0 + jnp.exp(-jnp.abs(x)))


def _tri_sum(mask, x):
    tri = mask.astype(BF16)
    hi = x.astype(BF16)
    rest = x - hi.astype(F32)
    mid = rest.astype(BF16)
    lo = (rest - mid.astype(F32)).astype(BF16)
    return _dot(tri, hi, "nn") + _dot(tri, mid, "nn") + _dot(tri, lo, "nn")


def _attn_masks():
    row = lax.broadcasted_iota(jnp.int32, (SUPER, SUPER), 0)
    col = lax.broadcasted_iota(jnp.int32, (SUPER, SUPER), 1)
    same = (row // CHUNK) == (col // CHUNK)
    return row, col, same


def _group_inputs(grp, pr, cos, sin_signed, lg, wa2, ba):
    seg = lambda lo, width: pr[:, lo:lo + width].astype(F32)
    if grp == 0:
        q = _rotary(seg(C_RQ, QK_W), cos, sin_signed)
        k = _rotary(seg(C_RK, QK_W), cos, sin_signed) * (DK ** -0.5)
        v = pr[:, C_RV:C_RV + V_W]
        gate = seg(C_RG, V_W)
        pos = lax.broadcasted_iota(jnp.int32, (SUPER, QK_W), 0).astype(F32) + 1.0
        return q, k, v, gate, pos * lg, None, None
    q = seg(C_GQ, QK_W) * (DK ** -0.5)
    k = seg(C_GK, QK_W)
    v = pr[:, C_GV:C_GV + V_W]
    gate = seg(C_GG, V_W)
    glow = pr[:, C_GL:C_GL + GL_W]
    logit = _dot(glow.astype(BF16), wa2.astype(BF16), "nn") + ba
    la = _log_sigmoid(logit) * (1.0 / GATE_NORM)
    row, col, _ = _attn_masks()
    b_cum = _tri_sum(col <= row, la)
    return q, k, v, gate, b_cum, glow, logit


def _decay_factors(q, k, b_cum):
    c = b_cum[SUPER // 2 - 1:SUPER // 2, :]
    bl = b_cum[SUPER - 1:SUPER, :]
    e1 = jnp.exp(b_cum - c)
    e2 = jnp.exp(c - b_cum)
    e_b = jnp.exp(b_cum)
    e_l = jnp.exp(bl - b_cum)
    return dict(e1=e1, e2=e2, eb=e_b, el=e_l, ebl=jnp.exp(bl),
                qp=q * e1, qm=q * e2, kp=k * e1, km=k * e2, qs=q * e_b, kl=k * e_l)


def _state_block_mask():
    r = lax.broadcasted_iota(jnp.int32, (V_W, QK_W), 0)
    c = lax.broadcasted_iota(jnp.int32, (V_W, QK_W), 1)
    return (r // DV) == (c // DK)


def _attn_fwd(proj, cos, sin_signed, lg, wa2p, ba, gn_ret, gn_gla, x_res, w_out, g_next, comm=None):
    T = proj.shape[0]
    n_s = T // SUPER
    D = x_res.shape[1]

    def main(ins, outs, scr):
        pr_ref, cos_ref, sin_ref, lg_ref, wa2_ref, ba_ref, gr_ref, gg_ref, xres_ref, wout_ref, gnext_ref = ins
        o_ref, y_ref, st_ref, x_ref, h_ref = outs
        (s_ref,) = scr
        i = pl.program_id(0)

        @pl.when(i == 0)
        def _():
            s_ref[...] = jnp.zeros_like(s_ref)

        pr = pr_ref
        row, col, same = _attn_masks()
        m1 = col <= row
        m2 = jnp.logical_and(col > row, same)
        lane = lax.broadcasted_iota(jnp.int32, (1, QK_W), 1)
        blockmask = _state_block_mask()
        for grp in range(2):
            q, k, v, gate, b_cum, _, _ = _group_inputs(grp, pr, cos_ref[...], sin_ref[...], lg_ref[...],
                                                      wa2_ref[...], ba_ref[...])
            f = _decay_factors(q, k, b_cum)
            gn = gr_ref[...] if grp == 0 else gg_ref[...]
            s_prev = s_ref[grp]
            st_ref[0, grp] = s_prev
            o_inter = _dot(f["qs"].astype(BF16), s_prev.astype(BF16), "nt")
            kmb = f["km"].astype(BF16)
            kpb = f["kp"].astype(BF16)
            vb = v.astype(BF16)
            for h in range(HEADS):
                hm = (lane // DK) == h
                a1 = _dot(jnp.where(hm, f["qp"], 0.0).astype(BF16), kmb, "nt")
                a2 = _dot(jnp.where(hm, f["qm"], 0.0).astype(BF16), kpb, "nt")
                a = jnp.where(m1, a1, jnp.where(m2, a2, 0.0))
                lo = grp * V_W + h * DV
                o_h = _dot(a.astype(BF16), vb[:, h * DV:(h + 1) * DV], "nn") + o_inter[:, h * DV:(h + 1) * DV]
                o_ref[:, lo:lo + DV] = o_h
                r = lax.rsqrt(jnp.mean(o_h * o_h, axis=-1, keepdims=True) + RMS_EPS)
                gte = gate[:, h * DV:(h + 1) * DV]
                y = o_h * r * gn[:, h * DV:(h + 1) * DV] * (gte * _sigmoid(gte))
                y_ref[:, lo:lo + DV] = y.astype(y_ref.dtype)
            upd = _dot(vb, f["kl"].astype(BF16), "tn")
            s_ref[grp] = s_prev * f["ebl"] + jnp.where(blockmask, upd, 0.0)
        xv = xres_ref[...] + _dot(y_ref[...], wout_ref[...], "nn")
        x_ref[...] = xv
        r = lax.rsqrt(jnp.mean(xv * xv, axis=-1, keepdims=True) + RMS_EPS)
        h_ref[...] = (xv * r * gnext_ref[...]).astype(h_ref.dtype)

    const = lambda shape: pl.BlockSpec(shape, lambda i: tuple(0 for _ in shape))
    rows = lambda w: pl.BlockSpec((SUPER, w), lambda i: (i, 0))
    return _call(
        "attn_fwd", main, (n_s,),
        [rows(PROJ_W), rows(128), rows(128),
         const((1, QK_W)), const((GL_W, QK_W)), const((1, QK_W)), const((1, V_W)), const((1, V_W)),
         rows(D), const((2 * V_W, D)), const((1, D))],
        [rows(2 * V_W), rows(2 * V_W), pl.BlockSpec((1, 2, V_W, QK_W), lambda i: (i, 0, 0, 0)), rows(D), rows(D)],
        [jax.ShapeDtypeStruct((T, 2 * V_W), F32), jax.ShapeDtypeStruct((T, 2 * V_W), BF16),
         jax.ShapeDtypeStruct((n_s, 2, V_W, QK_W), F32), jax.ShapeDtypeStruct((T, D), F32),
         jax.ShapeDtypeStruct((T, D), BF16)],
        (proj, cos, sin_signed, lg, wa2p, ba, gn_ret, gn_gla, x_res, w_out, g_next),
        scratch=[pltpu.VMEM((2, V_W, QK_W), F32)], comm=comm)


def _attn_bwd(proj, cos, sin_signed, lg, wa2p, ba, gn_ret, gn_gla, o, dx, w_out, states, comm=None, after=None):
    T = proj.shape[0]
    n_s = T // SUPER
    D = dx.shape[1]

    def main(ins, outs, scr):
        pr_ref, cos_ref, sin_ref, lg_ref, wa2_ref, ba_ref, gr_ref, gg_ref, o_ref, dx_ref, wout_ref, st_ref = ins
        dp_ref, dgr_ref, dgg_ref, dba_ref, dwa_ref = outs
        (ds_ref, dy_ref) = scr
        i = pl.program_id(0)
        dy_ref[...] = _dot(dx_ref[...], wout_ref[...], "nt")

        @pl.when(i == 0)
        def _():
            ds_ref[...] = jnp.zeros_like(ds_ref)
            dgr_ref[...] = jnp.zeros_like(dgr_ref)
            dgg_ref[...] = jnp.zeros_like(dgg_ref)
            dba_ref[...] = jnp.zeros_like(dba_ref)
            dwa_ref[...] = jnp.zeros_like(dwa_ref)

        pr = pr_ref
        cos = cos_ref[...]
        sin_signed = sin_ref[...]
        row, col, same = _attn_masks()
        m1 = col <= row
        m2 = jnp.logical_and(col > row, same)
        m1t = row <= col
        m2t = jnp.logical_and(row > col, same)
        lane = lax.broadcasted_iota(jnp.int32, (1, QK_W), 1)
        blockmask = _state_block_mask()
        for grp in range(2):
            q, k, v, gate, b_cum, glow, logit = _group_inputs(grp, pr, cos, sin_signed, lg_ref[...],
                                                              wa2_ref[...], ba_ref[...])
            f = _decay_factors(q, k, b_cum)
            gn = gr_ref[...] if grp == 0 else gg_ref[...]
            dgn_ref = dgr_ref if grp == 0 else dgg_ref
            do_parts, dgate_parts, dgn_parts = [], [], []
            for h in range(HEADS):
                lo = grp * V_W + h * DV
                o_h = o_ref[:, lo:lo + DV]
                r = lax.rsqrt(jnp.mean(o_h * o_h, axis=-1, keepdims=True) + RMS_EPS)
                n = o_h * r
                gte = gate[:, h * DV:(h + 1) * DV]
                sg = _sigmoid(gte)
                dy_h = dy_ref[:, lo:lo + DV]
                gn_h = gn[:, h * DV:(h + 1) * DV]
                dgate_parts.append(dy_h * n * gn_h * (sg * (1.0 + gte * (1.0 - sg))))
                dz = dy_h * (gte * sg)
                dgn_parts.append(jnp.sum(dz * n, axis=0, keepdims=True))
                dn = dz * gn_h
                do_parts.append(r * (dn - n * jnp.mean(dn * n, axis=-1, keepdims=True)))
            dgn_ref[...] += jnp.concatenate(dgn_parts, axis=1)
            dgate = jnp.concatenate(dgate_parts, axis=1)
            do = jnp.concatenate(do_parts, axis=1)
            dob = do.astype(BF16)
            vb = v.astype(BF16)
            s_prev = st_ref[0, grp]
            ds_new = ds_ref[grp]
            dsb = ds_new.astype(BF16)
            qpb, qmb = f["qp"].astype(BF16), f["qm"].astype(BF16)
            kpb, kmb = f["kp"].astype(BF16), f["km"].astype(BF16)
            dqp = jnp.zeros((SUPER, QK_W), F32)
            dqm = jnp.zeros((SUPER, QK_W), F32)
            dkp = jnp.zeros((SUPER, QK_W), F32)
            dkm = jnp.zeros((SUPER, QK_W), F32)
            dv_parts = []
            for h in range(HEADS):
                hm = (lane // DK) == h
                qp_h = jnp.where(hm, f["qp"], 0.0).astype(BF16)
                qm_h = jnp.where(hm, f["qm"], 0.0).astype(BF16)
                kp_h = jnp.where(hm, f["kp"], 0.0).astype(BF16)
                km_h = jnp.where(hm, f["km"], 0.0).astype(BF16)
                at = jnp.where(m1t, _dot(km_h, qpb, "nt"), jnp.where(m2t, _dot(kp_h, qmb, "nt"), 0.0))
                do_h = dob[:, h * DV:(h + 1) * DV]
                v_h = vb[:, h * DV:(h + 1) * DV]
                dv_parts.append(_dot(at.astype(BF16), do_h, "nn"))
                da = _dot(do_h, v_h, "nt")
                dat = _dot(v_h, do_h, "nt")
                da1 = jnp.where(m1, da, 0.0).astype(BF16)
                da2 = jnp.where(m2, da, 0.0).astype(BF16)
                da1t = jnp.where(m1t, dat, 0.0).astype(BF16)
                da2t = jnp.where(m2t, dat, 0.0).astype(BF16)
                dqp = dqp + _dot(da1, km_h, "nn")
                dqm = dqm + _dot(da2, kp_h, "nn")
                dkm = dkm + _dot(da1t, qp_h, "nn")
                dkp = dkp + _dot(da2t, qm_h, "nn")
            klb = f["kl"].astype(BF16)
            qsb = f["qs"].astype(BF16)
            dqs = _dot(dob, s_prev.astype(BF16), "nn")
            dkl = _dot(vb, dsb, "nn")
            dv = jnp.concatenate(dv_parts, axis=1) + _dot(klb, dsb, "nt")
            ds_ref[grp] = ds_new * f["ebl"] + jnp.where(blockmask, _dot(dob, qsb, "tn"), 0.0)
            dq = dqp * f["e1"] + dqm * f["e2"] + dqs * f["eb"]
            dk = dkm * f["e2"] + dkp * f["e1"] + dkl * f["el"]
            if grp == 0:
                dq = _rotary_transposed(dq, cos, sin_signed)
                dk = _rotary_transposed(dk * (DK ** -0.5), cos, sin_signed)
                dp_ref[:, C_RQ:C_RQ + QK_W] = dq.astype(dp_ref.dtype)
                dp_ref[:, C_RK:C_RK + QK_W] = dk.astype(dp_ref.dtype)
                dp_ref[:, C_RV:C_RV + V_W] = dv.astype(dp_ref.dtype)
                dp_ref[:, C_RG:C_RG + V_W] = dgate.astype(dp_ref.dtype)
            else:
                dkl_kl = dkl * klb.astype(F32)
                db = (dqp * qpb.astype(F32) - dkm * kmb.astype(F32) - dqm * qmb.astype(F32)
                      + dkp * kpb.astype(F32) + dqs * qsb.astype(F32) - dkl_kl)
                last = (jnp.sum(dkl_kl, axis=0, keepdims=True)
                        + f["ebl"] * jnp.sum(s_prev * ds_new, axis=0, keepdims=True))
                rowq = lax.broadcasted_iota(jnp.int32, (SUPER, QK_W), 0)
                db = db + jnp.where(rowq == SUPER - 1, last, 0.0)
                dla = _tri_sum(col >= row, db)
                dlogit = dla * (1.0 / GATE_NORM) * (1.0 - _sigmoid(logit))
                dlb = dlogit.astype(BF16)
                dglow = _dot(dlb, wa2_ref[...].astype(BF16), "nt")
                dwa_ref[...] += _dot(glow.astype(BF16), dlb, "tn")
                dba_ref[...] += jnp.sum(dlogit, axis=0, keepdims=True)
                dp_ref[:, C_GQ:C_GQ + QK_W] = (dq * (DK ** -0.5)).astype(dp_ref.dtype)
                dp_ref[:, C_GK:C_GK + QK_W] = dk.astype(dp_ref.dtype)
                dp_ref[:, C_GV:C_GV + V_W] = dv.astype(dp_ref.dtype)
                dp_ref[:, C_GG:C_GG + V_W] = dgate.astype(dp_ref.dtype)
                dp_ref[:, C_GL:C_GL + GL_W] = dglow.astype(dp_ref.dtype)

    rev = lambda i: n_s - 1 - i
    const = lambda shape: pl.BlockSpec(shape, lambda i: tuple(0 for _ in shape))
    return _call(
        "attn_bwd", main, (n_s,),
        [pl.BlockSpec((SUPER, PROJ_W), lambda i: (rev(i), 0)),
         pl.BlockSpec((SUPER, 128), lambda i: (rev(i), 0)), pl.BlockSpec((SUPER, 128), lambda i: (rev(i), 0)),
         const((1, QK_W)), const((GL_W, QK_W)), const((1, QK_W)), const((1, V_W)), const((1, V_W)),
         pl.BlockSpec((SUPER, 2 * V_W), lambda i: (rev(i), 0)),
         pl.BlockSpec((SUPER, D), lambda i: (rev(i), 0)), const((2 * V_W, D)),
         pl.BlockSpec((1, 2, V_W, QK_W), lambda i: (rev(i), 0, 0, 0))],
        [pl.BlockSpec((SUPER, PROJ_W), lambda i: (rev(i), 0)),
         const((1, V_W)), const((1, V_W)), const((1, QK_W)), const((GL_W, QK_W))],
        [jax.ShapeDtypeStruct((T, PROJ_W), BF16),
         jax.ShapeDtypeStruct((1, V_W), F32), jax.ShapeDtypeStruct((1, V_W), F32),
         jax.ShapeDtypeStruct((1, QK_W), F32), jax.ShapeDtypeStruct((GL_W, QK_W), F32)],
        (proj, cos, sin_signed, lg, wa2p, ba, gn_ret, gn_gla, o, dx, w_out, states),
        scratch=[pltpu.VMEM((2, V_W, QK_W), F32), pltpu.VMEM((SUPER, 2 * V_W), F32)], comm=comm, after=after)


def _rotary_tables(T):
    half = DK // 2
    inv = ROPE_BASE ** (-jnp.arange(half, dtype=F32) * 2.0 / DK)
    ang = jnp.arange(T, dtype=F32)[:, None] * inv[None, :]
    cos, sin = jnp.cos(ang), jnp.sin(ang)
    cos_head = jnp.concatenate([cos, cos], axis=1)
    sin_head = jnp.concatenate([-sin, sin], axis=1)
    return jnp.tile(cos_head, (1, 128 // DK)), jnp.tile(sin_head, (1, 128 // DK))


def _sum_devices(name, gathered, m_per):
    def body(g_ref, o_ref):
        acc = g_ref[0:m_per, :]
        for k in range(1, N_DEV):
            acc = acc + g_ref[k * m_per:(k + 1) * m_per, :]
        o_ref[...] = acc

    return pl.pallas_call(body, name=name, out_shape=jax.ShapeDtypeStruct((m_per, 128), F32))(gathered)


def _owner_sums(name, items, owner, comm=None):
    counts = [1 + len(landed) for _, landed in items]

    def main(ins, outs, scr):
        at = 0
        for o_ref, n in zip(outs, counts):
            acc = ins[at][...].astype(F32)
            for l_ref in ins[at + 1:at + n]:
                for j in range(l_ref.shape[0]):
                    acc = acc + l_ref[j].astype(F32)
            o_ref[...] = acc
            at += n

    once = pl.Buffered(1)
    in_specs, out_specs, out_shape, args = [], [], [], []
    for grad, landed in items:
        R, C = grad.shape[-2:]
        in_specs.append(pl.BlockSpec((None, None, R, C), lambda i, s: (s[0], s[1], 0, 0), pipeline_mode=once))
        in_specs += [pl.BlockSpec(tuple(l.shape), lambda i, s: (0, 0, 0), pipeline_mode=once) for l in landed]
        out_specs.append(pl.BlockSpec((R, C), lambda i, s: (0, 0)))
        out_shape.append(jax.ShapeDtypeStruct((R, C), F32))
        args += [grad, *landed]
    return _call(name, main, (1,), in_specs, out_specs, out_shape, args, comm=comm, prefetch=owner)


def _adamw_group(name, items, n_blocks, comm=None):
    n = len(items)

    def main(ins, outs, scr):
        for p in range(n):
            g_ref, w_ref, m_ref, v_ref = ins[4 * p:4 * p + 4]
            d_ref, nm_ref, nv_ref = outs[3 * p:3 * p + 3]
            gv = g_ref[...]
            nm = ADAM_B1 * m_ref[...] + (1.0 - ADAM_B1) * gv
            nv = ADAM_B2 * v_ref[...] + (1.0 - ADAM_B2) * (gv * gv)
            m_hat = nm / (1.0 - ADAM_B1 ** ADAM_STEP)
            v_hat = nv / (1.0 - ADAM_B2 ** ADAM_STEP)
            d_ref[...] = -ADAM_LR * (m_hat / (jnp.sqrt(v_hat) + ADAM_EPS) + ADAM_WD * w_ref[...])
            nm_ref[...] = nm
            nv_ref[...] = nv

    in_specs, out_specs, out_shape, args = [], [], [], []
    for item in items:
        R, C = item[1].shape
        assert R % n_blocks == 0
        spec = pl.BlockSpec((R // n_blocks, C), lambda i: (i, 0))
        in_specs += [spec] * 4
        out_specs += [spec] * 3
        out_shape += [jax.ShapeDtypeStruct((R, C), F32)] * 3
        args += list(item)
    outs, extra = _call(name, main, (n_blocks,), in_specs, out_specs, out_shape, args, comm=comm)
    return [tuple(outs[3 * p:3 * p + 3]) for p in range(n)], extra


SMALL_ORDER = ("ffn1", "mix", "ffn2", "final", "ret", "gla", "b_a")


def kernel(x, ffn1_norm_g, ffn1_w_gate, ffn1_w_up, ffn1_w_down, mix_norm_g, w_in, ret_norm_g, gla_w_a2, gla_b_a, gla_norm_g, w_out, ffn2_norm_g, ffn2_w_gate, ffn2_w_up, ffn2_w_down, final_norm_g, loss_target, m_ffn1_norm_g, m_ffn1_w_gate, m_ffn1_w_up, m_ffn1_w_down, m_mix_norm_g, m_w_in, m_ret_norm_g, m_gla_w_a2, m_gla_b_a, m_gla_norm_g, m_w_out, m_ffn2_norm_g, m_ffn2_w_gate, m_ffn2_w_up, m_ffn2_w_down, m_final_norm_g, v_ffn1_norm_g, v_ffn1_w_gate, v_ffn1_w_up, v_ffn1_w_down, v_mix_norm_g, v_w_in, v_ret_norm_g, v_gla_w_a2, v_gla_b_a, v_gla_norm_g, v_w_out, v_ffn2_norm_g, v_ffn2_w_gate, v_ffn2_w_up, v_ffn2_w_down, v_final_norm_g):
    xi, yi, ci = _coords()
    dev = 4 * xi + 2 * yi + ci
    owner = jnp.stack([2 * xi + yi, ci]).astype(jnp.int32)

    x0, target = x[0], loss_target[0]
    T, D = x0.shape
    fb = ffn1_w_gate.shape[2]
    ib = w_in.shape[2]
    ab = gla_w_a2.shape[2]
    F = N_DEV * fb
    cos, sin_signed = _rotary_tables(T)
    lg = jnp.repeat(jnp.log(1.0 - 2.0 ** (-5.0 - jnp.arange(HEADS, dtype=F32))), DK)[None, :]
    g_final = final_norm_g.reshape(1, D)

    g1_loc = ffn1_w_gate[0].T[None].astype(BF16)
    u1_loc = ffn1_w_up[0].T[None].astype(BF16)
    d1_loc = ffn1_w_down.astype(BF16)
    g2_loc = ffn2_w_gate[0].T[None].astype(BF16)
    u2_loc = ffn2_w_up[0].T[None].astype(BF16)
    d2_loc = ffn2_w_down.astype(BF16)
    in_loc = w_in[0].T.astype(BF16)
    out_loc = w_out[0].astype(BF16)

    h1, (g1,) = _rms_fwd("ffn1_rms", x0, ffn1_norm_g, comm=_AllGather([g1_loc], ["stack"]))
    g1 = g1.reshape(1, F, D)
    (dsl1, sl1), (u1,) = _mm_nstream("ffn1_gate", h1, [g1], [0], "nt", [], [BF16, BF16], _gate_parts_epilogue, cn=256,
                                     comm=_AllGather([u1_loc], ["stack"]))
    u1 = u1.reshape(1, F, D)
    (dsu1, act1), (d1,) = _mm_nstream("ffn1_up", h1, [u1], [0], "nt", [dsl1, sl1], [BF16, BF16],
                                      _up_act_epilogue, cn=256, comm=_AllGather([d1_loc], ["stack"]))
    d1 = d1.reshape(1, F, D)
    f32_tile, bf16_tile, f32_vec = (F32, "tile"), (BF16, "tile"), (F32, "vec")
    (x1, h2), (in_all, a_all) = _mm_mstream(
        "ffn1_down", [act1], [d1], [0], "nn", [(x0, "tile"), (mix_norm_g, "vec")], [f32_tile, bf16_tile],
        _residual_rms_epilogue(0.5), comm=_AllGather([in_loc, gla_w_a2[0]], ["plain", "plain"]))
    w_in_t = jnp.pad(in_all.reshape(1, N_DEV * ib, D), ((0, 0), (0, PROJ_W - N_DEV * ib), (0, 0)))
    wa2 = jnp.transpose(a_all, (1, 0, 2)).reshape(GATE_RANK, N_DEV * ab)
    wa2p = jnp.pad(wa2, ((0, GL_W - GATE_RANK), (0, 0)))

    (proj,), (g2, out_all) = _mm_nstream("mix_proj", h2, [w_in_t], [0], "nt", [], [BF16], _identity_epilogue, cn=640,
                                         comm=_AllGather([g2_loc, out_loc], ["stack", "plain"]))
    w_out_full = out_all.reshape(D, D)
    (o, ymix, states, x2, h3), (u2,) = _attn_fwd(proj, cos, sin_signed, lg, wa2p, gla_b_a, ret_norm_g, gla_norm_g,
                                                 x1, w_out_full, ffn2_norm_g, comm=_AllGather([u2_loc], ["stack"]))
    g2, u2 = g2.reshape(1, F, D), u2.reshape(1, F, D)

    (dsu2, sl2, act2), (d2,) = _mm_nstream(
        "ffn2_up", h3, [g2, u2], [0, 0], "nt", [], [BF16, BF16, BF16], _silu_mul_epilogue, cn=256,
        comm=_AllGather([d2_loc], ["stack"]))
    d2 = d2.reshape(1, F, D)
    (dx3, dy3b, d_final, loss), _ = _mm_mstream(
        "ffn2_down", [act2], [d2], [0], "nn", [(x2, "tile"), (g_final, "vec"), (target, "tile")],
        [f32_tile, bf16_tile, f32_vec, f32_vec], _final_loss_epilogue(0.5, 0.5))

    sent = {}

    def send(nm, grad):
        sent[nm] = _send_to_owners("send_" + nm, grad)
        return sent[nm][4]

    dwd2, _ = _mm_tn("ffn2b_dwd", act2, dy3b, F // 2, D, BF16)
    tok = send("wd2", dwd2.reshape(4, 2, fb, D))
    (dgate2, dup2), _ = _mm_nstream("ffn2b_dact", dy3b, [d2], [0], "nt", [dsu2, sl2], [BF16, BF16],
                                    _dact_epilogue, cn=256, after=tok)
    dwg2, _ = _mm_tn("ffn2b_dwg", dgate2, h3, F // 2, D, BF16)
    tok = send("wg2", dwg2.reshape(4, 2, fb, D))
    dwu2, _ = _mm_tn("ffn2b_dwu", dup2, h3, F // 2, D, BF16, after=tok)
    tok = send("wu2", dwu2.reshape(4, 2, fb, D))
    rms_outs = [f32_tile, bf16_tile, f32_vec]
    (dx2, dx2b, d_g2), _ = _mm_mstream(
        "ffn2b_dh", [dgate2, dup2], [g2, u2], [0, 0], "nn", [(x2, "tile"), (ffn2_norm_g, "vec"), (dx3, "tile")],
        rms_outs, _rms_bwd_epilogue(1.0), after=tok)

    dwout, _ = _mm_tn("mixb_dwout", ymix, dx2b, D, D, BF16)
    tok = send("wout", dwout.reshape(4, 2, D // N_DEV, D))
    (dproj, d_ret, d_gla, d_ba, d_wa2p), _ = _attn_bwd(
        proj, cos, sin_signed, lg, wa2p, gla_b_a, ret_norm_g, gla_norm_g, o, dx2b, w_out_full, states, after=tok)
    dwin_t, _ = _mm_tn("mixb_dwin", dproj, h2, 640, D, BF16, tk=2048)
    tok = send("win", dwin_t[:N_DEV * ib].reshape(4, 2, ib, D))
    (dx1, dy1b, d_gmix), _ = _mm_mstream(
        "mixb_dh", [dproj], [w_in_t], [0], "nn", [(x1, "tile"), (mix_norm_g, "vec"), (dx2, "tile")],
        rms_outs, _rms_bwd_epilogue(0.5), after=tok)

    dwd1, _ = _mm_tn("ffn1b_dwd", act1, dy1b, F // 2, D, BF16)
    tok = send("wd1", dwd1.reshape(4, 2, fb, D))
    (dgate1, dup1), _ = _mm_nstream("ffn1b_dact", dy1b, [d1], [0], "nt", [dsu1, sl1], [BF16, BF16],
                                    _dact_epilogue, cn=256, after=tok)
    dwg1, _ = _mm_tn("ffn1b_dwg", dgate1, h1, F // 2, D, BF16)
    tok = send("wg1", dwg1.reshape(4, 2, fb, D))
    dwu1, _ = _mm_tn("ffn1b_dwu", dup1, h1, F // 2, D, BF16, after=tok)
    tok = send("wu1", dwu1.reshape(4, 2, fb, D))
    (dx0, _, d_g1), _ = _mm_mstream(
        "ffn1b_dh", [dgate1, dup1], [g1, u1], [0, 0], "nn", [(x0, "tile"), (ffn1_norm_g, "vec"), (dx1, "tile")],
        rms_outs, _rms_bwd_epilogue(1.0), after=tok)

    small = dict(ffn1=d_g1, mix=d_gmix, ffn2=d_g2, final=d_final, ret=d_ret, gla=d_gla, b_a=d_ba)
    flat = jnp.concatenate([small[k].reshape(-1) for k in SMALL_ORDER]
                           + [d_wa2p[:GATE_RANK].reshape(-1), loss[0, :128]])
    rows = -(-flat.shape[0] // 128)
    rows = -(-rows // 8) * 8
    packed = jnp.pad(flat, (0, rows * 128 - flat.shape[0])).reshape(rows, 128)

    transposed = ("ffn1_w_gate", "ffn1_w_up", "ffn2_w_gate", "ffn2_w_up", "w_in")

    def to_2d(nm, a):
        if nm in transposed:
            return a[0].T
        return a.reshape((1, a.shape[0]) if a.ndim == 1 else a.shape[-2:])

    def from_2d(nm, a):
        return a.T[None] if nm in transposed else a.reshape(params[nm][0].shape)

    def arrived(nm, after):
        grad, landed = _await_owners("await_" + nm, sent[nm], after)
        return grad, [landed]

    sums_a, (gathered,) = _owner_sums(
        "sum_a", [arrived(nm, dx0) for nm in ("wg2", "wu2", "wd2", "win", "wout")], owner,
        comm=_AllGather([packed], ["plain"]))
    params = dict(
        ffn2_w_gate=(ffn2_w_gate, m_ffn2_w_gate, v_ffn2_w_gate), ffn2_w_up=(ffn2_w_up, m_ffn2_w_up, v_ffn2_w_up),
        ffn2_w_down=(ffn2_w_down, m_ffn2_w_down, v_ffn2_w_down), w_in=(w_in, m_w_in, v_w_in),
        w_out=(w_out, m_w_out, v_w_out), ffn1_w_gate=(ffn1_w_gate, m_ffn1_w_gate, v_ffn1_w_gate),
        ffn1_w_up=(ffn1_w_up, m_ffn1_w_up, v_ffn1_w_up), ffn1_w_down=(ffn1_w_down, m_ffn1_w_down, v_ffn1_w_down),
        ffn1_norm_g=(ffn1_norm_g, m_ffn1_norm_g, v_ffn1_norm_g), mix_norm_g=(mix_norm_g, m_mix_norm_g, v_mix_norm_g),
        ret_norm_g=(ret_norm_g, m_ret_norm_g, v_ret_norm_g), gla_w_a2=(gla_w_a2, m_gla_w_a2, v_gla_w_a2),
        gla_b_a=(gla_b_a, m_gla_b_a, v_gla_b_a), gla_norm_g=(gla_norm_g, m_gla_norm_g, v_gla_norm_g),
        ffn2_norm_g=(ffn2_norm_g, m_ffn2_norm_g, v_ffn2_norm_g), final_norm_g=(final_norm_g, m_final_norm_g, v_final_norm_g))
    grads, updates = {}, {}

    def run_adam(name, names, grad_2d, n_blocks):
        items = [(grad_2d[nm],) + tuple(to_2d(nm, a) for a in params[nm]) for nm in names]
        res, _ = _adamw_group(name, items, n_blocks)
        for nm, r in zip(names, res):
            grads[nm] = from_2d(nm, grad_2d[nm])
            updates[nm] = tuple(from_2d(nm, a) for a in r)
        return res

    grads_a = {"ffn2_w_gate": sums_a[0], "ffn2_w_up": sums_a[1], "ffn2_w_down": sums_a[2], "w_out": sums_a[4]}
    run_adam("adamw_w_in", ["w_in"], {"w_in": sums_a[3]}, 1)
    done_a = run_adam("adamw_a", list(grads_a), grads_a, 4)[0][0]
    sums_b, _ = _owner_sums("sum_b", [arrived(nm, done_a) for nm in ("wg1", "wu1", "wd1")], owner)
    grads_b = {"ffn1_w_gate": sums_b[0], "ffn1_w_up": sums_b[1], "ffn1_w_down": sums_b[2]}
    run_adam("adamw_b", list(grads_b), grads_b, 4)

    total = _sum_devices("sum_small", gathered.reshape(N_DEV * rows, 128), rows).reshape(-1)
    sizes = [small[k].size for k in SMALL_ORDER] + [GATE_RANK * QK_W, 128]
    offs = [0]
    for s in sizes:
        offs.append(offs[-1] + s)
    pieces = [total[offs[i]:offs[i + 1]] for i in range(len(sizes))]
    g_small = {k: pieces[i].reshape(small[k].shape) for i, k in enumerate(SMALL_ORDER)}
    g_wa2_full = pieces[len(SMALL_ORDER)].reshape(GATE_RANK, QK_W)
    g_wa2 = lax.dynamic_slice(g_wa2_full, (0, dev * ab), (GATE_RANK, ab))
    loss_total = pieces[len(SMALL_ORDER) + 1][0]

    small_grads = {"ffn1_norm_g": g_small["ffn1"], "mix_norm_g": g_small["mix"], "ret_norm_g": g_small["ret"],
                   "gla_w_a2": g_wa2, "gla_b_a": g_small["b_a"], "gla_norm_g": g_small["gla"],
                   "ffn2_norm_g": g_small["ffn2"], "final_norm_g": g_small["final"]}
    run_adam("adamw_small", list(small_grads), small_grads, 1)

    order = ("ffn1_norm_g", "ffn1_w_gate", "ffn1_w_up", "ffn1_w_down", "mix_norm_g", "w_in", "ret_norm_g", "gla_w_a2",
             "gla_b_a", "gla_norm_g", "w_out", "ffn2_norm_g", "ffn2_w_gate", "ffn2_w_up", "ffn2_w_down", "final_norm_g")
    return (loss_total, dx0[None], *[grads[nm] for nm in order], *[updates[nm][0] for nm in order],
            *[updates[nm][1] for nm in order], *[updates[nm][2] for nm in order])
```

```python
import functools
import math

import jax
import jax.numpy as jnp
from jax import lax
from jax.experimental import pallas as pl
from jax.experimental.pallas import tpu as pltpu

F32 = jnp.float32
BF16 = jnp.bfloat16
MESH = pl.DeviceIdType.MESH
HBM = pl.BlockSpec(memory_space=pltpu.HBM)

N_DEV = 8
RMS_EPS = 1e-6
ROPE_BASE = 10000.0
HEADS = 4
DK = 64
DV = 128
QK_W = HEADS * DK
V_W = HEADS * DV
GATE_RANK = 16
GATE_NORM = 16.0
CHUNK = 64
SUPER = 256
PROJ_W = 3200
C_RQ, C_RK, C_RV, C_RG, C_GQ, C_GK, C_GV, C_GG, C_GL = 0, 256, 512, 1024, 1536, 1792, 2048, 2560, 3072
GL_W = PROJ_W - C_GL
ADAM_LR, ADAM_B1, ADAM_B2, ADAM_EPS, ADAM_WD, ADAM_STEP = 0.001, 0.9, 0.999, 1e-08, 0.01, 10
VMEM_LIMIT_V7X = 52 * 1024 * 1024


def _cparams(**kw):
    return pltpu.CompilerParams(vmem_limit_bytes=VMEM_LIMIT_V7X, **kw)


def _dot(a, b, form, precision=None):
    dims = {"nn": (((1,), (0,)), ((), ())), "nt": (((1,), (1,)), ((), ())), "tn": (((0,), (0,)), ((), ()))}[form]
    return lax.dot_general(a, b, dims, preferred_element_type=F32, precision=precision)


def _sigmoid(x):
    return 1.0 / (1.0 + jnp.exp(-x))


def _coords():
    return lax.axis_index("x"), lax.axis_index("y"), lax.axis_index("c")


class _NoComm:
    inputs, out_shapes, scratch = (), (), ()


class _AllGather:
    def __init__(self, arrays, kinds):
        self.inputs = tuple(arrays)
        self.kinds = tuple(kinds)
        n = len(arrays)
        self.out_shapes = tuple(
            jax.ShapeDtypeStruct((a.shape[0], N_DEV) + a.shape[1:] if k == "stack" else (N_DEV,) + a.shape, a.dtype)
            for a, k in zip(arrays, kinds))
        self.scratch = (pltpu.SemaphoreType.DMA((n, 7)), pltpu.SemaphoreType.DMA((n, 7)),
                        pltpu.SemaphoreType.DMA((n,)))

    def _ctx(self, srcs, outs, sems):
        send_sems, recv_sems, local_sems = sems
        x, y, c = _coords()
        me, sibling = (x, y, c), (x, y, 1 - c)
        chips = [(1 - x, y), (x, 1 - y), (1 - x, 1 - y)]

        def blk(m, dev):
            k = 4 * dev[0] + 2 * dev[1] + dev[2]
            return outs[m].at[:, k] if self.kinds[m] == "stack" else outs[m].at[k]

        def copy(m, s, block, to, src=None):
            return pltpu.make_async_remote_copy(
                src_ref=blk(m, block) if src is None else src, dst_ref=blk(m, block),
                send_sem=send_sems.at[m, s], recv_sem=recv_sems.at[m, s], device_id=to, device_id_type=MESH)

        def mine(m):
            return pltpu.make_async_copy(srcs[m], blk(m, me), local_sems.at[m])

        def first(m):
            return [copy(m, 0, me, sibling, src=srcs[m])] + [
                copy(m, 1 + j, me, (*chip, c), src=srcs[m]) for j, chip in enumerate(chips)]

        return me, sibling, chips, c, copy, mine, first

    def start(self, srcs, outs, sems):
        me, sibling, chips, c, copy, mine, first = self._ctx(srcs, outs, sems)
        for m in range(len(srcs)):
            mine(m).start()
            for cp in first(m):
                cp.start()

    def mid(self, srcs, outs, sems):
        me, sibling, chips, c, copy, mine, first = self._ctx(srcs, outs, sems)
        for j, chip in enumerate(chips):
            for m in range(len(srcs)):
                copy(m, 1 + j, (*chip, c), me).wait_recv()
                copy(m, 4 + j, (*chip, c), sibling).start()

    def finish(self, srcs, outs, sems):
        me, sibling, chips, c, copy, mine, first = self._ctx(srcs, outs, sems)
        for m in range(len(srcs)):
            copy(m, 0, sibling, me).wait_recv()
            for j, chip in enumerate(chips):
                copy(m, 4 + j, (*chip, 1 - c), me).wait_recv()
            for cp in first(m):
                cp.wait_send()
            for j, chip in enumerate(chips):
                copy(m, 4 + j, (*chip, c), sibling).wait_send()
            mine(m).wait()


RELATIONS = ((0, 0, 1), (1, 0, 0), (0, 1, 0), (1, 1, 0), (1, 0, 1), (0, 1, 1), (1, 1, 1))
SEM = pl.BlockSpec(memory_space=pltpu.SEMAPHORE)
SPLIT_PARAMS = dict(has_side_effects=pltpu.SideEffectType.DATAFLOW_SIDE_EFFECTING)


def _owner_copies(grad_ref, land_ref, send_sems, recv_sems):
    x, y, c = _coords()
    copies = []
    for s, (fx, fy, fc) in enumerate(RELATIONS):
        px = 1 - x if fx else x
        py = 1 - y if fy else y
        pc = 1 - c if fc else c
        copies.append(pltpu.make_async_remote_copy(
            src_ref=grad_ref.at[2 * px + py, pc], dst_ref=land_ref.at[s], send_sem=send_sems.at[s],
            recv_sem=recv_sems.at[s], device_id=(px, py, pc), device_id_type=MESH))
    return copies


def _send_to_owners(name, grad):
    n = len(RELATIONS)
    land_shape = (n,) + grad.shape[2:]

    def body(g_ref, land_ref, send_sems, recv_sems, g_thru, land_thru, token):
        for cp in _owner_copies(g_ref, land_ref, send_sems, recv_sems):
            cp.start()
        token[...] = jnp.zeros_like(token)

    return pl.pallas_call(
        body, name=name,
        out_shape=(pltpu.SemaphoreType.DMA((n,)), pltpu.SemaphoreType.DMA((n,)), pltpu.HBM(grad.shape, grad.dtype),
                   pltpu.HBM(land_shape, grad.dtype), jax.ShapeDtypeStruct((8, 128), F32)),
        in_specs=(HBM, HBM), out_specs=(SEM, SEM, HBM, HBM, pl.BlockSpec(memory_space=pltpu.VMEM)),
        input_output_aliases={0: 2, 1: 3}, compiler_params=pltpu.CompilerParams(**SPLIT_PARAMS),
    )(pltpu.with_memory_space_constraint(grad, pltpu.HBM),
      pltpu.with_memory_space_constraint(lax.empty(land_shape, grad.dtype), pltpu.HBM))


def _await_owners(name, started, after):
    send_sems, recv_sems, g_thru, land_thru, _ = started

    def body(g_ref, land_ref, send_sems, recv_sems, after_ref, g_out, land_out):
        for cp in _owner_copies(g_ref, land_ref, send_sems, recv_sems):
            cp.wait_send()
            cp.wait_recv()

    return pl.pallas_call(
        body, name=name, out_shape=(pltpu.HBM(g_thru.shape, g_thru.dtype), pltpu.HBM(land_thru.shape, land_thru.dtype)),
        in_specs=(HBM, HBM, SEM, SEM, pl.BlockSpec(memory_space=pl.ANY)), out_specs=(HBM, HBM),
        input_output_aliases={0: 0, 1: 1}, compiler_params=pltpu.CompilerParams(**SPLIT_PARAMS),
    )(g_thru, land_thru, send_sems, recv_sems, after)


def _call(name, main, grid, in_specs, out_specs, out_shape, args, scratch=(), comm=None, prefetch=None, after=None):
    comm = comm or _NoComm()
    n_main = len(in_specs)
    if after is not None:
        in_specs = list(in_specs) + [pl.BlockSpec(after.shape, lambda *_: (0,) * after.ndim)]
        args = tuple(args) + (after,)
    counts = [len(in_specs), len(comm.inputs), len(out_shape), len(comm.out_shapes), len(scratch), len(comm.scratch)]
    n_steps = math.prod(grid)
    hosted = bool(comm.inputs)

    def body(*refs):
        if prefetch is not None:
            refs = refs[1:]
        parts, at = [], 0
        for n in counts:
            parts.append(refs[at:at + n])
            at += n
        ins, c_in, outs, c_out, scr, c_scr = parts
        ins = ins[:n_main]
        step = pl.program_id(0)
        for d in range(1, len(grid)):
            step = step * grid[d] + pl.program_id(d)
        if hosted:
            @pl.when(step == 0)
            def _():
                comm.start(c_in, c_out, c_scr)
        main(ins, outs, scr)
        if hosted:
            @pl.when(step == max(n_steps - 2, 0))
            def _():
                comm.mid(c_in, c_out, c_scr)

            @pl.when(step == n_steps - 1)
            def _():
                comm.finish(c_in, c_out, c_scr)

    all_in = list(in_specs) + [HBM] * counts[1]
    all_out = list(out_specs) + [HBM] * counts[3]
    all_scratch = list(scratch) + list(comm.scratch)
    shapes = list(out_shape) + list(comm.out_shapes)
    if prefetch is None:
        res = pl.pallas_call(body, name=name, grid=grid, in_specs=all_in, out_specs=all_out, out_shape=shapes,
                             scratch_shapes=all_scratch, compiler_params=_cparams())(*args, *comm.inputs)
    else:
        res = pl.pallas_call(
            body, name=name, out_shape=shapes,
            grid_spec=pltpu.PrefetchScalarGridSpec(num_scalar_prefetch=1, grid=grid, in_specs=all_in,
                                                   out_specs=all_out, scratch_shapes=all_scratch),
            compiler_params=_cparams())(prefetch, *args, *comm.inputs)
    return res[:counts[2]], res[counts[2]:]


def _rms_fwd(name, x, g, comm=None):
    T, D = x.shape
    tm = min(T, 512)

    def main(ins, outs, scr):
        x_ref, g_ref = ins
        xv = x_ref[...]
        r = lax.rsqrt(jnp.mean(xv * xv, axis=-1, keepdims=True) + RMS_EPS)
        outs[0][...] = (xv * r * g_ref[...]).astype(outs[0].dtype)

    tile = pl.BlockSpec((tm, D), lambda i: (i, 0))
    (h,), extra = _call(name, main, (T // tm,), [tile, pl.BlockSpec((1, D), lambda i: (0, 0))], [tile],
                        [jax.ShapeDtypeStruct((T, D), BF16)], (x, g), comm=comm)
    return h, extra


def _final_loss_epilogue(scale, out_scale):
    def ep(acc, ex, outs):
        res_ref, g_ref, t_ref = ex
        dx_ref, dxb_ref, dg_ref, loss_ref = outs
        n = acc.shape[-1]
        xv = res_ref[...] + scale * acc
        r = lax.rsqrt(jnp.mean(xv * xv, axis=-1, keepdims=True) + RMS_EPS)
        xhat = xv * r
        err = xhat * g_ref[...] - t_ref[...]

        @pl.when(pl.program_id(0) == 0)
        def _():
            dg_ref[...] = jnp.zeros_like(dg_ref)
            loss_ref[...] = jnp.zeros_like(loss_ref)

        loss_ref[...] += jnp.broadcast_to(jnp.sum(err * err) * (0.5 / n), loss_ref.shape)
        dy = err * (1.0 / n)
        dg_ref[...] += jnp.sum(dy * xhat, axis=0, keepdims=True)
        dxhat = dy * g_ref[...]
        dx = r * (dxhat - xhat * jnp.mean(dxhat * xhat, axis=-1, keepdims=True))
        dx_ref[...] = dx
        dxb_ref[...] = (out_scale * dx).astype(dxb_ref.dtype)
    return ep


def _mm_nstream(name, a, ws, w_sel, w_form, comps, out_dtypes, epilogue, cn, rows=1024, comm=None, after=None):
    T, K = a.shape
    N = ws[0].shape[1]
    rows = min(rows, T)
    assert N % cn == 0 and T % rows == 0
    n_w, n_c = len(ws), len(comps)

    def main(ins, outs, scr):
        a_ref = ins[0]
        w_refs = ins[1:1 + n_w]
        c_refs = ins[1 + n_w:1 + n_w + n_c]

        for r in range(T // rows):
            sl = slice(r * rows, (r + 1) * rows)
            a_blk = a_ref[sl, :]
            dots = [_dot(a_blk, w_ref[...], w_form) for w_ref in w_refs]
            res = epilogue(dots, [c_ref[sl, :] for c_ref in c_refs])
            for o_ref, o in zip(outs, res):
                o_ref[sl, :] = o.astype(o_ref.dtype)

    if w_form == "nt":
        w_specs = [pl.BlockSpec((None, cn, K), functools.partial(lambda j, s: (s, j, 0), s=s)) for s in w_sel]
    else:
        w_specs = [pl.BlockSpec((K, cn), lambda j: (0, j)) for _ in ws]
    chunk = pl.BlockSpec((T, cn), lambda j: (0, j))
    return _call(name, main, (N // cn,), [pl.BlockSpec((T, K), lambda j: (0, 0))] + w_specs + [chunk] * n_c,
                 [chunk] * len(out_dtypes), [jax.ShapeDtypeStruct((T, N), dt) for dt in out_dtypes],
                 (a, *ws, *comps), comm=comm, after=after)


def _mm_mstream(name, as_, ws, w_sel, w_form, extras, outs_desc, epilogue, tm=512, comm=None, after=None):
    T = as_[0].shape[0]
    tm = min(tm, T)
    n_a = len(as_)
    w_shapes = [w.shape[-2:] for w in ws]
    N = w_shapes[0][1] if w_form == "nn" else w_shapes[0][0]

    def main(ins, outs, scr):
        a_refs = ins[:n_a]
        w_refs = ins[n_a:2 * n_a]
        acc = None
        for a_ref, w_ref in zip(a_refs, w_refs):
            d = _dot(a_ref[...], w_ref[...], w_form)
            acc = d if acc is None else acc + d
        epilogue(acc, ins[2 * n_a:], outs)

    kind_spec = {"tile": pl.BlockSpec((tm, N), lambda i: (i, 0)), "vec": pl.BlockSpec((1, N), lambda i: (0, 0))}
    kind_shape = {"tile": (T, N), "vec": (1, N)}
    a_specs = [pl.BlockSpec((tm, a.shape[1]), lambda i: (i, 0)) for a in as_]
    w_specs = []
    for w, s in zip(ws, w_sel):
        if w.ndim == 3:
            w_specs.append(pl.BlockSpec((None,) + tuple(w.shape[1:]), functools.partial(lambda i, s: (s, 0, 0), s=s),
                                        pipeline_mode=pl.Buffered(1)))
        else:
            w_specs.append(pl.BlockSpec(tuple(w.shape), lambda i: (0, 0), pipeline_mode=pl.Buffered(1)))
    args = list(as_) + list(ws) + [e for e, _ in extras]
    return _call(name, main, (T // tm,), a_specs + w_specs + [kind_spec[k] for _, k in extras],
                 [kind_spec[k] for _, k in outs_desc],
                 [jax.ShapeDtypeStruct(kind_shape[k], dt) for dt, k in outs_desc], args, comm=comm, after=after)


def _residual_rms_epilogue(scale):
    def ep(acc, ex, outs):
        xv = ex[0][...] + scale * acc
        outs[0][...] = xv
        r = lax.rsqrt(jnp.mean(xv * xv, axis=-1, keepdims=True) + RMS_EPS)
        outs[1][...] = (xv * r * ex[1][...]).astype(outs[1].dtype)
    return ep


def _rms_bwd_epilogue(out_scale):
    def ep(acc, ex, outs):
        x_ref, g_ref, dres_ref = ex
        dx_ref, dxb_ref, dg_ref = outs
        xv = x_ref[...]
        r = lax.rsqrt(jnp.mean(xv * xv, axis=-1, keepdims=True) + RMS_EPS)
        xhat = xv * r

        @pl.when(pl.program_id(0) == 0)
        def _():
            dg_ref[...] = jnp.zeros_like(dg_ref)

        dg_ref[...] += jnp.sum(acc * xhat, axis=0, keepdims=True)
        dxhat = acc * g_ref[...]
        dx = r * (dxhat - xhat * jnp.mean(dxhat * xhat, axis=-1, keepdims=True)) + dres_ref[...]
        dx_ref[...] = dx
        dxb_ref[...] = (out_scale * dx).astype(dxb_ref.dtype)
    return ep


def _mm_tn(name, a, b, tmo, tno, out_dtype, tk=1024, comm=None, after=None):
    T, Ma = a.shape
    Nb = b.shape[1]
    tk = min(tk, T)
    nk = T // tk

    def main(ins, outs, scr):
        a_ref, b_ref = ins
        (acc_ref,) = scr
        k = pl.program_id(2)

        @pl.when(k == 0)
        def _():
            acc_ref[...] = jnp.zeros_like(acc_ref)

        acc_ref[...] += _dot(a_ref[...], b_ref[...], "tn")

        @pl.when(k == nk - 1)
        def _():
            outs[0][...] = acc_ref[...].astype(outs[0].dtype)

    (out,), extra = _call(
        name, main, (Ma // tmo, Nb // tno, nk),
        [pl.BlockSpec((tk, tmo), lambda i, j, k: (k, i)), pl.BlockSpec((tk, tno), lambda i, j, k: (k, j))],
        [pl.BlockSpec((tmo, tno), lambda i, j, k: (i, j))], [jax.ShapeDtypeStruct((Ma, Nb), out_dtype)],
        (a, b), scratch=[pltpu.VMEM((tmo, tno), F32)], comm=comm, after=after)
    return out, extra


def _swiglu_parts(g, u):
    s = _sigmoid(g)
    silu = g * s
    return [u * (s + silu * (1.0 - s)), silu, silu * u]


def _silu_mul_epilogue(dots, comps):
    g, u = dots
    return _swiglu_parts(g, u)


def _gate_parts_epilogue(dots, comps):
    (g,) = dots
    s = _sigmoid(g)
    silu = g * s
    return [s + silu * (1.0 - s), silu]


def _up_act_epilogue(dots, comps):
    (u,) = dots
    return [u * comps[0].astype(F32), u * comps[1].astype(F32)]


def _dact_epilogue(dots, comps):
    dact = dots[0].astype(BF16)
    return [dact * comps[0], dact * comps[1]]


def _identity_epilogue(dots, comps):
    return list(dots)


def _swap_halves(x):
    lane = lax.broadcasted_iota(jnp.int32, x.shape, 1)
    first = (lane % DK) < (DK // 2)
    return jnp.where(first, pltpu.roll(x, 128 - DK // 2, 1), pltpu.roll(x, DK // 2, 1))


def _rotary(t, cos, sin_signed):
    halves = []
    for p in range(QK_W // 128):
        th = t[:, 128 * p:128 * (p + 1)]
        halves.append(th * cos + _swap_halves(th) * sin_signed)
    return jnp.concatenate(halves, axis=1)


def _rotary_transposed(d, cos, sin_signed):
    halves = []
    for p in range(QK_W // 128):
        dh = d[:, 128 * p:128 * (p + 1)]
        halves.append(dh * cos + _swap_halves(dh * sin_signed))
    return jnp.concatenate(halves, axis=1)


def _log_sigmoid(x):
    return jnp.minimum(x, 0.0) - jnp.log(1.0 + jnp.exp(-jnp.abs(x)))


def _tri_sum(mask, x):
    tri = mask.astype(BF16)
    hi = x.astype(BF16)
    rest = x - hi.astype(F32)
    mid = rest.astype(BF16)
    lo = (rest - mid.astype(F32)).astype(BF16)
    return _dot(tri, hi, "nn") + _dot(tri, mid, "nn") + _dot(tri, lo, "nn")


def _attn_masks():
    row = lax.broadcasted_iota(jnp.int32, (SUPER, SUPER), 0)
    col = lax.broadcasted_iota(jnp.int32, (SUPER, SUPER), 1)
    same = (row // CHUNK) == (col // CHUNK)
    return row, col, same


def _group_inputs(grp, pr, cos, sin_signed, lg, wa2, ba):
    seg = lambda lo, width: pr[:, lo:lo + width].astype(F32)
    if grp == 0:
        q = _rotary(seg(C_RQ, QK_W), cos, sin_signed)
        k = _rotary(seg(C_RK, QK_W), cos, sin_signed) * (DK ** -0.5)
        v = pr[:, C_RV:C_RV + V_W]
        gate = seg(C_RG, V_W)
        pos = lax.broadcasted_iota(jnp.int32, (SUPER, QK_W), 0).astype(F32) + 1.0
        return q, k, v, gate, pos * lg, None, None
    q = seg(C_GQ, QK_W) * (DK ** -0.5)
    k = seg(C_GK, QK_W)
    v = pr[:, C_GV:C_GV + V_W]
    gate = seg(C_GG, V_W)
    glow = pr[:, C_GL:C_GL + GL_W]
    logit = _dot(glow.astype(BF16), wa2.astype(BF16), "nn") + ba
    la = _log_sigmoid(logit) * (1.0 / GATE_NORM)
    row, col, _ = _attn_masks()
    b_cum = _tri_sum(col <= row, la)
    return q, k, v, gate, b_cum, glow, logit


def _decay_factors(q, k, b_cum):
    c = b_cum[SUPER // 2 - 1:SUPER // 2, :]
    bl = b_cum[SUPER - 1:SUPER, :]
    e1 = jnp.exp(b_cum - c)
    e2 = jnp.exp(c - b_cum)
    e_b = jnp.exp(b_cum)
    e_l = jnp.exp(bl - b_cum)
    return dict(e1=e1, e2=e2, eb=e_b, el=e_l, ebl=jnp.exp(bl),
                qp=q * e1, qm=q * e2, kp=k * e1, km=k * e2, qs=q * e_b, kl=k * e_l)


def _state_block_mask():
    r = lax.broadcasted_iota(jnp.int32, (V_W, QK_W), 0)
    c = lax.broadcasted_iota(jnp.int32, (V_W, QK_W), 1)
    return (r // DV) == (c // DK)


def _attn_fwd(proj, cos, sin_signed, lg, wa2p, ba, gn_ret, gn_gla, x_res, w_out, g_next, comm=None):
    T = proj.shape[0]
    n_s = T // SUPER
    D = x_res.shape[1]

    def main(ins, outs, scr):
        pr_ref, cos_ref, sin_ref, lg_ref, wa2_ref, ba_ref, gr_ref, gg_ref, xres_ref, wout_ref, gnext_ref = ins
        o_ref, y_ref, st_ref, x_ref, h_ref = outs
        (s_ref,) = scr
        i = pl.program_id(0)

        @pl.when(i == 0)
        def _():
            s_ref[...] = jnp.zeros_like(s_ref)

        pr = pr_ref
        row, col, same = _attn_masks()
        m1 = col <= row
        m2 = jnp.logical_and(col > row, same)
        lane = lax.broadcasted_iota(jnp.int32, (1, QK_W), 1)
        blockmask = _state_block_mask()
        for grp in range(2):
            q, k, v, gate, b_cum, _, _ = _group_inputs(grp, pr, cos_ref[...], sin_ref[...], lg_ref[...],
                                                      wa2_ref[...], ba_ref[...])
            f = _decay_factors(q, k, b_cum)
            gn = gr_ref[...] if grp == 0 else gg_ref[...]
            s_prev = s_ref[grp]
            st_ref[0, grp] = s_prev
            o_inter = _dot(f["qs"].astype(BF16), s_prev.astype(BF16), "nt")
            kmb = f["km"].astype(BF16)
            kpb = f["kp"].astype(BF16)
            vb = v.astype(BF16)
            heads = [(lane // DK) == h for h in range(HEADS)]
            a1_all = _dot(jnp.concatenate([jnp.where(hm, f["qp"], 0.0).astype(BF16) for hm in heads], axis=0), kmb, "nt")
            a2_all = _dot(jnp.concatenate([jnp.where(hm, f["qm"], 0.0).astype(BF16) for hm in heads], axis=0), kpb, "nt")
            for h in range(HEADS):
                a1 = a1_all[h * SUPER:(h + 1) * SUPER]
                a2 = a2_all[h * SUPER:(h + 1) * SUPER]
                a = jnp.where(m1, a1, jnp.where(m2, a2, 0.0))
                lo = grp * V_W + h * DV
                o_h = _dot(a.astype(BF16), vb[:, h * DV:(h + 1) * DV], "nn") + o_inter[:, h * DV:(h + 1) * DV]
                o_ref[:, lo:lo + DV] = o_h
                r = lax.rsqrt(jnp.mean(o_h * o_h, axis=-1, keepdims=True) + RMS_EPS)
                gte = gate[:, h * DV:(h + 1) * DV]
                y = o_h * r * gn[:, h * DV:(h + 1) * DV] * (gte * _sigmoid(gte))
                y_ref[:, lo:lo + DV] = y.astype(y_ref.dtype)
            upd = _dot(vb, f["kl"].astype(BF16), "tn")
            s_ref[grp] = s_prev * f["ebl"] + jnp.where(blockmask, upd, 0.0)
        xv = xres_ref[...] + _dot(y_ref[...], wout_ref[...], "nn")
        x_ref[...] = xv
        r = lax.rsqrt(jnp.mean(xv * xv, axis=-1, keepdims=True) + RMS_EPS)
        h_ref[...] = (xv * r * gnext_ref[...]).astype(h_ref.dtype)

    const = lambda shape: pl.BlockSpec(shape, lambda i: tuple(0 for _ in shape))
    rows = lambda w: pl.BlockSpec((SUPER, w), lambda i: (i, 0))
    return _call(
        "attn_fwd", main, (n_s,),
        [rows(PROJ_W), rows(128), rows(128),
         const((1, QK_W)), const((GL_W, QK_W)), const((1, QK_W)), const((1, V_W)), const((1, V_W)),
         rows(D), const((2 * V_W, D)), const((1, D))],
        [rows(2 * V_W), rows(2 * V_W), pl.BlockSpec((1, 2, V_W, QK_W), lambda i: (i, 0, 0, 0)), rows(D), rows(D)],
        [jax.ShapeDtypeStruct((T, 2 * V_W), F32), jax.ShapeDtypeStruct((T, 2 * V_W), BF16),
         jax.ShapeDtypeStruct((n_s, 2, V_W, QK_W), F32), jax.ShapeDtypeStruct((T, D), F32),
         jax.ShapeDtypeStruct((T, D), BF16)],
        (proj, cos, sin_signed, lg, wa2p, ba, gn_ret, gn_gla, x_res, w_out, g_next),
        scratch=[pltpu.VMEM((2, V_W, QK_W), F32)], comm=comm)


def _attn_bwd(proj, cos, sin_signed, lg, wa2p, ba, gn_ret, gn_gla, o, dx, w_out, states, comm=None, after=None):
    T = proj.shape[0]
    n_s = T // SUPER
    D = dx.shape[1]

    def main(ins, outs, scr):
        pr_ref, cos_ref, sin_ref, lg_ref, wa2_ref, ba_ref, gr_ref, gg_ref, o_ref, dx_ref, wout_ref, st_ref = ins
        dp_ref, dgr_ref, dgg_ref, dba_ref, dwa_ref = outs
        (ds_ref, dy_ref) = scr
        i = pl.program_id(0)
        dy_ref[...] = _dot(dx_ref[...], wout_ref[...], "nt")

        @pl.when(i == 0)
        def _():
            ds_ref[...] = jnp.zeros_like(ds_ref)
            dgr_ref[...] = jnp.zeros_like(dgr_ref)
            dgg_ref[...] = jnp.zeros_like(dgg_ref)
            dba_ref[...] = jnp.zeros_like(dba_ref)
            dwa_ref[...] = jnp.zeros_like(dwa_ref)

        pr = pr_ref
        cos = cos_ref[...]
        sin_signed = sin_ref[...]
        row, col, same = _attn_masks()
        m1 = col <= row
        m2 = jnp.logical_and(col > row, same)
        m1t = row <= col
        m2t = jnp.logical_and(row > col, same)
        lane = lax.broadcasted_iota(jnp.int32, (1, QK_W), 1)
        blockmask = _state_block_mask()
        for grp in range(2):
            q, k, v, gate, b_cum, glow, logit = _group_inputs(grp, pr, cos, sin_signed, lg_ref[...],
                                                              wa2_ref[...], ba_ref[...])
            f = _decay_factors(q, k, b_cum)
            gn = gr_ref[...] if grp == 0 else gg_ref[...]
            dgn_ref = dgr_ref if grp == 0 else dgg_ref
            do_parts, dgate_parts, dgn_parts = [], [], []
            for h in range(HEADS):
                lo = grp * V_W + h * DV
                o_h = o_ref[:, lo:lo + DV]
                r = lax.rsqrt(jnp.mean(o_h * o_h, axis=-1, keepdims=True) + RMS_EPS)
                n = o_h * r
                gte = gate[:, h * DV:(h + 1) * DV]
                sg = _sigmoid(gte)
                dy_h = dy_ref[:, lo:lo + DV]
                gn_h = gn[:, h * DV:(h + 1) * DV]
                dgate_parts.append(dy_h * n * gn_h * (sg * (1.0 + gte * (1.0 - sg))))
                dz = dy_h * (gte * sg)
                dgn_parts.append(jnp.sum(dz * n, axis=0, keepdims=True))
                dn = dz * gn_h
                do_parts.append(r * (dn - n * jnp.mean(dn * n, axis=-1, keepdims=True)))
            dgn_ref[...] += jnp.concatenate(dgn_parts, axis=1)
            dgate = jnp.concatenate(dgate_parts, axis=1)
            do = jnp.concatenate(do_parts, axis=1)
            dob = do.astype(BF16)
            vb = v.astype(BF16)
            s_prev = st_ref[0, grp]
            ds_new = ds_ref[grp]
            dsb = ds_new.astype(BF16)
            qpb, qmb = f["qp"].astype(BF16), f["qm"].astype(BF16)
            kpb, kmb = f["kp"].astype(BF16), f["km"].astype(BF16)
            dqp = jnp.zeros((SUPER, QK_W), F32)
            dqm = jnp.zeros((SUPER, QK_W), F32)
            dkp = jnp.zeros((SUPER, QK_W), F32)
            dkm = jnp.zeros((SUPER, QK_W), F32)
            dv_parts = []
            heads = [(lane // DK) == h for h in range(HEADS)]
            qp_hs = [jnp.where(hm, f["qp"], 0.0).astype(BF16) for hm in heads]
            qm_hs = [jnp.where(hm, f["qm"], 0.0).astype(BF16) for hm in heads]
            kp_hs = [jnp.where(hm, f["kp"], 0.0).astype(BF16) for hm in heads]
            km_hs = [jnp.where(hm, f["km"], 0.0).astype(BF16) for hm in heads]
            at1_all = _dot(jnp.concatenate(km_hs, axis=0), qpb, "nt")
            at2_all = _dot(jnp.concatenate(kp_hs, axis=0), qmb, "nt")
            for h in range(HEADS):
                qp_h, qm_h, kp_h, km_h = qp_hs[h], qm_hs[h], kp_hs[h], km_hs[h]
                at = jnp.where(m1t, at1_all[h * SUPER:(h + 1) * SUPER],
                               jnp.where(m2t, at2_all[h * SUPER:(h + 1) * SUPER], 0.0))
                do_h = dob[:, h * DV:(h + 1) * DV]
                v_h = vb[:, h * DV:(h + 1) * DV]
                dv_parts.append(_dot(at.astype(BF16), do_h, "nn"))
                da = _dot(do_h, v_h, "nt")
                dat = _dot(v_h, do_h, "nt")
                da1 = jnp.where(m1, da, 0.0).astype(BF16)
                da2 = jnp.where(m2, da, 0.0).astype(BF16)
                da1t = jnp.where(m1t, dat, 0.0).astype(BF16)
                da2t = jnp.where(m2t, dat, 0.0).astype(BF16)
                dqp = dqp + _dot(da1, km_h, "nn")
                dqm = dqm + _dot(da2, kp_h, "nn")
                dkm = dkm + _dot(da1t, qp_h, "nn")
                dkp = dkp + _dot(da2t, qm_h, "nn")
            klb = f["kl"].astype(BF16)
            qsb = f["qs"].astype(BF16)
            dqs = _dot(dob, s_prev.astype(BF16), "nn")
            dkl = _dot(vb, dsb, "nn")
            dv = jnp.concatenate(dv_parts, axis=1) + _dot(klb, dsb, "nt")
            ds_ref[grp] = ds_new * f["ebl"] + jnp.where(blockmask, _dot(dob, qsb, "tn"), 0.0)
            dq = dqp * f["e1"] + dqm * f["e2"] + dqs * f["eb"]
            dk = dkm * f["e2"] + dkp * f["e1"] + dkl * f["el"]
            if grp == 0:
                dq = _rotary_transposed(dq, cos, sin_signed)
                dk = _rotary_transposed(dk * (DK ** -0.5), cos, sin_signed)
                dp_ref[:, C_RQ:C_RQ + QK_W] = dq.astype(dp_ref.dtype)
                dp_ref[:, C_RK:C_RK + QK_W] = dk.astype(dp_ref.dtype)
                dp_ref[:, C_RV:C_RV + V_W] = dv.astype(dp_ref.dtype)
                dp_ref[:, C_RG:C_RG + V_W] = dgate.astype(dp_ref.dtype)
            else:
                dkl_kl = dkl * klb.astype(F32)
                db = (dqp * qpb.astype(F32) - dkm * kmb.astype(F32) - dqm * qmb.astype(F32)
                      + dkp * kpb.astype(F32) + dqs * qsb.astype(F32) - dkl_kl)
                last = (jnp.sum(dkl_kl, axis=0, keepdims=True)
                        + f["ebl"] * jnp.sum(s_prev * ds_new, axis=0, keepdims=True))
                rowq = lax.broadcasted_iota(jnp.int32, (SUPER, QK_W), 0)
                db = db + jnp.where(rowq == SUPER - 1, last, 0.0)
                dla = _tri_sum(col >= row, db)
                dlogit = dla * (1.0 / GATE_NORM) * (1.0 - _sigmoid(logit))
                dlb = dlogit.astype(BF16)
                dglow = _dot(dlb, wa2_ref[...].astype(BF16), "nt")
                dwa_ref[...] += _dot(glow.astype(BF16), dlb, "tn")
                dba_ref[...] += jnp.sum(dlogit, axis=0, keepdims=True)
                dp_ref[:, C_GQ:C_GQ + QK_W] = (dq * (DK ** -0.5)).astype(dp_ref.dtype)
                dp_ref[:, C_GK:C_GK + QK_W] = dk.astype(dp_ref.dtype)
                dp_ref[:, C_GV:C_GV + V_W] = dv.astype(dp_ref.dtype)
                dp_ref[:, C_GG:C_GG + V_W] = dgate.astype(dp_ref.dtype)
                dp_ref[:, C_GL:C_GL + GL_W] = dglow.astype(dp_ref.dtype)

    rev = lambda i: n_s - 1 - i
    const = lambda shape: pl.BlockSpec(shape, lambda i: tuple(0 for _ in shape))
    return _call(
        "attn_bwd", main, (n_s,),
        [pl.BlockSpec((SUPER, PROJ_W), lambda i: (rev(i), 0)),
         pl.BlockSpec((SUPER, 128), lambda i: (rev(i), 0)), pl.BlockSpec((SUPER, 128), lambda i: (rev(i), 0)),
         const((1, QK_W)), const((GL_W, QK_W)), const((1, QK_W)), const((1, V_W)), const((1, V_W)),
         pl.BlockSpec((SUPER, 2 * V_W), lambda i: (rev(i), 0)),
         pl.BlockSpec((SUPER, D), lambda i: (rev(i), 0)), const((2 * V_W, D)),
         pl.BlockSpec((1, 2, V_W, QK_W), lambda i: (rev(i), 0, 0, 0))],
        [pl.BlockSpec((SUPER, PROJ_W), lambda i: (rev(i), 0)),
         const((1, V_W)), const((1, V_W)), const((1, QK_W)), const((GL_W, QK_W))],
        [jax.ShapeDtypeStruct((T, PROJ_W), BF16),
         jax.ShapeDtypeStruct((1, V_W), F32), jax.ShapeDtypeStruct((1, V_W), F32),
         jax.ShapeDtypeStruct((1, QK_W), F32), jax.ShapeDtypeStruct((GL_W, QK_W), F32)],
        (proj, cos, sin_signed, lg, wa2p, ba, gn_ret, gn_gla, o, dx, w_out, states),
        scratch=[pltpu.VMEM((2, V_W, QK_W), F32), pltpu.VMEM((SUPER, 2 * V_W), F32)], comm=comm, after=after)


def _rotary_tables(T):
    half = DK // 2
    inv = ROPE_BASE ** (-jnp.arange(half, dtype=F32) * 2.0 / DK)
    ang = jnp.arange(T, dtype=F32)[:, None] * inv[None, :]
    cos, sin = jnp.cos(ang), jnp.sin(ang)
    cos_head = jnp.concatenate([cos, cos], axis=1)
    sin_head = jnp.concatenate([-sin, sin], axis=1)
    return jnp.tile(cos_head, (1, 128 // DK)), jnp.tile(sin_head, (1, 128 // DK))


def _sum_devices(name, gathered, m_per):
    def body(g_ref, o_ref):
        acc = g_ref[0:m_per, :]
        for k in range(1, N_DEV):
            acc = acc + g_ref[k * m_per:(k + 1) * m_per, :]
        o_ref[...] = acc

    return pl.pallas_call(body, name=name, out_shape=jax.ShapeDtypeStruct((m_per, 128), F32))(gathered)


def _owner_sums(name, items, owner, comm=None):
    counts = [1 + len(landed) for _, landed in items]

    def main(ins, outs, scr):
        at = 0
        for o_ref, n in zip(outs, counts):
            acc = ins[at][...].astype(F32)
            for l_ref in ins[at + 1:at + n]:
                for j in range(l_ref.shape[0]):
                    acc = acc + l_ref[j].astype(F32)
            o_ref[...] = acc
            at += n

    once = pl.Buffered(1)
    in_specs, out_specs, out_shape, args = [], [], [], []
    for grad, landed in items:
        R, C = grad.shape[-2:]
        in_specs.append(pl.BlockSpec((None, None, R, C), lambda i, s: (s[0], s[1], 0, 0), pipeline_mode=once))
        in_specs += [pl.BlockSpec(tuple(l.shape), lambda i, s: (0, 0, 0), pipeline_mode=once) for l in landed]
        out_specs.append(pl.BlockSpec((R, C), lambda i, s: (0, 0)))
        out_shape.append(jax.ShapeDtypeStruct((R, C), F32))
        args += [grad, *landed]
    return _call(name, main, (1,), in_specs, out_specs, out_shape, args, comm=comm, prefetch=owner)


def _adamw_group(name, items, n_blocks, comm=None):
    n = len(items)

    def main(ins, outs, scr):
        for p in range(n):
            g_ref, w_ref, m_ref, v_ref = ins[4 * p:4 * p + 4]
            d_ref, nm_ref, nv_ref = outs[3 * p:3 * p + 3]
            gv = g_ref[...]
            nm = ADAM_B1 * m_ref[...] + (1.0 - ADAM_B1) * gv
            nv = ADAM_B2 * v_ref[...] + (1.0 - ADAM_B2) * (gv * gv)
            m_hat = nm / (1.0 - ADAM_B1 ** ADAM_STEP)
            v_hat = nv / (1.0 - ADAM_B2 ** ADAM_STEP)
            d_ref[...] = -ADAM_LR * (m_hat / (jnp.sqrt(v_hat) + ADAM_EPS) + ADAM_WD * w_ref[...])
            nm_ref[...] = nm
            nv_ref[...] = nv

    in_specs, out_specs, out_shape, args = [], [], [], []
    for item in items:
        R, C = item[1].shape
        assert R % n_blocks == 0
        spec = pl.BlockSpec((R // n_blocks, C), lambda i: (i, 0))
        in_specs += [spec] * 4
        out_specs += [spec] * 3
        out_shape += [jax.ShapeDtypeStruct((R, C), F32)] * 3
        args += list(item)
    outs, extra = _call(name, main, (n_blocks,), in_specs, out_specs, out_shape, args, comm=comm)
    return [tuple(outs[3 * p:3 * p + 3]) for p in range(n)], extra


SMALL_ORDER = ("ffn1", "mix", "ffn2", "final", "ret", "gla", "b_a")


def kernel(x, ffn1_norm_g, ffn1_w_gate, ffn1_w_up, ffn1_w_down, mix_norm_g, w_in, ret_norm_g, gla_w_a2, gla_b_a, gla_norm_g, w_out, ffn2_norm_g, ffn2_w_gate, ffn2_w_up, ffn2_w_down, final_norm_g, loss_target, m_ffn1_norm_g, m_ffn1_w_gate, m_ffn1_w_up, m_ffn1_w_down, m_mix_norm_g, m_w_in, m_ret_norm_g, m_gla_w_a2, m_gla_b_a, m_gla_norm_g, m_w_out, m_ffn2_norm_g, m_ffn2_w_gate, m_ffn2_w_up, m_ffn2_w_down, m_final_norm_g, v_ffn1_norm_g, v_ffn1_w_gate, v_ffn1_w_up, v_ffn1_w_down, v_mix_norm_g, v_w_in, v_ret_norm_g, v_gla_w_a2, v_gla_b_a, v_gla_norm_g, v_w_out, v_ffn2_norm_g, v_ffn2_w_gate, v_ffn2_w_up, v_ffn2_w_down, v_final_norm_g):
    xi, yi, ci = _coords()
    dev = 4 * xi + 2 * yi + ci
    owner = jnp.stack([2 * xi + yi, ci]).astype(jnp.int32)

    x0, target = x[0], loss_target[0]
    T, D = x0.shape
    fb = ffn1_w_gate.shape[2]
    ib = w_in.shape[2]
    ab = gla_w_a2.shape[2]
    F = N_DEV * fb
    cos, sin_signed = _rotary_tables(T)
    lg = jnp.repeat(jnp.log(1.0 - 2.0 ** (-5.0 - jnp.arange(HEADS, dtype=F32))), DK)[None, :]
    g_final = final_norm_g.reshape(1, D)

    g1_loc = ffn1_w_gate[0].T[None].astype(BF16)
    u1_loc = ffn1_w_up[0].T[None].astype(BF16)
    d1_loc = ffn1_w_down.astype(BF16)
    g2_loc = ffn2_w_gate[0].T[None].astype(BF16)
    u2_loc = ffn2_w_up[0].T[None].astype(BF16)
    d2_loc = ffn2_w_down.astype(BF16)
    in_loc = w_in[0].T.astype(BF16)
    out_loc = w_out[0].astype(BF16)

    h1, (g1,) = _rms_fwd("ffn1_rms", x0, ffn1_norm_g, comm=_AllGather([g1_loc], ["stack"]))
    g1 = g1.reshape(1, F, D)
    (dsl1, sl1), (u1,) = _mm_nstream("ffn1_gate", h1, [g1], [0], "nt", [], [BF16, BF16], _gate_parts_epilogue, cn=256,
                                     comm=_AllGather([u1_loc], ["stack"]))
    u1 = u1.reshape(1, F, D)
    (dsu1, act1), (d1,) = _mm_nstream("ffn1_up", h1, [u1], [0], "nt", [dsl1, sl1], [BF16, BF16],
                                      _up_act_epilogue, cn=256, comm=_AllGather([d1_loc], ["stack"]))
    d1 = d1.reshape(1, F, D)
    f32_tile, bf16_tile, f32_vec = (F32, "tile"), (BF16, "tile"), (F32, "vec")
    (x1, h2), (in_all, a_all) = _mm_mstream(
        "ffn1_down", [act1], [d1], [0], "nn", [(x0, "tile"), (mix_norm_g, "vec")], [f32_tile, bf16_tile],
        _residual_rms_epilogue(0.5), comm=_AllGather([in_loc, gla_w_a2[0]], ["plain", "plain"]))
    w_in_t = jnp.pad(in_all.reshape(1, N_DEV * ib, D), ((0, 0), (0, PROJ_W - N_DEV * ib), (0, 0)))
    wa2 = jnp.transpose(a_all, (1, 0, 2)).reshape(GATE_RANK, N_DEV * ab)
    wa2p = jnp.pad(wa2, ((0, GL_W - GATE_RANK), (0, 0)))

    (proj,), (g2, out_all) = _mm_nstream("mix_proj", h2, [w_in_t], [0], "nt", [], [BF16], _identity_epilogue, cn=640,
                                         comm=_AllGather([g2_loc, out_loc], ["stack", "plain"]))
    w_out_full = out_all.reshape(D, D)
    (o, ymix, states, x2, h3), (u2,) = _attn_fwd(proj, cos, sin_signed, lg, wa2p, gla_b_a, ret_norm_g, gla_norm_g,
                                                 x1, w_out_full, ffn2_norm_g, comm=_AllGather([u2_loc], ["stack"]))
    g2, u2 = g2.reshape(1, F, D), u2.reshape(1, F, D)

    (dsu2, sl2, act2), (d2,) = _mm_nstream(
        "ffn2_up", h3, [g2, u2], [0, 0], "nt", [], [BF16, BF16, BF16], _silu_mul_epilogue, cn=256,
        comm=_AllGather([d2_loc], ["stack"]))
    d2 = d2.reshape(1, F, D)
    (dx3, dy3b, d_final, loss), _ = _mm_mstream(
        "ffn2_down", [act2], [d2], [0], "nn", [(x2, "tile"), (g_final, "vec"), (target, "tile")],
        [f32_tile, bf16_tile, f32_vec, f32_vec], _final_loss_epilogue(0.5, 0.5))

    sent = {}

    def send(nm, grad):
        sent[nm] = _send_to_owners("send_" + nm, grad)
        return sent[nm][4]

    dwd2, _ = _mm_tn("ffn2b_dwd", act2, dy3b, F // 2, D, BF16)
    tok = send("wd2", dwd2.reshape(4, 2, fb, D))
    (dgate2, dup2), _ = _mm_nstream("ffn2b_dact", dy3b, [d2], [0], "nt", [dsu2, sl2], [BF16, BF16],
                                    _dact_epilogue, cn=256, after=tok)
    dwg2, _ = _mm_tn("ffn2b_dwg", dgate2, h3, F // 2, D, BF16)
    tok = send("wg2", dwg2.reshape(4, 2, fb, D))
    dwu2, _ = _mm_tn("ffn2b_dwu", dup2, h3, F // 2, D, BF16, after=tok)
    tok = send("wu2", dwu2.reshape(4, 2, fb, D))
    rms_outs = [f32_tile, bf16_tile, f32_vec]
    (dx2, dx2b, d_g2), _ = _mm_mstream(
        "ffn2b_dh", [dgate2, dup2], [g2, u2], [0, 0], "nn", [(x2, "tile"), (ffn2_norm_g, "vec"), (dx3, "tile")],
        rms_outs, _rms_bwd_epilogue(1.0), after=tok)

    dwout, _ = _mm_tn("mixb_dwout", ymix, dx2b, D, D, BF16)
    tok = send("wout", dwout.reshape(4, 2, D // N_DEV, D))
    (dproj, d_ret, d_gla, d_ba, d_wa2p), _ = _attn_bwd(
        proj, cos, sin_signed, lg, wa2p, gla_b_a, ret_norm_g, gla_norm_g, o, dx2b, w_out_full, states, after=tok)
    dwin_t, _ = _mm_tn("mixb_dwin", dproj, h2, 640, D, BF16, tk=2048)
    tok = send("win", dwin_t[:N_DEV * ib].reshape(4, 2, ib, D))
    (dx1, dy1b, d_gmix), _ = _mm_mstream(
        "mixb_dh", [dproj], [w_in_t], [0], "nn", [(x1, "tile"), (mix_norm_g, "vec"), (dx2, "tile")],
        rms_outs, _rms_bwd_epilogue(0.5), after=tok)

    dwd1, _ = _mm_tn("ffn1b_dwd", act1, dy1b, F // 2, D, BF16)
    tok = send("wd1", dwd1.reshape(4, 2, fb, D))
    (dgate1, dup1), _ = _mm_nstream("ffn1b_dact", dy1b, [d1], [0], "nt", [dsu1, sl1], [BF16, BF16],
                                    _dact_epilogue, cn=256, after=tok)
    dwg1, _ = _mm_tn("ffn1b_dwg", dgate1, h1, F // 2, D, BF16)
    tok = send("wg1", dwg1.reshape(4, 2, fb, D))
    dwu1, _ = _mm_tn("ffn1b_dwu", dup1, h1, F // 2, D, BF16, after=tok)
    tok = send("wu1", dwu1.reshape(4, 2, fb, D))
    (dx0, _, d_g1), _ = _mm_mstream(
        "ffn1b_dh", [dgate1, dup1], [g1, u1], [0, 0], "nn", [(x0, "tile"), (ffn1_norm_g, "vec"), (dx1, "tile")],
        rms_outs, _rms_bwd_epilogue(1.0), after=tok)

    small = dict(ffn1=d_g1, mix=d_gmix, ffn2=d_g2, final=d_final, ret=d_ret, gla=d_gla, b_a=d_ba)
    flat = jnp.concatenate([small[k].reshape(-1) for k in SMALL_ORDER]
                           + [d_wa2p[:GATE_RANK].reshape(-1), loss[0, :128]])
    rows = -(-flat.shape[0] // 128)
    rows = -(-rows // 8) * 8
    packed = jnp.pad(flat, (0, rows * 128 - flat.shape[0])).reshape(rows, 128)

    transposed = ("ffn1_w_gate", "ffn1_w_up", "ffn2_w_gate", "ffn2_w_up", "w_in")

    def to_2d(nm, a):
        if nm in transposed:
            return a[0].T
        return a.reshape((1, a.shape[0]) if a.ndim == 1 else a.shape[-2:])

    def from_2d(nm, a):
        return a.T[None] if nm in transposed else a.reshape(params[nm][0].shape)

    def arrived(nm, after):
        grad, landed = _await_owners("await_" + nm, sent[nm], after)
        return grad, [landed]

    sums_a, (gathered,) = _owner_sums(
        "sum_a", [arrived(nm, dx0) for nm in ("wg2", "wu2", "wd2", "win", "wout")], owner,
        comm=_AllGather([packed], ["plain"]))
    params = dict(
        ffn2_w_gate=(ffn2_w_gate, m_ffn2_w_gate, v_ffn2_w_gate), ffn2_w_up=(ffn2_w_up, m_ffn2_w_up, v_ffn2_w_up),
        ffn2_w_down=(ffn2_w_down, m_ffn2_w_down, v_ffn2_w_down), w_in=(w_in, m_w_in, v_w_in),
        w_out=(w_out, m_w_out, v_w_out), ffn1_w_gate=(ffn1_w_gate, m_ffn1_w_gate, v_ffn1_w_gate),
        ffn1_w_up=(ffn1_w_up, m_ffn1_w_up, v_ffn1_w_up), ffn1_w_down=(ffn1_w_down, m_ffn1_w_down, v_ffn1_w_down),
        ffn1_norm_g=(ffn1_norm_g, m_ffn1_norm_g, v_ffn1_norm_g), mix_norm_g=(mix_norm_g, m_mix_norm_g, v_mix_norm_g),
        ret_norm_g=(ret_norm_g, m_ret_norm_g, v_ret_norm_g), gla_w_a2=(gla_w_a2, m_gla_w_a2, v_gla_w_a2),
        gla_b_a=(gla_b_a, m_gla_b_a, v_gla_b_a), gla_norm_g=(gla_norm_g, m_gla_norm_g, v_gla_norm_g),
        ffn2_norm_g=(ffn2_norm_g, m_ffn2_norm_g, v_ffn2_norm_g), final_norm_g=(final_norm_g, m_final_norm_g, v_final_norm_g))
    grads, updates = {}, {}

    def run_adam(name, names, grad_2d, n_blocks):
        items = [(grad_2d[nm],) + tuple(to_2d(nm, a) for a in params[nm]) for nm in names]
        res, _ = _adamw_group(name, items, n_blocks)
        for nm, r in zip(names, res):
            grads[nm] = from_2d(nm, grad_2d[nm])
            updates[nm] = tuple(from_2d(nm, a) for a in r)
        return res

    grads_a = {"ffn2_w_gate": sums_a[0], "ffn2_w_up": sums_a[1], "ffn2_w_down": sums_a[2], "w_out": sums_a[4]}
    run_adam("adamw_w_in", ["w_in"], {"w_in": sums_a[3]}, 1)
    done_a = run_adam("adamw_a", list(grads_a), grads_a, 4)[0][0]
    sums_b, _ = _owner_sums("sum_b", [arrived(nm, done_a) for nm in ("wg1", "wu1", "wd1")], owner)
    grads_b = {"ffn1_w_gate": sums_b[0], "ffn1_w_up": sums_b[1], "ffn1_w_down": sums_b[2]}
    run_adam("adamw_b", list(grads_b), grads_b, 4)

    total = _sum_devices("sum_small", gathered.reshape(N_DEV * rows, 128), rows).reshape(-1)
    sizes = [small[k].size for k in SMALL_ORDER] + [GATE_RANK * QK_W, 128]
    offs = [0]
    for s in sizes:
        offs.append(offs[-1] + s)
    pieces = [total[offs[i]:offs[i + 1]] for i in range(len(sizes))]
    g_small = {k: pieces[i].reshape(small[k].shape) for i, k in enumerate(SMALL_ORDER)}
    g_wa2_full = pieces[len(SMALL_ORDER)].reshape(GATE_RANK, QK_W)
    g_wa2 = lax.dynamic_slice(g_wa2_full, (0, dev * ab), (GATE_RANK, ab))
    loss_total = pieces[len(SMALL_ORDER) + 1][0]

    small_grads = {"ffn1_norm_g": g_small["ffn1"], "mix_norm_g": g_small["mix"], "ret_norm_g": g_small["ret"],
                   "gla_w_a2": g_wa2, "gla_b_a": g_small["b_a"], "gla_norm_g": g_small["gla"],
                   "ffn2_norm_g": g_small["ffn2"], "final_norm_g": g_small["final"]}
    run_adam("adamw_small", list(small_grads), small_grads, 1)

    order = ("ffn1_norm_g", "ffn1_w_gate", "ffn1_w_up", "ffn1_w_down", "mix_norm_g", "w_in", "ret_norm_g", "gla_w_a2",
             "gla_b_a", "gla_norm_g", "w_out", "ffn2_norm_g", "ffn2_w_gate", "ffn2_w_up", "ffn2_w_down", "final_norm_g")
    return (loss_total, dx0[None], *[grads[nm] for nm in order], *[updates[nm][0] for nm in order],
            *[updates[nm][1] for nm in order], *[updates[nm][2] for nm in order])
```

```python
import functools
import math

import jax
import jax.numpy as jnp
from jax import lax
from jax.experimental import pallas as pl
from jax.experimental.pallas import tpu as pltpu

F32 = jnp.float32
BF16 = jnp.bfloat16
MESH = pl.DeviceIdType.MESH
HBM = pl.BlockSpec(memory_space=pltpu.HBM)

N_DEV = 8
RMS_EPS = 1e-6
ROPE_BASE = 10000.0
HEADS = 4
DK = 64
DV = 128
QK_W = HEADS * DK
V_W = HEADS * DV
GATE_RANK = 16
GATE_NORM = 16.0
CHUNK = 64
SUPER = 256
OUT_PROJ_GROUP = 4
PROJ_W = 3200
C_RQ, C_RK, C_RV, C_RG, C_GQ, C_GK, C_GV, C_GG, C_GL = 0, 256, 512, 1024, 1536, 1792, 2048, 2560, 3072
GL_W = PROJ_W - C_GL
ADAM_LR, ADAM_B1, ADAM_B2, ADAM_EPS, ADAM_WD, ADAM_STEP = 0.001, 0.9, 0.999, 1e-08, 0.01, 10
VMEM_LIMIT_V7X = 52 * 1024 * 1024


def _cparams(**kw):
    return pltpu.CompilerParams(vmem_limit_bytes=VMEM_LIMIT_V7X, **kw)


def _dot(a, b, form, precision=None):
    dims = {"nn": (((1,), (0,)), ((), ())), "nt": (((1,), (1,)), ((), ())), "tn": (((0,), (0,)), ((), ()))}[form]
    return lax.dot_general(a, b, dims, preferred_element_type=F32, precision=precision)


def _sigmoid(x):
    return 1.0 / (1.0 + jnp.exp(-x))


def _coords():
    return lax.axis_index("x"), lax.axis_index("y"), lax.axis_index("c")


class _NoComm:
    inputs, out_shapes, scratch = (), (), ()


class _AllGather:
    def __init__(self, arrays, kinds):
        self.inputs = tuple(arrays)
        self.kinds = tuple(kinds)
        n = len(arrays)
        self.out_shapes = tuple(
            jax.ShapeDtypeStruct((a.shape[0], N_DEV) + a.shape[1:] if k == "stack" else (N_DEV,) + a.shape, a.dtype)
            for a, k in zip(arrays, kinds))
        self.scratch = (pltpu.SemaphoreType.DMA((n, 7)), pltpu.SemaphoreType.DMA((n, 7)),
                        pltpu.SemaphoreType.DMA((n,)))

    def _ctx(self, srcs, outs, sems):
        send_sems, recv_sems, local_sems = sems
        x, y, c = _coords()
        me, sibling = (x, y, c), (x, y, 1 - c)
        chips = [(1 - x, y), (x, 1 - y), (1 - x, 1 - y)]

        def blk(m, dev):
            k = 4 * dev[0] + 2 * dev[1] + dev[2]
            return outs[m].at[:, k] if self.kinds[m] == "stack" else outs[m].at[k]

        def copy(m, s, block, to, src=None):
            return pltpu.make_async_remote_copy(
                src_ref=blk(m, block) if src is None else src, dst_ref=blk(m, block),
                send_sem=send_sems.at[m, s], recv_sem=recv_sems.at[m, s], device_id=to, device_id_type=MESH)

        def mine(m):
            return pltpu.make_async_copy(srcs[m], blk(m, me), local_sems.at[m])

        def first(m):
            return [copy(m, 0, me, sibling, src=srcs[m])] + [
                copy(m, 1 + j, me, (*chip, c), src=srcs[m]) for j, chip in enumerate(chips)]

        return me, sibling, chips, c, copy, mine, first

    def start(self, srcs, outs, sems):
        me, sibling, chips, c, copy, mine, first = self._ctx(srcs, outs, sems)
        for m in range(len(srcs)):
            mine(m).start()
            for cp in first(m):
                cp.start()

    def mid(self, srcs, outs, sems):
        me, sibling, chips, c, copy, mine, first = self._ctx(srcs, outs, sems)
        for j, chip in enumerate(chips):
            for m in range(len(srcs)):
                copy(m, 1 + j, (*chip, c), me).wait_recv()
                copy(m, 4 + j, (*chip, c), sibling).start()

    def finish(self, srcs, outs, sems):
        me, sibling, chips, c, copy, mine, first = self._ctx(srcs, outs, sems)
        for m in range(len(srcs)):
            copy(m, 0, sibling, me).wait_recv()
            for j, chip in enumerate(chips):
                copy(m, 4 + j, (*chip, 1 - c), me).wait_recv()
            for cp in first(m):
                cp.wait_send()
            for j, chip in enumerate(chips):
                copy(m, 4 + j, (*chip, c), sibling).wait_send()
            mine(m).wait()


RELATIONS = ((0, 0, 1), (1, 0, 0), (0, 1, 0), (1, 1, 0), (1, 0, 1), (0, 1, 1), (1, 1, 1))
SEM = pl.BlockSpec(memory_space=pltpu.SEMAPHORE)
SPLIT_PARAMS = dict(has_side_effects=pltpu.SideEffectType.DATAFLOW_SIDE_EFFECTING)


def _owner_copies(grad_ref, land_ref, send_sems, recv_sems):
    x, y, c = _coords()
    copies = []
    for s, (fx, fy, fc) in enumerate(RELATIONS):
        px = 1 - x if fx else x
        py = 1 - y if fy else y
        pc = 1 - c if fc else c
        copies.append(pltpu.make_async_remote_copy(
            src_ref=grad_ref.at[2 * px + py, pc], dst_ref=land_ref.at[s], send_sem=send_sems.at[s],
            recv_sem=recv_sems.at[s], device_id=(px, py, pc), device_id_type=MESH))
    return copies


def _send_to_owners(name, grad):
    n = len(RELATIONS)
    land_shape = (n,) + grad.shape[2:]

    def body(g_ref, land_ref, send_sems, recv_sems, g_thru, land_thru, token):
        for cp in _owner_copies(g_ref, land_ref, send_sems, recv_sems):
            cp.start()
        token[...] = jnp.zeros_like(token)

    return pl.pallas_call(
        body, name=name,
        out_shape=(pltpu.SemaphoreType.DMA((n,)), pltpu.SemaphoreType.DMA((n,)), pltpu.HBM(grad.shape, grad.dtype),
                   pltpu.HBM(land_shape, grad.dtype), jax.ShapeDtypeStruct((8, 128), F32)),
        in_specs=(HBM, HBM), out_specs=(SEM, SEM, HBM, HBM, pl.BlockSpec(memory_space=pltpu.VMEM)),
        input_output_aliases={0: 2, 1: 3}, compiler_params=pltpu.CompilerParams(**SPLIT_PARAMS),
    )(pltpu.with_memory_space_constraint(grad, pltpu.HBM),
      pltpu.with_memory_space_constraint(lax.empty(land_shape, grad.dtype), pltpu.HBM))


def _await_owners(name, started, after):
    send_sems, recv_sems, g_thru, land_thru, _ = started

    def body(g_ref, land_ref, send_sems, recv_sems, after_ref, g_out, land_out):
        for cp in _owner_copies(g_ref, land_ref, send_sems, recv_sems):
            cp.wait_send()
            cp.wait_recv()

    return pl.pallas_call(
        body, name=name, out_shape=(pltpu.HBM(g_thru.shape, g_thru.dtype), pltpu.HBM(land_thru.shape, land_thru.dtype)),
        in_specs=(HBM, HBM, SEM, SEM, pl.BlockSpec(memory_space=pl.ANY)), out_specs=(HBM, HBM),
        input_output_aliases={0: 0, 1: 1}, compiler_params=pltpu.CompilerParams(**SPLIT_PARAMS),
    )(g_thru, land_thru, send_sems, recv_sems, after)


def _call(name, main, grid, in_specs, out_specs, out_shape, args, scratch=(), comm=None, prefetch=None, after=None):
    comm = comm or _NoComm()
    n_main = len(in_specs)
    if after is not None:
        in_specs = list(in_specs) + [pl.BlockSpec(after.shape, lambda *_: (0,) * after.ndim)]
        args = tuple(args) + (after,)
    counts = [len(in_specs), len(comm.inputs), len(out_shape), len(comm.out_shapes), len(scratch), len(comm.scratch)]
    n_steps = math.prod(grid)
    hosted = bool(comm.inputs)

    def body(*refs):
        if prefetch is not None:
            refs = refs[1:]
        parts, at = [], 0
        for n in counts:
            parts.append(refs[at:at + n])
            at += n
        ins, c_in, outs, c_out, scr, c_scr = parts
        ins = ins[:n_main]
        step = pl.program_id(0)
        for d in range(1, len(grid)):
            step = step * grid[d] + pl.program_id(d)
        if hosted:
            @pl.when(step == 0)
            def _():
                comm.start(c_in, c_out, c_scr)
        main(ins, outs, scr)
        if hosted:
            @pl.when(step == max(n_steps - 2, 0))
            def _():
                comm.mid(c_in, c_out, c_scr)

            @pl.when(step == n_steps - 1)
            def _():
                comm.finish(c_in, c_out, c_scr)

    all_in = list(in_specs) + [HBM] * counts[1]
    all_out = list(out_specs) + [HBM] * counts[3]
    all_scratch = list(scratch) + list(comm.scratch)
    shapes = list(out_shape) + list(comm.out_shapes)
    if prefetch is None:
        res = pl.pallas_call(body, name=name, grid=grid, in_specs=all_in, out_specs=all_out, out_shape=shapes,
                             scratch_shapes=all_scratch, compiler_params=_cparams())(*args, *comm.inputs)
    else:
        res = pl.pallas_call(
            body, name=name, out_shape=shapes,
            grid_spec=pltpu.PrefetchScalarGridSpec(num_scalar_prefetch=1, grid=grid, in_specs=all_in,
                                                   out_specs=all_out, scratch_shapes=all_scratch),
            compiler_params=_cparams())(prefetch, *args, *comm.inputs)
    return res[:counts[2]], res[counts[2]:]


def _rms_fwd(name, x, g, comm=None):
    T, D = x.shape
    tm = min(T, 512)

    def main(ins, outs, scr):
        x_ref, g_ref = ins
        xv = x_ref[...]
        r = lax.rsqrt(jnp.mean(xv * xv, axis=-1, keepdims=True) + RMS_EPS)
        outs[0][...] = (xv * r * g_ref[...]).astype(outs[0].dtype)

    tile = pl.BlockSpec((tm, D), lambda i: (i, 0))
    (h,), extra = _call(name, main, (T // tm,), [tile, pl.BlockSpec((1, D), lambda i: (0, 0))], [tile],
                        [jax.ShapeDtypeStruct((T, D), BF16)], (x, g), comm=comm)
    return h, extra


def _final_loss_epilogue(scale, out_scale):
    def ep(acc, ex, outs):
        res_ref, g_ref, t_ref = ex
        dx_ref, dxb_ref, dg_ref, loss_ref = outs
        n = acc.shape[-1]
        xv = res_ref[...] + scale * acc
        r = lax.rsqrt(jnp.mean(xv * xv, axis=-1, keepdims=True) + RMS_EPS)
        xhat = xv * r
        err = xhat * g_ref[...] - t_ref[...]

        @pl.when(pl.program_id(0) == 0)
        def _():
            dg_ref[...] = jnp.zeros_like(dg_ref)
            loss_ref[...] = jnp.zeros_like(loss_ref)

        loss_ref[...] += jnp.broadcast_to(jnp.sum(err * err) * (0.5 / n), loss_ref.shape)
        dy = err * (1.0 / n)
        dg_ref[...] += jnp.sum(dy * xhat, axis=0, keepdims=True)
        dxhat = dy * g_ref[...]
        dx = r * (dxhat - xhat * jnp.mean(dxhat * xhat, axis=-1, keepdims=True))
        dx_ref[...] = dx
        dxb_ref[...] = (out_scale * dx).astype(dxb_ref.dtype)
    return ep


def _mm_nstream(name, a, ws, w_sel, w_form, comps, out_dtypes, epilogue, cn, rows=1024, comm=None, after=None):
    T, K = a.shape
    N = ws[0].shape[1]
    rows = min(rows, T)
    assert N % cn == 0 and T % rows == 0
    n_w, n_c = len(ws), len(comps)

    def main(ins, outs, scr):
        a_ref = ins[0]
        w_refs = ins[1:1 + n_w]
        c_refs = ins[1 + n_w:1 + n_w + n_c]

        for r in range(T // rows):
            sl = slice(r * rows, (r + 1) * rows)
            a_blk = a_ref[sl, :]
            dots = [_dot(a_blk, w_ref[...], w_form) for w_ref in w_refs]
            res = epilogue(dots, [c_ref[sl, :] for c_ref in c_refs])
            for o_ref, o in zip(outs, res):
                o_ref[sl, :] = o.astype(o_ref.dtype)

    if w_form == "nt":
        w_specs = [pl.BlockSpec((None, cn, K), functools.partial(lambda j, s: (s, j, 0), s=s)) for s in w_sel]
    else:
        w_specs = [pl.BlockSpec((K, cn), lambda j: (0, j)) for _ in ws]
    chunk = pl.BlockSpec((T, cn), lambda j: (0, j))
    return _call(name, main, (N // cn,), [pl.BlockSpec((T, K), lambda j: (0, 0))] + w_specs + [chunk] * n_c,
                 [chunk] * len(out_dtypes), [jax.ShapeDtypeStruct((T, N), dt) for dt in out_dtypes],
                 (a, *ws, *comps), comm=comm, after=after)


def _mm_mstream(name, as_, ws, w_sel, w_form, extras, outs_desc, epilogue, tm=512, comm=None, after=None):
    T = as_[0].shape[0]
    tm = min(tm, T)
    n_a = len(as_)
    w_shapes = [w.shape[-2:] for w in ws]
    N = w_shapes[0][1] if w_form == "nn" else w_shapes[0][0]

    def main(ins, outs, scr):
        a_refs = ins[:n_a]
        w_refs = ins[n_a:2 * n_a]
        acc = None
        for a_ref, w_ref in zip(a_refs, w_refs):
            d = _dot(a_ref[...], w_ref[...], w_form)
            acc = d if acc is None else acc + d
        epilogue(acc, ins[2 * n_a:], outs)

    kind_spec = {"tile": pl.BlockSpec((tm, N), lambda i: (i, 0)), "vec": pl.BlockSpec((1, N), lambda i: (0, 0))}
    kind_shape = {"tile": (T, N), "vec": (1, N)}
    a_specs = [pl.BlockSpec((tm, a.shape[1]), lambda i: (i, 0)) for a in as_]
    w_specs = []
    for w, s in zip(ws, w_sel):
        if w.ndim == 3:
            w_specs.append(pl.BlockSpec((None,) + tuple(w.shape[1:]), functools.partial(lambda i, s: (s, 0, 0), s=s),
                                        pipeline_mode=pl.Buffered(1)))
        else:
            w_specs.append(pl.BlockSpec(tuple(w.shape), lambda i: (0, 0), pipeline_mode=pl.Buffered(1)))
    args = list(as_) + list(ws) + [e for e, _ in extras]
    return _call(name, main, (T // tm,), a_specs + w_specs + [kind_spec[k] for _, k in extras],
                 [kind_spec[k] for _, k in outs_desc],
                 [jax.ShapeDtypeStruct(kind_shape[k], dt) for dt, k in outs_desc], args, comm=comm, after=after)


def _residual_rms_epilogue(scale):
    def ep(acc, ex, outs):
        xv = ex[0][...] + scale * acc
        outs[0][...] = xv
        r = lax.rsqrt(jnp.mean(xv * xv, axis=-1, keepdims=True) + RMS_EPS)
        outs[1][...] = (xv * r * ex[1][...]).astype(outs[1].dtype)
    return ep


def _rms_bwd_epilogue(out_scale):
    def ep(acc, ex, outs):
        x_ref, g_ref, dres_ref = ex
        dx_ref, dxb_ref, dg_ref = outs
        xv = x_ref[...]
        r = lax.rsqrt(jnp.mean(xv * xv, axis=-1, keepdims=True) + RMS_EPS)
        xhat = xv * r

        @pl.when(pl.program_id(0) == 0)
        def _():
            dg_ref[...] = jnp.zeros_like(dg_ref)

        dg_ref[...] += jnp.sum(acc * xhat, axis=0, keepdims=True)
        dxhat = acc * g_ref[...]
        dx = r * (dxhat - xhat * jnp.mean(dxhat * xhat, axis=-1, keepdims=True)) + dres_ref[...]
        dx_ref[...] = dx
        dxb_ref[...] = (out_scale * dx).astype(dxb_ref.dtype)
    return ep


def _mm_tn(name, a, b, tmo, tno, out_dtype, tk=1024, comm=None, after=None):
    T, Ma = a.shape
    Nb = b.shape[1]
    tk = min(tk, T)
    nk = T // tk

    def main(ins, outs, scr):
        a_ref, b_ref = ins
        (acc_ref,) = scr
        k = pl.program_id(2)

        @pl.when(k == 0)
        def _():
            acc_ref[...] = jnp.zeros_like(acc_ref)

        acc_ref[...] += _dot(a_ref[...], b_ref[...], "tn")

        @pl.when(k == nk - 1)
        def _():
            outs[0][...] = acc_ref[...].astype(outs[0].dtype)

    (out,), extra = _call(
        name, main, (Ma // tmo, Nb // tno, nk),
        [pl.BlockSpec((tk, tmo), lambda i, j, k: (k, i)), pl.BlockSpec((tk, tno), lambda i, j, k: (k, j))],
        [pl.BlockSpec((tmo, tno), lambda i, j, k: (i, j))], [jax.ShapeDtypeStruct((Ma, Nb), out_dtype)],
        (a, b), scratch=[pltpu.VMEM((tmo, tno), F32)], comm=comm, after=after)
    return out, extra


def _swiglu_parts(g, u):
    s = _sigmoid(g)
    silu = g * s
    return [u * (s + silu * (1.0 - s)), silu, silu * u]


def _silu_mul_epilogue(dots, comps):
    g, u = dots
    return _swiglu_parts(g, u)


def _gate_parts_epilogue(dots, comps):
    (g,) = dots
    s = _sigmoid(g)
    silu = g * s
    return [s + silu * (1.0 - s), silu]


def _up_act_epilogue(dots, comps):
    (u,) = dots
    return [u * comps[0].astype(F32), u * comps[1].astype(F32)]


def _dact_epilogue(dots, comps):
    dact = dots[0].astype(BF16)
    return [dact * comps[0], dact * comps[1]]


def _identity_epilogue(dots, comps):
    return list(dots)


def _swap_halves(x):
    lane = lax.broadcasted_iota(jnp.int32, x.shape, 1)
    first = (lane % DK) < (DK // 2)
    return jnp.where(first, pltpu.roll(x, 128 - DK // 2, 1), pltpu.roll(x, DK // 2, 1))


def _rotary(t, cos, sin_signed):
    halves = []
    for p in range(QK_W // 128):
        th = t[:, 128 * p:128 * (p + 1)]
        halves.append(th * cos + _swap_halves(th) * sin_signed)
    return jnp.concatenate(halves, axis=1)


def _rotary_transposed(d, cos, sin_signed):
    halves = []
    for p in range(QK_W // 128):
        dh = d[:, 128 * p:128 * (p + 1)]
        halves.append(dh * cos + _swap_halves(dh * sin_signed))
    return jnp.concatenate(halves, axis=1)


def _log_sigmoid(x):
    return jnp.minimum(x, 0.0) - jnp.log(1.0 + jnp.exp(-jnp.abs(x)))


def _tri_sum(mask, x):
    tri = mask.astype(BF16)
    hi = x.astype(BF16)
    rest = x - hi.astype(F32)
    mid = rest.astype(BF16)
    lo = (rest - mid.astype(F32)).astype(BF16)
    return _dot(tri, hi, "nn") + _dot(tri, mid, "nn") + _dot(tri, lo, "nn")


def _attn_masks():
    row = lax.broadcasted_iota(jnp.int32, (SUPER, SUPER), 0)
    col = lax.broadcasted_iota(jnp.int32, (SUPER, SUPER), 1)
    same = (row // CHUNK) == (col // CHUNK)
    return row, col, same


def _group_inputs(grp, pr, cos, sin_signed, lg, wa2, ba):
    seg = lambda lo, width: pr[:, lo:lo + width].astype(F32)
    if grp == 0:
        q = _rotary(seg(C_RQ, QK_W), cos, sin_signed)
        k = _rotary(seg(C_RK, QK_W), cos, sin_signed) * (DK ** -0.5)
        v = pr[:, C_RV:C_RV + V_W]
        gate = seg(C_RG, V_W)
        pos = lax.broadcasted_iota(jnp.int32, (SUPER, QK_W), 0).astype(F32) + 1.0
        return q, k, v, gate, pos * lg, None, None
    q = seg(C_GQ, QK_W) * (DK ** -0.5)
    k = seg(C_GK, QK_W)
    v = pr[:, C_GV:C_GV + V_W]
    gate = seg(C_GG, V_W)
    glow = pr[:, C_GL:C_GL + GL_W]
    logit = _dot(glow.astype(BF16), wa2.astype(BF16), "nn") + ba
    la = _log_sigmoid(logit) * (1.0 / GATE_NORM)
    row, col, _ = _attn_masks()
    b_cum = _tri_sum(col <= row, la)
    return q, k, v, gate, b_cum, glow, logit


def _decay_factors(q, k, b_cum):
    c = b_cum[SUPER // 2 - 1:SUPER // 2, :]
    bl = b_cum[SUPER - 1:SUPER, :]
    e1 = jnp.exp(b_cum - c)
    e2 = jnp.exp(c - b_cum)
    e_b = jnp.exp(b_cum)
    e_l = jnp.exp(bl - b_cum)
    return dict(e1=e1, e2=e2, eb=e_b, el=e_l, ebl=jnp.exp(bl),
                qp=q * e1, qm=q * e2, kp=k * e1, km=k * e2, qs=q * e_b, kl=k * e_l)


def _state_block_mask():
    r = lax.broadcasted_iota(jnp.int32, (V_W, QK_W), 0)
    c = lax.broadcasted_iota(jnp.int32, (V_W, QK_W), 1)
    return (r // DV) == (c // DK)


def _attn_fwd(proj, cos, sin_signed, lg, wa2p, ba, gn_ret, gn_gla, x_res, w_out, g_next, comm=None):
    T = proj.shape[0]
    n_s = T // SUPER
    D = x_res.shape[1]
    group = math.gcd(n_s, OUT_PROJ_GROUP)

    def main(ins, outs, scr):
        pr_ref, cos_ref, sin_ref, lg_ref, wa2_ref, ba_ref, gr_ref, gg_ref, xres_ref, wout_ref, gnext_ref = ins
        o_ref, y_ref, st_ref, x_ref, h_ref = outs
        s_ref, ys_ref = scr
        i = pl.program_id(0)

        @pl.when(i == 0)
        def _():
            s_ref[...] = jnp.zeros_like(s_ref)

        pr = pr_ref
        row, col, same = _attn_masks()
        m1 = col <= row
        m2 = jnp.logical_and(col > row, same)
        lane = lax.broadcasted_iota(jnp.int32, (1, QK_W), 1)
        blockmask = _state_block_mask()
        for grp in range(2):
            q, k, v, gate, b_cum, _, _ = _group_inputs(grp, pr, cos_ref[...], sin_ref[...], lg_ref[...],
                                                      wa2_ref[...], ba_ref[...])
            f = _decay_factors(q, k, b_cum)
            gn = gr_ref[...] if grp == 0 else gg_ref[...]
            s_prev = s_ref[grp]
            st_ref[0, grp] = s_prev
            o_inter = _dot(f["qs"].astype(BF16), s_prev.astype(BF16), "nt")
            kmb = f["km"].astype(BF16)
            kpb = f["kp"].astype(BF16)
            vb = v.astype(BF16)
            heads = [(lane // DK) == h for h in range(HEADS)]
            a1_all = _dot(jnp.concatenate([jnp.where(hm, f["qp"], 0.0).astype(BF16) for hm in heads], axis=0), kmb, "nt")
            a2_all = _dot(jnp.concatenate([jnp.where(hm, f["qm"], 0.0).astype(BF16) for hm in heads], axis=0), kpb, "nt")
            for h in range(HEADS):
                a1 = a1_all[h * SUPER:(h + 1) * SUPER]
                a2 = a2_all[h * SUPER:(h + 1) * SUPER]
                a = jnp.where(m1, a1, jnp.where(m2, a2, 0.0))
                lo = grp * V_W + h * DV
                o_h = _dot(a.astype(BF16), vb[:, h * DV:(h + 1) * DV], "nn") + o_inter[:, h * DV:(h + 1) * DV]
                o_ref[:, lo:lo + DV] = o_h
                r = lax.rsqrt(jnp.mean(o_h * o_h, axis=-1, keepdims=True) + RMS_EPS)
                gte = gate[:, h * DV:(h + 1) * DV]
                y = o_h * r * gn[:, h * DV:(h + 1) * DV] * (gte * _sigmoid(gte))
                y_ref[:, lo:lo + DV] = y.astype(y_ref.dtype)
            upd = _dot(vb, f["kl"].astype(BF16), "tn")
            s_ref[grp] = s_prev * f["ebl"] + jnp.where(blockmask, upd, 0.0)
        sub = i % group
        ys_ref[pl.ds(pl.multiple_of(sub * SUPER, SUPER), SUPER), :] = y_ref[...]

        @pl.when(sub == group - 1)
        def _():
            xv = xres_ref[...] + _dot(ys_ref[...], wout_ref[...], "nn")
            x_ref[...] = xv
            r = lax.rsqrt(jnp.mean(xv * xv, axis=-1, keepdims=True) + RMS_EPS)
            h_ref[...] = (xv * r * gnext_ref[...]).astype(h_ref.dtype)

    const = lambda shape: pl.BlockSpec(shape, lambda i: tuple(0 for _ in shape))
    rows = lambda w: pl.BlockSpec((SUPER, w), lambda i: (i, 0))
    grouped = pl.BlockSpec((group * SUPER, D), lambda i: (i // group, 0))
    return _call(
        "attn_fwd", main, (n_s,),
        [rows(PROJ_W), rows(128), rows(128),
         const((1, QK_W)), const((GL_W, QK_W)), const((1, QK_W)), const((1, V_W)), const((1, V_W)),
         grouped, const((2 * V_W, D)), const((1, D))],
        [rows(2 * V_W), rows(2 * V_W), pl.BlockSpec((1, 2, V_W, QK_W), lambda i: (i, 0, 0, 0)), grouped, grouped],
        [jax.ShapeDtypeStruct((T, 2 * V_W), F32), jax.ShapeDtypeStruct((T, 2 * V_W), BF16),
         jax.ShapeDtypeStruct((n_s, 2, V_W, QK_W), F32), jax.ShapeDtypeStruct((T, D), F32),
         jax.ShapeDtypeStruct((T, D), BF16)],
        (proj, cos, sin_signed, lg, wa2p, ba, gn_ret, gn_gla, x_res, w_out, g_next),
        scratch=[pltpu.VMEM((2, V_W, QK_W), F32), pltpu.VMEM((group * SUPER, 2 * V_W), BF16)], comm=comm)


def _attn_bwd(proj, cos, sin_signed, lg, wa2p, ba, gn_ret, gn_gla, o, dx, w_out, states, comm=None, after=None):
    T = proj.shape[0]
    n_s = T // SUPER
    D = dx.shape[1]
    group = math.gcd(n_s, OUT_PROJ_GROUP)

    def main(ins, outs, scr):
        pr_ref, cos_ref, sin_ref, lg_ref, wa2_ref, ba_ref, gr_ref, gg_ref, o_ref, dx_ref, wout_ref, st_ref = ins
        dp_ref, dgr_ref, dgg_ref, dba_ref, dwa_ref = outs
        (ds_ref, dys_ref) = scr
        i = pl.program_id(0)

        @pl.when(i % group == 0)
        def _():
            dys_ref[...] = _dot(dx_ref[...], wout_ref[...], "nt")

        dy_ref = dys_ref.at[pl.ds(pl.multiple_of((group - 1 - i % group) * SUPER, SUPER), SUPER), :]

        @pl.when(i == 0)
        def _():
            ds_ref[...] = jnp.zeros_like(ds_ref)
            dgr_ref[...] = jnp.zeros_like(dgr_ref)
            dgg_ref[...] = jnp.zeros_like(dgg_ref)
            dba_ref[...] = jnp.zeros_like(dba_ref)
            dwa_ref[...] = jnp.zeros_like(dwa_ref)

        pr = pr_ref
        cos = cos_ref[...]
        sin_signed = sin_ref[...]
        row, col, same = _attn_masks()
        m1 = col <= row
        m2 = jnp.logical_and(col > row, same)
        m1t = row <= col
        m2t = jnp.logical_and(row > col, same)
        lane = lax.broadcasted_iota(jnp.int32, (1, QK_W), 1)
        blockmask = _state_block_mask()
        for grp in range(2):
            q, k, v, gate, b_cum, glow, logit = _group_inputs(grp, pr, cos, sin_signed, lg_ref[...],
                                                              wa2_ref[...], ba_ref[...])
            f = _decay_factors(q, k, b_cum)
            gn = gr_ref[...] if grp == 0 else gg_ref[...]
            dgn_ref = dgr_ref if grp == 0 else dgg_ref
            do_parts, dgate_parts, dgn_parts = [], [], []
            for h in range(HEADS):
                lo = grp * V_W + h * DV
                o_h = o_ref[:, lo:lo + DV]
                r = lax.rsqrt(jnp.mean(o_h * o_h, axis=-1, keepdims=True) + RMS_EPS)
                n = o_h * r
                gte = gate[:, h * DV:(h + 1) * DV]
                sg = _sigmoid(gte)
                dy_h = dy_ref[:, lo:lo + DV]
                gn_h = gn[:, h * DV:(h + 1) * DV]
                dgate_parts.append(dy_h * n * gn_h * (sg * (1.0 + gte * (1.0 - sg))))
                dz = dy_h * (gte * sg)
                dgn_parts.append(jnp.sum(dz * n, axis=0, keepdims=True))
                dn = dz * gn_h
                do_parts.append(r * (dn - n * jnp.mean(dn * n, axis=-1, keepdims=True)))
            dgn_ref[...] += jnp.concatenate(dgn_parts, axis=1)
            dgate = jnp.concatenate(dgate_parts, axis=1)
            do = jnp.concatenate(do_parts, axis=1)
            dob = do.astype(BF16)
            vb = v.astype(BF16)
            s_prev = st_ref[0, grp]
            ds_new = ds_ref[grp]
            dsb = ds_new.astype(BF16)
            qpb, qmb = f["qp"].astype(BF16), f["qm"].astype(BF16)
            kpb, kmb = f["kp"].astype(BF16), f["km"].astype(BF16)
            dqp = jnp.zeros((SUPER, QK_W), F32)
            dqm = jnp.zeros((SUPER, QK_W), F32)
            dkp = jnp.zeros((SUPER, QK_W), F32)
            dkm = jnp.zeros((SUPER, QK_W), F32)
            dv_parts = []
            heads = [(lane // DK) == h for h in range(HEADS)]
            qp_hs = [jnp.where(hm, f["qp"], 0.0).astype(BF16) for hm in heads]
            qm_hs = [jnp.where(hm, f["qm"], 0.0).astype(BF16) for hm in heads]
            kp_hs = [jnp.where(hm, f["kp"], 0.0).astype(BF16) for hm in heads]
            km_hs = [jnp.where(hm, f["km"], 0.0).astype(BF16) for hm in heads]
            at1_all = _dot(jnp.concatenate(km_hs, axis=0), qpb, "nt")
            at2_all = _dot(jnp.concatenate(kp_hs, axis=0), qmb, "nt")
            for h in range(HEADS):
                qp_h, qm_h, kp_h, km_h = qp_hs[h], qm_hs[h], kp_hs[h], km_hs[h]
                at = jnp.where(m1t, at1_all[h * SUPER:(h + 1) * SUPER],
                               jnp.where(m2t, at2_all[h * SUPER:(h + 1) * SUPER], 0.0))
                do_h = dob[:, h * DV:(h + 1) * DV]
                v_h = vb[:, h * DV:(h + 1) * DV]
                dv_parts.append(_dot(at.astype(BF16), do_h, "nn"))
                da = _dot(do_h, v_h, "nt")
                dat = _dot(v_h, do_h, "nt")
                da1 = jnp.where(m1, da, 0.0).astype(BF16)
                da2 = jnp.where(m2, da, 0.0).astype(BF16)
                da1t = jnp.where(m1t, dat, 0.0).astype(BF16)
                da2t = jnp.where(m2t, dat, 0.0).astype(BF16)
                dqp = dqp + _dot(da1, km_h, "nn")
                dqm = dqm + _dot(da2, kp_h, "nn")
                dkm = dkm + _dot(da1t, qp_h, "nn")
                dkp = dkp + _dot(da2t, qm_h, "nn")
            klb = f["kl"].astype(BF16)
            qsb = f["qs"].astype(BF16)
            dqs = _dot(dob, s_prev.astype(BF16), "nn")
            dkl = _dot(vb, dsb, "nn")
            dv = jnp.concatenate(dv_parts, axis=1) + _dot(klb, dsb, "nt")
            ds_ref[grp] = ds_new * f["ebl"] + jnp.where(blockmask, _dot(dob, qsb, "tn"), 0.0)
            dq = dqp * f["e1"] + dqm * f["e2"] + dqs * f["eb"]
            dk = dkm * f["e2"] + dkp * f["e1"] + dkl * f["el"]
            if grp == 0:
                dq = _rotary_transposed(dq, cos, sin_signed)
                dk = _rotary_transposed(dk * (DK ** -0.5), cos, sin_signed)
                dp_ref[:, C_RQ:C_RQ + QK_W] = dq.astype(dp_ref.dtype)
                dp_ref[:, C_RK:C_RK + QK_W] = dk.astype(dp_ref.dtype)
                dp_ref[:, C_RV:C_RV + V_W] = dv.astype(dp_ref.dtype)
                dp_ref[:, C_RG:C_RG + V_W] = dgate.astype(dp_ref.dtype)
            else:
                dkl_kl = dkl * klb.astype(F32)
                db = (dqp * qpb.astype(F32) - dkm * kmb.astype(F32) - dqm * qmb.astype(F32)
                      + dkp * kpb.astype(F32) + dqs * qsb.astype(F32) - dkl_kl)
                last = (jnp.sum(dkl_kl, axis=0, keepdims=True)
                        + f["ebl"] * jnp.sum(s_prev * ds_new, axis=0, keepdims=True))
                rowq = lax.broadcasted_iota(jnp.int32, (SUPER, QK_W), 0)
                db = db + jnp.where(rowq == SUPER - 1, last, 0.0)
                dla = _tri_sum(col >= row, db)
                dlogit = dla * (1.0 / GATE_NORM) * (1.0 - _sigmoid(logit))
                dlb = dlogit.astype(BF16)
                dglow = _dot(dlb, wa2_ref[...].astype(BF16), "nt")
                dwa_ref[...] += _dot(glow.astype(BF16), dlb, "tn")
                dba_ref[...] += jnp.sum(dlogit, axis=0, keepdims=True)
                dp_ref[:, C_GQ:C_GQ + QK_W] = (dq * (DK ** -0.5)).astype(dp_ref.dtype)
                dp_ref[:, C_GK:C_GK + QK_W] = dk.astype(dp_ref.dtype)
                dp_ref[:, C_GV:C_GV + V_W] = dv.astype(dp_ref.dtype)
                dp_ref[:, C_GG:C_GG + V_W] = dgate.astype(dp_ref.dtype)
                dp_ref[:, C_GL:C_GL + GL_W] = dglow.astype(dp_ref.dtype)

    rev = lambda i: n_s - 1 - i
    const = lambda shape: pl.BlockSpec(shape, lambda i: tuple(0 for _ in shape))
    return _call(
        "attn_bwd", main, (n_s,),
        [pl.BlockSpec((SUPER, PROJ_W), lambda i: (rev(i), 0)),
         pl.BlockSpec((SUPER, 128), lambda i: (rev(i), 0)), pl.BlockSpec((SUPER, 128), lambda i: (rev(i), 0)),
         const((1, QK_W)), const((GL_W, QK_W)), const((1, QK_W)), const((1, V_W)), const((1, V_W)),
         pl.BlockSpec((SUPER, 2 * V_W), lambda i: (rev(i), 0)),
         pl.BlockSpec((group * SUPER, D), lambda i: (rev(i) // group, 0)), const((2 * V_W, D)),
         pl.BlockSpec((1, 2, V_W, QK_W), lambda i: (rev(i), 0, 0, 0))],
        [pl.BlockSpec((SUPER, PROJ_W), lambda i: (rev(i), 0)),
         const((1, V_W)), const((1, V_W)), const((1, QK_W)), const((GL_W, QK_W))],
        [jax.ShapeDtypeStruct((T, PROJ_W), BF16),
         jax.ShapeDtypeStruct((1, V_W), F32), jax.ShapeDtypeStruct((1, V_W), F32),
         jax.ShapeDtypeStruct((1, QK_W), F32), jax.ShapeDtypeStruct((GL_W, QK_W), F32)],
        (proj, cos, sin_signed, lg, wa2p, ba, gn_ret, gn_gla, o, dx, w_out, states),
        scratch=[pltpu.VMEM((2, V_W, QK_W), F32), pltpu.VMEM((group * SUPER, 2 * V_W), F32)], comm=comm, after=after)


def _rotary_tables(T):
    half = DK // 2
    inv = ROPE_BASE ** (-jnp.arange(half, dtype=F32) * 2.0 / DK)
    ang = jnp.arange(T, dtype=F32)[:, None] * inv[None, :]
    cos, sin = jnp.cos(ang), jnp.sin(ang)
    cos_head = jnp.concatenate([cos, cos], axis=1)
    sin_head = jnp.concatenate([-sin, sin], axis=1)
    return jnp.tile(cos_head, (1, 128 // DK)), jnp.tile(sin_head, (1, 128 // DK))


def _sum_devices(name, gathered, m_per):
    def body(g_ref, o_ref):
        acc = g_ref[0:m_per, :]
        for k in range(1, N_DEV):
            acc = acc + g_ref[k * m_per:(k + 1) * m_per, :]
        o_ref[...] = acc

    return pl.pallas_call(body, name=name, out_shape=jax.ShapeDtypeStruct((m_per, 128), F32))(gathered)


def _owner_sums(name, items, owner, comm=None):
    counts = [1 + len(landed) for _, landed in items]

    def main(ins, outs, scr):
        at = 0
        for o_ref, n in zip(outs, counts):
            acc = ins[at][...].astype(F32)
            for l_ref in ins[at + 1:at + n]:
                for j in range(l_ref.shape[0]):
                    acc = acc + l_ref[j].astype(F32)
            o_ref[...] = acc
            at += n

    once = pl.Buffered(1)
    in_specs, out_specs, out_shape, args = [], [], [], []
    for grad, landed in items:
        R, C = grad.shape[-2:]
        in_specs.append(pl.BlockSpec((None, None, R, C), lambda i, s: (s[0], s[1], 0, 0), pipeline_mode=once))
        in_specs += [pl.BlockSpec(tuple(l.shape), lambda i, s: (0, 0, 0), pipeline_mode=once) for l in landed]
        out_specs.append(pl.BlockSpec((R, C), lambda i, s: (0, 0)))
        out_shape.append(jax.ShapeDtypeStruct((R, C), F32))
        args += [grad, *landed]
    return _call(name, main, (1,), in_specs, out_specs, out_shape, args, comm=comm, prefetch=owner)


def _adamw_group(name, items, n_blocks, comm=None):
    n = len(items)

    def main(ins, outs, scr):
        for p in range(n):
            g_ref, w_ref, m_ref, v_ref = ins[4 * p:4 * p + 4]
            d_ref, nm_ref, nv_ref = outs[3 * p:3 * p + 3]
            gv = g_ref[...]
            nm = ADAM_B1 * m_ref[...] + (1.0 - ADAM_B1) * gv
            nv = ADAM_B2 * v_ref[...] + (1.0 - ADAM_B2) * (gv * gv)
            m_hat = nm / (1.0 - ADAM_B1 ** ADAM_STEP)
            v_hat = nv / (1.0 - ADAM_B2 ** ADAM_STEP)
            d_ref[...] = -ADAM_LR * (m_hat / (jnp.sqrt(v_hat) + ADAM_EPS) + ADAM_WD * w_ref[...])
            nm_ref[...] = nm
            nv_ref[...] = nv

    in_specs, out_specs, out_shape, args = [], [], [], []
    for item in items:
        R, C = item[1].shape
        assert R % n_blocks == 0
        spec = pl.BlockSpec((R // n_blocks, C), lambda i: (i, 0))
        in_specs += [spec] * 4
        out_specs += [spec] * 3
        out_shape += [jax.ShapeDtypeStruct((R, C), F32)] * 3
        args += list(item)
    outs, extra = _call(name, main, (n_blocks,), in_specs, out_specs, out_shape, args, comm=comm)
    return [tuple(outs[3 * p:3 * p + 3]) for p in range(n)], extra


SMALL_ORDER = ("ffn1", "mix", "ffn2", "final", "ret", "gla", "b_a")


def kernel(x, ffn1_norm_g, ffn1_w_gate, ffn1_w_up, ffn1_w_down, mix_norm_g, w_in, ret_norm_g, gla_w_a2, gla_b_a, gla_norm_g, w_out, ffn2_norm_g, ffn2_w_gate, ffn2_w_up, ffn2_w_down, final_norm_g, loss_target, m_ffn1_norm_g, m_ffn1_w_gate, m_ffn1_w_up, m_ffn1_w_down, m_mix_norm_g, m_w_in, m_ret_norm_g, m_gla_w_a2, m_gla_b_a, m_gla_norm_g, m_w_out, m_ffn2_norm_g, m_ffn2_w_gate, m_ffn2_w_up, m_ffn2_w_down, m_final_norm_g, v_ffn1_norm_g, v_ffn1_w_gate, v_ffn1_w_up, v_ffn1_w_down, v_mix_norm_g, v_w_in, v_ret_norm_g, v_gla_w_a2, v_gla_b_a, v_gla_norm_g, v_w_out, v_ffn2_norm_g, v_ffn2_w_gate, v_ffn2_w_up, v_ffn2_w_down, v_final_norm_g):
    xi, yi, ci = _coords()
    dev = 4 * xi + 2 * yi + ci
    owner = jnp.stack([2 * xi + yi, ci]).astype(jnp.int32)

    x0, target = x[0], loss_target[0]
    T, D = x0.shape
    fb = ffn1_w_gate.shape[2]
    ib = w_in.shape[2]
    ab = gla_w_a2.shape[2]
    F = N_DEV * fb
    cos, sin_signed = _rotary_tables(T)
    lg = jnp.repeat(jnp.log(1.0 - 2.0 ** (-5.0 - jnp.arange(HEADS, dtype=F32))), DK)[None, :]
    g_final = final_norm_g.reshape(1, D)

    g1_loc = ffn1_w_gate[0].T[None].astype(BF16)
    u1_loc = ffn1_w_up[0].T[None].astype(BF16)
    d1_loc = ffn1_w_down.astype(BF16)
    g2_loc = ffn2_w_gate[0].T[None].astype(BF16)
    u2_loc = ffn2_w_up[0].T[None].astype(BF16)
    d2_loc = ffn2_w_down.astype(BF16)
    in_loc = w_in[0].T.astype(BF16)
    out_loc = w_out[0].astype(BF16)

    h1, (g1,) = _rms_fwd("ffn1_rms", x0, ffn1_norm_g, comm=_AllGather([g1_loc], ["stack"]))
    g1 = g1.reshape(1, F, D)
    (dsl1, sl1), (u1,) = _mm_nstream("ffn1_gate", h1, [g1], [0], "nt", [], [BF16, BF16], _gate_parts_epilogue, cn=256,
                                     comm=_AllGather([u1_loc], ["stack"]))
    u1 = u1.reshape(1, F, D)
    (dsu1, act1), (d1,) = _mm_nstream("ffn1_up", h1, [u1], [0], "nt", [dsl1, sl1], [BF16, BF16],
                                      _up_act_epilogue, cn=256, comm=_AllGather([d1_loc], ["stack"]))
    d1 = d1.reshape(1, F, D)
    f32_tile, bf16_tile, f32_vec = (F32, "tile"), (BF16, "tile"), (F32, "vec")
    (x1, h2), (in_all, a_all) = _mm_mstream(
        "ffn1_down", [act1], [d1], [0], "nn", [(x0, "tile"), (mix_norm_g, "vec")], [f32_tile, bf16_tile],
        _residual_rms_epilogue(0.5), comm=_AllGather([in_loc, gla_w_a2[0]], ["plain", "plain"]))
    w_in_t = jnp.pad(in_all.reshape(1, N_DEV * ib, D), ((0, 0), (0, PROJ_W - N_DEV * ib), (0, 0)))
    wa2 = jnp.transpose(a_all, (1, 0, 2)).reshape(GATE_RANK, N_DEV * ab)
    wa2p = jnp.pad(wa2, ((0, GL_W - GATE_RANK), (0, 0)))

    (proj,), (g2, out_all) = _mm_nstream("mix_proj", h2, [w_in_t], [0], "nt", [], [BF16], _identity_epilogue, cn=640,
                                         comm=_AllGather([g2_loc, out_loc], ["stack", "plain"]))
    w_out_full = out_all.reshape(D, D)
    (o, ymix, states, x2, h3), (u2,) = _attn_fwd(proj, cos, sin_signed, lg, wa2p, gla_b_a, ret_norm_g, gla_norm_g,
                                                 x1, w_out_full, ffn2_norm_g, comm=_AllGather([u2_loc], ["stack"]))
    g2, u2 = g2.reshape(1, F, D), u2.reshape(1, F, D)

    (dsu2, sl2, act2), (d2,) = _mm_nstream(
        "ffn2_up", h3, [g2, u2], [0, 0], "nt", [], [BF16, BF16, BF16], _silu_mul_epilogue, cn=256,
        comm=_AllGather([d2_loc], ["stack"]))
    d2 = d2.reshape(1, F, D)
    (dx3, dy3b, d_final, loss), _ = _mm_mstream(
        "ffn2_down", [act2], [d2], [0], "nn", [(x2, "tile"), (g_final, "vec"), (target, "tile")],
        [f32_tile, bf16_tile, f32_vec, f32_vec], _final_loss_epilogue(0.5, 0.5))

    sent = {}

    def send(nm, grad):
        sent[nm] = _send_to_owners("send_" + nm, grad)
        return sent[nm][4]

    dwd2, _ = _mm_tn("ffn2b_dwd", act2, dy3b, F // 2, D, BF16)
    tok = send("wd2", dwd2.reshape(4, 2, fb, D))
    (dgate2, dup2), _ = _mm_nstream("ffn2b_dact", dy3b, [d2], [0], "nt", [dsu2, sl2], [BF16, BF16],
                                    _dact_epilogue, cn=256, after=tok)
    dwg2, _ = _mm_tn("ffn2b_dwg", dgate2, h3, F // 2, D, BF16)
    tok = send("wg2", dwg2.reshape(4, 2, fb, D))
    dwu2, _ = _mm_tn("ffn2b_dwu", dup2, h3, F // 2, D, BF16, after=tok)
    tok = send("wu2", dwu2.reshape(4, 2, fb, D))
    rms_outs = [f32_tile, bf16_tile, f32_vec]
    (dx2, dx2b, d_g2), _ = _mm_mstream(
        "ffn2b_dh", [dgate2, dup2], [g2, u2], [0, 0], "nn", [(x2, "tile"), (ffn2_norm_g, "vec"), (dx3, "tile")],
        rms_outs, _rms_bwd_epilogue(1.0), after=tok)

    dwout, _ = _mm_tn("mixb_dwout", ymix, dx2b, D, D, BF16)
    tok = send("wout", dwout.reshape(4, 2, D // N_DEV, D))
    (dproj, d_ret, d_gla, d_ba, d_wa2p), _ = _attn_bwd(
        proj, cos, sin_signed, lg, wa2p, gla_b_a, ret_norm_g, gla_norm_g, o, dx2b, w_out_full, states, after=tok)
    dwin_t, _ = _mm_tn("mixb_dwin", dproj, h2, 640, D, BF16, tk=2048)
    tok = send("win", dwin_t[:N_DEV * ib].reshape(4, 2, ib, D))
    (dx1, dy1b, d_gmix), _ = _mm_mstream(
        "mixb_dh", [dproj], [w_in_t], [0], "nn", [(x1, "tile"), (mix_norm_g, "vec"), (dx2, "tile")],
        rms_outs, _rms_bwd_epilogue(0.5), after=tok)

    dwd1, _ = _mm_tn("ffn1b_dwd", act1, dy1b, F // 2, D, BF16)
    tok = send("wd1", dwd1.reshape(4, 2, fb, D))
    (dgate1, dup1), _ = _mm_nstream("ffn1b_dact", dy1b, [d1], [0], "nt", [dsu1, sl1], [BF16, BF16],
                                    _dact_epilogue, cn=256, after=tok)
    dwg1, _ = _mm_tn("ffn1b_dwg", dgate1, h1, F // 2, D, BF16)
    tok = send("wg1", dwg1.reshape(4, 2, fb, D))
    dwu1, _ = _mm_tn("ffn1b_dwu", dup1, h1, F // 2, D, BF16, after=tok)
    tok = send("wu1", dwu1.reshape(4, 2, fb, D))
    (dx0, _, d_g1), _ = _mm_mstream(
        "ffn1b_dh", [dgate1, dup1], [g1, u1], [0, 0], "nn", [(x0, "tile"), (ffn1_norm_g, "vec"), (dx1, "tile")],
        rms_outs, _rms_bwd_epilogue(1.0), after=tok)

    small = dict(ffn1=d_g1, mix=d_gmix, ffn2=d_g2, final=d_final, ret=d_ret, gla=d_gla, b_a=d_ba)
    flat = jnp.concatenate([small[k].reshape(-1) for k in SMALL_ORDER]
                           + [d_wa2p[:GATE_RANK].reshape(-1), loss[0, :128]])
    rows = -(-flat.shape[0] // 128)
    rows = -(-rows // 8) * 8
    packed = jnp.pad(flat, (0, rows * 128 - flat.shape[0])).reshape(rows, 128)

    transposed = ("ffn1_w_gate", "ffn1_w_up", "ffn2_w_gate", "ffn2_w_up", "w_in")

    def to_2d(nm, a):
        if nm in transposed:
            return a[0].T
        return a.reshape((1, a.shape[0]) if a.ndim == 1 else a.shape[-2:])

    def from_2d(nm, a):
        return a.T[None] if nm in transposed else a.reshape(params[nm][0].shape)

    def arrived(nm, after):
        grad, landed = _await_owners("await_" + nm, sent[nm], after)
        return grad, [landed]

    sums_a, (gathered,) = _owner_sums(
        "sum_a", [arrived(nm, dx0) for nm in ("wg2", "wu2", "wd2", "win", "wout")], owner,
        comm=_AllGather([packed], ["plain"]))
    params = dict(
        ffn2_w_gate=(ffn2_w_gate, m_ffn2_w_gate, v_ffn2_w_gate), ffn2_w_up=(ffn2_w_up, m_ffn2_w_up, v_ffn2_w_up),
        ffn2_w_down=(ffn2_w_down, m_ffn2_w_down, v_ffn2_w_down), w_in=(w_in, m_w_in, v_w_in),
        w_out=(w_out, m_w_out, v_w_out), ffn1_w_gate=(ffn1_w_gate, m_ffn1_w_gate, v_ffn1_w_gate),
        ffn1_w_up=(ffn1_w_up, m_ffn1_w_up, v_ffn1_w_up), ffn1_w_down=(ffn1_w_down, m_ffn1_w_down, v_ffn1_w_down),
        ffn1_norm_g=(ffn1_norm_g, m_ffn1_norm_g, v_ffn1_norm_g), mix_norm_g=(mix_norm_g, m_mix_norm_g, v_mix_norm_g),
        ret_norm_g=(ret_norm_g, m_ret_norm_g, v_ret_norm_g), gla_w_a2=(gla_w_a2, m_gla_w_a2, v_gla_w_a2),
        gla_b_a=(gla_b_a, m_gla_b_a, v_gla_b_a), gla_norm_g=(gla_norm_g, m_gla_norm_g, v_gla_norm_g),
        ffn2_norm_g=(ffn2_norm_g, m_ffn2_norm_g, v_ffn2_norm_g), final_norm_g=(final_norm_g, m_final_norm_g, v_final_norm_g))
    grads, updates = {}, {}

    def run_adam(name, names, grad_2d, n_blocks):
        items = [(grad_2d[nm],) + tuple(to_2d(nm, a) for a in params[nm]) for nm in names]
        res, _ = _adamw_group(name, items, n_blocks)
        for nm, r in zip(names, res):
            grads[nm] = from_2d(nm, grad_2d[nm])
            updates[nm] = tuple(from_2d(nm, a) for a in r)
        return res

    grads_a = {"ffn2_w_gate": sums_a[0], "ffn2_w_up": sums_a[1], "ffn2_w_down": sums_a[2], "w_out": sums_a[4]}
    run_adam("adamw_w_in", ["w_in"], {"w_in": sums_a[3]}, 1)
    done_a = run_adam("adamw_a", list(grads_a), grads_a, 4)[0][0]
    sums_b, _ = _owner_sums("sum_b", [arrived(nm, done_a) for nm in ("wg1", "wu1", "wd1")], owner)
    grads_b = {"ffn1_w_gate": sums_b[0], "ffn1_w_up": sums_b[1], "ffn1_w_down": sums_b[2]}
    run_adam("adamw_b", list(grads_b), grads_b, 4)

    total = _sum_devices("sum_small", gathered.reshape(N_DEV * rows, 128), rows).reshape(-1)
    sizes = [small[k].size for k in SMALL_ORDER] + [GATE_RANK * QK_W, 128]
    offs = [0]
    for s in sizes:
        offs.append(offs[-1] + s)
    pieces = [total[offs[i]:offs[i + 1]] for i in range(len(sizes))]
    g_small = {k: pieces[i].reshape(small[k].shape) for i, k in enumerate(SMALL_ORDER)}
    g_wa2_full = pieces[len(SMALL_ORDER)].reshape(GATE_RANK, QK_W)
    g_wa2 = lax.dynamic_slice(g_wa2_full, (0, dev * ab), (GATE_RANK, ab))
    loss_total = pieces[len(SMALL_ORDER) + 1][0]

    small_grads = {"ffn1_norm_g": g_small["ffn1"], "mix_norm_g": g_small["mix"], "ret_norm_g": g_small["ret"],
                   "gla_w_a2": g_wa2, "gla_b_a": g_small["b_a"], "gla_norm_g": g_small["gla"],
                   "ffn2_norm_g": g_small["ffn2"], "final_norm_g": g_small["final"]}
    run_adam("adamw_small", list(small_grads), small_grads, 1)

    order = ("ffn1_norm_g", "ffn1_w_gate", "ffn1_w_up", "ffn1_w_down", "mix_norm_g", "w_in", "ret_norm_g", "gla_w_a2",
             "gla_b_a", "gla_norm_g", "w_out", "ffn2_norm_g", "ffn2_w_gate", "ffn2_w_up", "ffn2_w_down", "final_norm_g")
    return (loss_total, dx0[None], *[grads[nm] for nm in order], *[updates[nm][0] for nm in order],
            *[updates[nm][1] for nm in order], *[updates[nm][2] for nm in order])
```

```python
import functools
import math

import jax
import jax.numpy as jnp
from jax import lax
from jax.experimental import pallas as pl
from jax.experimental.pallas import tpu as pltpu

F32 = jnp.float32
BF16 = jnp.bfloat16
MESH = pl.DeviceIdType.MESH
HBM = pl.BlockSpec(memory_space=pltpu.HBM)

N_DEV = 8
RMS_EPS = 1e-6
ROPE_BASE = 10000.0
HEADS = 4
DK = 64
DV = 128
QK_W = HEADS * DK
V_W = HEADS * DV
GATE_RANK = 16
GATE_NORM = 16.0
CHUNK = 64
SUPER = 256
PROJ_W = 3200
C_RQ, C_RK, C_RV, C_RG, C_GQ, C_GK, C_GV, C_GG, C_GL = 0, 256, 512, 1024, 1536, 1792, 2048, 2560, 3072
GL_W = PROJ_W - C_GL
ADAM_LR, ADAM_B1, ADAM_B2, ADAM_EPS, ADAM_WD, ADAM_STEP = 0.001, 0.9, 0.999, 1e-08, 0.01, 10
VMEM_LIMIT_V7X = 52 * 1024 * 1024


def _cparams(**kw):
    return pltpu.CompilerParams(vmem_limit_bytes=VMEM_LIMIT_V7X, **kw)


def _dot(a, b, form, precision=None):
    dims = {"nn": (((1,), (0,)), ((), ())), "nt": (((1,), (1,)), ((), ())), "tn": (((0,), (0,)), ((), ()))}[form]
    return lax.dot_general(a, b, dims, preferred_element_type=F32, precision=precision)


def _sigmoid(x):
    return 1.0 / (1.0 + jnp.exp(-x))


def _coords():
    return lax.axis_index("x"), lax.axis_index("y"), lax.axis_index("c")


class _NoComm:
    inputs, out_shapes, scratch = (), (), ()


class _AllGather:
    def __init__(self, arrays, kinds):
        self.inputs = tuple(arrays)
        self.kinds = tuple(kinds)
        n = len(arrays)
        self.out_shapes = tuple(
            jax.ShapeDtypeStruct((a.shape[0], N_DEV) + a.shape[1:] if k == "stack" else (N_DEV,) + a.shape, a.dtype)
            for a, k in zip(arrays, kinds))
        self.scratch = (pltpu.SemaphoreType.DMA((n, 7)), pltpu.SemaphoreType.DMA((n, 7)),
                        pltpu.SemaphoreType.DMA((n,)))

    def _ctx(self, srcs, outs, sems):
        send_sems, recv_sems, local_sems = sems
        x, y, c = _coords()
        me, sibling = (x, y, c), (x, y, 1 - c)
        chips = [(1 - x, y), (x, 1 - y), (1 - x, 1 - y)]

        def blk(m, dev):
            k = 4 * dev[0] + 2 * dev[1] + dev[2]
            return outs[m].at[:, k] if self.kinds[m] == "stack" else outs[m].at[k]

        def copy(m, s, block, to, src=None):
            return pltpu.make_async_remote_copy(
                src_ref=blk(m, block) if src is None else src, dst_ref=blk(m, block),
                send_sem=send_sems.at[m, s], recv_sem=recv_sems.at[m, s], device_id=to, device_id_type=MESH)

        def mine(m):
            return pltpu.make_async_copy(srcs[m], blk(m, me), local_sems.at[m])

        def first(m):
            return [copy(m, 0, me, sibling, src=srcs[m])] + [
                copy(m, 1 + j, me, (*chip, c), src=srcs[m]) for j, chip in enumerate(chips)]

        return me, sibling, chips, c, copy, mine, first

    def start(self, srcs, outs, sems):
        me, sibling, chips, c, copy, mine, first = self._ctx(srcs, outs, sems)
        for m in range(len(srcs)):
            mine(m).start()
            for cp in first(m):
                cp.start()

    def mid(self, srcs, outs, sems):
        me, sibling, chips, c, copy, mine, first = self._ctx(srcs, outs, sems)
        for j, chip in enumerate(chips):
            for m in range(len(srcs)):
                copy(m, 1 + j, (*chip, c), me).wait_recv()
                copy(m, 4 + j, (*chip, c), sibling).start()

    def finish(self, srcs, outs, sems):
        me, sibling, chips, c, copy, mine, first = self._ctx(srcs, outs, sems)
        for m in range(len(srcs)):
            copy(m, 0, sibling, me).wait_recv()
            for j, chip in enumerate(chips):
                copy(m, 4 + j, (*chip, 1 - c), me).wait_recv()
            for cp in first(m):
                cp.wait_send()
            for j, chip in enumerate(chips):
                copy(m, 4 + j, (*chip, c), sibling).wait_send()
            mine(m).wait()


RELATIONS = ((0, 0, 1), (1, 0, 0), (0, 1, 0), (1, 1, 0), (1, 0, 1), (0, 1, 1), (1, 1, 1))
SEM = pl.BlockSpec(memory_space=pltpu.SEMAPHORE)
SPLIT_PARAMS = dict(has_side_effects=pltpu.SideEffectType.DATAFLOW_SIDE_EFFECTING)


def _owner_copies(grad_ref, land_ref, send_sems, recv_sems):
    x, y, c = _coords()
    copies = []
    for s, (fx, fy, fc) in enumerate(RELATIONS):
        px = 1 - x if fx else x
        py = 1 - y if fy else y
        pc = 1 - c if fc else c
        copies.append(pltpu.make_async_remote_copy(
            src_ref=grad_ref.at[2 * px + py, pc], dst_ref=land_ref.at[s], send_sem=send_sems.at[s],
            recv_sem=recv_sems.at[s], device_id=(px, py, pc), device_id_type=MESH))
    return copies


def _send_to_owners(name, grad):
    n = len(RELATIONS)
    land_shape = (n,) + grad.shape[2:]

    def body(g_ref, land_ref, send_sems, recv_sems, g_thru, land_thru, token):
        for cp in _owner_copies(g_ref, land_ref, send_sems, recv_sems):
            cp.start()
        token[...] = jnp.zeros_like(token)

    return pl.pallas_call(
        body, name=name,
        out_shape=(pltpu.SemaphoreType.DMA((n,)), pltpu.SemaphoreType.DMA((n,)), pltpu.HBM(grad.shape, grad.dtype),
                   pltpu.HBM(land_shape, grad.dtype), jax.ShapeDtypeStruct((8, 128), F32)),
        in_specs=(HBM, HBM), out_specs=(SEM, SEM, HBM, HBM, pl.BlockSpec(memory_space=pltpu.VMEM)),
        input_output_aliases={0: 2, 1: 3}, compiler_params=pltpu.CompilerParams(**SPLIT_PARAMS),
    )(pltpu.with_memory_space_constraint(grad, pltpu.HBM),
      pltpu.with_memory_space_constraint(lax.empty(land_shape, grad.dtype), pltpu.HBM))


def _await_owners(name, started, after):
    send_sems, recv_sems, g_thru, land_thru, _ = started

    def body(g_ref, land_ref, send_sems, recv_sems, after_ref, g_out, land_out):
        for cp in _owner_copies(g_ref, land_ref, send_sems, recv_sems):
            cp.wait_send()
            cp.wait_recv()

    return pl.pallas_call(
        body, name=name, out_shape=(pltpu.HBM(g_thru.shape, g_thru.dtype), pltpu.HBM(land_thru.shape, land_thru.dtype)),
        in_specs=(HBM, HBM, SEM, SEM, pl.BlockSpec(memory_space=pl.ANY)), out_specs=(HBM, HBM),
        input_output_aliases={0: 0, 1: 1}, compiler_params=pltpu.CompilerParams(**SPLIT_PARAMS),
    )(g_thru, land_thru, send_sems, recv_sems, after)


def _call(name, main, grid, in_specs, out_specs, out_shape, args, scratch=(), comm=None, prefetch=None, after=None):
    comm = comm or _NoComm()
    n_main = len(in_specs)
    if after is not None:
        in_specs = list(in_specs) + [pl.BlockSpec(after.shape, lambda *_: (0,) * after.ndim)]
        args = tuple(args) + (after,)
    counts = [len(in_specs), len(comm.inputs), len(out_shape), len(comm.out_shapes), len(scratch), len(comm.scratch)]
    n_steps = math.prod(grid)
    hosted = bool(comm.inputs)

    def body(*refs):
        if prefetch is not None:
            refs = refs[1:]
        parts, at = [], 0
        for n in counts:
            parts.append(refs[at:at + n])
            at += n
        ins, c_in, outs, c_out, scr, c_scr = parts
        ins = ins[:n_main]
        step = pl.program_id(0)
        for d in range(1, len(grid)):
            step = step * grid[d] + pl.program_id(d)
        if hosted:
            @pl.when(step == 0)
            def _():
                comm.start(c_in, c_out, c_scr)
        main(ins, outs, scr)
        if hosted:
            @pl.when(step == max(n_steps - 2, 0))
            def _():
                comm.mid(c_in, c_out, c_scr)

            @pl.when(step == n_steps - 1)
            def _():
                comm.finish(c_in, c_out, c_scr)

    all_in = list(in_specs) + [HBM] * counts[1]
    all_out = list(out_specs) + [HBM] * counts[3]
    all_scratch = list(scratch) + list(comm.scratch)
    shapes = list(out_shape) + list(comm.out_shapes)
    if prefetch is None:
        res = pl.pallas_call(body, name=name, grid=grid, in_specs=all_in, out_specs=all_out, out_shape=shapes,
                             scratch_shapes=all_scratch, compiler_params=_cparams())(*args, *comm.inputs)
    else:
        res = pl.pallas_call(
            body, name=name, out_shape=shapes,
            grid_spec=pltpu.PrefetchScalarGridSpec(num_scalar_prefetch=1, grid=grid, in_specs=all_in,
                                                   out_specs=all_out, scratch_shapes=all_scratch),
            compiler_params=_cparams())(prefetch, *args, *comm.inputs)
    return res[:counts[2]], res[counts[2]:]


def _rms_fwd(name, x, g, comm=None):
    T, D = x.shape
    tm = min(T, 512)

    def main(ins, outs, scr):
        x_ref, g_ref = ins
        xv = x_ref[...]
        r = lax.rsqrt(jnp.mean(xv * xv, axis=-1, keepdims=True) + RMS_EPS)
        outs[0][...] = (xv * r * g_ref[...]).astype(outs[0].dtype)

    tile = pl.BlockSpec((tm, D), lambda i: (i, 0))
    (h,), extra = _call(name, main, (T // tm,), [tile, pl.BlockSpec((1, D), lambda i: (0, 0))], [tile],
                        [jax.ShapeDtypeStruct((T, D), BF16)], (x, g), comm=comm)
    return h, extra


def _final_loss_epilogue(scale, out_scale):
    def ep(acc, ex, outs):
        res_ref, g_ref, t_ref = ex
        dx_ref, dxb_ref, dg_ref, loss_ref = outs
        n = acc.shape[-1]
        xv = res_ref[...] + scale * acc
        r = lax.rsqrt(jnp.mean(xv * xv, axis=-1, keepdims=True) + RMS_EPS)
        xhat = xv * r
        err = xhat * g_ref[...] - t_ref[...]

        @pl.when(pl.program_id(0) == 0)
        def _():
            dg_ref[...] = jnp.zeros_like(dg_ref)
            loss_ref[...] = jnp.zeros_like(loss_ref)

        loss_ref[...] += jnp.broadcast_to(jnp.sum(err * err) * (0.5 / n), loss_ref.shape)
        dy = err * (1.0 / n)
        dg_ref[...] += jnp.sum(dy * xhat, axis=0, keepdims=True)
        dxhat = dy * g_ref[...]
        dx = r * (dxhat - xhat * jnp.mean(dxhat * xhat, axis=-1, keepdims=True))
        dx_ref[...] = dx
        dxb_ref[...] = (out_scale * dx).astype(dxb_ref.dtype)
    return ep


def _mm_nstream(name, a, ws, w_sel, w_form, comps, out_dtypes, epilogue, cn, rows=1024, comm=None, after=None):
    T, K = a.shape
    N = ws[0].shape[1]
    rows = min(rows, T)
    assert N % cn == 0 and T % rows == 0
    n_w, n_c = len(ws), len(comps)

    def main(ins, outs, scr):
        a_ref = ins[0]
        w_refs = ins[1:1 + n_w]
        c_refs = ins[1 + n_w:1 + n_w + n_c]

        for r in range(T // rows):
            sl = slice(r * rows, (r + 1) * rows)
            a_blk = a_ref[sl, :]
            dots = [_dot(a_blk, w_ref[...], w_form) for w_ref in w_refs]
            res = epilogue(dots, [c_ref[sl, :] for c_ref in c_refs])
            for o_ref, o in zip(outs, res):
                o_ref[sl, :] = o.astype(o_ref.dtype)

    if w_form == "nt":
        w_specs = [pl.BlockSpec((None, cn, K), functools.partial(lambda j, s: (s, j, 0), s=s)) for s in w_sel]
    else:
        w_specs = [pl.BlockSpec((K, cn), lambda j: (0, j)) for _ in ws]
    chunk = pl.BlockSpec((T, cn), lambda j: (0, j))
    return _call(name, main, (N // cn,), [pl.BlockSpec((T, K), lambda j: (0, 0))] + w_specs + [chunk] * n_c,
                 [chunk] * len(out_dtypes), [jax.ShapeDtypeStruct((T, N), dt) for dt in out_dtypes],
                 (a, *ws, *comps), comm=comm, after=after)


def _mm_mstream(name, as_, ws, w_sel, w_form, extras, outs_desc, epilogue, tm=512, comm=None, after=None):
    T = as_[0].shape[0]
    tm = min(tm, T)
    n_a = len(as_)
    w_shapes = [w.shape[-2:] for w in ws]
    N = w_shapes[0][1] if w_form == "nn" else w_shapes[0][0]

    def main(ins, outs, scr):
        a_refs = ins[:n_a]
        w_refs = ins[n_a:2 * n_a]
        acc = None
        for a_ref, w_ref in zip(a_refs, w_refs):
            d = _dot(a_ref[...], w_ref[...], w_form)
            acc = d if acc is None else acc + d
        epilogue(acc, ins[2 * n_a:], outs)

    kind_spec = {"tile": pl.BlockSpec((tm, N), lambda i: (i, 0)), "vec": pl.BlockSpec((1, N), lambda i: (0, 0))}
    kind_shape = {"tile": (T, N), "vec": (1, N)}
    a_specs = [pl.BlockSpec((tm, a.shape[1]), lambda i: (i, 0)) for a in as_]
    w_specs = []
    for w, s in zip(ws, w_sel):
        if w.ndim == 3:
            w_specs.append(pl.BlockSpec((None,) + tuple(w.shape[1:]), functools.partial(lambda i, s: (s, 0, 0), s=s),
                                        pipeline_mode=pl.Buffered(1)))
        else:
            w_specs.append(pl.BlockSpec(tuple(w.shape), lambda i: (0, 0), pipeline_mode=pl.Buffered(1)))
    args = list(as_) + list(ws) + [e for e, _ in extras]
    return _call(name, main, (T // tm,), a_specs + w_specs + [kind_spec[k] for _, k in extras],
                 [kind_spec[k] for _, k in outs_desc],
                 [jax.ShapeDtypeStruct(kind_shape[k], dt) for dt, k in outs_desc], args, comm=comm, after=after)


def _residual_rms_epilogue(scale):
    def ep(acc, ex, outs):
        xv = ex[0][...] + scale * acc
        outs[0][...] = xv
        r = lax.rsqrt(jnp.mean(xv * xv, axis=-1, keepdims=True) + RMS_EPS)
        outs[1][...] = (xv * r * ex[1][...]).astype(outs[1].dtype)
    return ep


def _rms_bwd_epilogue(out_scale):
    def ep(acc, ex, outs):
        x_ref, g_ref, dres_ref = ex
        dx_ref, dxb_ref, dg_ref = outs
        xv = x_ref[...]
        r = lax.rsqrt(jnp.mean(xv * xv, axis=-1, keepdims=True) + RMS_EPS)
        xhat = xv * r

        @pl.when(pl.program_id(0) == 0)
        def _():
            dg_ref[...] = jnp.zeros_like(dg_ref)

        dg_ref[...] += jnp.sum(acc * xhat, axis=0, keepdims=True)
        dxhat = acc * g_ref[...]
        dx = r * (dxhat - xhat * jnp.mean(dxhat * xhat, axis=-1, keepdims=True)) + dres_ref[...]
        dx_ref[...] = dx
        dxb_ref[...] = (out_scale * dx).astype(dxb_ref.dtype)
    return ep


def _mm_tn(name, a, b, tmo, tno, out_dtype, tk=1024, comm=None, after=None):
    T, Ma = a.shape
    Nb = b.shape[1]
    tk = min(tk, T)
    nk = T // tk

    def main(ins, outs, scr):
        a_ref, b_ref = ins
        (acc_ref,) = scr
        k = pl.program_id(2)

        @pl.when(k == 0)
        def _():
            acc_ref[...] = jnp.zeros_like(acc_ref)

        acc_ref[...] += _dot(a_ref[...], b_ref[...], "tn")

        @pl.when(k == nk - 1)
        def _():
            outs[0][...] = acc_ref[...].astype(outs[0].dtype)

    (out,), extra = _call(
        name, main, (Ma // tmo, Nb // tno, nk),
        [pl.BlockSpec((tk, tmo), lambda i, j, k: (k, i)), pl.BlockSpec((tk, tno), lambda i, j, k: (k, j))],
        [pl.BlockSpec((tmo, tno), lambda i, j, k: (i, j))], [jax.ShapeDtypeStruct((Ma, Nb), out_dtype)],
        (a, b), scratch=[pltpu.VMEM((tmo, tno), F32)], comm=comm, after=after)
    return out, extra


def _swiglu_parts(g, u):
    s = _sigmoid(g)
    silu = g * s
    return [u * (s + silu * (1.0 - s)), silu, silu * u]


def _silu_mul_epilogue(dots, comps):
    g, u = dots
    return _swiglu_parts(g, u)


def _gate_parts_epilogue(dots, comps):
    (g,) = dots
    s = _sigmoid(g)
    silu = g * s
    return [s + silu * (1.0 - s), silu]


def _up_act_epilogue(dots, comps):
    (u,) = dots
    return [u * comps[0].astype(F32), u * comps[1].astype(F32)]


def _dact_epilogue(dots, comps):
    dact = dots[0].astype(BF16)
    return [dact * comps[0], dact * comps[1]]


def _identity_epilogue(dots, comps):
    return list(dots)


def _swap_halves(x):
    lane = lax.broadcasted_iota(jnp.int32, x.shape, 1)
    first = (lane % DK) < (DK // 2)
    return jnp.where(first, pltpu.roll(x, 128 - DK // 2, 1), pltpu.roll(x, DK // 2, 1))


def _rotary(t, cos, sin_signed):
    halves = []
    for p in range(QK_W // 128):
        th = t[:, 128 * p:128 * (p + 1)]
        halves.append(th * cos + _swap_halves(th) * sin_signed)
    return jnp.concatenate(halves, axis=1)


def _rotary_transposed(d, cos, sin_signed):
    halves = []
    for p in range(QK_W // 128):
        dh = d[:, 128 * p:128 * (p + 1)]
        halves.append(dh * cos + _swap_halves(dh * sin_signed))
    return jnp.concatenate(halves, axis=1)


def _log_sigmoid(x):
    return jnp.minimum(x, 0.0) - jnp.log(1.0 + jnp.exp(-jnp.abs(x)))


def _tri_sum(mask, x):
    tri = mask.astype(BF16)
    hi = x.astype(BF16)
    rest = x - hi.astype(F32)
    mid = rest.astype(BF16)
    lo = (rest - mid.astype(F32)).astype(BF16)
    return _dot(tri, hi, "nn") + _dot(tri, mid, "nn") + _dot(tri, lo, "nn")


def _attn_masks():
    row = lax.broadcasted_iota(jnp.int32, (SUPER, SUPER), 0)
    col = lax.broadcasted_iota(jnp.int32, (SUPER, SUPER), 1)
    same = (row // CHUNK) == (col // CHUNK)
    return row, col, same


def _group_inputs(grp, pr, cos, sin_signed, lg, wa2, ba):
    seg = lambda lo, width: pr[:, lo:lo + width].astype(F32)
    if grp == 0:
        q = _rotary(seg(C_RQ, QK_W), cos, sin_signed)
        k = _rotary(seg(C_RK, QK_W), cos, sin_signed) * (DK ** -0.5)
        v = pr[:, C_RV:C_RV + V_W]
        gate = seg(C_RG, V_W)
        pos = lax.broadcasted_iota(jnp.int32, (SUPER, QK_W), 0).astype(F32) + 1.0
        return q, k, v, gate, pos * lg, None, None
    q = seg(C_GQ, QK_W) * (DK ** -0.5)
    k = seg(C_GK, QK_W)
    v = pr[:, C_GV:C_GV + V_W]
    gate = seg(C_GG, V_W)
    glow = pr[:, C_GL:C_GL + GL_W]
    logit = _dot(glow.astype(BF16), wa2.astype(BF16), "nn") + ba
    la = _log_sigmoid(logit) * (1.0 / GATE_NORM)
    row, col, _ = _attn_masks()
    b_cum = _tri_sum(col <= row, la)
    return q, k, v, gate, b_cum, glow, logit


def _decay_factors(q, k, b_cum):
    c = b_cum[SUPER // 2 - 1:SUPER // 2, :]
    bl = b_cum[SUPER - 1:SUPER, :]
    e1 = jnp.exp(b_cum - c)
    e2 = jnp.exp(c - b_cum)
    e_b = jnp.exp(b_cum)
    e_l = jnp.exp(bl - b_cum)
    return dict(e1=e1, e2=e2, eb=e_b, el=e_l, ebl=jnp.exp(bl),
                qp=q * e1, qm=q * e2, kp=k * e1, km=k * e2, qs=q * e_b, kl=k * e_l)


def _state_block_mask():
    r = lax.broadcasted_iota(jnp.int32, (V_W, QK_W), 0)
    c = lax.broadcasted_iota(jnp.int32, (V_W, QK_W), 1)
    return (r // DV) == (c // DK)


def _attn_fwd(proj, cos, sin_signed, lg, wa2p, ba, gn_ret, gn_gla, x_res, w_out, g_next, comm=None):
    T = proj.shape[0]
    n_s = T // SUPER
    D = x_res.shape[1]

    def main(ins, outs, scr):
        pr_ref, cos_ref, sin_ref, lg_ref, wa2_ref, ba_ref, gr_ref, gg_ref, xres_ref, wout_ref, gnext_ref = ins
        o_ref, y_ref, st_ref, x_ref, h_ref = outs
        (s_ref,) = scr
        i = pl.program_id(0)

        @pl.when(i == 0)
        def _():
            s_ref[...] = jnp.zeros_like(s_ref)

        pr = pr_ref
        row, col, same = _attn_masks()
        m1 = col <= row
        m2 = jnp.logical_and(col > row, same)
        lane = lax.broadcasted_iota(jnp.int32, (1, QK_W), 1)
        blockmask = _state_block_mask()
        for grp in range(2):
            q, k, v, gate, b_cum, _, _ = _group_inputs(grp, pr, cos_ref[...], sin_ref[...], lg_ref[...],
                                                      wa2_ref[...], ba_ref[...])
            f = _decay_factors(q, k, b_cum)
            gn = gr_ref[...] if grp == 0 else gg_ref[...]
            s_prev = s_ref[grp]
            st_ref[0, grp] = s_prev
            o_inter = _dot(f["qs"].astype(BF16), s_prev.astype(BF16), "nt")
            kmb = f["km"].astype(BF16)
            kpb = f["kp"].astype(BF16)
            vb = v.astype(BF16)
            heads = [(lane // DK) == h for h in range(HEADS)]
            a1_all = _dot(jnp.concatenate([jnp.where(hm, f["qp"], 0.0).astype(BF16) for hm in heads], axis=0), kmb, "nt")
            a2_all = _dot(jnp.concatenate([jnp.where(hm, f["qm"], 0.0).astype(BF16) for hm in heads], axis=0), kpb, "nt")
            for h in range(HEADS):
                a1 = a1_all[h * SUPER:(h + 1) * SUPER]
                a2 = a2_all[h * SUPER:(h + 1) * SUPER]
                a = jnp.where(m1, a1, jnp.where(m2, a2, 0.0))
                lo = grp * V_W + h * DV
                o_h = _dot(a.astype(BF16), vb[:, h * DV:(h + 1) * DV], "nn") + o_inter[:, h * DV:(h + 1) * DV]
                o_ref[:, lo:lo + DV] = o_h
                r = lax.rsqrt(jnp.mean(o_h * o_h, axis=-1, keepdims=True) + RMS_EPS)
                gte = gate[:, h * DV:(h + 1) * DV]
                y = o_h * r * gn[:, h * DV:(h + 1) * DV] * (gte * _sigmoid(gte))
                y_ref[:, lo:lo + DV] = y.astype(y_ref.dtype)
            upd = _dot(vb, f["kl"].astype(BF16), "tn")
            s_ref[grp] = s_prev * f["ebl"] + jnp.where(blockmask, upd, 0.0)
        xv = xres_ref[...] + _dot(y_ref[...], wout_ref[...], "nn")
        x_ref[...] = xv
        r = lax.rsqrt(jnp.mean(xv * xv, axis=-1, keepdims=True) + RMS_EPS)
        h_ref[...] = (xv * r * gnext_ref[...]).astype(h_ref.dtype)

    const = lambda shape: pl.BlockSpec(shape, lambda i: tuple(0 for _ in shape))
    rows = lambda w: pl.BlockSpec((SUPER, w), lambda i: (i, 0))
    return _call(
        "attn_fwd", main, (n_s,),
        [rows(PROJ_W), rows(128), rows(128),
         const((1, QK_W)), const((GL_W, QK_W)), const((1, QK_W)), const((1, V_W)), const((1, V_W)),
         rows(D), const((2 * V_W, D)), const((1, D))],
        [rows(2 * V_W), rows(2 * V_W), pl.BlockSpec((1, 2, V_W, QK_W), lambda i: (i, 0, 0, 0)), rows(D), rows(D)],
        [jax.ShapeDtypeStruct((T, 2 * V_W), F32), jax.ShapeDtypeStruct((T, 2 * V_W), BF16),
         jax.ShapeDtypeStruct((n_s, 2, V_W, QK_W), F32), jax.ShapeDtypeStruct((T, D), F32),
         jax.ShapeDtypeStruct((T, D), BF16)],
        (proj, cos, sin_signed, lg, wa2p, ba, gn_ret, gn_gla, x_res, w_out, g_next),
        scratch=[pltpu.VMEM((2, V_W, QK_W), F32)], comm=comm)


def _attn_bwd(proj, cos, sin_signed, lg, wa2p, ba, gn_ret, gn_gla, o, dx, w_out, states, comm=None, after=None):
    T = proj.shape[0]
    n_s = T // SUPER
    D = dx.shape[1]

    def main(ins, outs, scr):
        pr_ref, cos_ref, sin_ref, lg_ref, wa2_ref, ba_ref, gr_ref, gg_ref, o_ref, dx_ref, wout_ref, st_ref = ins
        dp_ref, dgr_ref, dgg_ref, dba_ref, dwa_ref = outs
        (ds_ref, dy_ref) = scr
        i = pl.program_id(0)
        dy_ref[...] = _dot(dx_ref[...], wout_ref[...], "nt")

        @pl.when(i == 0)
        def _():
            ds_ref[...] = jnp.zeros_like(ds_ref)
            dgr_ref[...] = jnp.zeros_like(dgr_ref)
            dgg_ref[...] = jnp.zeros_like(dgg_ref)
            dba_ref[...] = jnp.zeros_like(dba_ref)
            dwa_ref[...] = jnp.zeros_like(dwa_ref)

        pr = pr_ref
        cos = cos_ref[...]
        sin_signed = sin_ref[...]
        row, col, same = _attn_masks()
        m1 = col <= row
        m2 = jnp.logical_and(col > row, same)
        m1t = row <= col
        m2t = jnp.logical_and(row > col, same)
        lane = lax.broadcasted_iota(jnp.int32, (1, QK_W), 1)
        blockmask = _state_block_mask()
        for grp in range(2):
            q, k, v, gate, b_cum, glow, logit = _group_inputs(grp, pr, cos, sin_signed, lg_ref[...],
                                                              wa2_ref[...], ba_ref[...])
            f = _decay_factors(q, k, b_cum)
            gn = gr_ref[...] if grp == 0 else gg_ref[...]
            dgn_ref = dgr_ref if grp == 0 else dgg_ref
            do_parts, dgate_parts, dgn_parts = [], [], []
            for h in range(HEADS):
                lo = grp * V_W + h * DV
                o_h = o_ref[:, lo:lo + DV]
                r = lax.rsqrt(jnp.mean(o_h * o_h, axis=-1, keepdims=True) + RMS_EPS)
                n = o_h * r
                gte = gate[:, h * DV:(h + 1) * DV]
                sg = _sigmoid(gte)
                dy_h = dy_ref[:, lo:lo + DV]
                gn_h = gn[:, h * DV:(h + 1) * DV]
                dgate_parts.append(dy_h * n * gn_h * (sg * (1.0 + gte * (1.0 - sg))))
                dz = dy_h * (gte * sg)
                dgn_parts.append(jnp.sum(dz * n, axis=0, keepdims=True))
                dn = dz * gn_h
                do_parts.append(r * (dn - n * jnp.mean(dn * n, axis=-1, keepdims=True)))
            dgn_ref[...] += jnp.concatenate(dgn_parts, axis=1)
            dgate = jnp.concatenate(dgate_parts, axis=1)
            do = jnp.concatenate(do_parts, axis=1)
            dob = do.astype(BF16)
            vb = v.astype(BF16)
            s_prev = st_ref[0, grp]
            ds_new = ds_ref[grp]
            dsb = ds_new.astype(BF16)
            qpb, qmb = f["qp"].astype(BF16), f["qm"].astype(BF16)
            kpb, kmb = f["kp"].astype(BF16), f["km"].astype(BF16)
            dqp = jnp.zeros((SUPER, QK_W), F32)
            dqm = jnp.zeros((SUPER, QK_W), F32)
            dkp = jnp.zeros((SUPER, QK_W), F32)
            dkm = jnp.zeros((SUPER, QK_W), F32)
            dv_parts = []
            heads = [(lane // DK) == h for h in range(HEADS)]
            qp_hs = [jnp.where(hm, f["qp"], 0.0).astype(BF16) for hm in heads]
            qm_hs = [jnp.where(hm, f["qm"], 0.0).astype(BF16) for hm in heads]
            kp_hs = [jnp.where(hm, f["kp"], 0.0).astype(BF16) for hm in heads]
            km_hs = [jnp.where(hm, f["km"], 0.0).astype(BF16) for hm in heads]
            at1_all = _dot(jnp.concatenate(km_hs, axis=0), qpb, "nt")
            at2_all = _dot(jnp.concatenate(kp_hs, axis=0), qmb, "nt")
            for h in range(HEADS):
                qp_h, qm_h, kp_h, km_h = qp_hs[h], qm_hs[h], kp_hs[h], km_hs[h]
                at = jnp.where(m1t, at1_all[h * SUPER:(h + 1) * SUPER],
                               jnp.where(m2t, at2_all[h * SUPER:(h + 1) * SUPER], 0.0))
                do_h = dob[:, h * DV:(h + 1) * DV]
                v_h = vb[:, h * DV:(h + 1) * DV]
                dv_parts.append(_dot(at.astype(BF16), do_h, "nn"))
                da = _dot(do_h, v_h, "nt")
                dat = _dot(v_h, do_h, "nt")
                da1 = jnp.where(m1, da, 0.0).astype(BF16)
                da2 = jnp.where(m2, da, 0.0).astype(BF16)
                da1t = jnp.where(m1t, dat, 0.0).astype(BF16)
                da2t = jnp.where(m2t, dat, 0.0).astype(BF16)
                dqp = dqp + _dot(da1, km_h, "nn")
                dqm = dqm + _dot(da2, kp_h, "nn")
                dkm = dkm + _dot(da1t, qp_h, "nn")
                dkp = dkp + _dot(da2t, qm_h, "nn")
            klb = f["kl"].astype(BF16)
            qsb = f["qs"].astype(BF16)
            dqs = _dot(dob, s_prev.astype(BF16), "nn")
            dkl = _dot(vb, dsb, "nn")
            dv = jnp.concatenate(dv_parts, axis=1) + _dot(klb, dsb, "nt")
            ds_ref[grp] = ds_new * f["ebl"] + jnp.where(blockmask, _dot(dob, qsb, "tn"), 0.0)
            dq = dqp * f["e1"] + dqm * f["e2"] + dqs * f["eb"]
            dk = dkm * f["e2"] + dkp * f["e1"] + dkl * f["el"]
            if grp == 0:
                dq = _rotary_transposed(dq, cos, sin_signed)
                dk = _rotary_transposed(dk * (DK ** -0.5), cos, sin_signed)
                dp_ref[:, C_RQ:C_RQ + QK_W] = dq.astype(dp_ref.dtype)
                dp_ref[:, C_RK:C_RK + QK_W] = dk.astype(dp_ref.dtype)
                dp_ref[:, C_RV:C_RV + V_W] = dv.astype(dp_ref.dtype)
                dp_ref[:, C_RG:C_RG + V_W] = dgate.astype(dp_ref.dtype)
            else:
                dkl_kl = dkl * klb.astype(F32)
                db = (dqp * qpb.astype(F32) - dkm * kmb.astype(F32) - dqm * qmb.astype(F32)
                      + dkp * kpb.astype(F32) + dqs * qsb.astype(F32) - dkl_kl)
                last = (jnp.sum(dkl_kl, axis=0, keepdims=True)
                        + f["ebl"] * jnp.sum(s_prev * ds_new, axis=0, keepdims=True))
                rowq = lax.broadcasted_iota(jnp.int32, (SUPER, QK_W), 0)
                db = db + jnp.where(rowq == SUPER - 1, last, 0.0)
                dla = _tri_sum(col >= row, db)
                dlogit = dla * (1.0 / GATE_NORM) * (1.0 - _sigmoid(logit))
                dlb = dlogit.astype(BF16)
                dglow = _dot(dlb, wa2_ref[...].astype(BF16), "nt")
                dwa_ref[...] += _dot(glow.astype(BF16), dlb, "tn")
                dba_ref[...] += jnp.sum(dlogit, axis=0, keepdims=True)
                dp_ref[:, C_GQ:C_GQ + QK_W] = (dq * (DK ** -0.5)).astype(dp_ref.dtype)
                dp_ref[:, C_GK:C_GK + QK_W] = dk.astype(dp_ref.dtype)
                dp_ref[:, C_GV:C_GV + V_W] = dv.astype(dp_ref.dtype)
                dp_ref[:, C_GG:C_GG + V_W] = dgate.astype(dp_ref.dtype)
                dp_ref[:, C_GL:C_GL + GL_W] = dglow.astype(dp_ref.dtype)

    rev = lambda i: n_s - 1 - i
    const = lambda shape: pl.BlockSpec(shape, lambda i: tuple(0 for _ in shape))
    return _call(
        "attn_bwd", main, (n_s,),
        [pl.BlockSpec((SUPER, PROJ_W), lambda i: (rev(i), 0)),
         pl.BlockSpec((SUPER, 128), lambda i: (rev(i), 0)), pl.BlockSpec((SUPER, 128), lambda i: (rev(i), 0)),
         const((1, QK_W)), const((GL_W, QK_W)), const((1, QK_W)), const((1, V_W)), const((1, V_W)),
         pl.BlockSpec((SUPER, 2 * V_W), lambda i: (rev(i), 0)),
         pl.BlockSpec((SUPER, D), lambda i: (rev(i), 0)), const((2 * V_W, D)),
         pl.BlockSpec((1, 2, V_W, QK_W), lambda i: (rev(i), 0, 0, 0))],
        [pl.BlockSpec((SUPER, PROJ_W), lambda i: (rev(i), 0)),
         const((1, V_W)), const((1, V_W)), const((1, QK_W)), const((GL_W, QK_W))],
        [jax.ShapeDtypeStruct((T, PROJ_W), BF16),
         jax.ShapeDtypeStruct((1, V_W), F32), jax.ShapeDtypeStruct((1, V_W), F32),
         jax.ShapeDtypeStruct((1, QK_W), F32), jax.ShapeDtypeStruct((GL_W, QK_W), F32)],
        (proj, cos, sin_signed, lg, wa2p, ba, gn_ret, gn_gla, o, dx, w_out, states),
        scratch=[pltpu.VMEM((2, V_W, QK_W), F32), pltpu.VMEM((SUPER, 2 * V_W), F32)], comm=comm, after=after)


def _rotary_tables(T):
    half = DK // 2
    inv = ROPE_BASE ** (-jnp.arange(half, dtype=F32) * 2.0 / DK)
    ang = jnp.arange(T, dtype=F32)[:, None] * inv[None, :]
    cos, sin = jnp.cos(ang), jnp.sin(ang)
    cos_head = jnp.concatenate([cos, cos], axis=1)
    sin_head = jnp.concatenate([-sin, sin], axis=1)
    return jnp.tile(cos_head, (1, 128 // DK)), jnp.tile(sin_head, (1, 128 // DK))


def _sum_devices(name, gathered, m_per):
    def body(g_ref, o_ref):
        acc = g_ref[0:m_per, :]
        for k in range(1, N_DEV):
            acc = acc + g_ref[k * m_per:(k + 1) * m_per, :]
        o_ref[...] = acc

    return pl.pallas_call(body, name=name, out_shape=jax.ShapeDtypeStruct((m_per, 128), F32))(gathered)


def _owner_sums(name, items, owner, comm=None):
    counts = [1 + len(landed) for _, landed in items]

    def main(ins, outs, scr):
        at = 0
        for o_ref, n in zip(outs, counts):
            acc = ins[at][...].astype(F32)
            for l_ref in ins[at + 1:at + n]:
                for j in range(l_ref.shape[0]):
                    acc = acc + l_ref[j].astype(F32)
            o_ref[...] = acc
            at += n

    once = pl.Buffered(1)
    in_specs, out_specs, out_shape, args = [], [], [], []
    for grad, landed in items:
        R, C = grad.shape[-2:]
        in_specs.append(pl.BlockSpec((None, None, R, C), lambda i, s: (s[0], s[1], 0, 0), pipeline_mode=once))
        in_specs += [pl.BlockSpec(tuple(l.shape), lambda i, s: (0, 0, 0), pipeline_mode=once) for l in landed]
        out_specs.append(pl.BlockSpec((R, C), lambda i, s: (0, 0)))
        out_shape.append(jax.ShapeDtypeStruct((R, C), F32))
        args += [grad, *landed]
    return _call(name, main, (1,), in_specs, out_specs, out_shape, args, comm=comm, prefetch=owner)


def _adamw_group(name, items, n_blocks, comm=None):
    n = len(items)

    def main(ins, outs, scr):
        for p in range(n):
            g_ref, w_ref, m_ref, v_ref = ins[4 * p:4 * p + 4]
            d_ref, nm_ref, nv_ref = outs[3 * p:3 * p + 3]
            gv = g_ref[...]
            nm = ADAM_B1 * m_ref[...] + (1.0 - ADAM_B1) * gv
            nv = ADAM_B2 * v_ref[...] + (1.0 - ADAM_B2) * (gv * gv)
            m_hat = nm / (1.0 - ADAM_B1 ** ADAM_STEP)
            v_hat = nv / (1.0 - ADAM_B2 ** ADAM_STEP)
            d_ref[...] = -ADAM_LR * (m_hat / (jnp.sqrt(v_hat) + ADAM_EPS) + ADAM_WD * w_ref[...])
            nm_ref[...] = nm
            nv_ref[...] = nv

    in_specs, out_specs, out_shape, args = [], [], [], []
    for item in items:
        R, C = item[1].shape
        assert R % n_blocks == 0
        spec = pl.BlockSpec((R // n_blocks, C), lambda i: (i, 0))
        in_specs += [spec] * 4
        out_specs += [spec] * 3
        out_shape += [jax.ShapeDtypeStruct((R, C), F32)] * 3
        args += list(item)
    outs, extra = _call(name, main, (n_blocks,), in_specs, out_specs, out_shape, args, comm=comm)
    return [tuple(outs[3 * p:3 * p + 3]) for p in range(n)], extra


SMALL_ORDER = ("ffn1", "mix", "ffn2", "final", "ret", "gla", "b_a")


def kernel(x, ffn1_norm_g, ffn1_w_gate, ffn1_w_up, ffn1_w_down, mix_norm_g, w_in, ret_norm_g, gla_w_a2, gla_b_a, gla_norm_g, w_out, ffn2_norm_g, ffn2_w_gate, ffn2_w_up, ffn2_w_down, final_norm_g, loss_target, m_ffn1_norm_g, m_ffn1_w_gate, m_ffn1_w_up, m_ffn1_w_down, m_mix_norm_g, m_w_in, m_ret_norm_g, m_gla_w_a2, m_gla_b_a, m_gla_norm_g, m_w_out, m_ffn2_norm_g, m_ffn2_w_gate, m_ffn2_w_up, m_ffn2_w_down, m_final_norm_g, v_ffn1_norm_g, v_ffn1_w_gate, v_ffn1_w_up, v_ffn1_w_down, v_mix_norm_g, v_w_in, v_ret_norm_g, v_gla_w_a2, v_gla_b_a, v_gla_norm_g, v_w_out, v_ffn2_norm_g, v_ffn2_w_gate, v_ffn2_w_up, v_ffn2_w_down, v_final_norm_g):
    xi, yi, ci = _coords()
    dev = 4 * xi + 2 * yi + ci
    owner = jnp.stack([2 * xi + yi, ci]).astype(jnp.int32)

    x0, target = x[0], loss_target[0]
    T, D = x0.shape
    fb = ffn1_w_gate.shape[2]
    ib = w_in.shape[2]
    ab = gla_w_a2.shape[2]
    F = N_DEV * fb
    cos, sin_signed = _rotary_tables(T)
    lg = jnp.repeat(jnp.log(1.0 - 2.0 ** (-5.0 - jnp.arange(HEADS, dtype=F32))), DK)[None, :]
    g_final = final_norm_g.reshape(1, D)

    g1_loc = ffn1_w_gate[0].T[None].astype(BF16)
    u1_loc = ffn1_w_up[0].T[None].astype(BF16)
    d1_loc = ffn1_w_down.astype(BF16)
    g2_loc = ffn2_w_gate[0].T[None].astype(BF16)
    u2_loc = ffn2_w_up[0].T[None].astype(BF16)
    d2_loc = ffn2_w_down.astype(BF16)
    in_loc = w_in[0].T.astype(BF16)
    out_loc = w_out[0].astype(BF16)

    h1, (g1,) = _rms_fwd("ffn1_rms", x0, ffn1_norm_g, comm=_AllGather([g1_loc], ["stack"]))
    g1 = g1.reshape(1, F, D)
    (dsl1, sl1), (u1,) = _mm_nstream("ffn1_gate", h1, [g1], [0], "nt", [], [BF16, BF16], _gate_parts_epilogue, cn=256,
                                     comm=_AllGather([u1_loc], ["stack"]))
    u1 = u1.reshape(1, F, D)
    (dsu1, act1), (d1,) = _mm_nstream("ffn1_up", h1, [u1], [0], "nt", [dsl1, sl1], [BF16, BF16],
                                      _up_act_epilogue, cn=256, comm=_AllGather([d1_loc], ["stack"]))
    d1 = d1.reshape(1, F, D)
    f32_tile, bf16_tile, f32_vec = (F32, "tile"), (BF16, "tile"), (F32, "vec")
    (x1, h2), (in_all, a_all) = _mm_mstream(
        "ffn1_down", [act1], [d1], [0], "nn", [(x0, "tile"), (mix_norm_g, "vec")], [f32_tile, bf16_tile],
        _residual_rms_epilogue(0.5), tm=1024, comm=_AllGather([in_loc, gla_w_a2[0]], ["plain", "plain"]))
    w_in_t = jnp.pad(in_all.reshape(1, N_DEV * ib, D), ((0, 0), (0, PROJ_W - N_DEV * ib), (0, 0)))
    wa2 = jnp.transpose(a_all, (1, 0, 2)).reshape(GATE_RANK, N_DEV * ab)
    wa2p = jnp.pad(wa2, ((0, GL_W - GATE_RANK), (0, 0)))

    (proj,), (g2, out_all) = _mm_nstream("mix_proj", h2, [w_in_t], [0], "nt", [], [BF16], _identity_epilogue, cn=640,
                                         comm=_AllGather([g2_loc, out_loc], ["stack", "plain"]))
    w_out_full = out_all.reshape(D, D)
    (o, ymix, states, x2, h3), (u2,) = _attn_fwd(proj, cos, sin_signed, lg, wa2p, gla_b_a, ret_norm_g, gla_norm_g,
                                                 x1, w_out_full, ffn2_norm_g, comm=_AllGather([u2_loc], ["stack"]))
    g2, u2 = g2.reshape(1, F, D), u2.reshape(1, F, D)

    (dsu2, sl2, act2), (d2,) = _mm_nstream(
        "ffn2_up", h3, [g2, u2], [0, 0], "nt", [], [BF16, BF16, BF16], _silu_mul_epilogue, cn=256,
        comm=_AllGather([d2_loc], ["stack"]))
    d2 = d2.reshape(1, F, D)
    (dx3, dy3b, d_final, loss), _ = _mm_mstream(
        "ffn2_down", [act2], [d2], [0], "nn", [(x2, "tile"), (g_final, "vec"), (target, "tile")],
        [f32_tile, bf16_tile, f32_vec, f32_vec], _final_loss_epilogue(0.5, 0.5))

    sent = {}

    def send(nm, grad):
        sent[nm] = _send_to_owners("send_" + nm, grad)
        return sent[nm][4]

    dwd2, _ = _mm_tn("ffn2b_dwd", act2, dy3b, F // 2, D, BF16)
    tok = send("wd2", dwd2.reshape(4, 2, fb, D))
    (dgate2, dup2), _ = _mm_nstream("ffn2b_dact", dy3b, [d2], [0], "nt", [dsu2, sl2], [BF16, BF16],
                                    _dact_epilogue, cn=256, after=tok)
    dwg2, _ = _mm_tn("ffn2b_dwg", dgate2, h3, F // 2, D, BF16)
    tok = send("wg2", dwg2.reshape(4, 2, fb, D))
    dwu2, _ = _mm_tn("ffn2b_dwu", dup2, h3, F // 2, D, BF16, after=tok)
    tok = send("wu2", dwu2.reshape(4, 2, fb, D))
    rms_outs = [f32_tile, bf16_tile, f32_vec]
    (dx2, dx2b, d_g2), _ = _mm_mstream(
        "ffn2b_dh", [dgate2, dup2], [g2, u2], [0, 0], "nn", [(x2, "tile"), (ffn2_norm_g, "vec"), (dx3, "tile")],
        rms_outs, _rms_bwd_epilogue(1.0), after=tok)

    dwout, _ = _mm_tn("mixb_dwout", ymix, dx2b, D, D, BF16)
    tok = send("wout", dwout.reshape(4, 2, D // N_DEV, D))
    (dproj, d_ret, d_gla, d_ba, d_wa2p), _ = _attn_bwd(
        proj, cos, sin_signed, lg, wa2p, gla_b_a, ret_norm_g, gla_norm_g, o, dx2b, w_out_full, states, after=tok)
    dwin_t, _ = _mm_tn("mixb_dwin", dproj, h2, 640, D, BF16, tk=2048)
    tok = send("win", dwin_t[:N_DEV * ib].reshape(4, 2, ib, D))
    (dx1, dy1b, d_gmix), _ = _mm_mstream(
        "mixb_dh", [dproj], [w_in_t], [0], "nn", [(x1, "tile"), (mix_norm_g, "vec"), (dx2, "tile")],
        rms_outs, _rms_bwd_epilogue(0.5), after=tok)

    dwd1, _ = _mm_tn("ffn1b_dwd", act1, dy1b, F // 2, D, BF16)
    tok = send("wd1", dwd1.reshape(4, 2, fb, D))
    (dgate1, dup1), _ = _mm_nstream("ffn1b_dact", dy1b, [d1], [0], "nt", [dsu1, sl1], [BF16, BF16],
                                    _dact_epilogue, cn=256, after=tok)
    dwg1, _ = _mm_tn("ffn1b_dwg", dgate1, h1, F // 2, D, BF16)
    tok = send("wg1", dwg1.reshape(4, 2, fb, D))
    dwu1, _ = _mm_tn("ffn1b_dwu", dup1, h1, F // 2, D, BF16, after=tok)
    tok = send("wu1", dwu1.reshape(4, 2, fb, D))
    (dx0, _, d_g1), _ = _mm_mstream(
        "ffn1b_dh", [dgate1, dup1], [g1, u1], [0, 0], "nn", [(x0, "tile"), (ffn1_norm_g, "vec"), (dx1, "tile")],
        rms_outs, _rms_bwd_epilogue(1.0), after=tok)

    small = dict(ffn1=d_g1, mix=d_gmix, ffn2=d_g2, final=d_final, ret=d_ret, gla=d_gla, b_a=d_ba)
    flat = jnp.concatenate([small[k].reshape(-1) for k in SMALL_ORDER]
                           + [d_wa2p[:GATE_RANK].reshape(-1), loss[0, :128]])
    rows = -(-flat.shape[0] // 128)
    rows = -(-rows // 8) * 8
    packed = jnp.pad(flat, (0, rows * 128 - flat.shape[0])).reshape(rows, 128)

    transposed = ("ffn1_w_gate", "ffn1_w_up", "ffn2_w_gate", "ffn2_w_up", "w_in")

    def to_2d(nm, a):
        if nm in transposed:
            return a[0].T
        return a.reshape((1, a.shape[0]) if a.ndim == 1 else a.shape[-2:])

    def from_2d(nm, a):
        return a.T[None] if nm in transposed else a.reshape(params[nm][0].shape)

    def arrived(nm, after):
        grad, landed = _await_owners("await_" + nm, sent[nm], after)
        return grad, [landed]

    sums_a, (gathered,) = _owner_sums(
        "sum_a", [arrived(nm, dx0) for nm in ("wg2", "wu2", "wd2", "win", "wout")], owner,
        comm=_AllGather([packed], ["plain"]))
    params = dict(
        ffn2_w_gate=(ffn2_w_gate, m_ffn2_w_gate, v_ffn2_w_gate), ffn2_w_up=(ffn2_w_up, m_ffn2_w_up, v_ffn2_w_up),
        ffn2_w_down=(ffn2_w_down, m_ffn2_w_down, v_ffn2_w_down), w_in=(w_in, m_w_in, v_w_in),
        w_out=(w_out, m_w_out, v_w_out), ffn1_w_gate=(ffn1_w_gate, m_ffn1_w_gate, v_ffn1_w_gate),
        ffn1_w_up=(ffn1_w_up, m_ffn1_w_up, v_ffn1_w_up), ffn1_w_down=(ffn1_w_down, m_ffn1_w_down, v_ffn1_w_down),
        ffn1_norm_g=(ffn1_norm_g, m_ffn1_norm_g, v_ffn1_norm_g), mix_norm_g=(mix_norm_g, m_mix_norm_g, v_mix_norm_g),
        ret_norm_g=(ret_norm_g, m_ret_norm_g, v_ret_norm_g), gla_w_a2=(gla_w_a2, m_gla_w_a2, v_gla_w_a2),
        gla_b_a=(gla_b_a, m_gla_b_a, v_gla_b_a), gla_norm_g=(gla_norm_g, m_gla_norm_g, v_gla_norm_g),
        ffn2_norm_g=(ffn2_norm_g, m_ffn2_norm_g, v_ffn2_norm_g), final_norm_g=(final_norm_g, m_final_norm_g, v_final_norm_g))
    grads, updates = {}, {}

    def run_adam(name, names, grad_2d, n_blocks):
        items = [(grad_2d[nm],) + tuple(to_2d(nm, a) for a in params[nm]) for nm in names]
        res, _ = _adamw_group(name, items, n_blocks)
        for nm, r in zip(names, res):
            grads[nm] = from_2d(nm, grad_2d[nm])
            updates[nm] = tuple(from_2d(nm, a) for a in r)
        return res

    grads_a = {"ffn2_w_gate": sums_a[0], "ffn2_w_up": sums_a[1], "ffn2_w_down": sums_a[2], "w_out": sums_a[4]}
    run_adam("adamw_w_in", ["w_in"], {"w_in": sums_a[3]}, 1)
    done_a = run_adam("adamw_a", list(grads_a), grads_a, 4)[0][0]
    sums_b, _ = _owner_sums("sum_b", [arrived(nm, done_a) for nm in ("wg1", "wu1", "wd1")], owner)
    grads_b = {"ffn1_w_gate": sums_b[0], "ffn1_w_up": sums_b[1], "ffn1_w_down": sums_b[2]}
    run_adam("adamw_b", list(grads_b), grads_b, 4)

    total = _sum_devices("sum_small", gathered.reshape(N_DEV * rows, 128), rows).reshape(-1)
    sizes = [small[k].size for k in SMALL_ORDER] + [GATE_RANK * QK_W, 128]
    offs = [0]
    for s in sizes:
        offs.append(offs[-1] + s)
    pieces = [total[offs[i]:offs[i + 1]] for i in range(len(sizes))]
    g_small = {k: pieces[i].reshape(small[k].shape) for i, k in enumerate(SMALL_ORDER)}
    g_wa2_full = pieces[len(SMALL_ORDER)].reshape(GATE_RANK, QK_W)
    g_wa2 = lax.dynamic_slice(g_wa2_full, (0, dev * ab), (GATE_RANK, ab))
    loss_total = pieces[len(SMALL_ORDER) + 1][0]

    small_grads = {"ffn1_norm_g": g_small["ffn1"], "mix_norm_g": g_small["mix"], "ret_norm_g": g_small["ret"],
                   "gla_w_a2": g_wa2, "gla_b_a": g_small["b_a"], "gla_norm_g": g_small["gla"],
                   "ffn2_norm_g": g_small["ffn2"], "final_norm_g": g_small["final"]}
    run_adam("adamw_small", list(small_grads), small_grads, 1)

    order = ("ffn1_norm_g", "ffn1_w_gate", "ffn1_w_up", "ffn1_w_down", "mix_norm_g", "w_in", "ret_norm_g", "gla_w_a2",
             "gla_b_a", "gla_norm_g", "w_out", "ffn2_norm_g", "ffn2_w_gate", "ffn2_w_up", "ffn2_w_down", "final_norm_g")
    return (loss_total, dx0[None], *[grads[nm] for nm in order], *[updates[nm][0] for nm in order],
            *[updates[nm][1] for nm in order], *[updates[nm][2] for nm in order])
```

```python
import functools
import math

import jax
import jax.numpy as jnp
from jax import lax
from jax.experimental import pallas as pl
from jax.experimental.pallas import tpu as pltpu

F32 = jnp.float32
BF16 = jnp.bfloat16
MESH = pl.DeviceIdType.MESH
HBM = pl.BlockSpec(memory_space=pltpu.HBM)

N_DEV = 8
RMS_EPS = 1e-6
ROPE_BASE = 10000.0
HEADS = 4
DK = 64
DV = 128
QK_W = HEADS * DK
V_W = HEADS * DV
GATE_RANK = 16
GATE_NORM = 16.0
CHUNK = 64
SUPER = 256
PROJ_W = 3200
C_RQ, C_RK, C_RV, C_RG, C_GQ, C_GK, C_GV, C_GG, C_GL = 0, 256, 512, 1024, 1536, 1792, 2048, 2560, 3072
GL_W = PROJ_W - C_GL
ADAM_LR, ADAM_B1, ADAM_B2, ADAM_EPS, ADAM_WD, ADAM_STEP = 0.001, 0.9, 0.999, 1e-08, 0.01, 10
VMEM_LIMIT_V7X = 52 * 1024 * 1024


def _cparams(**kw):
    return pltpu.CompilerParams(vmem_limit_bytes=VMEM_LIMIT_V7X, **kw)


def _dot(a, b, form, precision=None):
    dims = {"nn": (((1,), (0,)), ((), ())), "nt": (((1,), (1,)), ((), ())), "tn": (((0,), (0,)), ((), ()))}[form]
    return lax.dot_general(a, b, dims, preferred_element_type=F32, precision=precision)


def _sigmoid(x):
    return 1.0 / (1.0 + jnp.exp(-x))


def _coords():
    return lax.axis_index("x"), lax.axis_index("y"), lax.axis_index("c")


class _NoComm:
    inputs, out_shapes, scratch = (), (), ()


class _AllGather:
    def __init__(self, arrays, kinds):
        self.inputs = tuple(arrays)
        self.kinds = tuple(kinds)
        n = len(arrays)
        self.out_shapes = tuple(
            jax.ShapeDtypeStruct((a.shape[0], N_DEV) + a.shape[1:] if k == "stack" else (N_DEV,) + a.shape, a.dtype)
            for a, k in zip(arrays, kinds))
        self.scratch = (pltpu.SemaphoreType.DMA((n, 7)), pltpu.SemaphoreType.DMA((n, 7)),
                        pltpu.SemaphoreType.DMA((n,)))

    def _ctx(self, srcs, outs, sems):
        send_sems, recv_sems, local_sems = sems
        x, y, c = _coords()
        me, sibling = (x, y, c), (x, y, 1 - c)
        chips = [(1 - x, y), (x, 1 - y), (1 - x, 1 - y)]

        def blk(m, dev):
            k = 4 * dev[0] + 2 * dev[1] + dev[2]
            return outs[m].at[:, k] if self.kinds[m] == "stack" else outs[m].at[k]

        def copy(m, s, block, to, src=None):
            return pltpu.make_async_remote_copy(
                src_ref=blk(m, block) if src is None else src, dst_ref=blk(m, block),
                send_sem=send_sems.at[m, s], recv_sem=recv_sems.at[m, s], device_id=to, device_id_type=MESH)

        def mine(m):
            return pltpu.make_async_copy(srcs[m], blk(m, me), local_sems.at[m])

        def first(m):
            return [copy(m, 0, me, sibling, src=srcs[m])] + [
                copy(m, 1 + j, me, (*chip, c), src=srcs[m]) for j, chip in enumerate(chips)]

        return me, sibling, chips, c, copy, mine, first

    def start(self, srcs, outs, sems):
        me, sibling, chips, c, copy, mine, first = self._ctx(srcs, outs, sems)
        for m in range(len(srcs)):
            mine(m).start()
            for cp in first(m):
                cp.start()

    def mid(self, srcs, outs, sems):
        me, sibling, chips, c, copy, mine, first = self._ctx(srcs, outs, sems)
        for j, chip in enumerate(chips):
            for m in range(len(srcs)):
                copy(m, 1 + j, (*chip, c), me).wait_recv()
                copy(m, 4 + j, (*chip, c), sibling).start()

    def finish(self, srcs, outs, sems):
        me, sibling, chips, c, copy, mine, first = self._ctx(srcs, outs, sems)
        for m in range(len(srcs)):
            copy(m, 0, sibling, me).wait_recv()
            for j, chip in enumerate(chips):
                copy(m, 4 + j, (*chip, 1 - c), me).wait_recv()
            for cp in first(m):
                cp.wait_send()
            for j, chip in enumerate(chips):
                copy(m, 4 + j, (*chip, c), sibling).wait_send()
            mine(m).wait()


RELATIONS = ((0, 0, 1), (1, 0, 0), (0, 1, 0), (1, 1, 0), (1, 0, 1), (0, 1, 1), (1, 1, 1))
SEM = pl.BlockSpec(memory_space=pltpu.SEMAPHORE)
SPLIT_PARAMS = dict(has_side_effects=pltpu.SideEffectType.DATAFLOW_SIDE_EFFECTING)


def _owner_copies(grad_ref, land_ref, send_sems, recv_sems):
    x, y, c = _coords()
    copies = []
    for s, (fx, fy, fc) in enumerate(RELATIONS):
        px = 1 - x if fx else x
        py = 1 - y if fy else y
        pc = 1 - c if fc else c
        copies.append(pltpu.make_async_remote_copy(
            src_ref=grad_ref.at[2 * px + py, pc], dst_ref=land_ref.at[s], send_sem=send_sems.at[s],
            recv_sem=recv_sems.at[s], device_id=(px, py, pc), device_id_type=MESH))
    return copies


def _send_to_owners(name, grad):
    n = len(RELATIONS)
    land_shape = (n,) + grad.shape[2:]

    def body(g_ref, land_ref, send_sems, recv_sems, g_thru, land_thru, token):
        for cp in _owner_copies(g_ref, land_ref, send_sems, recv_sems):
            cp.start()
        token[...] = jnp.zeros_like(token)

    return pl.pallas_call(
        body, name=name,
        out_shape=(pltpu.SemaphoreType.DMA((n,)), pltpu.SemaphoreType.DMA((n,)), pltpu.HBM(grad.shape, grad.dtype),
                   pltpu.HBM(land_shape, grad.dtype), jax.ShapeDtypeStruct((8, 128), F32)),
        in_specs=(HBM, HBM), out_specs=(SEM, SEM, HBM, HBM, pl.BlockSpec(memory_space=pltpu.VMEM)),
        input_output_aliases={0: 2, 1: 3}, compiler_params=pltpu.CompilerParams(**SPLIT_PARAMS),
    )(pltpu.with_memory_space_constraint(grad, pltpu.HBM),
      pltpu.with_memory_space_constraint(lax.empty(land_shape, grad.dtype), pltpu.HBM))


def _await_owners(name, started, after):
    send_sems, recv_sems, g_thru, land_thru, _ = started

    def body(g_ref, land_ref, send_sems, recv_sems, after_ref, g_out, land_out):
        for cp in _owner_copies(g_ref, land_ref, send_sems, recv_sems):
            cp.wait_send()
            cp.wait_recv()

    return pl.pallas_call(
        body, name=name, out_shape=(pltpu.HBM(g_thru.shape, g_thru.dtype), pltpu.HBM(land_thru.shape, land_thru.dtype)),
        in_specs=(HBM, HBM, SEM, SEM, pl.BlockSpec(memory_space=pl.ANY)), out_specs=(HBM, HBM),
        input_output_aliases={0: 0, 1: 1}, compiler_params=pltpu.CompilerParams(**SPLIT_PARAMS),
    )(g_thru, land_thru, send_sems, recv_sems, after)


def _call(name, main, grid, in_specs, out_specs, out_shape, args, scratch=(), comm=None, prefetch=None, after=None):
    comm = comm or _NoComm()
    n_main = len(in_specs)
    if after is not None:
        in_specs = list(in_specs) + [pl.BlockSpec(after.shape, lambda *_: (0,) * after.ndim)]
        args = tuple(args) + (after,)
    counts = [len(in_specs), len(comm.inputs), len(out_shape), len(comm.out_shapes), len(scratch), len(comm.scratch)]
    n_steps = math.prod(grid)
    hosted = bool(comm.inputs)

    def body(*refs):
        if prefetch is not None:
            refs = refs[1:]
        parts, at = [], 0
        for n in counts:
            parts.append(refs[at:at + n])
            at += n
        ins, c_in, outs, c_out, scr, c_scr = parts
        ins = ins[:n_main]
        step = pl.program_id(0)
        for d in range(1, len(grid)):
            step = step * grid[d] + pl.program_id(d)
        if hosted:
            @pl.when(step == 0)
            def _():
                comm.start(c_in, c_out, c_scr)
        main(ins, outs, scr)
        if hosted:
            @pl.when(step == max(n_steps - 2, 0))
            def _():
                comm.mid(c_in, c_out, c_scr)

            @pl.when(step == n_steps - 1)
            def _():
                comm.finish(c_in, c_out, c_scr)

    all_in = list(in_specs) + [HBM] * counts[1]
    all_out = list(out_specs) + [HBM] * counts[3]
    all_scratch = list(scratch) + list(comm.scratch)
    shapes = list(out_shape) + list(comm.out_shapes)
    if prefetch is None:
        res = pl.pallas_call(body, name=name, grid=grid, in_specs=all_in, out_specs=all_out, out_shape=shapes,
                             scratch_shapes=all_scratch, compiler_params=_cparams())(*args, *comm.inputs)
    else:
        res = pl.pallas_call(
            body, name=name, out_shape=shapes,
            grid_spec=pltpu.PrefetchScalarGridSpec(num_scalar_prefetch=1, grid=grid, in_specs=all_in,
                                                   out_specs=all_out, scratch_shapes=all_scratch),
            compiler_params=_cparams())(prefetch, *args, *comm.inputs)
    return res[:counts[2]], res[counts[2]:]


def _rms_fwd(name, x, g, comm=None):
    T, D = x.shape
    tm = min(T, 512)

    def main(ins, outs, scr):
        x_ref, g_ref = ins
        xv = x_ref[...]
        r = lax.rsqrt(jnp.mean(xv * xv, axis=-1, keepdims=True) + RMS_EPS)
        outs[0][...] = (xv * r * g_ref[...]).astype(outs[0].dtype)

    tile = pl.BlockSpec((tm, D), lambda i: (i, 0))
    (h,), extra = _call(name, main, (T // tm,), [tile, pl.BlockSpec((1, D), lambda i: (0, 0))], [tile],
                        [jax.ShapeDtypeStruct((T, D), BF16)], (x, g), comm=comm)
    return h, extra


def _final_loss_epilogue(scale, out_scale):
    def ep(acc, ex, outs):
        res_ref, g_ref, t_ref = ex
        dx_ref, dxb_ref, dg_ref, loss_ref = outs
        n = acc.shape[-1]
        xv = res_ref[...] + scale * acc
        r = lax.rsqrt(jnp.mean(xv * xv, axis=-1, keepdims=True) + RMS_EPS)
        xhat = xv * r
        err = xhat * g_ref[...] - t_ref[...]

        @pl.when(pl.program_id(0) == 0)
        def _():
            dg_ref[...] = jnp.zeros_like(dg_ref)
            loss_ref[...] = jnp.zeros_like(loss_ref)

        loss_ref[...] += jnp.broadcast_to(jnp.sum(err * err) * (0.5 / n), loss_ref.shape)
        dy = err * (1.0 / n)
        dg_ref[...] += jnp.sum(dy * xhat, axis=0, keepdims=True)
        dxhat = dy * g_ref[...]
        dx = r * (dxhat - xhat * jnp.mean(dxhat * xhat, axis=-1, keepdims=True))
        dx_ref[...] = dx
        dxb_ref[...] = (out_scale * dx).astype(dxb_ref.dtype)
    return ep


def _mm_nstream(name, a, ws, w_sel, w_form, comps, out_dtypes, epilogue, cn, rows=1024, comm=None, after=None):
    T, K = a.shape
    N = ws[0].shape[1]
    rows = min(rows, T)
    assert N % cn == 0 and T % rows == 0
    n_w, n_c = len(ws), len(comps)

    def main(ins, outs, scr):
        a_ref = ins[0]
        w_refs = ins[1:1 + n_w]
        c_refs = ins[1 + n_w:1 + n_w + n_c]

        for r in range(T // rows):
            sl = slice(r * rows, (r + 1) * rows)
            a_blk = a_ref[sl, :]
            dots = [_dot(a_blk, w_ref[...], w_form) for w_ref in w_refs]
            res = epilogue(dots, [c_ref[sl, :] for c_ref in c_refs])
            for o_ref, o in zip(outs, res):
                o_ref[sl, :] = o.astype(o_ref.dtype)

    if w_form == "nt":
        w_specs = [pl.BlockSpec((None, cn, K), functools.partial(lambda j, s: (s, j, 0), s=s)) for s in w_sel]
    else:
        w_specs = [pl.BlockSpec((K, cn), lambda j: (0, j)) for _ in ws]
    chunk = pl.BlockSpec((T, cn), lambda j: (0, j))
    return _call(name, main, (N // cn,), [pl.BlockSpec((T, K), lambda j: (0, 0))] + w_specs + [chunk] * n_c,
                 [chunk] * len(out_dtypes), [jax.ShapeDtypeStruct((T, N), dt) for dt in out_dtypes],
                 (a, *ws, *comps), comm=comm, after=after)


def _mm_mstream(name, as_, ws, w_sel, w_form, extras, outs_desc, epilogue, tm=512, comm=None, after=None):
    T = as_[0].shape[0]
    tm = min(tm, T)
    n_a = len(as_)
    w_shapes = [w.shape[-2:] for w in ws]
    N = w_shapes[0][1] if w_form == "nn" else w_shapes[0][0]

    def main(ins, outs, scr):
        a_refs = ins[:n_a]
        w_refs = ins[n_a:2 * n_a]
        acc = None
        for a_ref, w_ref in zip(a_refs, w_refs):
            d = _dot(a_ref[...], w_ref[...], w_form)
            acc = d if acc is None else acc + d
        epilogue(acc, ins[2 * n_a:], outs)

    kind_spec = {"tile": pl.BlockSpec((tm, N), lambda i: (i, 0)), "vec": pl.BlockSpec((1, N), lambda i: (0, 0))}
    kind_shape = {"tile": (T, N), "vec": (1, N)}
    a_specs = [pl.BlockSpec((tm, a.shape[1]), lambda i: (i, 0)) for a in as_]
    w_specs = []
    for w, s in zip(ws, w_sel):
        if w.ndim == 3:
            w_specs.append(pl.BlockSpec((None,) + tuple(w.shape[1:]), functools.partial(lambda i, s: (s, 0, 0), s=s),
                                        pipeline_mode=pl.Buffered(1)))
        else:
            w_specs.append(pl.BlockSpec(tuple(w.shape), lambda i: (0, 0), pipeline_mode=pl.Buffered(1)))
    args = list(as_) + list(ws) + [e for e, _ in extras]
    return _call(name, main, (T // tm,), a_specs + w_specs + [kind_spec[k] for _, k in extras],
                 [kind_spec[k] for _, k in outs_desc],
                 [jax.ShapeDtypeStruct(kind_shape[k], dt) for dt, k in outs_desc], args, comm=comm, after=after)


def _residual_rms_epilogue(scale):
    def ep(acc, ex, outs):
        xv = ex[0][...] + scale * acc
        outs[0][...] = xv
        r = lax.rsqrt(jnp.mean(xv * xv, axis=-1, keepdims=True) + RMS_EPS)
        outs[1][...] = (xv * r * ex[1][...]).astype(outs[1].dtype)
    return ep


def _rms_bwd_epilogue(out_scale):
    def ep(acc, ex, outs):
        x_ref, g_ref, dres_ref = ex
        dx_ref, dxb_ref, dg_ref = outs
        xv = x_ref[...]
        r = lax.rsqrt(jnp.mean(xv * xv, axis=-1, keepdims=True) + RMS_EPS)
        xhat = xv * r

        @pl.when(pl.program_id(0) == 0)
        def _():
            dg_ref[...] = jnp.zeros_like(dg_ref)

        dg_ref[...] += jnp.sum(acc * xhat, axis=0, keepdims=True)
        dxhat = acc * g_ref[...]
        dx = r * (dxhat - xhat * jnp.mean(dxhat * xhat, axis=-1, keepdims=True)) + dres_ref[...]
        dx_ref[...] = dx
        dxb_ref[...] = (out_scale * dx).astype(dxb_ref.dtype)
    return ep


def _mm_tn(name, a, b, tmo, tno, out_dtype, tk=1024, comm=None, after=None):
    T, Ma = a.shape
    Nb = b.shape[1]
    tk = min(tk, T)
    nk = T // tk

    def main(ins, outs, scr):
        a_ref, b_ref = ins
        (acc_ref,) = scr
        k = pl.program_id(2)

        @pl.when(k == 0)
        def _():
            acc_ref[...] = jnp.zeros_like(acc_ref)

        acc_ref[...] += _dot(a_ref[...], b_ref[...], "tn")

        @pl.when(k == nk - 1)
        def _():
            outs[0][...] = acc_ref[...].astype(outs[0].dtype)

    (out,), extra = _call(
        name, main, (Ma // tmo, Nb // tno, nk),
        [pl.BlockSpec((tk, tmo), lambda i, j, k: (k, i)), pl.BlockSpec((tk, tno), lambda i, j, k: (k, j))],
        [pl.BlockSpec((tmo, tno), lambda i, j, k: (i, j))], [jax.ShapeDtypeStruct((Ma, Nb), out_dtype)],
        (a, b), scratch=[pltpu.VMEM((tmo, tno), F32)], comm=comm, after=after)
    return out, extra


def _swiglu_parts(g, u):
    s = _sigmoid(g)
    silu = g * s
    return [u * (s + silu * (1.0 - s)), silu, silu * u]


def _silu_mul_epilogue(dots, comps):
    g, u = dots
    return _swiglu_parts(g, u)


def _gate_parts_epilogue(dots, comps):
    (g,) = dots
    s = _sigmoid(g)
    silu = g * s
    return [s + silu * (1.0 - s), silu]


def _up_act_epilogue(dots, comps):
    (u,) = dots
    return [u * comps[0].astype(F32), u * comps[1].astype(F32)]


def _dact_epilogue(dots, comps):
    dact = dots[0].astype(BF16)
    return [dact * comps[0], dact * comps[1]]


def _identity_epilogue(dots, comps):
    return list(dots)


def _swap_halves(x):
    lane = lax.broadcasted_iota(jnp.int32, x.shape, 1)
    first = (lane % DK) < (DK // 2)
    return jnp.where(first, pltpu.roll(x, 128 - DK // 2, 1), pltpu.roll(x, DK // 2, 1))


def _rotary(t, cos, sin_signed):
    halves = []
    for p in range(QK_W // 128):
        th = t[:, 128 * p:128 * (p + 1)]
        halves.append(th * cos + _swap_halves(th) * sin_signed)
    return jnp.concatenate(halves, axis=1)


def _rotary_transposed(d, cos, sin_signed):
    halves = []
    for p in range(QK_W // 128):
        dh = d[:, 128 * p:128 * (p + 1)]
        halves.append(dh * cos + _swap_halves(dh * sin_signed))
    return jnp.concatenate(halves, axis=1)


def _log_sigmoid(x):
    return jnp.minimum(x, 0.0) - jnp.log(1.0 + jnp.exp(-jnp.abs(x)))


def _tri_sum(mask, x):
    tri = mask.astype(BF16)
    hi = x.astype(BF16)
    rest = x - hi.astype(F32)
    mid = rest.astype(BF16)
    lo = (rest - mid.astype(F32)).astype(BF16)
    return _dot(tri, hi, "nn") + _dot(tri, mid, "nn") + _dot(tri, lo, "nn")


def _attn_masks():
    row = lax.broadcasted_iota(jnp.int32, (SUPER, SUPER), 0)
    col = lax.broadcasted_iota(jnp.int32, (SUPER, SUPER), 1)
    same = (row // CHUNK) == (col // CHUNK)
    return row, col, same


def _group_inputs(grp, pr, cos, sin_signed, lg, wa2, ba):
    seg = lambda lo, width: pr[:, lo:lo + width].astype(F32)
    if grp == 0:
        q = _rotary(seg(C_RQ, QK_W), cos, sin_signed)
        k = _rotary(seg(C_RK, QK_W), cos, sin_signed) * (DK ** -0.5)
        v = pr[:, C_RV:C_RV + V_W]
        gate = seg(C_RG, V_W)
        pos = lax.broadcasted_iota(jnp.int32, (SUPER, QK_W), 0).astype(F32) + 1.0
        return q, k, v, gate, pos * lg, None, None
    q = seg(C_GQ, QK_W) * (DK ** -0.5)
    k = seg(C_GK, QK_W)
    v = pr[:, C_GV:C_GV + V_W]
    gate = seg(C_GG, V_W)
    glow = pr[:, C_GL:C_GL + GL_W]
    logit = _dot(glow.astype(BF16), wa2.astype(BF16), "nn") + ba
    la = _log_sigmoid(logit) * (1.0 / GATE_NORM)
    row, col, _ = _attn_masks()
    b_cum = _tri_sum(col <= row, la)
    return q, k, v, gate, b_cum, glow, logit


def _decay_factors(q, k, b_cum):
    c = b_cum[SUPER // 2 - 1:SUPER // 2, :]
    bl = b_cum[SUPER - 1:SUPER, :]
    e1 = jnp.exp(b_cum - c)
    e2 = jnp.exp(c - b_cum)
    e_b = jnp.exp(b_cum)
    e_l = jnp.exp(bl - b_cum)
    return dict(e1=e1, e2=e2, eb=e_b, el=e_l, ebl=jnp.exp(bl),
                qp=q * e1, qm=q * e2, kp=k * e1, km=k * e2, qs=q * e_b, kl=k * e_l)


def _state_block_mask():
    r = lax.broadcasted_iota(jnp.int32, (V_W, QK_W), 0)
    c = lax.broadcasted_iota(jnp.int32, (V_W, QK_W), 1)
    return (r // DV) == (c // DK)


def _attn_fwd(proj, cos, sin_signed, lg, wa2p, ba, gn_ret, gn_gla, x_res, w_out, g_next, comm=None):
    T = proj.shape[0]
    n_s = T // SUPER
    D = x_res.shape[1]

    def main(ins, outs, scr):
        pr_ref, cos_ref, sin_ref, lg_ref, wa2_ref, ba_ref, gr_ref, gg_ref, xres_ref, wout_ref, gnext_ref = ins
        o_ref, y_ref, st_ref, x_ref, h_ref = outs
        (s_ref,) = scr
        i = pl.program_id(0)

        @pl.when(i == 0)
        def _():
            s_ref[...] = jnp.zeros_like(s_ref)

        pr = pr_ref
        row, col, same = _attn_masks()
        m1 = col <= row
        m2 = jnp.logical_and(col > row, same)
        lane = lax.broadcasted_iota(jnp.int32, (1, QK_W), 1)
        blockmask = _state_block_mask()
        for grp in range(2):
            q, k, v, gate, b_cum, _, _ = _group_inputs(grp, pr, cos_ref[...], sin_ref[...], lg_ref[...],
                                                      wa2_ref[...], ba_ref[...])
            f = _decay_factors(q, k, b_cum)
            gn = gr_ref[...] if grp == 0 else gg_ref[...]
            s_prev = s_ref[grp]
            st_ref[0, grp] = s_prev
            o_inter = _dot(f["qs"].astype(BF16), s_prev.astype(BF16), "nt")
            kmb = f["km"].astype(BF16)
            kpb = f["kp"].astype(BF16)
            vb = v.astype(BF16)
            heads = [(lane // DK) == h for h in range(HEADS)]
            a1_all = _dot(jnp.concatenate([jnp.where(hm, f["qp"], 0.0).astype(BF16) for hm in heads], axis=0), kmb, "nt")
            a2_all = _dot(jnp.concatenate([jnp.where(hm, f["qm"], 0.0).astype(BF16) for hm in heads], axis=0), kpb, "nt")
            for h in range(HEADS):
                a1 = a1_all[h * SUPER:(h + 1) * SUPER]
                a2 = a2_all[h * SUPER:(h + 1) * SUPER]
                a = jnp.where(m1, a1, jnp.where(m2, a2, 0.0))
                lo = grp * V_W + h * DV
                o_h = _dot(a.astype(BF16), vb[:, h * DV:(h + 1) * DV], "nn") + o_inter[:, h * DV:(h + 1) * DV]
                o_ref[:, lo:lo + DV] = o_h
                r = lax.rsqrt(jnp.mean(o_h * o_h, axis=-1, keepdims=True) + RMS_EPS)
                gte = gate[:, h * DV:(h + 1) * DV]
                y = o_h * r * gn[:, h * DV:(h + 1) * DV] * (gte * _sigmoid(gte))
                y_ref[:, lo:lo + DV] = y.astype(y_ref.dtype)
            upd = _dot(vb, f["kl"].astype(BF16), "tn")
            s_ref[grp] = s_prev * f["ebl"] + jnp.where(blockmask, upd, 0.0)
        xv = xres_ref[...] + _dot(y_ref[...], wout_ref[...], "nn")
        x_ref[...] = xv
        r = lax.rsqrt(jnp.mean(xv * xv, axis=-1, keepdims=True) + RMS_EPS)
        h_ref[...] = (xv * r * gnext_ref[...]).astype(h_ref.dtype)

    const = lambda shape: pl.BlockSpec(shape, lambda i: tuple(0 for _ in shape))
    rows = lambda w: pl.BlockSpec((SUPER, w), lambda i: (i, 0))
    return _call(
        "attn_fwd", main, (n_s,),
        [rows(PROJ_W), rows(128), rows(128),
         const((1, QK_W)), const((GL_W, QK_W)), const((1, QK_W)), const((1, V_W)), const((1, V_W)),
         rows(D), const((2 * V_W, D)), const((1, D))],
        [rows(2 * V_W), rows(2 * V_W), pl.BlockSpec((1, 2, V_W, QK_W), lambda i: (i, 0, 0, 0)), rows(D), rows(D)],
        [jax.ShapeDtypeStruct((T, 2 * V_W), F32), jax.ShapeDtypeStruct((T, 2 * V_W), BF16),
         jax.ShapeDtypeStruct((n_s, 2, V_W, QK_W), F32), jax.ShapeDtypeStruct((T, D), F32),
         jax.ShapeDtypeStruct((T, D), BF16)],
        (proj, cos, sin_signed, lg, wa2p, ba, gn_ret, gn_gla, x_res, w_out, g_next),
        scratch=[pltpu.VMEM((2, V_W, QK_W), F32)], comm=comm)


def _attn_bwd(proj, cos, sin_signed, lg, wa2p, ba, gn_ret, gn_gla, o, dx, w_out, states, comm=None, after=None):
    T = proj.shape[0]
    n_s = T // SUPER
    D = dx.shape[1]

    def main(ins, outs, scr):
        pr_ref, cos_ref, sin_ref, lg_ref, wa2_ref, ba_ref, gr_ref, gg_ref, o_ref, dx_ref, wout_ref, st_ref = ins
        dp_ref, dgr_ref, dgg_ref, dba_ref, dwa_ref = outs
        (ds_ref, dy_ref) = scr
        i = pl.program_id(0)
        dy_ref[...] = _dot(dx_ref[...], wout_ref[...], "nt")

        @pl.when(i == 0)
        def _():
            ds_ref[...] = jnp.zeros_like(ds_ref)
            dgr_ref[...] = jnp.zeros_like(dgr_ref)
            dgg_ref[...] = jnp.zeros_like(dgg_ref)
            dba_ref[...] = jnp.zeros_like(dba_ref)
            dwa_ref[...] = jnp.zeros_like(dwa_ref)

        pr = pr_ref
        cos = cos_ref[...]
        sin_signed = sin_ref[...]
        row, col, same = _attn_masks()
        m1 = col <= row
        m2 = jnp.logical_and(col > row, same)
        m1t = row <= col
        m2t = jnp.logical_and(row > col, same)
        lane = lax.broadcasted_iota(jnp.int32, (1, QK_W), 1)
        blockmask = _state_block_mask()
        for grp in range(2):
            q, k, v, gate, b_cum, glow, logit = _group_inputs(grp, pr, cos, sin_signed, lg_ref[...],
                                                              wa2_ref[...], ba_ref[...])
            f = _decay_factors(q, k, b_cum)
            gn = gr_ref[...] if grp == 0 else gg_ref[...]
            dgn_ref = dgr_ref if grp == 0 else dgg_ref
            do_parts, dgate_parts, dgn_parts = [], [], []
            for h in range(HEADS):
                lo = grp * V_W + h * DV
                o_h = o_ref[:, lo:lo + DV]
                r = lax.rsqrt(jnp.mean(o_h * o_h, axis=-1, keepdims=True) + RMS_EPS)
                n = o_h * r
                gte = gate[:, h * DV:(h + 1) * DV]
                sg = _sigmoid(gte)
                dy_h = dy_ref[:, lo:lo + DV]
                gn_h = gn[:, h * DV:(h + 1) * DV]
                dgate_parts.append(dy_h * n * gn_h * (sg * (1.0 + gte * (1.0 - sg))))
                dz = dy_h * (gte * sg)
                dgn_parts.append(jnp.sum(dz * n, axis=0, keepdims=True))
                dn = dz * gn_h
                do_parts.append(r * (dn - n * jnp.mean(dn * n, axis=-1, keepdims=True)))
            dgn_ref[...] += jnp.concatenate(dgn_parts, axis=1)
            dgate = jnp.concatenate(dgate_parts, axis=1)
            do = jnp.concatenate(do_parts, axis=1)
            dob = do.astype(BF16)
            vb = v.astype(BF16)
            s_prev = st_ref[0, grp]
            ds_new = ds_ref[grp]
            dsb = ds_new.astype(BF16)
            qpb, qmb = f["qp"].astype(BF16), f["qm"].astype(BF16)
            kpb, kmb = f["kp"].astype(BF16), f["km"].astype(BF16)
            dv_parts = []
            heads = [(lane // DK) == h for h in range(HEADS)]
            qp_hs = [jnp.where(hm, f["qp"], 0.0).astype(BF16) for hm in heads]
            qm_hs = [jnp.where(hm, f["qm"], 0.0).astype(BF16) for hm in heads]
            kp_hs = [jnp.where(hm, f["kp"], 0.0).astype(BF16) for hm in heads]
            km_hs = [jnp.where(hm, f["km"], 0.0).astype(BF16) for hm in heads]
            km_stack, kp_stack = jnp.concatenate(km_hs, axis=0), jnp.concatenate(kp_hs, axis=0)
            at1_all = _dot(km_stack, qpb, "nt")
            at2_all = _dot(kp_stack, qmb, "nt")
            da1s, da2s, da1ts, da2ts = [], [], [], []
            for h in range(HEADS):
                at = jnp.where(m1t, at1_all[h * SUPER:(h + 1) * SUPER],
                               jnp.where(m2t, at2_all[h * SUPER:(h + 1) * SUPER], 0.0))
                do_h = dob[:, h * DV:(h + 1) * DV]
                v_h = vb[:, h * DV:(h + 1) * DV]
                dv_parts.append(_dot(at.astype(BF16), do_h, "nn"))
                da = _dot(do_h, v_h, "nt")
                dat = _dot(v_h, do_h, "nt")
                da1s.append(jnp.where(m1, da, 0.0).astype(BF16))
                da2s.append(jnp.where(m2, da, 0.0).astype(BF16))
                da1ts.append(jnp.where(m1t, dat, 0.0).astype(BF16))
                da2ts.append(jnp.where(m2t, dat, 0.0).astype(BF16))
            dqp = _dot(jnp.concatenate(da1s, axis=1), km_stack, "nn")
            dqm = _dot(jnp.concatenate(da2s, axis=1), kp_stack, "nn")
            dkm = _dot(jnp.concatenate(da1ts, axis=1), jnp.concatenate(qp_hs, axis=0), "nn")
            dkp = _dot(jnp.concatenate(da2ts, axis=1), jnp.concatenate(qm_hs, axis=0), "nn")
            klb = f["kl"].astype(BF16)
            qsb = f["qs"].astype(BF16)
            dqs = _dot(dob, s_prev.astype(BF16), "nn")
            dkl = _dot(vb, dsb, "nn")
            dv = jnp.concatenate(dv_parts, axis=1) + _dot(klb, dsb, "nt")
            ds_ref[grp] = ds_new * f["ebl"] + jnp.where(blockmask, _dot(dob, qsb, "tn"), 0.0)
            dq = dqp * f["e1"] + dqm * f["e2"] + dqs * f["eb"]
            dk = dkm * f["e2"] + dkp * f["e1"] + dkl * f["el"]
            if grp == 0:
                dq = _rotary_transposed(dq, cos, sin_signed)
                dk = _rotary_transposed(dk * (DK ** -0.5), cos, sin_signed)
                dp_ref[:, C_RQ:C_RQ + QK_W] = dq.astype(dp_ref.dtype)
                dp_ref[:, C_RK:C_RK + QK_W] = dk.astype(dp_ref.dtype)
                dp_ref[:, C_RV:C_RV + V_W] = dv.astype(dp_ref.dtype)
                dp_ref[:, C_RG:C_RG + V_W] = dgate.astype(dp_ref.dtype)
            else:
                dkl_kl = dkl * klb.astype(F32)
                db = (dqp * qpb.astype(F32) - dkm * kmb.astype(F32) - dqm * qmb.astype(F32)
                      + dkp * kpb.astype(F32) + dqs * qsb.astype(F32) - dkl_kl)
                last = (jnp.sum(dkl_kl, axis=0, keepdims=True)
                        + f["ebl"] * jnp.sum(s_prev * ds_new, axis=0, keepdims=True))
                rowq = lax.broadcasted_iota(jnp.int32, (SUPER, QK_W), 0)
                db = db + jnp.where(rowq == SUPER - 1, last, 0.0)
                dla = _tri_sum(col >= row, db)
                dlogit = dla * (1.0 / GATE_NORM) * (1.0 - _sigmoid(logit))
                dlb = dlogit.astype(BF16)
                dglow = _dot(dlb, wa2_ref[...].astype(BF16), "nt")
                dwa_ref[...] += _dot(glow.astype(BF16), dlb, "tn")
                dba_ref[...] += jnp.sum(dlogit, axis=0, keepdims=True)
                dp_ref[:, C_GQ:C_GQ + QK_W] = (dq * (DK ** -0.5)).astype(dp_ref.dtype)
                dp_ref[:, C_GK:C_GK + QK_W] = dk.astype(dp_ref.dtype)
                dp_ref[:, C_GV:C_GV + V_W] = dv.astype(dp_ref.dtype)
                dp_ref[:, C_GG:C_GG + V_W] = dgate.astype(dp_ref.dtype)
                dp_ref[:, C_GL:C_GL + GL_W] = dglow.astype(dp_ref.dtype)

    rev = lambda i: n_s - 1 - i
    const = lambda shape: pl.BlockSpec(shape, lambda i: tuple(0 for _ in shape))
    return _call(
        "attn_bwd", main, (n_s,),
        [pl.BlockSpec((SUPER, PROJ_W), lambda i: (rev(i), 0)),
         pl.BlockSpec((SUPER, 128), lambda i: (rev(i), 0)), pl.BlockSpec((SUPER, 128), lambda i: (rev(i), 0)),
         const((1, QK_W)), const((GL_W, QK_W)), const((1, QK_W)), const((1, V_W)), const((1, V_W)),
         pl.BlockSpec((SUPER, 2 * V_W), lambda i: (rev(i), 0)),
         pl.BlockSpec((SUPER, D), lambda i: (rev(i), 0)), const((2 * V_W, D)),
         pl.BlockSpec((1, 2, V_W, QK_W), lambda i: (rev(i), 0, 0, 0))],
        [pl.BlockSpec((SUPER, PROJ_W), lambda i: (rev(i), 0)),
         const((1, V_W)), const((1, V_W)), const((1, QK_W)), const((GL_W, QK_W))],
        [jax.ShapeDtypeStruct((T, PROJ_W), BF16),
         jax.ShapeDtypeStruct((1, V_W), F32), jax.ShapeDtypeStruct((1, V_W), F32),
         jax.ShapeDtypeStruct((1, QK_W), F32), jax.ShapeDtypeStruct((GL_W, QK_W), F32)],
        (proj, cos, sin_signed, lg, wa2p, ba, gn_ret, gn_gla, o, dx, w_out, states),
        scratch=[pltpu.VMEM((2, V_W, QK_W), F32), pltpu.VMEM((SUPER, 2 * V_W), F32)], comm=comm, after=after)


def _rotary_tables(T):
    half = DK // 2
    inv = ROPE_BASE ** (-jnp.arange(half, dtype=F32) * 2.0 / DK)
    ang = jnp.arange(T, dtype=F32)[:, None] * inv[None, :]
    cos, sin = jnp.cos(ang), jnp.sin(ang)
    cos_head = jnp.concatenate([cos, cos], axis=1)
    sin_head = jnp.concatenate([-sin, sin], axis=1)
    return jnp.tile(cos_head, (1, 128 // DK)), jnp.tile(sin_head, (1, 128 // DK))


def _sum_devices(name, gathered, m_per):
    def body(g_ref, o_ref):
        acc = g_ref[0:m_per, :]
        for k in range(1, N_DEV):
            acc = acc + g_ref[k * m_per:(k + 1) * m_per, :]
        o_ref[...] = acc

    return pl.pallas_call(body, name=name, out_shape=jax.ShapeDtypeStruct((m_per, 128), F32))(gathered)


def _owner_sums(name, items, owner, comm=None):
    counts = [1 + len(landed) for _, landed in items]

    def main(ins, outs, scr):
        at = 0
        for o_ref, n in zip(outs, counts):
            acc = ins[at][...].astype(F32)
            for l_ref in ins[at + 1:at + n]:
                for j in range(l_ref.shape[0]):
                    acc = acc + l_ref[j].astype(F32)
            o_ref[...] = acc
            at += n

    once = pl.Buffered(1)
    in_specs, out_specs, out_shape, args = [], [], [], []
    for grad, landed in items:
        R, C = grad.shape[-2:]
        in_specs.append(pl.BlockSpec((None, None, R, C), lambda i, s: (s[0], s[1], 0, 0), pipeline_mode=once))
        in_specs += [pl.BlockSpec(tuple(l.shape), lambda i, s: (0, 0, 0), pipeline_mode=once) for l in landed]
        out_specs.append(pl.BlockSpec((R, C), lambda i, s: (0, 0)))
        out_shape.append(jax.ShapeDtypeStruct((R, C), F32))
        args += [grad, *landed]
    return _call(name, main, (1,), in_specs, out_specs, out_shape, args, comm=comm, prefetch=owner)


def _adamw_group(name, items, n_blocks, comm=None):
    n = len(items)

    def main(ins, outs, scr):
        for p in range(n):
            g_ref, w_ref, m_ref, v_ref = ins[4 * p:4 * p + 4]
            d_ref, nm_ref, nv_ref = outs[3 * p:3 * p + 3]
            gv = g_ref[...]
            nm = ADAM_B1 * m_ref[...] + (1.0 - ADAM_B1) * gv
            nv = ADAM_B2 * v_ref[...] + (1.0 - ADAM_B2) * (gv * gv)
            m_hat = nm / (1.0 - ADAM_B1 ** ADAM_STEP)
            v_hat = nv / (1.0 - ADAM_B2 ** ADAM_STEP)
            d_ref[...] = -ADAM_LR * (m_hat / (jnp.sqrt(v_hat) + ADAM_EPS) + ADAM_WD * w_ref[...])
            nm_ref[...] = nm
            nv_ref[...] = nv

    in_specs, out_specs, out_shape, args = [], [], [], []
    for item in items:
        R, C = item[1].shape
        assert R % n_blocks == 0
        spec = pl.BlockSpec((R // n_blocks, C), lambda i: (i, 0))
        in_specs += [spec] * 4
        out_specs += [spec] * 3
        out_shape += [jax.ShapeDtypeStruct((R, C), F32)] * 3
        args += list(item)
    outs, extra = _call(name, main, (n_blocks,), in_specs, out_specs, out_shape, args, comm=comm)
    return [tuple(outs[3 * p:3 * p + 3]) for p in range(n)], extra


SMALL_ORDER = ("ffn1", "mix", "ffn2", "final", "ret", "gla", "b_a")


def kernel(x, ffn1_norm_g, ffn1_w_gate, ffn1_w_up, ffn1_w_down, mix_norm_g, w_in, ret_norm_g, gla_w_a2, gla_b_a, gla_norm_g, w_out, ffn2_norm_g, ffn2_w_gate, ffn2_w_up, ffn2_w_down, final_norm_g, loss_target, m_ffn1_norm_g, m_ffn1_w_gate, m_ffn1_w_up, m_ffn1_w_down, m_mix_norm_g, m_w_in, m_ret_norm_g, m_gla_w_a2, m_gla_b_a, m_gla_norm_g, m_w_out, m_ffn2_norm_g, m_ffn2_w_gate, m_ffn2_w_up, m_ffn2_w_down, m_final_norm_g, v_ffn1_norm_g, v_ffn1_w_gate, v_ffn1_w_up, v_ffn1_w_down, v_mix_norm_g, v_w_in, v_ret_norm_g, v_gla_w_a2, v_gla_b_a, v_gla_norm_g, v_w_out, v_ffn2_norm_g, v_ffn2_w_gate, v_ffn2_w_up, v_ffn2_w_down, v_final_norm_g):
    xi, yi, ci = _coords()
    dev = 4 * xi + 2 * yi + ci
    owner = jnp.stack([2 * xi + yi, ci]).astype(jnp.int32)

    x0, target = x[0], loss_target[0]
    T, D = x0.shape
    fb = ffn1_w_gate.shape[2]
    ib = w_in.shape[2]
    ab = gla_w_a2.shape[2]
    F = N_DEV * fb
    cos, sin_signed = _rotary_tables(T)
    lg = jnp.repeat(jnp.log(1.0 - 2.0 ** (-5.0 - jnp.arange(HEADS, dtype=F32))), DK)[None, :]
    g_final = final_norm_g.reshape(1, D)

    g1_loc = ffn1_w_gate[0].T[None].astype(BF16)
    u1_loc = ffn1_w_up[0].T[None].astype(BF16)
    d1_loc = ffn1_w_down.astype(BF16)
    g2_loc = ffn2_w_gate[0].T[None].astype(BF16)
    u2_loc = ffn2_w_up[0].T[None].astype(BF16)
    d2_loc = ffn2_w_down.astype(BF16)
    in_loc = w_in[0].T.astype(BF16)
    out_loc = w_out[0].astype(BF16)

    h1, (g1,) = _rms_fwd("ffn1_rms", x0, ffn1_norm_g, comm=_AllGather([g1_loc], ["stack"]))
    g1 = g1.reshape(1, F, D)
    (dsl1, sl1), (u1,) = _mm_nstream("ffn1_gate", h1, [g1], [0], "nt", [], [BF16, BF16], _gate_parts_epilogue, cn=256,
                                     comm=_AllGather([u1_loc], ["stack"]))
    u1 = u1.reshape(1, F, D)
    (dsu1, act1), (d1,) = _mm_nstream("ffn1_up", h1, [u1], [0], "nt", [dsl1, sl1], [BF16, BF16],
                                      _up_act_epilogue, cn=256, comm=_AllGather([d1_loc], ["stack"]))
    d1 = d1.reshape(1, F, D)
    f32_tile, bf16_tile, f32_vec = (F32, "tile"), (BF16, "tile"), (F32, "vec")
    (x1, h2), (in_all, a_all) = _mm_mstream(
        "ffn1_down", [act1], [d1], [0], "nn", [(x0, "tile"), (mix_norm_g, "vec")], [f32_tile, bf16_tile],
        _residual_rms_epilogue(0.5), comm=_AllGather([in_loc, gla_w_a2[0]], ["plain", "plain"]))
    w_in_t = jnp.pad(in_all.reshape(1, N_DEV * ib, D), ((0, 0), (0, PROJ_W - N_DEV * ib), (0, 0)))
    wa2 = jnp.transpose(a_all, (1, 0, 2)).reshape(GATE_RANK, N_DEV * ab)
    wa2p = jnp.pad(wa2, ((0, GL_W - GATE_RANK), (0, 0)))

    (proj,), (g2, out_all) = _mm_nstream("mix_proj", h2, [w_in_t], [0], "nt", [], [BF16], _identity_epilogue, cn=640,
                                         comm=_AllGather([g2_loc, out_loc], ["stack", "plain"]))
    w_out_full = out_all.reshape(D, D)
    (o, ymix, states, x2, h3), (u2,) = _attn_fwd(proj, cos, sin_signed, lg, wa2p, gla_b_a, ret_norm_g, gla_norm_g,
                                                 x1, w_out_full, ffn2_norm_g, comm=_AllGather([u2_loc], ["stack"]))
    g2, u2 = g2.reshape(1, F, D), u2.reshape(1, F, D)

    (dsu2, sl2, act2), (d2,) = _mm_nstream(
        "ffn2_up", h3, [g2, u2], [0, 0], "nt", [], [BF16, BF16, BF16], _silu_mul_epilogue, cn=256,
        comm=_AllGather([d2_loc], ["stack"]))
    d2 = d2.reshape(1, F, D)
    (dx3, dy3b, d_final, loss), _ = _mm_mstream(
        "ffn2_down", [act2], [d2], [0], "nn", [(x2, "tile"), (g_final, "vec"), (target, "tile")],
        [f32_tile, bf16_tile, f32_vec, f32_vec], _final_loss_epilogue(0.5, 0.5))

    sent = {}

    def send(nm, grad):
        sent[nm] = _send_to_owners("send_" + nm, grad)
        return sent[nm][4]

    dwd2, _ = _mm_tn("ffn2b_dwd", act2, dy3b, F // 2, D, BF16)
    tok = send("wd2", dwd2.reshape(4, 2, fb, D))
    (dgate2, dup2), _ = _mm_nstream("ffn2b_dact", dy3b, [d2], [0], "nt", [dsu2, sl2], [BF16, BF16],
                                    _dact_epilogue, cn=256, after=tok)
    dwg2, _ = _mm_tn("ffn2b_dwg", dgate2, h3, F // 2, D, BF16)
    tok = send("wg2", dwg2.reshape(4, 2, fb, D))
    dwu2, _ = _mm_tn("ffn2b_dwu", dup2, h3, F // 2, D, BF16, after=tok)
    tok = send("wu2", dwu2.reshape(4, 2, fb, D))
    rms_outs = [f32_tile, bf16_tile, f32_vec]
    (dx2, dx2b, d_g2), _ = _mm_mstream(
        "ffn2b_dh", [dgate2, dup2], [g2, u2], [0, 0], "nn", [(x2, "tile"), (ffn2_norm_g, "vec"), (dx3, "tile")],
        rms_outs, _rms_bwd_epilogue(1.0), after=tok)

    dwout, _ = _mm_tn("mixb_dwout", ymix, dx2b, D, D, BF16)
    tok = send("wout", dwout.reshape(4, 2, D // N_DEV, D))
    (dproj, d_ret, d_gla, d_ba, d_wa2p), _ = _attn_bwd(
        proj, cos, sin_signed, lg, wa2p, gla_b_a, ret_norm_g, gla_norm_g, o, dx2b, w_out_full, states, after=tok)
    dwin_t, _ = _mm_tn("mixb_dwin", dproj, h2, 640, D, BF16, tk=2048)
    tok = send("win", dwin_t[:N_DEV * ib].reshape(4, 2, ib, D))
    (dx1, dy1b, d_gmix), _ = _mm_mstream(
        "mixb_dh", [dproj], [w_in_t], [0], "nn", [(x1, "tile"), (mix_norm_g, "vec"), (dx2, "tile")],
        rms_outs, _rms_bwd_epilogue(0.5), after=tok)

    dwd1, _ = _mm_tn("ffn1b_dwd", act1, dy1b, F // 2, D, BF16)
    tok = send("wd1", dwd1.reshape(4, 2, fb, D))
    (dgate1, dup1), _ = _mm_nstream("ffn1b_dact", dy1b, [d1], [0], "nt", [dsu1, sl1], [BF16, BF16],
                                    _dact_epilogue, cn=256, after=tok)
    dwg1, _ = _mm_tn("ffn1b_dwg", dgate1, h1, F // 2, D, BF16)
    tok = send("wg1", dwg1.reshape(4, 2, fb, D))
    dwu1, _ = _mm_tn("ffn1b_dwu", dup1, h1, F // 2, D, BF16, after=tok)
    tok = send("wu1", dwu1.reshape(4, 2, fb, D))
    (dx0, _, d_g1), _ = _mm_mstream(
        "ffn1b_dh", [dgate1, dup1], [g1, u1], [0, 0], "nn", [(x0, "tile"), (ffn1_norm_g, "vec"), (dx1, "tile")],
        rms_outs, _rms_bwd_epilogue(1.0), after=tok)

    small = dict(ffn1=d_g1, mix=d_gmix, ffn2=d_g2, final=d_final, ret=d_ret, gla=d_gla, b_a=d_ba)
    flat = jnp.concatenate([small[k].reshape(-1) for k in SMALL_ORDER]
                           + [d_wa2p[:GATE_RANK].reshape(-1), loss[0, :128]])
    rows = -(-flat.shape[0] // 128)
    rows = -(-rows // 8) * 8
    packed = jnp.pad(flat, (0, rows * 128 - flat.shape[0])).reshape(rows, 128)

    transposed = ("ffn1_w_gate", "ffn1_w_up", "ffn2_w_gate", "ffn2_w_up", "w_in")

    def to_2d(nm, a):
        if nm in transposed:
            return a[0].T
        return a.reshape((1, a.shape[0]) if a.ndim == 1 else a.shape[-2:])

    def from_2d(nm, a):
        return a.T[None] if nm in transposed else a.reshape(params[nm][0].shape)

    def arrived(nm, after):
        grad, landed = _await_owners("await_" + nm, sent[nm], after)
        return grad, [landed]

    sums_a, (gathered,) = _owner_sums(
        "sum_a", [arrived(nm, dx0) for nm in ("wg2", "wu2", "wd2", "win", "wout")], owner,
        comm=_AllGather([packed], ["plain"]))
    params = dict(
        ffn2_w_gate=(ffn2_w_gate, m_ffn2_w_gate, v_ffn2_w_gate), ffn2_w_up=(ffn2_w_up, m_ffn2_w_up, v_ffn2_w_up),
        ffn2_w_down=(ffn2_w_down, m_ffn2_w_down, v_ffn2_w_down), w_in=(w_in, m_w_in, v_w_in),
        w_out=(w_out, m_w_out, v_w_out), ffn1_w_gate=(ffn1_w_gate, m_ffn1_w_gate, v_ffn1_w_gate),
        ffn1_w_up=(ffn1_w_up, m_ffn1_w_up, v_ffn1_w_up), ffn1_w_down=(ffn1_w_down, m_ffn1_w_down, v_ffn1_w_down),
        ffn1_norm_g=(ffn1_norm_g, m_ffn1_norm_g, v_ffn1_norm_g), mix_norm_g=(mix_norm_g, m_mix_norm_g, v_mix_norm_g),
        ret_norm_g=(ret_norm_g, m_ret_norm_g, v_ret_norm_g), gla_w_a2=(gla_w_a2, m_gla_w_a2, v_gla_w_a2),
        gla_b_a=(gla_b_a, m_gla_b_a, v_gla_b_a), gla_norm_g=(gla_norm_g, m_gla_norm_g, v_gla_norm_g),
        ffn2_norm_g=(ffn2_norm_g, m_ffn2_norm_g, v_ffn2_norm_g), final_norm_g=(final_norm_g, m_final_norm_g, v_final_norm_g))
    grads, updates = {}, {}

    def run_adam(name, names, grad_2d, n_blocks):
        items = [(grad_2d[nm],) + tuple(to_2d(nm, a) for a in params[nm]) for nm in names]
        res, _ = _adamw_group(name, items, n_blocks)
        for nm, r in zip(names, res):
            grads[nm] = from_2d(nm, grad_2d[nm])
            updates[nm] = tuple(from_2d(nm, a) for a in r)
        return res

    grads_a = {"ffn2_w_gate": sums_a[0], "ffn2_w_up": sums_a[1], "ffn2_w_down": sums_a[2], "w_out": sums_a[4]}
    run_adam("adamw_w_in", ["w_in"], {"w_in": sums_a[3]}, 1)
    done_a = run_adam("adamw_a", list(grads_a), grads_a, 4)[0][0]
    sums_b, _ = _owner_sums("sum_b", [arrived(nm, done_a) for nm in ("wg1", "wu1", "wd1")], owner)
    grads_b = {"ffn1_w_gate": sums_b[0], "ffn1_w_up": sums_b[1], "ffn1_w_down": sums_b[2]}
    run_adam("adamw_b", list(grads_b), grads_b, 4)

    total = _sum_devices("sum_small", gathered.reshape(N_DEV * rows, 128), rows).reshape(-1)
    sizes = [small[k].size for k in SMALL_ORDER] + [GATE_RANK * QK_W, 128]
    offs = [0]
    for s in sizes:
        offs.append(offs[-1] + s)
    pieces = [total[offs[i]:offs[i + 1]] for i in range(len(sizes))]
    g_small = {k: pieces[i].reshape(small[k].shape) for i, k in enumerate(SMALL_ORDER)}
    g_wa2_full = pieces[len(SMALL_ORDER)].reshape(GATE_RANK, QK_W)
    g_wa2 = lax.dynamic_slice(g_wa2_full, (0, dev * ab), (GATE_RANK, ab))
    loss_total = pieces[len(SMALL_ORDER) + 1][0]

    small_grads = {"ffn1_norm_g": g_small["ffn1"], "mix_norm_g": g_small["mix"], "ret_norm_g": g_small["ret"],
                   "gla_w_a2": g_wa2, "gla_b_a": g_small["b_a"], "gla_norm_g": g_small["gla"],
                   "ffn2_norm_g": g_small["ffn2"], "final_norm_g": g_small["final"]}
    run_adam("adamw_small", list(small_grads), small_grads, 1)

    order = ("ffn1_norm_g", "ffn1_w_gate", "ffn1_w_up", "ffn1_w_down", "mix_norm_g", "w_in", "ret_norm_g", "gla_w_a2",
             "gla_b_a", "gla_norm_g", "w_out", "ffn2_norm_g", "ffn2_w_gate", "ffn2_w_up", "ffn2_w_down", "final_norm_g")
    return (loss_total, dx0[None], *[grads[nm] for nm in order], *[updates[nm][0] for nm in order],
            *[updates[nm][1] for nm in order], *[updates[nm][2] for nm in order])
```

```python
import functools
import math

import jax
import jax.numpy as jnp
from jax import lax
from jax.experimental import pallas as pl
from jax.experimental.pallas import tpu as pltpu

F32 = jnp.float32
BF16 = jnp.bfloat16
MESH = pl.DeviceIdType.MESH
HBM = pl.BlockSpec(memory_space=pltpu.HBM)

N_DEV = 8
RMS_EPS = 1e-6
ROPE_BASE = 10000.0
HEADS = 4
DK = 64
DV = 128
QK_W = HEADS * DK
V_W = HEADS * DV
GATE_RANK = 16
GATE_NORM = 16.0
CHUNK = 64
SUPER = 256
PROJ_W = 3200
C_RQ, C_RK, C_RV, C_RG, C_GQ, C_GK, C_GV, C_GG, C_GL = 0, 256, 512, 1024, 1536, 1792, 2048, 2560, 3072
GL_W = PROJ_W - C_GL
ADAM_LR, ADAM_B1, ADAM_B2, ADAM_EPS, ADAM_WD, ADAM_STEP = 0.001, 0.9, 0.999, 1e-08, 0.01, 10
VMEM_LIMIT_V7X = 52 * 1024 * 1024


def _cparams(**kw):
    return pltpu.CompilerParams(vmem_limit_bytes=VMEM_LIMIT_V7X, **kw)


def _dot(a, b, form, precision=None):
    dims = {"nn": (((1,), (0,)), ((), ())), "nt": (((1,), (1,)), ((), ())), "tn": (((0,), (0,)), ((), ()))}[form]
    return lax.dot_general(a, b, dims, preferred_element_type=F32, precision=precision)


def _sigmoid(x):
    return 1.0 / (1.0 + jnp.exp(-x))


def _coords():
    return lax.axis_index("x"), lax.axis_index("y"), lax.axis_index("c")


class _NoComm:
    inputs, out_shapes, scratch = (), (), ()


class _AllGather:
    def __init__(self, arrays, kinds):
        self.inputs = tuple(arrays)
        self.kinds = tuple(kinds)
        n = len(arrays)
        self.out_shapes = tuple(
            jax.ShapeDtypeStruct((a.shape[0], N_DEV) + a.shape[1:] if k == "stack" else (N_DEV,) + a.shape, a.dtype)
            for a, k in zip(arrays, kinds))
        self.scratch = (pltpu.SemaphoreType.DMA((n, 7)), pltpu.SemaphoreType.DMA((n, 7)),
                        pltpu.SemaphoreType.DMA((n,)))

    def _ctx(self, srcs, outs, sems):
        send_sems, recv_sems, local_sems = sems
        x, y, c = _coords()
        me, sibling = (x, y, c), (x, y, 1 - c)
        chips = [(1 - x, y), (x, 1 - y), (1 - x, 1 - y)]

        def blk(m, dev):
            k = 4 * dev[0] + 2 * dev[1] + dev[2]
            return outs[m].at[:, k] if self.kinds[m] == "stack" else outs[m].at[k]

        def copy(m, s, block, to, src=None):
            return pltpu.make_async_remote_copy(
                src_ref=blk(m, block) if src is None else src, dst_ref=blk(m, block),
                send_sem=send_sems.at[m, s], recv_sem=recv_sems.at[m, s], device_id=to, device_id_type=MESH)

        def mine(m):
            return pltpu.make_async_copy(srcs[m], blk(m, me), local_sems.at[m])

        def first(m):
            return [copy(m, 0, me, sibling, src=srcs[m])] + [
                copy(m, 1 + j, me, (*chip, c), src=srcs[m]) for j, chip in enumerate(chips)]

        return me, sibling, chips, c, copy, mine, first

    def start(self, srcs, outs, sems):
        me, sibling, chips, c, copy, mine, first = self._ctx(srcs, outs, sems)
        for m in range(len(srcs)):
            mine(m).start()
            for cp in first(m):
                cp.start()

    def mid(self, srcs, outs, sems):
        me, sibling, chips, c, copy, mine, first = self._ctx(srcs, outs, sems)
        for j, chip in enumerate(chips):
            for m in range(len(srcs)):
                copy(m, 1 + j, (*chip, c), me).wait_recv()
                copy(m, 4 + j, (*chip, c), sibling).start()

    def finish(self, srcs, outs, sems):
        me, sibling, chips, c, copy, mine, first = self._ctx(srcs, outs, sems)
        for m in range(len(srcs)):
            copy(m, 0, sibling, me).wait_recv()
            for j, chip in enumerate(chips):
                copy(m, 4 + j, (*chip, 1 - c), me).wait_recv()
            for cp in first(m):
                cp.wait_send()
            for j, chip in enumerate(chips):
                copy(m, 4 + j, (*chip, c), sibling).wait_send()
            mine(m).wait()


RELATIONS = ((0, 0, 1), (1, 0, 0), (0, 1, 0), (1, 1, 0), (1, 0, 1), (0, 1, 1), (1, 1, 1))
SEM = pl.BlockSpec(memory_space=pltpu.SEMAPHORE)
SPLIT_PARAMS = dict(has_side_effects=pltpu.SideEffectType.DATAFLOW_SIDE_EFFECTING)


def _owner_copies(grad_ref, land_ref, send_sems, recv_sems):
    x, y, c = _coords()
    copies = []
    for s, (fx, fy, fc) in enumerate(RELATIONS):
        px = 1 - x if fx else x
        py = 1 - y if fy else y
        pc = 1 - c if fc else c
        copies.append(pltpu.make_async_remote_copy(
            src_ref=grad_ref.at[2 * px + py, pc], dst_ref=land_ref.at[s], send_sem=send_sems.at[s],
            recv_sem=recv_sems.at[s], device_id=(px, py, pc), device_id_type=MESH))
    return copies


def _send_to_owners(name, grads):
    n, k = len(RELATIONS), len(grads)
    land_shapes = [(n,) + g.shape[2:] for g in grads]

    def body(*refs):
        ins, outs = refs[:2 * k], refs[2 * k:]
        for j in range(k):
            for cp in _owner_copies(ins[2 * j], ins[2 * j + 1], outs[4 * j], outs[4 * j + 1]):
                cp.start()
        outs[-1][...] = jnp.zeros_like(outs[-1])

    out_shape, out_specs, args, aliases = [], [], [], {}
    for j, (g, land_shape) in enumerate(zip(grads, land_shapes)):
        out_shape += [pltpu.SemaphoreType.DMA((n,)), pltpu.SemaphoreType.DMA((n,)), pltpu.HBM(g.shape, g.dtype),
                      pltpu.HBM(land_shape, g.dtype)]
        out_specs += [SEM, SEM, HBM, HBM]
        args += [pltpu.with_memory_space_constraint(g, pltpu.HBM),
                 pltpu.with_memory_space_constraint(lax.empty(land_shape, g.dtype), pltpu.HBM)]
        aliases.update({2 * j: 4 * j + 2, 2 * j + 1: 4 * j + 3})
    res = pl.pallas_call(
        body, name=name, out_shape=out_shape + [jax.ShapeDtypeStruct((8, 128), F32)],
        in_specs=[HBM] * (2 * k), out_specs=out_specs + [pl.BlockSpec(memory_space=pltpu.VMEM)],
        input_output_aliases=aliases, compiler_params=pltpu.CompilerParams(**SPLIT_PARAMS),
    )(*args)
    return [tuple(res[4 * j:4 * j + 4]) + (res[-1],) for j in range(k)]


def _await_owners(name, started, after):
    send_sems, recv_sems, g_thru, land_thru, _ = started

    def body(g_ref, land_ref, send_sems, recv_sems, after_ref, g_out, land_out):
        for cp in _owner_copies(g_ref, land_ref, send_sems, recv_sems):
            cp.wait_send()
            cp.wait_recv()

    return pl.pallas_call(
        body, name=name, out_shape=(pltpu.HBM(g_thru.shape, g_thru.dtype), pltpu.HBM(land_thru.shape, land_thru.dtype)),
        in_specs=(HBM, HBM, SEM, SEM, pl.BlockSpec(memory_space=pl.ANY)), out_specs=(HBM, HBM),
        input_output_aliases={0: 0, 1: 1}, compiler_params=pltpu.CompilerParams(**SPLIT_PARAMS),
    )(g_thru, land_thru, send_sems, recv_sems, after)


def _call(name, main, grid, in_specs, out_specs, out_shape, args, scratch=(), comm=None, prefetch=None, after=None):
    comm = comm or _NoComm()
    n_main = len(in_specs)
    if after is not None:
        in_specs = list(in_specs) + [pl.BlockSpec(after.shape, lambda *_: (0,) * after.ndim)]
        args = tuple(args) + (after,)
    counts = [len(in_specs), len(comm.inputs), len(out_shape), len(comm.out_shapes), len(scratch), len(comm.scratch)]
    n_steps = math.prod(grid)
    hosted = bool(comm.inputs)

    def body(*refs):
        if prefetch is not None:
            refs = refs[1:]
        parts, at = [], 0
        for n in counts:
            parts.append(refs[at:at + n])
            at += n
        ins, c_in, outs, c_out, scr, c_scr = parts
        ins = ins[:n_main]
        step = pl.program_id(0)
        for d in range(1, len(grid)):
            step = step * grid[d] + pl.program_id(d)
        if hosted:
            @pl.when(step == 0)
            def _():
                comm.start(c_in, c_out, c_scr)
        main(ins, outs, scr)
        if hosted:
            @pl.when(step == max(n_steps - 2, 0))
            def _():
                comm.mid(c_in, c_out, c_scr)

            @pl.when(step == n_steps - 1)
            def _():
                comm.finish(c_in, c_out, c_scr)

    all_in = list(in_specs) + [HBM] * counts[1]
    all_out = list(out_specs) + [HBM] * counts[3]
    all_scratch = list(scratch) + list(comm.scratch)
    shapes = list(out_shape) + list(comm.out_shapes)
    if prefetch is None:
        res = pl.pallas_call(body, name=name, grid=grid, in_specs=all_in, out_specs=all_out, out_shape=shapes,
                             scratch_shapes=all_scratch, compiler_params=_cparams())(*args, *comm.inputs)
    else:
        res = pl.pallas_call(
            body, name=name, out_shape=shapes,
            grid_spec=pltpu.PrefetchScalarGridSpec(num_scalar_prefetch=1, grid=grid, in_specs=all_in,
                                                   out_specs=all_out, scratch_shapes=all_scratch),
            compiler_params=_cparams())(prefetch, *args, *comm.inputs)
    return res[:counts[2]], res[counts[2]:]


def _rms_fwd(name, x, g, comm=None):
    T, D = x.shape
    tm = min(T, 512)

    def main(ins, outs, scr):
        x_ref, g_ref = ins
        xv = x_ref[...]
        r = lax.rsqrt(jnp.mean(xv * xv, axis=-1, keepdims=True) + RMS_EPS)
        outs[0][...] = (xv * r * g_ref[...]).astype(outs[0].dtype)

    tile = pl.BlockSpec((tm, D), lambda i: (i, 0))
    (h,), extra = _call(name, main, (T // tm,), [tile, pl.BlockSpec((1, D), lambda i: (0, 0))], [tile],
                        [jax.ShapeDtypeStruct((T, D), BF16)], (x, g), comm=comm)
    return h, extra


def _final_loss_epilogue(scale, out_scale):
    def ep(acc, ex, outs):
        res_ref, g_ref, t_ref = ex
        dx_ref, dxb_ref, dg_ref, loss_ref = outs
        n = acc.shape[-1]
        xv = res_ref[...] + scale * acc
        r = lax.rsqrt(jnp.mean(xv * xv, axis=-1, keepdims=True) + RMS_EPS)
        xhat = xv * r
        err = xhat * g_ref[...] - t_ref[...]

        @pl.when(pl.program_id(0) == 0)
        def _():
            dg_ref[...] = jnp.zeros_like(dg_ref)
            loss_ref[...] = jnp.zeros_like(loss_ref)

        loss_ref[...] += jnp.broadcast_to(jnp.sum(err * err) * (0.5 / n), loss_ref.shape)
        dy = err * (1.0 / n)
        dg_ref[...] += jnp.sum(dy * xhat, axis=0, keepdims=True)
        dxhat = dy * g_ref[...]
        dx = r * (dxhat - xhat * jnp.mean(dxhat * xhat, axis=-1, keepdims=True))
        dx_ref[...] = dx
        dxb_ref[...] = (out_scale * dx).astype(dxb_ref.dtype)
    return ep


def _mm_nstream(name, a, ws, w_sel, w_form, comps, out_dtypes, epilogue, cn, rows=1024, comm=None, after=None):
    T, K = a.shape
    N = ws[0].shape[1]
    rows = min(rows, T)
    assert N % cn == 0 and T % rows == 0
    n_w, n_c = len(ws), len(comps)

    def main(ins, outs, scr):
        a_ref = ins[0]
        w_refs = ins[1:1 + n_w]
        c_refs = ins[1 + n_w:1 + n_w + n_c]

        for r in range(T // rows):
            sl = slice(r * rows, (r + 1) * rows)
            a_blk = a_ref[sl, :]
            dots = [_dot(a_blk, w_ref[...], w_form) for w_ref in w_refs]
            res = epilogue(dots, [c_ref[sl, :] for c_ref in c_refs])
            for o_ref, o in zip(outs, res):
                o_ref[sl, :] = o.astype(o_ref.dtype)

    if w_form == "nt":
        w_specs = [pl.BlockSpec((None, cn, K), functools.partial(lambda j, s: (s, j, 0), s=s)) for s in w_sel]
    else:
        w_specs = [pl.BlockSpec((K, cn), lambda j: (0, j)) for _ in ws]
    chunk = pl.BlockSpec((T, cn), lambda j: (0, j))
    return _call(name, main, (N // cn,), [pl.BlockSpec((T, K), lambda j: (0, 0))] + w_specs + [chunk] * n_c,
                 [chunk] * len(out_dtypes), [jax.ShapeDtypeStruct((T, N), dt) for dt in out_dtypes],
                 (a, *ws, *comps), comm=comm, after=after)


def _mm_mstream(name, as_, ws, w_sel, w_form, extras, outs_desc, epilogue, tm=512, comm=None, after=None):
    T = as_[0].shape[0]
    tm = min(tm, T)
    n_a = len(as_)
    w_shapes = [w.shape[-2:] for w in ws]
    N = w_shapes[0][1] if w_form == "nn" else w_shapes[0][0]

    def main(ins, outs, scr):
        a_refs = ins[:n_a]
        w_refs = ins[n_a:2 * n_a]
        acc = None
        for a_ref, w_ref in zip(a_refs, w_refs):
            d = _dot(a_ref[...], w_ref[...], w_form)
            acc = d if acc is None else acc + d
        epilogue(acc, ins[2 * n_a:], outs)

    kind_spec = {"tile": pl.BlockSpec((tm, N), lambda i: (i, 0)), "vec": pl.BlockSpec((1, N), lambda i: (0, 0))}
    kind_shape = {"tile": (T, N), "vec": (1, N)}
    a_specs = [pl.BlockSpec((tm, a.shape[1]), lambda i: (i, 0)) for a in as_]
    w_specs = []
    for w, s in zip(ws, w_sel):
        if w.ndim == 3:
            w_specs.append(pl.BlockSpec((None,) + tuple(w.shape[1:]), functools.partial(lambda i, s: (s, 0, 0), s=s),
                                        pipeline_mode=pl.Buffered(1)))
        else:
            w_specs.append(pl.BlockSpec(tuple(w.shape), lambda i: (0, 0), pipeline_mode=pl.Buffered(1)))
    args = list(as_) + list(ws) + [e for e, _ in extras]
    return _call(name, main, (T // tm,), a_specs + w_specs + [kind_spec[k] for _, k in extras],
                 [kind_spec[k] for _, k in outs_desc],
                 [jax.ShapeDtypeStruct(kind_shape[k], dt) for dt, k in outs_desc], args, comm=comm, after=after)


def _residual_rms_epilogue(scale):
    def ep(acc, ex, outs):
        xv = ex[0][...] + scale * acc
        outs[0][...] = xv
        r = lax.rsqrt(jnp.mean(xv * xv, axis=-1, keepdims=True) + RMS_EPS)
        outs[1][...] = (xv * r * ex[1][...]).astype(outs[1].dtype)
    return ep


def _rms_bwd_epilogue(out_scale):
    def ep(acc, ex, outs):
        x_ref, g_ref, dres_ref = ex
        dx_ref, dxb_ref, dg_ref = outs
        xv = x_ref[...]
        r = lax.rsqrt(jnp.mean(xv * xv, axis=-1, keepdims=True) + RMS_EPS)
        xhat = xv * r

        @pl.when(pl.program_id(0) == 0)
        def _():
            dg_ref[...] = jnp.zeros_like(dg_ref)

        dg_ref[...] += jnp.sum(acc * xhat, axis=0, keepdims=True)
        dxhat = acc * g_ref[...]
        dx = r * (dxhat - xhat * jnp.mean(dxhat * xhat, axis=-1, keepdims=True)) + dres_ref[...]
        dx_ref[...] = dx
        dxb_ref[...] = (out_scale * dx).astype(dxb_ref.dtype)
    return ep


def _mm_tn(name, a, b, tmo, tno, out_dtype, tk=1024, comm=None, after=None):
    T, Ma = a.shape
    Nb = b.shape[1]
    tk = min(tk, T)
    nk = T // tk

    def main(ins, outs, scr):
        a_ref, b_ref = ins
        (acc_ref,) = scr
        k = pl.program_id(2)

        @pl.when(k == 0)
        def _():
            acc_ref[...] = jnp.zeros_like(acc_ref)

        acc_ref[...] += _dot(a_ref[...], b_ref[...], "tn")

        @pl.when(k == nk - 1)
        def _():
            outs[0][...] = acc_ref[...].astype(outs[0].dtype)

    (out,), extra = _call(
        name, main, (Ma // tmo, Nb // tno, nk),
        [pl.BlockSpec((tk, tmo), lambda i, j, k: (k, i)), pl.BlockSpec((tk, tno), lambda i, j, k: (k, j))],
        [pl.BlockSpec((tmo, tno), lambda i, j, k: (i, j))], [jax.ShapeDtypeStruct((Ma, Nb), out_dtype)],
        (a, b), scratch=[pltpu.VMEM((tmo, tno), F32)], comm=comm, after=after)
    return out, extra


def _swiglu_parts(g, u):
    s = _sigmoid(g)
    silu = g * s
    return [u * (s + silu * (1.0 - s)), silu, silu * u]


def _silu_mul_epilogue(dots, comps):
    g, u = dots
    return _swiglu_parts(g, u)


def _gate_parts_epilogue(dots, comps):
    (g,) = dots
    s = _sigmoid(g)
    silu = g * s
    return [s + silu * (1.0 - s), silu]


def _up_act_epilogue(dots, comps):
    (u,) = dots
    return [u * comps[0].astype(F32), u * comps[1].astype(F32)]


def _dact_epilogue(dots, comps):
    dact = dots[0].astype(BF16)
    return [dact * comps[0], dact * comps[1]]


def _identity_epilogue(dots, comps):
    return list(dots)


def _swap_halves(x):
    lane = lax.broadcasted_iota(jnp.int32, x.shape, 1)
    first = (lane % DK) < (DK // 2)
    return jnp.where(first, pltpu.roll(x, 128 - DK // 2, 1), pltpu.roll(x, DK // 2, 1))


def _rotary(t, cos, sin_signed):
    halves = []
    for p in range(QK_W // 128):
        th = t[:, 128 * p:128 * (p + 1)]
        halves.append(th * cos + _swap_halves(th) * sin_signed)
    return jnp.concatenate(halves, axis=1)


def _rotary_transposed(d, cos, sin_signed):
    halves = []
    for p in range(QK_W // 128):
        dh = d[:, 128 * p:128 * (p + 1)]
        halves.append(dh * cos + _swap_halves(dh * sin_signed))
    return jnp.concatenate(halves, axis=1)


def _log_sigmoid(x):
    return jnp.minimum(x, 0.0) - jnp.log(1.0 + jnp.exp(-jnp.abs(x)))


def _tri_sum(mask, x):
    tri = mask.astype(BF16)
    hi = x.astype(BF16)
    rest = x - hi.astype(F32)
    mid = rest.astype(BF16)
    lo = (rest - mid.astype(F32)).astype(BF16)
    return _dot(tri, hi, "nn") + _dot(tri, mid, "nn") + _dot(tri, lo, "nn")


def _attn_masks():
    row = lax.broadcasted_iota(jnp.int32, (SUPER, SUPER), 0)
    col = lax.broadcasted_iota(jnp.int32, (SUPER, SUPER), 1)
    same = (row // CHUNK) == (col // CHUNK)
    return row, col, same


def _group_inputs(grp, pr, cos, sin_signed, lg, wa2, ba):
    seg = lambda lo, width: pr[:, lo:lo + width].astype(F32)
    if grp == 0:
        q = _rotary(seg(C_RQ, QK_W), cos, sin_signed)
        k = _rotary(seg(C_RK, QK_W), cos, sin_signed) * (DK ** -0.5)
        v = pr[:, C_RV:C_RV + V_W]
        gate = seg(C_RG, V_W)
        pos = lax.broadcasted_iota(jnp.int32, (SUPER, QK_W), 0).astype(F32) + 1.0
        return q, k, v, gate, pos * lg, None, None
    q = seg(C_GQ, QK_W) * (DK ** -0.5)
    k = seg(C_GK, QK_W)
    v = pr[:, C_GV:C_GV + V_W]
    gate = seg(C_GG, V_W)
    glow = pr[:, C_GL:C_GL + GL_W]
    logit = _dot(glow.astype(BF16), wa2.astype(BF16), "nn") + ba
    la = _log_sigmoid(logit) * (1.0 / GATE_NORM)
    row, col, _ = _attn_masks()
    b_cum = _tri_sum(col <= row, la)
    return q, k, v, gate, b_cum, glow, logit


def _decay_factors(q, k, b_cum):
    c = b_cum[SUPER // 2 - 1:SUPER // 2, :]
    bl = b_cum[SUPER - 1:SUPER, :]
    e1 = jnp.exp(b_cum - c)
    e2 = jnp.exp(c - b_cum)
    e_b = jnp.exp(b_cum)
    e_l = jnp.exp(bl - b_cum)
    return dict(e1=e1, e2=e2, eb=e_b, el=e_l, ebl=jnp.exp(bl),
                qp=q * e1, qm=q * e2, kp=k * e1, km=k * e2, qs=q * e_b, kl=k * e_l)


def _state_block_mask():
    r = lax.broadcasted_iota(jnp.int32, (V_W, QK_W), 0)
    c = lax.broadcasted_iota(jnp.int32, (V_W, QK_W), 1)
    return (r // DV) == (c // DK)


def _attn_fwd(proj, cos, sin_signed, lg, wa2p, ba, gn_ret, gn_gla, x_res, w_out, g_next, comm=None):
    T = proj.shape[0]
    n_s = T // SUPER
    D = x_res.shape[1]

    def main(ins, outs, scr):
        pr_ref, cos_ref, sin_ref, lg_ref, wa2_ref, ba_ref, gr_ref, gg_ref, xres_ref, wout_ref, gnext_ref = ins
        o_ref, y_ref, st_ref, x_ref, h_ref = outs
        (s_ref,) = scr
        i = pl.program_id(0)

        @pl.when(i == 0)
        def _():
            s_ref[...] = jnp.zeros_like(s_ref)

        pr = pr_ref
        row, col, same = _attn_masks()
        m1 = col <= row
        m2 = jnp.logical_and(col > row, same)
        lane = lax.broadcasted_iota(jnp.int32, (1, QK_W), 1)
        blockmask = _state_block_mask()
        for grp in range(2):
            q, k, v, gate, b_cum, _, _ = _group_inputs(grp, pr, cos_ref[...], sin_ref[...], lg_ref[...],
                                                      wa2_ref[...], ba_ref[...])
            f = _decay_factors(q, k, b_cum)
            gn = gr_ref[...] if grp == 0 else gg_ref[...]
            s_prev = s_ref[grp]
            st_ref[0, grp] = s_prev
            o_inter = _dot(f["qs"].astype(BF16), s_prev.astype(BF16), "nt")
            kmb = f["km"].astype(BF16)
            kpb = f["kp"].astype(BF16)
            vb = v.astype(BF16)
            heads = [(lane // DK) == h for h in range(HEADS)]
            a1_all = _dot(jnp.concatenate([jnp.where(hm, f["qp"], 0.0).astype(BF16) for hm in heads], axis=0), kmb, "nt")
            a2_all = _dot(jnp.concatenate([jnp.where(hm, f["qm"], 0.0).astype(BF16) for hm in heads], axis=0), kpb, "nt")
            for h in range(HEADS):
                a1 = a1_all[h * SUPER:(h + 1) * SUPER]
                a2 = a2_all[h * SUPER:(h + 1) * SUPER]
                a = jnp.where(m1, a1, jnp.where(m2, a2, 0.0))
                lo = grp * V_W + h * DV
                o_h = _dot(a.astype(BF16), vb[:, h * DV:(h + 1) * DV], "nn") + o_inter[:, h * DV:(h + 1) * DV]
                o_ref[:, lo:lo + DV] = o_h
                r = lax.rsqrt(jnp.mean(o_h * o_h, axis=-1, keepdims=True) + RMS_EPS)
                gte = gate[:, h * DV:(h + 1) * DV]
                y = o_h * r * gn[:, h * DV:(h + 1) * DV] * (gte * _sigmoid(gte))
                y_ref[:, lo:lo + DV] = y.astype(y_ref.dtype)
            upd = _dot(vb, f["kl"].astype(BF16), "tn")
            s_ref[grp] = s_prev * f["ebl"] + jnp.where(blockmask, upd, 0.0)
        xv = xres_ref[...] + _dot(y_ref[...], wout_ref[...], "nn")
        x_ref[...] = xv
        r = lax.rsqrt(jnp.mean(xv * xv, axis=-1, keepdims=True) + RMS_EPS)
        h_ref[...] = (xv * r * gnext_ref[...]).astype(h_ref.dtype)

    const = lambda shape: pl.BlockSpec(shape, lambda i: tuple(0 for _ in shape))
    rows = lambda w: pl.BlockSpec((SUPER, w), lambda i: (i, 0))
    return _call(
        "attn_fwd", main, (n_s,),
        [rows(PROJ_W), rows(128), rows(128),
         const((1, QK_W)), const((GL_W, QK_W)), const((1, QK_W)), const((1, V_W)), const((1, V_W)),
         rows(D), const((2 * V_W, D)), const((1, D))],
        [rows(2 * V_W), rows(2 * V_W), pl.BlockSpec((1, 2, V_W, QK_W), lambda i: (i, 0, 0, 0)), rows(D), rows(D)],
        [jax.ShapeDtypeStruct((T, 2 * V_W), F32), jax.ShapeDtypeStruct((T, 2 * V_W), BF16),
         jax.ShapeDtypeStruct((n_s, 2, V_W, QK_W), F32), jax.ShapeDtypeStruct((T, D), F32),
         jax.ShapeDtypeStruct((T, D), BF16)],
        (proj, cos, sin_signed, lg, wa2p, ba, gn_ret, gn_gla, x_res, w_out, g_next),
        scratch=[pltpu.VMEM((2, V_W, QK_W), F32)], comm=comm)


def _attn_bwd(proj, cos, sin_signed, lg, wa2p, ba, gn_ret, gn_gla, o, dx, w_out, states, comm=None, after=None):
    T = proj.shape[0]
    n_s = T // SUPER
    D = dx.shape[1]

    def main(ins, outs, scr):
        pr_ref, cos_ref, sin_ref, lg_ref, wa2_ref, ba_ref, gr_ref, gg_ref, o_ref, dx_ref, wout_ref, st_ref = ins
        dp_ref, dgr_ref, dgg_ref, dba_ref, dwa_ref = outs
        (ds_ref, dy_ref) = scr
        i = pl.program_id(0)
        dy_ref[...] = _dot(dx_ref[...], wout_ref[...], "nt")

        @pl.when(i == 0)
        def _():
            ds_ref[...] = jnp.zeros_like(ds_ref)
            dgr_ref[...] = jnp.zeros_like(dgr_ref)
            dgg_ref[...] = jnp.zeros_like(dgg_ref)
            dba_ref[...] = jnp.zeros_like(dba_ref)
            dwa_ref[...] = jnp.zeros_like(dwa_ref)

        pr = pr_ref
        cos = cos_ref[...]
        sin_signed = sin_ref[...]
        row, col, same = _attn_masks()
        m1 = col <= row
        m2 = jnp.logical_and(col > row, same)
        m1t = row <= col
        m2t = jnp.logical_and(row > col, same)
        lane = lax.broadcasted_iota(jnp.int32, (1, QK_W), 1)
        blockmask = _state_block_mask()
        for grp in range(2):
            q, k, v, gate, b_cum, glow, logit = _group_inputs(grp, pr, cos, sin_signed, lg_ref[...],
                                                              wa2_ref[...], ba_ref[...])
            f = _decay_factors(q, k, b_cum)
            gn = gr_ref[...] if grp == 0 else gg_ref[...]
            dgn_ref = dgr_ref if grp == 0 else dgg_ref
            do_parts, dgate_parts, dgn_parts = [], [], []
            for h in range(HEADS):
                lo = grp * V_W + h * DV
                o_h = o_ref[:, lo:lo + DV]
                r = lax.rsqrt(jnp.mean(o_h * o_h, axis=-1, keepdims=True) + RMS_EPS)
                n = o_h * r
                gte = gate[:, h * DV:(h + 1) * DV]
                sg = _sigmoid(gte)
                dy_h = dy_ref[:, lo:lo + DV]
                gn_h = gn[:, h * DV:(h + 1) * DV]
                dgate_parts.append(dy_h * n * gn_h * (sg * (1.0 + gte * (1.0 - sg))))
                dz = dy_h * (gte * sg)
                dgn_parts.append(jnp.sum(dz * n, axis=0, keepdims=True))
                dn = dz * gn_h
                do_parts.append(r * (dn - n * jnp.mean(dn * n, axis=-1, keepdims=True)))
            dgn_ref[...] += jnp.concatenate(dgn_parts, axis=1)
            dgate = jnp.concatenate(dgate_parts, axis=1)
            do = jnp.concatenate(do_parts, axis=1)
            dob = do.astype(BF16)
            vb = v.astype(BF16)
            s_prev = st_ref[0, grp]
            ds_new = ds_ref[grp]
            dsb = ds_new.astype(BF16)
            qpb, qmb = f["qp"].astype(BF16), f["qm"].astype(BF16)
            kpb, kmb = f["kp"].astype(BF16), f["km"].astype(BF16)
            dv_parts = []
            heads = [(lane // DK) == h for h in range(HEADS)]
            qp_hs = [jnp.where(hm, f["qp"], 0.0).astype(BF16) for hm in heads]
            qm_hs = [jnp.where(hm, f["qm"], 0.0).astype(BF16) for hm in heads]
            kp_hs = [jnp.where(hm, f["kp"], 0.0).astype(BF16) for hm in heads]
            km_hs = [jnp.where(hm, f["km"], 0.0).astype(BF16) for hm in heads]
            km_stack, kp_stack = jnp.concatenate(km_hs, axis=0), jnp.concatenate(kp_hs, axis=0)
            at1_all = _dot(km_stack, qpb, "nt")
            at2_all = _dot(kp_stack, qmb, "nt")
            da1s, da2s, da1ts, da2ts = [], [], [], []
            for h in range(HEADS):
                at = jnp.where(m1t, at1_all[h * SUPER:(h + 1) * SUPER],
                               jnp.where(m2t, at2_all[h * SUPER:(h + 1) * SUPER], 0.0))
                do_h = dob[:, h * DV:(h + 1) * DV]
                v_h = vb[:, h * DV:(h + 1) * DV]
                dv_parts.append(_dot(at.astype(BF16), do_h, "nn"))
                da = _dot(do_h, v_h, "nt")
                dat = _dot(v_h, do_h, "nt")
                da1s.append(jnp.where(m1, da, 0.0).astype(BF16))
                da2s.append(jnp.where(m2, da, 0.0).astype(BF16))
                da1ts.append(jnp.where(m1t, dat, 0.0).astype(BF16))
                da2ts.append(jnp.where(m2t, dat, 0.0).astype(BF16))
            dqp = _dot(jnp.concatenate(da1s, axis=1), km_stack, "nn")
            dqm = _dot(jnp.concatenate(da2s, axis=1), kp_stack, "nn")
            dkm = _dot(jnp.concatenate(da1ts, axis=1), jnp.concatenate(qp_hs, axis=0), "nn")
            dkp = _dot(jnp.concatenate(da2ts, axis=1), jnp.concatenate(qm_hs, axis=0), "nn")
            klb = f["kl"].astype(BF16)
            qsb = f["qs"].astype(BF16)
            dqs = _dot(dob, s_prev.astype(BF16), "nn")
            dkl = _dot(vb, dsb, "nn")
            dv = jnp.concatenate(dv_parts, axis=1) + _dot(klb, dsb, "nt")
            ds_ref[grp] = ds_new * f["ebl"] + jnp.where(blockmask, _dot(dob, qsb, "tn"), 0.0)
            dq = dqp * f["e1"] + dqm * f["e2"] + dqs * f["eb"]
            dk = dkm * f["e2"] + dkp * f["e1"] + dkl * f["el"]
            if grp == 0:
                dq = _rotary_transposed(dq, cos, sin_signed)
                dk = _rotary_transposed(dk * (DK ** -0.5), cos, sin_signed)
                dp_ref[:, C_RQ:C_RQ + QK_W] = dq.astype(dp_ref.dtype)
                dp_ref[:, C_RK:C_RK + QK_W] = dk.astype(dp_ref.dtype)
                dp_ref[:, C_RV:C_RV + V_W] = dv.astype(dp_ref.dtype)
                dp_ref[:, C_RG:C_RG + V_W] = dgate.astype(dp_ref.dtype)
            else:
                dkl_kl = dkl * klb.astype(F32)
                db = (dqp * qpb.astype(F32) - dkm * kmb.astype(F32) - dqm * qmb.astype(F32)
                      + dkp * kpb.astype(F32) + dqs * qsb.astype(F32) - dkl_kl)
                last = (jnp.sum(dkl_kl, axis=0, keepdims=True)
                        + f["ebl"] * jnp.sum(s_prev * ds_new, axis=0, keepdims=True))
                rowq = lax.broadcasted_iota(jnp.int32, (SUPER, QK_W), 0)
                db = db + jnp.where(rowq == SUPER - 1, last, 0.0)
                dla = _tri_sum(col >= row, db)
                dlogit = dla * (1.0 / GATE_NORM) * (1.0 - _sigmoid(logit))
                dlb = dlogit.astype(BF16)
                dglow = _dot(dlb, wa2_ref[...].astype(BF16), "nt")
                dwa_ref[...] += _dot(glow.astype(BF16), dlb, "tn")
                dba_ref[...] += jnp.sum(dlogit, axis=0, keepdims=True)
                dp_ref[:, C_GQ:C_GQ + QK_W] = (dq * (DK ** -0.5)).astype(dp_ref.dtype)
                dp_ref[:, C_GK:C_GK + QK_W] = dk.astype(dp_ref.dtype)
                dp_ref[:, C_GV:C_GV + V_W] = dv.astype(dp_ref.dtype)
                dp_ref[:, C_GG:C_GG + V_W] = dgate.astype(dp_ref.dtype)
                dp_ref[:, C_GL:C_GL + GL_W] = dglow.astype(dp_ref.dtype)

    rev = lambda i: n_s - 1 - i
    const = lambda shape: pl.BlockSpec(shape, lambda i: tuple(0 for _ in shape))
    return _call(
        "attn_bwd", main, (n_s,),
        [pl.BlockSpec((SUPER, PROJ_W), lambda i: (rev(i), 0)),
         pl.BlockSpec((SUPER, 128), lambda i: (rev(i), 0)), pl.BlockSpec((SUPER, 128), lambda i: (rev(i), 0)),
         const((1, QK_W)), const((GL_W, QK_W)), const((1, QK_W)), const((1, V_W)), const((1, V_W)),
         pl.BlockSpec((SUPER, 2 * V_W), lambda i: (rev(i), 0)),
         pl.BlockSpec((SUPER, D), lambda i: (rev(i), 0)), const((2 * V_W, D)),
         pl.BlockSpec((1, 2, V_W, QK_W), lambda i: (rev(i), 0, 0, 0))],
        [pl.BlockSpec((SUPER, PROJ_W), lambda i: (rev(i), 0)),
         const((1, V_W)), const((1, V_W)), const((1, QK_W)), const((GL_W, QK_W))],
        [jax.ShapeDtypeStruct((T, PROJ_W), BF16),
         jax.ShapeDtypeStruct((1, V_W), F32), jax.ShapeDtypeStruct((1, V_W), F32),
         jax.ShapeDtypeStruct((1, QK_W), F32), jax.ShapeDtypeStruct((GL_W, QK_W), F32)],
        (proj, cos, sin_signed, lg, wa2p, ba, gn_ret, gn_gla, o, dx, w_out, states),
        scratch=[pltpu.VMEM((2, V_W, QK_W), F32), pltpu.VMEM((SUPER, 2 * V_W), F32)], comm=comm, after=after)


def _rotary_tables(T):
    half = DK // 2
    inv = ROPE_BASE ** (-jnp.arange(half, dtype=F32) * 2.0 / DK)
    ang = jnp.arange(T, dtype=F32)[:, None] * inv[None, :]
    cos, sin = jnp.cos(ang), jnp.sin(ang)
    cos_head = jnp.concatenate([cos, cos], axis=1)
    sin_head = jnp.concatenate([-sin, sin], axis=1)
    return jnp.tile(cos_head, (1, 128 // DK)), jnp.tile(sin_head, (1, 128 // DK))


def _sum_devices(name, gathered, m_per):
    def body(g_ref, o_ref):
        acc = g_ref[0:m_per, :]
        for k in range(1, N_DEV):
            acc = acc + g_ref[k * m_per:(k + 1) * m_per, :]
        o_ref[...] = acc

    return pl.pallas_call(body, name=name, out_shape=jax.ShapeDtypeStruct((m_per, 128), F32))(gathered)


def _owner_sums(name, items, owner, comm=None):
    counts = [1 + len(landed) for _, landed in items]

    def main(ins, outs, scr):
        at = 0
        for o_ref, n in zip(outs, counts):
            acc = ins[at][...].astype(F32)
            for l_ref in ins[at + 1:at + n]:
                for j in range(l_ref.shape[0]):
                    acc = acc + l_ref[j].astype(F32)
            o_ref[...] = acc
            at += n

    once = pl.Buffered(1)
    in_specs, out_specs, out_shape, args = [], [], [], []
    for grad, landed in items:
        R, C = grad.shape[-2:]
        in_specs.append(pl.BlockSpec((None, None, R, C), lambda i, s: (s[0], s[1], 0, 0), pipeline_mode=once))
        in_specs += [pl.BlockSpec(tuple(l.shape), lambda i, s: (0, 0, 0), pipeline_mode=once) for l in landed]
        out_specs.append(pl.BlockSpec((R, C), lambda i, s: (0, 0)))
        out_shape.append(jax.ShapeDtypeStruct((R, C), F32))
        args += [grad, *landed]
    return _call(name, main, (1,), in_specs, out_specs, out_shape, args, comm=comm, prefetch=owner)


def _adamw_group(name, items, n_blocks, comm=None):
    n = len(items)

    def main(ins, outs, scr):
        for p in range(n):
            g_ref, w_ref, m_ref, v_ref = ins[4 * p:4 * p + 4]
            d_ref, nm_ref, nv_ref = outs[3 * p:3 * p + 3]
            gv = g_ref[...]
            nm = ADAM_B1 * m_ref[...] + (1.0 - ADAM_B1) * gv
            nv = ADAM_B2 * v_ref[...] + (1.0 - ADAM_B2) * (gv * gv)
            m_hat = nm / (1.0 - ADAM_B1 ** ADAM_STEP)
            v_hat = nv / (1.0 - ADAM_B2 ** ADAM_STEP)
            d_ref[...] = -ADAM_LR * (m_hat / (jnp.sqrt(v_hat) + ADAM_EPS) + ADAM_WD * w_ref[...])
            nm_ref[...] = nm
            nv_ref[...] = nv

    in_specs, out_specs, out_shape, args = [], [], [], []
    for item in items:
        R, C = item[1].shape
        assert R % n_blocks == 0
        spec = pl.BlockSpec((R // n_blocks, C), lambda i: (i, 0))
        in_specs += [spec] * 4
        out_specs += [spec] * 3
        out_shape += [jax.ShapeDtypeStruct((R, C), F32)] * 3
        args += list(item)
    outs, extra = _call(name, main, (n_blocks,), in_specs, out_specs, out_shape, args, comm=comm)
    return [tuple(outs[3 * p:3 * p + 3]) for p in range(n)], extra


SMALL_ORDER = ("ffn1", "mix", "ffn2", "final", "ret", "gla", "b_a")


def kernel(x, ffn1_norm_g, ffn1_w_gate, ffn1_w_up, ffn1_w_down, mix_norm_g, w_in, ret_norm_g, gla_w_a2, gla_b_a, gla_norm_g, w_out, ffn2_norm_g, ffn2_w_gate, ffn2_w_up, ffn2_w_down, final_norm_g, loss_target, m_ffn1_norm_g, m_ffn1_w_gate, m_ffn1_w_up, m_ffn1_w_down, m_mix_norm_g, m_w_in, m_ret_norm_g, m_gla_w_a2, m_gla_b_a, m_gla_norm_g, m_w_out, m_ffn2_norm_g, m_ffn2_w_gate, m_ffn2_w_up, m_ffn2_w_down, m_final_norm_g, v_ffn1_norm_g, v_ffn1_w_gate, v_ffn1_w_up, v_ffn1_w_down, v_mix_norm_g, v_w_in, v_ret_norm_g, v_gla_w_a2, v_gla_b_a, v_gla_norm_g, v_w_out, v_ffn2_norm_g, v_ffn2_w_gate, v_ffn2_w_up, v_ffn2_w_down, v_final_norm_g):
    xi, yi, ci = _coords()
    dev = 4 * xi + 2 * yi + ci
    owner = jnp.stack([2 * xi + yi, ci]).astype(jnp.int32)

    x0, target = x[0], loss_target[0]
    T, D = x0.shape
    fb = ffn1_w_gate.shape[2]
    ib = w_in.shape[2]
    ab = gla_w_a2.shape[2]
    F = N_DEV * fb
    cos, sin_signed = _rotary_tables(T)
    lg = jnp.repeat(jnp.log(1.0 - 2.0 ** (-5.0 - jnp.arange(HEADS, dtype=F32))), DK)[None, :]
    g_final = final_norm_g.reshape(1, D)

    g1_loc = ffn1_w_gate[0].T[None].astype(BF16)
    u1_loc = ffn1_w_up[0].T[None].astype(BF16)
    d1_loc = ffn1_w_down.astype(BF16)
    g2_loc = ffn2_w_gate[0].T[None].astype(BF16)
    u2_loc = ffn2_w_up[0].T[None].astype(BF16)
    d2_loc = ffn2_w_down.astype(BF16)
    in_loc = w_in[0].T.astype(BF16)
    out_loc = w_out[0].astype(BF16)

    h1, (g1,) = _rms_fwd("ffn1_rms", x0, ffn1_norm_g, comm=_AllGather([g1_loc], ["stack"]))
    g1 = g1.reshape(1, F, D)
    (dsl1, sl1), (u1,) = _mm_nstream("ffn1_gate", h1, [g1], [0], "nt", [], [BF16, BF16], _gate_parts_epilogue, cn=256,
                                     comm=_AllGather([u1_loc], ["stack"]))
    u1 = u1.reshape(1, F, D)
    (dsu1, act1), (d1,) = _mm_nstream("ffn1_up", h1, [u1], [0], "nt", [dsl1, sl1], [BF16, BF16],
                                      _up_act_epilogue, cn=256, comm=_AllGather([d1_loc], ["stack"]))
    d1 = d1.reshape(1, F, D)
    f32_tile, bf16_tile, f32_vec = (F32, "tile"), (BF16, "tile"), (F32, "vec")
    (x1, h2), (in_all, a_all) = _mm_mstream(
        "ffn1_down", [act1], [d1], [0], "nn", [(x0, "tile"), (mix_norm_g, "vec")], [f32_tile, bf16_tile],
        _residual_rms_epilogue(0.5), comm=_AllGather([in_loc, gla_w_a2[0]], ["plain", "plain"]))
    w_in_t = jnp.pad(in_all.reshape(1, N_DEV * ib, D), ((0, 0), (0, PROJ_W - N_DEV * ib), (0, 0)))
    wa2 = jnp.transpose(a_all, (1, 0, 2)).reshape(GATE_RANK, N_DEV * ab)
    wa2p = jnp.pad(wa2, ((0, GL_W - GATE_RANK), (0, 0)))

    (proj,), (g2, out_all) = _mm_nstream("mix_proj", h2, [w_in_t], [0], "nt", [], [BF16], _identity_epilogue, cn=640,
                                         comm=_AllGather([g2_loc, out_loc], ["stack", "plain"]))
    w_out_full = out_all.reshape(D, D)
    (o, ymix, states, x2, h3), (u2,) = _attn_fwd(proj, cos, sin_signed, lg, wa2p, gla_b_a, ret_norm_g, gla_norm_g,
                                                 x1, w_out_full, ffn2_norm_g, comm=_AllGather([u2_loc], ["stack"]))
    g2, u2 = g2.reshape(1, F, D), u2.reshape(1, F, D)

    (dsu2, sl2, act2), (d2,) = _mm_nstream(
        "ffn2_up", h3, [g2, u2], [0, 0], "nt", [], [BF16, BF16, BF16], _silu_mul_epilogue, cn=256,
        comm=_AllGather([d2_loc], ["stack"]))
    d2 = d2.reshape(1, F, D)
    (dx3, dy3b, d_final, loss), _ = _mm_mstream(
        "ffn2_down", [act2], [d2], [0], "nn", [(x2, "tile"), (g_final, "vec"), (target, "tile")],
        [f32_tile, bf16_tile, f32_vec, f32_vec], _final_loss_epilogue(0.5, 0.5))

    sent = {}

    def send(**grads):
        started = _send_to_owners("send_" + "_".join(grads), list(grads.values()))
        sent.update(zip(grads, started))
        return started[0][4]

    dwd2, _ = _mm_tn("ffn2b_dwd", act2, dy3b, F // 2, D, BF16)
    tok = send(wd2=dwd2.reshape(4, 2, fb, D))
    (dgate2, dup2), _ = _mm_nstream("ffn2b_dact", dy3b, [d2], [0], "nt", [dsu2, sl2], [BF16, BF16],
                                    _dact_epilogue, cn=256, after=tok)
    dwg2, _ = _mm_tn("ffn2b_dwg", dgate2, h3, F // 2, D, BF16)
    dwu2, _ = _mm_tn("ffn2b_dwu", dup2, h3, F // 2, D, BF16)
    tok = send(wg2=dwg2.reshape(4, 2, fb, D), wu2=dwu2.reshape(4, 2, fb, D))
    rms_outs = [f32_tile, bf16_tile, f32_vec]
    (dx2, dx2b, d_g2), _ = _mm_mstream(
        "ffn2b_dh", [dgate2, dup2], [g2, u2], [0, 0], "nn", [(x2, "tile"), (ffn2_norm_g, "vec"), (dx3, "tile")],
        rms_outs, _rms_bwd_epilogue(1.0), after=tok)

    dwout, _ = _mm_tn("mixb_dwout", ymix, dx2b, D, D, BF16)
    (dproj, d_ret, d_gla, d_ba, d_wa2p), _ = _attn_bwd(
        proj, cos, sin_signed, lg, wa2p, gla_b_a, ret_norm_g, gla_norm_g, o, dx2b, w_out_full, states)
    dwin_t, _ = _mm_tn("mixb_dwin", dproj, h2, 640, D, BF16, tk=2048)
    tok = send(wout=dwout.reshape(4, 2, D // N_DEV, D), win=dwin_t[:N_DEV * ib].reshape(4, 2, ib, D))
    (dx1, dy1b, d_gmix), _ = _mm_mstream(
        "mixb_dh", [dproj], [w_in_t], [0], "nn", [(x1, "tile"), (mix_norm_g, "vec"), (dx2, "tile")],
        rms_outs, _rms_bwd_epilogue(0.5), after=tok)

    dwd1, _ = _mm_tn("ffn1b_dwd", act1, dy1b, F // 2, D, BF16)
    tok = send(wd1=dwd1.reshape(4, 2, fb, D))
    (dgate1, dup1), _ = _mm_nstream("ffn1b_dact", dy1b, [d1], [0], "nt", [dsu1, sl1], [BF16, BF16],
                                    _dact_epilogue, cn=256, after=tok)
    dwg1, _ = _mm_tn("ffn1b_dwg", dgate1, h1, F // 2, D, BF16)
    tok = send(wg1=dwg1.reshape(4, 2, fb, D))
    dwu1, _ = _mm_tn("ffn1b_dwu", dup1, h1, F // 2, D, BF16, after=tok)
    tok = send(wu1=dwu1.reshape(4, 2, fb, D))
    (dx0, _, d_g1), _ = _mm_mstream(
        "ffn1b_dh", [dgate1, dup1], [g1, u1], [0, 0], "nn", [(x0, "tile"), (ffn1_norm_g, "vec"), (dx1, "tile")],
        rms_outs, _rms_bwd_epilogue(1.0), after=tok)

    small = dict(ffn1=d_g1, mix=d_gmix, ffn2=d_g2, final=d_final, ret=d_ret, gla=d_gla, b_a=d_ba)
    flat = jnp.concatenate([small[k].reshape(-1) for k in SMALL_ORDER]
                           + [d_wa2p[:GATE_RANK].reshape(-1), loss[0, :128]])
    rows = -(-flat.shape[0] // 128)
    rows = -(-rows // 8) * 8
    packed = jnp.pad(flat, (0, rows * 128 - flat.shape[0])).reshape(rows, 128)

    transposed = ("ffn1_w_gate", "ffn1_w_up", "ffn2_w_gate", "ffn2_w_up", "w_in")

    def to_2d(nm, a):
        if nm in transposed:
            return a[0].T
        return a.reshape((1, a.shape[0]) if a.ndim == 1 else a.shape[-2:])

    def from_2d(nm, a):
        return a.T[None] if nm in transposed else a.reshape(params[nm][0].shape)

    def arrived(nm, after):
        grad, landed = _await_owners("await_" + nm, sent[nm], after)
        return grad, [landed]

    sums_a, (gathered,) = _owner_sums(
        "sum_a", [arrived(nm, dx0) for nm in ("wg2", "wu2", "wd2", "win", "wout")], owner,
        comm=_AllGather([packed], ["plain"]))
    params = dict(
        ffn2_w_gate=(ffn2_w_gate, m_ffn2_w_gate, v_ffn2_w_gate), ffn2_w_up=(ffn2_w_up, m_ffn2_w_up, v_ffn2_w_up),
        ffn2_w_down=(ffn2_w_down, m_ffn2_w_down, v_ffn2_w_down), w_in=(w_in, m_w_in, v_w_in),
        w_out=(w_out, m_w_out, v_w_out), ffn1_w_gate=(ffn1_w_gate, m_ffn1_w_gate, v_ffn1_w_gate),
        ffn1_w_up=(ffn1_w_up, m_ffn1_w_up, v_ffn1_w_up), ffn1_w_down=(ffn1_w_down, m_ffn1_w_down, v_ffn1_w_down),
        ffn1_norm_g=(ffn1_norm_g, m_ffn1_norm_g, v_ffn1_norm_g), mix_norm_g=(mix_norm_g, m_mix_norm_g, v_mix_norm_g),
        ret_norm_g=(ret_norm_g, m_ret_norm_g, v_ret_norm_g), gla_w_a2=(gla_w_a2, m_gla_w_a2, v_gla_w_a2),
        gla_b_a=(gla_b_a, m_gla_b_a, v_gla_b_a), gla_norm_g=(gla_norm_g, m_gla_norm_g, v_gla_norm_g),
        ffn2_norm_g=(ffn2_norm_g, m_ffn2_norm_g, v_ffn2_norm_g), final_norm_g=(final_norm_g, m_final_norm_g, v_final_norm_g))
    grads, updates = {}, {}

    def run_adam(name, names, grad_2d, n_blocks):
        items = [(grad_2d[nm],) + tuple(to_2d(nm, a) for a in params[nm]) for nm in names]
        res, _ = _adamw_group(name, items, n_blocks)
        for nm, r in zip(names, res):
            grads[nm] = from_2d(nm, grad_2d[nm])
            updates[nm] = tuple(from_2d(nm, a) for a in r)
        return res

    grads_a = {"ffn2_w_gate": sums_a[0], "ffn2_w_up": sums_a[1], "ffn2_w_down": sums_a[2], "w_out": sums_a[4]}
    run_adam("adamw_w_in", ["w_in"], {"w_in": sums_a[3]}, 1)
    done_a = run_adam("adamw_a", list(grads_a), grads_a, 4)[0][0]
    sums_b, _ = _owner_sums("sum_b", [arrived(nm, done_a) for nm in ("wg1", "wu1", "wd1")], owner)
    grads_b = {"ffn1_w_gate": sums_b[0], "ffn1_w_up": sums_b[1], "ffn1_w_down": sums_b[2]}
    run_adam("adamw_b", list(grads_b), grads_b, 4)

    total = _sum_devices("sum_small", gathered.reshape(N_DEV * rows, 128), rows).reshape(-1)
    sizes = [small[k].size for k in SMALL_ORDER] + [GATE_RANK * QK_W, 128]
    offs = [0]
    for s in sizes:
        offs.append(offs[-1] + s)
    pieces = [total[offs[i]:offs[i + 1]] for i in range(len(sizes))]
    g_small = {k: pieces[i].reshape(small[k].shape) for i, k in enumerate(SMALL_ORDER)}
    g_wa2_full = pieces[len(SMALL_ORDER)].reshape(GATE_RANK, QK_W)
    g_wa2 = lax.dynamic_slice(g_wa2_full, (0, dev * ab), (GATE_RANK, ab))
    loss_total = pieces[len(SMALL_ORDER) + 1][0]

    small_grads = {"ffn1_norm_g": g_small["ffn1"], "mix_norm_g": g_small["mix"], "ret_norm_g": g_small["ret"],
                   "gla_w_a2": g_wa2, "gla_b_a": g_small["b_a"], "gla_norm_g": g_small["gla"],
                   "ffn2_norm_g": g_small["ffn2"], "final_norm_g": g_small["final"]}
    run_adam("adamw_small", list(small_grads), small_grads, 1)

    order = ("ffn1_norm_g", "ffn1_w_gate", "ffn1_w_up", "ffn1_w_down", "mix_norm_g", "w_in", "ret_norm_g", "gla_w_a2",
             "gla_b_a", "gla_norm_g", "w_out", "ffn2_norm_g", "ffn2_w_gate", "ffn2_w_up", "ffn2_w_down", "final_norm_g")
    return (loss_total, dx0[None], *[grads[nm] for nm in order], *[updates[nm][0] for nm in order],
            *[updates[nm][1] for nm in order], *[updates[nm][2] for nm in order])
```

```python
import functools
import math

import jax
import jax.numpy as jnp
from jax import lax
from jax.experimental import pallas as pl
from jax.experimental.pallas import tpu as pltpu

F32 = jnp.float32
BF16 = jnp.bfloat16
MESH = pl.DeviceIdType.MESH
HBM = pl.BlockSpec(memory_space=pltpu.HBM)

N_DEV = 8
RMS_EPS = 1e-6
ROPE_BASE = 10000.0
HEADS = 4
DK = 64
DV = 128
QK_W = HEADS * DK
V_W = HEADS * DV
GATE_RANK = 16
GATE_NORM = 16.0
CHUNK = 64
SUPER = 256
PROJ_W = 3200
C_RQ, C_RK, C_RV, C_RG, C_GQ, C_GK, C_GV, C_GG, C_GL = 0, 256, 512, 1024, 1536, 1792, 2048, 2560, 3072
GL_W = PROJ_W - C_GL
ADAM_LR, ADAM_B1, ADAM_B2, ADAM_EPS, ADAM_WD, ADAM_STEP = 0.001, 0.9, 0.999, 1e-08, 0.01, 10
VMEM_LIMIT_V7X = 52 * 1024 * 1024


def _cparams(**kw):
    return pltpu.CompilerParams(vmem_limit_bytes=VMEM_LIMIT_V7X, **kw)


def _dot(a, b, form, precision=None):
    dims = {"nn": (((1,), (0,)), ((), ())), "nt": (((1,), (1,)), ((), ())), "tn": (((0,), (0,)), ((), ()))}[form]
    return lax.dot_general(a, b, dims, preferred_element_type=F32, precision=precision)


def _sigmoid(x):
    return 1.0 / (1.0 + jnp.exp(-x))


def _coords():
    return lax.axis_index("x"), lax.axis_index("y"), lax.axis_index("c")


class _NoComm:
    inputs, out_shapes, scratch = (), (), ()


class _AllGather:
    def __init__(self, arrays, kinds):
        self.inputs = tuple(arrays)
        self.kinds = tuple(kinds)
        n = len(arrays)
        self.out_shapes = tuple(
            jax.ShapeDtypeStruct((a.shape[0], N_DEV) + a.shape[1:] if k == "stack" else (N_DEV,) + a.shape, a.dtype)
            for a, k in zip(arrays, kinds))
        self.scratch = (pltpu.SemaphoreType.DMA((n, 7)), pltpu.SemaphoreType.DMA((n, 7)),
                        pltpu.SemaphoreType.DMA((n,)))

    def _ctx(self, srcs, outs, sems):
        send_sems, recv_sems, local_sems = sems
        x, y, c = _coords()
        me, sibling = (x, y, c), (x, y, 1 - c)
        chips = [(1 - x, y), (x, 1 - y), (1 - x, 1 - y)]

        def blk(m, dev):
            k = 4 * dev[0] + 2 * dev[1] + dev[2]
            return outs[m].at[:, k] if self.kinds[m] == "stack" else outs[m].at[k]

        def copy(m, s, block, to, src=None):
            return pltpu.make_async_remote_copy(
                src_ref=blk(m, block) if src is None else src, dst_ref=blk(m, block),
                send_sem=send_sems.at[m, s], recv_sem=recv_sems.at[m, s], device_id=to, device_id_type=MESH)

        def mine(m):
            return pltpu.make_async_copy(srcs[m], blk(m, me), local_sems.at[m])

        def first(m):
            return [copy(m, 0, me, sibling, src=srcs[m])] + [
                copy(m, 1 + j, me, (*chip, c), src=srcs[m]) for j, chip in enumerate(chips)]

        return me, sibling, chips, c, copy, mine, first

    def start(self, srcs, outs, sems):
        me, sibling, chips, c, copy, mine, first = self._ctx(srcs, outs, sems)
        for m in range(len(srcs)):
            mine(m).start()
            for cp in first(m):
                cp.start()

    def mid(self, srcs, outs, sems):
        me, sibling, chips, c, copy, mine, first = self._ctx(srcs, outs, sems)
        for j, chip in enumerate(chips):
            for m in range(len(srcs)):
                copy(m, 1 + j, (*chip, c), me).wait_recv()
                copy(m, 4 + j, (*chip, c), sibling).start()

    def finish(self, srcs, outs, sems):
        me, sibling, chips, c, copy, mine, first = self._ctx(srcs, outs, sems)
        for m in range(len(srcs)):
            copy(m, 0, sibling, me).wait_recv()
            for j, chip in enumerate(chips):
                copy(m, 4 + j, (*chip, 1 - c), me).wait_recv()
            for cp in first(m):
                cp.wait_send()
            for j, chip in enumerate(chips):
                copy(m, 4 + j, (*chip, c), sibling).wait_send()
            mine(m).wait()


RELATIONS = ((0, 0, 1), (1, 0, 0), (0, 1, 0), (1, 1, 0), (1, 0, 1), (0, 1, 1), (1, 1, 1))
SEM = pl.BlockSpec(memory_space=pltpu.SEMAPHORE)
SPLIT_PARAMS = dict(has_side_effects=pltpu.SideEffectType.DATAFLOW_SIDE_EFFECTING)


def _owner_copies(grad_ref, land_ref, send_sems, recv_sems):
    x, y, c = _coords()
    copies = []
    for s, (fx, fy, fc) in enumerate(RELATIONS):
        px = 1 - x if fx else x
        py = 1 - y if fy else y
        pc = 1 - c if fc else c
        copies.append(pltpu.make_async_remote_copy(
            src_ref=grad_ref.at[2 * px + py, pc], dst_ref=land_ref.at[s], send_sem=send_sems.at[s],
            recv_sem=recv_sems.at[s], device_id=(px, py, pc), device_id_type=MESH))
    return copies


def _send_to_owners(name, grads):
    n, k = len(RELATIONS), len(grads)
    land_shapes = [(n,) + g.shape[2:] for g in grads]

    def body(*refs):
        ins, outs = refs[:2 * k], refs[2 * k:]
        for j in range(k):
            for cp in _owner_copies(ins[2 * j], ins[2 * j + 1], outs[4 * j], outs[4 * j + 1]):
                cp.start()
        outs[-1][...] = jnp.zeros_like(outs[-1])

    out_shape, out_specs, args, aliases = [], [], [], {}
    for j, (g, land_shape) in enumerate(zip(grads, land_shapes)):
        out_shape += [pltpu.SemaphoreType.DMA((n,)), pltpu.SemaphoreType.DMA((n,)), pltpu.HBM(g.shape, g.dtype),
                      pltpu.HBM(land_shape, g.dtype)]
        out_specs += [SEM, SEM, HBM, HBM]
        args += [pltpu.with_memory_space_constraint(g, pltpu.HBM),
                 pltpu.with_memory_space_constraint(lax.empty(land_shape, g.dtype), pltpu.HBM)]
        aliases.update({2 * j: 4 * j + 2, 2 * j + 1: 4 * j + 3})
    res = pl.pallas_call(
        body, name=name, out_shape=out_shape + [jax.ShapeDtypeStruct((8, 128), F32)],
        in_specs=[HBM] * (2 * k), out_specs=out_specs + [pl.BlockSpec(memory_space=pltpu.VMEM)],
        input_output_aliases=aliases, compiler_params=pltpu.CompilerParams(**SPLIT_PARAMS),
    )(*args)
    return [tuple(res[4 * j:4 * j + 4]) + (res[-1],) for j in range(k)]


def _await_owners(name, started, after):
    send_sems, recv_sems, g_thru, land_thru, _ = started

    def body(g_ref, land_ref, send_sems, recv_sems, after_ref, g_out, land_out):
        for cp in _owner_copies(g_ref, land_ref, send_sems, recv_sems):
            cp.wait_send()
            cp.wait_recv()

    return pl.pallas_call(
        body, name=name, out_shape=(pltpu.HBM(g_thru.shape, g_thru.dtype), pltpu.HBM(land_thru.shape, land_thru.dtype)),
        in_specs=(HBM, HBM, SEM, SEM, pl.BlockSpec(memory_space=pl.ANY)), out_specs=(HBM, HBM),
        input_output_aliases={0: 0, 1: 1}, compiler_params=pltpu.CompilerParams(**SPLIT_PARAMS),
    )(g_thru, land_thru, send_sems, recv_sems, after)


def _call(name, main, grid, in_specs, out_specs, out_shape, args, scratch=(), comm=None, prefetch=None, after=None):
    comm = comm or _NoComm()
    n_main = len(in_specs)
    if after is not None:
        in_specs = list(in_specs) + [pl.BlockSpec(after.shape, lambda *_: (0,) * after.ndim)]
        args = tuple(args) + (after,)
    counts = [len(in_specs), len(comm.inputs), len(out_shape), len(comm.out_shapes), len(scratch), len(comm.scratch)]
    n_steps = math.prod(grid)
    hosted = bool(comm.inputs)

    def body(*refs):
        if prefetch is not None:
            refs = refs[1:]
        parts, at = [], 0
        for n in counts:
            parts.append(refs[at:at + n])
            at += n
        ins, c_in, outs, c_out, scr, c_scr = parts
        ins = ins[:n_main]
        step = pl.program_id(0)
        for d in range(1, len(grid)):
            step = step * grid[d] + pl.program_id(d)
        if hosted:
            @pl.when(step == 0)
            def _():
                comm.start(c_in, c_out, c_scr)
        main(ins, outs, scr)
        if hosted:
            @pl.when(step == max(n_steps - 2, 0))
            def _():
                comm.mid(c_in, c_out, c_scr)

            @pl.when(step == n_steps - 1)
            def _():
                comm.finish(c_in, c_out, c_scr)

    all_in = list(in_specs) + [HBM] * counts[1]
    all_out = list(out_specs) + [HBM] * counts[3]
    all_scratch = list(scratch) + list(comm.scratch)
    shapes = list(out_shape) + list(comm.out_shapes)
    if prefetch is None:
        res = pl.pallas_call(body, name=name, grid=grid, in_specs=all_in, out_specs=all_out, out_shape=shapes,
                             scratch_shapes=all_scratch, compiler_params=_cparams())(*args, *comm.inputs)
    else:
        res = pl.pallas_call(
            body, name=name, out_shape=shapes,
            grid_spec=pltpu.PrefetchScalarGridSpec(num_scalar_prefetch=1, grid=grid, in_specs=all_in,
                                                   out_specs=all_out, scratch_shapes=all_scratch),
            compiler_params=_cparams())(prefetch, *args, *comm.inputs)
    return res[:counts[2]], res[counts[2]:]


def _rms_fwd(name, x, g, comm=None):
    T, D = x.shape
    tm = min(T, 512)

    def main(ins, outs, scr):
        x_ref, g_ref = ins
        xv = x_ref[...]
        r = lax.rsqrt(jnp.mean(xv * xv, axis=-1, keepdims=True) + RMS_EPS)
        outs[0][...] = (xv * r * g_ref[...]).astype(outs[0].dtype)

    tile = pl.BlockSpec((tm, D), lambda i: (i, 0))
    (h,), extra = _call(name, main, (T // tm,), [tile, pl.BlockSpec((1, D), lambda i: (0, 0))], [tile],
                        [jax.ShapeDtypeStruct((T, D), BF16)], (x, g), comm=comm)
    return h, extra


def _final_loss_epilogue(scale, out_scale):
    def ep(acc, ex, outs):
        res_ref, g_ref, t_ref = ex
        dx_ref, dxb_ref, dg_ref, loss_ref = outs
        n = acc.shape[-1]
        xv = res_ref[...] + scale * acc
        r = lax.rsqrt(jnp.mean(xv * xv, axis=-1, keepdims=True) + RMS_EPS)
        xhat = xv * r
        err = xhat * g_ref[...] - t_ref[...]

        @pl.when(pl.program_id(0) == 0)
        def _():
            dg_ref[...] = jnp.zeros_like(dg_ref)
            loss_ref[...] = jnp.zeros_like(loss_ref)

        loss_ref[...] += jnp.broadcast_to(jnp.sum(err * err) * (0.5 / n), loss_ref.shape)
        dy = err * (1.0 / n)
        dg_ref[...] += jnp.sum(dy * xhat, axis=0, keepdims=True)
        dxhat = dy * g_ref[...]
        dx = r * (dxhat - xhat * jnp.mean(dxhat * xhat, axis=-1, keepdims=True))
        dx_ref[...] = dx
        dxb_ref[...] = (out_scale * dx).astype(dxb_ref.dtype)
    return ep


def _mm_nstream(name, a, ws, w_sel, w_form, comps, out_dtypes, epilogue, cn, rows=1024, comm=None, after=None):
    T, K = a.shape
    N = ws[0].shape[1]
    rows = min(rows, T)
    assert N % cn == 0 and T % rows == 0
    n_w, n_c = len(ws), len(comps)

    def main(ins, outs, scr):
        a_ref = ins[0]
        w_refs = ins[1:1 + n_w]
        c_refs = ins[1 + n_w:1 + n_w + n_c]

        for r in range(T // rows):
            sl = slice(r * rows, (r + 1) * rows)
            a_blk = a_ref[sl, :]
            dots = [_dot(a_blk, w_ref[...], w_form) for w_ref in w_refs]
            res = epilogue(dots, [c_ref[sl, :] for c_ref in c_refs])
            for o_ref, o in zip(outs, res):
                o_ref[sl, :] = o.astype(o_ref.dtype)

    if w_form == "nt":
        w_specs = [pl.BlockSpec((None, cn, K), functools.partial(lambda j, s: (s, j, 0), s=s)) for s in w_sel]
    else:
        w_specs = [pl.BlockSpec((K, cn), lambda j: (0, j)) for _ in ws]
    chunk = pl.BlockSpec((T, cn), lambda j: (0, j))
    return _call(name, main, (N // cn,), [pl.BlockSpec((T, K), lambda j: (0, 0))] + w_specs + [chunk] * n_c,
                 [chunk] * len(out_dtypes), [jax.ShapeDtypeStruct((T, N), dt) for dt in out_dtypes],
                 (a, *ws, *comps), comm=comm, after=after)


def _mm_mstream(name, as_, ws, w_sel, w_form, extras, outs_desc, epilogue, tm=512, comm=None, after=None):
    T = as_[0].shape[0]
    tm = min(tm, T)
    n_a = len(as_)
    w_shapes = [w.shape[-2:] for w in ws]
    N = w_shapes[0][1] if w_form == "nn" else w_shapes[0][0]

    def main(ins, outs, scr):
        a_refs = ins[:n_a]
        w_refs = ins[n_a:2 * n_a]
        acc = None
        for a_ref, w_ref in zip(a_refs, w_refs):
            d = _dot(a_ref[...], w_ref[...], w_form)
            acc = d if acc is None else acc + d
        epilogue(acc, ins[2 * n_a:], outs)

    kind_spec = {"tile": pl.BlockSpec((tm, N), lambda i: (i, 0)), "vec": pl.BlockSpec((1, N), lambda i: (0, 0))}
    kind_shape = {"tile": (T, N), "vec": (1, N)}
    a_specs = [pl.BlockSpec((tm, a.shape[1]), lambda i: (i, 0)) for a in as_]
    w_specs = []
    for w, s in zip(ws, w_sel):
        if w.ndim == 3:
            w_specs.append(pl.BlockSpec((None,) + tuple(w.shape[1:]), functools.partial(lambda i, s: (s, 0, 0), s=s),
                                        pipeline_mode=pl.Buffered(1)))
        else:
            w_specs.append(pl.BlockSpec(tuple(w.shape), lambda i: (0, 0), pipeline_mode=pl.Buffered(1)))
    args = list(as_) + list(ws) + [e for e, _ in extras]
    return _call(name, main, (T // tm,), a_specs + w_specs + [kind_spec[k] for _, k in extras],
                 [kind_spec[k] for _, k in outs_desc],
                 [jax.ShapeDtypeStruct(kind_shape[k], dt) for dt, k in outs_desc], args, comm=comm, after=after)


def _residual_rms_epilogue(scale):
    def ep(acc, ex, outs):
        xv = ex[0][...] + scale * acc
        outs[0][...] = xv
        r = lax.rsqrt(jnp.mean(xv * xv, axis=-1, keepdims=True) + RMS_EPS)
        outs[1][...] = (xv * r * ex[1][...]).astype(outs[1].dtype)
    return ep


def _rms_bwd_epilogue(out_scale):
    def ep(acc, ex, outs):
        x_ref, g_ref, dres_ref = ex
        dx_ref, dxb_ref, dg_ref = outs
        xv = x_ref[...]
        r = lax.rsqrt(jnp.mean(xv * xv, axis=-1, keepdims=True) + RMS_EPS)
        xhat = xv * r

        @pl.when(pl.program_id(0) == 0)
        def _():
            dg_ref[...] = jnp.zeros_like(dg_ref)

        dg_ref[...] += jnp.sum(acc * xhat, axis=0, keepdims=True)
        dxhat = acc * g_ref[...]
        dx = r * (dxhat - xhat * jnp.mean(dxhat * xhat, axis=-1, keepdims=True)) + dres_ref[...]
        dx_ref[...] = dx
        dxb_ref[...] = (out_scale * dx).astype(dxb_ref.dtype)
    return ep


def _mm_tn(name, a, b, tmo, tno, out_dtype, tk=1024, comm=None, after=None):
    T, Ma = a.shape
    Nb = b.shape[1]
    tk = min(tk, T)
    nk = T // tk

    def main(ins, outs, scr):
        a_ref, b_ref = ins
        (acc_ref,) = scr
        k = pl.program_id(2)

        @pl.when(k == 0)
        def _():
            acc_ref[...] = jnp.zeros_like(acc_ref)

        acc_ref[...] += _dot(a_ref[...], b_ref[...], "tn")

        @pl.when(k == nk - 1)
        def _():
            outs[0][...] = acc_ref[...].astype(outs[0].dtype)

    (out,), extra = _call(
        name, main, (Ma // tmo, Nb // tno, nk),
        [pl.BlockSpec((tk, tmo), lambda i, j, k: (k, i)), pl.BlockSpec((tk, tno), lambda i, j, k: (k, j))],
        [pl.BlockSpec((tmo, tno), lambda i, j, k: (i, j))], [jax.ShapeDtypeStruct((Ma, Nb), out_dtype)],
        (a, b), scratch=[pltpu.VMEM((tmo, tno), F32)], comm=comm, after=after)
    return out, extra


def _swiglu_parts(g, u):
    s = _sigmoid(g)
    silu = g * s
    return [u * (s + silu * (1.0 - s)), silu, silu * u]


def _silu_mul_epilogue(dots, comps):
    g, u = dots
    return _swiglu_parts(g, u)


def _gate_parts_epilogue(dots, comps):
    (g,) = dots
    s = _sigmoid(g)
    silu = g * s
    return [s + silu * (1.0 - s), silu]


def _up_act_epilogue(dots, comps):
    (u,) = dots
    return [u * comps[0].astype(F32), u * comps[1].astype(F32)]


def _dact_epilogue(dots, comps):
    dact = dots[0].astype(BF16)
    return [dact * comps[0], dact * comps[1]]


def _identity_epilogue(dots, comps):
    return list(dots)


def _swap_halves(x):
    lane = lax.broadcasted_iota(jnp.int32, x.shape, 1)
    first = (lane % DK) < (DK // 2)
    return jnp.where(first, pltpu.roll(x, 128 - DK // 2, 1), pltpu.roll(x, DK // 2, 1))


def _rotary(t, cos, sin_signed):
    halves = []
    for p in range(QK_W // 128):
        th = t[:, 128 * p:128 * (p + 1)]
        halves.append(th * cos + _swap_halves(th) * sin_signed)
    return jnp.concatenate(halves, axis=1)


def _rotary_transposed(d, cos, sin_signed):
    halves = []
    for p in range(QK_W // 128):
        dh = d[:, 128 * p:128 * (p + 1)]
        halves.append(dh * cos + _swap_halves(dh * sin_signed))
    return jnp.concatenate(halves, axis=1)


def _log_sigmoid(x):
    return jnp.minimum(x, 0.0) - jnp.log(1.0 + jnp.exp(-jnp.abs(x)))


def _tri_sum(mask, x):
    tri = mask.astype(BF16)
    hi = x.astype(BF16)
    rest = x - hi.astype(F32)
    mid = rest.astype(BF16)
    lo = (rest - mid.astype(F32)).astype(BF16)
    return _dot(tri, hi, "nn") + _dot(tri, mid, "nn") + _dot(tri, lo, "nn")


def _attn_masks():
    row = lax.broadcasted_iota(jnp.int32, (SUPER, SUPER), 0)
    col = lax.broadcasted_iota(jnp.int32, (SUPER, SUPER), 1)
    same = (row // CHUNK) == (col // CHUNK)
    return row, col, same


def _group_inputs(grp, pr, cos, sin_signed, lg, wa2, ba):
    seg = lambda lo, width: pr[:, lo:lo + width].astype(F32)
    if grp == 0:
        q = _rotary(seg(C_RQ, QK_W), cos, sin_signed)
        k = _rotary(seg(C_RK, QK_W), cos, sin_signed) * (DK ** -0.5)
        v = pr[:, C_RV:C_RV + V_W]
        gate = seg(C_RG, V_W)
        pos = lax.broadcasted_iota(jnp.int32, (SUPER, QK_W), 0).astype(F32) + 1.0
        return q, k, v, gate, pos * lg, None, None
    q = seg(C_GQ, QK_W) * (DK ** -0.5)
    k = seg(C_GK, QK_W)
    v = pr[:, C_GV:C_GV + V_W]
    gate = seg(C_GG, V_W)
    glow = pr[:, C_GL:C_GL + GL_W]
    logit = _dot(glow.astype(BF16), wa2.astype(BF16), "nn") + ba
    la = _log_sigmoid(logit) * (1.0 / GATE_NORM)
    row, col, _ = _attn_masks()
    b_cum = _tri_sum(col <= row, la)
    return q, k, v, gate, b_cum, glow, logit


def _decay_factors(q, k, b_cum):
    c = b_cum[SUPER // 2 - 1:SUPER // 2, :]
    bl = b_cum[SUPER - 1:SUPER, :]
    e1 = jnp.exp(b_cum - c)
    e2 = jnp.exp(c - b_cum)
    e_b = jnp.exp(b_cum)
    e_l = jnp.exp(bl - b_cum)
    return dict(e1=e1, e2=e2, eb=e_b, el=e_l, ebl=jnp.exp(bl),
                qp=q * e1, qm=q * e2, kp=k * e1, km=k * e2, qs=q * e_b, kl=k * e_l)


def _state_block_mask():
    r = lax.broadcasted_iota(jnp.int32, (V_W, QK_W), 0)
    c = lax.broadcasted_iota(jnp.int32, (V_W, QK_W), 1)
    return (r // DV) == (c // DK)


def _attn_fwd(proj, cos, sin_signed, lg, wa2p, ba, gn_ret, gn_gla, x_res, w_out, g_next, comm=None):
    T = proj.shape[0]
    n_s = T // SUPER
    D = x_res.shape[1]

    def main(ins, outs, scr):
        pr_ref, cos_ref, sin_ref, lg_ref, wa2_ref, ba_ref, gr_ref, gg_ref, xres_ref, wout_ref, gnext_ref = ins
        o_ref, y_ref, st_ref, x_ref, h_ref = outs
        (s_ref,) = scr
        i = pl.program_id(0)

        @pl.when(i == 0)
        def _():
            s_ref[...] = jnp.zeros_like(s_ref)

        pr = pr_ref
        row, col, same = _attn_masks()
        m1 = col <= row
        m2 = jnp.logical_and(col > row, same)
        lane = lax.broadcasted_iota(jnp.int32, (1, QK_W), 1)
        blockmask = _state_block_mask()
        for grp in range(2):
            q, k, v, gate, b_cum, _, _ = _group_inputs(grp, pr, cos_ref[...], sin_ref[...], lg_ref[...],
                                                      wa2_ref[...], ba_ref[...])
            f = _decay_factors(q, k, b_cum)
            gn = gr_ref[...] if grp == 0 else gg_ref[...]
            s_prev = s_ref[grp]
            st_ref[0, grp] = s_prev
            o_inter = _dot(f["qs"].astype(BF16), s_prev.astype(BF16), "nt")
            kmb = f["km"].astype(BF16)
            kpb = f["kp"].astype(BF16)
            vb = v.astype(BF16)
            heads = [(lane // DK) == h for h in range(HEADS)]
            a1_all = _dot(jnp.concatenate([jnp.where(hm, f["qp"], 0.0).astype(BF16) for hm in heads], axis=0), kmb, "nt")
            a2_all = _dot(jnp.concatenate([jnp.where(hm, f["qm"], 0.0).astype(BF16) for hm in heads], axis=0), kpb, "nt")
            for h in range(HEADS):
                a1 = a1_all[h * SUPER:(h + 1) * SUPER]
                a2 = a2_all[h * SUPER:(h + 1) * SUPER]
                a = jnp.where(m1, a1, jnp.where(m2, a2, 0.0))
                lo = grp * V_W + h * DV
                o_h = _dot(a.astype(BF16), vb[:, h * DV:(h + 1) * DV], "nn") + o_inter[:, h * DV:(h + 1) * DV]
                o_ref[:, lo:lo + DV] = o_h
                r = lax.rsqrt(jnp.mean(o_h * o_h, axis=-1, keepdims=True) + RMS_EPS)
                gte = gate[:, h * DV:(h + 1) * DV]
                y = o_h * r * gn[:, h * DV:(h + 1) * DV] * (gte * _sigmoid(gte))
                y_ref[:, lo:lo + DV] = y.astype(y_ref.dtype)
            upd = _dot(vb, f["kl"].astype(BF16), "tn")
            s_ref[grp] = s_prev * f["ebl"] + jnp.where(blockmask, upd, 0.0)
        xv = xres_ref[...] + _dot(y_ref[...], wout_ref[...], "nn")
        x_ref[...] = xv
        r = lax.rsqrt(jnp.mean(xv * xv, axis=-1, keepdims=True) + RMS_EPS)
        h_ref[...] = (xv * r * gnext_ref[...]).astype(h_ref.dtype)

    const = lambda shape: pl.BlockSpec(shape, lambda i: tuple(0 for _ in shape))
    rows = lambda w: pl.BlockSpec((SUPER, w), lambda i: (i, 0))
    return _call(
        "attn_fwd", main, (n_s,),
        [rows(PROJ_W), rows(128), rows(128),
         const((1, QK_W)), const((GL_W, QK_W)), const((1, QK_W)), const((1, V_W)), const((1, V_W)),
         rows(D), const((2 * V_W, D)), const((1, D))],
        [rows(2 * V_W), rows(2 * V_W), pl.BlockSpec((1, 2, V_W, QK_W), lambda i: (i, 0, 0, 0)), rows(D), rows(D)],
        [jax.ShapeDtypeStruct((T, 2 * V_W), F32), jax.ShapeDtypeStruct((T, 2 * V_W), BF16),
         jax.ShapeDtypeStruct((n_s, 2, V_W, QK_W), F32), jax.ShapeDtypeStruct((T, D), F32),
         jax.ShapeDtypeStruct((T, D), BF16)],
        (proj, cos, sin_signed, lg, wa2p, ba, gn_ret, gn_gla, x_res, w_out, g_next),
        scratch=[pltpu.VMEM((2, V_W, QK_W), F32)], comm=comm)


def _attn_bwd(proj, cos, sin_signed, lg, wa2p, ba, gn_ret, gn_gla, o, dx, w_out, states, comm=None, after=None):
    T = proj.shape[0]
    n_s = T // SUPER
    D = dx.shape[1]

    def main(ins, outs, scr):
        pr_ref, cos_ref, sin_ref, lg_ref, wa2_ref, ba_ref, gr_ref, gg_ref, o_ref, dx_ref, wout_ref, st_ref = ins
        dp_ref, dgr_ref, dgg_ref, dba_ref, dwa_ref = outs
        (ds_ref, dy_ref) = scr
        i = pl.program_id(0)
        dy_ref[...] = _dot(dx_ref[...], wout_ref[...], "nt")

        @pl.when(i == 0)
        def _():
            ds_ref[...] = jnp.zeros_like(ds_ref)
            dgr_ref[...] = jnp.zeros_like(dgr_ref)
            dgg_ref[...] = jnp.zeros_like(dgg_ref)
            dba_ref[...] = jnp.zeros_like(dba_ref)
            dwa_ref[...] = jnp.zeros_like(dwa_ref)

        pr = pr_ref
        cos = cos_ref[...]
        sin_signed = sin_ref[...]
        row, col, same = _attn_masks()
        m1 = col <= row
        m2 = jnp.logical_and(col > row, same)
        m1t = row <= col
        m2t = jnp.logical_and(row > col, same)
        lane = lax.broadcasted_iota(jnp.int32, (1, QK_W), 1)
        blockmask = _state_block_mask()
        for grp in range(2):
            q, k, v, gate, b_cum, glow, logit = _group_inputs(grp, pr, cos, sin_signed, lg_ref[...],
                                                              wa2_ref[...], ba_ref[...])
            f = _decay_factors(q, k, b_cum)
            gn = gr_ref[...] if grp == 0 else gg_ref[...]
            dgn_ref = dgr_ref if grp == 0 else dgg_ref
            do_parts, dgate_parts, dgn_parts = [], [], []
            for h in range(HEADS):
                lo = grp * V_W + h * DV
                o_h = o_ref[:, lo:lo + DV]
                r = lax.rsqrt(jnp.mean(o_h * o_h, axis=-1, keepdims=True) + RMS_EPS)
                n = o_h * r
                gte = gate[:, h * DV:(h + 1) * DV]
                sg = _sigmoid(gte)
                dy_h = dy_ref[:, lo:lo + DV]
                gn_h = gn[:, h * DV:(h + 1) * DV]
                dgate_parts.append(dy_h * n * gn_h * (sg * (1.0 + gte * (1.0 - sg))))
                dz = dy_h * (gte * sg)
                dgn_parts.append(jnp.sum(dz * n, axis=0, keepdims=True))
                dn = dz * gn_h
                do_parts.append(r * (dn - n * jnp.mean(dn * n, axis=-1, keepdims=True)))
            dgn_ref[...] += jnp.concatenate(dgn_parts, axis=1)
            dgate = jnp.concatenate(dgate_parts, axis=1)
            do = jnp.concatenate(do_parts, axis=1)
            dob = do.astype(BF16)
            vb = v.astype(BF16)
            s_prev = st_ref[0, grp]
            ds_new = ds_ref[grp]
            dsb = ds_new.astype(BF16)
            qpb, qmb = f["qp"].astype(BF16), f["qm"].astype(BF16)
            kpb, kmb = f["kp"].astype(BF16), f["km"].astype(BF16)
            dv_parts = []
            heads = [(lane // DK) == h for h in range(HEADS)]
            qp_hs = [jnp.where(hm, f["qp"], 0.0).astype(BF16) for hm in heads]
            qm_hs = [jnp.where(hm, f["qm"], 0.0).astype(BF16) for hm in heads]
            kp_hs = [jnp.where(hm, f["kp"], 0.0).astype(BF16) for hm in heads]
            km_hs = [jnp.where(hm, f["km"], 0.0).astype(BF16) for hm in heads]
            km_stack, kp_stack = jnp.concatenate(km_hs, axis=0), jnp.concatenate(kp_hs, axis=0)
            at1_all = _dot(km_stack, qpb, "nt")
            at2_all = _dot(kp_stack, qmb, "nt")
            da1s, da2s, da1ts, da2ts = [], [], [], []
            for h in range(HEADS):
                at = jnp.where(m1t, at1_all[h * SUPER:(h + 1) * SUPER],
                               jnp.where(m2t, at2_all[h * SUPER:(h + 1) * SUPER], 0.0))
                do_h = dob[:, h * DV:(h + 1) * DV]
                v_h = vb[:, h * DV:(h + 1) * DV]
                dv_parts.append(_dot(at.astype(BF16), do_h, "nn"))
                da = _dot(do_h, v_h, "nt")
                dat = _dot(v_h, do_h, "nt")
                da1s.append(jnp.where(m1, da, 0.0).astype(BF16))
                da2s.append(jnp.where(m2, da, 0.0).astype(BF16))
                da1ts.append(jnp.where(m1t, dat, 0.0).astype(BF16))
                da2ts.append(jnp.where(m2t, dat, 0.0).astype(BF16))
            dqp = _dot(jnp.concatenate(da1s, axis=1), km_stack, "nn")
            dqm = _dot(jnp.concatenate(da2s, axis=1), kp_stack, "nn")
            dkm = _dot(jnp.concatenate(da1ts, axis=1), jnp.concatenate(qp_hs, axis=0), "nn")
            dkp = _dot(jnp.concatenate(da2ts, axis=1), jnp.concatenate(qm_hs, axis=0), "nn")
            klb = f["kl"].astype(BF16)
            qsb = f["qs"].astype(BF16)
            dqs = _dot(dob, s_prev.astype(BF16), "nn")
            dkl = _dot(vb, dsb, "nn")
            dv = jnp.concatenate(dv_parts, axis=1) + _dot(klb, dsb, "nt")
            ds_ref[grp] = ds_new * f["ebl"] + jnp.where(blockmask, _dot(dob, qsb, "tn"), 0.0)
            dq = dqp * f["e1"] + dqm * f["e2"] + dqs * f["eb"]
            dk = dkm * f["e2"] + dkp * f["e1"] + dkl * f["el"]
            if grp == 0:
                dq = _rotary_transposed(dq, cos, sin_signed)
                dk = _rotary_transposed(dk * (DK ** -0.5), cos, sin_signed)
                dp_ref[:, C_RQ:C_RQ + QK_W] = dq.astype(dp_ref.dtype)
                dp_ref[:, C_RK:C_RK + QK_W] = dk.astype(dp_ref.dtype)
                dp_ref[:, C_RV:C_RV + V_W] = dv.astype(dp_ref.dtype)
                dp_ref[:, C_RG:C_RG + V_W] = dgate.astype(dp_ref.dtype)
            else:
                dkl_kl = dkl * klb.astype(F32)
                db = (dqp * qpb.astype(F32) - dkm * kmb.astype(F32) - dqm * qmb.astype(F32)
                      + dkp * kpb.astype(F32) + dqs * qsb.astype(F32) - dkl_kl)
                last = (jnp.sum(dkl_kl, axis=0, keepdims=True)
                        + f["ebl"] * jnp.sum(s_prev * ds_new, axis=0, keepdims=True))
                rowq = lax.broadcasted_iota(jnp.int32, (SUPER, QK_W), 0)
                db = db + jnp.where(rowq == SUPER - 1, last, 0.0)
                dla = _tri_sum(col >= row, db)
                dlogit = dla * (1.0 / GATE_NORM) * (1.0 - _sigmoid(logit))
                dlb = dlogit.astype(BF16)
                dglow = _dot(dlb, wa2_ref[...].astype(BF16), "nt")
                dwa_ref[...] += _dot(glow.astype(BF16), dlb, "tn")
                dba_ref[...] += jnp.sum(dlogit, axis=0, keepdims=True)
                dp_ref[:, C_GQ:C_GQ + QK_W] = (dq * (DK ** -0.5)).astype(dp_ref.dtype)
                dp_ref[:, C_GK:C_GK + QK_W] = dk.astype(dp_ref.dtype)
                dp_ref[:, C_GV:C_GV + V_W] = dv.astype(dp_ref.dtype)
                dp_ref[:, C_GG:C_GG + V_W] = dgate.astype(dp_ref.dtype)
                dp_ref[:, C_GL:C_GL + GL_W] = dglow.astype(dp_ref.dtype)

    rev = lambda i: n_s - 1 - i
    const = lambda shape: pl.BlockSpec(shape, lambda i: tuple(0 for _ in shape))
    return _call(
        "attn_bwd", main, (n_s,),
        [pl.BlockSpec((SUPER, PROJ_W), lambda i: (rev(i), 0)),
         pl.BlockSpec((SUPER, 128), lambda i: (rev(i), 0)), pl.BlockSpec((SUPER, 128), lambda i: (rev(i), 0)),
         const((1, QK_W)), const((GL_W, QK_W)), const((1, QK_W)), const((1, V_W)), const((1, V_W)),
         pl.BlockSpec((SUPER, 2 * V_W), lambda i: (rev(i), 0)),
         pl.BlockSpec((SUPER, D), lambda i: (rev(i), 0)), const((2 * V_W, D)),
         pl.BlockSpec((1, 2, V_W, QK_W), lambda i: (rev(i), 0, 0, 0))],
        [pl.BlockSpec((SUPER, PROJ_W), lambda i: (rev(i), 0)),
         const((1, V_W)), const((1, V_W)), const((1, QK_W)), const((GL_W, QK_W))],
        [jax.ShapeDtypeStruct((T, PROJ_W), BF16),
         jax.ShapeDtypeStruct((1, V_W), F32), jax.ShapeDtypeStruct((1, V_W), F32),
         jax.ShapeDtypeStruct((1, QK_W), F32), jax.ShapeDtypeStruct((GL_W, QK_W), F32)],
        (proj, cos, sin_signed, lg, wa2p, ba, gn_ret, gn_gla, o, dx, w_out, states),
        scratch=[pltpu.VMEM((2, V_W, QK_W), F32), pltpu.VMEM((SUPER, 2 * V_W), F32)], comm=comm, after=after)


def _rotary_tables(T):
    half = DK // 2
    inv = ROPE_BASE ** (-jnp.arange(half, dtype=F32) * 2.0 / DK)
    ang = jnp.arange(T, dtype=F32)[:, None] * inv[None, :]
    cos, sin = jnp.cos(ang), jnp.sin(ang)
    cos_head = jnp.concatenate([cos, cos], axis=1)
    sin_head = jnp.concatenate([-sin, sin], axis=1)
    return jnp.tile(cos_head, (1, 128 // DK)), jnp.tile(sin_head, (1, 128 // DK))


def _sum_devices(name, gathered, m_per):
    def body(g_ref, o_ref):
        acc = g_ref[0:m_per, :]
        for k in range(1, N_DEV):
            acc = acc + g_ref[k * m_per:(k + 1) * m_per, :]
        o_ref[...] = acc

    return pl.pallas_call(body, name=name, out_shape=jax.ShapeDtypeStruct((m_per, 128), F32))(gathered)


def _owner_sums(name, items, owner, comm=None):
    counts = [1 + len(landed) for _, landed in items]

    def main(ins, outs, scr):
        at = 0
        for o_ref, n in zip(outs, counts):
            acc = ins[at][...].astype(F32)
            for l_ref in ins[at + 1:at + n]:
                for j in range(l_ref.shape[0]):
                    acc = acc + l_ref[j].astype(F32)
            o_ref[...] = acc
            at += n

    once = pl.Buffered(1)
    in_specs, out_specs, out_shape, args = [], [], [], []
    for grad, landed in items:
        R, C = grad.shape[-2:]
        in_specs.append(pl.BlockSpec((None, None, R, C), lambda i, s: (s[0], s[1], 0, 0), pipeline_mode=once))
        in_specs += [pl.BlockSpec(tuple(l.shape), lambda i, s: (0, 0, 0), pipeline_mode=once) for l in landed]
        out_specs.append(pl.BlockSpec((R, C), lambda i, s: (0, 0)))
        out_shape.append(jax.ShapeDtypeStruct((R, C), F32))
        args += [grad, *landed]
    return _call(name, main, (1,), in_specs, out_specs, out_shape, args, comm=comm, prefetch=owner)


def _adamw_group(name, items, n_blocks, comm=None):
    n = len(items)

    def main(ins, outs, scr):
        for p in range(n):
            g_ref, w_ref, m_ref, v_ref = ins[4 * p:4 * p + 4]
            d_ref, nm_ref, nv_ref = outs[3 * p:3 * p + 3]
            gv = g_ref[...]
            nm = ADAM_B1 * m_ref[...] + (1.0 - ADAM_B1) * gv
            nv = ADAM_B2 * v_ref[...] + (1.0 - ADAM_B2) * (gv * gv)
            m_hat = nm / (1.0 - ADAM_B1 ** ADAM_STEP)
            v_hat = nv / (1.0 - ADAM_B2 ** ADAM_STEP)
            d_ref[...] = -ADAM_LR * (m_hat / (jnp.sqrt(v_hat) + ADAM_EPS) + ADAM_WD * w_ref[...])
            nm_ref[...] = nm
            nv_ref[...] = nv

    in_specs, out_specs, out_shape, args = [], [], [], []
    for item in items:
        R, C = item[1].shape
        assert R % n_blocks == 0
        spec = pl.BlockSpec((R // n_blocks, C), lambda i: (i, 0))
        in_specs += [spec] * 4
        out_specs += [spec] * 3
        out_shape += [jax.ShapeDtypeStruct((R, C), F32)] * 3
        args += list(item)
    outs, extra = _call(name, main, (n_blocks,), in_specs, out_specs, out_shape, args, comm=comm)
    return [tuple(outs[3 * p:3 * p + 3]) for p in range(n)], extra


SMALL_ORDER = ("ffn1", "mix", "ffn2", "final", "ret", "gla", "b_a")


def kernel(x, ffn1_norm_g, ffn1_w_gate, ffn1_w_up, ffn1_w_down, mix_norm_g, w_in, ret_norm_g, gla_w_a2, gla_b_a, gla_norm_g, w_out, ffn2_norm_g, ffn2_w_gate, ffn2_w_up, ffn2_w_down, final_norm_g, loss_target, m_ffn1_norm_g, m_ffn1_w_gate, m_ffn1_w_up, m_ffn1_w_down, m_mix_norm_g, m_w_in, m_ret_norm_g, m_gla_w_a2, m_gla_b_a, m_gla_norm_g, m_w_out, m_ffn2_norm_g, m_ffn2_w_gate, m_ffn2_w_up, m_ffn2_w_down, m_final_norm_g, v_ffn1_norm_g, v_ffn1_w_gate, v_ffn1_w_up, v_ffn1_w_down, v_mix_norm_g, v_w_in, v_ret_norm_g, v_gla_w_a2, v_gla_b_a, v_gla_norm_g, v_w_out, v_ffn2_norm_g, v_ffn2_w_gate, v_ffn2_w_up, v_ffn2_w_down, v_final_norm_g):
    xi, yi, ci = _coords()
    dev = 4 * xi + 2 * yi + ci
    owner = jnp.stack([2 * xi + yi, ci]).astype(jnp.int32)

    x0, target = x[0], loss_target[0]
    T, D = x0.shape
    fb = ffn1_w_gate.shape[2]
    ib = w_in.shape[2]
    ab = gla_w_a2.shape[2]
    F = N_DEV * fb
    cos, sin_signed = _rotary_tables(T)
    lg = jnp.repeat(jnp.log(1.0 - 2.0 ** (-5.0 - jnp.arange(HEADS, dtype=F32))), DK)[None, :]
    g_final = final_norm_g.reshape(1, D)

    g1_loc = ffn1_w_gate[0].T[None].astype(BF16)
    u1_loc = ffn1_w_up[0].T[None].astype(BF16)
    d1_loc = ffn1_w_down.astype(BF16)
    g2_loc = ffn2_w_gate[0].T[None].astype(BF16)
    u2_loc = ffn2_w_up[0].T[None].astype(BF16)
    d2_loc = ffn2_w_down.astype(BF16)
    in_loc = w_in[0].T.astype(BF16)
    out_loc = w_out[0].astype(BF16)

    h1, (g1,) = _rms_fwd("ffn1_rms", x0, ffn1_norm_g, comm=_AllGather([g1_loc], ["stack"]))
    g1 = g1.reshape(1, F, D)
    (dsl1, sl1), (u1,) = _mm_nstream("ffn1_gate", h1, [g1], [0], "nt", [], [BF16, BF16], _gate_parts_epilogue, cn=256,
                                     comm=_AllGather([u1_loc], ["stack"]))
    u1 = u1.reshape(1, F, D)
    (dsu1, act1), (d1,) = _mm_nstream("ffn1_up", h1, [u1], [0], "nt", [dsl1, sl1], [BF16, BF16],
                                      _up_act_epilogue, cn=256, comm=_AllGather([d1_loc], ["stack"]))
    d1 = d1.reshape(1, F, D)
    f32_tile, bf16_tile, f32_vec = (F32, "tile"), (BF16, "tile"), (F32, "vec")
    (x1, h2), (in_all, a_all) = _mm_mstream(
        "ffn1_down", [act1], [d1], [0], "nn", [(x0, "tile"), (mix_norm_g, "vec")], [f32_tile, bf16_tile],
        _residual_rms_epilogue(0.5), comm=_AllGather([in_loc, gla_w_a2[0]], ["plain", "plain"]))
    w_in_t = jnp.pad(in_all.reshape(1, N_DEV * ib, D), ((0, 0), (0, PROJ_W - N_DEV * ib), (0, 0)))
    wa2 = jnp.transpose(a_all, (1, 0, 2)).reshape(GATE_RANK, N_DEV * ab)
    wa2p = jnp.pad(wa2, ((0, GL_W - GATE_RANK), (0, 0)))

    (proj,), (g2, out_all) = _mm_nstream("mix_proj", h2, [w_in_t], [0], "nt", [], [BF16], _identity_epilogue, cn=640,
                                         comm=_AllGather([g2_loc, out_loc], ["stack", "plain"]))
    w_out_full = out_all.reshape(D, D)
    (o, ymix, states, x2, h3), (u2,) = _attn_fwd(proj, cos, sin_signed, lg, wa2p, gla_b_a, ret_norm_g, gla_norm_g,
                                                 x1, w_out_full, ffn2_norm_g, comm=_AllGather([u2_loc], ["stack"]))
    g2, u2 = g2.reshape(1, F, D), u2.reshape(1, F, D)

    (dsu2, sl2, act2), (d2,) = _mm_nstream(
        "ffn2_up", h3, [g2, u2], [0, 0], "nt", [], [BF16, BF16, BF16], _silu_mul_epilogue, cn=256,
        comm=_AllGather([d2_loc], ["stack"]))
    d2 = d2.reshape(1, F, D)
    (dx3, dy3b, d_final, loss), _ = _mm_mstream(
        "ffn2_down", [act2], [d2], [0], "nn", [(x2, "tile"), (g_final, "vec"), (target, "tile")],
        [f32_tile, bf16_tile, f32_vec, f32_vec], _final_loss_epilogue(0.5, 0.5))

    sent = {}

    def send(**grads):
        started = _send_to_owners("send_" + "_".join(grads), list(grads.values()))
        sent.update(zip(grads, started))
        return started[0][4]

    dwd2, _ = _mm_tn("ffn2b_dwd", act2, dy3b, F // 2, D, BF16)
    (dgate2, dup2), _ = _mm_nstream("ffn2b_dact", dy3b, [d2], [0], "nt", [dsu2, sl2], [BF16, BF16],
                                    _dact_epilogue, cn=256)
    dwg2, _ = _mm_tn("ffn2b_dwg", dgate2, h3, F // 2, D, BF16)
    dwu2, _ = _mm_tn("ffn2b_dwu", dup2, h3, F // 2, D, BF16)
    tok = send(wd2=dwd2.reshape(4, 2, fb, D), wg2=dwg2.reshape(4, 2, fb, D), wu2=dwu2.reshape(4, 2, fb, D))
    rms_outs = [f32_tile, bf16_tile, f32_vec]
    (dx2, dx2b, d_g2), _ = _mm_mstream(
        "ffn2b_dh", [dgate2, dup2], [g2, u2], [0, 0], "nn", [(x2, "tile"), (ffn2_norm_g, "vec"), (dx3, "tile")],
        rms_outs, _rms_bwd_epilogue(1.0), after=tok)

    dwout, _ = _mm_tn("mixb_dwout", ymix, dx2b, D, D, BF16)
    (dproj, d_ret, d_gla, d_ba, d_wa2p), _ = _attn_bwd(
        proj, cos, sin_signed, lg, wa2p, gla_b_a, ret_norm_g, gla_norm_g, o, dx2b, w_out_full, states)
    dwin_t, _ = _mm_tn("mixb_dwin", dproj, h2, 640, D, BF16, tk=2048)
    tok = send(wout=dwout.reshape(4, 2, D // N_DEV, D), win=dwin_t[:N_DEV * ib].reshape(4, 2, ib, D))
    (dx1, dy1b, d_gmix), _ = _mm_mstream(
        "mixb_dh", [dproj], [w_in_t], [0], "nn", [(x1, "tile"), (mix_norm_g, "vec"), (dx2, "tile")],
        rms_outs, _rms_bwd_epilogue(0.5), after=tok)

    dwd1, _ = _mm_tn("ffn1b_dwd", act1, dy1b, F // 2, D, BF16)
    tok = send(wd1=dwd1.reshape(4, 2, fb, D))
    (dgate1, dup1), _ = _mm_nstream("ffn1b_dact", dy1b, [d1], [0], "nt", [dsu1, sl1], [BF16, BF16],
                                    _dact_epilogue, cn=256, after=tok)
    dwg1, _ = _mm_tn("ffn1b_dwg", dgate1, h1, F // 2, D, BF16)
    tok = send(wg1=dwg1.reshape(4, 2, fb, D))
    dwu1, _ = _mm_tn("ffn1b_dwu", dup1, h1, F // 2, D, BF16, after=tok)
    tok = send(wu1=dwu1.reshape(4, 2, fb, D))
    (dx0, _, d_g1), _ = _mm_mstream(
        "ffn1b_dh", [dgate1, dup1], [g1, u1], [0, 0], "nn", [(x0, "tile"), (ffn1_norm_g, "vec"), (dx1, "tile")],
        rms_outs, _rms_bwd_epilogue(1.0), after=tok)

    small = dict(ffn1=d_g1, mix=d_gmix, ffn2=d_g2, final=d_final, ret=d_ret, gla=d_gla, b_a=d_ba)
    flat = jnp.concatenate([small[k].reshape(-1) for k in SMALL_ORDER]
                           + [d_wa2p[:GATE_RANK].reshape(-1), loss[0, :128]])
    rows = -(-flat.shape[0] // 128)
    rows = -(-rows // 8) * 8
    packed = jnp.pad(flat, (0, rows * 128 - flat.shape[0])).reshape(rows, 128)

    transposed = ("ffn1_w_gate", "ffn1_w_up", "ffn2_w_gate", "ffn2_w_up", "w_in")

    def to_2d(nm, a):
        if nm in transposed:
            return a[0].T
        return a.reshape((1, a.shape[0]) if a.ndim == 1 else a.shape[-2:])

    def from_2d(nm, a):
        return a.T[None] if nm in transposed else a.reshape(params[nm][0].shape)

    def arrived(nm, after):
        grad, landed = _await_owners("await_" + nm, sent[nm], after)
        return grad, [landed]

    sums_a, (gathered,) = _owner_sums(
        "sum_a", [arrived(nm, dx0) for nm in ("wg2", "wu2", "wd2", "win", "wout")], owner,
        comm=_AllGather([packed], ["plain"]))
    params = dict(
        ffn2_w_gate=(ffn2_w_gate, m_ffn2_w_gate, v_ffn2_w_gate), ffn2_w_up=(ffn2_w_up, m_ffn2_w_up, v_ffn2_w_up),
        ffn2_w_down=(ffn2_w_down, m_ffn2_w_down, v_ffn2_w_down), w_in=(w_in, m_w_in, v_w_in),
        w_out=(w_out, m_w_out, v_w_out), ffn1_w_gate=(ffn1_w_gate, m_ffn1_w_gate, v_ffn1_w_gate),
        ffn1_w_up=(ffn1_w_up, m_ffn1_w_up, v_ffn1_w_up), ffn1_w_down=(ffn1_w_down, m_ffn1_w_down, v_ffn1_w_down),
        ffn1_norm_g=(ffn1_norm_g, m_ffn1_norm_g, v_ffn1_norm_g), mix_norm_g=(mix_norm_g, m_mix_norm_g, v_mix_norm_g),
        ret_norm_g=(ret_norm_g, m_ret_norm_g, v_ret_norm_g), gla_w_a2=(gla_w_a2, m_gla_w_a2, v_gla_w_a2),
        gla_b_a=(gla_b_a, m_gla_b_a, v_gla_b_a), gla_norm_g=(gla_norm_g, m_gla_norm_g, v_gla_norm_g),
        ffn2_norm_g=(ffn2_norm_g, m_ffn2_norm_g, v_ffn2_norm_g), final_norm_g=(final_norm_g, m_final_norm_g, v_final_norm_g))
    grads, updates = {}, {}

    def run_adam(name, names, grad_2d, n_blocks):
        items = [(grad_2d[nm],) + tuple(to_2d(nm, a) for a in params[nm]) for nm in names]
        res, _ = _adamw_group(name, items, n_blocks)
        for nm, r in zip(names, res):
            grads[nm] = from_2d(nm, grad_2d[nm])
            updates[nm] = tuple(from_2d(nm, a) for a in r)
        return res

    grads_a = {"ffn2_w_gate": sums_a[0], "ffn2_w_up": sums_a[1], "ffn2_w_down": sums_a[2], "w_out": sums_a[4]}
    run_adam("adamw_w_in", ["w_in"], {"w_in": sums_a[3]}, 1)
    done_a = run_adam("adamw_a", list(grads_a), grads_a, 4)[0][0]
    sums_b, _ = _owner_sums("sum_b", [arrived(nm, done_a) for nm in ("wg1", "wu1", "wd1")], owner)
    grads_b = {"ffn1_w_gate": sums_b[0], "ffn1_w_up": sums_b[1], "ffn1_w_down": sums_b[2]}
    run_adam("adamw_b", list(grads_b), grads_b, 4)

    total = _sum_devices("sum_small", gathered.reshape(N_DEV * rows, 128), rows).reshape(-1)
    sizes = [small[k].size for k in SMALL_ORDER] + [GATE_RANK * QK_W, 128]
    offs = [0]
    for s in sizes:
        offs.append(offs[-1] + s)
    pieces = [total[offs[i]:offs[i + 1]] for i in range(len(sizes))]
    g_small = {k: pieces[i].reshape(small[k].shape) for i, k in enumerate(SMALL_ORDER)}
    g_wa2_full = pieces[len(SMALL_ORDER)].reshape(GATE_RANK, QK_W)
    g_wa2 = lax.dynamic_slice(g_wa2_full, (0, dev * ab), (GATE_RANK, ab))
    loss_total = pieces[len(SMALL_ORDER) + 1][0]

    small_grads = {"ffn1_norm_g": g_small["ffn1"], "mix_norm_g": g_small["mix"], "ret_norm_g": g_small["ret"],
                   "gla_w_a2": g_wa2, "gla_b_a": g_small["b_a"], "gla_norm_g": g_small["gla"],
                   "ffn2_norm_g": g_small["ffn2"], "final_norm_g": g_small["final"]}
    run_adam("adamw_small", list(small_grads), small_grads, 1)

    order = ("ffn1_norm_g", "ffn1_w_gate", "ffn1_w_up", "ffn1_w_down", "mix_norm_g", "w_in", "ret_norm_g", "gla_w_a2",
             "gla_b_a", "gla_norm_g", "w_out", "ffn2_norm_g", "ffn2_w_gate", "ffn2_w_up", "ffn2_w_down", "final_norm_g")
    return (loss_total, dx0[None], *[grads[nm] for nm in order], *[updates[nm][0] for nm in order],
            *[updates[nm][1] for nm in order], *[updates[nm][2] for nm in order])
```

```python
import functools
import math

import jax
import jax.numpy as jnp
from jax import lax
from jax.experimental import pallas as pl
from jax.experimental.pallas import tpu as pltpu

F32 = jnp.float32
BF16 = jnp.bfloat16
MESH = pl.DeviceIdType.MESH
HBM = pl.BlockSpec(memory_space=pltpu.HBM)

N_DEV = 8
RMS_EPS = 1e-6
ROPE_BASE = 10000.0
HEADS = 4
DK = 64
DV = 128
QK_W = HEADS * DK
V_W = HEADS * DV
GATE_RANK = 16
GATE_NORM = 16.0
CHUNK = 64
SUPER = 256
PROJ_W = 3200
C_RQ, C_RK, C_RV, C_RG, C_GQ, C_GK, C_GV, C_GG, C_GL = 0, 256, 512, 1024, 1536, 1792, 2048, 2560, 3072
GL_W = PROJ_W - C_GL
ADAM_LR, ADAM_B1, ADAM_B2, ADAM_EPS, ADAM_WD, ADAM_STEP = 0.001, 0.9, 0.999, 1e-08, 0.01, 10
VMEM_LIMIT_V7X = 52 * 1024 * 1024


def _cparams(**kw):
    return pltpu.CompilerParams(vmem_limit_bytes=VMEM_LIMIT_V7X, **kw)


def _dot(a, b, form, precision=None):
    dims = {"nn": (((1,), (0,)), ((), ())), "nt": (((1,), (1,)), ((), ())), "tn": (((0,), (0,)), ((), ()))}[form]
    return lax.dot_general(a, b, dims, preferred_element_type=F32, precision=precision)


def _sigmoid(x):
    return 1.0 / (1.0 + jnp.exp(-x))


def _coords():
    return lax.axis_index("x"), lax.axis_index("y"), lax.axis_index("c")


class _NoComm:
    inputs, out_shapes, scratch = (), (), ()


class _AllGather:
    def __init__(self, arrays, kinds):
        self.inputs = tuple(arrays)
        self.kinds = tuple(kinds)
        n = len(arrays)
        self.out_shapes = tuple(
            jax.ShapeDtypeStruct((a.shape[0], N_DEV) + a.shape[1:] if k == "stack" else (N_DEV,) + a.shape, a.dtype)
            for a, k in zip(arrays, kinds))
        self.scratch = (pltpu.SemaphoreType.DMA((n, 7)), pltpu.SemaphoreType.DMA((n, 7)),
                        pltpu.SemaphoreType.DMA((n,)))

    def _ctx(self, srcs, outs, sems):
        send_sems, recv_sems, local_sems = sems
        x, y, c = _coords()
        me, sibling = (x, y, c), (x, y, 1 - c)
        chips = [(1 - x, y), (x, 1 - y), (1 - x, 1 - y)]

        def blk(m, dev):
            k = 4 * dev[0] + 2 * dev[1] + dev[2]
            return outs[m].at[:, k] if self.kinds[m] == "stack" else outs[m].at[k]

        def copy(m, s, block, to, src=None):
            return pltpu.make_async_remote_copy(
                src_ref=blk(m, block) if src is None else src, dst_ref=blk(m, block),
                send_sem=send_sems.at[m, s], recv_sem=recv_sems.at[m, s], device_id=to, device_id_type=MESH)

        def mine(m):
            return pltpu.make_async_copy(srcs[m], blk(m, me), local_sems.at[m])

        def first(m):
            return [copy(m, 0, me, sibling, src=srcs[m])] + [
                copy(m, 1 + j, me, (*chip, c), src=srcs[m]) for j, chip in enumerate(chips)]

        return me, sibling, chips, c, copy, mine, first

    def start(self, srcs, outs, sems):
        me, sibling, chips, c, copy, mine, first = self._ctx(srcs, outs, sems)
        for m in range(len(srcs)):
            mine(m).start()
            for cp in first(m):
                cp.start()

    def mid(self, srcs, outs, sems):
        me, sibling, chips, c, copy, mine, first = self._ctx(srcs, outs, sems)
        for j, chip in enumerate(chips):
            for m in range(len(srcs)):
                copy(m, 1 + j, (*chip, c), me).wait_recv()
                copy(m, 4 + j, (*chip, c), sibling).start()

    def finish(self, srcs, outs, sems):
        me, sibling, chips, c, copy, mine, first = self._ctx(srcs, outs, sems)
        for m in range(len(srcs)):
            copy(m, 0, sibling, me).wait_recv()
            for j, chip in enumerate(chips):
                copy(m, 4 + j, (*chip, 1 - c), me).wait_recv()
            for cp in first(m):
                cp.wait_send()
            for j, chip in enumerate(chips):
                copy(m, 4 + j, (*chip, c), sibling).wait_send()
            mine(m).wait()


RELATIONS = ((0, 0, 1), (1, 0, 0), (0, 1, 0), (1, 1, 0), (1, 0, 1), (0, 1, 1), (1, 1, 1))
SEM = pl.BlockSpec(memory_space=pltpu.SEMAPHORE)
SPLIT_PARAMS = dict(has_side_effects=pltpu.SideEffectType.DATAFLOW_SIDE_EFFECTING)


def _owner_copies(grad_ref, land_ref, send_sems, recv_sems):
    x, y, c = _coords()
    copies = []
    for s, (fx, fy, fc) in enumerate(RELATIONS):
        px = 1 - x if fx else x
        py = 1 - y if fy else y
        pc = 1 - c if fc else c
        copies.append(pltpu.make_async_remote_copy(
            src_ref=grad_ref.at[2 * px + py, pc], dst_ref=land_ref.at[s], send_sem=send_sems.at[s],
            recv_sem=recv_sems.at[s], device_id=(px, py, pc), device_id_type=MESH))
    return copies


def _send_to_owners(name, grads):
    n, k = len(RELATIONS), len(grads)
    land_shapes = [(n,) + g.shape[2:] for g in grads]

    def body(*refs):
        ins, outs = refs[:2 * k], refs[2 * k:]
        for j in range(k):
            for cp in _owner_copies(ins[2 * j], ins[2 * j + 1], outs[4 * j], outs[4 * j + 1]):
                cp.start()
        outs[-1][...] = jnp.zeros_like(outs[-1])

    out_shape, out_specs, args, aliases = [], [], [], {}
    for j, (g, land_shape) in enumerate(zip(grads, land_shapes)):
        out_shape += [pltpu.SemaphoreType.DMA((n,)), pltpu.SemaphoreType.DMA((n,)), pltpu.HBM(g.shape, g.dtype),
                      pltpu.HBM(land_shape, g.dtype)]
        out_specs += [SEM, SEM, HBM, HBM]
        args += [pltpu.with_memory_space_constraint(g, pltpu.HBM),
                 pltpu.with_memory_space_constraint(lax.empty(land_shape, g.dtype), pltpu.HBM)]
        aliases.update({2 * j: 4 * j + 2, 2 * j + 1: 4 * j + 3})
    res = pl.pallas_call(
        body, name=name, out_shape=out_shape + [jax.ShapeDtypeStruct((8, 128), F32)],
        in_specs=[HBM] * (2 * k), out_specs=out_specs + [pl.BlockSpec(memory_space=pltpu.VMEM)],
        input_output_aliases=aliases, compiler_params=pltpu.CompilerParams(**SPLIT_PARAMS),
    )(*args)
    return [tuple(res[4 * j:4 * j + 4]) + (res[-1],) for j in range(k)]


def _await_owners(name, started, after):
    send_sems, recv_sems, g_thru, land_thru, _ = started

    def body(g_ref, land_ref, send_sems, recv_sems, after_ref, g_out, land_out):
        for cp in _owner_copies(g_ref, land_ref, send_sems, recv_sems):
            cp.wait_send()
            cp.wait_recv()

    return pl.pallas_call(
        body, name=name, out_shape=(pltpu.HBM(g_thru.shape, g_thru.dtype), pltpu.HBM(land_thru.shape, land_thru.dtype)),
        in_specs=(HBM, HBM, SEM, SEM, pl.BlockSpec(memory_space=pl.ANY)), out_specs=(HBM, HBM),
        input_output_aliases={0: 0, 1: 1}, compiler_params=pltpu.CompilerParams(**SPLIT_PARAMS),
    )(g_thru, land_thru, send_sems, recv_sems, after)


def _call(name, main, grid, in_specs, out_specs, out_shape, args, scratch=(), comm=None, prefetch=None, after=None):
    comm = comm or _NoComm()
    n_main = len(in_specs)
    if after is not None:
        in_specs = list(in_specs) + [pl.BlockSpec(after.shape, lambda *_: (0,) * after.ndim)]
        args = tuple(args) + (after,)
    counts = [len(in_specs), len(comm.inputs), len(out_shape), len(comm.out_shapes), len(scratch), len(comm.scratch)]
    n_steps = math.prod(grid)
    hosted = bool(comm.inputs)

    def body(*refs):
        if prefetch is not None:
            refs = refs[1:]
        parts, at = [], 0
        for n in counts:
            parts.append(refs[at:at + n])
            at += n
        ins, c_in, outs, c_out, scr, c_scr = parts
        ins = ins[:n_main]
        step = pl.program_id(0)
        for d in range(1, len(grid)):
            step = step * grid[d] + pl.program_id(d)
        if hosted:
            @pl.when(step == 0)
            def _():
                comm.start(c_in, c_out, c_scr)
        main(ins, outs, scr)
        if hosted:
            @pl.when(step == max(n_steps - 2, 0))
            def _():
                comm.mid(c_in, c_out, c_scr)

            @pl.when(step == n_steps - 1)
            def _():
                comm.finish(c_in, c_out, c_scr)

    all_in = list(in_specs) + [HBM] * counts[1]
    all_out = list(out_specs) + [HBM] * counts[3]
    all_scratch = list(scratch) + list(comm.scratch)
    shapes = list(out_shape) + list(comm.out_shapes)
    if prefetch is None:
        res = pl.pallas_call(body, name=name, grid=grid, in_specs=all_in, out_specs=all_out, out_shape=shapes,
                             scratch_shapes=all_scratch, compiler_params=_cparams())(*args, *comm.inputs)
    else:
        res = pl.pallas_call(
            body, name=name, out_shape=shapes,
            grid_spec=pltpu.PrefetchScalarGridSpec(num_scalar_prefetch=1, grid=grid, in_specs=all_in,
                                                   out_specs=all_out, scratch_shapes=all_scratch),
            compiler_params=_cparams())(prefetch, *args, *comm.inputs)
    return res[:counts[2]], res[counts[2]:]


def _rms_fwd(name, x, g, comm=None):
    T, D = x.shape
    tm = min(T, 512)

    def main(ins, outs, scr):
        x_ref, g_ref = ins
        xv = x_ref[...]
        r = lax.rsqrt(jnp.mean(xv * xv, axis=-1, keepdims=True) + RMS_EPS)
        outs[0][...] = (xv * r * g_ref[...]).astype(outs[0].dtype)

    tile = pl.BlockSpec((tm, D), lambda i: (i, 0))
    (h,), extra = _call(name, main, (T // tm,), [tile, pl.BlockSpec((1, D), lambda i: (0, 0))], [tile],
                        [jax.ShapeDtypeStruct((T, D), BF16)], (x, g), comm=comm)
    return h, extra


def _final_loss_epilogue(scale, out_scale):
    def ep(acc, ex, outs):
        res_ref, g_ref, t_ref = ex
        dx_ref, dxb_ref, dg_ref, loss_ref = outs
        n = acc.shape[-1]
        xv = res_ref[...] + scale * acc
        r = lax.rsqrt(jnp.mean(xv * xv, axis=-1, keepdims=True) + RMS_EPS)
        xhat = xv * r
        err = xhat * g_ref[...] - t_ref[...]

        @pl.when(pl.program_id(0) == 0)
        def _():
            dg_ref[...] = jnp.zeros_like(dg_ref)
            loss_ref[...] = jnp.zeros_like(loss_ref)

        loss_ref[...] += jnp.broadcast_to(jnp.sum(err * err) * (0.5 / n), loss_ref.shape)
        dy = err * (1.0 / n)
        dg_ref[...] += jnp.sum(dy * xhat, axis=0, keepdims=True)
        dxhat = dy * g_ref[...]
        dx = r * (dxhat - xhat * jnp.mean(dxhat * xhat, axis=-1, keepdims=True))
        dx_ref[...] = dx
        dxb_ref[...] = (out_scale * dx).astype(dxb_ref.dtype)
    return ep


def _mm_nstream(name, a, ws, w_sel, w_form, comps, out_dtypes, epilogue, cn, rows=1024, comm=None, after=None):
    T, K = a.shape
    N = ws[0].shape[1]
    rows = min(rows, T)
    assert N % cn == 0 and T % rows == 0
    n_w, n_c = len(ws), len(comps)

    def main(ins, outs, scr):
        a_ref = ins[0]
        w_refs = ins[1:1 + n_w]
        c_refs = ins[1 + n_w:1 + n_w + n_c]

        for r in range(T // rows):
            sl = slice(r * rows, (r + 1) * rows)
            a_blk = a_ref[sl, :]
            dots = [_dot(a_blk, w_ref[...], w_form) for w_ref in w_refs]
            res = epilogue(dots, [c_ref[sl, :] for c_ref in c_refs])
            for o_ref, o in zip(outs, res):
                o_ref[sl, :] = o.astype(o_ref.dtype)

    if w_form == "nt":
        w_specs = [pl.BlockSpec((None, cn, K), functools.partial(lambda j, s: (s, j, 0), s=s)) for s in w_sel]
    else:
        w_specs = [pl.BlockSpec((K, cn), lambda j: (0, j)) for _ in ws]
    chunk = pl.BlockSpec((T, cn), lambda j: (0, j))
    return _call(name, main, (N // cn,), [pl.BlockSpec((T, K), lambda j: (0, 0))] + w_specs + [chunk] * n_c,
                 [chunk] * len(out_dtypes), [jax.ShapeDtypeStruct((T, N), dt) for dt in out_dtypes],
                 (a, *ws, *comps), comm=comm, after=after)


def _mm_mstream(name, as_, ws, w_sel, w_form, extras, outs_desc, epilogue, tm=512, comm=None, after=None):
    T = as_[0].shape[0]
    tm = min(tm, T)
    n_a = len(as_)
    w_shapes = [w.shape[-2:] for w in ws]
    N = w_shapes[0][1] if w_form == "nn" else w_shapes[0][0]

    def main(ins, outs, scr):
        a_refs = ins[:n_a]
        w_refs = ins[n_a:2 * n_a]
        acc = None
        for a_ref, w_ref in zip(a_refs, w_refs):
            d = _dot(a_ref[...], w_ref[...], w_form)
            acc = d if acc is None else acc + d
        epilogue(acc, ins[2 * n_a:], outs)

    kind_spec = {"tile": pl.BlockSpec((tm, N), lambda i: (i, 0)), "vec": pl.BlockSpec((1, N), lambda i: (0, 0))}
    kind_shape = {"tile": (T, N), "vec": (1, N)}
    a_specs = [pl.BlockSpec((tm, a.shape[1]), lambda i: (i, 0)) for a in as_]
    w_specs = []
    for w, s in zip(ws, w_sel):
        if w.ndim == 3:
            w_specs.append(pl.BlockSpec((None,) + tuple(w.shape[1:]), functools.partial(lambda i, s: (s, 0, 0), s=s),
                                        pipeline_mode=pl.Buffered(1)))
        else:
            w_specs.append(pl.BlockSpec(tuple(w.shape), lambda i: (0, 0), pipeline_mode=pl.Buffered(1)))
    args = list(as_) + list(ws) + [e for e, _ in extras]
    return _call(name, main, (T // tm,), a_specs + w_specs + [kind_spec[k] for _, k in extras],
                 [kind_spec[k] for _, k in outs_desc],
                 [jax.ShapeDtypeStruct(kind_shape[k], dt) for dt, k in outs_desc], args, comm=comm, after=after)


def _residual_rms_epilogue(scale):
    def ep(acc, ex, outs):
        xv = ex[0][...] + scale * acc
        outs[0][...] = xv
        r = lax.rsqrt(jnp.mean(xv * xv, axis=-1, keepdims=True) + RMS_EPS)
        outs[1][...] = (xv * r * ex[1][...]).astype(outs[1].dtype)
    return ep


def _rms_bwd_epilogue(out_scale):
    def ep(acc, ex, outs):
        x_ref, g_ref, dres_ref = ex
        dx_ref, dxb_ref, dg_ref = outs
        xv = x_ref[...]
        r = lax.rsqrt(jnp.mean(xv * xv, axis=-1, keepdims=True) + RMS_EPS)
        xhat = xv * r

        @pl.when(pl.program_id(0) == 0)
        def _():
            dg_ref[...] = jnp.zeros_like(dg_ref)

        dg_ref[...] += jnp.sum(acc * xhat, axis=0, keepdims=True)
        dxhat = acc * g_ref[...]
        dx = r * (dxhat - xhat * jnp.mean(dxhat * xhat, axis=-1, keepdims=True)) + dres_ref[...]
        dx_ref[...] = dx
        dxb_ref[...] = (out_scale * dx).astype(dxb_ref.dtype)
    return ep


def _mm_tn(name, a, b, tmo, tno, out_dtype, tk=1024, comm=None, after=None):
    T, Ma = a.shape
    Nb = b.shape[1]
    tk = min(tk, T)
    nk = T // tk

    def main(ins, outs, scr):
        a_ref, b_ref = ins
        (acc_ref,) = scr
        k = pl.program_id(2)

        @pl.when(k == 0)
        def _():
            acc_ref[...] = jnp.zeros_like(acc_ref)

        acc_ref[...] += _dot(a_ref[...], b_ref[...], "tn")

        @pl.when(k == nk - 1)
        def _():
            outs[0][...] = acc_ref[...].astype(outs[0].dtype)

    (out,), extra = _call(
        name, main, (Ma // tmo, Nb // tno, nk),
        [pl.BlockSpec((tk, tmo), lambda i, j, k: (k, i)), pl.BlockSpec((tk, tno), lambda i, j, k: (k, j))],
        [pl.BlockSpec((tmo, tno), lambda i, j, k: (i, j))], [jax.ShapeDtypeStruct((Ma, Nb), out_dtype)],
        (a, b), scratch=[pltpu.VMEM((tmo, tno), F32)], comm=comm, after=after)
    return out, extra


def _swiglu_parts(g, u):
    s = _sigmoid(g)
    silu = g * s
    return [u * (s + silu * (1.0 - s)), silu, silu * u]


def _silu_mul_epilogue(dots, comps):
    g, u = dots
    return _swiglu_parts(g, u)


def _gate_parts_epilogue(dots, comps):
    (g,) = dots
    s = _sigmoid(g)
    silu = g * s
    return [s + silu * (1.0 - s), silu]


def _up_act_epilogue(dots, comps):
    (u,) = dots
    return [u * comps[0].astype(F32), u * comps[1].astype(F32)]


def _dact_epilogue(dots, comps):
    dact = dots[0].astype(BF16)
    return [dact * comps[0], dact * comps[1]]


def _identity_epilogue(dots, comps):
    return list(dots)


def _swap_halves(x):
    lane = lax.broadcasted_iota(jnp.int32, x.shape, 1)
    first = (lane % DK) < (DK // 2)
    return jnp.where(first, pltpu.roll(x, 128 - DK // 2, 1), pltpu.roll(x, DK // 2, 1))


def _rotary(t, cos, sin_signed):
    halves = []
    for p in range(QK_W // 128):
        th = t[:, 128 * p:128 * (p + 1)]
        halves.append(th * cos + _swap_halves(th) * sin_signed)
    return jnp.concatenate(halves, axis=1)


def _rotary_transposed(d, cos, sin_signed):
    halves = []
    for p in range(QK_W // 128):
        dh = d[:, 128 * p:128 * (p + 1)]
        halves.append(dh * cos + _swap_halves(dh * sin_signed))
    return jnp.concatenate(halves, axis=1)


def _log_sigmoid(x):
    return jnp.minimum(x, 0.0) - jnp.log(1.0 + jnp.exp(-jnp.abs(x)))


def _tri_sum(mask, x):
    tri = mask.astype(BF16)
    hi = x.astype(BF16)
    rest = x - hi.astype(F32)
    mid = rest.astype(BF16)
    lo = (rest - mid.astype(F32)).astype(BF16)
    return _dot(tri, hi, "nn") + _dot(tri, mid, "nn") + _dot(tri, lo, "nn")


def _attn_masks():
    row = lax.broadcasted_iota(jnp.int32, (SUPER, SUPER), 0)
    col = lax.broadcasted_iota(jnp.int32, (SUPER, SUPER), 1)
    same = (row // CHUNK) == (col // CHUNK)
    return row, col, same


def _group_inputs(grp, pr, cos, sin_signed, lg, wa2, ba):
    seg = lambda lo, width: pr[:, lo:lo + width].astype(F32)
    if grp == 0:
        q = _rotary(seg(C_RQ, QK_W), cos, sin_signed)
        k = _rotary(seg(C_RK, QK_W), cos, sin_signed) * (DK ** -0.5)
        v = pr[:, C_RV:C_RV + V_W]
        gate = seg(C_RG, V_W)
        pos = lax.broadcasted_iota(jnp.int32, (SUPER, QK_W), 0).astype(F32) + 1.0
        return q, k, v, gate, pos * lg, None, None
    q = seg(C_GQ, QK_W) * (DK ** -0.5)
    k = seg(C_GK, QK_W)
    v = pr[:, C_GV:C_GV + V_W]
    gate = seg(C_GG, V_W)
    glow = pr[:, C_GL:C_GL + GL_W]
    logit = _dot(glow.astype(BF16), wa2.astype(BF16), "nn") + ba
    la = _log_sigmoid(logit) * (1.0 / GATE_NORM)
    row, col, _ = _attn_masks()
    b_cum = _tri_sum(col <= row, la)
    return q, k, v, gate, b_cum, glow, logit


def _decay_factors(q, k, b_cum):
    c = b_cum[SUPER // 2 - 1:SUPER // 2, :]
    bl = b_cum[SUPER - 1:SUPER, :]
    e1 = jnp.exp(b_cum - c)
    e2 = jnp.exp(c - b_cum)
    e_b = jnp.exp(b_cum)
    e_l = jnp.exp(bl - b_cum)
    return dict(e1=e1, e2=e2, eb=e_b, el=e_l, ebl=jnp.exp(bl),
                qp=q * e1, qm=q * e2, kp=k * e1, km=k * e2, qs=q * e_b, kl=k * e_l)


def _state_block_mask():
    r = lax.broadcasted_iota(jnp.int32, (V_W, QK_W), 0)
    c = lax.broadcasted_iota(jnp.int32, (V_W, QK_W), 1)
    return (r // DV) == (c // DK)


def _attn_fwd(proj, cos, sin_signed, lg, wa2p, ba, gn_ret, gn_gla, x_res, w_out, g_next, comm=None):
    T = proj.shape[0]
    n_s = T // SUPER
    D = x_res.shape[1]

    def main(ins, outs, scr):
        pr_ref, cos_ref, sin_ref, lg_ref, wa2_ref, ba_ref, gr_ref, gg_ref, xres_ref, wout_ref, gnext_ref = ins
        o_ref, y_ref, st_ref, x_ref, h_ref = outs
        (s_ref,) = scr
        i = pl.program_id(0)

        @pl.when(i == 0)
        def _():
            s_ref[...] = jnp.zeros_like(s_ref)

        pr = pr_ref
        row, col, same = _attn_masks()
        m1 = col <= row
        m2 = jnp.logical_and(col > row, same)
        lane = lax.broadcasted_iota(jnp.int32, (1, QK_W), 1)
        blockmask = _state_block_mask()
        for grp in range(2):
            q, k, v, gate, b_cum, _, _ = _group_inputs(grp, pr, cos_ref[...], sin_ref[...], lg_ref[...],
                                                      wa2_ref[...], ba_ref[...])
            f = _decay_factors(q, k, b_cum)
            gn = gr_ref[...] if grp == 0 else gg_ref[...]
            s_prev = s_ref[grp]
            st_ref[0, grp] = s_prev
            o_inter = _dot(f["qs"].astype(BF16), s_prev.astype(BF16), "nt")
            kmb = f["km"].astype(BF16)
            kpb = f["kp"].astype(BF16)
            vb = v.astype(BF16)
            heads = [(lane // DK) == h for h in range(HEADS)]
            a1_all = _dot(jnp.concatenate([jnp.where(hm, f["qp"], 0.0).astype(BF16) for hm in heads], axis=0), kmb, "nt")
            a2_all = _dot(jnp.concatenate([jnp.where(hm, f["qm"], 0.0).astype(BF16) for hm in heads], axis=0), kpb, "nt")
            for h in range(HEADS):
                a1 = a1_all[h * SUPER:(h + 1) * SUPER]
                a2 = a2_all[h * SUPER:(h + 1) * SUPER]
                a = jnp.where(m1, a1, jnp.where(m2, a2, 0.0))
                lo = grp * V_W + h * DV
                o_h = _dot(a.astype(BF16), vb[:, h * DV:(h + 1) * DV], "nn") + o_inter[:, h * DV:(h + 1) * DV]
                o_ref[:, lo:lo + DV] = o_h
                r = lax.rsqrt(jnp.mean(o_h * o_h, axis=-1, keepdims=True) + RMS_EPS)
                gte = gate[:, h * DV:(h + 1) * DV]
                y = o_h * r * gn[:, h * DV:(h + 1) * DV] * (gte * _sigmoid(gte))
                y_ref[:, lo:lo + DV] = y.astype(y_ref.dtype)
            upd = _dot(vb, f["kl"].astype(BF16), "tn")
            s_ref[grp] = s_prev * f["ebl"] + jnp.where(blockmask, upd, 0.0)
        xv = xres_ref[...] + _dot(wout_ref[...], y_ref[...], "nt").T
        x_ref[...] = xv
        r = lax.rsqrt(jnp.mean(xv * xv, axis=-1, keepdims=True) + RMS_EPS)
        h_ref[...] = (xv * r * gnext_ref[...]).astype(h_ref.dtype)

    const = lambda shape: pl.BlockSpec(shape, lambda i: tuple(0 for _ in shape))
    rows = lambda w: pl.BlockSpec((SUPER, w), lambda i: (i, 0))
    return _call(
        "attn_fwd", main, (n_s,),
        [rows(PROJ_W), rows(128), rows(128),
         const((1, QK_W)), const((GL_W, QK_W)), const((1, QK_W)), const((1, V_W)), const((1, V_W)),
         rows(D), const((2 * V_W, D)), const((1, D))],
        [rows(2 * V_W), rows(2 * V_W), pl.BlockSpec((1, 2, V_W, QK_W), lambda i: (i, 0, 0, 0)), rows(D), rows(D)],
        [jax.ShapeDtypeStruct((T, 2 * V_W), F32), jax.ShapeDtypeStruct((T, 2 * V_W), BF16),
         jax.ShapeDtypeStruct((n_s, 2, V_W, QK_W), F32), jax.ShapeDtypeStruct((T, D), F32),
         jax.ShapeDtypeStruct((T, D), BF16)],
        (proj, cos, sin_signed, lg, wa2p, ba, gn_ret, gn_gla, x_res, w_out, g_next),
        scratch=[pltpu.VMEM((2, V_W, QK_W), F32)], comm=comm)


def _attn_bwd(proj, cos, sin_signed, lg, wa2p, ba, gn_ret, gn_gla, o, dx, w_out, states, comm=None, after=None):
    T = proj.shape[0]
    n_s = T // SUPER
    D = dx.shape[1]

    def main(ins, outs, scr):
        pr_ref, cos_ref, sin_ref, lg_ref, wa2_ref, ba_ref, gr_ref, gg_ref, o_ref, dx_ref, wout_ref, st_ref = ins
        dp_ref, dgr_ref, dgg_ref, dba_ref, dwa_ref = outs
        (ds_ref, dy_ref) = scr
        i = pl.program_id(0)
        dy_ref[...] = _dot(wout_ref[...], dx_ref[...], "nt").T

        @pl.when(i == 0)
        def _():
            ds_ref[...] = jnp.zeros_like(ds_ref)
            dgr_ref[...] = jnp.zeros_like(dgr_ref)
            dgg_ref[...] = jnp.zeros_like(dgg_ref)
            dba_ref[...] = jnp.zeros_like(dba_ref)
            dwa_ref[...] = jnp.zeros_like(dwa_ref)

        pr = pr_ref
        cos = cos_ref[...]
        sin_signed = sin_ref[...]
        row, col, same = _attn_masks()
        m1 = col <= row
        m2 = jnp.logical_and(col > row, same)
        m1t = row <= col
        m2t = jnp.logical_and(row > col, same)
        lane = lax.broadcasted_iota(jnp.int32, (1, QK_W), 1)
        blockmask = _state_block_mask()
        for grp in range(2):
            q, k, v, gate, b_cum, glow, logit = _group_inputs(grp, pr, cos, sin_signed, lg_ref[...],
                                                              wa2_ref[...], ba_ref[...])
            f = _decay_factors(q, k, b_cum)
            gn = gr_ref[...] if grp == 0 else gg_ref[...]
            dgn_ref = dgr_ref if grp == 0 else dgg_ref
            do_parts, dgate_parts, dgn_parts = [], [], []
            for h in range(HEADS):
                lo = grp * V_W + h * DV
                o_h = o_ref[:, lo:lo + DV]
                r = lax.rsqrt(jnp.mean(o_h * o_h, axis=-1, keepdims=True) + RMS_EPS)
                n = o_h * r
                gte = gate[:, h * DV:(h + 1) * DV]
                sg = _sigmoid(gte)
                dy_h = dy_ref[:, lo:lo + DV]
                gn_h = gn[:, h * DV:(h + 1) * DV]
                dgate_parts.append(dy_h * n * gn_h * (sg * (1.0 + gte * (1.0 - sg))))
                dz = dy_h * (gte * sg)
                dgn_parts.append(jnp.sum(dz * n, axis=0, keepdims=True))
                dn = dz * gn_h
                do_parts.append(r * (dn - n * jnp.mean(dn * n, axis=-1, keepdims=True)))
            dgn_ref[...] += jnp.concatenate(dgn_parts, axis=1)
            dgate = jnp.concatenate(dgate_parts, axis=1)
            do = jnp.concatenate(do_parts, axis=1)
            dob = do.astype(BF16)
            vb = v.astype(BF16)
            s_prev = st_ref[0, grp]
            ds_new = ds_ref[grp]
            dsb = ds_new.astype(BF16)
            qpb, qmb = f["qp"].astype(BF16), f["qm"].astype(BF16)
            kpb, kmb = f["kp"].astype(BF16), f["km"].astype(BF16)
            dv_parts = []
            heads = [(lane // DK) == h for h in range(HEADS)]
            qp_hs = [jnp.where(hm, f["qp"], 0.0).astype(BF16) for hm in heads]
            qm_hs = [jnp.where(hm, f["qm"], 0.0).astype(BF16) for hm in heads]
            kp_hs = [jnp.where(hm, f["kp"], 0.0).astype(BF16) for hm in heads]
            km_hs = [jnp.where(hm, f["km"], 0.0).astype(BF16) for hm in heads]
            km_stack, kp_stack = jnp.concatenate(km_hs, axis=0), jnp.concatenate(kp_hs, axis=0)
            at1_all = _dot(km_stack, qpb, "nt")
            at2_all = _dot(kp_stack, qmb, "nt")
            da1s, da2s, da1ts, da2ts = [], [], [], []
            for h in range(HEADS):
                at = jnp.where(m1t, at1_all[h * SUPER:(h + 1) * SUPER],
                               jnp.where(m2t, at2_all[h * SUPER:(h + 1) * SUPER], 0.0))
                do_h = dob[:, h * DV:(h + 1) * DV]
                v_h = vb[:, h * DV:(h + 1) * DV]
                dv_parts.append(_dot(at.astype(BF16), do_h, "nn"))
                da = _dot(do_h, v_h, "nt")
                dat = _dot(v_h, do_h, "nt")
                da1s.append(jnp.where(m1, da, 0.0).astype(BF16))
                da2s.append(jnp.where(m2, da, 0.0).astype(BF16))
                da1ts.append(jnp.where(m1t, dat, 0.0).astype(BF16))
                da2ts.append(jnp.where(m2t, dat, 0.0).astype(BF16))
            dqp = _dot(jnp.concatenate(da1s, axis=1), km_stack, "nn")
            dqm = _dot(jnp.concatenate(da2s, axis=1), kp_stack, "nn")
            dkm = _dot(jnp.concatenate(da1ts, axis=1), jnp.concatenate(qp_hs, axis=0), "nn")
            dkp = _dot(jnp.concatenate(da2ts, axis=1), jnp.concatenate(qm_hs, axis=0), "nn")
            klb = f["kl"].astype(BF16)
            qsb = f["qs"].astype(BF16)
            dqs = _dot(dob, s_prev.astype(BF16), "nn")
            dkl = _dot(vb, dsb, "nn")
            dv = jnp.concatenate(dv_parts, axis=1) + _dot(klb, dsb, "nt")
            ds_ref[grp] = ds_new * f["ebl"] + jnp.where(blockmask, _dot(dob, qsb, "tn"), 0.0)
            dq = dqp * f["e1"] + dqm * f["e2"] + dqs * f["eb"]
            dk = dkm * f["e2"] + dkp * f["e1"] + dkl * f["el"]
            if grp == 0:
                dq = _rotary_transposed(dq, cos, sin_signed)
                dk = _rotary_transposed(dk * (DK ** -0.5), cos, sin_signed)
                dp_ref[:, C_RQ:C_RQ + QK_W] = dq.astype(dp_ref.dtype)
                dp_ref[:, C_RK:C_RK + QK_W] = dk.astype(dp_ref.dtype)
                dp_ref[:, C_RV:C_RV + V_W] = dv.astype(dp_ref.dtype)
                dp_ref[:, C_RG:C_RG + V_W] = dgate.astype(dp_ref.dtype)
            else:
                dkl_kl = dkl * klb.astype(F32)
                db = (dqp * qpb.astype(F32) - dkm * kmb.astype(F32) - dqm * qmb.astype(F32)
                      + dkp * kpb.astype(F32) + dqs * qsb.astype(F32) - dkl_kl)
                last = (jnp.sum(dkl_kl, axis=0, keepdims=True)
                        + f["ebl"] * jnp.sum(s_prev * ds_new, axis=0, keepdims=True))
                rowq = lax.broadcasted_iota(jnp.int32, (SUPER, QK_W), 0)
                db = db + jnp.where(rowq == SUPER - 1, last, 0.0)
                dla = _tri_sum(col >= row, db)
                dlogit = dla * (1.0 / GATE_NORM) * (1.0 - _sigmoid(logit))
                dlb = dlogit.astype(BF16)
                dglow = _dot(dlb, wa2_ref[...].astype(BF16), "nt")
                dwa_ref[...] += _dot(glow.astype(BF16), dlb, "tn")
                dba_ref[...] += jnp.sum(dlogit, axis=0, keepdims=True)
                dp_ref[:, C_GQ:C_GQ + QK_W] = (dq * (DK ** -0.5)).astype(dp_ref.dtype)
                dp_ref[:, C_GK:C_GK + QK_W] = dk.astype(dp_ref.dtype)
                dp_ref[:, C_GV:C_GV + V_W] = dv.astype(dp_ref.dtype)
                dp_ref[:, C_GG:C_GG + V_W] = dgate.astype(dp_ref.dtype)
                dp_ref[:, C_GL:C_GL + GL_W] = dglow.astype(dp_ref.dtype)

    rev = lambda i: n_s - 1 - i
    const = lambda shape: pl.BlockSpec(shape, lambda i: tuple(0 for _ in shape))
    return _call(
        "attn_bwd", main, (n_s,),
        [pl.BlockSpec((SUPER, PROJ_W), lambda i: (rev(i), 0)),
         pl.BlockSpec((SUPER, 128), lambda i: (rev(i), 0)), pl.BlockSpec((SUPER, 128), lambda i: (rev(i), 0)),
         const((1, QK_W)), const((GL_W, QK_W)), const((1, QK_W)), const((1, V_W)), const((1, V_W)),
         pl.BlockSpec((SUPER, 2 * V_W), lambda i: (rev(i), 0)),
         pl.BlockSpec((SUPER, D), lambda i: (rev(i), 0)), const((2 * V_W, D)),
         pl.BlockSpec((1, 2, V_W, QK_W), lambda i: (rev(i), 0, 0, 0))],
        [pl.BlockSpec((SUPER, PROJ_W), lambda i: (rev(i), 0)),
         const((1, V_W)), const((1, V_W)), const((1, QK_W)), const((GL_W, QK_W))],
        [jax.ShapeDtypeStruct((T, PROJ_W), BF16),
         jax.ShapeDtypeStruct((1, V_W), F32), jax.ShapeDtypeStruct((1, V_W), F32),
         jax.ShapeDtypeStruct((1, QK_W), F32), jax.ShapeDtypeStruct((GL_W, QK_W), F32)],
        (proj, cos, sin_signed, lg, wa2p, ba, gn_ret, gn_gla, o, dx, w_out, states),
        scratch=[pltpu.VMEM((2, V_W, QK_W), F32), pltpu.VMEM((SUPER, 2 * V_W), F32)], comm=comm, after=after)


def _rotary_tables(T):
    half = DK // 2
    inv = ROPE_BASE ** (-jnp.arange(half, dtype=F32) * 2.0 / DK)
    ang = jnp.arange(T, dtype=F32)[:, None] * inv[None, :]
    cos, sin = jnp.cos(ang), jnp.sin(ang)
    cos_head = jnp.concatenate([cos, cos], axis=1)
    sin_head = jnp.concatenate([-sin, sin], axis=1)
    return jnp.tile(cos_head, (1, 128 // DK)), jnp.tile(sin_head, (1, 128 // DK))


def _sum_devices(name, gathered, m_per):
    def body(g_ref, o_ref):
        acc = g_ref[0:m_per, :]
        for k in range(1, N_DEV):
            acc = acc + g_ref[k * m_per:(k + 1) * m_per, :]
        o_ref[...] = acc

    return pl.pallas_call(body, name=name, out_shape=jax.ShapeDtypeStruct((m_per, 128), F32))(gathered)


def _owner_sums(name, items, owner, comm=None):
    counts = [1 + len(landed) for _, landed in items]

    def main(ins, outs, scr):
        at = 0
        for o_ref, n in zip(outs, counts):
            acc = ins[at][...].astype(F32)
            for l_ref in ins[at + 1:at + n]:
                for j in range(l_ref.shape[0]):
                    acc = acc + l_ref[j].astype(F32)
            o_ref[...] = acc
            at += n

    once = pl.Buffered(1)
    in_specs, out_specs, out_shape, args = [], [], [], []
    for grad, landed in items:
        R, C = grad.shape[-2:]
        in_specs.append(pl.BlockSpec((None, None, R, C), lambda i, s: (s[0], s[1], 0, 0), pipeline_mode=once))
        in_specs += [pl.BlockSpec(tuple(l.shape), lambda i, s: (0, 0, 0), pipeline_mode=once) for l in landed]
        out_specs.append(pl.BlockSpec((R, C), lambda i, s: (0, 0)))
        out_shape.append(jax.ShapeDtypeStruct((R, C), F32))
        args += [grad, *landed]
    return _call(name, main, (1,), in_specs, out_specs, out_shape, args, comm=comm, prefetch=owner)


def _adamw_group(name, items, n_blocks, comm=None):
    n = len(items)

    def main(ins, outs, scr):
        for p in range(n):
            g_ref, w_ref, m_ref, v_ref = ins[4 * p:4 * p + 4]
            d_ref, nm_ref, nv_ref = outs[3 * p:3 * p + 3]
            gv = g_ref[...]
            nm = ADAM_B1 * m_ref[...] + (1.0 - ADAM_B1) * gv
            nv = ADAM_B2 * v_ref[...] + (1.0 - ADAM_B2) * (gv * gv)
            m_hat = nm / (1.0 - ADAM_B1 ** ADAM_STEP)
            v_hat = nv / (1.0 - ADAM_B2 ** ADAM_STEP)
            d_ref[...] = -ADAM_LR * (m_hat / (jnp.sqrt(v_hat) + ADAM_EPS) + ADAM_WD * w_ref[...])
            nm_ref[...] = nm
            nv_ref[...] = nv

    in_specs, out_specs, out_shape, args = [], [], [], []
    for item in items:
        R, C = item[1].shape
        assert R % n_blocks == 0
        spec = pl.BlockSpec((R // n_blocks, C), lambda i: (i, 0))
        in_specs += [spec] * 4
        out_specs += [spec] * 3
        out_shape += [jax.ShapeDtypeStruct((R, C), F32)] * 3
        args += list(item)
    outs, extra = _call(name, main, (n_blocks,), in_specs, out_specs, out_shape, args, comm=comm)
    return [tuple(outs[3 * p:3 * p + 3]) for p in range(n)], extra


SMALL_ORDER = ("ffn1", "mix", "ffn2", "final", "ret", "gla", "b_a")


def kernel(x, ffn1_norm_g, ffn1_w_gate, ffn1_w_up, ffn1_w_down, mix_norm_g, w_in, ret_norm_g, gla_w_a2, gla_b_a, gla_norm_g, w_out, ffn2_norm_g, ffn2_w_gate, ffn2_w_up, ffn2_w_down, final_norm_g, loss_target, m_ffn1_norm_g, m_ffn1_w_gate, m_ffn1_w_up, m_ffn1_w_down, m_mix_norm_g, m_w_in, m_ret_norm_g, m_gla_w_a2, m_gla_b_a, m_gla_norm_g, m_w_out, m_ffn2_norm_g, m_ffn2_w_gate, m_ffn2_w_up, m_ffn2_w_down, m_final_norm_g, v_ffn1_norm_g, v_ffn1_w_gate, v_ffn1_w_up, v_ffn1_w_down, v_mix_norm_g, v_w_in, v_ret_norm_g, v_gla_w_a2, v_gla_b_a, v_gla_norm_g, v_w_out, v_ffn2_norm_g, v_ffn2_w_gate, v_ffn2_w_up, v_ffn2_w_down, v_final_norm_g):
    xi, yi, ci = _coords()
    dev = 4 * xi + 2 * yi + ci
    owner = jnp.stack([2 * xi + yi, ci]).astype(jnp.int32)

    x0, target = x[0], loss_target[0]
    T, D = x0.shape
    fb = ffn1_w_gate.shape[2]
    ib = w_in.shape[2]
    ab = gla_w_a2.shape[2]
    F = N_DEV * fb
    cos, sin_signed = _rotary_tables(T)
    lg = jnp.repeat(jnp.log(1.0 - 2.0 ** (-5.0 - jnp.arange(HEADS, dtype=F32))), DK)[None, :]
    g_final = final_norm_g.reshape(1, D)

    g1_loc = ffn1_w_gate[0].T[None].astype(BF16)
    u1_loc = ffn1_w_up[0].T[None].astype(BF16)
    d1_loc = ffn1_w_down.astype(BF16)
    g2_loc = ffn2_w_gate[0].T[None].astype(BF16)
    u2_loc = ffn2_w_up[0].T[None].astype(BF16)
    d2_loc = ffn2_w_down.astype(BF16)
    in_loc = w_in[0].T.astype(BF16)
    out_loc = w_out[0].astype(BF16)

    h1, (g1,) = _rms_fwd("ffn1_rms", x0, ffn1_norm_g, comm=_AllGather([g1_loc], ["stack"]))
    g1 = g1.reshape(1, F, D)
    (dsl1, sl1), (u1,) = _mm_nstream("ffn1_gate", h1, [g1], [0], "nt", [], [BF16, BF16], _gate_parts_epilogue, cn=256,
                                     comm=_AllGather([u1_loc], ["stack"]))
    u1 = u1.reshape(1, F, D)
    (dsu1, act1), (d1,) = _mm_nstream("ffn1_up", h1, [u1], [0], "nt", [dsl1, sl1], [BF16, BF16],
                                      _up_act_epilogue, cn=256, comm=_AllGather([d1_loc], ["stack"]))
    d1 = d1.reshape(1, F, D)
    f32_tile, bf16_tile, f32_vec = (F32, "tile"), (BF16, "tile"), (F32, "vec")
    (x1, h2), (in_all, a_all) = _mm_mstream(
        "ffn1_down", [act1], [d1], [0], "nn", [(x0, "tile"), (mix_norm_g, "vec")], [f32_tile, bf16_tile],
        _residual_rms_epilogue(0.5), comm=_AllGather([in_loc, gla_w_a2[0]], ["plain", "plain"]))
    w_in_t = jnp.pad(in_all.reshape(1, N_DEV * ib, D), ((0, 0), (0, PROJ_W - N_DEV * ib), (0, 0)))
    wa2 = jnp.transpose(a_all, (1, 0, 2)).reshape(GATE_RANK, N_DEV * ab)
    wa2p = jnp.pad(wa2, ((0, GL_W - GATE_RANK), (0, 0)))

    (proj,), (g2, out_all) = _mm_nstream("mix_proj", h2, [w_in_t], [0], "nt", [], [BF16], _identity_epilogue, cn=640,
                                         comm=_AllGather([g2_loc, out_loc], ["stack", "plain"]))
    w_out_full = out_all.reshape(D, D)
    (o, ymix, states, x2, h3), (u2,) = _attn_fwd(proj, cos, sin_signed, lg, wa2p, gla_b_a, ret_norm_g, gla_norm_g,
                                                 x1, w_out_full.T, ffn2_norm_g, comm=_AllGather([u2_loc], ["stack"]))
    g2, u2 = g2.reshape(1, F, D), u2.reshape(1, F, D)

    (dsu2, sl2, act2), (d2,) = _mm_nstream(
        "ffn2_up", h3, [g2, u2], [0, 0], "nt", [], [BF16, BF16, BF16], _silu_mul_epilogue, cn=256,
        comm=_AllGather([d2_loc], ["stack"]))
    d2 = d2.reshape(1, F, D)
    (dx3, dy3b, d_final, loss), _ = _mm_mstream(
        "ffn2_down", [act2], [d2], [0], "nn", [(x2, "tile"), (g_final, "vec"), (target, "tile")],
        [f32_tile, bf16_tile, f32_vec, f32_vec], _final_loss_epilogue(0.5, 0.5))

    sent = {}

    def send(**grads):
        started = _send_to_owners("send_" + "_".join(grads), list(grads.values()))
        sent.update(zip(grads, started))
        return started[0][4]

    dwd2, _ = _mm_tn("ffn2b_dwd", act2, dy3b, F // 2, D, BF16)
    (dgate2, dup2), _ = _mm_nstream("ffn2b_dact", dy3b, [d2], [0], "nt", [dsu2, sl2], [BF16, BF16],
                                    _dact_epilogue, cn=256)
    dwg2, _ = _mm_tn("ffn2b_dwg", dgate2, h3, F // 2, D, BF16)
    dwu2, _ = _mm_tn("ffn2b_dwu", dup2, h3, F // 2, D, BF16)
    tok = send(wd2=dwd2.reshape(4, 2, fb, D), wg2=dwg2.reshape(4, 2, fb, D), wu2=dwu2.reshape(4, 2, fb, D))
    rms_outs = [f32_tile, bf16_tile, f32_vec]
    (dx2, dx2b, d_g2), _ = _mm_mstream(
        "ffn2b_dh", [dgate2, dup2], [g2, u2], [0, 0], "nn", [(x2, "tile"), (ffn2_norm_g, "vec"), (dx3, "tile")],
        rms_outs, _rms_bwd_epilogue(1.0), after=tok)

    dwout, _ = _mm_tn("mixb_dwout", ymix, dx2b, D, D, BF16)
    (dproj, d_ret, d_gla, d_ba, d_wa2p), _ = _attn_bwd(
        proj, cos, sin_signed, lg, wa2p, gla_b_a, ret_norm_g, gla_norm_g, o, dx2b, w_out_full, states)
    dwin_t, _ = _mm_tn("mixb_dwin", dproj, h2, 640, D, BF16, tk=2048)
    tok = send(wout=dwout.reshape(4, 2, D // N_DEV, D), win=dwin_t[:N_DEV * ib].reshape(4, 2, ib, D))
    (dx1, dy1b, d_gmix), _ = _mm_mstream(
        "mixb_dh", [dproj], [w_in_t], [0], "nn", [(x1, "tile"), (mix_norm_g, "vec"), (dx2, "tile")],
        rms_outs, _rms_bwd_epilogue(0.5), after=tok)

    dwd1, _ = _mm_tn("ffn1b_dwd", act1, dy1b, F // 2, D, BF16)
    tok = send(wd1=dwd1.reshape(4, 2, fb, D))
    (dgate1, dup1), _ = _mm_nstream("ffn1b_dact", dy1b, [d1], [0], "nt", [dsu1, sl1], [BF16, BF16],
                                    _dact_epilogue, cn=256, after=tok)
    dwg1, _ = _mm_tn("ffn1b_dwg", dgate1, h1, F // 2, D, BF16)
    tok = send(wg1=dwg1.reshape(4, 2, fb, D))
    dwu1, _ = _mm_tn("ffn1b_dwu", dup1, h1, F // 2, D, BF16, after=tok)
    tok = send(wu1=dwu1.reshape(4, 2, fb, D))
    (dx0, _, d_g1), _ = _mm_mstream(
        "ffn1b_dh", [dgate1, dup1], [g1, u1], [0, 0], "nn", [(x0, "tile"), (ffn1_norm_g, "vec"), (dx1, "tile")],
        rms_outs, _rms_bwd_epilogue(1.0), after=tok)

    small = dict(ffn1=d_g1, mix=d_gmix, ffn2=d_g2, final=d_final, ret=d_ret, gla=d_gla, b_a=d_ba)
    flat = jnp.concatenate([small[k].reshape(-1) for k in SMALL_ORDER]
                           + [d_wa2p[:GATE_RANK].reshape(-1), loss[0, :128]])
    rows = -(-flat.shape[0] // 128)
    rows = -(-rows // 8) * 8
    packed = jnp.pad(flat, (0, rows * 128 - flat.shape[0])).reshape(rows, 128)

    transposed = ("ffn1_w_gate", "ffn1_w_up", "ffn2_w_gate", "ffn2_w_up", "w_in")

    def to_2d(nm, a):
        if nm in transposed:
            return a[0].T
        return a.reshape((1, a.shape[0]) if a.ndim == 1 else a.shape[-2:])

    def from_2d(nm, a):
        return a.T[None] if nm in transposed else a.reshape(params[nm][0].shape)

    def arrived(nm, after):
        grad, landed = _await_owners("await_" + nm, sent[nm], after)
        return grad, [landed]

    sums_a, (gathered,) = _owner_sums(
        "sum_a", [arrived(nm, dx0) for nm in ("wg2", "wu2", "wd2", "win", "wout")], owner,
        comm=_AllGather([packed], ["plain"]))
    params = dict(
        ffn2_w_gate=(ffn2_w_gate, m_ffn2_w_gate, v_ffn2_w_gate), ffn2_w_up=(ffn2_w_up, m_ffn2_w_up, v_ffn2_w_up),
        ffn2_w_down=(ffn2_w_down, m_ffn2_w_down, v_ffn2_w_down), w_in=(w_in, m_w_in, v_w_in),
        w_out=(w_out, m_w_out, v_w_out), ffn1_w_gate=(ffn1_w_gate, m_ffn1_w_gate, v_ffn1_w_gate),
        ffn1_w_up=(ffn1_w_up, m_ffn1_w_up, v_ffn1_w_up), ffn1_w_down=(ffn1_w_down, m_ffn1_w_down, v_ffn1_w_down),
        ffn1_norm_g=(ffn1_norm_g, m_ffn1_norm_g, v_ffn1_norm_g), mix_norm_g=(mix_norm_g, m_mix_norm_g, v_mix_norm_g),
        ret_norm_g=(ret_norm_g, m_ret_norm_g, v_ret_norm_g), gla_w_a2=(gla_w_a2, m_gla_w_a2, v_gla_w_a2),
        gla_b_a=(gla_b_a, m_gla_b_a, v_gla_b_a), gla_norm_g=(gla_norm_g, m_gla_norm_g, v_gla_norm_g),
        ffn2_norm_g=(ffn2_norm_g, m_ffn2_norm_g, v_ffn2_norm_g), final_norm_g=(final_norm_g, m_final_norm_g, v_final_norm_g))
    grads, updates = {}, {}

    def run_adam(name, names, grad_2d, n_blocks):
        items = [(grad_2d[nm],) + tuple(to_2d(nm, a) for a in params[nm]) for nm in names]
        res, _ = _adamw_group(name, items, n_blocks)
        for nm, r in zip(names, res):
            grads[nm] = from_2d(nm, grad_2d[nm])
            updates[nm] = tuple(from_2d(nm, a) for a in r)
        return res

    grads_a = {"ffn2_w_gate": sums_a[0], "ffn2_w_up": sums_a[1], "ffn2_w_down": sums_a[2], "w_out": sums_a[4]}
    run_adam("adamw_w_in", ["w_in"], {"w_in": sums_a[3]}, 1)
    done_a = run_adam("adamw_a", list(grads_a), grads_a, 4)[0][0]
    sums_b, _ = _owner_sums("sum_b", [arrived(nm, done_a) for nm in ("wg1", "wu1", "wd1")], owner)
    grads_b = {"ffn1_w_gate": sums_b[0], "ffn1_w_up": sums_b[1], "ffn1_w_down": sums_b[2]}
    run_adam("adamw_b", list(grads_b), grads_b, 4)

    total = _sum_devices("sum_small", gathered.reshape(N_DEV * rows, 128), rows).reshape(-1)
    sizes = [small[k].size for k in SMALL_ORDER] + [GATE_RANK * QK_W, 128]
    offs = [0]
    for s in sizes:
        offs.append(offs[-1] + s)
    pieces = [total[offs[i]:offs[i + 1]] for i in range(len(sizes))]
    g_small = {k: pieces[i].reshape(small[k].shape) for i, k in enumerate(SMALL_ORDER)}
    g_wa2_full = pieces[len(SMALL_ORDER)].reshape(GATE_RANK, QK_W)
    g_wa2 = lax.dynamic_slice(g_wa2_full, (0, dev * ab), (GATE_RANK, ab))
    loss_total = pieces[len(SMALL_ORDER) + 1][0]

    small_grads = {"ffn1_norm_g": g_small["ffn1"], "mix_norm_g": g_small["mix"], "ret_norm_g": g_small["ret"],
                   "gla_w_a2": g_wa2, "gla_b_a": g_small["b_a"], "gla_norm_g": g_small["gla"],
                   "ffn2_norm_g": g_small["ffn2"], "final_norm_g": g_small["final"]}
    run_adam("adamw_small", list(small_grads), small_grads, 1)

    order = ("ffn1_norm_g", "ffn1_w_gate", "ffn1_w_up", "ffn1_w_down", "mix_norm_g", "w_in", "ret_norm_g", "gla_w_a2",
             "gla_b_a", "gla_norm_g", "w_out", "ffn2_norm_g", "ffn2_w_gate", "ffn2_w_up", "ffn2_w_down", "final_norm_g")
    return (loss_total, dx0[None], *[grads[nm] for nm in order], *[updates[nm][0] for nm in order],
            *[updates[nm][1] for nm in order], *[updates[nm][2] for nm in order])
```

```python
import functools
import math

import jax
import jax.numpy as jnp
from jax import lax
from jax.experimental import pallas as pl
from jax.experimental.pallas import tpu as pltpu

F32 = jnp.float32
BF16 = jnp.bfloat16
MESH = pl.DeviceIdType.MESH
HBM = pl.BlockSpec(memory_space=pltpu.HBM)

N_DEV = 8
RMS_EPS = 1e-6
ROPE_BASE = 10000.0
HEADS = 4
DK = 64
DV = 128
QK_W = HEADS * DK
V_W = HEADS * DV
GATE_RANK = 16
GATE_NORM = 16.0
CHUNK = 64
SUPER = 256
PROJ_W = 3200
C_RQ, C_RK, C_RV, C_RG, C_GQ, C_GK, C_GV, C_GG, C_GL = 0, 256, 512, 1024, 1536, 1792, 2048, 2560, 3072
GL_W = PROJ_W - C_GL
ADAM_LR, ADAM_B1, ADAM_B2, ADAM_EPS, ADAM_WD, ADAM_STEP = 0.001, 0.9, 0.999, 1e-08, 0.01, 10
VMEM_LIMIT_V7X = 52 * 1024 * 1024


def _cparams(**kw):
    return pltpu.CompilerParams(vmem_limit_bytes=VMEM_LIMIT_V7X, **kw)


def _dot(a, b, form, precision=None):
    dims = {"nn": (((1,), (0,)), ((), ())), "nt": (((1,), (1,)), ((), ())), "tn": (((0,), (0,)), ((), ()))}[form]
    return lax.dot_general(a, b, dims, preferred_element_type=F32, precision=precision)


def _sigmoid(x):
    return 1.0 / (1.0 + jnp.exp(-x))


def _coords():
    return lax.axis_index("x"), lax.axis_index("y"), lax.axis_index("c")


class _NoComm:
    inputs, out_shapes, scratch = (), (), ()


class _AllGather:
    def __init__(self, arrays, kinds):
        self.inputs = tuple(arrays)
        self.kinds = tuple(kinds)
        n = len(arrays)
        self.out_shapes = tuple(
            jax.ShapeDtypeStruct((a.shape[0], N_DEV) + a.shape[1:] if k == "stack" else (N_DEV,) + a.shape, a.dtype)
            for a, k in zip(arrays, kinds))
        self.scratch = (pltpu.SemaphoreType.DMA((n, 7)), pltpu.SemaphoreType.DMA((n, 7)),
                        pltpu.SemaphoreType.DMA((n,)))

    def _ctx(self, srcs, outs, sems):
        send_sems, recv_sems, local_sems = sems
        x, y, c = _coords()
        me, sibling = (x, y, c), (x, y, 1 - c)
        chips = [(1 - x, y), (x, 1 - y), (1 - x, 1 - y)]

        def blk(m, dev):
            k = 4 * dev[0] + 2 * dev[1] + dev[2]
            return outs[m].at[:, k] if self.kinds[m] == "stack" else outs[m].at[k]

        def copy(m, s, block, to, src=None):
            return pltpu.make_async_remote_copy(
                src_ref=blk(m, block) if src is None else src, dst_ref=blk(m, block),
                send_sem=send_sems.at[m, s], recv_sem=recv_sems.at[m, s], device_id=to, device_id_type=MESH)

        def mine(m):
            return pltpu.make_async_copy(srcs[m], blk(m, me), local_sems.at[m])

        def first(m):
            return [copy(m, 0, me, sibling, src=srcs[m])] + [
                copy(m, 1 + j, me, (*chip, c), src=srcs[m]) for j, chip in enumerate(chips)]

        return me, sibling, chips, c, copy, mine, first

    def start(self, srcs, outs, sems):
        me, sibling, chips, c, copy, mine, first = self._ctx(srcs, outs, sems)
        for m in range(len(srcs)):
            mine(m).start()
            for cp in first(m):
                cp.start()

    def mid(self, srcs, outs, sems):
        me, sibling, chips, c, copy, mine, first = self._ctx(srcs, outs, sems)
        for j, chip in enumerate(chips):
            for m in range(len(srcs)):
                copy(m, 1 + j, (*chip, c), me).wait_recv()
                copy(m, 4 + j, (*chip, c), sibling).start()

    def finish(self, srcs, outs, sems):
        me, sibling, chips, c, copy, mine, first = self._ctx(srcs, outs, sems)
        for m in range(len(srcs)):
            copy(m, 0, sibling, me).wait_recv()
            for j, chip in enumerate(chips):
                copy(m, 4 + j, (*chip, 1 - c), me).wait_recv()
            for cp in first(m):
                cp.wait_send()
            for j, chip in enumerate(chips):
                copy(m, 4 + j, (*chip, c), sibling).wait_send()
            mine(m).wait()


RELATIONS = ((0, 0, 1), (1, 0, 0), (0, 1, 0), (1, 1, 0), (1, 0, 1), (0, 1, 1), (1, 1, 1))
SEM = pl.BlockSpec(memory_space=pltpu.SEMAPHORE)
SPLIT_PARAMS = dict(has_side_effects=pltpu.SideEffectType.DATAFLOW_SIDE_EFFECTING)


def _owner_copies(grad_ref, land_ref, send_sems, recv_sems):
    x, y, c = _coords()
    copies = []
    for s, (fx, fy, fc) in enumerate(RELATIONS):
        px = 1 - x if fx else x
        py = 1 - y if fy else y
        pc = 1 - c if fc else c
        copies.append(pltpu.make_async_remote_copy(
            src_ref=grad_ref.at[2 * px + py, pc], dst_ref=land_ref.at[s], send_sem=send_sems.at[s],
            recv_sem=recv_sems.at[s], device_id=(px, py, pc), device_id_type=MESH))
    return copies


def _send_to_owners(name, grads):
    n, k = len(RELATIONS), len(grads)
    land_shapes = [(n,) + g.shape[2:] for g in grads]

    def body(*refs):
        ins, outs = refs[:2 * k], refs[2 * k:]
        for j in range(k):
            for cp in _owner_copies(ins[2 * j], ins[2 * j + 1], outs[4 * j], outs[4 * j + 1]):
                cp.start()
        outs[-1][...] = jnp.zeros_like(outs[-1])

    out_shape, out_specs, args, aliases = [], [], [], {}
    for j, (g, land_shape) in enumerate(zip(grads, land_shapes)):
        out_shape += [pltpu.SemaphoreType.DMA((n,)), pltpu.SemaphoreType.DMA((n,)), pltpu.HBM(g.shape, g.dtype),
                      pltpu.HBM(land_shape, g.dtype)]
        out_specs += [SEM, SEM, HBM, HBM]
        args += [pltpu.with_memory_space_constraint(g, pltpu.HBM),
                 pltpu.with_memory_space_constraint(lax.empty(land_shape, g.dtype), pltpu.HBM)]
        aliases.update({2 * j: 4 * j + 2, 2 * j + 1: 4 * j + 3})
    res = pl.pallas_call(
        body, name=name, out_shape=out_shape + [jax.ShapeDtypeStruct((8, 128), F32)],
        in_specs=[HBM] * (2 * k), out_specs=out_specs + [pl.BlockSpec(memory_space=pltpu.VMEM)],
        input_output_aliases=aliases, compiler_params=pltpu.CompilerParams(**SPLIT_PARAMS),
    )(*args)
    return [tuple(res[4 * j:4 * j + 4]) + (res[-1],) for j in range(k)]


def _await_owners(name, started, after):
    send_sems, recv_sems, g_thru, land_thru, _ = started

    def body(g_ref, land_ref, send_sems, recv_sems, after_ref, g_out, land_out):
        for cp in _owner_copies(g_ref, land_ref, send_sems, recv_sems):
            cp.wait_send()
            cp.wait_recv()

    return pl.pallas_call(
        body, name=name, out_shape=(pltpu.HBM(g_thru.shape, g_thru.dtype), pltpu.HBM(land_thru.shape, land_thru.dtype)),
        in_specs=(HBM, HBM, SEM, SEM, pl.BlockSpec(memory_space=pl.ANY)), out_specs=(HBM, HBM),
        input_output_aliases={0: 0, 1: 1}, compiler_params=pltpu.CompilerParams(**SPLIT_PARAMS),
    )(g_thru, land_thru, send_sems, recv_sems, after)


def _call(name, main, grid, in_specs, out_specs, out_shape, args, scratch=(), comm=None, prefetch=None, after=None):
    comm = comm or _NoComm()
    n_main = len(in_specs)
    if after is not None:
        in_specs = list(in_specs) + [pl.BlockSpec(after.shape, lambda *_: (0,) * after.ndim)]
        args = tuple(args) + (after,)
    counts = [len(in_specs), len(comm.inputs), len(out_shape), len(comm.out_shapes), len(scratch), len(comm.scratch)]
    n_steps = math.prod(grid)
    hosted = bool(comm.inputs)

    def body(*refs):
        if prefetch is not None:
            refs = refs[1:]
        parts, at = [], 0
        for n in counts:
            parts.append(refs[at:at + n])
            at += n
        ins, c_in, outs, c_out, scr, c_scr = parts
        ins = ins[:n_main]
        step = pl.program_id(0)
        for d in range(1, len(grid)):
            step = step * grid[d] + pl.program_id(d)
        if hosted:
            @pl.when(step == 0)
            def _():
                comm.start(c_in, c_out, c_scr)
        main(ins, outs, scr)
        if hosted:
            @pl.when(step == max(n_steps - 2, 0))
            def _():
                comm.mid(c_in, c_out, c_scr)

            @pl.when(step == n_steps - 1)
            def _():
                comm.finish(c_in, c_out, c_scr)

    all_in = list(in_specs) + [HBM] * counts[1]
    all_out = list(out_specs) + [HBM] * counts[3]
    all_scratch = list(scratch) + list(comm.scratch)
    shapes = list(out_shape) + list(comm.out_shapes)
    if prefetch is None:
        res = pl.pallas_call(body, name=name, grid=grid, in_specs=all_in, out_specs=all_out, out_shape=shapes,
                             scratch_shapes=all_scratch, compiler_params=_cparams())(*args, *comm.inputs)
    else:
        res = pl.pallas_call(
            body, name=name, out_shape=shapes,
            grid_spec=pltpu.PrefetchScalarGridSpec(num_scalar_prefetch=1, grid=grid, in_specs=all_in,
                                                   out_specs=all_out, scratch_shapes=all_scratch),
            compiler_params=_cparams())(prefetch, *args, *comm.inputs)
    return res[:counts[2]], res[counts[2]:]


def _rms_fwd(name, x, g, comm=None):
    T, D = x.shape
    tm = min(T, 512)

    def main(ins, outs, scr):
        x_ref, g_ref = ins
        xv = x_ref[...]
        r = lax.rsqrt(jnp.mean(xv * xv, axis=-1, keepdims=True) + RMS_EPS)
        outs[0][...] = (xv * r * g_ref[...]).astype(outs[0].dtype)

    tile = pl.BlockSpec((tm, D), lambda i: (i, 0))
    (h,), extra = _call(name, main, (T // tm,), [tile, pl.BlockSpec((1, D), lambda i: (0, 0))], [tile],
                        [jax.ShapeDtypeStruct((T, D), BF16)], (x, g), comm=comm)
    return h, extra


def _final_loss_epilogue(scale, out_scale):
    def ep(acc, ex, outs):
        res_ref, g_ref, t_ref = ex
        dx_ref, dxb_ref, dg_ref, loss_ref = outs
        n = acc.shape[-1]
        xv = res_ref[...] + scale * acc
        r = lax.rsqrt(jnp.mean(xv * xv, axis=-1, keepdims=True) + RMS_EPS)
        xhat = xv * r
        err = xhat * g_ref[...] - t_ref[...]

        @pl.when(pl.program_id(0) == 0)
        def _():
            dg_ref[...] = jnp.zeros_like(dg_ref)
            loss_ref[...] = jnp.zeros_like(loss_ref)

        loss_ref[...] += jnp.broadcast_to(jnp.sum(err * err) * (0.5 / n), loss_ref.shape)
        dy = err * (1.0 / n)
        dg_ref[...] += jnp.sum(dy * xhat, axis=0, keepdims=True)
        dxhat = dy * g_ref[...]
        dx = r * (dxhat - xhat * jnp.mean(dxhat * xhat, axis=-1, keepdims=True))
        dx_ref[...] = dx
        dxb_ref[...] = (out_scale * dx).astype(dxb_ref.dtype)
    return ep


def _mm_nstream(name, a, ws, w_sel, w_form, comps, out_dtypes, epilogue, cn, rows=2048, comm=None, after=None):
    T, K = a.shape
    N = ws[0].shape[1]
    rows = min(rows, T)
    assert N % cn == 0 and T % rows == 0
    n_w, n_c = len(ws), len(comps)

    def main(ins, outs, scr):
        a_ref = ins[0]
        w_refs = ins[1:1 + n_w]
        c_refs = ins[1 + n_w:1 + n_w + n_c]

        for r in range(T // rows):
            sl = slice(r * rows, (r + 1) * rows)
            a_blk = a_ref[sl, :]
            dots = [_dot(a_blk, w_ref[...], w_form) for w_ref in w_refs]
            res = epilogue(dots, [c_ref[sl, :] for c_ref in c_refs])
            for o_ref, o in zip(outs, res):
                o_ref[sl, :] = o.astype(o_ref.dtype)

    if w_form == "nt":
        w_specs = [pl.BlockSpec((None, cn, K), functools.partial(lambda j, s: (s, j, 0), s=s)) for s in w_sel]
    else:
        w_specs = [pl.BlockSpec((K, cn), lambda j: (0, j)) for _ in ws]
    chunk = pl.BlockSpec((T, cn), lambda j: (0, j))
    return _call(name, main, (N // cn,), [pl.BlockSpec((T, K), lambda j: (0, 0))] + w_specs + [chunk] * n_c,
                 [chunk] * len(out_dtypes), [jax.ShapeDtypeStruct((T, N), dt) for dt in out_dtypes],
                 (a, *ws, *comps), comm=comm, after=after)


def _mm_mstream(name, as_, ws, w_sel, w_form, extras, outs_desc, epilogue, tm=512, comm=None, after=None):
    T = as_[0].shape[0]
    tm = min(tm, T)
    n_a = len(as_)
    w_shapes = [w.shape[-2:] for w in ws]
    N = w_shapes[0][1] if w_form == "nn" else w_shapes[0][0]

    def main(ins, outs, scr):
        a_refs = ins[:n_a]
        w_refs = ins[n_a:2 * n_a]
        acc = None
        for a_ref, w_ref in zip(a_refs, w_refs):
            d = _dot(a_ref[...], w_ref[...], w_form)
            acc = d if acc is None else acc + d
        epilogue(acc, ins[2 * n_a:], outs)

    kind_spec = {"tile": pl.BlockSpec((tm, N), lambda i: (i, 0)), "vec": pl.BlockSpec((1, N), lambda i: (0, 0))}
    kind_shape = {"tile": (T, N), "vec": (1, N)}
    a_specs = [pl.BlockSpec((tm, a.shape[1]), lambda i: (i, 0)) for a in as_]
    w_specs = []
    for w, s in zip(ws, w_sel):
        if w.ndim == 3:
            w_specs.append(pl.BlockSpec((None,) + tuple(w.shape[1:]), functools.partial(lambda i, s: (s, 0, 0), s=s),
                                        pipeline_mode=pl.Buffered(1)))
        else:
            w_specs.append(pl.BlockSpec(tuple(w.shape), lambda i: (0, 0), pipeline_mode=pl.Buffered(1)))
    args = list(as_) + list(ws) + [e for e, _ in extras]
    return _call(name, main, (T // tm,), a_specs + w_specs + [kind_spec[k] for _, k in extras],
                 [kind_spec[k] for _, k in outs_desc],
                 [jax.ShapeDtypeStruct(kind_shape[k], dt) for dt, k in outs_desc], args, comm=comm, after=after)


def _residual_rms_epilogue(scale):
    def ep(acc, ex, outs):
        xv = ex[0][...] + scale * acc
        outs[0][...] = xv
        r = lax.rsqrt(jnp.mean(xv * xv, axis=-1, keepdims=True) + RMS_EPS)
        outs[1][...] = (xv * r * ex[1][...]).astype(outs[1].dtype)
    return ep


def _rms_bwd_epilogue(out_scale):
    def ep(acc, ex, outs):
        x_ref, g_ref, dres_ref = ex
        dx_ref, dxb_ref, dg_ref = outs
        xv = x_ref[...]
        r = lax.rsqrt(jnp.mean(xv * xv, axis=-1, keepdims=True) + RMS_EPS)
        xhat = xv * r

        @pl.when(pl.program_id(0) == 0)
        def _():
            dg_ref[...] = jnp.zeros_like(dg_ref)

        dg_ref[...] += jnp.sum(acc * xhat, axis=0, keepdims=True)
        dxhat = acc * g_ref[...]
        dx = r * (dxhat - xhat * jnp.mean(dxhat * xhat, axis=-1, keepdims=True)) + dres_ref[...]
        dx_ref[...] = dx
        dxb_ref[...] = (out_scale * dx).astype(dxb_ref.dtype)
    return ep


def _mm_tn(name, a, b, tmo, tno, out_dtype, tk=1024, comm=None, after=None):
    T, Ma = a.shape
    Nb = b.shape[1]
    tk = min(tk, T)
    nk = T // tk

    def main(ins, outs, scr):
        a_ref, b_ref = ins
        (acc_ref,) = scr
        k = pl.program_id(2)

        @pl.when(k == 0)
        def _():
            acc_ref[...] = jnp.zeros_like(acc_ref)

        acc_ref[...] += _dot(a_ref[...], b_ref[...], "tn")

        @pl.when(k == nk - 1)
        def _():
            outs[0][...] = acc_ref[...].astype(outs[0].dtype)

    (out,), extra = _call(
        name, main, (Ma // tmo, Nb // tno, nk),
        [pl.BlockSpec((tk, tmo), lambda i, j, k: (k, i)), pl.BlockSpec((tk, tno), lambda i, j, k: (k, j))],
        [pl.BlockSpec((tmo, tno), lambda i, j, k: (i, j))], [jax.ShapeDtypeStruct((Ma, Nb), out_dtype)],
        (a, b), scratch=[pltpu.VMEM((tmo, tno), F32)], comm=comm, after=after)
    return out, extra


def _swiglu_parts(g, u):
    s = _sigmoid(g)
    silu = g * s
    return [u * (s + silu * (1.0 - s)), silu, silu * u]


def _silu_mul_epilogue(dots, comps):
    g, u = dots
    return _swiglu_parts(g, u)


def _gate_parts_epilogue(dots, comps):
    (g,) = dots
    s = _sigmoid(g)
    silu = g * s
    return [s + silu * (1.0 - s), silu]


def _up_act_epilogue(dots, comps):
    (u,) = dots
    return [u * comps[0].astype(F32), u * comps[1].astype(F32)]


def _dact_epilogue(dots, comps):
    dact = dots[0].astype(BF16)
    return [dact * comps[0], dact * comps[1]]


def _identity_epilogue(dots, comps):
    return list(dots)


def _swap_halves(x):
    lane = lax.broadcasted_iota(jnp.int32, x.shape, 1)
    first = (lane % DK) < (DK // 2)
    return jnp.where(first, pltpu.roll(x, 128 - DK // 2, 1), pltpu.roll(x, DK // 2, 1))


def _rotary(t, cos, sin_signed):
    halves = []
    for p in range(QK_W // 128):
        th = t[:, 128 * p:128 * (p + 1)]
        halves.append(th * cos + _swap_halves(th) * sin_signed)
    return jnp.concatenate(halves, axis=1)


def _rotary_transposed(d, cos, sin_signed):
    halves = []
    for p in range(QK_W // 128):
        dh = d[:, 128 * p:128 * (p + 1)]
        halves.append(dh * cos + _swap_halves(dh * sin_signed))
    return jnp.concatenate(halves, axis=1)


def _log_sigmoid(x):
    return jnp.minimum(x, 0.0) - jnp.log(1.0 + jnp.exp(-jnp.abs(x)))


def _tri_sum(mask, x):
    tri = mask.astype(BF16)
    hi = x.astype(BF16)
    rest = x - hi.astype(F32)
    mid = rest.astype(BF16)
    lo = (rest - mid.astype(F32)).astype(BF16)
    return _dot(tri, hi, "nn") + _dot(tri, mid, "nn") + _dot(tri, lo, "nn")


def _attn_masks():
    row = lax.broadcasted_iota(jnp.int32, (SUPER, SUPER), 0)
    col = lax.broadcasted_iota(jnp.int32, (SUPER, SUPER), 1)
    same = (row // CHUNK) == (col // CHUNK)
    return row, col, same


def _group_inputs(grp, pr, cos, sin_signed, lg, wa2, ba):
    seg = lambda lo, width: pr[:, lo:lo + width].astype(F32)
    if grp == 0:
        q = _rotary(seg(C_RQ, QK_W), cos, sin_signed)
        k = _rotary(seg(C_RK, QK_W), cos, sin_signed) * (DK ** -0.5)
        v = pr[:, C_RV:C_RV + V_W]
        gate = seg(C_RG, V_W)
        pos = lax.broadcasted_iota(jnp.int32, (SUPER, QK_W), 0).astype(F32) + 1.0
        return q, k, v, gate, pos * lg, None, None
    q = seg(C_GQ, QK_W) * (DK ** -0.5)
    k = seg(C_GK, QK_W)
    v = pr[:, C_GV:C_GV + V_W]
    gate = seg(C_GG, V_W)
    glow = pr[:, C_GL:C_GL + GL_W]
    logit = _dot(glow.astype(BF16), wa2.astype(BF16), "nn") + ba
    la = _log_sigmoid(logit) * (1.0 / GATE_NORM)
    row, col, _ = _attn_masks()
    b_cum = _tri_sum(col <= row, la)
    return q, k, v, gate, b_cum, glow, logit


def _decay_factors(q, k, b_cum):
    c = b_cum[SUPER // 2 - 1:SUPER // 2, :]
    bl = b_cum[SUPER - 1:SUPER, :]
    e1 = jnp.exp(b_cum - c)
    e2 = jnp.exp(c - b_cum)
    e_b = jnp.exp(b_cum)
    e_l = jnp.exp(bl - b_cum)
    return dict(e1=e1, e2=e2, eb=e_b, el=e_l, ebl=jnp.exp(bl),
                qp=q * e1, qm=q * e2, kp=k * e1, km=k * e2, qs=q * e_b, kl=k * e_l)


def _state_block_mask():
    r = lax.broadcasted_iota(jnp.int32, (V_W, QK_W), 0)
    c = lax.broadcasted_iota(jnp.int32, (V_W, QK_W), 1)
    return (r // DV) == (c // DK)


def _attn_fwd(proj, cos, sin_signed, lg, wa2p, ba, gn_ret, gn_gla, x_res, w_out, g_next, comm=None):
    T = proj.shape[0]
    n_s = T // SUPER
    D = x_res.shape[1]

    def main(ins, outs, scr):
        pr_ref, cos_ref, sin_ref, lg_ref, wa2_ref, ba_ref, gr_ref, gg_ref, xres_ref, wout_ref, gnext_ref = ins
        o_ref, y_ref, st_ref, x_ref, h_ref = outs
        (s_ref,) = scr
        i = pl.program_id(0)

        @pl.when(i == 0)
        def _():
            s_ref[...] = jnp.zeros_like(s_ref)

        pr = pr_ref
        row, col, same = _attn_masks()
        m1 = col <= row
        m2 = jnp.logical_and(col > row, same)
        lane = lax.broadcasted_iota(jnp.int32, (1, QK_W), 1)
        blockmask = _state_block_mask()
        for grp in range(2):
            q, k, v, gate, b_cum, _, _ = _group_inputs(grp, pr, cos_ref[...], sin_ref[...], lg_ref[...],
                                                      wa2_ref[...], ba_ref[...])
            f = _decay_factors(q, k, b_cum)
            gn = gr_ref[...] if grp == 0 else gg_ref[...]
            s_prev = s_ref[grp]
            st_ref[0, grp] = s_prev
            o_inter = _dot(f["qs"].astype(BF16), s_prev.astype(BF16), "nt")
            kmb = f["km"].astype(BF16)
            kpb = f["kp"].astype(BF16)
            vb = v.astype(BF16)
            heads = [(lane // DK) == h for h in range(HEADS)]
            a1_all = _dot(jnp.concatenate([jnp.where(hm, f["qp"], 0.0).astype(BF16) for hm in heads], axis=0), kmb, "nt")
            a2_all = _dot(jnp.concatenate([jnp.where(hm, f["qm"], 0.0).astype(BF16) for hm in heads], axis=0), kpb, "nt")
            for h in range(HEADS):
                a1 = a1_all[h * SUPER:(h + 1) * SUPER]
                a2 = a2_all[h * SUPER:(h + 1) * SUPER]
                a = jnp.where(m1, a1, jnp.where(m2, a2, 0.0))
                lo = grp * V_W + h * DV
                o_h = _dot(a.astype(BF16), vb[:, h * DV:(h + 1) * DV], "nn") + o_inter[:, h * DV:(h + 1) * DV]
                o_ref[:, lo:lo + DV] = o_h
                r = lax.rsqrt(jnp.mean(o_h * o_h, axis=-1, keepdims=True) + RMS_EPS)
                gte = gate[:, h * DV:(h + 1) * DV]
                y = o_h * r * gn[:, h * DV:(h + 1) * DV] * (gte * _sigmoid(gte))
                y_ref[:, lo:lo + DV] = y.astype(y_ref.dtype)
            upd = _dot(vb, f["kl"].astype(BF16), "tn")
            s_ref[grp] = s_prev * f["ebl"] + jnp.where(blockmask, upd, 0.0)
        xv = xres_ref[...] + _dot(y_ref[...], wout_ref[...], "nn")
        x_ref[...] = xv
        r = lax.rsqrt(jnp.mean(xv * xv, axis=-1, keepdims=True) + RMS_EPS)
        h_ref[...] = (xv * r * gnext_ref[...]).astype(h_ref.dtype)

    const = lambda shape: pl.BlockSpec(shape, lambda i: tuple(0 for _ in shape))
    rows = lambda w: pl.BlockSpec((SUPER, w), lambda i: (i, 0))
    return _call(
        "attn_fwd", main, (n_s,),
        [rows(PROJ_W), rows(128), rows(128),
         const((1, QK_W)), const((GL_W, QK_W)), const((1, QK_W)), const((1, V_W)), const((1, V_W)),
         rows(D), const((2 * V_W, D)), const((1, D))],
        [rows(2 * V_W), rows(2 * V_W), pl.BlockSpec((1, 2, V_W, QK_W), lambda i: (i, 0, 0, 0)), rows(D), rows(D)],
        [jax.ShapeDtypeStruct((T, 2 * V_W), F32), jax.ShapeDtypeStruct((T, 2 * V_W), BF16),
         jax.ShapeDtypeStruct((n_s, 2, V_W, QK_W), F32), jax.ShapeDtypeStruct((T, D), F32),
         jax.ShapeDtypeStruct((T, D), BF16)],
        (proj, cos, sin_signed, lg, wa2p, ba, gn_ret, gn_gla, x_res, w_out, g_next),
        scratch=[pltpu.VMEM((2, V_W, QK_W), F32)], comm=comm)


def _attn_bwd(proj, cos, sin_signed, lg, wa2p, ba, gn_ret, gn_gla, o, dx, w_out, states, comm=None, after=None):
    T = proj.shape[0]
    n_s = T // SUPER
    D = dx.shape[1]

    def main(ins, outs, scr):
        pr_ref, cos_ref, sin_ref, lg_ref, wa2_ref, ba_ref, gr_ref, gg_ref, o_ref, dx_ref, wout_ref, st_ref = ins
        dp_ref, dgr_ref, dgg_ref, dba_ref, dwa_ref = outs
        (ds_ref, dy_ref) = scr
        i = pl.program_id(0)
        dy_ref[...] = _dot(dx_ref[...], wout_ref[...], "nt")

        @pl.when(i == 0)
        def _():
            ds_ref[...] = jnp.zeros_like(ds_ref)
            dgr_ref[...] = jnp.zeros_like(dgr_ref)
            dgg_ref[...] = jnp.zeros_like(dgg_ref)
            dba_ref[...] = jnp.zeros_like(dba_ref)
            dwa_ref[...] = jnp.zeros_like(dwa_ref)

        pr = pr_ref
        cos = cos_ref[...]
        sin_signed = sin_ref[...]
        row, col, same = _attn_masks()
        m1 = col <= row
        m2 = jnp.logical_and(col > row, same)
        m1t = row <= col
        m2t = jnp.logical_and(row > col, same)
        lane = lax.broadcasted_iota(jnp.int32, (1, QK_W), 1)
        blockmask = _state_block_mask()
        for grp in range(2):
            q, k, v, gate, b_cum, glow, logit = _group_inputs(grp, pr, cos, sin_signed, lg_ref[...],
                                                              wa2_ref[...], ba_ref[...])
            f = _decay_factors(q, k, b_cum)
            gn = gr_ref[...] if grp == 0 else gg_ref[...]
            dgn_ref = dgr_ref if grp == 0 else dgg_ref
            do_parts, dgate_parts, dgn_parts = [], [], []
            for h in range(HEADS):
                lo = grp * V_W + h * DV
                o_h = o_ref[:, lo:lo + DV]
                r = lax.rsqrt(jnp.mean(o_h * o_h, axis=-1, keepdims=True) + RMS_EPS)
                n = o_h * r
                gte = gate[:, h * DV:(h + 1) * DV]
                sg = _sigmoid(gte)
                dy_h = dy_ref[:, lo:lo + DV]
                gn_h = gn[:, h * DV:(h + 1) * DV]
                dgate_parts.append(dy_h * n * gn_h * (sg * (1.0 + gte * (1.0 - sg))))
                dz = dy_h * (gte * sg)
                dgn_parts.append(jnp.sum(dz * n, axis=0, keepdims=True))
                dn = dz * gn_h
                do_parts.append(r * (dn - n * jnp.mean(dn * n, axis=-1, keepdims=True)))
            dgn_ref[...] += jnp.concatenate(dgn_parts, axis=1)
            dgate = jnp.concatenate(dgate_parts, axis=1)
            do = jnp.concatenate(do_parts, axis=1)
            dob = do.astype(BF16)
            vb = v.astype(BF16)
            s_prev = st_ref[0, grp]
            ds_new = ds_ref[grp]
            dsb = ds_new.astype(BF16)
            qpb, qmb = f["qp"].astype(BF16), f["qm"].astype(BF16)
            kpb, kmb = f["kp"].astype(BF16), f["km"].astype(BF16)
            dv_parts = []
            heads = [(lane // DK) == h for h in range(HEADS)]
            qp_hs = [jnp.where(hm, f["qp"], 0.0).astype(BF16) for hm in heads]
            qm_hs = [jnp.where(hm, f["qm"], 0.0).astype(BF16) for hm in heads]
            kp_hs = [jnp.where(hm, f["kp"], 0.0).astype(BF16) for hm in heads]
            km_hs = [jnp.where(hm, f["km"], 0.0).astype(BF16) for hm in heads]
            km_stack, kp_stack = jnp.concatenate(km_hs, axis=0), jnp.concatenate(kp_hs, axis=0)
            at1_all = _dot(km_stack, qpb, "nt")
            at2_all = _dot(kp_stack, qmb, "nt")
            da1s, da2s, da1ts, da2ts = [], [], [], []
            for h in range(HEADS):
                at = jnp.where(m1t, at1_all[h * SUPER:(h + 1) * SUPER],
                               jnp.where(m2t, at2_all[h * SUPER:(h + 1) * SUPER], 0.0))
                do_h = dob[:, h * DV:(h + 1) * DV]
                v_h = vb[:, h * DV:(h + 1) * DV]
                dv_parts.append(_dot(at.astype(BF16), do_h, "nn"))
                da = _dot(do_h, v_h, "nt")
                dat = _dot(v_h, do_h, "nt")
                da1s.append(jnp.where(m1, da, 0.0).astype(BF16))
                da2s.append(jnp.where(m2, da, 0.0).astype(BF16))
                da1ts.append(jnp.where(m1t, dat, 0.0).astype(BF16))
                da2ts.append(jnp.where(m2t, dat, 0.0).astype(BF16))
            dqp = _dot(jnp.concatenate(da1s, axis=1), km_stack, "nn")
            dqm = _dot(jnp.concatenate(da2s, axis=1), kp_stack, "nn")
            dkm = _dot(jnp.concatenate(da1ts, axis=1), jnp.concatenate(qp_hs, axis=0), "nn")
            dkp = _dot(jnp.concatenate(da2ts, axis=1), jnp.concatenate(qm_hs, axis=0), "nn")
            klb = f["kl"].astype(BF16)
            qsb = f["qs"].astype(BF16)
            dqs = _dot(dob, s_prev.astype(BF16), "nn")
            dkl = _dot(vb, dsb, "nn")
            dv = jnp.concatenate(dv_parts, axis=1) + _dot(klb, dsb, "nt")
            ds_ref[grp] = ds_new * f["ebl"] + jnp.where(blockmask, _dot(dob, qsb, "tn"), 0.0)
            dq = dqp * f["e1"] + dqm * f["e2"] + dqs * f["eb"]
            dk = dkm * f["e2"] + dkp * f["e1"] + dkl * f["el"]
            if grp == 0:
                dq = _rotary_transposed(dq, cos, sin_signed)
                dk = _rotary_transposed(dk * (DK ** -0.5), cos, sin_signed)
                dp_ref[:, C_RQ:C_RQ + QK_W] = dq.astype(dp_ref.dtype)
                dp_ref[:, C_RK:C_RK + QK_W] = dk.astype(dp_ref.dtype)
                dp_ref[:, C_RV:C_RV + V_W] = dv.astype(dp_ref.dtype)
                dp_ref[:, C_RG:C_RG + V_W] = dgate.astype(dp_ref.dtype)
            else:
                dkl_kl = dkl * klb.astype(F32)
                db = (dqp * qpb.astype(F32) - dkm * kmb.astype(F32) - dqm * qmb.astype(F32)
                      + dkp * kpb.astype(F32) + dqs * qsb.astype(F32) - dkl_kl)
                last = (jnp.sum(dkl_kl, axis=0, keepdims=True)
                        + f["ebl"] * jnp.sum(s_prev * ds_new, axis=0, keepdims=True))
                rowq = lax.broadcasted_iota(jnp.int32, (SUPER, QK_W), 0)
                db = db + jnp.where(rowq == SUPER - 1, last, 0.0)
                dla = _tri_sum(col >= row, db)
                dlogit = dla * (1.0 / GATE_NORM) * (1.0 - _sigmoid(logit))
                dlb = dlogit.astype(BF16)
                dglow = _dot(dlb, wa2_ref[...].astype(BF16), "nt")
                dwa_ref[...] += _dot(glow.astype(BF16), dlb, "tn")
                dba_ref[...] += jnp.sum(dlogit, axis=0, keepdims=True)
                dp_ref[:, C_GQ:C_GQ + QK_W] = (dq * (DK ** -0.5)).astype(dp_ref.dtype)
                dp_ref[:, C_GK:C_GK + QK_W] = dk.astype(dp_ref.dtype)
                dp_ref[:, C_GV:C_GV + V_W] = dv.astype(dp_ref.dtype)
                dp_ref[:, C_GG:C_GG + V_W] = dgate.astype(dp_ref.dtype)
                dp_ref[:, C_GL:C_GL + GL_W] = dglow.astype(dp_ref.dtype)

    rev = lambda i: n_s - 1 - i
    const = lambda shape: pl.BlockSpec(shape, lambda i: tuple(0 for _ in shape))
    return _call(
        "attn_bwd", main, (n_s,),
        [pl.BlockSpec((SUPER, PROJ_W), lambda i: (rev(i), 0)),
         pl.BlockSpec((SUPER, 128), lambda i: (rev(i), 0)), pl.BlockSpec((SUPER, 128), lambda i: (rev(i), 0)),
         const((1, QK_W)), const((GL_W, QK_W)), const((1, QK_W)), const((1, V_W)), const((1, V_W)),
         pl.BlockSpec((SUPER, 2 * V_W), lambda i: (rev(i), 0)),
         pl.BlockSpec((SUPER, D), lambda i: (rev(i), 0)), const((2 * V_W, D)),
         pl.BlockSpec((1, 2, V_W, QK_W), lambda i: (rev(i), 0, 0, 0))],
        [pl.BlockSpec((SUPER, PROJ_W), lambda i: (rev(i), 0)),
         const((1, V_W)), const((1, V_W)), const((1, QK_W)), const((GL_W, QK_W))],
        [jax.ShapeDtypeStruct((T, PROJ_W), BF16),
         jax.ShapeDtypeStruct((1, V_W), F32), jax.ShapeDtypeStruct((1, V_W), F32),
         jax.ShapeDtypeStruct((1, QK_W), F32), jax.ShapeDtypeStruct((GL_W, QK_W), F32)],
        (proj, cos, sin_signed, lg, wa2p, ba, gn_ret, gn_gla, o, dx, w_out, states),
        scratch=[pltpu.VMEM((2, V_W, QK_W), F32), pltpu.VMEM((SUPER, 2 * V_W), F32)], comm=comm, after=after)


def _rotary_tables(T):
    half = DK // 2
    inv = ROPE_BASE ** (-jnp.arange(half, dtype=F32) * 2.0 / DK)
    ang = jnp.arange(T, dtype=F32)[:, None] * inv[None, :]
    cos, sin = jnp.cos(ang), jnp.sin(ang)
    cos_head = jnp.concatenate([cos, cos], axis=1)
    sin_head = jnp.concatenate([-sin, sin], axis=1)
    return jnp.tile(cos_head, (1, 128 // DK)), jnp.tile(sin_head, (1, 128 // DK))


def _sum_devices(name, gathered, m_per):
    def body(g_ref, o_ref):
        acc = g_ref[0:m_per, :]
        for k in range(1, N_DEV):
            acc = acc + g_ref[k * m_per:(k + 1) * m_per, :]
        o_ref[...] = acc

    return pl.pallas_call(body, name=name, out_shape=jax.ShapeDtypeStruct((m_per, 128), F32))(gathered)


def _owner_sums(name, items, owner, comm=None):
    counts = [1 + len(landed) for _, landed in items]

    def main(ins, outs, scr):
        at = 0
        for o_ref, n in zip(outs, counts):
            acc = ins[at][...].astype(F32)
            for l_ref in ins[at + 1:at + n]:
                for j in range(l_ref.shape[0]):
                    acc = acc + l_ref[j].astype(F32)
            o_ref[...] = acc
            at += n

    once = pl.Buffered(1)
    in_specs, out_specs, out_shape, args = [], [], [], []
    for grad, landed in items:
        R, C = grad.shape[-2:]
        in_specs.append(pl.BlockSpec((None, None, R, C), lambda i, s: (s[0], s[1], 0, 0), pipeline_mode=once))
        in_specs += [pl.BlockSpec(tuple(l.shape), lambda i, s: (0, 0, 0), pipeline_mode=once) for l in landed]
        out_specs.append(pl.BlockSpec((R, C), lambda i, s: (0, 0)))
        out_shape.append(jax.ShapeDtypeStruct((R, C), F32))
        args += [grad, *landed]
    return _call(name, main, (1,), in_specs, out_specs, out_shape, args, comm=comm, prefetch=owner)


def _adamw_group(name, items, n_blocks, comm=None):
    n = len(items)

    def main(ins, outs, scr):
        for p in range(n):
            g_ref, w_ref, m_ref, v_ref = ins[4 * p:4 * p + 4]
            d_ref, nm_ref, nv_ref = outs[3 * p:3 * p + 3]
            gv = g_ref[...]
            nm = ADAM_B1 * m_ref[...] + (1.0 - ADAM_B1) * gv
            nv = ADAM_B2 * v_ref[...] + (1.0 - ADAM_B2) * (gv * gv)
            m_hat = nm / (1.0 - ADAM_B1 ** ADAM_STEP)
            v_hat = nv / (1.0 - ADAM_B2 ** ADAM_STEP)
            d_ref[...] = -ADAM_LR * (m_hat / (jnp.sqrt(v_hat) + ADAM_EPS) + ADAM_WD * w_ref[...])
            nm_ref[...] = nm
            nv_ref[...] = nv

    in_specs, out_specs, out_shape, args = [], [], [], []
    for item in items:
        R, C = item[1].shape
        assert R % n_blocks == 0
        spec = pl.BlockSpec((R // n_blocks, C), lambda i: (i, 0))
        in_specs += [spec] * 4
        out_specs += [spec] * 3
        out_shape += [jax.ShapeDtypeStruct((R, C), F32)] * 3
        args += list(item)
    outs, extra = _call(name, main, (n_blocks,), in_specs, out_specs, out_shape, args, comm=comm)
    return [tuple(outs[3 * p:3 * p + 3]) for p in range(n)], extra


SMALL_ORDER = ("ffn1", "mix", "ffn2", "final", "ret", "gla", "b_a")


def kernel(x, ffn1_norm_g, ffn1_w_gate, ffn1_w_up, ffn1_w_down, mix_norm_g, w_in, ret_norm_g, gla_w_a2, gla_b_a, gla_norm_g, w_out, ffn2_norm_g, ffn2_w_gate, ffn2_w_up, ffn2_w_down, final_norm_g, loss_target, m_ffn1_norm_g, m_ffn1_w_gate, m_ffn1_w_up, m_ffn1_w_down, m_mix_norm_g, m_w_in, m_ret_norm_g, m_gla_w_a2, m_gla_b_a, m_gla_norm_g, m_w_out, m_ffn2_norm_g, m_ffn2_w_gate, m_ffn2_w_up, m_ffn2_w_down, m_final_norm_g, v_ffn1_norm_g, v_ffn1_w_gate, v_ffn1_w_up, v_ffn1_w_down, v_mix_norm_g, v_w_in, v_ret_norm_g, v_gla_w_a2, v_gla_b_a, v_gla_norm_g, v_w_out, v_ffn2_norm_g, v_ffn2_w_gate, v_ffn2_w_up, v_ffn2_w_down, v_final_norm_g):
    xi, yi, ci = _coords()
    dev = 4 * xi + 2 * yi + ci
    owner = jnp.stack([2 * xi + yi, ci]).astype(jnp.int32)

    x0, target = x[0], loss_target[0]
    T, D = x0.shape
    fb = ffn1_w_gate.shape[2]
    ib = w_in.shape[2]
    ab = gla_w_a2.shape[2]
    F = N_DEV * fb
    cos, sin_signed = _rotary_tables(T)
    lg = jnp.repeat(jnp.log(1.0 - 2.0 ** (-5.0 - jnp.arange(HEADS, dtype=F32))), DK)[None, :]
    g_final = final_norm_g.reshape(1, D)

    g1_loc = ffn1_w_gate[0].T[None].astype(BF16)
    u1_loc = ffn1_w_up[0].T[None].astype(BF16)
    d1_loc = ffn1_w_down.astype(BF16)
    g2_loc = ffn2_w_gate[0].T[None].astype(BF16)
    u2_loc = ffn2_w_up[0].T[None].astype(BF16)
    d2_loc = ffn2_w_down.astype(BF16)
    in_loc = w_in[0].T.astype(BF16)
    out_loc = w_out[0].astype(BF16)

    h1, (g1,) = _rms_fwd("ffn1_rms", x0, ffn1_norm_g, comm=_AllGather([g1_loc], ["stack"]))
    g1 = g1.reshape(1, F, D)
    (dsl1, sl1), (u1,) = _mm_nstream("ffn1_gate", h1, [g1], [0], "nt", [], [BF16, BF16], _gate_parts_epilogue, cn=256,
                                     comm=_AllGather([u1_loc], ["stack"]))
    u1 = u1.reshape(1, F, D)
    (dsu1, act1), (d1,) = _mm_nstream("ffn1_up", h1, [u1], [0], "nt", [dsl1, sl1], [BF16, BF16],
                                      _up_act_epilogue, cn=256, comm=_AllGather([d1_loc], ["stack"]))
    d1 = d1.reshape(1, F, D)
    f32_tile, bf16_tile, f32_vec = (F32, "tile"), (BF16, "tile"), (F32, "vec")
    (x1, h2), (in_all, a_all) = _mm_mstream(
        "ffn1_down", [act1], [d1], [0], "nn", [(x0, "tile"), (mix_norm_g, "vec")], [f32_tile, bf16_tile],
        _residual_rms_epilogue(0.5), comm=_AllGather([in_loc, gla_w_a2[0]], ["plain", "plain"]))
    w_in_t = jnp.pad(in_all.reshape(1, N_DEV * ib, D), ((0, 0), (0, PROJ_W - N_DEV * ib), (0, 0)))
    wa2 = jnp.transpose(a_all, (1, 0, 2)).reshape(GATE_RANK, N_DEV * ab)
    wa2p = jnp.pad(wa2, ((0, GL_W - GATE_RANK), (0, 0)))

    (proj,), (g2, out_all) = _mm_nstream("mix_proj", h2, [w_in_t], [0], "nt", [], [BF16], _identity_epilogue, cn=640,
                                         comm=_AllGather([g2_loc, out_loc], ["stack", "plain"]))
    w_out_full = out_all.reshape(D, D)
    (o, ymix, states, x2, h3), (u2,) = _attn_fwd(proj, cos, sin_signed, lg, wa2p, gla_b_a, ret_norm_g, gla_norm_g,
                                                 x1, w_out_full, ffn2_norm_g, comm=_AllGather([u2_loc], ["stack"]))
    g2, u2 = g2.reshape(1, F, D), u2.reshape(1, F, D)

    (dsu2, sl2, act2), (d2,) = _mm_nstream(
        "ffn2_up", h3, [g2, u2], [0, 0], "nt", [], [BF16, BF16, BF16], _silu_mul_epilogue, cn=256,
        comm=_AllGather([d2_loc], ["stack"]))
    d2 = d2.reshape(1, F, D)
    (dx3, dy3b, d_final, loss), _ = _mm_mstream(
        "ffn2_down", [act2], [d2], [0], "nn", [(x2, "tile"), (g_final, "vec"), (target, "tile")],
        [f32_tile, bf16_tile, f32_vec, f32_vec], _final_loss_epilogue(0.5, 0.5))

    sent = {}

    def send(**grads):
        started = _send_to_owners("send_" + "_".join(grads), list(grads.values()))
        sent.update(zip(grads, started))
        return started[0][4]

    dwd2, _ = _mm_tn("ffn2b_dwd", act2, dy3b, F // 2, D, BF16)
    (dgate2, dup2), _ = _mm_nstream("ffn2b_dact", dy3b, [d2], [0], "nt", [dsu2, sl2], [BF16, BF16],
                                    _dact_epilogue, cn=256)
    dwg2, _ = _mm_tn("ffn2b_dwg", dgate2, h3, F // 2, D, BF16)
    dwu2, _ = _mm_tn("ffn2b_dwu", dup2, h3, F // 2, D, BF16)
    tok = send(wd2=dwd2.reshape(4, 2, fb, D), wg2=dwg2.reshape(4, 2, fb, D), wu2=dwu2.reshape(4, 2, fb, D))
    rms_outs = [f32_tile, bf16_tile, f32_vec]
    (dx2, dx2b, d_g2), _ = _mm_mstream(
        "ffn2b_dh", [dgate2, dup2], [g2, u2], [0, 0], "nn", [(x2, "tile"), (ffn2_norm_g, "vec"), (dx3, "tile")],
        rms_outs, _rms_bwd_epilogue(1.0), after=tok)

    dwout, _ = _mm_tn("mixb_dwout", ymix, dx2b, D, D, BF16)
    (dproj, d_ret, d_gla, d_ba, d_wa2p), _ = _attn_bwd(
        proj, cos, sin_signed, lg, wa2p, gla_b_a, ret_norm_g, gla_norm_g, o, dx2b, w_out_full, states)
    dwin_t, _ = _mm_tn("mixb_dwin", dproj, h2, 640, D, BF16, tk=2048)
    tok = send(wout=dwout.reshape(4, 2, D // N_DEV, D), win=dwin_t[:N_DEV * ib].reshape(4, 2, ib, D))
    (dx1, dy1b, d_gmix), _ = _mm_mstream(
        "mixb_dh", [dproj], [w_in_t], [0], "nn", [(x1, "tile"), (mix_norm_g, "vec"), (dx2, "tile")],
        rms_outs, _rms_bwd_epilogue(0.5), after=tok)

    dwd1, _ = _mm_tn("ffn1b_dwd", act1, dy1b, F // 2, D, BF16)
    tok = send(wd1=dwd1.reshape(4, 2, fb, D))
    (dgate1, dup1), _ = _mm_nstream("ffn1b_dact", dy1b, [d1], [0], "nt", [dsu1, sl1], [BF16, BF16],
                                    _dact_epilogue, cn=256, after=tok)
    dwg1, _ = _mm_tn("ffn1b_dwg", dgate1, h1, F // 2, D, BF16)
    tok = send(wg1=dwg1.reshape(4, 2, fb, D))
    dwu1, _ = _mm_tn("ffn1b_dwu", dup1, h1, F // 2, D, BF16, after=tok)
    tok = send(wu1=dwu1.reshape(4, 2, fb, D))
    (dx0, _, d_g1), _ = _mm_mstream(
        "ffn1b_dh", [dgate1, dup1], [g1, u1], [0, 0], "nn", [(x0, "tile"), (ffn1_norm_g, "vec"), (dx1, "tile")],
        rms_outs, _rms_bwd_epilogue(1.0), after=tok)

    small = dict(ffn1=d_g1, mix=d_gmix, ffn2=d_g2, final=d_final, ret=d_ret, gla=d_gla, b_a=d_ba)
    flat = jnp.concatenate([small[k].reshape(-1) for k in SMALL_ORDER]
                           + [d_wa2p[:GATE_RANK].reshape(-1), loss[0, :128]])
    rows = -(-flat.shape[0] // 128)
    rows = -(-rows // 8) * 8
    packed = jnp.pad(flat, (0, rows * 128 - flat.shape[0])).reshape(rows, 128)

    transposed = ("ffn1_w_gate", "ffn1_w_up", "ffn2_w_gate", "ffn2_w_up", "w_in")

    def to_2d(nm, a):
        if nm in transposed:
            return a[0].T
        return a.reshape((1, a.shape[0]) if a.ndim == 1 else a.shape[-2:])

    def from_2d(nm, a):
        return a.T[None] if nm in transposed else a.reshape(params[nm][0].shape)

    def arrived(nm, after):
        grad, landed = _await_owners("await_" + nm, sent[nm], after)
        return grad, [landed]

    sums_a, (gathered,) = _owner_sums(
        "sum_a", [arrived(nm, dx0) for nm in ("wg2", "wu2", "wd2", "win", "wout")], owner,
        comm=_AllGather([packed], ["plain"]))
    params = dict(
        ffn2_w_gate=(ffn2_w_gate, m_ffn2_w_gate, v_ffn2_w_gate), ffn2_w_up=(ffn2_w_up, m_ffn2_w_up, v_ffn2_w_up),
        ffn2_w_down=(ffn2_w_down, m_ffn2_w_down, v_ffn2_w_down), w_in=(w_in, m_w_in, v_w_in),
        w_out=(w_out, m_w_out, v_w_out), ffn1_w_gate=(ffn1_w_gate, m_ffn1_w_gate, v_ffn1_w_gate),
        ffn1_w_up=(ffn1_w_up, m_ffn1_w_up, v_ffn1_w_up), ffn1_w_down=(ffn1_w_down, m_ffn1_w_down, v_ffn1_w_down),
        ffn1_norm_g=(ffn1_norm_g, m_ffn1_norm_g, v_ffn1_norm_g), mix_norm_g=(mix_norm_g, m_mix_norm_g, v_mix_norm_g),
        ret_norm_g=(ret_norm_g, m_ret_norm_g, v_ret_norm_g), gla_w_a2=(gla_w_a2, m_gla_w_a2, v_gla_w_a2),
        gla_b_a=(gla_b_a, m_gla_b_a, v_gla_b_a), gla_norm_g=(gla_norm_g, m_gla_norm_g, v_gla_norm_g),
        ffn2_norm_g=(ffn2_norm_g, m_ffn2_norm_g, v_ffn2_norm_g), final_norm_g=(final_norm_g, m_final_norm_g, v_final_norm_g))
    grads, updates = {}, {}

    def run_adam(name, names, grad_2d, n_blocks):
        items = [(grad_2d[nm],) + tuple(to_2d(nm, a) for a in params[nm]) for nm in names]
        res, _ = _adamw_group(name, items, n_blocks)
        for nm, r in zip(names, res):
            grads[nm] = from_2d(nm, grad_2d[nm])
            updates[nm] = tuple(from_2d(nm, a) for a in r)
        return res

    grads_a = {"ffn2_w_gate": sums_a[0], "ffn2_w_up": sums_a[1], "ffn2_w_down": sums_a[2], "w_out": sums_a[4]}
    run_adam("adamw_w_in", ["w_in"], {"w_in": sums_a[3]}, 1)
    done_a = run_adam("adamw_a", list(grads_a), grads_a, 4)[0][0]
    sums_b, _ = _owner_sums("sum_b", [arrived(nm, done_a) for nm in ("wg1", "wu1", "wd1")], owner)
    grads_b = {"ffn1_w_gate": sums_b[0], "ffn1_w_up": sums_b[1], "ffn1_w_down": sums_b[2]}
    run_adam("adamw_b", list(grads_b), grads_b, 4)

    total = _sum_devices("sum_small", gathered.reshape(N_DEV * rows, 128), rows).reshape(-1)
    sizes = [small[k].size for k in SMALL_ORDER] + [GATE_RANK * QK_W, 128]
    offs = [0]
    for s in sizes:
        offs.append(offs[-1] + s)
    pieces = [total[offs[i]:offs[i + 1]] for i in range(len(sizes))]
    g_small = {k: pieces[i].reshape(small[k].shape) for i, k in enumerate(SMALL_ORDER)}
    g_wa2_full = pieces[len(SMALL_ORDER)].reshape(GATE_RANK, QK_W)
    g_wa2 = lax.dynamic_slice(g_wa2_full, (0, dev * ab), (GATE_RANK, ab))
    loss_total = pieces[len(SMALL_ORDER) + 1][0]

    small_grads = {"ffn1_norm_g": g_small["ffn1"], "mix_norm_g": g_small["mix"], "ret_norm_g": g_small["ret"],
                   "gla_w_a2": g_wa2, "gla_b_a": g_small["b_a"], "gla_norm_g": g_small["gla"],
                   "ffn2_norm_g": g_small["ffn2"], "final_norm_g": g_small["final"]}
    run_adam("adamw_small", list(small_grads), small_grads, 1)

    order = ("ffn1_norm_g", "ffn1_w_gate", "ffn1_w_up", "ffn1_w_down", "mix_norm_g", "w_in", "ret_norm_g", "gla_w_a2",
             "gla_b_a", "gla_norm_g", "w_out", "ffn2_norm_g", "ffn2_w_gate", "ffn2_w_up", "ffn2_w_down", "final_norm_g")
    return (loss_total, dx0[None], *[grads[nm] for nm in order], *[updates[nm][0] for nm in order],
            *[updates[nm][1] for nm in order], *[updates[nm][2] for nm in order])
```

```python
import functools
import math

import jax
import jax.numpy as jnp
from jax import lax
from jax.experimental import pallas as pl
from jax.experimental.pallas import tpu as pltpu

F32 = jnp.float32
BF16 = jnp.bfloat16
MESH = pl.DeviceIdType.MESH
HBM = pl.BlockSpec(memory_space=pltpu.HBM)

N_DEV = 8
RMS_EPS = 1e-6
ROPE_BASE = 10000.0
HEADS = 4
DK = 64
DV = 128
QK_W = HEADS * DK
V_W = HEADS * DV
GATE_RANK = 16
GATE_NORM = 16.0
CHUNK = 64
SUPER = 256
PROJ_W = 3200
C_RQ, C_RK, C_RV, C_RG, C_GQ, C_GK, C_GV, C_GG, C_GL = 0, 256, 512, 1024, 1536, 1792, 2048, 2560, 3072
GL_W = PROJ_W - C_GL
ADAM_LR, ADAM_B1, ADAM_B2, ADAM_EPS, ADAM_WD, ADAM_STEP = 0.001, 0.9, 0.999, 1e-08, 0.01, 10
VMEM_LIMIT_V7X = 52 * 1024 * 1024


def _cparams(**kw):
    return pltpu.CompilerParams(vmem_limit_bytes=VMEM_LIMIT_V7X, **kw)


def _dot(a, b, form, precision=None):
    dims = {"nn": (((1,), (0,)), ((), ())), "nt": (((1,), (1,)), ((), ())), "tn": (((0,), (0,)), ((), ()))}[form]
    return lax.dot_general(a, b, dims, preferred_element_type=F32, precision=precision)


def _sigmoid(x):
    return 1.0 / (1.0 + jnp.exp(-x))


def _coords():
    return lax.axis_index("x"), lax.axis_index("y"), lax.axis_index("c")


class _NoComm:
    inputs, out_shapes, scratch = (), (), ()


class _AllGather:
    def __init__(self, arrays, kinds):
        self.inputs = tuple(arrays)
        self.kinds = tuple(kinds)
        n = len(arrays)
        self.out_shapes = tuple(
            jax.ShapeDtypeStruct((a.shape[0], N_DEV) + a.shape[1:] if k == "stack" else (N_DEV,) + a.shape, a.dtype)
            for a, k in zip(arrays, kinds))
        self.scratch = (pltpu.SemaphoreType.DMA((n, 7)), pltpu.SemaphoreType.DMA((n, 7)),
                        pltpu.SemaphoreType.DMA((n,)))

    def _ctx(self, srcs, outs, sems):
        send_sems, recv_sems, local_sems = sems
        x, y, c = _coords()
        me, sibling = (x, y, c), (x, y, 1 - c)
        chips = [(1 - x, y), (x, 1 - y), (1 - x, 1 - y)]

        def blk(m, dev):
            k = 4 * dev[0] + 2 * dev[1] + dev[2]
            return outs[m].at[:, k] if self.kinds[m] == "stack" else outs[m].at[k]

        def copy(m, s, block, to, src=None):
            return pltpu.make_async_remote_copy(
                src_ref=blk(m, block) if src is None else src, dst_ref=blk(m, block),
                send_sem=send_sems.at[m, s], recv_sem=recv_sems.at[m, s], device_id=to, device_id_type=MESH)

        def mine(m):
            return pltpu.make_async_copy(srcs[m], blk(m, me), local_sems.at[m])

        def first(m):
            return [copy(m, 0, me, sibling, src=srcs[m])] + [
                copy(m, 1 + j, me, (*chip, c), src=srcs[m]) for j, chip in enumerate(chips)]

        return me, sibling, chips, c, copy, mine, first

    def start(self, srcs, outs, sems):
        me, sibling, chips, c, copy, mine, first = self._ctx(srcs, outs, sems)
        for m in range(len(srcs)):
            mine(m).start()
            for cp in first(m):
                cp.start()

    def mid(self, srcs, outs, sems):
        me, sibling, chips, c, copy, mine, first = self._ctx(srcs, outs, sems)
        for j, chip in enumerate(chips):
            for m in range(len(srcs)):
                copy(m, 1 + j, (*chip, c), me).wait_recv()
                copy(m, 4 + j, (*chip, c), sibling).start()

    def finish(self, srcs, outs, sems):
        me, sibling, chips, c, copy, mine, first = self._ctx(srcs, outs, sems)
        for m in range(len(srcs)):
            copy(m, 0, sibling, me).wait_recv()
            for j, chip in enumerate(chips):
                copy(m, 4 + j, (*chip, 1 - c), me).wait_recv()
            for cp in first(m):
                cp.wait_send()
            for j, chip in enumerate(chips):
                copy(m, 4 + j, (*chip, c), sibling).wait_send()
            mine(m).wait()


RELATIONS = ((0, 0, 1), (1, 0, 0), (0, 1, 0), (1, 1, 0), (1, 0, 1), (0, 1, 1), (1, 1, 1))
SEM = pl.BlockSpec(memory_space=pltpu.SEMAPHORE)
SPLIT_PARAMS = dict(has_side_effects=pltpu.SideEffectType.DATAFLOW_SIDE_EFFECTING)


def _owner_copies(grad_ref, land_ref, send_sems, recv_sems):
    x, y, c = _coords()
    copies = []
    for s, (fx, fy, fc) in enumerate(RELATIONS):
        px = 1 - x if fx else x
        py = 1 - y if fy else y
        pc = 1 - c if fc else c
        copies.append(pltpu.make_async_remote_copy(
            src_ref=grad_ref.at[2 * px + py, pc], dst_ref=land_ref.at[s], send_sem=send_sems.at[s],
            recv_sem=recv_sems.at[s], device_id=(px, py, pc), device_id_type=MESH))
    return copies


def _send_to_owners(name, grads):
    n, k = len(RELATIONS), len(grads)
    land_shapes = [(n,) + g.shape[2:] for g in grads]

    def body(*refs):
        ins, outs = refs[:2 * k], refs[2 * k:]
        for j in range(k):
            for cp in _owner_copies(ins[2 * j], ins[2 * j + 1], outs[4 * j], outs[4 * j + 1]):
                cp.start()
        outs[-1][...] = jnp.zeros_like(outs[-1])

    out_shape, out_specs, args, aliases = [], [], [], {}
    for j, (g, land_shape) in enumerate(zip(grads, land_shapes)):
        out_shape += [pltpu.SemaphoreType.DMA((n,)), pltpu.SemaphoreType.DMA((n,)), pltpu.HBM(g.shape, g.dtype),
                      pltpu.HBM(land_shape, g.dtype)]
        out_specs += [SEM, SEM, HBM, HBM]
        args += [pltpu.with_memory_space_constraint(g, pltpu.HBM),
                 pltpu.with_memory_space_constraint(lax.empty(land_shape, g.dtype), pltpu.HBM)]
        aliases.update({2 * j: 4 * j + 2, 2 * j + 1: 4 * j + 3})
    res = pl.pallas_call(
        body, name=name, out_shape=out_shape + [jax.ShapeDtypeStruct((8, 128), F32)],
        in_specs=[HBM] * (2 * k), out_specs=out_specs + [pl.BlockSpec(memory_space=pltpu.VMEM)],
        input_output_aliases=aliases, compiler_params=pltpu.CompilerParams(**SPLIT_PARAMS),
    )(*args)
    return [tuple(res[4 * j:4 * j + 4]) + (res[-1],) for j in range(k)]


def _await_owners(name, started, after):
    send_sems, recv_sems, g_thru, land_thru, _ = started

    def body(g_ref, land_ref, send_sems, recv_sems, after_ref, g_out, land_out):
        for cp in _owner_copies(g_ref, land_ref, send_sems, recv_sems):
            cp.wait_send()
            cp.wait_recv()

    return pl.pallas_call(
        body, name=name, out_shape=(pltpu.HBM(g_thru.shape, g_thru.dtype), pltpu.HBM(land_thru.shape, land_thru.dtype)),
        in_specs=(HBM, HBM, SEM, SEM, pl.BlockSpec(memory_space=pl.ANY)), out_specs=(HBM, HBM),
        input_output_aliases={0: 0, 1: 1}, compiler_params=pltpu.CompilerParams(**SPLIT_PARAMS),
    )(g_thru, land_thru, send_sems, recv_sems, after)


def _call(name, main, grid, in_specs, out_specs, out_shape, args, scratch=(), comm=None, prefetch=None, after=None):
    comm = comm or _NoComm()
    n_main = len(in_specs)
    if after is not None:
        in_specs = list(in_specs) + [pl.BlockSpec(after.shape, lambda *_: (0,) * after.ndim)]
        args = tuple(args) + (after,)
    counts = [len(in_specs), len(comm.inputs), len(out_shape), len(comm.out_shapes), len(scratch), len(comm.scratch)]
    n_steps = math.prod(grid)
    hosted = bool(comm.inputs)

    def body(*refs):
        if prefetch is not None:
            refs = refs[1:]
        parts, at = [], 0
        for n in counts:
            parts.append(refs[at:at + n])
            at += n
        ins, c_in, outs, c_out, scr, c_scr = parts
        ins = ins[:n_main]
        step = pl.program_id(0)
        for d in range(1, len(grid)):
            step = step * grid[d] + pl.program_id(d)
        if hosted:
            @pl.when(step == 0)
            def _():
                comm.start(c_in, c_out, c_scr)
        main(ins, outs, scr)
        if hosted:
            @pl.when(step == max(n_steps - 2, 0))
            def _():
                comm.mid(c_in, c_out, c_scr)

            @pl.when(step == n_steps - 1)
            def _():
                comm.finish(c_in, c_out, c_scr)

    all_in = list(in_specs) + [HBM] * counts[1]
    all_out = list(out_specs) + [HBM] * counts[3]
    all_scratch = list(scratch) + list(comm.scratch)
    shapes = list(out_shape) + list(comm.out_shapes)
    if prefetch is None:
        res = pl.pallas_call(body, name=name, grid=grid, in_specs=all_in, out_specs=all_out, out_shape=shapes,
                             scratch_shapes=all_scratch, compiler_params=_cparams())(*args, *comm.inputs)
    else:
        res = pl.pallas_call(
            body, name=name, out_shape=shapes,
            grid_spec=pltpu.PrefetchScalarGridSpec(num_scalar_prefetch=1, grid=grid, in_specs=all_in,
                                                   out_specs=all_out, scratch_shapes=all_scratch),
            compiler_params=_cparams())(prefetch, *args, *comm.inputs)
    return res[:counts[2]], res[counts[2]:]


def _rms_fwd(name, x, g, comm=None):
    T, D = x.shape
    tm = min(T, 512)

    def main(ins, outs, scr):
        x_ref, g_ref = ins
        xv = x_ref[...]
        r = lax.rsqrt(jnp.mean(xv * xv, axis=-1, keepdims=True) + RMS_EPS)
        outs[0][...] = (xv * r * g_ref[...]).astype(outs[0].dtype)

    tile = pl.BlockSpec((tm, D), lambda i: (i, 0))
    (h,), extra = _call(name, main, (T // tm,), [tile, pl.BlockSpec((1, D), lambda i: (0, 0))], [tile],
                        [jax.ShapeDtypeStruct((T, D), BF16)], (x, g), comm=comm)
    return h, extra


def _final_loss_epilogue(scale, out_scale):
    def ep(acc, ex, outs):
        res_ref, g_ref, t_ref = ex
        dx_ref, dxb_ref, dg_ref, loss_ref = outs
        n = acc.shape[-1]
        xv = res_ref[...] + scale * acc
        r = lax.rsqrt(jnp.mean(xv * xv, axis=-1, keepdims=True) + RMS_EPS)
        xhat = xv * r
        err = xhat * g_ref[...] - t_ref[...]

        @pl.when(pl.program_id(0) == 0)
        def _():
            dg_ref[...] = jnp.zeros_like(dg_ref)
            loss_ref[...] = jnp.zeros_like(loss_ref)

        loss_ref[...] += jnp.broadcast_to(jnp.sum(err * err) * (0.5 / n), loss_ref.shape)
        dy = err * (1.0 / n)
        dg_ref[...] += jnp.sum(dy * xhat, axis=0, keepdims=True)
        dxhat = dy * g_ref[...]
        dx = r * (dxhat - xhat * jnp.mean(dxhat * xhat, axis=-1, keepdims=True))
        dx_ref[...] = dx
        dxb_ref[...] = (out_scale * dx).astype(dxb_ref.dtype)
    return ep


def _mm_nstream(name, a, ws, w_sel, w_form, comps, out_dtypes, epilogue, cn, rows=1024, comm=None, after=None):
    T, K = a.shape
    N = ws[0].shape[1]
    rows = min(rows, T)
    assert N % cn == 0 and T % rows == 0
    n_w, n_c = len(ws), len(comps)

    def main(ins, outs, scr):
        a_ref = ins[0]
        w_refs = ins[1:1 + n_w]
        c_refs = ins[1 + n_w:1 + n_w + n_c]

        for r in range(T // rows):
            sl = slice(r * rows, (r + 1) * rows)
            a_blk = a_ref[sl, :]
            dots = [_dot(a_blk, w_ref[...], w_form) for w_ref in w_refs]
            res = epilogue(dots, [c_ref[sl, :] for c_ref in c_refs])
            for o_ref, o in zip(outs, res):
                o_ref[sl, :] = o.astype(o_ref.dtype)

    if w_form == "nt":
        w_specs = [pl.BlockSpec((None, cn, K), functools.partial(lambda j, s: (s, j, 0), s=s)) for s in w_sel]
    else:
        w_specs = [pl.BlockSpec((K, cn), lambda j: (0, j)) for _ in ws]
    chunk = pl.BlockSpec((T, cn), lambda j: (0, j))
    return _call(name, main, (N // cn,), [pl.BlockSpec((T, K), lambda j: (0, 0))] + w_specs + [chunk] * n_c,
                 [chunk] * len(out_dtypes), [jax.ShapeDtypeStruct((T, N), dt) for dt in out_dtypes],
                 (a, *ws, *comps), comm=comm, after=after)


def _mm_mstream(name, as_, ws, w_sel, w_form, extras, outs_desc, epilogue, tm=512, comm=None, after=None):
    T = as_[0].shape[0]
    tm = min(tm, T)
    n_a = len(as_)
    w_shapes = [w.shape[-2:] for w in ws]
    N = w_shapes[0][1] if w_form == "nn" else w_shapes[0][0]

    def main(ins, outs, scr):
        a_refs = ins[:n_a]
        w_refs = ins[n_a:2 * n_a]
        acc = None
        for a_ref, w_ref in zip(a_refs, w_refs):
            d = _dot(a_ref[...], w_ref[...], w_form)
            acc = d if acc is None else acc + d
        epilogue(acc, ins[2 * n_a:], outs)

    kind_spec = {"tile": pl.BlockSpec((tm, N), lambda i: (i, 0)), "vec": pl.BlockSpec((1, N), lambda i: (0, 0))}
    kind_shape = {"tile": (T, N), "vec": (1, N)}
    a_specs = [pl.BlockSpec((tm, a.shape[1]), lambda i: (i, 0)) for a in as_]
    w_specs = []
    for w, s in zip(ws, w_sel):
        if w.ndim == 3:
            w_specs.append(pl.BlockSpec((None,) + tuple(w.shape[1:]), functools.partial(lambda i, s: (s, 0, 0), s=s),
                                        pipeline_mode=pl.Buffered(1)))
        else:
            w_specs.append(pl.BlockSpec(tuple(w.shape), lambda i: (0, 0), pipeline_mode=pl.Buffered(1)))
    args = list(as_) + list(ws) + [e for e, _ in extras]
    return _call(name, main, (T // tm,), a_specs + w_specs + [kind_spec[k] for _, k in extras],
                 [kind_spec[k] for _, k in outs_desc],
                 [jax.ShapeDtypeStruct(kind_shape[k], dt) for dt, k in outs_desc], args, comm=comm, after=after)


def _residual_rms_epilogue(scale):
    def ep(acc, ex, outs):
        xv = ex[0][...] + scale * acc
        outs[0][...] = xv
        r = lax.rsqrt(jnp.mean(xv * xv, axis=-1, keepdims=True) + RMS_EPS)
        outs[1][...] = (xv * r * ex[1][...]).astype(outs[1].dtype)
    return ep


def _rms_bwd_epilogue(out_scale):
    def ep(acc, ex, outs):
        x_ref, g_ref, dres_ref = ex
        dx_ref, dxb_ref, dg_ref = outs
        xv = x_ref[...]
        r = lax.rsqrt(jnp.mean(xv * xv, axis=-1, keepdims=True) + RMS_EPS)
        xhat = xv * r

        @pl.when(pl.program_id(0) == 0)
        def _():
            dg_ref[...] = jnp.zeros_like(dg_ref)

        dg_ref[...] += jnp.sum(acc * xhat, axis=0, keepdims=True)
        dxhat = acc * g_ref[...]
        dx = r * (dxhat - xhat * jnp.mean(dxhat * xhat, axis=-1, keepdims=True)) + dres_ref[...]
        dx_ref[...] = dx
        dxb_ref[...] = (out_scale * dx).astype(dxb_ref.dtype)
    return ep


def _mm_tn(name, a, b, tmo, tno, out_dtype, tk=1024, comm=None, after=None):
    T, Ma = a.shape
    Nb = b.shape[1]
    tk = min(tk, T)
    nk = T // tk

    def main(ins, outs, scr):
        a_ref, b_ref = ins
        (acc_ref,) = scr
        k = pl.program_id(2)

        @pl.when(k == 0)
        def _():
            acc_ref[...] = jnp.zeros_like(acc_ref)

        acc_ref[...] += _dot(a_ref[...], b_ref[...], "tn")

        @pl.when(k == nk - 1)
        def _():
            outs[0][...] = acc_ref[...].astype(outs[0].dtype)

    (out,), extra = _call(
        name, main, (Ma // tmo, Nb // tno, nk),
        [pl.BlockSpec((tk, tmo), lambda i, j, k: (k, i)), pl.BlockSpec((tk, tno), lambda i, j, k: (k, j))],
        [pl.BlockSpec((tmo, tno), lambda i, j, k: (i, j))], [jax.ShapeDtypeStruct((Ma, Nb), out_dtype)],
        (a, b), scratch=[pltpu.VMEM((tmo, tno), F32)], comm=comm, after=after)
    return out, extra


def _swiglu_parts(g, u):
    s = _sigmoid(g)
    silu = g * s
    return [u * (s + silu * (1.0 - s)), silu, silu * u]


def _silu_mul_epilogue(dots, comps):
    g, u = dots
    return _swiglu_parts(g, u)


def _gate_parts_epilogue(dots, comps):
    (g,) = dots
    s = _sigmoid(g)
    silu = g * s
    return [s + silu * (1.0 - s), silu]


def _up_act_epilogue(dots, comps):
    (u,) = dots
    return [u * comps[0].astype(F32), u * comps[1].astype(F32)]


def _dact_epilogue(dots, comps):
    dact = dots[0].astype(BF16)
    return [dact * comps[0], dact * comps[1]]


def _identity_epilogue(dots, comps):
    return list(dots)


def _swap_halves(x):
    lane = lax.broadcasted_iota(jnp.int32, x.shape, 1)
    first = (lane % DK) < (DK // 2)
    return jnp.where(first, pltpu.roll(x, 128 - DK // 2, 1), pltpu.roll(x, DK // 2, 1))


def _rotary(t, cos, sin_signed):
    halves = []
    for p in range(QK_W // 128):
        th = t[:, 128 * p:128 * (p + 1)]
        halves.append(th * cos + _swap_halves(th) * sin_signed)
    return jnp.concatenate(halves, axis=1)


def _rotary_transposed(d, cos, sin_signed):
    halves = []
    for p in range(QK_W // 128):
        dh = d[:, 128 * p:128 * (p + 1)]
        halves.append(dh * cos + _swap_halves(dh * sin_signed))
    return jnp.concatenate(halves, axis=1)


def _log_sigmoid(x):
    return jnp.minimum(x, 0.0) - jnp.log(1.0 + jnp.exp(-jnp.abs(x)))


def _tri_sum(mask, x):
    tri = mask.astype(BF16)
    hi = x.astype(BF16)
    rest = x - hi.astype(F32)
    mid = rest.astype(BF16)
    lo = (rest - mid.astype(F32)).astype(BF16)
    return _dot(tri, hi, "nn") + _dot(tri, mid, "nn") + _dot(tri, lo, "nn")


def _attn_masks():
    row = lax.broadcasted_iota(jnp.int32, (SUPER, SUPER), 0)
    col = lax.broadcasted_iota(jnp.int32, (SUPER, SUPER), 1)
    same = (row // CHUNK) == (col // CHUNK)
    return row, col, same


def _group_inputs(grp, pr, cos, sin_signed, lg, wa2, ba):
    seg = lambda lo, width: pr[:, lo:lo + width].astype(F32)
    if grp == 0:
        q = _rotary(seg(C_RQ, QK_W), cos, sin_signed)
        k = _rotary(seg(C_RK, QK_W), cos, sin_signed) * (DK ** -0.5)
        v = pr[:, C_RV:C_RV + V_W]
        gate = seg(C_RG, V_W)
        pos = lax.broadcasted_iota(jnp.int32, (SUPER, QK_W), 0).astype(F32) + 1.0
        return q, k, v, gate, pos * lg, None, None
    q = seg(C_GQ, QK_W) * (DK ** -0.5)
    k = seg(C_GK, QK_W)
    v = pr[:, C_GV:C_GV + V_W]
    gate = seg(C_GG, V_W)
    glow = pr[:, C_GL:C_GL + GL_W]
    logit = _dot(glow.astype(BF16), wa2.astype(BF16), "nn") + ba
    la = _log_sigmoid(logit) * (1.0 / GATE_NORM)
    row, col, _ = _attn_masks()
    b_cum = _tri_sum(col <= row, la)
    return q, k, v, gate, b_cum, glow, logit


def _decay_factors(q, k, b_cum):
    c = b_cum[SUPER // 2 - 1:SUPER // 2, :]
    bl = b_cum[SUPER - 1:SUPER, :]
    e1 = jnp.exp(b_cum - c)
    e2 = jnp.exp(c - b_cum)
    e_b = jnp.exp(b_cum)
    e_l = jnp.exp(bl - b_cum)
    return dict(e1=e1, e2=e2, eb=e_b, el=e_l, ebl=jnp.exp(bl),
                qp=q * e1, qm=q * e2, kp=k * e1, km=k * e2, qs=q * e_b, kl=k * e_l)


def _state_block_mask():
    r = lax.broadcasted_iota(jnp.int32, (V_W, QK_W), 0)
    c = lax.broadcasted_iota(jnp.int32, (V_W, QK_W), 1)
    return (r // DV) == (c // DK)


def _attn_fwd(proj, cos, sin_signed, lg, wa2p, ba, gn_ret, gn_gla, x_res, w_out, g_next, comm=None):
    T = proj.shape[0]
    n_s = T // SUPER
    D = x_res.shape[1]

    def main(ins, outs, scr):
        pr_ref, cos_ref, sin_ref, lg_ref, wa2_ref, ba_ref, gr_ref, gg_ref, xres_ref, wout_ref, gnext_ref = ins
        o_ref, y_ref, st_ref, x_ref, h_ref = outs
        (s_ref,) = scr
        i = pl.program_id(0)

        @pl.when(i == 0)
        def _():
            s_ref[...] = jnp.zeros_like(s_ref)

        pr = pr_ref
        row, col, same = _attn_masks()
        m1 = col <= row
        m2 = jnp.logical_and(col > row, same)
        lane = lax.broadcasted_iota(jnp.int32, (1, QK_W), 1)
        blockmask = _state_block_mask()
        for grp in range(2):
            q, k, v, gate, b_cum, _, _ = _group_inputs(grp, pr, cos_ref[...], sin_ref[...], lg_ref[...],
                                                      wa2_ref[...], ba_ref[...])
            f = _decay_factors(q, k, b_cum)
            gn = gr_ref[...] if grp == 0 else gg_ref[...]
            s_prev = s_ref[grp]
            st_ref[0, grp] = s_prev
            o_inter = _dot(f["qs"].astype(BF16), s_prev.astype(BF16), "nt")
            kmb = f["km"].astype(BF16)
            kpb = f["kp"].astype(BF16)
            vb = v.astype(BF16)
            heads = [(lane // DK) == h for h in range(HEADS)]
            a1_all = _dot(jnp.concatenate([jnp.where(hm, f["qp"], 0.0).astype(BF16) for hm in heads], axis=0), kmb, "nt")
            a2_all = _dot(jnp.concatenate([jnp.where(hm, f["qm"], 0.0).astype(BF16) for hm in heads], axis=0), kpb, "nt")
            for h in range(HEADS):
                a1 = a1_all[h * SUPER:(h + 1) * SUPER]
                a2 = a2_all[h * SUPER:(h + 1) * SUPER]
                a = jnp.where(m1, a1, jnp.where(m2, a2, 0.0))
                lo = grp * V_W + h * DV
                o_h = _dot(a.astype(BF16), vb[:, h * DV:(h + 1) * DV], "nn") + o_inter[:, h * DV:(h + 1) * DV]
                o_ref[:, lo:lo + DV] = o_h
                r = lax.rsqrt(jnp.mean(o_h * o_h, axis=-1, keepdims=True) + RMS_EPS)
                gte = gate[:, h * DV:(h + 1) * DV]
                y = o_h * r * gn[:, h * DV:(h + 1) * DV] * (gte * _sigmoid(gte))
                y_ref[:, lo:lo + DV] = y.astype(y_ref.dtype)
            upd = _dot(vb, f["kl"].astype(BF16), "tn")
            s_ref[grp] = s_prev * f["ebl"] + jnp.where(blockmask, upd, 0.0)
        xv = xres_ref[...] + _dot(y_ref[...], wout_ref[...], "nn")
        x_ref[...] = xv
        r = lax.rsqrt(jnp.mean(xv * xv, axis=-1, keepdims=True) + RMS_EPS)
        h_ref[...] = (xv * r * gnext_ref[...]).astype(h_ref.dtype)

    const = lambda shape: pl.BlockSpec(shape, lambda i: tuple(0 for _ in shape))
    rows = lambda w: pl.BlockSpec((SUPER, w), lambda i: (i, 0))
    return _call(
        "attn_fwd", main, (n_s,),
        [rows(PROJ_W), rows(128), rows(128),
         const((1, QK_W)), const((GL_W, QK_W)), const((1, QK_W)), const((1, V_W)), const((1, V_W)),
         rows(D), const((2 * V_W, D)), const((1, D))],
        [rows(2 * V_W), rows(2 * V_W), pl.BlockSpec((1, 2, V_W, QK_W), lambda i: (i, 0, 0, 0)), rows(D), rows(D)],
        [jax.ShapeDtypeStruct((T, 2 * V_W), F32), jax.ShapeDtypeStruct((T, 2 * V_W), BF16),
         jax.ShapeDtypeStruct((n_s, 2, V_W, QK_W), F32), jax.ShapeDtypeStruct((T, D), F32),
         jax.ShapeDtypeStruct((T, D), BF16)],
        (proj, cos, sin_signed, lg, wa2p, ba, gn_ret, gn_gla, x_res, w_out, g_next),
        scratch=[pltpu.VMEM((2, V_W, QK_W), F32)], comm=comm)


def _attn_bwd(proj, cos, sin_signed, lg, wa2p, ba, gn_ret, gn_gla, o, dx, w_out, states, comm=None, after=None):
    T = proj.shape[0]
    n_s = T // SUPER
    D = dx.shape[1]

    def main(ins, outs, scr):
        pr_ref, cos_ref, sin_ref, lg_ref, wa2_ref, ba_ref, gr_ref, gg_ref, o_ref, dx_ref, wout_ref, st_ref = ins
        dp_ref, dgr_ref, dgg_ref, dba_ref, dwa_ref = outs
        (ds_ref, dy_ref) = scr
        i = pl.program_id(0)
        dy_ref[...] = _dot(dx_ref[...], wout_ref[...], "nt")

        @pl.when(i == 0)
        def _():
            ds_ref[...] = jnp.zeros_like(ds_ref)
            dgr_ref[...] = jnp.zeros_like(dgr_ref)
            dgg_ref[...] = jnp.zeros_like(dgg_ref)
            dba_ref[...] = jnp.zeros_like(dba_ref)
            dwa_ref[...] = jnp.zeros_like(dwa_ref)

        pr = pr_ref
        cos = cos_ref[...]
        sin_signed = sin_ref[...]
        row, col, same = _attn_masks()
        m1 = col <= row
        m2 = jnp.logical_and(col > row, same)
        m1t = row <= col
        m2t = jnp.logical_and(row > col, same)
        lane = lax.broadcasted_iota(jnp.int32, (1, QK_W), 1)
        blockmask = _state_block_mask()
        for grp in range(2):
            q, k, v, gate, b_cum, glow, logit = _group_inputs(grp, pr, cos, sin_signed, lg_ref[...],
                                                              wa2_ref[...], ba_ref[...])
            f = _decay_factors(q, k, b_cum)
            gn = gr_ref[...] if grp == 0 else gg_ref[...]
            dgn_ref = dgr_ref if grp == 0 else dgg_ref
            do_parts, dgate_parts, dgn_parts = [], [], []
            for h in range(HEADS):
                lo = grp * V_W + h * DV
                o_h = o_ref[:, lo:lo + DV]
                r = lax.rsqrt(jnp.mean(o_h * o_h, axis=-1, keepdims=True) + RMS_EPS)
                n = o_h * r
                gte = gate[:, h * DV:(h + 1) * DV]
                sg = _sigmoid(gte)
                dy_h = dy_ref[:, lo:lo + DV]
                gn_h = gn[:, h * DV:(h + 1) * DV]
                dgate_parts.append(dy_h * n * gn_h * (sg * (1.0 + gte * (1.0 - sg))))
                dz = dy_h * (gte * sg)
                dgn_parts.append(jnp.sum(dz * n, axis=0, keepdims=True))
                dn = dz * gn_h
                do_parts.append(r * (dn - n * jnp.mean(dn * n, axis=-1, keepdims=True)))
            dgn_ref[...] += jnp.concatenate(dgn_parts, axis=1)
            dgate = jnp.concatenate(dgate_parts, axis=1)
            do = jnp.concatenate(do_parts, axis=1)
            dob = do.astype(BF16)
            vb = v.astype(BF16)
            s_prev = st_ref[0, grp]
            ds_new = ds_ref[grp]
            dsb = ds_new.astype(BF16)
            qpb, qmb = f["qp"].astype(BF16), f["qm"].astype(BF16)
            kpb, kmb = f["kp"].astype(BF16), f["km"].astype(BF16)
            dv_parts = []
            heads = [(lane // DK) == h for h in range(HEADS)]
            qp_hs = [jnp.where(hm, f["qp"], 0.0).astype(BF16) for hm in heads]
            qm_hs = [jnp.where(hm, f["qm"], 0.0).astype(BF16) for hm in heads]
            kp_hs = [jnp.where(hm, f["kp"], 0.0).astype(BF16) for hm in heads]
            km_hs = [jnp.where(hm, f["km"], 0.0).astype(BF16) for hm in heads]
            km_stack, kp_stack = jnp.concatenate(km_hs, axis=0), jnp.concatenate(kp_hs, axis=0)
            at1_all = _dot(km_stack, qpb, "nt")
            at2_all = _dot(kp_stack, qmb, "nt")
            da1s, da2s, da1ts, da2ts = [], [], [], []
            for h in range(HEADS):
                at = jnp.where(m1t, at1_all[h * SUPER:(h + 1) * SUPER],
                               jnp.where(m2t, at2_all[h * SUPER:(h + 1) * SUPER], 0.0))
                do_h = dob[:, h * DV:(h + 1) * DV]
                v_h = vb[:, h * DV:(h + 1) * DV]
                dv_parts.append(_dot(at.astype(BF16), do_h, "nn"))
                da = _dot(do_h, v_h, "nt")
                dat = _dot(v_h, do_h, "nt")
                da1s.append(jnp.where(m1, da, 0.0).astype(BF16))
                da2s.append(jnp.where(m2, da, 0.0).astype(BF16))
                da1ts.append(jnp.where(m1t, dat, 0.0).astype(BF16))
                da2ts.append(jnp.where(m2t, dat, 0.0).astype(BF16))
            dqp = _dot(jnp.concatenate(da1s, axis=1), km_stack, "nn")
            dqm = _dot(jnp.concatenate(da2s, axis=1), kp_stack, "nn")
            dkm = _dot(jnp.concatenate(da1ts, axis=1), jnp.concatenate(qp_hs, axis=0), "nn")
            dkp = _dot(jnp.concatenate(da2ts, axis=1), jnp.concatenate(qm_hs, axis=0), "nn")
            klb = f["kl"].astype(BF16)
            qsb = f["qs"].astype(BF16)
            dqs = _dot(dob, s_prev.astype(BF16), "nn")
            dkl = _dot(vb, dsb, "nn")
            dv = jnp.concatenate(dv_parts, axis=1) + _dot(klb, dsb, "nt")
            ds_ref[grp] = ds_new * f["ebl"] + jnp.where(blockmask, _dot(dob, qsb, "tn"), 0.0)
            dq = dqp * f["e1"] + dqm * f["e2"] + dqs * f["eb"]
            dk = dkm * f["e2"] + dkp * f["e1"] + dkl * f["el"]
            if grp == 0:
                dq = _rotary_transposed(dq, cos, sin_signed)
                dk = _rotary_transposed(dk * (DK ** -0.5), cos, sin_signed)
                dp_ref[:, C_RQ:C_RQ + QK_W] = dq.astype(dp_ref.dtype)
                dp_ref[:, C_RK:C_RK + QK_W] = dk.astype(dp_ref.dtype)
                dp_ref[:, C_RV:C_RV + V_W] = dv.astype(dp_ref.dtype)
                dp_ref[:, C_RG:C_RG + V_W] = dgate.astype(dp_ref.dtype)
            else:
                dkl_kl = dkl * klb.astype(F32)
                db = (dqp * qpb.astype(F32) - dkm * kmb.astype(F32) - dqm * qmb.astype(F32)
                      + dkp * kpb.astype(F32) + dqs * qsb.astype(F32) - dkl_kl)
                last = (jnp.sum(dkl_kl, axis=0, keepdims=True)
                        + f["ebl"] * jnp.sum(s_prev * ds_new, axis=0, keepdims=True))
                rowq = lax.broadcasted_iota(jnp.int32, (SUPER, QK_W), 0)
                db = db + jnp.where(rowq == SUPER - 1, last, 0.0)
                dla = _tri_sum(col >= row, db)
                dlogit = dla * (1.0 / GATE_NORM) * (1.0 - _sigmoid(logit))
                dlb = dlogit.astype(BF16)
                dglow = _dot(dlb, wa2_ref[...].astype(BF16), "nt")
                dwa_ref[...] += _dot(glow.astype(BF16), dlb, "tn")
                dba_ref[...] += jnp.sum(dlogit, axis=0, keepdims=True)
                dp_ref[:, C_GQ:C_GQ + QK_W] = (dq * (DK ** -0.5)).astype(dp_ref.dtype)
                dp_ref[:, C_GK:C_GK + QK_W] = dk.astype(dp_ref.dtype)
                dp_ref[:, C_GV:C_GV + V_W] = dv.astype(dp_ref.dtype)
                dp_ref[:, C_GG:C_GG + V_W] = dgate.astype(dp_ref.dtype)
                dp_ref[:, C_GL:C_GL + GL_W] = dglow.astype(dp_ref.dtype)

    rev = lambda i: n_s - 1 - i
    const = lambda shape: pl.BlockSpec(shape, lambda i: tuple(0 for _ in shape))
    return _call(
        "attn_bwd", main, (n_s,),
        [pl.BlockSpec((SUPER, PROJ_W), lambda i: (rev(i), 0)),
         pl.BlockSpec((SUPER, 128), lambda i: (rev(i), 0)), pl.BlockSpec((SUPER, 128), lambda i: (rev(i), 0)),
         const((1, QK_W)), const((GL_W, QK_W)), const((1, QK_W)), const((1, V_W)), const((1, V_W)),
         pl.BlockSpec((SUPER, 2 * V_W), lambda i: (rev(i), 0)),
         pl.BlockSpec((SUPER, D), lambda i: (rev(i), 0)), const((2 * V_W, D)),
         pl.BlockSpec((1, 2, V_W, QK_W), lambda i: (rev(i), 0, 0, 0))],
        [pl.BlockSpec((SUPER, PROJ_W), lambda i: (rev(i), 0)),
         const((1, V_W)), const((1, V_W)), const((1, QK_W)), const((GL_W, QK_W))],
        [jax.ShapeDtypeStruct((T, PROJ_W), BF16),
         jax.ShapeDtypeStruct((1, V_W), F32), jax.ShapeDtypeStruct((1, V_W), F32),
         jax.ShapeDtypeStruct((1, QK_W), F32), jax.ShapeDtypeStruct((GL_W, QK_W), F32)],
        (proj, cos, sin_signed, lg, wa2p, ba, gn_ret, gn_gla, o, dx, w_out, states),
        scratch=[pltpu.VMEM((2, V_W, QK_W), F32), pltpu.VMEM((SUPER, 2 * V_W), F32)], comm=comm, after=after)


def _rotary_tables(T):
    half = DK // 2
    inv = ROPE_BASE ** (-jnp.arange(half, dtype=F32) * 2.0 / DK)
    ang = jnp.arange(T, dtype=F32)[:, None] * inv[None, :]
    cos, sin = jnp.cos(ang), jnp.sin(ang)
    cos_head = jnp.concatenate([cos, cos], axis=1)
    sin_head = jnp.concatenate([-sin, sin], axis=1)
    return jnp.tile(cos_head, (1, 128 // DK)), jnp.tile(sin_head, (1, 128 // DK))


def _sum_devices(name, gathered, m_per):
    def body(g_ref, o_ref):
        acc = g_ref[0:m_per, :]
        for k in range(1, N_DEV):
            acc = acc + g_ref[k * m_per:(k + 1) * m_per, :]
        o_ref[...] = acc

    return pl.pallas_call(body, name=name, out_shape=jax.ShapeDtypeStruct((m_per, 128), F32))(gathered)


def _owner_sums(name, items, owner, comm=None):
    counts = [1 + len(landed) for _, landed in items]

    def main(ins, outs, scr):
        at = 0
        for o_ref, n in zip(outs, counts):
            acc = ins[at][...].astype(F32)
            for l_ref in ins[at + 1:at + n]:
                for j in range(l_ref.shape[0]):
                    acc = acc + l_ref[j].astype(F32)
            o_ref[...] = acc
            at += n

    once = pl.Buffered(1)
    in_specs, out_specs, out_shape, args = [], [], [], []
    for grad, landed in items:
        R, C = grad.shape[-2:]
        in_specs.append(pl.BlockSpec((None, None, R, C), lambda i, s: (s[0], s[1], 0, 0), pipeline_mode=once))
        in_specs += [pl.BlockSpec(tuple(l.shape), lambda i, s: (0, 0, 0), pipeline_mode=once) for l in landed]
        out_specs.append(pl.BlockSpec((R, C), lambda i, s: (0, 0)))
        out_shape.append(jax.ShapeDtypeStruct((R, C), F32))
        args += [grad, *landed]
    return _call(name, main, (1,), in_specs, out_specs, out_shape, args, comm=comm, prefetch=owner)


def _adamw_group(name, items, n_blocks, comm=None):
    n = len(items)

    def main(ins, outs, scr):
        for p in range(n):
            g_ref, w_ref, m_ref, v_ref = ins[4 * p:4 * p + 4]
            d_ref, nm_ref, nv_ref = outs[3 * p:3 * p + 3]
            gv = g_ref[...]
            nm = ADAM_B1 * m_ref[...] + (1.0 - ADAM_B1) * gv
            nv = ADAM_B2 * v_ref[...] + (1.0 - ADAM_B2) * (gv * gv)
            m_hat = nm / (1.0 - ADAM_B1 ** ADAM_STEP)
            v_hat = nv / (1.0 - ADAM_B2 ** ADAM_STEP)
            d_ref[...] = -ADAM_LR * (m_hat / (jnp.sqrt(v_hat) + ADAM_EPS) + ADAM_WD * w_ref[...])
            nm_ref[...] = nm
            nv_ref[...] = nv

    in_specs, out_specs, out_shape, args = [], [], [], []
    for item in items:
        R, C = item[1].shape
        assert R % n_blocks == 0
        spec = pl.BlockSpec((R // n_blocks, C), lambda i: (i, 0))
        in_specs += [spec] * 4
        out_specs += [spec] * 3
        out_shape += [jax.ShapeDtypeStruct((R, C), F32)] * 3
        args += list(item)
    outs, extra = _call(name, main, (n_blocks,), in_specs, out_specs, out_shape, args, comm=comm)
    return [tuple(outs[3 * p:3 * p + 3]) for p in range(n)], extra


SMALL_ORDER = ("ffn1", "mix", "ffn2", "final", "ret", "gla", "b_a")


def kernel(x, ffn1_norm_g, ffn1_w_gate, ffn1_w_up, ffn1_w_down, mix_norm_g, w_in, ret_norm_g, gla_w_a2, gla_b_a, gla_norm_g, w_out, ffn2_norm_g, ffn2_w_gate, ffn2_w_up, ffn2_w_down, final_norm_g, loss_target, m_ffn1_norm_g, m_ffn1_w_gate, m_ffn1_w_up, m_ffn1_w_down, m_mix_norm_g, m_w_in, m_ret_norm_g, m_gla_w_a2, m_gla_b_a, m_gla_norm_g, m_w_out, m_ffn2_norm_g, m_ffn2_w_gate, m_ffn2_w_up, m_ffn2_w_down, m_final_norm_g, v_ffn1_norm_g, v_ffn1_w_gate, v_ffn1_w_up, v_ffn1_w_down, v_mix_norm_g, v_w_in, v_ret_norm_g, v_gla_w_a2, v_gla_b_a, v_gla_norm_g, v_w_out, v_ffn2_norm_g, v_ffn2_w_gate, v_ffn2_w_up, v_ffn2_w_down, v_final_norm_g):
    xi, yi, ci = _coords()
    dev = 4 * xi + 2 * yi + ci
    owner = jnp.stack([2 * xi + yi, ci]).astype(jnp.int32)

    x0, target = x[0], loss_target[0]
    T, D = x0.shape
    fb = ffn1_w_gate.shape[2]
    ib = w_in.shape[2]
    ab = gla_w_a2.shape[2]
    F = N_DEV * fb
    cos, sin_signed = _rotary_tables(T)
    lg = jnp.repeat(jnp.log(1.0 - 2.0 ** (-5.0 - jnp.arange(HEADS, dtype=F32))), DK)[None, :]
    g_final = final_norm_g.reshape(1, D)

    g1_loc = ffn1_w_gate[0].T[None].astype(BF16)
    u1_loc = ffn1_w_up[0].T[None].astype(BF16)
    d1_loc = ffn1_w_down.astype(BF16)
    g2_loc = ffn2_w_gate[0].T[None].astype(BF16)
    u2_loc = ffn2_w_up[0].T[None].astype(BF16)
    d2_loc = ffn2_w_down.astype(BF16)
    in_loc = w_in[0].T.astype(BF16)
    out_loc = w_out[0].astype(BF16)

    h1, (g1,) = _rms_fwd("ffn1_rms", x0, ffn1_norm_g, comm=_AllGather([g1_loc], ["stack"]))
    g1 = g1.reshape(1, F, D)
    (dsl1, sl1), (u1,) = _mm_nstream("ffn1_gate", h1, [g1], [0], "nt", [], [BF16, BF16], _gate_parts_epilogue, cn=256,
                                     comm=_AllGather([u1_loc], ["stack"]))
    u1 = u1.reshape(1, F, D)
    (dsu1, act1), (d1,) = _mm_nstream("ffn1_up", h1, [u1], [0], "nt", [dsl1, sl1], [BF16, BF16],
                                      _up_act_epilogue, cn=256, comm=_AllGather([d1_loc], ["stack"]))
    d1 = d1.reshape(1, F, D)
    f32_tile, bf16_tile, f32_vec = (F32, "tile"), (BF16, "tile"), (F32, "vec")
    (x1, h2), (in_all, a_all) = _mm_mstream(
        "ffn1_down", [act1], [d1], [0], "nn", [(x0, "tile"), (mix_norm_g, "vec")], [f32_tile, bf16_tile],
        _residual_rms_epilogue(0.5), comm=_AllGather([in_loc, gla_w_a2[0]], ["plain", "plain"]))
    w_in_t = jnp.pad(in_all.reshape(1, N_DEV * ib, D), ((0, 0), (0, PROJ_W - N_DEV * ib), (0, 0)))
    wa2 = jnp.transpose(a_all, (1, 0, 2)).reshape(GATE_RANK, N_DEV * ab)
    wa2p = jnp.pad(wa2, ((0, GL_W - GATE_RANK), (0, 0)))

    (proj,), (g2, out_all) = _mm_nstream("mix_proj", h2, [w_in_t], [0], "nt", [], [BF16], _identity_epilogue, cn=640,
                                         comm=_AllGather([g2_loc, out_loc], ["stack", "plain"]))
    w_out_full = out_all.reshape(D, D)
    (o, ymix, states, x2, h3), (u2,) = _attn_fwd(proj, cos, sin_signed, lg, wa2p, gla_b_a, ret_norm_g, gla_norm_g,
                                                 x1, w_out_full, ffn2_norm_g, comm=_AllGather([u2_loc], ["stack"]))
    g2, u2 = g2.reshape(1, F, D), u2.reshape(1, F, D)

    (dsu2, sl2, act2), (d2,) = _mm_nstream(
        "ffn2_up", h3, [g2, u2], [0, 0], "nt", [], [BF16, BF16, BF16], _silu_mul_epilogue, cn=256,
        comm=_AllGather([d2_loc], ["stack"]))
    d2 = d2.reshape(1, F, D)
    (dx3, dy3b, d_final, loss), _ = _mm_mstream(
        "ffn2_down", [act2], [d2], [0], "nn", [(x2, "tile"), (g_final, "vec"), (target, "tile")],
        [f32_tile, bf16_tile, f32_vec, f32_vec], _final_loss_epilogue(0.5, 0.5))

    sent = {}

    def send(**grads):
        started = _send_to_owners("send_" + "_".join(grads), list(grads.values()))
        sent.update(zip(grads, started))
        return started[0][4]

    dwd2, _ = _mm_tn("ffn2b_dwd", act2, dy3b, F // 2, D, BF16)
    (dgate2, dup2), _ = _mm_nstream("ffn2b_dact", dy3b, [d2], [0], "nt", [dsu2, sl2], [BF16, BF16],
                                    _dact_epilogue, cn=256)
    dwg2, _ = _mm_tn("ffn2b_dwg", dgate2, h3, F // 2, D, BF16)
    dwu2, _ = _mm_tn("ffn2b_dwu", dup2, h3, F // 2, D, BF16)
    tok = send(wd2=dwd2.reshape(4, 2, fb, D), wg2=dwg2.reshape(4, 2, fb, D), wu2=dwu2.reshape(4, 2, fb, D))
    rms_outs = [f32_tile, bf16_tile, f32_vec]
    (dx2, dx2b, d_g2), _ = _mm_mstream(
        "ffn2b_dh", [dgate2, dup2], [g2, u2], [0, 0], "nn", [(x2, "tile"), (ffn2_norm_g, "vec"), (dx3, "tile")],
        rms_outs, _rms_bwd_epilogue(1.0), after=tok)

    dwout, _ = _mm_tn("mixb_dwout", ymix, dx2b, D, D, BF16, tk=4096)
    (dproj, d_ret, d_gla, d_ba, d_wa2p), _ = _attn_bwd(
        proj, cos, sin_signed, lg, wa2p, gla_b_a, ret_norm_g, gla_norm_g, o, dx2b, w_out_full, states)
    dwin_t, _ = _mm_tn("mixb_dwin", dproj, h2, 640, D, BF16, tk=4096)
    tok = send(wout=dwout.reshape(4, 2, D // N_DEV, D), win=dwin_t[:N_DEV * ib].reshape(4, 2, ib, D))
    (dx1, dy1b, d_gmix), _ = _mm_mstream(
        "mixb_dh", [dproj], [w_in_t], [0], "nn", [(x1, "tile"), (mix_norm_g, "vec"), (dx2, "tile")],
        rms_outs, _rms_bwd_epilogue(0.5), after=tok)

    dwd1, _ = _mm_tn("ffn1b_dwd", act1, dy1b, F // 2, D, BF16)
    tok = send(wd1=dwd1.reshape(4, 2, fb, D))
    (dgate1, dup1), _ = _mm_nstream("ffn1b_dact", dy1b, [d1], [0], "nt", [dsu1, sl1], [BF16, BF16],
                                    _dact_epilogue, cn=256, after=tok)
    dwg1, _ = _mm_tn("ffn1b_dwg", dgate1, h1, F // 2, D, BF16)
    tok = send(wg1=dwg1.reshape(4, 2, fb, D))
    dwu1, _ = _mm_tn("ffn1b_dwu", dup1, h1, F // 2, D, BF16, after=tok)
    tok = send(wu1=dwu1.reshape(4, 2, fb, D))
    (dx0, _, d_g1), _ = _mm_mstream(
        "ffn1b_dh", [dgate1, dup1], [g1, u1], [0, 0], "nn", [(x0, "tile"), (ffn1_norm_g, "vec"), (dx1, "tile")],
        rms_outs, _rms_bwd_epilogue(1.0), after=tok)

    small = dict(ffn1=d_g1, mix=d_gmix, ffn2=d_g2, final=d_final, ret=d_ret, gla=d_gla, b_a=d_ba)
    flat = jnp.concatenate([small[k].reshape(-1) for k in SMALL_ORDER]
                           + [d_wa2p[:GATE_RANK].reshape(-1), loss[0, :128]])
    rows = -(-flat.shape[0] // 128)
    rows = -(-rows // 8) * 8
    packed = jnp.pad(flat, (0, rows * 128 - flat.shape[0])).reshape(rows, 128)

    transposed = ("ffn1_w_gate", "ffn1_w_up", "ffn2_w_gate", "ffn2_w_up", "w_in")

    def to_2d(nm, a):
        if nm in transposed:
            return a[0].T
        return a.reshape((1, a.shape[0]) if a.ndim == 1 else a.shape[-2:])

    def from_2d(nm, a):
        return a.T[None] if nm in transposed else a.reshape(params[nm][0].shape)

    def arrived(nm, after):
        grad, landed = _await_owners("await_" + nm, sent[nm], after)
        return grad, [landed]

    sums_a, (gathered,) = _owner_sums(
        "sum_a", [arrived(nm, dx0) for nm in ("wg2", "wu2", "wd2", "win", "wout")], owner,
        comm=_AllGather([packed], ["plain"]))
    params = dict(
        ffn2_w_gate=(ffn2_w_gate, m_ffn2_w_gate, v_ffn2_w_gate), ffn2_w_up=(ffn2_w_up, m_ffn2_w_up, v_ffn2_w_up),
        ffn2_w_down=(ffn2_w_down, m_ffn2_w_down, v_ffn2_w_down), w_in=(w_in, m_w_in, v_w_in),
        w_out=(w_out, m_w_out, v_w_out), ffn1_w_gate=(ffn1_w_gate, m_ffn1_w_gate, v_ffn1_w_gate),
        ffn1_w_up=(ffn1_w_up, m_ffn1_w_up, v_ffn1_w_up), ffn1_w_down=(ffn1_w_down, m_ffn1_w_down, v_ffn1_w_down),
        ffn1_norm_g=(ffn1_norm_g, m_ffn1_norm_g, v_ffn1_norm_g), mix_norm_g=(mix_norm_g, m_mix_norm_g, v_mix_norm_g),
        ret_norm_g=(ret_norm_g, m_ret_norm_g, v_ret_norm_g), gla_w_a2=(gla_w_a2, m_gla_w_a2, v_gla_w_a2),
        gla_b_a=(gla_b_a, m_gla_b_a, v_gla_b_a), gla_norm_g=(gla_norm_g, m_gla_norm_g, v_gla_norm_g),
        ffn2_norm_g=(ffn2_norm_g, m_ffn2_norm_g, v_ffn2_norm_g), final_norm_g=(final_norm_g, m_final_norm_g, v_final_norm_g))
    grads, updates = {}, {}

    def run_adam(name, names, grad_2d, n_blocks):
        items = [(grad_2d[nm],) + tuple(to_2d(nm, a) for a in params[nm]) for nm in names]
        res, _ = _adamw_group(name, items, n_blocks)
        for nm, r in zip(names, res):
            grads[nm] = from_2d(nm, grad_2d[nm])
            updates[nm] = tuple(from_2d(nm, a) for a in r)
        return res

    grads_a = {"ffn2_w_gate": sums_a[0], "ffn2_w_up": sums_a[1], "ffn2_w_down": sums_a[2], "w_out": sums_a[4]}
    run_adam("adamw_w_in", ["w_in"], {"w_in": sums_a[3]}, 1)
    done_a = run_adam("adamw_a", list(grads_a), grads_a, 4)[0][0]
    sums_b, _ = _owner_sums("sum_b", [arrived(nm, done_a) for nm in ("wg1", "wu1", "wd1")], owner)
    grads_b = {"ffn1_w_gate": sums_b[0], "ffn1_w_up": sums_b[1], "ffn1_w_down": sums_b[2]}
    run_adam("adamw_b", list(grads_b), grads_b, 4)

    total = _sum_devices("sum_small", gathered.reshape(N_DEV * rows, 128), rows).reshape(-1)
    sizes = [small[k].size for k in SMALL_ORDER] + [GATE_RANK * QK_W, 128]
    offs = [0]
    for s in sizes:
        offs.append(offs[-1] + s)
    pieces = [total[offs[i]:offs[i + 1]] for i in range(len(sizes))]
    g_small = {k: pieces[i].reshape(small[k].shape) for i, k in enumerate(SMALL_ORDER)}
    g_wa2_full = pieces[len(SMALL_ORDER)].reshape(GATE_RANK, QK_W)
    g_wa2 = lax.dynamic_slice(g_wa2_full, (0, dev * ab), (GATE_RANK, ab))
    loss_total = pieces[len(SMALL_ORDER) + 1][0]

    small_grads = {"ffn1_norm_g": g_small["ffn1"], "mix_norm_g": g_small["mix"], "ret_norm_g": g_small["ret"],
                   "gla_w_a2": g_wa2, "gla_b_a": g_small["b_a"], "gla_norm_g": g_small["gla"],
                   "ffn2_norm_g": g_small["ffn2"], "final_norm_g": g_small["final"]}
    run_adam("adamw_small", list(small_grads), small_grads, 1)

    order = ("ffn1_norm_g", "ffn1_w_gate", "ffn1_w_up", "ffn1_w_down", "mix_norm_g", "w_in", "ret_norm_g", "gla_w_a2",
             "gla_b_a", "gla_norm_g", "w_out", "ffn2_norm_g", "ffn2_w_gate", "ffn2_w_up", "ffn2_w_down", "final_norm_g")
    return (loss_total, dx0[None], *[grads[nm] for nm in order], *[updates[nm][0] for nm in order],
            *[updates[nm][1] for nm in order], *[updates[nm][2] for nm in order])
```

```python
import functools
import math

import jax
import jax.numpy as jnp
from jax import lax
from jax.experimental import pallas as pl
from jax.experimental.pallas import tpu as pltpu

F32 = jnp.float32
BF16 = jnp.bfloat16
MESH = pl.DeviceIdType.MESH
HBM = pl.BlockSpec(memory_space=pltpu.HBM)

N_DEV = 8
RMS_EPS = 1e-6
ROPE_BASE = 10000.0
HEADS = 4
DK = 64
DV = 128
QK_W = HEADS * DK
V_W = HEADS * DV
GATE_RANK = 16
GATE_NORM = 16.0
CHUNK = 64
SUPER = 256
PROJ_W = 3200
C_RQ, C_RK, C_RV, C_RG, C_GQ, C_GK, C_GV, C_GG, C_GL = 0, 256, 512, 1024, 1536, 1792, 2048, 2560, 3072
GL_W = PROJ_W - C_GL
ADAM_LR, ADAM_B1, ADAM_B2, ADAM_EPS, ADAM_WD, ADAM_STEP = 0.001, 0.9, 0.999, 1e-08, 0.01, 10
VMEM_LIMIT_V7X = 52 * 1024 * 1024


def _cparams(**kw):
    return pltpu.CompilerParams(vmem_limit_bytes=VMEM_LIMIT_V7X, **kw)


def _dot(a, b, form, precision=None):
    dims = {"nn": (((1,), (0,)), ((), ())), "nt": (((1,), (1,)), ((), ())), "tn": (((0,), (0,)), ((), ()))}[form]
    return lax.dot_general(a, b, dims, preferred_element_type=F32, precision=precision)


def _sigmoid(x):
    return 1.0 / (1.0 + jnp.exp(-x))


def _coords():
    return lax.axis_index("x"), lax.axis_index("y"), lax.axis_index("c")


class _NoComm:
    inputs, out_shapes, scratch = (), (), ()


class _AllGather:
    def __init__(self, arrays, kinds):
        self.inputs = tuple(arrays)
        self.kinds = tuple(kinds)
        n = len(arrays)
        self.out_shapes = tuple(
            jax.ShapeDtypeStruct((a.shape[0], N_DEV) + a.shape[1:] if k == "stack" else (N_DEV,) + a.shape, a.dtype)
            for a, k in zip(arrays, kinds))
        self.scratch = (pltpu.SemaphoreType.DMA((n, 7)), pltpu.SemaphoreType.DMA((n, 7)),
                        pltpu.SemaphoreType.DMA((n,)))

    def _ctx(self, srcs, outs, sems):
        send_sems, recv_sems, local_sems = sems
        x, y, c = _coords()
        me, sibling = (x, y, c), (x, y, 1 - c)
        chips = [(1 - x, y), (x, 1 - y), (1 - x, 1 - y)]

        def blk(m, dev):
            k = 4 * dev[0] + 2 * dev[1] + dev[2]
            return outs[m].at[:, k] if self.kinds[m] == "stack" else outs[m].at[k]

        def copy(m, s, block, to, src=None):
            return pltpu.make_async_remote_copy(
                src_ref=blk(m, block) if src is None else src, dst_ref=blk(m, block),
                send_sem=send_sems.at[m, s], recv_sem=recv_sems.at[m, s], device_id=to, device_id_type=MESH)

        def mine(m):
            return pltpu.make_async_copy(srcs[m], blk(m, me), local_sems.at[m])

        def first(m):
            return [copy(m, 0, me, sibling, src=srcs[m])] + [
                copy(m, 1 + j, me, (*chip, c), src=srcs[m]) for j, chip in enumerate(chips)]

        return me, sibling, chips, c, copy, mine, first

    def start(self, srcs, outs, sems):
        me, sibling, chips, c, copy, mine, first = self._ctx(srcs, outs, sems)
        for m in range(len(srcs)):
            mine(m).start()
            for cp in first(m):
                cp.start()

    def mid(self, srcs, outs, sems):
        me, sibling, chips, c, copy, mine, first = self._ctx(srcs, outs, sems)
        for j, chip in enumerate(chips):
            for m in range(len(srcs)):
                copy(m, 1 + j, (*chip, c), me).wait_recv()
                copy(m, 4 + j, (*chip, c), sibling).start()

    def finish(self, srcs, outs, sems):
        me, sibling, chips, c, copy, mine, first = self._ctx(srcs, outs, sems)
        for m in range(len(srcs)):
            copy(m, 0, sibling, me).wait_recv()
            for j, chip in enumerate(chips):
                copy(m, 4 + j, (*chip, 1 - c), me).wait_recv()
            for cp in first(m):
                cp.wait_send()
            for j, chip in enumerate(chips):
                copy(m, 4 + j, (*chip, c), sibling).wait_send()
            mine(m).wait()


RELATIONS = ((0, 0, 1), (1, 0, 0), (0, 1, 0), (1, 1, 0), (1, 0, 1), (0, 1, 1), (1, 1, 1))
SEM = pl.BlockSpec(memory_space=pltpu.SEMAPHORE)
SPLIT_PARAMS = dict(has_side_effects=pltpu.SideEffectType.DATAFLOW_SIDE_EFFECTING)


def _owner_copies(grad_ref, land_ref, send_sems, recv_sems):
    x, y, c = _coords()
    copies = []
    for s, (fx, fy, fc) in enumerate(RELATIONS):
        px = 1 - x if fx else x
        py = 1 - y if fy else y
        pc = 1 - c if fc else c
        copies.append(pltpu.make_async_remote_copy(
            src_ref=grad_ref.at[2 * px + py, pc], dst_ref=land_ref.at[s], send_sem=send_sems.at[s],
            recv_sem=recv_sems.at[s], device_id=(px, py, pc), device_id_type=MESH))
    return copies


def _send_to_owners(name, grads):
    n, k = len(RELATIONS), len(grads)
    land_shapes = [(n,) + g.shape[2:] for g in grads]

    def body(*refs):
        ins, outs = refs[:2 * k], refs[2 * k:]
        for j in range(k):
            for cp in _owner_copies(ins[2 * j], ins[2 * j + 1], outs[4 * j], outs[4 * j + 1]):
                cp.start()
        outs[-1][...] = jnp.zeros_like(outs[-1])

    out_shape, out_specs, args, aliases = [], [], [], {}
    for j, (g, land_shape) in enumerate(zip(grads, land_shapes)):
        out_shape += [pltpu.SemaphoreType.DMA((n,)), pltpu.SemaphoreType.DMA((n,)), pltpu.HBM(g.shape, g.dtype),
                      pltpu.HBM(land_shape, g.dtype)]
        out_specs += [SEM, SEM, HBM, HBM]
        args += [pltpu.with_memory_space_constraint(g, pltpu.HBM),
                 pltpu.with_memory_space_constraint(lax.empty(land_shape, g.dtype), pltpu.HBM)]
        aliases.update({2 * j: 4 * j + 2, 2 * j + 1: 4 * j + 3})
    res = pl.pallas_call(
        body, name=name, out_shape=out_shape + [jax.ShapeDtypeStruct((8, 128), F32)],
        in_specs=[HBM] * (2 * k), out_specs=out_specs + [pl.BlockSpec(memory_space=pltpu.VMEM)],
        input_output_aliases=aliases, compiler_params=pltpu.CompilerParams(**SPLIT_PARAMS),
    )(*args)
    return [tuple(res[4 * j:4 * j + 4]) + (res[-1],) for j in range(k)]


def _await_owners(name, started, after):
    send_sems, recv_sems, g_thru, land_thru, _ = started

    def body(g_ref, land_ref, send_sems, recv_sems, after_ref, g_out, land_out):
        for cp in _owner_copies(g_ref, land_ref, send_sems, recv_sems):
            cp.wait_send()
            cp.wait_recv()

    return pl.pallas_call(
        body, name=name, out_shape=(pltpu.HBM(g_thru.shape, g_thru.dtype), pltpu.HBM(land_thru.shape, land_thru.dtype)),
        in_specs=(HBM, HBM, SEM, SEM, pl.BlockSpec(memory_space=pl.ANY)), out_specs=(HBM, HBM),
        input_output_aliases={0: 0, 1: 1}, compiler_params=pltpu.CompilerParams(**SPLIT_PARAMS),
    )(g_thru, land_thru, send_sems, recv_sems, after)


def _call(name, main, grid, in_specs, out_specs, out_shape, args, scratch=(), comm=None, prefetch=None, after=None):
    comm = comm or _NoComm()
    n_main = len(in_specs)
    if after is not None:
        in_specs = list(in_specs) + [pl.BlockSpec(after.shape, lambda *_: (0,) * after.ndim)]
        args = tuple(args) + (after,)
    counts = [len(in_specs), len(comm.inputs), len(out_shape), len(comm.out_shapes), len(scratch), len(comm.scratch)]
    n_steps = math.prod(grid)
    hosted = bool(comm.inputs)

    def body(*refs):
        if prefetch is not None:
            refs = refs[1:]
        parts, at = [], 0
        for n in counts:
            parts.append(refs[at:at + n])
            at += n
        ins, c_in, outs, c_out, scr, c_scr = parts
        ins = ins[:n_main]
        step = pl.program_id(0)
        for d in range(1, len(grid)):
            step = step * grid[d] + pl.program_id(d)
        if hosted:
            @pl.when(step == 0)
            def _():
                comm.start(c_in, c_out, c_scr)
        main(ins, outs, scr)
        if hosted:
            @pl.when(step == max(n_steps - 2, 0))
            def _():
                comm.mid(c_in, c_out, c_scr)

            @pl.when(step == n_steps - 1)
            def _():
                comm.finish(c_in, c_out, c_scr)

    all_in = list(in_specs) + [HBM] * counts[1]
    all_out = list(out_specs) + [HBM] * counts[3]
    all_scratch = list(scratch) + list(comm.scratch)
    shapes = list(out_shape) + list(comm.out_shapes)
    if prefetch is None:
        res = pl.pallas_call(body, name=name, grid=grid, in_specs=all_in, out_specs=all_out, out_shape=shapes,
                             scratch_shapes=all_scratch, compiler_params=_cparams())(*args, *comm.inputs)
    else:
        res = pl.pallas_call(
            body, name=name, out_shape=shapes,
            grid_spec=pltpu.PrefetchScalarGridSpec(num_scalar_prefetch=1, grid=grid, in_specs=all_in,
                                                   out_specs=all_out, scratch_shapes=all_scratch),
            compiler_params=_cparams())(prefetch, *args, *comm.inputs)
    return res[:counts[2]], res[counts[2]:]


def _rms_fwd(name, x, g, comm=None):
    T, D = x.shape
    tm = min(T, 512)

    def main(ins, outs, scr):
        x_ref, g_ref = ins
        xv = x_ref[...]
        r = lax.rsqrt(jnp.mean(xv * xv, axis=-1, keepdims=True) + RMS_EPS)
        outs[0][...] = (xv * r * g_ref[...]).astype(outs[0].dtype)

    tile = pl.BlockSpec((tm, D), lambda i: (i, 0))
    (h,), extra = _call(name, main, (T // tm,), [tile, pl.BlockSpec((1, D), lambda i: (0, 0))], [tile],
                        [jax.ShapeDtypeStruct((T, D), BF16)], (x, g), comm=comm)
    return h, extra


def _final_loss_epilogue(scale, out_scale):
    def ep(acc, ex, outs):
        res_ref, g_ref, t_ref = ex
        dx_ref, dxb_ref, dg_ref, loss_ref = outs
        n = acc.shape[-1]
        xv = res_ref[...] + scale * acc
        r = lax.rsqrt(jnp.mean(xv * xv, axis=-1, keepdims=True) + RMS_EPS)
        xhat = xv * r
        err = xhat * g_ref[...] - t_ref[...]

        @pl.when(pl.program_id(0) == 0)
        def _():
            dg_ref[...] = jnp.zeros_like(dg_ref)
            loss_ref[...] = jnp.zeros_like(loss_ref)

        loss_ref[...] += jnp.broadcast_to(jnp.sum(err * err) * (0.5 / n), loss_ref.shape)
        dy = err * (1.0 / n)
        dg_ref[...] += jnp.sum(dy * xhat, axis=0, keepdims=True)
        dxhat = dy * g_ref[...]
        dx = r * (dxhat - xhat * jnp.mean(dxhat * xhat, axis=-1, keepdims=True))
        dx_ref[...] = dx
        dxb_ref[...] = (out_scale * dx).astype(dxb_ref.dtype)
    return ep


def _mm_nstream(name, a, ws, w_sel, w_form, comps, out_dtypes, epilogue, cn, rows=1024, comm=None, after=None):
    T, K = a.shape
    N = ws[0].shape[1]
    rows = min(rows, T)
    assert N % cn == 0 and T % rows == 0
    n_w, n_c = len(ws), len(comps)

    def main(ins, outs, scr):
        a_ref = ins[0]
        w_refs = ins[1:1 + n_w]
        c_refs = ins[1 + n_w:1 + n_w + n_c]

        for r in range(T // rows):
            sl = slice(r * rows, (r + 1) * rows)
            a_blk = a_ref[sl, :]
            dots = [_dot(a_blk, w_ref[...], w_form) for w_ref in w_refs]
            res = epilogue(dots, [c_ref[sl, :] for c_ref in c_refs])
            for o_ref, o in zip(outs, res):
                o_ref[sl, :] = o.astype(o_ref.dtype)

    if w_form == "nt":
        w_specs = [pl.BlockSpec((None, cn, K), functools.partial(lambda j, s: (s, j, 0), s=s)) for s in w_sel]
    else:
        w_specs = [pl.BlockSpec((K, cn), lambda j: (0, j)) for _ in ws]
    chunk = pl.BlockSpec((T, cn), lambda j: (0, j))
    return _call(name, main, (N // cn,), [pl.BlockSpec((T, K), lambda j: (0, 0))] + w_specs + [chunk] * n_c,
                 [chunk] * len(out_dtypes), [jax.ShapeDtypeStruct((T, N), dt) for dt in out_dtypes],
                 (a, *ws, *comps), comm=comm, after=after)


def _mm_mstream(name, as_, ws, w_sel, w_form, extras, outs_desc, epilogue, tm=512, comm=None, after=None):
    T = as_[0].shape[0]
    tm = min(tm, T)
    n_a = len(as_)
    w_shapes = [w.shape[-2:] for w in ws]
    N = w_shapes[0][1] if w_form == "nn" else w_shapes[0][0]

    def main(ins, outs, scr):
        a_refs = ins[:n_a]
        w_refs = ins[n_a:2 * n_a]
        acc = None
        for a_ref, w_ref in zip(a_refs, w_refs):
            d = _dot(a_ref[...], w_ref[...], w_form)
            acc = d if acc is None else acc + d
        epilogue(acc, ins[2 * n_a:], outs)

    kind_spec = {"tile": pl.BlockSpec((tm, N), lambda i: (i, 0)), "vec": pl.BlockSpec((1, N), lambda i: (0, 0))}
    kind_shape = {"tile": (T, N), "vec": (1, N)}
    a_specs = [pl.BlockSpec((tm, a.shape[1]), lambda i: (i, 0)) for a in as_]
    w_specs = []
    for w, s in zip(ws, w_sel):
        if w.ndim == 3:
            w_specs.append(pl.BlockSpec((None,) + tuple(w.shape[1:]), functools.partial(lambda i, s: (s, 0, 0), s=s),
                                        pipeline_mode=pl.Buffered(1)))
        else:
            w_specs.append(pl.BlockSpec(tuple(w.shape), lambda i: (0, 0), pipeline_mode=pl.Buffered(1)))
    args = list(as_) + list(ws) + [e for e, _ in extras]
    return _call(name, main, (T // tm,), a_specs + w_specs + [kind_spec[k] for _, k in extras],
                 [kind_spec[k] for _, k in outs_desc],
                 [jax.ShapeDtypeStruct(kind_shape[k], dt) for dt, k in outs_desc], args, comm=comm, after=after)


def _residual_rms_epilogue(scale):
    def ep(acc, ex, outs):
        xv = ex[0][...] + scale * acc
        outs[0][...] = xv
        r = lax.rsqrt(jnp.mean(xv * xv, axis=-1, keepdims=True) + RMS_EPS)
        outs[1][...] = (xv * r * ex[1][...]).astype(outs[1].dtype)
    return ep


def _rms_bwd_epilogue(out_scale):
    def ep(acc, ex, outs):
        x_ref, g_ref, dres_ref = ex
        dx_ref, dxb_ref, dg_ref = outs
        xv = x_ref[...]
        r = lax.rsqrt(jnp.mean(xv * xv, axis=-1, keepdims=True) + RMS_EPS)
        xhat = xv * r

        @pl.when(pl.program_id(0) == 0)
        def _():
            dg_ref[...] = jnp.zeros_like(dg_ref)

        dg_ref[...] += jnp.sum(acc * xhat, axis=0, keepdims=True)
        dxhat = acc * g_ref[...]
        dx = r * (dxhat - xhat * jnp.mean(dxhat * xhat, axis=-1, keepdims=True)) + dres_ref[...]
        dx_ref[...] = dx
        dxb_ref[...] = (out_scale * dx).astype(dxb_ref.dtype)
    return ep


def _mm_tn(name, a, b, tmo, tno, out_dtype, tk=1024, comm=None, after=None):
    T, Ma = a.shape
    Nb = b.shape[1]
    tk = min(tk, T)
    nk = T // tk

    def main(ins, outs, scr):
        a_ref, b_ref = ins
        (acc_ref,) = scr
        k = pl.program_id(2)

        @pl.when(k == 0)
        def _():
            acc_ref[...] = jnp.zeros_like(acc_ref)

        acc_ref[...] += _dot(a_ref[...], b_ref[...], "tn")

        @pl.when(k == nk - 1)
        def _():
            outs[0][...] = acc_ref[...].astype(outs[0].dtype)

    (out,), extra = _call(
        name, main, (Ma // tmo, Nb // tno, nk),
        [pl.BlockSpec((tk, tmo), lambda i, j, k: (k, i)), pl.BlockSpec((tk, tno), lambda i, j, k: (k, j))],
        [pl.BlockSpec((tmo, tno), lambda i, j, k: (i, j))], [jax.ShapeDtypeStruct((Ma, Nb), out_dtype)],
        (a, b), scratch=[pltpu.VMEM((tmo, tno), F32)], comm=comm, after=after)
    return out, extra


def _swiglu_parts(g, u):
    s = _sigmoid(g)
    silu = g * s
    return [u * (s + silu * (1.0 - s)), silu, silu * u]


def _silu_mul_epilogue(dots, comps):
    g, u = dots
    return _swiglu_parts(g, u)


def _gate_parts_epilogue(dots, comps):
    (g,) = dots
    s = _sigmoid(g)
    silu = g * s
    return [s + silu * (1.0 - s), silu]


def _up_act_epilogue(dots, comps):
    (u,) = dots
    return [u * comps[0].astype(F32), u * comps[1].astype(F32)]


def _dact_epilogue(dots, comps):
    dact = dots[0].astype(BF16)
    return [dact * comps[0], dact * comps[1]]


def _identity_epilogue(dots, comps):
    return list(dots)


def _swap_halves(x):
    lane = lax.broadcasted_iota(jnp.int32, x.shape, 1)
    first = (lane % DK) < (DK // 2)
    return jnp.where(first, pltpu.roll(x, 128 - DK // 2, 1), pltpu.roll(x, DK // 2, 1))


def _rotary(t, cos, sin_signed):
    halves = []
    for p in range(QK_W // 128):
        th = t[:, 128 * p:128 * (p + 1)]
        halves.append(th * cos + _swap_halves(th) * sin_signed)
    return jnp.concatenate(halves, axis=1)


def _rotary_transposed(d, cos, sin_signed):
    halves = []
    for p in range(QK_W // 128):
        dh = d[:, 128 * p:128 * (p + 1)]
        halves.append(dh * cos + _swap_halves(dh * sin_signed))
    return jnp.concatenate(halves, axis=1)


def _log_sigmoid(x):
    return jnp.minimum(x, 0.0) - jnp.log(1.0 + jnp.exp(-jnp.abs(x)))


def _tri_sum(mask, x):
    tri = mask.astype(BF16)
    hi = x.astype(BF16)
    rest = x - hi.astype(F32)
    mid = rest.astype(BF16)
    lo = (rest - mid.astype(F32)).astype(BF16)
    return _dot(tri, hi, "nn") + _dot(tri, mid, "nn") + _dot(tri, lo, "nn")


def _attn_masks():
    row = lax.broadcasted_iota(jnp.int32, (SUPER, SUPER), 0)
    col = lax.broadcasted_iota(jnp.int32, (SUPER, SUPER), 1)
    same = (row // CHUNK) == (col // CHUNK)
    return row, col, same


def _group_inputs(grp, pr, cos, sin_signed, lg, wa2, ba):
    seg = lambda lo, width: pr[:, lo:lo + width].astype(F32)
    if grp == 0:
        q = _rotary(seg(C_RQ, QK_W), cos, sin_signed)
        k = _rotary(seg(C_RK, QK_W), cos, sin_signed) * (DK ** -0.5)
        v = pr[:, C_RV:C_RV + V_W]
        gate = seg(C_RG, V_W)
        pos = lax.broadcasted_iota(jnp.int32, (SUPER, QK_W), 0).astype(F32) + 1.0
        return q, k, v, gate, pos * lg, None, None
    q = seg(C_GQ, QK_W) * (DK ** -0.5)
    k = seg(C_GK, QK_W)
    v = pr[:, C_GV:C_GV + V_W]
    gate = seg(C_GG, V_W)
    glow = pr[:, C_GL:C_GL + GL_W]
    logit = _dot(glow.astype(BF16), wa2.astype(BF16), "nn") + ba
    la = _log_sigmoid(logit) * (1.0 / GATE_NORM)
    row, col, _ = _attn_masks()
    b_cum = _tri_sum(col <= row, la)
    return q, k, v, gate, b_cum, glow, logit


def _decay_factors(q, k, b_cum):
    c = b_cum[SUPER // 2 - 1:SUPER // 2, :]
    bl = b_cum[SUPER - 1:SUPER, :]
    e1 = jnp.exp(b_cum - c)
    e2 = jnp.exp(c - b_cum)
    e_b = jnp.exp(b_cum)
    e_l = jnp.exp(bl - b_cum)
    return dict(e1=e1, e2=e2, eb=e_b, el=e_l, ebl=jnp.exp(bl),
                qp=q * e1, qm=q * e2, kp=k * e1, km=k * e2, qs=q * e_b, kl=k * e_l)


def _state_block_mask():
    r = lax.broadcasted_iota(jnp.int32, (V_W, QK_W), 0)
    c = lax.broadcasted_iota(jnp.int32, (V_W, QK_W), 1)
    return (r // DV) == (c // DK)


def _attn_fwd(proj, cos, sin_signed, lg, wa2p, ba, gn_ret, gn_gla, x_res, w_out, g_next, comm=None):
    T = proj.shape[0]
    n_s = T // SUPER
    D = x_res.shape[1]

    def main(ins, outs, scr):
        pr_ref, cos_ref, sin_ref, lg_ref, wa2_ref, ba_ref, gr_ref, gg_ref, xres_ref, wout_ref, gnext_ref = ins
        o_ref, y_ref, st_ref, x_ref, h_ref = outs
        (s_ref,) = scr
        i = pl.program_id(0)

        @pl.when(i == 0)
        def _():
            s_ref[...] = jnp.zeros_like(s_ref)

        pr = pr_ref
        row, col, same = _attn_masks()
        m1 = col <= row
        m2 = jnp.logical_and(col > row, same)
        lane = lax.broadcasted_iota(jnp.int32, (1, QK_W), 1)
        blockmask = _state_block_mask()
        for grp in range(2):
            q, k, v, gate, b_cum, _, _ = _group_inputs(grp, pr, cos_ref[...], sin_ref[...], lg_ref[...],
                                                      wa2_ref[...], ba_ref[...])
            f = _decay_factors(q, k, b_cum)
            gn = gr_ref[...] if grp == 0 else gg_ref[...]
            s_prev = s_ref[grp]
            st_ref[0, grp] = s_prev
            o_inter = _dot(f["qs"].astype(BF16), s_prev.astype(BF16), "nt")
            kmb = f["km"].astype(BF16)
            kpb = f["kp"].astype(BF16)
            vb = v.astype(BF16)
            heads = [(lane // DK) == h for h in range(HEADS)]
            a1_all = _dot(jnp.concatenate([jnp.where(hm, f["qp"], 0.0).astype(BF16) for hm in heads], axis=0), kmb, "nt")
            a2_all = _dot(jnp.concatenate([jnp.where(hm, f["qm"], 0.0).astype(BF16) for hm in heads], axis=0), kpb, "nt")
            for h in range(HEADS):
                a1 = a1_all[h * SUPER:(h + 1) * SUPER]
                a2 = a2_all[h * SUPER:(h + 1) * SUPER]
                a = jnp.where(m1, a1, jnp.where(m2, a2, 0.0))
                lo = grp * V_W + h * DV
                o_h = _dot(a.astype(BF16), vb[:, h * DV:(h + 1) * DV], "nn") + o_inter[:, h * DV:(h + 1) * DV]
                o_ref[:, lo:lo + DV] = o_h
                r = lax.rsqrt(jnp.mean(o_h * o_h, axis=-1, keepdims=True) + RMS_EPS)
                gte = gate[:, h * DV:(h + 1) * DV]
                y = o_h * r * gn[:, h * DV:(h + 1) * DV] * (gte * _sigmoid(gte))
                y_ref[:, lo:lo + DV] = y.astype(y_ref.dtype)
            upd = _dot(vb, f["kl"].astype(BF16), "tn")
            s_ref[grp] = s_prev * f["ebl"] + jnp.where(blockmask, upd, 0.0)
        xv = xres_ref[...] + _dot(y_ref[...], wout_ref[...], "nn")
        x_ref[...] = xv
        r = lax.rsqrt(jnp.mean(xv * xv, axis=-1, keepdims=True) + RMS_EPS)
        h_ref[...] = (xv * r * gnext_ref[...]).astype(h_ref.dtype)

    const = lambda shape: pl.BlockSpec(shape, lambda i: tuple(0 for _ in shape))
    rows = lambda w: pl.BlockSpec((SUPER, w), lambda i: (i, 0))
    return _call(
        "attn_fwd", main, (n_s,),
        [rows(PROJ_W), rows(128), rows(128),
         const((1, QK_W)), const((GL_W, QK_W)), const((1, QK_W)), const((1, V_W)), const((1, V_W)),
         rows(D), const((2 * V_W, D)), const((1, D))],
        [rows(2 * V_W), rows(2 * V_W), pl.BlockSpec((1, 2, V_W, QK_W), lambda i: (i, 0, 0, 0)), rows(D), rows(D)],
        [jax.ShapeDtypeStruct((T, 2 * V_W), F32), jax.ShapeDtypeStruct((T, 2 * V_W), BF16),
         jax.ShapeDtypeStruct((n_s, 2, V_W, QK_W), F32), jax.ShapeDtypeStruct((T, D), F32),
         jax.ShapeDtypeStruct((T, D), BF16)],
        (proj, cos, sin_signed, lg, wa2p, ba, gn_ret, gn_gla, x_res, w_out, g_next),
        scratch=[pltpu.VMEM((2, V_W, QK_W), F32)], comm=comm)


def _attn_bwd(proj, cos, sin_signed, lg, wa2p, ba, gn_ret, gn_gla, o, dx, w_out, states, comm=None, after=None):
    T = proj.shape[0]
    n_s = T // SUPER
    D = dx.shape[1]

    def main(ins, outs, scr):
        pr_ref, cos_ref, sin_ref, lg_ref, wa2_ref, ba_ref, gr_ref, gg_ref, o_ref, dx_ref, wout_ref, st_ref = ins
        dp_ref, dgr_ref, dgg_ref, dba_ref, dwa_ref = outs
        (ds_ref, dy_ref) = scr
        i = pl.program_id(0)
        dy_ref[...] = _dot(dx_ref[...], wout_ref[...], "nt")

        @pl.when(i == 0)
        def _():
            ds_ref[...] = jnp.zeros_like(ds_ref)
            dgr_ref[...] = jnp.zeros_like(dgr_ref)
            dgg_ref[...] = jnp.zeros_like(dgg_ref)
            dba_ref[...] = jnp.zeros_like(dba_ref)
            dwa_ref[...] = jnp.zeros_like(dwa_ref)

        pr = pr_ref
        cos = cos_ref[...]
        sin_signed = sin_ref[...]
        row, col, same = _attn_masks()
        m1 = col <= row
        m2 = jnp.logical_and(col > row, same)
        m1t = row <= col
        m2t = jnp.logical_and(row > col, same)
        lane = lax.broadcasted_iota(jnp.int32, (1, QK_W), 1)
        blockmask = _state_block_mask()
        for grp in range(2):
            q, k, v, gate, b_cum, glow, logit = _group_inputs(grp, pr, cos, sin_signed, lg_ref[...],
                                                              wa2_ref[...], ba_ref[...])
            f = _decay_factors(q, k, b_cum)
            gn = gr_ref[...] if grp == 0 else gg_ref[...]
            dgn_ref = dgr_ref if grp == 0 else dgg_ref
            do_parts, dgate_parts, dgn_parts = [], [], []
            for h in range(HEADS):
                lo = grp * V_W + h * DV
                o_h = o_ref[:, lo:lo + DV]
                r = lax.rsqrt(jnp.mean(o_h * o_h, axis=-1, keepdims=True) + RMS_EPS)
                n = o_h * r
                gte = gate[:, h * DV:(h + 1) * DV]
                sg = _sigmoid(gte)
                dy_h = dy_ref[:, lo:lo + DV]
                gn_h = gn[:, h * DV:(h + 1) * DV]
                dgate_parts.append(dy_h * n * gn_h * (sg * (1.0 + gte * (1.0 - sg))))
                dz = dy_h * (gte * sg)
                dgn_parts.append(jnp.sum(dz * n, axis=0, keepdims=True))
                dn = dz * gn_h
                do_parts.append(r * (dn - n * jnp.mean(dn * n, axis=-1, keepdims=True)))
            dgn_ref[...] += jnp.concatenate(dgn_parts, axis=1)
            dgate = jnp.concatenate(dgate_parts, axis=1)
            do = jnp.concatenate(do_parts, axis=1)
            dob = do.astype(BF16)
            vb = v.astype(BF16)
            s_prev = st_ref[0, grp]
            ds_new = ds_ref[grp]
            dsb = ds_new.astype(BF16)
            qpb, qmb = f["qp"].astype(BF16), f["qm"].astype(BF16)
            kpb, kmb = f["kp"].astype(BF16), f["km"].astype(BF16)
            dv_parts = []
            heads = [(lane // DK) == h for h in range(HEADS)]
            qp_hs = [jnp.where(hm, f["qp"], 0.0).astype(BF16) for hm in heads]
            qm_hs = [jnp.where(hm, f["qm"], 0.0).astype(BF16) for hm in heads]
            kp_hs = [jnp.where(hm, f["kp"], 0.0).astype(BF16) for hm in heads]
            km_hs = [jnp.where(hm, f["km"], 0.0).astype(BF16) for hm in heads]
            km_stack, kp_stack = jnp.concatenate(km_hs, axis=0), jnp.concatenate(kp_hs, axis=0)
            at1_all = _dot(km_stack, qpb, "nt")
            at2_all = _dot(kp_stack, qmb, "nt")
            da1s, da2s, da1ts, da2ts = [], [], [], []
            for h in range(HEADS):
                at = jnp.where(m1t, at1_all[h * SUPER:(h + 1) * SUPER],
                               jnp.where(m2t, at2_all[h * SUPER:(h + 1) * SUPER], 0.0))
                do_h = dob[:, h * DV:(h + 1) * DV]
                v_h = vb[:, h * DV:(h + 1) * DV]
                dv_parts.append(_dot(at.astype(BF16), do_h, "nn"))
                da = _dot(do_h, v_h, "nt")
                dat = _dot(v_h, do_h, "nt")
                da1s.append(jnp.where(m1, da, 0.0).astype(BF16))
                da2s.append(jnp.where(m2, da, 0.0).astype(BF16))
                da1ts.append(jnp.where(m1t, dat, 0.0).astype(BF16))
                da2ts.append(jnp.where(m2t, dat, 0.0).astype(BF16))
            dqp = _dot(jnp.concatenate(da1s, axis=1), km_stack, "nn")
            dqm = _dot(jnp.concatenate(da2s, axis=1), kp_stack, "nn")
            dkm = _dot(jnp.concatenate(da1ts, axis=1), jnp.concatenate(qp_hs, axis=0), "nn")
            dkp = _dot(jnp.concatenate(da2ts, axis=1), jnp.concatenate(qm_hs, axis=0), "nn")
            klb = f["kl"].astype(BF16)
            qsb = f["qs"].astype(BF16)
            dqs = _dot(dob, s_prev.astype(BF16), "nn")
            dkl = _dot(vb, dsb, "nn")
            dv = jnp.concatenate(dv_parts, axis=1) + _dot(klb, dsb, "nt")
            ds_ref[grp] = ds_new * f["ebl"] + jnp.where(blockmask, _dot(dob, qsb, "tn"), 0.0)
            dq = dqp * f["e1"] + dqm * f["e2"] + dqs * f["eb"]
            dk = dkm * f["e2"] + dkp * f["e1"] + dkl * f["el"]
            if grp == 0:
                dq = _rotary_transposed(dq, cos, sin_signed)
                dk = _rotary_transposed(dk * (DK ** -0.5), cos, sin_signed)
                dp_ref[:, C_RQ:C_RQ + QK_W] = dq.astype(dp_ref.dtype)
                dp_ref[:, C_RK:C_RK + QK_W] = dk.astype(dp_ref.dtype)
                dp_ref[:, C_RV:C_RV + V_W] = dv.astype(dp_ref.dtype)
                dp_ref[:, C_RG:C_RG + V_W] = dgate.astype(dp_ref.dtype)
            else:
                dkl_kl = dkl * klb.astype(F32)
                db = (dqp * qpb.astype(F32) - dkm * kmb.astype(F32) - dqm * qmb.astype(F32)
                      + dkp * kpb.astype(F32) + dqs * qsb.astype(F32) - dkl_kl)
                last = (jnp.sum(dkl_kl, axis=0, keepdims=True)
                        + f["ebl"] * jnp.sum(s_prev * ds_new, axis=0, keepdims=True))
                rowq = lax.broadcasted_iota(jnp.int32, (SUPER, QK_W), 0)
                db = db + jnp.where(rowq == SUPER - 1, last, 0.0)
                dla = _tri_sum(col >= row, db)
                dlogit = dla * (1.0 / GATE_NORM) * (1.0 - _sigmoid(logit))
                dlb = dlogit.astype(BF16)
                dglow = _dot(dlb, wa2_ref[...].astype(BF16), "nt")
                dwa_ref[...] += _dot(glow.astype(BF16), dlb, "tn")
                dba_ref[...] += jnp.sum(dlogit, axis=0, keepdims=True)
                dp_ref[:, C_GQ:C_GQ + QK_W] = (dq * (DK ** -0.5)).astype(dp_ref.dtype)
                dp_ref[:, C_GK:C_GK + QK_W] = dk.astype(dp_ref.dtype)
                dp_ref[:, C_GV:C_GV + V_W] = dv.astype(dp_ref.dtype)
                dp_ref[:, C_GG:C_GG + V_W] = dgate.astype(dp_ref.dtype)
                dp_ref[:, C_GL:C_GL + GL_W] = dglow.astype(dp_ref.dtype)

    rev = lambda i: n_s - 1 - i
    const = lambda shape: pl.BlockSpec(shape, lambda i: tuple(0 for _ in shape))
    return _call(
        "attn_bwd", main, (n_s,),
        [pl.BlockSpec((SUPER, PROJ_W), lambda i: (rev(i), 0)),
         pl.BlockSpec((SUPER, 128), lambda i: (rev(i), 0)), pl.BlockSpec((SUPER, 128), lambda i: (rev(i), 0)),
         const((1, QK_W)), const((GL_W, QK_W)), const((1, QK_W)), const((1, V_W)), const((1, V_W)),
         pl.BlockSpec((SUPER, 2 * V_W), lambda i: (rev(i), 0)),
         pl.BlockSpec((SUPER, D), lambda i: (rev(i), 0)), const((2 * V_W, D)),
         pl.BlockSpec((1, 2, V_W, QK_W), lambda i: (rev(i), 0, 0, 0))],
        [pl.BlockSpec((SUPER, PROJ_W), lambda i: (rev(i), 0)),
         const((1, V_W)), const((1, V_W)), const((1, QK_W)), const((GL_W, QK_W))],
        [jax.ShapeDtypeStruct((T, PROJ_W), BF16),
         jax.ShapeDtypeStruct((1, V_W), F32), jax.ShapeDtypeStruct((1, V_W), F32),
         jax.ShapeDtypeStruct((1, QK_W), F32), jax.ShapeDtypeStruct((GL_W, QK_W), F32)],
        (proj, cos, sin_signed, lg, wa2p, ba, gn_ret, gn_gla, o, dx, w_out, states),
        scratch=[pltpu.VMEM((2, V_W, QK_W), F32), pltpu.VMEM((SUPER, 2 * V_W), F32)], comm=comm, after=after)


def _rotary_tables(T):
    half = DK // 2
    inv = ROPE_BASE ** (-jnp.arange(half, dtype=F32) * 2.0 / DK)
    ang = jnp.arange(T, dtype=F32)[:, None] * inv[None, :]
    cos, sin = jnp.cos(ang), jnp.sin(ang)
    cos_head = jnp.concatenate([cos, cos], axis=1)
    sin_head = jnp.concatenate([-sin, sin], axis=1)
    return jnp.tile(cos_head, (1, 128 // DK)), jnp.tile(sin_head, (1, 128 // DK))


def _sum_devices(name, gathered, m_per):
    def body(g_ref, o_ref):
        acc = g_ref[0:m_per, :]
        for k in range(1, N_DEV):
            acc = acc + g_ref[k * m_per:(k + 1) * m_per, :]
        o_ref[...] = acc

    return pl.pallas_call(body, name=name, out_shape=jax.ShapeDtypeStruct((m_per, 128), F32))(gathered)


def _owner_sums(name, items, owner, comm=None):
    counts = [1 + len(landed) for _, landed in items]

    def main(ins, outs, scr):
        at = 0
        for o_ref, n in zip(outs, counts):
            acc = ins[at][...].astype(F32)
            for l_ref in ins[at + 1:at + n]:
                for j in range(l_ref.shape[0]):
                    acc = acc + l_ref[j].astype(F32)
            o_ref[...] = acc
            at += n

    once = pl.Buffered(1)
    in_specs, out_specs, out_shape, args = [], [], [], []
    for grad, landed in items:
        R, C = grad.shape[-2:]
        in_specs.append(pl.BlockSpec((None, None, R, C), lambda i, s: (s[0], s[1], 0, 0), pipeline_mode=once))
        in_specs += [pl.BlockSpec(tuple(l.shape), lambda i, s: (0, 0, 0), pipeline_mode=once) for l in landed]
        out_specs.append(pl.BlockSpec((R, C), lambda i, s: (0, 0)))
        out_shape.append(jax.ShapeDtypeStruct((R, C), F32))
        args += [grad, *landed]
    return _call(name, main, (1,), in_specs, out_specs, out_shape, args, comm=comm, prefetch=owner)


def _adamw_group(name, items, n_blocks, comm=None):
    n = len(items)

    def main(ins, outs, scr):
        for p in range(n):
            g_ref, w_ref, m_ref, v_ref = ins[4 * p:4 * p + 4]
            d_ref, nm_ref, nv_ref = outs[3 * p:3 * p + 3]
            gv = g_ref[...]
            nm = ADAM_B1 * m_ref[...] + (1.0 - ADAM_B1) * gv
            nv = ADAM_B2 * v_ref[...] + (1.0 - ADAM_B2) * (gv * gv)
            m_hat = nm / (1.0 - ADAM_B1 ** ADAM_STEP)
            v_hat = nv / (1.0 - ADAM_B2 ** ADAM_STEP)
            d_ref[...] = -ADAM_LR * (m_hat / (jnp.sqrt(v_hat) + ADAM_EPS) + ADAM_WD * w_ref[...])
            nm_ref[...] = nm
            nv_ref[...] = nv

    in_specs, out_specs, out_shape, args = [], [], [], []
    for item in items:
        shape = item[1].shape
        assert shape[0] % n_blocks == 0
        spec = pl.BlockSpec((shape[0] // n_blocks,) + shape[1:], lambda i, nd=len(shape): (i,) + (0,) * (nd - 1))
        in_specs += [spec] * 4
        out_specs += [spec] * 3
        out_shape += [jax.ShapeDtypeStruct(shape, F32)] * 3
        args += list(item)
    outs, extra = _call(name, main, (n_blocks,), in_specs, out_specs, out_shape, args, comm=comm)
    return [tuple(outs[3 * p:3 * p + 3]) for p in range(n)], extra


SMALL_ORDER = ("ffn1", "mix", "ffn2", "final", "ret", "gla", "b_a")


def kernel(x, ffn1_norm_g, ffn1_w_gate, ffn1_w_up, ffn1_w_down, mix_norm_g, w_in, ret_norm_g, gla_w_a2, gla_b_a, gla_norm_g, w_out, ffn2_norm_g, ffn2_w_gate, ffn2_w_up, ffn2_w_down, final_norm_g, loss_target, m_ffn1_norm_g, m_ffn1_w_gate, m_ffn1_w_up, m_ffn1_w_down, m_mix_norm_g, m_w_in, m_ret_norm_g, m_gla_w_a2, m_gla_b_a, m_gla_norm_g, m_w_out, m_ffn2_norm_g, m_ffn2_w_gate, m_ffn2_w_up, m_ffn2_w_down, m_final_norm_g, v_ffn1_norm_g, v_ffn1_w_gate, v_ffn1_w_up, v_ffn1_w_down, v_mix_norm_g, v_w_in, v_ret_norm_g, v_gla_w_a2, v_gla_b_a, v_gla_norm_g, v_w_out, v_ffn2_norm_g, v_ffn2_w_gate, v_ffn2_w_up, v_ffn2_w_down, v_final_norm_g):
    xi, yi, ci = _coords()
    dev = 4 * xi + 2 * yi + ci
    owner = jnp.stack([2 * xi + yi, ci]).astype(jnp.int32)

    x0, target = x[0], loss_target[0]
    T, D = x0.shape
    fb = ffn1_w_gate.shape[2]
    ib = w_in.shape[2]
    ab = gla_w_a2.shape[2]
    F = N_DEV * fb
    cos, sin_signed = _rotary_tables(T)
    lg = jnp.repeat(jnp.log(1.0 - 2.0 ** (-5.0 - jnp.arange(HEADS, dtype=F32))), DK)[None, :]
    g_final = final_norm_g.reshape(1, D)

    g1_loc = ffn1_w_gate[0].T[None].astype(BF16)
    u1_loc = ffn1_w_up[0].T[None].astype(BF16)
    d1_loc = ffn1_w_down.astype(BF16)
    g2_loc = ffn2_w_gate[0].T[None].astype(BF16)
    u2_loc = ffn2_w_up[0].T[None].astype(BF16)
    d2_loc = ffn2_w_down.astype(BF16)
    in_loc = w_in[0].T.astype(BF16)
    out_loc = w_out[0].astype(BF16)

    h1, (g1,) = _rms_fwd("ffn1_rms", x0, ffn1_norm_g, comm=_AllGather([g1_loc], ["stack"]))
    g1 = g1.reshape(1, F, D)
    (dsl1, sl1), (u1,) = _mm_nstream("ffn1_gate", h1, [g1], [0], "nt", [], [BF16, BF16], _gate_parts_epilogue, cn=256,
                                     comm=_AllGather([u1_loc], ["stack"]))
    u1 = u1.reshape(1, F, D)
    (dsu1, act1), (d1,) = _mm_nstream("ffn1_up", h1, [u1], [0], "nt", [dsl1, sl1], [BF16, BF16],
                                      _up_act_epilogue, cn=256, comm=_AllGather([d1_loc], ["stack"]))
    d1 = d1.reshape(1, F, D)
    f32_tile, bf16_tile, f32_vec = (F32, "tile"), (BF16, "tile"), (F32, "vec")
    (x1, h2), (in_all, a_all) = _mm_mstream(
        "ffn1_down", [act1], [d1], [0], "nn", [(x0, "tile"), (mix_norm_g, "vec")], [f32_tile, bf16_tile],
        _residual_rms_epilogue(0.5), comm=_AllGather([in_loc, gla_w_a2[0]], ["plain", "plain"]))
    w_in_t = jnp.pad(in_all.reshape(1, N_DEV * ib, D), ((0, 0), (0, PROJ_W - N_DEV * ib), (0, 0)))
    wa2 = jnp.transpose(a_all, (1, 0, 2)).reshape(GATE_RANK, N_DEV * ab)
    wa2p = jnp.pad(wa2, ((0, GL_W - GATE_RANK), (0, 0)))

    (proj,), (g2, out_all) = _mm_nstream("mix_proj", h2, [w_in_t], [0], "nt", [], [BF16], _identity_epilogue, cn=640,
                                         comm=_AllGather([g2_loc, out_loc], ["stack", "plain"]))
    w_out_full = out_all.reshape(D, D)
    (o, ymix, states, x2, h3), (u2,) = _attn_fwd(proj, cos, sin_signed, lg, wa2p, gla_b_a, ret_norm_g, gla_norm_g,
                                                 x1, w_out_full, ffn2_norm_g, comm=_AllGather([u2_loc], ["stack"]))
    g2, u2 = g2.reshape(1, F, D), u2.reshape(1, F, D)

    (dsu2, sl2, act2), (d2,) = _mm_nstream(
        "ffn2_up", h3, [g2, u2], [0, 0], "nt", [], [BF16, BF16, BF16], _silu_mul_epilogue, cn=256,
        comm=_AllGather([d2_loc], ["stack"]))
    d2 = d2.reshape(1, F, D)
    (dx3, dy3b, d_final, loss), _ = _mm_mstream(
        "ffn2_down", [act2], [d2], [0], "nn", [(x2, "tile"), (g_final, "vec"), (target, "tile")],
        [f32_tile, bf16_tile, f32_vec, f32_vec], _final_loss_epilogue(0.5, 0.5))

    sent = {}

    def send(**grads):
        started = _send_to_owners("send_" + "_".join(grads), list(grads.values()))
        sent.update(zip(grads, started))
        return started[0][4]

    dwd2, _ = _mm_tn("ffn2b_dwd", act2, dy3b, F // 2, D, BF16)
    (dgate2, dup2), _ = _mm_nstream("ffn2b_dact", dy3b, [d2], [0], "nt", [dsu2, sl2], [BF16, BF16],
                                    _dact_epilogue, cn=256)
    dwg2, _ = _mm_tn("ffn2b_dwg", dgate2, h3, F // 2, D, BF16)
    dwu2, _ = _mm_tn("ffn2b_dwu", dup2, h3, F // 2, D, BF16)
    tok = send(wd2=dwd2.reshape(4, 2, fb, D), wg2=dwg2.reshape(4, 2, fb, D), wu2=dwu2.reshape(4, 2, fb, D))
    rms_outs = [f32_tile, bf16_tile, f32_vec]
    (dx2, dx2b, d_g2), _ = _mm_mstream(
        "ffn2b_dh", [dgate2, dup2], [g2, u2], [0, 0], "nn", [(x2, "tile"), (ffn2_norm_g, "vec"), (dx3, "tile")],
        rms_outs, _rms_bwd_epilogue(1.0), after=tok)

    dwout, _ = _mm_tn("mixb_dwout", ymix, dx2b, D, D, BF16)
    (dproj, d_ret, d_gla, d_ba, d_wa2p), _ = _attn_bwd(
        proj, cos, sin_signed, lg, wa2p, gla_b_a, ret_norm_g, gla_norm_g, o, dx2b, w_out_full, states)
    dwin_t, _ = _mm_tn("mixb_dwin", dproj, h2, 640, D, BF16, tk=2048)
    tok = send(wout=dwout.reshape(4, 2, D // N_DEV, D), win=dwin_t[:N_DEV * ib].reshape(4, 2, ib, D))
    (dx1, dy1b, d_gmix), _ = _mm_mstream(
        "mixb_dh", [dproj], [w_in_t], [0], "nn", [(x1, "tile"), (mix_norm_g, "vec"), (dx2, "tile")],
        rms_outs, _rms_bwd_epilogue(0.5), after=tok)

    dwd1, _ = _mm_tn("ffn1b_dwd", act1, dy1b, F // 2, D, BF16)
    tok = send(wd1=dwd1.reshape(4, 2, fb, D))
    (dgate1, dup1), _ = _mm_nstream("ffn1b_dact", dy1b, [d1], [0], "nt", [dsu1, sl1], [BF16, BF16],
                                    _dact_epilogue, cn=256, after=tok)
    dwg1, _ = _mm_tn("ffn1b_dwg", dgate1, h1, F // 2, D, BF16)
    tok = send(wg1=dwg1.reshape(4, 2, fb, D))
    dwu1, _ = _mm_tn("ffn1b_dwu", dup1, h1, F // 2, D, BF16, after=tok)
    tok = send(wu1=dwu1.reshape(4, 2, fb, D))
    (dx0, _, d_g1), _ = _mm_mstream(
        "ffn1b_dh", [dgate1, dup1], [g1, u1], [0, 0], "nn", [(x0, "tile"), (ffn1_norm_g, "vec"), (dx1, "tile")],
        rms_outs, _rms_bwd_epilogue(1.0), after=tok)

    small = dict(ffn1=d_g1, mix=d_gmix, ffn2=d_g2, final=d_final, ret=d_ret, gla=d_gla, b_a=d_ba)
    flat = jnp.concatenate([small[k].reshape(-1) for k in SMALL_ORDER]
                           + [d_wa2p[:GATE_RANK].reshape(-1), loss[0, :128]])
    rows = -(-flat.shape[0] // 128)
    rows = -(-rows // 8) * 8
    packed = jnp.pad(flat, (0, rows * 128 - flat.shape[0])).reshape(rows, 128)

    transposed = ("ffn1_w_gate", "ffn1_w_up", "ffn2_w_gate", "ffn2_w_up")

    def to_2d(nm, a):
        if nm == "w_in":
            return jnp.transpose(a, (2, 0, 1))
        if nm in transposed:
            return a[0].T
        return a.reshape((1, a.shape[0]) if a.ndim == 1 else a.shape[-2:])

    def from_2d(nm, a):
        if nm == "w_in":
            return jnp.transpose(a, (1, 2, 0))
        return a.T[None] if nm in transposed else a.reshape(params[nm][0].shape)

    def arrived(nm, after):
        grad, landed = _await_owners("await_" + nm, sent[nm], after)
        return grad, [landed]

    sums_a, (gathered,) = _owner_sums(
        "sum_a", [arrived(nm, dx0) for nm in ("wg2", "wu2", "wd2", "win", "wout")], owner,
        comm=_AllGather([packed], ["plain"]))
    params = dict(
        ffn2_w_gate=(ffn2_w_gate, m_ffn2_w_gate, v_ffn2_w_gate), ffn2_w_up=(ffn2_w_up, m_ffn2_w_up, v_ffn2_w_up),
        ffn2_w_down=(ffn2_w_down, m_ffn2_w_down, v_ffn2_w_down), w_in=(w_in, m_w_in, v_w_in),
        w_out=(w_out, m_w_out, v_w_out), ffn1_w_gate=(ffn1_w_gate, m_ffn1_w_gate, v_ffn1_w_gate),
        ffn1_w_up=(ffn1_w_up, m_ffn1_w_up, v_ffn1_w_up), ffn1_w_down=(ffn1_w_down, m_ffn1_w_down, v_ffn1_w_down),
        ffn1_norm_g=(ffn1_norm_g, m_ffn1_norm_g, v_ffn1_norm_g), mix_norm_g=(mix_norm_g, m_mix_norm_g, v_mix_norm_g),
        ret_norm_g=(ret_norm_g, m_ret_norm_g, v_ret_norm_g), gla_w_a2=(gla_w_a2, m_gla_w_a2, v_gla_w_a2),
        gla_b_a=(gla_b_a, m_gla_b_a, v_gla_b_a), gla_norm_g=(gla_norm_g, m_gla_norm_g, v_gla_norm_g),
        ffn2_norm_g=(ffn2_norm_g, m_ffn2_norm_g, v_ffn2_norm_g), final_norm_g=(final_norm_g, m_final_norm_g, v_final_norm_g))
    grads, updates = {}, {}

    def run_adam(name, names, grad_2d, n_blocks):
        items = [(grad_2d[nm],) + tuple(to_2d(nm, a) for a in params[nm]) for nm in names]
        res, _ = _adamw_group(name, items, n_blocks)
        for nm, r in zip(names, res):
            grads[nm] = from_2d(nm, grad_2d[nm])
            updates[nm] = tuple(from_2d(nm, a) for a in r)
        return res

    grads_a = {"ffn2_w_gate": sums_a[0], "ffn2_w_up": sums_a[1], "ffn2_w_down": sums_a[2], "w_out": sums_a[4]}
    run_adam("adamw_w_in", ["w_in"], {"w_in": sums_a[3][:, None, :]}, 2)
    done_a = run_adam("adamw_a", list(grads_a), grads_a, 4)[0][0]
    sums_b, _ = _owner_sums("sum_b", [arrived(nm, done_a) for nm in ("wg1", "wu1", "wd1")], owner)
    grads_b = {"ffn1_w_gate": sums_b[0], "ffn1_w_up": sums_b[1], "ffn1_w_down": sums_b[2]}
    run_adam("adamw_b", list(grads_b), grads_b, 4)

    total = _sum_devices("sum_small", gathered.reshape(N_DEV * rows, 128), rows).reshape(-1)
    sizes = [small[k].size for k in SMALL_ORDER] + [GATE_RANK * QK_W, 128]
    offs = [0]
    for s in sizes:
        offs.append(offs[-1] + s)
    pieces = [total[offs[i]:offs[i + 1]] for i in range(len(sizes))]
    g_small = {k: pieces[i].reshape(small[k].shape) for i, k in enumerate(SMALL_ORDER)}
    g_wa2_full = pieces[len(SMALL_ORDER)].reshape(GATE_RANK, QK_W)
    g_wa2 = lax.dynamic_slice(g_wa2_full, (0, dev * ab), (GATE_RANK, ab))
    loss_total = pieces[len(SMALL_ORDER) + 1][0]

    small_grads = {"ffn1_norm_g": g_small["ffn1"], "mix_norm_g": g_small["mix"], "ret_norm_g": g_small["ret"],
                   "gla_w_a2": g_wa2, "gla_b_a": g_small["b_a"], "gla_norm_g": g_small["gla"],
                   "ffn2_norm_g": g_small["ffn2"], "final_norm_g": g_small["final"]}
    run_adam("adamw_small", list(small_grads), small_grads, 1)

    order = ("ffn1_norm_g", "ffn1_w_gate", "ffn1_w_up", "ffn1_w_down", "mix_norm_g", "w_in", "ret_norm_g", "gla_w_a2",
             "gla_b_a", "gla_norm_g", "w_out", "ffn2_norm_g", "ffn2_w_gate", "ffn2_w_up", "ffn2_w_down", "final_norm_g")
    return (loss_total, dx0[None], *[grads[nm] for nm in order], *[updates[nm][0] for nm in order],
            *[updates[nm][1] for nm in order], *[updates[nm][2] for nm in order])
```

```python
import functools
import math

import jax
import jax.numpy as jnp
from jax import lax
from jax.experimental import pallas as pl
from jax.experimental.pallas import tpu as pltpu

F32 = jnp.float32
BF16 = jnp.bfloat16
MESH = pl.DeviceIdType.MESH
HBM = pl.BlockSpec(memory_space=pltpu.HBM)

N_DEV = 8
RMS_EPS = 1e-6
ROPE_BASE = 10000.0
HEADS = 4
DK = 64
DV = 128
QK_W = HEADS * DK
V_W = HEADS * DV
GATE_RANK = 16
GATE_NORM = 16.0
CHUNK = 64
SUPER = 256
PROJ_W = 3200
C_RQ, C_RK, C_RV, C_RG, C_GQ, C_GK, C_GV, C_GG, C_GL = 0, 256, 512, 1024, 1536, 1792, 2048, 2560, 3072
GL_W = PROJ_W - C_GL
ADAM_LR, ADAM_B1, ADAM_B2, ADAM_EPS, ADAM_WD, ADAM_STEP = 0.001, 0.9, 0.999, 1e-08, 0.01, 10
VMEM_LIMIT_V7X = 52 * 1024 * 1024


def _cparams(**kw):
    return pltpu.CompilerParams(vmem_limit_bytes=VMEM_LIMIT_V7X, **kw)


def _dot(a, b, form, precision=None):
    dims = {"nn": (((1,), (0,)), ((), ())), "nt": (((1,), (1,)), ((), ())), "tn": (((0,), (0,)), ((), ()))}[form]
    return lax.dot_general(a, b, dims, preferred_element_type=F32, precision=precision)


def _sigmoid(x):
    return 1.0 / (1.0 + jnp.exp(-x))


def _coords():
    return lax.axis_index("x"), lax.axis_index("y"), lax.axis_index("c")


class _NoComm:
    inputs, out_shapes, scratch = (), (), ()


class _AllGather:
    def __init__(self, arrays, kinds):
        self.inputs = tuple(arrays)
        self.kinds = tuple(kinds)
        n = len(arrays)
        self.out_shapes = tuple(
            jax.ShapeDtypeStruct((a.shape[0], N_DEV) + a.shape[1:] if k == "stack" else (N_DEV,) + a.shape, a.dtype)
            for a, k in zip(arrays, kinds))
        self.scratch = (pltpu.SemaphoreType.DMA((n, 7)), pltpu.SemaphoreType.DMA((n, 7)),
                        pltpu.SemaphoreType.DMA((n,)))

    def _ctx(self, srcs, outs, sems):
        send_sems, recv_sems, local_sems = sems
        x, y, c = _coords()
        me, sibling = (x, y, c), (x, y, 1 - c)
        chips = [(1 - x, y), (x, 1 - y), (1 - x, 1 - y)]

        def blk(m, dev):
            k = 4 * dev[0] + 2 * dev[1] + dev[2]
            return outs[m].at[:, k] if self.kinds[m] == "stack" else outs[m].at[k]

        def copy(m, s, block, to, src=None):
            return pltpu.make_async_remote_copy(
                src_ref=blk(m, block) if src is None else src, dst_ref=blk(m, block),
                send_sem=send_sems.at[m, s], recv_sem=recv_sems.at[m, s], device_id=to, device_id_type=MESH)

        def mine(m):
            return pltpu.make_async_copy(srcs[m], blk(m, me), local_sems.at[m])

        def first(m):
            return [copy(m, 0, me, sibling, src=srcs[m])] + [
                copy(m, 1 + j, me, (*chip, c), src=srcs[m]) for j, chip in enumerate(chips)]

        return me, sibling, chips, c, copy, mine, first

    def start(self, srcs, outs, sems):
        me, sibling, chips, c, copy, mine, first = self._ctx(srcs, outs, sems)
        for m in range(len(srcs)):
            mine(m).start()
            for cp in first(m):
                cp.start()

    def mid(self, srcs, outs, sems):
        me, sibling, chips, c, copy, mine, first = self._ctx(srcs, outs, sems)
        for j, chip in enumerate(chips):
            for m in range(len(srcs)):
                copy(m, 1 + j, (*chip, c), me).wait_recv()
                copy(m, 4 + j, (*chip, c), sibling).start()

    def finish(self, srcs, outs, sems):
        me, sibling, chips, c, copy, mine, first = self._ctx(srcs, outs, sems)
        for m in range(len(srcs)):
            copy(m, 0, sibling, me).wait_recv()
            for j, chip in enumerate(chips):
                copy(m, 4 + j, (*chip, 1 - c), me).wait_recv()
            for cp in first(m):
                cp.wait_send()
            for j, chip in enumerate(chips):
                copy(m, 4 + j, (*chip, c), sibling).wait_send()
            mine(m).wait()


RELATIONS = ((0, 0, 1), (1, 0, 0), (0, 1, 0), (1, 1, 0), (1, 0, 1), (0, 1, 1), (1, 1, 1))
SEM = pl.BlockSpec(memory_space=pltpu.SEMAPHORE)
SPLIT_PARAMS = dict(has_side_effects=pltpu.SideEffectType.DATAFLOW_SIDE_EFFECTING)


def _owner_copies(grad_ref, land_ref, send_sems, recv_sems):
    x, y, c = _coords()
    copies = []
    for s, (fx, fy, fc) in enumerate(RELATIONS):
        px = 1 - x if fx else x
        py = 1 - y if fy else y
        pc = 1 - c if fc else c
        copies.append(pltpu.make_async_remote_copy(
            src_ref=grad_ref.at[2 * px + py, pc], dst_ref=land_ref.at[s], send_sem=send_sems.at[s],
            recv_sem=recv_sems.at[s], device_id=(px, py, pc), device_id_type=MESH))
    return copies


def _send_to_owners(name, grads):
    n, k = len(RELATIONS), len(grads)
    land_shapes = [(n,) + g.shape[2:] for g in grads]

    def body(*refs):
        ins, outs = refs[:2 * k], refs[2 * k:]
        for j in range(k):
            for cp in _owner_copies(ins[2 * j], ins[2 * j + 1], outs[4 * j], outs[4 * j + 1]):
                cp.start()
        outs[-1][...] = jnp.zeros_like(outs[-1])

    out_shape, out_specs, args, aliases = [], [], [], {}
    for j, (g, land_shape) in enumerate(zip(grads, land_shapes)):
        out_shape += [pltpu.SemaphoreType.DMA((n,)), pltpu.SemaphoreType.DMA((n,)), pltpu.HBM(g.shape, g.dtype),
                      pltpu.HBM(land_shape, g.dtype)]
        out_specs += [SEM, SEM, HBM, HBM]
        args += [pltpu.with_memory_space_constraint(g, pltpu.HBM),
                 pltpu.with_memory_space_constraint(lax.empty(land_shape, g.dtype), pltpu.HBM)]
        aliases.update({2 * j: 4 * j + 2, 2 * j + 1: 4 * j + 3})
    res = pl.pallas_call(
        body, name=name, out_shape=out_shape + [jax.ShapeDtypeStruct((8, 128), F32)],
        in_specs=[HBM] * (2 * k), out_specs=out_specs + [pl.BlockSpec(memory_space=pltpu.VMEM)],
        input_output_aliases=aliases, compiler_params=pltpu.CompilerParams(**SPLIT_PARAMS),
    )(*args)
    return [tuple(res[4 * j:4 * j + 4]) + (res[-1],) for j in range(k)]


def _await_owners(name, started, after):
    send_sems, recv_sems, g_thru, land_thru, _ = started

    def body(g_ref, land_ref, send_sems, recv_sems, after_ref, g_out, land_out):
        for cp in _owner_copies(g_ref, land_ref, send_sems, recv_sems):
            cp.wait_send()
            cp.wait_recv()

    return pl.pallas_call(
        body, name=name, out_shape=(pltpu.HBM(g_thru.shape, g_thru.dtype), pltpu.HBM(land_thru.shape, land_thru.dtype)),
        in_specs=(HBM, HBM, SEM, SEM, pl.BlockSpec(memory_space=pl.ANY)), out_specs=(HBM, HBM),
        input_output_aliases={0: 0, 1: 1}, compiler_params=pltpu.CompilerParams(**SPLIT_PARAMS),
    )(g_thru, land_thru, send_sems, recv_sems, after)


def _call(name, main, grid, in_specs, out_specs, out_shape, args, scratch=(), comm=None, prefetch=None, after=None):
    comm = comm or _NoComm()
    n_main = len(in_specs)
    if after is not None:
        in_specs = list(in_specs) + [pl.BlockSpec(after.shape, lambda *_: (0,) * after.ndim)]
        args = tuple(args) + (after,)
    counts = [len(in_specs), len(comm.inputs), len(out_shape), len(comm.out_shapes), len(scratch), len(comm.scratch)]
    n_steps = math.prod(grid)
    hosted = bool(comm.inputs)

    def body(*refs):
        if prefetch is not None:
            refs = refs[1:]
        parts, at = [], 0
        for n in counts:
            parts.append(refs[at:at + n])
            at += n
        ins, c_in, outs, c_out, scr, c_scr = parts
        ins = ins[:n_main]
        step = pl.program_id(0)
        for d in range(1, len(grid)):
            step = step * grid[d] + pl.program_id(d)
        if hosted:
            @pl.when(step == 0)
            def _():
                comm.start(c_in, c_out, c_scr)
        main(ins, outs, scr)
        if hosted:
            @pl.when(step == max(n_steps - 2, 0))
            def _():
                comm.mid(c_in, c_out, c_scr)

            @pl.when(step == n_steps - 1)
            def _():
                comm.finish(c_in, c_out, c_scr)

    all_in = list(in_specs) + [HBM] * counts[1]
    all_out = list(out_specs) + [HBM] * counts[3]
    all_scratch = list(scratch) + list(comm.scratch)
    shapes = list(out_shape) + list(comm.out_shapes)
    if prefetch is None:
        res = pl.pallas_call(body, name=name, grid=grid, in_specs=all_in, out_specs=all_out, out_shape=shapes,
                             scratch_shapes=all_scratch, compiler_params=_cparams())(*args, *comm.inputs)
    else:
        res = pl.pallas_call(
            body, name=name, out_shape=shapes,
            grid_spec=pltpu.PrefetchScalarGridSpec(num_scalar_prefetch=1, grid=grid, in_specs=all_in,
                                                   out_specs=all_out, scratch_shapes=all_scratch),
            compiler_params=_cparams())(prefetch, *args, *comm.inputs)
    return res[:counts[2]], res[counts[2]:]


def _rms_fwd(name, x, g, comm=None):
    T, D = x.shape
    tm = min(T, 512)

    def main(ins, outs, scr):
        x_ref, g_ref = ins
        xv = x_ref[...]
        r = lax.rsqrt(jnp.mean(xv * xv, axis=-1, keepdims=True) + RMS_EPS)
        outs[0][...] = (xv * r * g_ref[...]).astype(outs[0].dtype)

    tile = pl.BlockSpec((tm, D), lambda i: (i, 0))
    (h,), extra = _call(name, main, (T // tm,), [tile, pl.BlockSpec((1, D), lambda i: (0, 0))], [tile],
                        [jax.ShapeDtypeStruct((T, D), BF16)], (x, g), comm=comm)
    return h, extra


def _final_loss_epilogue(scale, out_scale):
    def ep(acc, ex, outs):
        res_ref, g_ref, t_ref = ex
        dx_ref, dxb_ref, dg_ref, loss_ref = outs
        n = acc.shape[-1]
        xv = res_ref[...] + scale * acc
        r = lax.rsqrt(jnp.mean(xv * xv, axis=-1, keepdims=True) + RMS_EPS)
        xhat = xv * r
        err = xhat * g_ref[...] - t_ref[...]

        @pl.when(pl.program_id(0) == 0)
        def _():
            dg_ref[...] = jnp.zeros_like(dg_ref)
            loss_ref[...] = jnp.zeros_like(loss_ref)

        loss_ref[...] += jnp.broadcast_to(jnp.sum(err * err) * (0.5 / n), loss_ref.shape)
        dy = err * (1.0 / n)
        dg_ref[...] += jnp.sum(dy * xhat, axis=0, keepdims=True)
        dxhat = dy * g_ref[...]
        dx = r * (dxhat - xhat * jnp.mean(dxhat * xhat, axis=-1, keepdims=True))
        dx_ref[...] = dx
        dxb_ref[...] = (out_scale * dx).astype(dxb_ref.dtype)
    return ep


def _mm_nstream(name, a, ws, w_sel, w_form, comps, out_dtypes, epilogue, cn, rows=1024, comm=None, after=None):
    T, K = a.shape
    N = ws[0].shape[1]
    rows = min(rows, T)
    assert N % cn == 0 and T % rows == 0
    n_w, n_c = len(ws), len(comps)

    def main(ins, outs, scr):
        a_ref = ins[0]
        w_refs = ins[1:1 + n_w]
        c_refs = ins[1 + n_w:1 + n_w + n_c]

        for r in range(T // rows):
            sl = slice(r * rows, (r + 1) * rows)
            a_blk = a_ref[sl, :]
            dots = [_dot(a_blk, w_ref[...], w_form) for w_ref in w_refs]
            res = epilogue(dots, [c_ref[sl, :] for c_ref in c_refs])
            for o_ref, o in zip(outs, res):
                o_ref[sl, :] = o.astype(o_ref.dtype)

    if w_form == "nt":
        w_specs = [pl.BlockSpec((None, cn, K), functools.partial(lambda j, s: (s, j, 0), s=s)) for s in w_sel]
    else:
        w_specs = [pl.BlockSpec((K, cn), lambda j: (0, j)) for _ in ws]
    chunk = pl.BlockSpec((T, cn), lambda j: (0, j))
    return _call(name, main, (N // cn,), [pl.BlockSpec((T, K), lambda j: (0, 0))] + w_specs + [chunk] * n_c,
                 [chunk] * len(out_dtypes), [jax.ShapeDtypeStruct((T, N), dt) for dt in out_dtypes],
                 (a, *ws, *comps), comm=comm, after=after)


def _mm_mstream(name, as_, ws, w_sel, w_form, extras, outs_desc, epilogue, tm=512, comm=None, after=None):
    T = as_[0].shape[0]
    tm = min(tm, T)
    n_a = len(as_)
    w_shapes = [w.shape[-2:] for w in ws]
    N = w_shapes[0][1] if w_form == "nn" else w_shapes[0][0]

    def main(ins, outs, scr):
        a_refs = ins[:n_a]
        w_refs = ins[n_a:2 * n_a]
        acc = None
        for a_ref, w_ref in zip(a_refs, w_refs):
            d = _dot(a_ref[...], w_ref[...], w_form)
            acc = d if acc is None else acc + d
        epilogue(acc, ins[2 * n_a:], outs)

    kind_spec = {"tile": pl.BlockSpec((tm, N), lambda i: (i, 0)), "vec": pl.BlockSpec((1, N), lambda i: (0, 0))}
    kind_shape = {"tile": (T, N), "vec": (1, N)}
    a_specs = [pl.BlockSpec((tm, a.shape[1]), lambda i: (i, 0)) for a in as_]
    w_specs = []
    for w, s in zip(ws, w_sel):
        if w.ndim == 3:
            w_specs.append(pl.BlockSpec((None,) + tuple(w.shape[1:]), functools.partial(lambda i, s: (s, 0, 0), s=s),
                                        pipeline_mode=pl.Buffered(1)))
        else:
            w_specs.append(pl.BlockSpec(tuple(w.shape), lambda i: (0, 0), pipeline_mode=pl.Buffered(1)))
    args = list(as_) + list(ws) + [e for e, _ in extras]
    return _call(name, main, (T // tm,), a_specs + w_specs + [kind_spec[k] for _, k in extras],
                 [kind_spec[k] for _, k in outs_desc],
                 [jax.ShapeDtypeStruct(kind_shape[k], dt) for dt, k in outs_desc], args, comm=comm, after=after)


def _residual_rms_epilogue(scale):
    def ep(acc, ex, outs):
        xv = ex[0][...] + scale * acc
        outs[0][...] = xv
        r = lax.rsqrt(jnp.mean(xv * xv, axis=-1, keepdims=True) + RMS_EPS)
        outs[1][...] = (xv * r * ex[1][...]).astype(outs[1].dtype)
    return ep


def _rms_bwd_epilogue(out_scale):
    def ep(acc, ex, outs):
        x_ref, g_ref, dres_ref = ex
        dx_ref, dxb_ref, dg_ref = outs
        xv = x_ref[...]
        r = lax.rsqrt(jnp.mean(xv * xv, axis=-1, keepdims=True) + RMS_EPS)
        xhat = xv * r

        @pl.when(pl.program_id(0) == 0)
        def _():
            dg_ref[...] = jnp.zeros_like(dg_ref)

        dg_ref[...] += jnp.sum(acc * xhat, axis=0, keepdims=True)
        dxhat = acc * g_ref[...]
        dx = r * (dxhat - xhat * jnp.mean(dxhat * xhat, axis=-1, keepdims=True)) + dres_ref[...]
        dx_ref[...] = dx
        dxb_ref[...] = (out_scale * dx).astype(dxb_ref.dtype)
    return ep


def _mm_tn(name, a, b, tmo, tno, out_dtype, tk=1024, comm=None, after=None):
    T, Ma = a.shape
    Nb = b.shape[1]
    tk = min(tk, T)
    nk = T // tk

    def main(ins, outs, scr):
        a_ref, b_ref = ins
        (acc_ref,) = scr
        k = pl.program_id(2)

        @pl.when(k == 0)
        def _():
            acc_ref[...] = jnp.zeros_like(acc_ref)

        acc_ref[...] += _dot(a_ref[...], b_ref[...], "tn")

        @pl.when(k == nk - 1)
        def _():
            outs[0][...] = acc_ref[...].astype(outs[0].dtype)

    (out,), extra = _call(
        name, main, (Ma // tmo, Nb // tno, nk),
        [pl.BlockSpec((tk, tmo), lambda i, j, k: (k, i)), pl.BlockSpec((tk, tno), lambda i, j, k: (k, j))],
        [pl.BlockSpec((tmo, tno), lambda i, j, k: (i, j))], [jax.ShapeDtypeStruct((Ma, Nb), out_dtype)],
        (a, b), scratch=[pltpu.VMEM((tmo, tno), F32)], comm=comm, after=after)
    return out, extra


def _swiglu_parts(g, u):
    s = _sigmoid(g)
    silu = g * s
    return [u * (s + silu * (1.0 - s)), silu, silu * u]


def _silu_mul_epilogue(dots, comps):
    g, u = dots
    return _swiglu_parts(g, u)


def _gate_parts_epilogue(dots, comps):
    (g,) = dots
    s = _sigmoid(g)
    silu = g * s
    return [s + silu * (1.0 - s), silu]


def _up_act_epilogue(dots, comps):
    (u,) = dots
    return [u * comps[0].astype(F32), u * comps[1].astype(F32)]


def _dact_epilogue(dots, comps):
    dact = dots[0].astype(BF16)
    return [dact * comps[0], dact * comps[1]]


def _identity_epilogue(dots, comps):
    return list(dots)


def _swap_halves(x):
    lane = lax.broadcasted_iota(jnp.int32, x.shape, 1)
    first = (lane % DK) < (DK // 2)
    return jnp.where(first, pltpu.roll(x, 128 - DK // 2, 1), pltpu.roll(x, DK // 2, 1))


def _rotary(t, cos, sin_signed):
    halves = []
    for p in range(QK_W // 128):
        th = t[:, 128 * p:128 * (p + 1)]
        halves.append(th * cos + _swap_halves(th) * sin_signed)
    return jnp.concatenate(halves, axis=1)


def _rotary_transposed(d, cos, sin_signed):
    halves = []
    for p in range(QK_W // 128):
        dh = d[:, 128 * p:128 * (p + 1)]
        halves.append(dh * cos + _swap_halves(dh * sin_signed))
    return jnp.concatenate(halves, axis=1)


def _log_sigmoid(x):
    return jnp.minimum(x, 0.0) - jnp.log(1.0 + jnp.exp(-jnp.abs(x)))


def _tri_sum(mask, x):
    tri = mask.astype(BF16)
    hi = x.astype(BF16)
    rest = x - hi.astype(F32)
    mid = rest.astype(BF16)
    lo = (rest - mid.astype(F32)).astype(BF16)
    return _dot(tri, hi, "nn") + _dot(tri, mid, "nn") + _dot(tri, lo, "nn")


def _attn_masks():
    row = lax.broadcasted_iota(jnp.int32, (SUPER, SUPER), 0)
    col = lax.broadcasted_iota(jnp.int32, (SUPER, SUPER), 1)
    same = (row // CHUNK) == (col // CHUNK)
    return row, col, same


def _group_inputs(grp, pr, cos, sin_signed, lg, wa2, ba):
    seg = lambda lo, width: pr[:, lo:lo + width].astype(F32)
    if grp == 0:
        q = _rotary(seg(C_RQ, QK_W), cos, sin_signed)
        k = _rotary(seg(C_RK, QK_W), cos, sin_signed) * (DK ** -0.5)
        v = pr[:, C_RV:C_RV + V_W]
        gate = seg(C_RG, V_W)
        pos = lax.broadcasted_iota(jnp.int32, (SUPER, QK_W), 0).astype(F32) + 1.0
        return q, k, v, gate, pos * lg, None, None
    q = seg(C_GQ, QK_W) * (DK ** -0.5)
    k = seg(C_GK, QK_W)
    v = pr[:, C_GV:C_GV + V_W]
    gate = seg(C_GG, V_W)
    glow = pr[:, C_GL:C_GL + GL_W]
    logit = _dot(glow.astype(BF16), wa2.astype(BF16), "nn") + ba
    la = _log_sigmoid(logit) * (1.0 / GATE_NORM)
    row, col, _ = _attn_masks()
    b_cum = _tri_sum(col <= row, la)
    return q, k, v, gate, b_cum, glow, logit


def _decay_factors(q, k, b_cum):
    c = b_cum[SUPER // 2 - 1:SUPER // 2, :]
    bl = b_cum[SUPER - 1:SUPER, :]
    e1 = jnp.exp(b_cum - c)
    e2 = jnp.exp(c - b_cum)
    e_b = jnp.exp(b_cum)
    e_l = jnp.exp(bl - b_cum)
    return dict(e1=e1, e2=e2, eb=e_b, el=e_l, ebl=jnp.exp(bl),
                qp=q * e1, qm=q * e2, kp=k * e1, km=k * e2, qs=q * e_b, kl=k * e_l)


def _state_block_mask():
    r = lax.broadcasted_iota(jnp.int32, (V_W, QK_W), 0)
    c = lax.broadcasted_iota(jnp.int32, (V_W, QK_W), 1)
    return (r // DV) == (c // DK)


def _attn_fwd(proj, cos, sin_signed, lg, wa2p, ba, gn_ret, gn_gla, x_res, w_out, g_next, comm=None):
    T = proj.shape[0]
    n_s = T // SUPER
    D = x_res.shape[1]

    def main(ins, outs, scr):
        pr_ref, cos_ref, sin_ref, lg_ref, wa2_ref, ba_ref, gr_ref, gg_ref, xres_ref, wout_ref, gnext_ref = ins
        o_ref, y_ref, st_ref, x_ref, h_ref = outs
        (s_ref,) = scr
        i = pl.program_id(0)

        @pl.when(i == 0)
        def _():
            s_ref[...] = jnp.zeros_like(s_ref)

        pr = pr_ref
        row, col, same = _attn_masks()
        m1 = col <= row
        m2 = jnp.logical_and(col > row, same)
        lane = lax.broadcasted_iota(jnp.int32, (1, QK_W), 1)
        blockmask = _state_block_mask()
        for grp in range(2):
            q, k, v, gate, b_cum, _, _ = _group_inputs(grp, pr, cos_ref[...], sin_ref[...], lg_ref[...],
                                                      wa2_ref[...], ba_ref[...])
            f = _decay_factors(q, k, b_cum)
            gn = gr_ref[...] if grp == 0 else gg_ref[...]
            s_prev = s_ref[grp]
            st_ref[0, grp] = s_prev
            o_inter = _dot(f["qs"].astype(BF16), s_prev.astype(BF16), "nt")
            kmb = f["km"].astype(BF16)
            kpb = f["kp"].astype(BF16)
            vb = v.astype(BF16)
            heads = [(lane // DK) == h for h in range(HEADS)]
            a1_all = _dot(jnp.concatenate([jnp.where(hm, f["qp"], 0.0).astype(BF16) for hm in heads], axis=0), kmb, "nt")
            a2_all = _dot(jnp.concatenate([jnp.where(hm, f["qm"], 0.0).astype(BF16) for hm in heads], axis=0), kpb, "nt")
            for h in range(HEADS):
                a1 = a1_all[h * SUPER:(h + 1) * SUPER]
                a2 = a2_all[h * SUPER:(h + 1) * SUPER]
                a = jnp.where(m1, a1, jnp.where(m2, a2, 0.0))
                lo = grp * V_W + h * DV
                o_h = _dot(a.astype(BF16), vb[:, h * DV:(h + 1) * DV], "nn") + o_inter[:, h * DV:(h + 1) * DV]
                o_ref[:, lo:lo + DV] = o_h
                r = lax.rsqrt(jnp.mean(o_h * o_h, axis=-1, keepdims=True) + RMS_EPS)
                gte = gate[:, h * DV:(h + 1) * DV]
                y = o_h * r * gn[:, h * DV:(h + 1) * DV] * (gte * _sigmoid(gte))
                y_ref[:, lo:lo + DV] = y.astype(y_ref.dtype)
            upd = _dot(vb, f["kl"].astype(BF16), "tn")
            s_ref[grp] = s_prev * f["ebl"] + jnp.where(blockmask, upd, 0.0)
        xv = xres_ref[...] + _dot(y_ref[...], wout_ref[...], "nn")
        x_ref[...] = xv
        r = lax.rsqrt(jnp.mean(xv * xv, axis=-1, keepdims=True) + RMS_EPS)
        h_ref[...] = (xv * r * gnext_ref[...]).astype(h_ref.dtype)

    const = lambda shape: pl.BlockSpec(shape, lambda i: tuple(0 for _ in shape))
    rows = lambda w: pl.BlockSpec((SUPER, w), lambda i: (i, 0))
    return _call(
        "attn_fwd", main, (n_s,),
        [rows(PROJ_W), rows(128), rows(128),
         const((1, QK_W)), const((GL_W, QK_W)), const((1, QK_W)), const((1, V_W)), const((1, V_W)),
         rows(D), const((2 * V_W, D)), const((1, D))],
        [rows(2 * V_W), rows(2 * V_W), pl.BlockSpec((1, 2, V_W, QK_W), lambda i: (i, 0, 0, 0)), rows(D), rows(D)],
        [jax.ShapeDtypeStruct((T, 2 * V_W), F32), jax.ShapeDtypeStruct((T, 2 * V_W), BF16),
         jax.ShapeDtypeStruct((n_s, 2, V_W, QK_W), F32), jax.ShapeDtypeStruct((T, D), F32),
         jax.ShapeDtypeStruct((T, D), BF16)],
        (proj, cos, sin_signed, lg, wa2p, ba, gn_ret, gn_gla, x_res, w_out, g_next),
        scratch=[pltpu.VMEM((2, V_W, QK_W), F32)], comm=comm)


def _attn_bwd(proj, cos, sin_signed, lg, wa2p, ba, gn_ret, gn_gla, o, dx, w_out, states, comm=None, after=None):
    T = proj.shape[0]
    n_s = T // SUPER
    D = dx.shape[1]

    def main(ins, outs, scr):
        pr_ref, cos_ref, sin_ref, lg_ref, wa2_ref, ba_ref, gr_ref, gg_ref, o_ref, dx_ref, wout_ref, st_ref = ins
        dp_ref, dgr_ref, dgg_ref, dba_ref, dwa_ref = outs
        (ds_ref, dy_ref) = scr
        i = pl.program_id(0)
        dy_ref[...] = _dot(dx_ref[...], wout_ref[...], "nt")

        @pl.when(i == 0)
        def _():
            ds_ref[...] = jnp.zeros_like(ds_ref)
            dgr_ref[...] = jnp.zeros_like(dgr_ref)
            dgg_ref[...] = jnp.zeros_like(dgg_ref)
            dba_ref[...] = jnp.zeros_like(dba_ref)
            dwa_ref[...] = jnp.zeros_like(dwa_ref)

        pr = pr_ref
        cos = cos_ref[...]
        sin_signed = sin_ref[...]
        row, col, same = _attn_masks()
        m1 = col <= row
        m2 = jnp.logical_and(col > row, same)
        m1t = row <= col
        m2t = jnp.logical_and(row > col, same)
        lane = lax.broadcasted_iota(jnp.int32, (1, QK_W), 1)
        blockmask = _state_block_mask()
        for grp in range(2):
            q, k, v, gate, b_cum, glow, logit = _group_inputs(grp, pr, cos, sin_signed, lg_ref[...],
                                                              wa2_ref[...], ba_ref[...])
            f = _decay_factors(q, k, b_cum)
            gn = gr_ref[...] if grp == 0 else gg_ref[...]
            dgn_ref = dgr_ref if grp == 0 else dgg_ref
            do_parts, dgate_parts, dgn_parts = [], [], []
            for h in range(HEADS):
                lo = grp * V_W + h * DV
                o_h = o_ref[:, lo:lo + DV]
                r = lax.rsqrt(jnp.mean(o_h * o_h, axis=-1, keepdims=True) + RMS_EPS)
                n = o_h * r
                gte = gate[:, h * DV:(h + 1) * DV]
                sg = _sigmoid(gte)
                dy_h = dy_ref[:, lo:lo + DV]
                gn_h = gn[:, h * DV:(h + 1) * DV]
                dgate_parts.append(dy_h * n * gn_h * (sg * (1.0 + gte * (1.0 - sg))))
                dz = dy_h * (gte * sg)
                dgn_parts.append(jnp.sum(dz * n, axis=0, keepdims=True))
                dn = dz * gn_h
                do_parts.append(r * (dn - n * jnp.mean(dn * n, axis=-1, keepdims=True)))
            dgn_ref[...] += jnp.concatenate(dgn_parts, axis=1)
            dgate = jnp.concatenate(dgate_parts, axis=1)
            do = jnp.concatenate(do_parts, axis=1)
            dob = do.astype(BF16)
            vb = v.astype(BF16)
            s_prev = st_ref[0, grp]
            ds_new = ds_ref[grp]
            dsb = ds_new.astype(BF16)
            qpb, qmb = f["qp"].astype(BF16), f["qm"].astype(BF16)
            kpb, kmb = f["kp"].astype(BF16), f["km"].astype(BF16)
            dv_parts = []
            heads = [(lane // DK) == h for h in range(HEADS)]
            qp_hs = [jnp.where(hm, f["qp"], 0.0).astype(BF16) for hm in heads]
            qm_hs = [jnp.where(hm, f["qm"], 0.0).astype(BF16) for hm in heads]
            kp_hs = [jnp.where(hm, f["kp"], 0.0).astype(BF16) for hm in heads]
            km_hs = [jnp.where(hm, f["km"], 0.0).astype(BF16) for hm in heads]
            km_stack, kp_stack = jnp.concatenate(km_hs, axis=0), jnp.concatenate(kp_hs, axis=0)
            at1_all = _dot(km_stack, qpb, "nt")
            at2_all = _dot(kp_stack, qmb, "nt")
            da1s, da2s, da1ts, da2ts = [], [], [], []
            for h in range(HEADS):
                at = jnp.where(m1t, at1_all[h * SUPER:(h + 1) * SUPER],
                               jnp.where(m2t, at2_all[h * SUPER:(h + 1) * SUPER], 0.0))
                do_h = dob[:, h * DV:(h + 1) * DV]
                v_h = vb[:, h * DV:(h + 1) * DV]
                dv_parts.append(_dot(at.astype(BF16), do_h, "nn"))
                da = _dot(do_h, v_h, "nt")
                dat = _dot(v_h, do_h, "nt")
                da1s.append(jnp.where(m1, da, 0.0).astype(BF16))
                da2s.append(jnp.where(m2, da, 0.0).astype(BF16))
                da1ts.append(jnp.where(m1t, dat, 0.0).astype(BF16))
                da2ts.append(jnp.where(m2t, dat, 0.0).astype(BF16))
            dqp = _dot(jnp.concatenate(da1s, axis=1), km_stack, "nn")
            dqm = _dot(jnp.concatenate(da2s, axis=1), kp_stack, "nn")
            dkm = _dot(jnp.concatenate(da1ts, axis=1), jnp.concatenate(qp_hs, axis=0), "nn")
            dkp = _dot(jnp.concatenate(da2ts, axis=1), jnp.concatenate(qm_hs, axis=0), "nn")
            klb = f["kl"].astype(BF16)
            qsb = f["qs"].astype(BF16)
            dqs = _dot(dob, s_prev.astype(BF16), "nn")
            dkl = _dot(vb, dsb, "nn")
            dv = jnp.concatenate(dv_parts, axis=1) + _dot(klb, dsb, "nt")
            ds_ref[grp] = ds_new * f["ebl"] + jnp.where(blockmask, _dot(dob, qsb, "tn"), 0.0)
            dq = dqp * f["e1"] + dqm * f["e2"] + dqs * f["eb"]
            dk = dkm * f["e2"] + dkp * f["e1"] + dkl * f["el"]
            if grp == 0:
                dq = _rotary_transposed(dq, cos, sin_signed)
                dk = _rotary_transposed(dk * (DK ** -0.5), cos, sin_signed)
                dp_ref[:, C_RQ:C_RQ + QK_W] = dq.astype(dp_ref.dtype)
                dp_ref[:, C_RK:C_RK + QK_W] = dk.astype(dp_ref.dtype)
                dp_ref[:, C_RV:C_RV + V_W] = dv.astype(dp_ref.dtype)
                dp_ref[:, C_RG:C_RG + V_W] = dgate.astype(dp_ref.dtype)
            else:
                dkl_kl = dkl * klb.astype(F32)
                db = (dqp * qpb.astype(F32) - dkm * kmb.astype(F32) - dqm * qmb.astype(F32)
                      + dkp * kpb.astype(F32) + dqs * qsb.astype(F32) - dkl_kl)
                last = (jnp.sum(dkl_kl, axis=0, keepdims=True)
                        + f["ebl"] * jnp.sum(s_prev * ds_new, axis=0, keepdims=True))
                rowq = lax.broadcasted_iota(jnp.int32, (SUPER, QK_W), 0)
                db = db + jnp.where(rowq == SUPER - 1, last, 0.0)
                dla = _tri_sum(col >= row, db)
                dlogit = dla * (1.0 / GATE_NORM) * (1.0 - _sigmoid(logit))
                dlb = dlogit.astype(BF16)
                dglow = _dot(dlb, wa2_ref[...].astype(BF16), "nt")
                dwa_ref[...] += _dot(glow.astype(BF16), dlb, "tn")
                dba_ref[...] += jnp.sum(dlogit, axis=0, keepdims=True)
                dp_ref[:, C_GQ:C_GQ + QK_W] = (dq * (DK ** -0.5)).astype(dp_ref.dtype)
                dp_ref[:, C_GK:C_GK + QK_W] = dk.astype(dp_ref.dtype)
                dp_ref[:, C_GV:C_GV + V_W] = dv.astype(dp_ref.dtype)
                dp_ref[:, C_GG:C_GG + V_W] = dgate.astype(dp_ref.dtype)
                dp_ref[:, C_GL:C_GL + GL_W] = dglow.astype(dp_ref.dtype)

    rev = lambda i: n_s - 1 - i
    const = lambda shape: pl.BlockSpec(shape, lambda i: tuple(0 for _ in shape))
    return _call(
        "attn_bwd", main, (n_s,),
        [pl.BlockSpec((SUPER, PROJ_W), lambda i: (rev(i), 0)),
         pl.BlockSpec((SUPER, 128), lambda i: (rev(i), 0)), pl.BlockSpec((SUPER, 128), lambda i: (rev(i), 0)),
         const((1, QK_W)), const((GL_W, QK_W)), const((1, QK_W)), const((1, V_W)), const((1, V_W)),
         pl.BlockSpec((SUPER, 2 * V_W), lambda i: (rev(i), 0)),
         pl.BlockSpec((SUPER, D), lambda i: (rev(i), 0)), const((2 * V_W, D)),
         pl.BlockSpec((1, 2, V_W, QK_W), lambda i: (rev(i), 0, 0, 0))],
        [pl.BlockSpec((SUPER, PROJ_W), lambda i: (rev(i), 0)),
         const((1, V_W)), const((1, V_W)), const((1, QK_W)), const((GL_W, QK_W))],
        [jax.ShapeDtypeStruct((T, PROJ_W), BF16),
         jax.ShapeDtypeStruct((1, V_W), F32), jax.ShapeDtypeStruct((1, V_W), F32),
         jax.ShapeDtypeStruct((1, QK_W), F32), jax.ShapeDtypeStruct((GL_W, QK_W), F32)],
        (proj, cos, sin_signed, lg, wa2p, ba, gn_ret, gn_gla, o, dx, w_out, states),
        scratch=[pltpu.VMEM((2, V_W, QK_W), F32), pltpu.VMEM((SUPER, 2 * V_W), F32)], comm=comm, after=after)


def _rotary_tables(T):
    half = DK // 2
    inv = ROPE_BASE ** (-jnp.arange(half, dtype=F32) * 2.0 / DK)
    ang = jnp.arange(T, dtype=F32)[:, None] * inv[None, :]
    cos, sin = jnp.cos(ang), jnp.sin(ang)
    cos_head = jnp.concatenate([cos, cos], axis=1)
    sin_head = jnp.concatenate([-sin, sin], axis=1)
    return jnp.tile(cos_head, (1, 128 // DK)), jnp.tile(sin_head, (1, 128 // DK))


def _sum_devices(name, gathered, m_per):
    def body(g_ref, o_ref):
        acc = g_ref[0:m_per, :]
        for k in range(1, N_DEV):
            acc = acc + g_ref[k * m_per:(k + 1) * m_per, :]
        o_ref[...] = acc

    return pl.pallas_call(body, name=name, out_shape=jax.ShapeDtypeStruct((m_per, 128), F32))(gathered)


def _owner_sums(name, items, owner, comm=None):
    counts = [1 + len(landed) for _, landed in items]

    def main(ins, outs, scr):
        at = 0
        for o_ref, n in zip(outs, counts):
            acc = ins[at][...].astype(F32)
            for l_ref in ins[at + 1:at + n]:
                for j in range(l_ref.shape[0]):
                    acc = acc + l_ref[j].astype(F32)
            o_ref[...] = acc
            at += n

    once = pl.Buffered(1)
    in_specs, out_specs, out_shape, args = [], [], [], []
    for grad, landed in items:
        R, C = grad.shape[-2:]
        in_specs.append(pl.BlockSpec((None, None, R, C), lambda i, s: (s[0], s[1], 0, 0), pipeline_mode=once))
        in_specs += [pl.BlockSpec(tuple(l.shape), lambda i, s: (0, 0, 0), pipeline_mode=once) for l in landed]
        out_specs.append(pl.BlockSpec((R, C), lambda i, s: (0, 0)))
        out_shape.append(jax.ShapeDtypeStruct((R, C), F32))
        args += [grad, *landed]
    return _call(name, main, (1,), in_specs, out_specs, out_shape, args, comm=comm, prefetch=owner)


def _adamw_group(name, items, n_blocks, comm=None):
    n = len(items)

    def main(ins, outs, scr):
        for p in range(n):
            g_ref, w_ref, m_ref, v_ref = ins[4 * p:4 * p + 4]
            d_ref, nm_ref, nv_ref = outs[3 * p:3 * p + 3]
            gv = g_ref[...]
            nm = ADAM_B1 * m_ref[...] + (1.0 - ADAM_B1) * gv
            nv = ADAM_B2 * v_ref[...] + (1.0 - ADAM_B2) * (gv * gv)
            m_hat = nm / (1.0 - ADAM_B1 ** ADAM_STEP)
            v_hat = nv / (1.0 - ADAM_B2 ** ADAM_STEP)
            d_ref[...] = -ADAM_LR * (m_hat / (jnp.sqrt(v_hat) + ADAM_EPS) + ADAM_WD * w_ref[...])
            nm_ref[...] = nm
            nv_ref[...] = nv

    in_specs, out_specs, out_shape, args = [], [], [], []
    for item in items:
        shape = item[1].shape
        assert shape[0] % n_blocks == 0
        spec = pl.BlockSpec((shape[0] // n_blocks,) + shape[1:], lambda i, nd=len(shape): (i,) + (0,) * (nd - 1))
        in_specs += [spec] * 4
        out_specs += [spec] * 3
        out_shape += [jax.ShapeDtypeStruct(shape, F32)] * 3
        args += list(item)
    outs, extra = _call(name, main, (n_blocks,), in_specs, out_specs, out_shape, args, comm=comm)
    return [tuple(outs[3 * p:3 * p + 3]) for p in range(n)], extra


def _stack_row_blocks(name, blocks, rows_out):
    n, r, C = blocks.shape

    def main(ins, outs, scr):
        (stage,) = scr
        stage[n * r:, :] = jnp.zeros((rows_out - n * r, C), F32)
        for s in range(n):
            stage[s * r:(s + 1) * r, :] = ins[0][s].astype(F32)
        outs[0][...] = stage[...].astype(outs[0].dtype)

    (out,), _ = _call(
        name, main, (1,), [pl.BlockSpec((n, r, C), lambda i: (0, 0, 0), pipeline_mode=pl.Buffered(1))],
        [pl.BlockSpec((rows_out, C), lambda i: (0, 0))], [jax.ShapeDtypeStruct((rows_out, C), blocks.dtype)],
        (blocks,), scratch=[pltpu.VMEM((rows_out, C), F32)])
    return out


def _unstack_row_blocks(name, a, n, r):
    C = a.shape[1]

    def main(ins, outs, scr):
        (stage,) = scr
        stage[...] = ins[0][...].astype(F32)
        for s in range(n):
            outs[0][s] = stage[s * r:(s + 1) * r, :].astype(outs[0].dtype)

    (out,), _ = _call(
        name, main, (1,), [pl.BlockSpec((n * r, C), lambda i: (0, 0), pipeline_mode=pl.Buffered(1))],
        [pl.BlockSpec((n, r, C), lambda i: (0, 0, 0))], [jax.ShapeDtypeStruct((n, r, C), a.dtype)],
        (a,), scratch=[pltpu.VMEM((n * r, C), F32)])
    return out


SMALL_ORDER = ("ffn1", "mix", "ffn2", "final", "ret", "gla", "b_a")


def kernel(x, ffn1_norm_g, ffn1_w_gate, ffn1_w_up, ffn1_w_down, mix_norm_g, w_in, ret_norm_g, gla_w_a2, gla_b_a, gla_norm_g, w_out, ffn2_norm_g, ffn2_w_gate, ffn2_w_up, ffn2_w_down, final_norm_g, loss_target, m_ffn1_norm_g, m_ffn1_w_gate, m_ffn1_w_up, m_ffn1_w_down, m_mix_norm_g, m_w_in, m_ret_norm_g, m_gla_w_a2, m_gla_b_a, m_gla_norm_g, m_w_out, m_ffn2_norm_g, m_ffn2_w_gate, m_ffn2_w_up, m_ffn2_w_down, m_final_norm_g, v_ffn1_norm_g, v_ffn1_w_gate, v_ffn1_w_up, v_ffn1_w_down, v_mix_norm_g, v_w_in, v_ret_norm_g, v_gla_w_a2, v_gla_b_a, v_gla_norm_g, v_w_out, v_ffn2_norm_g, v_ffn2_w_gate, v_ffn2_w_up, v_ffn2_w_down, v_final_norm_g):
    xi, yi, ci = _coords()
    dev = 4 * xi + 2 * yi + ci
    owner = jnp.stack([2 * xi + yi, ci]).astype(jnp.int32)

    x0, target = x[0], loss_target[0]
    T, D = x0.shape
    fb = ffn1_w_gate.shape[2]
    ib = w_in.shape[2]
    ab = gla_w_a2.shape[2]
    F = N_DEV * fb
    cos, sin_signed = _rotary_tables(T)
    lg = jnp.repeat(jnp.log(1.0 - 2.0 ** (-5.0 - jnp.arange(HEADS, dtype=F32))), DK)[None, :]
    g_final = final_norm_g.reshape(1, D)

    g1_loc = ffn1_w_gate[0].T[None].astype(BF16)
    u1_loc = ffn1_w_up[0].T[None].astype(BF16)
    d1_loc = ffn1_w_down.astype(BF16)
    g2_loc = ffn2_w_gate[0].T[None].astype(BF16)
    u2_loc = ffn2_w_up[0].T[None].astype(BF16)
    d2_loc = ffn2_w_down.astype(BF16)
    in_loc = w_in[0].T.astype(BF16)
    out_loc = w_out[0].astype(BF16)

    h1, (g1,) = _rms_fwd("ffn1_rms", x0, ffn1_norm_g, comm=_AllGather([g1_loc], ["stack"]))
    g1 = g1.reshape(1, F, D)
    (dsl1, sl1), (u1,) = _mm_nstream("ffn1_gate", h1, [g1], [0], "nt", [], [BF16, BF16], _gate_parts_epilogue, cn=256,
                                     comm=_AllGather([u1_loc], ["stack"]))
    u1 = u1.reshape(1, F, D)
    (dsu1, act1), (d1,) = _mm_nstream("ffn1_up", h1, [u1], [0], "nt", [dsl1, sl1], [BF16, BF16],
                                      _up_act_epilogue, cn=256, comm=_AllGather([d1_loc], ["stack"]))
    d1 = d1.reshape(1, F, D)
    f32_tile, bf16_tile, f32_vec = (F32, "tile"), (BF16, "tile"), (F32, "vec")
    (x1, h2), (in_all, a_all) = _mm_mstream(
        "ffn1_down", [act1], [d1], [0], "nn", [(x0, "tile"), (mix_norm_g, "vec")], [f32_tile, bf16_tile],
        _residual_rms_epilogue(0.5), comm=_AllGather([in_loc, gla_w_a2[0]], ["plain", "plain"]))
    w_in_t = _stack_row_blocks("w_in_rows", in_all, PROJ_W)[None]
    wa2 = jnp.transpose(a_all, (1, 0, 2)).reshape(GATE_RANK, N_DEV * ab)
    wa2p = jnp.pad(wa2, ((0, GL_W - GATE_RANK), (0, 0)))

    (proj,), (g2, out_all) = _mm_nstream("mix_proj", h2, [w_in_t], [0], "nt", [], [BF16], _identity_epilogue, cn=640,
                                         comm=_AllGather([g2_loc, out_loc], ["stack", "plain"]))
    w_out_full = out_all.reshape(D, D)
    (o, ymix, states, x2, h3), (u2,) = _attn_fwd(proj, cos, sin_signed, lg, wa2p, gla_b_a, ret_norm_g, gla_norm_g,
                                                 x1, w_out_full, ffn2_norm_g, comm=_AllGather([u2_loc], ["stack"]))
    g2, u2 = g2.reshape(1, F, D), u2.reshape(1, F, D)

    (dsu2, sl2, act2), (d2,) = _mm_nstream(
        "ffn2_up", h3, [g2, u2], [0, 0], "nt", [], [BF16, BF16, BF16], _silu_mul_epilogue, cn=256,
        comm=_AllGather([d2_loc], ["stack"]))
    d2 = d2.reshape(1, F, D)
    (dx3, dy3b, d_final, loss), _ = _mm_mstream(
        "ffn2_down", [act2], [d2], [0], "nn", [(x2, "tile"), (g_final, "vec"), (target, "tile")],
        [f32_tile, bf16_tile, f32_vec, f32_vec], _final_loss_epilogue(0.5, 0.5))

    sent = {}

    def send(**grads):
        started = _send_to_owners("send_" + "_".join(grads), list(grads.values()))
        sent.update(zip(grads, started))
        return started[0][4]

    dwd2, _ = _mm_tn("ffn2b_dwd", act2, dy3b, F // 2, D, BF16)
    (dgate2, dup2), _ = _mm_nstream("ffn2b_dact", dy3b, [d2], [0], "nt", [dsu2, sl2], [BF16, BF16],
                                    _dact_epilogue, cn=256)
    dwg2, _ = _mm_tn("ffn2b_dwg", dgate2, h3, F // 2, D, BF16)
    dwu2, _ = _mm_tn("ffn2b_dwu", dup2, h3, F // 2, D, BF16)
    tok = send(wd2=dwd2.reshape(4, 2, fb, D), wg2=dwg2.reshape(4, 2, fb, D), wu2=dwu2.reshape(4, 2, fb, D))
    rms_outs = [f32_tile, bf16_tile, f32_vec]
    (dx2, dx2b, d_g2), _ = _mm_mstream(
        "ffn2b_dh", [dgate2, dup2], [g2, u2], [0, 0], "nn", [(x2, "tile"), (ffn2_norm_g, "vec"), (dx3, "tile")],
        rms_outs, _rms_bwd_epilogue(1.0), after=tok)

    dwout, _ = _mm_tn("mixb_dwout", ymix, dx2b, D, D, BF16)
    (dproj, d_ret, d_gla, d_ba, d_wa2p), _ = _attn_bwd(
        proj, cos, sin_signed, lg, wa2p, gla_b_a, ret_norm_g, gla_norm_g, o, dx2b, w_out_full, states)
    dwin_t, _ = _mm_tn("mixb_dwin", dproj, h2, 640, D, BF16, tk=2048)
    tok = send(wout=dwout.reshape(4, 2, D // N_DEV, D), win=_unstack_row_blocks("dwin_blocks", dwin_t, N_DEV, ib).reshape(4, 2, ib, D))
    (dx1, dy1b, d_gmix), _ = _mm_mstream(
        "mixb_dh", [dproj], [w_in_t], [0], "nn", [(x1, "tile"), (mix_norm_g, "vec"), (dx2, "tile")],
        rms_outs, _rms_bwd_epilogue(0.5), after=tok)

    dwd1, _ = _mm_tn("ffn1b_dwd", act1, dy1b, F // 2, D, BF16)
    tok = send(wd1=dwd1.reshape(4, 2, fb, D))
    (dgate1, dup1), _ = _mm_nstream("ffn1b_dact", dy1b, [d1], [0], "nt", [dsu1, sl1], [BF16, BF16],
                                    _dact_epilogue, cn=256, after=tok)
    dwg1, _ = _mm_tn("ffn1b_dwg", dgate1, h1, F // 2, D, BF16)
    tok = send(wg1=dwg1.reshape(4, 2, fb, D))
    dwu1, _ = _mm_tn("ffn1b_dwu", dup1, h1, F // 2, D, BF16, after=tok)
    tok = send(wu1=dwu1.reshape(4, 2, fb, D))
    (dx0, _, d_g1), _ = _mm_mstream(
        "ffn1b_dh", [dgate1, dup1], [g1, u1], [0, 0], "nn", [(x0, "tile"), (ffn1_norm_g, "vec"), (dx1, "tile")],
        rms_outs, _rms_bwd_epilogue(1.0), after=tok)

    small = dict(ffn1=d_g1, mix=d_gmix, ffn2=d_g2, final=d_final, ret=d_ret, gla=d_gla, b_a=d_ba)
    flat = jnp.concatenate([small[k].reshape(-1) for k in SMALL_ORDER]
                           + [d_wa2p[:GATE_RANK].reshape(-1), loss[0, :128]])
    rows = -(-flat.shape[0] // 128)
    rows = -(-rows // 8) * 8
    packed = jnp.pad(flat, (0, rows * 128 - flat.shape[0])).reshape(rows, 128)

    transposed = ("ffn1_w_gate", "ffn1_w_up", "ffn2_w_gate", "ffn2_w_up")

    def to_2d(nm, a):
        if nm == "w_in":
            return jnp.transpose(a, (2, 0, 1))
        if nm in transposed:
            return a[0].T
        return a.reshape((1, a.shape[0]) if a.ndim == 1 else a.shape[-2:])

    def from_2d(nm, a):
        if nm == "w_in":
            return jnp.transpose(a, (1, 2, 0))
        return a.T[None] if nm in transposed else a.reshape(params[nm][0].shape)

    def arrived(nm, after):
        grad, landed = _await_owners("await_" + nm, sent[nm], after)
        return grad, [landed]

    sums_a, (gathered,) = _owner_sums(
        "sum_a", [arrived(nm, dx0) for nm in ("wg2", "wu2", "wd2", "win", "wout")], owner,
        comm=_AllGather([packed], ["plain"]))
    params = dict(
        ffn2_w_gate=(ffn2_w_gate, m_ffn2_w_gate, v_ffn2_w_gate), ffn2_w_up=(ffn2_w_up, m_ffn2_w_up, v_ffn2_w_up),
        ffn2_w_down=(ffn2_w_down, m_ffn2_w_down, v_ffn2_w_down), w_in=(w_in, m_w_in, v_w_in),
        w_out=(w_out, m_w_out, v_w_out), ffn1_w_gate=(ffn1_w_gate, m_ffn1_w_gate, v_ffn1_w_gate),
        ffn1_w_up=(ffn1_w_up, m_ffn1_w_up, v_ffn1_w_up), ffn1_w_down=(ffn1_w_down, m_ffn1_w_down, v_ffn1_w_down),
        ffn1_norm_g=(ffn1_norm_g, m_ffn1_norm_g, v_ffn1_norm_g), mix_norm_g=(mix_norm_g, m_mix_norm_g, v_mix_norm_g),
        ret_norm_g=(ret_norm_g, m_ret_norm_g, v_ret_norm_g), gla_w_a2=(gla_w_a2, m_gla_w_a2, v_gla_w_a2),
        gla_b_a=(gla_b_a, m_gla_b_a, v_gla_b_a), gla_norm_g=(gla_norm_g, m_gla_norm_g, v_gla_norm_g),
        ffn2_norm_g=(ffn2_norm_g, m_ffn2_norm_g, v_ffn2_norm_g), final_norm_g=(final_norm_g, m_final_norm_g, v_final_norm_g))
    grads, updates = {}, {}

    def run_adam(name, names, grad_2d, n_blocks):
        items = [(grad_2d[nm],) + tuple(to_2d(nm, a) for a in params[nm]) for nm in names]
        res, _ = _adamw_group(name, items, n_blocks)
        for nm, r in zip(names, res):
            grads[nm] = from_2d(nm, grad_2d[nm])
            updates[nm] = tuple(from_2d(nm, a) for a in r)
        return res

    grads_a = {"ffn2_w_gate": sums_a[0], "ffn2_w_up": sums_a[1], "ffn2_w_down": sums_a[2], "w_out": sums_a[4]}
    run_adam("adamw_w_in", ["w_in"], {"w_in": sums_a[3][:, None, :]}, 2)
    done_a = run_adam("adamw_a", list(grads_a), grads_a, 4)[0][0]
    sums_b, _ = _owner_sums("sum_b", [arrived(nm, done_a) for nm in ("wg1", "wu1", "wd1")], owner)
    grads_b = {"ffn1_w_gate": sums_b[0], "ffn1_w_up": sums_b[1], "ffn1_w_down": sums_b[2]}
    run_adam("adamw_b", list(grads_b), grads_b, 4)

    total = _sum_devices("sum_small", gathered.reshape(N_DEV * rows, 128), rows).reshape(-1)
    sizes = [small[k].size for k in SMALL_ORDER] + [GATE_RANK * QK_W, 128]
    offs = [0]
    for s in sizes:
        offs.append(offs[-1] + s)
    pieces = [total[offs[i]:offs[i + 1]] for i in range(len(sizes))]
    g_small = {k: pieces[i].reshape(small[k].shape) for i, k in enumerate(SMALL_ORDER)}
    g_wa2_full = pieces[len(SMALL_ORDER)].reshape(GATE_RANK, QK_W)
    g_wa2 = lax.dynamic_slice(g_wa2_full, (0, dev * ab), (GATE_RANK, ab))
    loss_total = pieces[len(SMALL_ORDER) + 1][0]

    small_grads = {"ffn1_norm_g": g_small["ffn1"], "mix_norm_g": g_small["mix"], "ret_norm_g": g_small["ret"],
                   "gla_w_a2": g_wa2, "gla_b_a": g_small["b_a"], "gla_norm_g": g_small["gla"],
                   "ffn2_norm_g": g_small["ffn2"], "final_norm_g": g_small["final"]}
    run_adam("adamw_small", list(small_grads), small_grads, 1)

    order = ("ffn1_norm_g", "ffn1_w_gate", "ffn1_w_up", "ffn1_w_down", "mix_norm_g", "w_in", "ret_norm_g", "gla_w_a2",
             "gla_b_a", "gla_norm_g", "w_out", "ffn2_norm_g", "ffn2_w_gate", "ffn2_w_up", "ffn2_w_down", "final_norm_g")
    return (loss_total, dx0[None], *[grads[nm] for nm in order], *[updates[nm][0] for nm in order],
            *[updates[nm][1] for nm in order], *[updates[nm][2] for nm in order])
```

```python
import functools
import math

import jax
import jax.numpy as jnp
from jax import lax
from jax.experimental import pallas as pl
from jax.experimental.pallas import tpu as pltpu

F32 = jnp.float32
BF16 = jnp.bfloat16
MESH = pl.DeviceIdType.MESH
HBM = pl.BlockSpec(memory_space=pltpu.HBM)

N_DEV = 8
RMS_EPS = 1e-6
ROPE_BASE = 10000.0
HEADS = 4
DK = 64
DV = 128
QK_W = HEADS * DK
V_W = HEADS * DV
GATE_RANK = 16
GATE_NORM = 16.0
CHUNK = 64
SUPER = 256
PROJ_W = 3200
C_RQ, C_RK, C_RV, C_RG, C_GQ, C_GK, C_GV, C_GG, C_GL = 0, 256, 512, 1024, 1536, 1792, 2048, 2560, 3072
GL_W = PROJ_W - C_GL
ADAM_LR, ADAM_B1, ADAM_B2, ADAM_EPS, ADAM_WD, ADAM_STEP = 0.001, 0.9, 0.999, 1e-08, 0.01, 10
VMEM_LIMIT_V7X = 52 * 1024 * 1024


def _cparams(**kw):
    return pltpu.CompilerParams(vmem_limit_bytes=VMEM_LIMIT_V7X, **kw)


def _dot(a, b, form, precision=None):
    dims = {"nn": (((1,), (0,)), ((), ())), "nt": (((1,), (1,)), ((), ())), "tn": (((0,), (0,)), ((), ()))}[form]
    return lax.dot_general(a, b, dims, preferred_element_type=F32, precision=precision)


def _sigmoid(x):
    return 1.0 / (1.0 + jnp.exp(-x))


def _coords():
    return lax.axis_index("x"), lax.axis_index("y"), lax.axis_index("c")


class _NoComm:
    inputs, out_shapes, scratch = (), (), ()


class _AllGather:
    def __init__(self, arrays, kinds):
        self.inputs = tuple(arrays)
        self.kinds = tuple(kinds)
        n = len(arrays)
        self.out_shapes = tuple(
            jax.ShapeDtypeStruct((a.shape[0], N_DEV) + a.shape[1:] if k == "stack" else (N_DEV,) + a.shape, a.dtype)
            for a, k in zip(arrays, kinds))
        self.scratch = (pltpu.SemaphoreType.DMA((n, 7)), pltpu.SemaphoreType.DMA((n, 7)),
                        pltpu.SemaphoreType.DMA((n,)))

    def _ctx(self, srcs, outs, sems):
        send_sems, recv_sems, local_sems = sems
        x, y, c = _coords()
        me, sibling = (x, y, c), (x, y, 1 - c)
        chips = [(1 - x, y), (x, 1 - y), (1 - x, 1 - y)]

        def blk(m, dev):
            k = 4 * dev[0] + 2 * dev[1] + dev[2]
            return outs[m].at[:, k] if self.kinds[m] == "stack" else outs[m].at[k]

        def copy(m, s, block, to, src=None):
            return pltpu.make_async_remote_copy(
                src_ref=blk(m, block) if src is None else src, dst_ref=blk(m, block),
                send_sem=send_sems.at[m, s], recv_sem=recv_sems.at[m, s], device_id=to, device_id_type=MESH)

        def mine(m):
            return pltpu.make_async_copy(srcs[m], blk(m, me), local_sems.at[m])

        def first(m):
            return [copy(m, 0, me, sibling, src=srcs[m])] + [
                copy(m, 1 + j, me, (*chip, c), src=srcs[m]) for j, chip in enumerate(chips)]

        return me, sibling, chips, c, copy, mine, first

    def start(self, srcs, outs, sems):
        me, sibling, chips, c, copy, mine, first = self._ctx(srcs, outs, sems)
        for m in range(len(srcs)):
            mine(m).start()
            for cp in first(m):
                cp.start()

    def mid(self, srcs, outs, sems):
        me, sibling, chips, c, copy, mine, first = self._ctx(srcs, outs, sems)
        for j, chip in enumerate(chips):
            for m in range(len(srcs)):
                copy(m, 1 + j, (*chip, c), me).wait_recv()
                copy(m, 4 + j, (*chip, c), sibling).start()

    def finish(self, srcs, outs, sems):
        me, sibling, chips, c, copy, mine, first = self._ctx(srcs, outs, sems)
        for m in range(len(srcs)):
            copy(m, 0, sibling, me).wait_recv()
            for j, chip in enumerate(chips):
                copy(m, 4 + j, (*chip, 1 - c), me).wait_recv()
            for cp in first(m):
                cp.wait_send()
            for j, chip in enumerate(chips):
                copy(m, 4 + j, (*chip, c), sibling).wait_send()
            mine(m).wait()


RELATIONS = ((0, 0, 1), (1, 0, 0), (0, 1, 0), (1, 1, 0), (1, 0, 1), (0, 1, 1), (1, 1, 1))
SEM = pl.BlockSpec(memory_space=pltpu.SEMAPHORE)
SPLIT_PARAMS = dict(has_side_effects=pltpu.SideEffectType.DATAFLOW_SIDE_EFFECTING)


def _owner_copies(grad_ref, land_ref, send_sems, recv_sems):
    x, y, c = _coords()
    copies = []
    for s, (fx, fy, fc) in enumerate(RELATIONS):
        px = 1 - x if fx else x
        py = 1 - y if fy else y
        pc = 1 - c if fc else c
        copies.append(pltpu.make_async_remote_copy(
            src_ref=grad_ref.at[2 * px + py, pc], dst_ref=land_ref.at[s], send_sem=send_sems.at[s],
            recv_sem=recv_sems.at[s], device_id=(px, py, pc), device_id_type=MESH))
    return copies


def _send_to_owners(name, grads):
    n, k = len(RELATIONS), len(grads)
    land_shapes = [(n,) + g.shape[2:] for g in grads]

    def body(*refs):
        ins, outs = refs[:2 * k], refs[2 * k:]
        for j in range(k):
            for cp in _owner_copies(ins[2 * j], ins[2 * j + 1], outs[4 * j], outs[4 * j + 1]):
                cp.start()
        outs[-1][...] = jnp.zeros_like(outs[-1])

    out_shape, out_specs, args, aliases = [], [], [], {}
    for j, (g, land_shape) in enumerate(zip(grads, land_shapes)):
        out_shape += [pltpu.SemaphoreType.DMA((n,)), pltpu.SemaphoreType.DMA((n,)), pltpu.HBM(g.shape, g.dtype),
                      pltpu.HBM(land_shape, g.dtype)]
        out_specs += [SEM, SEM, HBM, HBM]
        args += [pltpu.with_memory_space_constraint(g, pltpu.HBM),
                 pltpu.with_memory_space_constraint(lax.empty(land_shape, g.dtype), pltpu.HBM)]
        aliases.update({2 * j: 4 * j + 2, 2 * j + 1: 4 * j + 3})
    res = pl.pallas_call(
        body, name=name, out_shape=out_shape + [jax.ShapeDtypeStruct((8, 128), F32)],
        in_specs=[HBM] * (2 * k), out_specs=out_specs + [pl.BlockSpec(memory_space=pltpu.VMEM)],
        input_output_aliases=aliases, compiler_params=pltpu.CompilerParams(**SPLIT_PARAMS),
    )(*args)
    return [tuple(res[4 * j:4 * j + 4]) + (res[-1],) for j in range(k)]


def _await_owners(name, started, after):
    send_sems, recv_sems, g_thru, land_thru, _ = started

    def body(g_ref, land_ref, send_sems, recv_sems, after_ref, g_out, land_out):
        for cp in _owner_copies(g_ref, land_ref, send_sems, recv_sems):
            cp.wait_send()
            cp.wait_recv()

    return pl.pallas_call(
        body, name=name, out_shape=(pltpu.HBM(g_thru.shape, g_thru.dtype), pltpu.HBM(land_thru.shape, land_thru.dtype)),
        in_specs=(HBM, HBM, SEM, SEM, pl.BlockSpec(memory_space=pl.ANY)), out_specs=(HBM, HBM),
        input_output_aliases={0: 0, 1: 1}, compiler_params=pltpu.CompilerParams(**SPLIT_PARAMS),
    )(g_thru, land_thru, send_sems, recv_sems, after)


def _call(name, main, grid, in_specs, out_specs, out_shape, args, scratch=(), comm=None, prefetch=None, after=None):
    comm = comm or _NoComm()
    n_main = len(in_specs)
    if after is not None:
        in_specs = list(in_specs) + [pl.BlockSpec(after.shape, lambda *_: (0,) * after.ndim)]
        args = tuple(args) + (after,)
    counts = [len(in_specs), len(comm.inputs), len(out_shape), len(comm.out_shapes), len(scratch), len(comm.scratch)]
    n_steps = math.prod(grid)
    hosted = bool(comm.inputs)

    def body(*refs):
        if prefetch is not None:
            refs = refs[1:]
        parts, at = [], 0
        for n in counts:
            parts.append(refs[at:at + n])
            at += n
        ins, c_in, outs, c_out, scr, c_scr = parts
        ins = ins[:n_main]
        step = pl.program_id(0)
        for d in range(1, len(grid)):
            step = step * grid[d] + pl.program_id(d)
        if hosted:
            @pl.when(step == 0)
            def _():
                comm.start(c_in, c_out, c_scr)
        main(ins, outs, scr)
        if hosted:
            @pl.when(step == max(n_steps - 2, 0))
            def _():
                comm.mid(c_in, c_out, c_scr)

            @pl.when(step == n_steps - 1)
            def _():
                comm.finish(c_in, c_out, c_scr)

    all_in = list(in_specs) + [HBM] * counts[1]
    all_out = list(out_specs) + [HBM] * counts[3]
    all_scratch = list(scratch) + list(comm.scratch)
    shapes = list(out_shape) + list(comm.out_shapes)
    if prefetch is None:
        res = pl.pallas_call(body, name=name, grid=grid, in_specs=all_in, out_specs=all_out, out_shape=shapes,
                             scratch_shapes=all_scratch, compiler_params=_cparams())(*args, *comm.inputs)
    else:
        res = pl.pallas_call(
            body, name=name, out_shape=shapes,
            grid_spec=pltpu.PrefetchScalarGridSpec(num_scalar_prefetch=1, grid=grid, in_specs=all_in,
                                                   out_specs=all_out, scratch_shapes=all_scratch),
            compiler_params=_cparams())(prefetch, *args, *comm.inputs)
    return res[:counts[2]], res[counts[2]:]


def _rms_fwd(name, x, g, comm=None):
    T, D = x.shape
    tm = min(T, 512)

    def main(ins, outs, scr):
        x_ref, g_ref = ins
        xv = x_ref[...]
        r = lax.rsqrt(jnp.mean(xv * xv, axis=-1, keepdims=True) + RMS_EPS)
        outs[0][...] = (xv * r * g_ref[...]).astype(outs[0].dtype)

    tile = pl.BlockSpec((tm, D), lambda i: (i, 0))
    (h,), extra = _call(name, main, (T // tm,), [tile, pl.BlockSpec((1, D), lambda i: (0, 0))], [tile],
                        [jax.ShapeDtypeStruct((T, D), BF16)], (x, g), comm=comm)
    return h, extra


def _final_loss_epilogue(scale, out_scale):
    def ep(acc, ex, outs):
        res_ref, g_ref, t_ref = ex
        dx_ref, dxb_ref, dg_ref, loss_ref = outs
        n = acc.shape[-1]
        xv = res_ref[...] + scale * acc
        r = lax.rsqrt(jnp.mean(xv * xv, axis=-1, keepdims=True) + RMS_EPS)
        xhat = xv * r
        err = xhat * g_ref[...] - t_ref[...]

        @pl.when(pl.program_id(0) == 0)
        def _():
            dg_ref[...] = jnp.zeros_like(dg_ref)
            loss_ref[...] = jnp.zeros_like(loss_ref)

        loss_ref[...] += jnp.broadcast_to(jnp.sum(err * err) * (0.5 / n), loss_ref.shape)
        dy = err * (1.0 / n)
        dg_ref[...] += jnp.sum(dy * xhat, axis=0, keepdims=True)
        dxhat = dy * g_ref[...]
        dx = r * (dxhat - xhat * jnp.mean(dxhat * xhat, axis=-1, keepdims=True))
        dx_ref[...] = dx
        dxb_ref[...] = (out_scale * dx).astype(dxb_ref.dtype)
    return ep


def _mm_nstream(name, a, ws, w_sel, w_form, comps, out_dtypes, epilogue, cn, rows=1024, comm=None, after=None):
    T, K = a.shape
    N = ws[0].shape[1]
    rows = min(rows, T)
    assert N % cn == 0 and T % rows == 0
    n_w, n_c = len(ws), len(comps)

    def main(ins, outs, scr):
        a_ref = ins[0]
        w_refs = ins[1:1 + n_w]
        c_refs = ins[1 + n_w:1 + n_w + n_c]

        for r in range(T // rows):
            sl = slice(r * rows, (r + 1) * rows)
            a_blk = a_ref[sl, :]
            dots = [_dot(a_blk, w_ref[...], w_form) for w_ref in w_refs]
            res = epilogue(dots, [c_ref[sl, :] for c_ref in c_refs])
            for o_ref, o in zip(outs, res):
                o_ref[sl, :] = o.astype(o_ref.dtype)

    if w_form == "nt":
        w_specs = [pl.BlockSpec((None, cn, K), functools.partial(lambda j, s: (s, j, 0), s=s)) for s in w_sel]
    else:
        w_specs = [pl.BlockSpec((K, cn), lambda j: (0, j)) for _ in ws]
    chunk = pl.BlockSpec((T, cn), lambda j: (0, j))
    return _call(name, main, (N // cn,), [pl.BlockSpec((T, K), lambda j: (0, 0))] + w_specs + [chunk] * n_c,
                 [chunk] * len(out_dtypes), [jax.ShapeDtypeStruct((T, N), dt) for dt in out_dtypes],
                 (a, *ws, *comps), comm=comm, after=after)


def _mm_mstream(name, as_, ws, w_sel, w_form, extras, outs_desc, epilogue, tm=512, comm=None, after=None):
    T = as_[0].shape[0]
    tm = min(tm, T)
    n_a = len(as_)
    w_shapes = [w.shape[-2:] for w in ws]
    N = w_shapes[0][1] if w_form == "nn" else w_shapes[0][0]

    def main(ins, outs, scr):
        a_refs = ins[:n_a]
        w_refs = ins[n_a:2 * n_a]
        acc = None
        for a_ref, w_ref in zip(a_refs, w_refs):
            d = _dot(a_ref[...], w_ref[...], w_form)
            acc = d if acc is None else acc + d
        epilogue(acc, ins[2 * n_a:], outs)

    kind_spec = {"tile": pl.BlockSpec((tm, N), lambda i: (i, 0)), "vec": pl.BlockSpec((1, N), lambda i: (0, 0))}
    kind_shape = {"tile": (T, N), "vec": (1, N)}
    a_specs = [pl.BlockSpec((tm, a.shape[1]), lambda i: (i, 0)) for a in as_]
    w_specs = []
    for w, s in zip(ws, w_sel):
        if w.ndim == 3:
            w_specs.append(pl.BlockSpec((None,) + tuple(w.shape[1:]), functools.partial(lambda i, s: (s, 0, 0), s=s),
                                        pipeline_mode=pl.Buffered(1)))
        else:
            w_specs.append(pl.BlockSpec(tuple(w.shape), lambda i: (0, 0), pipeline_mode=pl.Buffered(1)))
    args = list(as_) + list(ws) + [e for e, _ in extras]
    return _call(name, main, (T // tm,), a_specs + w_specs + [kind_spec[k] for _, k in extras],
                 [kind_spec[k] for _, k in outs_desc],
                 [jax.ShapeDtypeStruct(kind_shape[k], dt) for dt, k in outs_desc], args, comm=comm, after=after)


def _residual_rms_epilogue(scale):
    def ep(acc, ex, outs):
        xv = ex[0][...] + scale * acc
        outs[0][...] = xv
        r = lax.rsqrt(jnp.mean(xv * xv, axis=-1, keepdims=True) + RMS_EPS)
        outs[1][...] = (xv * r * ex[1][...]).astype(outs[1].dtype)
    return ep


def _rms_bwd_epilogue(out_scale):
    def ep(acc, ex, outs):
        x_ref, g_ref, dres_ref = ex
        dx_ref, dxb_ref, dg_ref = outs
        xv = x_ref[...]
        r = lax.rsqrt(jnp.mean(xv * xv, axis=-1, keepdims=True) + RMS_EPS)
        xhat = xv * r

        @pl.when(pl.program_id(0) == 0)
        def _():
            dg_ref[...] = jnp.zeros_like(dg_ref)

        dg_ref[...] += jnp.sum(acc * xhat, axis=0, keepdims=True)
        dxhat = acc * g_ref[...]
        dx = r * (dxhat - xhat * jnp.mean(dxhat * xhat, axis=-1, keepdims=True)) + dres_ref[...]
        dx_ref[...] = dx
        dxb_ref[...] = (out_scale * dx).astype(dxb_ref.dtype)
    return ep


def _mm_tn(name, a, b, tmo, tno, out_dtype, tk=1024, comm=None, after=None):
    T, Ma = a.shape
    Nb = b.shape[1]
    tk = min(tk, T)
    nk = T // tk

    def main(ins, outs, scr):
        a_ref, b_ref = ins
        (acc_ref,) = scr
        k = pl.program_id(2)

        @pl.when(k == 0)
        def _():
            acc_ref[...] = jnp.zeros_like(acc_ref)

        acc_ref[...] += _dot(a_ref[...], b_ref[...], "tn")

        @pl.when(k == nk - 1)
        def _():
            outs[0][...] = acc_ref[...].astype(outs[0].dtype)

    (out,), extra = _call(
        name, main, (Ma // tmo, Nb // tno, nk),
        [pl.BlockSpec((tk, tmo), lambda i, j, k: (k, i)), pl.BlockSpec((tk, tno), lambda i, j, k: (k, j))],
        [pl.BlockSpec((tmo, tno), lambda i, j, k: (i, j))], [jax.ShapeDtypeStruct((Ma, Nb), out_dtype)],
        (a, b), scratch=[pltpu.VMEM((tmo, tno), F32)], comm=comm, after=after)
    return out, extra


def _swiglu_parts(g, u):
    s = _sigmoid(g)
    silu = g * s
    return [u * (s + silu * (1.0 - s)), silu, silu * u]


def _silu_mul_epilogue(dots, comps):
    g, u = dots
    return _swiglu_parts(g, u)


def _gate_parts_epilogue(dots, comps):
    (g,) = dots
    s = _sigmoid(g)
    silu = g * s
    return [s + silu * (1.0 - s), silu]


def _up_act_epilogue(dots, comps):
    (u,) = dots
    return [u * comps[0].astype(F32), u * comps[1].astype(F32)]


def _dact_epilogue(dots, comps):
    dact = dots[0].astype(BF16)
    return [dact * comps[0], dact * comps[1]]


def _identity_epilogue(dots, comps):
    return list(dots)


def _swap_halves(x):
    lane = lax.broadcasted_iota(jnp.int32, x.shape, 1)
    first = (lane % DK) < (DK // 2)
    return jnp.where(first, pltpu.roll(x, 128 - DK // 2, 1), pltpu.roll(x, DK // 2, 1))


def _rotary(t, cos, sin_signed):
    halves = []
    for p in range(QK_W // 128):
        th = t[:, 128 * p:128 * (p + 1)]
        halves.append(th * cos + _swap_halves(th) * sin_signed)
    return jnp.concatenate(halves, axis=1)


def _rotary_transposed(d, cos, sin_signed):
    halves = []
    for p in range(QK_W // 128):
        dh = d[:, 128 * p:128 * (p + 1)]
        halves.append(dh * cos + _swap_halves(dh * sin_signed))
    return jnp.concatenate(halves, axis=1)


def _log_sigmoid(x):
    return jnp.minimum(x, 0.0) - jnp.log(1.0 + jnp.exp(-jnp.abs(x)))


def _tri_sum(mask, x):
    tri = mask.astype(BF16)
    hi = x.astype(BF16)
    rest = x - hi.astype(F32)
    mid = rest.astype(BF16)
    lo = (rest - mid.astype(F32)).astype(BF16)
    return _dot(tri, hi, "nn") + _dot(tri, mid, "nn") + _dot(tri, lo, "nn")


def _attn_masks():
    row = lax.broadcasted_iota(jnp.int32, (SUPER, SUPER), 0)
    col = lax.broadcasted_iota(jnp.int32, (SUPER, SUPER), 1)
    same = (row // CHUNK) == (col // CHUNK)
    return row, col, same


def _group_inputs(grp, pr, cos, sin_signed, lg, wa2, ba):
    seg = lambda lo, width: pr[:, lo:lo + width].astype(F32)
    if grp == 0:
        q = _rotary(seg(C_RQ, QK_W), cos, sin_signed)
        k = _rotary(seg(C_RK, QK_W), cos, sin_signed) * (DK ** -0.5)
        v = pr[:, C_RV:C_RV + V_W]
        gate = seg(C_RG, V_W)
        pos = lax.broadcasted_iota(jnp.int32, (SUPER, QK_W), 0).astype(F32) + 1.0
        return q, k, v, gate, pos * lg, None, None
    q = seg(C_GQ, QK_W) * (DK ** -0.5)
    k = seg(C_GK, QK_W)
    v = pr[:, C_GV:C_GV + V_W]
    gate = seg(C_GG, V_W)
    glow = pr[:, C_GL:C_GL + GL_W]
    logit = _dot(glow.astype(BF16), wa2.astype(BF16), "nn") + ba
    la = _log_sigmoid(logit) * (1.0 / GATE_NORM)
    row, col, _ = _attn_masks()
    b_cum = _tri_sum(col <= row, la)
    return q, k, v, gate, b_cum, glow, logit


def _decay_factors(q, k, b_cum):
    c = b_cum[SUPER // 2 - 1:SUPER // 2, :]
    bl = b_cum[SUPER - 1:SUPER, :]
    e1 = jnp.exp(b_cum - c)
    e2 = jnp.exp(c - b_cum)
    e_b = jnp.exp(b_cum)
    e_l = jnp.exp(bl - b_cum)
    return dict(e1=e1, e2=e2, eb=e_b, el=e_l, ebl=jnp.exp(bl),
                qp=q * e1, qm=q * e2, kp=k * e1, km=k * e2, qs=q * e_b, kl=k * e_l)


def _state_block_mask():
    r = lax.broadcasted_iota(jnp.int32, (V_W, QK_W), 0)
    c = lax.broadcasted_iota(jnp.int32, (V_W, QK_W), 1)
    return (r // DV) == (c // DK)


def _attn_fwd(proj, cos, sin_signed, lg, wa2p, ba, gn_ret, gn_gla, x_res, w_out, g_next, comm=None):
    T = proj.shape[0]
    n_s = T // SUPER
    D = x_res.shape[1]

    def main(ins, outs, scr):
        pr_ref, cos_ref, sin_ref, lg_ref, wa2_ref, ba_ref, gr_ref, gg_ref, xres_ref, wout_ref, gnext_ref = ins
        o_ref, y_ref, st_ref, x_ref, h_ref = outs
        (s_ref,) = scr
        i = pl.program_id(0)

        @pl.when(i == 0)
        def _():
            s_ref[...] = jnp.zeros_like(s_ref)

        pr = pr_ref
        row, col, same = _attn_masks()
        m1 = col <= row
        m2 = jnp.logical_and(col > row, same)
        lane = lax.broadcasted_iota(jnp.int32, (1, QK_W), 1)
        blockmask = _state_block_mask()
        for grp in range(2):
            q, k, v, gate, b_cum, _, _ = _group_inputs(grp, pr, cos_ref[...], sin_ref[...], lg_ref[...],
                                                      wa2_ref[...], ba_ref[...])
            f = _decay_factors(q, k, b_cum)
            gn = gr_ref[...] if grp == 0 else gg_ref[...]
            s_prev = s_ref[grp]
            st_ref[0, grp] = s_prev
            o_inter = _dot(f["qs"].astype(BF16), s_prev.astype(BF16), "nt")
            kmb = f["km"].astype(BF16)
            kpb = f["kp"].astype(BF16)
            vb = v.astype(BF16)
            heads = [(lane // DK) == h for h in range(HEADS)]
            a1_all = _dot(jnp.concatenate([jnp.where(hm, f["qp"], 0.0).astype(BF16) for hm in heads], axis=0), kmb, "nt")
            a2_all = _dot(jnp.concatenate([jnp.where(hm, f["qm"], 0.0).astype(BF16) for hm in heads], axis=0), kpb, "nt")
            for h in range(HEADS):
                a1 = a1_all[h * SUPER:(h + 1) * SUPER]
                a2 = a2_all[h * SUPER:(h + 1) * SUPER]
                a = jnp.where(m1, a1, jnp.where(m2, a2, 0.0))
                lo = grp * V_W + h * DV
                o_h = _dot(a.astype(BF16), vb[:, h * DV:(h + 1) * DV], "nn") + o_inter[:, h * DV:(h + 1) * DV]
                o_ref[:, lo:lo + DV] = o_h
                r = lax.rsqrt(jnp.mean(o_h * o_h, axis=-1, keepdims=True) + RMS_EPS)
                gte = gate[:, h * DV:(h + 1) * DV]
                y = o_h * r * gn[:, h * DV:(h + 1) * DV] * (gte * _sigmoid(gte))
                y_ref[:, lo:lo + DV] = y.astype(y_ref.dtype)
            upd = _dot(vb, f["kl"].astype(BF16), "tn")
            s_ref[grp] = s_prev * f["ebl"] + jnp.where(blockmask, upd, 0.0)
        xv = xres_ref[...] + _dot(y_ref[...], wout_ref[...], "nn")
        x_ref[...] = xv
        r = lax.rsqrt(jnp.mean(xv * xv, axis=-1, keepdims=True) + RMS_EPS)
        h_ref[...] = (xv * r * gnext_ref[...]).astype(h_ref.dtype)

    const = lambda shape: pl.BlockSpec(shape, lambda i: tuple(0 for _ in shape))
    rows = lambda w: pl.BlockSpec((SUPER, w), lambda i: (i, 0))
    return _call(
        "attn_fwd", main, (n_s,),
        [rows(PROJ_W), rows(128), rows(128),
         const((1, QK_W)), const((GL_W, QK_W)), const((1, QK_W)), const((1, V_W)), const((1, V_W)),
         rows(D), const((2 * V_W, D)), const((1, D))],
        [rows(2 * V_W), rows(2 * V_W), pl.BlockSpec((1, 2, V_W, QK_W), lambda i: (i, 0, 0, 0)), rows(D), rows(D)],
        [jax.ShapeDtypeStruct((T, 2 * V_W), F32), jax.ShapeDtypeStruct((T, 2 * V_W), BF16),
         jax.ShapeDtypeStruct((n_s, 2, V_W, QK_W), F32), jax.ShapeDtypeStruct((T, D), F32),
         jax.ShapeDtypeStruct((T, D), BF16)],
        (proj, cos, sin_signed, lg, wa2p, ba, gn_ret, gn_gla, x_res, w_out, g_next),
        scratch=[pltpu.VMEM((2, V_W, QK_W), F32)], comm=comm)


def _attn_bwd(proj, cos, sin_signed, lg, wa2p, ba, gn_ret, gn_gla, o, dx, w_out, states, comm=None, after=None):
    T = proj.shape[0]
    n_s = T // SUPER
    D = dx.shape[1]

    def main(ins, outs, scr):
        pr_ref, cos_ref, sin_ref, lg_ref, wa2_ref, ba_ref, gr_ref, gg_ref, o_ref, dx_ref, wout_ref, st_ref = ins
        dp_ref, dgr_ref, dgg_ref, dba_ref, dwa_ref = outs
        (ds_ref, dy_ref) = scr
        i = pl.program_id(0)
        dy_ref[...] = _dot(dx_ref[...], wout_ref[...], "nt")

        @pl.when(i == 0)
        def _():
            ds_ref[...] = jnp.zeros_like(ds_ref)
            dgr_ref[...] = jnp.zeros_like(dgr_ref)
            dgg_ref[...] = jnp.zeros_like(dgg_ref)
            dba_ref[...] = jnp.zeros_like(dba_ref)
            dwa_ref[...] = jnp.zeros_like(dwa_ref)

        pr = pr_ref
        cos = cos_ref[...]
        sin_signed = sin_ref[...]
        row, col, same = _attn_masks()
        m1 = col <= row
        m2 = jnp.logical_and(col > row, same)
        m1t = row <= col
        m2t = jnp.logical_and(row > col, same)
        lane = lax.broadcasted_iota(jnp.int32, (1, QK_W), 1)
        blockmask = _state_block_mask()
        for grp in range(2):
            q, k, v, gate, b_cum, glow, logit = _group_inputs(grp, pr, cos, sin_signed, lg_ref[...],
                                                              wa2_ref[...], ba_ref[...])
            f = _decay_factors(q, k, b_cum)
            gn = gr_ref[...] if grp == 0 else gg_ref[...]
            dgn_ref = dgr_ref if grp == 0 else dgg_ref
            do_parts, dgate_parts, dgn_parts = [], [], []
            for h in range(HEADS):
                lo = grp * V_W + h * DV
                o_h = o_ref[:, lo:lo + DV]
                r = lax.rsqrt(jnp.mean(o_h * o_h, axis=-1, keepdims=True) + RMS_EPS)
                n = o_h * r
                gte = gate[:, h * DV:(h + 1) * DV]
                sg = _sigmoid(gte)
                dy_h = dy_ref[:, lo:lo + DV]
                gn_h = gn[:, h * DV:(h + 1) * DV]
                dgate_parts.append(dy_h * n * gn_h * (sg * (1.0 + gte * (1.0 - sg))))
                dz = dy_h * (gte * sg)
                dgn_parts.append(jnp.sum(dz * n, axis=0, keepdims=True))
                dn = dz * gn_h
                do_parts.append(r * (dn - n * jnp.mean(dn * n, axis=-1, keepdims=True)))
            dgn_ref[...] += jnp.concatenate(dgn_parts, axis=1)
            dgate = jnp.concatenate(dgate_parts, axis=1)
            do = jnp.concatenate(do_parts, axis=1)
            dob = do.astype(BF16)
            vb = v.astype(BF16)
            s_prev = st_ref[0, grp]
            ds_new = ds_ref[grp]
            dsb = ds_new.astype(BF16)
            qpb, qmb = f["qp"].astype(BF16), f["qm"].astype(BF16)
            kpb, kmb = f["kp"].astype(BF16), f["km"].astype(BF16)
            dv_parts = []
            heads = [(lane // DK) == h for h in range(HEADS)]
            qp_hs = [jnp.where(hm, f["qp"], 0.0).astype(BF16) for hm in heads]
            qm_hs = [jnp.where(hm, f["qm"], 0.0).astype(BF16) for hm in heads]
            kp_hs = [jnp.where(hm, f["kp"], 0.0).astype(BF16) for hm in heads]
            km_hs = [jnp.where(hm, f["km"], 0.0).astype(BF16) for hm in heads]
            km_stack, kp_stack = jnp.concatenate(km_hs, axis=0), jnp.concatenate(kp_hs, axis=0)
            at1_all = _dot(km_stack, qpb, "nt")
            at2_all = _dot(kp_stack, qmb, "nt")
            da1s, da2s, da1ts, da2ts = [], [], [], []
            for h in range(HEADS):
                at = jnp.where(m1t, at1_all[h * SUPER:(h + 1) * SUPER],
                               jnp.where(m2t, at2_all[h * SUPER:(h + 1) * SUPER], 0.0))
                do_h = dob[:, h * DV:(h + 1) * DV]
                v_h = vb[:, h * DV:(h + 1) * DV]
                dv_parts.append(_dot(at.astype(BF16), do_h, "nn"))
                da = _dot(do_h, v_h, "nt")
                dat = _dot(v_h, do_h, "nt")
                da1s.append(jnp.where(m1, da, 0.0).astype(BF16))
                da2s.append(jnp.where(m2, da, 0.0).astype(BF16))
                da1ts.append(jnp.where(m1t, dat, 0.0).astype(BF16))
                da2ts.append(jnp.where(m2t, dat, 0.0).astype(BF16))
            dqp = _dot(jnp.concatenate(da1s, axis=1), km_stack, "nn")
            dqm = _dot(jnp.concatenate(da2s, axis=1), kp_stack, "nn")
            dkm = _dot(jnp.concatenate(da1ts, axis=1), jnp.concatenate(qp_hs, axis=0), "nn")
            dkp = _dot(jnp.concatenate(da2ts, axis=1), jnp.concatenate(qm_hs, axis=0), "nn")
            klb = f["kl"].astype(BF16)
            qsb = f["qs"].astype(BF16)
            dqs = _dot(dob, s_prev.astype(BF16), "nn")
            dkl = _dot(vb, dsb, "nn")
            dv = jnp.concatenate(dv_parts, axis=1) + _dot(klb, dsb, "nt")
            ds_ref[grp] = ds_new * f["ebl"] + jnp.where(blockmask, _dot(dob, qsb, "tn"), 0.0)
            dq = dqp * f["e1"] + dqm * f["e2"] + dqs * f["eb"]
            dk = dkm * f["e2"] + dkp * f["e1"] + dkl * f["el"]
            if grp == 0:
                dq = _rotary_transposed(dq, cos, sin_signed)
                dk = _rotary_transposed(dk * (DK ** -0.5), cos, sin_signed)
                dp_ref[:, C_RQ:C_RQ + QK_W] = dq.astype(dp_ref.dtype)
                dp_ref[:, C_RK:C_RK + QK_W] = dk.astype(dp_ref.dtype)
                dp_ref[:, C_RV:C_RV + V_W] = dv.astype(dp_ref.dtype)
                dp_ref[:, C_RG:C_RG + V_W] = dgate.astype(dp_ref.dtype)
            else:
                dkl_kl = dkl * klb.astype(F32)
                db = (dqp * qpb.astype(F32) - dkm * kmb.astype(F32) - dqm * qmb.astype(F32)
                      + dkp * kpb.astype(F32) + dqs * qsb.astype(F32) - dkl_kl)
                last = (jnp.sum(dkl_kl, axis=0, keepdims=True)
                        + f["ebl"] * jnp.sum(s_prev * ds_new, axis=0, keepdims=True))
                rowq = lax.broadcasted_iota(jnp.int32, (SUPER, QK_W), 0)
                db = db + jnp.where(rowq == SUPER - 1, last, 0.0)
                dla = _tri_sum(col >= row, db)
                dlogit = dla * (1.0 / GATE_NORM) * (1.0 - _sigmoid(logit))
                dlb = dlogit.astype(BF16)
                dglow = _dot(dlb, wa2_ref[...].astype(BF16), "nt")
                dwa_ref[...] += _dot(glow.astype(BF16), dlb, "tn")
                dba_ref[...] += jnp.sum(dlogit, axis=0, keepdims=True)
                dp_ref[:, C_GQ:C_GQ + QK_W] = (dq * (DK ** -0.5)).astype(dp_ref.dtype)
                dp_ref[:, C_GK:C_GK + QK_W] = dk.astype(dp_ref.dtype)
                dp_ref[:, C_GV:C_GV + V_W] = dv.astype(dp_ref.dtype)
                dp_ref[:, C_GG:C_GG + V_W] = dgate.astype(dp_ref.dtype)
                dp_ref[:, C_GL:C_GL + GL_W] = dglow.astype(dp_ref.dtype)

    rev = lambda i: n_s - 1 - i
    const = lambda shape: pl.BlockSpec(shape, lambda i: tuple(0 for _ in shape))
    return _call(
        "attn_bwd", main, (n_s,),
        [pl.BlockSpec((SUPER, PROJ_W), lambda i: (rev(i), 0)),
         pl.BlockSpec((SUPER, 128), lambda i: (rev(i), 0)), pl.BlockSpec((SUPER, 128), lambda i: (rev(i), 0)),
         const((1, QK_W)), const((GL_W, QK_W)), const((1, QK_W)), const((1, V_W)), const((1, V_W)),
         pl.BlockSpec((SUPER, 2 * V_W), lambda i: (rev(i), 0)),
         pl.BlockSpec((SUPER, D), lambda i: (rev(i), 0)), const((2 * V_W, D)),
         pl.BlockSpec((1, 2, V_W, QK_W), lambda i: (rev(i), 0, 0, 0))],
        [pl.BlockSpec((SUPER, PROJ_W), lambda i: (rev(i), 0)),
         const((1, V_W)), const((1, V_W)), const((1, QK_W)), const((GL_W, QK_W))],
        [jax.ShapeDtypeStruct((T, PROJ_W), BF16),
         jax.ShapeDtypeStruct((1, V_W), F32), jax.ShapeDtypeStruct((1, V_W), F32),
         jax.ShapeDtypeStruct((1, QK_W), F32), jax.ShapeDtypeStruct((GL_W, QK_W), F32)],
        (proj, cos, sin_signed, lg, wa2p, ba, gn_ret, gn_gla, o, dx, w_out, states),
        scratch=[pltpu.VMEM((2, V_W, QK_W), F32), pltpu.VMEM((SUPER, 2 * V_W), F32)], comm=comm, after=after)


def _rotary_tables(T):
    half = DK // 2
    inv = ROPE_BASE ** (-jnp.arange(half, dtype=F32) * 2.0 / DK)
    ang = jnp.arange(T, dtype=F32)[:, None] * inv[None, :]
    cos, sin = jnp.cos(ang), jnp.sin(ang)
    cos_head = jnp.concatenate([cos, cos], axis=1)
    sin_head = jnp.concatenate([-sin, sin], axis=1)
    return jnp.tile(cos_head, (1, 128 // DK)), jnp.tile(sin_head, (1, 128 // DK))


def _sum_devices(name, gathered, m_per):
    def body(g_ref, o_ref):
        acc = g_ref[0:m_per, :]
        for k in range(1, N_DEV):
            acc = acc + g_ref[k * m_per:(k + 1) * m_per, :]
        o_ref[...] = acc

    return pl.pallas_call(body, name=name, out_shape=jax.ShapeDtypeStruct((m_per, 128), F32))(gathered)


def _owner_sums(name, items, owner, comm=None):
    counts = [1 + len(landed) for _, landed in items]

    def main(ins, outs, scr):
        at = 0
        for o_ref, n in zip(outs, counts):
            acc = ins[at][...].astype(F32)
            for l_ref in ins[at + 1:at + n]:
                for j in range(l_ref.shape[0]):
                    acc = acc + l_ref[j].astype(F32)
            o_ref[...] = acc
            at += n

    once = pl.Buffered(1)
    in_specs, out_specs, out_shape, args = [], [], [], []
    for grad, landed in items:
        R, C = grad.shape[-2:]
        in_specs.append(pl.BlockSpec((None, None, R, C), lambda i, s: (s[0], s[1], 0, 0), pipeline_mode=once))
        in_specs += [pl.BlockSpec(tuple(l.shape), lambda i, s: (0, 0, 0), pipeline_mode=once) for l in landed]
        out_specs.append(pl.BlockSpec((R, C), lambda i, s: (0, 0)))
        out_shape.append(jax.ShapeDtypeStruct((R, C), F32))
        args += [grad, *landed]
    return _call(name, main, (1,), in_specs, out_specs, out_shape, args, comm=comm, prefetch=owner)


def _adamw_group(name, items, n_blocks, comm=None):
    n = len(items)

    def main(ins, outs, scr):
        for p in range(n):
            g_ref, w_ref, m_ref, v_ref = ins[4 * p:4 * p + 4]
            d_ref, nm_ref, nv_ref = outs[3 * p:3 * p + 3]
            gv = g_ref[...]
            nm = ADAM_B1 * m_ref[...] + (1.0 - ADAM_B1) * gv
            nv = ADAM_B2 * v_ref[...] + (1.0 - ADAM_B2) * (gv * gv)
            m_hat = nm / (1.0 - ADAM_B1 ** ADAM_STEP)
            v_hat = nv / (1.0 - ADAM_B2 ** ADAM_STEP)
            d_ref[...] = -ADAM_LR * (m_hat / (jnp.sqrt(v_hat) + ADAM_EPS) + ADAM_WD * w_ref[...])
            nm_ref[...] = nm
            nv_ref[...] = nv

    in_specs, out_specs, out_shape, args = [], [], [], []
    for item in items:
        shape = item[1].shape
        assert shape[0] % n_blocks == 0
        spec = pl.BlockSpec((shape[0] // n_blocks,) + shape[1:], lambda i, nd=len(shape): (i,) + (0,) * (nd - 1))
        in_specs += [spec] * 4
        out_specs += [spec] * 3
        out_shape += [jax.ShapeDtypeStruct(shape, F32)] * 3
        args += list(item)
    outs, extra = _call(name, main, (n_blocks,), in_specs, out_specs, out_shape, args, comm=comm)
    return [tuple(outs[3 * p:3 * p + 3]) for p in range(n)], extra


def _stack_row_blocks(name, blocks, rows_out):
    n, r, C = blocks.shape
    assert blocks.dtype == BF16 and r % 2 == 0 and rows_out % 2 == 0
    h = r // 2

    def main(ins, outs, scr):
        (stage,) = scr
        stage[n * h:, :] = jnp.zeros((rows_out // 2 - n * h, C), jnp.uint32)
        for s in range(n):
            stage[s * h:(s + 1) * h, :] = pltpu.bitcast(ins[0][s], jnp.uint32)
        outs[0][...] = pltpu.bitcast(stage[...], BF16)

    (out,), _ = _call(
        name, main, (1,), [pl.BlockSpec((n, r, C), lambda i: (0, 0, 0), pipeline_mode=pl.Buffered(1))],
        [pl.BlockSpec((rows_out, C), lambda i: (0, 0))], [jax.ShapeDtypeStruct((rows_out, C), BF16)],
        (blocks,), scratch=[pltpu.VMEM((rows_out // 2, C), jnp.uint32)])
    return out


def _unstack_row_blocks(name, a, n, r):
    C = a.shape[1]
    assert a.dtype == BF16 and r % 2 == 0
    h = r // 2

    def main(ins, outs, scr):
        (stage,) = scr
        stage[...] = pltpu.bitcast(ins[0][...], jnp.uint32)
        for s in range(n):
            outs[0][s] = pltpu.bitcast(stage[s * h:(s + 1) * h, :], BF16)

    (out,), _ = _call(
        name, main, (1,), [pl.BlockSpec((n * r, C), lambda i: (0, 0), pipeline_mode=pl.Buffered(1))],
        [pl.BlockSpec((n, r, C), lambda i: (0, 0, 0))], [jax.ShapeDtypeStruct((n, r, C), BF16)],
        (a,), scratch=[pltpu.VMEM((n * h, C), jnp.uint32)])
    return out


SMALL_ORDER = ("ffn1", "mix", "ffn2", "final", "ret", "gla", "b_a")


def kernel(x, ffn1_norm_g, ffn1_w_gate, ffn1_w_up, ffn1_w_down, mix_norm_g, w_in, ret_norm_g, gla_w_a2, gla_b_a, gla_norm_g, w_out, ffn2_norm_g, ffn2_w_gate, ffn2_w_up, ffn2_w_down, final_norm_g, loss_target, m_ffn1_norm_g, m_ffn1_w_gate, m_ffn1_w_up, m_ffn1_w_down, m_mix_norm_g, m_w_in, m_ret_norm_g, m_gla_w_a2, m_gla_b_a, m_gla_norm_g, m_w_out, m_ffn2_norm_g, m_ffn2_w_gate, m_ffn2_w_up, m_ffn2_w_down, m_final_norm_g, v_ffn1_norm_g, v_ffn1_w_gate, v_ffn1_w_up, v_ffn1_w_down, v_mix_norm_g, v_w_in, v_ret_norm_g, v_gla_w_a2, v_gla_b_a, v_gla_norm_g, v_w_out, v_ffn2_norm_g, v_ffn2_w_gate, v_ffn2_w_up, v_ffn2_w_down, v_final_norm_g):
    xi, yi, ci = _coords()
    dev = 4 * xi + 2 * yi + ci
    owner = jnp.stack([2 * xi + yi, ci]).astype(jnp.int32)

    x0, target = x[0], loss_target[0]
    T, D = x0.shape
    fb = ffn1_w_gate.shape[2]
    ib = w_in.shape[2]
    ab = gla_w_a2.shape[2]
    F = N_DEV * fb
    cos, sin_signed = _rotary_tables(T)
    lg = jnp.repeat(jnp.log(1.0 - 2.0 ** (-5.0 - jnp.arange(HEADS, dtype=F32))), DK)[None, :]
    g_final = final_norm_g.reshape(1, D)

    g1_loc = ffn1_w_gate[0].T[None].astype(BF16)
    u1_loc = ffn1_w_up[0].T[None].astype(BF16)
    d1_loc = ffn1_w_down.astype(BF16)
    g2_loc = ffn2_w_gate[0].T[None].astype(BF16)
    u2_loc = ffn2_w_up[0].T[None].astype(BF16)
    d2_loc = ffn2_w_down.astype(BF16)
    in_loc = w_in[0].T.astype(BF16)
    out_loc = w_out[0].astype(BF16)

    h1, (g1,) = _rms_fwd("ffn1_rms", x0, ffn1_norm_g, comm=_AllGather([g1_loc], ["stack"]))
    g1 = g1.reshape(1, F, D)
    (dsl1, sl1), (u1,) = _mm_nstream("ffn1_gate", h1, [g1], [0], "nt", [], [BF16, BF16], _gate_parts_epilogue, cn=256,
                                     comm=_AllGather([u1_loc], ["stack"]))
    u1 = u1.reshape(1, F, D)
    (dsu1, act1), (d1,) = _mm_nstream("ffn1_up", h1, [u1], [0], "nt", [dsl1, sl1], [BF16, BF16],
                                      _up_act_epilogue, cn=256, comm=_AllGather([d1_loc], ["stack"]))
    d1 = d1.reshape(1, F, D)
    f32_tile, bf16_tile, f32_vec = (F32, "tile"), (BF16, "tile"), (F32, "vec")
    (x1, h2), (in_all, a_all) = _mm_mstream(
        "ffn1_down", [act1], [d1], [0], "nn", [(x0, "tile"), (mix_norm_g, "vec")], [f32_tile, bf16_tile],
        _residual_rms_epilogue(0.5), comm=_AllGather([in_loc, gla_w_a2[0]], ["plain", "plain"]))
    w_in_t = _stack_row_blocks("w_in_rows", in_all, PROJ_W)[None]
    wa2 = jnp.transpose(a_all, (1, 0, 2)).reshape(GATE_RANK, N_DEV * ab)
    wa2p = jnp.pad(wa2, ((0, GL_W - GATE_RANK), (0, 0)))

    (proj,), (g2, out_all) = _mm_nstream("mix_proj", h2, [w_in_t], [0], "nt", [], [BF16], _identity_epilogue, cn=640,
                                         comm=_AllGather([g2_loc, out_loc], ["stack", "plain"]))
    w_out_full = out_all.reshape(D, D)
    (o, ymix, states, x2, h3), (u2,) = _attn_fwd(proj, cos, sin_signed, lg, wa2p, gla_b_a, ret_norm_g, gla_norm_g,
                                                 x1, w_out_full, ffn2_norm_g, comm=_AllGather([u2_loc], ["stack"]))
    g2, u2 = g2.reshape(1, F, D), u2.reshape(1, F, D)

    (dsu2, sl2, act2), (d2,) = _mm_nstream(
        "ffn2_up", h3, [g2, u2], [0, 0], "nt", [], [BF16, BF16, BF16], _silu_mul_epilogue, cn=256,
        comm=_AllGather([d2_loc], ["stack"]))
    d2 = d2.reshape(1, F, D)
    (dx3, dy3b, d_final, loss), _ = _mm_mstream(
        "ffn2_down", [act2], [d2], [0], "nn", [(x2, "tile"), (g_final, "vec"), (target, "tile")],
        [f32_tile, bf16_tile, f32_vec, f32_vec], _final_loss_epilogue(0.5, 0.5))

    sent = {}

    def send(**grads):
        started = _send_to_owners("send_" + "_".join(grads), list(grads.values()))
        sent.update(zip(grads, started))
        return started[0][4]

    dwd2, _ = _mm_tn("ffn2b_dwd", act2, dy3b, F // 2, D, BF16)
    (dgate2, dup2), _ = _mm_nstream("ffn2b_dact", dy3b, [d2], [0], "nt", [dsu2, sl2], [BF16, BF16],
                                    _dact_epilogue, cn=256)
    dwg2, _ = _mm_tn("ffn2b_dwg", dgate2, h3, F // 2, D, BF16)
    dwu2, _ = _mm_tn("ffn2b_dwu", dup2, h3, F // 2, D, BF16)
    tok = send(wd2=dwd2.reshape(4, 2, fb, D), wg2=dwg2.reshape(4, 2, fb, D), wu2=dwu2.reshape(4, 2, fb, D))
    rms_outs = [f32_tile, bf16_tile, f32_vec]
    (dx2, dx2b, d_g2), _ = _mm_mstream(
        "ffn2b_dh", [dgate2, dup2], [g2, u2], [0, 0], "nn", [(x2, "tile"), (ffn2_norm_g, "vec"), (dx3, "tile")],
        rms_outs, _rms_bwd_epilogue(1.0), after=tok)

    dwout, _ = _mm_tn("mixb_dwout", ymix, dx2b, D, D, BF16)
    (dproj, d_ret, d_gla, d_ba, d_wa2p), _ = _attn_bwd(
        proj, cos, sin_signed, lg, wa2p, gla_b_a, ret_norm_g, gla_norm_g, o, dx2b, w_out_full, states)
    dwin_t, _ = _mm_tn("mixb_dwin", dproj, h2, 640, D, BF16, tk=2048)
    tok = send(wout=dwout.reshape(4, 2, D // N_DEV, D), win=_unstack_row_blocks("dwin_blocks", dwin_t, N_DEV, ib).reshape(4, 2, ib, D))
    (dx1, dy1b, d_gmix), _ = _mm_mstream(
        "mixb_dh", [dproj], [w_in_t], [0], "nn", [(x1, "tile"), (mix_norm_g, "vec"), (dx2, "tile")],
        rms_outs, _rms_bwd_epilogue(0.5), after=tok)

    dwd1, _ = _mm_tn("ffn1b_dwd", act1, dy1b, F // 2, D, BF16)
    tok = send(wd1=dwd1.reshape(4, 2, fb, D))
    (dgate1, dup1), _ = _mm_nstream("ffn1b_dact", dy1b, [d1], [0], "nt", [dsu1, sl1], [BF16, BF16],
                                    _dact_epilogue, cn=256, after=tok)
    dwg1, _ = _mm_tn("ffn1b_dwg", dgate1, h1, F // 2, D, BF16)
    tok = send(wg1=dwg1.reshape(4, 2, fb, D))
    dwu1, _ = _mm_tn("ffn1b_dwu", dup1, h1, F // 2, D, BF16, after=tok)
    tok = send(wu1=dwu1.reshape(4, 2, fb, D))
    (dx0, _, d_g1), _ = _mm_mstream(
        "ffn1b_dh", [dgate1, dup1], [g1, u1], [0, 0], "nn", [(x0, "tile"), (ffn1_norm_g, "vec"), (dx1, "tile")],
        rms_outs, _rms_bwd_epilogue(1.0), after=tok)

    small = dict(ffn1=d_g1, mix=d_gmix, ffn2=d_g2, final=d_final, ret=d_ret, gla=d_gla, b_a=d_ba)
    flat = jnp.concatenate([small[k].reshape(-1) for k in SMALL_ORDER]
                           + [d_wa2p[:GATE_RANK].reshape(-1), loss[0, :128]])
    rows = -(-flat.shape[0] // 128)
    rows = -(-rows // 8) * 8
    packed = jnp.pad(flat, (0, rows * 128 - flat.shape[0])).reshape(rows, 128)

    transposed = ("ffn1_w_gate", "ffn1_w_up", "ffn2_w_gate", "ffn2_w_up")

    def to_2d(nm, a):
        if nm == "w_in":
            return jnp.transpose(a, (2, 0, 1))
        if nm in transposed:
            return a[0].T
        return a.reshape((1, a.shape[0]) if a.ndim == 1 else a.shape[-2:])

    def from_2d(nm, a):
        if nm == "w_in":
            return jnp.transpose(a, (1, 2, 0))
        return a.T[None] if nm in transposed else a.reshape(params[nm][0].shape)

    def arrived(nm, after):
        grad, landed = _await_owners("await_" + nm, sent[nm], after)
        return grad, [landed]

    sums_a, (gathered,) = _owner_sums(
        "sum_a", [arrived(nm, dx0) for nm in ("wg2", "wu2", "wd2", "win", "wout")], owner,
        comm=_AllGather([packed], ["plain"]))
    params = dict(
        ffn2_w_gate=(ffn2_w_gate, m_ffn2_w_gate, v_ffn2_w_gate), ffn2_w_up=(ffn2_w_up, m_ffn2_w_up, v_ffn2_w_up),
        ffn2_w_down=(ffn2_w_down, m_ffn2_w_down, v_ffn2_w_down), w_in=(w_in, m_w_in, v_w_in),
        w_out=(w_out, m_w_out, v_w_out), ffn1_w_gate=(ffn1_w_gate, m_ffn1_w_gate, v_ffn1_w_gate),
        ffn1_w_up=(ffn1_w_up, m_ffn1_w_up, v_ffn1_w_up), ffn1_w_down=(ffn1_w_down, m_ffn1_w_down, v_ffn1_w_down),
        ffn1_norm_g=(ffn1_norm_g, m_ffn1_norm_g, v_ffn1_norm_g), mix_norm_g=(mix_norm_g, m_mix_norm_g, v_mix_norm_g),
        ret_norm_g=(ret_norm_g, m_ret_norm_g, v_ret_norm_g), gla_w_a2=(gla_w_a2, m_gla_w_a2, v_gla_w_a2),
        gla_b_a=(gla_b_a, m_gla_b_a, v_gla_b_a), gla_norm_g=(gla_norm_g, m_gla_norm_g, v_gla_norm_g),
        ffn2_norm_g=(ffn2_norm_g, m_ffn2_norm_g, v_ffn2_norm_g), final_norm_g=(final_norm_g, m_final_norm_g, v_final_norm_g))
    grads, updates = {}, {}

    def run_adam(name, names, grad_2d, n_blocks):
        items = [(grad_2d[nm],) + tuple(to_2d(nm, a) for a in params[nm]) for nm in names]
        res, _ = _adamw_group(name, items, n_blocks)
        for nm, r in zip(names, res):
            grads[nm] = from_2d(nm, grad_2d[nm])
            updates[nm] = tuple(from_2d(nm, a) for a in r)
        return res

    grads_a = {"ffn2_w_gate": sums_a[0], "ffn2_w_up": sums_a[1], "ffn2_w_down": sums_a[2], "w_out": sums_a[4]}
    run_adam("adamw_w_in", ["w_in"], {"w_in": sums_a[3][:, None, :]}, 2)
    done_a = run_adam("adamw_a", list(grads_a), grads_a, 4)[0][0]
    sums_b, _ = _owner_sums("sum_b", [arrived(nm, done_a) for nm in ("wg1", "wu1", "wd1")], owner)
    grads_b = {"ffn1_w_gate": sums_b[0], "ffn1_w_up": sums_b[1], "ffn1_w_down": sums_b[2]}
    run_adam("adamw_b", list(grads_b), grads_b, 4)

    total = _sum_devices("sum_small", gathered.reshape(N_DEV * rows, 128), rows).reshape(-1)
    sizes = [small[k].size for k in SMALL_ORDER] + [GATE_RANK * QK_W, 128]
    offs = [0]
    for s in sizes:
        offs.append(offs[-1] + s)
    pieces = [total[offs[i]:offs[i + 1]] for i in range(len(sizes))]
    g_small = {k: pieces[i].reshape(small[k].shape) for i, k in enumerate(SMALL_ORDER)}
    g_wa2_full = pieces[len(SMALL_ORDER)].reshape(GATE_RANK, QK_W)
    g_wa2 = lax.dynamic_slice(g_wa2_full, (0, dev * ab), (GATE_RANK, ab))
    loss_total = pieces[len(SMALL_ORDER) + 1][0]

    small_grads = {"ffn1_norm_g": g_small["ffn1"], "mix_norm_g": g_small["mix"], "ret_norm_g": g_small["ret"],
                   "gla_w_a2": g_wa2, "gla_b_a": g_small["b_a"], "gla_norm_g": g_small["gla"],
                   "ffn2_norm_g": g_small["ffn2"], "final_norm_g": g_small["final"]}
    run_adam("adamw_small", list(small_grads), small_grads, 1)

    order = ("ffn1_norm_g", "ffn1_w_gate", "ffn1_w_up", "ffn1_w_down", "mix_norm_g", "w_in", "ret_norm_g", "gla_w_a2",
             "gla_b_a", "gla_norm_g", "w_out", "ffn2_norm_g", "ffn2_w_gate", "ffn2_w_up", "ffn2_w_down", "final_norm_g")
    return (loss_total, dx0[None], *[grads[nm] for nm in order], *[updates[nm][0] for nm in order],
            *[updates[nm][1] for nm in order], *[updates[nm][2] for nm in order])
```

```python
import functools
import math

import jax
import jax.numpy as jnp
from jax import lax
from jax.experimental import pallas as pl
from jax.experimental.pallas import tpu as pltpu

F32 = jnp.float32
BF16 = jnp.bfloat16
MESH = pl.DeviceIdType.MESH
HBM = pl.BlockSpec(memory_space=pltpu.HBM)

N_DEV = 8
RMS_EPS = 1e-6
ROPE_BASE = 10000.0
HEADS = 4
DK = 64
DV = 128
QK_W = HEADS * DK
V_W = HEADS * DV
GATE_RANK = 16
GATE_NORM = 16.0
CHUNK = 64
SUPER = 256
PROJ_W = 3200
C_RQ, C_RK, C_RV, C_RG, C_GQ, C_GK, C_GV, C_GG, C_GL = 0, 256, 512, 1024, 1536, 1792, 2048, 2560, 3072
GL_W = PROJ_W - C_GL
ADAM_LR, ADAM_B1, ADAM_B2, ADAM_EPS, ADAM_WD, ADAM_STEP = 0.001, 0.9, 0.999, 1e-08, 0.01, 10
VMEM_LIMIT_V7X = 52 * 1024 * 1024


def _cparams(**kw):
    return pltpu.CompilerParams(vmem_limit_bytes=VMEM_LIMIT_V7X, **kw)


def _dot(a, b, form, precision=None):
    dims = {"nn": (((1,), (0,)), ((), ())), "nt": (((1,), (1,)), ((), ())), "tn": (((0,), (0,)), ((), ()))}[form]
    return lax.dot_general(a, b, dims, preferred_element_type=F32, precision=precision)


def _sigmoid(x):
    return 1.0 / (1.0 + jnp.exp(-x))


def _coords():
    return lax.axis_index("x"), lax.axis_index("y"), lax.axis_index("c")


class _NoComm:
    inputs, out_shapes, scratch = (), (), ()


class _AllGather:
    def __init__(self, arrays, kinds):
        self.inputs = tuple(arrays)
        self.kinds = tuple(kinds)
        n = len(arrays)
        self.out_shapes = tuple(
            jax.ShapeDtypeStruct((a.shape[0], N_DEV) + a.shape[1:] if k == "stack" else (N_DEV,) + a.shape, a.dtype)
            for a, k in zip(arrays, kinds))
        self.scratch = (pltpu.SemaphoreType.DMA((n, 7)), pltpu.SemaphoreType.DMA((n, 7)),
                        pltpu.SemaphoreType.DMA((n,)))

    def _ctx(self, srcs, outs, sems):
        send_sems, recv_sems, local_sems = sems
        x, y, c = _coords()
        me, sibling = (x, y, c), (x, y, 1 - c)
        chips = [(1 - x, y), (x, 1 - y), (1 - x, 1 - y)]

        def blk(m, dev):
            k = 4 * dev[0] + 2 * dev[1] + dev[2]
            return outs[m].at[:, k] if self.kinds[m] == "stack" else outs[m].at[k]

        def copy(m, s, block, to, src=None):
            return pltpu.make_async_remote_copy(
                src_ref=blk(m, block) if src is None else src, dst_ref=blk(m, block),
                send_sem=send_sems.at[m, s], recv_sem=recv_sems.at[m, s], device_id=to, device_id_type=MESH)

        def mine(m):
            return pltpu.make_async_copy(srcs[m], blk(m, me), local_sems.at[m])

        def first(m):
            return [copy(m, 0, me, sibling, src=srcs[m])] + [
                copy(m, 1 + j, me, (*chip, c), src=srcs[m]) for j, chip in enumerate(chips)]

        return me, sibling, chips, c, copy, mine, first

    def start(self, srcs, outs, sems):
        me, sibling, chips, c, copy, mine, first = self._ctx(srcs, outs, sems)
        for m in range(len(srcs)):
            mine(m).start()
            for cp in first(m):
                cp.start()

    def mid(self, srcs, outs, sems):
        me, sibling, chips, c, copy, mine, first = self._ctx(srcs, outs, sems)
        for j, chip in enumerate(chips):
            for m in range(len(srcs)):
                copy(m, 1 + j, (*chip, c), me).wait_recv()
                copy(m, 4 + j, (*chip, c), sibling).start()

    def finish(self, srcs, outs, sems):
        me, sibling, chips, c, copy, mine, first = self._ctx(srcs, outs, sems)
        for m in range(len(srcs)):
            copy(m, 0, sibling, me).wait_recv()
            for j, chip in enumerate(chips):
                copy(m, 4 + j, (*chip, 1 - c), me).wait_recv()
            for cp in first(m):
                cp.wait_send()
            for j, chip in enumerate(chips):
                copy(m, 4 + j, (*chip, c), sibling).wait_send()
            mine(m).wait()


RELATIONS = ((0, 0, 1), (1, 0, 0), (0, 1, 0), (1, 1, 0), (1, 0, 1), (0, 1, 1), (1, 1, 1))
SEM = pl.BlockSpec(memory_space=pltpu.SEMAPHORE)
SPLIT_PARAMS = dict(has_side_effects=pltpu.SideEffectType.DATAFLOW_SIDE_EFFECTING)


def _owner_copies(grad_ref, land_ref, send_sems, recv_sems):
    x, y, c = _coords()
    copies = []
    for s, (fx, fy, fc) in enumerate(RELATIONS):
        px = 1 - x if fx else x
        py = 1 - y if fy else y
        pc = 1 - c if fc else c
        copies.append(pltpu.make_async_remote_copy(
            src_ref=grad_ref.at[2 * px + py, pc], dst_ref=land_ref.at[s], send_sem=send_sems.at[s],
            recv_sem=recv_sems.at[s], device_id=(px, py, pc), device_id_type=MESH))
    return copies


def _send_to_owners(name, grads):
    n, k = len(RELATIONS), len(grads)
    land_shapes = [(n,) + g.shape[2:] for g in grads]

    def body(*refs):
        ins, outs = refs[:2 * k], refs[2 * k:]
        for j in range(k):
            for cp in _owner_copies(ins[2 * j], ins[2 * j + 1], outs[4 * j], outs[4 * j + 1]):
                cp.start()
        outs[-1][...] = jnp.zeros_like(outs[-1])

    out_shape, out_specs, args, aliases = [], [], [], {}
    for j, (g, land_shape) in enumerate(zip(grads, land_shapes)):
        out_shape += [pltpu.SemaphoreType.DMA((n,)), pltpu.SemaphoreType.DMA((n,)), pltpu.HBM(g.shape, g.dtype),
                      pltpu.HBM(land_shape, g.dtype)]
        out_specs += [SEM, SEM, HBM, HBM]
        args += [pltpu.with_memory_space_constraint(g, pltpu.HBM),
                 pltpu.with_memory_space_constraint(lax.empty(land_shape, g.dtype), pltpu.HBM)]
        aliases.update({2 * j: 4 * j + 2, 2 * j + 1: 4 * j + 3})
    res = pl.pallas_call(
        body, name=name, out_shape=out_shape + [jax.ShapeDtypeStruct((8, 128), F32)],
        in_specs=[HBM] * (2 * k), out_specs=out_specs + [pl.BlockSpec(memory_space=pltpu.VMEM)],
        input_output_aliases=aliases, compiler_params=pltpu.CompilerParams(**SPLIT_PARAMS),
    )(*args)
    return [tuple(res[4 * j:4 * j + 4]) + (res[-1],) for j in range(k)]


def _await_owners(name, started, after):
    send_sems, recv_sems, g_thru, land_thru, _ = started

    def body(g_ref, land_ref, send_sems, recv_sems, after_ref, g_out, land_out):
        for cp in _owner_copies(g_ref, land_ref, send_sems, recv_sems):
            cp.wait_send()
            cp.wait_recv()

    return pl.pallas_call(
        body, name=name, out_shape=(pltpu.HBM(g_thru.shape, g_thru.dtype), pltpu.HBM(land_thru.shape, land_thru.dtype)),
        in_specs=(HBM, HBM, SEM, SEM, pl.BlockSpec(memory_space=pl.ANY)), out_specs=(HBM, HBM),
        input_output_aliases={0: 0, 1: 1}, compiler_params=pltpu.CompilerParams(**SPLIT_PARAMS),
    )(g_thru, land_thru, send_sems, recv_sems, after)


def _call(name, main, grid, in_specs, out_specs, out_shape, args, scratch=(), comm=None, prefetch=None, after=None):
    comm = comm or _NoComm()
    n_main = len(in_specs)
    if after is not None:
        in_specs = list(in_specs) + [pl.BlockSpec(after.shape, lambda *_: (0,) * after.ndim)]
        args = tuple(args) + (after,)
    counts = [len(in_specs), len(comm.inputs), len(out_shape), len(comm.out_shapes), len(scratch), len(comm.scratch)]
    n_steps = math.prod(grid)
    hosted = bool(comm.inputs)

    def body(*refs):
        if prefetch is not None:
            refs = refs[1:]
        parts, at = [], 0
        for n in counts:
            parts.append(refs[at:at + n])
            at += n
        ins, c_in, outs, c_out, scr, c_scr = parts
        ins = ins[:n_main]
        step = pl.program_id(0)
        for d in range(1, len(grid)):
            step = step * grid[d] + pl.program_id(d)
        if hosted:
            @pl.when(step == 0)
            def _():
                comm.start(c_in, c_out, c_scr)
        main(ins, outs, scr)
        if hosted:
            @pl.when(step == max(n_steps - 2, 0))
            def _():
                comm.mid(c_in, c_out, c_scr)

            @pl.when(step == n_steps - 1)
            def _():
                comm.finish(c_in, c_out, c_scr)

    all_in = list(in_specs) + [HBM] * counts[1]
    all_out = list(out_specs) + [HBM] * counts[3]
    all_scratch = list(scratch) + list(comm.scratch)
    shapes = list(out_shape) + list(comm.out_shapes)
    if prefetch is None:
        res = pl.pallas_call(body, name=name, grid=grid, in_specs=all_in, out_specs=all_out, out_shape=shapes,
                             scratch_shapes=all_scratch, compiler_params=_cparams())(*args, *comm.inputs)
    else:
        res = pl.pallas_call(
            body, name=name, out_shape=shapes,
            grid_spec=pltpu.PrefetchScalarGridSpec(num_scalar_prefetch=1, grid=grid, in_specs=all_in,
                                                   out_specs=all_out, scratch_shapes=all_scratch),
            compiler_params=_cparams())(prefetch, *args, *comm.inputs)
    return res[:counts[2]], res[counts[2]:]


def _rms_fwd(name, x, g, comm=None):
    T, D = x.shape
    tm = min(T, 512)

    def main(ins, outs, scr):
        x_ref, g_ref = ins
        xv = x_ref[...]
        r = lax.rsqrt(jnp.mean(xv * xv, axis=-1, keepdims=True) + RMS_EPS)
        outs[0][...] = (xv * r * g_ref[...]).astype(outs[0].dtype)

    tile = pl.BlockSpec((tm, D), lambda i: (i, 0))
    (h,), extra = _call(name, main, (T // tm,), [tile, pl.BlockSpec((1, D), lambda i: (0, 0))], [tile],
                        [jax.ShapeDtypeStruct((T, D), BF16)], (x, g), comm=comm)
    return h, extra


def _final_loss_epilogue(scale, out_scale):
    def ep(acc, ex, outs):
        res_ref, g_ref, t_ref = ex
        dx_ref, dxb_ref, dg_ref, loss_ref = outs
        n = acc.shape[-1]
        xv = res_ref[...] + scale * acc
        r = lax.rsqrt(jnp.mean(xv * xv, axis=-1, keepdims=True) + RMS_EPS)
        xhat = xv * r
        err = xhat * g_ref[...] - t_ref[...]

        @pl.when(pl.program_id(0) == 0)
        def _():
            dg_ref[...] = jnp.zeros_like(dg_ref)
            loss_ref[...] = jnp.zeros_like(loss_ref)

        loss_ref[...] += jnp.broadcast_to(jnp.sum(err * err) * (0.5 / n), loss_ref.shape)
        dy = err * (1.0 / n)
        dg_ref[...] += jnp.sum(dy * xhat, axis=0, keepdims=True)
        dxhat = dy * g_ref[...]
        dx = r * (dxhat - xhat * jnp.mean(dxhat * xhat, axis=-1, keepdims=True))
        dx_ref[...] = dx
        dxb_ref[...] = (out_scale * dx).astype(dxb_ref.dtype)
    return ep


def _mm_nstream(name, a, ws, w_sel, w_form, comps, out_dtypes, epilogue, cn, rows=1024, comm=None, after=None):
    T, K = a.shape
    N = ws[0].shape[1]
    rows = min(rows, T)
    assert N % cn == 0 and T % rows == 0
    n_w, n_c = len(ws), len(comps)

    def main(ins, outs, scr):
        a_ref = ins[0]
        w_refs = ins[1:1 + n_w]
        c_refs = ins[1 + n_w:1 + n_w + n_c]

        for r in range(T // rows):
            sl = slice(r * rows, (r + 1) * rows)
            a_blk = a_ref[sl, :]
            dots = [_dot(a_blk, w_ref[...], w_form) for w_ref in w_refs]
            res = epilogue(dots, [c_ref[sl, :] for c_ref in c_refs])
            for o_ref, o in zip(outs, res):
                o_ref[sl, :] = o.astype(o_ref.dtype)

    if w_form == "nt":
        w_specs = [pl.BlockSpec((None, cn, K), functools.partial(lambda j, s: (s, j, 0), s=s)) for s in w_sel]
    else:
        w_specs = [pl.BlockSpec((K, cn), lambda j: (0, j)) for _ in ws]
    chunk = pl.BlockSpec((T, cn), lambda j: (0, j))
    return _call(name, main, (N // cn,), [pl.BlockSpec((T, K), lambda j: (0, 0))] + w_specs + [chunk] * n_c,
                 [chunk] * len(out_dtypes), [jax.ShapeDtypeStruct((T, N), dt) for dt in out_dtypes],
                 (a, *ws, *comps), comm=comm, after=after)


def _mm_mstream(name, as_, ws, w_sel, w_form, extras, outs_desc, epilogue, tm=512, comm=None, after=None):
    T = as_[0].shape[0]
    tm = min(tm, T)
    n_a = len(as_)
    w_shapes = [w.shape[-2:] for w in ws]
    N = w_shapes[0][1] if w_form == "nn" else w_shapes[0][0]

    def main(ins, outs, scr):
        a_refs = ins[:n_a]
        w_refs = ins[n_a:2 * n_a]
        acc = None
        for a_ref, w_ref in zip(a_refs, w_refs):
            d = _dot(a_ref[...], w_ref[...], w_form)
            acc = d if acc is None else acc + d
        epilogue(acc, ins[2 * n_a:], outs)

    kind_spec = {"tile": pl.BlockSpec((tm, N), lambda i: (i, 0)), "vec": pl.BlockSpec((1, N), lambda i: (0, 0))}
    kind_shape = {"tile": (T, N), "vec": (1, N)}
    a_specs = [pl.BlockSpec((tm, a.shape[1]), lambda i: (i, 0)) for a in as_]
    w_specs = []
    for w, s in zip(ws, w_sel):
        if w.ndim == 3:
            w_specs.append(pl.BlockSpec((None,) + tuple(w.shape[1:]), functools.partial(lambda i, s: (s, 0, 0), s=s),
                                        pipeline_mode=pl.Buffered(1)))
        else:
            w_specs.append(pl.BlockSpec(tuple(w.shape), lambda i: (0, 0), pipeline_mode=pl.Buffered(1)))
    args = list(as_) + list(ws) + [e for e, _ in extras]
    return _call(name, main, (T // tm,), a_specs + w_specs + [kind_spec[k] for _, k in extras],
                 [kind_spec[k] for _, k in outs_desc],
                 [jax.ShapeDtypeStruct(kind_shape[k], dt) for dt, k in outs_desc], args, comm=comm, after=after)


def _residual_rms_epilogue(scale):
    def ep(acc, ex, outs):
        xv = ex[0][...] + scale * acc
        outs[0][...] = xv
        r = lax.rsqrt(jnp.mean(xv * xv, axis=-1, keepdims=True) + RMS_EPS)
        outs[1][...] = (xv * r * ex[1][...]).astype(outs[1].dtype)
    return ep


def _rms_bwd_epilogue(out_scale):
    def ep(acc, ex, outs):
        x_ref, g_ref, dres_ref = ex
        dx_ref, dxb_ref, dg_ref = outs
        xv = x_ref[...]
        r = lax.rsqrt(jnp.mean(xv * xv, axis=-1, keepdims=True) + RMS_EPS)
        xhat = xv * r

        @pl.when(pl.program_id(0) == 0)
        def _():
            dg_ref[...] = jnp.zeros_like(dg_ref)

        dg_ref[...] += jnp.sum(acc * xhat, axis=0, keepdims=True)
        dxhat = acc * g_ref[...]
        dx = r * (dxhat - xhat * jnp.mean(dxhat * xhat, axis=-1, keepdims=True)) + dres_ref[...]
        dx_ref[...] = dx
        dxb_ref[...] = (out_scale * dx).astype(dxb_ref.dtype)
    return ep


def _mm_tn(name, a, b, tmo, tno, out_dtype, tk=1024, comm=None, after=None):
    T, Ma = a.shape
    Nb = b.shape[1]
    tk = min(tk, T)
    nk = T // tk

    def main(ins, outs, scr):
        a_ref, b_ref = ins
        (acc_ref,) = scr
        k = pl.program_id(2)

        @pl.when(k == 0)
        def _():
            acc_ref[...] = jnp.zeros_like(acc_ref)

        acc_ref[...] += _dot(a_ref[...], b_ref[...], "tn")

        @pl.when(k == nk - 1)
        def _():
            outs[0][...] = acc_ref[...].astype(outs[0].dtype)

    (out,), extra = _call(
        name, main, (Ma // tmo, Nb // tno, nk),
        [pl.BlockSpec((tk, tmo), lambda i, j, k: (k, i)), pl.BlockSpec((tk, tno), lambda i, j, k: (k, j))],
        [pl.BlockSpec((tmo, tno), lambda i, j, k: (i, j))], [jax.ShapeDtypeStruct((Ma, Nb), out_dtype)],
        (a, b), scratch=[pltpu.VMEM((tmo, tno), F32)], comm=comm, after=after)
    return out, extra


def _swiglu_parts(g, u):
    s = _sigmoid(g)
    silu = g * s
    return [u * (s + silu * (1.0 - s)), silu, silu * u]


def _silu_mul_epilogue(dots, comps):
    g, u = dots
    return _swiglu_parts(g, u)


def _gate_parts_epilogue(dots, comps):
    (g,) = dots
    s = _sigmoid(g)
    silu = g * s
    return [s + silu * (1.0 - s), silu]


def _up_act_epilogue(dots, comps):
    (u,) = dots
    return [u * comps[0].astype(F32), u * comps[1].astype(F32)]


def _dact_epilogue(dots, comps):
    dact = dots[0].astype(BF16)
    return [dact * comps[0], dact * comps[1]]


def _identity_epilogue(dots, comps):
    return list(dots)


def _swap_halves(x):
    lane = lax.broadcasted_iota(jnp.int32, x.shape, 1)
    first = (lane % DK) < (DK // 2)
    return jnp.where(first, pltpu.roll(x, 128 - DK // 2, 1), pltpu.roll(x, DK // 2, 1))


def _rotary(t, cos, sin_signed):
    halves = []
    for p in range(QK_W // 128):
        th = t[:, 128 * p:128 * (p + 1)]
        halves.append(th * cos + _swap_halves(th) * sin_signed)
    return jnp.concatenate(halves, axis=1)


def _rotary_transposed(d, cos, sin_signed):
    halves = []
    for p in range(QK_W // 128):
        dh = d[:, 128 * p:128 * (p + 1)]
        halves.append(dh * cos + _swap_halves(dh * sin_signed))
    return jnp.concatenate(halves, axis=1)


def _log_sigmoid(x):
    return jnp.minimum(x, 0.0) - jnp.log(1.0 + jnp.exp(-jnp.abs(x)))


def _tri_sum(mask, x):
    tri = mask.astype(BF16)
    hi = x.astype(BF16)
    rest = x - hi.astype(F32)
    mid = rest.astype(BF16)
    lo = (rest - mid.astype(F32)).astype(BF16)
    return _dot(tri, hi, "nn") + _dot(tri, mid, "nn") + _dot(tri, lo, "nn")


def _attn_masks():
    row = lax.broadcasted_iota(jnp.int32, (SUPER, SUPER), 0)
    col = lax.broadcasted_iota(jnp.int32, (SUPER, SUPER), 1)
    same = (row // CHUNK) == (col // CHUNK)
    return row, col, same


def _group_inputs(grp, pr, cos, sin_signed, lg, wa2, ba):
    seg = lambda lo, width: pr[:, lo:lo + width].astype(F32)
    if grp == 0:
        q = _rotary(seg(C_RQ, QK_W), cos, sin_signed)
        k = _rotary(seg(C_RK, QK_W), cos, sin_signed) * (DK ** -0.5)
        v = pr[:, C_RV:C_RV + V_W]
        gate = seg(C_RG, V_W)
        pos = lax.broadcasted_iota(jnp.int32, (SUPER, QK_W), 0).astype(F32) + 1.0
        return q, k, v, gate, pos * lg, None, None
    q = seg(C_GQ, QK_W) * (DK ** -0.5)
    k = seg(C_GK, QK_W)
    v = pr[:, C_GV:C_GV + V_W]
    gate = seg(C_GG, V_W)
    glow = pr[:, C_GL:C_GL + GL_W]
    logit = _dot(glow.astype(BF16), wa2.astype(BF16), "nn") + ba
    la = _log_sigmoid(logit) * (1.0 / GATE_NORM)
    row, col, _ = _attn_masks()
    b_cum = _tri_sum(col <= row, la)
    return q, k, v, gate, b_cum, glow, logit


def _decay_factors(q, k, b_cum):
    c = b_cum[SUPER // 2 - 1:SUPER // 2, :]
    bl = b_cum[SUPER - 1:SUPER, :]
    e1 = jnp.exp(b_cum - c)
    e2 = jnp.exp(c - b_cum)
    e_b = jnp.exp(b_cum)
    e_l = jnp.exp(bl - b_cum)
    return dict(e1=e1, e2=e2, eb=e_b, el=e_l, ebl=jnp.exp(bl),
                qp=q * e1, qm=q * e2, kp=k * e1, km=k * e2, qs=q * e_b, kl=k * e_l)


def _state_block_mask():
    r = lax.broadcasted_iota(jnp.int32, (V_W, QK_W), 0)
    c = lax.broadcasted_iota(jnp.int32, (V_W, QK_W), 1)
    return (r // DV) == (c // DK)


def _attn_fwd(proj, cos, sin_signed, lg, wa2p, ba, gn_ret, gn_gla, x_res, w_out, g_next, comm=None):
    T = proj.shape[0]
    n_s = T // SUPER
    D = x_res.shape[1]

    def main(ins, outs, scr):
        pr_ref, cos_ref, sin_ref, lg_ref, wa2_ref, ba_ref, gr_ref, gg_ref, xres_ref, wout_ref, gnext_ref = ins
        o_ref, y_ref, st_ref, x_ref, h_ref = outs
        (s_ref,) = scr
        i = pl.program_id(0)

        @pl.when(i == 0)
        def _():
            s_ref[...] = jnp.zeros_like(s_ref)

        pr = pr_ref
        row, col, same = _attn_masks()
        m1 = col <= row
        m2 = jnp.logical_and(col > row, same)
        lane = lax.broadcasted_iota(jnp.int32, (1, QK_W), 1)
        blockmask = _state_block_mask()
        for grp in range(2):
            q, k, v, gate, b_cum, _, _ = _group_inputs(grp, pr, cos_ref[...], sin_ref[...], lg_ref[...],
                                                      wa2_ref[...], ba_ref[...])
            f = _decay_factors(q, k, b_cum)
            gn = gr_ref[...] if grp == 0 else gg_ref[...]
            s_prev = s_ref[grp]
            st_ref[0, grp] = s_prev
            o_inter = _dot(f["qs"].astype(BF16), s_prev.astype(BF16), "nt")
            kmb = f["km"].astype(BF16)
            kpb = f["kp"].astype(BF16)
            vb = v.astype(BF16)
            heads = [(lane // DK) == h for h in range(HEADS)]
            a1_all = _dot(jnp.concatenate([jnp.where(hm, f["qp"], 0.0).astype(BF16) for hm in heads], axis=0), kmb, "nt")
            a2_all = _dot(jnp.concatenate([jnp.where(hm, f["qm"], 0.0).astype(BF16) for hm in heads], axis=0), kpb, "nt")
            for h in range(HEADS):
                a1 = a1_all[h * SUPER:(h + 1) * SUPER]
                a2 = a2_all[h * SUPER:(h + 1) * SUPER]
                a = jnp.where(m1, a1, jnp.where(m2, a2, 0.0))
                lo = grp * V_W + h * DV
                o_h = _dot(a.astype(BF16), vb[:, h * DV:(h + 1) * DV], "nn") + o_inter[:, h * DV:(h + 1) * DV]
                o_ref[:, lo:lo + DV] = o_h
                r = lax.rsqrt(jnp.mean(o_h * o_h, axis=-1, keepdims=True) + RMS_EPS)
                gte = gate[:, h * DV:(h + 1) * DV]
                y = o_h * r * gn[:, h * DV:(h + 1) * DV] * (gte * _sigmoid(gte))
                y_ref[:, lo:lo + DV] = y.astype(y_ref.dtype)
            upd = _dot(vb, f["kl"].astype(BF16), "tn")
            s_ref[grp] = s_prev * f["ebl"] + jnp.where(blockmask, upd, 0.0)
        xv = xres_ref[...] + _dot(y_ref[...], wout_ref[...], "nn")
        x_ref[...] = xv
        r = lax.rsqrt(jnp.mean(xv * xv, axis=-1, keepdims=True) + RMS_EPS)
        h_ref[...] = (xv * r * gnext_ref[...]).astype(h_ref.dtype)

    const = lambda shape: pl.BlockSpec(shape, lambda i: tuple(0 for _ in shape))
    rows = lambda w: pl.BlockSpec((SUPER, w), lambda i: (i, 0))
    return _call(
        "attn_fwd", main, (n_s,),
        [rows(PROJ_W), rows(128), rows(128),
         const((1, QK_W)), const((GL_W, QK_W)), const((1, QK_W)), const((1, V_W)), const((1, V_W)),
         rows(D), const((2 * V_W, D)), const((1, D))],
        [rows(2 * V_W), rows(2 * V_W), pl.BlockSpec((1, 2, V_W, QK_W), lambda i: (i, 0, 0, 0)), rows(D), rows(D)],
        [jax.ShapeDtypeStruct((T, 2 * V_W), F32), jax.ShapeDtypeStruct((T, 2 * V_W), BF16),
         jax.ShapeDtypeStruct((n_s, 2, V_W, QK_W), F32), jax.ShapeDtypeStruct((T, D), F32),
         jax.ShapeDtypeStruct((T, D), BF16)],
        (proj, cos, sin_signed, lg, wa2p, ba, gn_ret, gn_gla, x_res, w_out, g_next),
        scratch=[pltpu.VMEM((2, V_W, QK_W), F32)], comm=comm)


def _attn_bwd(proj, cos, sin_signed, lg, wa2p, ba, gn_ret, gn_gla, o, dx, w_out, states, comm=None, after=None):
    T = proj.shape[0]
    n_s = T // SUPER
    D = dx.shape[1]

    def main(ins, outs, scr):
        pr_ref, cos_ref, sin_ref, lg_ref, wa2_ref, ba_ref, gr_ref, gg_ref, o_ref, dx_ref, wout_ref, st_ref = ins
        dp_ref, dgr_ref, dgg_ref, dba_ref, dwa_ref = outs
        (ds_ref, dy_ref) = scr
        i = pl.program_id(0)
        dy_ref[...] = _dot(dx_ref[...], wout_ref[...], "nt")

        @pl.when(i == 0)
        def _():
            ds_ref[...] = jnp.zeros_like(ds_ref)
            dgr_ref[...] = jnp.zeros_like(dgr_ref)
            dgg_ref[...] = jnp.zeros_like(dgg_ref)
            dba_ref[...] = jnp.zeros_like(dba_ref)
            dwa_ref[...] = jnp.zeros_like(dwa_ref)

        pr = pr_ref
        cos = cos_ref[...]
        sin_signed = sin_ref[...]
        row, col, same = _attn_masks()
        m1 = col <= row
        m2 = jnp.logical_and(col > row, same)
        m1t = row <= col
        m2t = jnp.logical_and(row > col, same)
        lane = lax.broadcasted_iota(jnp.int32, (1, QK_W), 1)
        blockmask = _state_block_mask()
        for grp in range(2):
            q, k, v, gate, b_cum, glow, logit = _group_inputs(grp, pr, cos, sin_signed, lg_ref[...],
                                                              wa2_ref[...], ba_ref[...])
            f = _decay_factors(q, k, b_cum)
            gn = gr_ref[...] if grp == 0 else gg_ref[...]
            dgn_ref = dgr_ref if grp == 0 else dgg_ref
            do_parts, dgate_parts, dgn_parts = [], [], []
            for h in range(HEADS):
                lo = grp * V_W + h * DV
                o_h = o_ref[:, lo:lo + DV]
                r = lax.rsqrt(jnp.mean(o_h * o_h, axis=-1, keepdims=True) + RMS_EPS)
                n = o_h * r
                gte = gate[:, h * DV:(h + 1) * DV]
                sg = _sigmoid(gte)
                dy_h = dy_ref[:, lo:lo + DV]
                gn_h = gn[:, h * DV:(h + 1) * DV]
                dgate_parts.append(dy_h * n * gn_h * (sg * (1.0 + gte * (1.0 - sg))))
                dz = dy_h * (gte * sg)
                dgn_parts.append(jnp.sum(dz * n, axis=0, keepdims=True))
                dn = dz * gn_h
                do_parts.append(r * (dn - n * jnp.mean(dn * n, axis=-1, keepdims=True)))
            dgn_ref[...] += jnp.concatenate(dgn_parts, axis=1)
            dgate = jnp.concatenate(dgate_parts, axis=1)
            do = jnp.concatenate(do_parts, axis=1)
            dob = do.astype(BF16)
            vb = v.astype(BF16)
            s_prev = st_ref[0, grp]
            ds_new = ds_ref[grp]
            dsb = ds_new.astype(BF16)
            qpb, qmb = f["qp"].astype(BF16), f["qm"].astype(BF16)
            kpb, kmb = f["kp"].astype(BF16), f["km"].astype(BF16)
            dv_parts = []
            heads = [(lane // DK) == h for h in range(HEADS)]
            qp_hs = [jnp.where(hm, f["qp"], 0.0).astype(BF16) for hm in heads]
            qm_hs = [jnp.where(hm, f["qm"], 0.0).astype(BF16) for hm in heads]
            kp_hs = [jnp.where(hm, f["kp"], 0.0).astype(BF16) for hm in heads]
            km_hs = [jnp.where(hm, f["km"], 0.0).astype(BF16) for hm in heads]
            km_stack, kp_stack = jnp.concatenate(km_hs, axis=0), jnp.concatenate(kp_hs, axis=0)
            at1_all = _dot(km_stack, qpb, "nt")
            at2_all = _dot(kp_stack, qmb, "nt")
            da1s, da2s, da1ts, da2ts = [], [], [], []
            for h in range(HEADS):
                at = jnp.where(m1t, at1_all[h * SUPER:(h + 1) * SUPER],
                               jnp.where(m2t, at2_all[h * SUPER:(h + 1) * SUPER], 0.0))
                do_h = dob[:, h * DV:(h + 1) * DV]
                v_h = vb[:, h * DV:(h + 1) * DV]
                dv_parts.append(_dot(at.astype(BF16), do_h, "nn"))
                da = _dot(do_h, v_h, "nt")
                dat = _dot(v_h, do_h, "nt")
                da1s.append(jnp.where(m1, da, 0.0).astype(BF16))
                da2s.append(jnp.where(m2, da, 0.0).astype(BF16))
                da1ts.append(jnp.where(m1t, dat, 0.0).astype(BF16))
                da2ts.append(jnp.where(m2t, dat, 0.0).astype(BF16))
            dqp = _dot(jnp.concatenate(da1s, axis=1), km_stack, "nn")
            dqm = _dot(jnp.concatenate(da2s, axis=1), kp_stack, "nn")
            dkm = _dot(jnp.concatenate(da1ts, axis=1), jnp.concatenate(qp_hs, axis=0), "nn")
            dkp = _dot(jnp.concatenate(da2ts, axis=1), jnp.concatenate(qm_hs, axis=0), "nn")
            klb = f["kl"].astype(BF16)
            qsb = f["qs"].astype(BF16)
            dqs = _dot(dob, s_prev.astype(BF16), "nn")
            dkl = _dot(vb, dsb, "nn")
            dv = jnp.concatenate(dv_parts, axis=1) + _dot(klb, dsb, "nt")
            ds_ref[grp] = ds_new * f["ebl"] + jnp.where(blockmask, _dot(dob, qsb, "tn"), 0.0)
            dq = dqp * f["e1"] + dqm * f["e2"] + dqs * f["eb"]
            dk = dkm * f["e2"] + dkp * f["e1"] + dkl * f["el"]
            if grp == 0:
                dq = _rotary_transposed(dq, cos, sin_signed)
                dk = _rotary_transposed(dk * (DK ** -0.5), cos, sin_signed)
                dp_ref[:, C_RQ:C_RQ + QK_W] = dq.astype(dp_ref.dtype)
                dp_ref[:, C_RK:C_RK + QK_W] = dk.astype(dp_ref.dtype)
                dp_ref[:, C_RV:C_RV + V_W] = dv.astype(dp_ref.dtype)
                dp_ref[:, C_RG:C_RG + V_W] = dgate.astype(dp_ref.dtype)
            else:
                dkl_kl = dkl * klb.astype(F32)
                db = (dqp * qpb.astype(F32) - dkm * kmb.astype(F32) - dqm * qmb.astype(F32)
                      + dkp * kpb.astype(F32) + dqs * qsb.astype(F32) - dkl_kl)
                last = (jnp.sum(dkl_kl, axis=0, keepdims=True)
                        + f["ebl"] * jnp.sum(s_prev * ds_new, axis=0, keepdims=True))
                rowq = lax.broadcasted_iota(jnp.int32, (SUPER, QK_W), 0)
                db = db + jnp.where(rowq == SUPER - 1, last, 0.0)
                dla = _tri_sum(col >= row, db)
                dlogit = dla * (1.0 / GATE_NORM) * (1.0 - _sigmoid(logit))
                dlb = dlogit.astype(BF16)
                dglow = _dot(dlb, wa2_ref[...].astype(BF16), "nt")
                dwa_ref[...] += _dot(glow.astype(BF16), dlb, "tn")
                dba_ref[...] += jnp.sum(dlogit, axis=0, keepdims=True)
                dp_ref[:, C_GQ:C_GQ + QK_W] = (dq * (DK ** -0.5)).astype(dp_ref.dtype)
                dp_ref[:, C_GK:C_GK + QK_W] = dk.astype(dp_ref.dtype)
                dp_ref[:, C_GV:C_GV + V_W] = dv.astype(dp_ref.dtype)
                dp_ref[:, C_GG:C_GG + V_W] = dgate.astype(dp_ref.dtype)
                dp_ref[:, C_GL:C_GL + GL_W] = dglow.astype(dp_ref.dtype)

    rev = lambda i: n_s - 1 - i
    const = lambda shape: pl.BlockSpec(shape, lambda i: tuple(0 for _ in shape))
    return _call(
        "attn_bwd", main, (n_s,),
        [pl.BlockSpec((SUPER, PROJ_W), lambda i: (rev(i), 0)),
         pl.BlockSpec((SUPER, 128), lambda i: (rev(i), 0)), pl.BlockSpec((SUPER, 128), lambda i: (rev(i), 0)),
         const((1, QK_W)), const((GL_W, QK_W)), const((1, QK_W)), const((1, V_W)), const((1, V_W)),
         pl.BlockSpec((SUPER, 2 * V_W), lambda i: (rev(i), 0)),
         pl.BlockSpec((SUPER, D), lambda i: (rev(i), 0)), const((2 * V_W, D)),
         pl.BlockSpec((1, 2, V_W, QK_W), lambda i: (rev(i), 0, 0, 0))],
        [pl.BlockSpec((SUPER, PROJ_W), lambda i: (rev(i), 0)),
         const((1, V_W)), const((1, V_W)), const((1, QK_W)), const((GL_W, QK_W))],
        [jax.ShapeDtypeStruct((T, PROJ_W), BF16),
         jax.ShapeDtypeStruct((1, V_W), F32), jax.ShapeDtypeStruct((1, V_W), F32),
         jax.ShapeDtypeStruct((1, QK_W), F32), jax.ShapeDtypeStruct((GL_W, QK_W), F32)],
        (proj, cos, sin_signed, lg, wa2p, ba, gn_ret, gn_gla, o, dx, w_out, states),
        scratch=[pltpu.VMEM((2, V_W, QK_W), F32), pltpu.VMEM((SUPER, 2 * V_W), F32)], comm=comm, after=after)


def _rotary_tables(T):
    half = DK // 2
    inv = ROPE_BASE ** (-jnp.arange(half, dtype=F32) * 2.0 / DK)
    ang = jnp.arange(T, dtype=F32)[:, None] * inv[None, :]
    cos, sin = jnp.cos(ang), jnp.sin(ang)
    cos_head = jnp.concatenate([cos, cos], axis=1)
    sin_head = jnp.concatenate([-sin, sin], axis=1)
    return jnp.tile(cos_head, (1, 128 // DK)), jnp.tile(sin_head, (1, 128 // DK))


def _sum_devices(name, gathered, m_per):
    def body(g_ref, o_ref):
        acc = g_ref[0:m_per, :]
        for k in range(1, N_DEV):
            acc = acc + g_ref[k * m_per:(k + 1) * m_per, :]
        o_ref[...] = acc

    return pl.pallas_call(body, name=name, out_shape=jax.ShapeDtypeStruct((m_per, 128), F32))(gathered)


def _owner_sums(name, items, owner, comm=None):
    n = len(items)

    def main(ins, outs, scr):
        bufs, sems = scr[:n], scr[n]
        copies = [pltpu.make_async_copy(ins[2 * p + 1], bufs[p], sems.at[p]) for p in range(n)]
        for cp in copies:
            cp.start()
        for p, o_ref in enumerate(outs):
            acc = ins[2 * p][...].astype(F32)
            copies[p].wait()
            for j in range(bufs[p].shape[0]):
                acc = acc + bufs[p][j].astype(F32)
            o_ref[...] = acc

    in_specs, out_specs, out_shape, args, scratch = [], [], [], [], []
    for grad, (landed,) in items:
        R, C = grad.shape[-2:]
        in_specs += [pl.BlockSpec((None, None, R, C), lambda i, s: (s[0], s[1], 0, 0), pipeline_mode=pl.Buffered(1)), HBM]
        out_specs.append(pl.BlockSpec((R, C), lambda i, s: (0, 0)))
        out_shape.append(jax.ShapeDtypeStruct((R, C), F32))
        args += [grad, landed]
        scratch.append(pltpu.VMEM(landed.shape, landed.dtype))
    scratch.append(pltpu.SemaphoreType.DMA((n,)))
    return _call(name, main, (1,), in_specs, out_specs, out_shape, args, scratch=scratch, comm=comm, prefetch=owner)


def _adamw_group(name, items, n_blocks, comm=None):
    n = len(items)

    def main(ins, outs, scr):
        for p in range(n):
            g_ref, w_ref, m_ref, v_ref = ins[4 * p:4 * p + 4]
            d_ref, nm_ref, nv_ref = outs[3 * p:3 * p + 3]
            gv = g_ref[...]
            nm = ADAM_B1 * m_ref[...] + (1.0 - ADAM_B1) * gv
            nv = ADAM_B2 * v_ref[...] + (1.0 - ADAM_B2) * (gv * gv)
            m_hat = nm / (1.0 - ADAM_B1 ** ADAM_STEP)
            v_hat = nv / (1.0 - ADAM_B2 ** ADAM_STEP)
            d_ref[...] = -ADAM_LR * (m_hat / (jnp.sqrt(v_hat) + ADAM_EPS) + ADAM_WD * w_ref[...])
            nm_ref[...] = nm
            nv_ref[...] = nv

    in_specs, out_specs, out_shape, args = [], [], [], []
    for item in items:
        shape = item[1].shape
        assert shape[0] % n_blocks == 0
        spec = pl.BlockSpec((shape[0] // n_blocks,) + shape[1:], lambda i, nd=len(shape): (i,) + (0,) * (nd - 1))
        in_specs += [spec] * 4
        out_specs += [spec] * 3
        out_shape += [jax.ShapeDtypeStruct(shape, F32)] * 3
        args += list(item)
    outs, extra = _call(name, main, (n_blocks,), in_specs, out_specs, out_shape, args, comm=comm)
    return [tuple(outs[3 * p:3 * p + 3]) for p in range(n)], extra


def _stack_row_blocks(name, blocks, rows_out):
    n, r, C = blocks.shape
    assert blocks.dtype == BF16 and r % 2 == 0 and rows_out % 2 == 0
    h = r // 2

    def main(ins, outs, scr):
        (stage,) = scr
        stage[n * h:, :] = jnp.zeros((rows_out // 2 - n * h, C), jnp.uint32)
        for s in range(n):
            stage[s * h:(s + 1) * h, :] = pltpu.bitcast(ins[0][s], jnp.uint32)
        outs[0][...] = pltpu.bitcast(stage[...], BF16)

    (out,), _ = _call(
        name, main, (1,), [pl.BlockSpec((n, r, C), lambda i: (0, 0, 0), pipeline_mode=pl.Buffered(1))],
        [pl.BlockSpec((rows_out, C), lambda i: (0, 0))], [jax.ShapeDtypeStruct((rows_out, C), BF16)],
        (blocks,), scratch=[pltpu.VMEM((rows_out // 2, C), jnp.uint32)])
    return out


def _unstack_row_blocks(name, a, n, r):
    C = a.shape[1]
    assert a.dtype == BF16 and r % 2 == 0
    h = r // 2

    def main(ins, outs, scr):
        (stage,) = scr
        stage[...] = pltpu.bitcast(ins[0][...], jnp.uint32)
        for s in range(n):
            outs[0][s] = pltpu.bitcast(stage[s * h:(s + 1) * h, :], BF16)

    (out,), _ = _call(
        name, main, (1,), [pl.BlockSpec((n * r, C), lambda i: (0, 0), pipeline_mode=pl.Buffered(1))],
        [pl.BlockSpec((n, r, C), lambda i: (0, 0, 0))], [jax.ShapeDtypeStruct((n, r, C), BF16)],
        (a,), scratch=[pltpu.VMEM((n * h, C), jnp.uint32)])
    return out


SMALL_ORDER = ("ffn1", "mix", "ffn2", "final", "ret", "gla", "b_a")


def kernel(x, ffn1_norm_g, ffn1_w_gate, ffn1_w_up, ffn1_w_down, mix_norm_g, w_in, ret_norm_g, gla_w_a2, gla_b_a, gla_norm_g, w_out, ffn2_norm_g, ffn2_w_gate, ffn2_w_up, ffn2_w_down, final_norm_g, loss_target, m_ffn1_norm_g, m_ffn1_w_gate, m_ffn1_w_up, m_ffn1_w_down, m_mix_norm_g, m_w_in, m_ret_norm_g, m_gla_w_a2, m_gla_b_a, m_gla_norm_g, m_w_out, m_ffn2_norm_g, m_ffn2_w_gate, m_ffn2_w_up, m_ffn2_w_down, m_final_norm_g, v_ffn1_norm_g, v_ffn1_w_gate, v_ffn1_w_up, v_ffn1_w_down, v_mix_norm_g, v_w_in, v_ret_norm_g, v_gla_w_a2, v_gla_b_a, v_gla_norm_g, v_w_out, v_ffn2_norm_g, v_ffn2_w_gate, v_ffn2_w_up, v_ffn2_w_down, v_final_norm_g):
    xi, yi, ci = _coords()
    dev = 4 * xi + 2 * yi + ci
    owner = jnp.stack([2 * xi + yi, ci]).astype(jnp.int32)

    x0, target = x[0], loss_target[0]
    T, D = x0.shape
    fb = ffn1_w_gate.shape[2]
    ib = w_in.shape[2]
    ab = gla_w_a2.shape[2]
    F = N_DEV * fb
    cos, sin_signed = _rotary_tables(T)
    lg = jnp.repeat(jnp.log(1.0 - 2.0 ** (-5.0 - jnp.arange(HEADS, dtype=F32))), DK)[None, :]
    g_final = final_norm_g.reshape(1, D)

    g1_loc = ffn1_w_gate[0].T[None].astype(BF16)
    u1_loc = ffn1_w_up[0].T[None].astype(BF16)
    d1_loc = ffn1_w_down.astype(BF16)
    g2_loc = ffn2_w_gate[0].T[None].astype(BF16)
    u2_loc = ffn2_w_up[0].T[None].astype(BF16)
    d2_loc = ffn2_w_down.astype(BF16)
    in_loc = w_in[0].T.astype(BF16)
    out_loc = w_out[0].astype(BF16)

    h1, (g1,) = _rms_fwd("ffn1_rms", x0, ffn1_norm_g, comm=_AllGather([g1_loc], ["stack"]))
    g1 = g1.reshape(1, F, D)
    (dsl1, sl1), (u1,) = _mm_nstream("ffn1_gate", h1, [g1], [0], "nt", [], [BF16, BF16], _gate_parts_epilogue, cn=256,
                                     comm=_AllGather([u1_loc], ["stack"]))
    u1 = u1.reshape(1, F, D)
    (dsu1, act1), (d1,) = _mm_nstream("ffn1_up", h1, [u1], [0], "nt", [dsl1, sl1], [BF16, BF16],
                                      _up_act_epilogue, cn=256, comm=_AllGather([d1_loc], ["stack"]))
    d1 = d1.reshape(1, F, D)
    f32_tile, bf16_tile, f32_vec = (F32, "tile"), (BF16, "tile"), (F32, "vec")
    (x1, h2), (in_all, a_all) = _mm_mstream(
        "ffn1_down", [act1], [d1], [0], "nn", [(x0, "tile"), (mix_norm_g, "vec")], [f32_tile, bf16_tile],
        _residual_rms_epilogue(0.5), comm=_AllGather([in_loc, gla_w_a2[0]], ["plain", "plain"]))
    w_in_t = _stack_row_blocks("w_in_rows", in_all, PROJ_W)[None]
    wa2 = jnp.transpose(a_all, (1, 0, 2)).reshape(GATE_RANK, N_DEV * ab)
    wa2p = jnp.pad(wa2, ((0, GL_W - GATE_RANK), (0, 0)))

    (proj,), (g2, out_all) = _mm_nstream("mix_proj", h2, [w_in_t], [0], "nt", [], [BF16], _identity_epilogue, cn=640,
                                         comm=_AllGather([g2_loc, out_loc], ["stack", "plain"]))
    w_out_full = out_all.reshape(D, D)
    (o, ymix, states, x2, h3), (u2,) = _attn_fwd(proj, cos, sin_signed, lg, wa2p, gla_b_a, ret_norm_g, gla_norm_g,
                                                 x1, w_out_full, ffn2_norm_g, comm=_AllGather([u2_loc], ["stack"]))
    g2, u2 = g2.reshape(1, F, D), u2.reshape(1, F, D)

    (dsu2, sl2, act2), (d2,) = _mm_nstream(
        "ffn2_up", h3, [g2, u2], [0, 0], "nt", [], [BF16, BF16, BF16], _silu_mul_epilogue, cn=256,
        comm=_AllGather([d2_loc], ["stack"]))
    d2 = d2.reshape(1, F, D)
    (dx3, dy3b, d_final, loss), _ = _mm_mstream(
        "ffn2_down", [act2], [d2], [0], "nn", [(x2, "tile"), (g_final, "vec"), (target, "tile")],
        [f32_tile, bf16_tile, f32_vec, f32_vec], _final_loss_epilogue(0.5, 0.5))

    sent = {}

    def send(**grads):
        started = _send_to_owners("send_" + "_".join(grads), list(grads.values()))
        sent.update(zip(grads, started))
        return started[0][4]

    dwd2, _ = _mm_tn("ffn2b_dwd", act2, dy3b, F // 2, D, BF16)
    (dgate2, dup2), _ = _mm_nstream("ffn2b_dact", dy3b, [d2], [0], "nt", [dsu2, sl2], [BF16, BF16],
                                    _dact_epilogue, cn=256)
    dwg2, _ = _mm_tn("ffn2b_dwg", dgate2, h3, F // 2, D, BF16)
    dwu2, _ = _mm_tn("ffn2b_dwu", dup2, h3, F // 2, D, BF16)
    tok = send(wd2=dwd2.reshape(4, 2, fb, D), wg2=dwg2.reshape(4, 2, fb, D), wu2=dwu2.reshape(4, 2, fb, D))
    rms_outs = [f32_tile, bf16_tile, f32_vec]
    (dx2, dx2b, d_g2), _ = _mm_mstream(
        "ffn2b_dh", [dgate2, dup2], [g2, u2], [0, 0], "nn", [(x2, "tile"), (ffn2_norm_g, "vec"), (dx3, "tile")],
        rms_outs, _rms_bwd_epilogue(1.0), after=tok)

    dwout, _ = _mm_tn("mixb_dwout", ymix, dx2b, D, D, BF16)
    (dproj, d_ret, d_gla, d_ba, d_wa2p), _ = _attn_bwd(
        proj, cos, sin_signed, lg, wa2p, gla_b_a, ret_norm_g, gla_norm_g, o, dx2b, w_out_full, states)
    dwin_t, _ = _mm_tn("mixb_dwin", dproj, h2, 640, D, BF16, tk=2048)
    tok = send(wout=dwout.reshape(4, 2, D // N_DEV, D), win=_unstack_row_blocks("dwin_blocks", dwin_t, N_DEV, ib).reshape(4, 2, ib, D))
    (dx1, dy1b, d_gmix), _ = _mm_mstream(
        "mixb_dh", [dproj], [w_in_t], [0], "nn", [(x1, "tile"), (mix_norm_g, "vec"), (dx2, "tile")],
        rms_outs, _rms_bwd_epilogue(0.5), after=tok)

    dwd1, _ = _mm_tn("ffn1b_dwd", act1, dy1b, F // 2, D, BF16)
    tok = send(wd1=dwd1.reshape(4, 2, fb, D))
    (dgate1, dup1), _ = _mm_nstream("ffn1b_dact", dy1b, [d1], [0], "nt", [dsu1, sl1], [BF16, BF16],
                                    _dact_epilogue, cn=256, after=tok)
    dwg1, _ = _mm_tn("ffn1b_dwg", dgate1, h1, F // 2, D, BF16)
    tok = send(wg1=dwg1.reshape(4, 2, fb, D))
    dwu1, _ = _mm_tn("ffn1b_dwu", dup1, h1, F // 2, D, BF16, after=tok)
    tok = send(wu1=dwu1.reshape(4, 2, fb, D))
    (dx0, _, d_g1), _ = _mm_mstream(
        "ffn1b_dh", [dgate1, dup1], [g1, u1], [0, 0], "nn", [(x0, "tile"), (ffn1_norm_g, "vec"), (dx1, "tile")],
        rms_outs, _rms_bwd_epilogue(1.0), after=tok)

    small = dict(ffn1=d_g1, mix=d_gmix, ffn2=d_g2, final=d_final, ret=d_ret, gla=d_gla, b_a=d_ba)
    flat = jnp.concatenate([small[k].reshape(-1) for k in SMALL_ORDER]
                           + [d_wa2p[:GATE_RANK].reshape(-1), loss[0, :128]])
    rows = -(-flat.shape[0] // 128)
    rows = -(-rows // 8) * 8
    packed = jnp.pad(flat, (0, rows * 128 - flat.shape[0])).reshape(rows, 128)

    transposed = ("ffn1_w_gate", "ffn1_w_up", "ffn2_w_gate", "ffn2_w_up")

    def to_2d(nm, a):
        if nm == "w_in":
            return jnp.transpose(a, (2, 0, 1))
        if nm in transposed:
            return a[0].T
        return a.reshape((1, a.shape[0]) if a.ndim == 1 else a.shape[-2:])

    def from_2d(nm, a):
        if nm == "w_in":
            return jnp.transpose(a, (1, 2, 0))
        return a.T[None] if nm in transposed else a.reshape(params[nm][0].shape)

    def arrived(nm, after):
        grad, landed = _await_owners("await_" + nm, sent[nm], after)
        return grad, [landed]

    sums_a, (gathered,) = _owner_sums(
        "sum_a", [arrived(nm, dx0) for nm in ("wg2", "wu2", "wd2", "win", "wout")], owner,
        comm=_AllGather([packed], ["plain"]))
    params = dict(
        ffn2_w_gate=(ffn2_w_gate, m_ffn2_w_gate, v_ffn2_w_gate), ffn2_w_up=(ffn2_w_up, m_ffn2_w_up, v_ffn2_w_up),
        ffn2_w_down=(ffn2_w_down, m_ffn2_w_down, v_ffn2_w_down), w_in=(w_in, m_w_in, v_w_in),
        w_out=(w_out, m_w_out, v_w_out), ffn1_w_gate=(ffn1_w_gate, m_ffn1_w_gate, v_ffn1_w_gate),
        ffn1_w_up=(ffn1_w_up, m_ffn1_w_up, v_ffn1_w_up), ffn1_w_down=(ffn1_w_down, m_ffn1_w_down, v_ffn1_w_down),
        ffn1_norm_g=(ffn1_norm_g, m_ffn1_norm_g, v_ffn1_norm_g), mix_norm_g=(mix_norm_g, m_mix_norm_g, v_mix_norm_g),
        ret_norm_g=(ret_norm_g, m_ret_norm_g, v_ret_norm_g), gla_w_a2=(gla_w_a2, m_gla_w_a2, v_gla_w_a2),
        gla_b_a=(gla_b_a, m_gla_b_a, v_gla_b_a), gla_norm_g=(gla_norm_g, m_gla_norm_g, v_gla_norm_g),
        ffn2_norm_g=(ffn2_norm_g, m_ffn2_norm_g, v_ffn2_norm_g), final_norm_g=(final_norm_g, m_final_norm_g, v_final_norm_g))
    grads, updates = {}, {}

    def run_adam(name, names, grad_2d, n_blocks):
        items = [(grad_2d[nm],) + tuple(to_2d(nm, a) for a in params[nm]) for nm in names]
        res, _ = _adamw_group(name, items, n_blocks)
        for nm, r in zip(names, res):
            grads[nm] = from_2d(nm, grad_2d[nm])
            updates[nm] = tuple(from_2d(nm, a) for a in r)
        return res

    grads_a = {"ffn2_w_gate": sums_a[0], "ffn2_w_up": sums_a[1], "ffn2_w_down": sums_a[2], "w_out": sums_a[4]}
    run_adam("adamw_w_in", ["w_in"], {"w_in": sums_a[3][:, None, :]}, 2)
    done_a = run_adam("adamw_a", list(grads_a), grads_a, 4)[0][0]
    sums_b, _ = _owner_sums("sum_b", [arrived(nm, done_a) for nm in ("wg1", "wu1", "wd1")], owner)
    grads_b = {"ffn1_w_gate": sums_b[0], "ffn1_w_up": sums_b[1], "ffn1_w_down": sums_b[2]}
    run_adam("adamw_b", list(grads_b), grads_b, 4)

    total = _sum_devices("sum_small", gathered.reshape(N_DEV * rows, 128), rows).reshape(-1)
    sizes = [small[k].size for k in SMALL_ORDER] + [GATE_RANK * QK_W, 128]
    offs = [0]
    for s in sizes:
        offs.append(offs[-1] + s)
    pieces = [total[offs[i]:offs[i + 1]] for i in range(len(sizes))]
    g_small = {k: pieces[i].reshape(small[k].shape) for i, k in enumerate(SMALL_ORDER)}
    g_wa2_full = pieces[len(SMALL_ORDER)].reshape(GATE_RANK, QK_W)
    g_wa2 = lax.dynamic_slice(g_wa2_full, (0, dev * ab), (GATE_RANK, ab))
    loss_total = pieces[len(SMALL_ORDER) + 1][0]

    small_grads = {"ffn1_norm_g": g_small["ffn1"], "mix_norm_g": g_small["mix"], "ret_norm_g": g_small["ret"],
                   "gla_w_a2": g_wa2, "gla_b_a": g_small["b_a"], "gla_norm_g": g_small["gla"],
                   "ffn2_norm_g": g_small["ffn2"], "final_norm_g": g_small["final"]}
    run_adam("adamw_small", list(small_grads), small_grads, 1)

    order = ("ffn1_norm_g", "ffn1_w_gate", "ffn1_w_up", "ffn1_w_down", "mix_norm_g", "w_in", "ret_norm_g", "gla_w_a2",
             "gla_b_a", "gla_norm_g", "w_out", "ffn2_norm_g", "ffn2_w_gate", "ffn2_w_up", "ffn2_w_down", "final_norm_g")
    return (loss_total, dx0[None], *[grads[nm] for nm in order], *[updates[nm][0] for nm in order],
            *[updates[nm][1] for nm in order], *[updates[nm][2] for nm in order])
```
